```python
import jax, jax.numpy as jnp
from jax import lax
import numpy as np

D_MODEL = 1024
BATCH = 8
SEQ = 4096
DEPTH = 4

N_META = 16
D_FF = 4 * D_MODEL
D_CONV = D_MODEL // 2
CONV_WIDTH = 31
N_POOL_GROUPS = 4
POOL_WINDOWS = (2, 4, 8, 16)
D_POOL = D_MODEL // 2
POOL_GROUP_DIM = D_POOL // N_POOL_GROUPS
D_EVEN_IN = 2 * D_CONV + D_POOL
HGRN_HEAD_DIM = 128
HGRN_HEADS = D_MODEL // HGRN_HEAD_DIM
D_HGRN = HGRN_HEADS * HGRN_HEAD_DIM
CHUNK = 64
N_EVEN = (DEPTH + 1) // 2
N_ODD = DEPTH // 2
EPS = 1e-6

kernel_name = 'hybrid_conv_pool_hgrn2_trunk'


def _rmsnorm(x, g):
    xf = x.astype(jnp.float32)
    y = xf * lax.rsqrt(jnp.mean(xf * xf, axis=-1, keepdims=True) + EPS)
    return (y * g.astype(jnp.float32)).astype(x.dtype)


def _layernorm(x, g, b):
    xf = x.astype(jnp.float32)
    mu = jnp.mean(xf, axis=-1, keepdims=True)
    xc = xf - mu
    y = xc * lax.rsqrt(jnp.mean(xc * xc, axis=-1, keepdims=True) + EPS)
    return (y * g.astype(jnp.float32) + b.astype(jnp.float32)).astype(x.dtype)


def _conv_mixer(val, gate, conv_w, conv_b, ln_g, ln_b):
    a = val * jax.nn.sigmoid(gate)
    y = lax.conv_general_dilated(
        a, conv_w[:, None, :].astype(a.dtype), window_strides=(1,),
        padding=[(CONV_WIDTH - 1, 0)], dimension_numbers=('NWC', 'WIO', 'NWC'),
        feature_group_count=D_CONV) + conv_b
    return jax.nn.silu(_layernorm(y, ln_g, ln_b))


def _causal_window_mean(x, w):
    L = x.shape[1]
    cs = jnp.cumsum(x.astype(jnp.float32), axis=1)
    cs0 = jnp.pad(cs, ((0, 0), (1, 0), (0, 0)))
    lower = jnp.pad(cs0[:, :L + 1 - w], ((0, 0), (w - 1, 0), (0, 0)))
    count = jnp.minimum(jnp.arange(1, L + 1, dtype=jnp.float32), float(w))
    return ((cs - lower) / count[None, :, None]).astype(x.dtype)


def _pool_mixer(u, pool_w, pool_b, pool_scale):
    Bn, L, _ = u.shape
    ug = u.reshape(Bn, L, N_POOL_GROUPS, POOL_GROUP_DIM)
    pooled = jnp.stack([_causal_window_mean(ug[:, :, gi], w) for gi, w in enumerate(POOL_WINDOWS)], axis=2)
    y = jnp.einsum('blgc,gcd->blgd', pooled - ug, pool_w) + pool_b
    return y.reshape(Bn, L, D_POOL) * pool_scale


def _hgrn2_chunk_scan(q, k, v, logf):
    C = q.shape[3]
    causal = jnp.tril(jnp.ones((C, C), dtype=bool))

    def step(S, inp):
        qc, kc, vc, lfc = inp
        b = jnp.cumsum(lfc, axis=2)
        diff = b[:, :, :, None, :] - b[:, :, None, :, :]
        decay = jnp.exp(jnp.where(causal[:, :, None], diff, -jnp.inf))
        scores = jnp.einsum('bhtk,bhsk,bhtsk->bhts', qc, kc, decay)
        o = (jnp.einsum('bhts,bhsv->bhtv', scores, vc)
             + jnp.einsum('bhtk,bhkv->bhtv', qc * jnp.exp(b), S))
        b_last = b[:, :, -1:, :]
        S = (jnp.exp(b_last[:, :, 0, :])[..., None] * S
             + jnp.einsum('bhsk,bhsv->bhkv', kc * jnp.exp(b_last - b), vc))
        return S, o

    S0 = jnp.zeros((q.shape[1], q.shape[2], q.shape[4], v.shape[4]), jnp.float32)
    _, o = lax.scan(step, S0, (q, k, v, logf))
    return o


def _hgrn2_mixer(u, lb, gnorm_g):
    Bn, L, _ = u.shape
    q, f, i, g = jnp.split(u, 4, axis=-1)
    q = jax.nn.silu(q.astype(jnp.float32))
    forget = lb + (1.0 - lb) * jax.nn.sigmoid(f.astype(jnp.float32))
    k = 1.0 - forget
    logf = jnp.log(forget)
    v = i.astype(jnp.float32)
    pad = CHUNK - N_META
    Lp = L + pad
    n_chunks = Lp // CHUNK

    def to_chunks(t):
        t = jnp.pad(t, ((0, 0), (pad, 0), (0, 0)))
        t = t.reshape(Bn, n_chunks, CHUNK, HGRN_HEADS, HGRN_HEAD_DIM)
        return t.transpose(1, 0, 3, 2, 4)

    o = _hgrn2_chunk_scan(to_chunks(q), to_chunks(k), to_chunks(v), to_chunks(logf))
    o = o.transpose(1, 0, 3, 2, 4).reshape(Bn, Lp, HGRN_HEADS, HGRN_HEAD_DIM)[:, pad:]
    gh = g.reshape(Bn, L, HGRN_HEADS, HGRN_HEAD_DIM).astype(jnp.float32)
    o = _rmsnorm(o, gnorm_g) * jax.nn.silu(gh)
    return o.reshape(Bn, L, D_HGRN).astype(u.dtype)


def _fwd_setup_inputs(seed: int = 0) -> dict:
    key = jax.random.key(seed)
    ks = jax.random.split(key, 24)
    f32 = jnp.float32
    nrm = lambda k, shape, s: jax.random.normal(k, shape, f32) * s
    return {
        'x': nrm(ks[0], (BATCH, SEQ, D_MODEL), 1.0),
        'meta_tokens': nrm(ks[1], (N_META, D_MODEL), 1.0),
        'mix_norm_g': 1.0 + nrm(ks[2], (DEPTH, D_MODEL), 0.02),
        'mlp_norm_g': 1.0 + nrm(ks[3], (DEPTH, D_MODEL), 0.02),
        'final_norm_g': 1.0 + nrm(ks[4], (D_MODEL,), 0.02),
        'ev_w_in': nrm(ks[5], (N_EVEN, D_MODEL, D_EVEN_IN), D_MODEL ** -0.5),
        'ev_conv_w': nrm(ks[6], (N_EVEN, CONV_WIDTH, D_CONV), CONV_WIDTH ** -0.5),
        'ev_conv_b': nrm(ks[7], (N_EVEN, D_CONV), 0.01),
        'ev_ln_g': 1.0 + nrm(ks[8], (N_EVEN, D_CONV), 0.02),
        'ev_ln_b': nrm(ks[9], (N_EVEN, D_CONV), 0.01),
        'ev_pool_w': nrm(ks[10], (N_EVEN, N_POOL_GROUPS, POOL_GROUP_DIM, POOL_GROUP_DIM), POOL_GROUP_DIM ** -0.5),
        'ev_pool_b': nrm(ks[11], (N_EVEN, N_POOL_GROUPS, POOL_GROUP_DIM), 0.01),
        'ev_pool_scale': 1.0 + nrm(ks[12], (N_EVEN, D_POOL), 0.02),
        'ev_w_out': nrm(ks[13], (N_EVEN, D_CONV + D_POOL, D_MODEL), (D_CONV + D_POOL) ** -0.5),
        'od_w_in': nrm(ks[14], (N_ODD, D_MODEL, 4 * D_HGRN), D_MODEL ** -0.5),
        'od_gnorm_g': 1.0 + nrm(ks[15], (N_ODD, HGRN_HEAD_DIM), 0.02),
        'od_w_out': nrm(ks[16], (N_ODD, D_HGRN, D_MODEL), D_HGRN ** -0.5),
        'lb_param': nrm(ks[17], (DEPTH, D_HGRN), 1.0),
        'mlp_w1': nrm(ks[18], (DEPTH, D_MODEL, D_FF), D_MODEL ** -0.5),
        'mlp_w2': nrm(ks[19], (DEPTH, D_FF, D_MODEL), D_FF ** -0.5),
    }


def _fwd_reference(x, meta_tokens, mix_norm_g, mlp_norm_g, final_norm_g,
              ev_w_in, ev_conv_w, ev_conv_b, ev_ln_g, ev_ln_b,
              ev_pool_w, ev_pool_b, ev_pool_scale, ev_w_out,
              od_w_in, od_gnorm_g, od_w_out, lb_param, mlp_w1, mlp_w2):
    Bn = x.shape[0]
    meta = jnp.broadcast_to(meta_tokens[None].astype(x.dtype), (Bn, N_META, D_MODEL))
    h = jnp.concatenate([meta, x], axis=1)
    lb_all = jnp.cumsum(jax.nn.softmax(lb_param.astype(jnp.float32), axis=0), axis=0)
    lb_all = lb_all - lb_all[0]
    for layer in range(DEPTH):
        j = layer // 2
        n = _rmsnorm(h, mix_norm_g[layer])
        if layer % 2 == 0:
            u = n @ ev_w_in[j]
            val, gate, pin = jnp.split(u, [D_CONV, 2 * D_CONV], axis=-1)
            ya = _conv_mixer(val, gate, ev_conv_w[j], ev_conv_b[j], ev_ln_g[j], ev_ln_b[j])
            yb = _pool_mixer(pin, ev_pool_w[j], ev_pool_b[j], ev_pool_scale[j])
            h = h + jnp.concatenate([ya, yb], axis=-1) @ ev_w_out[j]
        else:
            u = n @ od_w_in[j]
            y = _hgrn2_mixer(u, lb_all[layer], od_gnorm_g[j])
            h = h + y @ od_w_out[j]
        n = _rmsnorm(h, mlp_norm_g[layer])
        h = h + jnp.square(jax.nn.relu(n @ mlp_w1[layer])) @ mlp_w2[layer]
    return _rmsnorm(h, final_norm_g)[:, N_META:]


import jax as _jax
import jax.numpy as _jnp

TWIN_FORMAT = 'train_step'
FWD_PARAMS = ['x', 'meta_tokens', 'mix_norm_g', 'mlp_norm_g', 'final_norm_g', 'ev_w_in', 'ev_conv_w', 'ev_conv_b', 'ev_ln_g', 'ev_ln_b', 'ev_pool_w', 'ev_pool_b', 'ev_pool_scale', 'ev_w_out', 'od_w_in', 'od_gnorm_g', 'od_w_out', 'lb_param', 'mlp_w1', 'mlp_w2']
TWIN_WEIGHTS = ['meta_tokens', 'mix_norm_g', 'mlp_norm_g', 'final_norm_g', 'ev_w_in', 'ev_conv_w', 'ev_conv_b', 'ev_ln_g', 'ev_ln_b', 'ev_pool_w', 'ev_pool_b', 'ev_pool_scale', 'ev_w_out', 'od_w_in', 'od_gnorm_g', 'od_w_out', 'lb_param', 'mlp_w1', 'mlp_w2']
TWIN_DIFF_INPUT = 'x'
TWIN_INPUTS = ['x', 'meta_tokens', 'mix_norm_g', 'mlp_norm_g', 'final_norm_g', 'ev_w_in', 'ev_conv_w', 'ev_conv_b', 'ev_ln_g', 'ev_ln_b', 'ev_pool_w', 'ev_pool_b', 'ev_pool_scale', 'ev_w_out', 'od_w_in', 'od_gnorm_g', 'od_w_out', 'lb_param', 'mlp_w1', 'mlp_w2', 'loss_target', 'm_meta_tokens', 'm_mix_norm_g', 'm_mlp_norm_g', 'm_final_norm_g', 'm_ev_w_in', 'm_ev_conv_w', 'm_ev_conv_b', 'm_ev_ln_g', 'm_ev_ln_b', 'm_ev_pool_w', 'm_ev_pool_b', 'm_ev_pool_scale', 'm_ev_w_out', 'm_od_w_in', 'm_od_gnorm_g', 'm_od_w_out', 'm_lb_param', 'm_mlp_w1', 'm_mlp_w2', 'v_meta_tokens', 'v_mix_norm_g', 'v_mlp_norm_g', 'v_final_norm_g', 'v_ev_w_in', 'v_ev_conv_w', 'v_ev_conv_b', 'v_ev_ln_g', 'v_ev_ln_b', 'v_ev_pool_w', 'v_ev_pool_b', 'v_ev_pool_scale', 'v_ev_w_out', 'v_od_w_in', 'v_od_gnorm_g', 'v_od_w_out', 'v_lb_param', 'v_mlp_w1', 'v_mlp_w2']
TWIN_OUTPUTS = ['loss', 'grad_x', 'grad_meta_tokens', 'grad_mix_norm_g', 'grad_mlp_norm_g', 'grad_final_norm_g', 'grad_ev_w_in', 'grad_ev_conv_w', 'grad_ev_conv_b', 'grad_ev_ln_g', 'grad_ev_ln_b', 'grad_ev_pool_w', 'grad_ev_pool_b', 'grad_ev_pool_scale', 'grad_ev_w_out', 'grad_od_w_in', 'grad_od_gnorm_g', 'grad_od_w_out', 'grad_lb_param', 'grad_mlp_w1', 'grad_mlp_w2', 'delta_meta_tokens', 'delta_mix_norm_g', 'delta_mlp_norm_g', 'delta_final_norm_g', 'delta_ev_w_in', 'delta_ev_conv_w', 'delta_ev_conv_b', 'delta_ev_ln_g', 'delta_ev_ln_b', 'delta_ev_pool_w', 'delta_ev_pool_b', 'delta_ev_pool_scale', 'delta_ev_w_out', 'delta_od_w_in', 'delta_od_gnorm_g', 'delta_od_w_out', 'delta_lb_param', 'delta_mlp_w1', 'delta_mlp_w2', 'new_m_meta_tokens', 'new_m_mix_norm_g', 'new_m_mlp_norm_g', 'new_m_final_norm_g', 'new_m_ev_w_in', 'new_m_ev_conv_w', 'new_m_ev_conv_b', 'new_m_ev_ln_g', 'new_m_ev_ln_b', 'new_m_ev_pool_w', 'new_m_ev_pool_b', 'new_m_ev_pool_scale', 'new_m_ev_w_out', 'new_m_od_w_in', 'new_m_od_gnorm_g', 'new_m_od_w_out', 'new_m_lb_param', 'new_m_mlp_w1', 'new_m_mlp_w2', 'new_v_meta_tokens', 'new_v_mix_norm_g', 'new_v_mlp_norm_g', 'new_v_final_norm_g', 'new_v_ev_w_in', 'new_v_ev_conv_w', 'new_v_ev_conv_b', 'new_v_ev_ln_g', 'new_v_ev_ln_b', 'new_v_ev_pool_w', 'new_v_ev_pool_b', 'new_v_ev_pool_scale', 'new_v_ev_w_out', 'new_v_od_w_in', 'new_v_od_gnorm_g', 'new_v_od_w_out', 'new_v_lb_param', 'new_v_mlp_w1', 'new_v_mlp_w2']
TWIN_LEAF_KINDS = {'loss': 'loss', 'grad_x': 'grad_x', 'grad_meta_tokens': 'grad_w', 'grad_mix_norm_g': 'grad_w', 'grad_mlp_norm_g': 'grad_w', 'grad_final_norm_g': 'grad_w', 'grad_ev_w_in': 'grad_w', 'grad_ev_conv_w': 'grad_w', 'grad_ev_conv_b': 'grad_w', 'grad_ev_ln_g': 'grad_w', 'grad_ev_ln_b': 'grad_w', 'grad_ev_pool_w': 'grad_w', 'grad_ev_pool_b': 'grad_w', 'grad_ev_pool_scale': 'grad_w', 'grad_ev_w_out': 'grad_w', 'grad_od_w_in': 'grad_w', 'grad_od_gnorm_g': 'grad_w', 'grad_od_w_out': 'grad_w', 'grad_lb_param': 'grad_w', 'grad_mlp_w1': 'grad_w', 'grad_mlp_w2': 'grad_w', 'delta_meta_tokens': 'delta_w', 'delta_mix_norm_g': 'delta_w', 'delta_mlp_norm_g': 'delta_w', 'delta_final_norm_g': 'delta_w', 'delta_ev_w_in': 'delta_w', 'delta_ev_conv_w': 'delta_w', 'delta_ev_conv_b': 'delta_w', 'delta_ev_ln_g': 'delta_w', 'delta_ev_ln_b': 'delta_w', 'delta_ev_pool_w': 'delta_w', 'delta_ev_pool_b': 'delta_w', 'delta_ev_pool_scale': 'delta_w', 'delta_ev_w_out': 'delta_w', 'delta_od_w_in': 'delta_w', 'delta_od_gnorm_g': 'delta_w', 'delta_od_w_out': 'delta_w', 'delta_lb_param': 'delta_w', 'delta_mlp_w1': 'delta_w', 'delta_mlp_w2': 'delta_w', 'new_m_meta_tokens': 'new_m', 'new_m_mix_norm_g': 'new_m', 'new_m_mlp_norm_g': 'new_m', 'new_m_final_norm_g': 'new_m', 'new_m_ev_w_in': 'new_m', 'new_m_ev_conv_w': 'new_m', 'new_m_ev_conv_b': 'new_m', 'new_m_ev_ln_g': 'new_m', 'new_m_ev_ln_b': 'new_m', 'new_m_ev_pool_w': 'new_m', 'new_m_ev_pool_b': 'new_m', 'new_m_ev_pool_scale': 'new_m', 'new_m_ev_w_out': 'new_m', 'new_m_od_w_in': 'new_m', 'new_m_od_gnorm_g': 'new_m', 'new_m_od_w_out': 'new_m', 'new_m_lb_param': 'new_m', 'new_m_mlp_w1': 'new_m', 'new_m_mlp_w2': 'new_m', 'new_v_meta_tokens': 'new_v', 'new_v_mix_norm_g': 'new_v', 'new_v_mlp_norm_g': 'new_v', 'new_v_final_norm_g': 'new_v', 'new_v_ev_w_in': 'new_v', 'new_v_ev_conv_w': 'new_v', 'new_v_ev_conv_b': 'new_v', 'new_v_ev_ln_g': 'new_v', 'new_v_ev_ln_b': 'new_v', 'new_v_ev_pool_w': 'new_v', 'new_v_ev_pool_b': 'new_v', 'new_v_ev_pool_scale': 'new_v', 'new_v_ev_w_out': 'new_v', 'new_v_od_w_in': 'new_v', 'new_v_od_gnorm_g': 'new_v', 'new_v_od_w_out': 'new_v', 'new_v_lb_param': 'new_v', 'new_v_mlp_w1': 'new_v', 'new_v_mlp_w2': 'new_v'}


def _forward(args):
    return _fwd_reference(*[args[k] for k in FWD_PARAMS])


def _output_shape():
    def fwd():
        inp = _fwd_setup_inputs(0)
        return _fwd_reference(*[inp[k] for k in FWD_PARAMS])
    out = _jax.eval_shape(fwd)
    return out.shape, out.dtype

N_MICROBATCH = 1
ADAM_LR = 0.001
ADAM_B1 = 0.9
ADAM_B2 = 0.999
ADAM_EPS = 1e-08
ADAM_WD = 0.01
ADAM_STEP = 10
PER_EXAMPLE_BATCH_AXIS = {'x': 0, 'loss_target': 0}
SHARED_INPUTS = []
_WEIGHT_DTYPES = {'meta_tokens': _jnp.float32, 'mix_norm_g': _jnp.float32, 'mlp_norm_g': _jnp.float32, 'final_norm_g': _jnp.float32, 'ev_w_in': _jnp.float32, 'ev_conv_w': _jnp.float32, 'ev_conv_b': _jnp.float32, 'ev_ln_g': _jnp.float32, 'ev_ln_b': _jnp.float32, 'ev_pool_w': _jnp.float32, 'ev_pool_b': _jnp.float32, 'ev_pool_scale': _jnp.float32, 'ev_w_out': _jnp.float32, 'od_w_in': _jnp.float32, 'od_gnorm_g': _jnp.float32, 'od_w_out': _jnp.float32, 'lb_param': _jnp.float32, 'mlp_w1': _jnp.float32, 'mlp_w2': _jnp.float32}
MOMENT_SCALE = {'meta_tokens': 6.066529e-03, 'mix_norm_g': 9.718748e-02, 'mlp_norm_g': 1.287842e-01, 'final_norm_g': 3.282971e+01, 'ev_w_in': 8.952116e-02, 'ev_conv_w': 8.947767e-02, 'ev_conv_b': 2.369474e-01, 'ev_ln_g': 1.193342e-01, 'ev_ln_b': 1.291914e-01, 'ev_pool_w': 1.217871e-01, 'ev_pool_b': 2.498034e-01, 'ev_pool_scale': 1.305199e-01, 'ev_w_out': 1.091798e-01, 'od_w_in': 4.275749e-02, 'od_gnorm_g': 1.599080e-01, 'od_w_out': 6.071497e-02, 'lb_param': 2.587845e-03, 'mlp_w1': 6.412550e-02, 'mlp_w2': 1.356333e-01}


def _to_microbatches(a, axis):
    t = _jnp.moveaxis(a, axis, 0)
    t = t.reshape((N_MICROBATCH, t.shape[0] // N_MICROBATCH) + t.shape[1:])
    return _jnp.moveaxis(t, 1, axis + 1)


def setup_inputs(seed: int = 0) -> dict:
    inp = _fwd_setup_inputs(seed)
    key = _jax.random.fold_in(_jax.random.key(seed), 7919)
    shape, _ = _output_shape()
    out = dict(inp)
    out["loss_target"] = _jax.random.normal(_jax.random.fold_in(key, 0), shape, _jnp.float32)
    for i, name in enumerate(TWIN_WEIGHTS):
        w = inp[name].astype(_jnp.float32)
        if MOMENT_SCALE is None:
            s = _jnp.sqrt(_jnp.mean(_jnp.square(w)) + 1e-30)
        else:
            s = MOMENT_SCALE[name]
        km, kv = _jax.random.split(_jax.random.fold_in(key, i + 1))
        out[name] = w
        out["m_" + name] = s * _jax.random.normal(km, w.shape, _jnp.float32)
        out["v_" + name] = (s * s) * _jax.random.uniform(kv, w.shape, _jnp.float32, 0.5, 1.5)
    if N_MICROBATCH > 1:
        for name, axis in PER_EXAMPLE_BATCH_AXIS.items():
            out[name] = _to_microbatches(out[name], axis)
    return {'x': out['x'], 'meta_tokens': out['meta_tokens'], 'mix_norm_g': out['mix_norm_g'], 'mlp_norm_g': out['mlp_norm_g'], 'final_norm_g': out['final_norm_g'], 'ev_w_in': out['ev_w_in'], 'ev_conv_w': out['ev_conv_w'], 'ev_conv_b': out['ev_conv_b'], 'ev_ln_g': out['ev_ln_g'], 'ev_ln_b': out['ev_ln_b'], 'ev_pool_w': out['ev_pool_w'], 'ev_pool_b': out['ev_pool_b'], 'ev_pool_scale': out['ev_pool_scale'], 'ev_w_out': out['ev_w_out'], 'od_w_in': out['od_w_in'], 'od_gnorm_g': out['od_gnorm_g'], 'od_w_out': out['od_w_out'], 'lb_param': out['lb_param'], 'mlp_w1': out['mlp_w1'], 'mlp_w2': out['mlp_w2'], 'loss_target': out['loss_target'], 'm_meta_tokens': out['m_meta_tokens'], 'm_mix_norm_g': out['m_mix_norm_g'], 'm_mlp_norm_g': out['m_mlp_norm_g'], 'm_final_norm_g': out['m_final_norm_g'], 'm_ev_w_in': out['m_ev_w_in'], 'm_ev_conv_w': out['m_ev_conv_w'], 'm_ev_conv_b': out['m_ev_conv_b'], 'm_ev_ln_g': out['m_ev_ln_g'], 'm_ev_ln_b': out['m_ev_ln_b'], 'm_ev_pool_w': out['m_ev_pool_w'], 'm_ev_pool_b': out['m_ev_pool_b'], 'm_ev_pool_scale': out['m_ev_pool_scale'], 'm_ev_w_out': out['m_ev_w_out'], 'm_od_w_in': out['m_od_w_in'], 'm_od_gnorm_g': out['m_od_gnorm_g'], 'm_od_w_out': out['m_od_w_out'], 'm_lb_param': out['m_lb_param'], 'm_mlp_w1': out['m_mlp_w1'], 'm_mlp_w2': out['m_mlp_w2'], 'v_meta_tokens': out['v_meta_tokens'], 'v_mix_norm_g': out['v_mix_norm_g'], 'v_mlp_norm_g': out['v_mlp_norm_g'], 'v_final_norm_g': out['v_final_norm_g'], 'v_ev_w_in': out['v_ev_w_in'], 'v_ev_conv_w': out['v_ev_conv_w'], 'v_ev_conv_b': out['v_ev_conv_b'], 'v_ev_ln_g': out['v_ev_ln_g'], 'v_ev_ln_b': out['v_ev_ln_b'], 'v_ev_pool_w': out['v_ev_pool_w'], 'v_ev_pool_b': out['v_ev_pool_b'], 'v_ev_pool_scale': out['v_ev_pool_scale'], 'v_ev_w_out': out['v_ev_w_out'], 'v_od_w_in': out['v_od_w_in'], 'v_od_gnorm_g': out['v_od_gnorm_g'], 'v_od_w_out': out['v_od_w_out'], 'v_lb_param': out['v_lb_param'], 'v_mlp_w1': out['v_mlp_w1'], 'v_mlp_w2': out['v_mlp_w2']}


def _loss(weights, diff, rest, loss_target):
    with _jax.named_scope("forward"):
        args = {**rest, TWIN_DIFF_INPUT: diff, **{k: w.astype(_WEIGHT_DTYPES[k]) for k, w in weights.items()}}
        y = _forward(args)
    with _jax.named_scope("loss_head"):
        err = _jnp.square(y.astype(_jnp.float32) - loss_target)
        return 0.5 * _jnp.sum(_jnp.mean(err, axis=-1)) if err.ndim else 0.5 * err


def _adamw(w, g, m, v):
    m = ADAM_B1 * m + (1.0 - ADAM_B1) * g
    v = ADAM_B2 * v + (1.0 - ADAM_B2) * _jnp.square(g)
    m_hat = m / (1.0 - ADAM_B1 ** ADAM_STEP)
    v_hat = v / (1.0 - ADAM_B2 ** ADAM_STEP)
    delta = -ADAM_LR * (m_hat / (_jnp.sqrt(v_hat) + ADAM_EPS) + ADAM_WD * w)
    return delta, m, v


def reference(x, meta_tokens, mix_norm_g, mlp_norm_g, final_norm_g, ev_w_in, ev_conv_w, ev_conv_b, ev_ln_g, ev_ln_b, ev_pool_w, ev_pool_b, ev_pool_scale, ev_w_out, od_w_in, od_gnorm_g, od_w_out, lb_param, mlp_w1, mlp_w2, loss_target, m_meta_tokens, m_mix_norm_g, m_mlp_norm_g, m_final_norm_g, m_ev_w_in, m_ev_conv_w, m_ev_conv_b, m_ev_ln_g, m_ev_ln_b, m_ev_pool_w, m_ev_pool_b, m_ev_pool_scale, m_ev_w_out, m_od_w_in, m_od_gnorm_g, m_od_w_out, m_lb_param, m_mlp_w1, m_mlp_w2, v_meta_tokens, v_mix_norm_g, v_mlp_norm_g, v_final_norm_g, v_ev_w_in, v_ev_conv_w, v_ev_conv_b, v_ev_ln_g, v_ev_ln_b, v_ev_pool_w, v_ev_pool_b, v_ev_pool_scale, v_ev_w_out, v_od_w_in, v_od_gnorm_g, v_od_w_out, v_lb_param, v_mlp_w1, v_mlp_w2):
    given = dict(x=x, meta_tokens=meta_tokens, mix_norm_g=mix_norm_g, mlp_norm_g=mlp_norm_g, final_norm_g=final_norm_g, ev_w_in=ev_w_in, ev_conv_w=ev_conv_w, ev_conv_b=ev_conv_b, ev_ln_g=ev_ln_g, ev_ln_b=ev_ln_b, ev_pool_w=ev_pool_w, ev_pool_b=ev_pool_b, ev_pool_scale=ev_pool_scale, ev_w_out=ev_w_out, od_w_in=od_w_in, od_gnorm_g=od_gnorm_g, od_w_out=od_w_out, lb_param=lb_param, mlp_w1=mlp_w1, mlp_w2=mlp_w2, loss_target=loss_target, m_meta_tokens=m_meta_tokens, m_mix_norm_g=m_mix_norm_g, m_mlp_norm_g=m_mlp_norm_g, m_final_norm_g=m_final_norm_g, m_ev_w_in=m_ev_w_in, m_ev_conv_w=m_ev_conv_w, m_ev_conv_b=m_ev_conv_b, m_ev_ln_g=m_ev_ln_g, m_ev_ln_b=m_ev_ln_b, m_ev_pool_w=m_ev_pool_w, m_ev_pool_b=m_ev_pool_b, m_ev_pool_scale=m_ev_pool_scale, m_ev_w_out=m_ev_w_out, m_od_w_in=m_od_w_in, m_od_gnorm_g=m_od_gnorm_g, m_od_w_out=m_od_w_out, m_lb_param=m_lb_param, m_mlp_w1=m_mlp_w1, m_mlp_w2=m_mlp_w2, v_meta_tokens=v_meta_tokens, v_mix_norm_g=v_mix_norm_g, v_mlp_norm_g=v_mlp_norm_g, v_final_norm_g=v_final_norm_g, v_ev_w_in=v_ev_w_in, v_ev_conv_w=v_ev_conv_w, v_ev_conv_b=v_ev_conv_b, v_ev_ln_g=v_ev_ln_g, v_ev_ln_b=v_ev_ln_b, v_ev_pool_w=v_ev_pool_w, v_ev_pool_b=v_ev_pool_b, v_ev_pool_scale=v_ev_pool_scale, v_ev_w_out=v_ev_w_out, v_od_w_in=v_od_w_in, v_od_gnorm_g=v_od_gnorm_g, v_od_w_out=v_od_w_out, v_lb_param=v_lb_param, v_mlp_w1=v_mlp_w1, v_mlp_w2=v_mlp_w2)
    weights = {n: given[n] for n in TWIN_WEIGHTS}
    shared = {n: given[n] for n in SHARED_INPUTS}
    per_example = {n: given[n] for n in ['x']}
    grad_fn = _jax.value_and_grad(_loss, argnums=(0, 1))

    def one_microbatch(ex, loss_target):
        ex = dict(ex)
        diff = ex.pop(TWIN_DIFF_INPUT)
        return grad_fn(weights, diff, {**shared, **ex}, loss_target)

    if N_MICROBATCH == 1:
        loss, (grad_w, grad_x) = one_microbatch(per_example, given["loss_target"])
    else:
        def body(carry, xs):
            loss_sum, grad_sum = carry
            l_k, (gw_k, gx_k) = one_microbatch(xs[0], xs[1])
            with _jax.named_scope("update"):
                return (loss_sum + l_k, _jax.tree.map(_jnp.add, grad_sum, gw_k)), gx_k

        init = (_jnp.zeros((), _jnp.float32), _jax.tree.map(_jnp.zeros_like, weights))
        (loss, grad_w), grad_x = _jax.lax.scan(body, init, (per_example, given["loss_target"]))
    with _jax.named_scope("update"):
        delta_w, new_m, new_v = {}, {}, {}
        for n in TWIN_WEIGHTS:
            delta_w[n], new_m[n], new_v[n] = _adamw(weights[n], grad_w[n], given["m_" + n], given["v_" + n])
    return (loss, grad_x, *[grad_w[n] for n in TWIN_WEIGHTS], *[delta_w[n] for n in TWIN_WEIGHTS],
            *[new_m[n] for n in TWIN_WEIGHTS], *[new_v[n] for n in TWIN_WEIGHTS])
```

```python
import functools

import jax
import jax.numpy as jnp
from jax import lax
from jax.experimental import pallas as pl
from jax.experimental.pallas import tpu as pltpu

F32 = jnp.float32
BF16 = jnp.bfloat16
SDS = jax.ShapeDtypeStruct
MESH = pl.DeviceIdType.MESH

N_META = 16
CHUNK = 64
LEAD = CHUNK
PAD = LEAD - N_META
CONV_WIDTH = 31
CONV_ROWS = 32
POOL_WINDOWS = (2, 4, 8, 16)
HEAD_DIM = 128
SUB = 16
EXP_CAP = 80.0
EPS = 1e-6
ADAM_LR = 0.001
ADAM_B1 = 0.9
ADAM_B2 = 0.999
ADAM_EPS = 1e-08
ADAM_WD = 0.01
ADAM_STEP = 10
N_CHIPS = 4
VMEM_LIMIT = 52 << 20


def _params(*sem):
    return pltpu.CompilerParams(dimension_semantics=sem if sem else None, vmem_limit_bytes=VMEM_LIMIT)


def _tile(n, target, unit=CHUNK):
    best = None
    for t in range(unit, min(n, target) + 1, unit):
        if n % t == 0:
            best = t
    assert best is not None, (n, target, unit)
    return best


def _ctile(n, target=512):
    for t in (512, 384, 256, 128):
        if t <= target and n % t == 0:
            return t
    raise ValueError(n)


def _sigmoid(x):
    return 1.0 / (1.0 + jnp.exp(-x))


def _row_ids(shape, base):
    return lax.broadcasted_iota(jnp.int32, shape, 0) + base


def _cast_bf16(w2d, name):
    R, C = w2d.shape
    tr = _tile(R, 512, 16)

    def body(w_ref, o_ref):
        o_ref[...] = w_ref[...].astype(BF16)

    return pl.pallas_call(
        body, grid=(R // tr,), in_specs=[pl.BlockSpec((tr, C), lambda i: (i, 0))],
        out_specs=pl.BlockSpec((tr, C), lambda i: (i, 0)), out_shape=SDS((R, C), BF16),
        name=name, compiler_params=_params("parallel"))(w2d)


def _rms_fwd(h, g3, layer, name):
    T, D = h.shape
    tm = _tile(T, 832)

    def body(h_ref, g_ref, n_ref):
        x = h_ref[...]
        r = lax.rsqrt(jnp.mean(x * x, axis=-1, keepdims=True) + EPS)
        n_ref[...] = ((x * r) * g_ref[...]).astype(BF16)

    return pl.pallas_call(
        body, grid=(T // tm,),
        in_specs=[pl.BlockSpec((tm, D), lambda i: (i, 0)), pl.BlockSpec((None, 1, D), lambda i: (layer, 0, 0))],
        out_specs=pl.BlockSpec((tm, D), lambda i: (i, 0)), out_shape=SDS((T, D), BF16),
        name=name, compiler_params=_params("parallel"))(h, g3)


def _rms_bwd(h, g3, layer, dn, dh_in, name):
    T, D = h.shape
    tm = _tile(T, 320)

    def body(h_ref, g_ref, dn_ref, dhi_ref, dh_ref, dhb_ref, dg_ref):
        x = h_ref[...]
        r = lax.rsqrt(jnp.mean(x * x, axis=-1, keepdims=True) + EPS)
        xh = x * r
        dn_ = dn_ref[...]
        dxh = dn_ * g_ref[...]
        dh = dhi_ref[...] + r * (dxh - xh * jnp.mean(dxh * xh, axis=-1, keepdims=True))
        dh_ref[...] = dh
        dhb_ref[...] = dh.astype(BF16)

        @pl.when(pl.program_id(0) == 0)
        def _():
            dg_ref[...] = jnp.zeros_like(dg_ref)

        dg_ref[...] += jnp.sum(dn_ * xh, axis=0, keepdims=True)

    row = pl.BlockSpec((tm, D), lambda i: (i, 0))
    return pl.pallas_call(
        body, grid=(T // tm,),
        in_specs=[row, pl.BlockSpec((None, 1, D), lambda i: (layer, 0, 0)), row, row],
        out_specs=[row, row, pl.BlockSpec((1, D), lambda i: (0, 0))],
        out_shape=[SDS((T, D), F32), SDS((T, D), BF16), SDS((1, D), F32)],
        name=name, compiler_params=_params("arbitrary"))(h, g3, dn, dh_in)


def _final(h, g2, target):
    T, D = h.shape
    tm = CHUNK

    def body(h_ref, g_ref, t_ref, dh_ref, dhb_ref, dg_ref, loss_ref):
        i = pl.program_id(0)
        x = h_ref[...]
        r = lax.rsqrt(jnp.mean(x * x, axis=-1, keepdims=True) + EPS)
        xh = x * r
        g = g_ref[...]
        live = jnp.where(i > 0, 1.0, 0.0).astype(F32)
        e = ((xh * g) - t_ref[...]) * live
        dy = e * (1.0 / D)
        dxh = dy * g
        dh = r * (dxh - xh * jnp.mean(dxh * xh, axis=-1, keepdims=True))
        dh_ref[...] = dh
        dhb_ref[...] = dh.astype(BF16)

        @pl.when(i == 0)
        def _():
            dg_ref[...] = jnp.zeros_like(dg_ref)
            loss_ref[...] = jnp.zeros_like(loss_ref)

        dg_ref[...] += jnp.sum(dy * xh, axis=0, keepdims=True)
        loss_ref[...] += jnp.sum(e * e) * (0.5 / D)

    row = pl.BlockSpec((tm, D), lambda i: (i, 0))
    return pl.pallas_call(
        body, grid=(T // tm,),
        in_specs=[row, pl.BlockSpec((1, D), lambda i: (0, 0)),
                  pl.BlockSpec((tm, D), lambda i: (jnp.maximum(i - 1, 0), 0))],
        out_specs=[row, row, pl.BlockSpec((1, D), lambda i: (0, 0)), pl.BlockSpec((1, 128), lambda i: (0, 0))],
        out_shape=[SDS((T, D), F32), SDS((T, D), BF16), SDS((1, D), F32), SDS((1, 128), F32)],
        name="final_loss", compiler_params=_params("arbitrary"))(h, g2, target)


def _mm_nn(a, w3, layer, name, res=None, relu2=False):
    M, K = a.shape
    N = w3.shape[2]
    tm = _tile(M, 832)
    tn = _ctile(N, 512 if K <= 2048 else 256)

    def body(*refs):
        a_ref, w_ref = refs[0], refs[1]
        acc = jnp.dot(a_ref[...], w_ref[...], preferred_element_type=F32)
        k = 2
        if res is not None:
            acc = acc + refs[k][...]
            k += 1
        refs[k][...] = acc
        if relu2:
            p = jnp.maximum(acc, 0.0)
            refs[k + 1][...] = (p * p).astype(BF16)

    in_specs = [pl.BlockSpec((tm, K), lambda i, j: (i, 0)), pl.BlockSpec((None, K, tn), lambda i, j: (layer, 0, j))]
    args = [a, w3]
    tile = pl.BlockSpec((tm, tn), lambda i, j: (i, j))
    if res is not None:
        in_specs.append(tile)
        args.append(res)
    out_specs, out_shape = [tile], [SDS((M, N), F32)]
    if relu2:
        out_specs.append(tile)
        out_shape.append(SDS((M, N), BF16))
    out = pl.pallas_call(
        body, grid=(M // tm, N // tn), in_specs=in_specs, out_specs=out_specs, out_shape=out_shape,
        name=name, compiler_params=_params("parallel", "parallel"))(*args)
    return out if relu2 else out[0]


def _mm_nt(dy, w3, layer, name, z=None):
    M, N = dy.shape
    K = w3.shape[1]
    tm = _tile(M, 832)
    tk = _ctile(K, 512 if N <= 2048 else 256)

    def body(*refs):
        acc = lax.dot_general(refs[0][...], refs[1][...], (((1,), (1,)), ((), ())), preferred_element_type=F32)
        if z is None:
            refs[2][...] = acc
        else:
            refs[3][...] = (acc * (2.0 * jnp.maximum(refs[2][...], 0.0))).astype(BF16)

    tile = pl.BlockSpec((tm, tk), lambda i, j: (i, j))
    in_specs = [pl.BlockSpec((tm, N), lambda i, j: (i, 0)), pl.BlockSpec((None, tk, N), lambda i, j: (layer, j, 0))]
    args = [dy, w3]
    if z is not None:
        in_specs.append(tile)
        args.append(z)
    return pl.pallas_call(
        body, grid=(M // tm, K // tk), in_specs=in_specs, out_specs=tile,
        out_shape=SDS((M, K), F32 if z is None else BF16),
        name=name, compiler_params=_params("parallel", "parallel"))(*args)


def _fam_dims(kind, K, N):
    return (K // 2, N // N_CHIPS) if kind == "col" else (K // (2 * N_CHIPS), N)


def _mm_tn(x, dy, acc, kind, layer, name):
    M, K = x.shape
    N = dy.shape[1]
    nr, nc = _fam_dims(kind, K, N)
    tk = _ctile(nr)
    tn = _ctile(nc)
    rt, ct = nr // tk, nc // tn

    def body(x_ref, dy_ref, acc_ref, o_ref):
        del acc_ref
        o_ref[...] = lax.dot_general(x_ref[...], dy_ref[...], (((0,), (0,)), ((), ())),
                                     preferred_element_type=F32).astype(BF16)

    if kind == "col":
        omap = lambda i, j: (i // rt, j // ct, layer, i % rt, j % ct)
    else:
        omap = lambda i, j: ((i // rt) % 2, i // (2 * rt), layer, i % rt, j)
    return pl.pallas_call(
        body, grid=(K // tk, N // tn),
        in_specs=[pl.BlockSpec((M, tk), lambda i, j: (0, i)), pl.BlockSpec((M, tn), lambda i, j: (0, j)),
                  pl.BlockSpec(memory_space=pl.ANY)],
        out_specs=pl.BlockSpec((None, None, None, tk, tn), omap),
        out_shape=SDS(acc.shape, BF16), input_output_aliases={2: 0},
        name=name, compiler_params=_params("parallel", "parallel"))(x, dy, acc)


C_EVEN = 512


def _live(rows, base, total):
    r = _row_ids((rows, 1), base)
    return jnp.logical_and(r >= PAD, r < total).astype(F32)


def _conv_taps(win, w_ref, ls, acc, flip):
    for b in range(8):
        rb = win if b == 0 else pltpu.roll(win, 96 - b, 0)
        for a in range(5):
            o = 8 * a + b
            tap = (30 - o) if flip else (o - 2)
            if 0 <= tap < CONV_WIDTH:
                acc = acc + w_ref[pl.ds(tap, 1), ls] * rb[8 * a:8 * a + CHUNK]
    return acc


def _window_sum(win, levels, forward):
    s = win
    n = win.shape[0]
    for k in range(levels):
        step = 1 << k
        s = s + pltpu.roll(s, (n - step) if forward else step, 0)
    return s


def _pool_count(base, g):
    pos = _row_ids((CHUNK, 1), base) - PAD
    return jnp.clip(pos + 1, 1, POOL_WINDOWS[g]).astype(F32)


def _even_fwd(u, cw3, cb3, lg3, lb3, pw4, pb3, ps3, j, name):
    T = u.shape[0]
    C = C_EVEN
    tm = _tile(T, 320)
    nch = tm // CHUNK
    nblk = T // CHUNK

    def body(u_ref, up_ref, cw_ref, cb_ref, lg_ref, lb_ref, pw_ref, pb_ref, ps_ref, o_ref, a_s, p_s, yc_s):
        row0 = pl.program_id(0) * tm
        up = up_ref[...]
        lp = _live(CHUNK, row0 - CHUNK, T)
        a_s[0:CHUNK, :] = up[:, 0:C] * _sigmoid(up[:, C:2 * C]) * lp
        p_s[0:CHUNK, :] = up[:, 2 * C:3 * C] * lp

        def stage(c, _):
            rs = pl.multiple_of(c * CHUNK, CHUNK)
            lv = _live(CHUNK, row0 + rs, T)
            a_s[pl.ds(rs + CHUNK, CHUNK), :] = u_ref[pl.ds(rs, CHUNK), 0:C] * _sigmoid(u_ref[pl.ds(rs, CHUNK), C:2 * C]) * lv
            p_s[pl.ds(rs + CHUNK, CHUNK), :] = u_ref[pl.ds(rs, CHUNK), 2 * C:3 * C] * lv
            return 0

        lax.fori_loop(0, nch, stage, 0)

        def chunk(c, _):
            rs = pl.multiple_of(c * CHUNK, CHUNK)
            lv = _live(CHUNK, row0 + rs, T)
            for cb in range(4):
                ls = slice(cb * 128, (cb + 1) * 128)
                win = a_s[pl.ds(pl.multiple_of(rs + 32, 32), 96), ls]
                acc = jnp.broadcast_to(cb_ref[:, ls], (CHUNK, 128))
                yc_s[:, ls] = _conv_taps(win, cw_ref, ls, acc, False)
            y = yc_s[...]
            xc = y - jnp.mean(y, axis=-1, keepdims=True)
            yn = xc * lax.rsqrt(jnp.mean(xc * xc, axis=-1, keepdims=True) + EPS) * lg_ref[...] + lb_ref[...]
            o_ref[pl.ds(rs, CHUNK), 0:C] = (yn * _sigmoid(yn) * lv).astype(BF16)
            for g in range(4):
                ls = slice(g * 128, (g + 1) * 128)
                win = p_s[pl.ds(pl.multiple_of(rs + 48, 16), 80), ls]
                s = _window_sum(win, g + 1, False)
                d = s[16:80] / _pool_count(row0 + rs, g) - win[16:80]
                yv = jnp.dot(d.astype(BF16), pw_ref[g].astype(BF16), preferred_element_type=F32) + pb_ref[:, ls]
                o_ref[pl.ds(rs, CHUNK), C + g * 128:C + (g + 1) * 128] = (yv * ps_ref[:, ls] * lv).astype(BF16)
            return 0

        lax.fori_loop(0, nch, chunk, 0)

    vec = pl.BlockSpec((None, 1, C), lambda i: (j, 0, 0))
    return pl.pallas_call(
        body, grid=(T // tm,),
        in_specs=[pl.BlockSpec((tm, 3 * C), lambda i: (i, 0)),
                  pl.BlockSpec((CHUNK, 3 * C), lambda i: (jnp.maximum(i * nch - 1, 0), 0)),
                  pl.BlockSpec((None, CONV_ROWS, C), lambda i: (j, 0, 0)), vec, vec, vec,
                  pl.BlockSpec((None, 4, 128, 128), lambda i: (j, 0, 0, 0)), vec, vec],
        out_specs=pl.BlockSpec((tm, 2 * C), lambda i: (i, 0)),
        out_shape=SDS((T, 2 * C), BF16),
        scratch_shapes=[pltpu.VMEM((tm + CHUNK, C), F32), pltpu.VMEM((tm + CHUNK, C), F32), pltpu.VMEM((CHUNK, C), F32)],
        name=name, compiler_params=_params("parallel"))(u, u, cw3, cb3, lg3, lb3, pw4, pb3, ps3)


def _even_bwd(u, dy, cw3, cb3, lg3, lb3, pw4, pb3, ps3, j, name):
    T = u.shape[0]
    C = C_EVEN
    tm = _tile(T, 320)
    nch = tm // CHUNK
    nblk = T // CHUNK
    ntile = T // tm

    def body(u_ref, up_ref, un_ref, dy_ref, dyn_ref, cw_ref, cb_ref, lg_ref, lb_ref, pw_ref, pb_ref, ps_ref,
             du_ref, dcw_ref, dcb_ref, dlg_ref, dlb_ref, dpw_ref, dpb_ref, dps_ref,
             a_s, p_s, dy_s, yc_s, dyc_s, dd_s, ddc_s, dw_s):
        i = pl.program_id(0)
        row0 = i * tm

        @pl.when(i == 0)
        def _():
            for ref in (dcb_ref, dlg_ref, dlb_ref, dpw_ref, dpb_ref, dps_ref, dw_s):
                ref[...] = jnp.zeros_like(ref)

        up = up_ref[...]
        lp = _live(CHUNK, row0 - CHUNK, T)
        a_s[0:CHUNK, :] = up[:, 0:C] * _sigmoid(up[:, C:2 * C]) * lp
        p_s[0:CHUNK, :] = up[:, 2 * C:3 * C] * lp
        un = un_ref[...]
        ln_ = _live(CHUNK, row0 + tm, T)
        a_s[tm + CHUNK:tm + 2 * CHUNK, :] = un[:, 0:C] * _sigmoid(un[:, C:2 * C]) * ln_
        p_s[tm + CHUNK:tm + 2 * CHUNK, :] = un[:, 2 * C:3 * C] * ln_
        dy_s[tm:tm + CHUNK, :] = dyn_ref[...] * ln_
        dyc_s[tm + CHUNK:tm + CHUNK + 32, :] = jnp.zeros((32, C), F32)

        def stage(c, _):
            rs = pl.multiple_of(c * CHUNK, CHUNK)
            lv = _live(CHUNK, row0 + rs, T)
            a_s[pl.ds(rs + CHUNK, CHUNK), :] = u_ref[pl.ds(rs, CHUNK), 0:C] * _sigmoid(u_ref[pl.ds(rs, CHUNK), C:2 * C]) * lv
            p_s[pl.ds(rs + CHUNK, CHUNK), :] = u_ref[pl.ds(rs, CHUNK), 2 * C:3 * C] * lv
            dy_s[pl.ds(rs, CHUNK), :] = dy_ref[pl.ds(rs, CHUNK), :] * lv
            return 0

        lax.fori_loop(0, nch, stage, 0)

        def first(c, _):
            rs = pl.multiple_of(c * CHUNK, CHUNK)
            own = jnp.where(c < nch, 1.0, 0.0).astype(F32)
            for cb in range(4):
                ls = slice(cb * 128, (cb + 1) * 128)
                win = a_s[pl.ds(pl.multiple_of(rs + 32, 32), 96), ls]
                acc = jnp.broadcast_to(cb_ref[:, ls], (CHUNK, 128))
                yc_s[:, ls] = _conv_taps(win, cw_ref, ls, acc, False)
            y = yc_s[...]
            xc = y - jnp.mean(y, axis=-1, keepdims=True)
            rstd = lax.rsqrt(jnp.mean(xc * xc, axis=-1, keepdims=True) + EPS)
            xh = xc * rstd
            yn = xh * lg_ref[...] + lb_ref[...]
            sg = _sigmoid(yn)
            dyn = dy_s[pl.ds(rs, CHUNK), 0:C] * (sg * (1.0 + yn * (1.0 - sg)))
            dlg_ref[...] += jnp.sum(dyn * xh, axis=0, keepdims=True) * own
            dlb_ref[...] += jnp.sum(dyn, axis=0, keepdims=True) * own
            dxh = dyn * lg_ref[...]
            dyc = rstd * (dxh - jnp.mean(dxh, axis=-1, keepdims=True) - xh * jnp.mean(dxh * xh, axis=-1, keepdims=True))
            dyc_s[pl.ds(rs, CHUNK), :] = dyc
            dcb_ref[...] += jnp.sum(dyc, axis=0, keepdims=True) * own
            for g in range(4):
                ls = slice(g * 128, (g + 1) * 128)
                win = p_s[pl.ds(pl.multiple_of(rs + 48, 16), 80), ls]
                s = _window_sum(win, g + 1, False)
                cnt = _pool_count(row0 + rs, g)
                d = (s[16:80] / cnt - win[16:80]).astype(BF16)
                w = pw_ref[g].astype(BF16)
                pre = jnp.dot(d, w, preferred_element_type=F32) + pb_ref[:, ls]
                dyb = dy_s[pl.ds(rs, CHUNK), C + g * 128:C + (g + 1) * 128]
                dpre = dyb * ps_ref[:, ls]
                dps_ref[:, ls] += jnp.sum(dyb * pre, axis=0, keepdims=True) * own
                dpb_ref[:, ls] += jnp.sum(dpre, axis=0, keepdims=True) * own
                dpre_b = (dpre * own).astype(BF16)
                dpw_ref[g] += lax.dot_general(d, dpre_b, (((0,), (0,)), ((), ())), preferred_element_type=F32)
                dd = lax.dot_general(dpre.astype(BF16), w, (((1,), (1,)), ((), ())), preferred_element_type=F32)
                dd_s[pl.ds(rs, CHUNK), ls] = dd
                ddc_s[pl.ds(rs, CHUNK), ls] = dd / cnt
            return 0

        lax.fori_loop(0, nch + 1, first, 0)
        ddc_s[tm + CHUNK:tm + CHUNK + 16, :] = jnp.zeros((16, C), F32)

        def second(c, _):
            rs = pl.multiple_of(c * CHUNK, CHUNK)
            lv = _live(CHUNK, row0 + rs, T)
            for cb in range(4):
                ls = slice(cb * 128, (cb + 1) * 128)
                wd = dyc_s[pl.ds(rs, 96), ls]
                da = _conv_taps(wd, cw_ref, ls, jnp.zeros((CHUNK, 128), F32), True)
                wa = a_s[pl.ds(pl.multiple_of(rs + 32, 32), 96), ls]
                dyc = dyc_s[pl.ds(rs, CHUNK), ls]
                for b in range(8):
                    rb = wa if b == 0 else pltpu.roll(wa, 96 - b, 0)
                    for a in range(5):
                        tap = 8 * a + b - 2
                        if 0 <= tap < CONV_WIDTH:
                            prod = dyc * rb[8 * a:8 * a + CHUNK]
                            part = prod[0:8]
                            for q in range(1, 8):
                                part = part + prod[8 * q:8 * q + 8]
                            dw_s[8 * tap:8 * tap + 8, ls] += part
                val = u_ref[pl.ds(rs, CHUNK), ls]
                sg = _sigmoid(u_ref[pl.ds(rs, CHUNK), C + cb * 128:C + (cb + 1) * 128])
                du_ref[pl.ds(rs, CHUNK), ls] = (da * sg * lv).astype(BF16)
                du_ref[pl.ds(rs, CHUNK), C + cb * 128:C + (cb + 1) * 128] = (da * val * sg * (1.0 - sg) * lv).astype(BF16)
            for g in range(4):
                ls = slice(g * 128, (g + 1) * 128)
                z = _window_sum(ddc_s[pl.ds(rs, 80), ls], g + 1, True)
                dpin = (z[0:CHUNK] - dd_s[pl.ds(rs, CHUNK), ls]) * lv
                du_ref[pl.ds(rs, CHUNK), 2 * C + g * 128:2 * C + (g + 1) * 128] = dpin.astype(BF16)
            return 0

        lax.fori_loop(0, nch, second, 0)

        @pl.when(i == ntile - 1)
        def _():
            for tap in range(CONV_WIDTH):
                dcw_ref[tap:tap + 1, :] = jnp.sum(dw_s[8 * tap:8 * tap + 8, :], axis=0, keepdims=True)
            dcw_ref[CONV_WIDTH:CONV_ROWS, :] = jnp.zeros((CONV_ROWS - CONV_WIDTH, C), F32)

    vec = pl.BlockSpec((None, 1, C), lambda i: (j, 0, 0))
    ovec = pl.BlockSpec((1, C), lambda i: (0, 0))
    return pl.pallas_call(
        body, grid=(ntile,),
        in_specs=[pl.BlockSpec((tm, 3 * C), lambda i: (i, 0)),
                  pl.BlockSpec((CHUNK, 3 * C), lambda i: (jnp.maximum(i * nch - 1, 0), 0)),
                  pl.BlockSpec((CHUNK, 3 * C), lambda i: (jnp.minimum((i + 1) * nch, nblk - 1), 0)),
                  pl.BlockSpec((tm, 2 * C), lambda i: (i, 0)),
                  pl.BlockSpec((CHUNK, 2 * C), lambda i: (jnp.minimum((i + 1) * nch, nblk - 1), 0)),
                  pl.BlockSpec((None, CONV_ROWS, C), lambda i: (j, 0, 0)), vec, vec, vec,
                  pl.BlockSpec((None, 4, 128, 128), lambda i: (j, 0, 0, 0)), vec, vec],
        out_specs=[pl.BlockSpec((tm, 3 * C), lambda i: (i, 0)), pl.BlockSpec((CONV_ROWS, C), lambda i: (0, 0)),
                   ovec, ovec, ovec, pl.BlockSpec((4, 128, 128), lambda i: (0, 0, 0)), ovec, ovec],
        out_shape=[SDS((T, 3 * C), BF16), SDS((CONV_ROWS, C), F32), SDS((1, C), F32), SDS((1, C), F32), SDS((1, C), F32),
                   SDS((4, 128, 128), F32), SDS((1, C), F32), SDS((1, C), F32)],
        scratch_shapes=[pltpu.VMEM((tm + 2 * CHUNK, C), F32), pltpu.VMEM((tm + 2 * CHUNK, C), F32),
                        pltpu.VMEM((tm + CHUNK, 2 * C), F32), pltpu.VMEM((CHUNK, C), F32),
                        pltpu.VMEM((tm + CHUNK + 32, C), F32), pltpu.VMEM((tm + CHUNK, C), F32),
                        pltpu.VMEM((tm + CHUNK + 16, C), F32), pltpu.VMEM((8 * CONV_ROWS, C), F32)],
        name=name, compiler_params=_params("arbitrary"))(u, u, u, dy, dy, cw3, cb3, lg3, lb3, pw4, pb3, ps3)


HI = lax.Precision.HIGHEST


def _dot_nt(a, b):
    return lax.dot_general(a, b, (((1,), (1,)), ((), ())), preferred_element_type=F32)


def _dot_tn(a, b):
    return lax.dot_general(a, b, (((0,), (0,)), ((), ())), preferred_element_type=F32)


def _tri(lower):
    r = lax.broadcasted_iota(jnp.int32, (CHUNK, CHUNK), 0)
    c = lax.broadcasted_iota(jnp.int32, (CHUNK, CHUNK), 1)
    return jnp.where((c <= r) if lower else (c >= r), 1.0, 0.0).astype(F32)


def _hgrn_gates(u_ref, lb_ref, h, D, lv):
    ls = slice(h * HEAD_DIM, (h + 1) * HEAD_DIM)
    qraw = u_ref[:, ls]
    fraw = u_ref[:, D + h * HEAD_DIM:D + (h + 1) * HEAD_DIM]
    v = u_ref[:, 2 * D + h * HEAD_DIM:2 * D + (h + 1) * HEAD_DIM] * lv
    lbv = lb_ref[:, ls]
    sig = _sigmoid(fraw)
    forget = lbv + (1.0 - lbv) * sig
    logf = jnp.log(forget) * lv
    k = (1.0 - forget) * lv
    qsig = _sigmoid(qraw)
    q = qraw * qsig * lv
    return q, k, v, logf, (qraw, qsig, sig, forget, lbv)


def _sub_parts(q, k, b, b_s, I):
    rows = slice(SUB * I, SUB * (I + 1))
    rho = jnp.zeros((1, HEAD_DIM), F32) if I == 0 else b_s[SUB * I - 1:SUB * I, :]
    eI = jnp.exp(b[rows] - rho)
    EI = jnp.exp(jnp.minimum(rho - b, EXP_CAP))
    causal = (lax.broadcasted_iota(jnp.int32, (SUB, CHUNK), 1)
              <= lax.broadcasted_iota(jnp.int32, (SUB, CHUNK), 0) + SUB * I)
    return rows, q[rows] * eI, k * EI, eI, EI, causal


def _hgrn_fwd(u, lb3, layer, gn3, j, name):
    T = u.shape[0]
    D = u.shape[1] // 4
    H = D // HEAD_DIM
    NC = T // CHUNK

    def body(u_ref, lb_ref, gn_ref, y_ref, o_ref, sall_ref, st_s, b_s):
        n = pl.program_id(0)

        @pl.when(n == 0)
        def _():
            st_s[...] = jnp.zeros_like(st_s)

        lv = _live(CHUNK, n * CHUNK, T)
        tril = _tri(True)
        for h in range(H):
            ls = slice(h * HEAD_DIM, (h + 1) * HEAD_DIM)
            q, k, v, logf, _ = _hgrn_gates(u_ref, lb_ref, h, D, lv)
            b = jnp.dot(tril, logf, precision=HI, preferred_element_type=F32)
            b_s[...] = b
            blast = b_s[CHUNK - 1:CHUNK, :]
            st = st_s[h]
            sall_ref[h] = st
            vb = v.astype(BF16)
            o = _dot_nt((q * jnp.exp(b)).astype(BF16), st.astype(BF16))
            kt = k * jnp.exp(blast - b)
            st_s[h] = st * jnp.exp(blast) + _dot_tn(vb, kt.astype(BF16))
            pieces = []
            for I in range(CHUNK // SUB):
                _, qI, KI, _, _, causal = _sub_parts(q, k, b, b_s, I)
                p = jnp.where(causal, _dot_nt(qI.astype(BF16), KI.astype(BF16)), 0.0)
                pieces.append(jnp.dot(p.astype(BF16), vb, preferred_element_type=F32))
            o = o + jnp.concatenate(pieces, axis=0)
            o_ref[:, ls] = o
            graw = u_ref[:, 3 * D + h * HEAD_DIM:3 * D + (h + 1) * HEAD_DIM]
            r = lax.rsqrt(jnp.mean(o * o, axis=-1, keepdims=True) + EPS)
            y_ref[:, ls] = (((o * r) * gn_ref[...]) * (graw * _sigmoid(graw))).astype(BF16)

    return pl.pallas_call(
        body, grid=(NC,),
        in_specs=[pl.BlockSpec((CHUNK, 4 * D), lambda n: (n, 0)),
                  pl.BlockSpec((None, 1, D), lambda n: (layer, 0, 0)),
                  pl.BlockSpec((None, 1, HEAD_DIM), lambda n: (j, 0, 0))],
        out_specs=[pl.BlockSpec((CHUNK, D), lambda n: (n, 0)), pl.BlockSpec((CHUNK, D), lambda n: (n, 0)),
                   pl.BlockSpec((None, H, HEAD_DIM, HEAD_DIM), lambda n: (n, 0, 0, 0))],
        out_shape=[SDS((T, D), BF16), SDS((T, D), F32), SDS((NC, H, HEAD_DIM, HEAD_DIM), F32)],
        scratch_shapes=[pltpu.VMEM((H, HEAD_DIM, HEAD_DIM), F32), pltpu.VMEM((CHUNK, HEAD_DIM), F32)],
        name=name, compiler_params=_params("arbitrary"))(u, lb3, gn3)


def _hgrn_bwd(u, o_raw, dy, sall, lb3, layer, gn3, j, name):
    T = u.shape[0]
    D = u.shape[1] // 4
    H = D // HEAD_DIM
    NC = T // CHUNK

    def body(u_ref, o_ref, dy_ref, sall_ref, lb_ref, gn_ref, du_ref, dlb_ref, dgn_ref, dst_s, b_s):
        step = pl.program_id(0)
        n = NC - 1 - step

        @pl.when(step == 0)
        def _():
            dst_s[...] = jnp.zeros_like(dst_s)
            dlb_ref[...] = jnp.zeros_like(dlb_ref)
            dgn_ref[...] = jnp.zeros_like(dgn_ref)

        lv = _live(CHUNK, n * CHUNK, T)
        tril = _tri(True)
        triu = _tri(False)
        last_row = (_row_ids((CHUNK, 1), 0) == CHUNK - 1).astype(F32)
        gn = gn_ref[...]
        for h in range(H):
            ls = slice(h * HEAD_DIM, (h + 1) * HEAD_DIM)
            q, k, v, logf, (qraw, qsig, sig, forget, lbv) = _hgrn_gates(u_ref, lb_ref, h, D, lv)
            b = jnp.dot(tril, logf, precision=HI, preferred_element_type=F32)
            b_s[...] = b
            blast = b_s[CHUNK - 1:CHUNK, :]
            eb = jnp.exp(b)
            ekb = jnp.exp(blast - b)
            eblast = jnp.exp(blast)
            qh = q * eb
            kt = k * ekb
            st = sall_ref[h]
            dst = dst_s[h]
            graw = u_ref[:, 3 * D + h * HEAD_DIM:3 * D + (h + 1) * HEAD_DIM]
            gsig = _sigmoid(graw)
            o = o_ref[:, ls]
            r = lax.rsqrt(jnp.mean(o * o, axis=-1, keepdims=True) + EPS)
            xh = o * r
            dyv = dy_ref[:, ls]
            dsg = dyv * (graw * gsig)
            dgn_ref[...] += jnp.sum(dsg * xh, axis=0, keepdims=True)
            dxh = dsg * gn
            do = r * (dxh - xh * jnp.mean(dxh * xh, axis=-1, keepdims=True))
            dgraw = dyv * xh * gn * (gsig * (1.0 + graw * (1.0 - gsig)))
            dob = do.astype(BF16)
            vb = v.astype(BF16)
            stb = st.astype(BF16)
            dstb = dst.astype(BF16)
            dv = _dot_nt(kt.astype(BF16), dstb)
            dqh = jnp.dot(dob, stb, preferred_element_type=F32)
            dkt = jnp.dot(vb, dstb, preferred_element_type=F32)
            dblast = jnp.sum(dst * st, axis=0, keepdims=True) * eblast
            dst_s[h] = dst * eblast + _dot_tn(dob, qh.astype(BF16))
            dq = dqh * eb
            db = dqh * qh
            tmp = dkt * kt
            dk = dkt * ekb
            db = db - tmp
            dblast = dblast + jnp.sum(tmp, axis=0, keepdims=True)
            dp_full = _dot_nt(dob, vb)
            dq_parts, db_parts = [], []
            for I in range(CHUNK // SUB):
                rows, qI, KI, eI, EI, causal = _sub_parts(q, k, b, b_s, I)
                qIb = qI.astype(BF16)
                KIb = KI.astype(BF16)
                p = jnp.where(causal, _dot_nt(qIb, KIb), 0.0).astype(BF16)
                dv = dv + _dot_tn(p, dob[rows])
                dp = jnp.where(causal, dp_full[rows], 0.0).astype(BF16)
                dqI = jnp.dot(dp, KIb, preferred_element_type=F32)
                dKI = _dot_tn(dp, qIb)
                dq_parts.append(dqI * eI)
                db_parts.append(dqI * qIb.astype(F32))
                dk = dk + dKI * EI
                db = db - dKI * KIb.astype(F32)
            dq = dq + jnp.concatenate(dq_parts, axis=0)
            db = db + jnp.concatenate(db_parts, axis=0) + last_row * dblast
            dlogf = jnp.dot(triu, db, precision=HI, preferred_element_type=F32)
            dforget = (dlogf / forget - dk) * lv
            dlb_ref[:, ls] += jnp.sum(dforget * (1.0 - sig), axis=0, keepdims=True)
            du_ref[:, ls] = (dq * (qsig * (1.0 + qraw * (1.0 - qsig))) * lv).astype(BF16)
            du_ref[:, D + h * HEAD_DIM:D + (h + 1) * HEAD_DIM] = (dforget * (1.0 - lbv) * sig * (1.0 - sig)).astype(BF16)
            du_ref[:, 2 * D + h * HEAD_DIM:2 * D + (h + 1) * HEAD_DIM] = (dv * lv).astype(BF16)
            du_ref[:, 3 * D + h * HEAD_DIM:3 * D + (h + 1) * HEAD_DIM] = (dgraw * lv).astype(BF16)

    rev = lambda s: (NC - 1 - s, 0)
    return pl.pallas_call(
        body, grid=(NC,),
        in_specs=[pl.BlockSpec((CHUNK, 4 * D), rev), pl.BlockSpec((CHUNK, D), rev), pl.BlockSpec((CHUNK, D), rev),
                  pl.BlockSpec((None, H, HEAD_DIM, HEAD_DIM), lambda s: (NC - 1 - s, 0, 0, 0)),
                  pl.BlockSpec((None, 1, D), lambda s: (layer, 0, 0)),
                  pl.BlockSpec((None, 1, HEAD_DIM), lambda s: (j, 0, 0))],
        out_specs=[pl.BlockSpec((CHUNK, 4 * D), rev), pl.BlockSpec((1, D), lambda s: (0, 0)),
                   pl.BlockSpec((1, HEAD_DIM), lambda s: (0, 0))],
        out_shape=[SDS((T, 4 * D), BF16), SDS((1, D), F32), SDS((1, HEAD_DIM), F32)],
        scratch_shapes=[pltpu.VMEM((H, HEAD_DIM, HEAD_DIM), F32), pltpu.VMEM((CHUNK, HEAD_DIM), F32)],
        name=name, compiler_params=_params("arbitrary"))(u, o_raw, dy, sall, lb3, gn3)


def _softmax_layers(p_ref, n_layers):
    rows = [p_ref[l:l + 1, :] for l in range(n_layers)]
    m = functools.reduce(jnp.maximum, rows)
    e = [jnp.exp(x - m) for x in rows]
    tot = functools.reduce(lambda a, b: a + b, e)
    return [x / tot for x in e]


def _lb_fwd(p):
    n_layers, D = p.shape

    def body(p_ref, o_ref):
        s = _softmax_layers(p_ref, n_layers)
        acc = jnp.zeros((1, D), F32)
        o_ref[0:1, :] = acc
        for l in range(1, n_layers):
            acc = acc + s[l]
            o_ref[l:l + 1, :] = acc

    return pl.pallas_call(body, out_shape=SDS(p.shape, F32), name="lb_fwd")(p)


def _lb_bwd(p, dlb):
    n_layers, D = p.shape

    def body(p_ref, d_ref, o_ref):
        s = _softmax_layers(p_ref, n_layers)
        ds = [jnp.zeros((1, D), F32)] * n_layers
        acc = jnp.zeros((1, D), F32)
        for l in range(n_layers - 1, 0, -1):
            acc = acc + d_ref[l:l + 1, :]
            ds[l] = acc
        dot = functools.reduce(lambda a, b: a + b, [s[l] * ds[l] for l in range(n_layers)])
        for l in range(n_layers):
            o_ref[l:l + 1, :] = s[l] * (ds[l] - dot)

    return pl.pallas_call(body, out_shape=SDS(p.shape, F32), name="lb_bwd")(p, dlb)


def _adamw(w, g, m, v, name):
    R, C = w.shape
    tr = _tile(R, 256, 8) if R % 8 == 0 else R

    def body(w_ref, g_ref, m_ref, v_ref, d_ref, mo_ref, vo_ref):
        g_ = g_ref[...]
        m_ = ADAM_B1 * m_ref[...] + (1.0 - ADAM_B1) * g_
        v_ = ADAM_B2 * v_ref[...] + (1.0 - ADAM_B2) * (g_ * g_)
        mh = m_ / (1.0 - ADAM_B1 ** ADAM_STEP)
        vh = v_ / (1.0 - ADAM_B2 ** ADAM_STEP)
        d_ref[...] = -ADAM_LR * (mh / (jnp.sqrt(vh) + ADAM_EPS) + ADAM_WD * w_ref[...])
        mo_ref[...] = m_
        vo_ref[...] = v_

    blk = pl.BlockSpec((tr, C), lambda i: (i, 0))
    return pl.pallas_call(
        body, grid=(R // tr,), in_specs=[blk] * 4, out_specs=[blk] * 3, out_shape=[SDS((R, C), F32)] * 3,
        name=name, compiler_params=_params("parallel"))(w, g, m, v)


HBM_SPEC = pl.BlockSpec(memory_space=pl.ANY)


def _position():
    x, y, c = lax.axis_index("x"), lax.axis_index("y"), lax.axis_index("c")
    chips = [(1 - x, y), (x, 1 - y), (1 - x, 1 - y)]
    return x, y, c, chips


def _place(kind, full_ref, chip, half):
    _, K, N = full_ref.shape
    if kind == "col":
        ns = N // N_CHIPS
        if half is None:
            return full_ref.at[:, :, pl.ds(chip * ns, ns)]
        return full_ref.at[:, pl.ds(half * (K // 2), K // 2), pl.ds(chip * ns, ns)]
    ks = K // N_CHIPS
    if half is None:
        return full_ref.at[:, pl.ds(chip * ks, ks), :]
    return full_ref.at[:, pl.ds(chip * ks + half * (ks // 2), ks // 2), :]


def _gather_weights(shards, kinds):
    n = len(shards)
    fulls = []
    for s, kind in zip(shards, kinds):
        L, ks, ns = s.shape
        fulls.append(SDS((L, ks, ns * N_CHIPS) if kind == "col" else (L, ks * N_CHIPS, ns), s.dtype))

    def body(*refs):
        srcs, outs = refs[:n], refs[n:2 * n]
        send_sems, recv_sems, local_sems = refs[2 * n:]
        x, y, c, chips = _position()
        me, sibling, mine = (x, y, c), (x, y, 1 - c), 2 * x + y

        def copy(f, k, src, dst, to):
            return pltpu.make_async_remote_copy(src_ref=src, dst_ref=dst, send_sem=send_sems.at[6 * f + k],
                                                recv_sem=recv_sems.at[6 * f + k], device_id=to, device_id_type=MESH)

        def half_shard(f, half):
            nr = srcs[f].shape[1] // 2
            return srcs[f].at[:, pl.ds(half * nr, nr), :]

        def place(f, chip, half):
            return _place(kinds[f], outs[f], chip, half)

        local, sent = [], []
        for f in range(n):
            lc = pltpu.make_async_copy(srcs[f], place(f, mine, None), local_sems.at[f])
            lc.start()
            local.append(lc)
            for k, chip in enumerate(chips):
                cp = copy(f, k, half_shard(f, c), place(f, mine, c), (*chip, c))
                cp.start()
                sent.append(cp)
        for f in range(n):
            for k, chip in enumerate(chips):
                landed = place(f, 2 * chip[0] + chip[1], c)
                copy(f, k, landed, landed, me).wait_recv()
                fw = copy(f, 3 + k, landed, landed, sibling)
                fw.start()
                sent.append(fw)
        for f in range(n):
            for k, chip in enumerate(chips):
                landed = place(f, 2 * chip[0] + chip[1], 1 - c)
                copy(f, 3 + k, landed, landed, me).wait_recv()
        for cp in sent:
            cp.wait_send()
        for lc in local:
            lc.wait()

    return pl.pallas_call(
        body, in_specs=[HBM_SPEC] * n, out_specs=[HBM_SPEC] * n, out_shape=fulls,
        scratch_shapes=[pltpu.SemaphoreType.DMA((6 * n,)), pltpu.SemaphoreType.DMA((6 * n,)), pltpu.SemaphoreType.DMA((n,))],
        name="gather_weights")(*shards)


def _swap_halves(accs):
    n = len(accs)

    def body(*refs):
        srcs, outs = refs[:n], refs[n:2 * n]
        send_sems, recv_sems = refs[2 * n:]
        x, y, c, _ = _position()
        cps = []
        for f in range(n):
            cp = pltpu.make_async_remote_copy(src_ref=srcs[f].at[1 - c], dst_ref=outs[f], send_sem=send_sems.at[f],
                                              recv_sem=recv_sems.at[f], device_id=(x, y, 1 - c), device_id_type=MESH)
            cp.start()
            cps.append(cp)
        for cp in cps:
            cp.wait()

    return pl.pallas_call(
        body, in_specs=[HBM_SPEC] * n, out_specs=[HBM_SPEC] * n, out_shape=[SDS(a.shape[1:], a.dtype) for a in accs],
        scratch_shapes=[pltpu.SemaphoreType.DMA((n,)), pltpu.SemaphoreType.DMA((n,))],
        name="swap_halves")(*accs)


def _add_half(c1, acc, recv, name):
    _, nchip, L, nr, nc = acc.shape
    R = nchip * L * nr
    tr = _tile(R, 1024, 16)

    def body(c_ref, a_ref, r_ref, o_ref):
        del c_ref
        o_ref[...] = (a_ref[...].astype(F32) + r_ref[...].astype(F32)).astype(BF16)

    out = pl.pallas_call(
        body,
        grid_spec=pltpu.PrefetchScalarGridSpec(
            num_scalar_prefetch=1, grid=(R // tr,),
            in_specs=[pl.BlockSpec((None, tr, nc), lambda i, c: (c[0], i, 0)), pl.BlockSpec((tr, nc), lambda i, c: (i, 0))],
            out_specs=pl.BlockSpec((tr, nc), lambda i, c: (i, 0))),
        out_shape=SDS((R, nc), BF16), name=name, compiler_params=_params("parallel"),
    )(c1, acc.reshape(2, R, nc), recv.reshape(R, nc))
    return out.reshape(nchip, L, nr, nc)


def _to_owner(sums):
    n = len(sums)

    def body(*refs):
        srcs, outs = refs[:n], refs[n:2 * n]
        send_sems, recv_sems, local_sems = refs[2 * n:]
        x, y, c, chips = _position()
        cps = []
        for f in range(n):
            lc = pltpu.make_async_copy(srcs[f].at[2 * x + y], outs[f].at[3], local_sems.at[f])
            lc.start()
            cps.append(lc)
            for k, chip in enumerate(chips):
                cp = pltpu.make_async_remote_copy(
                    src_ref=srcs[f].at[2 * chip[0] + chip[1]], dst_ref=outs[f].at[k], send_sem=send_sems.at[3 * f + k],
                    recv_sem=recv_sems.at[3 * f + k], device_id=(*chip, c), device_id_type=MESH)
                cp.start()
                cps.append(cp)
        for cp in cps:
            cp.wait()

    return pl.pallas_call(
        body, in_specs=[HBM_SPEC] * n, out_specs=[HBM_SPEC] * n, out_shape=[SDS(s.shape, s.dtype) for s in sums],
        scratch_shapes=[pltpu.SemaphoreType.DMA((3 * n,)), pltpu.SemaphoreType.DMA((3 * n,)), pltpu.SemaphoreType.DMA((n,))],
        name="to_owner")(*sums)


def _sum_owner(c1, parts, name):
    _, L, nr, nc = parts.shape
    tr = _tile(nr, 512, 16)

    def body(c_ref, p_ref, o_ref):
        del c_ref
        o_ref[...] = ((p_ref[3].astype(F32) + p_ref[0].astype(F32)) + p_ref[1].astype(F32)) + p_ref[2].astype(F32)

    return pl.pallas_call(
        body,
        grid_spec=pltpu.PrefetchScalarGridSpec(
            num_scalar_prefetch=1, grid=(L, nr // tr),
            in_specs=[pl.BlockSpec((4, None, tr, nc), lambda l, i, c: (0, l, i, 0))],
            out_specs=pl.BlockSpec((None, None, tr, nc), lambda l, i, c: (l, c[0], i, 0))),
        out_shape=SDS((L, 2, nr, nc), F32), name=name, compiler_params=_params("parallel", "parallel"),
    )(c1, parts)


def _swap_back(grads):
    n = len(grads)

    def body(*refs):
        srcs, outs = refs[:n], refs[n:2 * n]
        send_sems, recv_sems = refs[2 * n:]
        del srcs
        x, y, c, _ = _position()
        cps = []
        for f in range(n):
            half = outs[f].at[:, pl.ds(c, 1)]
            cp = pltpu.make_async_remote_copy(src_ref=half, dst_ref=half, send_sem=send_sems.at[f],
                                              recv_sem=recv_sems.at[f], device_id=(x, y, 1 - c), device_id_type=MESH)
            cp.start()
            cps.append(cp)
        for cp in cps:
            cp.wait()

    return pl.pallas_call(
        body, in_specs=[HBM_SPEC] * n, out_specs=[HBM_SPEC] * n, out_shape=[SDS(g.shape, g.dtype) for g in grads],
        input_output_aliases={f: f for f in range(n)},
        scratch_shapes=[pltpu.SemaphoreType.DMA((n,)), pltpu.SemaphoreType.DMA((n,))],
        name="swap_back")(*grads)


def _all_sum(block):
    m_per, n = block.shape
    n_dev = 2 * N_CHIPS

    def body(x_ref, all_ref, sum_ref, send_sems, recv_sems, local_sem):
        x, y, c, chips = _position()
        me, sibling = (x, y, c), (x, y, 1 - c)

        def rows(px, py, pc):
            return all_ref.at[pl.ds((4 * px + 2 * py + pc) * m_per, m_per), :]

        def copy(k, blk, to, src=None):
            return pltpu.make_async_remote_copy(
                src_ref=rows(*blk) if src is None else src, dst_ref=rows(*blk), send_sem=send_sems.at[k],
                recv_sem=recv_sems.at[k], device_id=to, device_id_type=MESH)

        mine = pltpu.make_async_copy(x_ref, rows(*me), local_sem)
        mine.start()
        first = [copy(0, me, sibling, src=x_ref)]
        first += [copy(1 + k, me, (*chip, c), src=x_ref) for k, chip in enumerate(chips)]
        for cp in first:
            cp.start()
        passed = [copy(4 + k, (*chip, c), sibling) for k, chip in enumerate(chips)]
        for k, chip in enumerate(chips):
            copy(1 + k, (*chip, c), me).wait_recv()
            passed[k].start()
        copy(0, sibling, me).wait_recv()
        for k, chip in enumerate(chips):
            copy(4 + k, (*chip, 1 - c), me).wait_recv()
        for cp in first + passed:
            cp.wait_send()
        mine.wait()
        acc = all_ref[0:m_per, :]
        for d in range(1, n_dev):
            acc = acc + all_ref[d * m_per:(d + 1) * m_per, :]
        sum_ref[...] = acc

    vm = pl.BlockSpec(memory_space=pltpu.VMEM)
    return pl.pallas_call(
        body, in_specs=[vm], out_specs=[vm, vm], out_shape=[SDS((n_dev * m_per, n), F32), SDS((m_per, n), F32)],
        scratch_shapes=[pltpu.SemaphoreType.DMA((7,)), pltpu.SemaphoreType.DMA((7,)), pltpu.SemaphoreType.DMA],
        name="all_sum", compiler_params=_params())(block)[1]


BIG = (("ev_w_in", "col"), ("ev_w_out", "row"), ("od_w_in", "col"), ("od_w_out", "row"), ("mlp_w1", "col"), ("mlp_w2", "row"))
WEIGHTS = ("meta_tokens", "mix_norm_g", "mlp_norm_g", "final_norm_g", "ev_w_in", "ev_conv_w", "ev_conv_b", "ev_ln_g",
           "ev_ln_b", "ev_pool_w", "ev_pool_b", "ev_pool_scale", "ev_w_out", "od_w_in", "od_gnorm_g", "od_w_out",
           "lb_param", "mlp_w1", "mlp_w2")
PACK_UNIT = 1024


def _pack(arrays):
    flat = []
    for a in arrays:
        a = a.reshape(-1)
        flat.append(jnp.pad(a, (0, (-a.shape[0]) % PACK_UNIT)))
    return jnp.concatenate(flat).reshape(-1, 128)


def _unpack(packed, shapes):
    flat = packed.reshape(-1)
    out, off = [], 0
    for s in shapes:
        size = 1
        for d in s:
            size *= d
        out.append(flat[off:off + size].reshape(s))
        off += size + (-size) % PACK_UNIT
    return out


def _local_step(x2, target, W, P):
    D = x2.shape[1]
    n_layers = P["mix_norm_g"].shape[0]
    h = jnp.concatenate([jnp.zeros((PAD, D), F32), P["meta_full"], x2], axis=0)
    T = h.shape[0]
    mix_g = P["mix_norm_g"].reshape(n_layers, 1, D)
    mlp_g = P["mlp_norm_g"].reshape(n_layers, 1, D)
    vec = lambda a: a.reshape(a.shape[0], 1, -1)
    cb3, lg3, lnb3, ps3 = vec(P["ev_conv_b"]), vec(P["ev_ln_g"]), vec(P["ev_ln_b"]), vec(P["ev_pool_scale"])
    pb3 = vec(P["ev_pool_b"])
    gn3 = vec(P["od_gnorm_g"])
    lb_all = _lb_fwd(P["lb_param"])
    lb3 = lb_all.reshape(n_layers, 1, D)
    even = (cb3, lg3, lnb3, P["ev_pool_w"], pb3, ps3)

    saved = []
    for layer in range(n_layers):
        j = layer // 2
        s = {"h": h}
        s["n"] = _rms_fwd(h, mix_g, layer, f"mix_norm_{layer}")
        if layer % 2 == 0:
            s["u"] = _mm_nn(s["n"], W["ev_w_in"], j, f"ev_in_{layer}")
            s["y"] = _even_fwd(s["u"], P["conv_w_full"], *even, j, f"even_fwd_{layer}")
            h = _mm_nn(s["y"], W["ev_w_out"], j, f"ev_out_{layer}", res=h)
        else:
            s["u"] = _mm_nn(s["n"], W["od_w_in"], j, f"od_in_{layer}")
            s["y"], s["o"], s["sall"] = _hgrn_fwd(s["u"], lb3, layer, gn3, j, f"hgrn_fwd_{layer}")
            h = _mm_nn(s["y"], W["od_w_out"], j, f"od_out_{layer}", res=h)
        s["h1"] = h
        s["n2"] = _rms_fwd(h, mlp_g, layer, f"mlp_norm_{layer}")
        s["z"], s["act"] = _mm_nn(s["n2"], W["mlp_w1"], layer, f"mlp_up_{layer}", relu2=True)
        h = _mm_nn(s["act"], W["mlp_w2"], layer, f"mlp_down_{layer}", res=h)
        saved.append(s)

    dh, dhb, dg_final, loss = _final(h, P["final_norm_g"].reshape(1, D), target)

    acc = {}
    for name, kind in BIG:
        L, K, N = W[name].shape
        acc[name] = lax.empty((2, N_CHIPS, L) + _fam_dims(kind, K, N), BF16)
    small = {"final_norm_g": dg_final}
    per_layer = {k: [None] * n_layers for k in ("mix_norm_g", "mlp_norm_g", "lb")}
    per_pair = {k: [None] * (n_layers // 2) for k in
                ("ev_conv_w", "ev_conv_b", "ev_ln_g", "ev_ln_b", "ev_pool_w", "ev_pool_b", "ev_pool_scale", "od_gnorm_g")}
    for layer in reversed(range(n_layers)):
        j = layer // 2
        s = saved[layer]
        dz = _mm_nt(dhb, W["mlp_w2"], layer, f"d_act_{layer}", z=s["z"])
        acc["mlp_w2"] = _mm_tn(s["act"], dhb, acc["mlp_w2"], "row", layer, f"dw2_{layer}")
        dn2 = _mm_nt(dz, W["mlp_w1"], layer, f"d_n2_{layer}")
        acc["mlp_w1"] = _mm_tn(s["n2"], dz, acc["mlp_w1"], "col", layer, f"dw1_{layer}")
        dh, dhb, per_layer["mlp_norm_g"][layer] = _rms_bwd(s["h1"], mlp_g, layer, dn2, dh, f"mlp_norm_bwd_{layer}")
        if layer % 2 == 0:
            dy = _mm_nt(dhb, W["ev_w_out"], j, f"d_y_{layer}")
            acc["ev_w_out"] = _mm_tn(s["y"], dhb, acc["ev_w_out"], "row", j, f"dwout_{layer}")
            du, dcw, dcb, dlg, dlnb, dpw, dpb, dps = _even_bwd(s["u"], dy, P["conv_w_full"], *even, j, f"even_bwd_{layer}")
            for k, val in (("ev_conv_w", dcw), ("ev_conv_b", dcb), ("ev_ln_g", dlg), ("ev_ln_b", dlnb),
                           ("ev_pool_w", dpw), ("ev_pool_b", dpb), ("ev_pool_scale", dps)):
                per_pair[k][j] = val
            dn = _mm_nt(du, W["ev_w_in"], j, f"d_n_{layer}")
            acc["ev_w_in"] = _mm_tn(s["n"], du, acc["ev_w_in"], "col", j, f"dwin_{layer}")
        else:
            dy = _mm_nt(dhb, W["od_w_out"], j, f"d_y_{layer}")
            acc["od_w_out"] = _mm_tn(s["y"], dhb, acc["od_w_out"], "row", j, f"dwout_{layer}")
            du, per_layer["lb"][layer], per_pair["od_gnorm_g"][j] = _hgrn_bwd(
                s["u"], s["o"], dy, s["sall"], lb3, layer, gn3, j, f"hgrn_bwd_{layer}")
            dn = _mm_nt(du, W["od_w_in"], j, f"d_n_{layer}")
            acc["od_w_in"] = _mm_tn(s["n"], du, acc["od_w_in"], "col", j, f"dwin_{layer}")
        dh, dhb, per_layer["mix_norm_g"][layer] = _rms_bwd(s["h"], mix_g, layer, dn, dh, f"mix_norm_bwd_{layer}")

    small["mix_norm_g"] = jnp.concatenate(per_layer["mix_norm_g"], axis=0)
    small["mlp_norm_g"] = jnp.concatenate(per_layer["mlp_norm_g"], axis=0)
    dlb_all = jnp.concatenate([jnp.zeros((1, D), F32) if g is None else g for g in per_layer["lb"]], axis=0)
    small["lb_param"] = _lb_bwd(P["lb_param"], dlb_all)
    for k, vals in per_pair.items():
        small[k] = jnp.stack(vals, axis=0)
    small["meta_tokens"] = dh[PAD:LEAD]
    return loss, dh, acc, small


def kernel(x, meta_tokens, mix_norm_g, mlp_norm_g, final_norm_g, ev_w_in, ev_conv_w, ev_conv_b, ev_ln_g, ev_ln_b, ev_pool_w, ev_pool_b, ev_pool_scale, ev_w_out, od_w_in, od_gnorm_g, od_w_out, lb_param, mlp_w1, mlp_w2, loss_target, m_meta_tokens, m_mix_norm_g, m_mlp_norm_g, m_final_norm_g, m_ev_w_in, m_ev_conv_w, m_ev_conv_b, m_ev_ln_g, m_ev_ln_b, m_ev_pool_w, m_ev_pool_b, m_ev_pool_scale, m_ev_w_out, m_od_w_in, m_od_gnorm_g, m_od_w_out, m_lb_param, m_mlp_w1, m_mlp_w2, v_meta_tokens, v_mix_norm_g, v_mlp_norm_g, v_final_norm_g, v_ev_w_in, v_ev_conv_w, v_ev_conv_b, v_ev_ln_g, v_ev_ln_b, v_ev_pool_w, v_ev_pool_b, v_ev_pool_scale, v_ev_w_out, v_od_w_in, v_od_gnorm_g, v_od_w_out, v_lb_param, v_mlp_w1, v_mlp_w2):
    given = dict(locals())
    w = {n: given[n] for n in WEIGHTS}
    m = {n: given["m_" + n] for n in WEIGHTS}
    v = {n: given["v_" + n] for n in WEIGHTS}
    chip = 2 * lax.axis_index("x") + lax.axis_index("y")
    c1 = lax.axis_index("c").astype(jnp.int32).reshape(1)

    shards = [_cast_bf16(w[n].reshape(-1, w[n].shape[2]), f"cast_{n}").reshape(w[n].shape) for n, _ in BIG]
    conv_pad = jnp.pad(ev_conv_w, ((0, 0), (0, CONV_ROWS - CONV_WIDTH), (0, 0)))
    fulls = _gather_weights(shards + [meta_tokens[None], conv_pad], [k for _, k in BIG] + ["col", "col"])
    W = {n: f for (n, _), f in zip(BIG, fulls)}
    P = {n: w[n] for n in ("mix_norm_g", "mlp_norm_g", "final_norm_g", "ev_conv_b", "ev_ln_g", "ev_ln_b", "ev_pool_w",
                           "ev_pool_b", "ev_pool_scale", "od_gnorm_g", "lb_param")}
    P["meta_full"] = fulls[-2][0]
    P["conv_w_full"] = fulls[-1]

    loss, dh, acc, small = _local_step(x[0], loss_target[0], W, P)

    names = [n for n, _ in BIG]
    from_sibling = _swap_halves([acc[n] for n in names])
    chip_sums = [_add_half(c1, acc[n], r, f"add_half_{n}") for n, r in zip(names, from_sibling)]
    pieces = _to_owner(chip_sums)
    halves = [_sum_owner(c1, p, f"sum_owner_{n}") for n, p in zip(names, pieces)]
    grads = dict(zip(names, _swap_back(halves)))

    order = [n for n in WEIGHTS if n not in names]
    packed = _all_sum(_pack([small[n] for n in order] + [loss]))
    total = _unpack(packed, [small[n].shape for n in order] + [loss.shape])
    loss_sum = total[-1][0, 0]
    gsmall = dict(zip(order, total[:-1]))
    gsmall["meta_tokens"] = lax.dynamic_slice_in_dim(gsmall["meta_tokens"], chip * meta_tokens.shape[1], meta_tokens.shape[1], 1)
    gsmall["ev_conv_w"] = lax.dynamic_slice_in_dim(gsmall["ev_conv_w"][:, :CONV_WIDTH], chip * ev_conv_w.shape[2], ev_conv_w.shape[2], 2)

    g_out, d_out, m_out, v_out = {}, {}, {}, {}
    for n in WEIGHTS:
        shape = w[n].shape
        g = (grads[n] if n in grads else gsmall[n]).reshape(shape)
        cols = shape[-1] if len(shape) > 1 else 128
        two = lambda a: a.reshape(-1, cols)
        d_, m_, v_ = _adamw(two(w[n]), two(g), two(m[n]), two(v[n]), f"adamw_{n}")
        g_out[n], d_out[n], m_out[n], v_out[n] = g, d_.reshape(shape), m_.reshape(shape), v_.reshape(shape)

    grad_x = dh[LEAD:][None]
    return (loss_sum, grad_x, *[g_out[n] for n in WEIGHTS], *[d_out[n] for n in WEIGHTS],
            *[m_out[n] for n in WEIGHTS], *[v_out[n] for n in WEIGHTS])
```

```python
import functools

import jax
import jax.numpy as jnp
from jax import lax
from jax.experimental import pallas as pl
from jax.experimental.pallas import tpu as pltpu

F32 = jnp.float32
BF16 = jnp.bfloat16
SDS = jax.ShapeDtypeStruct
MESH = pl.DeviceIdType.MESH

N_META = 16
CHUNK = 64
LEAD = CHUNK
PAD = LEAD - N_META
CONV_WIDTH = 31
CONV_ROWS = 32
POOL_WINDOWS = (2, 4, 8, 16)
HEAD_DIM = 128
SUB = 16
EXP_CAP = 80.0
EPS = 1e-6
ADAM_LR = 0.001
ADAM_B1 = 0.9
ADAM_B2 = 0.999
ADAM_EPS = 1e-08
ADAM_WD = 0.01
ADAM_STEP = 10
N_CHIPS = 4
VMEM_LIMIT = 52 << 20


def _params(*sem):
    return pltpu.CompilerParams(dimension_semantics=sem if sem else None, vmem_limit_bytes=VMEM_LIMIT)


def _tile(n, target, unit=CHUNK):
    best = None
    for t in range(unit, min(n, target) + 1, unit):
        if n % t == 0:
            best = t
    assert best is not None, (n, target, unit)
    return best


def _ctile(n, target=512):
    for t in (512, 384, 256, 128):
        if t <= target and n % t == 0:
            return t
    raise ValueError(n)


def _sigmoid(x):
    return 1.0 / (1.0 + jnp.exp(-x))


def _row_ids(shape, base):
    return lax.broadcasted_iota(jnp.int32, shape, 0) + base


def _cast_bf16(w2d, name):
    R, C = w2d.shape
    tr = _tile(R, 512, 16)

    def body(w_ref, o_ref):
        o_ref[...] = w_ref[...].astype(BF16)

    return pl.pallas_call(
        body, grid=(R // tr,), in_specs=[pl.BlockSpec((tr, C), lambda i: (i, 0))],
        out_specs=pl.BlockSpec((tr, C), lambda i: (i, 0)), out_shape=SDS((R, C), BF16),
        name=name, compiler_params=_params("parallel"))(w2d)


def _rms_fwd(h, g3, layer, name):
    T, D = h.shape
    tm = _tile(T, 832)

    def body(h_ref, g_ref, n_ref):
        x = h_ref[...]
        r = lax.rsqrt(jnp.mean(x * x, axis=-1, keepdims=True) + EPS)
        n_ref[...] = ((x * r) * g_ref[...]).astype(BF16)

    return pl.pallas_call(
        body, grid=(T // tm,),
        in_specs=[pl.BlockSpec((tm, D), lambda i: (i, 0)), pl.BlockSpec((None, 1, D), lambda i: (layer, 0, 0))],
        out_specs=pl.BlockSpec((tm, D), lambda i: (i, 0)), out_shape=SDS((T, D), BF16),
        name=name, compiler_params=_params("parallel"))(h, g3)


def _rms_bwd(h, g3, layer, dn, dh_in, name):
    T, D = h.shape
    tm = _tile(T, 320)

    def body(h_ref, g_ref, dn_ref, dhi_ref, dh_ref, dhb_ref, dg_ref):
        x = h_ref[...]
        r = lax.rsqrt(jnp.mean(x * x, axis=-1, keepdims=True) + EPS)
        xh = x * r
        dn_ = dn_ref[...]
        dxh = dn_ * g_ref[...]
        dh = dhi_ref[...] + r * (dxh - xh * jnp.mean(dxh * xh, axis=-1, keepdims=True))
        dh_ref[...] = dh
        dhb_ref[...] = dh.astype(BF16)

        @pl.when(pl.program_id(0) == 0)
        def _():
            dg_ref[...] = jnp.zeros_like(dg_ref)

        dg_ref[...] += jnp.sum(dn_ * xh, axis=0, keepdims=True)

    row = pl.BlockSpec((tm, D), lambda i: (i, 0))
    return pl.pallas_call(
        body, grid=(T // tm,),
        in_specs=[row, pl.BlockSpec((None, 1, D), lambda i: (layer, 0, 0)), row, row],
        out_specs=[row, row, pl.BlockSpec((1, D), lambda i: (0, 0))],
        out_shape=[SDS((T, D), F32), SDS((T, D), BF16), SDS((1, D), F32)],
        name=name, compiler_params=_params("arbitrary"))(h, g3, dn, dh_in)


def _final(h, g2, target):
    T, D = h.shape
    tm = CHUNK

    def body(h_ref, g_ref, t_ref, dh_ref, dhb_ref, dg_ref, loss_ref):
        i = pl.program_id(0)
        x = h_ref[...]
        r = lax.rsqrt(jnp.mean(x * x, axis=-1, keepdims=True) + EPS)
        xh = x * r
        g = g_ref[...]
        live = jnp.where(i > 0, 1.0, 0.0).astype(F32)
        e = ((xh * g) - t_ref[...]) * live
        dy = e * (1.0 / D)
        dxh = dy * g
        dh = r * (dxh - xh * jnp.mean(dxh * xh, axis=-1, keepdims=True))
        dh_ref[...] = dh
        dhb_ref[...] = dh.astype(BF16)

        @pl.when(i == 0)
        def _():
            dg_ref[...] = jnp.zeros_like(dg_ref)
            loss_ref[...] = jnp.zeros_like(loss_ref)

        dg_ref[...] += jnp.sum(dy * xh, axis=0, keepdims=True)
        loss_ref[...] += jnp.sum(e * e) * (0.5 / D)

    row = pl.BlockSpec((tm, D), lambda i: (i, 0))
    return pl.pallas_call(
        body, grid=(T // tm,),
        in_specs=[row, pl.BlockSpec((1, D), lambda i: (0, 0)),
                  pl.BlockSpec((tm, D), lambda i: (jnp.maximum(i - 1, 0), 0))],
        out_specs=[row, row, pl.BlockSpec((1, D), lambda i: (0, 0)), pl.BlockSpec((1, 128), lambda i: (0, 0))],
        out_shape=[SDS((T, D), F32), SDS((T, D), BF16), SDS((1, D), F32), SDS((1, 128), F32)],
        name="final_loss", compiler_params=_params("arbitrary"))(h, g2, target)


def _mm_nn(a, w3, layer, name, res=None, relu2=False):
    M, K = a.shape
    N = w3.shape[2]
    tm = _tile(M, 832)
    tn = _ctile(N, 512 if K <= 2048 else 256)

    def body(*refs):
        acc = jnp.dot(refs[0][...], refs[1][...], preferred_element_type=F32)
        if res is not None:
            acc = acc + refs[2][...]
        if relu2:
            p = jnp.maximum(acc, 0.0)
            acc = (p * p).astype(BF16)
        refs[-1][...] = acc

    in_specs = [pl.BlockSpec((tm, K), lambda i, j: (i, 0)), pl.BlockSpec((None, K, tn), lambda i, j: (layer, 0, j))]
    args = [a, w3]
    tile = pl.BlockSpec((tm, tn), lambda i, j: (i, j))
    if res is not None:
        in_specs.append(tile)
        args.append(res)
    return pl.pallas_call(
        body, grid=(M // tm, N // tn), in_specs=in_specs, out_specs=tile,
        out_shape=SDS((M, N), BF16 if relu2 else F32),
        name=name, compiler_params=_params("parallel", "parallel"))(*args)


def _mm_nt(dy, w3, layer, name, act=None):
    M, N = dy.shape
    K = w3.shape[1]
    tm = _tile(M, 832)
    tk = _ctile(K, 512 if N <= 2048 else 256)

    def body(*refs):
        acc = lax.dot_general(refs[0][...], refs[1][...], (((1,), (1,)), ((), ())), preferred_element_type=F32)
        if act is None:
            refs[2][...] = acc
        else:
            refs[3][...] = (acc * (2.0 * jnp.sqrt(refs[2][...].astype(F32)))).astype(BF16)

    tile = pl.BlockSpec((tm, tk), lambda i, j: (i, j))
    in_specs = [pl.BlockSpec((tm, N), lambda i, j: (i, 0)), pl.BlockSpec((None, tk, N), lambda i, j: (layer, j, 0))]
    args = [dy, w3]
    if act is not None:
        in_specs.append(tile)
        args.append(act)
    return pl.pallas_call(
        body, grid=(M // tm, K // tk), in_specs=in_specs, out_specs=tile,
        out_shape=SDS((M, K), F32 if act is None else BF16),
        name=name, compiler_params=_params("parallel", "parallel"))(*args)


def _fam_dims(kind, K, N):
    return (K // 2, N // N_CHIPS) if kind == "col" else (K // (2 * N_CHIPS), N)


def _mm_tn(x, dy, acc, kind, layer, name):
    M, K = x.shape
    N = dy.shape[1]
    nr, nc = _fam_dims(kind, K, N)
    tk = _ctile(nr)
    tn = _ctile(nc)
    rt, ct = nr // tk, nc // tn

    def body(x_ref, dy_ref, acc_ref, o_ref):
        del acc_ref
        o_ref[...] = lax.dot_general(x_ref[...], dy_ref[...], (((0,), (0,)), ((), ())),
                                     preferred_element_type=F32).astype(BF16)

    if kind == "col":
        omap = lambda i, j: (i // rt, j // ct, layer, i % rt, j % ct)
    else:
        omap = lambda i, j: ((i // rt) % 2, i // (2 * rt), layer, i % rt, j)
    return pl.pallas_call(
        body, grid=(K // tk, N // tn),
        in_specs=[pl.BlockSpec((M, tk), lambda i, j: (0, i)), pl.BlockSpec((M, tn), lambda i, j: (0, j)),
                  pl.BlockSpec(memory_space=pl.ANY)],
        out_specs=pl.BlockSpec((None, None, None, tk, tn), omap),
        out_shape=SDS(acc.shape, BF16), input_output_aliases={2: 0},
        name=name, compiler_params=_params("parallel", "parallel"))(x, dy, acc)


C_EVEN = 512


def _live(rows, base, total):
    r = _row_ids((rows, 1), base)
    return jnp.logical_and(r >= PAD, r < total).astype(F32)


def _conv_taps(win, w_ref, ls, acc, flip):
    for b in range(8):
        rb = win if b == 0 else pltpu.roll(win, 96 - b, 0)
        for a in range(5):
            o = 8 * a + b
            tap = (30 - o) if flip else (o - 2)
            if 0 <= tap < CONV_WIDTH:
                acc = acc + w_ref[pl.ds(tap, 1), ls] * rb[8 * a:8 * a + CHUNK]
    return acc


def _window_sum(win, levels, forward):
    s = win
    n = win.shape[0]
    for k in range(levels):
        step = 1 << k
        s = s + pltpu.roll(s, (n - step) if forward else step, 0)
    return s


def _pool_count(base, g):
    pos = _row_ids((CHUNK, 1), base) - PAD
    return jnp.clip(pos + 1, 1, POOL_WINDOWS[g]).astype(F32)


def _even_fwd(u, cw3, cb3, lg3, lb3, pw4, pb3, ps3, j, name):
    T = u.shape[0]
    C = C_EVEN
    tm = _tile(T, 320)
    nch = tm // CHUNK
    nblk = T // CHUNK

    def body(u_ref, up_ref, cw_ref, cb_ref, lg_ref, lb_ref, pw_ref, pb_ref, ps_ref, o_ref, a_s, p_s, yc_s):
        row0 = pl.program_id(0) * tm
        up = up_ref[...]
        lp = _live(CHUNK, row0 - CHUNK, T)
        a_s[0:CHUNK, :] = up[:, 0:C] * _sigmoid(up[:, C:2 * C]) * lp
        p_s[0:CHUNK, :] = up[:, 2 * C:3 * C] * lp

        def stage(c, _):
            rs = pl.multiple_of(c * CHUNK, CHUNK)
            lv = _live(CHUNK, row0 + rs, T)
            a_s[pl.ds(rs + CHUNK, CHUNK), :] = u_ref[pl.ds(rs, CHUNK), 0:C] * _sigmoid(u_ref[pl.ds(rs, CHUNK), C:2 * C]) * lv
            p_s[pl.ds(rs + CHUNK, CHUNK), :] = u_ref[pl.ds(rs, CHUNK), 2 * C:3 * C] * lv
            return 0

        lax.fori_loop(0, nch, stage, 0)

        def chunk(c, _):
            rs = pl.multiple_of(c * CHUNK, CHUNK)
            lv = _live(CHUNK, row0 + rs, T)
            for cb in range(4):
                ls = slice(cb * 128, (cb + 1) * 128)
                win = a_s[pl.ds(pl.multiple_of(rs + 32, 32), 96), ls]
                acc = jnp.broadcast_to(cb_ref[:, ls], (CHUNK, 128))
                yc_s[:, ls] = _conv_taps(win, cw_ref, ls, acc, False)
            y = yc_s[...]
            xc = y - jnp.mean(y, axis=-1, keepdims=True)
            yn = xc * lax.rsqrt(jnp.mean(xc * xc, axis=-1, keepdims=True) + EPS) * lg_ref[...] + lb_ref[...]
            o_ref[pl.ds(rs, CHUNK), 0:C] = (yn * _sigmoid(yn) * lv).astype(BF16)
            for g in range(4):
                ls = slice(g * 128, (g + 1) * 128)
                win = p_s[pl.ds(pl.multiple_of(rs + 48, 16), 80), ls]
                s = _window_sum(win, g + 1, False)
                d = s[16:80] / _pool_count(row0 + rs, g) - win[16:80]
                yv = jnp.dot(d.astype(BF16), pw_ref[g].astype(BF16), preferred_element_type=F32) + pb_ref[:, ls]
                o_ref[pl.ds(rs, CHUNK), C + g * 128:C + (g + 1) * 128] = (yv * ps_ref[:, ls] * lv).astype(BF16)
            return 0

        lax.fori_loop(0, nch, chunk, 0)

    vec = pl.BlockSpec((None, 1, C), lambda i: (j, 0, 0))
    return pl.pallas_call(
        body, grid=(T // tm,),
        in_specs=[pl.BlockSpec((tm, 3 * C), lambda i: (i, 0)),
                  pl.BlockSpec((CHUNK, 3 * C), lambda i: (jnp.maximum(i * nch - 1, 0), 0)),
                  pl.BlockSpec((None, CONV_ROWS, C), lambda i: (j, 0, 0)), vec, vec, vec,
                  pl.BlockSpec((None, 4, 128, 128), lambda i: (j, 0, 0, 0)), vec, vec],
        out_specs=pl.BlockSpec((tm, 2 * C), lambda i: (i, 0)),
        out_shape=SDS((T, 2 * C), BF16),
        scratch_shapes=[pltpu.VMEM((tm + CHUNK, C), F32), pltpu.VMEM((tm + CHUNK, C), F32), pltpu.VMEM((CHUNK, C), F32)],
        name=name, compiler_params=_params("parallel"))(u, u, cw3, cb3, lg3, lb3, pw4, pb3, ps3)


def _even_bwd(u, dy, cw3, cb3, lg3, lb3, pw4, pb3, ps3, j, name):
    T = u.shape[0]
    C = C_EVEN
    tm = _tile(T, 320)
    nch = tm // CHUNK
    nblk = T // CHUNK
    ntile = T // tm

    def body(u_ref, up_ref, un_ref, dy_ref, dyn_ref, cw_ref, cb_ref, lg_ref, lb_ref, pw_ref, pb_ref, ps_ref,
             du_ref, dcw_ref, dcb_ref, dlg_ref, dlb_ref, dpw_ref, dpb_ref, dps_ref,
             a_s, p_s, dy_s, yc_s, dyc_s, dd_s, ddc_s, dw_s):
        i = pl.program_id(0)
        row0 = i * tm

        @pl.when(i == 0)
        def _():
            for ref in (dcb_ref, dlg_ref, dlb_ref, dpw_ref, dpb_ref, dps_ref, dw_s):
                ref[...] = jnp.zeros_like(ref)

        up = up_ref[...]
        lp = _live(CHUNK, row0 - CHUNK, T)
        a_s[0:CHUNK, :] = up[:, 0:C] * _sigmoid(up[:, C:2 * C]) * lp
        p_s[0:CHUNK, :] = up[:, 2 * C:3 * C] * lp
        un = un_ref[...]
        ln_ = _live(CHUNK, row0 + tm, T)
        a_s[tm + CHUNK:tm + 2 * CHUNK, :] = un[:, 0:C] * _sigmoid(un[:, C:2 * C]) * ln_
        p_s[tm + CHUNK:tm + 2 * CHUNK, :] = un[:, 2 * C:3 * C] * ln_
        dy_s[tm:tm + CHUNK, :] = dyn_ref[...] * ln_
        dyc_s[tm + CHUNK:tm + CHUNK + 32, :] = jnp.zeros((32, C), F32)

        def stage(c, _):
            rs = pl.multiple_of(c * CHUNK, CHUNK)
            lv = _live(CHUNK, row0 + rs, T)
            a_s[pl.ds(rs + CHUNK, CHUNK), :] = u_ref[pl.ds(rs, CHUNK), 0:C] * _sigmoid(u_ref[pl.ds(rs, CHUNK), C:2 * C]) * lv
            p_s[pl.ds(rs + CHUNK, CHUNK), :] = u_ref[pl.ds(rs, CHUNK), 2 * C:3 * C] * lv
            dy_s[pl.ds(rs, CHUNK), :] = dy_ref[pl.ds(rs, CHUNK), :] * lv
            return 0

        lax.fori_loop(0, nch, stage, 0)

        def first(c, _):
            rs = pl.multiple_of(c * CHUNK, CHUNK)
            own = jnp.where(c < nch, 1.0, 0.0).astype(F32)
            for cb in range(4):
                ls = slice(cb * 128, (cb + 1) * 128)
                win = a_s[pl.ds(pl.multiple_of(rs + 32, 32), 96), ls]
                acc = jnp.broadcast_to(cb_ref[:, ls], (CHUNK, 128))
                yc_s[:, ls] = _conv_taps(win, cw_ref, ls, acc, False)
            y = yc_s[...]
            xc = y - jnp.mean(y, axis=-1, keepdims=True)
            rstd = lax.rsqrt(jnp.mean(xc * xc, axis=-1, keepdims=True) + EPS)
            xh = xc * rstd
            yn = xh * lg_ref[...] + lb_ref[...]
            sg = _sigmoid(yn)
            dyn = dy_s[pl.ds(rs, CHUNK), 0:C] * (sg * (1.0 + yn * (1.0 - sg)))
            dlg_ref[...] += jnp.sum(dyn * xh, axis=0, keepdims=True) * own
            dlb_ref[...] += jnp.sum(dyn, axis=0, keepdims=True) * own
            dxh = dyn * lg_ref[...]
            dyc = rstd * (dxh - jnp.mean(dxh, axis=-1, keepdims=True) - xh * jnp.mean(dxh * xh, axis=-1, keepdims=True))
            dyc_s[pl.ds(rs, CHUNK), :] = dyc
            dcb_ref[...] += jnp.sum(dyc, axis=0, keepdims=True) * own
            for g in range(4):
                ls = slice(g * 128, (g + 1) * 128)
                win = p_s[pl.ds(pl.multiple_of(rs + 48, 16), 80), ls]
                s = _window_sum(win, g + 1, False)
                cnt = _pool_count(row0 + rs, g)
                d = (s[16:80] / cnt - win[16:80]).astype(BF16)
                w = pw_ref[g].astype(BF16)
                pre = jnp.dot(d, w, preferred_element_type=F32) + pb_ref[:, ls]
                dyb = dy_s[pl.ds(rs, CHUNK), C + g * 128:C + (g + 1) * 128]
                dpre = dyb * ps_ref[:, ls]
                dps_ref[:, ls] += jnp.sum(dyb * pre, axis=0, keepdims=True) * own
                dpb_ref[:, ls] += jnp.sum(dpre, axis=0, keepdims=True) * own
                dpre_b = (dpre * own).astype(BF16)
                dpw_ref[g] += lax.dot_general(d, dpre_b, (((0,), (0,)), ((), ())), preferred_element_type=F32)
                dd = lax.dot_general(dpre.astype(BF16), w, (((1,), (1,)), ((), ())), preferred_element_type=F32)
                dd_s[pl.ds(rs, CHUNK), ls] = dd
                ddc_s[pl.ds(rs, CHUNK), ls] = dd / cnt
            return 0

        lax.fori_loop(0, nch + 1, first, 0)
        ddc_s[tm + CHUNK:tm + CHUNK + 16, :] = jnp.zeros((16, C), F32)

        def second(c, _):
            rs = pl.multiple_of(c * CHUNK, CHUNK)
            lv = _live(CHUNK, row0 + rs, T)
            for cb in range(4):
                ls = slice(cb * 128, (cb + 1) * 128)
                wd = dyc_s[pl.ds(rs, 96), ls]
                da = _conv_taps(wd, cw_ref, ls, jnp.zeros((CHUNK, 128), F32), True)
                wa = a_s[pl.ds(pl.multiple_of(rs + 32, 32), 96), ls]
                dyc = dyc_s[pl.ds(rs, CHUNK), ls]
                for b in range(8):
                    rb = wa if b == 0 else pltpu.roll(wa, 96 - b, 0)
                    for a in range(5):
                        tap = 8 * a + b - 2
                        if 0 <= tap < CONV_WIDTH:
                            prod = dyc * rb[8 * a:8 * a + CHUNK]
                            part = prod[0:8]
                            for q in range(1, 8):
                                part = part + prod[8 * q:8 * q + 8]
                            dw_s[8 * tap:8 * tap + 8, ls] += part
                val = u_ref[pl.ds(rs, CHUNK), ls]
                sg = _sigmoid(u_ref[pl.ds(rs, CHUNK), C + cb * 128:C + (cb + 1) * 128])
                du_ref[pl.ds(rs, CHUNK), ls] = (da * sg * lv).astype(BF16)
                du_ref[pl.ds(rs, CHUNK), C + cb * 128:C + (cb + 1) * 128] = (da * val * sg * (1.0 - sg) * lv).astype(BF16)
            for g in range(4):
                ls = slice(g * 128, (g + 1) * 128)
                z = _window_sum(ddc_s[pl.ds(rs, 80), ls], g + 1, True)
                dpin = (z[0:CHUNK] - dd_s[pl.ds(rs, CHUNK), ls]) * lv
                du_ref[pl.ds(rs, CHUNK), 2 * C + g * 128:2 * C + (g + 1) * 128] = dpin.astype(BF16)
            return 0

        lax.fori_loop(0, nch, second, 0)

        @pl.when(i == ntile - 1)
        def _():
            for tap in range(CONV_WIDTH):
                dcw_ref[tap:tap + 1, :] = jnp.sum(dw_s[8 * tap:8 * tap + 8, :], axis=0, keepdims=True)
            dcw_ref[CONV_WIDTH:CONV_ROWS, :] = jnp.zeros((CONV_ROWS - CONV_WIDTH, C), F32)

    vec = pl.BlockSpec((None, 1, C), lambda i: (j, 0, 0))
    ovec = pl.BlockSpec((1, C), lambda i: (0, 0))
    return pl.pallas_call(
        body, grid=(ntile,),
        in_specs=[pl.BlockSpec((tm, 3 * C), lambda i: (i, 0)),
                  pl.BlockSpec((CHUNK, 3 * C), lambda i: (jnp.maximum(i * nch - 1, 0), 0)),
                  pl.BlockSpec((CHUNK, 3 * C), lambda i: (jnp.minimum((i + 1) * nch, nblk - 1), 0)),
                  pl.BlockSpec((tm, 2 * C), lambda i: (i, 0)),
                  pl.BlockSpec((CHUNK, 2 * C), lambda i: (jnp.minimum((i + 1) * nch, nblk - 1), 0)),
                  pl.BlockSpec((None, CONV_ROWS, C), lambda i: (j, 0, 0)), vec, vec, vec,
                  pl.BlockSpec((None, 4, 128, 128), lambda i: (j, 0, 0, 0)), vec, vec],
        out_specs=[pl.BlockSpec((tm, 3 * C), lambda i: (i, 0)), pl.BlockSpec((CONV_ROWS, C), lambda i: (0, 0)),
                   ovec, ovec, ovec, pl.BlockSpec((4, 128, 128), lambda i: (0, 0, 0)), ovec, ovec],
        out_shape=[SDS((T, 3 * C), BF16), SDS((CONV_ROWS, C), F32), SDS((1, C), F32), SDS((1, C), F32), SDS((1, C), F32),
                   SDS((4, 128, 128), F32), SDS((1, C), F32), SDS((1, C), F32)],
        scratch_shapes=[pltpu.VMEM((tm + 2 * CHUNK, C), F32), pltpu.VMEM((tm + 2 * CHUNK, C), F32),
                        pltpu.VMEM((tm + CHUNK, 2 * C), F32), pltpu.VMEM((CHUNK, C), F32),
                        pltpu.VMEM((tm + CHUNK + 32, C), F32), pltpu.VMEM((tm + CHUNK, C), F32),
                        pltpu.VMEM((tm + CHUNK + 16, C), F32), pltpu.VMEM((8 * CONV_ROWS, C), F32)],
        name=name, compiler_params=_params("arbitrary"))(u, u, u, dy, dy, cw3, cb3, lg3, lb3, pw4, pb3, ps3)


HI = lax.Precision.HIGHEST


def _dot_nt(a, b):
    return lax.dot_general(a, b, (((1,), (1,)), ((), ())), preferred_element_type=F32)


def _dot_tn(a, b):
    return lax.dot_general(a, b, (((0,), (0,)), ((), ())), preferred_element_type=F32)


def _tri(lower):
    r = lax.broadcasted_iota(jnp.int32, (CHUNK, CHUNK), 0)
    c = lax.broadcasted_iota(jnp.int32, (CHUNK, CHUNK), 1)
    return jnp.where((c <= r) if lower else (c >= r), 1.0, 0.0).astype(F32)


def _hgrn_gates(u_ref, lb_ref, h, D, lv):
    ls = slice(h * HEAD_DIM, (h + 1) * HEAD_DIM)
    qraw = u_ref[:, ls]
    fraw = u_ref[:, D + h * HEAD_DIM:D + (h + 1) * HEAD_DIM]
    v = u_ref[:, 2 * D + h * HEAD_DIM:2 * D + (h + 1) * HEAD_DIM] * lv
    lbv = lb_ref[:, ls]
    sig = _sigmoid(fraw)
    forget = lbv + (1.0 - lbv) * sig
    logf = jnp.log(forget) * lv
    k = (1.0 - forget) * lv
    qsig = _sigmoid(qraw)
    q = qraw * qsig * lv
    return q, k, v, logf, (qraw, qsig, sig, forget, lbv)


def _sub_parts(q, k, b, b_s, I):
    rows = slice(SUB * I, SUB * (I + 1))
    rho = jnp.zeros((1, HEAD_DIM), F32) if I == 0 else b_s[SUB * I - 1:SUB * I, :]
    eI = jnp.exp(b[rows] - rho)
    EI = jnp.exp(jnp.minimum(rho - b, EXP_CAP))
    causal = (lax.broadcasted_iota(jnp.int32, (SUB, CHUNK), 1)
              <= lax.broadcasted_iota(jnp.int32, (SUB, CHUNK), 0) + SUB * I)
    return rows, q[rows] * eI, k * EI, eI, EI, causal


def _hgrn_fwd(u, lb3, layer, gn3, j, name):
    T = u.shape[0]
    D = u.shape[1] // 4
    H = D // HEAD_DIM
    NC = T // CHUNK

    def body(u_ref, lb_ref, gn_ref, y_ref, o_ref, sall_ref, st_s, b_s, lf_s, q_s, k_s):
        n = pl.program_id(0)

        @pl.when(n == 0)
        def _():
            st_s[...] = jnp.zeros_like(st_s)

        lv = _live(CHUNK, n * CHUNK, T)
        heads = range(H)
        cols = [slice(h * HEAD_DIM, (h + 1) * HEAD_DIM) for h in heads]
        vb = []
        for h in heads:
            q, k, v, logf, _ = _hgrn_gates(u_ref, lb_ref, h, D, lv)
            q_s[:, cols[h]] = q
            k_s[:, cols[h]] = k
            lf_s[:, cols[h]] = logf
            vb.append(v.astype(BF16))
        b_s[...] = jnp.dot(_tri(True), lf_s[...], precision=HI, preferred_element_type=F32)
        ops = []
        for h in heads:
            b_h = b_s.at[:, cols[h]]
            b = b_h[...]
            q = q_s[:, cols[h]]
            k = k_s[:, cols[h]]
            blast = b_h[CHUNK - 1:CHUNK, :]
            qh = (q * jnp.exp(b)).astype(BF16)
            kt = (k * jnp.exp(blast - b)).astype(BF16)
            subs = []
            for I in range(CHUNK // SUB):
                _, qI, KI, _, _, causal = _sub_parts(q, k, b, b_h, I)
                subs.append((qI.astype(BF16), KI.astype(BF16), causal))
            ops.append((qh, kt, jnp.exp(blast), subs))
        mm = []
        for h in heads:
            qh, kt, eblast, subs = ops[h]
            st = st_s[h]
            sall_ref[h] = st
            o_inter = _dot_nt(qh, st.astype(BF16))
            st_s[h] = st * eblast + _dot_tn(vb[h], kt)
            mm.append((o_inter, [_dot_nt(qI, KI) for qI, KI, _ in subs]))
        for h in heads:
            o_inter, ps = mm[h]
            p = jnp.concatenate([jnp.where(c, x, 0.0) for x, (_, _, c) in zip(ps, ops[h][3])], axis=0).astype(BF16)
            o = o_inter + jnp.dot(p, vb[h], preferred_element_type=F32)
            o_ref[:, cols[h]] = o
            graw = u_ref[:, 3 * D + h * HEAD_DIM:3 * D + (h + 1) * HEAD_DIM]
            r = lax.rsqrt(jnp.mean(o * o, axis=-1, keepdims=True) + EPS)
            y_ref[:, cols[h]] = (((o * r) * gn_ref[...]) * (graw * _sigmoid(graw))).astype(BF16)

    return pl.pallas_call(
        body, grid=(NC,),
        in_specs=[pl.BlockSpec((CHUNK, 4 * D), lambda n: (n, 0)),
                  pl.BlockSpec((None, 1, D), lambda n: (layer, 0, 0)),
                  pl.BlockSpec((None, 1, HEAD_DIM), lambda n: (j, 0, 0))],
        out_specs=[pl.BlockSpec((CHUNK, D), lambda n: (n, 0)), pl.BlockSpec((CHUNK, D), lambda n: (n, 0)),
                   pl.BlockSpec((None, H, HEAD_DIM, HEAD_DIM), lambda n: (n, 0, 0, 0))],
        out_shape=[SDS((T, D), BF16), SDS((T, D), F32), SDS((NC, H, HEAD_DIM, HEAD_DIM), F32)],
        scratch_shapes=[pltpu.VMEM((H, HEAD_DIM, HEAD_DIM), F32)] + [pltpu.VMEM((CHUNK, D), F32)] * 4,
        name=name, compiler_params=_params("arbitrary"))(u, lb3, gn3)


def _hgrn_bwd(u, o_raw, dy, sall, lb3, layer, gn3, j, name):
    T = u.shape[0]
    D = u.shape[1] // 4
    H = D // HEAD_DIM
    NC = T // CHUNK

    def body(u_ref, o_ref, dy_ref, sall_ref, lb_ref, gn_ref, du_ref, dlb_ref, dgn_ref, dst_s, b_s, lf_s, q_s, k_s, db_s, dk_s):
        step = pl.program_id(0)
        n = NC - 1 - step

        @pl.when(step == 0)
        def _():
            dst_s[...] = jnp.zeros_like(dst_s)
            dlb_ref[...] = jnp.zeros_like(dlb_ref)
            dgn_ref[...] = jnp.zeros_like(dgn_ref)

        lv = _live(CHUNK, n * CHUNK, T)
        last_row = (_row_ids((CHUNK, 1), 0) == CHUNK - 1).astype(F32)
        gn = gn_ref[...]
        heads = range(H)
        cols = [slice(h * HEAD_DIM, (h + 1) * HEAD_DIM) for h in heads]
        vb, dob = [], []
        dgn = jnp.zeros((1, HEAD_DIM), F32)
        for h in heads:
            q, k, v, logf, _ = _hgrn_gates(u_ref, lb_ref, h, D, lv)
            q_s[:, cols[h]] = q
            k_s[:, cols[h]] = k
            lf_s[:, cols[h]] = logf
            vb.append(v.astype(BF16))
            graw = u_ref[:, 3 * D + h * HEAD_DIM:3 * D + (h + 1) * HEAD_DIM]
            gsig = _sigmoid(graw)
            o = o_ref[:, cols[h]]
            r = lax.rsqrt(jnp.mean(o * o, axis=-1, keepdims=True) + EPS)
            xh = o * r
            dyv = dy_ref[:, cols[h]]
            dsg = dyv * (graw * gsig)
            dgn = dgn + jnp.sum(dsg * xh, axis=0, keepdims=True)
            dxh = dsg * gn
            do = r * (dxh - xh * jnp.mean(dxh * xh, axis=-1, keepdims=True))
            dob.append(do.astype(BF16))
            dgraw = dyv * xh * gn * (gsig * (1.0 + graw * (1.0 - gsig)))
            du_ref[:, 3 * D + h * HEAD_DIM:3 * D + (h + 1) * HEAD_DIM] = (dgraw * lv).astype(BF16)
        dgn_ref[...] += dgn
        b_s[...] = jnp.dot(_tri(True), lf_s[...], precision=HI, preferred_element_type=F32)
        ops = []
        for h in heads:
            b_h = b_s.at[:, cols[h]]
            b = b_h[...]
            q = q_s[:, cols[h]]
            k = k_s[:, cols[h]]
            blast = b_h[CHUNK - 1:CHUNK, :]
            eb = jnp.exp(b)
            ekb = jnp.exp(blast - b)
            subs = []
            for I in range(CHUNK // SUB):
                rows, qI, KI, eI, EI, causal = _sub_parts(q, k, b, b_h, I)
                subs.append((rows, qI.astype(BF16), KI.astype(BF16), eI, EI, causal))
            ops.append((eb, ekb, jnp.exp(blast), (q * eb).astype(BF16), (k * ekb).astype(BF16), subs))
        mm = []
        for h in heads:
            eb, ekb, eblast, qhb, ktb, subs = ops[h]
            st = sall_ref[h]
            dst = dst_s[h]
            dstb = dst.astype(BF16)
            dv = _dot_nt(ktb, dstb)
            dqh = jnp.dot(dob[h], st.astype(BF16), preferred_element_type=F32)
            dkt = jnp.dot(vb[h], dstb, preferred_element_type=F32)
            dblast = jnp.sum(dst * st, axis=0, keepdims=True) * eblast
            dst_s[h] = dst * eblast + _dot_tn(dob[h], qhb)
            dp_full = _dot_nt(dob[h], vb[h])
            ps = [_dot_nt(qIb, KIb) for _, qIb, KIb, _, _, _ in subs]
            mm.append((dv, dqh, dkt, dblast, dp_full, ps))
        for h in heads:
            eb, ekb, eblast, qhb, ktb, subs = ops[h]
            dv, dqh, dkt, dblast, dp_full, ps = mm[h]
            p = jnp.concatenate([jnp.where(sub[5], x, 0.0) for x, sub in zip(ps, subs)], axis=0).astype(BF16)
            dv = dv + _dot_tn(p, dob[h])
            du_ref[:, 2 * D + h * HEAD_DIM:2 * D + (h + 1) * HEAD_DIM] = (dv * lv).astype(BF16)
            dq = dqh * eb
            db = dqh * qhb.astype(F32)
            tmp = dkt * ktb.astype(F32)
            dk = dkt * ekb
            db = db - tmp
            dblast = dblast + jnp.sum(tmp, axis=0, keepdims=True)
            dq_parts, db_parts = [], []
            for rows, qIb, KIb, eI, EI, causal in subs:
                dp = jnp.where(causal, dp_full[rows], 0.0).astype(BF16)
                dqI = jnp.dot(dp, KIb, preferred_element_type=F32)
                dKI = _dot_tn(dp, qIb)
                dq_parts.append(dqI * eI)
                db_parts.append(dqI * qIb.astype(F32))
                dk = dk + dKI * EI
                db = db - dKI * KIb.astype(F32)
            dq = dq + jnp.concatenate(dq_parts, axis=0)
            db_s[:, cols[h]] = db + jnp.concatenate(db_parts, axis=0) + last_row * dblast
            dk_s[:, cols[h]] = dk
            qraw = u_ref[:, cols[h]]
            qsig = _sigmoid(qraw)
            du_ref[:, cols[h]] = (dq * (qsig * (1.0 + qraw * (1.0 - qsig))) * lv).astype(BF16)
        lf_s[...] = jnp.dot(_tri(False), db_s[...], precision=HI, preferred_element_type=F32)
        for h in heads:
            fraw = u_ref[:, D + h * HEAD_DIM:D + (h + 1) * HEAD_DIM]
            lbv = lb_ref[:, cols[h]]
            sig = _sigmoid(fraw)
            forget = lbv + (1.0 - lbv) * sig
            dforget = (lf_s[:, cols[h]] / forget - dk_s[:, cols[h]]) * lv
            dlb_ref[:, cols[h]] += jnp.sum(dforget * (1.0 - sig), axis=0, keepdims=True)
            du_ref[:, D + h * HEAD_DIM:D + (h + 1) * HEAD_DIM] = (dforget * (1.0 - lbv) * sig * (1.0 - sig)).astype(BF16)

    rev = lambda s: (NC - 1 - s, 0)
    return pl.pallas_call(
        body, grid=(NC,),
        in_specs=[pl.BlockSpec((CHUNK, 4 * D), rev), pl.BlockSpec((CHUNK, D), rev), pl.BlockSpec((CHUNK, D), rev),
                  pl.BlockSpec((None, H, HEAD_DIM, HEAD_DIM), lambda s: (NC - 1 - s, 0, 0, 0)),
                  pl.BlockSpec((None, 1, D), lambda s: (layer, 0, 0)),
                  pl.BlockSpec((None, 1, HEAD_DIM), lambda s: (j, 0, 0))],
        out_specs=[pl.BlockSpec((CHUNK, 4 * D), rev), pl.BlockSpec((1, D), lambda s: (0, 0)),
                   pl.BlockSpec((1, HEAD_DIM), lambda s: (0, 0))],
        out_shape=[SDS((T, 4 * D), BF16), SDS((1, D), F32), SDS((1, HEAD_DIM), F32)],
        scratch_shapes=[pltpu.VMEM((H, HEAD_DIM, HEAD_DIM), F32)] + [pltpu.VMEM((CHUNK, D), F32)] * 6,
        name=name, compiler_params=_params("arbitrary"))(u, o_raw, dy, sall, lb3, gn3)


def _softmax_layers(p_ref, n_layers):
    rows = [p_ref[l:l + 1, :] for l in range(n_layers)]
    m = functools.reduce(jnp.maximum, rows)
    e = [jnp.exp(x - m) for x in rows]
    tot = functools.reduce(lambda a, b: a + b, e)
    return [x / tot for x in e]


def _lb_fwd(p):
    n_layers, D = p.shape

    def body(p_ref, o_ref):
        s = _softmax_layers(p_ref, n_layers)
        acc = jnp.zeros((1, D), F32)
        o_ref[0:1, :] = acc
        for l in range(1, n_layers):
            acc = acc + s[l]
            o_ref[l:l + 1, :] = acc

    return pl.pallas_call(body, out_shape=SDS(p.shape, F32), name="lb_fwd")(p)


def _lb_bwd(p, dlb):
    n_layers, D = p.shape

    def body(p_ref, d_ref, o_ref):
        s = _softmax_layers(p_ref, n_layers)
        ds = [jnp.zeros((1, D), F32)] * n_layers
        acc = jnp.zeros((1, D), F32)
        for l in range(n_layers - 1, 0, -1):
            acc = acc + d_ref[l:l + 1, :]
            ds[l] = acc
        dot = functools.reduce(lambda a, b: a + b, [s[l] * ds[l] for l in range(n_layers)])
        for l in range(n_layers):
            o_ref[l:l + 1, :] = s[l] * (ds[l] - dot)

    return pl.pallas_call(body, out_shape=SDS(p.shape, F32), name="lb_bwd")(p, dlb)


def _adamw(w, g, m, v, name):
    R, C = w.shape
    tr = _tile(R, 256, 8) if R % 8 == 0 else R

    def body(w_ref, g_ref, m_ref, v_ref, d_ref, mo_ref, vo_ref):
        g_ = g_ref[...]
        m_ = ADAM_B1 * m_ref[...] + (1.0 - ADAM_B1) * g_
        v_ = ADAM_B2 * v_ref[...] + (1.0 - ADAM_B2) * (g_ * g_)
        mh = m_ / (1.0 - ADAM_B1 ** ADAM_STEP)
        vh = v_ / (1.0 - ADAM_B2 ** ADAM_STEP)
        d_ref[...] = -ADAM_LR * (mh / (jnp.sqrt(vh) + ADAM_EPS) + ADAM_WD * w_ref[...])
        mo_ref[...] = m_
        vo_ref[...] = v_

    blk = pl.BlockSpec((tr, C), lambda i: (i, 0))
    return pl.pallas_call(
        body, grid=(R // tr,), in_specs=[blk] * 4, out_specs=[blk] * 3, out_shape=[SDS((R, C), F32)] * 3,
        name=name, compiler_params=_params("parallel"))(w, g, m, v)


HBM_SPEC = pl.BlockSpec(memory_space=pl.ANY)


def _position():
    x, y, c = lax.axis_index("x"), lax.axis_index("y"), lax.axis_index("c")
    chips = [(1 - x, y), (x, 1 - y), (1 - x, 1 - y)]
    return x, y, c, chips


def _place(kind, full_ref, chip, half):
    _, K, N = full_ref.shape
    if kind == "col":
        ns = N // N_CHIPS
        if half is None:
            return full_ref.at[:, :, pl.ds(chip * ns, ns)]
        return full_ref.at[:, pl.ds(half * (K // 2), K // 2), pl.ds(chip * ns, ns)]
    ks = K // N_CHIPS
    if half is None:
        return full_ref.at[:, pl.ds(chip * ks, ks), :]
    return full_ref.at[:, pl.ds(chip * ks + half * (ks // 2), ks // 2), :]


def _gather_weights(shards, kinds):
    n = len(shards)
    fulls = []
    for s, kind in zip(shards, kinds):
        L, ks, ns = s.shape
        fulls.append(SDS((L, ks, ns * N_CHIPS) if kind == "col" else (L, ks * N_CHIPS, ns), s.dtype))

    def body(*refs):
        srcs, outs = refs[:n], refs[n:2 * n]
        send_sems, recv_sems, local_sems = refs[2 * n:]
        x, y, c, chips = _position()
        me, sibling, mine = (x, y, c), (x, y, 1 - c), 2 * x + y

        def copy(f, k, src, dst, to):
            return pltpu.make_async_remote_copy(src_ref=src, dst_ref=dst, send_sem=send_sems.at[6 * f + k],
                                                recv_sem=recv_sems.at[6 * f + k], device_id=to, device_id_type=MESH)

        def half_shard(f, half):
            nr = srcs[f].shape[1] // 2
            return srcs[f].at[:, pl.ds(half * nr, nr), :]

        def place(f, chip, half):
            return _place(kinds[f], outs[f], chip, half)

        local, sent = [], []
        for f in range(n):
            lc = pltpu.make_async_copy(srcs[f], place(f, mine, None), local_sems.at[f])
            lc.start()
            local.append(lc)
            for k, chip in enumerate(chips):
                cp = copy(f, k, half_shard(f, c), place(f, mine, c), (*chip, c))
                cp.start()
                sent.append(cp)
        for f in range(n):
            for k, chip in enumerate(chips):
                landed = place(f, 2 * chip[0] + chip[1], c)
                copy(f, k, landed, landed, me).wait_recv()
                fw = copy(f, 3 + k, landed, landed, sibling)
                fw.start()
                sent.append(fw)
        for f in range(n):
            for k, chip in enumerate(chips):
                landed = place(f, 2 * chip[0] + chip[1], 1 - c)
                copy(f, 3 + k, landed, landed, me).wait_recv()
        for cp in sent:
            cp.wait_send()
        for lc in local:
            lc.wait()

    return pl.pallas_call(
        body, in_specs=[HBM_SPEC] * n, out_specs=[HBM_SPEC] * n, out_shape=fulls,
        scratch_shapes=[pltpu.SemaphoreType.DMA((6 * n,)), pltpu.SemaphoreType.DMA((6 * n,)), pltpu.SemaphoreType.DMA((n,))],
        name="gather_weights")(*shards)


def _swap_halves(accs):
    n = len(accs)

    def body(*refs):
        srcs, outs = refs[:n], refs[n:2 * n]
        send_sems, recv_sems = refs[2 * n:]
        x, y, c, _ = _position()
        cps = []
        for f in range(n):
            cp = pltpu.make_async_remote_copy(src_ref=srcs[f].at[1 - c], dst_ref=outs[f], send_sem=send_sems.at[f],
                                              recv_sem=recv_sems.at[f], device_id=(x, y, 1 - c), device_id_type=MESH)
            cp.start()
            cps.append(cp)
        for cp in cps:
            cp.wait()

    return pl.pallas_call(
        body, in_specs=[HBM_SPEC] * n, out_specs=[HBM_SPEC] * n, out_shape=[SDS(a.shape[1:], a.dtype) for a in accs],
        scratch_shapes=[pltpu.SemaphoreType.DMA((n,)), pltpu.SemaphoreType.DMA((n,))],
        name="swap_halves")(*accs)


def _add_half(c1, acc, recv, name):
    _, nchip, L, nr, nc = acc.shape
    R = nchip * L * nr
    tr = _tile(R, 1024, 16)

    def body(c_ref, a_ref, r_ref, o_ref):
        del c_ref
        o_ref[...] = (a_ref[...].astype(F32) + r_ref[...].astype(F32)).astype(BF16)

    out = pl.pallas_call(
        body,
        grid_spec=pltpu.PrefetchScalarGridSpec(
            num_scalar_prefetch=1, grid=(R // tr,),
            in_specs=[pl.BlockSpec((None, tr, nc), lambda i, c: (c[0], i, 0)), pl.BlockSpec((tr, nc), lambda i, c: (i, 0))],
            out_specs=pl.BlockSpec((tr, nc), lambda i, c: (i, 0))),
        out_shape=SDS((R, nc), BF16), name=name, compiler_params=_params("parallel"),
    )(c1, acc.reshape(2, R, nc), recv.reshape(R, nc))
    return out.reshape(nchip, L, nr, nc)


def _to_owner(sums):
    n = len(sums)

    def body(*refs):
        srcs, outs = refs[:n], refs[n:2 * n]
        send_sems, recv_sems, local_sems = refs[2 * n:]
        x, y, c, chips = _position()
        cps = []
        for f in range(n):
            lc = pltpu.make_async_copy(srcs[f].at[2 * x + y], outs[f].at[3], local_sems.at[f])
            lc.start()
            cps.append(lc)
            for k, chip in enumerate(chips):
                cp = pltpu.make_async_remote_copy(
                    src_ref=srcs[f].at[2 * chip[0] + chip[1]], dst_ref=outs[f].at[k], send_sem=send_sems.at[3 * f + k],
                    recv_sem=recv_sems.at[3 * f + k], device_id=(*chip, c), device_id_type=MESH)
                cp.start()
                cps.append(cp)
        for cp in cps:
            cp.wait()

    return pl.pallas_call(
        body, in_specs=[HBM_SPEC] * n, out_specs=[HBM_SPEC] * n, out_shape=[SDS(s.shape, s.dtype) for s in sums],
        scratch_shapes=[pltpu.SemaphoreType.DMA((3 * n,)), pltpu.SemaphoreType.DMA((3 * n,)), pltpu.SemaphoreType.DMA((n,))],
        name="to_owner")(*sums)


def _sum_owner(c1, parts, name):
    _, L, nr, nc = parts.shape
    tr = _tile(nr, 512, 16)

    def body(c_ref, p_ref, o_ref):
        del c_ref
        o_ref[...] = ((p_ref[3].astype(F32) + p_ref[0].astype(F32)) + p_ref[1].astype(F32)) + p_ref[2].astype(F32)

    return pl.pallas_call(
        body,
        grid_spec=pltpu.PrefetchScalarGridSpec(
            num_scalar_prefetch=1, grid=(L, nr // tr),
            in_specs=[pl.BlockSpec((4, None, tr, nc), lambda l, i, c: (0, l, i, 0))],
            out_specs=pl.BlockSpec((None, None, tr, nc), lambda l, i, c: (l, c[0], i, 0))),
        out_shape=SDS((L, 2, nr, nc), F32), name=name, compiler_params=_params("parallel", "parallel"),
    )(c1, parts)


def _swap_back(grads):
    n = len(grads)

    def body(*refs):
        srcs, outs = refs[:n], refs[n:2 * n]
        send_sems, recv_sems = refs[2 * n:]
        del srcs
        x, y, c, _ = _position()
        cps = []
        for f in range(n):
            half = outs[f].at[:, pl.ds(c, 1)]
            cp = pltpu.make_async_remote_copy(src_ref=half, dst_ref=half, send_sem=send_sems.at[f],
                                              recv_sem=recv_sems.at[f], device_id=(x, y, 1 - c), device_id_type=MESH)
            cp.start()
            cps.append(cp)
        for cp in cps:
            cp.wait()

    return pl.pallas_call(
        body, in_specs=[HBM_SPEC] * n, out_specs=[HBM_SPEC] * n, out_shape=[SDS(g.shape, g.dtype) for g in grads],
        input_output_aliases={f: f for f in range(n)},
        scratch_shapes=[pltpu.SemaphoreType.DMA((n,)), pltpu.SemaphoreType.DMA((n,))],
        name="swap_back")(*grads)


def _all_sum(block):
    m_per, n = block.shape
    n_dev = 2 * N_CHIPS

    def body(x_ref, all_ref, sum_ref, send_sems, recv_sems, local_sem):
        x, y, c, chips = _position()
        me, sibling = (x, y, c), (x, y, 1 - c)

        def rows(px, py, pc):
            return all_ref.at[pl.ds((4 * px + 2 * py + pc) * m_per, m_per), :]

        def copy(k, blk, to, src=None):
            return pltpu.make_async_remote_copy(
                src_ref=rows(*blk) if src is None else src, dst_ref=rows(*blk), send_sem=send_sems.at[k],
                recv_sem=recv_sems.at[k], device_id=to, device_id_type=MESH)

        mine = pltpu.make_async_copy(x_ref, rows(*me), local_sem)
        mine.start()
        first = [copy(0, me, sibling, src=x_ref)]
        first += [copy(1 + k, me, (*chip, c), src=x_ref) for k, chip in enumerate(chips)]
        for cp in first:
            cp.start()
        passed = [copy(4 + k, (*chip, c), sibling) for k, chip in enumerate(chips)]
        for k, chip in enumerate(chips):
            copy(1 + k, (*chip, c), me).wait_recv()
            passed[k].start()
        copy(0, sibling, me).wait_recv()
        for k, chip in enumerate(chips):
            copy(4 + k, (*chip, 1 - c), me).wait_recv()
        for cp in first + passed:
            cp.wait_send()
        mine.wait()
        acc = all_ref[0:m_per, :]
        for d in range(1, n_dev):
            acc = acc + all_ref[d * m_per:(d + 1) * m_per, :]
        sum_ref[...] = acc

    vm = pl.BlockSpec(memory_space=pltpu.VMEM)
    return pl.pallas_call(
        body, in_specs=[vm], out_specs=[vm, vm], out_shape=[SDS((n_dev * m_per, n), F32), SDS((m_per, n), F32)],
        scratch_shapes=[pltpu.SemaphoreType.DMA((7,)), pltpu.SemaphoreType.DMA((7,)), pltpu.SemaphoreType.DMA],
        name="all_sum", compiler_params=_params())(block)[1]


BIG = (("ev_w_in", "col"), ("ev_w_out", "row"), ("od_w_in", "col"), ("od_w_out", "row"), ("mlp_w1", "col"), ("mlp_w2", "row"))
WEIGHTS = ("meta_tokens", "mix_norm_g", "mlp_norm_g", "final_norm_g", "ev_w_in", "ev_conv_w", "ev_conv_b", "ev_ln_g",
           "ev_ln_b", "ev_pool_w", "ev_pool_b", "ev_pool_scale", "ev_w_out", "od_w_in", "od_gnorm_g", "od_w_out",
           "lb_param", "mlp_w1", "mlp_w2")
PACK_UNIT = 1024


def _pack(arrays):
    flat = []
    for a in arrays:
        a = a.reshape(-1)
        flat.append(jnp.pad(a, (0, (-a.shape[0]) % PACK_UNIT)))
    return jnp.concatenate(flat).reshape(-1, 128)


def _unpack(packed, shapes):
    flat = packed.reshape(-1)
    out, off = [], 0
    for s in shapes:
        size = 1
        for d in s:
            size *= d
        out.append(flat[off:off + size].reshape(s))
        off += size + (-size) % PACK_UNIT
    return out


def _local_step(x2, target, W, P):
    D = x2.shape[1]
    n_layers = P["mix_norm_g"].shape[0]
    h = jnp.concatenate([jnp.zeros((PAD, D), F32), P["meta_full"], x2], axis=0)
    T = h.shape[0]
    mix_g = P["mix_norm_g"].reshape(n_layers, 1, D)
    mlp_g = P["mlp_norm_g"].reshape(n_layers, 1, D)
    vec = lambda a: a.reshape(a.shape[0], 1, -1)
    cb3, lg3, lnb3, ps3 = vec(P["ev_conv_b"]), vec(P["ev_ln_g"]), vec(P["ev_ln_b"]), vec(P["ev_pool_scale"])
    pb3 = vec(P["ev_pool_b"])
    gn3 = vec(P["od_gnorm_g"])
    lb_all = _lb_fwd(P["lb_param"])
    lb3 = lb_all.reshape(n_layers, 1, D)
    even = (cb3, lg3, lnb3, P["ev_pool_w"], pb3, ps3)

    saved = []
    for layer in range(n_layers):
        j = layer // 2
        s = {"h": h}
        s["n"] = _rms_fwd(h, mix_g, layer, f"mix_norm_{layer}")
        if layer % 2 == 0:
            s["u"] = _mm_nn(s["n"], W["ev_w_in"], j, f"ev_in_{layer}")
            s["y"] = _even_fwd(s["u"], P["conv_w_full"], *even, j, f"even_fwd_{layer}")
            h = _mm_nn(s["y"], W["ev_w_out"], j, f"ev_out_{layer}", res=h)
        else:
            s["u"] = _mm_nn(s["n"], W["od_w_in"], j, f"od_in_{layer}")
            s["y"], s["o"], s["sall"] = _hgrn_fwd(s["u"], lb3, layer, gn3, j, f"hgrn_fwd_{layer}")
            h = _mm_nn(s["y"], W["od_w_out"], j, f"od_out_{layer}", res=h)
        s["h1"] = h
        s["n2"] = _rms_fwd(h, mlp_g, layer, f"mlp_norm_{layer}")
        s["act"] = _mm_nn(s["n2"], W["mlp_w1"], layer, f"mlp_up_{layer}", relu2=True)
        h = _mm_nn(s["act"], W["mlp_w2"], layer, f"mlp_down_{layer}", res=h)
        saved.append(s)

    dh, dhb, dg_final, loss = _final(h, P["final_norm_g"].reshape(1, D), target)

    acc = {}
    for name, kind in BIG:
        L, K, N = W[name].shape
        acc[name] = lax.empty((2, N_CHIPS, L) + _fam_dims(kind, K, N), BF16)
    small = {"final_norm_g": dg_final}
    per_layer = {k: [None] * n_layers for k in ("mix_norm_g", "mlp_norm_g", "lb")}
    per_pair = {k: [None] * (n_layers // 2) for k in
                ("ev_conv_w", "ev_conv_b", "ev_ln_g", "ev_ln_b", "ev_pool_w", "ev_pool_b", "ev_pool_scale", "od_gnorm_g")}
    for layer in reversed(range(n_layers)):
        j = layer // 2
        s = saved[layer]
        dz = _mm_nt(dhb, W["mlp_w2"], layer, f"d_act_{layer}", act=s["act"])
        acc["mlp_w2"] = _mm_tn(s["act"], dhb, acc["mlp_w2"], "row", layer, f"dw2_{layer}")
        dn2 = _mm_nt(dz, W["mlp_w1"], layer, f"d_n2_{layer}")
        acc["mlp_w1"] = _mm_tn(s["n2"], dz, acc["mlp_w1"], "col", layer, f"dw1_{layer}")
        dh, dhb, per_layer["mlp_norm_g"][layer] = _rms_bwd(s["h1"], mlp_g, layer, dn2, dh, f"mlp_norm_bwd_{layer}")
        if layer % 2 == 0:
            dy = _mm_nt(dhb, W["ev_w_out"], j, f"d_y_{layer}")
            acc["ev_w_out"] = _mm_tn(s["y"], dhb, acc["ev_w_out"], "row", j, f"dwout_{layer}")
            du, dcw, dcb, dlg, dlnb, dpw, dpb, dps = _even_bwd(s["u"], dy, P["conv_w_full"], *even, j, f"even_bwd_{layer}")
            for k, val in (("ev_conv_w", dcw), ("ev_conv_b", dcb), ("ev_ln_g", dlg), ("ev_ln_b", dlnb),
                           ("ev_pool_w", dpw), ("ev_pool_b", dpb), ("ev_pool_scale", dps)):
                per_pair[k][j] = val
            dn = _mm_nt(du, W["ev_w_in"], j, f"d_n_{layer}")
            acc["ev_w_in"] = _mm_tn(s["n"], du, acc["ev_w_in"], "col", j, f"dwin_{layer}")
        else:
            dy = _mm_nt(dhb, W["od_w_out"], j, f"d_y_{layer}")
            acc["od_w_out"] = _mm_tn(s["y"], dhb, acc["od_w_out"], "row", j, f"dwout_{layer}")
            du, per_layer["lb"][layer], per_pair["od_gnorm_g"][j] = _hgrn_bwd(
                s["u"], s["o"], dy, s["sall"], lb3, layer, gn3, j, f"hgrn_bwd_{layer}")
            dn = _mm_nt(du, W["od_w_in"], j, f"d_n_{layer}")
            acc["od_w_in"] = _mm_tn(s["n"], du, acc["od_w_in"], "col", j, f"dwin_{layer}")
        dh, dhb, per_layer["mix_norm_g"][layer] = _rms_bwd(s["h"], mix_g, layer, dn, dh, f"mix_norm_bwd_{layer}")

    small["mix_norm_g"] = jnp.concatenate(per_layer["mix_norm_g"], axis=0)
    small["mlp_norm_g"] = jnp.concatenate(per_layer["mlp_norm_g"], axis=0)
    dlb_all = jnp.concatenate([jnp.zeros((1, D), F32) if g is None else g for g in per_layer["lb"]], axis=0)
    small["lb_param"] = _lb_bwd(P["lb_param"], dlb_all)
    for k, vals in per_pair.items():
        small[k] = jnp.stack(vals, axis=0)
    small["meta_tokens"] = dh[PAD:LEAD]
    return loss, dh, acc, small


def kernel(x, meta_tokens, mix_norm_g, mlp_norm_g, final_norm_g, ev_w_in, ev_conv_w, ev_conv_b, ev_ln_g, ev_ln_b, ev_pool_w, ev_pool_b, ev_pool_scale, ev_w_out, od_w_in, od_gnorm_g, od_w_out, lb_param, mlp_w1, mlp_w2, loss_target, m_meta_tokens, m_mix_norm_g, m_mlp_norm_g, m_final_norm_g, m_ev_w_in, m_ev_conv_w, m_ev_conv_b, m_ev_ln_g, m_ev_ln_b, m_ev_pool_w, m_ev_pool_b, m_ev_pool_scale, m_ev_w_out, m_od_w_in, m_od_gnorm_g, m_od_w_out, m_lb_param, m_mlp_w1, m_mlp_w2, v_meta_tokens, v_mix_norm_g, v_mlp_norm_g, v_final_norm_g, v_ev_w_in, v_ev_conv_w, v_ev_conv_b, v_ev_ln_g, v_ev_ln_b, v_ev_pool_w, v_ev_pool_b, v_ev_pool_scale, v_ev_w_out, v_od_w_in, v_od_gnorm_g, v_od_w_out, v_lb_param, v_mlp_w1, v_mlp_w2):
    given = dict(locals())
    w = {n: given[n] for n in WEIGHTS}
    m = {n: given["m_" + n] for n in WEIGHTS}
    v = {n: given["v_" + n] for n in WEIGHTS}
    chip = 2 * lax.axis_index("x") + lax.axis_index("y")
    c1 = lax.axis_index("c").astype(jnp.int32).reshape(1)

    shards = [_cast_bf16(w[n].reshape(-1, w[n].shape[2]), f"cast_{n}").reshape(w[n].shape) for n, _ in BIG]
    conv_pad = jnp.pad(ev_conv_w, ((0, 0), (0, CONV_ROWS - CONV_WIDTH), (0, 0)))
    fulls = _gather_weights(shards + [meta_tokens[None], conv_pad], [k for _, k in BIG] + ["col", "col"])
    W = {n: f for (n, _), f in zip(BIG, fulls)}
    P = {n: w[n] for n in ("mix_norm_g", "mlp_norm_g", "final_norm_g", "ev_conv_b", "ev_ln_g", "ev_ln_b", "ev_pool_w",
                           "ev_pool_b", "ev_pool_scale", "od_gnorm_g", "lb_param")}
    P["meta_full"] = fulls[-2][0]
    P["conv_w_full"] = fulls[-1]

    loss, dh, acc, small = _local_step(x[0], loss_target[0], W, P)

    names = [n for n, _ in BIG]
    from_sibling = _swap_halves([acc[n] for n in names])
    chip_sums = [_add_half(c1, acc[n], r, f"add_half_{n}") for n, r in zip(names, from_sibling)]
    pieces = _to_owner(chip_sums)
    halves = [_sum_owner(c1, p, f"sum_owner_{n}") for n, p in zip(names, pieces)]
    grads = dict(zip(names, _swap_back(halves)))

    order = [n for n in WEIGHTS if n not in names]
    packed = _all_sum(_pack([small[n] for n in order] + [loss]))
    total = _unpack(packed, [small[n].shape for n in order] + [loss.shape])
    loss_sum = total[-1][0, 0]
    gsmall = dict(zip(order, total[:-1]))
    gsmall["meta_tokens"] = lax.dynamic_slice_in_dim(gsmall["meta_tokens"], chip * meta_tokens.shape[1], meta_tokens.shape[1], 1)
    gsmall["ev_conv_w"] = lax.dynamic_slice_in_dim(gsmall["ev_conv_w"][:, :CONV_WIDTH], chip * ev_conv_w.shape[2], ev_conv_w.shape[2], 2)

    g_out, d_out, m_out, v_out = {}, {}, {}, {}
    for n in WEIGHTS:
        shape = w[n].shape
        g = (grads[n] if n in grads else gsmall[n]).reshape(shape)
        cols = shape[-1] if len(shape) > 1 else 128
        two = lambda a: a.reshape(-1, cols)
        d_, m_, v_ = _adamw(two(w[n]), two(g), two(m[n]), two(v[n]), f"adamw_{n}")
        g_out[n], d_out[n], m_out[n], v_out[n] = g, d_.reshape(shape), m_.reshape(shape), v_.reshape(shape)

    grad_x = dh[LEAD:][None]
    return (loss_sum, grad_x, *[g_out[n] for n in WEIGHTS], *[d_out[n] for n in WEIGHTS],
            *[m_out[n] for n in WEIGHTS], *[v_out[n] for n in WEIGHTS])
```

```python
import functools

import jax
import jax.numpy as jnp
from jax import lax
from jax.experimental import pallas as pl
from jax.experimental.pallas import tpu as pltpu

F32 = jnp.float32
BF16 = jnp.bfloat16
SDS = jax.ShapeDtypeStruct
MESH = pl.DeviceIdType.MESH
ANY_SPEC = pl.BlockSpec(memory_space=pl.ANY)

N_META = 16
CHUNK = 64
LEAD = CHUNK
PAD = LEAD - N_META
CONV_WIDTH = 31
CONV_ROWS = 32
POOL_WINDOWS = (2, 4, 8, 16)
HEAD_DIM = 128
SUB = 16
EXP_CAP = 80.0
EPS = 1e-6
ADAM_LR = 0.001
ADAM_B1 = 0.9
ADAM_B2 = 0.999
ADAM_EPS = 1e-08
ADAM_WD = 0.01
ADAM_STEP = 10
N_CHIPS = 4
VMEM_LIMIT = 52 << 20


def _params(*sem):
    return pltpu.CompilerParams(dimension_semantics=sem if sem else None, vmem_limit_bytes=VMEM_LIMIT)


def _tile(n, target, unit=CHUNK):
    best = None
    for t in range(unit, min(n, target) + 1, unit):
        if n % t == 0:
            best = t
    assert best is not None, (n, target, unit)
    return best


def _ctile(n, target=512):
    for t in (512, 384, 256, 128):
        if t <= target and n % t == 0:
            return t
    raise ValueError(n)


def _sigmoid(x):
    return 1.0 / (1.0 + jnp.exp(-x))


def _row_ids(shape, base):
    return lax.broadcasted_iota(jnp.int32, shape, 0) + base


def _cast_place(w3, layer, kind, chip1, dtype, name):
    _, ks, ns = w3.shape
    tr = _tile(ks, 512, 16)
    full = (ks, ns * N_CHIPS) if kind == "col" else (ks * N_CHIPS, ns)

    def body(chip_ref, w_ref, o_ref):
        del chip_ref
        o_ref[...] = w_ref[...].astype(dtype)

    omap = (lambda i, chip: (i, chip[0])) if kind == "col" else (lambda i, chip: (chip[0] * (ks // tr) + i, 0))
    return pl.pallas_call(
        body,
        grid_spec=pltpu.PrefetchScalarGridSpec(
            num_scalar_prefetch=1, grid=(ks // tr,),
            in_specs=[pl.BlockSpec((None, tr, ns), lambda i, chip: (layer, i, 0))],
            out_specs=pl.BlockSpec((tr, ns), omap)),
        out_shape=SDS(full, dtype), name=name, compiler_params=_params("parallel"))(chip1, w3)


def _rms_fwd(h, g3, layer, name, deps=()):
    T, D = h.shape
    tm = _tile(T, 832)

    def body(h_ref, g_ref, *rest):
        n_ref = rest[-1]
        x = h_ref[...]
        r = lax.rsqrt(jnp.mean(x * x, axis=-1, keepdims=True) + EPS)
        n_ref[...] = ((x * r) * g_ref[...]).astype(BF16)

    return pl.pallas_call(
        body, grid=(T // tm,),
        in_specs=[pl.BlockSpec((tm, D), lambda i: (i, 0)), pl.BlockSpec((None, 1, D), lambda i: (layer, 0, 0))]
        + [ANY_SPEC] * len(deps),
        out_specs=pl.BlockSpec((tm, D), lambda i: (i, 0)), out_shape=SDS((T, D), BF16),
        name=name, compiler_params=_params("parallel"))(h, g3, *deps)


def _rms_bwd(h, g3, layer, dn, dh_in, name):
    T, D = h.shape
    tm = _tile(T, 320)

    def body(h_ref, g_ref, dn_ref, dhi_ref, dh_ref, dhb_ref, dg_ref):
        x = h_ref[...]
        r = lax.rsqrt(jnp.mean(x * x, axis=-1, keepdims=True) + EPS)
        xh = x * r
        dn_ = dn_ref[...]
        dxh = dn_ * g_ref[...]
        dh = dhi_ref[...] + r * (dxh - xh * jnp.mean(dxh * xh, axis=-1, keepdims=True))
        dh_ref[...] = dh
        dhb_ref[...] = dh.astype(BF16)

        @pl.when(pl.program_id(0) == 0)
        def _():
            dg_ref[...] = jnp.zeros_like(dg_ref)

        dg_ref[...] += jnp.sum(dn_ * xh, axis=0, keepdims=True)

    row = pl.BlockSpec((tm, D), lambda i: (i, 0))
    return pl.pallas_call(
        body, grid=(T // tm,),
        in_specs=[row, pl.BlockSpec((None, 1, D), lambda i: (layer, 0, 0)), row, row],
        out_specs=[row, row, pl.BlockSpec((1, D), lambda i: (0, 0))],
        out_shape=[SDS((T, D), F32), SDS((T, D), BF16), SDS((1, D), F32)],
        name=name, compiler_params=_params("arbitrary"))(h, g3, dn, dh_in)


def _final(h, g2, target):
    T, D = h.shape
    tm = CHUNK

    def body(h_ref, g_ref, t_ref, dh_ref, dhb_ref, dg_ref, loss_ref):
        i = pl.program_id(0)
        x = h_ref[...]
        r = lax.rsqrt(jnp.mean(x * x, axis=-1, keepdims=True) + EPS)
        xh = x * r
        g = g_ref[...]
        live = jnp.where(i > 0, 1.0, 0.0).astype(F32)
        e = ((xh * g) - t_ref[...]) * live
        dy = e * (1.0 / D)
        dxh = dy * g
        dh = r * (dxh - xh * jnp.mean(dxh * xh, axis=-1, keepdims=True))
        dh_ref[...] = dh
        dhb_ref[...] = dh.astype(BF16)

        @pl.when(i == 0)
        def _():
            dg_ref[...] = jnp.zeros_like(dg_ref)
            loss_ref[...] = jnp.zeros_like(loss_ref)

        dg_ref[...] += jnp.sum(dy * xh, axis=0, keepdims=True)
        loss_ref[...] += jnp.sum(e * e) * (0.5 / D)

    row = pl.BlockSpec((tm, D), lambda i: (i, 0))
    return pl.pallas_call(
        body, grid=(T // tm,),
        in_specs=[row, pl.BlockSpec((1, D), lambda i: (0, 0)),
                  pl.BlockSpec((tm, D), lambda i: (jnp.maximum(i - 1, 0), 0))],
        out_specs=[row, row, pl.BlockSpec((1, D), lambda i: (0, 0)), pl.BlockSpec((1, 128), lambda i: (0, 0))],
        out_shape=[SDS((T, D), F32), SDS((T, D), BF16), SDS((1, D), F32), SDS((1, 128), F32)],
        name="final_loss", compiler_params=_params("arbitrary"))(h, g2, target)


def _mm_nn(a, w3, layer, name, res=None, relu2=False):
    M, K = a.shape
    N = w3.shape[2]
    tm = _tile(M, 832)
    tn = _ctile(N, 512 if K <= 2048 else 256)

    def body(*refs):
        acc = jnp.dot(refs[0][...], refs[1][...], preferred_element_type=F32)
        if res is not None:
            acc = acc + refs[2][...]
        if relu2:
            p = jnp.maximum(acc, 0.0)
            acc = (p * p).astype(BF16)
        refs[-1][...] = acc

    in_specs = [pl.BlockSpec((tm, K), lambda i, j: (i, 0)), pl.BlockSpec((None, K, tn), lambda i, j: (layer, 0, j))]
    args = [a, w3]
    tile = pl.BlockSpec((tm, tn), lambda i, j: (i, j))
    if res is not None:
        in_specs.append(tile)
        args.append(res)
    return pl.pallas_call(
        body, grid=(M // tm, N // tn), in_specs=in_specs, out_specs=tile,
        out_shape=SDS((M, N), BF16 if relu2 else F32),
        name=name, compiler_params=_params("parallel", "parallel"))(*args)


def _mm_nt(dy, w3, layer, name, act=None, deps=()):
    M, N = dy.shape
    K = w3.shape[1]
    tm = _tile(M, 832)
    tk = _ctile(K, 512 if N <= 2048 else 256)

    def body(*refs):
        acc = lax.dot_general(refs[0][...], refs[1][...], (((1,), (1,)), ((), ())), preferred_element_type=F32)
        if act is not None:
            acc = (acc * (2.0 * jnp.sqrt(refs[2][...].astype(F32)))).astype(BF16)
        refs[-1][...] = acc

    tile = pl.BlockSpec((tm, tk), lambda i, j: (i, j))
    in_specs = [pl.BlockSpec((tm, N), lambda i, j: (i, 0)), pl.BlockSpec((None, tk, N), lambda i, j: (layer, j, 0))]
    args = [dy, w3]
    if act is not None:
        in_specs.append(tile)
        args.append(act)
    in_specs += [ANY_SPEC] * len(deps)
    args += list(deps)
    return pl.pallas_call(
        body, grid=(M // tm, K // tk), in_specs=in_specs, out_specs=tile,
        out_shape=SDS((M, K), F32 if act is None else BF16),
        name=name, compiler_params=_params("parallel", "parallel"))(*args)


def _fam_dims(kind, K, N):
    return (K // 2, N // N_CHIPS) if kind == "col" else (K // (2 * N_CHIPS), N)


def _mm_tn(x, dy, kind, name):
    M, K = x.shape
    N = dy.shape[1]
    nr, nc = _fam_dims(kind, K, N)
    tk = _ctile(nr)
    tn = _ctile(nc)
    rt, ct = nr // tk, nc // tn

    def body(x_ref, dy_ref, o_ref):
        o_ref[...] = lax.dot_general(x_ref[...], dy_ref[...], (((0,), (0,)), ((), ())),
                                     preferred_element_type=F32).astype(BF16)

    if kind == "col":
        omap = lambda i, j: (i // rt, j // ct, i % rt, j % ct)
    else:
        omap = lambda i, j: ((i // rt) % 2, i // (2 * rt), i % rt, j)
    return pl.pallas_call(
        body, grid=(K // tk, N // tn),
        in_specs=[pl.BlockSpec((M, tk), lambda i, j: (0, i)), pl.BlockSpec((M, tn), lambda i, j: (0, j))],
        out_specs=pl.BlockSpec((None, None, tk, tn), omap),
        out_shape=SDS((2, N_CHIPS, nr, nc), BF16),
        name=name, compiler_params=_params("parallel", "parallel"))(x, dy)


C_EVEN = 512


def _live(rows, base, total):
    r = _row_ids((rows, 1), base)
    return jnp.logical_and(r >= PAD, r < total).astype(F32)


def _conv_taps(win, w_ref, ls, acc, flip):
    for b in range(8):
        rb = win if b == 0 else pltpu.roll(win, 96 - b, 0)
        for a in range(5):
            o = 8 * a + b
            tap = (30 - o) if flip else (o - 2)
            if 0 <= tap < CONV_WIDTH:
                acc = acc + w_ref[pl.ds(tap, 1), ls] * rb[8 * a:8 * a + CHUNK]
    return acc


def _window_sum(win, levels, forward):
    s = win
    n = win.shape[0]
    for k in range(levels):
        step = 1 << k
        s = s + pltpu.roll(s, (n - step) if forward else step, 0)
    return s


def _pool_count(base, g):
    pos = _row_ids((CHUNK, 1), base) - PAD
    return jnp.clip(pos + 1, 1, POOL_WINDOWS[g]).astype(F32)


def _even_fwd(u, cw3, cb3, lg3, lb3, pw4, pb3, ps3, j, name):
    T = u.shape[0]
    C = C_EVEN
    tm = _tile(T, 320)
    nch = tm // CHUNK
    nblk = T // CHUNK

    def body(u_ref, up_ref, cw_ref, cb_ref, lg_ref, lb_ref, pw_ref, pb_ref, ps_ref, o_ref, a_s, p_s, yc_s):
        row0 = pl.program_id(0) * tm
        up = up_ref[...]
        lp = _live(CHUNK, row0 - CHUNK, T)
        a_s[0:CHUNK, :] = up[:, 0:C] * _sigmoid(up[:, C:2 * C]) * lp
        p_s[0:CHUNK, :] = up[:, 2 * C:3 * C] * lp

        def stage(c, _):
            rs = pl.multiple_of(c * CHUNK, CHUNK)
            lv = _live(CHUNK, row0 + rs, T)
            a_s[pl.ds(rs + CHUNK, CHUNK), :] = u_ref[pl.ds(rs, CHUNK), 0:C] * _sigmoid(u_ref[pl.ds(rs, CHUNK), C:2 * C]) * lv
            p_s[pl.ds(rs + CHUNK, CHUNK), :] = u_ref[pl.ds(rs, CHUNK), 2 * C:3 * C] * lv
            return 0

        lax.fori_loop(0, nch, stage, 0)

        def chunk(c, _):
            rs = pl.multiple_of(c * CHUNK, CHUNK)
            lv = _live(CHUNK, row0 + rs, T)
            for cb in range(4):
                ls = slice(cb * 128, (cb + 1) * 128)
                win = a_s[pl.ds(pl.multiple_of(rs + 32, 32), 96), ls]
                acc = jnp.broadcast_to(cb_ref[:, ls], (CHUNK, 128))
                yc_s[:, ls] = _conv_taps(win, cw_ref, ls, acc, False)
            y = yc_s[...]
            xc = y - jnp.mean(y, axis=-1, keepdims=True)
            yn = xc * lax.rsqrt(jnp.mean(xc * xc, axis=-1, keepdims=True) + EPS) * lg_ref[...] + lb_ref[...]
            o_ref[pl.ds(rs, CHUNK), 0:C] = (yn * _sigmoid(yn) * lv).astype(BF16)
            for g in range(4):
                ls = slice(g * 128, (g + 1) * 128)
                win = p_s[pl.ds(pl.multiple_of(rs + 48, 16), 80), ls]
                s = _window_sum(win, g + 1, False)
                d = s[16:80] / _pool_count(row0 + rs, g) - win[16:80]
                yv = jnp.dot(d.astype(BF16), pw_ref[g].astype(BF16), preferred_element_type=F32) + pb_ref[:, ls]
                o_ref[pl.ds(rs, CHUNK), C + g * 128:C + (g + 1) * 128] = (yv * ps_ref[:, ls] * lv).astype(BF16)
            return 0

        lax.fori_loop(0, nch, chunk, 0)

    vec = pl.BlockSpec((None, 1, C), lambda i: (j, 0, 0))
    return pl.pallas_call(
        body, grid=(T // tm,),
        in_specs=[pl.BlockSpec((tm, 3 * C), lambda i: (i, 0)),
                  pl.BlockSpec((CHUNK, 3 * C), lambda i: (jnp.maximum(i * nch - 1, 0), 0)),
                  pl.BlockSpec((None, CONV_ROWS, C), lambda i: (j, 0, 0)), vec, vec, vec,
                  pl.BlockSpec((None, 4, 128, 128), lambda i: (j, 0, 0, 0)), vec, vec],
        out_specs=pl.BlockSpec((tm, 2 * C), lambda i: (i, 0)),
        out_shape=SDS((T, 2 * C), BF16),
        scratch_shapes=[pltpu.VMEM((tm + CHUNK, C), F32), pltpu.VMEM((tm + CHUNK, C), F32), pltpu.VMEM((CHUNK, C), F32)],
        name=name, compiler_params=_params("parallel"))(u, u, cw3, cb3, lg3, lb3, pw4, pb3, ps3)


def _even_bwd(u, dy, cw3, cb3, lg3, lb3, pw4, pb3, ps3, j, name):
    T = u.shape[0]
    C = C_EVEN
    tm = _tile(T, 320)
    nch = tm // CHUNK
    nblk = T // CHUNK
    ntile = T // tm

    def body(u_ref, up_ref, un_ref, dy_ref, dyn_ref, cw_ref, cb_ref, lg_ref, lb_ref, pw_ref, pb_ref, ps_ref,
             du_ref, dcw_ref, dcb_ref, dlg_ref, dlb_ref, dpw_ref, dpb_ref, dps_ref,
             a_s, p_s, dy_s, yc_s, dyc_s, dd_s, ddc_s, dw_s):
        i = pl.program_id(0)
        row0 = i * tm

        @pl.when(i == 0)
        def _():
            for ref in (dcb_ref, dlg_ref, dlb_ref, dpw_ref, dpb_ref, dps_ref, dw_s):
                ref[...] = jnp.zeros_like(ref)

        up = up_ref[...]
        lp = _live(CHUNK, row0 - CHUNK, T)
        a_s[0:CHUNK, :] = up[:, 0:C] * _sigmoid(up[:, C:2 * C]) * lp
        p_s[0:CHUNK, :] = up[:, 2 * C:3 * C] * lp
        un = un_ref[...]
        ln_ = _live(CHUNK, row0 + tm, T)
        a_s[tm + CHUNK:tm + 2 * CHUNK, :] = un[:, 0:C] * _sigmoid(un[:, C:2 * C]) * ln_
        p_s[tm + CHUNK:tm + 2 * CHUNK, :] = un[:, 2 * C:3 * C] * ln_
        dy_s[tm:tm + CHUNK, :] = dyn_ref[...] * ln_
        dyc_s[tm + CHUNK:tm + CHUNK + 32, :] = jnp.zeros((32, C), F32)

        def stage(c, _):
            rs = pl.multiple_of(c * CHUNK, CHUNK)
            lv = _live(CHUNK, row0 + rs, T)
            a_s[pl.ds(rs + CHUNK, CHUNK), :] = u_ref[pl.ds(rs, CHUNK), 0:C] * _sigmoid(u_ref[pl.ds(rs, CHUNK), C:2 * C]) * lv
            p_s[pl.ds(rs + CHUNK, CHUNK), :] = u_ref[pl.ds(rs, CHUNK), 2 * C:3 * C] * lv
            dy_s[pl.ds(rs, CHUNK), :] = dy_ref[pl.ds(rs, CHUNK), :] * lv
            return 0

        lax.fori_loop(0, nch, stage, 0)

        def first(c, _):
            rs = pl.multiple_of(c * CHUNK, CHUNK)
            own = jnp.where(c < nch, 1.0, 0.0).astype(F32)
            for cb in range(4):
                ls = slice(cb * 128, (cb + 1) * 128)
                win = a_s[pl.ds(pl.multiple_of(rs + 32, 32), 96), ls]
                acc = jnp.broadcast_to(cb_ref[:, ls], (CHUNK, 128))
                yc_s[:, ls] = _conv_taps(win, cw_ref, ls, acc, False)
            y = yc_s[...]
            xc = y - jnp.mean(y, axis=-1, keepdims=True)
            rstd = lax.rsqrt(jnp.mean(xc * xc, axis=-1, keepdims=True) + EPS)
            xh = xc * rstd
            yn = xh * lg_ref[...] + lb_ref[...]
            sg = _sigmoid(yn)
            dyn = dy_s[pl.ds(rs, CHUNK), 0:C] * (sg * (1.0 + yn * (1.0 - sg)))
            dlg_ref[...] += jnp.sum(dyn * xh, axis=0, keepdims=True) * own
            dlb_ref[...] += jnp.sum(dyn, axis=0, keepdims=True) * own
            dxh = dyn * lg_ref[...]
            dyc = rstd * (dxh - jnp.mean(dxh, axis=-1, keepdims=True) - xh * jnp.mean(dxh * xh, axis=-1, keepdims=True))
            dyc_s[pl.ds(rs, CHUNK), :] = dyc
            dcb_ref[...] += jnp.sum(dyc, axis=0, keepdims=True) * own
            for g in range(4):
                ls = slice(g * 128, (g + 1) * 128)
                win = p_s[pl.ds(pl.multiple_of(rs + 48, 16), 80), ls]
                s = _window_sum(win, g + 1, False)
                cnt = _pool_count(row0 + rs, g)
                d = (s[16:80] / cnt - win[16:80]).astype(BF16)
                w = pw_ref[g].astype(BF16)
                pre = jnp.dot(d, w, preferred_element_type=F32) + pb_ref[:, ls]
                dyb = dy_s[pl.ds(rs, CHUNK), C + g * 128:C + (g + 1) * 128]
                dpre = dyb * ps_ref[:, ls]
                dps_ref[:, ls] += jnp.sum(dyb * pre, axis=0, keepdims=True) * own
                dpb_ref[:, ls] += jnp.sum(dpre, axis=0, keepdims=True) * own
                dpre_b = (dpre * own).astype(BF16)
                dpw_ref[g] += lax.dot_general(d, dpre_b, (((0,), (0,)), ((), ())), preferred_element_type=F32)
                dd = lax.dot_general(dpre.astype(BF16), w, (((1,), (1,)), ((), ())), preferred_element_type=F32)
                dd_s[pl.ds(rs, CHUNK), ls] = dd
                ddc_s[pl.ds(rs, CHUNK), ls] = dd / cnt
            return 0

        lax.fori_loop(0, nch + 1, first, 0)
        ddc_s[tm + CHUNK:tm + CHUNK + 16, :] = jnp.zeros((16, C), F32)

        def second(c, _):
            rs = pl.multiple_of(c * CHUNK, CHUNK)
            lv = _live(CHUNK, row0 + rs, T)
            for cb in range(4):
                ls = slice(cb * 128, (cb + 1) * 128)
                wd = dyc_s[pl.ds(rs, 96), ls]
                da = _conv_taps(wd, cw_ref, ls, jnp.zeros((CHUNK, 128), F32), True)
                wa = a_s[pl.ds(pl.multiple_of(rs + 32, 32), 96), ls]
                dyc = dyc_s[pl.ds(rs, CHUNK), ls]
                for b in range(8):
                    rb = wa if b == 0 else pltpu.roll(wa, 96 - b, 0)
                    for a in range(5):
                        tap = 8 * a + b - 2
                        if 0 <= tap < CONV_WIDTH:
                            prod = dyc * rb[8 * a:8 * a + CHUNK]
                            part = prod[0:8]
                            for q in range(1, 8):
                                part = part + prod[8 * q:8 * q + 8]
                            dw_s[8 * tap:8 * tap + 8, ls] += part
                val = u_ref[pl.ds(rs, CHUNK), ls]
                sg = _sigmoid(u_ref[pl.ds(rs, CHUNK), C + cb * 128:C + (cb + 1) * 128])
                du_ref[pl.ds(rs, CHUNK), ls] = (da * sg * lv).astype(BF16)
                du_ref[pl.ds(rs, CHUNK), C + cb * 128:C + (cb + 1) * 128] = (da * val * sg * (1.0 - sg) * lv).astype(BF16)
            for g in range(4):
                ls = slice(g * 128, (g + 1) * 128)
                z = _window_sum(ddc_s[pl.ds(rs, 80), ls], g + 1, True)
                dpin = (z[0:CHUNK] - dd_s[pl.ds(rs, CHUNK), ls]) * lv
                du_ref[pl.ds(rs, CHUNK), 2 * C + g * 128:2 * C + (g + 1) * 128] = dpin.astype(BF16)
            return 0

        lax.fori_loop(0, nch, second, 0)

        @pl.when(i == ntile - 1)
        def _():
            for tap in range(CONV_WIDTH):
                dcw_ref[tap:tap + 1, :] = jnp.sum(dw_s[8 * tap:8 * tap + 8, :], axis=0, keepdims=True)
            dcw_ref[CONV_WIDTH:CONV_ROWS, :] = jnp.zeros((CONV_ROWS - CONV_WIDTH, C), F32)

    vec = pl.BlockSpec((None, 1, C), lambda i: (j, 0, 0))
    ovec = pl.BlockSpec((1, C), lambda i: (0, 0))
    return pl.pallas_call(
        body, grid=(ntile,),
        in_specs=[pl.BlockSpec((tm, 3 * C), lambda i: (i, 0)),
                  pl.BlockSpec((CHUNK, 3 * C), lambda i: (jnp.maximum(i * nch - 1, 0), 0)),
                  pl.BlockSpec((CHUNK, 3 * C), lambda i: (jnp.minimum((i + 1) * nch, nblk - 1), 0)),
                  pl.BlockSpec((tm, 2 * C), lambda i: (i, 0)),
                  pl.BlockSpec((CHUNK, 2 * C), lambda i: (jnp.minimum((i + 1) * nch, nblk - 1), 0)),
                  pl.BlockSpec((None, CONV_ROWS, C), lambda i: (j, 0, 0)), vec, vec, vec,
                  pl.BlockSpec((None, 4, 128, 128), lambda i: (j, 0, 0, 0)), vec, vec],
        out_specs=[pl.BlockSpec((tm, 3 * C), lambda i: (i, 0)), pl.BlockSpec((CONV_ROWS, C), lambda i: (0, 0)),
                   ovec, ovec, ovec, pl.BlockSpec((4, 128, 128), lambda i: (0, 0, 0)), ovec, ovec],
        out_shape=[SDS((T, 3 * C), BF16), SDS((CONV_ROWS, C), F32), SDS((1, C), F32), SDS((1, C), F32), SDS((1, C), F32),
                   SDS((4, 128, 128), F32), SDS((1, C), F32), SDS((1, C), F32)],
        scratch_shapes=[pltpu.VMEM((tm + 2 * CHUNK, C), F32), pltpu.VMEM((tm + 2 * CHUNK, C), F32),
                        pltpu.VMEM((tm + CHUNK, 2 * C), F32), pltpu.VMEM((CHUNK, C), F32),
                        pltpu.VMEM((tm + CHUNK + 32, C), F32), pltpu.VMEM((tm + CHUNK, C), F32),
                        pltpu.VMEM((tm + CHUNK + 16, C), F32), pltpu.VMEM((8 * CONV_ROWS, C), F32)],
        name=name, compiler_params=_params("arbitrary"))(u, u, u, dy, dy, cw3, cb3, lg3, lb3, pw4, pb3, ps3)


HI = lax.Precision.HIGHEST


def _dot_nt(a, b):
    return lax.dot_general(a, b, (((1,), (1,)), ((), ())), preferred_element_type=F32)


def _dot_tn(a, b):
    return lax.dot_general(a, b, (((0,), (0,)), ((), ())), preferred_element_type=F32)


def _tri(lower):
    r = lax.broadcasted_iota(jnp.int32, (CHUNK, CHUNK), 0)
    c = lax.broadcasted_iota(jnp.int32, (CHUNK, CHUNK), 1)
    return jnp.where((c <= r) if lower else (c >= r), 1.0, 0.0).astype(F32)


def _hgrn_gates(u_ref, lb_ref, h, D, lv):
    ls = slice(h * HEAD_DIM, (h + 1) * HEAD_DIM)
    qraw = u_ref[:, ls]
    fraw = u_ref[:, D + h * HEAD_DIM:D + (h + 1) * HEAD_DIM]
    v = u_ref[:, 2 * D + h * HEAD_DIM:2 * D + (h + 1) * HEAD_DIM] * lv
    lbv = lb_ref[:, ls]
    sig = _sigmoid(fraw)
    forget = lbv + (1.0 - lbv) * sig
    logf = jnp.log(forget) * lv
    k = (1.0 - forget) * lv
    qsig = _sigmoid(qraw)
    q = qraw * qsig * lv
    return q, k, v, logf, (qraw, qsig, sig, forget, lbv)


def _sub_parts(q, k, b, b_s, I):
    rows = slice(SUB * I, SUB * (I + 1))
    rho = jnp.zeros((1, HEAD_DIM), F32) if I == 0 else b_s[SUB * I - 1:SUB * I, :]
    eI = jnp.exp(b[rows] - rho)
    EI = jnp.exp(jnp.minimum(rho - b, EXP_CAP))
    causal = (lax.broadcasted_iota(jnp.int32, (SUB, CHUNK), 1)
              <= lax.broadcasted_iota(jnp.int32, (SUB, CHUNK), 0) + SUB * I)
    return rows, q[rows] * eI, k * EI, eI, EI, causal


def _hgrn_fwd(u, lb3, layer, gn3, j, name):
    T = u.shape[0]
    D = u.shape[1] // 4
    H = D // HEAD_DIM
    NC = T // CHUNK

    def body(u_ref, lb_ref, gn_ref, y_ref, o_ref, sall_ref, st_s, b_s, lf_s, q_s, k_s):
        n = pl.program_id(0)

        @pl.when(n == 0)
        def _():
            st_s[...] = jnp.zeros_like(st_s)

        lv = _live(CHUNK, n * CHUNK, T)
        heads = range(H)
        cols = [slice(h * HEAD_DIM, (h + 1) * HEAD_DIM) for h in heads]
        vb = []
        for h in heads:
            q, k, v, logf, _ = _hgrn_gates(u_ref, lb_ref, h, D, lv)
            q_s[:, cols[h]] = q
            k_s[:, cols[h]] = k
            lf_s[:, cols[h]] = logf
            vb.append(v.astype(BF16))
        b_s[...] = jnp.dot(_tri(True), lf_s[...], precision=HI, preferred_element_type=F32)
        ops = []
        for h in heads:
            b_h = b_s.at[:, cols[h]]
            b = b_h[...]
            q = q_s[:, cols[h]]
            k = k_s[:, cols[h]]
            blast = b_h[CHUNK - 1:CHUNK, :]
            qh = (q * jnp.exp(b)).astype(BF16)
            kt = (k * jnp.exp(blast - b)).astype(BF16)
            subs = []
            for I in range(CHUNK // SUB):
                _, qI, KI, _, _, causal = _sub_parts(q, k, b, b_h, I)
                subs.append((qI.astype(BF16), KI.astype(BF16), causal))
            ops.append((qh, kt, jnp.exp(blast), subs))
        mm = []
        for h in heads:
            qh, kt, eblast, subs = ops[h]
            st = st_s[h]
            sall_ref[h] = st
            o_inter = _dot_nt(qh, st.astype(BF16))
            st_s[h] = st * eblast + _dot_tn(vb[h], kt)
            mm.append((o_inter, [_dot_nt(qI, KI) for qI, KI, _ in subs]))
        for h in heads:
            o_inter, ps = mm[h]
            p = jnp.concatenate([jnp.where(c, x, 0.0) for x, (_, _, c) in zip(ps, ops[h][3])], axis=0).astype(BF16)
            o = o_inter + jnp.dot(p, vb[h], preferred_element_type=F32)
            o_ref[:, cols[h]] = o
            graw = u_ref[:, 3 * D + h * HEAD_DIM:3 * D + (h + 1) * HEAD_DIM]
            r = lax.rsqrt(jnp.mean(o * o, axis=-1, keepdims=True) + EPS)
            y_ref[:, cols[h]] = (((o * r) * gn_ref[...]) * (graw * _sigmoid(graw))).astype(BF16)

    return pl.pallas_call(
        body, grid=(NC,),
        in_specs=[pl.BlockSpec((CHUNK, 4 * D), lambda n: (n, 0)),
                  pl.BlockSpec((None, 1, D), lambda n: (layer, 0, 0)),
                  pl.BlockSpec((None, 1, HEAD_DIM), lambda n: (j, 0, 0))],
        out_specs=[pl.BlockSpec((CHUNK, D), lambda n: (n, 0)), pl.BlockSpec((CHUNK, D), lambda n: (n, 0)),
                   pl.BlockSpec((None, H, HEAD_DIM, HEAD_DIM), lambda n: (n, 0, 0, 0))],
        out_shape=[SDS((T, D), BF16), SDS((T, D), F32), SDS((NC, H, HEAD_DIM, HEAD_DIM), F32)],
        scratch_shapes=[pltpu.VMEM((H, HEAD_DIM, HEAD_DIM), F32)] + [pltpu.VMEM((CHUNK, D), F32)] * 4,
        name=name, compiler_params=_params("arbitrary"))(u, lb3, gn3)


def _hgrn_bwd(u, o_raw, dy, sall, lb3, layer, gn3, j, name):
    T = u.shape[0]
    D = u.shape[1] // 4
    H = D // HEAD_DIM
    NC = T // CHUNK

    def body(u_ref, o_ref, dy_ref, sall_ref, lb_ref, gn_ref, du_ref, dlb_ref, dgn_ref, dst_s, b_s, lf_s, q_s, k_s, db_s, dk_s):
        step = pl.program_id(0)
        n = NC - 1 - step

        @pl.when(step == 0)
        def _():
            dst_s[...] = jnp.zeros_like(dst_s)
            dlb_ref[...] = jnp.zeros_like(dlb_ref)
            dgn_ref[...] = jnp.zeros_like(dgn_ref)

        lv = _live(CHUNK, n * CHUNK, T)
        last_row = (_row_ids((CHUNK, 1), 0) == CHUNK - 1).astype(F32)
        gn = gn_ref[...]
        heads = range(H)
        cols = [slice(h * HEAD_DIM, (h + 1) * HEAD_DIM) for h in heads]
        vb, dob = [], []
        dgn = jnp.zeros((1, HEAD_DIM), F32)
        for h in heads:
            q, k, v, logf, _ = _hgrn_gates(u_ref, lb_ref, h, D, lv)
            q_s[:, cols[h]] = q
            k_s[:, cols[h]] = k
            lf_s[:, cols[h]] = logf
            vb.append(v.astype(BF16))
            graw = u_ref[:, 3 * D + h * HEAD_DIM:3 * D + (h + 1) * HEAD_DIM]
            gsig = _sigmoid(graw)
            o = o_ref[:, cols[h]]
            r = lax.rsqrt(jnp.mean(o * o, axis=-1, keepdims=True) + EPS)
            xh = o * r
            dyv = dy_ref[:, cols[h]]
            dsg = dyv * (graw * gsig)
            dgn = dgn + jnp.sum(dsg * xh, axis=0, keepdims=True)
            dxh = dsg * gn
            do = r * (dxh - xh * jnp.mean(dxh * xh, axis=-1, keepdims=True))
            dob.append(do.astype(BF16))
            dgraw = dyv * xh * gn * (gsig * (1.0 + graw * (1.0 - gsig)))
            du_ref[:, 3 * D + h * HEAD_DIM:3 * D + (h + 1) * HEAD_DIM] = (dgraw * lv).astype(BF16)
        dgn_ref[...] += dgn
        b_s[...] = jnp.dot(_tri(True), lf_s[...], precision=HI, preferred_element_type=F32)
        ops = []
        for h in heads:
            b_h = b_s.at[:, cols[h]]
            b = b_h[...]
            q = q_s[:, cols[h]]
            k = k_s[:, cols[h]]
            blast = b_h[CHUNK - 1:CHUNK, :]
            eb = jnp.exp(b)
            ekb = jnp.exp(blast - b)
            subs = []
            for I in range(CHUNK // SUB):
                rows, qI, KI, eI, EI, causal = _sub_parts(q, k, b, b_h, I)
                subs.append((rows, qI.astype(BF16), KI.astype(BF16), eI, EI, causal))
            ops.append((eb, ekb, jnp.exp(blast), (q * eb).astype(BF16), (k * ekb).astype(BF16), subs))
        mm = []
        for h in heads:
            eb, ekb, eblast, qhb, ktb, subs = ops[h]
            st = sall_ref[h]
            dst = dst_s[h]
            dstb = dst.astype(BF16)
            dv = _dot_nt(ktb, dstb)
            dqh = jnp.dot(dob[h], st.astype(BF16), preferred_element_type=F32)
            dkt = jnp.dot(vb[h], dstb, preferred_element_type=F32)
            dblast = jnp.sum(dst * st, axis=0, keepdims=True) * eblast
            dst_s[h] = dst * eblast + _dot_tn(dob[h], qhb)
            dp_full = _dot_nt(dob[h], vb[h])
            ps = [_dot_nt(qIb, KIb) for _, qIb, KIb, _, _, _ in subs]
            mm.append((dv, dqh, dkt, dblast, dp_full, ps))
        for h in heads:
            eb, ekb, eblast, qhb, ktb, subs = ops[h]
            dv, dqh, dkt, dblast, dp_full, ps = mm[h]
            p = jnp.concatenate([jnp.where(sub[5], x, 0.0) for x, sub in zip(ps, subs)], axis=0).astype(BF16)
            dv = dv + _dot_tn(p, dob[h])
            du_ref[:, 2 * D + h * HEAD_DIM:2 * D + (h + 1) * HEAD_DIM] = (dv * lv).astype(BF16)
            dq = dqh * eb
            db = dqh * qhb.astype(F32)
            tmp = dkt * ktb.astype(F32)
            dk = dkt * ekb
            db = db - tmp
            dblast = dblast + jnp.sum(tmp, axis=0, keepdims=True)
            dq_parts, db_parts = [], []
            for rows, qIb, KIb, eI, EI, causal in subs:
                dp = jnp.where(causal, dp_full[rows], 0.0).astype(BF16)
                dqI = jnp.dot(dp, KIb, preferred_element_type=F32)
                dKI = _dot_tn(dp, qIb)
                dq_parts.append(dqI * eI)
                db_parts.append(dqI * qIb.astype(F32))
                dk = dk + dKI * EI
                db = db - dKI * KIb.astype(F32)
            dq = dq + jnp.concatenate(dq_parts, axis=0)
            db_s[:, cols[h]] = db + jnp.concatenate(db_parts, axis=0) + last_row * dblast
            dk_s[:, cols[h]] = dk
            qraw = u_ref[:, cols[h]]
            qsig = _sigmoid(qraw)
            du_ref[:, cols[h]] = (dq * (qsig * (1.0 + qraw * (1.0 - qsig))) * lv).astype(BF16)
        lf_s[...] = jnp.dot(_tri(False), db_s[...], precision=HI, preferred_element_type=F32)
        for h in heads:
            fraw = u_ref[:, D + h * HEAD_DIM:D + (h + 1) * HEAD_DIM]
            lbv = lb_ref[:, cols[h]]
            sig = _sigmoid(fraw)
            forget = lbv + (1.0 - lbv) * sig
            dforget = (lf_s[:, cols[h]] / forget - dk_s[:, cols[h]]) * lv
            dlb_ref[:, cols[h]] += jnp.sum(dforget * (1.0 - sig), axis=0, keepdims=True)
            du_ref[:, D + h * HEAD_DIM:D + (h + 1) * HEAD_DIM] = (dforget * (1.0 - lbv) * sig * (1.0 - sig)).astype(BF16)

    rev = lambda s: (NC - 1 - s, 0)
    return pl.pallas_call(
        body, grid=(NC,),
        in_specs=[pl.BlockSpec((CHUNK, 4 * D), rev), pl.BlockSpec((CHUNK, D), rev), pl.BlockSpec((CHUNK, D), rev),
                  pl.BlockSpec((None, H, HEAD_DIM, HEAD_DIM), lambda s: (NC - 1 - s, 0, 0, 0)),
                  pl.BlockSpec((None, 1, D), lambda s: (layer, 0, 0)),
                  pl.BlockSpec((None, 1, HEAD_DIM), lambda s: (j, 0, 0))],
        out_specs=[pl.BlockSpec((CHUNK, 4 * D), rev), pl.BlockSpec((1, D), lambda s: (0, 0)),
                   pl.BlockSpec((1, HEAD_DIM), lambda s: (0, 0))],
        out_shape=[SDS((T, 4 * D), BF16), SDS((1, D), F32), SDS((1, HEAD_DIM), F32)],
        scratch_shapes=[pltpu.VMEM((H, HEAD_DIM, HEAD_DIM), F32)] + [pltpu.VMEM((CHUNK, D), F32)] * 6,
        name=name, compiler_params=_params("arbitrary"))(u, o_raw, dy, sall, lb3, gn3)


def _softmax_layers(p_ref, n_layers):
    rows = [p_ref[l:l + 1, :] for l in range(n_layers)]
    m = functools.reduce(jnp.maximum, rows)
    e = [jnp.exp(x - m) for x in rows]
    tot = functools.reduce(lambda a, b: a + b, e)
    return [x / tot for x in e]


def _lb_fwd(p):
    n_layers, D = p.shape

    def body(p_ref, o_ref):
        s = _softmax_layers(p_ref, n_layers)
        acc = jnp.zeros((1, D), F32)
        o_ref[0:1, :] = acc
        for l in range(1, n_layers):
            acc = acc + s[l]
            o_ref[l:l + 1, :] = acc

    return pl.pallas_call(body, out_shape=SDS(p.shape, F32), name="lb_fwd")(p)


def _lb_bwd(p, dlb):
    n_layers, D = p.shape

    def body(p_ref, d_ref, o_ref):
        s = _softmax_layers(p_ref, n_layers)
        ds = [jnp.zeros((1, D), F32)] * n_layers
        acc = jnp.zeros((1, D), F32)
        for l in range(n_layers - 1, 0, -1):
            acc = acc + d_ref[l:l + 1, :]
            ds[l] = acc
        dot = functools.reduce(lambda a, b: a + b, [s[l] * ds[l] for l in range(n_layers)])
        for l in range(n_layers):
            o_ref[l:l + 1, :] = s[l] * (ds[l] - dot)

    return pl.pallas_call(body, out_shape=SDS(p.shape, F32), name="lb_bwd")(p, dlb)


def _adamw(w, g, m, v, name):
    R, C = w.shape
    tr = _tile(R, 256, 8) if R % 8 == 0 else R

    def body(w_ref, g_ref, m_ref, v_ref, d_ref, mo_ref, vo_ref):
        g_ = g_ref[...]
        m_ = ADAM_B1 * m_ref[...] + (1.0 - ADAM_B1) * g_
        v_ = ADAM_B2 * v_ref[...] + (1.0 - ADAM_B2) * (g_ * g_)
        mh = m_ / (1.0 - ADAM_B1 ** ADAM_STEP)
        vh = v_ / (1.0 - ADAM_B2 ** ADAM_STEP)
        d_ref[...] = -ADAM_LR * (mh / (jnp.sqrt(vh) + ADAM_EPS) + ADAM_WD * w_ref[...])
        mo_ref[...] = m_
        vo_ref[...] = v_

    blk = pl.BlockSpec((tr, C), lambda i: (i, 0))
    return pl.pallas_call(
        body, grid=(R // tr,), in_specs=[blk] * 4, out_specs=[blk] * 3, out_shape=[SDS((R, C), F32)] * 3,
        name=name, compiler_params=_params("parallel"))(w, g, m, v)


def _adamw_layer(w3, m3, v3, g2, layer, outs, name):
    L, R, C = w3.shape
    tr = _tile(R, 256, 8)
    if outs is None:
        outs = tuple(lax.empty(w3.shape, F32) for _ in range(4))

    def body(w_ref, m_ref, v_ref, g_ref, a0, a1, a2, a3, go_ref, d_ref, mo_ref, vo_ref):
        del a0, a1, a2, a3
        g_ = g_ref[...]
        m_ = ADAM_B1 * m_ref[...] + (1.0 - ADAM_B1) * g_
        v_ = ADAM_B2 * v_ref[...] + (1.0 - ADAM_B2) * (g_ * g_)
        mh = m_ / (1.0 - ADAM_B1 ** ADAM_STEP)
        vh = v_ / (1.0 - ADAM_B2 ** ADAM_STEP)
        go_ref[...] = g_
        d_ref[...] = -ADAM_LR * (mh / (jnp.sqrt(vh) + ADAM_EPS) + ADAM_WD * w_ref[...])
        mo_ref[...] = m_
        vo_ref[...] = v_

    lay = pl.BlockSpec((None, tr, C), lambda i: (layer, i, 0))
    return pl.pallas_call(
        body, grid=(R // tr,), in_specs=[lay] * 3 + [pl.BlockSpec((tr, C), lambda i: (i, 0))] + [ANY_SPEC] * 4,
        out_specs=[lay] * 4, out_shape=[SDS(w3.shape, F32)] * 4, input_output_aliases={4: 0, 5: 1, 6: 2, 7: 3},
        name=name, compiler_params=_params("parallel"))(w3, m3, v3, g2, *outs)


SEM_SPEC = pl.BlockSpec(memory_space=pltpu.SEMAPHORE)
HBM_SPEC = pl.BlockSpec(memory_space=pltpu.HBM)
EFFECT = pltpu.SideEffectType.DATAFLOW_SIDE_EFFECTING
N_DEV = 2 * N_CHIPS


def _position():
    x, y, c = lax.axis_index("x"), lax.axis_index("y"), lax.axis_index("c")
    chips = [(1 - x, y), (x, 1 - y), (1 - x, 1 - y)]
    return x, y, c, chips


def _split_start(name, plan, bufs, n_sems, deps=()):
    n = len(bufs)

    def body(*refs):
        send_sems, recv_sems = refs[n + len(deps)], refs[n + len(deps) + 1]
        sends, _ = plan(refs[:n], send_sems, recv_sems)
        for kw in sends:
            pltpu.make_async_remote_copy(**kw).start()
        refs[-1][...] = jnp.zeros_like(refs[-1])

    out = pl.pallas_call(
        body, name=name,
        out_shape=(pltpu.SemaphoreType.DMA((n_sems,)), pltpu.SemaphoreType.DMA((n_sems,)),
                   *[pltpu.HBM(b.shape, b.dtype) for b in bufs], SDS((8, 128), F32)),
        in_specs=[HBM_SPEC] * n + [ANY_SPEC] * len(deps),
        out_specs=(SEM_SPEC, SEM_SPEC, *[HBM_SPEC] * n, pl.BlockSpec(memory_space=pltpu.VMEM)),
        input_output_aliases={i: 2 + i for i in range(n)},
        compiler_params=pltpu.CompilerParams(has_side_effects=EFFECT),
    )(*[pltpu.with_memory_space_constraint(b, pltpu.HBM) for b in bufs], *deps)
    return out[0], out[1], list(out[2:2 + n]), out[-1]


def _split_wait(name, plan, send_sems, recv_sems, bufs, after=()):
    n = len(bufs)

    def body(*refs):
        sends, recvs = plan(refs[:n], refs[n], refs[n + 1])
        for kw in sends:
            pltpu.make_async_remote_copy(**kw).wait_send()
        for kw in recvs:
            pltpu.make_async_remote_copy(**kw).wait_recv()

    out = pl.pallas_call(
        body, name=name, out_shape=tuple(pltpu.HBM(b.shape, b.dtype) for b in bufs),
        in_specs=[HBM_SPEC] * n + [SEM_SPEC, SEM_SPEC] + [ANY_SPEC] * len(after),
        out_specs=tuple([HBM_SPEC] * n), input_output_aliases={i: i for i in range(n)},
        compiler_params=pltpu.CompilerParams(has_side_effects=EFFECT),
    )(*bufs, send_sems, recv_sems, *after)
    return list(out)


def _region(kind, ref, chip):
    K, N = ref.shape
    if kind == "col":
        return ref.at[:, pl.ds(chip * (N // N_CHIPS), N // N_CHIPS)]
    return ref.at[pl.ds(chip * (K // N_CHIPS), K // N_CHIPS), :]


def _gather_plan(kinds):
    def plan(refs, send_sems, recv_sems):
        x, y, c, chips = _position()
        sends, recvs = [], []
        for f, (ref, kind) in enumerate(zip(refs, kinds)):
            own = _region(kind, ref, 2 * x + y)
            for k, chip in enumerate(chips):
                sem = dict(send_sem=send_sems.at[3 * f + k], recv_sem=recv_sems.at[3 * f + k],
                           device_id=(*chip, c), device_id_type=MESH)
                theirs = _region(kind, ref, 2 * chip[0] + chip[1])
                sends.append(dict(src_ref=own, dst_ref=own, **sem))
                recvs.append(dict(src_ref=theirs, dst_ref=theirs, **sem))
        return sends, recvs
    return plan


def _reduce_plan(refs, send_sems, recv_sems):
    x, y, c, _ = _position()
    me = 4 * x + 2 * y + c
    sends, recvs = [], []
    for f in range(len(refs) // 2):
        acc, land = refs[2 * f], refs[2 * f + 1]
        for d in range(1, N_DEV):
            t = (me + d) % N_DEV
            to = dict(device_id=(t // 4, (t // 2) % 2, t % 2), device_id_type=MESH)
            slot = N_DEV - 1 - d
            sends.append(dict(src_ref=acc.at[t % 2, t // 2], dst_ref=land.at[slot], send_sem=send_sems.at[7 * f + d - 1],
                              recv_sem=recv_sems.at[7 * f + slot], **to))
            recvs.append(dict(src_ref=land.at[d - 1], dst_ref=land.at[d - 1], send_sem=send_sems.at[7 * f + d - 1],
                              recv_sem=recv_sems.at[7 * f + d - 1], **to))
    return sends, recvs


def _swap_plan(refs, send_sems, recv_sems):
    x, y, c, _ = _position()
    sends, recvs = [], []
    for f, g in enumerate(refs):
        sem = dict(send_sem=send_sems.at[f], recv_sem=recv_sems.at[f], device_id=(x, y, 1 - c), device_id_type=MESH)
        sends.append(dict(src_ref=g.at[c], dst_ref=g.at[c], **sem))
        recvs.append(dict(src_ref=g.at[1 - c], dst_ref=g.at[1 - c], **sem))
    return sends, recvs


def _sum_pieces(ids2, acc, land, name):
    _, _, nr, nc = acc.shape
    tr = _tile(nr, 256, 16)

    def body(ids_ref, own_ref, land_ref, o_ref):
        del ids_ref
        s = own_ref[...].astype(F32)
        for k in range(N_DEV - 1):
            s = s + land_ref[k].astype(F32)
        o_ref[...] = s

    return pl.pallas_call(
        body,
        grid_spec=pltpu.PrefetchScalarGridSpec(
            num_scalar_prefetch=1, grid=(nr // tr,),
            in_specs=[pl.BlockSpec((None, None, tr, nc), lambda i, ids: (ids[0], ids[1], i, 0)),
                      pl.BlockSpec((N_DEV - 1, tr, nc), lambda i, ids: (0, i, 0))],
            out_specs=pl.BlockSpec((None, tr, nc), lambda i, ids: (ids[0], i, 0))),
        out_shape=SDS((2, nr, nc), F32), name=name, compiler_params=_params("parallel"))(ids2, acc, land)


def _all_sum(block):
    m_per, n = block.shape

    def body(x_ref, all_ref, sum_ref, send_sems, recv_sems, local_sem):
        x, y, c, chips = _position()
        me, sibling = (x, y, c), (x, y, 1 - c)

        def rows(px, py, pc):
            return all_ref.at[pl.ds((4 * px + 2 * py + pc) * m_per, m_per), :]

        def copy(k, blk, to, src=None):
            return pltpu.make_async_remote_copy(
                src_ref=rows(*blk) if src is None else src, dst_ref=rows(*blk), send_sem=send_sems.at[k],
                recv_sem=recv_sems.at[k], device_id=to, device_id_type=MESH)

        mine = pltpu.make_async_copy(x_ref, rows(*me), local_sem)
        mine.start()
        first = [copy(0, me, sibling, src=x_ref)]
        first += [copy(1 + k, me, (*chip, c), src=x_ref) for k, chip in enumerate(chips)]
        for cp in first:
            cp.start()
        passed = [copy(4 + k, (*chip, c), sibling) for k, chip in enumerate(chips)]
        for k, chip in enumerate(chips):
            copy(1 + k, (*chip, c), me).wait_recv()
            passed[k].start()
        copy(0, sibling, me).wait_recv()
        for k, chip in enumerate(chips):
            copy(4 + k, (*chip, 1 - c), me).wait_recv()
        for cp in first + passed:
            cp.wait_send()
        mine.wait()
        acc = all_ref[0:m_per, :]
        for d in range(1, N_DEV):
            acc = acc + all_ref[d * m_per:(d + 1) * m_per, :]
        sum_ref[...] = acc

    vm = pl.BlockSpec(memory_space=pltpu.VMEM)
    return pl.pallas_call(
        body, in_specs=[vm], out_specs=[vm, vm], out_shape=[SDS((N_DEV * m_per, n), F32), SDS((m_per, n), F32)],
        scratch_shapes=[pltpu.SemaphoreType.DMA((7,)), pltpu.SemaphoreType.DMA((7,)), pltpu.SemaphoreType.DMA],
        name="all_sum", compiler_params=_params())(block)[1]


BIG = {"ev_w_in": "col", "ev_w_out": "row", "od_w_in": "col", "od_w_out": "row", "mlp_w1": "col", "mlp_w2": "row"}
WEIGHTS = ("meta_tokens", "mix_norm_g", "mlp_norm_g", "final_norm_g", "ev_w_in", "ev_conv_w", "ev_conv_b", "ev_ln_g",
           "ev_ln_b", "ev_pool_w", "ev_pool_b", "ev_pool_scale", "ev_w_out", "od_w_in", "od_gnorm_g", "od_w_out",
           "lb_param", "mlp_w1", "mlp_w2")
PACK_UNIT = 1024


def _mixer_names(layer):
    return ("ev_w_in", "ev_w_out") if layer % 2 == 0 else ("od_w_in", "od_w_out")


def _pack(arrays):
    flat = []
    for a in arrays:
        a = a.reshape(-1)
        flat.append(jnp.pad(a, (0, (-a.shape[0]) % PACK_UNIT)))
    return jnp.concatenate(flat).reshape(-1, 128)


def _unpack(packed, shapes):
    flat = packed.reshape(-1)
    out, off = [], 0
    for s in shapes:
        size = 1
        for d in s:
            size *= d
        out.append(flat[off:off + size].reshape(s))
        off += size + (-size) % PACK_UNIT
    return out


def _local_step(x2, target, P, weights, boundary, first_deps=()):
    D = x2.shape[1]
    n_layers = P["mix_norm_g"].shape[0]
    h = jnp.concatenate([jnp.zeros((PAD, D), F32), P["meta_full"], x2], axis=0)
    mix_g = P["mix_norm_g"].reshape(n_layers, 1, D)
    mlp_g = P["mlp_norm_g"].reshape(n_layers, 1, D)
    vec = lambda a: a.reshape(a.shape[0], 1, -1)
    cb3, lg3, lnb3, ps3 = vec(P["ev_conv_b"]), vec(P["ev_ln_g"]), vec(P["ev_ln_b"]), vec(P["ev_pool_scale"])
    pb3 = vec(P["ev_pool_b"])
    gn3 = vec(P["od_gnorm_g"])
    lb_all = _lb_fwd(P["lb_param"])
    lb3 = lb_all.reshape(n_layers, 1, D)
    even = (cb3, lg3, lnb3, P["ev_pool_w"], pb3, ps3)

    saved = []
    deps = tuple(first_deps)
    for layer in range(n_layers):
        j = layer // 2
        w_in, w_out = _mixer_names(layer)
        W = weights(layer, (h,))
        s = {"h": h, "W": W}
        s["n"] = _rms_fwd(h, mix_g, layer, f"mix_norm_{layer}", deps=deps)
        deps = ()
        s["u"] = _mm_nn(s["n"], W[w_in], 0, f"mix_in_{layer}")
        if layer % 2 == 0:
            s["y"] = _even_fwd(s["u"], P["conv_w_full"], *even, j, f"even_fwd_{layer}")
        else:
            s["y"], s["o"], s["sall"] = _hgrn_fwd(s["u"], lb3, layer, gn3, j, f"hgrn_fwd_{layer}")
        h = _mm_nn(s["y"], W[w_out], 0, f"mix_out_{layer}", res=h)
        s["h1"] = h
        s["n2"] = _rms_fwd(h, mlp_g, layer, f"mlp_norm_{layer}")
        s["act"] = _mm_nn(s["n2"], W["mlp_w1"], 0, f"mlp_up_{layer}", relu2=True)
        h = _mm_nn(s["act"], W["mlp_w2"], 0, f"mlp_down_{layer}", res=h)
        saved.append(s)

    dh, dhb, dg_final, loss = _final(h, P["final_norm_g"].reshape(1, D), target)

    small = {"final_norm_g": dg_final}
    per_layer = {k: [None] * n_layers for k in ("mix_norm_g", "mlp_norm_g", "lb")}
    per_pair = {k: [None] * (n_layers // 2) for k in
                ("ev_conv_w", "ev_conv_b", "ev_ln_g", "ev_ln_b", "ev_pool_w", "ev_pool_b", "ev_pool_scale", "od_gnorm_g")}
    for layer in reversed(range(n_layers)):
        j = layer // 2
        s = saved[layer]
        W = s["W"]
        w_in, w_out = _mixer_names(layer)
        dz = _mm_nt(dhb, W["mlp_w2"], 0, f"d_act_{layer}", act=s["act"], deps=deps)
        dw2 = _mm_tn(s["act"], dhb, "row", f"dw2_{layer}")
        dn2 = _mm_nt(dz, W["mlp_w1"], 0, f"d_n2_{layer}")
        dw1 = _mm_tn(s["n2"], dz, "col", f"dw1_{layer}")
        dh, dhb, per_layer["mlp_norm_g"][layer] = _rms_bwd(s["h1"], mlp_g, layer, dn2, dh, f"mlp_norm_bwd_{layer}")
        deps = boundary(f"mlp{layer}", {("mlp_w1", layer): dw1, ("mlp_w2", layer): dw2}, (dhb,))
        dy = _mm_nt(dhb, W[w_out], 0, f"d_y_{layer}", deps=deps)
        dwout = _mm_tn(s["y"], dhb, "row", f"dwout_{layer}")
        if layer % 2 == 0:
            du, dcw, dcb, dlg, dlnb, dpw, dpb, dps = _even_bwd(s["u"], dy, P["conv_w_full"], *even, j, f"even_bwd_{layer}")
            for k, val in (("ev_conv_w", dcw), ("ev_conv_b", dcb), ("ev_ln_g", dlg), ("ev_ln_b", dlnb),
                           ("ev_pool_w", dpw), ("ev_pool_b", dpb), ("ev_pool_scale", dps)):
                per_pair[k][j] = val
        else:
            du, per_layer["lb"][layer], per_pair["od_gnorm_g"][j] = _hgrn_bwd(
                s["u"], s["o"], dy, s["sall"], lb3, layer, gn3, j, f"hgrn_bwd_{layer}")
        dn = _mm_nt(du, W[w_in], 0, f"d_n_{layer}")
        dwin = _mm_tn(s["n"], du, "col", f"dwin_{layer}")
        dh, dhb, per_layer["mix_norm_g"][layer] = _rms_bwd(s["h"], mix_g, layer, dn, dh, f"mix_norm_bwd_{layer}")
        deps = boundary(f"mix{layer}", {(w_in, j): dwin, (w_out, j): dwout}, (dhb,))

    small["mix_norm_g"] = jnp.concatenate(per_layer["mix_norm_g"], axis=0)
    small["mlp_norm_g"] = jnp.concatenate(per_layer["mlp_norm_g"], axis=0)
    dlb_all = jnp.concatenate([jnp.zeros((1, D), F32) if g is None else g for g in per_layer["lb"]], axis=0)
    small["lb_param"] = _lb_bwd(P["lb_param"], dlb_all)
    for k, vals in per_pair.items():
        small[k] = jnp.stack(vals, axis=0)
    small["meta_tokens"] = dh[PAD:LEAD]
    return loss, dh, small


def kernel(x, meta_tokens, mix_norm_g, mlp_norm_g, final_norm_g, ev_w_in, ev_conv_w, ev_conv_b, ev_ln_g, ev_ln_b, ev_pool_w, ev_pool_b, ev_pool_scale, ev_w_out, od_w_in, od_gnorm_g, od_w_out, lb_param, mlp_w1, mlp_w2, loss_target, m_meta_tokens, m_mix_norm_g, m_mlp_norm_g, m_final_norm_g, m_ev_w_in, m_ev_conv_w, m_ev_conv_b, m_ev_ln_g, m_ev_ln_b, m_ev_pool_w, m_ev_pool_b, m_ev_pool_scale, m_ev_w_out, m_od_w_in, m_od_gnorm_g, m_od_w_out, m_lb_param, m_mlp_w1, m_mlp_w2, v_meta_tokens, v_mix_norm_g, v_mlp_norm_g, v_final_norm_g, v_ev_w_in, v_ev_conv_w, v_ev_conv_b, v_ev_ln_g, v_ev_ln_b, v_ev_pool_w, v_ev_pool_b, v_ev_pool_scale, v_ev_w_out, v_od_w_in, v_od_gnorm_g, v_od_w_out, v_lb_param, v_mlp_w1, v_mlp_w2):
    given = dict(locals())
    w = {n: given[n] for n in WEIGHTS}
    m = {n: given["m_" + n] for n in WEIGHTS}
    v = {n: given["v_" + n] for n in WEIGHTS}
    n_layers = mix_norm_g.shape[0]
    core = lax.axis_index("c").astype(jnp.int32)
    chip = (2 * lax.axis_index("x") + lax.axis_index("y")).astype(jnp.int32)
    chip1 = chip.reshape(1)
    ids2 = jnp.stack([core, chip])

    conv_pad = jnp.pad(ev_conv_w, ((0, 0), (0, CONV_ROWS - CONV_WIDTH), (0, 0)))
    gathers, token = [], ()
    for layer in range(n_layers):
        names = list(_mixer_names(layer)) + ["mlp_w1", "mlp_w2"]
        index = [layer // 2, layer // 2, layer, layer]
        kinds = [BIG[n] for n in names]
        bufs = [_cast_place(w[n], i, BIG[n], chip1, BF16, f"place_{n}_{i}") for n, i in zip(names, index)]
        if layer == 0:
            bufs.append(_cast_place(meta_tokens[None], 0, "col", chip1, F32, "place_meta"))
            bufs.append(_cast_place(conv_pad.reshape(1, -1, conv_pad.shape[2]), 0, "col", chip1, F32, "place_conv_w"))
            kinds += ["col", "col"]
        plan = _gather_plan(kinds)
        ss, rs, bufs, tok = _split_start(f"gather_start_{layer}", plan, bufs, 3 * len(bufs), deps=token)
        token = (tok,)
        gathers.append((plan, ss, rs, bufs, names))

    landed = {}

    def weights(layer, after):
        if layer not in landed:
            plan, ss, rs, bufs, names = gathers[layer]
            landed[layer] = (_split_wait(f"gather_wait_{layer}", plan, ss, rs, bufs, after), names)
        fulls, names = landed[layer]
        return {n: f[None] for n, f in zip(names, fulls)}

    weights(0, ())
    P = {n: w[n] for n in ("mix_norm_g", "mlp_norm_g", "final_norm_g", "ev_conv_b", "ev_ln_g", "ev_ln_b", "ev_pool_w",
                           "ev_pool_b", "ev_pool_scale", "od_gnorm_g", "lb_param")}
    P["meta_full"] = landed[0][0][4]
    P["conv_w_full"] = landed[0][0][5].reshape(ev_conv_w.shape[0], CONV_ROWS, -1)

    pending, outs = [], {n: None for n in BIG}

    def advance(after):
        tokens, still = [], []
        for st in pending:
            if st["phase"] == 1:
                bufs = _split_wait(f"reduce_wait_{st['tag']}", _reduce_plan, st["ss"], st["rs"], st["bufs"], after)
                halves = [_sum_pieces(ids2, bufs[2 * f], bufs[2 * f + 1], f"sum_{st['tag']}_{f}") for f in range(len(bufs) // 2)]
                ss, rs, halves, tok = _split_start(f"swap_start_{st['tag']}", _swap_plan, halves, len(halves))
                tokens.append(tok)
                still.append(dict(st, phase=2, ss=ss, rs=rs, bufs=halves))
            else:
                grads = _split_wait(f"swap_wait_{st['tag']}", _swap_plan, st["ss"], st["rs"], st["bufs"], after)
                for (n, i), g in zip(st["keys"], grads):
                    outs[n] = _adamw_layer(w[n], m[n], v[n], g.reshape(w[n].shape[1:]), i, outs[n], f"adamw_{n}_{i}")
        pending[:] = still
        return tokens

    def boundary(tag, grads, after):
        tokens = advance(after)
        bufs = []
        for acc in grads.values():
            bufs += [acc, lax.empty((N_DEV - 1,) + acc.shape[2:], BF16)]
        ss, rs, bufs, tok = _split_start(f"reduce_start_{tag}", _reduce_plan, bufs, 7 * len(grads))
        pending.append(dict(phase=1, tag=tag, keys=list(grads), ss=ss, rs=rs, bufs=bufs))
        return tuple(tokens + [tok])

    loss, dh, small = _local_step(x[0], loss_target[0], P, weights, boundary, first_deps=token)

    order = [n for n in WEIGHTS if n not in BIG]
    block = _pack([small[n] for n in order] + [loss])
    advance((block,))
    packed = _all_sum(block)
    advance((packed,))
    total = _unpack(packed, [small[n].shape for n in order] + [loss.shape])
    loss_sum = total[-1][0, 0]
    gsmall = dict(zip(order, total[:-1]))
    gsmall["meta_tokens"] = lax.dynamic_slice_in_dim(gsmall["meta_tokens"], chip * meta_tokens.shape[1], meta_tokens.shape[1], 1)
    gsmall["ev_conv_w"] = lax.dynamic_slice_in_dim(gsmall["ev_conv_w"][:, :CONV_WIDTH], chip * ev_conv_w.shape[2], ev_conv_w.shape[2], 2)

    g_out, d_out, m_out, v_out = {}, {}, {}, {}
    for n in WEIGHTS:
        if n in BIG:
            g_out[n], d_out[n], m_out[n], v_out[n] = outs[n]
            continue
        shape = w[n].shape
        g = gsmall[n].reshape(shape)
        cols = shape[-1] if len(shape) > 1 else 128
        two = lambda a: a.reshape(-1, cols)
        d_, m_, v_ = _adamw(two(w[n]), two(g), two(m[n]), two(v[n]), f"adamw_{n}")
        g_out[n], d_out[n], m_out[n], v_out[n] = g, d_.reshape(shape), m_.reshape(shape), v_.reshape(shape)

    grad_x = dh[LEAD:][None]
    return (loss_sum, grad_x, *[g_out[n] for n in WEIGHTS], *[d_out[n] for n in WEIGHTS],
            *[m_out[n] for n in WEIGHTS], *[v_out[n] for n in WEIGHTS])
```

```python
import functools

import jax
import jax.numpy as jnp
from jax import lax
from jax.experimental import pallas as pl
from jax.experimental.pallas import tpu as pltpu

F32 = jnp.float32
BF16 = jnp.bfloat16
SDS = jax.ShapeDtypeStruct
MESH = pl.DeviceIdType.MESH
ANY_SPEC = pl.BlockSpec(memory_space=pl.ANY)

N_META = 16
CHUNK = 64
LEAD = CHUNK
PAD = LEAD - N_META
CONV_WIDTH = 31
CONV_ROWS = 32
POOL_WINDOWS = (2, 4, 8, 16)
HEAD_DIM = 128
SUB = 16
EXP_CAP = 80.0
EPS = 1e-6
ADAM_LR = 0.001
ADAM_B1 = 0.9
ADAM_B2 = 0.999
ADAM_EPS = 1e-08
ADAM_WD = 0.01
ADAM_STEP = 10
N_CHIPS = 4
VMEM_LIMIT = 52 << 20
MM_VMEM_BUDGET = 44 << 20


def _params(*sem):
    return pltpu.CompilerParams(dimension_semantics=sem if sem else None, vmem_limit_bytes=VMEM_LIMIT)


def _tile(n, target, unit=CHUNK):
    best = None
    for t in range(unit, min(n, target) + 1, unit):
        if n % t == 0:
            best = t
    assert best is not None, (n, target, unit)
    return best


def _ctile(n, target=512):
    for t in (512, 384, 256, 128):
        if t <= target and n % t == 0:
            return t
    raise ValueError(n)


def _mm_tiles(M, N, per_row, per_col, per_elem):
    best = None
    for tn in (512, 384, 256, 128):
        if N % tn:
            continue
        for tm in sorted((d for d in range(16, M + 1, 16) if M % d == 0), reverse=True):
            if 2 * (tm * per_row + tn * per_col + tm * tn * per_elem) <= MM_VMEM_BUDGET:
                if best is None or tm * tn > best[0] * best[1]:
                    best = (tm, tn)
                break
    assert best is not None, (M, N)
    return best


def _sigmoid(x):
    return 1.0 / (1.0 + jnp.exp(-x))


def _row_ids(shape, base):
    return lax.broadcasted_iota(jnp.int32, shape, 0) + base


def _cast_place(w3, layer, kind, chip1, dtype, name):
    _, ks, ns = w3.shape
    tr = _tile(ks, 512, 16)
    full = (ks, ns * N_CHIPS) if kind == "col" else (ks * N_CHIPS, ns)

    def body(chip_ref, w_ref, o_ref):
        del chip_ref
        o_ref[...] = w_ref[...].astype(dtype)

    omap = (lambda i, chip: (i, chip[0])) if kind == "col" else (lambda i, chip: (chip[0] * (ks // tr) + i, 0))
    return pl.pallas_call(
        body,
        grid_spec=pltpu.PrefetchScalarGridSpec(
            num_scalar_prefetch=1, grid=(ks // tr,),
            in_specs=[pl.BlockSpec((None, tr, ns), lambda i, chip: (layer, i, 0))],
            out_specs=pl.BlockSpec((tr, ns), omap)),
        out_shape=SDS(full, dtype), name=name, compiler_params=_params("parallel"))(chip1, w3)


def _rms_fwd(h, g3, layer, name, deps=()):
    T, D = h.shape
    tm = _tile(T, 832)

    def body(h_ref, g_ref, *rest):
        n_ref = rest[-1]
        x = h_ref[...]
        r = lax.rsqrt(jnp.mean(x * x, axis=-1, keepdims=True) + EPS)
        n_ref[...] = ((x * r) * g_ref[...]).astype(BF16)

    return pl.pallas_call(
        body, grid=(T // tm,),
        in_specs=[pl.BlockSpec((tm, D), lambda i: (i, 0)), pl.BlockSpec((None, 1, D), lambda i: (layer, 0, 0))]
        + [ANY_SPEC] * len(deps),
        out_specs=pl.BlockSpec((tm, D), lambda i: (i, 0)), out_shape=SDS((T, D), BF16),
        name=name, compiler_params=_params("parallel"))(h, g3, *deps)


def _rms_bwd(h, g3, layer, dn, dh_in, name):
    T, D = h.shape
    tm = _tile(T, 320)

    def body(h_ref, g_ref, dn_ref, dhi_ref, dh_ref, dhb_ref, dg_ref):
        x = h_ref[...]
        r = lax.rsqrt(jnp.mean(x * x, axis=-1, keepdims=True) + EPS)
        xh = x * r
        dn_ = dn_ref[...]
        dxh = dn_ * g_ref[...]
        dh = dhi_ref[...] + r * (dxh - xh * jnp.mean(dxh * xh, axis=-1, keepdims=True))
        dh_ref[...] = dh
        dhb_ref[...] = dh.astype(BF16)

        @pl.when(pl.program_id(0) == 0)
        def _():
            dg_ref[...] = jnp.zeros_like(dg_ref)

        dg_ref[...] += jnp.sum(dn_ * xh, axis=0, keepdims=True)

    row = pl.BlockSpec((tm, D), lambda i: (i, 0))
    return pl.pallas_call(
        body, grid=(T // tm,),
        in_specs=[row, pl.BlockSpec((None, 1, D), lambda i: (layer, 0, 0)), row, row],
        out_specs=[row, row, pl.BlockSpec((1, D), lambda i: (0, 0))],
        out_shape=[SDS((T, D), F32), SDS((T, D), BF16), SDS((1, D), F32)],
        name=name, compiler_params=_params("arbitrary"))(h, g3, dn, dh_in)


def _final(h, g2, target):
    T, D = h.shape
    tm = CHUNK

    def body(h_ref, g_ref, t_ref, dh_ref, dhb_ref, dg_ref, loss_ref):
        i = pl.program_id(0)
        x = h_ref[...]
        r = lax.rsqrt(jnp.mean(x * x, axis=-1, keepdims=True) + EPS)
        xh = x * r
        g = g_ref[...]
        live = jnp.where(i > 0, 1.0, 0.0).astype(F32)
        e = ((xh * g) - t_ref[...]) * live
        dy = e * (1.0 / D)
        dxh = dy * g
        dh = r * (dxh - xh * jnp.mean(dxh * xh, axis=-1, keepdims=True))
        dh_ref[...] = dh
        dhb_ref[...] = dh.astype(BF16)

        @pl.when(i == 0)
        def _():
            dg_ref[...] = jnp.zeros_like(dg_ref)
            loss_ref[...] = jnp.zeros_like(loss_ref)

        dg_ref[...] += jnp.sum(dy * xh, axis=0, keepdims=True)
        loss_ref[...] += jnp.sum(e * e) * (0.5 / D)

    row = pl.BlockSpec((tm, D), lambda i: (i, 0))
    return pl.pallas_call(
        body, grid=(T // tm,),
        in_specs=[row, pl.BlockSpec((1, D), lambda i: (0, 0)),
                  pl.BlockSpec((tm, D), lambda i: (jnp.maximum(i - 1, 0), 0))],
        out_specs=[row, row, pl.BlockSpec((1, D), lambda i: (0, 0)), pl.BlockSpec((1, 128), lambda i: (0, 0))],
        out_shape=[SDS((T, D), F32), SDS((T, D), BF16), SDS((1, D), F32), SDS((1, 128), F32)],
        name="final_loss", compiler_params=_params("arbitrary"))(h, g2, target)


def _mm_nn(a, w3, layer, name, res=None, relu2=False):
    M, K = a.shape
    N = w3.shape[2]
    tm, tn = _mm_tiles(M, N, 2 * K, 2 * K, (4 if relu2 else 4) + (4 if res is not None else 0))

    def body(*refs):
        acc = jnp.dot(refs[0][...], refs[1][...], preferred_element_type=F32)
        if res is not None:
            acc = acc + refs[2][...]
        if relu2:
            p = jnp.maximum(acc, 0.0)
            refs[-2][...] = (p * p).astype(BF16)
            refs[-1][...] = p.astype(BF16)
        else:
            refs[-1][...] = acc

    in_specs = [pl.BlockSpec((tm, K), lambda i, j: (i, 0)), pl.BlockSpec((None, K, tn), lambda i, j: (layer, 0, j))]
    args = [a, w3]
    tile = pl.BlockSpec((tm, tn), lambda i, j: (i, j))
    if res is not None:
        in_specs.append(tile)
        args.append(res)
    return pl.pallas_call(
        body, grid=(M // tm, N // tn), in_specs=in_specs, out_specs=[tile, tile] if relu2 else tile,
        out_shape=[SDS((M, N), BF16)] * 2 if relu2 else SDS((M, N), F32),
        name=name, compiler_params=_params("parallel", "parallel"))(*args)


def _mm_nt(dy, w3, layer, name, relu=None, deps=()):
    M, N = dy.shape
    K = w3.shape[1]
    tm, tk = _mm_tiles(M, K, 2 * N, 2 * N, 4)

    def body(*refs):
        acc = lax.dot_general(refs[0][...], refs[1][...], (((1,), (1,)), ((), ())), preferred_element_type=F32)
        if relu is not None:
            acc = (acc * (2.0 * refs[2][...].astype(F32))).astype(BF16)
        refs[-1][...] = acc

    tile = pl.BlockSpec((tm, tk), lambda i, j: (i, j))
    in_specs = [pl.BlockSpec((tm, N), lambda i, j: (i, 0)), pl.BlockSpec((None, tk, N), lambda i, j: (layer, j, 0))]
    args = [dy, w3]
    if relu is not None:
        in_specs.append(tile)
        args.append(relu)
    in_specs += [ANY_SPEC] * len(deps)
    args += list(deps)
    return pl.pallas_call(
        body, grid=(M // tm, K // tk), in_specs=in_specs, out_specs=tile,
        out_shape=SDS((M, K), F32 if relu is None else BF16),
        name=name, compiler_params=_params("parallel", "parallel"))(*args)


def _fam_dims(kind, K, N):
    return (K // 2, N // N_CHIPS) if kind == "col" else (K // (2 * N_CHIPS), N)


def _mm_tn(x, dy, kind, name):
    M, K = x.shape
    N = dy.shape[1]
    nr, nc = _fam_dims(kind, K, N)
    tk = _ctile(nr)
    tn = _ctile(nc)
    rt, ct = nr // tk, nc // tn

    def body(x_ref, dy_ref, o_ref):
        o_ref[...] = lax.dot_general(x_ref[...], dy_ref[...], (((0,), (0,)), ((), ())),
                                     preferred_element_type=F32).astype(BF16)

    if kind == "col":
        omap = lambda i, j: (i // rt, j // ct, i % rt, j % ct)
    else:
        omap = lambda i, j: ((i // rt) % 2, i // (2 * rt), i % rt, j)
    return pl.pallas_call(
        body, grid=(K // tk, N // tn),
        in_specs=[pl.BlockSpec((M, tk), lambda i, j: (0, i)), pl.BlockSpec((M, tn), lambda i, j: (0, j))],
        out_specs=pl.BlockSpec((None, None, tk, tn), omap),
        out_shape=SDS((2, N_CHIPS, nr, nc), BF16),
        name=name, compiler_params=_params("parallel", "parallel"))(x, dy)


C_EVEN = 512


def _live(rows, base, total):
    r = _row_ids((rows, 1), base)
    return jnp.logical_and(r >= PAD, r < total).astype(F32)


def _conv_taps(win, w_ref, ls, acc, flip):
    for b in range(8):
        rb = win if b == 0 else pltpu.roll(win, 96 - b, 0)
        for a in range(5):
            o = 8 * a + b
            tap = (30 - o) if flip else (o - 2)
            if 0 <= tap < CONV_WIDTH:
                acc = acc + w_ref[pl.ds(tap, 1), ls] * rb[8 * a:8 * a + CHUNK]
    return acc


def _window_sum(win, levels, forward):
    s = win
    n = win.shape[0]
    for k in range(levels):
        step = 1 << k
        s = s + pltpu.roll(s, (n - step) if forward else step, 0)
    return s


def _pool_count(base, g):
    pos = _row_ids((CHUNK, 1), base) - PAD
    return jnp.clip(pos + 1, 1, POOL_WINDOWS[g]).astype(F32)


def _even_fwd(u, cw3, cb3, lg3, lb3, pw4, pb3, ps3, j, name):
    T = u.shape[0]
    C = C_EVEN
    tm = _tile(T, 320)
    nch = tm // CHUNK
    nblk = T // CHUNK

    def body(u_ref, up_ref, cw_ref, cb_ref, lg_ref, lb_ref, pw_ref, pb_ref, ps_ref, o_ref, a_s, p_s, yc_s):
        row0 = pl.program_id(0) * tm
        up = up_ref[...]
        lp = _live(CHUNK, row0 - CHUNK, T)
        a_s[0:CHUNK, :] = up[:, 0:C] * _sigmoid(up[:, C:2 * C]) * lp
        p_s[0:CHUNK, :] = up[:, 2 * C:3 * C] * lp

        def stage(c, _):
            rs = pl.multiple_of(c * CHUNK, CHUNK)
            lv = _live(CHUNK, row0 + rs, T)
            a_s[pl.ds(rs + CHUNK, CHUNK), :] = u_ref[pl.ds(rs, CHUNK), 0:C] * _sigmoid(u_ref[pl.ds(rs, CHUNK), C:2 * C]) * lv
            p_s[pl.ds(rs + CHUNK, CHUNK), :] = u_ref[pl.ds(rs, CHUNK), 2 * C:3 * C] * lv
            return 0

        lax.fori_loop(0, nch, stage, 0)

        def chunk(c, _):
            rs = pl.multiple_of(c * CHUNK, CHUNK)
            lv = _live(CHUNK, row0 + rs, T)
            for cb in range(4):
                ls = slice(cb * 128, (cb + 1) * 128)
                win = a_s[pl.ds(pl.multiple_of(rs + 32, 32), 96), ls]
                acc = jnp.broadcast_to(cb_ref[:, ls], (CHUNK, 128))
                yc_s[:, ls] = _conv_taps(win, cw_ref, ls, acc, False)
            y = yc_s[...]
            xc = y - jnp.mean(y, axis=-1, keepdims=True)
            yn = xc * lax.rsqrt(jnp.mean(xc * xc, axis=-1, keepdims=True) + EPS) * lg_ref[...] + lb_ref[...]
            o_ref[pl.ds(rs, CHUNK), 0:C] = (yn * _sigmoid(yn) * lv).astype(BF16)
            for g in range(4):
                ls = slice(g * 128, (g + 1) * 128)
                win = p_s[pl.ds(pl.multiple_of(rs + 48, 16), 80), ls]
                s = _window_sum(win, g + 1, False)
                d = s[16:80] / _pool_count(row0 + rs, g) - win[16:80]
                yv = jnp.dot(d.astype(BF16), pw_ref[g].astype(BF16), preferred_element_type=F32) + pb_ref[:, ls]
                o_ref[pl.ds(rs, CHUNK), C + g * 128:C + (g + 1) * 128] = (yv * ps_ref[:, ls] * lv).astype(BF16)
            return 0

        lax.fori_loop(0, nch, chunk, 0)

    vec = pl.BlockSpec((None, 1, C), lambda i: (j, 0, 0))
    return pl.pallas_call(
        body, grid=(T // tm,),
        in_specs=[pl.BlockSpec((tm, 3 * C), lambda i: (i, 0)),
                  pl.BlockSpec((CHUNK, 3 * C), lambda i: (jnp.maximum(i * nch - 1, 0), 0)),
                  pl.BlockSpec((None, CONV_ROWS, C), lambda i: (j, 0, 0)), vec, vec, vec,
                  pl.BlockSpec((None, 4, 128, 128), lambda i: (j, 0, 0, 0)), vec, vec],
        out_specs=pl.BlockSpec((tm, 2 * C), lambda i: (i, 0)),
        out_shape=SDS((T, 2 * C), BF16),
        scratch_shapes=[pltpu.VMEM((tm + CHUNK, C), F32), pltpu.VMEM((tm + CHUNK, C), F32), pltpu.VMEM((CHUNK, C), F32)],
        name=name, compiler_params=_params("parallel"))(u, u, cw3, cb3, lg3, lb3, pw4, pb3, ps3)


def _even_bwd(u, dy, cw3, cb3, lg3, lb3, pw4, pb3, ps3, j, name):
    T = u.shape[0]
    C = C_EVEN
    tm = _tile(T, 320)
    nch = tm // CHUNK
    nblk = T // CHUNK
    ntile = T // tm

    def body(u_ref, up_ref, un_ref, dy_ref, dyn_ref, cw_ref, cb_ref, lg_ref, lb_ref, pw_ref, pb_ref, ps_ref,
             du_ref, dcw_ref, dcb_ref, dlg_ref, dlb_ref, dpw_ref, dpb_ref, dps_ref,
             a_s, p_s, dy_s, yc_s, dyc_s, dd_s, ddc_s, dw_s):
        i = pl.program_id(0)
        row0 = i * tm

        @pl.when(i == 0)
        def _():
            for ref in (dcb_ref, dlg_ref, dlb_ref, dpw_ref, dpb_ref, dps_ref, dw_s):
                ref[...] = jnp.zeros_like(ref)

        up = up_ref[...]
        lp = _live(CHUNK, row0 - CHUNK, T)
        a_s[0:CHUNK, :] = up[:, 0:C] * _sigmoid(up[:, C:2 * C]) * lp
        p_s[0:CHUNK, :] = up[:, 2 * C:3 * C] * lp
        un = un_ref[...]
        ln_ = _live(CHUNK, row0 + tm, T)
        a_s[tm + CHUNK:tm + 2 * CHUNK, :] = un[:, 0:C] * _sigmoid(un[:, C:2 * C]) * ln_
        p_s[tm + CHUNK:tm + 2 * CHUNK, :] = un[:, 2 * C:3 * C] * ln_
        dy_s[tm:tm + CHUNK, :] = dyn_ref[...] * ln_
        dyc_s[tm + CHUNK:tm + CHUNK + 32, :] = jnp.zeros((32, C), F32)

        def stage(c, _):
            rs = pl.multiple_of(c * CHUNK, CHUNK)
            lv = _live(CHUNK, row0 + rs, T)
            a_s[pl.ds(rs + CHUNK, CHUNK), :] = u_ref[pl.ds(rs, CHUNK), 0:C] * _sigmoid(u_ref[pl.ds(rs, CHUNK), C:2 * C]) * lv
            p_s[pl.ds(rs + CHUNK, CHUNK), :] = u_ref[pl.ds(rs, CHUNK), 2 * C:3 * C] * lv
            dy_s[pl.ds(rs, CHUNK), :] = dy_ref[pl.ds(rs, CHUNK), :] * lv
            return 0

        lax.fori_loop(0, nch, stage, 0)

        def first(c, _):
            rs = pl.multiple_of(c * CHUNK, CHUNK)
            own = jnp.where(c < nch, 1.0, 0.0).astype(F32)
            for cb in range(4):
                ls = slice(cb * 128, (cb + 1) * 128)
                win = a_s[pl.ds(pl.multiple_of(rs + 32, 32), 96), ls]
                acc = jnp.broadcast_to(cb_ref[:, ls], (CHUNK, 128))
                yc_s[:, ls] = _conv_taps(win, cw_ref, ls, acc, False)
            y = yc_s[...]
            xc = y - jnp.mean(y, axis=-1, keepdims=True)
            rstd = lax.rsqrt(jnp.mean(xc * xc, axis=-1, keepdims=True) + EPS)
            xh = xc * rstd
            yn = xh * lg_ref[...] + lb_ref[...]
            sg = _sigmoid(yn)
            dyn = dy_s[pl.ds(rs, CHUNK), 0:C] * (sg * (1.0 + yn * (1.0 - sg)))
            dlg_ref[...] += jnp.sum(dyn * xh, axis=0, keepdims=True) * own
            dlb_ref[...] += jnp.sum(dyn, axis=0, keepdims=True) * own
            dxh = dyn * lg_ref[...]
            dyc = rstd * (dxh - jnp.mean(dxh, axis=-1, keepdims=True) - xh * jnp.mean(dxh * xh, axis=-1, keepdims=True))
            dyc_s[pl.ds(rs, CHUNK), :] = dyc
            dcb_ref[...] += jnp.sum(dyc, axis=0, keepdims=True) * own
            for g in range(4):
                ls = slice(g * 128, (g + 1) * 128)
                win = p_s[pl.ds(pl.multiple_of(rs + 48, 16), 80), ls]
                s = _window_sum(win, g + 1, False)
                cnt = _pool_count(row0 + rs, g)
                d = (s[16:80] / cnt - win[16:80]).astype(BF16)
                w = pw_ref[g].astype(BF16)
                pre = jnp.dot(d, w, preferred_element_type=F32) + pb_ref[:, ls]
                dyb = dy_s[pl.ds(rs, CHUNK), C + g * 128:C + (g + 1) * 128]
                dpre = dyb * ps_ref[:, ls]
                dps_ref[:, ls] += jnp.sum(dyb * pre, axis=0, keepdims=True) * own
                dpb_ref[:, ls] += jnp.sum(dpre, axis=0, keepdims=True) * own
                dpre_b = (dpre * own).astype(BF16)
                dpw_ref[g] += lax.dot_general(d, dpre_b, (((0,), (0,)), ((), ())), preferred_element_type=F32)
                dd = lax.dot_general(dpre.astype(BF16), w, (((1,), (1,)), ((), ())), preferred_element_type=F32)
                dd_s[pl.ds(rs, CHUNK), ls] = dd
                ddc_s[pl.ds(rs, CHUNK), ls] = dd / cnt
            return 0

        lax.fori_loop(0, nch + 1, first, 0)
        ddc_s[tm + CHUNK:tm + CHUNK + 16, :] = jnp.zeros((16, C), F32)

        def second(c, _):
            rs = pl.multiple_of(c * CHUNK, CHUNK)
            lv = _live(CHUNK, row0 + rs, T)
            for cb in range(4):
                ls = slice(cb * 128, (cb + 1) * 128)
                wd = dyc_s[pl.ds(rs, 96), ls]
                da = _conv_taps(wd, cw_ref, ls, jnp.zeros((CHUNK, 128), F32), True)
                wa = a_s[pl.ds(pl.multiple_of(rs + 32, 32), 96), ls]
                dyc = dyc_s[pl.ds(rs, CHUNK), ls]
                for b in range(8):
                    rb = wa if b == 0 else pltpu.roll(wa, 96 - b, 0)
                    for a in range(5):
                        tap = 8 * a + b - 2
                        if 0 <= tap < CONV_WIDTH:
                            prod = dyc * rb[8 * a:8 * a + CHUNK]
                            part = prod[0:8]
                            for q in range(1, 8):
                                part = part + prod[8 * q:8 * q + 8]
                            dw_s[8 * tap:8 * tap + 8, ls] += part
                val = u_ref[pl.ds(rs, CHUNK), ls]
                sg = _sigmoid(u_ref[pl.ds(rs, CHUNK), C + cb * 128:C + (cb + 1) * 128])
                du_ref[pl.ds(rs, CHUNK), ls] = (da * sg * lv).astype(BF16)
                du_ref[pl.ds(rs, CHUNK), C + cb * 128:C + (cb + 1) * 128] = (da * val * sg * (1.0 - sg) * lv).astype(BF16)
            for g in range(4):
                ls = slice(g * 128, (g + 1) * 128)
                z = _window_sum(ddc_s[pl.ds(rs, 80), ls], g + 1, True)
                dpin = (z[0:CHUNK] - dd_s[pl.ds(rs, CHUNK), ls]) * lv
                du_ref[pl.ds(rs, CHUNK), 2 * C + g * 128:2 * C + (g + 1) * 128] = dpin.astype(BF16)
            return 0

        lax.fori_loop(0, nch, second, 0)

        @pl.when(i == ntile - 1)
        def _():
            for tap in range(CONV_WIDTH):
                dcw_ref[tap:tap + 1, :] = jnp.sum(dw_s[8 * tap:8 * tap + 8, :], axis=0, keepdims=True)
            dcw_ref[CONV_WIDTH:CONV_ROWS, :] = jnp.zeros((CONV_ROWS - CONV_WIDTH, C), F32)

    vec = pl.BlockSpec((None, 1, C), lambda i: (j, 0, 0))
    ovec = pl.BlockSpec((1, C), lambda i: (0, 0))
    return pl.pallas_call(
        body, grid=(ntile,),
        in_specs=[pl.BlockSpec((tm, 3 * C), lambda i: (i, 0)),
                  pl.BlockSpec((CHUNK, 3 * C), lambda i: (jnp.maximum(i * nch - 1, 0), 0)),
                  pl.BlockSpec((CHUNK, 3 * C), lambda i: (jnp.minimum((i + 1) * nch, nblk - 1), 0)),
                  pl.BlockSpec((tm, 2 * C), lambda i: (i, 0)),
                  pl.BlockSpec((CHUNK, 2 * C), lambda i: (jnp.minimum((i + 1) * nch, nblk - 1), 0)),
                  pl.BlockSpec((None, CONV_ROWS, C), lambda i: (j, 0, 0)), vec, vec, vec,
                  pl.BlockSpec((None, 4, 128, 128), lambda i: (j, 0, 0, 0)), vec, vec],
        out_specs=[pl.BlockSpec((tm, 3 * C), lambda i: (i, 0)), pl.BlockSpec((CONV_ROWS, C), lambda i: (0, 0)),
                   ovec, ovec, ovec, pl.BlockSpec((4, 128, 128), lambda i: (0, 0, 0)), ovec, ovec],
        out_shape=[SDS((T, 3 * C), BF16), SDS((CONV_ROWS, C), F32), SDS((1, C), F32), SDS((1, C), F32), SDS((1, C), F32),
                   SDS((4, 128, 128), F32), SDS((1, C), F32), SDS((1, C), F32)],
        scratch_shapes=[pltpu.VMEM((tm + 2 * CHUNK, C), F32), pltpu.VMEM((tm + 2 * CHUNK, C), F32),
                        pltpu.VMEM((tm + CHUNK, 2 * C), F32), pltpu.VMEM((CHUNK, C), F32),
                        pltpu.VMEM((tm + CHUNK + 32, C), F32), pltpu.VMEM((tm + CHUNK, C), F32),
                        pltpu.VMEM((tm + CHUNK + 16, C), F32), pltpu.VMEM((8 * CONV_ROWS, C), F32)],
        name=name, compiler_params=_params("arbitrary"))(u, u, u, dy, dy, cw3, cb3, lg3, lb3, pw4, pb3, ps3)


HI = lax.Precision.HIGHEST


def _dot_nt(a, b):
    return lax.dot_general(a, b, (((1,), (1,)), ((), ())), preferred_element_type=F32)


def _dot_tn(a, b):
    return lax.dot_general(a, b, (((0,), (0,)), ((), ())), preferred_element_type=F32)


def _tri(lower):
    r = lax.broadcasted_iota(jnp.int32, (CHUNK, CHUNK), 0)
    c = lax.broadcasted_iota(jnp.int32, (CHUNK, CHUNK), 1)
    return jnp.where((c <= r) if lower else (c >= r), 1.0, 0.0).astype(F32)


def _hgrn_gates(u_ref, lb_ref, h, D, lv):
    ls = slice(h * HEAD_DIM, (h + 1) * HEAD_DIM)
    qraw = u_ref[:, ls]
    fraw = u_ref[:, D + h * HEAD_DIM:D + (h + 1) * HEAD_DIM]
    v = u_ref[:, 2 * D + h * HEAD_DIM:2 * D + (h + 1) * HEAD_DIM] * lv
    lbv = lb_ref[:, ls]
    sig = _sigmoid(fraw)
    forget = lbv + (1.0 - lbv) * sig
    logf = jnp.log(forget) * lv
    k = (1.0 - forget) * lv
    qsig = _sigmoid(qraw)
    q = qraw * qsig * lv
    return q, k, v, logf, (qraw, qsig, sig, forget, lbv)


def _sub_parts(q, k, b, b_s, I):
    rows = slice(SUB * I, SUB * (I + 1))
    rho = jnp.zeros((1, HEAD_DIM), F32) if I == 0 else b_s[SUB * I - 1:SUB * I, :]
    eI = jnp.exp(b[rows] - rho)
    EI = jnp.exp(jnp.minimum(rho - b, EXP_CAP))
    causal = (lax.broadcasted_iota(jnp.int32, (SUB, CHUNK), 1)
              <= lax.broadcasted_iota(jnp.int32, (SUB, CHUNK), 0) + SUB * I)
    return rows, q[rows] * eI, k * EI, eI, EI, causal


def _hgrn_fwd(u, lb3, layer, gn3, j, name):
    T = u.shape[0]
    D = u.shape[1] // 4
    H = D // HEAD_DIM
    NC = T // CHUNK

    def body(u_ref, lb_ref, gn_ref, y_ref, o_ref, sall_ref, st_s, b_s, lf_s, q_s, k_s):
        n = pl.program_id(0)

        @pl.when(n == 0)
        def _():
            st_s[...] = jnp.zeros_like(st_s)

        lv = _live(CHUNK, n * CHUNK, T)
        heads = range(H)
        cols = [slice(h * HEAD_DIM, (h + 1) * HEAD_DIM) for h in heads]
        vb = []
        for h in heads:
            q, k, v, logf, _ = _hgrn_gates(u_ref, lb_ref, h, D, lv)
            q_s[:, cols[h]] = q
            k_s[:, cols[h]] = k
            lf_s[:, cols[h]] = logf
            vb.append(v.astype(BF16))
        b_s[...] = jnp.dot(_tri(True), lf_s[...], precision=HI, preferred_element_type=F32)
        ops = []
        for h in heads:
            b_h = b_s.at[:, cols[h]]
            b = b_h[...]
            q = q_s[:, cols[h]]
            k = k_s[:, cols[h]]
            blast = b_h[CHUNK - 1:CHUNK, :]
            qh = (q * jnp.exp(b)).astype(BF16)
            kt = (k * jnp.exp(blast - b)).astype(BF16)
            subs = []
            for I in range(CHUNK // SUB):
                _, qI, KI, _, _, causal = _sub_parts(q, k, b, b_h, I)
                subs.append((qI.astype(BF16), KI.astype(BF16), causal))
            ops.append((qh, kt, jnp.exp(blast), subs))
        mm = []
        for h in heads:
            qh, kt, eblast, subs = ops[h]
            st = st_s[h]
            sall_ref[h] = st
            o_inter = _dot_nt(qh, st.astype(BF16))
            st_s[h] = st * eblast + _dot_tn(vb[h], kt)
            mm.append((o_inter, [_dot_nt(qI, KI) for qI, KI, _ in subs]))
        for h in heads:
            o_inter, ps = mm[h]
            p = jnp.concatenate([jnp.where(c, x, 0.0) for x, (_, _, c) in zip(ps, ops[h][3])], axis=0).astype(BF16)
            o = o_inter + jnp.dot(p, vb[h], preferred_element_type=F32)
            o_ref[:, cols[h]] = o
            graw = u_ref[:, 3 * D + h * HEAD_DIM:3 * D + (h + 1) * HEAD_DIM]
            r = lax.rsqrt(jnp.mean(o * o, axis=-1, keepdims=True) + EPS)
            y_ref[:, cols[h]] = (((o * r) * gn_ref[...]) * (graw * _sigmoid(graw))).astype(BF16)

    return pl.pallas_call(
        body, grid=(NC,),
        in_specs=[pl.BlockSpec((CHUNK, 4 * D), lambda n: (n, 0)),
                  pl.BlockSpec((None, 1, D), lambda n: (layer, 0, 0)),
                  pl.BlockSpec((None, 1, HEAD_DIM), lambda n: (j, 0, 0))],
        out_specs=[pl.BlockSpec((CHUNK, D), lambda n: (n, 0)), pl.BlockSpec((CHUNK, D), lambda n: (n, 0)),
                   pl.BlockSpec((None, H, HEAD_DIM, HEAD_DIM), lambda n: (n, 0, 0, 0))],
        out_shape=[SDS((T, D), BF16), SDS((T, D), F32), SDS((NC, H, HEAD_DIM, HEAD_DIM), F32)],
        scratch_shapes=[pltpu.VMEM((H, HEAD_DIM, HEAD_DIM), F32)] + [pltpu.VMEM((CHUNK, D), F32)] * 4,
        name=name, compiler_params=_params("arbitrary"))(u, lb3, gn3)


def _hgrn_bwd(u, o_raw, dy, sall, lb3, layer, gn3, j, name):
    T = u.shape[0]
    D = u.shape[1] // 4
    H = D // HEAD_DIM
    NC = T // CHUNK

    def body(u_ref, o_ref, dy_ref, sall_ref, lb_ref, gn_ref, du_ref, dlb_ref, dgn_ref, dst_s, b_s, lf_s, q_s, k_s, db_s, dk_s):
        step = pl.program_id(0)
        n = NC - 1 - step

        @pl.when(step == 0)
        def _():
            dst_s[...] = jnp.zeros_like(dst_s)
            dlb_ref[...] = jnp.zeros_like(dlb_ref)
            dgn_ref[...] = jnp.zeros_like(dgn_ref)

        lv = _live(CHUNK, n * CHUNK, T)
        last_row = (_row_ids((CHUNK, 1), 0) == CHUNK - 1).astype(F32)
        gn = gn_ref[...]
        heads = range(H)
        cols = [slice(h * HEAD_DIM, (h + 1) * HEAD_DIM) for h in heads]
        vb, dob = [], []
        dgn = jnp.zeros((1, HEAD_DIM), F32)
        for h in heads:
            q, k, v, logf, _ = _hgrn_gates(u_ref, lb_ref, h, D, lv)
            q_s[:, cols[h]] = q
            k_s[:, cols[h]] = k
            lf_s[:, cols[h]] = logf
            vb.append(v.astype(BF16))
            graw = u_ref[:, 3 * D + h * HEAD_DIM:3 * D + (h + 1) * HEAD_DIM]
            gsig = _sigmoid(graw)
            o = o_ref[:, cols[h]]
            r = lax.rsqrt(jnp.mean(o * o, axis=-1, keepdims=True) + EPS)
            xh = o * r
            dyv = dy_ref[:, cols[h]]
            dsg = dyv * (graw * gsig)
            dgn = dgn + jnp.sum(dsg * xh, axis=0, keepdims=True)
            dxh = dsg * gn
            do = r * (dxh - xh * jnp.mean(dxh * xh, axis=-1, keepdims=True))
            dob.append(do.astype(BF16))
            dgraw = dyv * xh * gn * (gsig * (1.0 + graw * (1.0 - gsig)))
            du_ref[:, 3 * D + h * HEAD_DIM:3 * D + (h + 1) * HEAD_DIM] = (dgraw * lv).astype(BF16)
        dgn_ref[...] += dgn
        b_s[...] = jnp.dot(_tri(True), lf_s[...], precision=HI, preferred_element_type=F32)
        ops = []
        for h in heads:
            b_h = b_s.at[:, cols[h]]
            b = b_h[...]
            q = q_s[:, cols[h]]
            k = k_s[:, cols[h]]
            blast = b_h[CHUNK - 1:CHUNK, :]
            eb = jnp.exp(b)
            ekb = jnp.exp(blast - b)
            subs = []
            for I in range(CHUNK // SUB):
                rows, qI, KI, eI, EI, causal = _sub_parts(q, k, b, b_h, I)
                subs.append((rows, qI.astype(BF16), KI.astype(BF16), eI, EI, causal))
            ops.append((eb, ekb, jnp.exp(blast), (q * eb).astype(BF16), (k * ekb).astype(BF16), subs))
        mm = []
        for h in heads:
            eb, ekb, eblast, qhb, ktb, subs = ops[h]
            st = sall_ref[h]
            dst = dst_s[h]
            dstb = dst.astype(BF16)
            dv = _dot_nt(ktb, dstb)
            dqh = jnp.dot(dob[h], st.astype(BF16), preferred_element_type=F32)
            dkt = jnp.dot(vb[h], dstb, preferred_element_type=F32)
            dblast = jnp.sum(dst * st, axis=0, keepdims=True) * eblast
            dst_s[h] = dst * eblast + _dot_tn(dob[h], qhb)
            dp_full = _dot_nt(dob[h], vb[h])
            ps = [_dot_nt(qIb, KIb) for _, qIb, KIb, _, _, _ in subs]
            mm.append((dv, dqh, dkt, dblast, dp_full, ps))
        for h in heads:
            eb, ekb, eblast, qhb, ktb, subs = ops[h]
            dv, dqh, dkt, dblast, dp_full, ps = mm[h]
            p = jnp.concatenate([jnp.where(sub[5], x, 0.0) for x, sub in zip(ps, subs)], axis=0).astype(BF16)
            dv = dv + _dot_tn(p, dob[h])
            du_ref[:, 2 * D + h * HEAD_DIM:2 * D + (h + 1) * HEAD_DIM] = (dv * lv).astype(BF16)
            dq = dqh * eb
            db = dqh * qhb.astype(F32)
            tmp = dkt * ktb.astype(F32)
            dk = dkt * ekb
            db = db - tmp
            dblast = dblast + jnp.sum(tmp, axis=0, keepdims=True)
            dq_parts, db_parts = [], []
            for rows, qIb, KIb, eI, EI, causal in subs:
                dp = jnp.where(causal, dp_full[rows], 0.0).astype(BF16)
                dqI = jnp.dot(dp, KIb, preferred_element_type=F32)
                dKI = _dot_tn(dp, qIb)
                dq_parts.append(dqI * eI)
                db_parts.append(dqI * qIb.astype(F32))
                dk = dk + dKI * EI
                db = db - dKI * KIb.astype(F32)
            dq = dq + jnp.concatenate(dq_parts, axis=0)
            db_s[:, cols[h]] = db + jnp.concatenate(db_parts, axis=0) + last_row * dblast
            dk_s[:, cols[h]] = dk
            qraw = u_ref[:, cols[h]]
            qsig = _sigmoid(qraw)
            du_ref[:, cols[h]] = (dq * (qsig * (1.0 + qraw * (1.0 - qsig))) * lv).astype(BF16)
        lf_s[...] = jnp.dot(_tri(False), db_s[...], precision=HI, preferred_element_type=F32)
        for h in heads:
            fraw = u_ref[:, D + h * HEAD_DIM:D + (h + 1) * HEAD_DIM]
            lbv = lb_ref[:, cols[h]]
            sig = _sigmoid(fraw)
            forget = lbv + (1.0 - lbv) * sig
            dforget = (lf_s[:, cols[h]] / forget - dk_s[:, cols[h]]) * lv
            dlb_ref[:, cols[h]] += jnp.sum(dforget * (1.0 - sig), axis=0, keepdims=True)
            du_ref[:, D + h * HEAD_DIM:D + (h + 1) * HEAD_DIM] = (dforget * (1.0 - lbv) * sig * (1.0 - sig)).astype(BF16)

    rev = lambda s: (NC - 1 - s, 0)
    return pl.pallas_call(
        body, grid=(NC,),
        in_specs=[pl.BlockSpec((CHUNK, 4 * D), rev), pl.BlockSpec((CHUNK, D), rev), pl.BlockSpec((CHUNK, D), rev),
                  pl.BlockSpec((None, H, HEAD_DIM, HEAD_DIM), lambda s: (NC - 1 - s, 0, 0, 0)),
                  pl.BlockSpec((None, 1, D), lambda s: (layer, 0, 0)),
                  pl.BlockSpec((None, 1, HEAD_DIM), lambda s: (j, 0, 0))],
        out_specs=[pl.BlockSpec((CHUNK, 4 * D), rev), pl.BlockSpec((1, D), lambda s: (0, 0)),
                   pl.BlockSpec((1, HEAD_DIM), lambda s: (0, 0))],
        out_shape=[SDS((T, 4 * D), BF16), SDS((1, D), F32), SDS((1, HEAD_DIM), F32)],
        scratch_shapes=[pltpu.VMEM((H, HEAD_DIM, HEAD_DIM), F32)] + [pltpu.VMEM((CHUNK, D), F32)] * 6,
        name=name, compiler_params=_params("arbitrary"))(u, o_raw, dy, sall, lb3, gn3)


def _softmax_layers(p_ref, n_layers):
    rows = [p_ref[l:l + 1, :] for l in range(n_layers)]
    m = functools.reduce(jnp.maximum, rows)
    e = [jnp.exp(x - m) for x in rows]
    tot = functools.reduce(lambda a, b: a + b, e)
    return [x / tot for x in e]


def _lb_fwd(p):
    n_layers, D = p.shape

    def body(p_ref, o_ref):
        s = _softmax_layers(p_ref, n_layers)
        acc = jnp.zeros((1, D), F32)
        o_ref[0:1, :] = acc
        for l in range(1, n_layers):
            acc = acc + s[l]
            o_ref[l:l + 1, :] = acc

    return pl.pallas_call(body, out_shape=SDS(p.shape, F32), name="lb_fwd")(p)


def _lb_bwd(p, dlb):
    n_layers, D = p.shape

    def body(p_ref, d_ref, o_ref):
        s = _softmax_layers(p_ref, n_layers)
        ds = [jnp.zeros((1, D), F32)] * n_layers
        acc = jnp.zeros((1, D), F32)
        for l in range(n_layers - 1, 0, -1):
            acc = acc + d_ref[l:l + 1, :]
            ds[l] = acc
        dot = functools.reduce(lambda a, b: a + b, [s[l] * ds[l] for l in range(n_layers)])
        for l in range(n_layers):
            o_ref[l:l + 1, :] = s[l] * (ds[l] - dot)

    return pl.pallas_call(body, out_shape=SDS(p.shape, F32), name="lb_bwd")(p, dlb)


def _adamw(w, g, m, v, name):
    R, C = w.shape
    tr = _tile(R, 256, 8) if R % 8 == 0 else R

    def body(w_ref, g_ref, m_ref, v_ref, d_ref, mo_ref, vo_ref):
        g_ = g_ref[...]
        m_ = ADAM_B1 * m_ref[...] + (1.0 - ADAM_B1) * g_
        v_ = ADAM_B2 * v_ref[...] + (1.0 - ADAM_B2) * (g_ * g_)
        mh = m_ / (1.0 - ADAM_B1 ** ADAM_STEP)
        vh = v_ / (1.0 - ADAM_B2 ** ADAM_STEP)
        d_ref[...] = -ADAM_LR * (mh / (jnp.sqrt(vh) + ADAM_EPS) + ADAM_WD * w_ref[...])
        mo_ref[...] = m_
        vo_ref[...] = v_

    blk = pl.BlockSpec((tr, C), lambda i: (i, 0))
    return pl.pallas_call(
        body, grid=(R // tr,), in_specs=[blk] * 4, out_specs=[blk] * 3, out_shape=[SDS((R, C), F32)] * 3,
        name=name, compiler_params=_params("parallel"))(w, g, m, v)


def _adamw_layer(w3, m3, v3, g2, layer, outs, name):
    L, R, C = w3.shape
    tr = _tile(R, 256, 8)
    if outs is None:
        outs = tuple(lax.empty(w3.shape, F32) for _ in range(4))

    def body(w_ref, m_ref, v_ref, g_ref, a0, a1, a2, a3, go_ref, d_ref, mo_ref, vo_ref):
        del a0, a1, a2, a3
        g_ = g_ref[...]
        m_ = ADAM_B1 * m_ref[...] + (1.0 - ADAM_B1) * g_
        v_ = ADAM_B2 * v_ref[...] + (1.0 - ADAM_B2) * (g_ * g_)
        mh = m_ / (1.0 - ADAM_B1 ** ADAM_STEP)
        vh = v_ / (1.0 - ADAM_B2 ** ADAM_STEP)
        go_ref[...] = g_
        d_ref[...] = -ADAM_LR * (mh / (jnp.sqrt(vh) + ADAM_EPS) + ADAM_WD * w_ref[...])
        mo_ref[...] = m_
        vo_ref[...] = v_

    lay = pl.BlockSpec((None, tr, C), lambda i: (layer, i, 0))
    return pl.pallas_call(
        body, grid=(R // tr,), in_specs=[lay] * 3 + [pl.BlockSpec((tr, C), lambda i: (i, 0))] + [ANY_SPEC] * 4,
        out_specs=[lay] * 4, out_shape=[SDS(w3.shape, F32)] * 4, input_output_aliases={4: 0, 5: 1, 6: 2, 7: 3},
        name=name, compiler_params=_params("parallel"))(w3, m3, v3, g2, *outs)


SEM_SPEC = pl.BlockSpec(memory_space=pltpu.SEMAPHORE)
HBM_SPEC = pl.BlockSpec(memory_space=pltpu.HBM)
EFFECT = pltpu.SideEffectType.DATAFLOW_SIDE_EFFECTING
N_DEV = 2 * N_CHIPS


def _position():
    x, y, c = lax.axis_index("x"), lax.axis_index("y"), lax.axis_index("c")
    chips = [(1 - x, y), (x, 1 - y), (1 - x, 1 - y)]
    return x, y, c, chips


def _split_start(name, plan, bufs, n_sems, deps=()):
    n = len(bufs)

    def body(*refs):
        send_sems, recv_sems = refs[n + len(deps)], refs[n + len(deps) + 1]
        sends, _ = plan(refs[:n], send_sems, recv_sems)
        for kw in sends:
            pltpu.make_async_remote_copy(**kw).start()
        refs[-1][...] = jnp.zeros_like(refs[-1])

    out = pl.pallas_call(
        body, name=name,
        out_shape=(pltpu.SemaphoreType.DMA((n_sems,)), pltpu.SemaphoreType.DMA((n_sems,)),
                   *[pltpu.HBM(b.shape, b.dtype) for b in bufs], SDS((8, 128), F32)),
        in_specs=[HBM_SPEC] * n + [ANY_SPEC] * len(deps),
        out_specs=(SEM_SPEC, SEM_SPEC, *[HBM_SPEC] * n, pl.BlockSpec(memory_space=pltpu.VMEM)),
        input_output_aliases={i: 2 + i for i in range(n)},
        compiler_params=pltpu.CompilerParams(has_side_effects=EFFECT),
    )(*[pltpu.with_memory_space_constraint(b, pltpu.HBM) for b in bufs], *deps)
    return out[0], out[1], list(out[2:2 + n]), out[-1]


def _split_wait(name, plan, send_sems, recv_sems, bufs, after=()):
    n = len(bufs)

    def body(*refs):
        sends, recvs = plan(refs[:n], refs[n], refs[n + 1])
        for kw in sends:
            pltpu.make_async_remote_copy(**kw).wait_send()
        for kw in recvs:
            pltpu.make_async_remote_copy(**kw).wait_recv()

    out = pl.pallas_call(
        body, name=name, out_shape=tuple(pltpu.HBM(b.shape, b.dtype) for b in bufs),
        in_specs=[HBM_SPEC] * n + [SEM_SPEC, SEM_SPEC] + [ANY_SPEC] * len(after),
        out_specs=tuple([HBM_SPEC] * n), input_output_aliases={i: i for i in range(n)},
        compiler_params=pltpu.CompilerParams(has_side_effects=EFFECT),
    )(*bufs, send_sems, recv_sems, *after)
    return list(out)


def _region(kind, ref, chip):
    K, N = ref.shape
    if kind == "col":
        return ref.at[:, pl.ds(chip * (N // N_CHIPS), N // N_CHIPS)]
    return ref.at[pl.ds(chip * (K // N_CHIPS), K // N_CHIPS), :]


def _gather_plan(kinds):
    def plan(refs, send_sems, recv_sems):
        x, y, c, chips = _position()
        sends, recvs = [], []
        for f, (ref, kind) in enumerate(zip(refs, kinds)):
            own = _region(kind, ref, 2 * x + y)
            for k, chip in enumerate(chips):
                sem = dict(send_sem=send_sems.at[3 * f + k], recv_sem=recv_sems.at[3 * f + k],
                           device_id=(*chip, c), device_id_type=MESH)
                theirs = _region(kind, ref, 2 * chip[0] + chip[1])
                sends.append(dict(src_ref=own, dst_ref=own, **sem))
                recvs.append(dict(src_ref=theirs, dst_ref=theirs, **sem))
        return sends, recvs
    return plan


def _reduce_plan(refs, send_sems, recv_sems):
    x, y, c, _ = _position()
    me = 4 * x + 2 * y + c
    sends, recvs = [], []
    for f in range(len(refs) // 2):
        acc, land = refs[2 * f], refs[2 * f + 1]
        for d in range(1, N_DEV):
            t = (me + d) % N_DEV
            to = dict(device_id=(t // 4, (t // 2) % 2, t % 2), device_id_type=MESH)
            slot = N_DEV - 1 - d
            sends.append(dict(src_ref=acc.at[t % 2, t // 2], dst_ref=land.at[slot], send_sem=send_sems.at[7 * f + d - 1],
                              recv_sem=recv_sems.at[7 * f + slot], **to))
            recvs.append(dict(src_ref=land.at[d - 1], dst_ref=land.at[d - 1], send_sem=send_sems.at[7 * f + d - 1],
                              recv_sem=recv_sems.at[7 * f + d - 1], **to))
    return sends, recvs


def _swap_plan(refs, send_sems, recv_sems):
    x, y, c, _ = _position()
    sends, recvs = [], []
    for f, g in enumerate(refs):
        sem = dict(send_sem=send_sems.at[f], recv_sem=recv_sems.at[f], device_id=(x, y, 1 - c), device_id_type=MESH)
        sends.append(dict(src_ref=g.at[c], dst_ref=g.at[c], **sem))
        recvs.append(dict(src_ref=g.at[1 - c], dst_ref=g.at[1 - c], **sem))
    return sends, recvs


def _sum_pieces(ids2, acc, land, name):
    _, _, nr, nc = acc.shape
    tr = _tile(nr, 256, 16)

    def body(ids_ref, own_ref, land_ref, o_ref):
        del ids_ref
        s = own_ref[...].astype(F32)
        for k in range(N_DEV - 1):
            s = s + land_ref[k].astype(F32)
        o_ref[...] = s

    return pl.pallas_call(
        body,
        grid_spec=pltpu.PrefetchScalarGridSpec(
            num_scalar_prefetch=1, grid=(nr // tr,),
            in_specs=[pl.BlockSpec((None, None, tr, nc), lambda i, ids: (ids[0], ids[1], i, 0)),
                      pl.BlockSpec((N_DEV - 1, tr, nc), lambda i, ids: (0, i, 0))],
            out_specs=pl.BlockSpec((None, tr, nc), lambda i, ids: (ids[0], i, 0))),
        out_shape=SDS((2, nr, nc), F32), name=name, compiler_params=_params("parallel"))(ids2, acc, land)


def _all_sum(block):
    m_per, n = block.shape

    def body(x_ref, all_ref, sum_ref, send_sems, recv_sems, local_sem):
        x, y, c, chips = _position()
        me, sibling = (x, y, c), (x, y, 1 - c)

        def rows(px, py, pc):
            return all_ref.at[pl.ds((4 * px + 2 * py + pc) * m_per, m_per), :]

        def copy(k, blk, to, src=None):
            return pltpu.make_async_remote_copy(
                src_ref=rows(*blk) if src is None else src, dst_ref=rows(*blk), send_sem=send_sems.at[k],
                recv_sem=recv_sems.at[k], device_id=to, device_id_type=MESH)

        mine = pltpu.make_async_copy(x_ref, rows(*me), local_sem)
        mine.start()
        first = [copy(0, me, sibling, src=x_ref)]
        first += [copy(1 + k, me, (*chip, c), src=x_ref) for k, chip in enumerate(chips)]
        for cp in first:
            cp.start()
        passed = [copy(4 + k, (*chip, c), sibling) for k, chip in enumerate(chips)]
        for k, chip in enumerate(chips):
            copy(1 + k, (*chip, c), me).wait_recv()
            passed[k].start()
        copy(0, sibling, me).wait_recv()
        for k, chip in enumerate(chips):
            copy(4 + k, (*chip, 1 - c), me).wait_recv()
        for cp in first + passed:
            cp.wait_send()
        mine.wait()
        acc = all_ref[0:m_per, :]
        for d in range(1, N_DEV):
            acc = acc + all_ref[d * m_per:(d + 1) * m_per, :]
        sum_ref[...] = acc

    vm = pl.BlockSpec(memory_space=pltpu.VMEM)
    return pl.pallas_call(
        body, in_specs=[vm], out_specs=[vm, vm], out_shape=[SDS((N_DEV * m_per, n), F32), SDS((m_per, n), F32)],
        scratch_shapes=[pltpu.SemaphoreType.DMA((7,)), pltpu.SemaphoreType.DMA((7,)), pltpu.SemaphoreType.DMA],
        name="all_sum", compiler_params=_params())(block)[1]


BIG = {"ev_w_in": "col", "ev_w_out": "row", "od_w_in": "col", "od_w_out": "row", "mlp_w1": "col", "mlp_w2": "row"}
WEIGHTS = ("meta_tokens", "mix_norm_g", "mlp_norm_g", "final_norm_g", "ev_w_in", "ev_conv_w", "ev_conv_b", "ev_ln_g",
           "ev_ln_b", "ev_pool_w", "ev_pool_b", "ev_pool_scale", "ev_w_out", "od_w_in", "od_gnorm_g", "od_w_out",
           "lb_param", "mlp_w1", "mlp_w2")
PACK_UNIT = 1024


def _mixer_names(layer):
    return ("ev_w_in", "ev_w_out") if layer % 2 == 0 else ("od_w_in", "od_w_out")


def _pack(arrays):
    flat = []
    for a in arrays:
        a = a.reshape(-1)
        flat.append(jnp.pad(a, (0, (-a.shape[0]) % PACK_UNIT)))
    return jnp.concatenate(flat).reshape(-1, 128)


def _unpack(packed, shapes):
    flat = packed.reshape(-1)
    out, off = [], 0
    for s in shapes:
        size = 1
        for d in s:
            size *= d
        out.append(flat[off:off + size].reshape(s))
        off += size + (-size) % PACK_UNIT
    return out


def _local_step(x2, target, P, weights, boundary, first_deps=()):
    D = x2.shape[1]
    n_layers = P["mix_norm_g"].shape[0]
    h = jnp.concatenate([jnp.zeros((PAD, D), F32), P["meta_full"], x2], axis=0)
    mix_g = P["mix_norm_g"].reshape(n_layers, 1, D)
    mlp_g = P["mlp_norm_g"].reshape(n_layers, 1, D)
    vec = lambda a: a.reshape(a.shape[0], 1, -1)
    cb3, lg3, lnb3, ps3 = vec(P["ev_conv_b"]), vec(P["ev_ln_g"]), vec(P["ev_ln_b"]), vec(P["ev_pool_scale"])
    pb3 = vec(P["ev_pool_b"])
    gn3 = vec(P["od_gnorm_g"])
    lb_all = _lb_fwd(P["lb_param"])
    lb3 = lb_all.reshape(n_layers, 1, D)
    even = (cb3, lg3, lnb3, P["ev_pool_w"], pb3, ps3)

    saved = []
    deps = tuple(first_deps)
    for layer in range(n_layers):
        j = layer // 2
        w_in, w_out = _mixer_names(layer)
        W = {}
        s = {"h": h, "W": W}
        s["n"] = _rms_fwd(h, mix_g, layer, f"mix_norm_{layer}", deps=deps)
        deps = ()
        W[w_in] = weights(layer, w_in, (s["n"],))
        s["u"] = _mm_nn(s["n"], W[w_in], 0, f"mix_in_{layer}")
        if layer % 2 == 0:
            s["y"] = _even_fwd(s["u"], P["conv_w_full"], *even, j, f"even_fwd_{layer}")
        else:
            s["y"], s["o"], s["sall"] = _hgrn_fwd(s["u"], lb3, layer, gn3, j, f"hgrn_fwd_{layer}")
        W[w_out] = weights(layer, w_out, (s["y"],))
        h = _mm_nn(s["y"], W[w_out], 0, f"mix_out_{layer}", res=h)
        s["h1"] = h
        s["n2"] = _rms_fwd(h, mlp_g, layer, f"mlp_norm_{layer}")
        W["mlp_w1"] = weights(layer, "mlp_w1", (s["n2"],))
        s["act"], s["relu"] = _mm_nn(s["n2"], W["mlp_w1"], 0, f"mlp_up_{layer}", relu2=True)
        W["mlp_w2"] = weights(layer, "mlp_w2", (s["act"],))
        h = _mm_nn(s["act"], W["mlp_w2"], 0, f"mlp_down_{layer}", res=h)
        saved.append(s)

    dh, dhb, dg_final, loss = _final(h, P["final_norm_g"].reshape(1, D), target)

    small = {"final_norm_g": dg_final}
    per_layer = {k: [None] * n_layers for k in ("mix_norm_g", "mlp_norm_g", "lb")}
    per_pair = {k: [None] * (n_layers // 2) for k in
                ("ev_conv_w", "ev_conv_b", "ev_ln_g", "ev_ln_b", "ev_pool_w", "ev_pool_b", "ev_pool_scale", "od_gnorm_g")}
    for layer in reversed(range(n_layers)):
        j = layer // 2
        s = saved[layer]
        W = s["W"]
        w_in, w_out = _mixer_names(layer)
        dz = _mm_nt(dhb, W["mlp_w2"], 0, f"d_act_{layer}", relu=s["relu"], deps=deps)
        dw2 = _mm_tn(s["act"], dhb, "row", f"dw2_{layer}")
        dn2 = _mm_nt(dz, W["mlp_w1"], 0, f"d_n2_{layer}")
        dw1 = _mm_tn(s["n2"], dz, "col", f"dw1_{layer}")
        dh, dhb, per_layer["mlp_norm_g"][layer] = _rms_bwd(s["h1"], mlp_g, layer, dn2, dh, f"mlp_norm_bwd_{layer}")
        deps = boundary(f"mlp{layer}", {("mlp_w1", layer): dw1, ("mlp_w2", layer): dw2}, (dhb,))
        dy = _mm_nt(dhb, W[w_out], 0, f"d_y_{layer}", deps=deps)
        dwout = _mm_tn(s["y"], dhb, "row", f"dwout_{layer}")
        if layer % 2 == 0:
            du, dcw, dcb, dlg, dlnb, dpw, dpb, dps = _even_bwd(s["u"], dy, P["conv_w_full"], *even, j, f"even_bwd_{layer}")
            for k, val in (("ev_conv_w", dcw), ("ev_conv_b", dcb), ("ev_ln_g", dlg), ("ev_ln_b", dlnb),
                           ("ev_pool_w", dpw), ("ev_pool_b", dpb), ("ev_pool_scale", dps)):
                per_pair[k][j] = val
        else:
            du, per_layer["lb"][layer], per_pair["od_gnorm_g"][j] = _hgrn_bwd(
                s["u"], s["o"], dy, s["sall"], lb3, layer, gn3, j, f"hgrn_bwd_{layer}")
        dwin = _mm_tn(s["n"], du, "col", f"dwin_{layer}")
        deps = boundary(f"mix{layer}", {(w_in, j): dwin, (w_out, j): dwout}, (du,))
        dn = _mm_nt(du, W[w_in], 0, f"d_n_{layer}", deps=deps)
        dh, dhb, per_layer["mix_norm_g"][layer] = _rms_bwd(s["h"], mix_g, layer, dn, dh, f"mix_norm_bwd_{layer}")
        deps = ()

    small["mix_norm_g"] = jnp.concatenate(per_layer["mix_norm_g"], axis=0)
    small["mlp_norm_g"] = jnp.concatenate(per_layer["mlp_norm_g"], axis=0)
    dlb_all = jnp.concatenate([jnp.zeros((1, D), F32) if g is None else g for g in per_layer["lb"]], axis=0)
    small["lb_param"] = _lb_bwd(P["lb_param"], dlb_all)
    for k, vals in per_pair.items():
        small[k] = jnp.stack(vals, axis=0)
    small["meta_tokens"] = dh[PAD:LEAD]
    return loss, dh, small


def kernel(x, meta_tokens, mix_norm_g, mlp_norm_g, final_norm_g, ev_w_in, ev_conv_w, ev_conv_b, ev_ln_g, ev_ln_b, ev_pool_w, ev_pool_b, ev_pool_scale, ev_w_out, od_w_in, od_gnorm_g, od_w_out, lb_param, mlp_w1, mlp_w2, loss_target, m_meta_tokens, m_mix_norm_g, m_mlp_norm_g, m_final_norm_g, m_ev_w_in, m_ev_conv_w, m_ev_conv_b, m_ev_ln_g, m_ev_ln_b, m_ev_pool_w, m_ev_pool_b, m_ev_pool_scale, m_ev_w_out, m_od_w_in, m_od_gnorm_g, m_od_w_out, m_lb_param, m_mlp_w1, m_mlp_w2, v_meta_tokens, v_mix_norm_g, v_mlp_norm_g, v_final_norm_g, v_ev_w_in, v_ev_conv_w, v_ev_conv_b, v_ev_ln_g, v_ev_ln_b, v_ev_pool_w, v_ev_pool_b, v_ev_pool_scale, v_ev_w_out, v_od_w_in, v_od_gnorm_g, v_od_w_out, v_lb_param, v_mlp_w1, v_mlp_w2):
    given = dict(locals())
    w = {n: given[n] for n in WEIGHTS}
    m = {n: given["m_" + n] for n in WEIGHTS}
    v = {n: given["v_" + n] for n in WEIGHTS}
    n_layers = mix_norm_g.shape[0]
    core = lax.axis_index("c").astype(jnp.int32)
    chip = (2 * lax.axis_index("x") + lax.axis_index("y")).astype(jnp.int32)
    chip1 = chip.reshape(1)
    ids2 = jnp.stack([core, chip])

    conv_pad = jnp.pad(ev_conv_w, ((0, 0), (0, CONV_ROWS - CONV_WIDTH), (0, 0)))
    stages = [[(0, n)] for n in (*_mixer_names(0), "mlp_w1", "mlp_w2")]
    stages += [[(layer, n) for n in (*_mixer_names(layer), "mlp_w1", "mlp_w2")] for layer in range(1, n_layers)]
    gathers, where, token = [], {}, ()
    for k, stage in enumerate(stages):
        index = [layer if n.startswith("mlp") else layer // 2 for layer, n in stage]
        kinds = [BIG[n] for _, n in stage]
        bufs = [_cast_place(w[n], i, BIG[n], chip1, BF16, f"place_{n}_{i}") for (_, n), i in zip(stage, index)]
        if k == 0:
            bufs.append(_cast_place(meta_tokens[None], 0, "col", chip1, F32, "place_meta"))
            bufs.append(_cast_place(conv_pad.reshape(1, -1, conv_pad.shape[2]), 0, "col", chip1, F32, "place_conv_w"))
            kinds += ["col", "col"]
        plan = _gather_plan(kinds)
        ss, rs, bufs, tok = _split_start(f"gather_start_{k}", plan, bufs, 3 * len(bufs), deps=token)
        token = (tok,)
        gathers.append((plan, ss, rs, bufs))
        where.update({key: (k, f) for f, key in enumerate(stage)})

    landed = {}

    def arrived(k, after):
        if k not in landed:
            plan, ss, rs, bufs = gathers[k]
            landed[k] = _split_wait(f"gather_wait_{k}", plan, ss, rs, bufs, after)
        return landed[k]

    def weights(layer, name, after):
        k, f = where[(layer, name)]
        return arrived(k, after)[f][None]

    first = arrived(0, ())
    P = {n: w[n] for n in ("mix_norm_g", "mlp_norm_g", "final_norm_g", "ev_conv_b", "ev_ln_g", "ev_ln_b", "ev_pool_w",
                           "ev_pool_b", "ev_pool_scale", "od_gnorm_g", "lb_param")}
    P["meta_full"] = first[1]
    P["conv_w_full"] = first[2].reshape(ev_conv_w.shape[0], CONV_ROWS, -1)

    pending, outs = [], {n: None for n in BIG}

    def advance(after):
        tokens, still = [], []
        for st in pending:
            if st["phase"] == 1:
                bufs = _split_wait(f"reduce_wait_{st['tag']}", _reduce_plan, st["ss"], st["rs"], st["bufs"], after)
                halves = [_sum_pieces(ids2, bufs[2 * f], bufs[2 * f + 1], f"sum_{st['tag']}_{f}") for f in range(len(bufs) // 2)]
                ss, rs, halves, tok = _split_start(f"swap_start_{st['tag']}", _swap_plan, halves, len(halves))
                tokens.append(tok)
                still.append(dict(st, phase=2, ss=ss, rs=rs, bufs=halves))
            else:
                grads = _split_wait(f"swap_wait_{st['tag']}", _swap_plan, st["ss"], st["rs"], st["bufs"], after)
                for (n, i), g in zip(st["keys"], grads):
                    outs[n] = _adamw_layer(w[n], m[n], v[n], g.reshape(w[n].shape[1:]), i, outs[n], f"adamw_{n}_{i}")
        pending[:] = still
        return tokens

    def boundary(tag, grads, after):
        tokens = advance(after)
        bufs = []
        for acc in grads.values():
            bufs += [acc, lax.empty((N_DEV - 1,) + acc.shape[2:], BF16)]
        ss, rs, bufs, tok = _split_start(f"reduce_start_{tag}", _reduce_plan, bufs, 7 * len(grads))
        pending.append(dict(phase=1, tag=tag, keys=list(grads), ss=ss, rs=rs, bufs=bufs))
        return tuple(tokens + [tok])

    loss, dh, small = _local_step(x[0], loss_target[0], P, weights, boundary, first_deps=token)

    order = [n for n in WEIGHTS if n not in BIG]
    block = _pack([small[n] for n in order] + [loss])
    advance((block,))
    packed = _all_sum(block)
    advance((packed,))
    total = _unpack(packed, [small[n].shape for n in order] + [loss.shape])
    loss_sum = total[-1][0, 0]
    gsmall = dict(zip(order, total[:-1]))
    gsmall["meta_tokens"] = lax.dynamic_slice_in_dim(gsmall["meta_tokens"], chip * meta_tokens.shape[1], meta_tokens.shape[1], 1)
    gsmall["ev_conv_w"] = lax.dynamic_slice_in_dim(gsmall["ev_conv_w"][:, :CONV_WIDTH], chip * ev_conv_w.shape[2], ev_conv_w.shape[2], 2)

    g_out, d_out, m_out, v_out = {}, {}, {}, {}
    for n in WEIGHTS:
        if n in BIG:
            g_out[n], d_out[n], m_out[n], v_out[n] = outs[n]
            continue
        shape = w[n].shape
        g = gsmall[n].reshape(shape)
        cols = shape[-1] if len(shape) > 1 else 128
        two = lambda a: a.reshape(-1, cols)
        d_, m_, v_ = _adamw(two(w[n]), two(g), two(m[n]), two(v[n]), f"adamw_{n}")
        g_out[n], d_out[n], m_out[n], v_out[n] = g, d_.reshape(shape), m_.reshape(shape), v_.reshape(shape)

    grad_x = dh[LEAD:][None]
    return (loss_sum, grad_x, *[g_out[n] for n in WEIGHTS], *[d_out[n] for n in WEIGHTS],
            *[m_out[n] for n in WEIGHTS], *[v_out[n] for n in WEIGHTS])
```

```python
import functools

import jax
import jax.numpy as jnp
from jax import lax
from jax.experimental import pallas as pl
from jax.experimental.pallas import tpu as pltpu

F32 = jnp.float32
BF16 = jnp.bfloat16
SDS = jax.ShapeDtypeStruct
MESH = pl.DeviceIdType.MESH
ANY_SPEC = pl.BlockSpec(memory_space=pl.ANY)

N_META = 16
CHUNK = 64
LEAD = CHUNK
PAD = LEAD - N_META
CONV_WIDTH = 31
CONV_ROWS = 32
POOL_WINDOWS = (2, 4, 8, 16)
HEAD_DIM = 128
SUB = 16
EXP_CAP = 80.0
EPS = 1e-6
ADAM_LR = 0.001
ADAM_B1 = 0.9
ADAM_B2 = 0.999
ADAM_EPS = 1e-08
ADAM_WD = 0.01
ADAM_STEP = 10
N_CHIPS = 4
VMEM_LIMIT = 52 << 20
MM_VMEM_BUDGET = 44 << 20


def _params(*sem):
    return pltpu.CompilerParams(dimension_semantics=sem if sem else None, vmem_limit_bytes=VMEM_LIMIT)


def _tile(n, target, unit=CHUNK):
    best = None
    for t in range(unit, min(n, target) + 1, unit):
        if n % t == 0:
            best = t
    assert best is not None, (n, target, unit)
    return best


def _ctile(n, target=512):
    for t in (512, 384, 256, 128):
        if t <= target and n % t == 0:
            return t
    raise ValueError(n)


def _mm_tiles(M, N, per_row, per_col, per_elem):
    best = None
    for tn in (512, 384, 256, 128):
        if N % tn:
            continue
        for tm in sorted((d for d in range(16, M + 1, 16) if M % d == 0), reverse=True):
            if 2 * (tm * per_row + tn * per_col + tm * tn * per_elem) <= MM_VMEM_BUDGET:
                if best is None or tm * tn > best[0] * best[1]:
                    best = (tm, tn)
                break
    assert best is not None, (M, N)
    return best


def _sigmoid(x):
    return 1.0 / (1.0 + jnp.exp(-x))


def _row_ids(shape, base):
    return lax.broadcasted_iota(jnp.int32, shape, 0) + base


def _cast_place(w3, layer, kind, chip1, dtype, name):
    _, ks, ns = w3.shape
    tr = _tile(ks, 512, 16)
    full = (ks, ns * N_CHIPS) if kind == "col" else (ks * N_CHIPS, ns)

    def body(chip_ref, w_ref, o_ref):
        del chip_ref
        o_ref[...] = w_ref[...].astype(dtype)

    omap = (lambda i, chip: (i, chip[0])) if kind == "col" else (lambda i, chip: (chip[0] * (ks // tr) + i, 0))
    return pl.pallas_call(
        body,
        grid_spec=pltpu.PrefetchScalarGridSpec(
            num_scalar_prefetch=1, grid=(ks // tr,),
            in_specs=[pl.BlockSpec((None, tr, ns), lambda i, chip: (layer, i, 0))],
            out_specs=pl.BlockSpec((tr, ns), omap)),
        out_shape=SDS(full, dtype), name=name, compiler_params=_params("parallel"))(chip1, w3)


def _rms_fwd(h, g3, layer, name, deps=()):
    T, D = h.shape
    tm = _tile(T, 832)

    def body(h_ref, g_ref, *rest):
        n_ref = rest[-1]
        x = h_ref[...]
        r = lax.rsqrt(jnp.mean(x * x, axis=-1, keepdims=True) + EPS)
        n_ref[...] = ((x * r) * g_ref[...]).astype(BF16)

    return pl.pallas_call(
        body, grid=(T // tm,),
        in_specs=[pl.BlockSpec((tm, D), lambda i: (i, 0)), pl.BlockSpec((None, 1, D), lambda i: (layer, 0, 0))]
        + [ANY_SPEC] * len(deps),
        out_specs=pl.BlockSpec((tm, D), lambda i: (i, 0)), out_shape=SDS((T, D), BF16),
        name=name, compiler_params=_params("parallel"))(h, g3, *deps)


def _final(h, g2, target):
    T, D = h.shape
    tm = CHUNK

    def body(h_ref, g_ref, t_ref, dh_ref, dhb_ref, dg_ref, loss_ref):
        i = pl.program_id(0)
        x = h_ref[...]
        r = lax.rsqrt(jnp.mean(x * x, axis=-1, keepdims=True) + EPS)
        xh = x * r
        g = g_ref[...]
        live = jnp.where(i > 0, 1.0, 0.0).astype(F32)
        e = ((xh * g) - t_ref[...]) * live
        dy = e * (1.0 / D)
        dxh = dy * g
        dh = r * (dxh - xh * jnp.mean(dxh * xh, axis=-1, keepdims=True))
        dh_ref[...] = dh
        dhb_ref[...] = dh.astype(BF16)

        @pl.when(i == 0)
        def _():
            dg_ref[...] = jnp.zeros_like(dg_ref)
            loss_ref[...] = jnp.zeros_like(loss_ref)

        dg_ref[...] += jnp.sum(dy * xh, axis=0, keepdims=True)
        loss_ref[...] += jnp.sum(e * e) * (0.5 / D)

    row = pl.BlockSpec((tm, D), lambda i: (i, 0))
    return pl.pallas_call(
        body, grid=(T // tm,),
        in_specs=[row, pl.BlockSpec((1, D), lambda i: (0, 0)),
                  pl.BlockSpec((tm, D), lambda i: (jnp.maximum(i - 1, 0), 0))],
        out_specs=[row, row, pl.BlockSpec((1, D), lambda i: (0, 0)), pl.BlockSpec((1, 128), lambda i: (0, 0))],
        out_shape=[SDS((T, D), F32), SDS((T, D), BF16), SDS((1, D), F32), SDS((1, 128), F32)],
        name="final_loss", compiler_params=_params("arbitrary"))(h, g2, target)


def _mm_nn(a, w3, layer, name, res=None, relu2=False):
    M, K = a.shape
    N = w3.shape[2]
    tm, tn = _mm_tiles(M, N, 2 * K, 2 * K, (4 if relu2 else 4) + (4 if res is not None else 0))

    def body(*refs):
        acc = jnp.dot(refs[0][...], refs[1][...], preferred_element_type=F32)
        if res is not None:
            acc = acc + refs[2][...]
        if relu2:
            p = jnp.maximum(acc, 0.0)
            refs[-2][...] = (p * p).astype(BF16)
            refs[-1][...] = p.astype(BF16)
        else:
            refs[-1][...] = acc

    in_specs = [pl.BlockSpec((tm, K), lambda i, j: (i, 0)), pl.BlockSpec((None, K, tn), lambda i, j: (layer, 0, j))]
    args = [a, w3]
    tile = pl.BlockSpec((tm, tn), lambda i, j: (i, j))
    if res is not None:
        in_specs.append(tile)
        args.append(res)
    return pl.pallas_call(
        body, grid=(M // tm, N // tn), in_specs=in_specs, out_specs=[tile, tile] if relu2 else tile,
        out_shape=[SDS((M, N), BF16)] * 2 if relu2 else SDS((M, N), F32),
        name=name, compiler_params=_params("parallel", "parallel"))(*args)


def _mm_nt(dy, w3, layer, name, relu=None, deps=()):
    M, N = dy.shape
    K = w3.shape[1]
    tm, tk = _mm_tiles(M, K, 2 * N, 2 * N, 4)

    def body(*refs):
        acc = lax.dot_general(refs[0][...], refs[1][...], (((1,), (1,)), ((), ())), preferred_element_type=F32)
        if relu is not None:
            acc = (acc * (2.0 * refs[2][...].astype(F32))).astype(BF16)
        refs[-1][...] = acc

    tile = pl.BlockSpec((tm, tk), lambda i, j: (i, j))
    in_specs = [pl.BlockSpec((tm, N), lambda i, j: (i, 0)), pl.BlockSpec((None, tk, N), lambda i, j: (layer, j, 0))]
    args = [dy, w3]
    if relu is not None:
        in_specs.append(tile)
        args.append(relu)
    in_specs += [ANY_SPEC] * len(deps)
    args += list(deps)
    return pl.pallas_call(
        body, grid=(M // tm, K // tk), in_specs=in_specs, out_specs=tile,
        out_shape=SDS((M, K), F32 if relu is None else BF16),
        name=name, compiler_params=_params("parallel", "parallel"))(*args)


def _row_tile(M, per_row, fixed):
    for tm in sorted((d for d in range(16, M + 1, 16) if M % d == 0), reverse=True):
        if 2 * (tm * per_row + fixed) <= MM_VMEM_BUDGET:
            return tm
    raise ValueError((M, per_row, fixed))


def _mm_nn_norm(a, w3, layer, res, g3, glayer, name):
    M, K = a.shape
    D = w3.shape[2]
    tm = _row_tile(M, 2 * K + 10 * D, 2 * K * D)

    def body(a_ref, w_ref, r_ref, g_ref, h_ref, n_ref):
        x = r_ref[...] + jnp.dot(a_ref[...], w_ref[...], preferred_element_type=F32)
        h_ref[...] = x
        r = lax.rsqrt(jnp.mean(x * x, axis=-1, keepdims=True) + EPS)
        n_ref[...] = ((x * r) * g_ref[...]).astype(BF16)

    row = pl.BlockSpec((tm, D), lambda i: (i, 0))
    return pl.pallas_call(
        body, grid=(M // tm,),
        in_specs=[pl.BlockSpec((tm, K), lambda i: (i, 0)), pl.BlockSpec((None, K, D), lambda i: (layer, 0, 0)), row,
                  pl.BlockSpec((None, 1, D), lambda i: (glayer, 0, 0))],
        out_specs=[row, row], out_shape=[SDS((M, D), F32), SDS((M, D), BF16)],
        name=name, compiler_params=_params("parallel"))(a, w3, res, g3)


def _mm_nt_norm(dy, w3, layer, h, g3, glayer, dh_in, name, deps=()):
    M, N = dy.shape
    D = w3.shape[1]
    tm = _row_tile(M, 2 * N + 14 * D, 2 * N * D)

    def body(dy_ref, w_ref, h_ref, g_ref, dhi_ref, *rest):
        dh_ref, dhb_ref, dg_ref = rest[-3:]
        dn = lax.dot_general(dy_ref[...], w_ref[...], (((1,), (1,)), ((), ())), preferred_element_type=F32)
        x = h_ref[...]
        r = lax.rsqrt(jnp.mean(x * x, axis=-1, keepdims=True) + EPS)
        xh = x * r
        dxh = dn * g_ref[...]
        dh = dhi_ref[...] + r * (dxh - xh * jnp.mean(dxh * xh, axis=-1, keepdims=True))
        dh_ref[...] = dh
        dhb_ref[...] = dh.astype(BF16)

        @pl.when(pl.program_id(0) == 0)
        def _():
            dg_ref[...] = jnp.zeros_like(dg_ref)

        dg_ref[...] += jnp.sum(dn * xh, axis=0, keepdims=True)

    row = pl.BlockSpec((tm, D), lambda i: (i, 0))
    return pl.pallas_call(
        body, grid=(M // tm,),
        in_specs=[pl.BlockSpec((tm, N), lambda i: (i, 0)), pl.BlockSpec((None, D, N), lambda i: (layer, 0, 0)), row,
                  pl.BlockSpec((None, 1, D), lambda i: (glayer, 0, 0)), row] + [ANY_SPEC] * len(deps),
        out_specs=[row, row, pl.BlockSpec((1, D), lambda i: (0, 0))],
        out_shape=[SDS((M, D), F32), SDS((M, D), BF16), SDS((1, D), F32)],
        name=name, compiler_params=_params("arbitrary"))(dy, w3, h, g3, dh_in, *deps)


def _fam_dims(kind, K, N):
    return (K // 2, N // N_CHIPS) if kind == "col" else (K // (2 * N_CHIPS), N)


def _mm_tn(x, dy, kind, name):
    M, K = x.shape
    N = dy.shape[1]
    nr, nc = _fam_dims(kind, K, N)
    tk = _ctile(nr)
    tn = _ctile(nc)
    rt, ct = nr // tk, nc // tn

    def body(x_ref, dy_ref, o_ref):
        o_ref[...] = lax.dot_general(x_ref[...], dy_ref[...], (((0,), (0,)), ((), ())),
                                     preferred_element_type=F32).astype(BF16)

    if kind == "col":
        omap = lambda i, j: (i // rt, j // ct, i % rt, j % ct)
    else:
        omap = lambda i, j: ((i // rt) % 2, i // (2 * rt), i % rt, j)
    return pl.pallas_call(
        body, grid=(K // tk, N // tn),
        in_specs=[pl.BlockSpec((M, tk), lambda i, j: (0, i)), pl.BlockSpec((M, tn), lambda i, j: (0, j))],
        out_specs=pl.BlockSpec((None, None, tk, tn), omap),
        out_shape=SDS((2, N_CHIPS, nr, nc), BF16),
        name=name, compiler_params=_params("parallel", "parallel"))(x, dy)


C_EVEN = 512


def _live(rows, base, total):
    r = _row_ids((rows, 1), base)
    return jnp.logical_and(r >= PAD, r < total).astype(F32)


def _conv_taps(win, w_ref, ls, acc, flip):
    for b in range(8):
        rb = win if b == 0 else pltpu.roll(win, 96 - b, 0)
        for a in range(5):
            o = 8 * a + b
            tap = (30 - o) if flip else (o - 2)
            if 0 <= tap < CONV_WIDTH:
                acc = acc + w_ref[pl.ds(tap, 1), ls] * rb[8 * a:8 * a + CHUNK]
    return acc


def _window_sum(win, levels, forward):
    s = win
    n = win.shape[0]
    for k in range(levels):
        step = 1 << k
        s = s + pltpu.roll(s, (n - step) if forward else step, 0)
    return s


def _pool_count(base, g):
    pos = _row_ids((CHUNK, 1), base) - PAD
    return jnp.clip(pos + 1, 1, POOL_WINDOWS[g]).astype(F32)


def _even_fwd(u, cw3, cb3, lg3, lb3, pw4, pb3, ps3, j, name):
    T = u.shape[0]
    C = C_EVEN
    tm = _tile(T, 320)
    nch = tm // CHUNK
    nblk = T // CHUNK

    def body(u_ref, up_ref, cw_ref, cb_ref, lg_ref, lb_ref, pw_ref, pb_ref, ps_ref, o_ref, a_s, p_s, yc_s):
        row0 = pl.program_id(0) * tm
        up = up_ref[...]
        lp = _live(CHUNK, row0 - CHUNK, T)
        a_s[0:CHUNK, :] = up[:, 0:C] * _sigmoid(up[:, C:2 * C]) * lp
        p_s[0:CHUNK, :] = up[:, 2 * C:3 * C] * lp

        def stage(c, _):
            rs = pl.multiple_of(c * CHUNK, CHUNK)
            lv = _live(CHUNK, row0 + rs, T)
            a_s[pl.ds(rs + CHUNK, CHUNK), :] = u_ref[pl.ds(rs, CHUNK), 0:C] * _sigmoid(u_ref[pl.ds(rs, CHUNK), C:2 * C]) * lv
            p_s[pl.ds(rs + CHUNK, CHUNK), :] = u_ref[pl.ds(rs, CHUNK), 2 * C:3 * C] * lv
            return 0

        lax.fori_loop(0, nch, stage, 0)

        def chunk(c, _):
            rs = pl.multiple_of(c * CHUNK, CHUNK)
            lv = _live(CHUNK, row0 + rs, T)
            for cb in range(4):
                ls = slice(cb * 128, (cb + 1) * 128)
                win = a_s[pl.ds(pl.multiple_of(rs + 32, 32), 96), ls]
                acc = jnp.broadcast_to(cb_ref[:, ls], (CHUNK, 128))
                yc_s[:, ls] = _conv_taps(win, cw_ref, ls, acc, False)
            y = yc_s[...]
            xc = y - jnp.mean(y, axis=-1, keepdims=True)
            yn = xc * lax.rsqrt(jnp.mean(xc * xc, axis=-1, keepdims=True) + EPS) * lg_ref[...] + lb_ref[...]
            o_ref[pl.ds(rs, CHUNK), 0:C] = (yn * _sigmoid(yn) * lv).astype(BF16)
            for g in range(4):
                ls = slice(g * 128, (g + 1) * 128)
                win = p_s[pl.ds(pl.multiple_of(rs + 48, 16), 80), ls]
                s = _window_sum(win, g + 1, False)
                d = s[16:80] / _pool_count(row0 + rs, g) - win[16:80]
                yv = jnp.dot(d.astype(BF16), pw_ref[g].astype(BF16), preferred_element_type=F32) + pb_ref[:, ls]
                o_ref[pl.ds(rs, CHUNK), C + g * 128:C + (g + 1) * 128] = (yv * ps_ref[:, ls] * lv).astype(BF16)
            return 0

        lax.fori_loop(0, nch, chunk, 0)

    vec = pl.BlockSpec((None, 1, C), lambda i: (j, 0, 0))
    return pl.pallas_call(
        body, grid=(T // tm,),
        in_specs=[pl.BlockSpec((tm, 3 * C), lambda i: (i, 0)),
                  pl.BlockSpec((CHUNK, 3 * C), lambda i: (jnp.maximum(i * nch - 1, 0), 0)),
                  pl.BlockSpec((None, CONV_ROWS, C), lambda i: (j, 0, 0)), vec, vec, vec,
                  pl.BlockSpec((None, 4, 128, 128), lambda i: (j, 0, 0, 0)), vec, vec],
        out_specs=pl.BlockSpec((tm, 2 * C), lambda i: (i, 0)),
        out_shape=SDS((T, 2 * C), BF16),
        scratch_shapes=[pltpu.VMEM((tm + CHUNK, C), F32), pltpu.VMEM((tm + CHUNK, C), F32), pltpu.VMEM((CHUNK, C), F32)],
        name=name, compiler_params=_params("parallel"))(u, u, cw3, cb3, lg3, lb3, pw4, pb3, ps3)


def _even_bwd(u, dy, cw3, cb3, lg3, lb3, pw4, pb3, ps3, j, name):
    T = u.shape[0]
    C = C_EVEN
    tm = _tile(T, 320)
    nch = tm // CHUNK
    nblk = T // CHUNK
    ntile = T // tm

    def body(u_ref, up_ref, un_ref, dy_ref, dyn_ref, cw_ref, cb_ref, lg_ref, lb_ref, pw_ref, pb_ref, ps_ref,
             du_ref, dcw_ref, dcb_ref, dlg_ref, dlb_ref, dpw_ref, dpb_ref, dps_ref,
             a_s, p_s, dy_s, yc_s, dyc_s, dd_s, ddc_s, dw_s):
        i = pl.program_id(0)
        row0 = i * tm

        @pl.when(i == 0)
        def _():
            for ref in (dcb_ref, dlg_ref, dlb_ref, dpw_ref, dpb_ref, dps_ref, dw_s):
                ref[...] = jnp.zeros_like(ref)

        up = up_ref[...]
        lp = _live(CHUNK, row0 - CHUNK, T)
        a_s[0:CHUNK, :] = up[:, 0:C] * _sigmoid(up[:, C:2 * C]) * lp
        p_s[0:CHUNK, :] = up[:, 2 * C:3 * C] * lp
        un = un_ref[...]
        ln_ = _live(CHUNK, row0 + tm, T)
        a_s[tm + CHUNK:tm + 2 * CHUNK, :] = un[:, 0:C] * _sigmoid(un[:, C:2 * C]) * ln_
        p_s[tm + CHUNK:tm + 2 * CHUNK, :] = un[:, 2 * C:3 * C] * ln_
        dy_s[tm:tm + CHUNK, :] = dyn_ref[...] * ln_
        dyc_s[tm + CHUNK:tm + CHUNK + 32, :] = jnp.zeros((32, C), F32)

        def stage(c, _):
            rs = pl.multiple_of(c * CHUNK, CHUNK)
            lv = _live(CHUNK, row0 + rs, T)
            a_s[pl.ds(rs + CHUNK, CHUNK), :] = u_ref[pl.ds(rs, CHUNK), 0:C] * _sigmoid(u_ref[pl.ds(rs, CHUNK), C:2 * C]) * lv
            p_s[pl.ds(rs + CHUNK, CHUNK), :] = u_ref[pl.ds(rs, CHUNK), 2 * C:3 * C] * lv
            dy_s[pl.ds(rs, CHUNK), :] = dy_ref[pl.ds(rs, CHUNK), :] * lv
            return 0

        lax.fori_loop(0, nch, stage, 0)

        def first(c, _):
            rs = pl.multiple_of(c * CHUNK, CHUNK)
            own = jnp.where(c < nch, 1.0, 0.0).astype(F32)
            for cb in range(4):
                ls = slice(cb * 128, (cb + 1) * 128)
                win = a_s[pl.ds(pl.multiple_of(rs + 32, 32), 96), ls]
                acc = jnp.broadcast_to(cb_ref[:, ls], (CHUNK, 128))
                yc_s[:, ls] = _conv_taps(win, cw_ref, ls, acc, False)
            y = yc_s[...]
            xc = y - jnp.mean(y, axis=-1, keepdims=True)
            rstd = lax.rsqrt(jnp.mean(xc * xc, axis=-1, keepdims=True) + EPS)
            xh = xc * rstd
            yn = xh * lg_ref[...] + lb_ref[...]
            sg = _sigmoid(yn)
            dyn = dy_s[pl.ds(rs, CHUNK), 0:C] * (sg * (1.0 + yn * (1.0 - sg)))
            dlg_ref[...] += jnp.sum(dyn * xh, axis=0, keepdims=True) * own
            dlb_ref[...] += jnp.sum(dyn, axis=0, keepdims=True) * own
            dxh = dyn * lg_ref[...]
            dyc = rstd * (dxh - jnp.mean(dxh, axis=-1, keepdims=True) - xh * jnp.mean(dxh * xh, axis=-1, keepdims=True))
            dyc_s[pl.ds(rs, CHUNK), :] = dyc
            dcb_ref[...] += jnp.sum(dyc, axis=0, keepdims=True) * own
            for g in range(4):
                ls = slice(g * 128, (g + 1) * 128)
                win = p_s[pl.ds(pl.multiple_of(rs + 48, 16), 80), ls]
                s = _window_sum(win, g + 1, False)
                cnt = _pool_count(row0 + rs, g)
                d = (s[16:80] / cnt - win[16:80]).astype(BF16)
                w = pw_ref[g].astype(BF16)
                pre = jnp.dot(d, w, preferred_element_type=F32) + pb_ref[:, ls]
                dyb = dy_s[pl.ds(rs, CHUNK), C + g * 128:C + (g + 1) * 128]
                dpre = dyb * ps_ref[:, ls]
                dps_ref[:, ls] += jnp.sum(dyb * pre, axis=0, keepdims=True) * own
                dpb_ref[:, ls] += jnp.sum(dpre, axis=0, keepdims=True) * own
                dpre_b = (dpre * own).astype(BF16)
                dpw_ref[g] += lax.dot_general(d, dpre_b, (((0,), (0,)), ((), ())), preferred_element_type=F32)
                dd = lax.dot_general(dpre.astype(BF16), w, (((1,), (1,)), ((), ())), preferred_element_type=F32)
                dd_s[pl.ds(rs, CHUNK), ls] = dd
                ddc_s[pl.ds(rs, CHUNK), ls] = dd / cnt
            return 0

        lax.fori_loop(0, nch + 1, first, 0)
        ddc_s[tm + CHUNK:tm + CHUNK + 16, :] = jnp.zeros((16, C), F32)

        def second(c, _):
            rs = pl.multiple_of(c * CHUNK, CHUNK)
            lv = _live(CHUNK, row0 + rs, T)
            for cb in range(4):
                ls = slice(cb * 128, (cb + 1) * 128)
                wd = dyc_s[pl.ds(rs, 96), ls]
                da = _conv_taps(wd, cw_ref, ls, jnp.zeros((CHUNK, 128), F32), True)
                wa = a_s[pl.ds(pl.multiple_of(rs + 32, 32), 96), ls]
                dyc = dyc_s[pl.ds(rs, CHUNK), ls]
                for b in range(8):
                    rb = wa if b == 0 else pltpu.roll(wa, 96 - b, 0)
                    for a in range(5):
                        tap = 8 * a + b - 2
                        if 0 <= tap < CONV_WIDTH:
                            prod = dyc * rb[8 * a:8 * a + CHUNK]
                            part = prod[0:8]
                            for q in range(1, 8):
                                part = part + prod[8 * q:8 * q + 8]
                            dw_s[8 * tap:8 * tap + 8, ls] += part
                val = u_ref[pl.ds(rs, CHUNK), ls]
                sg = _sigmoid(u_ref[pl.ds(rs, CHUNK), C + cb * 128:C + (cb + 1) * 128])
                du_ref[pl.ds(rs, CHUNK), ls] = (da * sg * lv).astype(BF16)
                du_ref[pl.ds(rs, CHUNK), C + cb * 128:C + (cb + 1) * 128] = (da * val * sg * (1.0 - sg) * lv).astype(BF16)
            for g in range(4):
                ls = slice(g * 128, (g + 1) * 128)
                z = _window_sum(ddc_s[pl.ds(rs, 80), ls], g + 1, True)
                dpin = (z[0:CHUNK] - dd_s[pl.ds(rs, CHUNK), ls]) * lv
                du_ref[pl.ds(rs, CHUNK), 2 * C + g * 128:2 * C + (g + 1) * 128] = dpin.astype(BF16)
            return 0

        lax.fori_loop(0, nch, second, 0)

        @pl.when(i == ntile - 1)
        def _():
            for tap in range(CONV_WIDTH):
                dcw_ref[tap:tap + 1, :] = jnp.sum(dw_s[8 * tap:8 * tap + 8, :], axis=0, keepdims=True)
            dcw_ref[CONV_WIDTH:CONV_ROWS, :] = jnp.zeros((CONV_ROWS - CONV_WIDTH, C), F32)

    vec = pl.BlockSpec((None, 1, C), lambda i: (j, 0, 0))
    ovec = pl.BlockSpec((1, C), lambda i: (0, 0))
    return pl.pallas_call(
        body, grid=(ntile,),
        in_specs=[pl.BlockSpec((tm, 3 * C), lambda i: (i, 0)),
                  pl.BlockSpec((CHUNK, 3 * C), lambda i: (jnp.maximum(i * nch - 1, 0), 0)),
                  pl.BlockSpec((CHUNK, 3 * C), lambda i: (jnp.minimum((i + 1) * nch, nblk - 1), 0)),
                  pl.BlockSpec((tm, 2 * C), lambda i: (i, 0)),
                  pl.BlockSpec((CHUNK, 2 * C), lambda i: (jnp.minimum((i + 1) * nch, nblk - 1), 0)),
                  pl.BlockSpec((None, CONV_ROWS, C), lambda i: (j, 0, 0)), vec, vec, vec,
                  pl.BlockSpec((None, 4, 128, 128), lambda i: (j, 0, 0, 0)), vec, vec],
        out_specs=[pl.BlockSpec((tm, 3 * C), lambda i: (i, 0)), pl.BlockSpec((CONV_ROWS, C), lambda i: (0, 0)),
                   ovec, ovec, ovec, pl.BlockSpec((4, 128, 128), lambda i: (0, 0, 0)), ovec, ovec],
        out_shape=[SDS((T, 3 * C), BF16), SDS((CONV_ROWS, C), F32), SDS((1, C), F32), SDS((1, C), F32), SDS((1, C), F32),
                   SDS((4, 128, 128), F32), SDS((1, C), F32), SDS((1, C), F32)],
        scratch_shapes=[pltpu.VMEM((tm + 2 * CHUNK, C), F32), pltpu.VMEM((tm + 2 * CHUNK, C), F32),
                        pltpu.VMEM((tm + CHUNK, 2 * C), F32), pltpu.VMEM((CHUNK, C), F32),
                        pltpu.VMEM((tm + CHUNK + 32, C), F32), pltpu.VMEM((tm + CHUNK, C), F32),
                        pltpu.VMEM((tm + CHUNK + 16, C), F32), pltpu.VMEM((8 * CONV_ROWS, C), F32)],
        name=name, compiler_params=_params("arbitrary"))(u, u, u, dy, dy, cw3, cb3, lg3, lb3, pw4, pb3, ps3)


HI = lax.Precision.HIGHEST


def _dot_nt(a, b):
    return lax.dot_general(a, b, (((1,), (1,)), ((), ())), preferred_element_type=F32)


def _dot_tn(a, b):
    return lax.dot_general(a, b, (((0,), (0,)), ((), ())), preferred_element_type=F32)


def _tri(lower):
    r = lax.broadcasted_iota(jnp.int32, (CHUNK, CHUNK), 0)
    c = lax.broadcasted_iota(jnp.int32, (CHUNK, CHUNK), 1)
    return jnp.where((c <= r) if lower else (c >= r), 1.0, 0.0).astype(F32)


def _hgrn_gates(u_ref, lb_ref, h, D, lv):
    ls = slice(h * HEAD_DIM, (h + 1) * HEAD_DIM)
    qraw = u_ref[:, ls]
    fraw = u_ref[:, D + h * HEAD_DIM:D + (h + 1) * HEAD_DIM]
    v = u_ref[:, 2 * D + h * HEAD_DIM:2 * D + (h + 1) * HEAD_DIM] * lv
    lbv = lb_ref[:, ls]
    sig = _sigmoid(fraw)
    forget = lbv + (1.0 - lbv) * sig
    logf = jnp.log(forget) * lv
    k = (1.0 - forget) * lv
    qsig = _sigmoid(qraw)
    q = qraw * qsig * lv
    return q, k, v, logf, (qraw, qsig, sig, forget, lbv)


def _sub_parts(q, k, b, b_s, I):
    rows = slice(SUB * I, SUB * (I + 1))
    rho = jnp.zeros((1, HEAD_DIM), F32) if I == 0 else b_s[SUB * I - 1:SUB * I, :]
    eI = jnp.exp(b[rows] - rho)
    EI = jnp.exp(jnp.minimum(rho - b, EXP_CAP))
    causal = (lax.broadcasted_iota(jnp.int32, (SUB, CHUNK), 1)
              <= lax.broadcasted_iota(jnp.int32, (SUB, CHUNK), 0) + SUB * I)
    return rows, q[rows] * eI, k * EI, eI, EI, causal


def _hgrn_fwd(u, lb3, layer, gn3, j, name):
    T = u.shape[0]
    D = u.shape[1] // 4
    H = D // HEAD_DIM
    NC = T // CHUNK

    def body(u_ref, lb_ref, gn_ref, y_ref, o_ref, sall_ref, st_s, b_s, lf_s, q_s, k_s):
        n = pl.program_id(0)

        @pl.when(n == 0)
        def _():
            st_s[...] = jnp.zeros_like(st_s)

        lv = _live(CHUNK, n * CHUNK, T)
        heads = range(H)
        cols = [slice(h * HEAD_DIM, (h + 1) * HEAD_DIM) for h in heads]
        vb = []
        for h in heads:
            q, k, v, logf, _ = _hgrn_gates(u_ref, lb_ref, h, D, lv)
            q_s[:, cols[h]] = q
            k_s[:, cols[h]] = k
            lf_s[:, cols[h]] = logf
            vb.append(v.astype(BF16))
        b_s[...] = jnp.dot(_tri(True), lf_s[...], precision=HI, preferred_element_type=F32)
        ops = []
        for h in heads:
            b_h = b_s.at[:, cols[h]]
            b = b_h[...]
            q = q_s[:, cols[h]]
            k = k_s[:, cols[h]]
            blast = b_h[CHUNK - 1:CHUNK, :]
            qh = (q * jnp.exp(b)).astype(BF16)
            kt = (k * jnp.exp(blast - b)).astype(BF16)
            subs = []
            for I in range(CHUNK // SUB):
                _, qI, KI, _, _, causal = _sub_parts(q, k, b, b_h, I)
                subs.append((qI.astype(BF16), KI.astype(BF16), causal))
            ops.append((qh, kt, jnp.exp(blast), subs))
        mm = []
        for h in heads:
            qh, kt, eblast, subs = ops[h]
            st = st_s[h]
            sall_ref[h] = st
            o_inter = _dot_nt(qh, st.astype(BF16))
            st_s[h] = st * eblast + _dot_tn(vb[h], kt)
            mm.append((o_inter, [_dot_nt(qI, KI) for qI, KI, _ in subs]))
        for h in heads:
            o_inter, ps = mm[h]
            p = jnp.concatenate([jnp.where(c, x, 0.0) for x, (_, _, c) in zip(ps, ops[h][3])], axis=0).astype(BF16)
            o = o_inter + jnp.dot(p, vb[h], preferred_element_type=F32)
            o_ref[:, cols[h]] = o
            graw = u_ref[:, 3 * D + h * HEAD_DIM:3 * D + (h + 1) * HEAD_DIM]
            r = lax.rsqrt(jnp.mean(o * o, axis=-1, keepdims=True) + EPS)
            y_ref[:, cols[h]] = (((o * r) * gn_ref[...]) * (graw * _sigmoid(graw))).astype(BF16)

    return pl.pallas_call(
        body, grid=(NC,),
        in_specs=[pl.BlockSpec((CHUNK, 4 * D), lambda n: (n, 0)),
                  pl.BlockSpec((None, 1, D), lambda n: (layer, 0, 0)),
                  pl.BlockSpec((None, 1, HEAD_DIM), lambda n: (j, 0, 0))],
        out_specs=[pl.BlockSpec((CHUNK, D), lambda n: (n, 0)), pl.BlockSpec((CHUNK, D), lambda n: (n, 0)),
                   pl.BlockSpec((None, H, HEAD_DIM, HEAD_DIM), lambda n: (n, 0, 0, 0))],
        out_shape=[SDS((T, D), BF16), SDS((T, D), F32), SDS((NC, H, HEAD_DIM, HEAD_DIM), F32)],
        scratch_shapes=[pltpu.VMEM((H, HEAD_DIM, HEAD_DIM), F32)] + [pltpu.VMEM((CHUNK, D), F32)] * 4,
        name=name, compiler_params=_params("arbitrary"))(u, lb3, gn3)


def _hgrn_bwd(u, o_raw, dy, sall, lb3, layer, gn3, j, name):
    T = u.shape[0]
    D = u.shape[1] // 4
    H = D // HEAD_DIM
    NC = T // CHUNK

    def body(u_ref, o_ref, dy_ref, sall_ref, lb_ref, gn_ref, du_ref, dlb_ref, dgn_ref, dst_s, b_s, lf_s, q_s, k_s, db_s, dk_s):
        step = pl.program_id(0)
        n = NC - 1 - step

        @pl.when(step == 0)
        def _():
            dst_s[...] = jnp.zeros_like(dst_s)
            dlb_ref[...] = jnp.zeros_like(dlb_ref)
            dgn_ref[...] = jnp.zeros_like(dgn_ref)

        lv = _live(CHUNK, n * CHUNK, T)
        last_row = (_row_ids((CHUNK, 1), 0) == CHUNK - 1).astype(F32)
        gn = gn_ref[...]
        heads = range(H)
        cols = [slice(h * HEAD_DIM, (h + 1) * HEAD_DIM) for h in heads]
        vb, dob = [], []
        dgn = jnp.zeros((1, HEAD_DIM), F32)
        for h in heads:
            q, k, v, logf, _ = _hgrn_gates(u_ref, lb_ref, h, D, lv)
            q_s[:, cols[h]] = q
            k_s[:, cols[h]] = k
            lf_s[:, cols[h]] = logf
            vb.append(v.astype(BF16))
            graw = u_ref[:, 3 * D + h * HEAD_DIM:3 * D + (h + 1) * HEAD_DIM]
            gsig = _sigmoid(graw)
            o = o_ref[:, cols[h]]
            r = lax.rsqrt(jnp.mean(o * o, axis=-1, keepdims=True) + EPS)
            xh = o * r
            dyv = dy_ref[:, cols[h]]
            dsg = dyv * (graw * gsig)
            dgn = dgn + jnp.sum(dsg * xh, axis=0, keepdims=True)
            dxh = dsg * gn
            do = r * (dxh - xh * jnp.mean(dxh * xh, axis=-1, keepdims=True))
            dob.append(do.astype(BF16))
            dgraw = dyv * xh * gn * (gsig * (1.0 + graw * (1.0 - gsig)))
            du_ref[:, 3 * D + h * HEAD_DIM:3 * D + (h + 1) * HEAD_DIM] = (dgraw * lv).astype(BF16)
        dgn_ref[...] += dgn
        b_s[...] = jnp.dot(_tri(True), lf_s[...], precision=HI, preferred_element_type=F32)
        ops = []
        for h in heads:
            b_h = b_s.at[:, cols[h]]
            b = b_h[...]
            q = q_s[:, cols[h]]
            k = k_s[:, cols[h]]
            blast = b_h[CHUNK - 1:CHUNK, :]
            eb = jnp.exp(b)
            ekb = jnp.exp(blast - b)
            subs = []
            for I in range(CHUNK // SUB):
                rows, qI, KI, eI, EI, causal = _sub_parts(q, k, b, b_h, I)
                subs.append((rows, qI.astype(BF16), KI.astype(BF16), eI, EI, causal))
            ops.append((eb, ekb, jnp.exp(blast), (q * eb).astype(BF16), (k * ekb).astype(BF16), subs))
        mm = []
        for h in heads:
            eb, ekb, eblast, qhb, ktb, subs = ops[h]
            st = sall_ref[h]
            dst = dst_s[h]
            dstb = dst.astype(BF16)
            dv = _dot_nt(ktb, dstb)
            dqh = jnp.dot(dob[h], st.astype(BF16), preferred_element_type=F32)
            dkt = jnp.dot(vb[h], dstb, preferred_element_type=F32)
            dblast = jnp.sum(dst * st, axis=0, keepdims=True) * eblast
            dst_s[h] = dst * eblast + _dot_tn(dob[h], qhb)
            dp_full = _dot_nt(dob[h], vb[h])
            ps = [_dot_nt(qIb, KIb) for _, qIb, KIb, _, _, _ in subs]
            mm.append((dv, dqh, dkt, dblast, dp_full, ps))
        for h in heads:
            eb, ekb, eblast, qhb, ktb, subs = ops[h]
            dv, dqh, dkt, dblast, dp_full, ps = mm[h]
            p = jnp.concatenate([jnp.where(sub[5], x, 0.0) for x, sub in zip(ps, subs)], axis=0).astype(BF16)
            dv = dv + _dot_tn(p, dob[h])
            du_ref[:, 2 * D + h * HEAD_DIM:2 * D + (h + 1) * HEAD_DIM] = (dv * lv).astype(BF16)
            dq = dqh * eb
            db = dqh * qhb.astype(F32)
            tmp = dkt * ktb.astype(F32)
            dk = dkt * ekb
            db = db - tmp
            dblast = dblast + jnp.sum(tmp, axis=0, keepdims=True)
            dq_parts, db_parts = [], []
            for rows, qIb, KIb, eI, EI, causal in subs:
                dp = jnp.where(causal, dp_full[rows], 0.0).astype(BF16)
                dqI = jnp.dot(dp, KIb, preferred_element_type=F32)
                dKI = _dot_tn(dp, qIb)
                dq_parts.append(dqI * eI)
                db_parts.append(dqI * qIb.astype(F32))
                dk = dk + dKI * EI
                db = db - dKI * KIb.astype(F32)
            dq = dq + jnp.concatenate(dq_parts, axis=0)
            db_s[:, cols[h]] = db + jnp.concatenate(db_parts, axis=0) + last_row * dblast
            dk_s[:, cols[h]] = dk
            qraw = u_ref[:, cols[h]]
            qsig = _sigmoid(qraw)
            du_ref[:, cols[h]] = (dq * (qsig * (1.0 + qraw * (1.0 - qsig))) * lv).astype(BF16)
        lf_s[...] = jnp.dot(_tri(False), db_s[...], precision=HI, preferred_element_type=F32)
        for h in heads:
            fraw = u_ref[:, D + h * HEAD_DIM:D + (h + 1) * HEAD_DIM]
            lbv = lb_ref[:, cols[h]]
            sig = _sigmoid(fraw)
            forget = lbv + (1.0 - lbv) * sig
            dforget = (lf_s[:, cols[h]] / forget - dk_s[:, cols[h]]) * lv
            dlb_ref[:, cols[h]] += jnp.sum(dforget * (1.0 - sig), axis=0, keepdims=True)
            du_ref[:, D + h * HEAD_DIM:D + (h + 1) * HEAD_DIM] = (dforget * (1.0 - lbv) * sig * (1.0 - sig)).astype(BF16)

    rev = lambda s: (NC - 1 - s, 0)
    return pl.pallas_call(
        body, grid=(NC,),
        in_specs=[pl.BlockSpec((CHUNK, 4 * D), rev), pl.BlockSpec((CHUNK, D), rev), pl.BlockSpec((CHUNK, D), rev),
                  pl.BlockSpec((None, H, HEAD_DIM, HEAD_DIM), lambda s: (NC - 1 - s, 0, 0, 0)),
                  pl.BlockSpec((None, 1, D), lambda s: (layer, 0, 0)),
                  pl.BlockSpec((None, 1, HEAD_DIM), lambda s: (j, 0, 0))],
        out_specs=[pl.BlockSpec((CHUNK, 4 * D), rev), pl.BlockSpec((1, D), lambda s: (0, 0)),
                   pl.BlockSpec((1, HEAD_DIM), lambda s: (0, 0))],
        out_shape=[SDS((T, 4 * D), BF16), SDS((1, D), F32), SDS((1, HEAD_DIM), F32)],
        scratch_shapes=[pltpu.VMEM((H, HEAD_DIM, HEAD_DIM), F32)] + [pltpu.VMEM((CHUNK, D), F32)] * 6,
        name=name, compiler_params=_params("arbitrary"))(u, o_raw, dy, sall, lb3, gn3)


def _softmax_layers(p_ref, n_layers):
    rows = [p_ref[l:l + 1, :] for l in range(n_layers)]
    m = functools.reduce(jnp.maximum, rows)
    e = [jnp.exp(x - m) for x in rows]
    tot = functools.reduce(lambda a, b: a + b, e)
    return [x / tot for x in e]


def _lb_fwd(p):
    n_layers, D = p.shape

    def body(p_ref, o_ref):
        s = _softmax_layers(p_ref, n_layers)
        acc = jnp.zeros((1, D), F32)
        o_ref[0:1, :] = acc
        for l in range(1, n_layers):
            acc = acc + s[l]
            o_ref[l:l + 1, :] = acc

    return pl.pallas_call(body, out_shape=SDS(p.shape, F32), name="lb_fwd")(p)


def _lb_bwd(p, dlb):
    n_layers, D = p.shape

    def body(p_ref, d_ref, o_ref):
        s = _softmax_layers(p_ref, n_layers)
        ds = [jnp.zeros((1, D), F32)] * n_layers
        acc = jnp.zeros((1, D), F32)
        for l in range(n_layers - 1, 0, -1):
            acc = acc + d_ref[l:l + 1, :]
            ds[l] = acc
        dot = functools.reduce(lambda a, b: a + b, [s[l] * ds[l] for l in range(n_layers)])
        for l in range(n_layers):
            o_ref[l:l + 1, :] = s[l] * (ds[l] - dot)

    return pl.pallas_call(body, out_shape=SDS(p.shape, F32), name="lb_bwd")(p, dlb)


def _adamw(w, g, m, v, name):
    R, C = w.shape
    tr = _tile(R, 256, 8) if R % 8 == 0 else R

    def body(w_ref, g_ref, m_ref, v_ref, d_ref, mo_ref, vo_ref):
        g_ = g_ref[...]
        m_ = ADAM_B1 * m_ref[...] + (1.0 - ADAM_B1) * g_
        v_ = ADAM_B2 * v_ref[...] + (1.0 - ADAM_B2) * (g_ * g_)
        mh = m_ / (1.0 - ADAM_B1 ** ADAM_STEP)
        vh = v_ / (1.0 - ADAM_B2 ** ADAM_STEP)
        d_ref[...] = -ADAM_LR * (mh / (jnp.sqrt(vh) + ADAM_EPS) + ADAM_WD * w_ref[...])
        mo_ref[...] = m_
        vo_ref[...] = v_

    blk = pl.BlockSpec((tr, C), lambda i: (i, 0))
    return pl.pallas_call(
        body, grid=(R // tr,), in_specs=[blk] * 4, out_specs=[blk] * 3, out_shape=[SDS((R, C), F32)] * 3,
        name=name, compiler_params=_params("parallel"))(w, g, m, v)


def _adamw_layer(w3, m3, v3, g2, layer, outs, name):
    L, R, C = w3.shape
    tr = _tile(R, 256, 8)
    if outs is None:
        outs = tuple(lax.empty(w3.shape, F32) for _ in range(4))

    def body(w_ref, m_ref, v_ref, g_ref, a0, a1, a2, a3, go_ref, d_ref, mo_ref, vo_ref):
        del a0, a1, a2, a3
        g_ = g_ref[...]
        m_ = ADAM_B1 * m_ref[...] + (1.0 - ADAM_B1) * g_
        v_ = ADAM_B2 * v_ref[...] + (1.0 - ADAM_B2) * (g_ * g_)
        mh = m_ / (1.0 - ADAM_B1 ** ADAM_STEP)
        vh = v_ / (1.0 - ADAM_B2 ** ADAM_STEP)
        go_ref[...] = g_
        d_ref[...] = -ADAM_LR * (mh / (jnp.sqrt(vh) + ADAM_EPS) + ADAM_WD * w_ref[...])
        mo_ref[...] = m_
        vo_ref[...] = v_

    lay = pl.BlockSpec((None, tr, C), lambda i: (layer, i, 0))
    return pl.pallas_call(
        body, grid=(R // tr,), in_specs=[lay] * 3 + [pl.BlockSpec((tr, C), lambda i: (i, 0))] + [ANY_SPEC] * 4,
        out_specs=[lay] * 4, out_shape=[SDS(w3.shape, F32)] * 4, input_output_aliases={4: 0, 5: 1, 6: 2, 7: 3},
        name=name, compiler_params=_params("parallel"))(w3, m3, v3, g2, *outs)


SEM_SPEC = pl.BlockSpec(memory_space=pltpu.SEMAPHORE)
HBM_SPEC = pl.BlockSpec(memory_space=pltpu.HBM)
EFFECT = pltpu.SideEffectType.DATAFLOW_SIDE_EFFECTING
N_DEV = 2 * N_CHIPS


def _position():
    x, y, c = lax.axis_index("x"), lax.axis_index("y"), lax.axis_index("c")
    chips = [(1 - x, y), (x, 1 - y), (1 - x, 1 - y)]
    return x, y, c, chips


def _split_start(name, plan, bufs, n_sems, deps=(), earlier=None):
    n = len(bufs)
    held = () if earlier is None else tuple(earlier[1:])

    def body(*refs):
        first_out = n + len(held) + len(deps)
        if earlier is not None:
            sends, recvs = earlier[0](refs[:n], refs[n], refs[n + 1])
            for kw in sends:
                pltpu.make_async_remote_copy(**kw).wait_send()
            for kw in recvs:
                pltpu.make_async_remote_copy(**kw).wait_recv()
        sends, _ = plan(refs[:n], refs[first_out], refs[first_out + 1])
        for kw in sends:
            pltpu.make_async_remote_copy(**kw).start()
        refs[-1][...] = jnp.zeros_like(refs[-1])

    out = pl.pallas_call(
        body, name=name,
        out_shape=(pltpu.SemaphoreType.DMA((n_sems,)), pltpu.SemaphoreType.DMA((n_sems,)),
                   *[pltpu.HBM(b.shape, b.dtype) for b in bufs], SDS((8, 128), F32)),
        in_specs=[HBM_SPEC] * n + [SEM_SPEC] * len(held) + [ANY_SPEC] * len(deps),
        out_specs=(SEM_SPEC, SEM_SPEC, *[HBM_SPEC] * n, pl.BlockSpec(memory_space=pltpu.VMEM)),
        input_output_aliases={i: 2 + i for i in range(n)},
        compiler_params=pltpu.CompilerParams(has_side_effects=EFFECT),
    )(*[pltpu.with_memory_space_constraint(b, pltpu.HBM) for b in bufs], *held, *deps)
    return out[0], out[1], list(out[2:2 + n]), out[-1]


def _split_wait(name, plan, send_sems, recv_sems, bufs, after=()):
    n = len(bufs)

    def body(*refs):
        sends, recvs = plan(refs[:n], refs[n], refs[n + 1])
        for kw in sends:
            pltpu.make_async_remote_copy(**kw).wait_send()
        for kw in recvs:
            pltpu.make_async_remote_copy(**kw).wait_recv()

    out = pl.pallas_call(
        body, name=name, out_shape=tuple(pltpu.HBM(b.shape, b.dtype) for b in bufs),
        in_specs=[HBM_SPEC] * n + [SEM_SPEC, SEM_SPEC] + [ANY_SPEC] * len(after),
        out_specs=tuple([HBM_SPEC] * n), input_output_aliases={i: i for i in range(n)},
        compiler_params=pltpu.CompilerParams(has_side_effects=EFFECT),
    )(*bufs, send_sems, recv_sems, *after)
    return list(out)


def _region(kind, ref, chip, half):
    K, N = ref.shape
    if kind == "col":
        return ref.at[pl.ds(half * (K // 2), K // 2), pl.ds(chip * (N // N_CHIPS), N // N_CHIPS)]
    rows = K // (2 * N_CHIPS)
    return ref.at[pl.ds((2 * chip + half) * rows, rows), :]


def _gather_plan(kinds, over_chips):
    def plan(refs, send_sems, recv_sems):
        x, y, c, chips = _position()
        sends, recvs = [], []
        for f, (ref, kind) in enumerate(zip(refs, kinds)):
            for k, chip in enumerate(chips):
                theirs = 2 * chip[0] + chip[1]
                sem = dict(send_sem=send_sems.at[3 * f + k], recv_sem=recv_sems.at[3 * f + k], device_id_type=MESH)
                if over_chips:
                    out, back, to = _region(kind, ref, 2 * x + y, c), _region(kind, ref, theirs, c), (*chip, c)
                else:
                    out, back, to = _region(kind, ref, theirs, c), _region(kind, ref, theirs, 1 - c), (x, y, 1 - c)
                sends.append(dict(src_ref=out, dst_ref=out, device_id=to, **sem))
                recvs.append(dict(src_ref=back, dst_ref=back, device_id=to, **sem))
        return sends, recvs
    return plan


def _reduce_plan(refs, send_sems, recv_sems):
    x, y, c, _ = _position()
    me = 4 * x + 2 * y + c
    sends, recvs = [], []
    for f in range(len(refs) // 2):
        acc, land = refs[2 * f], refs[2 * f + 1]
        for d in range(1, N_DEV):
            t = (me + d) % N_DEV
            to = dict(device_id=(t // 4, (t // 2) % 2, t % 2), device_id_type=MESH)
            slot = N_DEV - 1 - d
            sends.append(dict(src_ref=acc.at[t % 2, t // 2], dst_ref=land.at[slot], send_sem=send_sems.at[7 * f + d - 1],
                              recv_sem=recv_sems.at[7 * f + slot], **to))
            recvs.append(dict(src_ref=land.at[d - 1], dst_ref=land.at[d - 1], send_sem=send_sems.at[7 * f + d - 1],
                              recv_sem=recv_sems.at[7 * f + d - 1], **to))
    return sends, recvs


def _swap_plan(refs, send_sems, recv_sems):
    x, y, c, _ = _position()
    sends, recvs = [], []
    for f, g in enumerate(refs):
        sem = dict(send_sem=send_sems.at[f], recv_sem=recv_sems.at[f], device_id=(x, y, 1 - c), device_id_type=MESH)
        sends.append(dict(src_ref=g.at[c], dst_ref=g.at[c], **sem))
        recvs.append(dict(src_ref=g.at[1 - c], dst_ref=g.at[1 - c], **sem))
    return sends, recvs


def _sum_pieces(ids2, acc, land, name):
    _, _, nr, nc = acc.shape
    tr = _tile(nr, 256, 16)

    def body(ids_ref, own_ref, land_ref, o_ref):
        del ids_ref
        s = own_ref[...].astype(F32)
        for k in range(N_DEV - 1):
            s = s + land_ref[k].astype(F32)
        o_ref[...] = s

    return pl.pallas_call(
        body,
        grid_spec=pltpu.PrefetchScalarGridSpec(
            num_scalar_prefetch=1, grid=(nr // tr,),
            in_specs=[pl.BlockSpec((None, None, tr, nc), lambda i, ids: (ids[0], ids[1], i, 0)),
                      pl.BlockSpec((N_DEV - 1, tr, nc), lambda i, ids: (0, i, 0))],
            out_specs=pl.BlockSpec((None, tr, nc), lambda i, ids: (ids[0], i, 0))),
        out_shape=SDS((2, nr, nc), F32), name=name, compiler_params=_params("parallel"))(ids2, acc, land)


def _all_sum(block):
    m_per, n = block.shape

    def body(x_ref, all_ref, sum_ref, send_sems, recv_sems, local_sem):
        x, y, c, chips = _position()
        me, sibling = (x, y, c), (x, y, 1 - c)

        def rows(px, py, pc):
            return all_ref.at[pl.ds((4 * px + 2 * py + pc) * m_per, m_per), :]

        def copy(k, blk, to, src=None):
            return pltpu.make_async_remote_copy(
                src_ref=rows(*blk) if src is None else src, dst_ref=rows(*blk), send_sem=send_sems.at[k],
                recv_sem=recv_sems.at[k], device_id=to, device_id_type=MESH)

        mine = pltpu.make_async_copy(x_ref, rows(*me), local_sem)
        mine.start()
        first = [copy(0, me, sibling, src=x_ref)]
        first += [copy(1 + k, me, (*chip, c), src=x_ref) for k, chip in enumerate(chips)]
        for cp in first:
            cp.start()
        passed = [copy(4 + k, (*chip, c), sibling) for k, chip in enumerate(chips)]
        for k, chip in enumerate(chips):
            copy(1 + k, (*chip, c), me).wait_recv()
            passed[k].start()
        copy(0, sibling, me).wait_recv()
        for k, chip in enumerate(chips):
            copy(4 + k, (*chip, 1 - c), me).wait_recv()
        for cp in first + passed:
            cp.wait_send()
        mine.wait()
        acc = all_ref[0:m_per, :]
        for d in range(1, N_DEV):
            acc = acc + all_ref[d * m_per:(d + 1) * m_per, :]
        sum_ref[...] = acc

    vm = pl.BlockSpec(memory_space=pltpu.VMEM)
    return pl.pallas_call(
        body, in_specs=[vm], out_specs=[vm, vm], out_shape=[SDS((N_DEV * m_per, n), F32), SDS((m_per, n), F32)],
        scratch_shapes=[pltpu.SemaphoreType.DMA((7,)), pltpu.SemaphoreType.DMA((7,)), pltpu.SemaphoreType.DMA],
        name="all_sum", compiler_params=_params())(block)[1]


BIG = {"ev_w_in": "col", "ev_w_out": "row", "od_w_in": "col", "od_w_out": "row", "mlp_w1": "col", "mlp_w2": "row"}
WEIGHTS = ("meta_tokens", "mix_norm_g", "mlp_norm_g", "final_norm_g", "ev_w_in", "ev_conv_w", "ev_conv_b", "ev_ln_g",
           "ev_ln_b", "ev_pool_w", "ev_pool_b", "ev_pool_scale", "ev_w_out", "od_w_in", "od_gnorm_g", "od_w_out",
           "lb_param", "mlp_w1", "mlp_w2")
PACK_UNIT = 1024


def _mixer_names(layer):
    return ("ev_w_in", "ev_w_out") if layer % 2 == 0 else ("od_w_in", "od_w_out")


def _pack(arrays):
    flat = []
    for a in arrays:
        a = a.reshape(-1)
        flat.append(jnp.pad(a, (0, (-a.shape[0]) % PACK_UNIT)))
    return jnp.concatenate(flat).reshape(-1, 128)


def _unpack(packed, shapes):
    flat = packed.reshape(-1)
    out, off = [], 0
    for s in shapes:
        size = 1
        for d in s:
            size *= d
        out.append(flat[off:off + size].reshape(s))
        off += size + (-size) % PACK_UNIT
    return out


def _local_step(x2, target, P, weights, boundary, first_deps=()):
    D = x2.shape[1]
    n_layers = P["mix_norm_g"].shape[0]
    h = jnp.concatenate([jnp.zeros((PAD, D), F32), P["meta_full"], x2], axis=0)
    mix_g = P["mix_norm_g"].reshape(n_layers, 1, D)
    mlp_g = P["mlp_norm_g"].reshape(n_layers, 1, D)
    vec = lambda a: a.reshape(a.shape[0], 1, -1)
    cb3, lg3, lnb3, ps3 = vec(P["ev_conv_b"]), vec(P["ev_ln_g"]), vec(P["ev_ln_b"]), vec(P["ev_pool_scale"])
    pb3 = vec(P["ev_pool_b"])
    gn3 = vec(P["od_gnorm_g"])
    lb_all = _lb_fwd(P["lb_param"])
    lb3 = lb_all.reshape(n_layers, 1, D)
    even = (cb3, lg3, lnb3, P["ev_pool_w"], pb3, ps3)

    saved = []
    deps = tuple(first_deps)
    for layer in range(n_layers):
        j = layer // 2
        w_in, w_out = _mixer_names(layer)
        W = {}
        s = {"h": h, "W": W}
        s["n"] = _rms_fwd(h, mix_g, layer, "mix_norm_0", deps=deps) if layer == 0 else n_next
        deps = ()
        W[w_in] = weights(layer, w_in, (s["n"],))
        s["u"] = _mm_nn(s["n"], W[w_in], 0, f"mix_in_{layer}")
        if layer % 2 == 0:
            s["y"] = _even_fwd(s["u"], P["conv_w_full"], *even, j, f"even_fwd_{layer}")
        else:
            s["y"], s["o"], s["sall"] = _hgrn_fwd(s["u"], lb3, layer, gn3, j, f"hgrn_fwd_{layer}")
        W[w_out] = weights(layer, w_out, (s["y"],))
        h, s["n2"] = _mm_nn_norm(s["y"], W[w_out], 0, h, mlp_g, layer, f"mix_out_{layer}")
        s["h1"] = h
        W["mlp_w1"] = weights(layer, "mlp_w1", (s["n2"],))
        s["act"], s["relu"] = _mm_nn(s["n2"], W["mlp_w1"], 0, f"mlp_up_{layer}", relu2=True)
        W["mlp_w2"] = weights(layer, "mlp_w2", (s["act"],))
        if layer + 1 < n_layers:
            h, n_next = _mm_nn_norm(s["act"], W["mlp_w2"], 0, h, mix_g, layer + 1, f"mlp_down_{layer}")
        else:
            h = _mm_nn(s["act"], W["mlp_w2"], 0, f"mlp_down_{layer}", res=h)
        saved.append(s)

    dh, dhb, dg_final, loss = _final(h, P["final_norm_g"].reshape(1, D), target)

    small = {"final_norm_g": dg_final}
    per_layer = {k: [None] * n_layers for k in ("mix_norm_g", "mlp_norm_g", "lb")}
    per_pair = {k: [None] * (n_layers // 2) for k in
                ("ev_conv_w", "ev_conv_b", "ev_ln_g", "ev_ln_b", "ev_pool_w", "ev_pool_b", "ev_pool_scale", "od_gnorm_g")}
    for layer in reversed(range(n_layers)):
        j = layer // 2
        s = saved[layer]
        W = s["W"]
        w_in, w_out = _mixer_names(layer)
        dz = _mm_nt(dhb, W["mlp_w2"], 0, f"d_act_{layer}", relu=s["relu"], deps=deps)
        dw2 = _mm_tn(s["act"], dhb, "row", f"dw2_{layer}")
        dw1 = _mm_tn(s["n2"], dz, "col", f"dw1_{layer}")
        dh, dhb, per_layer["mlp_norm_g"][layer] = _mm_nt_norm(dz, W["mlp_w1"], 0, s["h1"], mlp_g, layer, dh, f"d_n2_{layer}")
        deps = boundary(f"mlp{layer}", {("mlp_w1", layer): dw1, ("mlp_w2", layer): dw2}, (dhb,))
        dy = _mm_nt(dhb, W[w_out], 0, f"d_y_{layer}", deps=deps)
        dwout = _mm_tn(s["y"], dhb, "row", f"dwout_{layer}")
        if layer % 2 == 0:
            du, dcw, dcb, dlg, dlnb, dpw, dpb, dps = _even_bwd(s["u"], dy, P["conv_w_full"], *even, j, f"even_bwd_{layer}")
            for k, val in (("ev_conv_w", dcw), ("ev_conv_b", dcb), ("ev_ln_g", dlg), ("ev_ln_b", dlnb),
                           ("ev_pool_w", dpw), ("ev_pool_b", dpb), ("ev_pool_scale", dps)):
                per_pair[k][j] = val
        else:
            du, per_layer["lb"][layer], per_pair["od_gnorm_g"][j] = _hgrn_bwd(
                s["u"], s["o"], dy, s["sall"], lb3, layer, gn3, j, f"hgrn_bwd_{layer}")
        dwin = _mm_tn(s["n"], du, "col", f"dwin_{layer}")
        deps = boundary(f"mix{layer}", {(w_in, j): dwin, (w_out, j): dwout}, (du,))
        dh, dhb, per_layer["mix_norm_g"][layer] = _mm_nt_norm(du, W[w_in], 0, s["h"], mix_g, layer, dh, f"d_n_{layer}", deps=deps)
        deps = ()

    small["mix_norm_g"] = jnp.concatenate(per_layer["mix_norm_g"], axis=0)
    small["mlp_norm_g"] = jnp.concatenate(per_layer["mlp_norm_g"], axis=0)
    dlb_all = jnp.concatenate([jnp.zeros((1, D), F32) if g is None else g for g in per_layer["lb"]], axis=0)
    small["lb_param"] = _lb_bwd(P["lb_param"], dlb_all)
    for k, vals in per_pair.items():
        small[k] = jnp.stack(vals, axis=0)
    small["meta_tokens"] = dh[PAD:LEAD]
    return loss, dh, small


def kernel(x, meta_tokens, mix_norm_g, mlp_norm_g, final_norm_g, ev_w_in, ev_conv_w, ev_conv_b, ev_ln_g, ev_ln_b, ev_pool_w, ev_pool_b, ev_pool_scale, ev_w_out, od_w_in, od_gnorm_g, od_w_out, lb_param, mlp_w1, mlp_w2, loss_target, m_meta_tokens, m_mix_norm_g, m_mlp_norm_g, m_final_norm_g, m_ev_w_in, m_ev_conv_w, m_ev_conv_b, m_ev_ln_g, m_ev_ln_b, m_ev_pool_w, m_ev_pool_b, m_ev_pool_scale, m_ev_w_out, m_od_w_in, m_od_gnorm_g, m_od_w_out, m_lb_param, m_mlp_w1, m_mlp_w2, v_meta_tokens, v_mix_norm_g, v_mlp_norm_g, v_final_norm_g, v_ev_w_in, v_ev_conv_w, v_ev_conv_b, v_ev_ln_g, v_ev_ln_b, v_ev_pool_w, v_ev_pool_b, v_ev_pool_scale, v_ev_w_out, v_od_w_in, v_od_gnorm_g, v_od_w_out, v_lb_param, v_mlp_w1, v_mlp_w2):
    given = dict(locals())
    w = {n: given[n] for n in WEIGHTS}
    m = {n: given["m_" + n] for n in WEIGHTS}
    v = {n: given["v_" + n] for n in WEIGHTS}
    n_layers = mix_norm_g.shape[0]
    core = lax.axis_index("c").astype(jnp.int32)
    chip = (2 * lax.axis_index("x") + lax.axis_index("y")).astype(jnp.int32)
    chip1 = chip.reshape(1)
    ids2 = jnp.stack([core, chip])

    conv_pad = jnp.pad(ev_conv_w, ((0, 0), (0, CONV_ROWS - CONV_WIDTH), (0, 0)))
    stages = [[(0, n)] for n in (*_mixer_names(0), "mlp_w1", "mlp_w2")]
    stages += [[(layer, n) for n in (*_mixer_names(layer), "mlp_w1", "mlp_w2")] for layer in range(1, n_layers)]
    gathers, where, token = [], {}, ()
    for k, stage in enumerate(stages):
        index = [layer if n.startswith("mlp") else layer // 2 for layer, n in stage]
        kinds = [BIG[n] for _, n in stage]
        bufs = [_cast_place(w[n], i, BIG[n], chip1, BF16, f"place_{n}_{i}") for (_, n), i in zip(stage, index)]
        if k == 0:
            bufs.append(_cast_place(meta_tokens[None], 0, "col", chip1, F32, "place_meta"))
            bufs.append(_cast_place(conv_pad.reshape(1, -1, conv_pad.shape[2]), 0, "col", chip1, F32, "place_conv_w"))
            kinds += ["col", "col"]
        plan = _gather_plan(kinds, True)
        ss, rs, bufs, tok = _split_start(f"gather_start_{k}", plan, bufs, 3 * len(bufs), deps=token)
        token = (tok,)
        gathers.append((kinds, plan, ss, rs, bufs))
        where.update({key: (k, f) for f, key in enumerate(stage)})

    landed = {}

    def arrived(k, after):
        if k not in landed:
            kinds, plan, ss, rs, bufs = gathers[k]
            hand_on = _gather_plan(kinds, False)
            ss, rs, bufs, _ = _split_start(f"gather_pass_{k}", hand_on, bufs, 3 * len(bufs), deps=after, earlier=(plan, ss, rs))
            landed[k] = _split_wait(f"gather_wait_{k}", hand_on, ss, rs, bufs)
        return landed[k]

    def weights(layer, name, after):
        k, f = where[(layer, name)]
        return arrived(k, after)[f][None]

    first = arrived(0, ())
    P = {n: w[n] for n in ("mix_norm_g", "mlp_norm_g", "final_norm_g", "ev_conv_b", "ev_ln_g", "ev_ln_b", "ev_pool_w",
                           "ev_pool_b", "ev_pool_scale", "od_gnorm_g", "lb_param")}
    P["meta_full"] = first[1]
    P["conv_w_full"] = first[2].reshape(ev_conv_w.shape[0], CONV_ROWS, -1)

    pending, outs = [], {n: None for n in BIG}

    def advance(after):
        tokens, still = [], []
        for st in pending:
            if st["phase"] == 1:
                bufs = _split_wait(f"reduce_wait_{st['tag']}", _reduce_plan, st["ss"], st["rs"], st["bufs"], after)
                halves = [_sum_pieces(ids2, bufs[2 * f], bufs[2 * f + 1], f"sum_{st['tag']}_{f}") for f in range(len(bufs) // 2)]
                ss, rs, halves, tok = _split_start(f"swap_start_{st['tag']}", _swap_plan, halves, len(halves))
                tokens.append(tok)
                still.append(dict(st, phase=2, ss=ss, rs=rs, bufs=halves))
            else:
                grads = _split_wait(f"swap_wait_{st['tag']}", _swap_plan, st["ss"], st["rs"], st["bufs"], after)
                for (n, i), g in zip(st["keys"], grads):
                    outs[n] = _adamw_layer(w[n], m[n], v[n], g.reshape(w[n].shape[1:]), i, outs[n], f"adamw_{n}_{i}")
        pending[:] = still
        return tokens

    def boundary(tag, grads, after):
        tokens = advance(after)
        bufs = []
        for acc in grads.values():
            bufs += [acc, lax.empty((N_DEV - 1,) + acc.shape[2:], BF16)]
        ss, rs, bufs, tok = _split_start(f"reduce_start_{tag}", _reduce_plan, bufs, 7 * len(grads))
        pending.append(dict(phase=1, tag=tag, keys=list(grads), ss=ss, rs=rs, bufs=bufs))
        return tuple(tokens + [tok])

    loss, dh, small = _local_step(x[0], loss_target[0], P, weights, boundary, first_deps=token)

    order = [n for n in WEIGHTS if n not in BIG]
    block = _pack([small[n] for n in order] + [loss])
    advance((block,))
    packed = _all_sum(block)
    advance((packed,))
    total = _unpack(packed, [small[n].shape for n in order] + [loss.shape])
    loss_sum = total[-1][0, 0]
    gsmall = dict(zip(order, total[:-1]))
    gsmall["meta_tokens"] = lax.dynamic_slice_in_dim(gsmall["meta_tokens"], chip * meta_tokens.shape[1], meta_tokens.shape[1], 1)
    gsmall["ev_conv_w"] = lax.dynamic_slice_in_dim(gsmall["ev_conv_w"][:, :CONV_WIDTH], chip * ev_conv_w.shape[2], ev_conv_w.shape[2], 2)

    g_out, d_out, m_out, v_out = {}, {}, {}, {}
    for n in WEIGHTS:
        if n in BIG:
            g_out[n], d_out[n], m_out[n], v_out[n] = outs[n]
            continue
        shape = w[n].shape
        g = gsmall[n].reshape(shape)
        cols = shape[-1] if len(shape) > 1 else 128
        two = lambda a: a.reshape(-1, cols)
        d_, m_, v_ = _adamw(two(w[n]), two(g), two(m[n]), two(v[n]), f"adamw_{n}")
        g_out[n], d_out[n], m_out[n], v_out[n] = g, d_.reshape(shape), m_.reshape(shape), v_.reshape(shape)

    grad_x = dh[LEAD:][None]
    return (loss_sum, grad_x, *[g_out[n] for n in WEIGHTS], *[d_out[n] for n in WEIGHTS],
            *[m_out[n] for n in WEIGHTS], *[v_out[n] for n in WEIGHTS])
```

```python
import functools

import jax
import jax.numpy as jnp
from jax import lax
from jax.experimental import pallas as pl
from jax.experimental.pallas import tpu as pltpu

F32 = jnp.float32
BF16 = jnp.bfloat16
SDS = jax.ShapeDtypeStruct
MESH = pl.DeviceIdType.MESH
ANY_SPEC = pl.BlockSpec(memory_space=pl.ANY)

N_META = 16
CHUNK = 64
LEAD = CHUNK
PAD = LEAD - N_META
CONV_WIDTH = 31
CONV_ROWS = 32
POOL_WINDOWS = (2, 4, 8, 16)
HEAD_DIM = 128
SUB = 16
EXP_CAP = 80.0
EPS = 1e-6
ADAM_LR = 0.001
ADAM_B1 = 0.9
ADAM_B2 = 0.999
ADAM_EPS = 1e-08
ADAM_WD = 0.01
ADAM_STEP = 10
N_CHIPS = 4
VMEM_LIMIT = 52 << 20
MM_VMEM_BUDGET = 44 << 20


def _params(*sem):
    return pltpu.CompilerParams(dimension_semantics=sem if sem else None, vmem_limit_bytes=VMEM_LIMIT)


def _tile(n, target, unit=CHUNK):
    best = None
    for t in range(unit, min(n, target) + 1, unit):
        if n % t == 0:
            best = t
    assert best is not None, (n, target, unit)
    return best


def _ctile(n, target=512):
    for t in (512, 384, 256, 128):
        if t <= target and n % t == 0:
            return t
    raise ValueError(n)


def _mm_tiles(M, N, per_row, per_col, per_elem):
    best = None
    for tn in (512, 384, 256, 128):
        if N % tn:
            continue
        for tm in sorted((d for d in range(16, M + 1, 16) if M % d == 0), reverse=True):
            if 2 * (tm * per_row + tn * per_col + tm * tn * per_elem) <= MM_VMEM_BUDGET:
                if best is None or tm * tn > best[0] * best[1]:
                    best = (tm, tn)
                break
    assert best is not None, (M, N)
    return best


def _sigmoid(x):
    return 1.0 / (1.0 + jnp.exp(-x))


def _row_ids(shape, base):
    return lax.broadcasted_iota(jnp.int32, shape, 0) + base


def _cast_place(w3, layer, kind, chip1, dtype, name):
    _, ks, ns = w3.shape
    tr = _tile(ks, 512, 16)
    full = (ks, ns * N_CHIPS) if kind == "col" else (ks * N_CHIPS, ns)

    def body(chip_ref, w_ref, o_ref):
        del chip_ref
        o_ref[...] = w_ref[...].astype(dtype)

    omap = (lambda i, chip: (i, chip[0])) if kind == "col" else (lambda i, chip: (chip[0] * (ks // tr) + i, 0))
    return pl.pallas_call(
        body,
        grid_spec=pltpu.PrefetchScalarGridSpec(
            num_scalar_prefetch=1, grid=(ks // tr,),
            in_specs=[pl.BlockSpec((None, tr, ns), lambda i, chip: (layer, i, 0))],
            out_specs=pl.BlockSpec((tr, ns), omap)),
        out_shape=SDS(full, dtype), name=name, compiler_params=_params("parallel"))(chip1, w3)


def _rms_fwd(h, g3, layer, name, deps=()):
    T, D = h.shape
    tm = _tile(T, 832)

    def body(h_ref, g_ref, *rest):
        n_ref = rest[-1]
        x = h_ref[...]
        r = lax.rsqrt(jnp.mean(x * x, axis=-1, keepdims=True) + EPS)
        n_ref[...] = ((x * r) * g_ref[...]).astype(BF16)

    return pl.pallas_call(
        body, grid=(T // tm,),
        in_specs=[pl.BlockSpec((tm, D), lambda i: (i, 0)), pl.BlockSpec((None, 1, D), lambda i: (layer, 0, 0))]
        + [ANY_SPEC] * len(deps),
        out_specs=pl.BlockSpec((tm, D), lambda i: (i, 0)), out_shape=SDS((T, D), BF16),
        name=name, compiler_params=_params("parallel"))(h, g3, *deps)


def _final(h, g2, target):
    T, D = h.shape
    tm = _tile(T, 320)
    nsub = tm // CHUNK
    nblk = target.shape[0] // CHUNK

    def body(h_ref, g_ref, *rest):
        t_refs = rest[:nsub]
        dh_ref, dhb_ref, dg_ref, loss_ref = rest[nsub:]
        i = pl.program_id(0)

        @pl.when(i == 0)
        def _():
            dg_ref[...] = jnp.zeros_like(dg_ref)
            loss_ref[...] = jnp.zeros_like(loss_ref)

        g = g_ref[...]
        for q in range(nsub):
            rows = slice(q * CHUNK, (q + 1) * CHUNK)
            x = h_ref[rows, :]
            r = lax.rsqrt(jnp.mean(x * x, axis=-1, keepdims=True) + EPS)
            xh = x * r
            live = jnp.where(i * nsub + q > 0, 1.0, 0.0).astype(F32)
            e = ((xh * g) - t_refs[q][...]) * live
            dy = e * (1.0 / D)
            dxh = dy * g
            dh = r * (dxh - xh * jnp.mean(dxh * xh, axis=-1, keepdims=True))
            dh_ref[rows, :] = dh
            dhb_ref[rows, :] = dh.astype(BF16)
            dg_ref[...] += jnp.sum(dy * xh, axis=0, keepdims=True)
            loss_ref[...] += jnp.sum(e * e) * (0.5 / D)

    row = pl.BlockSpec((tm, D), lambda i: (i, 0))
    t_specs = [pl.BlockSpec((CHUNK, D), functools.partial(lambda i, q: (jnp.clip(i * nsub + q - 1, 0, nblk - 1), 0), q=q))
               for q in range(nsub)]
    return pl.pallas_call(
        body, grid=(T // tm,),
        in_specs=[row, pl.BlockSpec((1, D), lambda i: (0, 0))] + t_specs,
        out_specs=[row, row, pl.BlockSpec((1, D), lambda i: (0, 0)), pl.BlockSpec((1, 128), lambda i: (0, 0))],
        out_shape=[SDS((T, D), F32), SDS((T, D), BF16), SDS((1, D), F32), SDS((1, 128), F32)],
        name="final_loss", compiler_params=_params("arbitrary"))(h, g2, *([target] * nsub))


def _mm_nn(a, w3, layer, name, res=None, relu2=False):
    M, K = a.shape
    N = w3.shape[2]
    tm, tn = _mm_tiles(M, N, 2 * K, 2 * K, (4 if relu2 else 4) + (4 if res is not None else 0))

    def body(*refs):
        acc = jnp.dot(refs[0][...], refs[1][...], preferred_element_type=F32)
        if res is not None:
            acc = acc + refs[2][...]
        if relu2:
            p = jnp.maximum(acc, 0.0)
            refs[-2][...] = (p * p).astype(BF16)
            refs[-1][...] = p.astype(BF16)
        else:
            refs[-1][...] = acc

    in_specs = [pl.BlockSpec((tm, K), lambda i, j: (i, 0)), pl.BlockSpec((None, K, tn), lambda i, j: (layer, 0, j))]
    args = [a, w3]
    tile = pl.BlockSpec((tm, tn), lambda i, j: (i, j))
    if res is not None:
        in_specs.append(tile)
        args.append(res)
    return pl.pallas_call(
        body, grid=(M // tm, N // tn), in_specs=in_specs, out_specs=[tile, tile] if relu2 else tile,
        out_shape=[SDS((M, N), BF16)] * 2 if relu2 else SDS((M, N), F32),
        name=name, compiler_params=_params("parallel", "parallel"))(*args)


def _mm_nt(dy, w3, layer, name, relu=None, deps=()):
    M, N = dy.shape
    K = w3.shape[1]
    tm, tk = _mm_tiles(M, K, 2 * N, 2 * N, 4)

    def body(*refs):
        acc = lax.dot_general(refs[0][...], refs[1][...], (((1,), (1,)), ((), ())), preferred_element_type=F32)
        if relu is not None:
            acc = (acc * (2.0 * refs[2][...].astype(F32))).astype(BF16)
        refs[-1][...] = acc

    tile = pl.BlockSpec((tm, tk), lambda i, j: (i, j))
    in_specs = [pl.BlockSpec((tm, N), lambda i, j: (i, 0)), pl.BlockSpec((None, tk, N), lambda i, j: (layer, j, 0))]
    args = [dy, w3]
    if relu is not None:
        in_specs.append(tile)
        args.append(relu)
    in_specs += [ANY_SPEC] * len(deps)
    args += list(deps)
    return pl.pallas_call(
        body, grid=(M // tm, K // tk), in_specs=in_specs, out_specs=tile,
        out_shape=SDS((M, K), F32 if relu is None else BF16),
        name=name, compiler_params=_params("parallel", "parallel"))(*args)


def _row_tile(M, per_row, fixed):
    for tm in sorted((d for d in range(16, M + 1, 16) if M % d == 0), reverse=True):
        if 2 * (tm * per_row + fixed) <= MM_VMEM_BUDGET:
            return tm
    raise ValueError((M, per_row, fixed))


def _mm_nn_norm(a, w3, layer, res, g3, glayer, name):
    M, K = a.shape
    D = w3.shape[2]
    tm = _row_tile(M, 2 * K + 10 * D, 2 * K * D)

    def body(a_ref, w_ref, r_ref, g_ref, h_ref, n_ref):
        x = r_ref[...] + jnp.dot(a_ref[...], w_ref[...], preferred_element_type=F32)
        h_ref[...] = x
        r = lax.rsqrt(jnp.mean(x * x, axis=-1, keepdims=True) + EPS)
        n_ref[...] = ((x * r) * g_ref[...]).astype(BF16)

    row = pl.BlockSpec((tm, D), lambda i: (i, 0))
    return pl.pallas_call(
        body, grid=(M // tm,),
        in_specs=[pl.BlockSpec((tm, K), lambda i: (i, 0)), pl.BlockSpec((None, K, D), lambda i: (layer, 0, 0)), row,
                  pl.BlockSpec((None, 1, D), lambda i: (glayer, 0, 0))],
        out_specs=[row, row], out_shape=[SDS((M, D), F32), SDS((M, D), BF16)],
        name=name, compiler_params=_params("parallel"))(a, w3, res, g3)


def _mm_nt_norm(dy, w3, layer, h, g3, glayer, dh_in, name, deps=()):
    M, N = dy.shape
    D = w3.shape[1]
    tm = _row_tile(M, 2 * N + 14 * D, 2 * N * D)

    def body(dy_ref, w_ref, h_ref, g_ref, dhi_ref, *rest):
        dh_ref, dhb_ref, dg_ref = rest[-3:]
        dn = lax.dot_general(dy_ref[...], w_ref[...], (((1,), (1,)), ((), ())), preferred_element_type=F32)
        x = h_ref[...]
        r = lax.rsqrt(jnp.mean(x * x, axis=-1, keepdims=True) + EPS)
        xh = x * r
        dxh = dn * g_ref[...]
        dh = dhi_ref[...] + r * (dxh - xh * jnp.mean(dxh * xh, axis=-1, keepdims=True))
        dh_ref[...] = dh
        dhb_ref[...] = dh.astype(BF16)

        @pl.when(pl.program_id(0) == 0)
        def _():
            dg_ref[...] = jnp.zeros_like(dg_ref)

        dg_ref[...] += jnp.sum(dn * xh, axis=0, keepdims=True)

    row = pl.BlockSpec((tm, D), lambda i: (i, 0))
    return pl.pallas_call(
        body, grid=(M // tm,),
        in_specs=[pl.BlockSpec((tm, N), lambda i: (i, 0)), pl.BlockSpec((None, D, N), lambda i: (layer, 0, 0)), row,
                  pl.BlockSpec((None, 1, D), lambda i: (glayer, 0, 0)), row] + [ANY_SPEC] * len(deps),
        out_specs=[row, row, pl.BlockSpec((1, D), lambda i: (0, 0))],
        out_shape=[SDS((M, D), F32), SDS((M, D), BF16), SDS((1, D), F32)],
        name=name, compiler_params=_params("arbitrary"))(dy, w3, h, g3, dh_in, *deps)


def _fam_dims(kind, K, N):
    return (K // 2, N // N_CHIPS) if kind == "col" else (K // (2 * N_CHIPS), N)


def _mm_tn(x, dy, kind, name):
    M, K = x.shape
    N = dy.shape[1]
    nr, nc = _fam_dims(kind, K, N)

    def body(x_ref, dy_ref, o_ref):
        res = lax.dot_general(x_ref[...], dy_ref[...], (((0,), (0,)), ((), ())), preferred_element_type=F32)
        o_ref[...] = res.astype(BF16).reshape(o_ref.shape)

    if kind == "col":
        tn = _ctile(nc)
        ct = nc // tn
        grid = (N // tn,)
        in_specs = [pl.BlockSpec((M, K), lambda j: (0, 0)), pl.BlockSpec((M, tn), lambda j: (0, j))]
        out_spec = pl.BlockSpec((2, None, nr, tn), lambda j: (0, j // ct, 0, j % ct))
    else:
        grid = (N_CHIPS,)
        in_specs = [pl.BlockSpec((M, 2 * nr), lambda i: (0, i)), pl.BlockSpec((M, N), lambda i: (0, 0))]
        out_spec = pl.BlockSpec((2, None, nr, N), lambda i: (0, i, 0, 0))
    return pl.pallas_call(
        body, grid=grid, in_specs=in_specs, out_specs=out_spec, out_shape=SDS((2, N_CHIPS, nr, nc), BF16),
        name=name, compiler_params=_params("parallel"))(x, dy)


C_EVEN = 512


def _live(rows, base, total):
    r = _row_ids((rows, 1), base)
    return jnp.logical_and(r >= PAD, r < total).astype(F32)


def _conv_taps(win, w_ref, ls, acc, flip):
    for b in range(8):
        rb = win if b == 0 else pltpu.roll(win, 96 - b, 0)
        for a in range(5):
            o = 8 * a + b
            tap = (30 - o) if flip else (o - 2)
            if 0 <= tap < CONV_WIDTH:
                acc = acc + w_ref[pl.ds(tap, 1), ls] * rb[8 * a:8 * a + CHUNK]
    return acc


def _window_sum(win, levels, forward):
    s = win
    n = win.shape[0]
    for k in range(levels):
        step = 1 << k
        s = s + pltpu.roll(s, (n - step) if forward else step, 0)
    return s


def _pool_count(base, g):
    pos = _row_ids((CHUNK, 1), base) - PAD
    return jnp.clip(pos + 1, 1, POOL_WINDOWS[g]).astype(F32)


def _even_fwd(u, cw3, cb3, lg3, lb3, pw4, pb3, ps3, j, name):
    T = u.shape[0]
    C = C_EVEN
    tm = _tile(T, 320)
    nch = tm // CHUNK
    nblk = T // CHUNK

    def body(u_ref, up_ref, cw_ref, cb_ref, lg_ref, lb_ref, pw_ref, pb_ref, ps_ref, o_ref, a_s, p_s, yc_s):
        row0 = pl.program_id(0) * tm
        up = up_ref[...]
        lp = _live(CHUNK, row0 - CHUNK, T)
        a_s[0:CHUNK, :] = up[:, 0:C] * _sigmoid(up[:, C:2 * C]) * lp
        p_s[0:CHUNK, :] = up[:, 2 * C:3 * C] * lp

        def stage(c, _):
            rs = pl.multiple_of(c * CHUNK, CHUNK)
            lv = _live(CHUNK, row0 + rs, T)
            a_s[pl.ds(rs + CHUNK, CHUNK), :] = u_ref[pl.ds(rs, CHUNK), 0:C] * _sigmoid(u_ref[pl.ds(rs, CHUNK), C:2 * C]) * lv
            p_s[pl.ds(rs + CHUNK, CHUNK), :] = u_ref[pl.ds(rs, CHUNK), 2 * C:3 * C] * lv
            return 0

        lax.fori_loop(0, nch, stage, 0)

        def chunk(c, _):
            rs = pl.multiple_of(c * CHUNK, CHUNK)
            lv = _live(CHUNK, row0 + rs, T)
            for cb in range(4):
                ls = slice(cb * 128, (cb + 1) * 128)
                win = a_s[pl.ds(pl.multiple_of(rs + 32, 32), 96), ls]
                acc = jnp.broadcast_to(cb_ref[:, ls], (CHUNK, 128))
                yc_s[:, ls] = _conv_taps(win, cw_ref, ls, acc, False)
            y = yc_s[...]
            xc = y - jnp.mean(y, axis=-1, keepdims=True)
            yn = xc * lax.rsqrt(jnp.mean(xc * xc, axis=-1, keepdims=True) + EPS) * lg_ref[...] + lb_ref[...]
            o_ref[pl.ds(rs, CHUNK), 0:C] = (yn * _sigmoid(yn) * lv).astype(BF16)
            for g in range(4):
                ls = slice(g * 128, (g + 1) * 128)
                win = p_s[pl.ds(pl.multiple_of(rs + 48, 16), 80), ls]
                s = _window_sum(win, g + 1, False)
                d = s[16:80] / _pool_count(row0 + rs, g) - win[16:80]
                yv = jnp.dot(d.astype(BF16), pw_ref[g].astype(BF16), preferred_element_type=F32) + pb_ref[:, ls]
                o_ref[pl.ds(rs, CHUNK), C + g * 128:C + (g + 1) * 128] = (yv * ps_ref[:, ls] * lv).astype(BF16)
            return 0

        lax.fori_loop(0, nch, chunk, 0)

    vec = pl.BlockSpec((None, 1, C), lambda i: (j, 0, 0))
    return pl.pallas_call(
        body, grid=(T // tm,),
        in_specs=[pl.BlockSpec((tm, 3 * C), lambda i: (i, 0)),
                  pl.BlockSpec((CHUNK, 3 * C), lambda i: (jnp.maximum(i * nch - 1, 0), 0)),
                  pl.BlockSpec((None, CONV_ROWS, C), lambda i: (j, 0, 0)), vec, vec, vec,
                  pl.BlockSpec((None, 4, 128, 128), lambda i: (j, 0, 0, 0)), vec, vec],
        out_specs=pl.BlockSpec((tm, 2 * C), lambda i: (i, 0)),
        out_shape=SDS((T, 2 * C), BF16),
        scratch_shapes=[pltpu.VMEM((tm + CHUNK, C), F32), pltpu.VMEM((tm + CHUNK, C), F32), pltpu.VMEM((CHUNK, C), F32)],
        name=name, compiler_params=_params("parallel"))(u, u, cw3, cb3, lg3, lb3, pw4, pb3, ps3)


def _even_bwd(u, dy, cw3, cb3, lg3, lb3, pw4, pb3, ps3, j, name):
    T = u.shape[0]
    C = C_EVEN
    tm = _tile(T, 320)
    nch = tm // CHUNK
    nblk = T // CHUNK
    ntile = T // tm

    def body(u_ref, up_ref, un_ref, dy_ref, dyn_ref, cw_ref, cb_ref, lg_ref, lb_ref, pw_ref, pb_ref, ps_ref,
             du_ref, dcw_ref, dcb_ref, dlg_ref, dlb_ref, dpw_ref, dpb_ref, dps_ref,
             a_s, p_s, dy_s, yc_s, dyc_s, dd_s, ddc_s, dw_s):
        i = pl.program_id(0)
        row0 = i * tm

        @pl.when(i == 0)
        def _():
            for ref in (dcb_ref, dlg_ref, dlb_ref, dpw_ref, dpb_ref, dps_ref, dw_s):
                ref[...] = jnp.zeros_like(ref)

        up = up_ref[...]
        lp = _live(CHUNK, row0 - CHUNK, T)
        a_s[0:CHUNK, :] = up[:, 0:C] * _sigmoid(up[:, C:2 * C]) * lp
        p_s[0:CHUNK, :] = up[:, 2 * C:3 * C] * lp
        un = un_ref[...]
        ln_ = _live(CHUNK, row0 + tm, T)
        a_s[tm + CHUNK:tm + 2 * CHUNK, :] = un[:, 0:C] * _sigmoid(un[:, C:2 * C]) * ln_
        p_s[tm + CHUNK:tm + 2 * CHUNK, :] = un[:, 2 * C:3 * C] * ln_
        dy_s[tm:tm + CHUNK, :] = dyn_ref[...] * ln_
        dyc_s[tm + CHUNK:tm + CHUNK + 32, :] = jnp.zeros((32, C), F32)

        def stage(c, _):
            rs = pl.multiple_of(c * CHUNK, CHUNK)
            lv = _live(CHUNK, row0 + rs, T)
            a_s[pl.ds(rs + CHUNK, CHUNK), :] = u_ref[pl.ds(rs, CHUNK), 0:C] * _sigmoid(u_ref[pl.ds(rs, CHUNK), C:2 * C]) * lv
            p_s[pl.ds(rs + CHUNK, CHUNK), :] = u_ref[pl.ds(rs, CHUNK), 2 * C:3 * C] * lv
            dy_s[pl.ds(rs, CHUNK), :] = dy_ref[pl.ds(rs, CHUNK), :] * lv
            return 0

        lax.fori_loop(0, nch, stage, 0)

        def first(c, _):
            rs = pl.multiple_of(c * CHUNK, CHUNK)
            own = jnp.where(c < nch, 1.0, 0.0).astype(F32)
            for cb in range(4):
                ls = slice(cb * 128, (cb + 1) * 128)
                win = a_s[pl.ds(pl.multiple_of(rs + 32, 32), 96), ls]
                acc = jnp.broadcast_to(cb_ref[:, ls], (CHUNK, 128))
                yc_s[:, ls] = _conv_taps(win, cw_ref, ls, acc, False)
            y = yc_s[...]
            xc = y - jnp.mean(y, axis=-1, keepdims=True)
            rstd = lax.rsqrt(jnp.mean(xc * xc, axis=-1, keepdims=True) + EPS)
            xh = xc * rstd
            yn = xh * lg_ref[...] + lb_ref[...]
            sg = _sigmoid(yn)
            dyn = dy_s[pl.ds(rs, CHUNK), 0:C] * (sg * (1.0 + yn * (1.0 - sg)))
            dlg_ref[...] += jnp.sum(dyn * xh, axis=0, keepdims=True) * own
            dlb_ref[...] += jnp.sum(dyn, axis=0, keepdims=True) * own
            dxh = dyn * lg_ref[...]
            dyc = rstd * (dxh - jnp.mean(dxh, axis=-1, keepdims=True) - xh * jnp.mean(dxh * xh, axis=-1, keepdims=True))
            dyc_s[pl.ds(rs, CHUNK), :] = dyc
            dcb_ref[...] += jnp.sum(dyc, axis=0, keepdims=True) * own
            for g in range(4):
                ls = slice(g * 128, (g + 1) * 128)
                win = p_s[pl.ds(pl.multiple_of(rs + 48, 16), 80), ls]
                s = _window_sum(win, g + 1, False)
                cnt = _pool_count(row0 + rs, g)
                d = (s[16:80] / cnt - win[16:80]).astype(BF16)
                w = pw_ref[g].astype(BF16)
                pre = jnp.dot(d, w, preferred_element_type=F32) + pb_ref[:, ls]
                dyb = dy_s[pl.ds(rs, CHUNK), C + g * 128:C + (g + 1) * 128]
                dpre = dyb * ps_ref[:, ls]
                dps_ref[:, ls] += jnp.sum(dyb * pre, axis=0, keepdims=True) * own
                dpb_ref[:, ls] += jnp.sum(dpre, axis=0, keepdims=True) * own
                dpre_b = (dpre * own).astype(BF16)
                dpw_ref[g] += lax.dot_general(d, dpre_b, (((0,), (0,)), ((), ())), preferred_element_type=F32)
                dd = lax.dot_general(dpre.astype(BF16), w, (((1,), (1,)), ((), ())), preferred_element_type=F32)
                dd_s[pl.ds(rs, CHUNK), ls] = dd
                ddc_s[pl.ds(rs, CHUNK), ls] = dd / cnt
            return 0

        lax.fori_loop(0, nch + 1, first, 0)
        ddc_s[tm + CHUNK:tm + CHUNK + 16, :] = jnp.zeros((16, C), F32)

        def second(c, _):
            rs = pl.multiple_of(c * CHUNK, CHUNK)
            lv = _live(CHUNK, row0 + rs, T)
            for cb in range(4):
                ls = slice(cb * 128, (cb + 1) * 128)
                wd = dyc_s[pl.ds(rs, 96), ls]
                da = _conv_taps(wd, cw_ref, ls, jnp.zeros((CHUNK, 128), F32), True)
                wa = a_s[pl.ds(pl.multiple_of(rs + 32, 32), 96), ls]
                dyc = dyc_s[pl.ds(rs, CHUNK), ls]
                for b in range(8):
                    rb = wa if b == 0 else pltpu.roll(wa, 96 - b, 0)
                    for a in range(5):
                        tap = 8 * a + b - 2
                        if 0 <= tap < CONV_WIDTH:
                            prod = dyc * rb[8 * a:8 * a + CHUNK]
                            part = prod[0:8]
                            for q in range(1, 8):
                                part = part + prod[8 * q:8 * q + 8]
                            dw_s[8 * tap:8 * tap + 8, ls] += part
                val = u_ref[pl.ds(rs, CHUNK), ls]
                sg = _sigmoid(u_ref[pl.ds(rs, CHUNK), C + cb * 128:C + (cb + 1) * 128])
                du_ref[pl.ds(rs, CHUNK), ls] = (da * sg * lv).astype(BF16)
                du_ref[pl.ds(rs, CHUNK), C + cb * 128:C + (cb + 1) * 128] = (da * val * sg * (1.0 - sg) * lv).astype(BF16)
            for g in range(4):
                ls = slice(g * 128, (g + 1) * 128)
                z = _window_sum(ddc_s[pl.ds(rs, 80), ls], g + 1, True)
                dpin = (z[0:CHUNK] - dd_s[pl.ds(rs, CHUNK), ls]) * lv
                du_ref[pl.ds(rs, CHUNK), 2 * C + g * 128:2 * C + (g + 1) * 128] = dpin.astype(BF16)
            return 0

        lax.fori_loop(0, nch, second, 0)

        @pl.when(i == ntile - 1)
        def _():
            for tap in range(CONV_WIDTH):
                dcw_ref[tap:tap + 1, :] = jnp.sum(dw_s[8 * tap:8 * tap + 8, :], axis=0, keepdims=True)
            dcw_ref[CONV_WIDTH:CONV_ROWS, :] = jnp.zeros((CONV_ROWS - CONV_WIDTH, C), F32)

    vec = pl.BlockSpec((None, 1, C), lambda i: (j, 0, 0))
    ovec = pl.BlockSpec((1, C), lambda i: (0, 0))
    return pl.pallas_call(
        body, grid=(ntile,),
        in_specs=[pl.BlockSpec((tm, 3 * C), lambda i: (i, 0)),
                  pl.BlockSpec((CHUNK, 3 * C), lambda i: (jnp.maximum(i * nch - 1, 0), 0)),
                  pl.BlockSpec((CHUNK, 3 * C), lambda i: (jnp.minimum((i + 1) * nch, nblk - 1), 0)),
                  pl.BlockSpec((tm, 2 * C), lambda i: (i, 0)),
                  pl.BlockSpec((CHUNK, 2 * C), lambda i: (jnp.minimum((i + 1) * nch, nblk - 1), 0)),
                  pl.BlockSpec((None, CONV_ROWS, C), lambda i: (j, 0, 0)), vec, vec, vec,
                  pl.BlockSpec((None, 4, 128, 128), lambda i: (j, 0, 0, 0)), vec, vec],
        out_specs=[pl.BlockSpec((tm, 3 * C), lambda i: (i, 0)), pl.BlockSpec((CONV_ROWS, C), lambda i: (0, 0)),
                   ovec, ovec, ovec, pl.BlockSpec((4, 128, 128), lambda i: (0, 0, 0)), ovec, ovec],
        out_shape=[SDS((T, 3 * C), BF16), SDS((CONV_ROWS, C), F32), SDS((1, C), F32), SDS((1, C), F32), SDS((1, C), F32),
                   SDS((4, 128, 128), F32), SDS((1, C), F32), SDS((1, C), F32)],
        scratch_shapes=[pltpu.VMEM((tm + 2 * CHUNK, C), F32), pltpu.VMEM((tm + 2 * CHUNK, C), F32),
                        pltpu.VMEM((tm + CHUNK, 2 * C), F32), pltpu.VMEM((CHUNK, C), F32),
                        pltpu.VMEM((tm + CHUNK + 32, C), F32), pltpu.VMEM((tm + CHUNK, C), F32),
                        pltpu.VMEM((tm + CHUNK + 16, C), F32), pltpu.VMEM((8 * CONV_ROWS, C), F32)],
        name=name, compiler_params=_params("arbitrary"))(u, u, u, dy, dy, cw3, cb3, lg3, lb3, pw4, pb3, ps3)


HI = lax.Precision.HIGHEST


def _dot_nt(a, b):
    return lax.dot_general(a, b, (((1,), (1,)), ((), ())), preferred_element_type=F32)


def _dot_tn(a, b):
    return lax.dot_general(a, b, (((0,), (0,)), ((), ())), preferred_element_type=F32)


def _tri(lower):
    r = lax.broadcasted_iota(jnp.int32, (CHUNK, CHUNK), 0)
    c = lax.broadcasted_iota(jnp.int32, (CHUNK, CHUNK), 1)
    return jnp.where((c <= r) if lower else (c >= r), 1.0, 0.0).astype(F32)


def _hgrn_gates(u_ref, lb_ref, h, D, lv):
    ls = slice(h * HEAD_DIM, (h + 1) * HEAD_DIM)
    qraw = u_ref[:, ls]
    fraw = u_ref[:, D + h * HEAD_DIM:D + (h + 1) * HEAD_DIM]
    v = u_ref[:, 2 * D + h * HEAD_DIM:2 * D + (h + 1) * HEAD_DIM] * lv
    lbv = lb_ref[:, ls]
    sig = _sigmoid(fraw)
    forget = lbv + (1.0 - lbv) * sig
    logf = jnp.log(forget) * lv
    k = (1.0 - forget) * lv
    qsig = _sigmoid(qraw)
    q = qraw * qsig * lv
    return q, k, v, logf, (qraw, qsig, sig, forget, lbv)


def _sub_parts(q, k, b, b_s, I):
    rows = slice(SUB * I, SUB * (I + 1))
    rho = jnp.zeros((1, HEAD_DIM), F32) if I == 0 else b_s[SUB * I - 1:SUB * I, :]
    eI = jnp.exp(b[rows] - rho)
    EI = jnp.exp(jnp.minimum(rho - b, EXP_CAP))
    causal = (lax.broadcasted_iota(jnp.int32, (SUB, CHUNK), 1)
              <= lax.broadcasted_iota(jnp.int32, (SUB, CHUNK), 0) + SUB * I)
    return rows, q[rows] * eI, k * EI, eI, EI, causal


def _hgrn_fwd(u, lb3, layer, gn3, j, name):
    T = u.shape[0]
    D = u.shape[1] // 4
    H = D // HEAD_DIM
    NC = T // CHUNK

    def body(u_ref, lb_ref, gn_ref, y_ref, o_ref, sall_ref, st_s, b_s, lf_s, q_s, k_s):
        n = pl.program_id(0)

        @pl.when(n == 0)
        def _():
            st_s[...] = jnp.zeros_like(st_s)

        lv = _live(CHUNK, n * CHUNK, T)
        heads = range(H)
        cols = [slice(h * HEAD_DIM, (h + 1) * HEAD_DIM) for h in heads]
        vb = []
        for h in heads:
            q, k, v, logf, _ = _hgrn_gates(u_ref, lb_ref, h, D, lv)
            q_s[:, cols[h]] = q
            k_s[:, cols[h]] = k
            lf_s[:, cols[h]] = logf
            vb.append(v.astype(BF16))
        b_s[...] = jnp.dot(_tri(True), lf_s[...], precision=HI, preferred_element_type=F32)
        ops = []
        for h in heads:
            b_h = b_s.at[:, cols[h]]
            b = b_h[...]
            q = q_s[:, cols[h]]
            k = k_s[:, cols[h]]
            blast = b_h[CHUNK - 1:CHUNK, :]
            qh = (q * jnp.exp(b)).astype(BF16)
            kt = (k * jnp.exp(blast - b)).astype(BF16)
            subs = []
            for I in range(CHUNK // SUB):
                _, qI, KI, _, _, causal = _sub_parts(q, k, b, b_h, I)
                subs.append((qI.astype(BF16), KI.astype(BF16), causal))
            ops.append((qh, kt, jnp.exp(blast), subs))
        mm = []
        for h in heads:
            qh, kt, eblast, subs = ops[h]
            st = st_s[h]
            sall_ref[h] = st
            o_inter = _dot_nt(qh, st.astype(BF16))
            st_s[h] = st * eblast + _dot_tn(vb[h], kt)
            mm.append((o_inter, [_dot_nt(qI, KI) for qI, KI, _ in subs]))
        for h in heads:
            o_inter, ps = mm[h]
            p = jnp.concatenate([jnp.where(c, x, 0.0) for x, (_, _, c) in zip(ps, ops[h][3])], axis=0).astype(BF16)
            o = o_inter + jnp.dot(p, vb[h], preferred_element_type=F32)
            o_ref[:, cols[h]] = o
            graw = u_ref[:, 3 * D + h * HEAD_DIM:3 * D + (h + 1) * HEAD_DIM]
            r = lax.rsqrt(jnp.mean(o * o, axis=-1, keepdims=True) + EPS)
            y_ref[:, cols[h]] = (((o * r) * gn_ref[...]) * (graw * _sigmoid(graw))).astype(BF16)

    return pl.pallas_call(
        body, grid=(NC,),
        in_specs=[pl.BlockSpec((CHUNK, 4 * D), lambda n: (n, 0)),
                  pl.BlockSpec((None, 1, D), lambda n: (layer, 0, 0)),
                  pl.BlockSpec((None, 1, HEAD_DIM), lambda n: (j, 0, 0))],
        out_specs=[pl.BlockSpec((CHUNK, D), lambda n: (n, 0)), pl.BlockSpec((CHUNK, D), lambda n: (n, 0)),
                   pl.BlockSpec((None, H, HEAD_DIM, HEAD_DIM), lambda n: (n, 0, 0, 0))],
        out_shape=[SDS((T, D), BF16), SDS((T, D), F32), SDS((NC, H, HEAD_DIM, HEAD_DIM), F32)],
        scratch_shapes=[pltpu.VMEM((H, HEAD_DIM, HEAD_DIM), F32)] + [pltpu.VMEM((CHUNK, D), F32)] * 4,
        name=name, compiler_params=_params("arbitrary"))(u, lb3, gn3)


def _hgrn_bwd(u, o_raw, dy, sall, lb3, layer, gn3, j, name):
    T = u.shape[0]
    D = u.shape[1] // 4
    H = D // HEAD_DIM
    NC = T // CHUNK

    def body(u_ref, o_ref, dy_ref, sall_ref, lb_ref, gn_ref, du_ref, dlb_ref, dgn_ref, dst_s, b_s, lf_s, q_s, k_s, db_s, dk_s):
        step = pl.program_id(0)
        n = NC - 1 - step

        @pl.when(step == 0)
        def _():
            dst_s[...] = jnp.zeros_like(dst_s)
            dlb_ref[...] = jnp.zeros_like(dlb_ref)
            dgn_ref[...] = jnp.zeros_like(dgn_ref)

        lv = _live(CHUNK, n * CHUNK, T)
        last_row = (_row_ids((CHUNK, 1), 0) == CHUNK - 1).astype(F32)
        gn = gn_ref[...]
        heads = range(H)
        cols = [slice(h * HEAD_DIM, (h + 1) * HEAD_DIM) for h in heads]
        vb, dob = [], []
        dgn = jnp.zeros((1, HEAD_DIM), F32)
        for h in heads:
            q, k, v, logf, _ = _hgrn_gates(u_ref, lb_ref, h, D, lv)
            q_s[:, cols[h]] = q
            k_s[:, cols[h]] = k
            lf_s[:, cols[h]] = logf
            vb.append(v.astype(BF16))
            graw = u_ref[:, 3 * D + h * HEAD_DIM:3 * D + (h + 1) * HEAD_DIM]
            gsig = _sigmoid(graw)
            o = o_ref[:, cols[h]]
            r = lax.rsqrt(jnp.mean(o * o, axis=-1, keepdims=True) + EPS)
            xh = o * r
            dyv = dy_ref[:, cols[h]]
            dsg = dyv * (graw * gsig)
            dgn = dgn + jnp.sum(dsg * xh, axis=0, keepdims=True)
            dxh = dsg * gn
            do = r * (dxh - xh * jnp.mean(dxh * xh, axis=-1, keepdims=True))
            dob.append(do.astype(BF16))
            dgraw = dyv * xh * gn * (gsig * (1.0 + graw * (1.0 - gsig)))
            du_ref[:, 3 * D + h * HEAD_DIM:3 * D + (h + 1) * HEAD_DIM] = (dgraw * lv).astype(BF16)
        dgn_ref[...] += dgn
        b_s[...] = jnp.dot(_tri(True), lf_s[...], precision=HI, preferred_element_type=F32)
        ops = []
        for h in heads:
            b_h = b_s.at[:, cols[h]]
            b = b_h[...]
            q = q_s[:, cols[h]]
            k = k_s[:, cols[h]]
            blast = b_h[CHUNK - 1:CHUNK, :]
            eb = jnp.exp(b)
            ekb = jnp.exp(blast - b)
            subs = []
            for I in range(CHUNK // SUB):
                rows, qI, KI, eI, EI, causal = _sub_parts(q, k, b, b_h, I)
                subs.append((rows, qI.astype(BF16), KI.astype(BF16), eI, EI, causal))
            ops.append((eb, ekb, jnp.exp(blast), (q * eb).astype(BF16), (k * ekb).astype(BF16), subs))
        mm = []
        for h in heads:
            eb, ekb, eblast, qhb, ktb, subs = ops[h]
            st = sall_ref[h]
            dst = dst_s[h]
            dstb = dst.astype(BF16)
            dv = _dot_nt(ktb, dstb)
            dqh = jnp.dot(dob[h], st.astype(BF16), preferred_element_type=F32)
            dkt = jnp.dot(vb[h], dstb, preferred_element_type=F32)
            dblast = jnp.sum(dst * st, axis=0, keepdims=True) * eblast
            dst_s[h] = dst * eblast + _dot_tn(dob[h], qhb)
            dp_full = _dot_nt(dob[h], vb[h])
            ps = [_dot_nt(qIb, KIb) for _, qIb, KIb, _, _, _ in subs]
            mm.append((dv, dqh, dkt, dblast, dp_full, ps))
        for h in heads:
            eb, ekb, eblast, qhb, ktb, subs = ops[h]
            dv, dqh, dkt, dblast, dp_full, ps = mm[h]
            p = jnp.concatenate([jnp.where(sub[5], x, 0.0) for x, sub in zip(ps, subs)], axis=0).astype(BF16)
            dv = dv + _dot_tn(p, dob[h])
            du_ref[:, 2 * D + h * HEAD_DIM:2 * D + (h + 1) * HEAD_DIM] = (dv * lv).astype(BF16)
            dq = dqh * eb
            db = dqh * qhb.astype(F32)
            tmp = dkt * ktb.astype(F32)
            dk = dkt * ekb
            db = db - tmp
            dblast = dblast + jnp.sum(tmp, axis=0, keepdims=True)
            dq_parts, db_parts = [], []
            for rows, qIb, KIb, eI, EI, causal in subs:
                dp = jnp.where(causal, dp_full[rows], 0.0).astype(BF16)
                dqI = jnp.dot(dp, KIb, preferred_element_type=F32)
                dKI = _dot_tn(dp, qIb)
                dq_parts.append(dqI * eI)
                db_parts.append(dqI * qIb.astype(F32))
                dk = dk + dKI * EI
                db = db - dKI * KIb.astype(F32)
            dq = dq + jnp.concatenate(dq_parts, axis=0)
            db_s[:, cols[h]] = db + jnp.concatenate(db_parts, axis=0) + last_row * dblast
            dk_s[:, cols[h]] = dk
            qraw = u_ref[:, cols[h]]
            qsig = _sigmoid(qraw)
            du_ref[:, cols[h]] = (dq * (qsig * (1.0 + qraw * (1.0 - qsig))) * lv).astype(BF16)
        lf_s[...] = jnp.dot(_tri(False), db_s[...], precision=HI, preferred_element_type=F32)
        for h in heads:
            fraw = u_ref[:, D + h * HEAD_DIM:D + (h + 1) * HEAD_DIM]
            lbv = lb_ref[:, cols[h]]
            sig = _sigmoid(fraw)
            forget = lbv + (1.0 - lbv) * sig
            dforget = (lf_s[:, cols[h]] / forget - dk_s[:, cols[h]]) * lv
            dlb_ref[:, cols[h]] += jnp.sum(dforget * (1.0 - sig), axis=0, keepdims=True)
            du_ref[:, D + h * HEAD_DIM:D + (h + 1) * HEAD_DIM] = (dforget * (1.0 - lbv) * sig * (1.0 - sig)).astype(BF16)

    rev = lambda s: (NC - 1 - s, 0)
    return pl.pallas_call(
        body, grid=(NC,),
        in_specs=[pl.BlockSpec((CHUNK, 4 * D), rev), pl.BlockSpec((CHUNK, D), rev), pl.BlockSpec((CHUNK, D), rev),
                  pl.BlockSpec((None, H, HEAD_DIM, HEAD_DIM), lambda s: (NC - 1 - s, 0, 0, 0)),
                  pl.BlockSpec((None, 1, D), lambda s: (layer, 0, 0)),
                  pl.BlockSpec((None, 1, HEAD_DIM), lambda s: (j, 0, 0))],
        out_specs=[pl.BlockSpec((CHUNK, 4 * D), rev), pl.BlockSpec((1, D), lambda s: (0, 0)),
                   pl.BlockSpec((1, HEAD_DIM), lambda s: (0, 0))],
        out_shape=[SDS((T, 4 * D), BF16), SDS((1, D), F32), SDS((1, HEAD_DIM), F32)],
        scratch_shapes=[pltpu.VMEM((H, HEAD_DIM, HEAD_DIM), F32)] + [pltpu.VMEM((CHUNK, D), F32)] * 6,
        name=name, compiler_params=_params("arbitrary"))(u, o_raw, dy, sall, lb3, gn3)


def _softmax_layers(p_ref, n_layers):
    rows = [p_ref[l:l + 1, :] for l in range(n_layers)]
    m = functools.reduce(jnp.maximum, rows)
    e = [jnp.exp(x - m) for x in rows]
    tot = functools.reduce(lambda a, b: a + b, e)
    return [x / tot for x in e]


def _lb_fwd(p):
    n_layers, D = p.shape

    def body(p_ref, o_ref):
        s = _softmax_layers(p_ref, n_layers)
        acc = jnp.zeros((1, D), F32)
        o_ref[0:1, :] = acc
        for l in range(1, n_layers):
            acc = acc + s[l]
            o_ref[l:l + 1, :] = acc

    return pl.pallas_call(body, out_shape=SDS(p.shape, F32), name="lb_fwd")(p)


def _lb_bwd(p, dlb):
    n_layers, D = p.shape

    def body(p_ref, d_ref, o_ref):
        s = _softmax_layers(p_ref, n_layers)
        ds = [jnp.zeros((1, D), F32)] * n_layers
        acc = jnp.zeros((1, D), F32)
        for l in range(n_layers - 1, 0, -1):
            acc = acc + d_ref[l:l + 1, :]
            ds[l] = acc
        dot = functools.reduce(lambda a, b: a + b, [s[l] * ds[l] for l in range(n_layers)])
        for l in range(n_layers):
            o_ref[l:l + 1, :] = s[l] * (ds[l] - dot)

    return pl.pallas_call(body, out_shape=SDS(p.shape, F32), name="lb_bwd")(p, dlb)


def _adamw(w, g, m, v, name):
    R, C = w.shape
    tr = _tile(R, 256, 8) if R % 8 == 0 else R

    def body(w_ref, g_ref, m_ref, v_ref, d_ref, mo_ref, vo_ref):
        g_ = g_ref[...]
        m_ = ADAM_B1 * m_ref[...] + (1.0 - ADAM_B1) * g_
        v_ = ADAM_B2 * v_ref[...] + (1.0 - ADAM_B2) * (g_ * g_)
        mh = m_ / (1.0 - ADAM_B1 ** ADAM_STEP)
        vh = v_ / (1.0 - ADAM_B2 ** ADAM_STEP)
        d_ref[...] = -ADAM_LR * (mh / (jnp.sqrt(vh) + ADAM_EPS) + ADAM_WD * w_ref[...])
        mo_ref[...] = m_
        vo_ref[...] = v_

    blk = pl.BlockSpec((tr, C), lambda i: (i, 0))
    return pl.pallas_call(
        body, grid=(R // tr,), in_specs=[blk] * 4, out_specs=[blk] * 3, out_shape=[SDS((R, C), F32)] * 3,
        name=name, compiler_params=_params("parallel"))(w, g, m, v)


def _adamw_layer(w3, m3, v3, g2, layer, outs, name):
    L, R, C = w3.shape
    tr = _tile(R, 256, 8)
    if outs is None:
        outs = tuple(lax.empty(w3.shape, F32) for _ in range(4))

    def body(w_ref, m_ref, v_ref, g_ref, a0, a1, a2, a3, go_ref, d_ref, mo_ref, vo_ref):
        del a0, a1, a2, a3
        g_ = g_ref[...]
        m_ = ADAM_B1 * m_ref[...] + (1.0 - ADAM_B1) * g_
        v_ = ADAM_B2 * v_ref[...] + (1.0 - ADAM_B2) * (g_ * g_)
        mh = m_ / (1.0 - ADAM_B1 ** ADAM_STEP)
        vh = v_ / (1.0 - ADAM_B2 ** ADAM_STEP)
        go_ref[...] = g_
        d_ref[...] = -ADAM_LR * (mh / (jnp.sqrt(vh) + ADAM_EPS) + ADAM_WD * w_ref[...])
        mo_ref[...] = m_
        vo_ref[...] = v_

    lay = pl.BlockSpec((None, tr, C), lambda i: (layer, i, 0))
    return pl.pallas_call(
        body, grid=(R // tr,), in_specs=[lay] * 3 + [pl.BlockSpec((tr, C), lambda i: (i, 0))] + [ANY_SPEC] * 4,
        out_specs=[lay] * 4, out_shape=[SDS(w3.shape, F32)] * 4, input_output_aliases={4: 0, 5: 1, 6: 2, 7: 3},
        name=name, compiler_params=_params("parallel"))(w3, m3, v3, g2, *outs)


SEM_SPEC = pl.BlockSpec(memory_space=pltpu.SEMAPHORE)
HBM_SPEC = pl.BlockSpec(memory_space=pltpu.HBM)
EFFECT = pltpu.SideEffectType.DATAFLOW_SIDE_EFFECTING
N_DEV = 2 * N_CHIPS


def _position():
    x, y, c = lax.axis_index("x"), lax.axis_index("y"), lax.axis_index("c")
    chips = [(1 - x, y), (x, 1 - y), (1 - x, 1 - y)]
    return x, y, c, chips


def _split_start(name, plan, bufs, n_sems, deps=(), earlier=None):
    n = len(bufs)
    held = () if earlier is None else tuple(earlier[1:])

    def body(*refs):
        first_out = n + len(held) + len(deps)
        if earlier is not None:
            sends, recvs = earlier[0](refs[:n], refs[n], refs[n + 1])
            for kw in sends:
                pltpu.make_async_remote_copy(**kw).wait_send()
            for kw in recvs:
                pltpu.make_async_remote_copy(**kw).wait_recv()
        sends, _ = plan(refs[:n], refs[first_out], refs[first_out + 1])
        for kw in sends:
            pltpu.make_async_remote_copy(**kw).start()
        refs[-1][...] = jnp.zeros_like(refs[-1])

    out = pl.pallas_call(
        body, name=name,
        out_shape=(pltpu.SemaphoreType.DMA((n_sems,)), pltpu.SemaphoreType.DMA((n_sems,)),
                   *[pltpu.HBM(b.shape, b.dtype) for b in bufs], SDS((8, 128), F32)),
        in_specs=[HBM_SPEC] * n + [SEM_SPEC] * len(held) + [ANY_SPEC] * len(deps),
        out_specs=(SEM_SPEC, SEM_SPEC, *[HBM_SPEC] * n, pl.BlockSpec(memory_space=pltpu.VMEM)),
        input_output_aliases={i: 2 + i for i in range(n)},
        compiler_params=pltpu.CompilerParams(has_side_effects=EFFECT),
    )(*[pltpu.with_memory_space_constraint(b, pltpu.HBM) for b in bufs], *held, *deps)
    return out[0], out[1], list(out[2:2 + n]), out[-1]


def _split_wait(name, plan, send_sems, recv_sems, bufs, after=()):
    n = len(bufs)

    def body(*refs):
        sends, recvs = plan(refs[:n], refs[n], refs[n + 1])
        for kw in sends:
            pltpu.make_async_remote_copy(**kw).wait_send()
        for kw in recvs:
            pltpu.make_async_remote_copy(**kw).wait_recv()

    out = pl.pallas_call(
        body, name=name, out_shape=tuple(pltpu.HBM(b.shape, b.dtype) for b in bufs),
        in_specs=[HBM_SPEC] * n + [SEM_SPEC, SEM_SPEC] + [ANY_SPEC] * len(after),
        out_specs=tuple([HBM_SPEC] * n), input_output_aliases={i: i for i in range(n)},
        compiler_params=pltpu.CompilerParams(has_side_effects=EFFECT),
    )(*bufs, send_sems, recv_sems, *after)
    return list(out)


def _region(kind, ref, chip, half):
    K, N = ref.shape
    if kind == "col":
        return ref.at[pl.ds(half * (K // 2), K // 2), pl.ds(chip * (N // N_CHIPS), N // N_CHIPS)]
    rows = K // (2 * N_CHIPS)
    return ref.at[pl.ds((2 * chip + half) * rows, rows), :]


def _gather_plan(kinds, over_chips):
    def plan(refs, send_sems, recv_sems):
        x, y, c, chips = _position()
        sends, recvs = [], []
        for f, (ref, kind) in enumerate(zip(refs, kinds)):
            for k, chip in enumerate(chips):
                theirs = 2 * chip[0] + chip[1]
                sem = dict(send_sem=send_sems.at[3 * f + k], recv_sem=recv_sems.at[3 * f + k], device_id_type=MESH)
                if over_chips:
                    out, back, to = _region(kind, ref, 2 * x + y, c), _region(kind, ref, theirs, c), (*chip, c)
                else:
                    out, back, to = _region(kind, ref, theirs, c), _region(kind, ref, theirs, 1 - c), (x, y, 1 - c)
                sends.append(dict(src_ref=out, dst_ref=out, device_id=to, **sem))
                recvs.append(dict(src_ref=back, dst_ref=back, device_id=to, **sem))
        return sends, recvs
    return plan


def _reduce_plan(refs, send_sems, recv_sems):
    x, y, c, _ = _position()
    me = 4 * x + 2 * y + c
    sends, recvs = [], []
    for f in range(len(refs) // 2):
        acc, land = refs[2 * f], refs[2 * f + 1]
        for d in range(1, N_DEV):
            t = (me + d) % N_DEV
            to = dict(device_id=(t // 4, (t // 2) % 2, t % 2), device_id_type=MESH)
            slot = N_DEV - 1 - d
            sends.append(dict(src_ref=acc.at[t % 2, t // 2], dst_ref=land.at[slot], send_sem=send_sems.at[7 * f + d - 1],
                              recv_sem=recv_sems.at[7 * f + slot], **to))
            recvs.append(dict(src_ref=land.at[d - 1], dst_ref=land.at[d - 1], send_sem=send_sems.at[7 * f + d - 1],
                              recv_sem=recv_sems.at[7 * f + d - 1], **to))
    return sends, recvs


def _swap_plan(refs, send_sems, recv_sems):
    x, y, c, _ = _position()
    sends, recvs = [], []
    for f, g in enumerate(refs):
        sem = dict(send_sem=send_sems.at[f], recv_sem=recv_sems.at[f], device_id=(x, y, 1 - c), device_id_type=MESH)
        sends.append(dict(src_ref=g.at[c], dst_ref=g.at[c], **sem))
        recvs.append(dict(src_ref=g.at[1 - c], dst_ref=g.at[1 - c], **sem))
    return sends, recvs


def _sum_pieces(ids2, acc, land, name):
    _, _, nr, nc = acc.shape
    tr = _tile(nr, 256, 16)

    def body(ids_ref, own_ref, land_ref, o_ref):
        del ids_ref
        s = own_ref[...].astype(F32)
        for k in range(N_DEV - 1):
            s = s + land_ref[k].astype(F32)
        o_ref[...] = s

    return pl.pallas_call(
        body,
        grid_spec=pltpu.PrefetchScalarGridSpec(
            num_scalar_prefetch=1, grid=(nr // tr,),
            in_specs=[pl.BlockSpec((None, None, tr, nc), lambda i, ids: (ids[0], ids[1], i, 0)),
                      pl.BlockSpec((N_DEV - 1, tr, nc), lambda i, ids: (0, i, 0))],
            out_specs=pl.BlockSpec((None, tr, nc), lambda i, ids: (ids[0], i, 0))),
        out_shape=SDS((2, nr, nc), F32), name=name, compiler_params=_params("parallel"))(ids2, acc, land)


def _all_sum(block):
    m_per, n = block.shape

    def body(x_ref, all_ref, sum_ref, send_sems, recv_sems, local_sem):
        x, y, c, chips = _position()
        me, sibling = (x, y, c), (x, y, 1 - c)

        def rows(px, py, pc):
            return all_ref.at[pl.ds((4 * px + 2 * py + pc) * m_per, m_per), :]

        def copy(k, blk, to, src=None):
            return pltpu.make_async_remote_copy(
                src_ref=rows(*blk) if src is None else src, dst_ref=rows(*blk), send_sem=send_sems.at[k],
                recv_sem=recv_sems.at[k], device_id=to, device_id_type=MESH)

        mine = pltpu.make_async_copy(x_ref, rows(*me), local_sem)
        mine.start()
        first = [copy(0, me, sibling, src=x_ref)]
        first += [copy(1 + k, me, (*chip, c), src=x_ref) for k, chip in enumerate(chips)]
        for cp in first:
            cp.start()
        passed = [copy(4 + k, (*chip, c), sibling) for k, chip in enumerate(chips)]
        for k, chip in enumerate(chips):
            copy(1 + k, (*chip, c), me).wait_recv()
            passed[k].start()
        copy(0, sibling, me).wait_recv()
        for k, chip in enumerate(chips):
            copy(4 + k, (*chip, 1 - c), me).wait_recv()
        for cp in first + passed:
            cp.wait_send()
        mine.wait()
        acc = all_ref[0:m_per, :]
        for d in range(1, N_DEV):
            acc = acc + all_ref[d * m_per:(d + 1) * m_per, :]
        sum_ref[...] = acc

    vm = pl.BlockSpec(memory_space=pltpu.VMEM)
    return pl.pallas_call(
        body, in_specs=[vm], out_specs=[vm, vm], out_shape=[SDS((N_DEV * m_per, n), F32), SDS((m_per, n), F32)],
        scratch_shapes=[pltpu.SemaphoreType.DMA((7,)), pltpu.SemaphoreType.DMA((7,)), pltpu.SemaphoreType.DMA],
        name="all_sum", compiler_params=_params())(block)[1]


BIG = {"ev_w_in": "col", "ev_w_out": "row", "od_w_in": "col", "od_w_out": "row", "mlp_w1": "col", "mlp_w2": "row"}
WEIGHTS = ("meta_tokens", "mix_norm_g", "mlp_norm_g", "final_norm_g", "ev_w_in", "ev_conv_w", "ev_conv_b", "ev_ln_g",
           "ev_ln_b", "ev_pool_w", "ev_pool_b", "ev_pool_scale", "ev_w_out", "od_w_in", "od_gnorm_g", "od_w_out",
           "lb_param", "mlp_w1", "mlp_w2")
PACK_UNIT = 1024


def _mixer_names(layer):
    return ("ev_w_in", "ev_w_out") if layer % 2 == 0 else ("od_w_in", "od_w_out")


def _pack(arrays):
    flat = []
    for a in arrays:
        a = a.reshape(-1)
        flat.append(jnp.pad(a, (0, (-a.shape[0]) % PACK_UNIT)))
    return jnp.concatenate(flat).reshape(-1, 128)


def _unpack(packed, shapes):
    flat = packed.reshape(-1)
    out, off = [], 0
    for s in shapes:
        size = 1
        for d in s:
            size *= d
        out.append(flat[off:off + size].reshape(s))
        off += size + (-size) % PACK_UNIT
    return out


def _local_step(x2, target, P, weights, boundary, first_deps=()):
    D = x2.shape[1]
    n_layers = P["mix_norm_g"].shape[0]
    h = jnp.concatenate([jnp.zeros((PAD, D), F32), P["meta_full"], x2], axis=0)
    mix_g = P["mix_norm_g"].reshape(n_layers, 1, D)
    mlp_g = P["mlp_norm_g"].reshape(n_layers, 1, D)
    vec = lambda a: a.reshape(a.shape[0], 1, -1)
    cb3, lg3, lnb3, ps3 = vec(P["ev_conv_b"]), vec(P["ev_ln_g"]), vec(P["ev_ln_b"]), vec(P["ev_pool_scale"])
    pb3 = vec(P["ev_pool_b"])
    gn3 = vec(P["od_gnorm_g"])
    lb_all = _lb_fwd(P["lb_param"])
    lb3 = lb_all.reshape(n_layers, 1, D)
    even = (cb3, lg3, lnb3, P["ev_pool_w"], pb3, ps3)

    saved = []
    deps = tuple(first_deps)
    for layer in range(n_layers):
        j = layer // 2
        w_in, w_out = _mixer_names(layer)
        W = {}
        s = {"h": h, "W": W}
        s["n"] = _rms_fwd(h, mix_g, layer, "mix_norm_0", deps=deps) if layer == 0 else n_next
        deps = ()
        W[w_in] = weights(layer, w_in, (s["n"],))
        s["u"] = _mm_nn(s["n"], W[w_in], 0, f"mix_in_{layer}")
        if layer % 2 == 0:
            s["y"] = _even_fwd(s["u"], P["conv_w_full"], *even, j, f"even_fwd_{layer}")
        else:
            s["y"], s["o"], s["sall"] = _hgrn_fwd(s["u"], lb3, layer, gn3, j, f"hgrn_fwd_{layer}")
        W[w_out] = weights(layer, w_out, (s["y"],))
        h, s["n2"] = _mm_nn_norm(s["y"], W[w_out], 0, h, mlp_g, layer, f"mix_out_{layer}")
        s["h1"] = h
        W["mlp_w1"] = weights(layer, "mlp_w1", (s["n2"],))
        s["act"], s["relu"] = _mm_nn(s["n2"], W["mlp_w1"], 0, f"mlp_up_{layer}", relu2=True)
        W["mlp_w2"] = weights(layer, "mlp_w2", (s["act"],))
        if layer + 1 < n_layers:
            h, n_next = _mm_nn_norm(s["act"], W["mlp_w2"], 0, h, mix_g, layer + 1, f"mlp_down_{layer}")
        else:
            h = _mm_nn(s["act"], W["mlp_w2"], 0, f"mlp_down_{layer}", res=h)
        saved.append(s)

    dh, dhb, dg_final, loss = _final(h, P["final_norm_g"].reshape(1, D), target)

    small = {"final_norm_g": dg_final}
    per_layer = {k: [None] * n_layers for k in ("mix_norm_g", "mlp_norm_g", "lb")}
    per_pair = {k: [None] * (n_layers // 2) for k in
                ("ev_conv_w", "ev_conv_b", "ev_ln_g", "ev_ln_b", "ev_pool_w", "ev_pool_b", "ev_pool_scale", "od_gnorm_g")}
    for layer in reversed(range(n_layers)):
        j = layer // 2
        s = saved[layer]
        W = s["W"]
        w_in, w_out = _mixer_names(layer)
        dz = _mm_nt(dhb, W["mlp_w2"], 0, f"d_act_{layer}", relu=s["relu"], deps=deps)
        dw2 = _mm_tn(s["act"], dhb, "row", f"dw2_{layer}")
        dw1 = _mm_tn(s["n2"], dz, "col", f"dw1_{layer}")
        dh, dhb, per_layer["mlp_norm_g"][layer] = _mm_nt_norm(dz, W["mlp_w1"], 0, s["h1"], mlp_g, layer, dh, f"d_n2_{layer}")
        deps = boundary(f"mlp{layer}", {("mlp_w1", layer): dw1, ("mlp_w2", layer): dw2}, (dhb,))
        dy = _mm_nt(dhb, W[w_out], 0, f"d_y_{layer}", deps=deps)
        dwout = _mm_tn(s["y"], dhb, "row", f"dwout_{layer}")
        if layer % 2 == 0:
            du, dcw, dcb, dlg, dlnb, dpw, dpb, dps = _even_bwd(s["u"], dy, P["conv_w_full"], *even, j, f"even_bwd_{layer}")
            for k, val in (("ev_conv_w", dcw), ("ev_conv_b", dcb), ("ev_ln_g", dlg), ("ev_ln_b", dlnb),
                           ("ev_pool_w", dpw), ("ev_pool_b", dpb), ("ev_pool_scale", dps)):
                per_pair[k][j] = val
        else:
            du, per_layer["lb"][layer], per_pair["od_gnorm_g"][j] = _hgrn_bwd(
                s["u"], s["o"], dy, s["sall"], lb3, layer, gn3, j, f"hgrn_bwd_{layer}")
        dwin = _mm_tn(s["n"], du, "col", f"dwin_{layer}")
        deps = boundary(f"mix{layer}", {(w_in, j): dwin, (w_out, j): dwout}, (du,))
        dh, dhb, per_layer["mix_norm_g"][layer] = _mm_nt_norm(du, W[w_in], 0, s["h"], mix_g, layer, dh, f"d_n_{layer}", deps=deps)
        deps = ()

    small["mix_norm_g"] = jnp.concatenate(per_layer["mix_norm_g"], axis=0)
    small["mlp_norm_g"] = jnp.concatenate(per_layer["mlp_norm_g"], axis=0)
    dlb_all = jnp.concatenate([jnp.zeros((1, D), F32) if g is None else g for g in per_layer["lb"]], axis=0)
    small["lb_param"] = _lb_bwd(P["lb_param"], dlb_all)
    for k, vals in per_pair.items():
        small[k] = jnp.stack(vals, axis=0)
    small["meta_tokens"] = dh[PAD:LEAD]
    return loss, dh, small


def kernel(x, meta_tokens, mix_norm_g, mlp_norm_g, final_norm_g, ev_w_in, ev_conv_w, ev_conv_b, ev_ln_g, ev_ln_b, ev_pool_w, ev_pool_b, ev_pool_scale, ev_w_out, od_w_in, od_gnorm_g, od_w_out, lb_param, mlp_w1, mlp_w2, loss_target, m_meta_tokens, m_mix_norm_g, m_mlp_norm_g, m_final_norm_g, m_ev_w_in, m_ev_conv_w, m_ev_conv_b, m_ev_ln_g, m_ev_ln_b, m_ev_pool_w, m_ev_pool_b, m_ev_pool_scale, m_ev_w_out, m_od_w_in, m_od_gnorm_g, m_od_w_out, m_lb_param, m_mlp_w1, m_mlp_w2, v_meta_tokens, v_mix_norm_g, v_mlp_norm_g, v_final_norm_g, v_ev_w_in, v_ev_conv_w, v_ev_conv_b, v_ev_ln_g, v_ev_ln_b, v_ev_pool_w, v_ev_pool_b, v_ev_pool_scale, v_ev_w_out, v_od_w_in, v_od_gnorm_g, v_od_w_out, v_lb_param, v_mlp_w1, v_mlp_w2):
    given = dict(locals())
    w = {n: given[n] for n in WEIGHTS}
    m = {n: given["m_" + n] for n in WEIGHTS}
    v = {n: given["v_" + n] for n in WEIGHTS}
    n_layers = mix_norm_g.shape[0]
    core = lax.axis_index("c").astype(jnp.int32)
    chip = (2 * lax.axis_index("x") + lax.axis_index("y")).astype(jnp.int32)
    chip1 = chip.reshape(1)
    ids2 = jnp.stack([core, chip])

    conv_pad = jnp.pad(ev_conv_w, ((0, 0), (0, CONV_ROWS - CONV_WIDTH), (0, 0)))
    stages = [[(0, n)] for n in (*_mixer_names(0), "mlp_w1", "mlp_w2")]
    stages += [[(layer, n) for n in (*_mixer_names(layer), "mlp_w1", "mlp_w2")] for layer in range(1, n_layers)]
    gathers, where, token = [], {}, ()
    for k, stage in enumerate(stages):
        index = [layer if n.startswith("mlp") else layer // 2 for layer, n in stage]
        kinds = [BIG[n] for _, n in stage]
        bufs = [_cast_place(w[n], i, BIG[n], chip1, BF16, f"place_{n}_{i}") for (_, n), i in zip(stage, index)]
        if k == 0:
            bufs.append(_cast_place(meta_tokens[None], 0, "col", chip1, F32, "place_meta"))
            bufs.append(_cast_place(conv_pad.reshape(1, -1, conv_pad.shape[2]), 0, "col", chip1, F32, "place_conv_w"))
            kinds += ["col", "col"]
        plan = _gather_plan(kinds, True)
        ss, rs, bufs, tok = _split_start(f"gather_start_{k}", plan, bufs, 3 * len(bufs), deps=token)
        token = (tok,)
        gathers.append((kinds, plan, ss, rs, bufs))
        where.update({key: (k, f) for f, key in enumerate(stage)})

    landed = {}

    def arrived(k, after):
        if k not in landed:
            kinds, plan, ss, rs, bufs = gathers[k]
            hand_on = _gather_plan(kinds, False)
            ss, rs, bufs, _ = _split_start(f"gather_pass_{k}", hand_on, bufs, 3 * len(bufs), deps=after, earlier=(plan, ss, rs))
            landed[k] = _split_wait(f"gather_wait_{k}", hand_on, ss, rs, bufs)
        return landed[k]

    def weights(layer, name, after):
        k, f = where[(layer, name)]
        return arrived(k, after)[f][None]

    first = arrived(0, ())
    P = {n: w[n] for n in ("mix_norm_g", "mlp_norm_g", "final_norm_g", "ev_conv_b", "ev_ln_g", "ev_ln_b", "ev_pool_w",
                           "ev_pool_b", "ev_pool_scale", "od_gnorm_g", "lb_param")}
    P["meta_full"] = first[1]
    P["conv_w_full"] = first[2].reshape(ev_conv_w.shape[0], CONV_ROWS, -1)

    pending, outs = [], {n: None for n in BIG}

    def advance(after):
        tokens, still = [], []
        for st in pending:
            if st["phase"] == 1:
                bufs = _split_wait(f"reduce_wait_{st['tag']}", _reduce_plan, st["ss"], st["rs"], st["bufs"], after)
                halves = [_sum_pieces(ids2, bufs[2 * f], bufs[2 * f + 1], f"sum_{st['tag']}_{f}") for f in range(len(bufs) // 2)]
                ss, rs, halves, tok = _split_start(f"swap_start_{st['tag']}", _swap_plan, halves, len(halves))
                tokens.append(tok)
                still.append(dict(st, phase=2, ss=ss, rs=rs, bufs=halves))
            else:
                grads = _split_wait(f"swap_wait_{st['tag']}", _swap_plan, st["ss"], st["rs"], st["bufs"], after)
                for (n, i), g in zip(st["keys"], grads):
                    outs[n] = _adamw_layer(w[n], m[n], v[n], g.reshape(w[n].shape[1:]), i, outs[n], f"adamw_{n}_{i}")
        pending[:] = still
        return tokens

    def boundary(tag, grads, after):
        tokens = advance(after)
        bufs = []
        for acc in grads.values():
            bufs += [acc, lax.empty((N_DEV - 1,) + acc.shape[2:], BF16)]
        ss, rs, bufs, tok = _split_start(f"reduce_start_{tag}", _reduce_plan, bufs, 7 * len(grads))
        pending.append(dict(phase=1, tag=tag, keys=list(grads), ss=ss, rs=rs, bufs=bufs))
        return tuple(tokens + [tok])

    loss, dh, small = _local_step(x[0], loss_target[0], P, weights, boundary, first_deps=token)

    order = [n for n in WEIGHTS if n not in BIG]
    block = _pack([small[n] for n in order] + [loss])
    advance((block,))
    packed = _all_sum(block)
    advance((packed,))
    total = _unpack(packed, [small[n].shape for n in order] + [loss.shape])
    loss_sum = total[-1][0, 0]
    gsmall = dict(zip(order, total[:-1]))
    gsmall["meta_tokens"] = lax.dynamic_slice_in_dim(gsmall["meta_tokens"], chip * meta_tokens.shape[1], meta_tokens.shape[1], 1)
    gsmall["ev_conv_w"] = lax.dynamic_slice_in_dim(gsmall["ev_conv_w"][:, :CONV_WIDTH], chip * ev_conv_w.shape[2], ev_conv_w.shape[2], 2)

    g_out, d_out, m_out, v_out = {}, {}, {}, {}
    for n in WEIGHTS:
        if n in BIG:
            g_out[n], d_out[n], m_out[n], v_out[n] = outs[n]
            continue
        shape = w[n].shape
        g = gsmall[n].reshape(shape)
        cols = shape[-1] if len(shape) > 1 else 128
        two = lambda a: a.reshape(-1, cols)
        d_, m_, v_ = _adamw(two(w[n]), two(g), two(m[n]), two(v[n]), f"adamw_{n}")
        g_out[n], d_out[n], m_out[n], v_out[n] = g, d_.reshape(shape), m_.reshape(shape), v_.reshape(shape)

    grad_x = dh[LEAD:][None]
    return (loss_sum, grad_x, *[g_out[n] for n in WEIGHTS], *[d_out[n] for n in WEIGHTS],
            *[m_out[n] for n in WEIGHTS], *[v_out[n] for n in WEIGHTS])
```

```python
import functools

import jax
import jax.numpy as jnp
from jax import lax
from jax.experimental import pallas as pl
from jax.experimental.pallas import tpu as pltpu

F32 = jnp.float32
BF16 = jnp.bfloat16
SDS = jax.ShapeDtypeStruct
MESH = pl.DeviceIdType.MESH
ANY_SPEC = pl.BlockSpec(memory_space=pl.ANY)

N_META = 16
CHUNK = 64
LEAD = CHUNK
PAD = LEAD - N_META
CONV_WIDTH = 31
CONV_ROWS = 32
POOL_WINDOWS = (2, 4, 8, 16)
HEAD_DIM = 128
SUB = 16
EXP_CAP = 80.0
EPS = 1e-6
ADAM_LR = 0.001
ADAM_B1 = 0.9
ADAM_B2 = 0.999
ADAM_EPS = 1e-08
ADAM_WD = 0.01
ADAM_STEP = 10
N_CHIPS = 4
VMEM_LIMIT = 52 << 20
MM_VMEM_BUDGET = 44 << 20


def _params(*sem):
    return pltpu.CompilerParams(dimension_semantics=sem if sem else None, vmem_limit_bytes=VMEM_LIMIT)


def _tile(n, target, unit=CHUNK):
    best = None
    for t in range(unit, min(n, target) + 1, unit):
        if n % t == 0:
            best = t
    assert best is not None, (n, target, unit)
    return best


def _ctile(n, target=512):
    for t in (512, 384, 256, 128):
        if t <= target and n % t == 0:
            return t
    raise ValueError(n)


def _mm_tiles(M, N, per_row, per_col, per_elem):
    best = None
    for tn in (512, 384, 256, 128):
        if N % tn:
            continue
        for tm in sorted((d for d in range(16, M + 1, 16) if M % d == 0), reverse=True):
            if 2 * (tm * per_row + tn * per_col + tm * tn * per_elem) <= MM_VMEM_BUDGET:
                if best is None or tm * tn > best[0] * best[1]:
                    best = (tm, tn)
                break
    assert best is not None, (M, N)
    return best


def _sigmoid(x):
    return 1.0 / (1.0 + jnp.exp(-x))


def _row_ids(shape, base):
    return lax.broadcasted_iota(jnp.int32, shape, 0) + base


def _cast_place(w3, layer, kind, chip1, dtype, name):
    _, ks, ns = w3.shape
    tr = _tile(ks, 512, 16)
    full = (ks, ns * N_CHIPS) if kind == "col" else (ks * N_CHIPS, ns)

    def body(chip_ref, w_ref, o_ref):
        del chip_ref
        o_ref[...] = w_ref[...].astype(dtype)

    omap = (lambda i, chip: (i, chip[0])) if kind == "col" else (lambda i, chip: (chip[0] * (ks // tr) + i, 0))
    return pl.pallas_call(
        body,
        grid_spec=pltpu.PrefetchScalarGridSpec(
            num_scalar_prefetch=1, grid=(ks // tr,),
            in_specs=[pl.BlockSpec((None, tr, ns), lambda i, chip: (layer, i, 0))],
            out_specs=pl.BlockSpec((tr, ns), omap)),
        out_shape=SDS(full, dtype), name=name, compiler_params=_params("parallel"))(chip1, w3)


def _rms_fwd(h, g3, layer, name, deps=()):
    T, D = h.shape
    tm = _tile(T, 832)

    def body(h_ref, g_ref, *rest):
        n_ref = rest[-1]
        x = h_ref[...]
        r = lax.rsqrt(jnp.mean(x * x, axis=-1, keepdims=True) + EPS)
        n_ref[...] = ((x * r) * g_ref[...]).astype(BF16)

    return pl.pallas_call(
        body, grid=(T // tm,),
        in_specs=[pl.BlockSpec((tm, D), lambda i: (i, 0)), pl.BlockSpec((None, 1, D), lambda i: (layer, 0, 0))]
        + [ANY_SPEC] * len(deps),
        out_specs=pl.BlockSpec((tm, D), lambda i: (i, 0)), out_shape=SDS((T, D), BF16),
        name=name, compiler_params=_params("parallel"))(h, g3, *deps)


def _final(h, g2, target):
    T, D = h.shape
    tm = _tile(T, 320)
    nsub = tm // CHUNK
    nblk = target.shape[0] // CHUNK

    def body(h_ref, g_ref, *rest):
        t_refs = rest[:nsub]
        dh_ref, dhb_ref, dg_ref, loss_ref = rest[nsub:]
        i = pl.program_id(0)

        @pl.when(i == 0)
        def _():
            dg_ref[...] = jnp.zeros_like(dg_ref)
            loss_ref[...] = jnp.zeros_like(loss_ref)

        g = g_ref[...]
        for q in range(nsub):
            rows = slice(q * CHUNK, (q + 1) * CHUNK)
            x = h_ref[rows, :]
            r = lax.rsqrt(jnp.mean(x * x, axis=-1, keepdims=True) + EPS)
            xh = x * r
            live = jnp.where(i * nsub + q > 0, 1.0, 0.0).astype(F32)
            e = ((xh * g) - t_refs[q][...]) * live
            dy = e * (1.0 / D)
            dxh = dy * g
            dh = r * (dxh - xh * jnp.mean(dxh * xh, axis=-1, keepdims=True))
            dh_ref[rows, :] = dh
            dhb_ref[rows, :] = dh.astype(BF16)
            dg_ref[...] += jnp.sum(dy * xh, axis=0, keepdims=True)
            loss_ref[...] += jnp.sum(e * e) * (0.5 / D)

    row = pl.BlockSpec((tm, D), lambda i: (i, 0))
    t_specs = [pl.BlockSpec((CHUNK, D), functools.partial(lambda i, q: (jnp.clip(i * nsub + q - 1, 0, nblk - 1), 0), q=q))
               for q in range(nsub)]
    return pl.pallas_call(
        body, grid=(T // tm,),
        in_specs=[row, pl.BlockSpec((1, D), lambda i: (0, 0))] + t_specs,
        out_specs=[row, row, pl.BlockSpec((1, D), lambda i: (0, 0)), pl.BlockSpec((1, 128), lambda i: (0, 0))],
        out_shape=[SDS((T, D), F32), SDS((T, D), BF16), SDS((1, D), F32), SDS((1, 128), F32)],
        name="final_loss", compiler_params=_params("arbitrary"))(h, g2, *([target] * nsub))


def _mm_nn(a, w3, layer, name, res=None, relu2=False):
    M, K = a.shape
    N = w3.shape[2]
    tm, tn = _mm_tiles(M, N, 2 * K, 2 * K, (4 if relu2 else 4) + (4 if res is not None else 0))

    def body(*refs):
        acc = jnp.dot(refs[0][...], refs[1][...], preferred_element_type=F32)
        if res is not None:
            acc = acc + refs[2][...]
        if relu2:
            p = jnp.maximum(acc, 0.0)
            refs[-2][...] = (p * p).astype(BF16)
            refs[-1][...] = p.astype(BF16)
        else:
            refs[-1][...] = acc

    in_specs = [pl.BlockSpec((tm, K), lambda i, j: (i, 0)), pl.BlockSpec((None, K, tn), lambda i, j: (layer, 0, j))]
    args = [a, w3]
    tile = pl.BlockSpec((tm, tn), lambda i, j: (i, j))
    if res is not None:
        in_specs.append(tile)
        args.append(res)
    return pl.pallas_call(
        body, grid=(M // tm, N // tn), in_specs=in_specs, out_specs=[tile, tile] if relu2 else tile,
        out_shape=[SDS((M, N), BF16)] * 2 if relu2 else SDS((M, N), F32),
        name=name, compiler_params=_params("parallel", "parallel"))(*args)


def _mm_nt(dy, w3, layer, name, relu=None, deps=()):
    M, N = dy.shape
    K = w3.shape[1]
    tm, tk = _mm_tiles(M, K, 2 * N, 2 * N, 4)

    def body(*refs):
        acc = lax.dot_general(refs[0][...], refs[1][...], (((1,), (1,)), ((), ())), preferred_element_type=F32)
        if relu is not None:
            acc = (acc * (2.0 * refs[2][...].astype(F32))).astype(BF16)
        refs[-1][...] = acc

    tile = pl.BlockSpec((tm, tk), lambda i, j: (i, j))
    in_specs = [pl.BlockSpec((tm, N), lambda i, j: (i, 0)), pl.BlockSpec((None, tk, N), lambda i, j: (layer, j, 0))]
    args = [dy, w3]
    if relu is not None:
        in_specs.append(tile)
        args.append(relu)
    in_specs += [ANY_SPEC] * len(deps)
    args += list(deps)
    return pl.pallas_call(
        body, grid=(M // tm, K // tk), in_specs=in_specs, out_specs=tile,
        out_shape=SDS((M, K), F32 if relu is None else BF16),
        name=name, compiler_params=_params("parallel", "parallel"))(*args)


def _row_tile(M, per_row, fixed):
    for tm in sorted((d for d in range(16, M + 1, 16) if M % d == 0), reverse=True):
        if 2 * (tm * per_row + fixed) <= MM_VMEM_BUDGET:
            return tm
    raise ValueError((M, per_row, fixed))


def _mm_nn_norm(a, w3, layer, res, g3, glayer, name):
    M, K = a.shape
    D = w3.shape[2]
    tm = _row_tile(M, 2 * K + 10 * D, 2 * K * D)

    def body(a_ref, w_ref, r_ref, g_ref, h_ref, n_ref):
        x = r_ref[...] + jnp.dot(a_ref[...], w_ref[...], preferred_element_type=F32)
        h_ref[...] = x
        r = lax.rsqrt(jnp.mean(x * x, axis=-1, keepdims=True) + EPS)
        n_ref[...] = ((x * r) * g_ref[...]).astype(BF16)

    row = pl.BlockSpec((tm, D), lambda i: (i, 0))
    return pl.pallas_call(
        body, grid=(M // tm,),
        in_specs=[pl.BlockSpec((tm, K), lambda i: (i, 0)), pl.BlockSpec((None, K, D), lambda i: (layer, 0, 0)), row,
                  pl.BlockSpec((None, 1, D), lambda i: (glayer, 0, 0))],
        out_specs=[row, row], out_shape=[SDS((M, D), F32), SDS((M, D), BF16)],
        name=name, compiler_params=_params("parallel"))(a, w3, res, g3)


def _mm_nt_norm(dy, w3, layer, h, g3, glayer, dh_in, name, deps=()):
    M, N = dy.shape
    D = w3.shape[1]
    tm = _row_tile(M, 2 * N + 14 * D, 2 * N * D)

    def body(dy_ref, w_ref, h_ref, g_ref, dhi_ref, *rest):
        dh_ref, dhb_ref, dg_ref = rest[-3:]
        dn = lax.dot_general(dy_ref[...], w_ref[...], (((1,), (1,)), ((), ())), preferred_element_type=F32)
        x = h_ref[...]
        r = lax.rsqrt(jnp.mean(x * x, axis=-1, keepdims=True) + EPS)
        xh = x * r
        dxh = dn * g_ref[...]
        dh = dhi_ref[...] + r * (dxh - xh * jnp.mean(dxh * xh, axis=-1, keepdims=True))
        dh_ref[...] = dh
        dhb_ref[...] = dh.astype(BF16)

        @pl.when(pl.program_id(0) == 0)
        def _():
            dg_ref[...] = jnp.zeros_like(dg_ref)

        dg_ref[...] += jnp.sum(dn * xh, axis=0, keepdims=True)

    row = pl.BlockSpec((tm, D), lambda i: (i, 0))
    return pl.pallas_call(
        body, grid=(M // tm,),
        in_specs=[pl.BlockSpec((tm, N), lambda i: (i, 0)), pl.BlockSpec((None, D, N), lambda i: (layer, 0, 0)), row,
                  pl.BlockSpec((None, 1, D), lambda i: (glayer, 0, 0)), row] + [ANY_SPEC] * len(deps),
        out_specs=[row, row, pl.BlockSpec((1, D), lambda i: (0, 0))],
        out_shape=[SDS((M, D), F32), SDS((M, D), BF16), SDS((1, D), F32)],
        name=name, compiler_params=_params("arbitrary"))(dy, w3, h, g3, dh_in, *deps)


def _fam_dims(kind, K, N):
    return (K // 2, N // N_CHIPS) if kind == "col" else (K // (2 * N_CHIPS), N)


def _mm_tn(x, dy, kind, name):
    M, K = x.shape
    N = dy.shape[1]
    nr, nc = _fam_dims(kind, K, N)

    def body(x_ref, dy_ref, o_ref):
        res = lax.dot_general(x_ref[...], dy_ref[...], (((0,), (0,)), ((), ())), preferred_element_type=F32)
        o_ref[...] = res.astype(BF16).reshape(o_ref.shape)

    if kind == "col":
        tn = _ctile(nc)
        ct = nc // tn
        grid = (N // tn,)
        in_specs = [pl.BlockSpec((M, K), lambda j: (0, 0)), pl.BlockSpec((M, tn), lambda j: (0, j))]
        out_spec = pl.BlockSpec((2, None, nr, tn), lambda j: (0, j // ct, 0, j % ct))
    else:
        grid = (N_CHIPS,)
        in_specs = [pl.BlockSpec((M, 2 * nr), lambda i: (0, i)), pl.BlockSpec((M, N), lambda i: (0, 0))]
        out_spec = pl.BlockSpec((2, None, nr, N), lambda i: (0, i, 0, 0))
    return pl.pallas_call(
        body, grid=grid, in_specs=in_specs, out_specs=out_spec, out_shape=SDS((2, N_CHIPS, nr, nc), BF16),
        name=name, compiler_params=_params("parallel"))(x, dy)


C_EVEN = 512


def _live(rows, base, total):
    r = _row_ids((rows, 1), base)
    return jnp.logical_and(r >= PAD, r < total).astype(F32)


def _conv_taps(win, w_ref, ls, acc, flip):
    for b in range(8):
        rb = win if b == 0 else pltpu.roll(win, 96 - b, 0)
        for a in range(5):
            o = 8 * a + b
            tap = (30 - o) if flip else (o - 2)
            if 0 <= tap < CONV_WIDTH:
                acc = acc + w_ref[pl.ds(tap, 1), ls] * rb[8 * a:8 * a + CHUNK]
    return acc


def _window_sum(win, levels, forward):
    s = win
    n = win.shape[0]
    for k in range(levels):
        step = 1 << k
        s = s + pltpu.roll(s, (n - step) if forward else step, 0)
    return s


def _pool_count(base, g):
    pos = _row_ids((CHUNK, 1), base) - PAD
    return jnp.clip(pos + 1, 1, POOL_WINDOWS[g]).astype(F32)


def _even_fwd(u, cw3, cb3, lg3, lb3, pw4, pb3, ps3, j, name):
    T = u.shape[0]
    C = C_EVEN
    tm = _tile(T, 320)
    nch = tm // CHUNK
    nblk = T // CHUNK

    def body(u_ref, up_ref, cw_ref, cb_ref, lg_ref, lb_ref, pw_ref, pb_ref, ps_ref, o_ref, a_s, p_s, yc_s):
        row0 = pl.program_id(0) * tm
        up = up_ref[...]
        lp = _live(CHUNK, row0 - CHUNK, T)
        a_s[0:CHUNK, :] = up[:, 0:C] * _sigmoid(up[:, C:2 * C]) * lp
        p_s[0:CHUNK, :] = up[:, 2 * C:3 * C] * lp

        def stage(c, _):
            rs = pl.multiple_of(c * CHUNK, CHUNK)
            lv = _live(CHUNK, row0 + rs, T)
            a_s[pl.ds(rs + CHUNK, CHUNK), :] = u_ref[pl.ds(rs, CHUNK), 0:C] * _sigmoid(u_ref[pl.ds(rs, CHUNK), C:2 * C]) * lv
            p_s[pl.ds(rs + CHUNK, CHUNK), :] = u_ref[pl.ds(rs, CHUNK), 2 * C:3 * C] * lv
            return 0

        lax.fori_loop(0, nch, stage, 0)

        def chunk(c, _):
            rs = pl.multiple_of(c * CHUNK, CHUNK)
            lv = _live(CHUNK, row0 + rs, T)
            for cb in range(4):
                ls = slice(cb * 128, (cb + 1) * 128)
                win = a_s[pl.ds(pl.multiple_of(rs + 32, 32), 96), ls]
                acc = jnp.broadcast_to(cb_ref[:, ls], (CHUNK, 128))
                yc_s[:, ls] = _conv_taps(win, cw_ref, ls, acc, False)
            y = yc_s[...]
            xc = y - jnp.mean(y, axis=-1, keepdims=True)
            yn = xc * lax.rsqrt(jnp.mean(xc * xc, axis=-1, keepdims=True) + EPS) * lg_ref[...] + lb_ref[...]
            o_ref[pl.ds(rs, CHUNK), 0:C] = (yn * _sigmoid(yn) * lv).astype(BF16)
            for g in range(4):
                ls = slice(g * 128, (g + 1) * 128)
                win = p_s[pl.ds(pl.multiple_of(rs + 48, 16), 80), ls]
                s = _window_sum(win, g + 1, False)
                d = s[16:80] / _pool_count(row0 + rs, g) - win[16:80]
                yv = jnp.dot(d.astype(BF16), pw_ref[g].astype(BF16), preferred_element_type=F32) + pb_ref[:, ls]
                o_ref[pl.ds(rs, CHUNK), C + g * 128:C + (g + 1) * 128] = (yv * ps_ref[:, ls] * lv).astype(BF16)
            return 0

        lax.fori_loop(0, nch, chunk, 0)

    vec = pl.BlockSpec((None, 1, C), lambda i: (j, 0, 0))
    return pl.pallas_call(
        body, grid=(T // tm,),
        in_specs=[pl.BlockSpec((tm, 3 * C), lambda i: (i, 0)),
                  pl.BlockSpec((CHUNK, 3 * C), lambda i: (jnp.maximum(i * nch - 1, 0), 0)),
                  pl.BlockSpec((None, CONV_ROWS, C), lambda i: (j, 0, 0)), vec, vec, vec,
                  pl.BlockSpec((None, 4, 128, 128), lambda i: (j, 0, 0, 0)), vec, vec],
        out_specs=pl.BlockSpec((tm, 2 * C), lambda i: (i, 0)),
        out_shape=SDS((T, 2 * C), BF16),
        scratch_shapes=[pltpu.VMEM((tm + CHUNK, C), F32), pltpu.VMEM((tm + CHUNK, C), F32), pltpu.VMEM((CHUNK, C), F32)],
        name=name, compiler_params=_params("parallel"))(u, u, cw3, cb3, lg3, lb3, pw4, pb3, ps3)


def _even_bwd(u, dy, cw3, cb3, lg3, lb3, pw4, pb3, ps3, j, name):
    T = u.shape[0]
    C = C_EVEN
    tm = _tile(T, 320)
    nch = tm // CHUNK
    nblk = T // CHUNK
    ntile = T // tm

    def body(u_ref, up_ref, un_ref, dy_ref, dyn_ref, cw_ref, cb_ref, lg_ref, lb_ref, pw_ref, pb_ref, ps_ref,
             du_ref, dcw_ref, dcb_ref, dlg_ref, dlb_ref, dpw_ref, dpb_ref, dps_ref,
             a_s, p_s, dy_s, yc_s, dyc_s, dd_s, ddc_s, dw_s):
        i = pl.program_id(0)
        row0 = i * tm

        @pl.when(i == 0)
        def _():
            for ref in (dcb_ref, dlg_ref, dlb_ref, dpw_ref, dpb_ref, dps_ref, dw_s):
                ref[...] = jnp.zeros_like(ref)

        up = up_ref[...]
        lp = _live(CHUNK, row0 - CHUNK, T)
        a_s[0:CHUNK, :] = up[:, 0:C] * _sigmoid(up[:, C:2 * C]) * lp
        p_s[0:CHUNK, :] = up[:, 2 * C:3 * C] * lp
        un = un_ref[...]
        ln_ = _live(CHUNK, row0 + tm, T)
        a_s[tm + CHUNK:tm + 2 * CHUNK, :] = un[:, 0:C] * _sigmoid(un[:, C:2 * C]) * ln_
        p_s[tm + CHUNK:tm + 2 * CHUNK, :] = un[:, 2 * C:3 * C] * ln_
        dy_s[tm:tm + CHUNK, :] = dyn_ref[...] * ln_
        dyc_s[tm + CHUNK:tm + CHUNK + 32, :] = jnp.zeros((32, C), F32)

        def stage(c, _):
            rs = pl.multiple_of(c * CHUNK, CHUNK)
            lv = _live(CHUNK, row0 + rs, T)
            a_s[pl.ds(rs + CHUNK, CHUNK), :] = u_ref[pl.ds(rs, CHUNK), 0:C] * _sigmoid(u_ref[pl.ds(rs, CHUNK), C:2 * C]) * lv
            p_s[pl.ds(rs + CHUNK, CHUNK), :] = u_ref[pl.ds(rs, CHUNK), 2 * C:3 * C] * lv
            dy_s[pl.ds(rs, CHUNK), :] = dy_ref[pl.ds(rs, CHUNK), :] * lv
            return 0

        lax.fori_loop(0, nch, stage, 0)

        def first(c, _):
            rs = pl.multiple_of(c * CHUNK, CHUNK)
            own = jnp.where(c < nch, 1.0, 0.0).astype(F32)
            for cb in range(4):
                ls = slice(cb * 128, (cb + 1) * 128)
                win = a_s[pl.ds(pl.multiple_of(rs + 32, 32), 96), ls]
                acc = jnp.broadcast_to(cb_ref[:, ls], (CHUNK, 128))
                yc_s[:, ls] = _conv_taps(win, cw_ref, ls, acc, False)
            y = yc_s[...]
            xc = y - jnp.mean(y, axis=-1, keepdims=True)
            rstd = lax.rsqrt(jnp.mean(xc * xc, axis=-1, keepdims=True) + EPS)
            xh = xc * rstd
            yn = xh * lg_ref[...] + lb_ref[...]
            sg = _sigmoid(yn)
            dyn = dy_s[pl.ds(rs, CHUNK), 0:C] * (sg * (1.0 + yn * (1.0 - sg)))
            dlg_ref[...] += jnp.sum(dyn * xh, axis=0, keepdims=True) * own
            dlb_ref[...] += jnp.sum(dyn, axis=0, keepdims=True) * own
            dxh = dyn * lg_ref[...]
            dyc = rstd * (dxh - jnp.mean(dxh, axis=-1, keepdims=True) - xh * jnp.mean(dxh * xh, axis=-1, keepdims=True))
            dyc_s[pl.ds(rs, CHUNK), :] = dyc
            dcb_ref[...] += jnp.sum(dyc, axis=0, keepdims=True) * own
            for g in range(4):
                ls = slice(g * 128, (g + 1) * 128)
                win = p_s[pl.ds(pl.multiple_of(rs + 48, 16), 80), ls]
                s = _window_sum(win, g + 1, False)
                cnt = _pool_count(row0 + rs, g)
                d = (s[16:80] / cnt - win[16:80]).astype(BF16)
                w = pw_ref[g].astype(BF16)
                pre = jnp.dot(d, w, preferred_element_type=F32) + pb_ref[:, ls]
                dyb = dy_s[pl.ds(rs, CHUNK), C + g * 128:C + (g + 1) * 128]
                dpre = dyb * ps_ref[:, ls]
                dps_ref[:, ls] += jnp.sum(dyb * pre, axis=0, keepdims=True) * own
                dpb_ref[:, ls] += jnp.sum(dpre, axis=0, keepdims=True) * own
                dpre_b = (dpre * own).astype(BF16)
                dpw_ref[g] += lax.dot_general(d, dpre_b, (((0,), (0,)), ((), ())), preferred_element_type=F32)
                dd = lax.dot_general(dpre.astype(BF16), w, (((1,), (1,)), ((), ())), preferred_element_type=F32)
                dd_s[pl.ds(rs, CHUNK), ls] = dd
                ddc_s[pl.ds(rs, CHUNK), ls] = dd / cnt
            return 0

        lax.fori_loop(0, nch + 1, first, 0)
        ddc_s[tm + CHUNK:tm + CHUNK + 16, :] = jnp.zeros((16, C), F32)

        def second(c, _):
            rs = pl.multiple_of(c * CHUNK, CHUNK)
            lv = _live(CHUNK, row0 + rs, T)
            for cb in range(4):
                ls = slice(cb * 128, (cb + 1) * 128)
                wd = dyc_s[pl.ds(rs, 96), ls]
                da = _conv_taps(wd, cw_ref, ls, jnp.zeros((CHUNK, 128), F32), True)
                wa = a_s[pl.ds(pl.multiple_of(rs + 32, 32), 96), ls]
                dyc = dyc_s[pl.ds(rs, CHUNK), ls]
                for b in range(8):
                    rb = wa if b == 0 else pltpu.roll(wa, 96 - b, 0)
                    for a in range(5):
                        tap = 8 * a + b - 2
                        if 0 <= tap < CONV_WIDTH:
                            prod = dyc * rb[8 * a:8 * a + CHUNK]
                            part = prod[0:8]
                            for q in range(1, 8):
                                part = part + prod[8 * q:8 * q + 8]
                            dw_s[8 * tap:8 * tap + 8, ls] += part
                val = u_ref[pl.ds(rs, CHUNK), ls]
                sg = _sigmoid(u_ref[pl.ds(rs, CHUNK), C + cb * 128:C + (cb + 1) * 128])
                du_ref[pl.ds(rs, CHUNK), ls] = (da * sg * lv).astype(BF16)
                du_ref[pl.ds(rs, CHUNK), C + cb * 128:C + (cb + 1) * 128] = (da * val * sg * (1.0 - sg) * lv).astype(BF16)
            for g in range(4):
                ls = slice(g * 128, (g + 1) * 128)
                z = _window_sum(ddc_s[pl.ds(rs, 80), ls], g + 1, True)
                dpin = (z[0:CHUNK] - dd_s[pl.ds(rs, CHUNK), ls]) * lv
                du_ref[pl.ds(rs, CHUNK), 2 * C + g * 128:2 * C + (g + 1) * 128] = dpin.astype(BF16)
            return 0

        lax.fori_loop(0, nch, second, 0)

        @pl.when(i == ntile - 1)
        def _():
            for tap in range(CONV_WIDTH):
                dcw_ref[tap:tap + 1, :] = jnp.sum(dw_s[8 * tap:8 * tap + 8, :], axis=0, keepdims=True)
            dcw_ref[CONV_WIDTH:CONV_ROWS, :] = jnp.zeros((CONV_ROWS - CONV_WIDTH, C), F32)

    vec = pl.BlockSpec((None, 1, C), lambda i: (j, 0, 0))
    ovec = pl.BlockSpec((1, C), lambda i: (0, 0))
    return pl.pallas_call(
        body, grid=(ntile,),
        in_specs=[pl.BlockSpec((tm, 3 * C), lambda i: (i, 0)),
                  pl.BlockSpec((CHUNK, 3 * C), lambda i: (jnp.maximum(i * nch - 1, 0), 0)),
                  pl.BlockSpec((CHUNK, 3 * C), lambda i: (jnp.minimum((i + 1) * nch, nblk - 1), 0)),
                  pl.BlockSpec((tm, 2 * C), lambda i: (i, 0)),
                  pl.BlockSpec((CHUNK, 2 * C), lambda i: (jnp.minimum((i + 1) * nch, nblk - 1), 0)),
                  pl.BlockSpec((None, CONV_ROWS, C), lambda i: (j, 0, 0)), vec, vec, vec,
                  pl.BlockSpec((None, 4, 128, 128), lambda i: (j, 0, 0, 0)), vec, vec],
        out_specs=[pl.BlockSpec((tm, 3 * C), lambda i: (i, 0)), pl.BlockSpec((CONV_ROWS, C), lambda i: (0, 0)),
                   ovec, ovec, ovec, pl.BlockSpec((4, 128, 128), lambda i: (0, 0, 0)), ovec, ovec],
        out_shape=[SDS((T, 3 * C), BF16), SDS((CONV_ROWS, C), F32), SDS((1, C), F32), SDS((1, C), F32), SDS((1, C), F32),
                   SDS((4, 128, 128), F32), SDS((1, C), F32), SDS((1, C), F32)],
        scratch_shapes=[pltpu.VMEM((tm + 2 * CHUNK, C), F32), pltpu.VMEM((tm + 2 * CHUNK, C), F32),
                        pltpu.VMEM((tm + CHUNK, 2 * C), F32), pltpu.VMEM((CHUNK, C), F32),
                        pltpu.VMEM((tm + CHUNK + 32, C), F32), pltpu.VMEM((tm + CHUNK, C), F32),
                        pltpu.VMEM((tm + CHUNK + 16, C), F32), pltpu.VMEM((8 * CONV_ROWS, C), F32)],
        name=name, compiler_params=_params("arbitrary"))(u, u, u, dy, dy, cw3, cb3, lg3, lb3, pw4, pb3, ps3)


HI = lax.Precision.HIGHEST


def _dot_nt(a, b):
    return lax.dot_general(a, b, (((1,), (1,)), ((), ())), preferred_element_type=F32)


def _dot_tn(a, b):
    return lax.dot_general(a, b, (((0,), (0,)), ((), ())), preferred_element_type=F32)


def _tri(lower):
    r = lax.broadcasted_iota(jnp.int32, (CHUNK, CHUNK), 0)
    c = lax.broadcasted_iota(jnp.int32, (CHUNK, CHUNK), 1)
    return jnp.where((c <= r) if lower else (c >= r), 1.0, 0.0).astype(F32)


def _hgrn_gates(u_ref, lb_ref, h, D, lv):
    ls = slice(h * HEAD_DIM, (h + 1) * HEAD_DIM)
    qraw = u_ref[:, ls]
    fraw = u_ref[:, D + h * HEAD_DIM:D + (h + 1) * HEAD_DIM]
    v = u_ref[:, 2 * D + h * HEAD_DIM:2 * D + (h + 1) * HEAD_DIM] * lv
    lbv = lb_ref[:, ls]
    sig = _sigmoid(fraw)
    forget = lbv + (1.0 - lbv) * sig
    logf = jnp.log(forget) * lv
    k = (1.0 - forget) * lv
    qsig = _sigmoid(qraw)
    q = qraw * qsig * lv
    return q, k, v, logf, (qraw, qsig, sig, forget, lbv)


def _sub_parts(q, k, b, b_s, I):
    rows = slice(SUB * I, SUB * (I + 1))
    rho = jnp.zeros((1, HEAD_DIM), F32) if I == 0 else b_s[SUB * I - 1:SUB * I, :]
    eI = jnp.exp(b[rows] - rho)
    EI = jnp.exp(jnp.minimum(rho - b, EXP_CAP))
    causal = (lax.broadcasted_iota(jnp.int32, (SUB, CHUNK), 1)
              <= lax.broadcasted_iota(jnp.int32, (SUB, CHUNK), 0) + SUB * I)
    return rows, q[rows] * eI, k * EI, eI, EI, causal


def _hgrn_fwd(u, lb3, layer, gn3, j, name):
    T = u.shape[0]
    D = u.shape[1] // 4
    H = D // HEAD_DIM
    NC = T // CHUNK

    def body(u_ref, lb_ref, gn_ref, y_ref, o_ref, sall_ref, st_s, b_s, lf_s, q_s, k_s):
        n = pl.program_id(0)

        @pl.when(n == 0)
        def _():
            st_s[...] = jnp.zeros_like(st_s)

        lv = _live(CHUNK, n * CHUNK, T)
        heads = range(H)
        cols = [slice(h * HEAD_DIM, (h + 1) * HEAD_DIM) for h in heads]
        vb = []
        for h in heads:
            q, k, v, logf, _ = _hgrn_gates(u_ref, lb_ref, h, D, lv)
            q_s[:, cols[h]] = q
            k_s[:, cols[h]] = k
            lf_s[:, cols[h]] = logf
            vb.append(v.astype(BF16))
        b_s[...] = jnp.dot(_tri(True), lf_s[...], precision=HI, preferred_element_type=F32)
        ops = []
        for h in heads:
            b_h = b_s.at[:, cols[h]]
            b = b_h[...]
            q = q_s[:, cols[h]]
            k = k_s[:, cols[h]]
            blast = b_h[CHUNK - 1:CHUNK, :]
            qh = (q * jnp.exp(b)).astype(BF16)
            kt = (k * jnp.exp(blast - b)).astype(BF16)
            subs = []
            for I in range(CHUNK // SUB):
                _, qI, KI, _, _, causal = _sub_parts(q, k, b, b_h, I)
                subs.append((qI.astype(BF16), KI.astype(BF16), causal))
            ops.append((qh, kt, jnp.exp(blast), subs))
        mm = []
        for h in heads:
            qh, kt, eblast, subs = ops[h]
            st = st_s[h]
            sall_ref[h] = st
            o_inter = _dot_nt(qh, st.astype(BF16))
            st_s[h] = st * eblast + _dot_tn(vb[h], kt)
            mm.append((o_inter, [_dot_nt(qI, KI) for qI, KI, _ in subs]))
        for h in heads:
            o_inter, ps = mm[h]
            p = jnp.concatenate([jnp.where(c, x, 0.0) for x, (_, _, c) in zip(ps, ops[h][3])], axis=0).astype(BF16)
            o = o_inter + jnp.dot(p, vb[h], preferred_element_type=F32)
            o_ref[:, cols[h]] = o
            graw = u_ref[:, 3 * D + h * HEAD_DIM:3 * D + (h + 1) * HEAD_DIM]
            r = lax.rsqrt(jnp.mean(o * o, axis=-1, keepdims=True) + EPS)
            y_ref[:, cols[h]] = (((o * r) * gn_ref[...]) * (graw * _sigmoid(graw))).astype(BF16)

    return pl.pallas_call(
        body, grid=(NC,),
        in_specs=[pl.BlockSpec((CHUNK, 4 * D), lambda n: (n, 0)),
                  pl.BlockSpec((None, 1, D), lambda n: (layer, 0, 0)),
                  pl.BlockSpec((None, 1, HEAD_DIM), lambda n: (j, 0, 0))],
        out_specs=[pl.BlockSpec((CHUNK, D), lambda n: (n, 0)), pl.BlockSpec((CHUNK, D), lambda n: (n, 0)),
                   pl.BlockSpec((None, H, HEAD_DIM, HEAD_DIM), lambda n: (n, 0, 0, 0))],
        out_shape=[SDS((T, D), BF16), SDS((T, D), F32), SDS((NC, H, HEAD_DIM, HEAD_DIM), F32)],
        scratch_shapes=[pltpu.VMEM((H, HEAD_DIM, HEAD_DIM), F32)] + [pltpu.VMEM((CHUNK, D), F32)] * 4,
        name=name, compiler_params=_params("arbitrary"))(u, lb3, gn3)


def _hgrn_bwd(u, o_raw, dy, sall, lb3, layer, gn3, j, name):
    T = u.shape[0]
    D = u.shape[1] // 4
    H = D // HEAD_DIM
    NC = T // CHUNK

    def body(u_ref, o_ref, dy_ref, sall_ref, lb_ref, gn_ref, du_ref, dlb_ref, dgn_ref, dst_s, b_s, lf_s, q_s, k_s, db_s, dk_s):
        step = pl.program_id(0)
        n = NC - 1 - step

        @pl.when(step == 0)
        def _():
            dst_s[...] = jnp.zeros_like(dst_s)
            dlb_ref[...] = jnp.zeros_like(dlb_ref)
            dgn_ref[...] = jnp.zeros_like(dgn_ref)

        lv = _live(CHUNK, n * CHUNK, T)
        last_row = (_row_ids((CHUNK, 1), 0) == CHUNK - 1).astype(F32)
        gn = gn_ref[...]
        heads = range(H)
        cols = [slice(h * HEAD_DIM, (h + 1) * HEAD_DIM) for h in heads]
        vb, dob = [], []
        dgn = jnp.zeros((1, HEAD_DIM), F32)
        for h in heads:
            q, k, v, logf, _ = _hgrn_gates(u_ref, lb_ref, h, D, lv)
            q_s[:, cols[h]] = q
            k_s[:, cols[h]] = k
            lf_s[:, cols[h]] = logf
            vb.append(v.astype(BF16))
            graw = u_ref[:, 3 * D + h * HEAD_DIM:3 * D + (h + 1) * HEAD_DIM]
            gsig = _sigmoid(graw)
            o = o_ref[:, cols[h]]
            r = lax.rsqrt(jnp.mean(o * o, axis=-1, keepdims=True) + EPS)
            xh = o * r
            dyv = dy_ref[:, cols[h]]
            dsg = dyv * (graw * gsig)
            dgn = dgn + jnp.sum(dsg * xh, axis=0, keepdims=True)
            dxh = dsg * gn
            do = r * (dxh - xh * jnp.mean(dxh * xh, axis=-1, keepdims=True))
            dob.append(do.astype(BF16))
            dgraw = dyv * xh * gn * (gsig * (1.0 + graw * (1.0 - gsig)))
            du_ref[:, 3 * D + h * HEAD_DIM:3 * D + (h + 1) * HEAD_DIM] = (dgraw * lv).astype(BF16)
        dgn_ref[...] += dgn
        b_s[...] = jnp.dot(_tri(True), lf_s[...], precision=HI, preferred_element_type=F32)
        ops = []
        for h in heads:
            b_h = b_s.at[:, cols[h]]
            b = b_h[...]
            q = q_s[:, cols[h]]
            k = k_s[:, cols[h]]
            blast = b_h[CHUNK - 1:CHUNK, :]
            eb = jnp.exp(b)
            ekb = jnp.exp(blast - b)
            subs = []
            for I in range(CHUNK // SUB):
                rows, qI, KI, eI, EI, causal = _sub_parts(q, k, b, b_h, I)
                subs.append((rows, qI.astype(BF16), KI.astype(BF16), eI, EI, causal))
            ops.append((eb, ekb, jnp.exp(blast), (q * eb).astype(BF16), (k * ekb).astype(BF16), subs))
        mm = []
        for h in heads:
            eb, ekb, eblast, qhb, ktb, subs = ops[h]
            st = sall_ref[h]
            dst = dst_s[h]
            dstb = dst.astype(BF16)
            dv = _dot_nt(ktb, dstb)
            dqh = jnp.dot(dob[h], st.astype(BF16), preferred_element_type=F32)
            dkt = jnp.dot(vb[h], dstb, preferred_element_type=F32)
            dblast = jnp.sum(dst * st, axis=0, keepdims=True) * eblast
            dst_s[h] = dst * eblast + _dot_tn(dob[h], qhb)
            dp_full = _dot_nt(dob[h], vb[h])
            ps = [_dot_nt(qIb, KIb) for _, qIb, KIb, _, _, _ in subs]
            mm.append((dv, dqh, dkt, dblast, dp_full, ps))
        for h in heads:
            eb, ekb, eblast, qhb, ktb, subs = ops[h]
            dv, dqh, dkt, dblast, dp_full, ps = mm[h]
            p = jnp.concatenate([jnp.where(sub[5], x, 0.0) for x, sub in zip(ps, subs)], axis=0).astype(BF16)
            dv = dv + _dot_tn(p, dob[h])
            du_ref[:, 2 * D + h * HEAD_DIM:2 * D + (h + 1) * HEAD_DIM] = (dv * lv).astype(BF16)
            dq = dqh * eb
            db = dqh * qhb.astype(F32)
            tmp = dkt * ktb.astype(F32)
            dk = dkt * ekb
            db = db - tmp
            dblast = dblast + jnp.sum(tmp, axis=0, keepdims=True)
            dq_parts, db_parts = [], []
            for rows, qIb, KIb, eI, EI, causal in subs:
                dp = jnp.where(causal, dp_full[rows], 0.0).astype(BF16)
                dqI = jnp.dot(dp, KIb, preferred_element_type=F32)
                dKI = _dot_tn(dp, qIb)
                dq_parts.append(dqI * eI)
                db_parts.append(dqI * qIb.astype(F32))
                dk = dk + dKI * EI
                db = db - dKI * KIb.astype(F32)
            dq = dq + jnp.concatenate(dq_parts, axis=0)
            db_s[:, cols[h]] = db + jnp.concatenate(db_parts, axis=0) + last_row * dblast
            dk_s[:, cols[h]] = dk
            qraw = u_ref[:, cols[h]]
            qsig = _sigmoid(qraw)
            du_ref[:, cols[h]] = (dq * (qsig * (1.0 + qraw * (1.0 - qsig))) * lv).astype(BF16)
        lf_s[...] = jnp.dot(_tri(False), db_s[...], precision=HI, preferred_element_type=F32)
        for h in heads:
            fraw = u_ref[:, D + h * HEAD_DIM:D + (h + 1) * HEAD_DIM]
            lbv = lb_ref[:, cols[h]]
            sig = _sigmoid(fraw)
            forget = lbv + (1.0 - lbv) * sig
            dforget = (lf_s[:, cols[h]] / forget - dk_s[:, cols[h]]) * lv
            dlb_ref[:, cols[h]] += jnp.sum(dforget * (1.0 - sig), axis=0, keepdims=True)
            du_ref[:, D + h * HEAD_DIM:D + (h + 1) * HEAD_DIM] = (dforget * (1.0 - lbv) * sig * (1.0 - sig)).astype(BF16)

    rev = lambda s: (NC - 1 - s, 0)
    return pl.pallas_call(
        body, grid=(NC,),
        in_specs=[pl.BlockSpec((CHUNK, 4 * D), rev), pl.BlockSpec((CHUNK, D), rev), pl.BlockSpec((CHUNK, D), rev),
                  pl.BlockSpec((None, H, HEAD_DIM, HEAD_DIM), lambda s: (NC - 1 - s, 0, 0, 0)),
                  pl.BlockSpec((None, 1, D), lambda s: (layer, 0, 0)),
                  pl.BlockSpec((None, 1, HEAD_DIM), lambda s: (j, 0, 0))],
        out_specs=[pl.BlockSpec((CHUNK, 4 * D), rev), pl.BlockSpec((1, D), lambda s: (0, 0)),
                   pl.BlockSpec((1, HEAD_DIM), lambda s: (0, 0))],
        out_shape=[SDS((T, 4 * D), BF16), SDS((1, D), F32), SDS((1, HEAD_DIM), F32)],
        scratch_shapes=[pltpu.VMEM((H, HEAD_DIM, HEAD_DIM), F32)] + [pltpu.VMEM((CHUNK, D), F32)] * 6,
        name=name, compiler_params=_params("arbitrary"))(u, o_raw, dy, sall, lb3, gn3)


def _softmax_layers(p_ref, n_layers):
    rows = [p_ref[l:l + 1, :] for l in range(n_layers)]
    m = functools.reduce(jnp.maximum, rows)
    e = [jnp.exp(x - m) for x in rows]
    tot = functools.reduce(lambda a, b: a + b, e)
    return [x / tot for x in e]


def _lb_fwd(p):
    n_layers, D = p.shape

    def body(p_ref, o_ref):
        s = _softmax_layers(p_ref, n_layers)
        acc = jnp.zeros((1, D), F32)
        o_ref[0:1, :] = acc
        for l in range(1, n_layers):
            acc = acc + s[l]
            o_ref[l:l + 1, :] = acc

    return pl.pallas_call(body, out_shape=SDS(p.shape, F32), name="lb_fwd")(p)


def _lb_bwd(p, dlb):
    n_layers, D = p.shape

    def body(p_ref, d_ref, o_ref):
        s = _softmax_layers(p_ref, n_layers)
        ds = [jnp.zeros((1, D), F32)] * n_layers
        acc = jnp.zeros((1, D), F32)
        for l in range(n_layers - 1, 0, -1):
            acc = acc + d_ref[l:l + 1, :]
            ds[l] = acc
        dot = functools.reduce(lambda a, b: a + b, [s[l] * ds[l] for l in range(n_layers)])
        for l in range(n_layers):
            o_ref[l:l + 1, :] = s[l] * (ds[l] - dot)

    return pl.pallas_call(body, out_shape=SDS(p.shape, F32), name="lb_bwd")(p, dlb)


def _adamw(w, g, m, v, name):
    R, C = w.shape
    tr = _tile(R, 256, 8) if R % 8 == 0 else R

    def body(w_ref, g_ref, m_ref, v_ref, d_ref, mo_ref, vo_ref):
        g_ = g_ref[...]
        m_ = ADAM_B1 * m_ref[...] + (1.0 - ADAM_B1) * g_
        v_ = ADAM_B2 * v_ref[...] + (1.0 - ADAM_B2) * (g_ * g_)
        mh = m_ / (1.0 - ADAM_B1 ** ADAM_STEP)
        vh = v_ / (1.0 - ADAM_B2 ** ADAM_STEP)
        d_ref[...] = -ADAM_LR * (mh / (jnp.sqrt(vh) + ADAM_EPS) + ADAM_WD * w_ref[...])
        mo_ref[...] = m_
        vo_ref[...] = v_

    blk = pl.BlockSpec((tr, C), lambda i: (i, 0))
    return pl.pallas_call(
        body, grid=(R // tr,), in_specs=[blk] * 4, out_specs=[blk] * 3, out_shape=[SDS((R, C), F32)] * 3,
        name=name, compiler_params=_params("parallel"))(w, g, m, v)


def _adamw_layer(w3, m3, v3, g2, layer, outs, name):
    L, R, C = w3.shape
    tr = _tile(R, 256, 8)
    if outs is None:
        outs = tuple(lax.empty(w3.shape, F32) for _ in range(4))

    def body(w_ref, m_ref, v_ref, g_ref, a0, a1, a2, a3, go_ref, d_ref, mo_ref, vo_ref):
        del a0, a1, a2, a3
        g_ = g_ref[...]
        m_ = ADAM_B1 * m_ref[...] + (1.0 - ADAM_B1) * g_
        v_ = ADAM_B2 * v_ref[...] + (1.0 - ADAM_B2) * (g_ * g_)
        mh = m_ / (1.0 - ADAM_B1 ** ADAM_STEP)
        vh = v_ / (1.0 - ADAM_B2 ** ADAM_STEP)
        go_ref[...] = g_
        d_ref[...] = -ADAM_LR * (mh / (jnp.sqrt(vh) + ADAM_EPS) + ADAM_WD * w_ref[...])
        mo_ref[...] = m_
        vo_ref[...] = v_

    lay = pl.BlockSpec((None, tr, C), lambda i: (layer, i, 0))
    return pl.pallas_call(
        body, grid=(R // tr,), in_specs=[lay] * 3 + [pl.BlockSpec((tr, C), lambda i: (i, 0))] + [ANY_SPEC] * 4,
        out_specs=[lay] * 4, out_shape=[SDS(w3.shape, F32)] * 4, input_output_aliases={4: 0, 5: 1, 6: 2, 7: 3},
        name=name, compiler_params=_params("parallel"))(w3, m3, v3, g2, *outs)


SEM_SPEC = pl.BlockSpec(memory_space=pltpu.SEMAPHORE)
HBM_SPEC = pl.BlockSpec(memory_space=pltpu.HBM)
EFFECT = pltpu.SideEffectType.DATAFLOW_SIDE_EFFECTING
N_DEV = 2 * N_CHIPS


def _position():
    x, y, c = lax.axis_index("x"), lax.axis_index("y"), lax.axis_index("c")
    chips = [(1 - x, y), (x, 1 - y), (1 - x, 1 - y)]
    return x, y, c, chips


def _split_start(name, plan, bufs, n_sems, deps=(), earlier=None):
    n = len(bufs)
    held = () if earlier is None else tuple(earlier[1:])

    def body(*refs):
        first_out = n + len(held) + len(deps)
        if earlier is not None:
            sends, recvs = earlier[0](refs[:n], refs[n], refs[n + 1])
            for kw in sends:
                pltpu.make_async_remote_copy(**kw).wait_send()
            for kw in recvs:
                pltpu.make_async_remote_copy(**kw).wait_recv()
        sends, _ = plan(refs[:n], refs[first_out], refs[first_out + 1])
        for kw in sends:
            pltpu.make_async_remote_copy(**kw).start()
        refs[-1][...] = jnp.zeros_like(refs[-1])

    out = pl.pallas_call(
        body, name=name,
        out_shape=(pltpu.SemaphoreType.DMA((n_sems,)), pltpu.SemaphoreType.DMA((n_sems,)),
                   *[pltpu.HBM(b.shape, b.dtype) for b in bufs], SDS((8, 128), F32)),
        in_specs=[HBM_SPEC] * n + [SEM_SPEC] * len(held) + [ANY_SPEC] * len(deps),
        out_specs=(SEM_SPEC, SEM_SPEC, *[HBM_SPEC] * n, pl.BlockSpec(memory_space=pltpu.VMEM)),
        input_output_aliases={i: 2 + i for i in range(n)},
        compiler_params=pltpu.CompilerParams(has_side_effects=EFFECT),
    )(*[pltpu.with_memory_space_constraint(b, pltpu.HBM) for b in bufs], *held, *deps)
    return out[0], out[1], list(out[2:2 + n]), out[-1]


def _split_wait(name, plan, send_sems, recv_sems, bufs, after=()):
    n = len(bufs)

    def body(*refs):
        sends, recvs = plan(refs[:n], refs[n], refs[n + 1])
        for kw in sends:
            pltpu.make_async_remote_copy(**kw).wait_send()
        for kw in recvs:
            pltpu.make_async_remote_copy(**kw).wait_recv()

    out = pl.pallas_call(
        body, name=name, out_shape=tuple(pltpu.HBM(b.shape, b.dtype) for b in bufs),
        in_specs=[HBM_SPEC] * n + [SEM_SPEC, SEM_SPEC] + [ANY_SPEC] * len(after),
        out_specs=tuple([HBM_SPEC] * n), input_output_aliases={i: i for i in range(n)},
        compiler_params=pltpu.CompilerParams(has_side_effects=EFFECT),
    )(*bufs, send_sems, recv_sems, *after)
    return list(out)


def _region(kind, ref, chip, half):
    K, N = ref.shape
    if kind == "col":
        return ref.at[pl.ds(half * (K // 2), K // 2), pl.ds(chip * (N // N_CHIPS), N // N_CHIPS)]
    rows = K // (2 * N_CHIPS)
    return ref.at[pl.ds((2 * chip + half) * rows, rows), :]


def _gather_plan(kinds, over_chips):
    def plan(refs, send_sems, recv_sems):
        x, y, c, chips = _position()
        sends, recvs = [], []
        for f, (ref, kind) in enumerate(zip(refs, kinds)):
            for k, chip in enumerate(chips):
                theirs = 2 * chip[0] + chip[1]
                sem = dict(send_sem=send_sems.at[3 * f + k], recv_sem=recv_sems.at[3 * f + k], device_id_type=MESH)
                if over_chips:
                    out, back, to = _region(kind, ref, 2 * x + y, c), _region(kind, ref, theirs, c), (*chip, c)
                else:
                    out, back, to = _region(kind, ref, theirs, c), _region(kind, ref, theirs, 1 - c), (x, y, 1 - c)
                sends.append(dict(src_ref=out, dst_ref=out, device_id=to, **sem))
                recvs.append(dict(src_ref=back, dst_ref=back, device_id=to, **sem))
        return sends, recvs
    return plan


def _reduce_plan(refs, send_sems, recv_sems):
    x, y, c, _ = _position()
    me = 4 * x + 2 * y + c
    sends, recvs = [], []
    for f in range(len(refs) // 2):
        acc, land = refs[2 * f], refs[2 * f + 1]
        for d in range(1, N_DEV):
            t = (me + d) % N_DEV
            to = dict(device_id=(t // 4, (t // 2) % 2, t % 2), device_id_type=MESH)
            slot = N_DEV - 1 - d
            sends.append(dict(src_ref=acc.at[t % 2, t // 2], dst_ref=land.at[slot], send_sem=send_sems.at[7 * f + d - 1],
                              recv_sem=recv_sems.at[7 * f + slot], **to))
            recvs.append(dict(src_ref=land.at[d - 1], dst_ref=land.at[d - 1], send_sem=send_sems.at[7 * f + d - 1],
                              recv_sem=recv_sems.at[7 * f + d - 1], **to))
    return sends, recvs


def _swap_plan(refs, send_sems, recv_sems):
    x, y, c, _ = _position()
    sends, recvs = [], []
    for f, g in enumerate(refs):
        sem = dict(send_sem=send_sems.at[f], recv_sem=recv_sems.at[f], device_id=(x, y, 1 - c), device_id_type=MESH)
        sends.append(dict(src_ref=g.at[c], dst_ref=g.at[c], **sem))
        recvs.append(dict(src_ref=g.at[1 - c], dst_ref=g.at[1 - c], **sem))
    return sends, recvs


def _sum_pieces(ids2, acc, land, name):
    _, _, nr, nc = acc.shape
    tr = _tile(nr, 256, 16)

    def body(ids_ref, own_ref, land_ref, o_ref):
        del ids_ref
        s = own_ref[...].astype(F32)
        for k in range(N_DEV - 1):
            s = s + land_ref[k].astype(F32)
        o_ref[...] = s

    return pl.pallas_call(
        body,
        grid_spec=pltpu.PrefetchScalarGridSpec(
            num_scalar_prefetch=1, grid=(nr // tr,),
            in_specs=[pl.BlockSpec((None, None, tr, nc), lambda i, ids: (ids[0], ids[1], i, 0)),
                      pl.BlockSpec((N_DEV - 1, tr, nc), lambda i, ids: (0, i, 0))],
            out_specs=pl.BlockSpec((None, tr, nc), lambda i, ids: (ids[0], i, 0))),
        out_shape=SDS((2, nr, nc), F32), name=name, compiler_params=_params("parallel"))(ids2, acc, land)


def _small_plan(refs, send_sems, recv_sems):
    x, y, c, _ = _position()
    me = 4 * x + 2 * y + c
    own, land = refs
    sends, recvs = [], []
    for d in range(1, N_DEV):
        t = (me + d) % N_DEV
        to = dict(device_id=(t // 4, (t // 2) % 2, t % 2), device_id_type=MESH)
        sends.append(dict(src_ref=own, dst_ref=land.at[me], send_sem=send_sems.at[d - 1],
                          recv_sem=recv_sems.at[N_DEV - 1 - d], **to))
        recvs.append(dict(src_ref=land.at[t], dst_ref=land.at[t], send_sem=send_sems.at[d - 1],
                          recv_sem=recv_sems.at[d - 1], **to))
    return sends, recvs


def _sum_blocks(me1, own, land):
    def body(me_ref, own_ref, land_ref, o_ref):
        acc = None
        for d in range(N_DEV):
            term = jnp.where(me_ref[0] == d, own_ref[...], land_ref[d])
            acc = term if acc is None else acc + term
        o_ref[...] = acc

    return pl.pallas_call(
        body,
        grid_spec=pltpu.PrefetchScalarGridSpec(
            num_scalar_prefetch=1, grid=(1,),
            in_specs=[pl.BlockSpec(own.shape, lambda i, me: (0, 0)), pl.BlockSpec(land.shape, lambda i, me: (0, 0, 0))],
            out_specs=pl.BlockSpec(own.shape, lambda i, me: (0, 0))),
        out_shape=SDS(own.shape, F32), name="sum_small", compiler_params=_params("arbitrary"))(me1, own, land)


BIG = {"ev_w_in": "col", "ev_w_out": "row", "od_w_in": "col", "od_w_out": "row", "mlp_w1": "col", "mlp_w2": "row"}
WEIGHTS = ("meta_tokens", "mix_norm_g", "mlp_norm_g", "final_norm_g", "ev_w_in", "ev_conv_w", "ev_conv_b", "ev_ln_g",
           "ev_ln_b", "ev_pool_w", "ev_pool_b", "ev_pool_scale", "ev_w_out", "od_w_in", "od_gnorm_g", "od_w_out",
           "lb_param", "mlp_w1", "mlp_w2")
PACK_UNIT = 1024


def _mixer_names(layer):
    return ("ev_w_in", "ev_w_out") if layer % 2 == 0 else ("od_w_in", "od_w_out")


def _pack(arrays):
    flat = []
    for a in arrays:
        a = a.reshape(-1)
        flat.append(jnp.pad(a, (0, (-a.shape[0]) % PACK_UNIT)))
    return jnp.concatenate(flat).reshape(-1, 128)


def _unpack(packed, shapes):
    flat = packed.reshape(-1)
    out, off = [], 0
    for s in shapes:
        size = 1
        for d in s:
            size *= d
        out.append(flat[off:off + size].reshape(s))
        off += size + (-size) % PACK_UNIT
    return out


def _local_step(x2, target, P, weights, boundary, first_deps=()):
    D = x2.shape[1]
    n_layers = P["mix_norm_g"].shape[0]
    h = jnp.concatenate([jnp.zeros((PAD, D), F32), P["meta_full"], x2], axis=0)
    mix_g = P["mix_norm_g"].reshape(n_layers, 1, D)
    mlp_g = P["mlp_norm_g"].reshape(n_layers, 1, D)
    vec = lambda a: a.reshape(a.shape[0], 1, -1)
    cb3, lg3, lnb3, ps3 = vec(P["ev_conv_b"]), vec(P["ev_ln_g"]), vec(P["ev_ln_b"]), vec(P["ev_pool_scale"])
    pb3 = vec(P["ev_pool_b"])
    gn3 = vec(P["od_gnorm_g"])
    lb_all = _lb_fwd(P["lb_param"])
    lb3 = lb_all.reshape(n_layers, 1, D)
    even = (cb3, lg3, lnb3, P["ev_pool_w"], pb3, ps3)

    saved = []
    deps = tuple(first_deps)
    for layer in range(n_layers):
        j = layer // 2
        w_in, w_out = _mixer_names(layer)
        W = {}
        s = {"h": h, "W": W}
        s["n"] = _rms_fwd(h, mix_g, layer, "mix_norm_0", deps=deps) if layer == 0 else n_next
        deps = ()
        W[w_in] = weights(layer, w_in, (s["n"],))
        s["u"] = _mm_nn(s["n"], W[w_in], 0, f"mix_in_{layer}")
        if layer % 2 == 0:
            s["y"] = _even_fwd(s["u"], P["conv_w_full"], *even, j, f"even_fwd_{layer}")
        else:
            s["y"], s["o"], s["sall"] = _hgrn_fwd(s["u"], lb3, layer, gn3, j, f"hgrn_fwd_{layer}")
        W[w_out] = weights(layer, w_out, (s["y"],))
        h, s["n2"] = _mm_nn_norm(s["y"], W[w_out], 0, h, mlp_g, layer, f"mix_out_{layer}")
        s["h1"] = h
        W["mlp_w1"] = weights(layer, "mlp_w1", (s["n2"],))
        s["act"], s["relu"] = _mm_nn(s["n2"], W["mlp_w1"], 0, f"mlp_up_{layer}", relu2=True)
        W["mlp_w2"] = weights(layer, "mlp_w2", (s["act"],))
        if layer + 1 < n_layers:
            h, n_next = _mm_nn_norm(s["act"], W["mlp_w2"], 0, h, mix_g, layer + 1, f"mlp_down_{layer}")
        else:
            h = _mm_nn(s["act"], W["mlp_w2"], 0, f"mlp_down_{layer}", res=h)
        saved.append(s)

    dh, dhb, dg_final, loss = _final(h, P["final_norm_g"].reshape(1, D), target)

    small = {"final_norm_g": dg_final}
    per_layer = {k: [None] * n_layers for k in ("mix_norm_g", "mlp_norm_g", "lb")}
    per_pair = {k: [None] * (n_layers // 2) for k in
                ("ev_conv_w", "ev_conv_b", "ev_ln_g", "ev_ln_b", "ev_pool_w", "ev_pool_b", "ev_pool_scale", "od_gnorm_g")}
    for layer in reversed(range(n_layers)):
        j = layer // 2
        s = saved[layer]
        W = s["W"]
        w_in, w_out = _mixer_names(layer)
        dz = _mm_nt(dhb, W["mlp_w2"], 0, f"d_act_{layer}", relu=s["relu"], deps=deps)
        dw2 = _mm_tn(s["act"], dhb, "row", f"dw2_{layer}")
        dw1 = _mm_tn(s["n2"], dz, "col", f"dw1_{layer}")
        dh, dhb, per_layer["mlp_norm_g"][layer] = _mm_nt_norm(dz, W["mlp_w1"], 0, s["h1"], mlp_g, layer, dh, f"d_n2_{layer}")
        deps = boundary(f"mlp{layer}", {("mlp_w1", layer): dw1, ("mlp_w2", layer): dw2}, (dhb,))
        dy = _mm_nt(dhb, W[w_out], 0, f"d_y_{layer}", deps=deps)
        dwout = _mm_tn(s["y"], dhb, "row", f"dwout_{layer}")
        if layer % 2 == 0:
            du, dcw, dcb, dlg, dlnb, dpw, dpb, dps = _even_bwd(s["u"], dy, P["conv_w_full"], *even, j, f"even_bwd_{layer}")
            for k, val in (("ev_conv_w", dcw), ("ev_conv_b", dcb), ("ev_ln_g", dlg), ("ev_ln_b", dlnb),
                           ("ev_pool_w", dpw), ("ev_pool_b", dpb), ("ev_pool_scale", dps)):
                per_pair[k][j] = val
        else:
            du, per_layer["lb"][layer], per_pair["od_gnorm_g"][j] = _hgrn_bwd(
                s["u"], s["o"], dy, s["sall"], lb3, layer, gn3, j, f"hgrn_bwd_{layer}")
        dwin = _mm_tn(s["n"], du, "col", f"dwin_{layer}")
        deps = boundary(f"mix{layer}", {(w_in, j): dwin, (w_out, j): dwout}, (du,))
        dh, dhb, per_layer["mix_norm_g"][layer] = _mm_nt_norm(du, W[w_in], 0, s["h"], mix_g, layer, dh, f"d_n_{layer}", deps=deps)
        deps = ()

    small["mix_norm_g"] = jnp.concatenate(per_layer["mix_norm_g"], axis=0)
    small["mlp_norm_g"] = jnp.concatenate(per_layer["mlp_norm_g"], axis=0)
    dlb_all = jnp.concatenate([jnp.zeros((1, D), F32) if g is None else g for g in per_layer["lb"]], axis=0)
    small["lb_param"] = _lb_bwd(P["lb_param"], dlb_all)
    for k, vals in per_pair.items():
        small[k] = jnp.stack(vals, axis=0)
    small["meta_tokens"] = dh[PAD:LEAD]
    return loss, dh, small


def kernel(x, meta_tokens, mix_norm_g, mlp_norm_g, final_norm_g, ev_w_in, ev_conv_w, ev_conv_b, ev_ln_g, ev_ln_b, ev_pool_w, ev_pool_b, ev_pool_scale, ev_w_out, od_w_in, od_gnorm_g, od_w_out, lb_param, mlp_w1, mlp_w2, loss_target, m_meta_tokens, m_mix_norm_g, m_mlp_norm_g, m_final_norm_g, m_ev_w_in, m_ev_conv_w, m_ev_conv_b, m_ev_ln_g, m_ev_ln_b, m_ev_pool_w, m_ev_pool_b, m_ev_pool_scale, m_ev_w_out, m_od_w_in, m_od_gnorm_g, m_od_w_out, m_lb_param, m_mlp_w1, m_mlp_w2, v_meta_tokens, v_mix_norm_g, v_mlp_norm_g, v_final_norm_g, v_ev_w_in, v_ev_conv_w, v_ev_conv_b, v_ev_ln_g, v_ev_ln_b, v_ev_pool_w, v_ev_pool_b, v_ev_pool_scale, v_ev_w_out, v_od_w_in, v_od_gnorm_g, v_od_w_out, v_lb_param, v_mlp_w1, v_mlp_w2):
    given = dict(locals())
    w = {n: given[n] for n in WEIGHTS}
    m = {n: given["m_" + n] for n in WEIGHTS}
    v = {n: given["v_" + n] for n in WEIGHTS}
    n_layers = mix_norm_g.shape[0]
    core = lax.axis_index("c").astype(jnp.int32)
    chip = (2 * lax.axis_index("x") + lax.axis_index("y")).astype(jnp.int32)
    chip1 = chip.reshape(1)
    ids2 = jnp.stack([core, chip])

    conv_pad = jnp.pad(ev_conv_w, ((0, 0), (0, CONV_ROWS - CONV_WIDTH), (0, 0)))
    stages = [[(0, n)] for n in (*_mixer_names(0), "mlp_w1", "mlp_w2")]
    stages += [[(layer, n) for n in (*_mixer_names(layer), "mlp_w1", "mlp_w2")] for layer in range(1, n_layers)]
    gathers, where, token = [], {}, ()
    for k, stage in enumerate(stages):
        index = [layer if n.startswith("mlp") else layer // 2 for layer, n in stage]
        kinds = [BIG[n] for _, n in stage]
        bufs = [_cast_place(w[n], i, BIG[n], chip1, BF16, f"place_{n}_{i}") for (_, n), i in zip(stage, index)]
        if k == 0:
            bufs.append(_cast_place(meta_tokens[None], 0, "col", chip1, F32, "place_meta"))
            bufs.append(_cast_place(conv_pad.reshape(1, -1, conv_pad.shape[2]), 0, "col", chip1, F32, "place_conv_w"))
            kinds += ["col", "col"]
        plan = _gather_plan(kinds, True)
        ss, rs, bufs, tok = _split_start(f"gather_start_{k}", plan, bufs, 3 * len(bufs), deps=token)
        token = (tok,)
        gathers.append((kinds, plan, ss, rs, bufs))
        where.update({key: (k, f) for f, key in enumerate(stage)})

    landed, passed = {}, {}

    def hand_on(k, deps):
        if k not in passed:
            kinds, plan, ss, rs, bufs = gathers[k]
            to_sibling = _gather_plan(kinds, False)
            ss, rs, bufs, _ = _split_start(f"gather_pass_{k}", to_sibling, bufs, 3 * len(bufs), deps=deps, earlier=(plan, ss, rs))
            passed[k] = (to_sibling, ss, rs, bufs)

    def arrived(k, after):
        if k not in landed:
            hand_on(k, after)
            landed[k] = _split_wait(f"gather_wait_{k}", *passed[k], after)
            if k + 1 < len(stages) and len(stages[k + 1]) > 1:
                hand_on(k + 1, tuple(landed[k][:1]))
        return landed[k]

    def weights(layer, name, after):
        k, f = where[(layer, name)]
        return arrived(k, after)[f][None]

    first = arrived(0, token)
    P = {n: w[n] for n in ("mix_norm_g", "mlp_norm_g", "final_norm_g", "ev_conv_b", "ev_ln_g", "ev_ln_b", "ev_pool_w",
                           "ev_pool_b", "ev_pool_scale", "od_gnorm_g", "lb_param")}
    P["meta_full"] = first[1]
    P["conv_w_full"] = first[2].reshape(ev_conv_w.shape[0], CONV_ROWS, -1)

    pending, outs = [], {n: None for n in BIG}

    def advance(after):
        tokens, still = [], []
        for st in pending:
            if st["phase"] == 1:
                bufs = _split_wait(f"reduce_wait_{st['tag']}", _reduce_plan, st["ss"], st["rs"], st["bufs"], after)
                halves = [_sum_pieces(ids2, bufs[2 * f], bufs[2 * f + 1], f"sum_{st['tag']}_{f}") for f in range(len(bufs) // 2)]
                ss, rs, halves, tok = _split_start(f"swap_start_{st['tag']}", _swap_plan, halves, len(halves))
                tokens.append(tok)
                still.append(dict(st, phase=2, ss=ss, rs=rs, bufs=halves))
            else:
                grads = _split_wait(f"swap_wait_{st['tag']}", _swap_plan, st["ss"], st["rs"], st["bufs"], after)
                for (n, i), g in zip(st["keys"], grads):
                    outs[n] = _adamw_layer(w[n], m[n], v[n], g.reshape(w[n].shape[1:]), i, outs[n], f"adamw_{n}_{i}")
        pending[:] = still
        return tokens

    def boundary(tag, grads, after):
        tokens = advance(after)
        bufs = []
        for acc in grads.values():
            bufs += [acc, lax.empty((N_DEV - 1,) + acc.shape[2:], BF16)]
        ss, rs, bufs, tok = _split_start(f"reduce_start_{tag}", _reduce_plan, bufs, 7 * len(grads))
        pending.append(dict(phase=1, tag=tag, keys=list(grads), ss=ss, rs=rs, bufs=bufs))
        return tuple(tokens + [tok])

    loss, dh, small = _local_step(x[0], loss_target[0], P, weights, boundary, first_deps=token)

    order = [n for n in WEIGHTS if n not in BIG]
    block = _pack([small[n] for n in order] + [loss])
    ss, rs, bufs, tok = _split_start("small_start", _small_plan, [block, lax.empty((N_DEV,) + block.shape, F32)], N_DEV - 1)
    advance((tok,))
    advance((tok,))
    block, land = _split_wait("small_wait", _small_plan, ss, rs, bufs, tuple(outs[n][0] for n in BIG))
    packed = _sum_blocks((4 * lax.axis_index("x") + 2 * lax.axis_index("y") + lax.axis_index("c")).astype(jnp.int32).reshape(1), block, land)
    total = _unpack(packed, [small[n].shape for n in order] + [loss.shape])
    loss_sum = total[-1][0, 0]
    gsmall = dict(zip(order, total[:-1]))
    gsmall["meta_tokens"] = lax.dynamic_slice_in_dim(gsmall["meta_tokens"], chip * meta_tokens.shape[1], meta_tokens.shape[1], 1)
    gsmall["ev_conv_w"] = lax.dynamic_slice_in_dim(gsmall["ev_conv_w"][:, :CONV_WIDTH], chip * ev_conv_w.shape[2], ev_conv_w.shape[2], 2)

    g_out, d_out, m_out, v_out = {}, {}, {}, {}
    for n in WEIGHTS:
        if n in BIG:
            g_out[n], d_out[n], m_out[n], v_out[n] = outs[n]
            continue
        shape = w[n].shape
        g = gsmall[n].reshape(shape)
        cols = shape[-1] if len(shape) > 1 else 128
        two = lambda a: a.reshape(-1, cols)
        d_, m_, v_ = _adamw(two(w[n]), two(g), two(m[n]), two(v[n]), f"adamw_{n}")
        g_out[n], d_out[n], m_out[n], v_out[n] = g, d_.reshape(shape), m_.reshape(shape), v_.reshape(shape)

    grad_x = dh[LEAD:][None]
    return (loss_sum, grad_x, *[g_out[n] for n in WEIGHTS], *[d_out[n] for n in WEIGHTS],
            *[m_out[n] for n in WEIGHTS], *[v_out[n] for n in WEIGHTS])
```

```python
import functools

import jax
import jax.numpy as jnp
from jax import lax
from jax.experimental import pallas as pl
from jax.experimental.pallas import tpu as pltpu

F32 = jnp.float32
BF16 = jnp.bfloat16
SDS = jax.ShapeDtypeStruct
MESH = pl.DeviceIdType.MESH
ANY_SPEC = pl.BlockSpec(memory_space=pl.ANY)

N_META = 16
CHUNK = 64
LEAD = CHUNK
PAD = LEAD - N_META
CONV_WIDTH = 31
CONV_ROWS = 32
POOL_WINDOWS = (2, 4, 8, 16)
HEAD_DIM = 128
SUB = 16
EXP_CAP = 80.0
EPS = 1e-6
ADAM_LR = 0.001
ADAM_B1 = 0.9
ADAM_B2 = 0.999
ADAM_EPS = 1e-08
ADAM_WD = 0.01
ADAM_STEP = 10
N_CHIPS = 4
VMEM_LIMIT = 52 << 20
MM_VMEM_BUDGET = 44 << 20


def _params(*sem):
    return pltpu.CompilerParams(dimension_semantics=sem if sem else None, vmem_limit_bytes=VMEM_LIMIT)


def _tile(n, target, unit=CHUNK):
    best = None
    for t in range(unit, min(n, target) + 1, unit):
        if n % t == 0:
            best = t
    assert best is not None, (n, target, unit)
    return best


def _ctile(n, target=512):
    for t in (512, 384, 256, 128):
        if t <= target and n % t == 0:
            return t
    raise ValueError(n)


def _mm_tiles(M, N, per_row, per_col, per_elem):
    best = None
    for tn in (512, 384, 256, 128):
        if N % tn:
            continue
        for tm in sorted((d for d in range(16, M + 1, 16) if M % d == 0), reverse=True):
            if 2 * (tm * per_row + tn * per_col + tm * tn * per_elem) <= MM_VMEM_BUDGET:
                if best is None or tm * tn > best[0] * best[1]:
                    best = (tm, tn)
                break
    assert best is not None, (M, N)
    return best


def _sigmoid(x):
    return 1.0 / (1.0 + jnp.exp(-x))


def _row_ids(shape, base):
    return lax.broadcasted_iota(jnp.int32, shape, 0) + base


def _cast_place(w3, layer, kind, chip1, dtype, name):
    _, ks, ns = w3.shape
    tr = _tile(ks, 512, 16)
    full = (ks, ns * N_CHIPS) if kind == "col" else (ks * N_CHIPS, ns)

    def body(chip_ref, w_ref, o_ref):
        del chip_ref
        o_ref[...] = w_ref[...].astype(dtype)

    omap = (lambda i, chip: (i, chip[0])) if kind == "col" else (lambda i, chip: (chip[0] * (ks // tr) + i, 0))
    return pl.pallas_call(
        body,
        grid_spec=pltpu.PrefetchScalarGridSpec(
            num_scalar_prefetch=1, grid=(ks // tr,),
            in_specs=[pl.BlockSpec((None, tr, ns), lambda i, chip: (layer, i, 0))],
            out_specs=pl.BlockSpec((tr, ns), omap)),
        out_shape=SDS(full, dtype), name=name, compiler_params=_params("parallel"))(chip1, w3)


def _rms_fwd(h, g3, layer, name, deps=()):
    T, D = h.shape
    tm = _tile(T, 832)

    def body(h_ref, g_ref, *rest):
        n_ref = rest[-1]
        x = h_ref[...]
        r = lax.rsqrt(jnp.mean(x * x, axis=-1, keepdims=True) + EPS)
        n_ref[...] = ((x * r) * g_ref[...]).astype(BF16)

    return pl.pallas_call(
        body, grid=(T // tm,),
        in_specs=[pl.BlockSpec((tm, D), lambda i: (i, 0)), pl.BlockSpec((None, 1, D), lambda i: (layer, 0, 0))]
        + [ANY_SPEC] * len(deps),
        out_specs=pl.BlockSpec((tm, D), lambda i: (i, 0)), out_shape=SDS((T, D), BF16),
        name=name, compiler_params=_params("parallel"))(h, g3, *deps)


def _final(h, g2, target):
    T, D = h.shape
    tm = _tile(T, 320)
    nsub = tm // CHUNK
    nblk = target.shape[0] // CHUNK

    def body(h_ref, g_ref, *rest):
        t_refs = rest[:nsub]
        dh_ref, dhb_ref, dg_ref, loss_ref = rest[nsub:]
        i = pl.program_id(0)

        @pl.when(i == 0)
        def _():
            dg_ref[...] = jnp.zeros_like(dg_ref)
            loss_ref[...] = jnp.zeros_like(loss_ref)

        g = g_ref[...]
        for q in range(nsub):
            rows = slice(q * CHUNK, (q + 1) * CHUNK)
            x = h_ref[rows, :]
            r = lax.rsqrt(jnp.mean(x * x, axis=-1, keepdims=True) + EPS)
            xh = x * r
            live = jnp.where(i * nsub + q > 0, 1.0, 0.0).astype(F32)
            e = ((xh * g) - t_refs[q][...]) * live
            dy = e * (1.0 / D)
            dxh = dy * g
            dh = r * (dxh - xh * jnp.mean(dxh * xh, axis=-1, keepdims=True))
            dh_ref[rows, :] = dh
            dhb_ref[rows, :] = dh.astype(BF16)
            dg_ref[...] += jnp.sum(dy * xh, axis=0, keepdims=True)
            loss_ref[...] += jnp.sum(e * e) * (0.5 / D)

    row = pl.BlockSpec((tm, D), lambda i: (i, 0))
    t_specs = [pl.BlockSpec((CHUNK, D), functools.partial(lambda i, q: (jnp.clip(i * nsub + q - 1, 0, nblk - 1), 0), q=q))
               for q in range(nsub)]
    return pl.pallas_call(
        body, grid=(T // tm,),
        in_specs=[row, pl.BlockSpec((1, D), lambda i: (0, 0))] + t_specs,
        out_specs=[row, row, pl.BlockSpec((1, D), lambda i: (0, 0)), pl.BlockSpec((1, 128), lambda i: (0, 0))],
        out_shape=[SDS((T, D), F32), SDS((T, D), BF16), SDS((1, D), F32), SDS((1, 128), F32)],
        name="final_loss", compiler_params=_params("arbitrary"))(h, g2, *([target] * nsub))


def _mm_nn(a, w3, layer, name, res=None, relu2=False, deps=()):
    M, K = a.shape
    N = w3.shape[2]
    tm, tn = _mm_tiles(M, N, 2 * K, 2 * K, (4 if relu2 else 4) + (4 if res is not None else 0))

    def body(*refs):
        acc = jnp.dot(refs[0][...], refs[1][...], preferred_element_type=F32)
        if res is not None:
            acc = acc + refs[2][...]
        if relu2:
            p = jnp.maximum(acc, 0.0)
            refs[-2][...] = (p * p).astype(BF16)
            refs[-1][...] = p.astype(BF16)
        else:
            refs[-1][...] = acc

    in_specs = [pl.BlockSpec((tm, K), lambda i, j: (i, 0)), pl.BlockSpec((None, K, tn), lambda i, j: (layer, 0, j))]
    args = [a, w3]
    tile = pl.BlockSpec((tm, tn), lambda i, j: (i, j))
    if res is not None:
        in_specs.append(tile)
        args.append(res)
    in_specs += [ANY_SPEC] * len(deps)
    args += list(deps)
    return pl.pallas_call(
        body, grid=(M // tm, N // tn), in_specs=in_specs, out_specs=[tile, tile] if relu2 else tile,
        out_shape=[SDS((M, N), BF16)] * 2 if relu2 else SDS((M, N), F32),
        name=name, compiler_params=_params("parallel", "parallel"))(*args)


def _mm_nt(dy, w3, layer, name, relu=None, deps=()):
    M, N = dy.shape
    K = w3.shape[1]
    tm, tk = _mm_tiles(M, K, 2 * N, 2 * N, 4)

    def body(*refs):
        acc = lax.dot_general(refs[0][...], refs[1][...], (((1,), (1,)), ((), ())), preferred_element_type=F32)
        if relu is not None:
            acc = (acc * (2.0 * refs[2][...].astype(F32))).astype(BF16)
        refs[-1][...] = acc

    tile = pl.BlockSpec((tm, tk), lambda i, j: (i, j))
    in_specs = [pl.BlockSpec((tm, N), lambda i, j: (i, 0)), pl.BlockSpec((None, tk, N), lambda i, j: (layer, j, 0))]
    args = [dy, w3]
    if relu is not None:
        in_specs.append(tile)
        args.append(relu)
    in_specs += [ANY_SPEC] * len(deps)
    args += list(deps)
    return pl.pallas_call(
        body, grid=(M // tm, K // tk), in_specs=in_specs, out_specs=tile,
        out_shape=SDS((M, K), F32 if relu is None else BF16),
        name=name, compiler_params=_params("parallel", "parallel"))(*args)


def _row_tile(M, per_row, fixed):
    for tm in sorted((d for d in range(16, M + 1, 16) if M % d == 0), reverse=True):
        if 2 * (tm * per_row + fixed) <= MM_VMEM_BUDGET:
            return tm
    raise ValueError((M, per_row, fixed))


def _mm_nn_norm(a, w3, layer, res, g3, glayer, name, deps=()):
    M, K = a.shape
    D = w3.shape[2]
    tm = _row_tile(M, 2 * K + 10 * D, 2 * K * D)

    def body(a_ref, w_ref, r_ref, g_ref, *rest):
        h_ref, n_ref = rest[-2:]
        x = r_ref[...] + jnp.dot(a_ref[...], w_ref[...], preferred_element_type=F32)
        h_ref[...] = x
        r = lax.rsqrt(jnp.mean(x * x, axis=-1, keepdims=True) + EPS)
        n_ref[...] = ((x * r) * g_ref[...]).astype(BF16)

    row = pl.BlockSpec((tm, D), lambda i: (i, 0))
    return pl.pallas_call(
        body, grid=(M // tm,),
        in_specs=[pl.BlockSpec((tm, K), lambda i: (i, 0)), pl.BlockSpec((None, K, D), lambda i: (layer, 0, 0)), row,
                  pl.BlockSpec((None, 1, D), lambda i: (glayer, 0, 0))] + [ANY_SPEC] * len(deps),
        out_specs=[row, row], out_shape=[SDS((M, D), F32), SDS((M, D), BF16)],
        name=name, compiler_params=_params("parallel"))(a, w3, res, g3, *deps)


def _mm_nt_norm(dy, w3, layer, h, g3, glayer, dh_in, name, deps=()):
    M, N = dy.shape
    D = w3.shape[1]
    tm = _row_tile(M, 2 * N + 14 * D, 2 * N * D)

    def body(dy_ref, w_ref, h_ref, g_ref, dhi_ref, *rest):
        dh_ref, dhb_ref, dg_ref = rest[-3:]
        dn = lax.dot_general(dy_ref[...], w_ref[...], (((1,), (1,)), ((), ())), preferred_element_type=F32)
        x = h_ref[...]
        r = lax.rsqrt(jnp.mean(x * x, axis=-1, keepdims=True) + EPS)
        xh = x * r
        dxh = dn * g_ref[...]
        dh = dhi_ref[...] + r * (dxh - xh * jnp.mean(dxh * xh, axis=-1, keepdims=True))
        dh_ref[...] = dh
        dhb_ref[...] = dh.astype(BF16)

        @pl.when(pl.program_id(0) == 0)
        def _():
            dg_ref[...] = jnp.zeros_like(dg_ref)

        dg_ref[...] += jnp.sum(dn * xh, axis=0, keepdims=True)

    row = pl.BlockSpec((tm, D), lambda i: (i, 0))
    return pl.pallas_call(
        body, grid=(M // tm,),
        in_specs=[pl.BlockSpec((tm, N), lambda i: (i, 0)), pl.BlockSpec((None, D, N), lambda i: (layer, 0, 0)), row,
                  pl.BlockSpec((None, 1, D), lambda i: (glayer, 0, 0)), row] + [ANY_SPEC] * len(deps),
        out_specs=[row, row, pl.BlockSpec((1, D), lambda i: (0, 0))],
        out_shape=[SDS((M, D), F32), SDS((M, D), BF16), SDS((1, D), F32)],
        name=name, compiler_params=_params("arbitrary"))(dy, w3, h, g3, dh_in, *deps)


def _fam_dims(kind, K, N):
    return (K // 2, N // N_CHIPS) if kind == "col" else (K // (2 * N_CHIPS), N)


def _mm_tn(x, dy, kind, name):
    M, K = x.shape
    N = dy.shape[1]
    nr, nc = _fam_dims(kind, K, N)

    def body(x_ref, dy_ref, o_ref):
        res = lax.dot_general(x_ref[...], dy_ref[...], (((0,), (0,)), ((), ())), preferred_element_type=F32)
        o_ref[...] = res.astype(BF16).reshape(o_ref.shape)

    if kind == "col":
        tn = _ctile(nc)
        ct = nc // tn
        grid = (N // tn,)
        in_specs = [pl.BlockSpec((M, K), lambda j: (0, 0)), pl.BlockSpec((M, tn), lambda j: (0, j))]
        out_spec = pl.BlockSpec((2, None, nr, tn), lambda j: (0, j // ct, 0, j % ct))
    else:
        grid = (N_CHIPS,)
        in_specs = [pl.BlockSpec((M, 2 * nr), lambda i: (0, i)), pl.BlockSpec((M, N), lambda i: (0, 0))]
        out_spec = pl.BlockSpec((2, None, nr, N), lambda i: (0, i, 0, 0))
    return pl.pallas_call(
        body, grid=grid, in_specs=in_specs, out_specs=out_spec, out_shape=SDS((2, N_CHIPS, nr, nc), BF16),
        name=name, compiler_params=_params("parallel"))(x, dy)


C_EVEN = 512


def _live(rows, base, total):
    r = _row_ids((rows, 1), base)
    return jnp.logical_and(r >= PAD, r < total).astype(F32)


def _conv_taps(win, w_ref, ls, acc, flip):
    for b in range(8):
        rb = win if b == 0 else pltpu.roll(win, 96 - b, 0)
        for a in range(5):
            o = 8 * a + b
            tap = (30 - o) if flip else (o - 2)
            if 0 <= tap < CONV_WIDTH:
                acc = acc + w_ref[pl.ds(tap, 1), ls] * rb[8 * a:8 * a + CHUNK]
    return acc


def _window_sum(win, levels, forward):
    s = win
    n = win.shape[0]
    for k in range(levels):
        step = 1 << k
        s = s + pltpu.roll(s, (n - step) if forward else step, 0)
    return s


def _pool_count(base, g):
    pos = _row_ids((CHUNK, 1), base) - PAD
    return jnp.clip(pos + 1, 1, POOL_WINDOWS[g]).astype(F32)


def _even_fwd(u, cw3, cb3, lg3, lb3, pw4, pb3, ps3, j, name):
    T = u.shape[0]
    C = C_EVEN
    tm = _tile(T, 320)
    nch = tm // CHUNK
    nblk = T // CHUNK

    def body(u_ref, up_ref, cw_ref, cb_ref, lg_ref, lb_ref, pw_ref, pb_ref, ps_ref, o_ref, a_s, p_s, yc_s):
        row0 = pl.program_id(0) * tm
        up = up_ref[...]
        lp = _live(CHUNK, row0 - CHUNK, T)
        a_s[0:CHUNK, :] = up[:, 0:C] * _sigmoid(up[:, C:2 * C]) * lp
        p_s[0:CHUNK, :] = up[:, 2 * C:3 * C] * lp

        def stage(c, _):
            rs = pl.multiple_of(c * CHUNK, CHUNK)
            lv = _live(CHUNK, row0 + rs, T)
            a_s[pl.ds(rs + CHUNK, CHUNK), :] = u_ref[pl.ds(rs, CHUNK), 0:C] * _sigmoid(u_ref[pl.ds(rs, CHUNK), C:2 * C]) * lv
            p_s[pl.ds(rs + CHUNK, CHUNK), :] = u_ref[pl.ds(rs, CHUNK), 2 * C:3 * C] * lv
            return 0

        lax.fori_loop(0, nch, stage, 0)

        def chunk(c, _):
            rs = pl.multiple_of(c * CHUNK, CHUNK)
            lv = _live(CHUNK, row0 + rs, T)
            for cb in range(4):
                ls = slice(cb * 128, (cb + 1) * 128)
                win = a_s[pl.ds(pl.multiple_of(rs + 32, 32), 96), ls]
                acc = jnp.broadcast_to(cb_ref[:, ls], (CHUNK, 128))
                yc_s[:, ls] = _conv_taps(win, cw_ref, ls, acc, False)
            y = yc_s[...]
            xc = y - jnp.mean(y, axis=-1, keepdims=True)
            yn = xc * lax.rsqrt(jnp.mean(xc * xc, axis=-1, keepdims=True) + EPS) * lg_ref[...] + lb_ref[...]
            o_ref[pl.ds(rs, CHUNK), 0:C] = (yn * _sigmoid(yn) * lv).astype(BF16)
            for g in range(4):
                ls = slice(g * 128, (g + 1) * 128)
                win = p_s[pl.ds(pl.multiple_of(rs + 48, 16), 80), ls]
                s = _window_sum(win, g + 1, False)
                d = s[16:80] / _pool_count(row0 + rs, g) - win[16:80]
                yv = jnp.dot(d.astype(BF16), pw_ref[g].astype(BF16), preferred_element_type=F32) + pb_ref[:, ls]
                o_ref[pl.ds(rs, CHUNK), C + g * 128:C + (g + 1) * 128] = (yv * ps_ref[:, ls] * lv).astype(BF16)
            return 0

        lax.fori_loop(0, nch, chunk, 0)

    vec = pl.BlockSpec((None, 1, C), lambda i: (j, 0, 0))
    return pl.pallas_call(
        body, grid=(T // tm,),
        in_specs=[pl.BlockSpec((tm, 3 * C), lambda i: (i, 0)),
                  pl.BlockSpec((CHUNK, 3 * C), lambda i: (jnp.maximum(i * nch - 1, 0), 0)),
                  pl.BlockSpec((None, CONV_ROWS, C), lambda i: (j, 0, 0)), vec, vec, vec,
                  pl.BlockSpec((None, 4, 128, 128), lambda i: (j, 0, 0, 0)), vec, vec],
        out_specs=pl.BlockSpec((tm, 2 * C), lambda i: (i, 0)),
        out_shape=SDS((T, 2 * C), BF16),
        scratch_shapes=[pltpu.VMEM((tm + CHUNK, C), F32), pltpu.VMEM((tm + CHUNK, C), F32), pltpu.VMEM((CHUNK, C), F32)],
        name=name, compiler_params=_params("parallel"))(u, u, cw3, cb3, lg3, lb3, pw4, pb3, ps3)


def _even_bwd(u, dy, cw3, cb3, lg3, lb3, pw4, pb3, ps3, j, name):
    T = u.shape[0]
    C = C_EVEN
    tm = _tile(T, 320)
    nch = tm // CHUNK
    nblk = T // CHUNK
    ntile = T // tm

    def body(u_ref, up_ref, un_ref, dy_ref, dyn_ref, cw_ref, cb_ref, lg_ref, lb_ref, pw_ref, pb_ref, ps_ref,
             du_ref, dcw_ref, dcb_ref, dlg_ref, dlb_ref, dpw_ref, dpb_ref, dps_ref,
             a_s, p_s, dy_s, yc_s, dyc_s, dd_s, ddc_s, dw_s):
        i = pl.program_id(0)
        row0 = i * tm

        @pl.when(i == 0)
        def _():
            for ref in (dcb_ref, dlg_ref, dlb_ref, dpw_ref, dpb_ref, dps_ref, dw_s):
                ref[...] = jnp.zeros_like(ref)

        up = up_ref[...]
        lp = _live(CHUNK, row0 - CHUNK, T)
        a_s[0:CHUNK, :] = up[:, 0:C] * _sigmoid(up[:, C:2 * C]) * lp
        p_s[0:CHUNK, :] = up[:, 2 * C:3 * C] * lp
        un = un_ref[...]
        ln_ = _live(CHUNK, row0 + tm, T)
        a_s[tm + CHUNK:tm + 2 * CHUNK, :] = un[:, 0:C] * _sigmoid(un[:, C:2 * C]) * ln_
        p_s[tm + CHUNK:tm + 2 * CHUNK, :] = un[:, 2 * C:3 * C] * ln_
        dy_s[tm:tm + CHUNK, :] = dyn_ref[...] * ln_
        dyc_s[tm + CHUNK:tm + CHUNK + 32, :] = jnp.zeros((32, C), F32)

        def stage(c, _):
            rs = pl.multiple_of(c * CHUNK, CHUNK)
            lv = _live(CHUNK, row0 + rs, T)
            a_s[pl.ds(rs + CHUNK, CHUNK), :] = u_ref[pl.ds(rs, CHUNK), 0:C] * _sigmoid(u_ref[pl.ds(rs, CHUNK), C:2 * C]) * lv
            p_s[pl.ds(rs + CHUNK, CHUNK), :] = u_ref[pl.ds(rs, CHUNK), 2 * C:3 * C] * lv
            dy_s[pl.ds(rs, CHUNK), :] = dy_ref[pl.ds(rs, CHUNK), :] * lv
            return 0

        lax.fori_loop(0, nch, stage, 0)

        def first(c, _):
            rs = pl.multiple_of(c * CHUNK, CHUNK)
            own = jnp.where(c < nch, 1.0, 0.0).astype(F32)
            for cb in range(4):
                ls = slice(cb * 128, (cb + 1) * 128)
                win = a_s[pl.ds(pl.multiple_of(rs + 32, 32), 96), ls]
                acc = jnp.broadcast_to(cb_ref[:, ls], (CHUNK, 128))
                yc_s[:, ls] = _conv_taps(win, cw_ref, ls, acc, False)
            y = yc_s[...]
            xc = y - jnp.mean(y, axis=-1, keepdims=True)
            rstd = lax.rsqrt(jnp.mean(xc * xc, axis=-1, keepdims=True) + EPS)
            xh = xc * rstd
            yn = xh * lg_ref[...] + lb_ref[...]
            sg = _sigmoid(yn)
            dyn = dy_s[pl.ds(rs, CHUNK), 0:C] * (sg * (1.0 + yn * (1.0 - sg)))
            dlg_ref[...] += jnp.sum(dyn * xh, axis=0, keepdims=True) * own
            dlb_ref[...] += jnp.sum(dyn, axis=0, keepdims=True) * own
            dxh = dyn * lg_ref[...]
            dyc = rstd * (dxh - jnp.mean(dxh, axis=-1, keepdims=True) - xh * jnp.mean(dxh * xh, axis=-1, keepdims=True))
            dyc_s[pl.ds(rs, CHUNK), :] = dyc
            dcb_ref[...] += jnp.sum(dyc, axis=0, keepdims=True) * own
            for g in range(4):
                ls = slice(g * 128, (g + 1) * 128)
                win = p_s[pl.ds(pl.multiple_of(rs + 48, 16), 80), ls]
                s = _window_sum(win, g + 1, False)
                cnt = _pool_count(row0 + rs, g)
                d = (s[16:80] / cnt - win[16:80]).astype(BF16)
                w = pw_ref[g].astype(BF16)
                pre = jnp.dot(d, w, preferred_element_type=F32) + pb_ref[:, ls]
                dyb = dy_s[pl.ds(rs, CHUNK), C + g * 128:C + (g + 1) * 128]
                dpre = dyb * ps_ref[:, ls]
                dps_ref[:, ls] += jnp.sum(dyb * pre, axis=0, keepdims=True) * own
                dpb_ref[:, ls] += jnp.sum(dpre, axis=0, keepdims=True) * own
                dpre_b = (dpre * own).astype(BF16)
                dpw_ref[g] += lax.dot_general(d, dpre_b, (((0,), (0,)), ((), ())), preferred_element_type=F32)
                dd = lax.dot_general(dpre.astype(BF16), w, (((1,), (1,)), ((), ())), preferred_element_type=F32)
                dd_s[pl.ds(rs, CHUNK), ls] = dd
                ddc_s[pl.ds(rs, CHUNK), ls] = dd / cnt
            return 0

        lax.fori_loop(0, nch + 1, first, 0)
        ddc_s[tm + CHUNK:tm + CHUNK + 16, :] = jnp.zeros((16, C), F32)

        def second(c, _):
            rs = pl.multiple_of(c * CHUNK, CHUNK)
            lv = _live(CHUNK, row0 + rs, T)
            for cb in range(4):
                ls = slice(cb * 128, (cb + 1) * 128)
                wd = dyc_s[pl.ds(rs, 96), ls]
                da = _conv_taps(wd, cw_ref, ls, jnp.zeros((CHUNK, 128), F32), True)
                wa = a_s[pl.ds(pl.multiple_of(rs + 32, 32), 96), ls]
                dyc = dyc_s[pl.ds(rs, CHUNK), ls]
                for b in range(8):
                    rb = wa if b == 0 else pltpu.roll(wa, 96 - b, 0)
                    for a in range(5):
                        tap = 8 * a + b - 2
                        if 0 <= tap < CONV_WIDTH:
                            prod = dyc * rb[8 * a:8 * a + CHUNK]
                            part = prod[0:8]
                            for q in range(1, 8):
                                part = part + prod[8 * q:8 * q + 8]
                            dw_s[8 * tap:8 * tap + 8, ls] += part
                val = u_ref[pl.ds(rs, CHUNK), ls]
                sg = _sigmoid(u_ref[pl.ds(rs, CHUNK), C + cb * 128:C + (cb + 1) * 128])
                du_ref[pl.ds(rs, CHUNK), ls] = (da * sg * lv).astype(BF16)
                du_ref[pl.ds(rs, CHUNK), C + cb * 128:C + (cb + 1) * 128] = (da * val * sg * (1.0 - sg) * lv).astype(BF16)
            for g in range(4):
                ls = slice(g * 128, (g + 1) * 128)
                z = _window_sum(ddc_s[pl.ds(rs, 80), ls], g + 1, True)
                dpin = (z[0:CHUNK] - dd_s[pl.ds(rs, CHUNK), ls]) * lv
                du_ref[pl.ds(rs, CHUNK), 2 * C + g * 128:2 * C + (g + 1) * 128] = dpin.astype(BF16)
            return 0

        lax.fori_loop(0, nch, second, 0)

        @pl.when(i == ntile - 1)
        def _():
            for tap in range(CONV_WIDTH):
                dcw_ref[tap:tap + 1, :] = jnp.sum(dw_s[8 * tap:8 * tap + 8, :], axis=0, keepdims=True)
            dcw_ref[CONV_WIDTH:CONV_ROWS, :] = jnp.zeros((CONV_ROWS - CONV_WIDTH, C), F32)

    vec = pl.BlockSpec((None, 1, C), lambda i: (j, 0, 0))
    ovec = pl.BlockSpec((1, C), lambda i: (0, 0))
    return pl.pallas_call(
        body, grid=(ntile,),
        in_specs=[pl.BlockSpec((tm, 3 * C), lambda i: (i, 0)),
                  pl.BlockSpec((CHUNK, 3 * C), lambda i: (jnp.maximum(i * nch - 1, 0), 0)),
                  pl.BlockSpec((CHUNK, 3 * C), lambda i: (jnp.minimum((i + 1) * nch, nblk - 1), 0)),
                  pl.BlockSpec((tm, 2 * C), lambda i: (i, 0)),
                  pl.BlockSpec((CHUNK, 2 * C), lambda i: (jnp.minimum((i + 1) * nch, nblk - 1), 0)),
                  pl.BlockSpec((None, CONV_ROWS, C), lambda i: (j, 0, 0)), vec, vec, vec,
                  pl.BlockSpec((None, 4, 128, 128), lambda i: (j, 0, 0, 0)), vec, vec],
        out_specs=[pl.BlockSpec((tm, 3 * C), lambda i: (i, 0)), pl.BlockSpec((CONV_ROWS, C), lambda i: (0, 0)),
                   ovec, ovec, ovec, pl.BlockSpec((4, 128, 128), lambda i: (0, 0, 0)), ovec, ovec],
        out_shape=[SDS((T, 3 * C), BF16), SDS((CONV_ROWS, C), F32), SDS((1, C), F32), SDS((1, C), F32), SDS((1, C), F32),
                   SDS((4, 128, 128), F32), SDS((1, C), F32), SDS((1, C), F32)],
        scratch_shapes=[pltpu.VMEM((tm + 2 * CHUNK, C), F32), pltpu.VMEM((tm + 2 * CHUNK, C), F32),
                        pltpu.VMEM((tm + CHUNK, 2 * C), F32), pltpu.VMEM((CHUNK, C), F32),
                        pltpu.VMEM((tm + CHUNK + 32, C), F32), pltpu.VMEM((tm + CHUNK, C), F32),
                        pltpu.VMEM((tm + CHUNK + 16, C), F32), pltpu.VMEM((8 * CONV_ROWS, C), F32)],
        name=name, compiler_params=_params("arbitrary"))(u, u, u, dy, dy, cw3, cb3, lg3, lb3, pw4, pb3, ps3)


HI = lax.Precision.HIGHEST


def _dot_nt(a, b):
    return lax.dot_general(a, b, (((1,), (1,)), ((), ())), preferred_element_type=F32)


def _dot_tn(a, b):
    return lax.dot_general(a, b, (((0,), (0,)), ((), ())), preferred_element_type=F32)


def _tri(lower):
    r = lax.broadcasted_iota(jnp.int32, (CHUNK, CHUNK), 0)
    c = lax.broadcasted_iota(jnp.int32, (CHUNK, CHUNK), 1)
    return jnp.where((c <= r) if lower else (c >= r), 1.0, 0.0).astype(F32)


def _hgrn_gates(u_ref, lb_ref, h, D, lv):
    ls = slice(h * HEAD_DIM, (h + 1) * HEAD_DIM)
    qraw = u_ref[:, ls]
    fraw = u_ref[:, D + h * HEAD_DIM:D + (h + 1) * HEAD_DIM]
    v = u_ref[:, 2 * D + h * HEAD_DIM:2 * D + (h + 1) * HEAD_DIM] * lv
    lbv = lb_ref[:, ls]
    sig = _sigmoid(fraw)
    forget = lbv + (1.0 - lbv) * sig
    logf = jnp.log(forget) * lv
    k = (1.0 - forget) * lv
    qsig = _sigmoid(qraw)
    q = qraw * qsig * lv
    return q, k, v, logf, (qraw, qsig, sig, forget, lbv)


def _sub_parts(q, k, b, b_s, I):
    rows = slice(SUB * I, SUB * (I + 1))
    rho = jnp.zeros((1, HEAD_DIM), F32) if I == 0 else b_s[SUB * I - 1:SUB * I, :]
    eI = jnp.exp(b[rows] - rho)
    EI = jnp.exp(jnp.minimum(rho - b, EXP_CAP))
    causal = (lax.broadcasted_iota(jnp.int32, (SUB, CHUNK), 1)
              <= lax.broadcasted_iota(jnp.int32, (SUB, CHUNK), 0) + SUB * I)
    return rows, q[rows] * eI, k * EI, eI, EI, causal


def _hgrn_fwd(u, lb3, layer, gn3, j, name):
    T = u.shape[0]
    D = u.shape[1] // 4
    H = D // HEAD_DIM
    NC = T // CHUNK

    def body(u_ref, lb_ref, gn_ref, y_ref, o_ref, sall_ref, st_s, b_s, lf_s, q_s, k_s):
        n = pl.program_id(0)

        @pl.when(n == 0)
        def _():
            st_s[...] = jnp.zeros_like(st_s)

        lv = _live(CHUNK, n * CHUNK, T)
        heads = range(H)
        cols = [slice(h * HEAD_DIM, (h + 1) * HEAD_DIM) for h in heads]
        vb = []
        for h in heads:
            q, k, v, logf, _ = _hgrn_gates(u_ref, lb_ref, h, D, lv)
            q_s[:, cols[h]] = q
            k_s[:, cols[h]] = k
            lf_s[:, cols[h]] = logf
            vb.append(v.astype(BF16))
        b_s[...] = jnp.dot(_tri(True), lf_s[...], precision=HI, preferred_element_type=F32)
        ops = []
        for h in heads:
            b_h = b_s.at[:, cols[h]]
            b = b_h[...]
            q = q_s[:, cols[h]]
            k = k_s[:, cols[h]]
            blast = b_h[CHUNK - 1:CHUNK, :]
            qh = (q * jnp.exp(b)).astype(BF16)
            kt = (k * jnp.exp(blast - b)).astype(BF16)
            subs = []
            for I in range(CHUNK // SUB):
                _, qI, KI, _, _, causal = _sub_parts(q, k, b, b_h, I)
                subs.append((qI.astype(BF16), KI.astype(BF16), causal))
            ops.append((qh, kt, jnp.exp(blast), subs))
        mm = []
        for h in heads:
            qh, kt, eblast, subs = ops[h]
            st = st_s[h]
            sall_ref[h] = st
            o_inter = _dot_nt(qh, st.astype(BF16))
            st_s[h] = st * eblast + _dot_tn(vb[h], kt)
            mm.append((o_inter, [_dot_nt(qI, KI) for qI, KI, _ in subs]))
        for h in heads:
            o_inter, ps = mm[h]
            p = jnp.concatenate([jnp.where(c, x, 0.0) for x, (_, _, c) in zip(ps, ops[h][3])], axis=0).astype(BF16)
            o = o_inter + jnp.dot(p, vb[h], preferred_element_type=F32)
            o_ref[:, cols[h]] = o
            graw = u_ref[:, 3 * D + h * HEAD_DIM:3 * D + (h + 1) * HEAD_DIM]
            r = lax.rsqrt(jnp.mean(o * o, axis=-1, keepdims=True) + EPS)
            y_ref[:, cols[h]] = (((o * r) * gn_ref[...]) * (graw * _sigmoid(graw))).astype(BF16)

    return pl.pallas_call(
        body, grid=(NC,),
        in_specs=[pl.BlockSpec((CHUNK, 4 * D), lambda n: (n, 0)),
                  pl.BlockSpec((None, 1, D), lambda n: (layer, 0, 0)),
                  pl.BlockSpec((None, 1, HEAD_DIM), lambda n: (j, 0, 0))],
        out_specs=[pl.BlockSpec((CHUNK, D), lambda n: (n, 0)), pl.BlockSpec((CHUNK, D), lambda n: (n, 0)),
                   pl.BlockSpec((None, H, HEAD_DIM, HEAD_DIM), lambda n: (n, 0, 0, 0))],
        out_shape=[SDS((T, D), BF16), SDS((T, D), F32), SDS((NC, H, HEAD_DIM, HEAD_DIM), F32)],
        scratch_shapes=[pltpu.VMEM((H, HEAD_DIM, HEAD_DIM), F32)] + [pltpu.VMEM((CHUNK, D), F32)] * 4,
        name=name, compiler_params=_params("arbitrary"))(u, lb3, gn3)


def _hgrn_bwd(u, o_raw, dy, sall, lb3, layer, gn3, j, name):
    T = u.shape[0]
    D = u.shape[1] // 4
    H = D // HEAD_DIM
    NC = T // CHUNK

    def body(u_ref, o_ref, dy_ref, sall_ref, lb_ref, gn_ref, du_ref, dlb_ref, dgn_ref, dst_s, b_s, lf_s, q_s, k_s, db_s, dk_s):
        step = pl.program_id(0)
        n = NC - 1 - step

        @pl.when(step == 0)
        def _():
            dst_s[...] = jnp.zeros_like(dst_s)
            dlb_ref[...] = jnp.zeros_like(dlb_ref)
            dgn_ref[...] = jnp.zeros_like(dgn_ref)

        lv = _live(CHUNK, n * CHUNK, T)
        last_row = (_row_ids((CHUNK, 1), 0) == CHUNK - 1).astype(F32)
        gn = gn_ref[...]
        heads = range(H)
        cols = [slice(h * HEAD_DIM, (h + 1) * HEAD_DIM) for h in heads]
        vb, dob = [], []
        dgn = jnp.zeros((1, HEAD_DIM), F32)
        for h in heads:
            q, k, v, logf, _ = _hgrn_gates(u_ref, lb_ref, h, D, lv)
            q_s[:, cols[h]] = q
            k_s[:, cols[h]] = k
            lf_s[:, cols[h]] = logf
            vb.append(v.astype(BF16))
            graw = u_ref[:, 3 * D + h * HEAD_DIM:3 * D + (h + 1) * HEAD_DIM]
            gsig = _sigmoid(graw)
            o = o_ref[:, cols[h]]
            r = lax.rsqrt(jnp.mean(o * o, axis=-1, keepdims=True) + EPS)
            xh = o * r
            dyv = dy_ref[:, cols[h]]
            dsg = dyv * (graw * gsig)
            dgn = dgn + jnp.sum(dsg * xh, axis=0, keepdims=True)
            dxh = dsg * gn
            do = r * (dxh - xh * jnp.mean(dxh * xh, axis=-1, keepdims=True))
            dob.append(do.astype(BF16))
            dgraw = dyv * xh * gn * (gsig * (1.0 + graw * (1.0 - gsig)))
            du_ref[:, 3 * D + h * HEAD_DIM:3 * D + (h + 1) * HEAD_DIM] = (dgraw * lv).astype(BF16)
        dgn_ref[...] += dgn
        b_s[...] = jnp.dot(_tri(True), lf_s[...], precision=HI, preferred_element_type=F32)
        ops = []
        for h in heads:
            b_h = b_s.at[:, cols[h]]
            b = b_h[...]
            q = q_s[:, cols[h]]
            k = k_s[:, cols[h]]
            blast = b_h[CHUNK - 1:CHUNK, :]
            eb = jnp.exp(b)
            ekb = jnp.exp(blast - b)
            subs = []
            for I in range(CHUNK // SUB):
                rows, qI, KI, eI, EI, causal = _sub_parts(q, k, b, b_h, I)
                subs.append((rows, qI.astype(BF16), KI.astype(BF16), eI, EI, causal))
            ops.append((eb, ekb, jnp.exp(blast), (q * eb).astype(BF16), (k * ekb).astype(BF16), subs))
        mm = []
        for h in heads:
            eb, ekb, eblast, qhb, ktb, subs = ops[h]
            st = sall_ref[h]
            dst = dst_s[h]
            dstb = dst.astype(BF16)
            dv = _dot_nt(ktb, dstb)
            dqh = jnp.dot(dob[h], st.astype(BF16), preferred_element_type=F32)
            dkt = jnp.dot(vb[h], dstb, preferred_element_type=F32)
            dblast = jnp.sum(dst * st, axis=0, keepdims=True) * eblast
            dst_s[h] = dst * eblast + _dot_tn(dob[h], qhb)
            dp_full = _dot_nt(dob[h], vb[h])
            ps = [_dot_nt(qIb, KIb) for _, qIb, KIb, _, _, _ in subs]
            mm.append((dv, dqh, dkt, dblast, dp_full, ps))
        for h in heads:
            eb, ekb, eblast, qhb, ktb, subs = ops[h]
            dv, dqh, dkt, dblast, dp_full, ps = mm[h]
            p = jnp.concatenate([jnp.where(sub[5], x, 0.0) for x, sub in zip(ps, subs)], axis=0).astype(BF16)
            dv = dv + _dot_tn(p, dob[h])
            du_ref[:, 2 * D + h * HEAD_DIM:2 * D + (h + 1) * HEAD_DIM] = (dv * lv).astype(BF16)
            dq = dqh * eb
            db = dqh * qhb.astype(F32)
            tmp = dkt * ktb.astype(F32)
            dk = dkt * ekb
            db = db - tmp
            dblast = dblast + jnp.sum(tmp, axis=0, keepdims=True)
            dq_parts, db_parts = [], []
            for rows, qIb, KIb, eI, EI, causal in subs:
                dp = jnp.where(causal, dp_full[rows], 0.0).astype(BF16)
                dqI = jnp.dot(dp, KIb, preferred_element_type=F32)
                dKI = _dot_tn(dp, qIb)
                dq_parts.append(dqI * eI)
                db_parts.append(dqI * qIb.astype(F32))
                dk = dk + dKI * EI
                db = db - dKI * KIb.astype(F32)
            dq = dq + jnp.concatenate(dq_parts, axis=0)
            db_s[:, cols[h]] = db + jnp.concatenate(db_parts, axis=0) + last_row * dblast
            dk_s[:, cols[h]] = dk
            qraw = u_ref[:, cols[h]]
            qsig = _sigmoid(qraw)
            du_ref[:, cols[h]] = (dq * (qsig * (1.0 + qraw * (1.0 - qsig))) * lv).astype(BF16)
        lf_s[...] = jnp.dot(_tri(False), db_s[...], precision=HI, preferred_element_type=F32)
        for h in heads:
            fraw = u_ref[:, D + h * HEAD_DIM:D + (h + 1) * HEAD_DIM]
            lbv = lb_ref[:, cols[h]]
            sig = _sigmoid(fraw)
            forget = lbv + (1.0 - lbv) * sig
            dforget = (lf_s[:, cols[h]] / forget - dk_s[:, cols[h]]) * lv
            dlb_ref[:, cols[h]] += jnp.sum(dforget * (1.0 - sig), axis=0, keepdims=True)
            du_ref[:, D + h * HEAD_DIM:D + (h + 1) * HEAD_DIM] = (dforget * (1.0 - lbv) * sig * (1.0 - sig)).astype(BF16)

    rev = lambda s: (NC - 1 - s, 0)
    return pl.pallas_call(
        body, grid=(NC,),
        in_specs=[pl.BlockSpec((CHUNK, 4 * D), rev), pl.BlockSpec((CHUNK, D), rev), pl.BlockSpec((CHUNK, D), rev),
                  pl.BlockSpec((None, H, HEAD_DIM, HEAD_DIM), lambda s: (NC - 1 - s, 0, 0, 0)),
                  pl.BlockSpec((None, 1, D), lambda s: (layer, 0, 0)),
                  pl.BlockSpec((None, 1, HEAD_DIM), lambda s: (j, 0, 0))],
        out_specs=[pl.BlockSpec((CHUNK, 4 * D), rev), pl.BlockSpec((1, D), lambda s: (0, 0)),
                   pl.BlockSpec((1, HEAD_DIM), lambda s: (0, 0))],
        out_shape=[SDS((T, 4 * D), BF16), SDS((1, D), F32), SDS((1, HEAD_DIM), F32)],
        scratch_shapes=[pltpu.VMEM((H, HEAD_DIM, HEAD_DIM), F32)] + [pltpu.VMEM((CHUNK, D), F32)] * 6,
        name=name, compiler_params=_params("arbitrary"))(u, o_raw, dy, sall, lb3, gn3)


def _softmax_layers(p_ref, n_layers):
    rows = [p_ref[l:l + 1, :] for l in range(n_layers)]
    m = functools.reduce(jnp.maximum, rows)
    e = [jnp.exp(x - m) for x in rows]
    tot = functools.reduce(lambda a, b: a + b, e)
    return [x / tot for x in e]


def _lb_fwd(p):
    n_layers, D = p.shape

    def body(p_ref, o_ref):
        s = _softmax_layers(p_ref, n_layers)
        acc = jnp.zeros((1, D), F32)
        o_ref[0:1, :] = acc
        for l in range(1, n_layers):
            acc = acc + s[l]
            o_ref[l:l + 1, :] = acc

    return pl.pallas_call(body, out_shape=SDS(p.shape, F32), name="lb_fwd")(p)


def _lb_bwd(p, dlb):
    n_layers, D = p.shape

    def body(p_ref, d_ref, o_ref):
        s = _softmax_layers(p_ref, n_layers)
        ds = [jnp.zeros((1, D), F32)] * n_layers
        acc = jnp.zeros((1, D), F32)
        for l in range(n_layers - 1, 0, -1):
            acc = acc + d_ref[l:l + 1, :]
            ds[l] = acc
        dot = functools.reduce(lambda a, b: a + b, [s[l] * ds[l] for l in range(n_layers)])
        for l in range(n_layers):
            o_ref[l:l + 1, :] = s[l] * (ds[l] - dot)

    return pl.pallas_call(body, out_shape=SDS(p.shape, F32), name="lb_bwd")(p, dlb)


def _adamw(w, g, m, v, name):
    R, C = w.shape
    tr = _tile(R, 256, 8) if R % 8 == 0 else R

    def body(w_ref, g_ref, m_ref, v_ref, d_ref, mo_ref, vo_ref):
        g_ = g_ref[...]
        m_ = ADAM_B1 * m_ref[...] + (1.0 - ADAM_B1) * g_
        v_ = ADAM_B2 * v_ref[...] + (1.0 - ADAM_B2) * (g_ * g_)
        mh = m_ / (1.0 - ADAM_B1 ** ADAM_STEP)
        vh = v_ / (1.0 - ADAM_B2 ** ADAM_STEP)
        d_ref[...] = -ADAM_LR * (mh / (jnp.sqrt(vh) + ADAM_EPS) + ADAM_WD * w_ref[...])
        mo_ref[...] = m_
        vo_ref[...] = v_

    blk = pl.BlockSpec((tr, C), lambda i: (i, 0))
    return pl.pallas_call(
        body, grid=(R // tr,), in_specs=[blk] * 4, out_specs=[blk] * 3, out_shape=[SDS((R, C), F32)] * 3,
        name=name, compiler_params=_params("parallel"))(w, g, m, v)


def _adamw_layer(w3, m3, v3, g2, layer, outs, name):
    L, R, C = w3.shape
    tr = _tile(R, 256, 8)
    if outs is None:
        outs = tuple(lax.empty(w3.shape, F32) for _ in range(4))

    def body(w_ref, m_ref, v_ref, g_ref, a0, a1, a2, a3, go_ref, d_ref, mo_ref, vo_ref):
        del a0, a1, a2, a3
        g_ = g_ref[...]
        m_ = ADAM_B1 * m_ref[...] + (1.0 - ADAM_B1) * g_
        v_ = ADAM_B2 * v_ref[...] + (1.0 - ADAM_B2) * (g_ * g_)
        mh = m_ / (1.0 - ADAM_B1 ** ADAM_STEP)
        vh = v_ / (1.0 - ADAM_B2 ** ADAM_STEP)
        go_ref[...] = g_
        d_ref[...] = -ADAM_LR * (mh / (jnp.sqrt(vh) + ADAM_EPS) + ADAM_WD * w_ref[...])
        mo_ref[...] = m_
        vo_ref[...] = v_

    lay = pl.BlockSpec((None, tr, C), lambda i: (layer, i, 0))
    return pl.pallas_call(
        body, grid=(R // tr,), in_specs=[lay] * 3 + [pl.BlockSpec((tr, C), lambda i: (i, 0))] + [ANY_SPEC] * 4,
        out_specs=[lay] * 4, out_shape=[SDS(w3.shape, F32)] * 4, input_output_aliases={4: 0, 5: 1, 6: 2, 7: 3},
        name=name, compiler_params=_params("parallel"))(w3, m3, v3, g2, *outs)


SEM_SPEC = pl.BlockSpec(memory_space=pltpu.SEMAPHORE)
HBM_SPEC = pl.BlockSpec(memory_space=pltpu.HBM)
EFFECT = pltpu.SideEffectType.DATAFLOW_SIDE_EFFECTING
N_DEV = 2 * N_CHIPS


def _position():
    x, y, c = lax.axis_index("x"), lax.axis_index("y"), lax.axis_index("c")
    chips = [(1 - x, y), (x, 1 - y), (1 - x, 1 - y)]
    return x, y, c, chips


def _split_start(name, plan, bufs, n_sems, deps=(), earlier=None):
    n = len(bufs)
    held = () if earlier is None else tuple(earlier[1:])

    def body(*refs):
        first_out = n + len(held) + len(deps)
        if earlier is not None:
            sends, recvs = earlier[0](refs[:n], refs[n], refs[n + 1])
            for kw in sends:
                pltpu.make_async_remote_copy(**kw).wait_send()
            for kw in recvs:
                pltpu.make_async_remote_copy(**kw).wait_recv()
        sends, _ = plan(refs[:n], refs[first_out], refs[first_out + 1])
        for kw in sends:
            pltpu.make_async_remote_copy(**kw).start()
        refs[-1][...] = jnp.zeros_like(refs[-1])

    out = pl.pallas_call(
        body, name=name,
        out_shape=(pltpu.SemaphoreType.DMA((n_sems,)), pltpu.SemaphoreType.DMA((n_sems,)),
                   *[pltpu.HBM(b.shape, b.dtype) for b in bufs], SDS((8, 128), F32)),
        in_specs=[HBM_SPEC] * n + [SEM_SPEC] * len(held) + [ANY_SPEC] * len(deps),
        out_specs=(SEM_SPEC, SEM_SPEC, *[HBM_SPEC] * n, pl.BlockSpec(memory_space=pltpu.VMEM)),
        input_output_aliases={i: 2 + i for i in range(n)},
        compiler_params=pltpu.CompilerParams(has_side_effects=EFFECT),
    )(*[pltpu.with_memory_space_constraint(b, pltpu.HBM) for b in bufs], *held, *deps)
    return out[0], out[1], list(out[2:2 + n]), out[-1]


def _split_wait(name, plan, send_sems, recv_sems, bufs, after=()):
    n = len(bufs)

    def body(*refs):
        sends, recvs = plan(refs[:n], refs[n], refs[n + 1])
        for kw in sends:
            pltpu.make_async_remote_copy(**kw).wait_send()
        for kw in recvs:
            pltpu.make_async_remote_copy(**kw).wait_recv()

    out = pl.pallas_call(
        body, name=name, out_shape=tuple(pltpu.HBM(b.shape, b.dtype) for b in bufs),
        in_specs=[HBM_SPEC] * n + [SEM_SPEC, SEM_SPEC] + [ANY_SPEC] * len(after),
        out_specs=tuple([HBM_SPEC] * n), input_output_aliases={i: i for i in range(n)},
        compiler_params=pltpu.CompilerParams(has_side_effects=EFFECT),
    )(*bufs, send_sems, recv_sems, *after)
    return list(out)


def _region(kind, ref, chip, half):
    K, N = ref.shape
    if kind == "col":
        return ref.at[pl.ds(half * (K // 2), K // 2), pl.ds(chip * (N // N_CHIPS), N // N_CHIPS)]
    rows = K // (2 * N_CHIPS)
    return ref.at[pl.ds((2 * chip + half) * rows, rows), :]


def _gather_plan(kinds, over_chips):
    def plan(refs, send_sems, recv_sems):
        x, y, c, chips = _position()
        sends, recvs = [], []
        for f, (ref, kind) in enumerate(zip(refs, kinds)):
            for k, chip in enumerate(chips):
                theirs = 2 * chip[0] + chip[1]
                sem = dict(send_sem=send_sems.at[3 * f + k], recv_sem=recv_sems.at[3 * f + k], device_id_type=MESH)
                if over_chips:
                    out, back, to = _region(kind, ref, 2 * x + y, c), _region(kind, ref, theirs, c), (*chip, c)
                else:
                    out, back, to = _region(kind, ref, theirs, c), _region(kind, ref, theirs, 1 - c), (x, y, 1 - c)
                sends.append(dict(src_ref=out, dst_ref=out, device_id=to, **sem))
                recvs.append(dict(src_ref=back, dst_ref=back, device_id=to, **sem))
        return sends, recvs
    return plan


def _reduce_plan(refs, send_sems, recv_sems):
    x, y, c, _ = _position()
    me = 4 * x + 2 * y + c
    sends, recvs = [], []
    for f in range(len(refs) // 2):
        acc, land = refs[2 * f], refs[2 * f + 1]
        for d in range(1, N_DEV):
            t = (me + d) % N_DEV
            to = dict(device_id=(t // 4, (t // 2) % 2, t % 2), device_id_type=MESH)
            slot = N_DEV - 1 - d
            sends.append(dict(src_ref=acc.at[t % 2, t // 2], dst_ref=land.at[slot], send_sem=send_sems.at[7 * f + d - 1],
                              recv_sem=recv_sems.at[7 * f + slot], **to))
            recvs.append(dict(src_ref=land.at[d - 1], dst_ref=land.at[d - 1], send_sem=send_sems.at[7 * f + d - 1],
                              recv_sem=recv_sems.at[7 * f + d - 1], **to))
    return sends, recvs


def _swap_plan(refs, send_sems, recv_sems):
    x, y, c, _ = _position()
    sends, recvs = [], []
    for f, g in enumerate(refs):
        sem = dict(send_sem=send_sems.at[f], recv_sem=recv_sems.at[f], device_id=(x, y, 1 - c), device_id_type=MESH)
        sends.append(dict(src_ref=g.at[c], dst_ref=g.at[c], **sem))
        recvs.append(dict(src_ref=g.at[1 - c], dst_ref=g.at[1 - c], **sem))
    return sends, recvs


def _sum_pieces(ids2, acc, land, name):
    _, _, nr, nc = acc.shape
    tr = _tile(nr, 256, 16)

    def body(ids_ref, own_ref, land_ref, o_ref):
        del ids_ref
        s = own_ref[...].astype(F32)
        for k in range(N_DEV - 1):
            s = s + land_ref[k].astype(F32)
        o_ref[...] = s

    return pl.pallas_call(
        body,
        grid_spec=pltpu.PrefetchScalarGridSpec(
            num_scalar_prefetch=1, grid=(nr // tr,),
            in_specs=[pl.BlockSpec((None, None, tr, nc), lambda i, ids: (ids[0], ids[1], i, 0)),
                      pl.BlockSpec((N_DEV - 1, tr, nc), lambda i, ids: (0, i, 0))],
            out_specs=pl.BlockSpec((None, tr, nc), lambda i, ids: (ids[0], i, 0))),
        out_shape=SDS((2, nr, nc), F32), name=name, compiler_params=_params("parallel"))(ids2, acc, land)


def _small_plan(refs, send_sems, recv_sems):
    x, y, c, _ = _position()
    me = 4 * x + 2 * y + c
    own, land = refs
    sends, recvs = [], []
    for d in range(1, N_DEV):
        t = (me + d) % N_DEV
        to = dict(device_id=(t // 4, (t // 2) % 2, t % 2), device_id_type=MESH)
        sends.append(dict(src_ref=own, dst_ref=land.at[me], send_sem=send_sems.at[d - 1],
                          recv_sem=recv_sems.at[N_DEV - 1 - d], **to))
        recvs.append(dict(src_ref=land.at[t], dst_ref=land.at[t], send_sem=send_sems.at[d - 1],
                          recv_sem=recv_sems.at[d - 1], **to))
    return sends, recvs


def _sum_blocks(me1, own, land):
    def body(me_ref, own_ref, land_ref, o_ref):
        acc = None
        for d in range(N_DEV):
            term = jnp.where(me_ref[0] == d, own_ref[...], land_ref[d])
            acc = term if acc is None else acc + term
        o_ref[...] = acc

    return pl.pallas_call(
        body,
        grid_spec=pltpu.PrefetchScalarGridSpec(
            num_scalar_prefetch=1, grid=(1,),
            in_specs=[pl.BlockSpec(own.shape, lambda i, me: (0, 0)), pl.BlockSpec(land.shape, lambda i, me: (0, 0, 0))],
            out_specs=pl.BlockSpec(own.shape, lambda i, me: (0, 0))),
        out_shape=SDS(own.shape, F32), name="sum_small", compiler_params=_params("arbitrary"))(me1, own, land)


BIG = {"ev_w_in": "col", "ev_w_out": "row", "od_w_in": "col", "od_w_out": "row", "mlp_w1": "col", "mlp_w2": "row"}
WEIGHTS = ("meta_tokens", "mix_norm_g", "mlp_norm_g", "final_norm_g", "ev_w_in", "ev_conv_w", "ev_conv_b", "ev_ln_g",
           "ev_ln_b", "ev_pool_w", "ev_pool_b", "ev_pool_scale", "ev_w_out", "od_w_in", "od_gnorm_g", "od_w_out",
           "lb_param", "mlp_w1", "mlp_w2")
PACK_UNIT = 1024


def _mixer_names(layer):
    return ("ev_w_in", "ev_w_out") if layer % 2 == 0 else ("od_w_in", "od_w_out")


def _pack(arrays):
    flat = []
    for a in arrays:
        a = a.reshape(-1)
        flat.append(jnp.pad(a, (0, (-a.shape[0]) % PACK_UNIT)))
    return jnp.concatenate(flat).reshape(-1, 128)


def _unpack(packed, shapes):
    flat = packed.reshape(-1)
    out, off = [], 0
    for s in shapes:
        size = 1
        for d in s:
            size *= d
        out.append(flat[off:off + size].reshape(s))
        off += size + (-size) % PACK_UNIT
    return out


def _local_step(x2, target, P, weights, boundary, first_deps=()):
    D = x2.shape[1]
    n_layers = P["mix_norm_g"].shape[0]
    h = jnp.concatenate([jnp.zeros((PAD, D), F32), P["meta_full"], x2], axis=0)
    mix_g = P["mix_norm_g"].reshape(n_layers, 1, D)
    mlp_g = P["mlp_norm_g"].reshape(n_layers, 1, D)
    vec = lambda a: a.reshape(a.shape[0], 1, -1)
    cb3, lg3, lnb3, ps3 = vec(P["ev_conv_b"]), vec(P["ev_ln_g"]), vec(P["ev_ln_b"]), vec(P["ev_pool_scale"])
    pb3 = vec(P["ev_pool_b"])
    gn3 = vec(P["od_gnorm_g"])
    lb_all = _lb_fwd(P["lb_param"])
    lb3 = lb_all.reshape(n_layers, 1, D)
    even = (cb3, lg3, lnb3, P["ev_pool_w"], pb3, ps3)

    saved = []
    deps = tuple(first_deps)
    for layer in range(n_layers):
        j = layer // 2
        w_in, w_out = _mixer_names(layer)
        W = {}
        s = {"h": h, "W": W}
        s["n"] = _rms_fwd(h, mix_g, layer, "mix_norm_0", deps=deps) if layer == 0 else n_next
        deps = ()
        W[w_in], held = weights(layer, w_in, (s["n"],))
        s["u"] = _mm_nn(s["n"], W[w_in], 0, f"mix_in_{layer}", deps=held)
        if layer % 2 == 0:
            s["y"] = _even_fwd(s["u"], P["conv_w_full"], *even, j, f"even_fwd_{layer}")
        else:
            s["y"], s["o"], s["sall"] = _hgrn_fwd(s["u"], lb3, layer, gn3, j, f"hgrn_fwd_{layer}")
        W[w_out], held = weights(layer, w_out, (s["y"],))
        h, s["n2"] = _mm_nn_norm(s["y"], W[w_out], 0, h, mlp_g, layer, f"mix_out_{layer}", deps=held)
        s["h1"] = h
        W["mlp_w1"], held = weights(layer, "mlp_w1", (s["n2"],))
        s["act"], s["relu"] = _mm_nn(s["n2"], W["mlp_w1"], 0, f"mlp_up_{layer}", relu2=True, deps=held)
        W["mlp_w2"], held = weights(layer, "mlp_w2", (s["act"],))
        if layer + 1 < n_layers:
            h, n_next = _mm_nn_norm(s["act"], W["mlp_w2"], 0, h, mix_g, layer + 1, f"mlp_down_{layer}", deps=held)
        else:
            h = _mm_nn(s["act"], W["mlp_w2"], 0, f"mlp_down_{layer}", res=h, deps=held)
        saved.append(s)

    dh, dhb, dg_final, loss = _final(h, P["final_norm_g"].reshape(1, D), target)

    small = {"final_norm_g": dg_final}
    per_layer = {k: [None] * n_layers for k in ("mix_norm_g", "mlp_norm_g", "lb")}
    per_pair = {k: [None] * (n_layers // 2) for k in
                ("ev_conv_w", "ev_conv_b", "ev_ln_g", "ev_ln_b", "ev_pool_w", "ev_pool_b", "ev_pool_scale", "od_gnorm_g")}
    for layer in reversed(range(n_layers)):
        j = layer // 2
        s = saved[layer]
        W = s["W"]
        w_in, w_out = _mixer_names(layer)
        dz = _mm_nt(dhb, W["mlp_w2"], 0, f"d_act_{layer}", relu=s["relu"], deps=deps)
        dw2 = _mm_tn(s["act"], dhb, "row", f"dw2_{layer}")
        dw1 = _mm_tn(s["n2"], dz, "col", f"dw1_{layer}")
        dh, dhb, per_layer["mlp_norm_g"][layer] = _mm_nt_norm(dz, W["mlp_w1"], 0, s["h1"], mlp_g, layer, dh, f"d_n2_{layer}")
        deps = boundary(f"mlp{layer}", {("mlp_w1", layer): dw1, ("mlp_w2", layer): dw2}, (dhb, dw1, dw2))
        dy = _mm_nt(dhb, W[w_out], 0, f"d_y_{layer}", deps=deps)
        dwout = _mm_tn(s["y"], dhb, "row", f"dwout_{layer}")
        if layer % 2 == 0:
            du, dcw, dcb, dlg, dlnb, dpw, dpb, dps = _even_bwd(s["u"], dy, P["conv_w_full"], *even, j, f"even_bwd_{layer}")
            for k, val in (("ev_conv_w", dcw), ("ev_conv_b", dcb), ("ev_ln_g", dlg), ("ev_ln_b", dlnb),
                           ("ev_pool_w", dpw), ("ev_pool_b", dpb), ("ev_pool_scale", dps)):
                per_pair[k][j] = val
        else:
            du, per_layer["lb"][layer], per_pair["od_gnorm_g"][j] = _hgrn_bwd(
                s["u"], s["o"], dy, s["sall"], lb3, layer, gn3, j, f"hgrn_bwd_{layer}")
        dwin = _mm_tn(s["n"], du, "col", f"dwin_{layer}")
        deps = boundary(f"mix{layer}", {(w_in, j): dwin, (w_out, j): dwout}, (du, dwin, dwout))
        dh, dhb, per_layer["mix_norm_g"][layer] = _mm_nt_norm(du, W[w_in], 0, s["h"], mix_g, layer, dh, f"d_n_{layer}", deps=deps)
        deps = ()

    small["mix_norm_g"] = jnp.concatenate(per_layer["mix_norm_g"], axis=0)
    small["mlp_norm_g"] = jnp.concatenate(per_layer["mlp_norm_g"], axis=0)
    dlb_all = jnp.concatenate([jnp.zeros((1, D), F32) if g is None else g for g in per_layer["lb"]], axis=0)
    small["lb_param"] = _lb_bwd(P["lb_param"], dlb_all)
    for k, vals in per_pair.items():
        small[k] = jnp.stack(vals, axis=0)
    small["meta_tokens"] = dh[PAD:LEAD]
    return loss, dh, small


def kernel(x, meta_tokens, mix_norm_g, mlp_norm_g, final_norm_g, ev_w_in, ev_conv_w, ev_conv_b, ev_ln_g, ev_ln_b, ev_pool_w, ev_pool_b, ev_pool_scale, ev_w_out, od_w_in, od_gnorm_g, od_w_out, lb_param, mlp_w1, mlp_w2, loss_target, m_meta_tokens, m_mix_norm_g, m_mlp_norm_g, m_final_norm_g, m_ev_w_in, m_ev_conv_w, m_ev_conv_b, m_ev_ln_g, m_ev_ln_b, m_ev_pool_w, m_ev_pool_b, m_ev_pool_scale, m_ev_w_out, m_od_w_in, m_od_gnorm_g, m_od_w_out, m_lb_param, m_mlp_w1, m_mlp_w2, v_meta_tokens, v_mix_norm_g, v_mlp_norm_g, v_final_norm_g, v_ev_w_in, v_ev_conv_w, v_ev_conv_b, v_ev_ln_g, v_ev_ln_b, v_ev_pool_w, v_ev_pool_b, v_ev_pool_scale, v_ev_w_out, v_od_w_in, v_od_gnorm_g, v_od_w_out, v_lb_param, v_mlp_w1, v_mlp_w2):
    given = dict(locals())
    w = {n: given[n] for n in WEIGHTS}
    m = {n: given["m_" + n] for n in WEIGHTS}
    v = {n: given["v_" + n] for n in WEIGHTS}
    n_layers = mix_norm_g.shape[0]
    core = lax.axis_index("c").astype(jnp.int32)
    chip = (2 * lax.axis_index("x") + lax.axis_index("y")).astype(jnp.int32)
    chip1 = chip.reshape(1)
    ids2 = jnp.stack([core, chip])

    conv_pad = jnp.pad(ev_conv_w, ((0, 0), (0, CONV_ROWS - CONV_WIDTH), (0, 0)))
    stages = [[(0, n)] for n in (*_mixer_names(0), "mlp_w1", "mlp_w2")]
    stages += [[(layer, n) for n in (*_mixer_names(layer), "mlp_w1", "mlp_w2")] for layer in range(1, n_layers)]
    gathers, where, token = [], {}, ()
    for k, stage in enumerate(stages):
        index = [layer if n.startswith("mlp") else layer // 2 for layer, n in stage]
        kinds = [BIG[n] for _, n in stage]
        bufs = [_cast_place(w[n], i, BIG[n], chip1, BF16, f"place_{n}_{i}") for (_, n), i in zip(stage, index)]
        if k == 0:
            bufs.append(_cast_place(meta_tokens[None], 0, "col", chip1, F32, "place_meta"))
            bufs.append(_cast_place(conv_pad.reshape(1, -1, conv_pad.shape[2]), 0, "col", chip1, F32, "place_conv_w"))
            kinds += ["col", "col"]
        plan = _gather_plan(kinds, True)
        ss, rs, bufs, tok = _split_start(f"gather_start_{k}", plan, bufs, 3 * len(bufs), deps=token)
        token = (tok,)
        gathers.append((kinds, plan, ss, rs, bufs))
        where.update({key: (k, f) for f, key in enumerate(stage)})

    landed, passed, held = {}, {}, []

    def hand_on(k, deps):
        if k not in passed:
            kinds, plan, ss, rs, bufs = gathers[k]
            to_sibling = _gather_plan(kinds, False)
            ss, rs, bufs, tok = _split_start(f"gather_pass_{k}", to_sibling, bufs, 3 * len(bufs), deps=deps, earlier=(plan, ss, rs))
            passed[k] = (to_sibling, ss, rs, bufs)
            held.append(tok)

    def arrived(k, after):
        if k not in landed:
            hand_on(k, after)
            landed[k] = _split_wait(f"gather_wait_{k}", *passed[k], after)
            if k + 1 < len(stages) and len(stages[k + 1]) > 1:
                hand_on(k + 1, tuple(landed[k][:1]))
        return landed[k]

    def weights(layer, name, after):
        k, f = where[(layer, name)]
        full = arrived(k, after)[f][None]
        tokens = tuple(held)
        held.clear()
        return full, tokens

    first = arrived(0, token)
    P = {n: w[n] for n in ("mix_norm_g", "mlp_norm_g", "final_norm_g", "ev_conv_b", "ev_ln_g", "ev_ln_b", "ev_pool_w",
                           "ev_pool_b", "ev_pool_scale", "od_gnorm_g", "lb_param")}
    P["meta_full"] = first[1]
    P["conv_w_full"] = first[2].reshape(ev_conv_w.shape[0], CONV_ROWS, -1)

    pending, outs = [], {n: None for n in BIG}

    def advance(after):
        tokens, still = [], []
        for st in pending:
            if st["phase"] == 1:
                bufs = _split_wait(f"reduce_wait_{st['tag']}", _reduce_plan, st["ss"], st["rs"], st["bufs"], after)
                halves = [_sum_pieces(ids2, bufs[2 * f], bufs[2 * f + 1], f"sum_{st['tag']}_{f}") for f in range(len(bufs) // 2)]
                ss, rs, halves, tok = _split_start(f"swap_start_{st['tag']}", _swap_plan, halves, len(halves))
                tokens.append(tok)
                still.append(dict(st, phase=2, ss=ss, rs=rs, bufs=halves))
            else:
                grads = _split_wait(f"swap_wait_{st['tag']}", _swap_plan, st["ss"], st["rs"], st["bufs"], after)
                for (n, i), g in zip(st["keys"], grads):
                    outs[n] = _adamw_layer(w[n], m[n], v[n], g.reshape(w[n].shape[1:]), i, outs[n], f"adamw_{n}_{i}")
        pending[:] = still
        return tokens

    def boundary(tag, grads, after):
        tokens = advance(after)
        bufs = []
        for acc in grads.values():
            bufs += [acc, lax.empty((N_DEV - 1,) + acc.shape[2:], BF16)]
        ss, rs, bufs, tok = _split_start(f"reduce_start_{tag}", _reduce_plan, bufs, 7 * len(grads))
        pending.append(dict(phase=1, tag=tag, keys=list(grads), ss=ss, rs=rs, bufs=bufs))
        return tuple(tokens + [tok])

    loss, dh, small = _local_step(x[0], loss_target[0], P, weights, boundary, first_deps=token)

    order = [n for n in WEIGHTS if n not in BIG]
    block = _pack([small[n] for n in order] + [loss])
    ss, rs, bufs, tok = _split_start("small_start", _small_plan, [block, lax.empty((N_DEV,) + block.shape, F32)], N_DEV - 1)
    advance((tok,))
    advance((tok,))
    block, land = _split_wait("small_wait", _small_plan, ss, rs, bufs, tuple(outs[n][0] for n in BIG))
    packed = _sum_blocks((4 * lax.axis_index("x") + 2 * lax.axis_index("y") + lax.axis_index("c")).astype(jnp.int32).reshape(1), block, land)
    total = _unpack(packed, [small[n].shape for n in order] + [loss.shape])
    loss_sum = total[-1][0, 0]
    gsmall = dict(zip(order, total[:-1]))
    gsmall["meta_tokens"] = lax.dynamic_slice_in_dim(gsmall["meta_tokens"], chip * meta_tokens.shape[1], meta_tokens.shape[1], 1)
    gsmall["ev_conv_w"] = lax.dynamic_slice_in_dim(gsmall["ev_conv_w"][:, :CONV_WIDTH], chip * ev_conv_w.shape[2], ev_conv_w.shape[2], 2)

    g_out, d_out, m_out, v_out = {}, {}, {}, {}
    for n in WEIGHTS:
        if n in BIG:
            g_out[n], d_out[n], m_out[n], v_out[n] = outs[n]
            continue
        shape = w[n].shape
        g = gsmall[n].reshape(shape)
        cols = shape[-1] if len(shape) > 1 else 128
        two = lambda a: a.reshape(-1, cols)
        d_, m_, v_ = _adamw(two(w[n]), two(g), two(m[n]), two(v[n]), f"adamw_{n}")
        g_out[n], d_out[n], m_out[n], v_out[n] = g, d_.reshape(shape), m_.reshape(shape), v_.reshape(shape)

    grad_x = dh[LEAD:][None]
    return (loss_sum, grad_x, *[g_out[n] for n in WEIGHTS], *[d_out[n] for n in WEIGHTS],
            *[m_out[n] for n in WEIGHTS], *[v_out[n] for n in WEIGHTS])
```

```python
import functools

import jax
import jax.numpy as jnp
from jax import lax
from jax.experimental import pallas as pl
from jax.experimental.pallas import tpu as pltpu

F32 = jnp.float32
BF16 = jnp.bfloat16
SDS = jax.ShapeDtypeStruct
MESH = pl.DeviceIdType.MESH
ANY_SPEC = pl.BlockSpec(memory_space=pl.ANY)

N_META = 16
CHUNK = 64
LEAD = CHUNK
PAD = LEAD - N_META
CONV_WIDTH = 31
CONV_ROWS = 32
POOL_WINDOWS = (2, 4, 8, 16)
HEAD_DIM = 128
SUB = 16
EXP_CAP = 80.0
EPS = 1e-6
ADAM_LR = 0.001
ADAM_B1 = 0.9
ADAM_B2 = 0.999
ADAM_EPS = 1e-08
ADAM_WD = 0.01
ADAM_STEP = 10
N_CHIPS = 4
VMEM_LIMIT = 52 << 20
MM_VMEM_BUDGET = 44 << 20


def _params(*sem):
    return pltpu.CompilerParams(dimension_semantics=sem if sem else None, vmem_limit_bytes=VMEM_LIMIT)


def _tile(n, target, unit=CHUNK):
    best = None
    for t in range(unit, min(n, target) + 1, unit):
        if n % t == 0:
            best = t
    assert best is not None, (n, target, unit)
    return best


def _ctile(n, target=512):
    for t in (512, 384, 256, 128):
        if t <= target and n % t == 0:
            return t
    raise ValueError(n)


def _mm_tiles(M, N, per_row, per_col, per_elem):
    best = None
    for tn in (512, 384, 256, 128):
        if N % tn:
            continue
        for tm in sorted((d for d in range(16, M + 1, 16) if M % d == 0), reverse=True):
            if 2 * (tm * per_row + tn * per_col + tm * tn * per_elem) <= MM_VMEM_BUDGET:
                if best is None or tm * tn > best[0] * best[1]:
                    best = (tm, tn)
                break
    assert best is not None, (M, N)
    return best


def _sigmoid(x):
    return 1.0 / (1.0 + jnp.exp(-x))


def _row_ids(shape, base):
    return lax.broadcasted_iota(jnp.int32, shape, 0) + base


def _cast_place(w3, layer, kind, chip1, dtype, name):
    _, ks, ns = w3.shape
    tr = _tile(ks, 512, 16)
    full = (ks, ns * N_CHIPS) if kind == "col" else (ks * N_CHIPS, ns)

    def body(chip_ref, w_ref, o_ref):
        del chip_ref
        o_ref[...] = w_ref[...].astype(dtype)

    omap = (lambda i, chip: (i, chip[0])) if kind == "col" else (lambda i, chip: (chip[0] * (ks // tr) + i, 0))
    return pl.pallas_call(
        body,
        grid_spec=pltpu.PrefetchScalarGridSpec(
            num_scalar_prefetch=1, grid=(ks // tr,),
            in_specs=[pl.BlockSpec((None, tr, ns), lambda i, chip: (layer, i, 0))],
            out_specs=pl.BlockSpec((tr, ns), omap)),
        out_shape=SDS(full, dtype), name=name, compiler_params=_params("parallel"))(chip1, w3)


def _rms_fwd(h, g3, layer, name, deps=()):
    T, D = h.shape
    tm = _tile(T, 832)

    def body(h_ref, g_ref, *rest):
        n_ref = rest[-1]
        x = h_ref[...]
        r = lax.rsqrt(jnp.mean(x * x, axis=-1, keepdims=True) + EPS)
        n_ref[...] = ((x * r) * g_ref[...]).astype(BF16)

    return pl.pallas_call(
        body, grid=(T // tm,),
        in_specs=[pl.BlockSpec((tm, D), lambda i: (i, 0)), pl.BlockSpec((None, 1, D), lambda i: (layer, 0, 0))]
        + [ANY_SPEC] * len(deps),
        out_specs=pl.BlockSpec((tm, D), lambda i: (i, 0)), out_shape=SDS((T, D), BF16),
        name=name, compiler_params=_params("parallel"))(h, g3, *deps)


def _final(h, g2, target):
    T, D = h.shape
    tm = _tile(T, 320)
    nsub = tm // CHUNK
    nblk = target.shape[0] // CHUNK

    def body(h_ref, g_ref, *rest):
        t_refs = rest[:nsub]
        dh_ref, dhb_ref, dg_ref, loss_ref = rest[nsub:]
        i = pl.program_id(0)

        @pl.when(i == 0)
        def _():
            dg_ref[...] = jnp.zeros_like(dg_ref)
            loss_ref[...] = jnp.zeros_like(loss_ref)

        g = g_ref[...]
        for q in range(nsub):
            rows = slice(q * CHUNK, (q + 1) * CHUNK)
            x = h_ref[rows, :]
            r = lax.rsqrt(jnp.mean(x * x, axis=-1, keepdims=True) + EPS)
            xh = x * r
            live = jnp.where(i * nsub + q > 0, 1.0, 0.0).astype(F32)
            e = ((xh * g) - t_refs[q][...]) * live
            dy = e * (1.0 / D)
            dxh = dy * g
            dh = r * (dxh - xh * jnp.mean(dxh * xh, axis=-1, keepdims=True))
            dh_ref[rows, :] = dh
            dhb_ref[rows, :] = dh.astype(BF16)
            dg_ref[...] += jnp.sum(dy * xh, axis=0, keepdims=True)
            loss_ref[...] += jnp.sum(e * e) * (0.5 / D)

    row = pl.BlockSpec((tm, D), lambda i: (i, 0))
    t_specs = [pl.BlockSpec((CHUNK, D), functools.partial(lambda i, q: (jnp.clip(i * nsub + q - 1, 0, nblk - 1), 0), q=q))
               for q in range(nsub)]
    return pl.pallas_call(
        body, grid=(T // tm,),
        in_specs=[row, pl.BlockSpec((1, D), lambda i: (0, 0))] + t_specs,
        out_specs=[row, row, pl.BlockSpec((1, D), lambda i: (0, 0)), pl.BlockSpec((1, 128), lambda i: (0, 0))],
        out_shape=[SDS((T, D), F32), SDS((T, D), BF16), SDS((1, D), F32), SDS((1, 128), F32)],
        name="final_loss", compiler_params=_params("arbitrary"))(h, g2, *([target] * nsub))


def _mm_nn(a, w3, layer, name, res=None, relu=False, square=False, deps=()):
    M, K = a.shape
    N = w3.shape[2]
    tm, tn = _mm_tiles(M, N, 2 * K, 2 * K, (2 if relu else 4) + (4 if res is not None else 0))

    def body(*refs):
        lhs = refs[0][...]
        acc = jnp.dot(lhs * lhs if square else lhs, refs[1][...], preferred_element_type=F32)
        if res is not None:
            acc = acc + refs[2][...]
        refs[-1][...] = jnp.maximum(acc, 0.0).astype(BF16) if relu else acc

    in_specs = [pl.BlockSpec((tm, K), lambda i, j: (i, 0)), pl.BlockSpec((None, K, tn), lambda i, j: (layer, 0, j))]
    args = [a, w3]
    tile = pl.BlockSpec((tm, tn), lambda i, j: (i, j))
    if res is not None:
        in_specs.append(tile)
        args.append(res)
    in_specs += [ANY_SPEC] * len(deps)
    args += list(deps)
    return pl.pallas_call(
        body, grid=(M // tm, N // tn), in_specs=in_specs, out_specs=tile,
        out_shape=SDS((M, N), BF16 if relu else F32),
        name=name, compiler_params=_params("parallel", "parallel"))(*args)


def _mm_nt(dy, w3, layer, name, relu=None, deps=()):
    M, N = dy.shape
    K = w3.shape[1]
    tm, tk = _mm_tiles(M, K, 2 * N, 2 * N, 4)

    def body(*refs):
        acc = lax.dot_general(refs[0][...], refs[1][...], (((1,), (1,)), ((), ())), preferred_element_type=F32)
        if relu is not None:
            acc = (acc * (2.0 * refs[2][...].astype(F32))).astype(BF16)
        refs[-1][...] = acc

    tile = pl.BlockSpec((tm, tk), lambda i, j: (i, j))
    in_specs = [pl.BlockSpec((tm, N), lambda i, j: (i, 0)), pl.BlockSpec((None, tk, N), lambda i, j: (layer, j, 0))]
    args = [dy, w3]
    if relu is not None:
        in_specs.append(tile)
        args.append(relu)
    in_specs += [ANY_SPEC] * len(deps)
    args += list(deps)
    return pl.pallas_call(
        body, grid=(M // tm, K // tk), in_specs=in_specs, out_specs=tile,
        out_shape=SDS((M, K), F32 if relu is None else BF16),
        name=name, compiler_params=_params("parallel", "parallel"))(*args)


def _row_tile(M, per_row, fixed):
    for tm in sorted((d for d in range(16, M + 1, 16) if M % d == 0), reverse=True):
        if 2 * (tm * per_row + fixed) <= MM_VMEM_BUDGET:
            return tm
    raise ValueError((M, per_row, fixed))


def _mm_nn_norm(a, w3, layer, res, g3, glayer, name, square=False, deps=()):
    M, K = a.shape
    D = w3.shape[2]
    tm = _row_tile(M, 2 * K + 10 * D, 2 * K * D)

    def body(a_ref, w_ref, r_ref, g_ref, *rest):
        h_ref, n_ref = rest[-2:]
        lhs = a_ref[...]
        x = r_ref[...] + jnp.dot(lhs * lhs if square else lhs, w_ref[...], preferred_element_type=F32)
        h_ref[...] = x
        r = lax.rsqrt(jnp.mean(x * x, axis=-1, keepdims=True) + EPS)
        n_ref[...] = ((x * r) * g_ref[...]).astype(BF16)

    row = pl.BlockSpec((tm, D), lambda i: (i, 0))
    return pl.pallas_call(
        body, grid=(M // tm,),
        in_specs=[pl.BlockSpec((tm, K), lambda i: (i, 0)), pl.BlockSpec((None, K, D), lambda i: (layer, 0, 0)), row,
                  pl.BlockSpec((None, 1, D), lambda i: (glayer, 0, 0))] + [ANY_SPEC] * len(deps),
        out_specs=[row, row], out_shape=[SDS((M, D), F32), SDS((M, D), BF16)],
        name=name, compiler_params=_params("parallel"))(a, w3, res, g3, *deps)


def _mm_nt_norm(dy, w3, layer, h, g3, glayer, dh_in, name, deps=()):
    M, N = dy.shape
    D = w3.shape[1]
    tm = _row_tile(M, 2 * N + 14 * D, 2 * N * D)

    def body(dy_ref, w_ref, h_ref, g_ref, dhi_ref, *rest):
        dh_ref, dhb_ref, dg_ref = rest[-3:]
        dn = lax.dot_general(dy_ref[...], w_ref[...], (((1,), (1,)), ((), ())), preferred_element_type=F32)
        x = h_ref[...]
        r = lax.rsqrt(jnp.mean(x * x, axis=-1, keepdims=True) + EPS)
        xh = x * r
        dxh = dn * g_ref[...]
        dh = dhi_ref[...] + r * (dxh - xh * jnp.mean(dxh * xh, axis=-1, keepdims=True))
        dh_ref[...] = dh
        dhb_ref[...] = dh.astype(BF16)

        @pl.when(pl.program_id(0) == 0)
        def _():
            dg_ref[...] = jnp.zeros_like(dg_ref)

        dg_ref[...] += jnp.sum(dn * xh, axis=0, keepdims=True)

    row = pl.BlockSpec((tm, D), lambda i: (i, 0))
    return pl.pallas_call(
        body, grid=(M // tm,),
        in_specs=[pl.BlockSpec((tm, N), lambda i: (i, 0)), pl.BlockSpec((None, D, N), lambda i: (layer, 0, 0)), row,
                  pl.BlockSpec((None, 1, D), lambda i: (glayer, 0, 0)), row] + [ANY_SPEC] * len(deps),
        out_specs=[row, row, pl.BlockSpec((1, D), lambda i: (0, 0))],
        out_shape=[SDS((M, D), F32), SDS((M, D), BF16), SDS((1, D), F32)],
        name=name, compiler_params=_params("arbitrary"))(dy, w3, h, g3, dh_in, *deps)


def _fam_dims(kind, K, N):
    return (K // 2, N // N_CHIPS) if kind == "col" else (K // (2 * N_CHIPS), N)


def _mm_tn(x, dy, kind, name, square=False):
    M, K = x.shape
    N = dy.shape[1]
    nr, nc = _fam_dims(kind, K, N)

    def body(x_ref, dy_ref, o_ref):
        lhs = x_ref[...]
        res = lax.dot_general(lhs * lhs if square else lhs, dy_ref[...], (((0,), (0,)), ((), ())), preferred_element_type=F32)
        o_ref[...] = res.astype(BF16).reshape(o_ref.shape)

    if kind == "col":
        tn = _ctile(nc)
        ct = nc // tn
        grid = (N // tn,)
        in_specs = [pl.BlockSpec((M, K), lambda j: (0, 0)), pl.BlockSpec((M, tn), lambda j: (0, j))]
        out_spec = pl.BlockSpec((2, None, nr, tn), lambda j: (0, j // ct, 0, j % ct))
    else:
        grid = (N_CHIPS,)
        in_specs = [pl.BlockSpec((M, 2 * nr), lambda i: (0, i)), pl.BlockSpec((M, N), lambda i: (0, 0))]
        out_spec = pl.BlockSpec((2, None, nr, N), lambda i: (0, i, 0, 0))
    return pl.pallas_call(
        body, grid=grid, in_specs=in_specs, out_specs=out_spec, out_shape=SDS((2, N_CHIPS, nr, nc), BF16),
        name=name, compiler_params=_params("parallel"))(x, dy)


C_EVEN = 512


def _live(rows, base, total):
    r = _row_ids((rows, 1), base)
    return jnp.logical_and(r >= PAD, r < total).astype(F32)


def _conv_taps(win, w_ref, ls, acc, flip):
    for b in range(8):
        rb = win if b == 0 else pltpu.roll(win, 96 - b, 0)
        for a in range(5):
            o = 8 * a + b
            tap = (30 - o) if flip else (o - 2)
            if 0 <= tap < CONV_WIDTH:
                acc = acc + w_ref[pl.ds(tap, 1), ls] * rb[8 * a:8 * a + CHUNK]
    return acc


def _window_sum(win, levels, forward):
    s = win
    n = win.shape[0]
    for k in range(levels):
        step = 1 << k
        s = s + pltpu.roll(s, (n - step) if forward else step, 0)
    return s


def _pool_count(base, g):
    pos = _row_ids((CHUNK, 1), base) - PAD
    return jnp.clip(pos + 1, 1, POOL_WINDOWS[g]).astype(F32)


def _even_fwd(u, cw3, cb3, lg3, lb3, pw4, pb3, ps3, j, name):
    T = u.shape[0]
    C = C_EVEN
    tm = _tile(T, 320)
    nch = tm // CHUNK
    nblk = T // CHUNK

    def body(u_ref, up_ref, cw_ref, cb_ref, lg_ref, lb_ref, pw_ref, pb_ref, ps_ref, o_ref, a_s, p_s, yc_s):
        row0 = pl.program_id(0) * tm
        up = up_ref[...]
        lp = _live(CHUNK, row0 - CHUNK, T)
        a_s[0:CHUNK, :] = up[:, 0:C] * _sigmoid(up[:, C:2 * C]) * lp
        p_s[0:CHUNK, :] = up[:, 2 * C:3 * C] * lp

        def stage(c, _):
            rs = pl.multiple_of(c * CHUNK, CHUNK)
            lv = _live(CHUNK, row0 + rs, T)
            a_s[pl.ds(rs + CHUNK, CHUNK), :] = u_ref[pl.ds(rs, CHUNK), 0:C] * _sigmoid(u_ref[pl.ds(rs, CHUNK), C:2 * C]) * lv
            p_s[pl.ds(rs + CHUNK, CHUNK), :] = u_ref[pl.ds(rs, CHUNK), 2 * C:3 * C] * lv
            return 0

        lax.fori_loop(0, nch, stage, 0)

        def chunk(c, _):
            rs = pl.multiple_of(c * CHUNK, CHUNK)
            lv = _live(CHUNK, row0 + rs, T)
            for cb in range(4):
                ls = slice(cb * 128, (cb + 1) * 128)
                win = a_s[pl.ds(pl.multiple_of(rs + 32, 32), 96), ls]
                acc = jnp.broadcast_to(cb_ref[:, ls], (CHUNK, 128))
                yc_s[:, ls] = _conv_taps(win, cw_ref, ls, acc, False)
            y = yc_s[...]
            xc = y - jnp.mean(y, axis=-1, keepdims=True)
            yn = xc * lax.rsqrt(jnp.mean(xc * xc, axis=-1, keepdims=True) + EPS) * lg_ref[...] + lb_ref[...]
            o_ref[pl.ds(rs, CHUNK), 0:C] = (yn * _sigmoid(yn) * lv).astype(BF16)
            for g in range(4):
                ls = slice(g * 128, (g + 1) * 128)
                win = p_s[pl.ds(pl.multiple_of(rs + 48, 16), 80), ls]
                s = _window_sum(win, g + 1, False)
                d = s[16:80] / _pool_count(row0 + rs, g) - win[16:80]
                yv = jnp.dot(d.astype(BF16), pw_ref[g].astype(BF16), preferred_element_type=F32) + pb_ref[:, ls]
                o_ref[pl.ds(rs, CHUNK), C + g * 128:C + (g + 1) * 128] = (yv * ps_ref[:, ls] * lv).astype(BF16)
            return 0

        lax.fori_loop(0, nch, chunk, 0)

    vec = pl.BlockSpec((None, 1, C), lambda i: (j, 0, 0))
    return pl.pallas_call(
        body, grid=(T // tm,),
        in_specs=[pl.BlockSpec((tm, 3 * C), lambda i: (i, 0)),
                  pl.BlockSpec((CHUNK, 3 * C), lambda i: (jnp.maximum(i * nch - 1, 0), 0)),
                  pl.BlockSpec((None, CONV_ROWS, C), lambda i: (j, 0, 0)), vec, vec, vec,
                  pl.BlockSpec((None, 4, 128, 128), lambda i: (j, 0, 0, 0)), vec, vec],
        out_specs=pl.BlockSpec((tm, 2 * C), lambda i: (i, 0)),
        out_shape=SDS((T, 2 * C), BF16),
        scratch_shapes=[pltpu.VMEM((tm + CHUNK, C), F32), pltpu.VMEM((tm + CHUNK, C), F32), pltpu.VMEM((CHUNK, C), F32)],
        name=name, compiler_params=_params("parallel"))(u, u, cw3, cb3, lg3, lb3, pw4, pb3, ps3)


def _even_bwd(u, dy, cw3, cb3, lg3, lb3, pw4, pb3, ps3, j, name):
    T = u.shape[0]
    C = C_EVEN
    tm = _tile(T, 320)
    nch = tm // CHUNK
    nblk = T // CHUNK
    ntile = T // tm

    def body(u_ref, up_ref, un_ref, dy_ref, dyn_ref, cw_ref, cb_ref, lg_ref, lb_ref, pw_ref, pb_ref, ps_ref,
             du_ref, dcw_ref, dcb_ref, dlg_ref, dlb_ref, dpw_ref, dpb_ref, dps_ref,
             a_s, p_s, dy_s, yc_s, dyc_s, dd_s, ddc_s, dw_s):
        i = pl.program_id(0)
        row0 = i * tm

        @pl.when(i == 0)
        def _():
            for ref in (dcb_ref, dlg_ref, dlb_ref, dpw_ref, dpb_ref, dps_ref, dw_s):
                ref[...] = jnp.zeros_like(ref)

        up = up_ref[...]
        lp = _live(CHUNK, row0 - CHUNK, T)
        a_s[0:CHUNK, :] = up[:, 0:C] * _sigmoid(up[:, C:2 * C]) * lp
        p_s[0:CHUNK, :] = up[:, 2 * C:3 * C] * lp
        un = un_ref[...]
        ln_ = _live(CHUNK, row0 + tm, T)
        a_s[tm + CHUNK:tm + 2 * CHUNK, :] = un[:, 0:C] * _sigmoid(un[:, C:2 * C]) * ln_
        p_s[tm + CHUNK:tm + 2 * CHUNK, :] = un[:, 2 * C:3 * C] * ln_
        dy_s[tm:tm + CHUNK, :] = dyn_ref[...] * ln_
        dyc_s[tm + CHUNK:tm + CHUNK + 32, :] = jnp.zeros((32, C), F32)

        def stage(c, _):
            rs = pl.multiple_of(c * CHUNK, CHUNK)
            lv = _live(CHUNK, row0 + rs, T)
            a_s[pl.ds(rs + CHUNK, CHUNK), :] = u_ref[pl.ds(rs, CHUNK), 0:C] * _sigmoid(u_ref[pl.ds(rs, CHUNK), C:2 * C]) * lv
            p_s[pl.ds(rs + CHUNK, CHUNK), :] = u_ref[pl.ds(rs, CHUNK), 2 * C:3 * C] * lv
            dy_s[pl.ds(rs, CHUNK), :] = dy_ref[pl.ds(rs, CHUNK), :] * lv
            return 0

        lax.fori_loop(0, nch, stage, 0)

        def first(c, _):
            rs = pl.multiple_of(c * CHUNK, CHUNK)
            own = jnp.where(c < nch, 1.0, 0.0).astype(F32)
            for cb in range(4):
                ls = slice(cb * 128, (cb + 1) * 128)
                win = a_s[pl.ds(pl.multiple_of(rs + 32, 32), 96), ls]
                acc = jnp.broadcast_to(cb_ref[:, ls], (CHUNK, 128))
                yc_s[:, ls] = _conv_taps(win, cw_ref, ls, acc, False)
            y = yc_s[...]
            xc = y - jnp.mean(y, axis=-1, keepdims=True)
            rstd = lax.rsqrt(jnp.mean(xc * xc, axis=-1, keepdims=True) + EPS)
            xh = xc * rstd
            yn = xh * lg_ref[...] + lb_ref[...]
            sg = _sigmoid(yn)
            dyn = dy_s[pl.ds(rs, CHUNK), 0:C] * (sg * (1.0 + yn * (1.0 - sg)))
            dlg_ref[...] += jnp.sum(dyn * xh, axis=0, keepdims=True) * own
            dlb_ref[...] += jnp.sum(dyn, axis=0, keepdims=True) * own
            dxh = dyn * lg_ref[...]
            dyc = rstd * (dxh - jnp.mean(dxh, axis=-1, keepdims=True) - xh * jnp.mean(dxh * xh, axis=-1, keepdims=True))
            dyc_s[pl.ds(rs, CHUNK), :] = dyc
            dcb_ref[...] += jnp.sum(dyc, axis=0, keepdims=True) * own
            for g in range(4):
                ls = slice(g * 128, (g + 1) * 128)
                win = p_s[pl.ds(pl.multiple_of(rs + 48, 16), 80), ls]
                s = _window_sum(win, g + 1, False)
                cnt = _pool_count(row0 + rs, g)
                d = (s[16:80] / cnt - win[16:80]).astype(BF16)
                w = pw_ref[g].astype(BF16)
                pre = jnp.dot(d, w, preferred_element_type=F32) + pb_ref[:, ls]
                dyb = dy_s[pl.ds(rs, CHUNK), C + g * 128:C + (g + 1) * 128]
                dpre = dyb * ps_ref[:, ls]
                dps_ref[:, ls] += jnp.sum(dyb * pre, axis=0, keepdims=True) * own
                dpb_ref[:, ls] += jnp.sum(dpre, axis=0, keepdims=True) * own
                dpre_b = (dpre * own).astype(BF16)
                dpw_ref[g] += lax.dot_general(d, dpre_b, (((0,), (0,)), ((), ())), preferred_element_type=F32)
                dd = lax.dot_general(dpre.astype(BF16), w, (((1,), (1,)), ((), ())), preferred_element_type=F32)
                dd_s[pl.ds(rs, CHUNK), ls] = dd
                ddc_s[pl.ds(rs, CHUNK), ls] = dd / cnt
            return 0

        lax.fori_loop(0, nch + 1, first, 0)
        ddc_s[tm + CHUNK:tm + CHUNK + 16, :] = jnp.zeros((16, C), F32)

        def second(c, _):
            rs = pl.multiple_of(c * CHUNK, CHUNK)
            lv = _live(CHUNK, row0 + rs, T)
            for cb in range(4):
                ls = slice(cb * 128, (cb + 1) * 128)
                wd = dyc_s[pl.ds(rs, 96), ls]
                da = _conv_taps(wd, cw_ref, ls, jnp.zeros((CHUNK, 128), F32), True)
                wa = a_s[pl.ds(pl.multiple_of(rs + 32, 32), 96), ls]
                dyc = dyc_s[pl.ds(rs, CHUNK), ls]
                for b in range(8):
                    rb = wa if b == 0 else pltpu.roll(wa, 96 - b, 0)
                    for a in range(5):
                        tap = 8 * a + b - 2
                        if 0 <= tap < CONV_WIDTH:
                            prod = dyc * rb[8 * a:8 * a + CHUNK]
                            part = prod[0:8]
                            for q in range(1, 8):
                                part = part + prod[8 * q:8 * q + 8]
                            dw_s[8 * tap:8 * tap + 8, ls] += part
                val = u_ref[pl.ds(rs, CHUNK), ls]
                sg = _sigmoid(u_ref[pl.ds(rs, CHUNK), C + cb * 128:C + (cb + 1) * 128])
                du_ref[pl.ds(rs, CHUNK), ls] = (da * sg * lv).astype(BF16)
                du_ref[pl.ds(rs, CHUNK), C + cb * 128:C + (cb + 1) * 128] = (da * val * sg * (1.0 - sg) * lv).astype(BF16)
            for g in range(4):
                ls = slice(g * 128, (g + 1) * 128)
                z = _window_sum(ddc_s[pl.ds(rs, 80), ls], g + 1, True)
                dpin = (z[0:CHUNK] - dd_s[pl.ds(rs, CHUNK), ls]) * lv
                du_ref[pl.ds(rs, CHUNK), 2 * C + g * 128:2 * C + (g + 1) * 128] = dpin.astype(BF16)
            return 0

        lax.fori_loop(0, nch, second, 0)

        @pl.when(i == ntile - 1)
        def _():
            for tap in range(CONV_WIDTH):
                dcw_ref[tap:tap + 1, :] = jnp.sum(dw_s[8 * tap:8 * tap + 8, :], axis=0, keepdims=True)
            dcw_ref[CONV_WIDTH:CONV_ROWS, :] = jnp.zeros((CONV_ROWS - CONV_WIDTH, C), F32)

    vec = pl.BlockSpec((None, 1, C), lambda i: (j, 0, 0))
    ovec = pl.BlockSpec((1, C), lambda i: (0, 0))
    return pl.pallas_call(
        body, grid=(ntile,),
        in_specs=[pl.BlockSpec((tm, 3 * C), lambda i: (i, 0)),
                  pl.BlockSpec((CHUNK, 3 * C), lambda i: (jnp.maximum(i * nch - 1, 0), 0)),
                  pl.BlockSpec((CHUNK, 3 * C), lambda i: (jnp.minimum((i + 1) * nch, nblk - 1), 0)),
                  pl.BlockSpec((tm, 2 * C), lambda i: (i, 0)),
                  pl.BlockSpec((CHUNK, 2 * C), lambda i: (jnp.minimum((i + 1) * nch, nblk - 1), 0)),
                  pl.BlockSpec((None, CONV_ROWS, C), lambda i: (j, 0, 0)), vec, vec, vec,
                  pl.BlockSpec((None, 4, 128, 128), lambda i: (j, 0, 0, 0)), vec, vec],
        out_specs=[pl.BlockSpec((tm, 3 * C), lambda i: (i, 0)), pl.BlockSpec((CONV_ROWS, C), lambda i: (0, 0)),
                   ovec, ovec, ovec, pl.BlockSpec((4, 128, 128), lambda i: (0, 0, 0)), ovec, ovec],
        out_shape=[SDS((T, 3 * C), BF16), SDS((CONV_ROWS, C), F32), SDS((1, C), F32), SDS((1, C), F32), SDS((1, C), F32),
                   SDS((4, 128, 128), F32), SDS((1, C), F32), SDS((1, C), F32)],
        scratch_shapes=[pltpu.VMEM((tm + 2 * CHUNK, C), F32), pltpu.VMEM((tm + 2 * CHUNK, C), F32),
                        pltpu.VMEM((tm + CHUNK, 2 * C), F32), pltpu.VMEM((CHUNK, C), F32),
                        pltpu.VMEM((tm + CHUNK + 32, C), F32), pltpu.VMEM((tm + CHUNK, C), F32),
                        pltpu.VMEM((tm + CHUNK + 16, C), F32), pltpu.VMEM((8 * CONV_ROWS, C), F32)],
        name=name, compiler_params=_params("arbitrary"))(u, u, u, dy, dy, cw3, cb3, lg3, lb3, pw4, pb3, ps3)


HI = lax.Precision.HIGHEST


def _dot_nt(a, b):
    return lax.dot_general(a, b, (((1,), (1,)), ((), ())), preferred_element_type=F32)


def _dot_tn(a, b):
    return lax.dot_general(a, b, (((0,), (0,)), ((), ())), preferred_element_type=F32)


def _tri(lower):
    r = lax.broadcasted_iota(jnp.int32, (CHUNK, CHUNK), 0)
    c = lax.broadcasted_iota(jnp.int32, (CHUNK, CHUNK), 1)
    return jnp.where((c <= r) if lower else (c >= r), 1.0, 0.0).astype(F32)


def _hgrn_gates(u_ref, lb_ref, h, D, lv):
    ls = slice(h * HEAD_DIM, (h + 1) * HEAD_DIM)
    qraw = u_ref[:, ls]
    fraw = u_ref[:, D + h * HEAD_DIM:D + (h + 1) * HEAD_DIM]
    v = u_ref[:, 2 * D + h * HEAD_DIM:2 * D + (h + 1) * HEAD_DIM] * lv
    lbv = lb_ref[:, ls]
    sig = _sigmoid(fraw)
    forget = lbv + (1.0 - lbv) * sig
    logf = jnp.log(forget) * lv
    k = (1.0 - forget) * lv
    qsig = _sigmoid(qraw)
    q = qraw * qsig * lv
    return q, k, v, logf, (qraw, qsig, sig, forget, lbv)


def _sub_parts(q, k, b, b_s, I):
    rows = slice(SUB * I, SUB * (I + 1))
    rho = jnp.zeros((1, HEAD_DIM), F32) if I == 0 else b_s[SUB * I - 1:SUB * I, :]
    eI = jnp.exp(b[rows] - rho)
    EI = jnp.exp(jnp.minimum(rho - b, EXP_CAP))
    causal = (lax.broadcasted_iota(jnp.int32, (SUB, CHUNK), 1)
              <= lax.broadcasted_iota(jnp.int32, (SUB, CHUNK), 0) + SUB * I)
    return rows, q[rows] * eI, k * EI, eI, EI, causal


def _hgrn_fwd(u, lb3, layer, gn3, j, name):
    T = u.shape[0]
    D = u.shape[1] // 4
    H = D // HEAD_DIM
    NC = T // CHUNK

    def body(u_ref, lb_ref, gn_ref, y_ref, o_ref, sall_ref, st_s, b_s, lf_s, q_s, k_s):
        n = pl.program_id(0)

        @pl.when(n == 0)
        def _():
            st_s[...] = jnp.zeros_like(st_s)

        lv = _live(CHUNK, n * CHUNK, T)
        heads = range(H)
        cols = [slice(h * HEAD_DIM, (h + 1) * HEAD_DIM) for h in heads]
        vb = []
        for h in heads:
            q, k, v, logf, _ = _hgrn_gates(u_ref, lb_ref, h, D, lv)
            q_s[:, cols[h]] = q
            k_s[:, cols[h]] = k
            lf_s[:, cols[h]] = logf
            vb.append(v.astype(BF16))
        b_s[...] = jnp.dot(_tri(True), lf_s[...], precision=HI, preferred_element_type=F32)
        ops = []
        for h in heads:
            b_h = b_s.at[:, cols[h]]
            b = b_h[...]
            q = q_s[:, cols[h]]
            k = k_s[:, cols[h]]
            blast = b_h[CHUNK - 1:CHUNK, :]
            qh = (q * jnp.exp(b)).astype(BF16)
            kt = (k * jnp.exp(blast - b)).astype(BF16)
            subs = []
            for I in range(CHUNK // SUB):
                _, qI, KI, _, _, causal = _sub_parts(q, k, b, b_h, I)
                subs.append((qI.astype(BF16), KI.astype(BF16), causal))
            ops.append((qh, kt, jnp.exp(blast), subs))
        mm = []
        for h in heads:
            qh, kt, eblast, subs = ops[h]
            st = st_s[h]
            sall_ref[h] = st
            o_inter = _dot_nt(qh, st.astype(BF16))
            st_s[h] = st * eblast + _dot_tn(vb[h], kt)
            mm.append((o_inter, [_dot_nt(qI, KI) for qI, KI, _ in subs]))
        for h in heads:
            o_inter, ps = mm[h]
            p = jnp.concatenate([jnp.where(c, x, 0.0) for x, (_, _, c) in zip(ps, ops[h][3])], axis=0).astype(BF16)
            o = o_inter + jnp.dot(p, vb[h], preferred_element_type=F32)
            o_ref[:, cols[h]] = o
            graw = u_ref[:, 3 * D + h * HEAD_DIM:3 * D + (h + 1) * HEAD_DIM]
            r = lax.rsqrt(jnp.mean(o * o, axis=-1, keepdims=True) + EPS)
            y_ref[:, cols[h]] = (((o * r) * gn_ref[...]) * (graw * _sigmoid(graw))).astype(BF16)

    return pl.pallas_call(
        body, grid=(NC,),
        in_specs=[pl.BlockSpec((CHUNK, 4 * D), lambda n: (n, 0)),
                  pl.BlockSpec((None, 1, D), lambda n: (layer, 0, 0)),
                  pl.BlockSpec((None, 1, HEAD_DIM), lambda n: (j, 0, 0))],
        out_specs=[pl.BlockSpec((CHUNK, D), lambda n: (n, 0)), pl.BlockSpec((CHUNK, D), lambda n: (n, 0)),
                   pl.BlockSpec((None, H, HEAD_DIM, HEAD_DIM), lambda n: (n, 0, 0, 0))],
        out_shape=[SDS((T, D), BF16), SDS((T, D), F32), SDS((NC, H, HEAD_DIM, HEAD_DIM), F32)],
        scratch_shapes=[pltpu.VMEM((H, HEAD_DIM, HEAD_DIM), F32)] + [pltpu.VMEM((CHUNK, D), F32)] * 4,
        name=name, compiler_params=_params("arbitrary"))(u, lb3, gn3)


def _hgrn_bwd(u, o_raw, dy, sall, lb3, layer, gn3, j, name):
    T = u.shape[0]
    D = u.shape[1] // 4
    H = D // HEAD_DIM
    NC = T // CHUNK

    def body(u_ref, o_ref, dy_ref, sall_ref, lb_ref, gn_ref, du_ref, dlb_ref, dgn_ref, dst_s, b_s, lf_s, q_s, k_s, db_s, dk_s):
        step = pl.program_id(0)
        n = NC - 1 - step

        @pl.when(step == 0)
        def _():
            dst_s[...] = jnp.zeros_like(dst_s)
            dlb_ref[...] = jnp.zeros_like(dlb_ref)
            dgn_ref[...] = jnp.zeros_like(dgn_ref)

        lv = _live(CHUNK, n * CHUNK, T)
        last_row = (_row_ids((CHUNK, 1), 0) == CHUNK - 1).astype(F32)
        gn = gn_ref[...]
        heads = range(H)
        cols = [slice(h * HEAD_DIM, (h + 1) * HEAD_DIM) for h in heads]
        vb, dob = [], []
        dgn = jnp.zeros((1, HEAD_DIM), F32)
        for h in heads:
            q, k, v, logf, _ = _hgrn_gates(u_ref, lb_ref, h, D, lv)
            q_s[:, cols[h]] = q
            k_s[:, cols[h]] = k
            lf_s[:, cols[h]] = logf
            vb.append(v.astype(BF16))
            graw = u_ref[:, 3 * D + h * HEAD_DIM:3 * D + (h + 1) * HEAD_DIM]
            gsig = _sigmoid(graw)
            o = o_ref[:, cols[h]]
            r = lax.rsqrt(jnp.mean(o * o, axis=-1, keepdims=True) + EPS)
            xh = o * r
            dyv = dy_ref[:, cols[h]]
            dsg = dyv * (graw * gsig)
            dgn = dgn + jnp.sum(dsg * xh, axis=0, keepdims=True)
            dxh = dsg * gn
            do = r * (dxh - xh * jnp.mean(dxh * xh, axis=-1, keepdims=True))
            dob.append(do.astype(BF16))
            dgraw = dyv * xh * gn * (gsig * (1.0 + graw * (1.0 - gsig)))
            du_ref[:, 3 * D + h * HEAD_DIM:3 * D + (h + 1) * HEAD_DIM] = (dgraw * lv).astype(BF16)
        dgn_ref[...] += dgn
        b_s[...] = jnp.dot(_tri(True), lf_s[...], precision=HI, preferred_element_type=F32)
        ops = []
        for h in heads:
            b_h = b_s.at[:, cols[h]]
            b = b_h[...]
            q = q_s[:, cols[h]]
            k = k_s[:, cols[h]]
            blast = b_h[CHUNK - 1:CHUNK, :]
            eb = jnp.exp(b)
            ekb = jnp.exp(blast - b)
            subs = []
            for I in range(CHUNK // SUB):
                rows, qI, KI, eI, EI, causal = _sub_parts(q, k, b, b_h, I)
                subs.append((rows, qI.astype(BF16), KI.astype(BF16), eI, EI, causal))
            ops.append((eb, ekb, jnp.exp(blast), (q * eb).astype(BF16), (k * ekb).astype(BF16), subs))
        mm = []
        for h in heads:
            eb, ekb, eblast, qhb, ktb, subs = ops[h]
            st = sall_ref[h]
            dst = dst_s[h]
            dstb = dst.astype(BF16)
            dv = _dot_nt(ktb, dstb)
            dqh = jnp.dot(dob[h], st.astype(BF16), preferred_element_type=F32)
            dkt = jnp.dot(vb[h], dstb, preferred_element_type=F32)
            dblast = jnp.sum(dst * st, axis=0, keepdims=True) * eblast
            dst_s[h] = dst * eblast + _dot_tn(dob[h], qhb)
            dp_full = _dot_nt(dob[h], vb[h])
            ps = [_dot_nt(qIb, KIb) for _, qIb, KIb, _, _, _ in subs]
            mm.append((dv, dqh, dkt, dblast, dp_full, ps))
        for h in heads:
            eb, ekb, eblast, qhb, ktb, subs = ops[h]
            dv, dqh, dkt, dblast, dp_full, ps = mm[h]
            p = jnp.concatenate([jnp.where(sub[5], x, 0.0) for x, sub in zip(ps, subs)], axis=0).astype(BF16)
            dv = dv + _dot_tn(p, dob[h])
            du_ref[:, 2 * D + h * HEAD_DIM:2 * D + (h + 1) * HEAD_DIM] = (dv * lv).astype(BF16)
            dq = dqh * eb
            db = dqh * qhb.astype(F32)
            tmp = dkt * ktb.astype(F32)
            dk = dkt * ekb
            db = db - tmp
            dblast = dblast + jnp.sum(tmp, axis=0, keepdims=True)
            dq_parts, db_parts = [], []
            for rows, qIb, KIb, eI, EI, causal in subs:
                dp = jnp.where(causal, dp_full[rows], 0.0).astype(BF16)
                dqI = jnp.dot(dp, KIb, preferred_element_type=F32)
                dKI = _dot_tn(dp, qIb)
                dq_parts.append(dqI * eI)
                db_parts.append(dqI * qIb.astype(F32))
                dk = dk + dKI * EI
                db = db - dKI * KIb.astype(F32)
            dq = dq + jnp.concatenate(dq_parts, axis=0)
            db_s[:, cols[h]] = db + jnp.concatenate(db_parts, axis=0) + last_row * dblast
            dk_s[:, cols[h]] = dk
            qraw = u_ref[:, cols[h]]
            qsig = _sigmoid(qraw)
            du_ref[:, cols[h]] = (dq * (qsig * (1.0 + qraw * (1.0 - qsig))) * lv).astype(BF16)
        lf_s[...] = jnp.dot(_tri(False), db_s[...], precision=HI, preferred_element_type=F32)
        for h in heads:
            fraw = u_ref[:, D + h * HEAD_DIM:D + (h + 1) * HEAD_DIM]
            lbv = lb_ref[:, cols[h]]
            sig = _sigmoid(fraw)
            forget = lbv + (1.0 - lbv) * sig
            dforget = (lf_s[:, cols[h]] / forget - dk_s[:, cols[h]]) * lv
            dlb_ref[:, cols[h]] += jnp.sum(dforget * (1.0 - sig), axis=0, keepdims=True)
            du_ref[:, D + h * HEAD_DIM:D + (h + 1) * HEAD_DIM] = (dforget * (1.0 - lbv) * sig * (1.0 - sig)).astype(BF16)

    rev = lambda s: (NC - 1 - s, 0)
    return pl.pallas_call(
        body, grid=(NC,),
        in_specs=[pl.BlockSpec((CHUNK, 4 * D), rev), pl.BlockSpec((CHUNK, D), rev), pl.BlockSpec((CHUNK, D), rev),
                  pl.BlockSpec((None, H, HEAD_DIM, HEAD_DIM), lambda s: (NC - 1 - s, 0, 0, 0)),
                  pl.BlockSpec((None, 1, D), lambda s: (layer, 0, 0)),
                  pl.BlockSpec((None, 1, HEAD_DIM), lambda s: (j, 0, 0))],
        out_specs=[pl.BlockSpec((CHUNK, 4 * D), rev), pl.BlockSpec((1, D), lambda s: (0, 0)),
                   pl.BlockSpec((1, HEAD_DIM), lambda s: (0, 0))],
        out_shape=[SDS((T, 4 * D), BF16), SDS((1, D), F32), SDS((1, HEAD_DIM), F32)],
        scratch_shapes=[pltpu.VMEM((H, HEAD_DIM, HEAD_DIM), F32)] + [pltpu.VMEM((CHUNK, D), F32)] * 6,
        name=name, compiler_params=_params("arbitrary"))(u, o_raw, dy, sall, lb3, gn3)


def _softmax_layers(p_ref, n_layers):
    rows = [p_ref[l:l + 1, :] for l in range(n_layers)]
    m = functools.reduce(jnp.maximum, rows)
    e = [jnp.exp(x - m) for x in rows]
    tot = functools.reduce(lambda a, b: a + b, e)
    return [x / tot for x in e]


def _lb_fwd(p):
    n_layers, D = p.shape

    def body(p_ref, o_ref):
        s = _softmax_layers(p_ref, n_layers)
        acc = jnp.zeros((1, D), F32)
        o_ref[0:1, :] = acc
        for l in range(1, n_layers):
            acc = acc + s[l]
            o_ref[l:l + 1, :] = acc

    return pl.pallas_call(body, out_shape=SDS(p.shape, F32), name="lb_fwd")(p)


def _lb_bwd(p, dlb):
    n_layers, D = p.shape

    def body(p_ref, d_ref, o_ref):
        s = _softmax_layers(p_ref, n_layers)
        ds = [jnp.zeros((1, D), F32)] * n_layers
        acc = jnp.zeros((1, D), F32)
        for l in range(n_layers - 1, 0, -1):
            acc = acc + d_ref[l:l + 1, :]
            ds[l] = acc
        dot = functools.reduce(lambda a, b: a + b, [s[l] * ds[l] for l in range(n_layers)])
        for l in range(n_layers):
            o_ref[l:l + 1, :] = s[l] * (ds[l] - dot)

    return pl.pallas_call(body, out_shape=SDS(p.shape, F32), name="lb_bwd")(p, dlb)


def _adamw(w, g, m, v, name):
    R, C = w.shape
    tr = _tile(R, 256, 8) if R % 8 == 0 else R

    def body(w_ref, g_ref, m_ref, v_ref, d_ref, mo_ref, vo_ref):
        g_ = g_ref[...]
        m_ = ADAM_B1 * m_ref[...] + (1.0 - ADAM_B1) * g_
        v_ = ADAM_B2 * v_ref[...] + (1.0 - ADAM_B2) * (g_ * g_)
        mh = m_ / (1.0 - ADAM_B1 ** ADAM_STEP)
        vh = v_ / (1.0 - ADAM_B2 ** ADAM_STEP)
        d_ref[...] = -ADAM_LR * (mh / (jnp.sqrt(vh) + ADAM_EPS) + ADAM_WD * w_ref[...])
        mo_ref[...] = m_
        vo_ref[...] = v_

    blk = pl.BlockSpec((tr, C), lambda i: (i, 0))
    return pl.pallas_call(
        body, grid=(R // tr,), in_specs=[blk] * 4, out_specs=[blk] * 3, out_shape=[SDS((R, C), F32)] * 3,
        name=name, compiler_params=_params("parallel"))(w, g, m, v)


def _adamw_layer(w3, m3, v3, g2, layer, outs, name):
    L, R, C = w3.shape
    tr = _tile(R, 256, 8)
    if outs is None:
        outs = tuple(lax.empty(w3.shape, F32) for _ in range(4))

    def body(w_ref, m_ref, v_ref, g_ref, a0, a1, a2, a3, go_ref, d_ref, mo_ref, vo_ref):
        del a0, a1, a2, a3
        g_ = g_ref[...]
        m_ = ADAM_B1 * m_ref[...] + (1.0 - ADAM_B1) * g_
        v_ = ADAM_B2 * v_ref[...] + (1.0 - ADAM_B2) * (g_ * g_)
        mh = m_ / (1.0 - ADAM_B1 ** ADAM_STEP)
        vh = v_ / (1.0 - ADAM_B2 ** ADAM_STEP)
        go_ref[...] = g_
        d_ref[...] = -ADAM_LR * (mh / (jnp.sqrt(vh) + ADAM_EPS) + ADAM_WD * w_ref[...])
        mo_ref[...] = m_
        vo_ref[...] = v_

    lay = pl.BlockSpec((None, tr, C), lambda i: (layer, i, 0))
    return pl.pallas_call(
        body, grid=(R // tr,), in_specs=[lay] * 3 + [pl.BlockSpec((tr, C), lambda i: (i, 0))] + [ANY_SPEC] * 4,
        out_specs=[lay] * 4, out_shape=[SDS(w3.shape, F32)] * 4, input_output_aliases={4: 0, 5: 1, 6: 2, 7: 3},
        name=name, compiler_params=_params("parallel"))(w3, m3, v3, g2, *outs)


SEM_SPEC = pl.BlockSpec(memory_space=pltpu.SEMAPHORE)
HBM_SPEC = pl.BlockSpec(memory_space=pltpu.HBM)
EFFECT = pltpu.SideEffectType.DATAFLOW_SIDE_EFFECTING
N_DEV = 2 * N_CHIPS


def _position():
    x, y, c = lax.axis_index("x"), lax.axis_index("y"), lax.axis_index("c")
    chips = [(1 - x, y), (x, 1 - y), (1 - x, 1 - y)]
    return x, y, c, chips


def _split_start(name, plan, bufs, n_sems, deps=(), earlier=None):
    n = len(bufs)
    held = () if earlier is None else tuple(earlier[1:])

    def body(*refs):
        first_out = n + len(held) + len(deps)
        if earlier is not None:
            sends, recvs = earlier[0](refs[:n], refs[n], refs[n + 1])
            for kw in sends:
                pltpu.make_async_remote_copy(**kw).wait_send()
            for kw in recvs:
                pltpu.make_async_remote_copy(**kw).wait_recv()
        sends, _ = plan(refs[:n], refs[first_out], refs[first_out + 1])
        for kw in sends:
            pltpu.make_async_remote_copy(**kw).start()
        refs[-1][...] = jnp.zeros_like(refs[-1])

    out = pl.pallas_call(
        body, name=name,
        out_shape=(pltpu.SemaphoreType.DMA((n_sems,)), pltpu.SemaphoreType.DMA((n_sems,)),
                   *[pltpu.HBM(b.shape, b.dtype) for b in bufs], SDS((8, 128), F32)),
        in_specs=[HBM_SPEC] * n + [SEM_SPEC] * len(held) + [ANY_SPEC] * len(deps),
        out_specs=(SEM_SPEC, SEM_SPEC, *[HBM_SPEC] * n, pl.BlockSpec(memory_space=pltpu.VMEM)),
        input_output_aliases={i: 2 + i for i in range(n)},
        compiler_params=pltpu.CompilerParams(has_side_effects=EFFECT),
    )(*[pltpu.with_memory_space_constraint(b, pltpu.HBM) for b in bufs], *held, *deps)
    return out[0], out[1], list(out[2:2 + n]), out[-1]


def _split_wait(name, plan, send_sems, recv_sems, bufs, after=()):
    n = len(bufs)

    def body(*refs):
        sends, recvs = plan(refs[:n], refs[n], refs[n + 1])
        for kw in sends:
            pltpu.make_async_remote_copy(**kw).wait_send()
        for kw in recvs:
            pltpu.make_async_remote_copy(**kw).wait_recv()

    out = pl.pallas_call(
        body, name=name, out_shape=tuple(pltpu.HBM(b.shape, b.dtype) for b in bufs),
        in_specs=[HBM_SPEC] * n + [SEM_SPEC, SEM_SPEC] + [ANY_SPEC] * len(after),
        out_specs=tuple([HBM_SPEC] * n), input_output_aliases={i: i for i in range(n)},
        compiler_params=pltpu.CompilerParams(has_side_effects=EFFECT),
    )(*bufs, send_sems, recv_sems, *after)
    return list(out)


def _region(kind, ref, chip, half):
    K, N = ref.shape
    if kind == "col":
        return ref.at[pl.ds(half * (K // 2), K // 2), pl.ds(chip * (N // N_CHIPS), N // N_CHIPS)]
    rows = K // (2 * N_CHIPS)
    return ref.at[pl.ds((2 * chip + half) * rows, rows), :]


def _gather_plan(kinds, over_chips):
    def plan(refs, send_sems, recv_sems):
        x, y, c, chips = _position()
        sends, recvs = [], []
        for f, (ref, kind) in enumerate(zip(refs, kinds)):
            for k, chip in enumerate(chips):
                theirs = 2 * chip[0] + chip[1]
                sem = dict(send_sem=send_sems.at[3 * f + k], recv_sem=recv_sems.at[3 * f + k], device_id_type=MESH)
                if over_chips:
                    out, back, to = _region(kind, ref, 2 * x + y, c), _region(kind, ref, theirs, c), (*chip, c)
                else:
                    out, back, to = _region(kind, ref, theirs, c), _region(kind, ref, theirs, 1 - c), (x, y, 1 - c)
                sends.append(dict(src_ref=out, dst_ref=out, device_id=to, **sem))
                recvs.append(dict(src_ref=back, dst_ref=back, device_id=to, **sem))
        return sends, recvs
    return plan


def _reduce_plan(refs, send_sems, recv_sems):
    x, y, c, _ = _position()
    me = 4 * x + 2 * y + c
    sends, recvs = [], []
    for f in range(len(refs) // 2):
        acc, land = refs[2 * f], refs[2 * f + 1]
        for d in range(1, N_DEV):
            t = (me + d) % N_DEV
            to = dict(device_id=(t // 4, (t // 2) % 2, t % 2), device_id_type=MESH)
            slot = N_DEV - 1 - d
            sends.append(dict(src_ref=acc.at[t % 2, t // 2], dst_ref=land.at[slot], send_sem=send_sems.at[7 * f + d - 1],
                              recv_sem=recv_sems.at[7 * f + slot], **to))
            recvs.append(dict(src_ref=land.at[d - 1], dst_ref=land.at[d - 1], send_sem=send_sems.at[7 * f + d - 1],
                              recv_sem=recv_sems.at[7 * f + d - 1], **to))
    return sends, recvs


def _swap_plan(refs, send_sems, recv_sems):
    x, y, c, _ = _position()
    sends, recvs = [], []
    for f, g in enumerate(refs):
        sem = dict(send_sem=send_sems.at[f], recv_sem=recv_sems.at[f], device_id=(x, y, 1 - c), device_id_type=MESH)
        sends.append(dict(src_ref=g.at[c], dst_ref=g.at[c], **sem))
        recvs.append(dict(src_ref=g.at[1 - c], dst_ref=g.at[1 - c], **sem))
    return sends, recvs


def _sum_pieces(ids2, acc, land, name):
    _, _, nr, nc = acc.shape
    tr = _tile(nr, 256, 16)

    def body(ids_ref, own_ref, land_ref, o_ref):
        del ids_ref
        s = own_ref[...].astype(F32)
        for k in range(N_DEV - 1):
            s = s + land_ref[k].astype(F32)
        o_ref[...] = s

    return pl.pallas_call(
        body,
        grid_spec=pltpu.PrefetchScalarGridSpec(
            num_scalar_prefetch=1, grid=(nr // tr,),
            in_specs=[pl.BlockSpec((None, None, tr, nc), lambda i, ids: (ids[0], ids[1], i, 0)),
                      pl.BlockSpec((N_DEV - 1, tr, nc), lambda i, ids: (0, i, 0))],
            out_specs=pl.BlockSpec((None, tr, nc), lambda i, ids: (ids[0], i, 0))),
        out_shape=SDS((2, nr, nc), F32), name=name, compiler_params=_params("parallel"))(ids2, acc, land)


def _small_plan(refs, send_sems, recv_sems):
    x, y, c, _ = _position()
    me = 4 * x + 2 * y + c
    own, land = refs
    sends, recvs = [], []
    for d in range(1, N_DEV):
        t = (me + d) % N_DEV
        to = dict(device_id=(t // 4, (t // 2) % 2, t % 2), device_id_type=MESH)
        sends.append(dict(src_ref=own, dst_ref=land.at[me], send_sem=send_sems.at[d - 1],
                          recv_sem=recv_sems.at[N_DEV - 1 - d], **to))
        recvs.append(dict(src_ref=land.at[t], dst_ref=land.at[t], send_sem=send_sems.at[d - 1],
                          recv_sem=recv_sems.at[d - 1], **to))
    return sends, recvs


def _sum_blocks(me1, own, land):
    def body(me_ref, own_ref, land_ref, o_ref):
        acc = None
        for d in range(N_DEV):
            term = jnp.where(me_ref[0] == d, own_ref[...], land_ref[d])
            acc = term if acc is None else acc + term
        o_ref[...] = acc

    return pl.pallas_call(
        body,
        grid_spec=pltpu.PrefetchScalarGridSpec(
            num_scalar_prefetch=1, grid=(1,),
            in_specs=[pl.BlockSpec(own.shape, lambda i, me: (0, 0)), pl.BlockSpec(land.shape, lambda i, me: (0, 0, 0))],
            out_specs=pl.BlockSpec(own.shape, lambda i, me: (0, 0))),
        out_shape=SDS(own.shape, F32), name="sum_small", compiler_params=_params("arbitrary"))(me1, own, land)


BIG = {"ev_w_in": "col", "ev_w_out": "row", "od_w_in": "col", "od_w_out": "row", "mlp_w1": "col", "mlp_w2": "row"}
WEIGHTS = ("meta_tokens", "mix_norm_g", "mlp_norm_g", "final_norm_g", "ev_w_in", "ev_conv_w", "ev_conv_b", "ev_ln_g",
           "ev_ln_b", "ev_pool_w", "ev_pool_b", "ev_pool_scale", "ev_w_out", "od_w_in", "od_gnorm_g", "od_w_out",
           "lb_param", "mlp_w1", "mlp_w2")
PACK_UNIT = 1024


def _mixer_names(layer):
    return ("ev_w_in", "ev_w_out") if layer % 2 == 0 else ("od_w_in", "od_w_out")


def _pack(arrays):
    flat = []
    for a in arrays:
        a = a.reshape(-1)
        flat.append(jnp.pad(a, (0, (-a.shape[0]) % PACK_UNIT)))
    return jnp.concatenate(flat).reshape(-1, 128)


def _unpack(packed, shapes):
    flat = packed.reshape(-1)
    out, off = [], 0
    for s in shapes:
        size = 1
        for d in s:
            size *= d
        out.append(flat[off:off + size].reshape(s))
        off += size + (-size) % PACK_UNIT
    return out


def _local_step(x2, target, P, weights, boundary, first_deps=()):
    D = x2.shape[1]
    n_layers = P["mix_norm_g"].shape[0]
    h = jnp.concatenate([jnp.zeros((PAD, D), F32), P["meta_full"], x2], axis=0)
    mix_g = P["mix_norm_g"].reshape(n_layers, 1, D)
    mlp_g = P["mlp_norm_g"].reshape(n_layers, 1, D)
    vec = lambda a: a.reshape(a.shape[0], 1, -1)
    cb3, lg3, lnb3, ps3 = vec(P["ev_conv_b"]), vec(P["ev_ln_g"]), vec(P["ev_ln_b"]), vec(P["ev_pool_scale"])
    pb3 = vec(P["ev_pool_b"])
    gn3 = vec(P["od_gnorm_g"])
    lb_all = _lb_fwd(P["lb_param"])
    lb3 = lb_all.reshape(n_layers, 1, D)
    even = (cb3, lg3, lnb3, P["ev_pool_w"], pb3, ps3)

    saved = []
    deps = tuple(first_deps)
    for layer in range(n_layers):
        j = layer // 2
        w_in, w_out = _mixer_names(layer)
        W = {}
        s = {"h": h, "W": W}
        s["n"] = _rms_fwd(h, mix_g, layer, "mix_norm_0", deps=deps) if layer == 0 else n_next
        deps = ()
        W[w_in], held = weights(layer, w_in, (s["n"],))
        s["u"] = _mm_nn(s["n"], W[w_in], 0, f"mix_in_{layer}", deps=held)
        if layer % 2 == 0:
            s["y"] = _even_fwd(s["u"], P["conv_w_full"], *even, j, f"even_fwd_{layer}")
        else:
            s["y"], s["o"], s["sall"] = _hgrn_fwd(s["u"], lb3, layer, gn3, j, f"hgrn_fwd_{layer}")
        W[w_out], held = weights(layer, w_out, (s["y"],))
        h, s["n2"] = _mm_nn_norm(s["y"], W[w_out], 0, h, mlp_g, layer, f"mix_out_{layer}", deps=held)
        s["h1"] = h
        W["mlp_w1"], held = weights(layer, "mlp_w1", (s["n2"],))
        s["relu"] = _mm_nn(s["n2"], W["mlp_w1"], 0, f"mlp_up_{layer}", relu=True, deps=held)
        W["mlp_w2"], held = weights(layer, "mlp_w2", (s["relu"],))
        if layer + 1 < n_layers:
            h, n_next = _mm_nn_norm(s["relu"], W["mlp_w2"], 0, h, mix_g, layer + 1, f"mlp_down_{layer}", square=True, deps=held)
        else:
            h = _mm_nn(s["relu"], W["mlp_w2"], 0, f"mlp_down_{layer}", res=h, square=True, deps=held)
        saved.append(s)

    dh, dhb, dg_final, loss = _final(h, P["final_norm_g"].reshape(1, D), target)

    small = {"final_norm_g": dg_final}
    per_layer = {k: [None] * n_layers for k in ("mix_norm_g", "mlp_norm_g", "lb")}
    per_pair = {k: [None] * (n_layers // 2) for k in
                ("ev_conv_w", "ev_conv_b", "ev_ln_g", "ev_ln_b", "ev_pool_w", "ev_pool_b", "ev_pool_scale", "od_gnorm_g")}
    for layer in reversed(range(n_layers)):
        j = layer // 2
        s = saved[layer]
        W = s["W"]
        w_in, w_out = _mixer_names(layer)
        dz = _mm_nt(dhb, W["mlp_w2"], 0, f"d_act_{layer}", relu=s["relu"], deps=deps)
        dw2 = _mm_tn(s["relu"], dhb, "row", f"dw2_{layer}", square=True)
        dw1 = _mm_tn(s["n2"], dz, "col", f"dw1_{layer}")
        dh, dhb, per_layer["mlp_norm_g"][layer] = _mm_nt_norm(dz, W["mlp_w1"], 0, s["h1"], mlp_g, layer, dh, f"d_n2_{layer}")
        deps = boundary(f"mlp{layer}", {("mlp_w1", layer): dw1, ("mlp_w2", layer): dw2}, (dhb, dw1, dw2))
        dy = _mm_nt(dhb, W[w_out], 0, f"d_y_{layer}", deps=deps)
        dwout = _mm_tn(s["y"], dhb, "row", f"dwout_{layer}")
        if layer % 2 == 0:
            du, dcw, dcb, dlg, dlnb, dpw, dpb, dps = _even_bwd(s["u"], dy, P["conv_w_full"], *even, j, f"even_bwd_{layer}")
            for k, val in (("ev_conv_w", dcw), ("ev_conv_b", dcb), ("ev_ln_g", dlg), ("ev_ln_b", dlnb),
                           ("ev_pool_w", dpw), ("ev_pool_b", dpb), ("ev_pool_scale", dps)):
                per_pair[k][j] = val
        else:
            du, per_layer["lb"][layer], per_pair["od_gnorm_g"][j] = _hgrn_bwd(
                s["u"], s["o"], dy, s["sall"], lb3, layer, gn3, j, f"hgrn_bwd_{layer}")
        dwin = _mm_tn(s["n"], du, "col", f"dwin_{layer}")
        deps = boundary(f"mix{layer}", {(w_in, j): dwin, (w_out, j): dwout}, (du, dwin, dwout))
        dh, dhb, per_layer["mix_norm_g"][layer] = _mm_nt_norm(du, W[w_in], 0, s["h"], mix_g, layer, dh, f"d_n_{layer}", deps=deps)
        deps = ()

    small["mix_norm_g"] = jnp.concatenate(per_layer["mix_norm_g"], axis=0)
    small["mlp_norm_g"] = jnp.concatenate(per_layer["mlp_norm_g"], axis=0)
    dlb_all = jnp.concatenate([jnp.zeros((1, D), F32) if g is None else g for g in per_layer["lb"]], axis=0)
    small["lb_param"] = _lb_bwd(P["lb_param"], dlb_all)
    for k, vals in per_pair.items():
        small[k] = jnp.stack(vals, axis=0)
    small["meta_tokens"] = dh[PAD:LEAD]
    return loss, dh, small


def kernel(x, meta_tokens, mix_norm_g, mlp_norm_g, final_norm_g, ev_w_in, ev_conv_w, ev_conv_b, ev_ln_g, ev_ln_b, ev_pool_w, ev_pool_b, ev_pool_scale, ev_w_out, od_w_in, od_gnorm_g, od_w_out, lb_param, mlp_w1, mlp_w2, loss_target, m_meta_tokens, m_mix_norm_g, m_mlp_norm_g, m_final_norm_g, m_ev_w_in, m_ev_conv_w, m_ev_conv_b, m_ev_ln_g, m_ev_ln_b, m_ev_pool_w, m_ev_pool_b, m_ev_pool_scale, m_ev_w_out, m_od_w_in, m_od_gnorm_g, m_od_w_out, m_lb_param, m_mlp_w1, m_mlp_w2, v_meta_tokens, v_mix_norm_g, v_mlp_norm_g, v_final_norm_g, v_ev_w_in, v_ev_conv_w, v_ev_conv_b, v_ev_ln_g, v_ev_ln_b, v_ev_pool_w, v_ev_pool_b, v_ev_pool_scale, v_ev_w_out, v_od_w_in, v_od_gnorm_g, v_od_w_out, v_lb_param, v_mlp_w1, v_mlp_w2):
    given = dict(locals())
    w = {n: given[n] for n in WEIGHTS}
    m = {n: given["m_" + n] for n in WEIGHTS}
    v = {n: given["v_" + n] for n in WEIGHTS}
    n_layers = mix_norm_g.shape[0]
    core = lax.axis_index("c").astype(jnp.int32)
    chip = (2 * lax.axis_index("x") + lax.axis_index("y")).astype(jnp.int32)
    chip1 = chip.reshape(1)
    ids2 = jnp.stack([core, chip])

    conv_pad = jnp.pad(ev_conv_w, ((0, 0), (0, CONV_ROWS - CONV_WIDTH), (0, 0)))
    stages = [[(0, n)] for n in (*_mixer_names(0), "mlp_w1", "mlp_w2")]
    stages += [[(layer, n) for n in (*_mixer_names(layer), "mlp_w1", "mlp_w2")] for layer in range(1, n_layers)]
    gathers, where, token = [], {}, ()
    for k, stage in enumerate(stages):
        index = [layer if n.startswith("mlp") else layer // 2 for layer, n in stage]
        kinds = [BIG[n] for _, n in stage]
        bufs = [_cast_place(w[n], i, BIG[n], chip1, BF16, f"place_{n}_{i}") for (_, n), i in zip(stage, index)]
        if k == 0:
            bufs.append(_cast_place(meta_tokens[None], 0, "col", chip1, F32, "place_meta"))
            bufs.append(_cast_place(conv_pad.reshape(1, -1, conv_pad.shape[2]), 0, "col", chip1, F32, "place_conv_w"))
            kinds += ["col", "col"]
        plan = _gather_plan(kinds, True)
        ss, rs, bufs, tok = _split_start(f"gather_start_{k}", plan, bufs, 3 * len(bufs), deps=token)
        token = (tok,)
        gathers.append((kinds, plan, ss, rs, bufs))
        where.update({key: (k, f) for f, key in enumerate(stage)})

    landed, passed, held = {}, {}, []

    def hand_on(k, deps):
        if k not in passed:
            kinds, plan, ss, rs, bufs = gathers[k]
            to_sibling = _gather_plan(kinds, False)
            ss, rs, bufs, tok = _split_start(f"gather_pass_{k}", to_sibling, bufs, 3 * len(bufs), deps=deps, earlier=(plan, ss, rs))
            passed[k] = (to_sibling, ss, rs, bufs)
            held.append(tok)

    def arrived(k, after):
        if k not in landed:
            hand_on(k, after)
            landed[k] = _split_wait(f"gather_wait_{k}", *passed[k], after)
        return landed[k]

    def weights(layer, name, after):
        k, f = where[(layer, name)]
        full = arrived(k, after)[f][None]
        if name == "mlp_w2" and layer + 1 < n_layers:
            hand_on(where[(layer + 1, "mlp_w2")][0], after)
        tokens = tuple(held)
        held.clear()
        return full, tokens

    first = arrived(0, token)
    P = {n: w[n] for n in ("mix_norm_g", "mlp_norm_g", "final_norm_g", "ev_conv_b", "ev_ln_g", "ev_ln_b", "ev_pool_w",
                           "ev_pool_b", "ev_pool_scale", "od_gnorm_g", "lb_param")}
    P["meta_full"] = first[1]
    P["conv_w_full"] = first[2].reshape(ev_conv_w.shape[0], CONV_ROWS, -1)

    pending, outs = [], {n: None for n in BIG}

    def advance(after):
        tokens, still = [], []
        for st in pending:
            if st["phase"] == 1:
                bufs = _split_wait(f"reduce_wait_{st['tag']}", _reduce_plan, st["ss"], st["rs"], st["bufs"], after)
                halves = [_sum_pieces(ids2, bufs[2 * f], bufs[2 * f + 1], f"sum_{st['tag']}_{f}") for f in range(len(bufs) // 2)]
                ss, rs, halves, tok = _split_start(f"swap_start_{st['tag']}", _swap_plan, halves, len(halves))
                tokens.append(tok)
                still.append(dict(st, phase=2, ss=ss, rs=rs, bufs=halves))
            else:
                grads = _split_wait(f"swap_wait_{st['tag']}", _swap_plan, st["ss"], st["rs"], st["bufs"], after)
                for (n, i), g in zip(st["keys"], grads):
                    outs[n] = _adamw_layer(w[n], m[n], v[n], g.reshape(w[n].shape[1:]), i, outs[n], f"adamw_{n}_{i}")
        pending[:] = still
        return tokens

    def boundary(tag, grads, after):
        tokens = advance(after)
        bufs = []
        for acc in grads.values():
            bufs += [acc, lax.empty((N_DEV - 1,) + acc.shape[2:], BF16)]
        ss, rs, bufs, tok = _split_start(f"reduce_start_{tag}", _reduce_plan, bufs, 7 * len(grads))
        pending.append(dict(phase=1, tag=tag, keys=list(grads), ss=ss, rs=rs, bufs=bufs))
        return tuple(tokens + [tok])

    loss, dh, small = _local_step(x[0], loss_target[0], P, weights, boundary, first_deps=token)

    order = [n for n in WEIGHTS if n not in BIG]
    block = _pack([small[n] for n in order] + [loss])
    ss, rs, bufs, tok = _split_start("small_start", _small_plan, [block, lax.empty((N_DEV,) + block.shape, F32)], N_DEV - 1)
    advance((tok,))
    advance((tok,))
    block, land = _split_wait("small_wait", _small_plan, ss, rs, bufs, tuple(outs[n][0] for n in BIG))
    packed = _sum_blocks((4 * lax.axis_index("x") + 2 * lax.axis_index("y") + lax.axis_index("c")).astype(jnp.int32).reshape(1), block, land)
    total = _unpack(packed, [small[n].shape for n in order] + [loss.shape])
    loss_sum = total[-1][0, 0]
    gsmall = dict(zip(order, total[:-1]))
    gsmall["meta_tokens"] = lax.dynamic_slice_in_dim(gsmall["meta_tokens"], chip * meta_tokens.shape[1], meta_tokens.shape[1], 1)
    gsmall["ev_conv_w"] = lax.dynamic_slice_in_dim(gsmall["ev_conv_w"][:, :CONV_WIDTH], chip * ev_conv_w.shape[2], ev_conv_w.shape[2], 2)

    g_out, d_out, m_out, v_out = {}, {}, {}, {}
    for n in WEIGHTS:
        if n in BIG:
            g_out[n], d_out[n], m_out[n], v_out[n] = outs[n]
            continue
        shape = w[n].shape
        g = gsmall[n].reshape(shape)
        cols = shape[-1] if len(shape) > 1 else 128
        two = lambda a: a.reshape(-1, cols)
        d_, m_, v_ = _adamw(two(w[n]), two(g), two(m[n]), two(v[n]), f"adamw_{n}")
        g_out[n], d_out[n], m_out[n], v_out[n] = g, d_.reshape(shape), m_.reshape(shape), v_.reshape(shape)

    grad_x = dh[LEAD:][None]
    return (loss_sum, grad_x, *[g_out[n] for n in WEIGHTS], *[d_out[n] for n in WEIGHTS],
            *[m_out[n] for n in WEIGHTS], *[v_out[n] for n in WEIGHTS])
```

```python
import functools

import jax
import jax.numpy as jnp
from jax import lax
from jax.experimental import pallas as pl
from jax.experimental.pallas import tpu as pltpu

F32 = jnp.float32
BF16 = jnp.bfloat16
SDS = jax.ShapeDtypeStruct
MESH = pl.DeviceIdType.MESH
ANY_SPEC = pl.BlockSpec(memory_space=pl.ANY)

N_META = 16
CHUNK = 64
LEAD = CHUNK
PAD = LEAD - N_META
CONV_WIDTH = 31
CONV_ROWS = 32
POOL_WINDOWS = (2, 4, 8, 16)
HEAD_DIM = 128
SUB = 16
EXP_CAP = 80.0
EPS = 1e-6
ADAM_LR = 0.001
ADAM_B1 = 0.9
ADAM_B2 = 0.999
ADAM_EPS = 1e-08
ADAM_WD = 0.01
ADAM_STEP = 10
N_CHIPS = 4
VMEM_LIMIT = 52 << 20
MM_VMEM_BUDGET = 44 << 20


def _params(*sem):
    return pltpu.CompilerParams(dimension_semantics=sem if sem else None, vmem_limit_bytes=VMEM_LIMIT)


def _tile(n, target, unit=CHUNK):
    best = None
    for t in range(unit, min(n, target) + 1, unit):
        if n % t == 0:
            best = t
    assert best is not None, (n, target, unit)
    return best


def _ctile(n, target=512):
    for t in (512, 384, 256, 128):
        if t <= target and n % t == 0:
            return t
    raise ValueError(n)


def _mm_tiles(M, N, per_row, per_col, per_elem):
    best = None
    for tn in (512, 384, 256, 128):
        if N % tn:
            continue
        for tm in sorted((d for d in range(16, M + 1, 16) if M % d == 0), reverse=True):
            if 2 * (tm * per_row + tn * per_col + tm * tn * per_elem) <= MM_VMEM_BUDGET:
                if best is None or tm * tn > best[0] * best[1]:
                    best = (tm, tn)
                break
    assert best is not None, (M, N)
    return best


def _sigmoid(x):
    return 1.0 / (1.0 + jnp.exp(-x))


def _row_ids(shape, base):
    return lax.broadcasted_iota(jnp.int32, shape, 0) + base


def _cast_place(w3, layer, kind, chip1, dtype, name):
    _, ks, ns = w3.shape
    tr = _tile(ks, 512, 16)
    full = (ks, ns * N_CHIPS) if kind == "col" else (ks * N_CHIPS, ns)

    def body(chip_ref, w_ref, o_ref):
        del chip_ref
        o_ref[...] = w_ref[...].astype(dtype)

    omap = (lambda i, chip: (i, chip[0])) if kind == "col" else (lambda i, chip: (chip[0] * (ks // tr) + i, 0))
    return pl.pallas_call(
        body,
        grid_spec=pltpu.PrefetchScalarGridSpec(
            num_scalar_prefetch=1, grid=(ks // tr,),
            in_specs=[pl.BlockSpec((None, tr, ns), lambda i, chip: (layer, i, 0))],
            out_specs=pl.BlockSpec((tr, ns), omap)),
        out_shape=SDS(full, dtype), name=name, compiler_params=_params("parallel"))(chip1, w3)


def _rms_fwd(h, g3, layer, name, deps=()):
    T, D = h.shape
    tm = _tile(T, 832)

    def body(h_ref, g_ref, *rest):
        n_ref = rest[-1]
        x = h_ref[...]
        r = lax.rsqrt(jnp.mean(x * x, axis=-1, keepdims=True) + EPS)
        n_ref[...] = ((x * r) * g_ref[...]).astype(BF16)

    return pl.pallas_call(
        body, grid=(T // tm,),
        in_specs=[pl.BlockSpec((tm, D), lambda i: (i, 0)), pl.BlockSpec((None, 1, D), lambda i: (layer, 0, 0))]
        + [ANY_SPEC] * len(deps),
        out_specs=pl.BlockSpec((tm, D), lambda i: (i, 0)), out_shape=SDS((T, D), BF16),
        name=name, compiler_params=_params("parallel"))(h, g3, *deps)


def _final(h, g2, target):
    T, D = h.shape
    tm = _tile(T, 320)
    nsub = tm // CHUNK
    nblk = target.shape[0] // CHUNK

    def body(h_ref, g_ref, *rest):
        t_refs = rest[:nsub]
        dh_ref, dhb_ref, dg_ref, loss_ref = rest[nsub:]
        i = pl.program_id(0)

        @pl.when(i == 0)
        def _():
            dg_ref[...] = jnp.zeros_like(dg_ref)
            loss_ref[...] = jnp.zeros_like(loss_ref)

        g = g_ref[...]
        for q in range(nsub):
            rows = slice(q * CHUNK, (q + 1) * CHUNK)
            x = h_ref[rows, :]
            r = lax.rsqrt(jnp.mean(x * x, axis=-1, keepdims=True) + EPS)
            xh = x * r
            live = jnp.where(i * nsub + q > 0, 1.0, 0.0).astype(F32)
            e = ((xh * g) - t_refs[q][...]) * live
            dy = e * (1.0 / D)
            dxh = dy * g
            dh = r * (dxh - xh * jnp.mean(dxh * xh, axis=-1, keepdims=True))
            dh_ref[rows, :] = dh
            dhb_ref[rows, :] = dh.astype(BF16)
            dg_ref[...] += jnp.sum(dy * xh, axis=0, keepdims=True)
            loss_ref[...] += jnp.sum(e * e) * (0.5 / D)

    row = pl.BlockSpec((tm, D), lambda i: (i, 0))
    t_specs = [pl.BlockSpec((CHUNK, D), functools.partial(lambda i, q: (jnp.clip(i * nsub + q - 1, 0, nblk - 1), 0), q=q))
               for q in range(nsub)]
    return pl.pallas_call(
        body, grid=(T // tm,),
        in_specs=[row, pl.BlockSpec((1, D), lambda i: (0, 0))] + t_specs,
        out_specs=[row, row, pl.BlockSpec((1, D), lambda i: (0, 0)), pl.BlockSpec((1, 128), lambda i: (0, 0))],
        out_shape=[SDS((T, D), F32), SDS((T, D), BF16), SDS((1, D), F32), SDS((1, 128), F32)],
        name="final_loss", compiler_params=_params("arbitrary"))(h, g2, *([target] * nsub))


def _mm_nn(a, w3, layer, name, res=None, relu=False, square=False, deps=()):
    M, K = a.shape
    N = w3.shape[2]
    tm, tn = _mm_tiles(M, N, 2 * K, 2 * K, (2 if relu else 4) + (4 if res is not None else 0))

    def body(*refs):
        lhs = refs[0][...]
        acc = jnp.dot(lhs * lhs if square else lhs, refs[1][...], preferred_element_type=F32)
        if res is not None:
            acc = acc + refs[2][...]
        refs[-1][...] = jnp.maximum(acc, 0.0).astype(BF16) if relu else acc

    in_specs = [pl.BlockSpec((tm, K), lambda i, j: (i, 0)), pl.BlockSpec((None, K, tn), lambda i, j: (layer, 0, j))]
    args = [a, w3]
    tile = pl.BlockSpec((tm, tn), lambda i, j: (i, j))
    if res is not None:
        in_specs.append(tile)
        args.append(res)
    in_specs += [ANY_SPEC] * len(deps)
    args += list(deps)
    return pl.pallas_call(
        body, grid=(M // tm, N // tn), in_specs=in_specs, out_specs=tile,
        out_shape=SDS((M, N), BF16 if relu else F32),
        name=name, compiler_params=_params("parallel", "parallel"))(*args)


def _mm_nt(dy, w3, layer, name, relu=None, deps=()):
    M, N = dy.shape
    K = w3.shape[1]
    tm, tk = _mm_tiles(M, K, 2 * N, 2 * N, 4)

    def body(*refs):
        acc = lax.dot_general(refs[0][...], refs[1][...], (((1,), (1,)), ((), ())), preferred_element_type=F32)
        if relu is not None:
            acc = (acc * (2.0 * refs[2][...].astype(F32))).astype(BF16)
        refs[-1][...] = acc

    tile = pl.BlockSpec((tm, tk), lambda i, j: (i, j))
    in_specs = [pl.BlockSpec((tm, N), lambda i, j: (i, 0)), pl.BlockSpec((None, tk, N), lambda i, j: (layer, j, 0))]
    args = [dy, w3]
    if relu is not None:
        in_specs.append(tile)
        args.append(relu)
    in_specs += [ANY_SPEC] * len(deps)
    args += list(deps)
    return pl.pallas_call(
        body, grid=(M // tm, K // tk), in_specs=in_specs, out_specs=tile,
        out_shape=SDS((M, K), F32 if relu is None else BF16),
        name=name, compiler_params=_params("parallel", "parallel"))(*args)


def _row_tile(M, per_row, fixed):
    for tm in sorted((d for d in range(16, M + 1, 16) if M % d == 0), reverse=True):
        if 2 * (tm * per_row + fixed) <= MM_VMEM_BUDGET:
            return tm
    raise ValueError((M, per_row, fixed))


def _mm_nn_norm(a, w3, layer, res, g3, glayer, name, square=False, deps=()):
    M, K = a.shape
    D = w3.shape[2]
    tm = _row_tile(M, 2 * K + 10 * D, 2 * K * D)

    def body(a_ref, w_ref, r_ref, g_ref, *rest):
        h_ref, n_ref = rest[-2:]
        lhs = a_ref[...]
        x = r_ref[...] + jnp.dot(lhs * lhs if square else lhs, w_ref[...], preferred_element_type=F32)
        h_ref[...] = x
        r = lax.rsqrt(jnp.mean(x * x, axis=-1, keepdims=True) + EPS)
        n_ref[...] = ((x * r) * g_ref[...]).astype(BF16)

    row = pl.BlockSpec((tm, D), lambda i: (i, 0))
    return pl.pallas_call(
        body, grid=(M // tm,),
        in_specs=[pl.BlockSpec((tm, K), lambda i: (i, 0)), pl.BlockSpec((None, K, D), lambda i: (layer, 0, 0)), row,
                  pl.BlockSpec((None, 1, D), lambda i: (glayer, 0, 0))] + [ANY_SPEC] * len(deps),
        out_specs=[row, row], out_shape=[SDS((M, D), F32), SDS((M, D), BF16)],
        name=name, compiler_params=_params("parallel"))(a, w3, res, g3, *deps)


def _mm_nt_norm(dy, w3, layer, h, g3, glayer, dh_in, name, deps=()):
    M, N = dy.shape
    D = w3.shape[1]
    tm = _row_tile(M, 2 * N + 14 * D, 2 * N * D)

    def body(dy_ref, w_ref, h_ref, g_ref, dhi_ref, *rest):
        dh_ref, dhb_ref, dg_ref = rest[-3:]
        dn = lax.dot_general(dy_ref[...], w_ref[...], (((1,), (1,)), ((), ())), preferred_element_type=F32)
        x = h_ref[...]
        r = lax.rsqrt(jnp.mean(x * x, axis=-1, keepdims=True) + EPS)
        xh = x * r
        dxh = dn * g_ref[...]
        dh = dhi_ref[...] + r * (dxh - xh * jnp.mean(dxh * xh, axis=-1, keepdims=True))
        dh_ref[...] = dh
        dhb_ref[...] = dh.astype(BF16)

        @pl.when(pl.program_id(0) == 0)
        def _():
            dg_ref[...] = jnp.zeros_like(dg_ref)

        dg_ref[...] += jnp.sum(dn * xh, axis=0, keepdims=True)

    row = pl.BlockSpec((tm, D), lambda i: (i, 0))
    return pl.pallas_call(
        body, grid=(M // tm,),
        in_specs=[pl.BlockSpec((tm, N), lambda i: (i, 0)), pl.BlockSpec((None, D, N), lambda i: (layer, 0, 0)), row,
                  pl.BlockSpec((None, 1, D), lambda i: (glayer, 0, 0)), row] + [ANY_SPEC] * len(deps),
        out_specs=[row, row, pl.BlockSpec((1, D), lambda i: (0, 0))],
        out_shape=[SDS((M, D), F32), SDS((M, D), BF16), SDS((1, D), F32)],
        name=name, compiler_params=_params("arbitrary"))(dy, w3, h, g3, dh_in, *deps)


def _fam_dims(kind, K, N):
    return (K // 2, N // N_CHIPS) if kind == "col" else (K // (2 * N_CHIPS), N)


def _mm_tn(x, dy, kind, name, square=False):
    M, K = x.shape
    N = dy.shape[1]
    nr, nc = _fam_dims(kind, K, N)

    def body(x_ref, dy_ref, o_ref):
        lhs = x_ref[...]
        res = lax.dot_general(lhs * lhs if square else lhs, dy_ref[...], (((0,), (0,)), ((), ())), preferred_element_type=F32)
        o_ref[...] = res.astype(BF16).reshape(o_ref.shape)

    if kind == "col":
        tn = _ctile(nc)
        ct = nc // tn
        grid = (N // tn,)
        in_specs = [pl.BlockSpec((M, K), lambda j: (0, 0)), pl.BlockSpec((M, tn), lambda j: (0, j))]
        out_spec = pl.BlockSpec((2, None, nr, tn), lambda j: (0, j // ct, 0, j % ct))
    else:
        grid = (N_CHIPS,)
        in_specs = [pl.BlockSpec((M, 2 * nr), lambda i: (0, i)), pl.BlockSpec((M, N), lambda i: (0, 0))]
        out_spec = pl.BlockSpec((2, None, nr, N), lambda i: (0, i, 0, 0))
    return pl.pallas_call(
        body, grid=grid, in_specs=in_specs, out_specs=out_spec, out_shape=SDS((2, N_CHIPS, nr, nc), BF16),
        name=name, compiler_params=_params("parallel"))(x, dy)


C_EVEN = 512


def _live(rows, base, total):
    r = _row_ids((rows, 1), base)
    return jnp.logical_and(r >= PAD, r < total).astype(F32)


def _conv_taps(win, w_ref, ls, acc, flip):
    for b in range(8):
        rb = win if b == 0 else pltpu.roll(win, 96 - b, 0)
        for a in range(5):
            o = 8 * a + b
            tap = (30 - o) if flip else (o - 2)
            if 0 <= tap < CONV_WIDTH:
                acc = acc + w_ref[pl.ds(tap, 1), ls] * rb[8 * a:8 * a + CHUNK]
    return acc


def _window_sum(win, levels, forward):
    s = win
    n = win.shape[0]
    for k in range(levels):
        step = 1 << k
        s = s + pltpu.roll(s, (n - step) if forward else step, 0)
    return s


def _pool_count(base, g):
    pos = _row_ids((CHUNK, 1), base) - PAD
    return jnp.clip(pos + 1, 1, POOL_WINDOWS[g]).astype(F32)


def _even_fwd(u, cw3, cb3, lg3, lb3, pw4, pb3, ps3, j, name):
    T = u.shape[0]
    C = C_EVEN
    tm = _tile(T, 320)
    nch = tm // CHUNK
    nblk = T // CHUNK

    def body(u_ref, up_ref, cw_ref, cb_ref, lg_ref, lb_ref, pw_ref, pb_ref, ps_ref, o_ref, yc_ref, a_s, p_s, yc_s):
        row0 = pl.program_id(0) * tm
        up = up_ref[...]
        lp = _live(CHUNK, row0 - CHUNK, T)
        a_s[0:CHUNK, :] = up[:, 0:C] * _sigmoid(up[:, C:2 * C]) * lp
        p_s[0:CHUNK, :] = up[:, 2 * C:3 * C] * lp

        def stage(c, _):
            rs = pl.multiple_of(c * CHUNK, CHUNK)
            lv = _live(CHUNK, row0 + rs, T)
            a_s[pl.ds(rs + CHUNK, CHUNK), :] = u_ref[pl.ds(rs, CHUNK), 0:C] * _sigmoid(u_ref[pl.ds(rs, CHUNK), C:2 * C]) * lv
            p_s[pl.ds(rs + CHUNK, CHUNK), :] = u_ref[pl.ds(rs, CHUNK), 2 * C:3 * C] * lv
            return 0

        lax.fori_loop(0, nch, stage, 0)

        def chunk(c, _):
            rs = pl.multiple_of(c * CHUNK, CHUNK)
            lv = _live(CHUNK, row0 + rs, T)
            for cb in range(4):
                ls = slice(cb * 128, (cb + 1) * 128)
                win = a_s[pl.ds(pl.multiple_of(rs + 32, 32), 96), ls]
                acc = jnp.broadcast_to(cb_ref[:, ls], (CHUNK, 128))
                yc_s[:, ls] = _conv_taps(win, cw_ref, ls, acc, False)
            y = yc_s[...]
            yc_ref[pl.ds(rs, CHUNK), :] = y
            xc = y - jnp.mean(y, axis=-1, keepdims=True)
            yn = xc * lax.rsqrt(jnp.mean(xc * xc, axis=-1, keepdims=True) + EPS) * lg_ref[...] + lb_ref[...]
            o_ref[pl.ds(rs, CHUNK), 0:C] = (yn * _sigmoid(yn) * lv).astype(BF16)
            for g in range(4):
                ls = slice(g * 128, (g + 1) * 128)
                win = p_s[pl.ds(pl.multiple_of(rs + 48, 16), 80), ls]
                s = _window_sum(win, g + 1, False)
                d = s[16:80] / _pool_count(row0 + rs, g) - win[16:80]
                yv = jnp.dot(d.astype(BF16), pw_ref[g].astype(BF16), preferred_element_type=F32) + pb_ref[:, ls]
                o_ref[pl.ds(rs, CHUNK), C + g * 128:C + (g + 1) * 128] = (yv * ps_ref[:, ls] * lv).astype(BF16)
            return 0

        lax.fori_loop(0, nch, chunk, 0)

    vec = pl.BlockSpec((None, 1, C), lambda i: (j, 0, 0))
    return pl.pallas_call(
        body, grid=(T // tm,),
        in_specs=[pl.BlockSpec((tm, 3 * C), lambda i: (i, 0)),
                  pl.BlockSpec((CHUNK, 3 * C), lambda i: (jnp.maximum(i * nch - 1, 0), 0)),
                  pl.BlockSpec((None, CONV_ROWS, C), lambda i: (j, 0, 0)), vec, vec, vec,
                  pl.BlockSpec((None, 4, 128, 128), lambda i: (j, 0, 0, 0)), vec, vec],
        out_specs=[pl.BlockSpec((tm, 2 * C), lambda i: (i, 0)), pl.BlockSpec((tm, C), lambda i: (i, 0))],
        out_shape=[SDS((T, 2 * C), BF16), SDS((T, C), F32)],
        scratch_shapes=[pltpu.VMEM((tm + CHUNK, C), F32), pltpu.VMEM((tm + CHUNK, C), F32), pltpu.VMEM((CHUNK, C), F32)],
        name=name, compiler_params=_params("parallel"))(u, u, cw3, cb3, lg3, lb3, pw4, pb3, ps3)


def _even_bwd(u, yc, dy, cw3, cb3, lg3, lb3, pw4, pb3, ps3, j, name):
    T = u.shape[0]
    C = C_EVEN
    tm = _tile(T, 320)
    nch = tm // CHUNK
    nblk = T // CHUNK
    ntile = T // tm

    def body(u_ref, up_ref, un_ref, yc_ref, ycn_ref, dy_ref, dyn_ref, cw_ref, cb_ref, lg_ref, lb_ref, pw_ref, pb_ref, ps_ref,
             du_ref, dcw_ref, dcb_ref, dlg_ref, dlb_ref, dpw_ref, dpb_ref, dps_ref,
             a_s, p_s, dy_s, dyc_s, dd_s, ddc_s, dw_s):
        i = pl.program_id(0)
        row0 = i * tm

        @pl.when(i == 0)
        def _():
            for ref in (dcb_ref, dlg_ref, dlb_ref, dpw_ref, dpb_ref, dps_ref, dw_s):
                ref[...] = jnp.zeros_like(ref)

        up = up_ref[...]
        lp = _live(CHUNK, row0 - CHUNK, T)
        a_s[0:CHUNK, :] = up[:, 0:C] * _sigmoid(up[:, C:2 * C]) * lp
        p_s[0:CHUNK, :] = up[:, 2 * C:3 * C] * lp
        ln_ = _live(CHUNK, row0 + tm, T)
        p_s[tm + CHUNK:tm + 2 * CHUNK, :] = un_ref[:, 2 * C:3 * C] * ln_
        dy_s[tm:tm + CHUNK, :] = dyn_ref[...] * ln_
        dyc_s[tm + CHUNK:tm + CHUNK + 32, :] = jnp.zeros((32, C), F32)

        def stage(c, _):
            rs = pl.multiple_of(c * CHUNK, CHUNK)
            lv = _live(CHUNK, row0 + rs, T)
            a_s[pl.ds(rs + CHUNK, CHUNK), :] = u_ref[pl.ds(rs, CHUNK), 0:C] * _sigmoid(u_ref[pl.ds(rs, CHUNK), C:2 * C]) * lv
            p_s[pl.ds(rs + CHUNK, CHUNK), :] = u_ref[pl.ds(rs, CHUNK), 2 * C:3 * C] * lv
            dy_s[pl.ds(rs, CHUNK), :] = dy_ref[pl.ds(rs, CHUNK), :] * lv
            return 0

        lax.fori_loop(0, nch, stage, 0)

        def first(rs, y, own):
            xc = y - jnp.mean(y, axis=-1, keepdims=True)
            rstd = lax.rsqrt(jnp.mean(xc * xc, axis=-1, keepdims=True) + EPS)
            xh = xc * rstd
            yn = xh * lg_ref[...] + lb_ref[...]
            sg = _sigmoid(yn)
            dyn = dy_s[pl.ds(rs, CHUNK), 0:C] * (sg * (1.0 + yn * (1.0 - sg)))
            dlg_ref[...] += jnp.sum(dyn * xh, axis=0, keepdims=True) * own
            dlb_ref[...] += jnp.sum(dyn, axis=0, keepdims=True) * own
            dxh = dyn * lg_ref[...]
            dyc = rstd * (dxh - jnp.mean(dxh, axis=-1, keepdims=True) - xh * jnp.mean(dxh * xh, axis=-1, keepdims=True))
            dyc_s[pl.ds(rs, CHUNK), :] = dyc
            dcb_ref[...] += jnp.sum(dyc, axis=0, keepdims=True) * own
            for g in range(4):
                ls = slice(g * 128, (g + 1) * 128)
                win = p_s[pl.ds(rs + 48, 80), ls]
                s = _window_sum(win, g + 1, False)
                cnt = _pool_count(row0 + rs, g)
                d = (s[16:80] / cnt - win[16:80]).astype(BF16)
                w = pw_ref[g].astype(BF16)
                pre = jnp.dot(d, w, preferred_element_type=F32) + pb_ref[:, ls]
                dyb = dy_s[pl.ds(rs, CHUNK), C + g * 128:C + (g + 1) * 128]
                dpre = dyb * ps_ref[:, ls]
                dps_ref[:, ls] += jnp.sum(dyb * pre, axis=0, keepdims=True) * own
                dpb_ref[:, ls] += jnp.sum(dpre, axis=0, keepdims=True) * own
                dpre_b = (dpre * own).astype(BF16)
                dpw_ref[g] += lax.dot_general(d, dpre_b, (((0,), (0,)), ((), ())), preferred_element_type=F32)
                dd = lax.dot_general(dpre.astype(BF16), w, (((1,), (1,)), ((), ())), preferred_element_type=F32)
                dd_s[pl.ds(rs, CHUNK), ls] = dd
                ddc_s[pl.ds(rs, CHUNK), ls] = dd / cnt

        def first_in_tile(c, _):
            rs = pl.multiple_of(c * CHUNK, 16 * CHUNK // 16)
            first(rs, yc_ref[pl.ds(rs, CHUNK), :], 1.0)
            return 0

        lax.fori_loop(0, nch, first_in_tile, 0)
        first(tm, ycn_ref[...], 0.0)
        ddc_s[tm + CHUNK:tm + CHUNK + 16, :] = jnp.zeros((16, C), F32)

        def second(c, _):
            rs = pl.multiple_of(c * CHUNK, CHUNK)
            lv = _live(CHUNK, row0 + rs, T)
            for cb in range(4):
                ls = slice(cb * 128, (cb + 1) * 128)
                wd = dyc_s[pl.ds(rs, 96), ls]
                da = _conv_taps(wd, cw_ref, ls, jnp.zeros((CHUNK, 128), F32), True)
                wa = a_s[pl.ds(pl.multiple_of(rs + 32, 32), 96), ls]
                dyc = dyc_s[pl.ds(rs, CHUNK), ls]
                for b in range(8):
                    rb = wa if b == 0 else pltpu.roll(wa, 96 - b, 0)
                    for a in range(5):
                        tap = 8 * a + b - 2
                        if 0 <= tap < CONV_WIDTH:
                            prod = dyc * rb[8 * a:8 * a + CHUNK]
                            part = prod[0:8]
                            for q in range(1, 8):
                                part = part + prod[8 * q:8 * q + 8]
                            dw_s[8 * tap:8 * tap + 8, ls] += part
                val = u_ref[pl.ds(rs, CHUNK), ls]
                sg = _sigmoid(u_ref[pl.ds(rs, CHUNK), C + cb * 128:C + (cb + 1) * 128])
                du_ref[pl.ds(rs, CHUNK), ls] = (da * sg * lv).astype(BF16)
                du_ref[pl.ds(rs, CHUNK), C + cb * 128:C + (cb + 1) * 128] = (da * val * sg * (1.0 - sg) * lv).astype(BF16)
            for g in range(4):
                ls = slice(g * 128, (g + 1) * 128)
                z = _window_sum(ddc_s[pl.ds(rs, 80), ls], g + 1, True)
                dpin = (z[0:CHUNK] - dd_s[pl.ds(rs, CHUNK), ls]) * lv
                du_ref[pl.ds(rs, CHUNK), 2 * C + g * 128:2 * C + (g + 1) * 128] = dpin.astype(BF16)
            return 0

        lax.fori_loop(0, nch, second, 0)

        @pl.when(i == ntile - 1)
        def _():
            for tap in range(CONV_WIDTH):
                dcw_ref[tap:tap + 1, :] = jnp.sum(dw_s[8 * tap:8 * tap + 8, :], axis=0, keepdims=True)
            dcw_ref[CONV_WIDTH:CONV_ROWS, :] = jnp.zeros((CONV_ROWS - CONV_WIDTH, C), F32)

    vec = pl.BlockSpec((None, 1, C), lambda i: (j, 0, 0))
    ovec = pl.BlockSpec((1, C), lambda i: (0, 0))
    return pl.pallas_call(
        body, grid=(ntile,),
        in_specs=[pl.BlockSpec((tm, 3 * C), lambda i: (i, 0)),
                  pl.BlockSpec((CHUNK, 3 * C), lambda i: (jnp.maximum(i * nch - 1, 0), 0)),
                  pl.BlockSpec((CHUNK, 3 * C), lambda i: (jnp.minimum((i + 1) * nch, nblk - 1), 0)),
                  pl.BlockSpec((tm, C), lambda i: (i, 0)),
                  pl.BlockSpec((CHUNK, C), lambda i: (jnp.minimum((i + 1) * nch, nblk - 1), 0)),
                  pl.BlockSpec((tm, 2 * C), lambda i: (i, 0)),
                  pl.BlockSpec((CHUNK, 2 * C), lambda i: (jnp.minimum((i + 1) * nch, nblk - 1), 0)),
                  pl.BlockSpec((None, CONV_ROWS, C), lambda i: (j, 0, 0)), vec, vec, vec,
                  pl.BlockSpec((None, 4, 128, 128), lambda i: (j, 0, 0, 0)), vec, vec],
        out_specs=[pl.BlockSpec((tm, 3 * C), lambda i: (i, 0)), pl.BlockSpec((CONV_ROWS, C), lambda i: (0, 0)),
                   ovec, ovec, ovec, pl.BlockSpec((4, 128, 128), lambda i: (0, 0, 0)), ovec, ovec],
        out_shape=[SDS((T, 3 * C), BF16), SDS((CONV_ROWS, C), F32), SDS((1, C), F32), SDS((1, C), F32), SDS((1, C), F32),
                   SDS((4, 128, 128), F32), SDS((1, C), F32), SDS((1, C), F32)],
        scratch_shapes=[pltpu.VMEM((tm + CHUNK, C), F32), pltpu.VMEM((tm + 2 * CHUNK, C), F32),
                        pltpu.VMEM((tm + CHUNK, 2 * C), F32),
                        pltpu.VMEM((tm + CHUNK + 32, C), F32), pltpu.VMEM((tm + CHUNK, C), F32),
                        pltpu.VMEM((tm + CHUNK + 16, C), F32), pltpu.VMEM((8 * CONV_ROWS, C), F32)],
        name=name, compiler_params=_params("arbitrary"))(u, u, u, yc, yc, dy, dy, cw3, cb3, lg3, lb3, pw4, pb3, ps3)


HI = lax.Precision.HIGHEST


def _dot_nt(a, b):
    return lax.dot_general(a, b, (((1,), (1,)), ((), ())), preferred_element_type=F32)


def _dot_tn(a, b):
    return lax.dot_general(a, b, (((0,), (0,)), ((), ())), preferred_element_type=F32)


def _tri(lower):
    r = lax.broadcasted_iota(jnp.int32, (CHUNK, CHUNK), 0)
    c = lax.broadcasted_iota(jnp.int32, (CHUNK, CHUNK), 1)
    return jnp.where((c <= r) if lower else (c >= r), 1.0, 0.0).astype(F32)


def _hgrn_gates(u_ref, lb_ref, h, D, lv):
    ls = slice(h * HEAD_DIM, (h + 1) * HEAD_DIM)
    qraw = u_ref[:, ls]
    fraw = u_ref[:, D + h * HEAD_DIM:D + (h + 1) * HEAD_DIM]
    v = u_ref[:, 2 * D + h * HEAD_DIM:2 * D + (h + 1) * HEAD_DIM] * lv
    lbv = lb_ref[:, ls]
    sig = _sigmoid(fraw)
    forget = lbv + (1.0 - lbv) * sig
    logf = jnp.log(forget) * lv
    k = (1.0 - forget) * lv
    qsig = _sigmoid(qraw)
    q = qraw * qsig * lv
    return q, k, v, logf, (qraw, qsig, sig, forget, lbv)


def _sub_parts(q, k, b, b_s, I):
    rows = slice(SUB * I, SUB * (I + 1))
    rho = jnp.zeros((1, HEAD_DIM), F32) if I == 0 else b_s[SUB * I - 1:SUB * I, :]
    eI = jnp.exp(b[rows] - rho)
    EI = jnp.exp(jnp.minimum(rho - b, EXP_CAP))
    causal = (lax.broadcasted_iota(jnp.int32, (SUB, CHUNK), 1)
              <= lax.broadcasted_iota(jnp.int32, (SUB, CHUNK), 0) + SUB * I)
    return rows, q[rows] * eI, k * EI, eI, EI, causal


def _hgrn_fwd(u, lb3, layer, gn3, j, name):
    T = u.shape[0]
    D = u.shape[1] // 4
    H = D // HEAD_DIM
    NC = T // CHUNK

    def body(u_ref, lb_ref, gn_ref, y_ref, o_ref, sall_ref, st_s, b_s, lf_s, q_s, k_s):
        n = pl.program_id(0)

        @pl.when(n == 0)
        def _():
            st_s[...] = jnp.zeros_like(st_s)

        lv = _live(CHUNK, n * CHUNK, T)
        heads = range(H)
        cols = [slice(h * HEAD_DIM, (h + 1) * HEAD_DIM) for h in heads]
        vb = []
        for h in heads:
            q, k, v, logf, _ = _hgrn_gates(u_ref, lb_ref, h, D, lv)
            q_s[:, cols[h]] = q
            k_s[:, cols[h]] = k
            lf_s[:, cols[h]] = logf
            vb.append(v.astype(BF16))
        b_s[...] = jnp.dot(_tri(True), lf_s[...], precision=HI, preferred_element_type=F32)
        ops = []
        for h in heads:
            b_h = b_s.at[:, cols[h]]
            b = b_h[...]
            q = q_s[:, cols[h]]
            k = k_s[:, cols[h]]
            blast = b_h[CHUNK - 1:CHUNK, :]
            qh = (q * jnp.exp(b)).astype(BF16)
            kt = (k * jnp.exp(blast - b)).astype(BF16)
            subs = []
            for I in range(CHUNK // SUB):
                _, qI, KI, _, _, causal = _sub_parts(q, k, b, b_h, I)
                subs.append((qI.astype(BF16), KI.astype(BF16), causal))
            ops.append((qh, kt, jnp.exp(blast), subs))
        mm = []
        for h in heads:
            qh, kt, eblast, subs = ops[h]
            st = st_s[h]
            sall_ref[h] = st
            o_inter = _dot_nt(qh, st.astype(BF16))
            st_s[h] = st * eblast + _dot_tn(vb[h], kt)
            mm.append((o_inter, [_dot_nt(qI, KI) for qI, KI, _ in subs]))
        for h in heads:
            o_inter, ps = mm[h]
            p = jnp.concatenate([jnp.where(c, x, 0.0) for x, (_, _, c) in zip(ps, ops[h][3])], axis=0).astype(BF16)
            o = o_inter + jnp.dot(p, vb[h], preferred_element_type=F32)
            o_ref[:, cols[h]] = o
            graw = u_ref[:, 3 * D + h * HEAD_DIM:3 * D + (h + 1) * HEAD_DIM]
            r = lax.rsqrt(jnp.mean(o * o, axis=-1, keepdims=True) + EPS)
            y_ref[:, cols[h]] = (((o * r) * gn_ref[...]) * (graw * _sigmoid(graw))).astype(BF16)

    return pl.pallas_call(
        body, grid=(NC,),
        in_specs=[pl.BlockSpec((CHUNK, 4 * D), lambda n: (n, 0)),
                  pl.BlockSpec((None, 1, D), lambda n: (layer, 0, 0)),
                  pl.BlockSpec((None, 1, HEAD_DIM), lambda n: (j, 0, 0))],
        out_specs=[pl.BlockSpec((CHUNK, D), lambda n: (n, 0)), pl.BlockSpec((CHUNK, D), lambda n: (n, 0)),
                   pl.BlockSpec((None, H, HEAD_DIM, HEAD_DIM), lambda n: (n, 0, 0, 0))],
        out_shape=[SDS((T, D), BF16), SDS((T, D), F32), SDS((NC, H, HEAD_DIM, HEAD_DIM), F32)],
        scratch_shapes=[pltpu.VMEM((H, HEAD_DIM, HEAD_DIM), F32)] + [pltpu.VMEM((CHUNK, D), F32)] * 4,
        name=name, compiler_params=_params("arbitrary"))(u, lb3, gn3)


def _hgrn_bwd(u, o_raw, dy, sall, lb3, layer, gn3, j, name):
    T = u.shape[0]
    D = u.shape[1] // 4
    H = D // HEAD_DIM
    NC = T // CHUNK

    def body(u_ref, o_ref, dy_ref, sall_ref, lb_ref, gn_ref, du_ref, dlb_ref, dgn_ref, dst_s, b_s, lf_s, q_s, k_s, db_s, dk_s):
        step = pl.program_id(0)
        n = NC - 1 - step

        @pl.when(step == 0)
        def _():
            dst_s[...] = jnp.zeros_like(dst_s)
            dlb_ref[...] = jnp.zeros_like(dlb_ref)
            dgn_ref[...] = jnp.zeros_like(dgn_ref)

        lv = _live(CHUNK, n * CHUNK, T)
        last_row = (_row_ids((CHUNK, 1), 0) == CHUNK - 1).astype(F32)
        gn = gn_ref[...]
        heads = range(H)
        cols = [slice(h * HEAD_DIM, (h + 1) * HEAD_DIM) for h in heads]
        vb, dob = [], []
        dgn = jnp.zeros((1, HEAD_DIM), F32)
        for h in heads:
            q, k, v, logf, _ = _hgrn_gates(u_ref, lb_ref, h, D, lv)
            q_s[:, cols[h]] = q
            k_s[:, cols[h]] = k
            lf_s[:, cols[h]] = logf
            vb.append(v.astype(BF16))
            graw = u_ref[:, 3 * D + h * HEAD_DIM:3 * D + (h + 1) * HEAD_DIM]
            gsig = _sigmoid(graw)
            o = o_ref[:, cols[h]]
            r = lax.rsqrt(jnp.mean(o * o, axis=-1, keepdims=True) + EPS)
            xh = o * r
            dyv = dy_ref[:, cols[h]]
            dsg = dyv * (graw * gsig)
            dgn = dgn + jnp.sum(dsg * xh, axis=0, keepdims=True)
            dxh = dsg * gn
            do = r * (dxh - xh * jnp.mean(dxh * xh, axis=-1, keepdims=True))
            dob.append(do.astype(BF16))
            dgraw = dyv * xh * gn * (gsig * (1.0 + graw * (1.0 - gsig)))
            du_ref[:, 3 * D + h * HEAD_DIM:3 * D + (h + 1) * HEAD_DIM] = (dgraw * lv).astype(BF16)
        dgn_ref[...] += dgn
        b_s[...] = jnp.dot(_tri(True), lf_s[...], precision=HI, preferred_element_type=F32)
        ops = []
        for h in heads:
            b_h = b_s.at[:, cols[h]]
            b = b_h[...]
            q = q_s[:, cols[h]]
            k = k_s[:, cols[h]]
            blast = b_h[CHUNK - 1:CHUNK, :]
            eb = jnp.exp(b)
            ekb = jnp.exp(blast - b)
            subs = []
            for I in range(CHUNK // SUB):
                rows, qI, KI, eI, EI, causal = _sub_parts(q, k, b, b_h, I)
                subs.append((rows, qI.astype(BF16), KI.astype(BF16), eI, EI, causal))
            ops.append((eb, ekb, jnp.exp(blast), (q * eb).astype(BF16), (k * ekb).astype(BF16), subs))
        mm = []
        for h in heads:
            eb, ekb, eblast, qhb, ktb, subs = ops[h]
            st = sall_ref[h]
            dst = dst_s[h]
            dstb = dst.astype(BF16)
            dv = _dot_nt(ktb, dstb)
            dqh = jnp.dot(dob[h], st.astype(BF16), preferred_element_type=F32)
            dkt = jnp.dot(vb[h], dstb, preferred_element_type=F32)
            dblast = jnp.sum(dst * st, axis=0, keepdims=True) * eblast
            dst_s[h] = dst * eblast + _dot_tn(dob[h], qhb)
            dp_full = _dot_nt(dob[h], vb[h])
            ps = [_dot_nt(qIb, KIb) for _, qIb, KIb, _, _, _ in subs]
            mm.append((dv, dqh, dkt, dblast, dp_full, ps))
        for h in heads:
            eb, ekb, eblast, qhb, ktb, subs = ops[h]
            dv, dqh, dkt, dblast, dp_full, ps = mm[h]
            p = jnp.concatenate([jnp.where(sub[5], x, 0.0) for x, sub in zip(ps, subs)], axis=0).astype(BF16)
            dv = dv + _dot_tn(p, dob[h])
            du_ref[:, 2 * D + h * HEAD_DIM:2 * D + (h + 1) * HEAD_DIM] = (dv * lv).astype(BF16)
            dq = dqh * eb
            db = dqh * qhb.astype(F32)
            tmp = dkt * ktb.astype(F32)
            dk = dkt * ekb
            db = db - tmp
            dblast = dblast + jnp.sum(tmp, axis=0, keepdims=True)
            dq_parts, db_parts = [], []
            for rows, qIb, KIb, eI, EI, causal in subs:
                dp = jnp.where(causal, dp_full[rows], 0.0).astype(BF16)
                dqI = jnp.dot(dp, KIb, preferred_element_type=F32)
                dKI = _dot_tn(dp, qIb)
                dq_parts.append(dqI * eI)
                db_parts.append(dqI * qIb.astype(F32))
                dk = dk + dKI * EI
                db = db - dKI * KIb.astype(F32)
            dq = dq + jnp.concatenate(dq_parts, axis=0)
            db_s[:, cols[h]] = db + jnp.concatenate(db_parts, axis=0) + last_row * dblast
            dk_s[:, cols[h]] = dk
            qraw = u_ref[:, cols[h]]
            qsig = _sigmoid(qraw)
            du_ref[:, cols[h]] = (dq * (qsig * (1.0 + qraw * (1.0 - qsig))) * lv).astype(BF16)
        lf_s[...] = jnp.dot(_tri(False), db_s[...], precision=HI, preferred_element_type=F32)
        for h in heads:
            fraw = u_ref[:, D + h * HEAD_DIM:D + (h + 1) * HEAD_DIM]
            lbv = lb_ref[:, cols[h]]
            sig = _sigmoid(fraw)
            forget = lbv + (1.0 - lbv) * sig
            dforget = (lf_s[:, cols[h]] / forget - dk_s[:, cols[h]]) * lv
            dlb_ref[:, cols[h]] += jnp.sum(dforget * (1.0 - sig), axis=0, keepdims=True)
            du_ref[:, D + h * HEAD_DIM:D + (h + 1) * HEAD_DIM] = (dforget * (1.0 - lbv) * sig * (1.0 - sig)).astype(BF16)

    rev = lambda s: (NC - 1 - s, 0)
    return pl.pallas_call(
        body, grid=(NC,),
        in_specs=[pl.BlockSpec((CHUNK, 4 * D), rev), pl.BlockSpec((CHUNK, D), rev), pl.BlockSpec((CHUNK, D), rev),
                  pl.BlockSpec((None, H, HEAD_DIM, HEAD_DIM), lambda s: (NC - 1 - s, 0, 0, 0)),
                  pl.BlockSpec((None, 1, D), lambda s: (layer, 0, 0)),
                  pl.BlockSpec((None, 1, HEAD_DIM), lambda s: (j, 0, 0))],
        out_specs=[pl.BlockSpec((CHUNK, 4 * D), rev), pl.BlockSpec((1, D), lambda s: (0, 0)),
                   pl.BlockSpec((1, HEAD_DIM), lambda s: (0, 0))],
        out_shape=[SDS((T, 4 * D), BF16), SDS((1, D), F32), SDS((1, HEAD_DIM), F32)],
        scratch_shapes=[pltpu.VMEM((H, HEAD_DIM, HEAD_DIM), F32)] + [pltpu.VMEM((CHUNK, D), F32)] * 6,
        name=name, compiler_params=_params("arbitrary"))(u, o_raw, dy, sall, lb3, gn3)


def _softmax_layers(p_ref, n_layers):
    rows = [p_ref[l:l + 1, :] for l in range(n_layers)]
    m = functools.reduce(jnp.maximum, rows)
    e = [jnp.exp(x - m) for x in rows]
    tot = functools.reduce(lambda a, b: a + b, e)
    return [x / tot for x in e]


def _lb_fwd(p):
    n_layers, D = p.shape

    def body(p_ref, o_ref):
        s = _softmax_layers(p_ref, n_layers)
        acc = jnp.zeros((1, D), F32)
        o_ref[0:1, :] = acc
        for l in range(1, n_layers):
            acc = acc + s[l]
            o_ref[l:l + 1, :] = acc

    return pl.pallas_call(body, out_shape=SDS(p.shape, F32), name="lb_fwd")(p)


def _lb_bwd(p, dlb):
    n_layers, D = p.shape

    def body(p_ref, d_ref, o_ref):
        s = _softmax_layers(p_ref, n_layers)
        ds = [jnp.zeros((1, D), F32)] * n_layers
        acc = jnp.zeros((1, D), F32)
        for l in range(n_layers - 1, 0, -1):
            acc = acc + d_ref[l:l + 1, :]
            ds[l] = acc
        dot = functools.reduce(lambda a, b: a + b, [s[l] * ds[l] for l in range(n_layers)])
        for l in range(n_layers):
            o_ref[l:l + 1, :] = s[l] * (ds[l] - dot)

    return pl.pallas_call(body, out_shape=SDS(p.shape, F32), name="lb_bwd")(p, dlb)


def _adamw(w, g, m, v, name):
    R, C = w.shape
    tr = _tile(R, 256, 8) if R % 8 == 0 else R

    def body(w_ref, g_ref, m_ref, v_ref, d_ref, mo_ref, vo_ref):
        g_ = g_ref[...]
        m_ = ADAM_B1 * m_ref[...] + (1.0 - ADAM_B1) * g_
        v_ = ADAM_B2 * v_ref[...] + (1.0 - ADAM_B2) * (g_ * g_)
        mh = m_ / (1.0 - ADAM_B1 ** ADAM_STEP)
        vh = v_ / (1.0 - ADAM_B2 ** ADAM_STEP)
        d_ref[...] = -ADAM_LR * (mh / (jnp.sqrt(vh) + ADAM_EPS) + ADAM_WD * w_ref[...])
        mo_ref[...] = m_
        vo_ref[...] = v_

    blk = pl.BlockSpec((tr, C), lambda i: (i, 0))
    return pl.pallas_call(
        body, grid=(R // tr,), in_specs=[blk] * 4, out_specs=[blk] * 3, out_shape=[SDS((R, C), F32)] * 3,
        name=name, compiler_params=_params("parallel"))(w, g, m, v)


def _adamw_layer(w3, m3, v3, g2, layer, outs, name):
    L, R, C = w3.shape
    tr = _tile(R, 256, 8)
    if outs is None:
        outs = tuple(lax.empty(w3.shape, F32) for _ in range(4))

    def body(w_ref, m_ref, v_ref, g_ref, a0, a1, a2, a3, go_ref, d_ref, mo_ref, vo_ref):
        del a0, a1, a2, a3
        g_ = g_ref[...]
        m_ = ADAM_B1 * m_ref[...] + (1.0 - ADAM_B1) * g_
        v_ = ADAM_B2 * v_ref[...] + (1.0 - ADAM_B2) * (g_ * g_)
        mh = m_ / (1.0 - ADAM_B1 ** ADAM_STEP)
        vh = v_ / (1.0 - ADAM_B2 ** ADAM_STEP)
        go_ref[...] = g_
        d_ref[...] = -ADAM_LR * (mh / (jnp.sqrt(vh) + ADAM_EPS) + ADAM_WD * w_ref[...])
        mo_ref[...] = m_
        vo_ref[...] = v_

    lay = pl.BlockSpec((None, tr, C), lambda i: (layer, i, 0))
    return pl.pallas_call(
        body, grid=(R // tr,), in_specs=[lay] * 3 + [pl.BlockSpec((tr, C), lambda i: (i, 0))] + [ANY_SPEC] * 4,
        out_specs=[lay] * 4, out_shape=[SDS(w3.shape, F32)] * 4, input_output_aliases={4: 0, 5: 1, 6: 2, 7: 3},
        name=name, compiler_params=_params("parallel"))(w3, m3, v3, g2, *outs)


SEM_SPEC = pl.BlockSpec(memory_space=pltpu.SEMAPHORE)
HBM_SPEC = pl.BlockSpec(memory_space=pltpu.HBM)
EFFECT = pltpu.SideEffectType.DATAFLOW_SIDE_EFFECTING
N_DEV = 2 * N_CHIPS


def _position():
    x, y, c = lax.axis_index("x"), lax.axis_index("y"), lax.axis_index("c")
    chips = [(1 - x, y), (x, 1 - y), (1 - x, 1 - y)]
    return x, y, c, chips


def _split_start(name, plan, bufs, n_sems, deps=(), earlier=None):
    n = len(bufs)
    held = () if earlier is None else tuple(earlier[1:])

    def body(*refs):
        first_out = n + len(held) + len(deps)
        if earlier is not None:
            sends, recvs = earlier[0](refs[:n], refs[n], refs[n + 1])
            for kw in sends:
                pltpu.make_async_remote_copy(**kw).wait_send()
            for kw in recvs:
                pltpu.make_async_remote_copy(**kw).wait_recv()
        sends, _ = plan(refs[:n], refs[first_out], refs[first_out + 1])
        for kw in sends:
            pltpu.make_async_remote_copy(**kw).start()
        refs[-1][...] = jnp.zeros_like(refs[-1])

    out = pl.pallas_call(
        body, name=name,
        out_shape=(pltpu.SemaphoreType.DMA((n_sems,)), pltpu.SemaphoreType.DMA((n_sems,)),
                   *[pltpu.HBM(b.shape, b.dtype) for b in bufs], SDS((8, 128), F32)),
        in_specs=[HBM_SPEC] * n + [SEM_SPEC] * len(held) + [ANY_SPEC] * len(deps),
        out_specs=(SEM_SPEC, SEM_SPEC, *[HBM_SPEC] * n, pl.BlockSpec(memory_space=pltpu.VMEM)),
        input_output_aliases={i: 2 + i for i in range(n)},
        compiler_params=pltpu.CompilerParams(has_side_effects=EFFECT),
    )(*[pltpu.with_memory_space_constraint(b, pltpu.HBM) for b in bufs], *held, *deps)
    return out[0], out[1], list(out[2:2 + n]), out[-1]


def _split_wait(name, plan, send_sems, recv_sems, bufs, after=()):
    n = len(bufs)

    def body(*refs):
        sends, recvs = plan(refs[:n], refs[n], refs[n + 1])
        for kw in sends:
            pltpu.make_async_remote_copy(**kw).wait_send()
        for kw in recvs:
            pltpu.make_async_remote_copy(**kw).wait_recv()

    out = pl.pallas_call(
        body, name=name, out_shape=tuple(pltpu.HBM(b.shape, b.dtype) for b in bufs),
        in_specs=[HBM_SPEC] * n + [SEM_SPEC, SEM_SPEC] + [ANY_SPEC] * len(after),
        out_specs=tuple([HBM_SPEC] * n), input_output_aliases={i: i for i in range(n)},
        compiler_params=pltpu.CompilerParams(has_side_effects=EFFECT),
    )(*bufs, send_sems, recv_sems, *after)
    return list(out)


def _region(kind, ref, chip, half):
    K, N = ref.shape
    if kind == "col":
        return ref.at[pl.ds(half * (K // 2), K // 2), pl.ds(chip * (N // N_CHIPS), N // N_CHIPS)]
    rows = K // (2 * N_CHIPS)
    return ref.at[pl.ds((2 * chip + half) * rows, rows), :]


def _gather_plan(kinds, over_chips):
    def plan(refs, send_sems, recv_sems):
        x, y, c, chips = _position()
        sends, recvs = [], []
        for f, (ref, kind) in enumerate(zip(refs, kinds)):
            for k, chip in enumerate(chips):
                theirs = 2 * chip[0] + chip[1]
                sem = dict(send_sem=send_sems.at[3 * f + k], recv_sem=recv_sems.at[3 * f + k], device_id_type=MESH)
                if over_chips:
                    out, back, to = _region(kind, ref, 2 * x + y, c), _region(kind, ref, theirs, c), (*chip, c)
                else:
                    out, back, to = _region(kind, ref, theirs, c), _region(kind, ref, theirs, 1 - c), (x, y, 1 - c)
                sends.append(dict(src_ref=out, dst_ref=out, device_id=to, **sem))
                recvs.append(dict(src_ref=back, dst_ref=back, device_id=to, **sem))
        return sends, recvs
    return plan


def _reduce_plan(refs, send_sems, recv_sems):
    x, y, c, _ = _position()
    me = 4 * x + 2 * y + c
    sends, recvs = [], []
    for f in range(len(refs) // 2):
        acc, land = refs[2 * f], refs[2 * f + 1]
        for d in range(1, N_DEV):
            t = (me + d) % N_DEV
            to = dict(device_id=(t // 4, (t // 2) % 2, t % 2), device_id_type=MESH)
            slot = N_DEV - 1 - d
            sends.append(dict(src_ref=acc.at[t % 2, t // 2], dst_ref=land.at[slot], send_sem=send_sems.at[7 * f + d - 1],
                              recv_sem=recv_sems.at[7 * f + slot], **to))
            recvs.append(dict(src_ref=land.at[d - 1], dst_ref=land.at[d - 1], send_sem=send_sems.at[7 * f + d - 1],
                              recv_sem=recv_sems.at[7 * f + d - 1], **to))
    return sends, recvs


def _swap_plan(refs, send_sems, recv_sems):
    x, y, c, _ = _position()
    sends, recvs = [], []
    for f, g in enumerate(refs):
        sem = dict(send_sem=send_sems.at[f], recv_sem=recv_sems.at[f], device_id=(x, y, 1 - c), device_id_type=MESH)
        sends.append(dict(src_ref=g.at[c], dst_ref=g.at[c], **sem))
        recvs.append(dict(src_ref=g.at[1 - c], dst_ref=g.at[1 - c], **sem))
    return sends, recvs


def _sum_pieces(ids2, acc, land, name):
    _, _, nr, nc = acc.shape
    tr = _tile(nr, 256, 16)

    def body(ids_ref, own_ref, land_ref, o_ref):
        del ids_ref
        s = own_ref[...].astype(F32)
        for k in range(N_DEV - 1):
            s = s + land_ref[k].astype(F32)
        o_ref[...] = s

    return pl.pallas_call(
        body,
        grid_spec=pltpu.PrefetchScalarGridSpec(
            num_scalar_prefetch=1, grid=(nr // tr,),
            in_specs=[pl.BlockSpec((None, None, tr, nc), lambda i, ids: (ids[0], ids[1], i, 0)),
                      pl.BlockSpec((N_DEV - 1, tr, nc), lambda i, ids: (0, i, 0))],
            out_specs=pl.BlockSpec((None, tr, nc), lambda i, ids: (ids[0], i, 0))),
        out_shape=SDS((2, nr, nc), F32), name=name, compiler_params=_params("parallel"))(ids2, acc, land)


def _small_plan(refs, send_sems, recv_sems):
    x, y, c, _ = _position()
    me = 4 * x + 2 * y + c
    own, land = refs
    sends, recvs = [], []
    for d in range(1, N_DEV):
        t = (me + d) % N_DEV
        to = dict(device_id=(t // 4, (t // 2) % 2, t % 2), device_id_type=MESH)
        sends.append(dict(src_ref=own, dst_ref=land.at[me], send_sem=send_sems.at[d - 1],
                          recv_sem=recv_sems.at[N_DEV - 1 - d], **to))
        recvs.append(dict(src_ref=land.at[t], dst_ref=land.at[t], send_sem=send_sems.at[d - 1],
                          recv_sem=recv_sems.at[d - 1], **to))
    return sends, recvs


def _sum_blocks(me1, own, land):
    def body(me_ref, own_ref, land_ref, o_ref):
        acc = None
        for d in range(N_DEV):
            term = jnp.where(me_ref[0] == d, own_ref[...], land_ref[d])
            acc = term if acc is None else acc + term
        o_ref[...] = acc

    return pl.pallas_call(
        body,
        grid_spec=pltpu.PrefetchScalarGridSpec(
            num_scalar_prefetch=1, grid=(1,),
            in_specs=[pl.BlockSpec(own.shape, lambda i, me: (0, 0)), pl.BlockSpec(land.shape, lambda i, me: (0, 0, 0))],
            out_specs=pl.BlockSpec(own.shape, lambda i, me: (0, 0))),
        out_shape=SDS(own.shape, F32), name="sum_small", compiler_params=_params("arbitrary"))(me1, own, land)


BIG = {"ev_w_in": "col", "ev_w_out": "row", "od_w_in": "col", "od_w_out": "row", "mlp_w1": "col", "mlp_w2": "row"}
WEIGHTS = ("meta_tokens", "mix_norm_g", "mlp_norm_g", "final_norm_g", "ev_w_in", "ev_conv_w", "ev_conv_b", "ev_ln_g",
           "ev_ln_b", "ev_pool_w", "ev_pool_b", "ev_pool_scale", "ev_w_out", "od_w_in", "od_gnorm_g", "od_w_out",
           "lb_param", "mlp_w1", "mlp_w2")
PACK_UNIT = 1024


def _mixer_names(layer):
    return ("ev_w_in", "ev_w_out") if layer % 2 == 0 else ("od_w_in", "od_w_out")


def _pack(arrays):
    flat = []
    for a in arrays:
        a = a.reshape(-1)
        flat.append(jnp.pad(a, (0, (-a.shape[0]) % PACK_UNIT)))
    return jnp.concatenate(flat).reshape(-1, 128)


def _unpack(packed, shapes):
    flat = packed.reshape(-1)
    out, off = [], 0
    for s in shapes:
        size = 1
        for d in s:
            size *= d
        out.append(flat[off:off + size].reshape(s))
        off += size + (-size) % PACK_UNIT
    return out


def _local_step(x2, target, P, weights, boundary, first_deps=()):
    D = x2.shape[1]
    n_layers = P["mix_norm_g"].shape[0]
    h = jnp.concatenate([jnp.zeros((PAD, D), F32), P["meta_full"], x2], axis=0)
    mix_g = P["mix_norm_g"].reshape(n_layers, 1, D)
    mlp_g = P["mlp_norm_g"].reshape(n_layers, 1, D)
    vec = lambda a: a.reshape(a.shape[0], 1, -1)
    cb3, lg3, lnb3, ps3 = vec(P["ev_conv_b"]), vec(P["ev_ln_g"]), vec(P["ev_ln_b"]), vec(P["ev_pool_scale"])
    pb3 = vec(P["ev_pool_b"])
    gn3 = vec(P["od_gnorm_g"])
    lb_all = _lb_fwd(P["lb_param"])
    lb3 = lb_all.reshape(n_layers, 1, D)
    even = (cb3, lg3, lnb3, P["ev_pool_w"], pb3, ps3)

    saved = []
    deps = tuple(first_deps)
    for layer in range(n_layers):
        j = layer // 2
        w_in, w_out = _mixer_names(layer)
        W = {}
        s = {"h": h, "W": W}
        s["n"] = _rms_fwd(h, mix_g, layer, "mix_norm_0", deps=deps) if layer == 0 else n_next
        deps = ()
        W[w_in], held = weights(layer, w_in, (s["n"],))
        s["u"] = _mm_nn(s["n"], W[w_in], 0, f"mix_in_{layer}", deps=held)
        if layer % 2 == 0:
            s["y"], s["yc"] = _even_fwd(s["u"], P["conv_w_full"], *even, j, f"even_fwd_{layer}")
        else:
            s["y"], s["o"], s["sall"] = _hgrn_fwd(s["u"], lb3, layer, gn3, j, f"hgrn_fwd_{layer}")
        W[w_out], held = weights(layer, w_out, (s["y"],))
        h, s["n2"] = _mm_nn_norm(s["y"], W[w_out], 0, h, mlp_g, layer, f"mix_out_{layer}", deps=held)
        s["h1"] = h
        W["mlp_w1"], held = weights(layer, "mlp_w1", (s["n2"],))
        s["relu"] = _mm_nn(s["n2"], W["mlp_w1"], 0, f"mlp_up_{layer}", relu=True, deps=held)
        W["mlp_w2"], held = weights(layer, "mlp_w2", (s["relu"],))
        if layer + 1 < n_layers:
            h, n_next = _mm_nn_norm(s["relu"], W["mlp_w2"], 0, h, mix_g, layer + 1, f"mlp_down_{layer}", square=True, deps=held)
        else:
            h = _mm_nn(s["relu"], W["mlp_w2"], 0, f"mlp_down_{layer}", res=h, square=True, deps=held)
        saved.append(s)

    dh, dhb, dg_final, loss = _final(h, P["final_norm_g"].reshape(1, D), target)

    small = {"final_norm_g": dg_final}
    per_layer = {k: [None] * n_layers for k in ("mix_norm_g", "mlp_norm_g", "lb")}
    per_pair = {k: [None] * (n_layers // 2) for k in
                ("ev_conv_w", "ev_conv_b", "ev_ln_g", "ev_ln_b", "ev_pool_w", "ev_pool_b", "ev_pool_scale", "od_gnorm_g")}
    for layer in reversed(range(n_layers)):
        j = layer // 2
        s = saved[layer]
        W = s["W"]
        w_in, w_out = _mixer_names(layer)
        dz = _mm_nt(dhb, W["mlp_w2"], 0, f"d_act_{layer}", relu=s["relu"], deps=deps)
        dw2 = _mm_tn(s["relu"], dhb, "row", f"dw2_{layer}", square=True)
        dw1 = _mm_tn(s["n2"], dz, "col", f"dw1_{layer}")
        dh, dhb, per_layer["mlp_norm_g"][layer] = _mm_nt_norm(dz, W["mlp_w1"], 0, s["h1"], mlp_g, layer, dh, f"d_n2_{layer}")
        deps = boundary(f"mlp{layer}", {("mlp_w1", layer): dw1, ("mlp_w2", layer): dw2}, (dhb, dw1, dw2))
        dy = _mm_nt(dhb, W[w_out], 0, f"d_y_{layer}", deps=deps)
        dwout = _mm_tn(s["y"], dhb, "row", f"dwout_{layer}")
        if layer % 2 == 0:
            du, dcw, dcb, dlg, dlnb, dpw, dpb, dps = _even_bwd(s["u"], s["yc"], dy, P["conv_w_full"], *even, j, f"even_bwd_{layer}")
            for k, val in (("ev_conv_w", dcw), ("ev_conv_b", dcb), ("ev_ln_g", dlg), ("ev_ln_b", dlnb),
                           ("ev_pool_w", dpw), ("ev_pool_b", dpb), ("ev_pool_scale", dps)):
                per_pair[k][j] = val
        else:
            du, per_layer["lb"][layer], per_pair["od_gnorm_g"][j] = _hgrn_bwd(
                s["u"], s["o"], dy, s["sall"], lb3, layer, gn3, j, f"hgrn_bwd_{layer}")
        dwin = _mm_tn(s["n"], du, "col", f"dwin_{layer}")
        deps = boundary(f"mix{layer}", {(w_in, j): dwin, (w_out, j): dwout}, (du, dwin, dwout))
        dh, dhb, per_layer["mix_norm_g"][layer] = _mm_nt_norm(du, W[w_in], 0, s["h"], mix_g, layer, dh, f"d_n_{layer}", deps=deps)
        deps = ()

    small["mix_norm_g"] = jnp.concatenate(per_layer["mix_norm_g"], axis=0)
    small["mlp_norm_g"] = jnp.concatenate(per_layer["mlp_norm_g"], axis=0)
    dlb_all = jnp.concatenate([jnp.zeros((1, D), F32) if g is None else g for g in per_layer["lb"]], axis=0)
    small["lb_param"] = _lb_bwd(P["lb_param"], dlb_all)
    for k, vals in per_pair.items():
        small[k] = jnp.stack(vals, axis=0)
    small["meta_tokens"] = dh[PAD:LEAD]
    return loss, dh, small


def kernel(x, meta_tokens, mix_norm_g, mlp_norm_g, final_norm_g, ev_w_in, ev_conv_w, ev_conv_b, ev_ln_g, ev_ln_b, ev_pool_w, ev_pool_b, ev_pool_scale, ev_w_out, od_w_in, od_gnorm_g, od_w_out, lb_param, mlp_w1, mlp_w2, loss_target, m_meta_tokens, m_mix_norm_g, m_mlp_norm_g, m_final_norm_g, m_ev_w_in, m_ev_conv_w, m_ev_conv_b, m_ev_ln_g, m_ev_ln_b, m_ev_pool_w, m_ev_pool_b, m_ev_pool_scale, m_ev_w_out, m_od_w_in, m_od_gnorm_g, m_od_w_out, m_lb_param, m_mlp_w1, m_mlp_w2, v_meta_tokens, v_mix_norm_g, v_mlp_norm_g, v_final_norm_g, v_ev_w_in, v_ev_conv_w, v_ev_conv_b, v_ev_ln_g, v_ev_ln_b, v_ev_pool_w, v_ev_pool_b, v_ev_pool_scale, v_ev_w_out, v_od_w_in, v_od_gnorm_g, v_od_w_out, v_lb_param, v_mlp_w1, v_mlp_w2):
    given = dict(locals())
    w = {n: given[n] for n in WEIGHTS}
    m = {n: given["m_" + n] for n in WEIGHTS}
    v = {n: given["v_" + n] for n in WEIGHTS}
    n_layers = mix_norm_g.shape[0]
    core = lax.axis_index("c").astype(jnp.int32)
    chip = (2 * lax.axis_index("x") + lax.axis_index("y")).astype(jnp.int32)
    chip1 = chip.reshape(1)
    ids2 = jnp.stack([core, chip])

    conv_pad = jnp.pad(ev_conv_w, ((0, 0), (0, CONV_ROWS - CONV_WIDTH), (0, 0)))
    stages = [[(0, n)] for n in (*_mixer_names(0), "mlp_w1", "mlp_w2")]
    stages += [[(layer, n) for n in (*_mixer_names(layer), "mlp_w1", "mlp_w2")] for layer in range(1, n_layers)]
    gathers, where, token = [], {}, ()
    for k, stage in enumerate(stages):
        index = [layer if n.startswith("mlp") else layer // 2 for layer, n in stage]
        kinds = [BIG[n] for _, n in stage]
        bufs = [_cast_place(w[n], i, BIG[n], chip1, BF16, f"place_{n}_{i}") for (_, n), i in zip(stage, index)]
        if k == 0:
            bufs.append(_cast_place(meta_tokens[None], 0, "col", chip1, F32, "place_meta"))
            bufs.append(_cast_place(conv_pad.reshape(1, -1, conv_pad.shape[2]), 0, "col", chip1, F32, "place_conv_w"))
            kinds += ["col", "col"]
        plan = _gather_plan(kinds, True)
        ss, rs, bufs, tok = _split_start(f"gather_start_{k}", plan, bufs, 3 * len(bufs), deps=token)
        token = (tok,)
        gathers.append((kinds, plan, ss, rs, bufs))
        where.update({key: (k, f) for f, key in enumerate(stage)})

    landed, passed, held = {}, {}, []

    def hand_on(k, deps):
        if k not in passed:
            kinds, plan, ss, rs, bufs = gathers[k]
            to_sibling = _gather_plan(kinds, False)
            ss, rs, bufs, tok = _split_start(f"gather_pass_{k}", to_sibling, bufs, 3 * len(bufs), deps=deps, earlier=(plan, ss, rs))
            passed[k] = (to_sibling, ss, rs, bufs)
            held.append(tok)

    def arrived(k, after):
        if k not in landed:
            hand_on(k, after)
            landed[k] = _split_wait(f"gather_wait_{k}", *passed[k], after)
        return landed[k]

    def weights(layer, name, after):
        k, f = where[(layer, name)]
        full = arrived(k, after)[f][None]
        if name == "mlp_w2" and layer + 1 < n_layers:
            hand_on(where[(layer + 1, "mlp_w2")][0], after)
        tokens = tuple(held)
        held.clear()
        return full, tokens

    first = arrived(0, token)
    P = {n: w[n] for n in ("mix_norm_g", "mlp_norm_g", "final_norm_g", "ev_conv_b", "ev_ln_g", "ev_ln_b", "ev_pool_w",
                           "ev_pool_b", "ev_pool_scale", "od_gnorm_g", "lb_param")}
    P["meta_full"] = first[1]
    P["conv_w_full"] = first[2].reshape(ev_conv_w.shape[0], CONV_ROWS, -1)

    pending, outs = [], {n: None for n in BIG}

    def advance(after):
        tokens, still = [], []
        for st in pending:
            if st["phase"] == 1:
                bufs = _split_wait(f"reduce_wait_{st['tag']}", _reduce_plan, st["ss"], st["rs"], st["bufs"], after)
                halves = [_sum_pieces(ids2, bufs[2 * f], bufs[2 * f + 1], f"sum_{st['tag']}_{f}") for f in range(len(bufs) // 2)]
                ss, rs, halves, tok = _split_start(f"swap_start_{st['tag']}", _swap_plan, halves, len(halves))
                tokens.append(tok)
                still.append(dict(st, phase=2, ss=ss, rs=rs, bufs=halves))
            else:
                grads = _split_wait(f"swap_wait_{st['tag']}", _swap_plan, st["ss"], st["rs"], st["bufs"], after)
                for (n, i), g in zip(st["keys"], grads):
                    outs[n] = _adamw_layer(w[n], m[n], v[n], g.reshape(w[n].shape[1:]), i, outs[n], f"adamw_{n}_{i}")
        pending[:] = still
        return tokens

    def boundary(tag, grads, after):
        tokens = advance(after)
        bufs = []
        for acc in grads.values():
            bufs += [acc, lax.empty((N_DEV - 1,) + acc.shape[2:], BF16)]
        ss, rs, bufs, tok = _split_start(f"reduce_start_{tag}", _reduce_plan, bufs, 7 * len(grads))
        pending.append(dict(phase=1, tag=tag, keys=list(grads), ss=ss, rs=rs, bufs=bufs))
        return tuple(tokens + [tok])

    loss, dh, small = _local_step(x[0], loss_target[0], P, weights, boundary, first_deps=token)

    order = [n for n in WEIGHTS if n not in BIG]
    block = _pack([small[n] for n in order] + [loss])
    ss, rs, bufs, tok = _split_start("small_start", _small_plan, [block, lax.empty((N_DEV,) + block.shape, F32)], N_DEV - 1)
    advance((tok,))
    advance((tok,))
    block, land = _split_wait("small_wait", _small_plan, ss, rs, bufs, tuple(outs[n][0] for n in BIG))
    packed = _sum_blocks((4 * lax.axis_index("x") + 2 * lax.axis_index("y") + lax.axis_index("c")).astype(jnp.int32).reshape(1), block, land)
    total = _unpack(packed, [small[n].shape for n in order] + [loss.shape])
    loss_sum = total[-1][0, 0]
    gsmall = dict(zip(order, total[:-1]))
    gsmall["meta_tokens"] = lax.dynamic_slice_in_dim(gsmall["meta_tokens"], chip * meta_tokens.shape[1], meta_tokens.shape[1], 1)
    gsmall["ev_conv_w"] = lax.dynamic_slice_in_dim(gsmall["ev_conv_w"][:, :CONV_WIDTH], chip * ev_conv_w.shape[2], ev_conv_w.shape[2], 2)

    g_out, d_out, m_out, v_out = {}, {}, {}, {}
    for n in WEIGHTS:
        if n in BIG:
            g_out[n], d_out[n], m_out[n], v_out[n] = outs[n]
            continue
        shape = w[n].shape
        g = gsmall[n].reshape(shape)
        cols = shape[-1] if len(shape) > 1 else 128
        two = lambda a: a.reshape(-1, cols)
        d_, m_, v_ = _adamw(two(w[n]), two(g), two(m[n]), two(v[n]), f"adamw_{n}")
        g_out[n], d_out[n], m_out[n], v_out[n] = g, d_.reshape(shape), m_.reshape(shape), v_.reshape(shape)

    grad_x = dh[LEAD:][None]
    return (loss_sum, grad_x, *[g_out[n] for n in WEIGHTS], *[d_out[n] for n in WEIGHTS],
            *[m_out[n] for n in WEIGHTS], *[v_out[n] for n in WEIGHTS])
```

```python
import functools

import jax
import jax.numpy as jnp
from jax import lax
from jax.experimental import pallas as pl
from jax.experimental.pallas import tpu as pltpu

F32 = jnp.float32
BF16 = jnp.bfloat16
SDS = jax.ShapeDtypeStruct
MESH = pl.DeviceIdType.MESH
ANY_SPEC = pl.BlockSpec(memory_space=pl.ANY)

N_META = 16
CHUNK = 64
LEAD = CHUNK
PAD = LEAD - N_META
CONV_WIDTH = 31
CONV_ROWS = 32
POOL_WINDOWS = (2, 4, 8, 16)
HEAD_DIM = 128
SUB = 16
EXP_CAP = 80.0
EPS = 1e-6
ADAM_LR = 0.001
ADAM_B1 = 0.9
ADAM_B2 = 0.999
ADAM_EPS = 1e-08
ADAM_WD = 0.01
ADAM_STEP = 10
N_CHIPS = 4
VMEM_LIMIT = 52 << 20
MM_VMEM_BUDGET = 44 << 20


def _params(*sem):
    return pltpu.CompilerParams(dimension_semantics=sem if sem else None, vmem_limit_bytes=VMEM_LIMIT)


def _tile(n, target, unit=CHUNK):
    best = None
    for t in range(unit, min(n, target) + 1, unit):
        if n % t == 0:
            best = t
    assert best is not None, (n, target, unit)
    return best


def _ctile(n, target=512):
    for t in (512, 384, 256, 128):
        if t <= target and n % t == 0:
            return t
    raise ValueError(n)


def _mm_tiles(M, N, per_row, per_col, per_elem):
    best = None
    for tn in (512, 384, 256, 128):
        if N % tn:
            continue
        for tm in sorted((d for d in range(16, M + 1, 16) if M % d == 0), reverse=True):
            if 2 * (tm * per_row + tn * per_col + tm * tn * per_elem) <= MM_VMEM_BUDGET:
                if best is None or tm * tn > best[0] * best[1]:
                    best = (tm, tn)
                break
    assert best is not None, (M, N)
    return best


def _sigmoid(x):
    return 1.0 / (1.0 + jnp.exp(-x))


def _row_ids(shape, base):
    return lax.broadcasted_iota(jnp.int32, shape, 0) + base


def _cast_place(w3, layer, kind, chip1, dtype, name):
    _, ks, ns = w3.shape
    tr = _tile(ks, 512, 16)
    full = (ks, ns * N_CHIPS) if kind == "col" else (ks * N_CHIPS, ns)

    def body(chip_ref, w_ref, o_ref):
        del chip_ref
        o_ref[...] = w_ref[...].astype(dtype)

    omap = (lambda i, chip: (i, chip[0])) if kind == "col" else (lambda i, chip: (chip[0] * (ks // tr) + i, 0))
    return pl.pallas_call(
        body,
        grid_spec=pltpu.PrefetchScalarGridSpec(
            num_scalar_prefetch=1, grid=(ks // tr,),
            in_specs=[pl.BlockSpec((None, tr, ns), lambda i, chip: (layer, i, 0))],
            out_specs=pl.BlockSpec((tr, ns), omap)),
        out_shape=SDS(full, dtype), name=name, compiler_params=_params("parallel"))(chip1, w3)


def _rms_fwd(h, g3, layer, name, deps=()):
    T, D = h.shape
    tm = _tile(T, 832)

    def body(h_ref, g_ref, *rest):
        n_ref = rest[-1]
        x = h_ref[...]
        r = lax.rsqrt(jnp.mean(x * x, axis=-1, keepdims=True) + EPS)
        n_ref[...] = ((x * r) * g_ref[...]).astype(BF16)

    return pl.pallas_call(
        body, grid=(T // tm,),
        in_specs=[pl.BlockSpec((tm, D), lambda i: (i, 0)), pl.BlockSpec((None, 1, D), lambda i: (layer, 0, 0))]
        + [ANY_SPEC] * len(deps),
        out_specs=pl.BlockSpec((tm, D), lambda i: (i, 0)), out_shape=SDS((T, D), BF16),
        name=name, compiler_params=_params("parallel"))(h, g3, *deps)


def _final(h, g2, target):
    T, D = h.shape
    tm = _tile(T, 320)
    nsub = tm // CHUNK
    nblk = target.shape[0] // CHUNK

    def body(h_ref, g_ref, *rest):
        t_refs = rest[:nsub]
        dh_ref, dhb_ref, dg_ref, loss_ref = rest[nsub:]
        i = pl.program_id(0)

        @pl.when(i == 0)
        def _():
            dg_ref[...] = jnp.zeros_like(dg_ref)
            loss_ref[...] = jnp.zeros_like(loss_ref)

        g = g_ref[...]
        for q in range(nsub):
            rows = slice(q * CHUNK, (q + 1) * CHUNK)
            x = h_ref[rows, :]
            r = lax.rsqrt(jnp.mean(x * x, axis=-1, keepdims=True) + EPS)
            xh = x * r
            live = jnp.where(i * nsub + q > 0, 1.0, 0.0).astype(F32)
            e = ((xh * g) - t_refs[q][...]) * live
            dy = e * (1.0 / D)
            dxh = dy * g
            dh = r * (dxh - xh * jnp.mean(dxh * xh, axis=-1, keepdims=True))
            dh_ref[rows, :] = dh
            dhb_ref[rows, :] = dh.astype(BF16)
            dg_ref[...] += jnp.sum(dy * xh, axis=0, keepdims=True)
            loss_ref[...] += jnp.sum(e * e) * (0.5 / D)

    row = pl.BlockSpec((tm, D), lambda i: (i, 0))
    t_specs = [pl.BlockSpec((CHUNK, D), functools.partial(lambda i, q: (jnp.clip(i * nsub + q - 1, 0, nblk - 1), 0), q=q))
               for q in range(nsub)]
    return pl.pallas_call(
        body, grid=(T // tm,),
        in_specs=[row, pl.BlockSpec((1, D), lambda i: (0, 0))] + t_specs,
        out_specs=[row, row, pl.BlockSpec((1, D), lambda i: (0, 0)), pl.BlockSpec((1, 128), lambda i: (0, 0))],
        out_shape=[SDS((T, D), F32), SDS((T, D), BF16), SDS((1, D), F32), SDS((1, 128), F32)],
        name="final_loss", compiler_params=_params("arbitrary"))(h, g2, *([target] * nsub))


def _mm_nn(a, w3, layer, name, res=None, relu=False, square=False, deps=()):
    M, K = a.shape
    N = w3.shape[2]
    tm, tn = _mm_tiles(M, N, 2 * K, 2 * K, (2 if relu else 4) + (4 if res is not None else 0))

    def body(*refs):
        lhs = refs[0][...]
        acc = jnp.dot(lhs * lhs if square else lhs, refs[1][...], preferred_element_type=F32)
        if res is not None:
            acc = acc + refs[2][...]
        refs[-1][...] = jnp.maximum(acc, 0.0).astype(BF16) if relu else acc

    in_specs = [pl.BlockSpec((tm, K), lambda i, j: (i, 0)), pl.BlockSpec((None, K, tn), lambda i, j: (layer, 0, j))]
    args = [a, w3]
    tile = pl.BlockSpec((tm, tn), lambda i, j: (i, j))
    if res is not None:
        in_specs.append(tile)
        args.append(res)
    in_specs += [ANY_SPEC] * len(deps)
    args += list(deps)
    return pl.pallas_call(
        body, grid=(M // tm, N // tn), in_specs=in_specs, out_specs=tile,
        out_shape=SDS((M, N), BF16 if relu else F32),
        name=name, compiler_params=_params("parallel", "parallel"))(*args)


def _mm_nt(dy, w3, layer, name, relu=None, deps=()):
    M, N = dy.shape
    K = w3.shape[1]
    tm, tk = _mm_tiles(M, K, 2 * N, 2 * N, 4)

    def body(*refs):
        acc = lax.dot_general(refs[0][...], refs[1][...], (((1,), (1,)), ((), ())), preferred_element_type=F32)
        if relu is not None:
            acc = (acc * (2.0 * refs[2][...].astype(F32))).astype(BF16)
        refs[-1][...] = acc

    tile = pl.BlockSpec((tm, tk), lambda i, j: (i, j))
    in_specs = [pl.BlockSpec((tm, N), lambda i, j: (i, 0)), pl.BlockSpec((None, tk, N), lambda i, j: (layer, j, 0))]
    args = [dy, w3]
    if relu is not None:
        in_specs.append(tile)
        args.append(relu)
    in_specs += [ANY_SPEC] * len(deps)
    args += list(deps)
    return pl.pallas_call(
        body, grid=(M // tm, K // tk), in_specs=in_specs, out_specs=tile,
        out_shape=SDS((M, K), F32 if relu is None else BF16),
        name=name, compiler_params=_params("parallel", "parallel"))(*args)


def _row_tile(M, per_row, fixed):
    for tm in sorted((d for d in range(16, M + 1, 16) if M % d == 0), reverse=True):
        if 2 * (tm * per_row + fixed) <= MM_VMEM_BUDGET:
            return tm
    raise ValueError((M, per_row, fixed))


def _mm_nn_norm(a, w3, layer, res, g3, glayer, name, square=False, deps=()):
    M, K = a.shape
    D = w3.shape[2]
    tm = _row_tile(M, 2 * K + 10 * D, 2 * K * D)

    def body(a_ref, w_ref, r_ref, g_ref, *rest):
        h_ref, n_ref = rest[-2:]
        lhs = a_ref[...]
        x = r_ref[...] + jnp.dot(lhs * lhs if square else lhs, w_ref[...], preferred_element_type=F32)
        h_ref[...] = x
        r = lax.rsqrt(jnp.mean(x * x, axis=-1, keepdims=True) + EPS)
        n_ref[...] = ((x * r) * g_ref[...]).astype(BF16)

    row = pl.BlockSpec((tm, D), lambda i: (i, 0))
    return pl.pallas_call(
        body, grid=(M // tm,),
        in_specs=[pl.BlockSpec((tm, K), lambda i: (i, 0)), pl.BlockSpec((None, K, D), lambda i: (layer, 0, 0)), row,
                  pl.BlockSpec((None, 1, D), lambda i: (glayer, 0, 0))] + [ANY_SPEC] * len(deps),
        out_specs=[row, row], out_shape=[SDS((M, D), F32), SDS((M, D), BF16)],
        name=name, compiler_params=_params("parallel"))(a, w3, res, g3, *deps)


def _mlp_fwd(n2, w1, w2, res, g3, glayer, name, deps=()):
    M, D = n2.shape
    F = w1.shape[2]
    hb = _ctile(F)
    tm = _row_tile(M, 2 * D + 8 * D + (2 * D if g3 is not None else 0) + 2 * F, 2 * D * F)

    def body(n_ref, w1_ref, w2_ref, res_ref, *rest):
        outs = rest[-3:] if g3 is not None else rest[-2:]
        x = n_ref[...]
        acc = res_ref[...]
        for jb in range(F // hb):
            cols = slice(jb * hb, (jb + 1) * hb)
            r = jnp.maximum(jnp.dot(x, w1_ref[:, cols], preferred_element_type=F32), 0.0).astype(BF16)
            outs[-1][:, cols] = r
            acc = acc + jnp.dot(r * r, w2_ref[cols, :], preferred_element_type=F32)
        outs[0][...] = acc
        if g3 is not None:
            rr = lax.rsqrt(jnp.mean(acc * acc, axis=-1, keepdims=True) + EPS)
            outs[1][...] = ((acc * rr) * rest[0][...]).astype(BF16)

    row = pl.BlockSpec((tm, D), lambda i: (i, 0))
    once = dict(pipeline_mode=pl.Buffered(1))
    in_specs = [row, pl.BlockSpec((None, D, F), lambda i: (0, 0, 0), **once), pl.BlockSpec((None, F, D), lambda i: (0, 0, 0), **once), row]
    args = [n2, w1, w2, res]
    out_specs, out_shape = [row], [SDS((M, D), F32)]
    if g3 is not None:
        in_specs.append(pl.BlockSpec((None, 1, D), lambda i: (glayer, 0, 0)))
        args.append(g3)
        out_specs.append(row)
        out_shape.append(SDS((M, D), BF16))
    out_specs.append(pl.BlockSpec((tm, F), lambda i: (i, 0)))
    out_shape.append(SDS((M, F), BF16))
    in_specs += [ANY_SPEC] * len(deps)
    args += list(deps)
    return pl.pallas_call(
        body, grid=(M // tm,), in_specs=in_specs, out_specs=out_specs, out_shape=out_shape,
        name=name, compiler_params=_params("parallel"))(*args)


def _mm_nt_norm(dy, w3, layer, h, g3, glayer, dh_in, name, deps=()):
    M, N = dy.shape
    D = w3.shape[1]
    tm = _row_tile(M, 2 * N + 14 * D, 2 * N * D)

    def body(dy_ref, w_ref, h_ref, g_ref, dhi_ref, *rest):
        dh_ref, dhb_ref, dg_ref = rest[-3:]
        dn = lax.dot_general(dy_ref[...], w_ref[...], (((1,), (1,)), ((), ())), preferred_element_type=F32)
        x = h_ref[...]
        r = lax.rsqrt(jnp.mean(x * x, axis=-1, keepdims=True) + EPS)
        xh = x * r
        dxh = dn * g_ref[...]
        dh = dhi_ref[...] + r * (dxh - xh * jnp.mean(dxh * xh, axis=-1, keepdims=True))
        dh_ref[...] = dh
        dhb_ref[...] = dh.astype(BF16)

        @pl.when(pl.program_id(0) == 0)
        def _():
            dg_ref[...] = jnp.zeros_like(dg_ref)

        dg_ref[...] += jnp.sum(dn * xh, axis=0, keepdims=True)

    row = pl.BlockSpec((tm, D), lambda i: (i, 0))
    return pl.pallas_call(
        body, grid=(M // tm,),
        in_specs=[pl.BlockSpec((tm, N), lambda i: (i, 0)), pl.BlockSpec((None, D, N), lambda i: (layer, 0, 0)), row,
                  pl.BlockSpec((None, 1, D), lambda i: (glayer, 0, 0)), row] + [ANY_SPEC] * len(deps),
        out_specs=[row, row, pl.BlockSpec((1, D), lambda i: (0, 0))],
        out_shape=[SDS((M, D), F32), SDS((M, D), BF16), SDS((1, D), F32)],
        name=name, compiler_params=_params("arbitrary"))(dy, w3, h, g3, dh_in, *deps)


def _fam_dims(kind, K, N):
    return (K // 2, N // N_CHIPS) if kind == "col" else (K // (2 * N_CHIPS), N)


def _mm_tn(x, dy, kind, name, square=False):
    M, K = x.shape
    N = dy.shape[1]
    nr, nc = _fam_dims(kind, K, N)

    def body(x_ref, dy_ref, o_ref):
        lhs = x_ref[...]
        res = lax.dot_general(lhs * lhs if square else lhs, dy_ref[...], (((0,), (0,)), ((), ())), preferred_element_type=F32)
        o_ref[...] = res.astype(BF16).reshape(o_ref.shape)

    if kind == "col":
        tn = _ctile(nc)
        ct = nc // tn
        grid = (N // tn,)
        in_specs = [pl.BlockSpec((M, K), lambda j: (0, 0)), pl.BlockSpec((M, tn), lambda j: (0, j))]
        out_spec = pl.BlockSpec((2, None, nr, tn), lambda j: (0, j // ct, 0, j % ct))
    else:
        grid = (N_CHIPS,)
        in_specs = [pl.BlockSpec((M, 2 * nr), lambda i: (0, i)), pl.BlockSpec((M, N), lambda i: (0, 0))]
        out_spec = pl.BlockSpec((2, None, nr, N), lambda i: (0, i, 0, 0))
    return pl.pallas_call(
        body, grid=grid, in_specs=in_specs, out_specs=out_spec, out_shape=SDS((2, N_CHIPS, nr, nc), BF16),
        name=name, compiler_params=_params("parallel"))(x, dy)


C_EVEN = 512


def _live(rows, base, total):
    r = _row_ids((rows, 1), base)
    return jnp.logical_and(r >= PAD, r < total).astype(F32)


def _conv_taps(win, w_ref, ls, acc, flip):
    for b in range(8):
        rb = win if b == 0 else pltpu.roll(win, 96 - b, 0)
        for a in range(5):
            o = 8 * a + b
            tap = (30 - o) if flip else (o - 2)
            if 0 <= tap < CONV_WIDTH:
                acc = acc + w_ref[pl.ds(tap, 1), ls] * rb[8 * a:8 * a + CHUNK]
    return acc


def _window_sum(win, levels, forward):
    s = win
    n = win.shape[0]
    for k in range(levels):
        step = 1 << k
        s = s + pltpu.roll(s, (n - step) if forward else step, 0)
    return s


def _pool_count(base, g):
    pos = _row_ids((CHUNK, 1), base) - PAD
    return jnp.clip(pos + 1, 1, POOL_WINDOWS[g]).astype(F32)


def _even_fwd(u, cw3, cb3, lg3, lb3, pw4, pb3, ps3, j, name):
    T = u.shape[0]
    C = C_EVEN
    tm = _tile(T, 320)
    nch = tm // CHUNK
    nblk = T // CHUNK

    def body(u_ref, up_ref, cw_ref, cb_ref, lg_ref, lb_ref, pw_ref, pb_ref, ps_ref, o_ref, yc_ref, a_s, p_s, yc_s):
        row0 = pl.program_id(0) * tm
        up = up_ref[...]
        lp = _live(CHUNK, row0 - CHUNK, T)
        a_s[0:CHUNK, :] = up[:, 0:C] * _sigmoid(up[:, C:2 * C]) * lp
        p_s[0:CHUNK, :] = up[:, 2 * C:3 * C] * lp

        def stage(c, _):
            rs = pl.multiple_of(c * CHUNK, CHUNK)
            lv = _live(CHUNK, row0 + rs, T)
            a_s[pl.ds(rs + CHUNK, CHUNK), :] = u_ref[pl.ds(rs, CHUNK), 0:C] * _sigmoid(u_ref[pl.ds(rs, CHUNK), C:2 * C]) * lv
            p_s[pl.ds(rs + CHUNK, CHUNK), :] = u_ref[pl.ds(rs, CHUNK), 2 * C:3 * C] * lv
            return 0

        lax.fori_loop(0, nch, stage, 0)

        def chunk(c, _):
            rs = pl.multiple_of(c * CHUNK, CHUNK)
            lv = _live(CHUNK, row0 + rs, T)
            for cb in range(4):
                ls = slice(cb * 128, (cb + 1) * 128)
                win = a_s[pl.ds(pl.multiple_of(rs + 32, 32), 96), ls]
                acc = jnp.broadcast_to(cb_ref[:, ls], (CHUNK, 128))
                yc_s[:, ls] = _conv_taps(win, cw_ref, ls, acc, False)
            y = yc_s[...]
            yc_ref[pl.ds(rs, CHUNK), :] = y
            xc = y - jnp.mean(y, axis=-1, keepdims=True)
            yn = xc * lax.rsqrt(jnp.mean(xc * xc, axis=-1, keepdims=True) + EPS) * lg_ref[...] + lb_ref[...]
            o_ref[pl.ds(rs, CHUNK), 0:C] = (yn * _sigmoid(yn) * lv).astype(BF16)
            for g in range(4):
                ls = slice(g * 128, (g + 1) * 128)
                win = p_s[pl.ds(pl.multiple_of(rs + 48, 16), 80), ls]
                s = _window_sum(win, g + 1, False)
                d = s[16:80] / _pool_count(row0 + rs, g) - win[16:80]
                yv = jnp.dot(d.astype(BF16), pw_ref[g].astype(BF16), preferred_element_type=F32) + pb_ref[:, ls]
                o_ref[pl.ds(rs, CHUNK), C + g * 128:C + (g + 1) * 128] = (yv * ps_ref[:, ls] * lv).astype(BF16)
            return 0

        lax.fori_loop(0, nch, chunk, 0)

    vec = pl.BlockSpec((None, 1, C), lambda i: (j, 0, 0))
    return pl.pallas_call(
        body, grid=(T // tm,),
        in_specs=[pl.BlockSpec((tm, 3 * C), lambda i: (i, 0)),
                  pl.BlockSpec((CHUNK, 3 * C), lambda i: (jnp.maximum(i * nch - 1, 0), 0)),
                  pl.BlockSpec((None, CONV_ROWS, C), lambda i: (j, 0, 0)), vec, vec, vec,
                  pl.BlockSpec((None, 4, 128, 128), lambda i: (j, 0, 0, 0)), vec, vec],
        out_specs=[pl.BlockSpec((tm, 2 * C), lambda i: (i, 0)), pl.BlockSpec((tm, C), lambda i: (i, 0))],
        out_shape=[SDS((T, 2 * C), BF16), SDS((T, C), F32)],
        scratch_shapes=[pltpu.VMEM((tm + CHUNK, C), F32), pltpu.VMEM((tm + CHUNK, C), F32), pltpu.VMEM((CHUNK, C), F32)],
        name=name, compiler_params=_params("parallel"))(u, u, cw3, cb3, lg3, lb3, pw4, pb3, ps3)


def _even_bwd(u, yc, dy, cw3, cb3, lg3, lb3, pw4, pb3, ps3, j, name):
    T = u.shape[0]
    C = C_EVEN
    tm = _tile(T, 320)
    nch = tm // CHUNK
    nblk = T // CHUNK
    ntile = T // tm

    def body(u_ref, up_ref, un_ref, yc_ref, ycn_ref, dy_ref, dyn_ref, cw_ref, cb_ref, lg_ref, lb_ref, pw_ref, pb_ref, ps_ref,
             du_ref, dcw_ref, dcb_ref, dlg_ref, dlb_ref, dpw_ref, dpb_ref, dps_ref,
             a_s, p_s, dy_s, dyc_s, dd_s, ddc_s, dw_s):
        i = pl.program_id(0)
        row0 = i * tm

        @pl.when(i == 0)
        def _():
            for ref in (dcb_ref, dlg_ref, dlb_ref, dpw_ref, dpb_ref, dps_ref, dw_s):
                ref[...] = jnp.zeros_like(ref)

        up = up_ref[...]
        lp = _live(CHUNK, row0 - CHUNK, T)
        a_s[0:CHUNK, :] = up[:, 0:C] * _sigmoid(up[:, C:2 * C]) * lp
        p_s[0:CHUNK, :] = up[:, 2 * C:3 * C] * lp
        ln_ = _live(CHUNK, row0 + tm, T)
        p_s[tm + CHUNK:tm + 2 * CHUNK, :] = un_ref[:, 2 * C:3 * C] * ln_
        dy_s[tm:tm + CHUNK, :] = dyn_ref[...] * ln_
        dyc_s[tm + CHUNK:tm + CHUNK + 32, :] = jnp.zeros((32, C), F32)

        def stage(c, _):
            rs = pl.multiple_of(c * CHUNK, CHUNK)
            lv = _live(CHUNK, row0 + rs, T)
            a_s[pl.ds(rs + CHUNK, CHUNK), :] = u_ref[pl.ds(rs, CHUNK), 0:C] * _sigmoid(u_ref[pl.ds(rs, CHUNK), C:2 * C]) * lv
            p_s[pl.ds(rs + CHUNK, CHUNK), :] = u_ref[pl.ds(rs, CHUNK), 2 * C:3 * C] * lv
            dy_s[pl.ds(rs, CHUNK), :] = dy_ref[pl.ds(rs, CHUNK), :] * lv
            return 0

        lax.fori_loop(0, nch, stage, 0)

        def first(rs, y, own):
            xc = y - jnp.mean(y, axis=-1, keepdims=True)
            rstd = lax.rsqrt(jnp.mean(xc * xc, axis=-1, keepdims=True) + EPS)
            xh = xc * rstd
            yn = xh * lg_ref[...] + lb_ref[...]
            sg = _sigmoid(yn)
            dyn = dy_s[pl.ds(rs, CHUNK), 0:C] * (sg * (1.0 + yn * (1.0 - sg)))
            dlg_ref[...] += jnp.sum(dyn * xh, axis=0, keepdims=True) * own
            dlb_ref[...] += jnp.sum(dyn, axis=0, keepdims=True) * own
            dxh = dyn * lg_ref[...]
            dyc = rstd * (dxh - jnp.mean(dxh, axis=-1, keepdims=True) - xh * jnp.mean(dxh * xh, axis=-1, keepdims=True))
            dyc_s[pl.ds(rs, CHUNK), :] = dyc
            dcb_ref[...] += jnp.sum(dyc, axis=0, keepdims=True) * own
            for g in range(4):
                ls = slice(g * 128, (g + 1) * 128)
                win = p_s[pl.ds(rs + 48, 80), ls]
                s = _window_sum(win, g + 1, False)
                cnt = _pool_count(row0 + rs, g)
                d = (s[16:80] / cnt - win[16:80]).astype(BF16)
                w = pw_ref[g].astype(BF16)
                pre = jnp.dot(d, w, preferred_element_type=F32) + pb_ref[:, ls]
                dyb = dy_s[pl.ds(rs, CHUNK), C + g * 128:C + (g + 1) * 128]
                dpre = dyb * ps_ref[:, ls]
                dps_ref[:, ls] += jnp.sum(dyb * pre, axis=0, keepdims=True) * own
                dpb_ref[:, ls] += jnp.sum(dpre, axis=0, keepdims=True) * own
                dpre_b = (dpre * own).astype(BF16)
                dpw_ref[g] += lax.dot_general(d, dpre_b, (((0,), (0,)), ((), ())), preferred_element_type=F32)
                dd = lax.dot_general(dpre.astype(BF16), w, (((1,), (1,)), ((), ())), preferred_element_type=F32)
                dd_s[pl.ds(rs, CHUNK), ls] = dd
                ddc_s[pl.ds(rs, CHUNK), ls] = dd / cnt

        def first_in_tile(c, _):
            rs = pl.multiple_of(c * CHUNK, 16 * CHUNK // 16)
            first(rs, yc_ref[pl.ds(rs, CHUNK), :], 1.0)
            return 0

        lax.fori_loop(0, nch, first_in_tile, 0)
        first(tm, ycn_ref[...], 0.0)
        ddc_s[tm + CHUNK:tm + CHUNK + 16, :] = jnp.zeros((16, C), F32)

        def second(c, _):
            rs = pl.multiple_of(c * CHUNK, CHUNK)
            lv = _live(CHUNK, row0 + rs, T)
            for cb in range(4):
                ls = slice(cb * 128, (cb + 1) * 128)
                wd = dyc_s[pl.ds(rs, 96), ls]
                da = _conv_taps(wd, cw_ref, ls, jnp.zeros((CHUNK, 128), F32), True)
                wa = a_s[pl.ds(pl.multiple_of(rs + 32, 32), 96), ls]
                dyc = dyc_s[pl.ds(rs, CHUNK), ls]
                for b in range(8):
                    rb = wa if b == 0 else pltpu.roll(wa, 96 - b, 0)
                    for a in range(5):
                        tap = 8 * a + b - 2
                        if 0 <= tap < CONV_WIDTH:
                            prod = dyc * rb[8 * a:8 * a + CHUNK]
                            part = prod[0:8]
                            for q in range(1, 8):
                                part = part + prod[8 * q:8 * q + 8]
                            dw_s[8 * tap:8 * tap + 8, ls] += part
                val = u_ref[pl.ds(rs, CHUNK), ls]
                sg = _sigmoid(u_ref[pl.ds(rs, CHUNK), C + cb * 128:C + (cb + 1) * 128])
                du_ref[pl.ds(rs, CHUNK), ls] = (da * sg * lv).astype(BF16)
                du_ref[pl.ds(rs, CHUNK), C + cb * 128:C + (cb + 1) * 128] = (da * val * sg * (1.0 - sg) * lv).astype(BF16)
            for g in range(4):
                ls = slice(g * 128, (g + 1) * 128)
                z = _window_sum(ddc_s[pl.ds(rs, 80), ls], g + 1, True)
                dpin = (z[0:CHUNK] - dd_s[pl.ds(rs, CHUNK), ls]) * lv
                du_ref[pl.ds(rs, CHUNK), 2 * C + g * 128:2 * C + (g + 1) * 128] = dpin.astype(BF16)
            return 0

        lax.fori_loop(0, nch, second, 0)

        @pl.when(i == ntile - 1)
        def _():
            for tap in range(CONV_WIDTH):
                dcw_ref[tap:tap + 1, :] = jnp.sum(dw_s[8 * tap:8 * tap + 8, :], axis=0, keepdims=True)
            dcw_ref[CONV_WIDTH:CONV_ROWS, :] = jnp.zeros((CONV_ROWS - CONV_WIDTH, C), F32)

    vec = pl.BlockSpec((None, 1, C), lambda i: (j, 0, 0))
    ovec = pl.BlockSpec((1, C), lambda i: (0, 0))
    return pl.pallas_call(
        body, grid=(ntile,),
        in_specs=[pl.BlockSpec((tm, 3 * C), lambda i: (i, 0)),
                  pl.BlockSpec((CHUNK, 3 * C), lambda i: (jnp.maximum(i * nch - 1, 0), 0)),
                  pl.BlockSpec((CHUNK, 3 * C), lambda i: (jnp.minimum((i + 1) * nch, nblk - 1), 0)),
                  pl.BlockSpec((tm, C), lambda i: (i, 0)),
                  pl.BlockSpec((CHUNK, C), lambda i: (jnp.minimum((i + 1) * nch, nblk - 1), 0)),
                  pl.BlockSpec((tm, 2 * C), lambda i: (i, 0)),
                  pl.BlockSpec((CHUNK, 2 * C), lambda i: (jnp.minimum((i + 1) * nch, nblk - 1), 0)),
                  pl.BlockSpec((None, CONV_ROWS, C), lambda i: (j, 0, 0)), vec, vec, vec,
                  pl.BlockSpec((None, 4, 128, 128), lambda i: (j, 0, 0, 0)), vec, vec],
        out_specs=[pl.BlockSpec((tm, 3 * C), lambda i: (i, 0)), pl.BlockSpec((CONV_ROWS, C), lambda i: (0, 0)),
                   ovec, ovec, ovec, pl.BlockSpec((4, 128, 128), lambda i: (0, 0, 0)), ovec, ovec],
        out_shape=[SDS((T, 3 * C), BF16), SDS((CONV_ROWS, C), F32), SDS((1, C), F32), SDS((1, C), F32), SDS((1, C), F32),
                   SDS((4, 128, 128), F32), SDS((1, C), F32), SDS((1, C), F32)],
        scratch_shapes=[pltpu.VMEM((tm + CHUNK, C), F32), pltpu.VMEM((tm + 2 * CHUNK, C), F32),
                        pltpu.VMEM((tm + CHUNK, 2 * C), F32),
                        pltpu.VMEM((tm + CHUNK + 32, C), F32), pltpu.VMEM((tm + CHUNK, C), F32),
                        pltpu.VMEM((tm + CHUNK + 16, C), F32), pltpu.VMEM((8 * CONV_ROWS, C), F32)],
        name=name, compiler_params=_params("arbitrary"))(u, u, u, yc, yc, dy, dy, cw3, cb3, lg3, lb3, pw4, pb3, ps3)


HI = lax.Precision.HIGHEST


def _dot_nt(a, b):
    return lax.dot_general(a, b, (((1,), (1,)), ((), ())), preferred_element_type=F32)


def _dot_tn(a, b):
    return lax.dot_general(a, b, (((0,), (0,)), ((), ())), preferred_element_type=F32)


def _tri(lower):
    r = lax.broadcasted_iota(jnp.int32, (CHUNK, CHUNK), 0)
    c = lax.broadcasted_iota(jnp.int32, (CHUNK, CHUNK), 1)
    return jnp.where((c <= r) if lower else (c >= r), 1.0, 0.0).astype(F32)


def _hgrn_gates(u_ref, lb_ref, h, D, lv):
    ls = slice(h * HEAD_DIM, (h + 1) * HEAD_DIM)
    qraw = u_ref[:, ls]
    fraw = u_ref[:, D + h * HEAD_DIM:D + (h + 1) * HEAD_DIM]
    v = u_ref[:, 2 * D + h * HEAD_DIM:2 * D + (h + 1) * HEAD_DIM] * lv
    lbv = lb_ref[:, ls]
    sig = _sigmoid(fraw)
    forget = lbv + (1.0 - lbv) * sig
    logf = jnp.log(forget) * lv
    k = (1.0 - forget) * lv
    qsig = _sigmoid(qraw)
    q = qraw * qsig * lv
    return q, k, v, logf, (qraw, qsig, sig, forget, lbv)


def _sub_parts(q, k, b, b_s, I):
    rows = slice(SUB * I, SUB * (I + 1))
    rho = jnp.zeros((1, HEAD_DIM), F32) if I == 0 else b_s[SUB * I - 1:SUB * I, :]
    eI = jnp.exp(b[rows] - rho)
    EI = jnp.exp(jnp.minimum(rho - b, EXP_CAP))
    causal = (lax.broadcasted_iota(jnp.int32, (SUB, CHUNK), 1)
              <= lax.broadcasted_iota(jnp.int32, (SUB, CHUNK), 0) + SUB * I)
    return rows, q[rows] * eI, k * EI, eI, EI, causal


def _hgrn_fwd(u, lb3, layer, gn3, j, name):
    T = u.shape[0]
    D = u.shape[1] // 4
    H = D // HEAD_DIM
    NC = T // CHUNK

    def body(u_ref, lb_ref, gn_ref, y_ref, o_ref, sall_ref, st_s, b_s, lf_s, q_s, k_s):
        n = pl.program_id(0)

        @pl.when(n == 0)
        def _():
            st_s[...] = jnp.zeros_like(st_s)

        lv = _live(CHUNK, n * CHUNK, T)
        heads = range(H)
        cols = [slice(h * HEAD_DIM, (h + 1) * HEAD_DIM) for h in heads]
        vb = []
        for h in heads:
            q, k, v, logf, _ = _hgrn_gates(u_ref, lb_ref, h, D, lv)
            q_s[:, cols[h]] = q
            k_s[:, cols[h]] = k
            lf_s[:, cols[h]] = logf
            vb.append(v.astype(BF16))
        b_s[...] = jnp.dot(_tri(True), lf_s[...], precision=HI, preferred_element_type=F32)
        ops = []
        for h in heads:
            b_h = b_s.at[:, cols[h]]
            b = b_h[...]
            q = q_s[:, cols[h]]
            k = k_s[:, cols[h]]
            blast = b_h[CHUNK - 1:CHUNK, :]
            qh = (q * jnp.exp(b)).astype(BF16)
            kt = (k * jnp.exp(blast - b)).astype(BF16)
            subs = []
            for I in range(CHUNK // SUB):
                _, qI, KI, _, _, causal = _sub_parts(q, k, b, b_h, I)
                subs.append((qI.astype(BF16), KI.astype(BF16), causal))
            ops.append((qh, kt, jnp.exp(blast), subs))
        mm = []
        for h in heads:
            qh, kt, eblast, subs = ops[h]
            st = st_s[h]
            sall_ref[h] = st
            o_inter = _dot_nt(qh, st.astype(BF16))
            st_s[h] = st * eblast + _dot_tn(vb[h], kt)
            mm.append((o_inter, [_dot_nt(qI, KI) for qI, KI, _ in subs]))
        for h in heads:
            o_inter, ps = mm[h]
            p = jnp.concatenate([jnp.where(c, x, 0.0) for x, (_, _, c) in zip(ps, ops[h][3])], axis=0).astype(BF16)
            o = o_inter + jnp.dot(p, vb[h], preferred_element_type=F32)
            o_ref[:, cols[h]] = o
            graw = u_ref[:, 3 * D + h * HEAD_DIM:3 * D + (h + 1) * HEAD_DIM]
            r = lax.rsqrt(jnp.mean(o * o, axis=-1, keepdims=True) + EPS)
            y_ref[:, cols[h]] = (((o * r) * gn_ref[...]) * (graw * _sigmoid(graw))).astype(BF16)

    return pl.pallas_call(
        body, grid=(NC,),
        in_specs=[pl.BlockSpec((CHUNK, 4 * D), lambda n: (n, 0)),
                  pl.BlockSpec((None, 1, D), lambda n: (layer, 0, 0)),
                  pl.BlockSpec((None, 1, HEAD_DIM), lambda n: (j, 0, 0))],
        out_specs=[pl.BlockSpec((CHUNK, D), lambda n: (n, 0)), pl.BlockSpec((CHUNK, D), lambda n: (n, 0)),
                   pl.BlockSpec((None, H, HEAD_DIM, HEAD_DIM), lambda n: (n, 0, 0, 0))],
        out_shape=[SDS((T, D), BF16), SDS((T, D), F32), SDS((NC, H, HEAD_DIM, HEAD_DIM), F32)],
        scratch_shapes=[pltpu.VMEM((H, HEAD_DIM, HEAD_DIM), F32)] + [pltpu.VMEM((CHUNK, D), F32)] * 4,
        name=name, compiler_params=_params("arbitrary"))(u, lb3, gn3)


def _hgrn_bwd(u, o_raw, dy, sall, lb3, layer, gn3, j, name):
    T = u.shape[0]
    D = u.shape[1] // 4
    H = D // HEAD_DIM
    NC = T // CHUNK

    def body(u_ref, o_ref, dy_ref, sall_ref, lb_ref, gn_ref, du_ref, dlb_ref, dgn_ref, dst_s, b_s, lf_s, q_s, k_s, db_s, dk_s):
        step = pl.program_id(0)
        n = NC - 1 - step

        @pl.when(step == 0)
        def _():
            dst_s[...] = jnp.zeros_like(dst_s)
            dlb_ref[...] = jnp.zeros_like(dlb_ref)
            dgn_ref[...] = jnp.zeros_like(dgn_ref)

        lv = _live(CHUNK, n * CHUNK, T)
        last_row = (_row_ids((CHUNK, 1), 0) == CHUNK - 1).astype(F32)
        gn = gn_ref[...]
        heads = range(H)
        cols = [slice(h * HEAD_DIM, (h + 1) * HEAD_DIM) for h in heads]
        vb, dob = [], []
        dgn = jnp.zeros((1, HEAD_DIM), F32)
        for h in heads:
            q, k, v, logf, _ = _hgrn_gates(u_ref, lb_ref, h, D, lv)
            q_s[:, cols[h]] = q
            k_s[:, cols[h]] = k
            lf_s[:, cols[h]] = logf
            vb.append(v.astype(BF16))
            graw = u_ref[:, 3 * D + h * HEAD_DIM:3 * D + (h + 1) * HEAD_DIM]
            gsig = _sigmoid(graw)
            o = o_ref[:, cols[h]]
            r = lax.rsqrt(jnp.mean(o * o, axis=-1, keepdims=True) + EPS)
            xh = o * r
            dyv = dy_ref[:, cols[h]]
            dsg = dyv * (graw * gsig)
            dgn = dgn + jnp.sum(dsg * xh, axis=0, keepdims=True)
            dxh = dsg * gn
            do = r * (dxh - xh * jnp.mean(dxh * xh, axis=-1, keepdims=True))
            dob.append(do.astype(BF16))
            dgraw = dyv * xh * gn * (gsig * (1.0 + graw * (1.0 - gsig)))
            du_ref[:, 3 * D + h * HEAD_DIM:3 * D + (h + 1) * HEAD_DIM] = (dgraw * lv).astype(BF16)
        dgn_ref[...] += dgn
        b_s[...] = jnp.dot(_tri(True), lf_s[...], precision=HI, preferred_element_type=F32)
        ops = []
        for h in heads:
            b_h = b_s.at[:, cols[h]]
            b = b_h[...]
            q = q_s[:, cols[h]]
            k = k_s[:, cols[h]]
            blast = b_h[CHUNK - 1:CHUNK, :]
            eb = jnp.exp(b)
            ekb = jnp.exp(blast - b)
            subs = []
            for I in range(CHUNK // SUB):
                rows, qI, KI, eI, EI, causal = _sub_parts(q, k, b, b_h, I)
                subs.append((rows, qI.astype(BF16), KI.astype(BF16), eI, EI, causal))
            ops.append((eb, ekb, jnp.exp(blast), (q * eb).astype(BF16), (k * ekb).astype(BF16), subs))
        mm = []
        for h in heads:
            eb, ekb, eblast, qhb, ktb, subs = ops[h]
            st = sall_ref[h]
            dst = dst_s[h]
            dstb = dst.astype(BF16)
            dv = _dot_nt(ktb, dstb)
            dqh = jnp.dot(dob[h], st.astype(BF16), preferred_element_type=F32)
            dkt = jnp.dot(vb[h], dstb, preferred_element_type=F32)
            dblast = jnp.sum(dst * st, axis=0, keepdims=True) * eblast
            dst_s[h] = dst * eblast + _dot_tn(dob[h], qhb)
            dp_full = _dot_nt(dob[h], vb[h])
            ps = [_dot_nt(qIb, KIb) for _, qIb, KIb, _, _, _ in subs]
            mm.append((dv, dqh, dkt, dblast, dp_full, ps))
        for h in heads:
            eb, ekb, eblast, qhb, ktb, subs = ops[h]
            dv, dqh, dkt, dblast, dp_full, ps = mm[h]
            p = jnp.concatenate([jnp.where(sub[5], x, 0.0) for x, sub in zip(ps, subs)], axis=0).astype(BF16)
            dv = dv + _dot_tn(p, dob[h])
            du_ref[:, 2 * D + h * HEAD_DIM:2 * D + (h + 1) * HEAD_DIM] = (dv * lv).astype(BF16)
            dq = dqh * eb
            db = dqh * qhb.astype(F32)
            tmp = dkt * ktb.astype(F32)
            dk = dkt * ekb
            db = db - tmp
            dblast = dblast + jnp.sum(tmp, axis=0, keepdims=True)
            dq_parts, db_parts = [], []
            for rows, qIb, KIb, eI, EI, causal in subs:
                dp = jnp.where(causal, dp_full[rows], 0.0).astype(BF16)
                dqI = jnp.dot(dp, KIb, preferred_element_type=F32)
                dKI = _dot_tn(dp, qIb)
                dq_parts.append(dqI * eI)
                db_parts.append(dqI * qIb.astype(F32))
                dk = dk + dKI * EI
                db = db - dKI * KIb.astype(F32)
            dq = dq + jnp.concatenate(dq_parts, axis=0)
            db_s[:, cols[h]] = db + jnp.concatenate(db_parts, axis=0) + last_row * dblast
            dk_s[:, cols[h]] = dk
            qraw = u_ref[:, cols[h]]
            qsig = _sigmoid(qraw)
            du_ref[:, cols[h]] = (dq * (qsig * (1.0 + qraw * (1.0 - qsig))) * lv).astype(BF16)
        lf_s[...] = jnp.dot(_tri(False), db_s[...], precision=HI, preferred_element_type=F32)
        for h in heads:
            fraw = u_ref[:, D + h * HEAD_DIM:D + (h + 1) * HEAD_DIM]
            lbv = lb_ref[:, cols[h]]
            sig = _sigmoid(fraw)
            forget = lbv + (1.0 - lbv) * sig
            dforget = (lf_s[:, cols[h]] / forget - dk_s[:, cols[h]]) * lv
            dlb_ref[:, cols[h]] += jnp.sum(dforget * (1.0 - sig), axis=0, keepdims=True)
            du_ref[:, D + h * HEAD_DIM:D + (h + 1) * HEAD_DIM] = (dforget * (1.0 - lbv) * sig * (1.0 - sig)).astype(BF16)

    rev = lambda s: (NC - 1 - s, 0)
    return pl.pallas_call(
        body, grid=(NC,),
        in_specs=[pl.BlockSpec((CHUNK, 4 * D), rev), pl.BlockSpec((CHUNK, D), rev), pl.BlockSpec((CHUNK, D), rev),
                  pl.BlockSpec((None, H, HEAD_DIM, HEAD_DIM), lambda s: (NC - 1 - s, 0, 0, 0)),
                  pl.BlockSpec((None, 1, D), lambda s: (layer, 0, 0)),
                  pl.BlockSpec((None, 1, HEAD_DIM), lambda s: (j, 0, 0))],
        out_specs=[pl.BlockSpec((CHUNK, 4 * D), rev), pl.BlockSpec((1, D), lambda s: (0, 0)),
                   pl.BlockSpec((1, HEAD_DIM), lambda s: (0, 0))],
        out_shape=[SDS((T, 4 * D), BF16), SDS((1, D), F32), SDS((1, HEAD_DIM), F32)],
        scratch_shapes=[pltpu.VMEM((H, HEAD_DIM, HEAD_DIM), F32)] + [pltpu.VMEM((CHUNK, D), F32)] * 6,
        name=name, compiler_params=_params("arbitrary"))(u, o_raw, dy, sall, lb3, gn3)


def _softmax_layers(p_ref, n_layers):
    rows = [p_ref[l:l + 1, :] for l in range(n_layers)]
    m = functools.reduce(jnp.maximum, rows)
    e = [jnp.exp(x - m) for x in rows]
    tot = functools.reduce(lambda a, b: a + b, e)
    return [x / tot for x in e]


def _lb_fwd(p):
    n_layers, D = p.shape

    def body(p_ref, o_ref):
        s = _softmax_layers(p_ref, n_layers)
        acc = jnp.zeros((1, D), F32)
        o_ref[0:1, :] = acc
        for l in range(1, n_layers):
            acc = acc + s[l]
            o_ref[l:l + 1, :] = acc

    return pl.pallas_call(body, out_shape=SDS(p.shape, F32), name="lb_fwd")(p)


def _lb_bwd(p, dlb):
    n_layers, D = p.shape

    def body(p_ref, d_ref, o_ref):
        s = _softmax_layers(p_ref, n_layers)
        ds = [jnp.zeros((1, D), F32)] * n_layers
        acc = jnp.zeros((1, D), F32)
        for l in range(n_layers - 1, 0, -1):
            acc = acc + d_ref[l:l + 1, :]
            ds[l] = acc
        dot = functools.reduce(lambda a, b: a + b, [s[l] * ds[l] for l in range(n_layers)])
        for l in range(n_layers):
            o_ref[l:l + 1, :] = s[l] * (ds[l] - dot)

    return pl.pallas_call(body, out_shape=SDS(p.shape, F32), name="lb_bwd")(p, dlb)


def _adamw(w, g, m, v, name):
    R, C = w.shape
    tr = _tile(R, 256, 8) if R % 8 == 0 else R

    def body(w_ref, g_ref, m_ref, v_ref, d_ref, mo_ref, vo_ref):
        g_ = g_ref[...]
        m_ = ADAM_B1 * m_ref[...] + (1.0 - ADAM_B1) * g_
        v_ = ADAM_B2 * v_ref[...] + (1.0 - ADAM_B2) * (g_ * g_)
        mh = m_ / (1.0 - ADAM_B1 ** ADAM_STEP)
        vh = v_ / (1.0 - ADAM_B2 ** ADAM_STEP)
        d_ref[...] = -ADAM_LR * (mh / (jnp.sqrt(vh) + ADAM_EPS) + ADAM_WD * w_ref[...])
        mo_ref[...] = m_
        vo_ref[...] = v_

    blk = pl.BlockSpec((tr, C), lambda i: (i, 0))
    return pl.pallas_call(
        body, grid=(R // tr,), in_specs=[blk] * 4, out_specs=[blk] * 3, out_shape=[SDS((R, C), F32)] * 3,
        name=name, compiler_params=_params("parallel"))(w, g, m, v)


def _adamw_layer(w3, m3, v3, g2, layer, outs, name):
    L, R, C = w3.shape
    tr = _tile(R, 256, 8)
    if outs is None:
        outs = tuple(lax.empty(w3.shape, F32) for _ in range(4))

    def body(w_ref, m_ref, v_ref, g_ref, a0, a1, a2, a3, go_ref, d_ref, mo_ref, vo_ref):
        del a0, a1, a2, a3
        g_ = g_ref[...]
        m_ = ADAM_B1 * m_ref[...] + (1.0 - ADAM_B1) * g_
        v_ = ADAM_B2 * v_ref[...] + (1.0 - ADAM_B2) * (g_ * g_)
        mh = m_ / (1.0 - ADAM_B1 ** ADAM_STEP)
        vh = v_ / (1.0 - ADAM_B2 ** ADAM_STEP)
        go_ref[...] = g_
        d_ref[...] = -ADAM_LR * (mh / (jnp.sqrt(vh) + ADAM_EPS) + ADAM_WD * w_ref[...])
        mo_ref[...] = m_
        vo_ref[...] = v_

    lay = pl.BlockSpec((None, tr, C), lambda i: (layer, i, 0))
    return pl.pallas_call(
        body, grid=(R // tr,), in_specs=[lay] * 3 + [pl.BlockSpec((tr, C), lambda i: (i, 0))] + [ANY_SPEC] * 4,
        out_specs=[lay] * 4, out_shape=[SDS(w3.shape, F32)] * 4, input_output_aliases={4: 0, 5: 1, 6: 2, 7: 3},
        name=name, compiler_params=_params("parallel"))(w3, m3, v3, g2, *outs)


SEM_SPEC = pl.BlockSpec(memory_space=pltpu.SEMAPHORE)
HBM_SPEC = pl.BlockSpec(memory_space=pltpu.HBM)
EFFECT = pltpu.SideEffectType.DATAFLOW_SIDE_EFFECTING
N_DEV = 2 * N_CHIPS


def _position():
    x, y, c = lax.axis_index("x"), lax.axis_index("y"), lax.axis_index("c")
    chips = [(1 - x, y), (x, 1 - y), (1 - x, 1 - y)]
    return x, y, c, chips


def _split_start(name, plan, bufs, n_sems, deps=(), earlier=None):
    n = len(bufs)
    held = () if earlier is None else tuple(earlier[1:])

    def body(*refs):
        first_out = n + len(held) + len(deps)
        if earlier is not None:
            sends, recvs = earlier[0](refs[:n], refs[n], refs[n + 1])
            for kw in sends:
                pltpu.make_async_remote_copy(**kw).wait_send()
            for kw in recvs:
                pltpu.make_async_remote_copy(**kw).wait_recv()
        sends, _ = plan(refs[:n], refs[first_out], refs[first_out + 1])
        for kw in sends:
            pltpu.make_async_remote_copy(**kw).start()
        refs[-1][...] = jnp.zeros_like(refs[-1])

    out = pl.pallas_call(
        body, name=name,
        out_shape=(pltpu.SemaphoreType.DMA((n_sems,)), pltpu.SemaphoreType.DMA((n_sems,)),
                   *[pltpu.HBM(b.shape, b.dtype) for b in bufs], SDS((8, 128), F32)),
        in_specs=[HBM_SPEC] * n + [SEM_SPEC] * len(held) + [ANY_SPEC] * len(deps),
        out_specs=(SEM_SPEC, SEM_SPEC, *[HBM_SPEC] * n, pl.BlockSpec(memory_space=pltpu.VMEM)),
        input_output_aliases={i: 2 + i for i in range(n)},
        compiler_params=pltpu.CompilerParams(has_side_effects=EFFECT),
    )(*[pltpu.with_memory_space_constraint(b, pltpu.HBM) for b in bufs], *held, *deps)
    return out[0], out[1], list(out[2:2 + n]), out[-1]


def _split_wait(name, plan, send_sems, recv_sems, bufs, after=()):
    n = len(bufs)

    def body(*refs):
        sends, recvs = plan(refs[:n], refs[n], refs[n + 1])
        for kw in sends:
            pltpu.make_async_remote_copy(**kw).wait_send()
        for kw in recvs:
            pltpu.make_async_remote_copy(**kw).wait_recv()

    out = pl.pallas_call(
        body, name=name, out_shape=tuple(pltpu.HBM(b.shape, b.dtype) for b in bufs),
        in_specs=[HBM_SPEC] * n + [SEM_SPEC, SEM_SPEC] + [ANY_SPEC] * len(after),
        out_specs=tuple([HBM_SPEC] * n), input_output_aliases={i: i for i in range(n)},
        compiler_params=pltpu.CompilerParams(has_side_effects=EFFECT),
    )(*bufs, send_sems, recv_sems, *after)
    return list(out)


def _region(kind, ref, chip, half):
    K, N = ref.shape
    if kind == "col":
        return ref.at[pl.ds(half * (K // 2), K // 2), pl.ds(chip * (N // N_CHIPS), N // N_CHIPS)]
    rows = K // (2 * N_CHIPS)
    return ref.at[pl.ds((2 * chip + half) * rows, rows), :]


def _gather_plan(kinds, over_chips):
    def plan(refs, send_sems, recv_sems):
        x, y, c, chips = _position()
        sends, recvs = [], []
        for f, (ref, kind) in enumerate(zip(refs, kinds)):
            for k, chip in enumerate(chips):
                theirs = 2 * chip[0] + chip[1]
                sem = dict(send_sem=send_sems.at[3 * f + k], recv_sem=recv_sems.at[3 * f + k], device_id_type=MESH)
                if over_chips:
                    out, back, to = _region(kind, ref, 2 * x + y, c), _region(kind, ref, theirs, c), (*chip, c)
                else:
                    out, back, to = _region(kind, ref, theirs, c), _region(kind, ref, theirs, 1 - c), (x, y, 1 - c)
                sends.append(dict(src_ref=out, dst_ref=out, device_id=to, **sem))
                recvs.append(dict(src_ref=back, dst_ref=back, device_id=to, **sem))
        return sends, recvs
    return plan


def _reduce_plan(refs, send_sems, recv_sems):
    x, y, c, _ = _position()
    me = 4 * x + 2 * y + c
    sends, recvs = [], []
    for f in range(len(refs) // 2):
        acc, land = refs[2 * f], refs[2 * f + 1]
        for d in range(1, N_DEV):
            t = (me + d) % N_DEV
            to = dict(device_id=(t // 4, (t // 2) % 2, t % 2), device_id_type=MESH)
            slot = N_DEV - 1 - d
            sends.append(dict(src_ref=acc.at[t % 2, t // 2], dst_ref=land.at[slot], send_sem=send_sems.at[7 * f + d - 1],
                              recv_sem=recv_sems.at[7 * f + slot], **to))
            recvs.append(dict(src_ref=land.at[d - 1], dst_ref=land.at[d - 1], send_sem=send_sems.at[7 * f + d - 1],
                              recv_sem=recv_sems.at[7 * f + d - 1], **to))
    return sends, recvs


def _swap_plan(refs, send_sems, recv_sems):
    x, y, c, _ = _position()
    sends, recvs = [], []
    for f, g in enumerate(refs):
        sem = dict(send_sem=send_sems.at[f], recv_sem=recv_sems.at[f], device_id=(x, y, 1 - c), device_id_type=MESH)
        sends.append(dict(src_ref=g.at[c], dst_ref=g.at[c], **sem))
        recvs.append(dict(src_ref=g.at[1 - c], dst_ref=g.at[1 - c], **sem))
    return sends, recvs


def _sum_pieces(ids2, acc, land, name):
    _, _, nr, nc = acc.shape
    tr = _tile(nr, 256, 16)

    def body(ids_ref, own_ref, land_ref, o_ref):
        del ids_ref
        s = own_ref[...].astype(F32)
        for k in range(N_DEV - 1):
            s = s + land_ref[k].astype(F32)
        o_ref[...] = s

    return pl.pallas_call(
        body,
        grid_spec=pltpu.PrefetchScalarGridSpec(
            num_scalar_prefetch=1, grid=(nr // tr,),
            in_specs=[pl.BlockSpec((None, None, tr, nc), lambda i, ids: (ids[0], ids[1], i, 0)),
                      pl.BlockSpec((N_DEV - 1, tr, nc), lambda i, ids: (0, i, 0))],
            out_specs=pl.BlockSpec((None, tr, nc), lambda i, ids: (ids[0], i, 0))),
        out_shape=SDS((2, nr, nc), F32), name=name, compiler_params=_params("parallel"))(ids2, acc, land)


def _small_plan(refs, send_sems, recv_sems):
    x, y, c, _ = _position()
    me = 4 * x + 2 * y + c
    own, land = refs
    sends, recvs = [], []
    for d in range(1, N_DEV):
        t = (me + d) % N_DEV
        to = dict(device_id=(t // 4, (t // 2) % 2, t % 2), device_id_type=MESH)
        sends.append(dict(src_ref=own, dst_ref=land.at[me], send_sem=send_sems.at[d - 1],
                          recv_sem=recv_sems.at[N_DEV - 1 - d], **to))
        recvs.append(dict(src_ref=land.at[t], dst_ref=land.at[t], send_sem=send_sems.at[d - 1],
                          recv_sem=recv_sems.at[d - 1], **to))
    return sends, recvs


def _sum_blocks(me1, own, land):
    def body(me_ref, own_ref, land_ref, o_ref):
        acc = None
        for d in range(N_DEV):
            term = jnp.where(me_ref[0] == d, own_ref[...], land_ref[d])
            acc = term if acc is None else acc + term
        o_ref[...] = acc

    return pl.pallas_call(
        body,
        grid_spec=pltpu.PrefetchScalarGridSpec(
            num_scalar_prefetch=1, grid=(1,),
            in_specs=[pl.BlockSpec(own.shape, lambda i, me: (0, 0)), pl.BlockSpec(land.shape, lambda i, me: (0, 0, 0))],
            out_specs=pl.BlockSpec(own.shape, lambda i, me: (0, 0))),
        out_shape=SDS(own.shape, F32), name="sum_small", compiler_params=_params("arbitrary"))(me1, own, land)


BIG = {"ev_w_in": "col", "ev_w_out": "row", "od_w_in": "col", "od_w_out": "row", "mlp_w1": "col", "mlp_w2": "row"}
WEIGHTS = ("meta_tokens", "mix_norm_g", "mlp_norm_g", "final_norm_g", "ev_w_in", "ev_conv_w", "ev_conv_b", "ev_ln_g",
           "ev_ln_b", "ev_pool_w", "ev_pool_b", "ev_pool_scale", "ev_w_out", "od_w_in", "od_gnorm_g", "od_w_out",
           "lb_param", "mlp_w1", "mlp_w2")
PACK_UNIT = 1024


def _mixer_names(layer):
    return ("ev_w_in", "ev_w_out") if layer % 2 == 0 else ("od_w_in", "od_w_out")


def _pack(arrays):
    flat = []
    for a in arrays:
        a = a.reshape(-1)
        flat.append(jnp.pad(a, (0, (-a.shape[0]) % PACK_UNIT)))
    return jnp.concatenate(flat).reshape(-1, 128)


def _unpack(packed, shapes):
    flat = packed.reshape(-1)
    out, off = [], 0
    for s in shapes:
        size = 1
        for d in s:
            size *= d
        out.append(flat[off:off + size].reshape(s))
        off += size + (-size) % PACK_UNIT
    return out


def _local_step(x2, target, P, weights, boundary, first_deps=()):
    D = x2.shape[1]
    n_layers = P["mix_norm_g"].shape[0]
    h = jnp.concatenate([jnp.zeros((PAD, D), F32), P["meta_full"], x2], axis=0)
    mix_g = P["mix_norm_g"].reshape(n_layers, 1, D)
    mlp_g = P["mlp_norm_g"].reshape(n_layers, 1, D)
    vec = lambda a: a.reshape(a.shape[0], 1, -1)
    cb3, lg3, lnb3, ps3 = vec(P["ev_conv_b"]), vec(P["ev_ln_g"]), vec(P["ev_ln_b"]), vec(P["ev_pool_scale"])
    pb3 = vec(P["ev_pool_b"])
    gn3 = vec(P["od_gnorm_g"])
    lb_all = _lb_fwd(P["lb_param"])
    lb3 = lb_all.reshape(n_layers, 1, D)
    even = (cb3, lg3, lnb3, P["ev_pool_w"], pb3, ps3)

    saved = []
    deps = tuple(first_deps)
    for layer in range(n_layers):
        j = layer // 2
        w_in, w_out = _mixer_names(layer)
        W = {}
        s = {"h": h, "W": W}
        s["n"] = _rms_fwd(h, mix_g, layer, "mix_norm_0", deps=deps) if layer == 0 else n_next
        deps = ()
        W[w_in], held = weights(layer, w_in, (s["n"],))
        s["u"] = _mm_nn(s["n"], W[w_in], 0, f"mix_in_{layer}", deps=held)
        if layer % 2 == 0:
            s["y"], s["yc"] = _even_fwd(s["u"], P["conv_w_full"], *even, j, f"even_fwd_{layer}")
        else:
            s["y"], s["o"], s["sall"] = _hgrn_fwd(s["u"], lb3, layer, gn3, j, f"hgrn_fwd_{layer}")
        W[w_out], held = weights(layer, w_out, (s["y"],))
        h, s["n2"] = _mm_nn_norm(s["y"], W[w_out], 0, h, mlp_g, layer, f"mix_out_{layer}", deps=held)
        s["h1"] = h
        W["mlp_w1"], held = weights(layer, "mlp_w1", (s["n2"],))
        if layer == 0:
            s["relu"] = _mm_nn(s["n2"], W["mlp_w1"], 0, "mlp_up_0", relu=True, deps=held)
            W["mlp_w2"], held = weights(layer, "mlp_w2", (s["relu"],))
            h, n_next = _mm_nn_norm(s["relu"], W["mlp_w2"], 0, h, mix_g, 1, "mlp_down_0", square=True, deps=held)
        else:
            W["mlp_w2"], more = weights(layer, "mlp_w2", (s["n2"],))
            last = layer + 1 == n_layers
            out = _mlp_fwd(s["n2"], W["mlp_w1"], W["mlp_w2"], h, None if last else mix_g, layer + 1, f"mlp_{layer}", deps=held + more)
            h, s["relu"] = out[0], out[-1]
            n_next = None if last else out[1]
        saved.append(s)

    dh, dhb, dg_final, loss = _final(h, P["final_norm_g"].reshape(1, D), target)

    small = {"final_norm_g": dg_final}
    per_layer = {k: [None] * n_layers for k in ("mix_norm_g", "mlp_norm_g", "lb")}
    per_pair = {k: [None] * (n_layers // 2) for k in
                ("ev_conv_w", "ev_conv_b", "ev_ln_g", "ev_ln_b", "ev_pool_w", "ev_pool_b", "ev_pool_scale", "od_gnorm_g")}
    for layer in reversed(range(n_layers)):
        j = layer // 2
        s = saved[layer]
        W = s["W"]
        w_in, w_out = _mixer_names(layer)
        dz = _mm_nt(dhb, W["mlp_w2"], 0, f"d_act_{layer}", relu=s["relu"], deps=deps)
        dw2 = _mm_tn(s["relu"], dhb, "row", f"dw2_{layer}", square=True)
        dw1 = _mm_tn(s["n2"], dz, "col", f"dw1_{layer}")
        dh, dhb, per_layer["mlp_norm_g"][layer] = _mm_nt_norm(dz, W["mlp_w1"], 0, s["h1"], mlp_g, layer, dh, f"d_n2_{layer}")
        deps = boundary(f"mlp{layer}", {("mlp_w1", layer): dw1, ("mlp_w2", layer): dw2}, (dhb, dw1, dw2))
        dy = _mm_nt(dhb, W[w_out], 0, f"d_y_{layer}", deps=deps)
        dwout = _mm_tn(s["y"], dhb, "row", f"dwout_{layer}")
        if layer % 2 == 0:
            du, dcw, dcb, dlg, dlnb, dpw, dpb, dps = _even_bwd(s["u"], s["yc"], dy, P["conv_w_full"], *even, j, f"even_bwd_{layer}")
            for k, val in (("ev_conv_w", dcw), ("ev_conv_b", dcb), ("ev_ln_g", dlg), ("ev_ln_b", dlnb),
                           ("ev_pool_w", dpw), ("ev_pool_b", dpb), ("ev_pool_scale", dps)):
                per_pair[k][j] = val
        else:
            du, per_layer["lb"][layer], per_pair["od_gnorm_g"][j] = _hgrn_bwd(
                s["u"], s["o"], dy, s["sall"], lb3, layer, gn3, j, f"hgrn_bwd_{layer}")
        dwin = _mm_tn(s["n"], du, "col", f"dwin_{layer}")
        deps = boundary(f"mix{layer}", {(w_in, j): dwin, (w_out, j): dwout}, (du, dwin, dwout))
        dh, dhb, per_layer["mix_norm_g"][layer] = _mm_nt_norm(du, W[w_in], 0, s["h"], mix_g, layer, dh, f"d_n_{layer}", deps=deps)
        deps = ()

    small["mix_norm_g"] = jnp.concatenate(per_layer["mix_norm_g"], axis=0)
    small["mlp_norm_g"] = jnp.concatenate(per_layer["mlp_norm_g"], axis=0)
    dlb_all = jnp.concatenate([jnp.zeros((1, D), F32) if g is None else g for g in per_layer["lb"]], axis=0)
    small["lb_param"] = _lb_bwd(P["lb_param"], dlb_all)
    for k, vals in per_pair.items():
        small[k] = jnp.stack(vals, axis=0)
    small["meta_tokens"] = dh[PAD:LEAD]
    return loss, dh, small


def kernel(x, meta_tokens, mix_norm_g, mlp_norm_g, final_norm_g, ev_w_in, ev_conv_w, ev_conv_b, ev_ln_g, ev_ln_b, ev_pool_w, ev_pool_b, ev_pool_scale, ev_w_out, od_w_in, od_gnorm_g, od_w_out, lb_param, mlp_w1, mlp_w2, loss_target, m_meta_tokens, m_mix_norm_g, m_mlp_norm_g, m_final_norm_g, m_ev_w_in, m_ev_conv_w, m_ev_conv_b, m_ev_ln_g, m_ev_ln_b, m_ev_pool_w, m_ev_pool_b, m_ev_pool_scale, m_ev_w_out, m_od_w_in, m_od_gnorm_g, m_od_w_out, m_lb_param, m_mlp_w1, m_mlp_w2, v_meta_tokens, v_mix_norm_g, v_mlp_norm_g, v_final_norm_g, v_ev_w_in, v_ev_conv_w, v_ev_conv_b, v_ev_ln_g, v_ev_ln_b, v_ev_pool_w, v_ev_pool_b, v_ev_pool_scale, v_ev_w_out, v_od_w_in, v_od_gnorm_g, v_od_w_out, v_lb_param, v_mlp_w1, v_mlp_w2):
    given = dict(locals())
    w = {n: given[n] for n in WEIGHTS}
    m = {n: given["m_" + n] for n in WEIGHTS}
    v = {n: given["v_" + n] for n in WEIGHTS}
    n_layers = mix_norm_g.shape[0]
    core = lax.axis_index("c").astype(jnp.int32)
    chip = (2 * lax.axis_index("x") + lax.axis_index("y")).astype(jnp.int32)
    chip1 = chip.reshape(1)
    ids2 = jnp.stack([core, chip])

    conv_pad = jnp.pad(ev_conv_w, ((0, 0), (0, CONV_ROWS - CONV_WIDTH), (0, 0)))
    stages = [[(0, n)] for n in (*_mixer_names(0), "mlp_w1", "mlp_w2")]
    stages += [[(layer, n) for n in (*_mixer_names(layer), "mlp_w1", "mlp_w2")] for layer in range(1, n_layers)]
    gathers, where, token = [], {}, ()
    for k, stage in enumerate(stages):
        index = [layer if n.startswith("mlp") else layer // 2 for layer, n in stage]
        kinds = [BIG[n] for _, n in stage]
        bufs = [_cast_place(w[n], i, BIG[n], chip1, BF16, f"place_{n}_{i}") for (_, n), i in zip(stage, index)]
        if k == 0:
            bufs.append(_cast_place(meta_tokens[None], 0, "col", chip1, F32, "place_meta"))
            bufs.append(_cast_place(conv_pad.reshape(1, -1, conv_pad.shape[2]), 0, "col", chip1, F32, "place_conv_w"))
            kinds += ["col", "col"]
        plan = _gather_plan(kinds, True)
        ss, rs, bufs, tok = _split_start(f"gather_start_{k}", plan, bufs, 3 * len(bufs), deps=token)
        token = (tok,)
        gathers.append((kinds, plan, ss, rs, bufs))
        where.update({key: (k, f) for f, key in enumerate(stage)})

    landed, passed, held = {}, {}, []

    def hand_on(k, deps):
        if k not in passed:
            kinds, plan, ss, rs, bufs = gathers[k]
            to_sibling = _gather_plan(kinds, False)
            ss, rs, bufs, tok = _split_start(f"gather_pass_{k}", to_sibling, bufs, 3 * len(bufs), deps=deps, earlier=(plan, ss, rs))
            passed[k] = (to_sibling, ss, rs, bufs)
            held.append(tok)

    def arrived(k, after):
        if k not in landed:
            hand_on(k, after)
            landed[k] = _split_wait(f"gather_wait_{k}", *passed[k], after)
        return landed[k]

    def weights(layer, name, after):
        k, f = where[(layer, name)]
        full = arrived(k, after)[f][None]
        if name == "mlp_w2" and layer + 1 < n_layers:
            hand_on(where[(layer + 1, "mlp_w2")][0], after)
        tokens = tuple(held)
        held.clear()
        return full, tokens

    first = arrived(0, token)
    P = {n: w[n] for n in ("mix_norm_g", "mlp_norm_g", "final_norm_g", "ev_conv_b", "ev_ln_g", "ev_ln_b", "ev_pool_w",
                           "ev_pool_b", "ev_pool_scale", "od_gnorm_g", "lb_param")}
    P["meta_full"] = first[1]
    P["conv_w_full"] = first[2].reshape(ev_conv_w.shape[0], CONV_ROWS, -1)

    pending, outs = [], {n: None for n in BIG}

    def advance(after):
        tokens, still = [], []
        for st in pending:
            if st["phase"] == 1:
                bufs = _split_wait(f"reduce_wait_{st['tag']}", _reduce_plan, st["ss"], st["rs"], st["bufs"], after)
                halves = [_sum_pieces(ids2, bufs[2 * f], bufs[2 * f + 1], f"sum_{st['tag']}_{f}") for f in range(len(bufs) // 2)]
                ss, rs, halves, tok = _split_start(f"swap_start_{st['tag']}", _swap_plan, halves, len(halves))
                tokens.append(tok)
                still.append(dict(st, phase=2, ss=ss, rs=rs, bufs=halves))
            else:
                grads = _split_wait(f"swap_wait_{st['tag']}", _swap_plan, st["ss"], st["rs"], st["bufs"], after)
                for (n, i), g in zip(st["keys"], grads):
                    outs[n] = _adamw_layer(w[n], m[n], v[n], g.reshape(w[n].shape[1:]), i, outs[n], f"adamw_{n}_{i}")
        pending[:] = still
        return tokens

    def boundary(tag, grads, after):
        tokens = advance(after)
        bufs = []
        for acc in grads.values():
            bufs += [acc, lax.empty((N_DEV - 1,) + acc.shape[2:], BF16)]
        ss, rs, bufs, tok = _split_start(f"reduce_start_{tag}", _reduce_plan, bufs, 7 * len(grads))
        pending.append(dict(phase=1, tag=tag, keys=list(grads), ss=ss, rs=rs, bufs=bufs))
        return tuple(tokens + [tok])

    loss, dh, small = _local_step(x[0], loss_target[0], P, weights, boundary, first_deps=token)

    order = [n for n in WEIGHTS if n not in BIG]
    block = _pack([small[n] for n in order] + [loss])
    ss, rs, bufs, tok = _split_start("small_start", _small_plan, [block, lax.empty((N_DEV,) + block.shape, F32)], N_DEV - 1)
    for _ in range(2):
        advance((tok,) + tuple(o[0] for o in outs.values() if o is not None))
    block, land = _split_wait("small_wait", _small_plan, ss, rs, bufs, tuple(outs[n][0] for n in BIG))
    packed = _sum_blocks((4 * lax.axis_index("x") + 2 * lax.axis_index("y") + lax.axis_index("c")).astype(jnp.int32).reshape(1), block, land)
    total = _unpack(packed, [small[n].shape for n in order] + [loss.shape])
    loss_sum = total[-1][0, 0]
    gsmall = dict(zip(order, total[:-1]))
    gsmall["meta_tokens"] = lax.dynamic_slice_in_dim(gsmall["meta_tokens"], chip * meta_tokens.shape[1], meta_tokens.shape[1], 1)
    gsmall["ev_conv_w"] = lax.dynamic_slice_in_dim(gsmall["ev_conv_w"][:, :CONV_WIDTH], chip * ev_conv_w.shape[2], ev_conv_w.shape[2], 2)

    g_out, d_out, m_out, v_out = {}, {}, {}, {}
    for n in WEIGHTS:
        if n in BIG:
            g_out[n], d_out[n], m_out[n], v_out[n] = outs[n]
            continue
        shape = w[n].shape
        g = gsmall[n].reshape(shape)
        cols = shape[-1] if len(shape) > 1 else 128
        two = lambda a: a.reshape(-1, cols)
        d_, m_, v_ = _adamw(two(w[n]), two(g), two(m[n]), two(v[n]), f"adamw_{n}")
        g_out[n], d_out[n], m_out[n], v_out[n] = g, d_.reshape(shape), m_.reshape(shape), v_.reshape(shape)

    grad_x = dh[LEAD:][None]
    return (loss_sum, grad_x, *[g_out[n] for n in WEIGHTS], *[d_out[n] for n in WEIGHTS],
            *[m_out[n] for n in WEIGHTS], *[v_out[n] for n in WEIGHTS])
```

```python
import functools

import jax
import jax.numpy as jnp
from jax import lax
from jax.experimental import pallas as pl
from jax.experimental.pallas import tpu as pltpu

F32 = jnp.float32
BF16 = jnp.bfloat16
SDS = jax.ShapeDtypeStruct
MESH = pl.DeviceIdType.MESH
ANY_SPEC = pl.BlockSpec(memory_space=pl.ANY)

N_META = 16
CHUNK = 64
LEAD = CHUNK
PAD = LEAD - N_META
CONV_WIDTH = 31
CONV_ROWS = 32
POOL_WINDOWS = (2, 4, 8, 16)
HEAD_DIM = 128
SUB = 16
EXP_CAP = 80.0
EPS = 1e-6
ADAM_LR = 0.001
ADAM_B1 = 0.9
ADAM_B2 = 0.999
ADAM_EPS = 1e-08
ADAM_WD = 0.01
ADAM_STEP = 10
N_CHIPS = 4
VMEM_LIMIT = 52 << 20
MM_VMEM_BUDGET = 44 << 20


def _params(*sem):
    return pltpu.CompilerParams(dimension_semantics=sem if sem else None, vmem_limit_bytes=VMEM_LIMIT)


def _tile(n, target, unit=CHUNK):
    best = None
    for t in range(unit, min(n, target) + 1, unit):
        if n % t == 0:
            best = t
    assert best is not None, (n, target, unit)
    return best


def _ctile(n, target=512):
    for t in (512, 384, 256, 128):
        if t <= target and n % t == 0:
            return t
    raise ValueError(n)


def _mm_tiles(M, N, per_row, per_col, per_elem):
    best = None
    for tn in (512, 384, 256, 128):
        if N % tn:
            continue
        for tm in sorted((d for d in range(16, M + 1, 16) if M % d == 0), reverse=True):
            if 2 * (tm * per_row + tn * per_col + tm * tn * per_elem) <= MM_VMEM_BUDGET:
                if best is None or tm * tn > best[0] * best[1]:
                    best = (tm, tn)
                break
    assert best is not None, (M, N)
    return best


def _sigmoid(x):
    return 1.0 / (1.0 + jnp.exp(-x))


def _row_ids(shape, base):
    return lax.broadcasted_iota(jnp.int32, shape, 0) + base


def _cast_place(w3, layer, kind, chip1, dtype, name):
    _, ks, ns = w3.shape
    tr = _tile(ks, 512, 16)
    full = (ks, ns * N_CHIPS) if kind == "col" else (ks * N_CHIPS, ns)

    def body(chip_ref, w_ref, o_ref):
        del chip_ref
        o_ref[...] = w_ref[...].astype(dtype)

    omap = (lambda i, chip: (i, chip[0])) if kind == "col" else (lambda i, chip: (chip[0] * (ks // tr) + i, 0))
    return pl.pallas_call(
        body,
        grid_spec=pltpu.PrefetchScalarGridSpec(
            num_scalar_prefetch=1, grid=(ks // tr,),
            in_specs=[pl.BlockSpec((None, tr, ns), lambda i, chip: (layer, i, 0))],
            out_specs=pl.BlockSpec((tr, ns), omap)),
        out_shape=SDS(full, dtype), name=name, compiler_params=_params("parallel"))(chip1, w3)


def _rms_fwd(h, g3, layer, name, deps=()):
    T, D = h.shape
    tm = _tile(T, 832)

    def body(h_ref, g_ref, *rest):
        n_ref = rest[-1]
        x = h_ref[...]
        r = lax.rsqrt(jnp.mean(x * x, axis=-1, keepdims=True) + EPS)
        n_ref[...] = ((x * r) * g_ref[...]).astype(BF16)

    return pl.pallas_call(
        body, grid=(T // tm,),
        in_specs=[pl.BlockSpec((tm, D), lambda i: (i, 0)), pl.BlockSpec((None, 1, D), lambda i: (layer, 0, 0))]
        + [ANY_SPEC] * len(deps),
        out_specs=pl.BlockSpec((tm, D), lambda i: (i, 0)), out_shape=SDS((T, D), BF16),
        name=name, compiler_params=_params("parallel"))(h, g3, *deps)


def _final(h, g2, target):
    T, D = h.shape
    tm = _tile(T, 320)
    nsub = tm // CHUNK
    nblk = target.shape[0] // CHUNK

    def body(h_ref, g_ref, *rest):
        t_refs = rest[:nsub]
        dh_ref, dhb_ref, dg_ref, loss_ref = rest[nsub:]
        i = pl.program_id(0)

        @pl.when(i == 0)
        def _():
            dg_ref[...] = jnp.zeros_like(dg_ref)
            loss_ref[...] = jnp.zeros_like(loss_ref)

        g = g_ref[...]
        for q in range(nsub):
            rows = slice(q * CHUNK, (q + 1) * CHUNK)
            x = h_ref[rows, :]
            r = lax.rsqrt(jnp.mean(x * x, axis=-1, keepdims=True) + EPS)
            xh = x * r
            live = jnp.where(i * nsub + q > 0, 1.0, 0.0).astype(F32)
            e = ((xh * g) - t_refs[q][...]) * live
            dy = e * (1.0 / D)
            dxh = dy * g
            dh = r * (dxh - xh * jnp.mean(dxh * xh, axis=-1, keepdims=True))
            dh_ref[rows, :] = dh
            dhb_ref[rows, :] = dh.astype(BF16)
            dg_ref[...] += jnp.sum(dy * xh, axis=0, keepdims=True)
            loss_ref[...] += jnp.sum(e * e) * (0.5 / D)

    row = pl.BlockSpec((tm, D), lambda i: (i, 0))
    t_specs = [pl.BlockSpec((CHUNK, D), functools.partial(lambda i, q: (jnp.clip(i * nsub + q - 1, 0, nblk - 1), 0), q=q))
               for q in range(nsub)]
    return pl.pallas_call(
        body, grid=(T // tm,),
        in_specs=[row, pl.BlockSpec((1, D), lambda i: (0, 0))] + t_specs,
        out_specs=[row, row, pl.BlockSpec((1, D), lambda i: (0, 0)), pl.BlockSpec((1, 128), lambda i: (0, 0))],
        out_shape=[SDS((T, D), F32), SDS((T, D), BF16), SDS((1, D), F32), SDS((1, 128), F32)],
        name="final_loss", compiler_params=_params("arbitrary"))(h, g2, *([target] * nsub))


def _mm_nn(a, w3, layer, name, res=None, relu=False, square=False, deps=()):
    M, K = a.shape
    N = w3.shape[2]
    tm, tn = _mm_tiles(M, N, 2 * K, 2 * K, (2 if relu else 4) + (4 if res is not None else 0))

    def body(*refs):
        lhs = refs[0][...]
        acc = jnp.dot(lhs * lhs if square else lhs, refs[1][...], preferred_element_type=F32)
        if res is not None:
            acc = acc + refs[2][...]
        refs[-1][...] = jnp.maximum(acc, 0.0).astype(BF16) if relu else acc

    in_specs = [pl.BlockSpec((tm, K), lambda i, j: (i, 0)), pl.BlockSpec((None, K, tn), lambda i, j: (layer, 0, j))]
    args = [a, w3]
    tile = pl.BlockSpec((tm, tn), lambda i, j: (i, j))
    if res is not None:
        in_specs.append(tile)
        args.append(res)
    in_specs += [ANY_SPEC] * len(deps)
    args += list(deps)
    return pl.pallas_call(
        body, grid=(M // tm, N // tn), in_specs=in_specs, out_specs=tile,
        out_shape=SDS((M, N), BF16 if relu else F32),
        name=name, compiler_params=_params("parallel", "parallel"))(*args)


def _mm_nt(dy, w3, layer, name, relu=None, deps=()):
    M, N = dy.shape
    K = w3.shape[1]
    tm, tk = _mm_tiles(M, K, 2 * N, 2 * N, 4)

    def body(*refs):
        acc = lax.dot_general(refs[0][...], refs[1][...], (((1,), (1,)), ((), ())), preferred_element_type=F32)
        if relu is not None:
            acc = (acc * (2.0 * refs[2][...].astype(F32))).astype(BF16)
        refs[-1][...] = acc

    tile = pl.BlockSpec((tm, tk), lambda i, j: (i, j))
    in_specs = [pl.BlockSpec((tm, N), lambda i, j: (i, 0)), pl.BlockSpec((None, tk, N), lambda i, j: (layer, j, 0))]
    args = [dy, w3]
    if relu is not None:
        in_specs.append(tile)
        args.append(relu)
    in_specs += [ANY_SPEC] * len(deps)
    args += list(deps)
    return pl.pallas_call(
        body, grid=(M // tm, K // tk), in_specs=in_specs, out_specs=tile,
        out_shape=SDS((M, K), F32 if relu is None else BF16),
        name=name, compiler_params=_params("parallel", "parallel"))(*args)


def _row_tile(M, per_row, fixed):
    for tm in sorted((d for d in range(16, M + 1, 16) if M % d == 0), reverse=True):
        if 2 * (tm * per_row + fixed) <= MM_VMEM_BUDGET:
            return tm
    raise ValueError((M, per_row, fixed))


def _mm_nn_norm(a, w3, layer, res, g3, glayer, name, square=False, deps=()):
    M, K = a.shape
    D = w3.shape[2]
    tm = _row_tile(M, 2 * K + 10 * D, 2 * K * D)

    def body(a_ref, w_ref, r_ref, g_ref, *rest):
        h_ref, n_ref = rest[-2:]
        lhs = a_ref[...]
        x = r_ref[...] + jnp.dot(lhs * lhs if square else lhs, w_ref[...], preferred_element_type=F32)
        h_ref[...] = x
        r = lax.rsqrt(jnp.mean(x * x, axis=-1, keepdims=True) + EPS)
        n_ref[...] = ((x * r) * g_ref[...]).astype(BF16)

    row = pl.BlockSpec((tm, D), lambda i: (i, 0))
    return pl.pallas_call(
        body, grid=(M // tm,),
        in_specs=[pl.BlockSpec((tm, K), lambda i: (i, 0)), pl.BlockSpec((None, K, D), lambda i: (layer, 0, 0)), row,
                  pl.BlockSpec((None, 1, D), lambda i: (glayer, 0, 0))] + [ANY_SPEC] * len(deps),
        out_specs=[row, row], out_shape=[SDS((M, D), F32), SDS((M, D), BF16)],
        name=name, compiler_params=_params("parallel"))(a, w3, res, g3, *deps)


def _mlp_fwd(n2, w1, w2, res, g3, glayer, name, deps=()):
    M, D = n2.shape
    F = w1.shape[2]
    hb = _ctile(F)
    tm = _row_tile(M, 2 * D + 8 * D + (2 * D if g3 is not None else 0) + 2 * F, 2 * D * F)

    def body(n_ref, w1_ref, w2_ref, res_ref, *rest):
        outs = rest[-3:] if g3 is not None else rest[-2:]
        x = n_ref[...]
        acc = res_ref[...]
        for jb in range(F // hb):
            cols = slice(jb * hb, (jb + 1) * hb)
            r = jnp.maximum(jnp.dot(x, w1_ref[:, cols], preferred_element_type=F32), 0.0).astype(BF16)
            outs[-1][:, cols] = r
            acc = acc + jnp.dot(r * r, w2_ref[cols, :], preferred_element_type=F32)
        outs[0][...] = acc
        if g3 is not None:
            rr = lax.rsqrt(jnp.mean(acc * acc, axis=-1, keepdims=True) + EPS)
            outs[1][...] = ((acc * rr) * rest[0][...]).astype(BF16)

    row = pl.BlockSpec((tm, D), lambda i: (i, 0))
    once = dict(pipeline_mode=pl.Buffered(1))
    in_specs = [row, pl.BlockSpec((None, D, F), lambda i: (0, 0, 0), **once), pl.BlockSpec((None, F, D), lambda i: (0, 0, 0), **once), row]
    args = [n2, w1, w2, res]
    out_specs, out_shape = [row], [SDS((M, D), F32)]
    if g3 is not None:
        in_specs.append(pl.BlockSpec((None, 1, D), lambda i: (glayer, 0, 0)))
        args.append(g3)
        out_specs.append(row)
        out_shape.append(SDS((M, D), BF16))
    out_specs.append(pl.BlockSpec((tm, F), lambda i: (i, 0)))
    out_shape.append(SDS((M, F), BF16))
    in_specs += [ANY_SPEC] * len(deps)
    args += list(deps)
    return pl.pallas_call(
        body, grid=(M // tm,), in_specs=in_specs, out_specs=out_specs, out_shape=out_shape,
        name=name, compiler_params=_params("parallel"))(*args)


def _mm_nt_norm(dy, w3, layer, h, g3, glayer, dh_in, name, deps=()):
    M, N = dy.shape
    D = w3.shape[1]
    tm = _row_tile(M, 2 * N + 14 * D, 2 * N * D)

    def body(dy_ref, w_ref, h_ref, g_ref, dhi_ref, *rest):
        dh_ref, dhb_ref, dg_ref = rest[-3:]
        dn = lax.dot_general(dy_ref[...], w_ref[...], (((1,), (1,)), ((), ())), preferred_element_type=F32)
        x = h_ref[...]
        r = lax.rsqrt(jnp.mean(x * x, axis=-1, keepdims=True) + EPS)
        xh = x * r
        dxh = dn * g_ref[...]
        dh = dhi_ref[...] + r * (dxh - xh * jnp.mean(dxh * xh, axis=-1, keepdims=True))
        dh_ref[...] = dh
        dhb_ref[...] = dh.astype(BF16)

        @pl.when(pl.program_id(0) == 0)
        def _():
            dg_ref[...] = jnp.zeros_like(dg_ref)

        dg_ref[...] += jnp.sum(dn * xh, axis=0, keepdims=True)

    row = pl.BlockSpec((tm, D), lambda i: (i, 0))
    return pl.pallas_call(
        body, grid=(M // tm,),
        in_specs=[pl.BlockSpec((tm, N), lambda i: (i, 0)), pl.BlockSpec((None, D, N), lambda i: (layer, 0, 0)), row,
                  pl.BlockSpec((None, 1, D), lambda i: (glayer, 0, 0)), row] + [ANY_SPEC] * len(deps),
        out_specs=[row, row, pl.BlockSpec((1, D), lambda i: (0, 0))],
        out_shape=[SDS((M, D), F32), SDS((M, D), BF16), SDS((1, D), F32)],
        name=name, compiler_params=_params("arbitrary"))(dy, w3, h, g3, dh_in, *deps)


def _fam_dims(kind, K, N):
    return (K // 2, N // N_CHIPS) if kind == "col" else (K // (2 * N_CHIPS), N)


def _mm_tn(x, dy, kind, name, square=False):
    M, K = x.shape
    N = dy.shape[1]
    nr, nc = _fam_dims(kind, K, N)

    def body(x_ref, dy_ref, o_ref):
        lhs = x_ref[...]
        res = lax.dot_general(lhs * lhs if square else lhs, dy_ref[...], (((0,), (0,)), ((), ())), preferred_element_type=F32)
        o_ref[...] = res.astype(BF16).reshape(o_ref.shape)

    if kind == "col":
        tn = _ctile(nc)
        ct = nc // tn
        grid = (N // tn,)
        in_specs = [pl.BlockSpec((M, K), lambda j: (0, 0)), pl.BlockSpec((M, tn), lambda j: (0, j))]
        out_spec = pl.BlockSpec((2, None, nr, tn), lambda j: (0, j // ct, 0, j % ct))
    else:
        grid = (N_CHIPS,)
        in_specs = [pl.BlockSpec((M, 2 * nr), lambda i: (0, i)), pl.BlockSpec((M, N), lambda i: (0, 0))]
        out_spec = pl.BlockSpec((2, None, nr, N), lambda i: (0, i, 0, 0))
    return pl.pallas_call(
        body, grid=grid, in_specs=in_specs, out_specs=out_spec, out_shape=SDS((2, N_CHIPS, nr, nc), BF16),
        name=name, compiler_params=_params("parallel"))(x, dy)


C_EVEN = 512


def _live(rows, base, total):
    r = _row_ids((rows, 1), base)
    return jnp.logical_and(r >= PAD, r < total).astype(F32)


def _conv_taps(win, w_ref, ls, acc, flip):
    for b in range(8):
        rb = win if b == 0 else pltpu.roll(win, 96 - b, 0)
        for a in range(5):
            o = 8 * a + b
            tap = (30 - o) if flip else (o - 2)
            if 0 <= tap < CONV_WIDTH:
                acc = acc + w_ref[pl.ds(tap, 1), ls] * rb[8 * a:8 * a + CHUNK]
    return acc


def _window_sum(win, levels, forward):
    s = win
    n = win.shape[0]
    for k in range(levels):
        step = 1 << k
        s = s + pltpu.roll(s, (n - step) if forward else step, 0)
    return s


def _pool_count(base, g):
    pos = _row_ids((CHUNK, 1), base) - PAD
    return jnp.clip(pos + 1, 1, POOL_WINDOWS[g]).astype(F32)


def _even_fwd(u, cw3, cb3, lg3, lb3, pw4, pb3, ps3, j, name):
    T = u.shape[0]
    C = C_EVEN
    tm = _tile(T, 320)
    nch = tm // CHUNK
    nblk = T // CHUNK

    def body(u_ref, up_ref, cw_ref, cb_ref, lg_ref, lb_ref, pw_ref, pb_ref, ps_ref, o_ref, yc_ref, a_s, p_s, yc_s):
        row0 = pl.program_id(0) * tm
        up = up_ref[...]
        lp = _live(CHUNK, row0 - CHUNK, T)
        a_s[0:CHUNK, :] = up[:, 0:C] * _sigmoid(up[:, C:2 * C]) * lp
        p_s[0:CHUNK, :] = up[:, 2 * C:3 * C] * lp

        def stage(c, _):
            rs = pl.multiple_of(c * CHUNK, CHUNK)
            lv = _live(CHUNK, row0 + rs, T)
            a_s[pl.ds(rs + CHUNK, CHUNK), :] = u_ref[pl.ds(rs, CHUNK), 0:C] * _sigmoid(u_ref[pl.ds(rs, CHUNK), C:2 * C]) * lv
            p_s[pl.ds(rs + CHUNK, CHUNK), :] = u_ref[pl.ds(rs, CHUNK), 2 * C:3 * C] * lv
            return 0

        lax.fori_loop(0, nch, stage, 0)

        def chunk(c, _):
            rs = pl.multiple_of(c * CHUNK, CHUNK)
            lv = _live(CHUNK, row0 + rs, T)
            for cb in range(4):
                ls = slice(cb * 128, (cb + 1) * 128)
                win = a_s[pl.ds(pl.multiple_of(rs + 32, 32), 96), ls]
                acc = jnp.broadcast_to(cb_ref[:, ls], (CHUNK, 128))
                yc_s[:, ls] = _conv_taps(win, cw_ref, ls, acc, False)
            y = yc_s[...]
            yc_ref[pl.ds(rs, CHUNK), :] = y
            xc = y - jnp.mean(y, axis=-1, keepdims=True)
            yn = xc * lax.rsqrt(jnp.mean(xc * xc, axis=-1, keepdims=True) + EPS) * lg_ref[...] + lb_ref[...]
            o_ref[pl.ds(rs, CHUNK), 0:C] = (yn * _sigmoid(yn) * lv).astype(BF16)
            for g in range(4):
                ls = slice(g * 128, (g + 1) * 128)
                win = p_s[pl.ds(pl.multiple_of(rs + 48, 16), 80), ls]
                s = _window_sum(win, g + 1, False)
                d = s[16:80] / _pool_count(row0 + rs, g) - win[16:80]
                yv = jnp.dot(d.astype(BF16), pw_ref[g].astype(BF16), preferred_element_type=F32) + pb_ref[:, ls]
                o_ref[pl.ds(rs, CHUNK), C + g * 128:C + (g + 1) * 128] = (yv * ps_ref[:, ls] * lv).astype(BF16)
            return 0

        lax.fori_loop(0, nch, chunk, 0)

    vec = pl.BlockSpec((None, 1, C), lambda i: (j, 0, 0))
    return pl.pallas_call(
        body, grid=(T // tm,),
        in_specs=[pl.BlockSpec((tm, 3 * C), lambda i: (i, 0)),
                  pl.BlockSpec((CHUNK, 3 * C), lambda i: (jnp.maximum(i * nch - 1, 0), 0)),
                  pl.BlockSpec((None, CONV_ROWS, C), lambda i: (j, 0, 0)), vec, vec, vec,
                  pl.BlockSpec((None, 4, 128, 128), lambda i: (j, 0, 0, 0)), vec, vec],
        out_specs=[pl.BlockSpec((tm, 2 * C), lambda i: (i, 0)), pl.BlockSpec((tm, C), lambda i: (i, 0))],
        out_shape=[SDS((T, 2 * C), BF16), SDS((T, C), F32)],
        scratch_shapes=[pltpu.VMEM((tm + CHUNK, C), F32), pltpu.VMEM((tm + CHUNK, C), F32), pltpu.VMEM((CHUNK, C), F32)],
        name=name, compiler_params=_params("parallel"))(u, u, cw3, cb3, lg3, lb3, pw4, pb3, ps3)


def _even_bwd(u, yc, dy, cw3, cb3, lg3, lb3, pw4, pb3, ps3, j, name):
    T = u.shape[0]
    C = C_EVEN
    tm = _tile(T, 320)
    nch = tm // CHUNK
    nblk = T // CHUNK
    ntile = T // tm

    def body(u_ref, up_ref, un_ref, yc_ref, ycn_ref, dy_ref, dyn_ref, cw_ref, cb_ref, lg_ref, lb_ref, pw_ref, pb_ref, ps_ref,
             du_ref, dcw_ref, dcb_ref, dlg_ref, dlb_ref, dpw_ref, dpb_ref, dps_ref,
             a_s, p_s, dy_s, dyc_s, dd_s, ddc_s, dw_s):
        i = pl.program_id(0)
        row0 = i * tm

        @pl.when(i == 0)
        def _():
            for ref in (dcb_ref, dlg_ref, dlb_ref, dpw_ref, dpb_ref, dps_ref, dw_s):
                ref[...] = jnp.zeros_like(ref)

        up = up_ref[...]
        lp = _live(CHUNK, row0 - CHUNK, T)
        a_s[0:CHUNK, :] = up[:, 0:C] * _sigmoid(up[:, C:2 * C]) * lp
        p_s[0:CHUNK, :] = up[:, 2 * C:3 * C] * lp
        ln_ = _live(CHUNK, row0 + tm, T)
        p_s[tm + CHUNK:tm + 2 * CHUNK, :] = un_ref[:, 2 * C:3 * C] * ln_
        dy_s[tm:tm + CHUNK, :] = dyn_ref[...] * ln_
        dyc_s[tm + CHUNK:tm + CHUNK + 32, :] = jnp.zeros((32, C), F32)

        def stage(c, _):
            rs = pl.multiple_of(c * CHUNK, CHUNK)
            lv = _live(CHUNK, row0 + rs, T)
            a_s[pl.ds(rs + CHUNK, CHUNK), :] = u_ref[pl.ds(rs, CHUNK), 0:C] * _sigmoid(u_ref[pl.ds(rs, CHUNK), C:2 * C]) * lv
            p_s[pl.ds(rs + CHUNK, CHUNK), :] = u_ref[pl.ds(rs, CHUNK), 2 * C:3 * C] * lv
            dy_s[pl.ds(rs, CHUNK), :] = dy_ref[pl.ds(rs, CHUNK), :] * lv
            return 0

        lax.fori_loop(0, nch, stage, 0)

        def first(rs, y, own):
            xc = y - jnp.mean(y, axis=-1, keepdims=True)
            rstd = lax.rsqrt(jnp.mean(xc * xc, axis=-1, keepdims=True) + EPS)
            xh = xc * rstd
            yn = xh * lg_ref[...] + lb_ref[...]
            sg = _sigmoid(yn)
            dyn = dy_s[pl.ds(rs, CHUNK), 0:C] * (sg * (1.0 + yn * (1.0 - sg)))
            dlg_ref[...] += jnp.sum(dyn * xh, axis=0, keepdims=True) * own
            dlb_ref[...] += jnp.sum(dyn, axis=0, keepdims=True) * own
            dxh = dyn * lg_ref[...]
            dyc = rstd * (dxh - jnp.mean(dxh, axis=-1, keepdims=True) - xh * jnp.mean(dxh * xh, axis=-1, keepdims=True))
            dyc_s[pl.ds(rs, CHUNK), :] = dyc
            dcb_ref[...] += jnp.sum(dyc, axis=0, keepdims=True) * own
            for g in range(4):
                ls = slice(g * 128, (g + 1) * 128)
                win = p_s[pl.ds(rs + 48, 80), ls]
                s = _window_sum(win, g + 1, False)
                cnt = _pool_count(row0 + rs, g)
                d = (s[16:80] / cnt - win[16:80]).astype(BF16)
                w = pw_ref[g].astype(BF16)
                pre = jnp.dot(d, w, preferred_element_type=F32) + pb_ref[:, ls]
                dyb = dy_s[pl.ds(rs, CHUNK), C + g * 128:C + (g + 1) * 128]
                dpre = dyb * ps_ref[:, ls]
                dps_ref[:, ls] += jnp.sum(dyb * pre, axis=0, keepdims=True) * own
                dpb_ref[:, ls] += jnp.sum(dpre, axis=0, keepdims=True) * own
                dpre_b = (dpre * own).astype(BF16)
                dpw_ref[g] += lax.dot_general(d, dpre_b, (((0,), (0,)), ((), ())), preferred_element_type=F32)
                dd = lax.dot_general(dpre.astype(BF16), w, (((1,), (1,)), ((), ())), preferred_element_type=F32)
                dd_s[pl.ds(rs, CHUNK), ls] = dd
                ddc_s[pl.ds(rs, CHUNK), ls] = dd / cnt

        def first_in_tile(c, _):
            rs = pl.multiple_of(c * CHUNK, 16 * CHUNK // 16)
            first(rs, yc_ref[pl.ds(rs, CHUNK), :], 1.0)
            return 0

        lax.fori_loop(0, nch, first_in_tile, 0)
        first(tm, ycn_ref[...], 0.0)
        ddc_s[tm + CHUNK:tm + CHUNK + 16, :] = jnp.zeros((16, C), F32)

        def second(c, _):
            rs = pl.multiple_of(c * CHUNK, CHUNK)
            lv = _live(CHUNK, row0 + rs, T)
            for cb in range(4):
                ls = slice(cb * 128, (cb + 1) * 128)
                wd = dyc_s[pl.ds(rs, 96), ls]
                da = _conv_taps(wd, cw_ref, ls, jnp.zeros((CHUNK, 128), F32), True)
                wa = a_s[pl.ds(pl.multiple_of(rs + 32, 32), 96), ls]
                dyc = dyc_s[pl.ds(rs, CHUNK), ls]
                for b in range(8):
                    rb = wa if b == 0 else pltpu.roll(wa, 96 - b, 0)
                    for a in range(5):
                        tap = 8 * a + b - 2
                        if 0 <= tap < CONV_WIDTH:
                            prod = dyc * rb[8 * a:8 * a + CHUNK]
                            part = prod[0:8]
                            for q in range(1, 8):
                                part = part + prod[8 * q:8 * q + 8]
                            dw_s[8 * tap:8 * tap + 8, ls] += part
                val = u_ref[pl.ds(rs, CHUNK), ls]
                sg = _sigmoid(u_ref[pl.ds(rs, CHUNK), C + cb * 128:C + (cb + 1) * 128])
                du_ref[pl.ds(rs, CHUNK), ls] = (da * sg * lv).astype(BF16)
                du_ref[pl.ds(rs, CHUNK), C + cb * 128:C + (cb + 1) * 128] = (da * val * sg * (1.0 - sg) * lv).astype(BF16)
            for g in range(4):
                ls = slice(g * 128, (g + 1) * 128)
                z = _window_sum(ddc_s[pl.ds(rs, 80), ls], g + 1, True)
                dpin = (z[0:CHUNK] - dd_s[pl.ds(rs, CHUNK), ls]) * lv
                du_ref[pl.ds(rs, CHUNK), 2 * C + g * 128:2 * C + (g + 1) * 128] = dpin.astype(BF16)
            return 0

        lax.fori_loop(0, nch, second, 0)

        @pl.when(i == ntile - 1)
        def _():
            for tap in range(CONV_WIDTH):
                dcw_ref[tap:tap + 1, :] = jnp.sum(dw_s[8 * tap:8 * tap + 8, :], axis=0, keepdims=True)
            dcw_ref[CONV_WIDTH:CONV_ROWS, :] = jnp.zeros((CONV_ROWS - CONV_WIDTH, C), F32)

    vec = pl.BlockSpec((None, 1, C), lambda i: (j, 0, 0))
    ovec = pl.BlockSpec((1, C), lambda i: (0, 0))
    return pl.pallas_call(
        body, grid=(ntile,),
        in_specs=[pl.BlockSpec((tm, 3 * C), lambda i: (i, 0)),
                  pl.BlockSpec((CHUNK, 3 * C), lambda i: (jnp.maximum(i * nch - 1, 0), 0)),
                  pl.BlockSpec((CHUNK, 3 * C), lambda i: (jnp.minimum((i + 1) * nch, nblk - 1), 0)),
                  pl.BlockSpec((tm, C), lambda i: (i, 0)),
                  pl.BlockSpec((CHUNK, C), lambda i: (jnp.minimum((i + 1) * nch, nblk - 1), 0)),
                  pl.BlockSpec((tm, 2 * C), lambda i: (i, 0)),
                  pl.BlockSpec((CHUNK, 2 * C), lambda i: (jnp.minimum((i + 1) * nch, nblk - 1), 0)),
                  pl.BlockSpec((None, CONV_ROWS, C), lambda i: (j, 0, 0)), vec, vec, vec,
                  pl.BlockSpec((None, 4, 128, 128), lambda i: (j, 0, 0, 0)), vec, vec],
        out_specs=[pl.BlockSpec((tm, 3 * C), lambda i: (i, 0)), pl.BlockSpec((CONV_ROWS, C), lambda i: (0, 0)),
                   ovec, ovec, ovec, pl.BlockSpec((4, 128, 128), lambda i: (0, 0, 0)), ovec, ovec],
        out_shape=[SDS((T, 3 * C), BF16), SDS((CONV_ROWS, C), F32), SDS((1, C), F32), SDS((1, C), F32), SDS((1, C), F32),
                   SDS((4, 128, 128), F32), SDS((1, C), F32), SDS((1, C), F32)],
        scratch_shapes=[pltpu.VMEM((tm + CHUNK, C), F32), pltpu.VMEM((tm + 2 * CHUNK, C), F32),
                        pltpu.VMEM((tm + CHUNK, 2 * C), F32),
                        pltpu.VMEM((tm + CHUNK + 32, C), F32), pltpu.VMEM((tm + CHUNK, C), F32),
                        pltpu.VMEM((tm + CHUNK + 16, C), F32), pltpu.VMEM((8 * CONV_ROWS, C), F32)],
        name=name, compiler_params=_params("arbitrary"))(u, u, u, yc, yc, dy, dy, cw3, cb3, lg3, lb3, pw4, pb3, ps3)


HI = lax.Precision.HIGHEST


def _dot_nt(a, b):
    return lax.dot_general(a, b, (((1,), (1,)), ((), ())), preferred_element_type=F32)


def _dot_tn(a, b):
    return lax.dot_general(a, b, (((0,), (0,)), ((), ())), preferred_element_type=F32)


def _tri(lower):
    r = lax.broadcasted_iota(jnp.int32, (CHUNK, CHUNK), 0)
    c = lax.broadcasted_iota(jnp.int32, (CHUNK, CHUNK), 1)
    return jnp.where((c <= r) if lower else (c >= r), 1.0, 0.0).astype(F32)


def _hgrn_gates(u_ref, lb_ref, h, D, lv):
    ls = slice(h * HEAD_DIM, (h + 1) * HEAD_DIM)
    qraw = u_ref[:, ls]
    fraw = u_ref[:, D + h * HEAD_DIM:D + (h + 1) * HEAD_DIM]
    v = u_ref[:, 2 * D + h * HEAD_DIM:2 * D + (h + 1) * HEAD_DIM] * lv
    lbv = lb_ref[:, ls]
    sig = _sigmoid(fraw)
    forget = lbv + (1.0 - lbv) * sig
    logf = jnp.log(forget) * lv
    k = (1.0 - forget) * lv
    qsig = _sigmoid(qraw)
    q = qraw * qsig * lv
    return q, k, v, logf, (qraw, qsig, sig, forget, lbv)


def _sub_parts(q, k, b, b_s, I):
    rows = slice(SUB * I, SUB * (I + 1))
    rho = jnp.zeros((1, HEAD_DIM), F32) if I == 0 else b_s[SUB * I - 1:SUB * I, :]
    eI = jnp.exp(b[rows] - rho)
    EI = jnp.exp(jnp.minimum(rho - b, EXP_CAP))
    causal = (lax.broadcasted_iota(jnp.int32, (SUB, CHUNK), 1)
              <= lax.broadcasted_iota(jnp.int32, (SUB, CHUNK), 0) + SUB * I)
    return rows, q[rows] * eI, k * EI, eI, EI, causal


def _hgrn_fwd(u, lb3, layer, gn3, j, name):
    T = u.shape[0]
    D = u.shape[1] // 4
    H = D // HEAD_DIM
    NC = T // CHUNK

    def body(u_ref, lb_ref, gn_ref, y_ref, o_ref, sall_ref, st_s, b_s, lf_s, q_s, k_s):
        n = pl.program_id(0)

        @pl.when(n == 0)
        def _():
            st_s[...] = jnp.zeros_like(st_s)

        lv = _live(CHUNK, n * CHUNK, T)
        heads = range(H)
        cols = [slice(h * HEAD_DIM, (h + 1) * HEAD_DIM) for h in heads]
        vb = []
        for h in heads:
            q, k, v, logf, _ = _hgrn_gates(u_ref, lb_ref, h, D, lv)
            q_s[:, cols[h]] = q
            k_s[:, cols[h]] = k
            lf_s[:, cols[h]] = logf
            vb.append(v.astype(BF16))
        b_s[...] = jnp.dot(_tri(True), lf_s[...], precision=HI, preferred_element_type=F32)
        ops = []
        for h in heads:
            b_h = b_s.at[:, cols[h]]
            b = b_h[...]
            q = q_s[:, cols[h]]
            k = k_s[:, cols[h]]
            blast = b_h[CHUNK - 1:CHUNK, :]
            qh = (q * jnp.exp(b)).astype(BF16)
            kt = (k * jnp.exp(blast - b)).astype(BF16)
            subs = []
            for I in range(CHUNK // SUB):
                _, qI, KI, _, _, causal = _sub_parts(q, k, b, b_h, I)
                subs.append((qI.astype(BF16), KI.astype(BF16), causal))
            ops.append((qh, kt, jnp.exp(blast), subs))
        mm = []
        for h in heads:
            qh, kt, eblast, subs = ops[h]
            st = st_s[h]
            sall_ref[h] = st
            o_inter = _dot_nt(qh, st.astype(BF16))
            st_s[h] = st * eblast + _dot_tn(vb[h], kt)
            mm.append((o_inter, [_dot_nt(qI, KI) for qI, KI, _ in subs]))
        for h in heads:
            o_inter, ps = mm[h]
            p = jnp.concatenate([jnp.where(c, x, 0.0) for x, (_, _, c) in zip(ps, ops[h][3])], axis=0).astype(BF16)
            o = o_inter + jnp.dot(p, vb[h], preferred_element_type=F32)
            o_ref[:, cols[h]] = o
            graw = u_ref[:, 3 * D + h * HEAD_DIM:3 * D + (h + 1) * HEAD_DIM]
            r = lax.rsqrt(jnp.mean(o * o, axis=-1, keepdims=True) + EPS)
            y_ref[:, cols[h]] = (((o * r) * gn_ref[...]) * (graw * _sigmoid(graw))).astype(BF16)

    return pl.pallas_call(
        body, grid=(NC,),
        in_specs=[pl.BlockSpec((CHUNK, 4 * D), lambda n: (n, 0)),
                  pl.BlockSpec((None, 1, D), lambda n: (layer, 0, 0)),
                  pl.BlockSpec((None, 1, HEAD_DIM), lambda n: (j, 0, 0))],
        out_specs=[pl.BlockSpec((CHUNK, D), lambda n: (n, 0)), pl.BlockSpec((CHUNK, D), lambda n: (n, 0)),
                   pl.BlockSpec((None, H, HEAD_DIM, HEAD_DIM), lambda n: (n, 0, 0, 0))],
        out_shape=[SDS((T, D), BF16), SDS((T, D), F32), SDS((NC, H, HEAD_DIM, HEAD_DIM), F32)],
        scratch_shapes=[pltpu.VMEM((H, HEAD_DIM, HEAD_DIM), F32)] + [pltpu.VMEM((CHUNK, D), F32)] * 4,
        name=name, compiler_params=_params("arbitrary"))(u, lb3, gn3)


def _hgrn_bwd(u, o_raw, dy, sall, lb3, layer, gn3, j, name):
    T = u.shape[0]
    D = u.shape[1] // 4
    H = D // HEAD_DIM
    NC = T // CHUNK

    def body(u_ref, o_ref, dy_ref, sall_ref, lb_ref, gn_ref, du_ref, dlb_ref, dgn_ref, dst_s, b_s, lf_s, q_s, k_s, db_s, dk_s):
        step = pl.program_id(0)
        n = NC - 1 - step

        @pl.when(step == 0)
        def _():
            dst_s[...] = jnp.zeros_like(dst_s)
            dlb_ref[...] = jnp.zeros_like(dlb_ref)
            dgn_ref[...] = jnp.zeros_like(dgn_ref)

        lv = _live(CHUNK, n * CHUNK, T)
        last_row = (_row_ids((CHUNK, 1), 0) == CHUNK - 1).astype(F32)
        gn = gn_ref[...]
        heads = range(H)
        cols = [slice(h * HEAD_DIM, (h + 1) * HEAD_DIM) for h in heads]
        vb, dob = [], []
        dgn = jnp.zeros((1, HEAD_DIM), F32)
        for h in heads:
            q, k, v, logf, _ = _hgrn_gates(u_ref, lb_ref, h, D, lv)
            q_s[:, cols[h]] = q
            k_s[:, cols[h]] = k
            lf_s[:, cols[h]] = logf
            vb.append(v.astype(BF16))
            graw = u_ref[:, 3 * D + h * HEAD_DIM:3 * D + (h + 1) * HEAD_DIM]
            gsig = _sigmoid(graw)
            o = o_ref[:, cols[h]]
            r = lax.rsqrt(jnp.mean(o * o, axis=-1, keepdims=True) + EPS)
            xh = o * r
            dyv = dy_ref[:, cols[h]]
            dsg = dyv * (graw * gsig)
            dgn = dgn + jnp.sum(dsg * xh, axis=0, keepdims=True)
            dxh = dsg * gn
            do = r * (dxh - xh * jnp.mean(dxh * xh, axis=-1, keepdims=True))
            dob.append(do.astype(BF16))
            dgraw = dyv * xh * gn * (gsig * (1.0 + graw * (1.0 - gsig)))
            du_ref[:, 3 * D + h * HEAD_DIM:3 * D + (h + 1) * HEAD_DIM] = (dgraw * lv).astype(BF16)
        dgn_ref[...] += dgn
        b_s[...] = jnp.dot(_tri(True), lf_s[...], precision=HI, preferred_element_type=F32)
        ops = []
        for h in heads:
            b_h = b_s.at[:, cols[h]]
            b = b_h[...]
            q = q_s[:, cols[h]]
            k = k_s[:, cols[h]]
            blast = b_h[CHUNK - 1:CHUNK, :]
            eb = jnp.exp(b)
            ekb = jnp.exp(blast - b)
            subs = []
            for I in range(CHUNK // SUB):
                rows, qI, KI, eI, EI, causal = _sub_parts(q, k, b, b_h, I)
                subs.append((rows, qI.astype(BF16), KI.astype(BF16), eI, EI, causal))
            ops.append((eb, ekb, jnp.exp(blast), (q * eb).astype(BF16), (k * ekb).astype(BF16), subs))
        mm = []
        for h in heads:
            eb, ekb, eblast, qhb, ktb, subs = ops[h]
            st = sall_ref[h]
            dst = dst_s[h]
            dstb = dst.astype(BF16)
            dv = _dot_nt(ktb, dstb)
            dqh = jnp.dot(dob[h], st.astype(BF16), preferred_element_type=F32)
            dkt = jnp.dot(vb[h], dstb, preferred_element_type=F32)
            dblast = jnp.sum(dst * st, axis=0, keepdims=True) * eblast
            dst_s[h] = dst * eblast + _dot_tn(dob[h], qhb)
            dp_full = _dot_nt(dob[h], vb[h])
            ps = [_dot_nt(qIb, KIb) for _, qIb, KIb, _, _, _ in subs]
            mm.append((dv, dqh, dkt, dblast, dp_full, ps))
        for h in heads:
            eb, ekb, eblast, qhb, ktb, subs = ops[h]
            dv, dqh, dkt, dblast, dp_full, ps = mm[h]
            p = jnp.concatenate([jnp.where(sub[5], x, 0.0) for x, sub in zip(ps, subs)], axis=0).astype(BF16)
            dv = dv + _dot_tn(p, dob[h])
            du_ref[:, 2 * D + h * HEAD_DIM:2 * D + (h + 1) * HEAD_DIM] = (dv * lv).astype(BF16)
            dq = dqh * eb
            db = dqh * qhb.astype(F32)
            tmp = dkt * ktb.astype(F32)
            dk = dkt * ekb
            db = db - tmp
            dblast = dblast + jnp.sum(tmp, axis=0, keepdims=True)
            dq_parts, db_parts = [], []
            for rows, qIb, KIb, eI, EI, causal in subs:
                dp = jnp.where(causal, dp_full[rows], 0.0).astype(BF16)
                dqI = jnp.dot(dp, KIb, preferred_element_type=F32)
                dKI = _dot_tn(dp, qIb)
                dq_parts.append(dqI * eI)
                db_parts.append(dqI * qIb.astype(F32))
                dk = dk + dKI * EI
                db = db - dKI * KIb.astype(F32)
            dq = dq + jnp.concatenate(dq_parts, axis=0)
            db_s[:, cols[h]] = db + jnp.concatenate(db_parts, axis=0) + last_row * dblast
            dk_s[:, cols[h]] = dk
            qraw = u_ref[:, cols[h]]
            qsig = _sigmoid(qraw)
            du_ref[:, cols[h]] = (dq * (qsig * (1.0 + qraw * (1.0 - qsig))) * lv).astype(BF16)
        lf_s[...] = jnp.dot(_tri(False), db_s[...], precision=HI, preferred_element_type=F32)
        for h in heads:
            fraw = u_ref[:, D + h * HEAD_DIM:D + (h + 1) * HEAD_DIM]
            lbv = lb_ref[:, cols[h]]
            sig = _sigmoid(fraw)
            forget = lbv + (1.0 - lbv) * sig
            dforget = (lf_s[:, cols[h]] / forget - dk_s[:, cols[h]]) * lv
            dlb_ref[:, cols[h]] += jnp.sum(dforget * (1.0 - sig), axis=0, keepdims=True)
            du_ref[:, D + h * HEAD_DIM:D + (h + 1) * HEAD_DIM] = (dforget * (1.0 - lbv) * sig * (1.0 - sig)).astype(BF16)

    rev = lambda s: (NC - 1 - s, 0)
    return pl.pallas_call(
        body, grid=(NC,),
        in_specs=[pl.BlockSpec((CHUNK, 4 * D), rev), pl.BlockSpec((CHUNK, D), rev), pl.BlockSpec((CHUNK, D), rev),
                  pl.BlockSpec((None, H, HEAD_DIM, HEAD_DIM), lambda s: (NC - 1 - s, 0, 0, 0)),
                  pl.BlockSpec((None, 1, D), lambda s: (layer, 0, 0)),
                  pl.BlockSpec((None, 1, HEAD_DIM), lambda s: (j, 0, 0))],
        out_specs=[pl.BlockSpec((CHUNK, 4 * D), rev), pl.BlockSpec((1, D), lambda s: (0, 0)),
                   pl.BlockSpec((1, HEAD_DIM), lambda s: (0, 0))],
        out_shape=[SDS((T, 4 * D), BF16), SDS((1, D), F32), SDS((1, HEAD_DIM), F32)],
        scratch_shapes=[pltpu.VMEM((H, HEAD_DIM, HEAD_DIM), F32)] + [pltpu.VMEM((CHUNK, D), F32)] * 6,
        name=name, compiler_params=_params("arbitrary"))(u, o_raw, dy, sall, lb3, gn3)


def _softmax_layers(p_ref, n_layers):
    rows = [p_ref[l:l + 1, :] for l in range(n_layers)]
    m = functools.reduce(jnp.maximum, rows)
    e = [jnp.exp(x - m) for x in rows]
    tot = functools.reduce(lambda a, b: a + b, e)
    return [x / tot for x in e]


def _lb_fwd(p):
    n_layers, D = p.shape

    def body(p_ref, o_ref):
        s = _softmax_layers(p_ref, n_layers)
        acc = jnp.zeros((1, D), F32)
        o_ref[0:1, :] = acc
        for l in range(1, n_layers):
            acc = acc + s[l]
            o_ref[l:l + 1, :] = acc

    return pl.pallas_call(body, out_shape=SDS(p.shape, F32), name="lb_fwd")(p)


def _lb_bwd(p, dlb):
    n_layers, D = p.shape

    def body(p_ref, d_ref, o_ref):
        s = _softmax_layers(p_ref, n_layers)
        ds = [jnp.zeros((1, D), F32)] * n_layers
        acc = jnp.zeros((1, D), F32)
        for l in range(n_layers - 1, 0, -1):
            acc = acc + d_ref[l:l + 1, :]
            ds[l] = acc
        dot = functools.reduce(lambda a, b: a + b, [s[l] * ds[l] for l in range(n_layers)])
        for l in range(n_layers):
            o_ref[l:l + 1, :] = s[l] * (ds[l] - dot)

    return pl.pallas_call(body, out_shape=SDS(p.shape, F32), name="lb_bwd")(p, dlb)


def _adamw(w, g, m, v, name):
    R, C = w.shape
    tr = _tile(R, 256, 8) if R % 8 == 0 else R

    def body(w_ref, g_ref, m_ref, v_ref, d_ref, mo_ref, vo_ref):
        g_ = g_ref[...]
        m_ = ADAM_B1 * m_ref[...] + (1.0 - ADAM_B1) * g_
        v_ = ADAM_B2 * v_ref[...] + (1.0 - ADAM_B2) * (g_ * g_)
        mh = m_ / (1.0 - ADAM_B1 ** ADAM_STEP)
        vh = v_ / (1.0 - ADAM_B2 ** ADAM_STEP)
        d_ref[...] = -ADAM_LR * (mh / (jnp.sqrt(vh) + ADAM_EPS) + ADAM_WD * w_ref[...])
        mo_ref[...] = m_
        vo_ref[...] = v_

    blk = pl.BlockSpec((tr, C), lambda i: (i, 0))
    return pl.pallas_call(
        body, grid=(R // tr,), in_specs=[blk] * 4, out_specs=[blk] * 3, out_shape=[SDS((R, C), F32)] * 3,
        name=name, compiler_params=_params("parallel"))(w, g, m, v)


def _adamw_layer(w3, m3, v3, g2, layer, outs, name):
    L, R, C = w3.shape
    tr = _tile(R, 256, 8)
    if outs is None:
        outs = tuple(lax.empty(w3.shape, F32) for _ in range(4))

    def body(w_ref, m_ref, v_ref, g_ref, a0, a1, a2, a3, go_ref, d_ref, mo_ref, vo_ref):
        del a0, a1, a2, a3
        g_ = g_ref[...]
        m_ = ADAM_B1 * m_ref[...] + (1.0 - ADAM_B1) * g_
        v_ = ADAM_B2 * v_ref[...] + (1.0 - ADAM_B2) * (g_ * g_)
        mh = m_ / (1.0 - ADAM_B1 ** ADAM_STEP)
        vh = v_ / (1.0 - ADAM_B2 ** ADAM_STEP)
        go_ref[...] = g_
        d_ref[...] = -ADAM_LR * (mh / (jnp.sqrt(vh) + ADAM_EPS) + ADAM_WD * w_ref[...])
        mo_ref[...] = m_
        vo_ref[...] = v_

    lay = pl.BlockSpec((None, tr, C), lambda i: (layer, i, 0))
    return pl.pallas_call(
        body, grid=(R // tr,), in_specs=[lay] * 3 + [pl.BlockSpec((tr, C), lambda i: (i, 0))] + [ANY_SPEC] * 4,
        out_specs=[lay] * 4, out_shape=[SDS(w3.shape, F32)] * 4, input_output_aliases={4: 0, 5: 1, 6: 2, 7: 3},
        name=name, compiler_params=_params("parallel"))(w3, m3, v3, g2, *outs)


SEM_SPEC = pl.BlockSpec(memory_space=pltpu.SEMAPHORE)
HBM_SPEC = pl.BlockSpec(memory_space=pltpu.HBM)
EFFECT = pltpu.SideEffectType.DATAFLOW_SIDE_EFFECTING
N_DEV = 2 * N_CHIPS


def _position():
    x, y, c = lax.axis_index("x"), lax.axis_index("y"), lax.axis_index("c")
    chips = [(1 - x, y), (x, 1 - y), (1 - x, 1 - y)]
    return x, y, c, chips


def _split_start(name, plan, bufs, n_sems, deps=(), earlier=None):
    n = len(bufs)
    held = () if earlier is None else tuple(earlier[1:])

    def body(*refs):
        first_out = n + len(held) + len(deps)
        if earlier is not None:
            sends, recvs = earlier[0](refs[:n], refs[n], refs[n + 1])
            for kw in sends:
                pltpu.make_async_remote_copy(**kw).wait_send()
            for kw in recvs:
                pltpu.make_async_remote_copy(**kw).wait_recv()
        sends, _ = plan(refs[:n], refs[first_out], refs[first_out + 1])
        for kw in sends:
            pltpu.make_async_remote_copy(**kw).start()
        refs[-1][...] = jnp.zeros_like(refs[-1])

    out = pl.pallas_call(
        body, name=name,
        out_shape=(pltpu.SemaphoreType.DMA((n_sems,)), pltpu.SemaphoreType.DMA((n_sems,)),
                   *[pltpu.HBM(b.shape, b.dtype) for b in bufs], SDS((8, 128), F32)),
        in_specs=[HBM_SPEC] * n + [SEM_SPEC] * len(held) + [ANY_SPEC] * len(deps),
        out_specs=(SEM_SPEC, SEM_SPEC, *[HBM_SPEC] * n, pl.BlockSpec(memory_space=pltpu.VMEM)),
        input_output_aliases={i: 2 + i for i in range(n)},
        compiler_params=pltpu.CompilerParams(has_side_effects=EFFECT),
    )(*[pltpu.with_memory_space_constraint(b, pltpu.HBM) for b in bufs], *held, *deps)
    return out[0], out[1], list(out[2:2 + n]), out[-1]


def _split_wait(name, plan, send_sems, recv_sems, bufs, after=()):
    n = len(bufs)

    def body(*refs):
        sends, recvs = plan(refs[:n], refs[n], refs[n + 1])
        for kw in sends:
            pltpu.make_async_remote_copy(**kw).wait_send()
        for kw in recvs:
            pltpu.make_async_remote_copy(**kw).wait_recv()

    out = pl.pallas_call(
        body, name=name, out_shape=tuple(pltpu.HBM(b.shape, b.dtype) for b in bufs),
        in_specs=[HBM_SPEC] * n + [SEM_SPEC, SEM_SPEC] + [ANY_SPEC] * len(after),
        out_specs=tuple([HBM_SPEC] * n), input_output_aliases={i: i for i in range(n)},
        compiler_params=pltpu.CompilerParams(has_side_effects=EFFECT),
    )(*bufs, send_sems, recv_sems, *after)
    return list(out)


def _region(kind, ref, chip, half):
    K, N = ref.shape
    if kind == "col":
        return ref.at[pl.ds(half * (K // 2), K // 2), pl.ds(chip * (N // N_CHIPS), N // N_CHIPS)]
    rows = K // (2 * N_CHIPS)
    return ref.at[pl.ds((2 * chip + half) * rows, rows), :]


def _gather_plan(kinds, over_chips):
    def plan(refs, send_sems, recv_sems):
        x, y, c, chips = _position()
        sends, recvs = [], []
        for f, (ref, kind) in enumerate(zip(refs, kinds)):
            for k, chip in enumerate(chips):
                theirs = 2 * chip[0] + chip[1]
                sem = dict(send_sem=send_sems.at[3 * f + k], recv_sem=recv_sems.at[3 * f + k], device_id_type=MESH)
                if over_chips:
                    out, back, to = _region(kind, ref, 2 * x + y, c), _region(kind, ref, theirs, c), (*chip, c)
                else:
                    out, back, to = _region(kind, ref, theirs, c), _region(kind, ref, theirs, 1 - c), (x, y, 1 - c)
                sends.append(dict(src_ref=out, dst_ref=out, device_id=to, **sem))
                recvs.append(dict(src_ref=back, dst_ref=back, device_id=to, **sem))
        return sends, recvs
    return plan


def _reduce_plan(refs, send_sems, recv_sems):
    x, y, c, _ = _position()
    me = 4 * x + 2 * y + c
    sends, recvs = [], []
    for f in range(len(refs) // 2):
        acc, land = refs[2 * f], refs[2 * f + 1]
        for d in range(1, N_DEV):
            t = (me + d) % N_DEV
            to = dict(device_id=(t // 4, (t // 2) % 2, t % 2), device_id_type=MESH)
            slot = N_DEV - 1 - d
            sends.append(dict(src_ref=acc.at[t % 2, t // 2], dst_ref=land.at[slot], send_sem=send_sems.at[7 * f + d - 1],
                              recv_sem=recv_sems.at[7 * f + slot], **to))
            recvs.append(dict(src_ref=land.at[d - 1], dst_ref=land.at[d - 1], send_sem=send_sems.at[7 * f + d - 1],
                              recv_sem=recv_sems.at[7 * f + d - 1], **to))
    return sends, recvs


def _swap_plan(refs, send_sems, recv_sems):
    x, y, c, _ = _position()
    sends, recvs = [], []
    for f, g in enumerate(refs):
        sem = dict(send_sem=send_sems.at[f], recv_sem=recv_sems.at[f], device_id=(x, y, 1 - c), device_id_type=MESH)
        sends.append(dict(src_ref=g.at[c], dst_ref=g.at[c], **sem))
        recvs.append(dict(src_ref=g.at[1 - c], dst_ref=g.at[1 - c], **sem))
    return sends, recvs


def _sum_pieces(ids2, acc, land, name):
    _, _, nr, nc = acc.shape
    tr = _tile(nr, 256, 16)

    def body(ids_ref, own_ref, land_ref, o_ref):
        del ids_ref
        s = own_ref[...].astype(F32)
        for k in range(N_DEV - 1):
            s = s + land_ref[k].astype(F32)
        o_ref[...] = s

    return pl.pallas_call(
        body,
        grid_spec=pltpu.PrefetchScalarGridSpec(
            num_scalar_prefetch=1, grid=(nr // tr,),
            in_specs=[pl.BlockSpec((None, None, tr, nc), lambda i, ids: (ids[0], ids[1], i, 0)),
                      pl.BlockSpec((N_DEV - 1, tr, nc), lambda i, ids: (0, i, 0))],
            out_specs=pl.BlockSpec((None, tr, nc), lambda i, ids: (ids[0], i, 0))),
        out_shape=SDS((2, nr, nc), F32), name=name, compiler_params=_params("parallel"))(ids2, acc, land)


def _small_plan(refs, send_sems, recv_sems):
    x, y, c, _ = _position()
    me = 4 * x + 2 * y + c
    own, land = refs
    sends, recvs = [], []
    for d in range(1, N_DEV):
        t = (me + d) % N_DEV
        to = dict(device_id=(t // 4, (t // 2) % 2, t % 2), device_id_type=MESH)
        sends.append(dict(src_ref=own, dst_ref=land.at[me], send_sem=send_sems.at[d - 1],
                          recv_sem=recv_sems.at[N_DEV - 1 - d], **to))
        recvs.append(dict(src_ref=land.at[t], dst_ref=land.at[t], send_sem=send_sems.at[d - 1],
                          recv_sem=recv_sems.at[d - 1], **to))
    return sends, recvs


def _sum_blocks(me1, own, land):
    def body(me_ref, own_ref, land_ref, o_ref):
        acc = None
        for d in range(N_DEV):
            term = jnp.where(me_ref[0] == d, own_ref[...], land_ref[d])
            acc = term if acc is None else acc + term
        o_ref[...] = acc

    return pl.pallas_call(
        body,
        grid_spec=pltpu.PrefetchScalarGridSpec(
            num_scalar_prefetch=1, grid=(1,),
            in_specs=[pl.BlockSpec(own.shape, lambda i, me: (0, 0)), pl.BlockSpec(land.shape, lambda i, me: (0, 0, 0))],
            out_specs=pl.BlockSpec(own.shape, lambda i, me: (0, 0))),
        out_shape=SDS(own.shape, F32), name="sum_small", compiler_params=_params("arbitrary"))(me1, own, land)


BIG = {"ev_w_in": "col", "ev_w_out": "row", "od_w_in": "col", "od_w_out": "row", "mlp_w1": "col", "mlp_w2": "row"}
WEIGHTS = ("meta_tokens", "mix_norm_g", "mlp_norm_g", "final_norm_g", "ev_w_in", "ev_conv_w", "ev_conv_b", "ev_ln_g",
           "ev_ln_b", "ev_pool_w", "ev_pool_b", "ev_pool_scale", "ev_w_out", "od_w_in", "od_gnorm_g", "od_w_out",
           "lb_param", "mlp_w1", "mlp_w2")
PACK_UNIT = 1024


def _mixer_names(layer):
    return ("ev_w_in", "ev_w_out") if layer % 2 == 0 else ("od_w_in", "od_w_out")


def _pack(arrays):
    flat = []
    for a in arrays:
        a = a.reshape(-1)
        flat.append(jnp.pad(a, (0, (-a.shape[0]) % PACK_UNIT)))
    return jnp.concatenate(flat).reshape(-1, 128)


def _unpack(packed, shapes):
    flat = packed.reshape(-1)
    out, off = [], 0
    for s in shapes:
        size = 1
        for d in s:
            size *= d
        out.append(flat[off:off + size].reshape(s))
        off += size + (-size) % PACK_UNIT
    return out


def _local_step(x2, target, P, weights, boundary, first_deps=()):
    D = x2.shape[1]
    n_layers = P["mix_norm_g"].shape[0]
    h = jnp.concatenate([jnp.zeros((PAD, D), F32), P["meta_full"], x2], axis=0)
    mix_g = P["mix_norm_g"].reshape(n_layers, 1, D)
    mlp_g = P["mlp_norm_g"].reshape(n_layers, 1, D)
    vec = lambda a: a.reshape(a.shape[0], 1, -1)
    cb3, lg3, lnb3, ps3 = vec(P["ev_conv_b"]), vec(P["ev_ln_g"]), vec(P["ev_ln_b"]), vec(P["ev_pool_scale"])
    pb3 = vec(P["ev_pool_b"])
    gn3 = vec(P["od_gnorm_g"])
    lb_all = _lb_fwd(P["lb_param"])
    lb3 = lb_all.reshape(n_layers, 1, D)
    even = (cb3, lg3, lnb3, P["ev_pool_w"], pb3, ps3)

    saved = []
    deps = tuple(first_deps)
    for layer in range(n_layers):
        j = layer // 2
        w_in, w_out = _mixer_names(layer)
        W = {}
        s = {"h": h, "W": W}
        s["n"] = _rms_fwd(h, mix_g, layer, "mix_norm_0", deps=deps) if layer == 0 else n_next
        deps = ()
        W[w_in], held = weights(layer, w_in, (s["n"],))
        s["u"] = _mm_nn(s["n"], W[w_in], 0, f"mix_in_{layer}", deps=held)
        if layer % 2 == 0:
            s["y"], s["yc"] = _even_fwd(s["u"], P["conv_w_full"], *even, j, f"even_fwd_{layer}")
        else:
            s["y"], s["o"], s["sall"] = _hgrn_fwd(s["u"], lb3, layer, gn3, j, f"hgrn_fwd_{layer}")
        W[w_out], held = weights(layer, w_out, (s["y"],))
        h, s["n2"] = _mm_nn_norm(s["y"], W[w_out], 0, h, mlp_g, layer, f"mix_out_{layer}", deps=held)
        s["h1"] = h
        W["mlp_w1"], held = weights(layer, "mlp_w1", (s["n2"],))
        if layer == 0:
            s["relu"] = _mm_nn(s["n2"], W["mlp_w1"], 0, "mlp_up_0", relu=True, deps=held)
            W["mlp_w2"], held = weights(layer, "mlp_w2", (s["relu"],))
            h, n_next = _mm_nn_norm(s["relu"], W["mlp_w2"], 0, h, mix_g, 1, "mlp_down_0", square=True, deps=held)
        else:
            W["mlp_w2"], more = weights(layer, "mlp_w2", (s["n2"],))
            last = layer + 1 == n_layers
            out = _mlp_fwd(s["n2"], W["mlp_w1"], W["mlp_w2"], h, None if last else mix_g, layer + 1, f"mlp_{layer}", deps=held + more)
            h, s["relu"] = out[0], out[-1]
            n_next = None if last else out[1]
        saved.append(s)

    dh, dhb, dg_final, loss = _final(h, P["final_norm_g"].reshape(1, D), target)

    small = {"final_norm_g": dg_final}
    per_layer = {k: [None] * n_layers for k in ("mix_norm_g", "mlp_norm_g", "lb")}
    per_pair = {k: [None] * (n_layers // 2) for k in
                ("ev_conv_w", "ev_conv_b", "ev_ln_g", "ev_ln_b", "ev_pool_w", "ev_pool_b", "ev_pool_scale", "od_gnorm_g")}
    for layer in reversed(range(n_layers)):
        j = layer // 2
        s = saved[layer]
        W = s["W"]
        w_in, w_out = _mixer_names(layer)
        dz = _mm_nt(dhb, W["mlp_w2"], 0, f"d_act_{layer}", relu=s["relu"], deps=deps)
        dw2 = _mm_tn(s["relu"], dhb, "row", f"dw2_{layer}", square=True)
        dw1 = _mm_tn(s["n2"], dz, "col", f"dw1_{layer}")
        dh, dhb, per_layer["mlp_norm_g"][layer] = _mm_nt_norm(dz, W["mlp_w1"], 0, s["h1"], mlp_g, layer, dh, f"d_n2_{layer}")
        deps = boundary(f"mlp{layer}", {("mlp_w1", layer): dw1, ("mlp_w2", layer): dw2}, (dhb, dw1, dw2))
        dy = _mm_nt(dhb, W[w_out], 0, f"d_y_{layer}", deps=deps)
        dwout = _mm_tn(s["y"], dhb, "row", f"dwout_{layer}")
        if layer % 2 == 0:
            du, dcw, dcb, dlg, dlnb, dpw, dpb, dps = _even_bwd(s["u"], s["yc"], dy, P["conv_w_full"], *even, j, f"even_bwd_{layer}")
            for k, val in (("ev_conv_w", dcw), ("ev_conv_b", dcb), ("ev_ln_g", dlg), ("ev_ln_b", dlnb),
                           ("ev_pool_w", dpw), ("ev_pool_b", dpb), ("ev_pool_scale", dps)):
                per_pair[k][j] = val
        else:
            du, per_layer["lb"][layer], per_pair["od_gnorm_g"][j] = _hgrn_bwd(
                s["u"], s["o"], dy, s["sall"], lb3, layer, gn3, j, f"hgrn_bwd_{layer}")
        dwin = _mm_tn(s["n"], du, "col", f"dwin_{layer}")
        deps = boundary(f"mix{layer}", {(w_in, j): dwin, (w_out, j): dwout}, (du, dwin, dwout))
        dh, dhb, per_layer["mix_norm_g"][layer] = _mm_nt_norm(du, W[w_in], 0, s["h"], mix_g, layer, dh, f"d_n_{layer}", deps=deps)
        deps = ()

    small["mix_norm_g"] = jnp.concatenate(per_layer["mix_norm_g"], axis=0)
    small["mlp_norm_g"] = jnp.concatenate(per_layer["mlp_norm_g"], axis=0)
    dlb_all = jnp.concatenate([jnp.zeros((1, D), F32) if g is None else g for g in per_layer["lb"]], axis=0)
    small["lb_param"] = _lb_bwd(P["lb_param"], dlb_all)
    for k, vals in per_pair.items():
        small[k] = jnp.stack(vals, axis=0)
    small["meta_tokens"] = dh[PAD:LEAD]
    return loss, dh, small


def kernel(x, meta_tokens, mix_norm_g, mlp_norm_g, final_norm_g, ev_w_in, ev_conv_w, ev_conv_b, ev_ln_g, ev_ln_b, ev_pool_w, ev_pool_b, ev_pool_scale, ev_w_out, od_w_in, od_gnorm_g, od_w_out, lb_param, mlp_w1, mlp_w2, loss_target, m_meta_tokens, m_mix_norm_g, m_mlp_norm_g, m_final_norm_g, m_ev_w_in, m_ev_conv_w, m_ev_conv_b, m_ev_ln_g, m_ev_ln_b, m_ev_pool_w, m_ev_pool_b, m_ev_pool_scale, m_ev_w_out, m_od_w_in, m_od_gnorm_g, m_od_w_out, m_lb_param, m_mlp_w1, m_mlp_w2, v_meta_tokens, v_mix_norm_g, v_mlp_norm_g, v_final_norm_g, v_ev_w_in, v_ev_conv_w, v_ev_conv_b, v_ev_ln_g, v_ev_ln_b, v_ev_pool_w, v_ev_pool_b, v_ev_pool_scale, v_ev_w_out, v_od_w_in, v_od_gnorm_g, v_od_w_out, v_lb_param, v_mlp_w1, v_mlp_w2):
    given = dict(locals())
    w = {n: given[n] for n in WEIGHTS}
    m = {n: given["m_" + n] for n in WEIGHTS}
    v = {n: given["v_" + n] for n in WEIGHTS}
    n_layers = mix_norm_g.shape[0]
    core = lax.axis_index("c").astype(jnp.int32)
    chip = (2 * lax.axis_index("x") + lax.axis_index("y")).astype(jnp.int32)
    chip1 = chip.reshape(1)
    ids2 = jnp.stack([core, chip])

    conv_pad = jnp.pad(ev_conv_w, ((0, 0), (0, CONV_ROWS - CONV_WIDTH), (0, 0)))
    stages = [[(0, n)] for n in (*_mixer_names(0), "mlp_w1", "mlp_w2")]
    for layer in range(1, n_layers):
        stages += [[(layer, n) for n in _mixer_names(layer)], [(layer, "mlp_w1"), (layer, "mlp_w2")]]
    gathers, where, token = [], {}, ()
    for k, stage in enumerate(stages):
        index = [layer if n.startswith("mlp") else layer // 2 for layer, n in stage]
        kinds = [BIG[n] for _, n in stage]
        bufs = [_cast_place(w[n], i, BIG[n], chip1, BF16, f"place_{n}_{i}") for (_, n), i in zip(stage, index)]
        if k == 0:
            bufs.append(_cast_place(meta_tokens[None], 0, "col", chip1, F32, "place_meta"))
            bufs.append(_cast_place(conv_pad.reshape(1, -1, conv_pad.shape[2]), 0, "col", chip1, F32, "place_conv_w"))
            kinds += ["col", "col"]
        plan = _gather_plan(kinds, True)
        ss, rs, bufs, tok = _split_start(f"gather_start_{k}", plan, bufs, 3 * len(bufs), deps=token)
        token = (tok,)
        gathers.append((kinds, plan, ss, rs, bufs))
        where.update({key: (k, f) for f, key in enumerate(stage)})

    landed, passed, held = {}, {}, []

    def hand_on(k, deps):
        if k not in passed:
            kinds, plan, ss, rs, bufs = gathers[k]
            to_sibling = _gather_plan(kinds, False)
            ss, rs, bufs, tok = _split_start(f"gather_pass_{k}", to_sibling, bufs, 3 * len(bufs), deps=deps, earlier=(plan, ss, rs))
            passed[k] = (to_sibling, ss, rs, bufs)
            held.append(tok)

    def arrived(k, after):
        if k not in landed:
            hand_on(k, after)
            landed[k] = _split_wait(f"gather_wait_{k}", *passed[k], after)
        return landed[k]

    def weights(layer, name, after):
        k, f = where[(layer, name)]
        full = arrived(k, after)[f][None]
        if name == "mlp_w2" and layer + 1 < n_layers:
            hand_on(where[(layer + 1, _mixer_names(layer + 1)[0])][0], after)
        if layer > 0 and name == _mixer_names(layer)[0]:
            hand_on(where[(layer, "mlp_w1")][0], after)
        tokens = tuple(held)
        held.clear()
        return full, tokens

    first = arrived(0, token)
    P = {n: w[n] for n in ("mix_norm_g", "mlp_norm_g", "final_norm_g", "ev_conv_b", "ev_ln_g", "ev_ln_b", "ev_pool_w",
                           "ev_pool_b", "ev_pool_scale", "od_gnorm_g", "lb_param")}
    P["meta_full"] = first[1]
    P["conv_w_full"] = first[2].reshape(ev_conv_w.shape[0], CONV_ROWS, -1)

    pending, outs = [], {n: None for n in BIG}

    def advance(after, fresh=1):
        tokens, still = [], []
        for pos, st in enumerate(pending):
            if st["phase"] == 1 and pos >= len(pending) - fresh:
                still.append(st)
            elif st["phase"] == 1:
                bufs = _split_wait(f"reduce_wait_{st['tag']}", _reduce_plan, st["ss"], st["rs"], st["bufs"], after)
                halves = [_sum_pieces(ids2, bufs[2 * f], bufs[2 * f + 1], f"sum_{st['tag']}_{f}") for f in range(len(bufs) // 2)]
                ss, rs, halves, tok = _split_start(f"swap_start_{st['tag']}", _swap_plan, halves, len(halves))
                tokens.append(tok)
                still.append(dict(st, phase=2, ss=ss, rs=rs, bufs=halves))
            else:
                grads = _split_wait(f"swap_wait_{st['tag']}", _swap_plan, st["ss"], st["rs"], st["bufs"], after)
                for (n, i), g in zip(st["keys"], grads):
                    outs[n] = _adamw_layer(w[n], m[n], v[n], g.reshape(w[n].shape[1:]), i, outs[n], f"adamw_{n}_{i}")
        pending[:] = still
        return tokens

    def boundary(tag, grads, after):
        tokens = advance(after)
        bufs = []
        for acc in grads.values():
            bufs += [acc, lax.empty((N_DEV - 1,) + acc.shape[2:], BF16)]
        ss, rs, bufs, tok = _split_start(f"reduce_start_{tag}", _reduce_plan, bufs, 7 * len(grads))
        pending.append(dict(phase=1, tag=tag, keys=list(grads), ss=ss, rs=rs, bufs=bufs))
        return tuple(tokens + [tok])

    loss, dh, small = _local_step(x[0], loss_target[0], P, weights, boundary, first_deps=token)

    order = [n for n in WEIGHTS if n not in BIG]
    block = _pack([small[n] for n in order] + [loss])
    ss, rs, bufs, tok = _split_start("small_start", _small_plan, [block, lax.empty((N_DEV,) + block.shape, F32)], N_DEV - 1)
    while pending:
        advance((tok,) + tuple(o[0] for o in outs.values() if o is not None), fresh=0)
    block, land = _split_wait("small_wait", _small_plan, ss, rs, bufs, tuple(outs[n][0] for n in BIG))
    packed = _sum_blocks((4 * lax.axis_index("x") + 2 * lax.axis_index("y") + lax.axis_index("c")).astype(jnp.int32).reshape(1), block, land)
    total = _unpack(packed, [small[n].shape for n in order] + [loss.shape])
    loss_sum = total[-1][0, 0]
    gsmall = dict(zip(order, total[:-1]))
    gsmall["meta_tokens"] = lax.dynamic_slice_in_dim(gsmall["meta_tokens"], chip * meta_tokens.shape[1], meta_tokens.shape[1], 1)
    gsmall["ev_conv_w"] = lax.dynamic_slice_in_dim(gsmall["ev_conv_w"][:, :CONV_WIDTH], chip * ev_conv_w.shape[2], ev_conv_w.shape[2], 2)

    g_out, d_out, m_out, v_out = {}, {}, {}, {}
    for n in WEIGHTS:
        if n in BIG:
            g_out[n], d_out[n], m_out[n], v_out[n] = outs[n]
            continue
        shape = w[n].shape
        g = gsmall[n].reshape(shape)
        cols = shape[-1] if len(shape) > 1 else 128
        two = lambda a: a.reshape(-1, cols)
        d_, m_, v_ = _adamw(two(w[n]), two(g), two(m[n]), two(v[n]), f"adamw_{n}")
        g_out[n], d_out[n], m_out[n], v_out[n] = g, d_.reshape(shape), m_.reshape(shape), v_.reshape(shape)

    grad_x = dh[LEAD:][None]
    return (loss_sum, grad_x, *[g_out[n] for n in WEIGHTS], *[d_out[n] for n in WEIGHTS],
            *[m_out[n] for n in WEIGHTS], *[v_out[n] for n in WEIGHTS])
```

```python
import functools

import jax
import jax.numpy as jnp
from jax import lax
from jax.experimental import pallas as pl
from jax.experimental.pallas import tpu as pltpu

F32 = jnp.float32
BF16 = jnp.bfloat16
SDS = jax.ShapeDtypeStruct
MESH = pl.DeviceIdType.MESH
ANY_SPEC = pl.BlockSpec(memory_space=pl.ANY)

N_META = 16
CHUNK = 64
LEAD = CHUNK
PAD = LEAD - N_META
CONV_WIDTH = 31
CONV_ROWS = 32
POOL_WINDOWS = (2, 4, 8, 16)
HEAD_DIM = 128
SUB = 16
EXP_CAP = 80.0
EPS = 1e-6
ADAM_LR = 0.001
ADAM_B1 = 0.9
ADAM_B2 = 0.999
ADAM_EPS = 1e-08
ADAM_WD = 0.01
ADAM_STEP = 10
N_CHIPS = 4
VMEM_LIMIT = 52 << 20
MM_VMEM_BUDGET = 44 << 20


def _params(*sem):
    return pltpu.CompilerParams(dimension_semantics=sem if sem else None, vmem_limit_bytes=VMEM_LIMIT)


def _tile(n, target, unit=CHUNK):
    best = None
    for t in range(unit, min(n, target) + 1, unit):
        if n % t == 0:
            best = t
    assert best is not None, (n, target, unit)
    return best


def _ctile(n, target=512):
    for t in (512, 384, 256, 128):
        if t <= target and n % t == 0:
            return t
    raise ValueError(n)


def _mm_tiles(M, N, per_row, per_col, per_elem):
    best = None
    for tn in (512, 384, 256, 128):
        if N % tn:
            continue
        for tm in sorted((d for d in range(16, M + 1, 16) if M % d == 0), reverse=True):
            if 2 * (tm * per_row + tn * per_col + tm * tn * per_elem) <= MM_VMEM_BUDGET:
                if best is None or tm * tn > best[0] * best[1]:
                    best = (tm, tn)
                break
    assert best is not None, (M, N)
    return best


def _sigmoid(x):
    return 1.0 / (1.0 + jnp.exp(-x))


def _row_ids(shape, base):
    return lax.broadcasted_iota(jnp.int32, shape, 0) + base


def _cast_place(w3, layer, kind, chip1, dtype, name):
    _, ks, ns = w3.shape
    tr = _tile(ks, 512, 16)
    full = (ks, ns * N_CHIPS) if kind == "col" else (ks * N_CHIPS, ns)

    def body(chip_ref, w_ref, o_ref):
        del chip_ref
        o_ref[...] = w_ref[...].astype(dtype)

    omap = (lambda i, chip: (i, chip[0])) if kind == "col" else (lambda i, chip: (chip[0] * (ks // tr) + i, 0))
    return pl.pallas_call(
        body,
        grid_spec=pltpu.PrefetchScalarGridSpec(
            num_scalar_prefetch=1, grid=(ks // tr,),
            in_specs=[pl.BlockSpec((None, tr, ns), lambda i, chip: (layer, i, 0))],
            out_specs=pl.BlockSpec((tr, ns), omap)),
        out_shape=SDS(full, dtype), name=name, compiler_params=_params("parallel"))(chip1, w3)


def _rms_fwd(h, g3, layer, name, deps=()):
    T, D = h.shape
    tm = _tile(T, 832)

    def body(h_ref, g_ref, *rest):
        n_ref = rest[-1]
        x = h_ref[...]
        r = lax.rsqrt(jnp.mean(x * x, axis=-1, keepdims=True) + EPS)
        n_ref[...] = ((x * r) * g_ref[...]).astype(BF16)

    return pl.pallas_call(
        body, grid=(T // tm,),
        in_specs=[pl.BlockSpec((tm, D), lambda i: (i, 0)), pl.BlockSpec((None, 1, D), lambda i: (layer, 0, 0))]
        + [ANY_SPEC] * len(deps),
        out_specs=pl.BlockSpec((tm, D), lambda i: (i, 0)), out_shape=SDS((T, D), BF16),
        name=name, compiler_params=_params("parallel"))(h, g3, *deps)


def _final(h, g2, target):
    T, D = h.shape
    tm = _tile(T, 320)
    nsub = tm // CHUNK
    nblk = target.shape[0] // CHUNK

    def body(h_ref, g_ref, *rest):
        t_refs = rest[:nsub]
        dh_ref, dhb_ref, dg_ref, loss_ref = rest[nsub:]
        i = pl.program_id(0)

        @pl.when(i == 0)
        def _():
            dg_ref[...] = jnp.zeros_like(dg_ref)
            loss_ref[...] = jnp.zeros_like(loss_ref)

        g = g_ref[...]
        for q in range(nsub):
            rows = slice(q * CHUNK, (q + 1) * CHUNK)
            x = h_ref[rows, :]
            r = lax.rsqrt(jnp.mean(x * x, axis=-1, keepdims=True) + EPS)
            xh = x * r
            live = jnp.where(i * nsub + q > 0, 1.0, 0.0).astype(F32)
            e = ((xh * g) - t_refs[q][...]) * live
            dy = e * (1.0 / D)
            dxh = dy * g
            dh = r * (dxh - xh * jnp.mean(dxh * xh, axis=-1, keepdims=True))
            dh_ref[rows, :] = dh
            dhb_ref[rows, :] = dh.astype(BF16)
            dg_ref[...] += jnp.sum(dy * xh, axis=0, keepdims=True)
            loss_ref[...] += jnp.sum(e * e) * (0.5 / D)

    row = pl.BlockSpec((tm, D), lambda i: (i, 0))
    t_specs = [pl.BlockSpec((CHUNK, D), functools.partial(lambda i, q: (jnp.clip(i * nsub + q - 1, 0, nblk - 1), 0), q=q))
               for q in range(nsub)]
    return pl.pallas_call(
        body, grid=(T // tm,),
        in_specs=[row, pl.BlockSpec((1, D), lambda i: (0, 0))] + t_specs,
        out_specs=[row, row, pl.BlockSpec((1, D), lambda i: (0, 0)), pl.BlockSpec((1, 128), lambda i: (0, 0))],
        out_shape=[SDS((T, D), F32), SDS((T, D), BF16), SDS((1, D), F32), SDS((1, 128), F32)],
        name="final_loss", compiler_params=_params("arbitrary"))(h, g2, *([target] * nsub))


def _mm_nn(a, w3, layer, name, res=None, relu=False, square=False, deps=()):
    M, K = a.shape
    N = w3.shape[2]
    tm, tn = _mm_tiles(M, N, 2 * K, 2 * K, (2 if relu else 4) + (4 if res is not None else 0))

    def body(*refs):
        lhs = refs[0][...]
        acc = jnp.dot(lhs * lhs if square else lhs, refs[1][...], preferred_element_type=F32)
        if res is not None:
            acc = acc + refs[2][...]
        refs[-1][...] = jnp.maximum(acc, 0.0).astype(BF16) if relu else acc

    in_specs = [pl.BlockSpec((tm, K), lambda i, j: (i, 0)), pl.BlockSpec((None, K, tn), lambda i, j: (layer, 0, j))]
    args = [a, w3]
    tile = pl.BlockSpec((tm, tn), lambda i, j: (i, j))
    if res is not None:
        in_specs.append(tile)
        args.append(res)
    in_specs += [ANY_SPEC] * len(deps)
    args += list(deps)
    return pl.pallas_call(
        body, grid=(M // tm, N // tn), in_specs=in_specs, out_specs=tile,
        out_shape=SDS((M, N), BF16 if relu else F32),
        name=name, compiler_params=_params("parallel", "parallel"))(*args)


def _mm_nt(dy, w3, layer, name, relu=None, deps=()):
    M, N = dy.shape
    K = w3.shape[1]
    tm, tk = _mm_tiles(M, K, 2 * N, 2 * N, 4)

    def body(*refs):
        acc = lax.dot_general(refs[0][...], refs[1][...], (((1,), (1,)), ((), ())), preferred_element_type=F32)
        if relu is not None:
            acc = (acc * (2.0 * refs[2][...].astype(F32))).astype(BF16)
        refs[-1][...] = acc

    tile = pl.BlockSpec((tm, tk), lambda i, j: (i, j))
    in_specs = [pl.BlockSpec((tm, N), lambda i, j: (i, 0)), pl.BlockSpec((None, tk, N), lambda i, j: (layer, j, 0))]
    args = [dy, w3]
    if relu is not None:
        in_specs.append(tile)
        args.append(relu)
    in_specs += [ANY_SPEC] * len(deps)
    args += list(deps)
    return pl.pallas_call(
        body, grid=(M // tm, K // tk), in_specs=in_specs, out_specs=tile,
        out_shape=SDS((M, K), F32 if relu is None else BF16),
        name=name, compiler_params=_params("parallel", "parallel"))(*args)


def _row_tile(M, per_row, fixed):
    for tm in sorted((d for d in range(16, M + 1, 16) if M % d == 0), reverse=True):
        if 2 * (tm * per_row + fixed) <= MM_VMEM_BUDGET:
            return tm
    raise ValueError((M, per_row, fixed))


def _mm_nn_norm(a, w3, layer, res, g3, glayer, name, square=False, deps=()):
    M, K = a.shape
    D = w3.shape[2]
    tm = _row_tile(M, 2 * K + 10 * D, 2 * K * D)

    def body(a_ref, w_ref, r_ref, g_ref, *rest):
        h_ref, n_ref = rest[-2:]
        lhs = a_ref[...]
        x = r_ref[...] + jnp.dot(lhs * lhs if square else lhs, w_ref[...], preferred_element_type=F32)
        h_ref[...] = x
        r = lax.rsqrt(jnp.mean(x * x, axis=-1, keepdims=True) + EPS)
        n_ref[...] = ((x * r) * g_ref[...]).astype(BF16)

    row = pl.BlockSpec((tm, D), lambda i: (i, 0))
    return pl.pallas_call(
        body, grid=(M // tm,),
        in_specs=[pl.BlockSpec((tm, K), lambda i: (i, 0)), pl.BlockSpec((None, K, D), lambda i: (layer, 0, 0)), row,
                  pl.BlockSpec((None, 1, D), lambda i: (glayer, 0, 0))] + [ANY_SPEC] * len(deps),
        out_specs=[row, row], out_shape=[SDS((M, D), F32), SDS((M, D), BF16)],
        name=name, compiler_params=_params("parallel"))(a, w3, res, g3, *deps)


def _mlp_fwd(n2, w1, w2, res, g3, glayer, name, deps=()):
    M, D = n2.shape
    F = w1.shape[2]
    hb = _ctile(F)
    tm = _row_tile(M, 2 * D + 8 * D + (2 * D if g3 is not None else 0) + 2 * F, 2 * D * F)

    def body(n_ref, w1_ref, w2_ref, res_ref, *rest):
        outs = rest[-3:] if g3 is not None else rest[-2:]
        x = n_ref[...]
        acc = res_ref[...]
        for jb in range(F // hb):
            cols = slice(jb * hb, (jb + 1) * hb)
            r = jnp.maximum(jnp.dot(x, w1_ref[:, cols], preferred_element_type=F32), 0.0).astype(BF16)
            outs[-1][:, cols] = r
            acc = acc + jnp.dot(r * r, w2_ref[cols, :], preferred_element_type=F32)
        outs[0][...] = acc
        if g3 is not None:
            rr = lax.rsqrt(jnp.mean(acc * acc, axis=-1, keepdims=True) + EPS)
            outs[1][...] = ((acc * rr) * rest[0][...]).astype(BF16)

    row = pl.BlockSpec((tm, D), lambda i: (i, 0))
    once = dict(pipeline_mode=pl.Buffered(1))
    in_specs = [row, pl.BlockSpec((None, D, F), lambda i: (0, 0, 0), **once), pl.BlockSpec((None, F, D), lambda i: (0, 0, 0), **once), row]
    args = [n2, w1, w2, res]
    out_specs, out_shape = [row], [SDS((M, D), F32)]
    if g3 is not None:
        in_specs.append(pl.BlockSpec((None, 1, D), lambda i: (glayer, 0, 0)))
        args.append(g3)
        out_specs.append(row)
        out_shape.append(SDS((M, D), BF16))
    out_specs.append(pl.BlockSpec((tm, F), lambda i: (i, 0)))
    out_shape.append(SDS((M, F), BF16))
    in_specs += [ANY_SPEC] * len(deps)
    args += list(deps)
    return pl.pallas_call(
        body, grid=(M // tm,), in_specs=in_specs, out_specs=out_specs, out_shape=out_shape,
        name=name, compiler_params=_params("parallel"))(*args)


def _mm_nt_norm(dy, w3, layer, h, g3, glayer, dh_in, name, deps=()):
    M, N = dy.shape
    D = w3.shape[1]
    tm = _row_tile(M, 2 * N + 14 * D, 2 * N * D)

    def body(dy_ref, w_ref, h_ref, g_ref, dhi_ref, *rest):
        dh_ref, dhb_ref, dg_ref = rest[-3:]
        dn = lax.dot_general(dy_ref[...], w_ref[...], (((1,), (1,)), ((), ())), preferred_element_type=F32)
        x = h_ref[...]
        r = lax.rsqrt(jnp.mean(x * x, axis=-1, keepdims=True) + EPS)
        xh = x * r
        dxh = dn * g_ref[...]
        dh = dhi_ref[...] + r * (dxh - xh * jnp.mean(dxh * xh, axis=-1, keepdims=True))
        dh_ref[...] = dh
        dhb_ref[...] = dh.astype(BF16)

        @pl.when(pl.program_id(0) == 0)
        def _():
            dg_ref[...] = jnp.zeros_like(dg_ref)

        dg_ref[...] += jnp.sum(dn * xh, axis=0, keepdims=True)

    row = pl.BlockSpec((tm, D), lambda i: (i, 0))
    return pl.pallas_call(
        body, grid=(M // tm,),
        in_specs=[pl.BlockSpec((tm, N), lambda i: (i, 0)), pl.BlockSpec((None, D, N), lambda i: (layer, 0, 0)), row,
                  pl.BlockSpec((None, 1, D), lambda i: (glayer, 0, 0)), row] + [ANY_SPEC] * len(deps),
        out_specs=[row, row, pl.BlockSpec((1, D), lambda i: (0, 0))],
        out_shape=[SDS((M, D), F32), SDS((M, D), BF16), SDS((1, D), F32)],
        name=name, compiler_params=_params("arbitrary"))(dy, w3, h, g3, dh_in, *deps)


def _fam_dims(kind, K, N):
    return (K // 2, N // N_CHIPS) if kind == "col" else (K // (2 * N_CHIPS), N)


def _mm_tn(x, dy, kind, name, square=False):
    M, K = x.shape
    N = dy.shape[1]
    nr, nc = _fam_dims(kind, K, N)

    def body(x_ref, dy_ref, o_ref):
        lhs = x_ref[...]
        res = lax.dot_general(lhs * lhs if square else lhs, dy_ref[...], (((0,), (0,)), ((), ())), preferred_element_type=F32)
        o_ref[...] = res.astype(BF16).reshape(o_ref.shape)

    if kind == "col":
        tn = _ctile(nc)
        ct = nc // tn
        grid = (N // tn,)
        in_specs = [pl.BlockSpec((M, K), lambda j: (0, 0)), pl.BlockSpec((M, tn), lambda j: (0, j))]
        out_spec = pl.BlockSpec((2, None, nr, tn), lambda j: (0, j // ct, 0, j % ct))
    else:
        grid = (N_CHIPS,)
        in_specs = [pl.BlockSpec((M, 2 * nr), lambda i: (0, i)), pl.BlockSpec((M, N), lambda i: (0, 0))]
        out_spec = pl.BlockSpec((2, None, nr, N), lambda i: (0, i, 0, 0))
    return pl.pallas_call(
        body, grid=grid, in_specs=in_specs, out_specs=out_spec, out_shape=SDS((2, N_CHIPS, nr, nc), BF16),
        name=name, compiler_params=_params("parallel"))(x, dy)


C_EVEN = 512


def _live(rows, base, total):
    r = _row_ids((rows, 1), base)
    return jnp.logical_and(r >= PAD, r < total).astype(F32)


def _conv_taps(win, w_ref, ls, acc, flip):
    for b in range(8):
        rb = win if b == 0 else pltpu.roll(win, 96 - b, 0)
        for a in range(5):
            o = 8 * a + b
            tap = (30 - o) if flip else (o - 2)
            if 0 <= tap < CONV_WIDTH:
                acc = acc + w_ref[pl.ds(tap, 1), ls] * rb[8 * a:8 * a + CHUNK]
    return acc


def _window_sum(win, levels, forward):
    s = win
    n = win.shape[0]
    for k in range(levels):
        step = 1 << k
        s = s + pltpu.roll(s, (n - step) if forward else step, 0)
    return s


def _pool_count(base, g):
    pos = _row_ids((CHUNK, 1), base) - PAD
    return jnp.clip(pos + 1, 1, POOL_WINDOWS[g]).astype(F32)


def _even_fwd(u, cw3, cb3, lg3, lb3, pw4, pb3, ps3, j, name):
    T = u.shape[0]
    C = C_EVEN
    tm = _tile(T, 320)
    nch = tm // CHUNK
    nblk = T // CHUNK

    def body(u_ref, up_ref, cw_ref, cb_ref, lg_ref, lb_ref, pw_ref, pb_ref, ps_ref, o_ref, yc_ref, a_s, p_s, yc_s):
        row0 = pl.program_id(0) * tm
        up = up_ref[...]
        lp = _live(CHUNK, row0 - CHUNK, T)
        a_s[0:CHUNK, :] = up[:, 0:C] * _sigmoid(up[:, C:2 * C]) * lp
        p_s[0:CHUNK, :] = up[:, 2 * C:3 * C] * lp

        def stage(c, _):
            rs = pl.multiple_of(c * CHUNK, CHUNK)
            lv = _live(CHUNK, row0 + rs, T)
            a_s[pl.ds(rs + CHUNK, CHUNK), :] = u_ref[pl.ds(rs, CHUNK), 0:C] * _sigmoid(u_ref[pl.ds(rs, CHUNK), C:2 * C]) * lv
            p_s[pl.ds(rs + CHUNK, CHUNK), :] = u_ref[pl.ds(rs, CHUNK), 2 * C:3 * C] * lv
            return 0

        lax.fori_loop(0, nch, stage, 0)

        def chunk(c, _):
            rs = pl.multiple_of(c * CHUNK, CHUNK)
            lv = _live(CHUNK, row0 + rs, T)
            for cb in range(4):
                ls = slice(cb * 128, (cb + 1) * 128)
                win = a_s[pl.ds(pl.multiple_of(rs + 32, 32), 96), ls]
                acc = jnp.broadcast_to(cb_ref[:, ls], (CHUNK, 128))
                yc_s[:, ls] = _conv_taps(win, cw_ref, ls, acc, False)
            y = yc_s[...]
            yc_ref[pl.ds(rs, CHUNK), :] = y
            xc = y - jnp.mean(y, axis=-1, keepdims=True)
            yn = xc * lax.rsqrt(jnp.mean(xc * xc, axis=-1, keepdims=True) + EPS) * lg_ref[...] + lb_ref[...]
            o_ref[pl.ds(rs, CHUNK), 0:C] = (yn * _sigmoid(yn) * lv).astype(BF16)
            for g in range(4):
                ls = slice(g * 128, (g + 1) * 128)
                win = p_s[pl.ds(pl.multiple_of(rs + 48, 16), 80), ls]
                s = _window_sum(win, g + 1, False)
                d = s[16:80] / _pool_count(row0 + rs, g) - win[16:80]
                yv = jnp.dot(d.astype(BF16), pw_ref[g].astype(BF16), preferred_element_type=F32) + pb_ref[:, ls]
                o_ref[pl.ds(rs, CHUNK), C + g * 128:C + (g + 1) * 128] = (yv * ps_ref[:, ls] * lv).astype(BF16)
            return 0

        lax.fori_loop(0, nch, chunk, 0)

    vec = pl.BlockSpec((None, 1, C), lambda i: (j, 0, 0))
    return pl.pallas_call(
        body, grid=(T // tm,),
        in_specs=[pl.BlockSpec((tm, 3 * C), lambda i: (i, 0)),
                  pl.BlockSpec((CHUNK, 3 * C), lambda i: (jnp.maximum(i * nch - 1, 0), 0)),
                  pl.BlockSpec((None, CONV_ROWS, C), lambda i: (j, 0, 0)), vec, vec, vec,
                  pl.BlockSpec((None, 4, 128, 128), lambda i: (j, 0, 0, 0)), vec, vec],
        out_specs=[pl.BlockSpec((tm, 2 * C), lambda i: (i, 0)), pl.BlockSpec((tm, C), lambda i: (i, 0))],
        out_shape=[SDS((T, 2 * C), BF16), SDS((T, C), F32)],
        scratch_shapes=[pltpu.VMEM((tm + CHUNK, C), F32), pltpu.VMEM((tm + CHUNK, C), F32), pltpu.VMEM((CHUNK, C), F32)],
        name=name, compiler_params=_params("parallel"))(u, u, cw3, cb3, lg3, lb3, pw4, pb3, ps3)


def _even_bwd(u, yc, dy, cw3, cb3, lg3, lb3, pw4, pb3, ps3, j, name):
    T = u.shape[0]
    C = C_EVEN
    tm = _tile(T, 320)
    nch = tm // CHUNK
    nblk = T // CHUNK
    ntile = T // tm

    def body(u_ref, up_ref, un_ref, yc_ref, ycn_ref, dy_ref, dyn_ref, cw_ref, cb_ref, lg_ref, lb_ref, pw_ref, pb_ref, ps_ref,
             du_ref, dcw_ref, dcb_ref, dlg_ref, dlb_ref, dpw_ref, dpb_ref, dps_ref,
             a_s, p_s, dy_s, dyc_s, dd_s, ddc_s, dw_s):
        i = pl.program_id(0)
        row0 = i * tm

        @pl.when(i == 0)
        def _():
            for ref in (dcb_ref, dlg_ref, dlb_ref, dpw_ref, dpb_ref, dps_ref, dw_s):
                ref[...] = jnp.zeros_like(ref)

        up = up_ref[...]
        lp = _live(CHUNK, row0 - CHUNK, T)
        a_s[0:CHUNK, :] = up[:, 0:C] * _sigmoid(up[:, C:2 * C]) * lp
        p_s[0:CHUNK, :] = up[:, 2 * C:3 * C] * lp
        ln_ = _live(CHUNK, row0 + tm, T)
        p_s[tm + CHUNK:tm + 2 * CHUNK, :] = un_ref[:, 2 * C:3 * C] * ln_
        dy_s[tm:tm + CHUNK, :] = dyn_ref[...] * ln_
        dyc_s[tm + CHUNK:tm + CHUNK + 32, :] = jnp.zeros((32, C), F32)

        def stage(c, _):
            rs = pl.multiple_of(c * CHUNK, CHUNK)
            lv = _live(CHUNK, row0 + rs, T)
            a_s[pl.ds(rs + CHUNK, CHUNK), :] = u_ref[pl.ds(rs, CHUNK), 0:C] * _sigmoid(u_ref[pl.ds(rs, CHUNK), C:2 * C]) * lv
            p_s[pl.ds(rs + CHUNK, CHUNK), :] = u_ref[pl.ds(rs, CHUNK), 2 * C:3 * C] * lv
            dy_s[pl.ds(rs, CHUNK), :] = dy_ref[pl.ds(rs, CHUNK), :] * lv
            return 0

        lax.fori_loop(0, nch, stage, 0)

        def first(rs, y, own):
            xc = y - jnp.mean(y, axis=-1, keepdims=True)
            rstd = lax.rsqrt(jnp.mean(xc * xc, axis=-1, keepdims=True) + EPS)
            xh = xc * rstd
            yn = xh * lg_ref[...] + lb_ref[...]
            sg = _sigmoid(yn)
            dyn = dy_s[pl.ds(rs, CHUNK), 0:C] * (sg * (1.0 + yn * (1.0 - sg)))
            dlg_ref[...] += jnp.sum(dyn * xh, axis=0, keepdims=True) * own
            dlb_ref[...] += jnp.sum(dyn, axis=0, keepdims=True) * own
            dxh = dyn * lg_ref[...]
            dyc = rstd * (dxh - jnp.mean(dxh, axis=-1, keepdims=True) - xh * jnp.mean(dxh * xh, axis=-1, keepdims=True))
            dyc_s[pl.ds(rs, CHUNK), :] = dyc
            dcb_ref[...] += jnp.sum(dyc, axis=0, keepdims=True) * own
            for g in range(4):
                ls = slice(g * 128, (g + 1) * 128)
                win = p_s[pl.ds(rs + 48, 80), ls]
                s = _window_sum(win, g + 1, False)
                cnt = _pool_count(row0 + rs, g)
                d = (s[16:80] / cnt - win[16:80]).astype(BF16)
                w = pw_ref[g].astype(BF16)
                pre = jnp.dot(d, w, preferred_element_type=F32) + pb_ref[:, ls]
                dyb = dy_s[pl.ds(rs, CHUNK), C + g * 128:C + (g + 1) * 128]
                dpre = dyb * ps_ref[:, ls]
                dps_ref[:, ls] += jnp.sum(dyb * pre, axis=0, keepdims=True) * own
                dpb_ref[:, ls] += jnp.sum(dpre, axis=0, keepdims=True) * own
                dpre_b = (dpre * own).astype(BF16)
                dpw_ref[g] += lax.dot_general(d, dpre_b, (((0,), (0,)), ((), ())), preferred_element_type=F32)
                dd = lax.dot_general(dpre.astype(BF16), w, (((1,), (1,)), ((), ())), preferred_element_type=F32)
                dd_s[pl.ds(rs, CHUNK), ls] = dd
                ddc_s[pl.ds(rs, CHUNK), ls] = dd / cnt

        def first_in_tile(c, _):
            rs = pl.multiple_of(c * CHUNK, 16 * CHUNK // 16)
            first(rs, yc_ref[pl.ds(rs, CHUNK), :], 1.0)
            return 0

        lax.fori_loop(0, nch, first_in_tile, 0)
        first(tm, ycn_ref[...], 0.0)
        ddc_s[tm + CHUNK:tm + CHUNK + 16, :] = jnp.zeros((16, C), F32)

        def second(c, _):
            rs = pl.multiple_of(c * CHUNK, CHUNK)
            lv = _live(CHUNK, row0 + rs, T)
            for cb in range(4):
                ls = slice(cb * 128, (cb + 1) * 128)
                wd = dyc_s[pl.ds(rs, 96), ls]
                da = _conv_taps(wd, cw_ref, ls, jnp.zeros((CHUNK, 128), F32), True)
                wa = a_s[pl.ds(pl.multiple_of(rs + 32, 32), 96), ls]
                dyc = dyc_s[pl.ds(rs, CHUNK), ls]
                for b in range(8):
                    rb = wa if b == 0 else pltpu.roll(wa, 96 - b, 0)
                    for a in range(5):
                        tap = 8 * a + b - 2
                        if 0 <= tap < CONV_WIDTH:
                            prod = dyc * rb[8 * a:8 * a + CHUNK]
                            part = prod[0:8]
                            for q in range(1, 8):
                                part = part + prod[8 * q:8 * q + 8]
                            dw_s[8 * tap:8 * tap + 8, ls] += part
                val = u_ref[pl.ds(rs, CHUNK), ls]
                sg = _sigmoid(u_ref[pl.ds(rs, CHUNK), C + cb * 128:C + (cb + 1) * 128])
                du_ref[pl.ds(rs, CHUNK), ls] = (da * sg * lv).astype(BF16)
                du_ref[pl.ds(rs, CHUNK), C + cb * 128:C + (cb + 1) * 128] = (da * val * sg * (1.0 - sg) * lv).astype(BF16)
            for g in range(4):
                ls = slice(g * 128, (g + 1) * 128)
                z = _window_sum(ddc_s[pl.ds(rs, 80), ls], g + 1, True)
                dpin = (z[0:CHUNK] - dd_s[pl.ds(rs, CHUNK), ls]) * lv
                du_ref[pl.ds(rs, CHUNK), 2 * C + g * 128:2 * C + (g + 1) * 128] = dpin.astype(BF16)
            return 0

        lax.fori_loop(0, nch, second, 0)

        @pl.when(i == ntile - 1)
        def _():
            for tap in range(CONV_WIDTH):
                dcw_ref[tap:tap + 1, :] = jnp.sum(dw_s[8 * tap:8 * tap + 8, :], axis=0, keepdims=True)
            dcw_ref[CONV_WIDTH:CONV_ROWS, :] = jnp.zeros((CONV_ROWS - CONV_WIDTH, C), F32)

    vec = pl.BlockSpec((None, 1, C), lambda i: (j, 0, 0))
    ovec = pl.BlockSpec((1, C), lambda i: (0, 0))
    return pl.pallas_call(
        body, grid=(ntile,),
        in_specs=[pl.BlockSpec((tm, 3 * C), lambda i: (i, 0)),
                  pl.BlockSpec((CHUNK, 3 * C), lambda i: (jnp.maximum(i * nch - 1, 0), 0)),
                  pl.BlockSpec((CHUNK, 3 * C), lambda i: (jnp.minimum((i + 1) * nch, nblk - 1), 0)),
                  pl.BlockSpec((tm, C), lambda i: (i, 0)),
                  pl.BlockSpec((CHUNK, C), lambda i: (jnp.minimum((i + 1) * nch, nblk - 1), 0)),
                  pl.BlockSpec((tm, 2 * C), lambda i: (i, 0)),
                  pl.BlockSpec((CHUNK, 2 * C), lambda i: (jnp.minimum((i + 1) * nch, nblk - 1), 0)),
                  pl.BlockSpec((None, CONV_ROWS, C), lambda i: (j, 0, 0)), vec, vec, vec,
                  pl.BlockSpec((None, 4, 128, 128), lambda i: (j, 0, 0, 0)), vec, vec],
        out_specs=[pl.BlockSpec((tm, 3 * C), lambda i: (i, 0)), pl.BlockSpec((CONV_ROWS, C), lambda i: (0, 0)),
                   ovec, ovec, ovec, pl.BlockSpec((4, 128, 128), lambda i: (0, 0, 0)), ovec, ovec],
        out_shape=[SDS((T, 3 * C), BF16), SDS((CONV_ROWS, C), F32), SDS((1, C), F32), SDS((1, C), F32), SDS((1, C), F32),
                   SDS((4, 128, 128), F32), SDS((1, C), F32), SDS((1, C), F32)],
        scratch_shapes=[pltpu.VMEM((tm + CHUNK, C), F32), pltpu.VMEM((tm + 2 * CHUNK, C), F32),
                        pltpu.VMEM((tm + CHUNK, 2 * C), F32),
                        pltpu.VMEM((tm + CHUNK + 32, C), F32), pltpu.VMEM((tm + CHUNK, C), F32),
                        pltpu.VMEM((tm + CHUNK + 16, C), F32), pltpu.VMEM((8 * CONV_ROWS, C), F32)],
        name=name, compiler_params=_params("arbitrary"))(u, u, u, yc, yc, dy, dy, cw3, cb3, lg3, lb3, pw4, pb3, ps3)


HI = lax.Precision.HIGHEST


def _dot_nt(a, b):
    return lax.dot_general(a, b, (((1,), (1,)), ((), ())), preferred_element_type=F32)


def _dot_tn(a, b):
    return lax.dot_general(a, b, (((0,), (0,)), ((), ())), preferred_element_type=F32)


def _tri(lower):
    r = lax.broadcasted_iota(jnp.int32, (CHUNK, CHUNK), 0)
    c = lax.broadcasted_iota(jnp.int32, (CHUNK, CHUNK), 1)
    return jnp.where((c <= r) if lower else (c >= r), 1.0, 0.0).astype(F32)


def _hgrn_gates(u_ref, lb_ref, h, D, lv):
    ls = slice(h * HEAD_DIM, (h + 1) * HEAD_DIM)
    qraw = u_ref[:, ls]
    fraw = u_ref[:, D + h * HEAD_DIM:D + (h + 1) * HEAD_DIM]
    v = u_ref[:, 2 * D + h * HEAD_DIM:2 * D + (h + 1) * HEAD_DIM] * lv
    lbv = lb_ref[:, ls]
    sig = _sigmoid(fraw)
    forget = lbv + (1.0 - lbv) * sig
    logf = jnp.log(forget) * lv
    k = (1.0 - forget) * lv
    qsig = _sigmoid(qraw)
    q = qraw * qsig * lv
    return q, k, v, logf, (qraw, qsig, sig, forget, lbv)


def _sub_parts(q, k, b, b_s, I):
    rows = slice(SUB * I, SUB * (I + 1))
    rho = jnp.zeros((1, HEAD_DIM), F32) if I == 0 else b_s[SUB * I - 1:SUB * I, :]
    eI = jnp.exp(b[rows] - rho)
    EI = jnp.exp(jnp.minimum(rho - b, EXP_CAP))
    causal = (lax.broadcasted_iota(jnp.int32, (SUB, CHUNK), 1)
              <= lax.broadcasted_iota(jnp.int32, (SUB, CHUNK), 0) + SUB * I)
    return rows, q[rows] * eI, k * EI, eI, EI, causal


def _chunks_per_step(NC):
    for n in (5, 4, 3, 2):
        if NC % n == 0:
            return n
    return 1


def _hgrn_fwd(u, lb3, layer, gn3, j, name):
    T = u.shape[0]
    D = u.shape[1] // 4
    H = D // HEAD_DIM
    NC = T // CHUNK
    CH = _chunks_per_step(NC)
    R = CH * CHUNK

    def body(u_ref, lb_ref, gn_ref, y_ref, o_ref, sall_ref, st_s, b_s, lf_s, q_s, k_s):
        n = pl.program_id(0)

        @pl.when(n == 0)
        def _():
            st_s[...] = jnp.zeros_like(st_s)

        heads = range(H)
        cols = [slice(h * HEAD_DIM, (h + 1) * HEAD_DIM) for h in heads]
        rows = [slice(c * CHUNK, (c + 1) * CHUNK) for c in range(CH)]
        vb = {}
        for c in range(CH):
            lv = _live(CHUNK, (n * CH + c) * CHUNK, T)
            for h in heads:
                q, k, v, logf, _ = _hgrn_gates(u_ref.at[rows[c]], lb_ref, h, D, lv)
                q_s[rows[c], cols[h]] = q
                k_s[rows[c], cols[h]] = k
                lf_s[rows[c], cols[h]] = logf
                vb[c, h] = v.astype(BF16)
        for c in range(CH):
            b_s[rows[c], :] = jnp.dot(_tri(True), lf_s[rows[c], :], precision=HI, preferred_element_type=F32)
        ops = {}
        for c in range(CH):
            for h in heads:
                b_h = b_s.at[rows[c], cols[h]]
                b = b_h[...]
                q = q_s[rows[c], cols[h]]
                k = k_s[rows[c], cols[h]]
                blast = b_h[CHUNK - 1:CHUNK, :]
                qh = (q * jnp.exp(b)).astype(BF16)
                kt = (k * jnp.exp(blast - b)).astype(BF16)
                subs = []
                for I in range(CHUNK // SUB):
                    _, qI, KI, _, _, causal = _sub_parts(q, k, b, b_h, I)
                    subs.append((qI.astype(BF16), KI.astype(BF16), causal))
                ops[c, h] = (qh, kt, jnp.exp(blast), subs)
        mm = {}
        for h in heads:
            st = st_s[h]
            for c in range(CH):
                qh, kt, eblast, subs = ops[c, h]
                sall_ref[c, h] = st
                o_inter = _dot_nt(qh, st.astype(BF16))
                st = st * eblast + _dot_tn(vb[c, h], kt)
                mm[c, h] = (o_inter, [_dot_nt(qI, KI) for qI, KI, _ in subs])
            st_s[h] = st
        for c in range(CH):
            for h in heads:
                o_inter, ps = mm[c, h]
                p = jnp.concatenate([jnp.where(m, x, 0.0) for x, (_, _, m) in zip(ps, ops[c, h][3])], axis=0).astype(BF16)
                o = o_inter + jnp.dot(p, vb[c, h], preferred_element_type=F32)
                o_ref[rows[c], cols[h]] = o
                graw = u_ref[rows[c], 3 * D + h * HEAD_DIM:3 * D + (h + 1) * HEAD_DIM]
                r = lax.rsqrt(jnp.mean(o * o, axis=-1, keepdims=True) + EPS)
                y_ref[rows[c], cols[h]] = (((o * r) * gn_ref[...]) * (graw * _sigmoid(graw))).astype(BF16)

    return pl.pallas_call(
        body, grid=(NC // CH,),
        in_specs=[pl.BlockSpec((R, 4 * D), lambda n: (n, 0)),
                  pl.BlockSpec((None, 1, D), lambda n: (layer, 0, 0)),
                  pl.BlockSpec((None, 1, HEAD_DIM), lambda n: (j, 0, 0))],
        out_specs=[pl.BlockSpec((R, D), lambda n: (n, 0)), pl.BlockSpec((R, D), lambda n: (n, 0)),
                   pl.BlockSpec((CH, H, HEAD_DIM, HEAD_DIM), lambda n: (n, 0, 0, 0))],
        out_shape=[SDS((T, D), BF16), SDS((T, D), F32), SDS((NC, H, HEAD_DIM, HEAD_DIM), F32)],
        scratch_shapes=[pltpu.VMEM((H, HEAD_DIM, HEAD_DIM), F32)] + [pltpu.VMEM((R, D), F32)] * 4,
        name=name, compiler_params=_params("arbitrary"))(u, lb3, gn3)


def _hgrn_bwd(u, o_raw, dy, sall, lb3, layer, gn3, j, name):
    T = u.shape[0]
    D = u.shape[1] // 4
    H = D // HEAD_DIM
    NC = T // CHUNK
    CH = _chunks_per_step(NC)
    R = CH * CHUNK
    NS = NC // CH

    def body(u_ref, o_ref, dy_ref, sall_ref, lb_ref, gn_ref, du_ref, dlb_ref, dgn_ref, dst_s, b_s, lf_s, q_s, k_s, db_s, dk_s):
        step = pl.program_id(0)
        n = NS - 1 - step

        @pl.when(step == 0)
        def _():
            dst_s[...] = jnp.zeros_like(dst_s)
            dlb_ref[...] = jnp.zeros_like(dlb_ref)
            dgn_ref[...] = jnp.zeros_like(dgn_ref)

        last_row = (_row_ids((CHUNK, 1), 0) == CHUNK - 1).astype(F32)
        gn = gn_ref[...]
        heads = range(H)
        chunks = range(CH)
        cols = [slice(h * HEAD_DIM, (h + 1) * HEAD_DIM) for h in heads]
        rows = [slice(c * CHUNK, (c + 1) * CHUNK) for c in chunks]
        lv = [_live(CHUNK, (n * CH + c) * CHUNK, T) for c in chunks]
        vb, dob = {}, {}
        dgn = jnp.zeros((1, HEAD_DIM), F32)
        for c in chunks:
            for h in heads:
                q, k, v, logf, _ = _hgrn_gates(u_ref.at[rows[c]], lb_ref, h, D, lv[c])
                q_s[rows[c], cols[h]] = q
                k_s[rows[c], cols[h]] = k
                lf_s[rows[c], cols[h]] = logf
                vb[c, h] = v.astype(BF16)
                graw = u_ref[rows[c], 3 * D + h * HEAD_DIM:3 * D + (h + 1) * HEAD_DIM]
                gsig = _sigmoid(graw)
                o = o_ref[rows[c], cols[h]]
                r = lax.rsqrt(jnp.mean(o * o, axis=-1, keepdims=True) + EPS)
                xh = o * r
                dyv = dy_ref[rows[c], cols[h]]
                dsg = dyv * (graw * gsig)
                dgn = dgn + jnp.sum(dsg * xh, axis=0, keepdims=True)
                dxh = dsg * gn
                do = r * (dxh - xh * jnp.mean(dxh * xh, axis=-1, keepdims=True))
                dob[c, h] = do.astype(BF16)
                dgraw = dyv * xh * gn * (gsig * (1.0 + graw * (1.0 - gsig)))
                du_ref[rows[c], 3 * D + h * HEAD_DIM:3 * D + (h + 1) * HEAD_DIM] = (dgraw * lv[c]).astype(BF16)
        dgn_ref[...] += dgn
        for c in chunks:
            b_s[rows[c], :] = jnp.dot(_tri(True), lf_s[rows[c], :], precision=HI, preferred_element_type=F32)
        ops = {}
        for c in chunks:
            for h in heads:
                b_h = b_s.at[rows[c], cols[h]]
                b = b_h[...]
                q = q_s[rows[c], cols[h]]
                k = k_s[rows[c], cols[h]]
                blast = b_h[CHUNK - 1:CHUNK, :]
                eb = jnp.exp(b)
                ekb = jnp.exp(blast - b)
                subs = []
                for I in range(CHUNK // SUB):
                    rws, qI, KI, eI, EI, causal = _sub_parts(q, k, b, b_h, I)
                    subs.append((rws, qI.astype(BF16), KI.astype(BF16), eI, EI, causal))
                ops[c, h] = (eb, ekb, jnp.exp(blast), (q * eb).astype(BF16), (k * ekb).astype(BF16), subs)
        mm = {}
        for h in heads:
            dst = dst_s[h]
            for c in reversed(chunks):
                eb, ekb, eblast, qhb, ktb, subs = ops[c, h]
                st = sall_ref[c, h]
                dstb = dst.astype(BF16)
                dv = _dot_nt(ktb, dstb)
                dqh = jnp.dot(dob[c, h], st.astype(BF16), preferred_element_type=F32)
                dkt = jnp.dot(vb[c, h], dstb, preferred_element_type=F32)
                dblast = jnp.sum(dst * st, axis=0, keepdims=True) * eblast
                dst = dst * eblast + _dot_tn(dob[c, h], qhb)
                dp_full = _dot_nt(dob[c, h], vb[c, h])
                ps = [_dot_nt(qIb, KIb) for _, qIb, KIb, _, _, _ in subs]
                mm[c, h] = (dv, dqh, dkt, dblast, dp_full, ps)
            dst_s[h] = dst
        for c in chunks:
            for h in heads:
                eb, ekb, eblast, qhb, ktb, subs = ops[c, h]
                dv, dqh, dkt, dblast, dp_full, ps = mm[c, h]
                p = jnp.concatenate([jnp.where(sub[5], x, 0.0) for x, sub in zip(ps, subs)], axis=0).astype(BF16)
                dv = dv + _dot_tn(p, dob[c, h])
                du_ref[rows[c], 2 * D + h * HEAD_DIM:2 * D + (h + 1) * HEAD_DIM] = (dv * lv[c]).astype(BF16)
                dq = dqh * eb
                db = dqh * qhb.astype(F32)
                tmp = dkt * ktb.astype(F32)
                dk = dkt * ekb
                db = db - tmp
                dblast = dblast + jnp.sum(tmp, axis=0, keepdims=True)
                dq_parts, db_parts = [], []
                for rws, qIb, KIb, eI, EI, causal in subs:
                    dp = jnp.where(causal, dp_full[rws], 0.0).astype(BF16)
                    dqI = jnp.dot(dp, KIb, preferred_element_type=F32)
                    dKI = _dot_tn(dp, qIb)
                    dq_parts.append(dqI * eI)
                    db_parts.append(dqI * qIb.astype(F32))
                    dk = dk + dKI * EI
                    db = db - dKI * KIb.astype(F32)
                dq = dq + jnp.concatenate(dq_parts, axis=0)
                db_s[rows[c], cols[h]] = db + jnp.concatenate(db_parts, axis=0) + last_row * dblast
                dk_s[rows[c], cols[h]] = dk
                qraw = u_ref[rows[c], cols[h]]
                qsig = _sigmoid(qraw)
                du_ref[rows[c], cols[h]] = (dq * (qsig * (1.0 + qraw * (1.0 - qsig))) * lv[c]).astype(BF16)
        for c in chunks:
            lf_s[rows[c], :] = jnp.dot(_tri(False), db_s[rows[c], :], precision=HI, preferred_element_type=F32)
        for h in heads:
            lbv = lb_ref[:, cols[h]]
            dlb = jnp.zeros((1, HEAD_DIM), F32)
            for c in chunks:
                fraw = u_ref[rows[c], D + h * HEAD_DIM:D + (h + 1) * HEAD_DIM]
                sig = _sigmoid(fraw)
                forget = lbv + (1.0 - lbv) * sig
                dforget = (lf_s[rows[c], cols[h]] / forget - dk_s[rows[c], cols[h]]) * lv[c]
                dlb = dlb + jnp.sum(dforget * (1.0 - sig), axis=0, keepdims=True)
                du_ref[rows[c], D + h * HEAD_DIM:D + (h + 1) * HEAD_DIM] = (dforget * (1.0 - lbv) * sig * (1.0 - sig)).astype(BF16)
            dlb_ref[:, cols[h]] += dlb

    rev = lambda s: (NS - 1 - s, 0)
    return pl.pallas_call(
        body, grid=(NS,),
        in_specs=[pl.BlockSpec((R, 4 * D), rev), pl.BlockSpec((R, D), rev), pl.BlockSpec((R, D), rev),
                  pl.BlockSpec((CH, H, HEAD_DIM, HEAD_DIM), lambda s: (NS - 1 - s, 0, 0, 0)),
                  pl.BlockSpec((None, 1, D), lambda s: (layer, 0, 0)),
                  pl.BlockSpec((None, 1, HEAD_DIM), lambda s: (j, 0, 0))],
        out_specs=[pl.BlockSpec((R, 4 * D), rev), pl.BlockSpec((1, D), lambda s: (0, 0)),
                   pl.BlockSpec((1, HEAD_DIM), lambda s: (0, 0))],
        out_shape=[SDS((T, 4 * D), BF16), SDS((1, D), F32), SDS((1, HEAD_DIM), F32)],
        scratch_shapes=[pltpu.VMEM((H, HEAD_DIM, HEAD_DIM), F32)] + [pltpu.VMEM((R, D), F32)] * 6,
        name=name, compiler_params=_params("arbitrary"))(u, o_raw, dy, sall, lb3, gn3)


def _softmax_layers(p_ref, n_layers):
    rows = [p_ref[l:l + 1, :] for l in range(n_layers)]
    m = functools.reduce(jnp.maximum, rows)
    e = [jnp.exp(x - m) for x in rows]
    tot = functools.reduce(lambda a, b: a + b, e)
    return [x / tot for x in e]


def _lb_fwd(p):
    n_layers, D = p.shape

    def body(p_ref, o_ref):
        s = _softmax_layers(p_ref, n_layers)
        acc = jnp.zeros((1, D), F32)
        o_ref[0:1, :] = acc
        for l in range(1, n_layers):
            acc = acc + s[l]
            o_ref[l:l + 1, :] = acc

    return pl.pallas_call(body, out_shape=SDS(p.shape, F32), name="lb_fwd")(p)


def _lb_bwd(p, dlb):
    n_layers, D = p.shape

    def body(p_ref, d_ref, o_ref):
        s = _softmax_layers(p_ref, n_layers)
        ds = [jnp.zeros((1, D), F32)] * n_layers
        acc = jnp.zeros((1, D), F32)
        for l in range(n_layers - 1, 0, -1):
            acc = acc + d_ref[l:l + 1, :]
            ds[l] = acc
        dot = functools.reduce(lambda a, b: a + b, [s[l] * ds[l] for l in range(n_layers)])
        for l in range(n_layers):
            o_ref[l:l + 1, :] = s[l] * (ds[l] - dot)

    return pl.pallas_call(body, out_shape=SDS(p.shape, F32), name="lb_bwd")(p, dlb)


def _adamw(w, g, m, v, name):
    R, C = w.shape
    tr = _tile(R, 256, 8) if R % 8 == 0 else R

    def body(w_ref, g_ref, m_ref, v_ref, d_ref, mo_ref, vo_ref):
        g_ = g_ref[...]
        m_ = ADAM_B1 * m_ref[...] + (1.0 - ADAM_B1) * g_
        v_ = ADAM_B2 * v_ref[...] + (1.0 - ADAM_B2) * (g_ * g_)
        mh = m_ / (1.0 - ADAM_B1 ** ADAM_STEP)
        vh = v_ / (1.0 - ADAM_B2 ** ADAM_STEP)
        d_ref[...] = -ADAM_LR * (mh / (jnp.sqrt(vh) + ADAM_EPS) + ADAM_WD * w_ref[...])
        mo_ref[...] = m_
        vo_ref[...] = v_

    blk = pl.BlockSpec((tr, C), lambda i: (i, 0))
    return pl.pallas_call(
        body, grid=(R // tr,), in_specs=[blk] * 4, out_specs=[blk] * 3, out_shape=[SDS((R, C), F32)] * 3,
        name=name, compiler_params=_params("parallel"))(w, g, m, v)


def _adamw_layer(w3, m3, v3, g2, layer, outs, name):
    L, R, C = w3.shape
    tr = _tile(R, 256, 8)
    if outs is None:
        outs = tuple(lax.empty(w3.shape, F32) for _ in range(4))

    def body(w_ref, m_ref, v_ref, g_ref, a0, a1, a2, a3, go_ref, d_ref, mo_ref, vo_ref):
        del a0, a1, a2, a3
        g_ = g_ref[...]
        m_ = ADAM_B1 * m_ref[...] + (1.0 - ADAM_B1) * g_
        v_ = ADAM_B2 * v_ref[...] + (1.0 - ADAM_B2) * (g_ * g_)
        mh = m_ / (1.0 - ADAM_B1 ** ADAM_STEP)
        vh = v_ / (1.0 - ADAM_B2 ** ADAM_STEP)
        go_ref[...] = g_
        d_ref[...] = -ADAM_LR * (mh / (jnp.sqrt(vh) + ADAM_EPS) + ADAM_WD * w_ref[...])
        mo_ref[...] = m_
        vo_ref[...] = v_

    lay = pl.BlockSpec((None, tr, C), lambda i: (layer, i, 0))
    return pl.pallas_call(
        body, grid=(R // tr,), in_specs=[lay] * 3 + [pl.BlockSpec((tr, C), lambda i: (i, 0))] + [ANY_SPEC] * 4,
        out_specs=[lay] * 4, out_shape=[SDS(w3.shape, F32)] * 4, input_output_aliases={4: 0, 5: 1, 6: 2, 7: 3},
        name=name, compiler_params=_params("parallel"))(w3, m3, v3, g2, *outs)


SEM_SPEC = pl.BlockSpec(memory_space=pltpu.SEMAPHORE)
HBM_SPEC = pl.BlockSpec(memory_space=pltpu.HBM)
EFFECT = pltpu.SideEffectType.DATAFLOW_SIDE_EFFECTING
N_DEV = 2 * N_CHIPS


def _position():
    x, y, c = lax.axis_index("x"), lax.axis_index("y"), lax.axis_index("c")
    chips = [(1 - x, y), (x, 1 - y), (1 - x, 1 - y)]
    return x, y, c, chips


def _split_start(name, plan, bufs, n_sems, deps=(), earlier=None):
    n = len(bufs)
    held = () if earlier is None else tuple(earlier[1:])

    def body(*refs):
        first_out = n + len(held) + len(deps)
        if earlier is not None:
            sends, recvs = earlier[0](refs[:n], refs[n], refs[n + 1])
            for kw in sends:
                pltpu.make_async_remote_copy(**kw).wait_send()
            for kw in recvs:
                pltpu.make_async_remote_copy(**kw).wait_recv()
        sends, _ = plan(refs[:n], refs[first_out], refs[first_out + 1])
        for kw in sends:
            pltpu.make_async_remote_copy(**kw).start()
        refs[-1][...] = jnp.zeros_like(refs[-1])

    out = pl.pallas_call(
        body, name=name,
        out_shape=(pltpu.SemaphoreType.DMA((n_sems,)), pltpu.SemaphoreType.DMA((n_sems,)),
                   *[pltpu.HBM(b.shape, b.dtype) for b in bufs], SDS((8, 128), F32)),
        in_specs=[HBM_SPEC] * n + [SEM_SPEC] * len(held) + [ANY_SPEC] * len(deps),
        out_specs=(SEM_SPEC, SEM_SPEC, *[HBM_SPEC] * n, pl.BlockSpec(memory_space=pltpu.VMEM)),
        input_output_aliases={i: 2 + i for i in range(n)},
        compiler_params=pltpu.CompilerParams(has_side_effects=EFFECT),
    )(*[pltpu.with_memory_space_constraint(b, pltpu.HBM) for b in bufs], *held, *deps)
    return out[0], out[1], list(out[2:2 + n]), out[-1]


def _split_wait(name, plan, send_sems, recv_sems, bufs, after=()):
    n = len(bufs)

    def body(*refs):
        sends, recvs = plan(refs[:n], refs[n], refs[n + 1])
        for kw in sends:
            pltpu.make_async_remote_copy(**kw).wait_send()
        for kw in recvs:
            pltpu.make_async_remote_copy(**kw).wait_recv()

    out = pl.pallas_call(
        body, name=name, out_shape=tuple(pltpu.HBM(b.shape, b.dtype) for b in bufs),
        in_specs=[HBM_SPEC] * n + [SEM_SPEC, SEM_SPEC] + [ANY_SPEC] * len(after),
        out_specs=tuple([HBM_SPEC] * n), input_output_aliases={i: i for i in range(n)},
        compiler_params=pltpu.CompilerParams(has_side_effects=EFFECT),
    )(*bufs, send_sems, recv_sems, *after)
    return list(out)


def _region(kind, ref, chip, half):
    K, N = ref.shape
    if kind == "col":
        return ref.at[pl.ds(half * (K // 2), K // 2), pl.ds(chip * (N // N_CHIPS), N // N_CHIPS)]
    rows = K // (2 * N_CHIPS)
    return ref.at[pl.ds((2 * chip + half) * rows, rows), :]


def _gather_plan(kinds, over_chips):
    def plan(refs, send_sems, recv_sems):
        x, y, c, chips = _position()
        sends, recvs = [], []
        for f, (ref, kind) in enumerate(zip(refs, kinds)):
            for k, chip in enumerate(chips):
                theirs = 2 * chip[0] + chip[1]
                sem = dict(send_sem=send_sems.at[3 * f + k], recv_sem=recv_sems.at[3 * f + k], device_id_type=MESH)
                if over_chips:
                    out, back, to = _region(kind, ref, 2 * x + y, c), _region(kind, ref, theirs, c), (*chip, c)
                else:
                    out, back, to = _region(kind, ref, theirs, c), _region(kind, ref, theirs, 1 - c), (x, y, 1 - c)
                sends.append(dict(src_ref=out, dst_ref=out, device_id=to, **sem))
                recvs.append(dict(src_ref=back, dst_ref=back, device_id=to, **sem))
        return sends, recvs
    return plan


def _reduce_plan(refs, send_sems, recv_sems):
    x, y, c, _ = _position()
    me = 4 * x + 2 * y + c
    sends, recvs = [], []
    for f in range(len(refs) // 2):
        acc, land = refs[2 * f], refs[2 * f + 1]
        for d in range(1, N_DEV):
            t = (me + d) % N_DEV
            to = dict(device_id=(t // 4, (t // 2) % 2, t % 2), device_id_type=MESH)
            slot = N_DEV - 1 - d
            sends.append(dict(src_ref=acc.at[t % 2, t // 2], dst_ref=land.at[slot], send_sem=send_sems.at[7 * f + d - 1],
                              recv_sem=recv_sems.at[7 * f + slot], **to))
            recvs.append(dict(src_ref=land.at[d - 1], dst_ref=land.at[d - 1], send_sem=send_sems.at[7 * f + d - 1],
                              recv_sem=recv_sems.at[7 * f + d - 1], **to))
    return sends, recvs


def _swap_plan(refs, send_sems, recv_sems):
    x, y, c, _ = _position()
    sends, recvs = [], []
    for f, g in enumerate(refs):
        sem = dict(send_sem=send_sems.at[f], recv_sem=recv_sems.at[f], device_id=(x, y, 1 - c), device_id_type=MESH)
        sends.append(dict(src_ref=g.at[c], dst_ref=g.at[c], **sem))
        recvs.append(dict(src_ref=g.at[1 - c], dst_ref=g.at[1 - c], **sem))
    return sends, recvs


def _sum_pieces(ids2, acc, land, name):
    _, _, nr, nc = acc.shape
    tr = _tile(nr, 256, 16)

    def body(ids_ref, own_ref, land_ref, o_ref):
        del ids_ref
        s = own_ref[...].astype(F32)
        for k in range(N_DEV - 1):
            s = s + land_ref[k].astype(F32)
        o_ref[...] = s

    return pl.pallas_call(
        body,
        grid_spec=pltpu.PrefetchScalarGridSpec(
            num_scalar_prefetch=1, grid=(nr // tr,),
            in_specs=[pl.BlockSpec((None, None, tr, nc), lambda i, ids: (ids[0], ids[1], i, 0)),
                      pl.BlockSpec((N_DEV - 1, tr, nc), lambda i, ids: (0, i, 0))],
            out_specs=pl.BlockSpec((None, tr, nc), lambda i, ids: (ids[0], i, 0))),
        out_shape=SDS((2, nr, nc), F32), name=name, compiler_params=_params("parallel"))(ids2, acc, land)


def _small_plan(refs, send_sems, recv_sems):
    x, y, c, _ = _position()
    me = 4 * x + 2 * y + c
    own, land = refs
    sends, recvs = [], []
    for d in range(1, N_DEV):
        t = (me + d) % N_DEV
        to = dict(device_id=(t // 4, (t // 2) % 2, t % 2), device_id_type=MESH)
        sends.append(dict(src_ref=own, dst_ref=land.at[me], send_sem=send_sems.at[d - 1],
                          recv_sem=recv_sems.at[N_DEV - 1 - d], **to))
        recvs.append(dict(src_ref=land.at[t], dst_ref=land.at[t], send_sem=send_sems.at[d - 1],
                          recv_sem=recv_sems.at[d - 1], **to))
    return sends, recvs


def _sum_blocks(me1, own, land):
    def body(me_ref, own_ref, land_ref, o_ref):
        acc = None
        for d in range(N_DEV):
            term = jnp.where(me_ref[0] == d, own_ref[...], land_ref[d])
            acc = term if acc is None else acc + term
        o_ref[...] = acc

    return pl.pallas_call(
        body,
        grid_spec=pltpu.PrefetchScalarGridSpec(
            num_scalar_prefetch=1, grid=(1,),
            in_specs=[pl.BlockSpec(own.shape, lambda i, me: (0, 0)), pl.BlockSpec(land.shape, lambda i, me: (0, 0, 0))],
            out_specs=pl.BlockSpec(own.shape, lambda i, me: (0, 0))),
        out_shape=SDS(own.shape, F32), name="sum_small", compiler_params=_params("arbitrary"))(me1, own, land)


BIG = {"ev_w_in": "col", "ev_w_out": "row", "od_w_in": "col", "od_w_out": "row", "mlp_w1": "col", "mlp_w2": "row"}
WEIGHTS = ("meta_tokens", "mix_norm_g", "mlp_norm_g", "final_norm_g", "ev_w_in", "ev_conv_w", "ev_conv_b", "ev_ln_g",
           "ev_ln_b", "ev_pool_w", "ev_pool_b", "ev_pool_scale", "ev_w_out", "od_w_in", "od_gnorm_g", "od_w_out",
           "lb_param", "mlp_w1", "mlp_w2")
PACK_UNIT = 1024


def _mixer_names(layer):
    return ("ev_w_in", "ev_w_out") if layer % 2 == 0 else ("od_w_in", "od_w_out")


def _pack(arrays):
    flat = []
    for a in arrays:
        a = a.reshape(-1)
        flat.append(jnp.pad(a, (0, (-a.shape[0]) % PACK_UNIT)))
    return jnp.concatenate(flat).reshape(-1, 128)


def _unpack(packed, shapes):
    flat = packed.reshape(-1)
    out, off = [], 0
    for s in shapes:
        size = 1
        for d in s:
            size *= d
        out.append(flat[off:off + size].reshape(s))
        off += size + (-size) % PACK_UNIT
    return out


def _local_step(x2, target, P, weights, boundary, first_deps=()):
    D = x2.shape[1]
    n_layers = P["mix_norm_g"].shape[0]
    h = jnp.concatenate([jnp.zeros((PAD, D), F32), P["meta_full"], x2], axis=0)
    mix_g = P["mix_norm_g"].reshape(n_layers, 1, D)
    mlp_g = P["mlp_norm_g"].reshape(n_layers, 1, D)
    vec = lambda a: a.reshape(a.shape[0], 1, -1)
    cb3, lg3, lnb3, ps3 = vec(P["ev_conv_b"]), vec(P["ev_ln_g"]), vec(P["ev_ln_b"]), vec(P["ev_pool_scale"])
    pb3 = vec(P["ev_pool_b"])
    gn3 = vec(P["od_gnorm_g"])
    lb_all = _lb_fwd(P["lb_param"])
    lb3 = lb_all.reshape(n_layers, 1, D)
    even = (cb3, lg3, lnb3, P["ev_pool_w"], pb3, ps3)

    saved = []
    deps = tuple(first_deps)
    for layer in range(n_layers):
        j = layer // 2
        w_in, w_out = _mixer_names(layer)
        W = {}
        s = {"h": h, "W": W}
        s["n"] = _rms_fwd(h, mix_g, layer, "mix_norm_0", deps=deps) if layer == 0 else n_next
        deps = ()
        W[w_in], held = weights(layer, w_in, (s["n"],))
        s["u"] = _mm_nn(s["n"], W[w_in], 0, f"mix_in_{layer}", deps=held)
        if layer % 2 == 0:
            s["y"], s["yc"] = _even_fwd(s["u"], P["conv_w_full"], *even, j, f"even_fwd_{layer}")
        else:
            s["y"], s["o"], s["sall"] = _hgrn_fwd(s["u"], lb3, layer, gn3, j, f"hgrn_fwd_{layer}")
        W[w_out], held = weights(layer, w_out, (s["y"],))
        h, s["n2"] = _mm_nn_norm(s["y"], W[w_out], 0, h, mlp_g, layer, f"mix_out_{layer}", deps=held)
        s["h1"] = h
        W["mlp_w1"], held = weights(layer, "mlp_w1", (s["n2"],))
        if layer == 0:
            s["relu"] = _mm_nn(s["n2"], W["mlp_w1"], 0, "mlp_up_0", relu=True, deps=held)
            W["mlp_w2"], held = weights(layer, "mlp_w2", (s["relu"],))
            h, n_next = _mm_nn_norm(s["relu"], W["mlp_w2"], 0, h, mix_g, 1, "mlp_down_0", square=True, deps=held)
        else:
            W["mlp_w2"], more = weights(layer, "mlp_w2", (s["n2"],))
            last = layer + 1 == n_layers
            out = _mlp_fwd(s["n2"], W["mlp_w1"], W["mlp_w2"], h, None if last else mix_g, layer + 1, f"mlp_{layer}", deps=held + more)
            h, s["relu"] = out[0], out[-1]
            n_next = None if last else out[1]
        saved.append(s)

    dh, dhb, dg_final, loss = _final(h, P["final_norm_g"].reshape(1, D), target)

    small = {"final_norm_g": dg_final}
    per_layer = {k: [None] * n_layers for k in ("mix_norm_g", "mlp_norm_g", "lb")}
    per_pair = {k: [None] * (n_layers // 2) for k in
                ("ev_conv_w", "ev_conv_b", "ev_ln_g", "ev_ln_b", "ev_pool_w", "ev_pool_b", "ev_pool_scale", "od_gnorm_g")}
    for layer in reversed(range(n_layers)):
        j = layer // 2
        s = saved[layer]
        W = s["W"]
        w_in, w_out = _mixer_names(layer)
        dz = _mm_nt(dhb, W["mlp_w2"], 0, f"d_act_{layer}", relu=s["relu"], deps=deps)
        dw2 = _mm_tn(s["relu"], dhb, "row", f"dw2_{layer}", square=True)
        dw1 = _mm_tn(s["n2"], dz, "col", f"dw1_{layer}")
        dh, dhb, per_layer["mlp_norm_g"][layer] = _mm_nt_norm(dz, W["mlp_w1"], 0, s["h1"], mlp_g, layer, dh, f"d_n2_{layer}")
        deps = boundary(f"mlp{layer}", {("mlp_w1", layer): dw1, ("mlp_w2", layer): dw2}, (dhb, dw1, dw2))
        dy = _mm_nt(dhb, W[w_out], 0, f"d_y_{layer}", deps=deps)
        dwout = _mm_tn(s["y"], dhb, "row", f"dwout_{layer}")
        if layer % 2 == 0:
            du, dcw, dcb, dlg, dlnb, dpw, dpb, dps = _even_bwd(s["u"], s["yc"], dy, P["conv_w_full"], *even, j, f"even_bwd_{layer}")
            for k, val in (("ev_conv_w", dcw), ("ev_conv_b", dcb), ("ev_ln_g", dlg), ("ev_ln_b", dlnb),
                           ("ev_pool_w", dpw), ("ev_pool_b", dpb), ("ev_pool_scale", dps)):
                per_pair[k][j] = val
        else:
            du, per_layer["lb"][layer], per_pair["od_gnorm_g"][j] = _hgrn_bwd(
                s["u"], s["o"], dy, s["sall"], lb3, layer, gn3, j, f"hgrn_bwd_{layer}")
        dwin = _mm_tn(s["n"], du, "col", f"dwin_{layer}")
        deps = boundary(f"mix{layer}", {(w_in, j): dwin, (w_out, j): dwout}, (du, dwin, dwout))
        dh, dhb, per_layer["mix_norm_g"][layer] = _mm_nt_norm(du, W[w_in], 0, s["h"], mix_g, layer, dh, f"d_n_{layer}", deps=deps)
        deps = ()

    small["mix_norm_g"] = jnp.concatenate(per_layer["mix_norm_g"], axis=0)
    small["mlp_norm_g"] = jnp.concatenate(per_layer["mlp_norm_g"], axis=0)
    dlb_all = jnp.concatenate([jnp.zeros((1, D), F32) if g is None else g for g in per_layer["lb"]], axis=0)
    small["lb_param"] = _lb_bwd(P["lb_param"], dlb_all)
    for k, vals in per_pair.items():
        small[k] = jnp.stack(vals, axis=0)
    small["meta_tokens"] = dh[PAD:LEAD]
    return loss, dh, small


def kernel(x, meta_tokens, mix_norm_g, mlp_norm_g, final_norm_g, ev_w_in, ev_conv_w, ev_conv_b, ev_ln_g, ev_ln_b, ev_pool_w, ev_pool_b, ev_pool_scale, ev_w_out, od_w_in, od_gnorm_g, od_w_out, lb_param, mlp_w1, mlp_w2, loss_target, m_meta_tokens, m_mix_norm_g, m_mlp_norm_g, m_final_norm_g, m_ev_w_in, m_ev_conv_w, m_ev_conv_b, m_ev_ln_g, m_ev_ln_b, m_ev_pool_w, m_ev_pool_b, m_ev_pool_scale, m_ev_w_out, m_od_w_in, m_od_gnorm_g, m_od_w_out, m_lb_param, m_mlp_w1, m_mlp_w2, v_meta_tokens, v_mix_norm_g, v_mlp_norm_g, v_final_norm_g, v_ev_w_in, v_ev_conv_w, v_ev_conv_b, v_ev_ln_g, v_ev_ln_b, v_ev_pool_w, v_ev_pool_b, v_ev_pool_scale, v_ev_w_out, v_od_w_in, v_od_gnorm_g, v_od_w_out, v_lb_param, v_mlp_w1, v_mlp_w2):
    given = dict(locals())
    w = {n: given[n] for n in WEIGHTS}
    m = {n: given["m_" + n] for n in WEIGHTS}
    v = {n: given["v_" + n] for n in WEIGHTS}
    n_layers = mix_norm_g.shape[0]
    core = lax.axis_index("c").astype(jnp.int32)
    chip = (2 * lax.axis_index("x") + lax.axis_index("y")).astype(jnp.int32)
    chip1 = chip.reshape(1)
    ids2 = jnp.stack([core, chip])

    conv_pad = jnp.pad(ev_conv_w, ((0, 0), (0, CONV_ROWS - CONV_WIDTH), (0, 0)))
    stages = [[(0, n)] for n in (*_mixer_names(0), "mlp_w1", "mlp_w2")]
    for layer in range(1, n_layers):
        stages += [[(layer, n) for n in _mixer_names(layer)], [(layer, "mlp_w1"), (layer, "mlp_w2")]]
    gathers, where, token = [], {}, ()
    for k, stage in enumerate(stages):
        index = [layer if n.startswith("mlp") else layer // 2 for layer, n in stage]
        kinds = [BIG[n] for _, n in stage]
        bufs = [_cast_place(w[n], i, BIG[n], chip1, BF16, f"place_{n}_{i}") for (_, n), i in zip(stage, index)]
        if k == 0:
            bufs.append(_cast_place(meta_tokens[None], 0, "col", chip1, F32, "place_meta"))
            bufs.append(_cast_place(conv_pad.reshape(1, -1, conv_pad.shape[2]), 0, "col", chip1, F32, "place_conv_w"))
            kinds += ["col", "col"]
        plan = _gather_plan(kinds, True)
        ss, rs, bufs, tok = _split_start(f"gather_start_{k}", plan, bufs, 3 * len(bufs), deps=token)
        token = (tok,)
        gathers.append((kinds, plan, ss, rs, bufs))
        where.update({key: (k, f) for f, key in enumerate(stage)})

    landed, passed, held = {}, {}, []

    def hand_on(k, deps):
        if k not in passed:
            kinds, plan, ss, rs, bufs = gathers[k]
            to_sibling = _gather_plan(kinds, False)
            ss, rs, bufs, tok = _split_start(f"gather_pass_{k}", to_sibling, bufs, 3 * len(bufs), deps=deps, earlier=(plan, ss, rs))
            passed[k] = (to_sibling, ss, rs, bufs)
            held.append(tok)

    def arrived(k, after):
        if k not in landed:
            hand_on(k, after)
            landed[k] = _split_wait(f"gather_wait_{k}", *passed[k], after)
        return landed[k]

    def weights(layer, name, after):
        k, f = where[(layer, name)]
        full = arrived(k, after)[f][None]
        if name == "mlp_w2" and layer + 1 < n_layers:
            hand_on(where[(layer + 1, _mixer_names(layer + 1)[0])][0], after)
        if layer > 0 and name == _mixer_names(layer)[0]:
            hand_on(where[(layer, "mlp_w1")][0], after)
        tokens = tuple(held)
        held.clear()
        return full, tokens

    first = arrived(0, token)
    P = {n: w[n] for n in ("mix_norm_g", "mlp_norm_g", "final_norm_g", "ev_conv_b", "ev_ln_g", "ev_ln_b", "ev_pool_w",
                           "ev_pool_b", "ev_pool_scale", "od_gnorm_g", "lb_param")}
    P["meta_full"] = first[1]
    P["conv_w_full"] = first[2].reshape(ev_conv_w.shape[0], CONV_ROWS, -1)

    pending, outs = [], {n: None for n in BIG}

    def advance(after, fresh=1):
        tokens, still = [], []
        for pos, st in enumerate(pending):
            if st["phase"] == 1 and pos >= len(pending) - fresh:
                still.append(st)
            elif st["phase"] == 1:
                bufs = _split_wait(f"reduce_wait_{st['tag']}", _reduce_plan, st["ss"], st["rs"], st["bufs"], after)
                halves = [_sum_pieces(ids2, bufs[2 * f], bufs[2 * f + 1], f"sum_{st['tag']}_{f}") for f in range(len(bufs) // 2)]
                ss, rs, halves, tok = _split_start(f"swap_start_{st['tag']}", _swap_plan, halves, len(halves))
                tokens.append(tok)
                still.append(dict(st, phase=2, ss=ss, rs=rs, bufs=halves))
            else:
                grads = _split_wait(f"swap_wait_{st['tag']}", _swap_plan, st["ss"], st["rs"], st["bufs"], after)
                for (n, i), g in zip(st["keys"], grads):
                    outs[n] = _adamw_layer(w[n], m[n], v[n], g.reshape(w[n].shape[1:]), i, outs[n], f"adamw_{n}_{i}")
        pending[:] = still
        return tokens

    def boundary(tag, grads, after):
        tokens = advance(after)
        bufs = []
        for acc in grads.values():
            bufs += [acc, lax.empty((N_DEV - 1,) + acc.shape[2:], BF16)]
        ss, rs, bufs, tok = _split_start(f"reduce_start_{tag}", _reduce_plan, bufs, 7 * len(grads))
        pending.append(dict(phase=1, tag=tag, keys=list(grads), ss=ss, rs=rs, bufs=bufs))
        return tuple(tokens + [tok])

    loss, dh, small = _local_step(x[0], loss_target[0], P, weights, boundary, first_deps=token)

    order = [n for n in WEIGHTS if n not in BIG]
    block = _pack([small[n] for n in order] + [loss])
    ss, rs, bufs, tok = _split_start("small_start", _small_plan, [block, lax.empty((N_DEV,) + block.shape, F32)], N_DEV - 1)
    while pending:
        advance((tok,) + tuple(o[0] for o in outs.values() if o is not None), fresh=0)
    block, land = _split_wait("small_wait", _small_plan, ss, rs, bufs, tuple(outs[n][0] for n in BIG))
    packed = _sum_blocks((4 * lax.axis_index("x") + 2 * lax.axis_index("y") + lax.axis_index("c")).astype(jnp.int32).reshape(1), block, land)
    total = _unpack(packed, [small[n].shape for n in order] + [loss.shape])
    loss_sum = total[-1][0, 0]
    gsmall = dict(zip(order, total[:-1]))
    gsmall["meta_tokens"] = lax.dynamic_slice_in_dim(gsmall["meta_tokens"], chip * meta_tokens.shape[1], meta_tokens.shape[1], 1)
    gsmall["ev_conv_w"] = lax.dynamic_slice_in_dim(gsmall["ev_conv_w"][:, :CONV_WIDTH], chip * ev_conv_w.shape[2], ev_conv_w.shape[2], 2)

    g_out, d_out, m_out, v_out = {}, {}, {}, {}
    for n in WEIGHTS:
        if n in BIG:
            g_out[n], d_out[n], m_out[n], v_out[n] = outs[n]
            continue
        shape = w[n].shape
        g = gsmall[n].reshape(shape)
        cols = shape[-1] if len(shape) > 1 else 128
        two = lambda a: a.reshape(-1, cols)
        d_, m_, v_ = _adamw(two(w[n]), two(g), two(m[n]), two(v[n]), f"adamw_{n}")
        g_out[n], d_out[n], m_out[n], v_out[n] = g, d_.reshape(shape), m_.reshape(shape), v_.reshape(shape)

    grad_x = dh[LEAD:][None]
    return (loss_sum, grad_x, *[g_out[n] for n in WEIGHTS], *[d_out[n] for n in WEIGHTS],
            *[m_out[n] for n in WEIGHTS], *[v_out[n] for n in WEIGHTS])
```

```python
import functools

import jax
import jax.numpy as jnp
from jax import lax
from jax.experimental import pallas as pl
from jax.experimental.pallas import tpu as pltpu

F32 = jnp.float32
BF16 = jnp.bfloat16
SDS = jax.ShapeDtypeStruct
MESH = pl.DeviceIdType.MESH
ANY_SPEC = pl.BlockSpec(memory_space=pl.ANY)

N_META = 16
CHUNK = 64
LEAD = CHUNK
PAD = LEAD - N_META
CONV_WIDTH = 31
CONV_ROWS = 32
POOL_WINDOWS = (2, 4, 8, 16)
HEAD_DIM = 128
SUB = 16
EXP_CAP = 80.0
EPS = 1e-6
ADAM_LR = 0.001
ADAM_B1 = 0.9
ADAM_B2 = 0.999
ADAM_EPS = 1e-08
ADAM_WD = 0.01
ADAM_STEP = 10
N_CHIPS = 4
VMEM_LIMIT = 52 << 20
MM_VMEM_BUDGET = 44 << 20


def _params(*sem):
    return pltpu.CompilerParams(dimension_semantics=sem if sem else None, vmem_limit_bytes=VMEM_LIMIT)


def _tile(n, target, unit=CHUNK):
    best = None
    for t in range(unit, min(n, target) + 1, unit):
        if n % t == 0:
            best = t
    assert best is not None, (n, target, unit)
    return best


def _ctile(n, target=512):
    for t in (512, 384, 256, 128):
        if t <= target and n % t == 0:
            return t
    raise ValueError(n)


def _mm_tiles(M, N, per_row, per_col, per_elem):
    best = None
    for tn in (512, 384, 256, 128):
        if N % tn:
            continue
        for tm in sorted((d for d in range(16, M + 1, 16) if M % d == 0), reverse=True):
            if 2 * (tm * per_row + tn * per_col + tm * tn * per_elem) <= MM_VMEM_BUDGET:
                if best is None or tm * tn > best[0] * best[1]:
                    best = (tm, tn)
                break
    assert best is not None, (M, N)
    return best


def _sigmoid(x):
    return 1.0 / (1.0 + jnp.exp(-x))


def _row_ids(shape, base):
    return lax.broadcasted_iota(jnp.int32, shape, 0) + base


def _cast_place(w3, layer, kind, chip1, dtype, name):
    _, ks, ns = w3.shape
    tr = _tile(ks, 512, 16)
    full = (ks, ns * N_CHIPS) if kind == "col" else (ks * N_CHIPS, ns)

    def body(chip_ref, w_ref, o_ref):
        del chip_ref
        o_ref[...] = w_ref[...].astype(dtype)

    omap = (lambda i, chip: (i, chip[0])) if kind == "col" else (lambda i, chip: (chip[0] * (ks // tr) + i, 0))
    return pl.pallas_call(
        body,
        grid_spec=pltpu.PrefetchScalarGridSpec(
            num_scalar_prefetch=1, grid=(ks // tr,),
            in_specs=[pl.BlockSpec((None, tr, ns), lambda i, chip: (layer, i, 0))],
            out_specs=pl.BlockSpec((tr, ns), omap)),
        out_shape=SDS(full, dtype), name=name, compiler_params=_params("parallel"))(chip1, w3)


def _rms_fwd(h, g3, layer, name, deps=()):
    T, D = h.shape
    tm = _tile(T, 832)

    def body(h_ref, g_ref, *rest):
        n_ref = rest[-1]
        x = h_ref[...]
        r = lax.rsqrt(jnp.mean(x * x, axis=-1, keepdims=True) + EPS)
        n_ref[...] = ((x * r) * g_ref[...]).astype(BF16)

    return pl.pallas_call(
        body, grid=(T // tm,),
        in_specs=[pl.BlockSpec((tm, D), lambda i: (i, 0)), pl.BlockSpec((None, 1, D), lambda i: (layer, 0, 0))]
        + [ANY_SPEC] * len(deps),
        out_specs=pl.BlockSpec((tm, D), lambda i: (i, 0)), out_shape=SDS((T, D), BF16),
        name=name, compiler_params=_params("parallel"))(h, g3, *deps)


def _final(h, g2, target):
    T, D = h.shape
    tm = _tile(T, 320)
    nsub = tm // CHUNK
    nblk = target.shape[0] // CHUNK

    def body(h_ref, g_ref, *rest):
        t_refs = rest[:nsub]
        dh_ref, dhb_ref, dg_ref, loss_ref = rest[nsub:]
        i = pl.program_id(0)

        @pl.when(i == 0)
        def _():
            dg_ref[...] = jnp.zeros_like(dg_ref)
            loss_ref[...] = jnp.zeros_like(loss_ref)

        g = g_ref[...]
        for q in range(nsub):
            rows = slice(q * CHUNK, (q + 1) * CHUNK)
            x = h_ref[rows, :]
            r = lax.rsqrt(jnp.mean(x * x, axis=-1, keepdims=True) + EPS)
            xh = x * r
            live = jnp.where(i * nsub + q > 0, 1.0, 0.0).astype(F32)
            e = ((xh * g) - t_refs[q][...]) * live
            dy = e * (1.0 / D)
            dxh = dy * g
            dh = r * (dxh - xh * jnp.mean(dxh * xh, axis=-1, keepdims=True))
            dh_ref[rows, :] = dh
            dhb_ref[rows, :] = dh.astype(BF16)
            dg_ref[...] += jnp.sum(dy * xh, axis=0, keepdims=True)
            loss_ref[...] += jnp.sum(e * e) * (0.5 / D)

    row = pl.BlockSpec((tm, D), lambda i: (i, 0))
    t_specs = [pl.BlockSpec((CHUNK, D), functools.partial(lambda i, q: (jnp.clip(i * nsub + q - 1, 0, nblk - 1), 0), q=q))
               for q in range(nsub)]
    return pl.pallas_call(
        body, grid=(T // tm,),
        in_specs=[row, pl.BlockSpec((1, D), lambda i: (0, 0))] + t_specs,
        out_specs=[row, row, pl.BlockSpec((1, D), lambda i: (0, 0)), pl.BlockSpec((1, 128), lambda i: (0, 0))],
        out_shape=[SDS((T, D), F32), SDS((T, D), BF16), SDS((1, D), F32), SDS((1, 128), F32)],
        name="final_loss", compiler_params=_params("arbitrary"))(h, g2, *([target] * nsub))


def _mm_nn(a, w3, layer, name, res=None, relu=False, square=False, deps=()):
    M, K = a.shape
    N = w3.shape[2]
    tm, tn = _mm_tiles(M, N, 2 * K, 2 * K, (2 if relu else 4) + (4 if res is not None else 0))

    def body(*refs):
        lhs = refs[0][...]
        acc = jnp.dot(lhs * lhs if square else lhs, refs[1][...], preferred_element_type=F32)
        if res is not None:
            acc = acc + refs[2][...]
        refs[-1][...] = jnp.maximum(acc, 0.0).astype(BF16) if relu else acc

    in_specs = [pl.BlockSpec((tm, K), lambda i, j: (i, 0)), pl.BlockSpec((None, K, tn), lambda i, j: (layer, 0, j))]
    args = [a, w3]
    tile = pl.BlockSpec((tm, tn), lambda i, j: (i, j))
    if res is not None:
        in_specs.append(tile)
        args.append(res)
    in_specs += [ANY_SPEC] * len(deps)
    args += list(deps)
    return pl.pallas_call(
        body, grid=(M // tm, N // tn), in_specs=in_specs, out_specs=tile,
        out_shape=SDS((M, N), BF16 if relu else F32),
        name=name, compiler_params=_params("parallel", "parallel"))(*args)


def _mm_nt(dy, w3, layer, name, relu=None, deps=()):
    M, N = dy.shape
    K = w3.shape[1]
    tm, tk = _mm_tiles(M, K, 2 * N, 2 * N, 4)

    def body(*refs):
        acc = lax.dot_general(refs[0][...], refs[1][...], (((1,), (1,)), ((), ())), preferred_element_type=F32)
        if relu is not None:
            acc = (acc * (2.0 * refs[2][...].astype(F32))).astype(BF16)
        refs[-1][...] = acc

    tile = pl.BlockSpec((tm, tk), lambda i, j: (i, j))
    in_specs = [pl.BlockSpec((tm, N), lambda i, j: (i, 0)), pl.BlockSpec((None, tk, N), lambda i, j: (layer, j, 0))]
    args = [dy, w3]
    if relu is not None:
        in_specs.append(tile)
        args.append(relu)
    in_specs += [ANY_SPEC] * len(deps)
    args += list(deps)
    return pl.pallas_call(
        body, grid=(M // tm, K // tk), in_specs=in_specs, out_specs=tile,
        out_shape=SDS((M, K), F32 if relu is None else BF16),
        name=name, compiler_params=_params("parallel", "parallel"))(*args)


def _row_tile(M, per_row, fixed):
    for tm in sorted((d for d in range(16, M + 1, 16) if M % d == 0), reverse=True):
        if 2 * (tm * per_row + fixed) <= MM_VMEM_BUDGET:
            return tm
    raise ValueError((M, per_row, fixed))


def _mm_nn_norm(a, w3, layer, res, g3, glayer, name, square=False, deps=()):
    M, K = a.shape
    D = w3.shape[2]
    tm = _row_tile(M, 2 * K + 10 * D, 2 * K * D)

    def body(a_ref, w_ref, r_ref, g_ref, *rest):
        h_ref, n_ref = rest[-2:]
        lhs = a_ref[...]
        x = r_ref[...] + jnp.dot(lhs * lhs if square else lhs, w_ref[...], preferred_element_type=F32)
        h_ref[...] = x
        r = lax.rsqrt(jnp.mean(x * x, axis=-1, keepdims=True) + EPS)
        n_ref[...] = ((x * r) * g_ref[...]).astype(BF16)

    row = pl.BlockSpec((tm, D), lambda i: (i, 0))
    return pl.pallas_call(
        body, grid=(M // tm,),
        in_specs=[pl.BlockSpec((tm, K), lambda i: (i, 0)), pl.BlockSpec((None, K, D), lambda i: (layer, 0, 0)), row,
                  pl.BlockSpec((None, 1, D), lambda i: (glayer, 0, 0))] + [ANY_SPEC] * len(deps),
        out_specs=[row, row], out_shape=[SDS((M, D), F32), SDS((M, D), BF16)],
        name=name, compiler_params=_params("parallel"))(a, w3, res, g3, *deps)


def _mlp_fwd(n2, w1, w2, res, g3, glayer, name, deps=()):
    M, D = n2.shape
    F = w1.shape[2]
    hb = _ctile(F)
    tm = _row_tile(M, 2 * D + 8 * D + (2 * D if g3 is not None else 0) + 2 * F, 2 * D * F)

    def body(n_ref, w1_ref, w2_ref, res_ref, *rest):
        outs = rest[-3:] if g3 is not None else rest[-2:]
        x = n_ref[...]
        acc = res_ref[...]
        for jb in range(F // hb):
            cols = slice(jb * hb, (jb + 1) * hb)
            r = jnp.maximum(jnp.dot(x, w1_ref[:, cols], preferred_element_type=F32), 0.0).astype(BF16)
            outs[-1][:, cols] = r
            acc = acc + jnp.dot(r * r, w2_ref[cols, :], preferred_element_type=F32)
        outs[0][...] = acc
        if g3 is not None:
            rr = lax.rsqrt(jnp.mean(acc * acc, axis=-1, keepdims=True) + EPS)
            outs[1][...] = ((acc * rr) * rest[0][...]).astype(BF16)

    row = pl.BlockSpec((tm, D), lambda i: (i, 0))
    once = dict(pipeline_mode=pl.Buffered(1))
    in_specs = [row, pl.BlockSpec((None, D, F), lambda i: (0, 0, 0), **once), pl.BlockSpec((None, F, D), lambda i: (0, 0, 0), **once), row]
    args = [n2, w1, w2, res]
    out_specs, out_shape = [row], [SDS((M, D), F32)]
    if g3 is not None:
        in_specs.append(pl.BlockSpec((None, 1, D), lambda i: (glayer, 0, 0)))
        args.append(g3)
        out_specs.append(row)
        out_shape.append(SDS((M, D), BF16))
    out_specs.append(pl.BlockSpec((tm, F), lambda i: (i, 0)))
    out_shape.append(SDS((M, F), BF16))
    in_specs += [ANY_SPEC] * len(deps)
    args += list(deps)
    return pl.pallas_call(
        body, grid=(M // tm,), in_specs=in_specs, out_specs=out_specs, out_shape=out_shape,
        name=name, compiler_params=_params("parallel"))(*args)


def _mlp_bwd(dhb, relu, w1, w2, h, g3, glayer, dh_in, name, deps=()):
    M, D = dhb.shape
    F = w1.shape[2]
    hb = _ctile(F)
    tm = _row_tile(M, 16 * D + 4 * F, 2 * D * F)

    def body(dy_ref, r_ref, w1_ref, w2_ref, h_ref, g_ref, dhi_ref, *rest):
        dz_ref, dh_ref, dhb_ref, dg_ref = rest[-4:]
        dy = dy_ref[...]
        dn = jnp.zeros((tm, D), F32)
        for jb in range(F // hb):
            cols = slice(jb * hb, (jb + 1) * hb)
            dact = lax.dot_general(dy, w2_ref[cols, :], (((1,), (1,)), ((), ())), preferred_element_type=F32)
            dz = (dact * (2.0 * r_ref[:, cols].astype(F32))).astype(BF16)
            dz_ref[:, cols] = dz
            dn = dn + lax.dot_general(dz, w1_ref[:, cols], (((1,), (1,)), ((), ())), preferred_element_type=F32)
        x = h_ref[...]
        r = lax.rsqrt(jnp.mean(x * x, axis=-1, keepdims=True) + EPS)
        xh = x * r
        dxh = dn * g_ref[...]
        dh = dhi_ref[...] + r * (dxh - xh * jnp.mean(dxh * xh, axis=-1, keepdims=True))
        dh_ref[...] = dh
        dhb_ref[...] = dh.astype(BF16)

        @pl.when(pl.program_id(0) == 0)
        def _():
            dg_ref[...] = jnp.zeros_like(dg_ref)

        dg_ref[...] += jnp.sum(dn * xh, axis=0, keepdims=True)

    row = pl.BlockSpec((tm, D), lambda i: (i, 0))
    wide = pl.BlockSpec((tm, F), lambda i: (i, 0))
    once = dict(pipeline_mode=pl.Buffered(1))
    return pl.pallas_call(
        body, grid=(M // tm,),
        in_specs=[row, wide, pl.BlockSpec((None, D, F), lambda i: (0, 0, 0), **once),
                  pl.BlockSpec((None, F, D), lambda i: (0, 0, 0), **once), row,
                  pl.BlockSpec((None, 1, D), lambda i: (glayer, 0, 0)), row] + [ANY_SPEC] * len(deps),
        out_specs=[wide, row, row, pl.BlockSpec((1, D), lambda i: (0, 0))],
        out_shape=[SDS((M, F), BF16), SDS((M, D), F32), SDS((M, D), BF16), SDS((1, D), F32)],
        name=name, compiler_params=_params("arbitrary"))(dhb, relu, w1, w2, h, g3, dh_in, *deps)


def _mm_nt_norm(dy, w3, layer, h, g3, glayer, dh_in, name, deps=()):
    M, N = dy.shape
    D = w3.shape[1]
    tm = _row_tile(M, 2 * N + 14 * D, 2 * N * D)

    def body(dy_ref, w_ref, h_ref, g_ref, dhi_ref, *rest):
        dh_ref, dhb_ref, dg_ref = rest[-3:]
        dn = lax.dot_general(dy_ref[...], w_ref[...], (((1,), (1,)), ((), ())), preferred_element_type=F32)
        x = h_ref[...]
        r = lax.rsqrt(jnp.mean(x * x, axis=-1, keepdims=True) + EPS)
        xh = x * r
        dxh = dn * g_ref[...]
        dh = dhi_ref[...] + r * (dxh - xh * jnp.mean(dxh * xh, axis=-1, keepdims=True))
        dh_ref[...] = dh
        dhb_ref[...] = dh.astype(BF16)

        @pl.when(pl.program_id(0) == 0)
        def _():
            dg_ref[...] = jnp.zeros_like(dg_ref)

        dg_ref[...] += jnp.sum(dn * xh, axis=0, keepdims=True)

    row = pl.BlockSpec((tm, D), lambda i: (i, 0))
    return pl.pallas_call(
        body, grid=(M // tm,),
        in_specs=[pl.BlockSpec((tm, N), lambda i: (i, 0)), pl.BlockSpec((None, D, N), lambda i: (layer, 0, 0)), row,
                  pl.BlockSpec((None, 1, D), lambda i: (glayer, 0, 0)), row] + [ANY_SPEC] * len(deps),
        out_specs=[row, row, pl.BlockSpec((1, D), lambda i: (0, 0))],
        out_shape=[SDS((M, D), F32), SDS((M, D), BF16), SDS((1, D), F32)],
        name=name, compiler_params=_params("arbitrary"))(dy, w3, h, g3, dh_in, *deps)


def _fam_dims(kind, K, N):
    return (K // 2, N // N_CHIPS) if kind == "col" else (K // (2 * N_CHIPS), N)


def _mm_tn(x, dy, kind, name, square=False):
    M, K = x.shape
    N = dy.shape[1]
    nr, nc = _fam_dims(kind, K, N)

    def body(x_ref, dy_ref, o_ref):
        lhs = x_ref[...]
        res = lax.dot_general(lhs * lhs if square else lhs, dy_ref[...], (((0,), (0,)), ((), ())), preferred_element_type=F32)
        o_ref[...] = res.astype(BF16).reshape(o_ref.shape)

    if kind == "col":
        tn = _ctile(nc)
        ct = nc // tn
        grid = (N // tn,)
        in_specs = [pl.BlockSpec((M, K), lambda j: (0, 0)), pl.BlockSpec((M, tn), lambda j: (0, j))]
        out_spec = pl.BlockSpec((2, None, nr, tn), lambda j: (0, j // ct, 0, j % ct))
    else:
        grid = (N_CHIPS,)
        in_specs = [pl.BlockSpec((M, 2 * nr), lambda i: (0, i)), pl.BlockSpec((M, N), lambda i: (0, 0))]
        out_spec = pl.BlockSpec((2, None, nr, N), lambda i: (0, i, 0, 0))
    return pl.pallas_call(
        body, grid=grid, in_specs=in_specs, out_specs=out_spec, out_shape=SDS((2, N_CHIPS, nr, nc), BF16),
        name=name, compiler_params=_params("parallel"))(x, dy)


C_EVEN = 512


def _live(rows, base, total):
    r = _row_ids((rows, 1), base)
    return jnp.logical_and(r >= PAD, r < total).astype(F32)


def _conv_taps(win, w_ref, ls, acc, flip):
    for b in range(8):
        rb = win if b == 0 else pltpu.roll(win, 96 - b, 0)
        for a in range(5):
            o = 8 * a + b
            tap = (30 - o) if flip else (o - 2)
            if 0 <= tap < CONV_WIDTH:
                acc = acc + w_ref[pl.ds(tap, 1), ls] * rb[8 * a:8 * a + CHUNK]
    return acc


def _window_sum(win, levels, forward):
    s = win
    n = win.shape[0]
    for k in range(levels):
        step = 1 << k
        s = s + pltpu.roll(s, (n - step) if forward else step, 0)
    return s


def _pool_count(base, g):
    pos = _row_ids((CHUNK, 1), base) - PAD
    return jnp.clip(pos + 1, 1, POOL_WINDOWS[g]).astype(F32)


def _even_fwd(u, cw3, cb3, lg3, lb3, pw4, pb3, ps3, j, name):
    T = u.shape[0]
    C = C_EVEN
    tm = _tile(T, 320)
    nch = tm // CHUNK
    nblk = T // CHUNK

    def body(u_ref, up_ref, cw_ref, cb_ref, lg_ref, lb_ref, pw_ref, pb_ref, ps_ref, o_ref, yc_ref, a_s, p_s, yc_s):
        row0 = pl.program_id(0) * tm
        up = up_ref[...]
        lp = _live(CHUNK, row0 - CHUNK, T)
        a_s[0:CHUNK, :] = up[:, 0:C] * _sigmoid(up[:, C:2 * C]) * lp
        p_s[0:CHUNK, :] = up[:, 2 * C:3 * C] * lp

        def stage(c, _):
            rs = pl.multiple_of(c * CHUNK, CHUNK)
            lv = _live(CHUNK, row0 + rs, T)
            a_s[pl.ds(rs + CHUNK, CHUNK), :] = u_ref[pl.ds(rs, CHUNK), 0:C] * _sigmoid(u_ref[pl.ds(rs, CHUNK), C:2 * C]) * lv
            p_s[pl.ds(rs + CHUNK, CHUNK), :] = u_ref[pl.ds(rs, CHUNK), 2 * C:3 * C] * lv
            return 0

        lax.fori_loop(0, nch, stage, 0)

        def chunk(c, _):
            rs = pl.multiple_of(c * CHUNK, CHUNK)
            lv = _live(CHUNK, row0 + rs, T)
            for cb in range(4):
                ls = slice(cb * 128, (cb + 1) * 128)
                win = a_s[pl.ds(pl.multiple_of(rs + 32, 32), 96), ls]
                acc = jnp.broadcast_to(cb_ref[:, ls], (CHUNK, 128))
                yc_s[:, ls] = _conv_taps(win, cw_ref, ls, acc, False)
            y = yc_s[...]
            yc_ref[pl.ds(rs, CHUNK), :] = y
            xc = y - jnp.mean(y, axis=-1, keepdims=True)
            yn = xc * lax.rsqrt(jnp.mean(xc * xc, axis=-1, keepdims=True) + EPS) * lg_ref[...] + lb_ref[...]
            o_ref[pl.ds(rs, CHUNK), 0:C] = (yn * _sigmoid(yn) * lv).astype(BF16)
            for g in range(4):
                ls = slice(g * 128, (g + 1) * 128)
                win = p_s[pl.ds(pl.multiple_of(rs + 48, 16), 80), ls]
                s = _window_sum(win, g + 1, False)
                d = s[16:80] / _pool_count(row0 + rs, g) - win[16:80]
                yv = jnp.dot(d.astype(BF16), pw_ref[g].astype(BF16), preferred_element_type=F32) + pb_ref[:, ls]
                o_ref[pl.ds(rs, CHUNK), C + g * 128:C + (g + 1) * 128] = (yv * ps_ref[:, ls] * lv).astype(BF16)
            return 0

        lax.fori_loop(0, nch, chunk, 0)

    vec = pl.BlockSpec((None, 1, C), lambda i: (j, 0, 0))
    return pl.pallas_call(
        body, grid=(T // tm,),
        in_specs=[pl.BlockSpec((tm, 3 * C), lambda i: (i, 0)),
                  pl.BlockSpec((CHUNK, 3 * C), lambda i: (jnp.maximum(i * nch - 1, 0), 0)),
                  pl.BlockSpec((None, CONV_ROWS, C), lambda i: (j, 0, 0)), vec, vec, vec,
                  pl.BlockSpec((None, 4, 128, 128), lambda i: (j, 0, 0, 0)), vec, vec],
        out_specs=[pl.BlockSpec((tm, 2 * C), lambda i: (i, 0)), pl.BlockSpec((tm, C), lambda i: (i, 0))],
        out_shape=[SDS((T, 2 * C), BF16), SDS((T, C), F32)],
        scratch_shapes=[pltpu.VMEM((tm + CHUNK, C), F32), pltpu.VMEM((tm + CHUNK, C), F32), pltpu.VMEM((CHUNK, C), F32)],
        name=name, compiler_params=_params("parallel"))(u, u, cw3, cb3, lg3, lb3, pw4, pb3, ps3)


def _even_bwd(u, yc, dy, cw3, cb3, lg3, lb3, pw4, pb3, ps3, j, name):
    T = u.shape[0]
    C = C_EVEN
    tm = _tile(T, 320)
    nch = tm // CHUNK
    nblk = T // CHUNK
    ntile = T // tm

    def body(u_ref, up_ref, un_ref, yc_ref, ycn_ref, dy_ref, dyn_ref, cw_ref, cb_ref, lg_ref, lb_ref, pw_ref, pb_ref, ps_ref,
             du_ref, dcw_ref, dcb_ref, dlg_ref, dlb_ref, dpw_ref, dpb_ref, dps_ref,
             a_s, p_s, dy_s, dyc_s, dd_s, ddc_s, dw_s):
        i = pl.program_id(0)
        row0 = i * tm

        @pl.when(i == 0)
        def _():
            for ref in (dcb_ref, dlg_ref, dlb_ref, dpw_ref, dpb_ref, dps_ref, dw_s):
                ref[...] = jnp.zeros_like(ref)

        up = up_ref[...]
        lp = _live(CHUNK, row0 - CHUNK, T)
        a_s[0:CHUNK, :] = up[:, 0:C] * _sigmoid(up[:, C:2 * C]) * lp
        p_s[0:CHUNK, :] = up[:, 2 * C:3 * C] * lp
        ln_ = _live(CHUNK, row0 + tm, T)
        p_s[tm + CHUNK:tm + 2 * CHUNK, :] = un_ref[:, 2 * C:3 * C] * ln_
        dy_s[tm:tm + CHUNK, :] = dyn_ref[...] * ln_
        dyc_s[tm + CHUNK:tm + CHUNK + 32, :] = jnp.zeros((32, C), F32)

        def stage(c, _):
            rs = pl.multiple_of(c * CHUNK, CHUNK)
            lv = _live(CHUNK, row0 + rs, T)
            a_s[pl.ds(rs + CHUNK, CHUNK), :] = u_ref[pl.ds(rs, CHUNK), 0:C] * _sigmoid(u_ref[pl.ds(rs, CHUNK), C:2 * C]) * lv
            p_s[pl.ds(rs + CHUNK, CHUNK), :] = u_ref[pl.ds(rs, CHUNK), 2 * C:3 * C] * lv
            dy_s[pl.ds(rs, CHUNK), :] = dy_ref[pl.ds(rs, CHUNK), :] * lv
            return 0

        lax.fori_loop(0, nch, stage, 0)

        def first(rs, y, own):
            xc = y - jnp.mean(y, axis=-1, keepdims=True)
            rstd = lax.rsqrt(jnp.mean(xc * xc, axis=-1, keepdims=True) + EPS)
            xh = xc * rstd
            yn = xh * lg_ref[...] + lb_ref[...]
            sg = _sigmoid(yn)
            dyn = dy_s[pl.ds(rs, CHUNK), 0:C] * (sg * (1.0 + yn * (1.0 - sg)))
            dlg_ref[...] += jnp.sum(dyn * xh, axis=0, keepdims=True) * own
            dlb_ref[...] += jnp.sum(dyn, axis=0, keepdims=True) * own
            dxh = dyn * lg_ref[...]
            dyc = rstd * (dxh - jnp.mean(dxh, axis=-1, keepdims=True) - xh * jnp.mean(dxh * xh, axis=-1, keepdims=True))
            dyc_s[pl.ds(rs, CHUNK), :] = dyc
            dcb_ref[...] += jnp.sum(dyc, axis=0, keepdims=True) * own
            for g in range(4):
                ls = slice(g * 128, (g + 1) * 128)
                win = p_s[pl.ds(rs + 48, 80), ls]
                s = _window_sum(win, g + 1, False)
                cnt = _pool_count(row0 + rs, g)
                d = (s[16:80] / cnt - win[16:80]).astype(BF16)
                w = pw_ref[g].astype(BF16)
                pre = jnp.dot(d, w, preferred_element_type=F32) + pb_ref[:, ls]
                dyb = dy_s[pl.ds(rs, CHUNK), C + g * 128:C + (g + 1) * 128]
                dpre = dyb * ps_ref[:, ls]
                dps_ref[:, ls] += jnp.sum(dyb * pre, axis=0, keepdims=True) * own
                dpb_ref[:, ls] += jnp.sum(dpre, axis=0, keepdims=True) * own
                dpre_b = (dpre * own).astype(BF16)
                dpw_ref[g] += lax.dot_general(d, dpre_b, (((0,), (0,)), ((), ())), preferred_element_type=F32)
                dd = lax.dot_general(dpre.astype(BF16), w, (((1,), (1,)), ((), ())), preferred_element_type=F32)
                dd_s[pl.ds(rs, CHUNK), ls] = dd
                ddc_s[pl.ds(rs, CHUNK), ls] = dd / cnt

        def first_in_tile(c, _):
            rs = pl.multiple_of(c * CHUNK, 16 * CHUNK // 16)
            first(rs, yc_ref[pl.ds(rs, CHUNK), :], 1.0)
            return 0

        lax.fori_loop(0, nch, first_in_tile, 0)
        first(tm, ycn_ref[...], 0.0)
        ddc_s[tm + CHUNK:tm + CHUNK + 16, :] = jnp.zeros((16, C), F32)

        def second(c, _):
            rs = pl.multiple_of(c * CHUNK, CHUNK)
            lv = _live(CHUNK, row0 + rs, T)
            for cb in range(4):
                ls = slice(cb * 128, (cb + 1) * 128)
                wd = dyc_s[pl.ds(rs, 96), ls]
                da = _conv_taps(wd, cw_ref, ls, jnp.zeros((CHUNK, 128), F32), True)
                wa = a_s[pl.ds(pl.multiple_of(rs + 32, 32), 96), ls]
                dyc = dyc_s[pl.ds(rs, CHUNK), ls]
                for b in range(8):
                    rb = wa if b == 0 else pltpu.roll(wa, 96 - b, 0)
                    for a in range(5):
                        tap = 8 * a + b - 2
                        if 0 <= tap < CONV_WIDTH:
                            prod = dyc * rb[8 * a:8 * a + CHUNK]
                            part = prod[0:8]
                            for q in range(1, 8):
                                part = part + prod[8 * q:8 * q + 8]
                            dw_s[8 * tap:8 * tap + 8, ls] += part
                val = u_ref[pl.ds(rs, CHUNK), ls]
                sg = _sigmoid(u_ref[pl.ds(rs, CHUNK), C + cb * 128:C + (cb + 1) * 128])
                du_ref[pl.ds(rs, CHUNK), ls] = (da * sg * lv).astype(BF16)
                du_ref[pl.ds(rs, CHUNK), C + cb * 128:C + (cb + 1) * 128] = (da * val * sg * (1.0 - sg) * lv).astype(BF16)
            for g in range(4):
                ls = slice(g * 128, (g + 1) * 128)
                z = _window_sum(ddc_s[pl.ds(rs, 80), ls], g + 1, True)
                dpin = (z[0:CHUNK] - dd_s[pl.ds(rs, CHUNK), ls]) * lv
                du_ref[pl.ds(rs, CHUNK), 2 * C + g * 128:2 * C + (g + 1) * 128] = dpin.astype(BF16)
            return 0

        lax.fori_loop(0, nch, second, 0)

        @pl.when(i == ntile - 1)
        def _():
            for tap in range(CONV_WIDTH):
                dcw_ref[tap:tap + 1, :] = jnp.sum(dw_s[8 * tap:8 * tap + 8, :], axis=0, keepdims=True)
            dcw_ref[CONV_WIDTH:CONV_ROWS, :] = jnp.zeros((CONV_ROWS - CONV_WIDTH, C), F32)

    vec = pl.BlockSpec((None, 1, C), lambda i: (j, 0, 0))
    ovec = pl.BlockSpec((1, C), lambda i: (0, 0))
    return pl.pallas_call(
        body, grid=(ntile,),
        in_specs=[pl.BlockSpec((tm, 3 * C), lambda i: (i, 0)),
                  pl.BlockSpec((CHUNK, 3 * C), lambda i: (jnp.maximum(i * nch - 1, 0), 0)),
                  pl.BlockSpec((CHUNK, 3 * C), lambda i: (jnp.minimum((i + 1) * nch, nblk - 1), 0)),
                  pl.BlockSpec((tm, C), lambda i: (i, 0)),
                  pl.BlockSpec((CHUNK, C), lambda i: (jnp.minimum((i + 1) * nch, nblk - 1), 0)),
                  pl.BlockSpec((tm, 2 * C), lambda i: (i, 0)),
                  pl.BlockSpec((CHUNK, 2 * C), lambda i: (jnp.minimum((i + 1) * nch, nblk - 1), 0)),
                  pl.BlockSpec((None, CONV_ROWS, C), lambda i: (j, 0, 0)), vec, vec, vec,
                  pl.BlockSpec((None, 4, 128, 128), lambda i: (j, 0, 0, 0)), vec, vec],
        out_specs=[pl.BlockSpec((tm, 3 * C), lambda i: (i, 0)), pl.BlockSpec((CONV_ROWS, C), lambda i: (0, 0)),
                   ovec, ovec, ovec, pl.BlockSpec((4, 128, 128), lambda i: (0, 0, 0)), ovec, ovec],
        out_shape=[SDS((T, 3 * C), BF16), SDS((CONV_ROWS, C), F32), SDS((1, C), F32), SDS((1, C), F32), SDS((1, C), F32),
                   SDS((4, 128, 128), F32), SDS((1, C), F32), SDS((1, C), F32)],
        scratch_shapes=[pltpu.VMEM((tm + CHUNK, C), F32), pltpu.VMEM((tm + 2 * CHUNK, C), F32),
                        pltpu.VMEM((tm + CHUNK, 2 * C), F32),
                        pltpu.VMEM((tm + CHUNK + 32, C), F32), pltpu.VMEM((tm + CHUNK, C), F32),
                        pltpu.VMEM((tm + CHUNK + 16, C), F32), pltpu.VMEM((8 * CONV_ROWS, C), F32)],
        name=name, compiler_params=_params("arbitrary"))(u, u, u, yc, yc, dy, dy, cw3, cb3, lg3, lb3, pw4, pb3, ps3)


HI = lax.Precision.HIGHEST


def _dot_nt(a, b):
    return lax.dot_general(a, b, (((1,), (1,)), ((), ())), preferred_element_type=F32)


def _dot_tn(a, b):
    return lax.dot_general(a, b, (((0,), (0,)), ((), ())), preferred_element_type=F32)


def _tri(lower):
    r = lax.broadcasted_iota(jnp.int32, (CHUNK, CHUNK), 0)
    c = lax.broadcasted_iota(jnp.int32, (CHUNK, CHUNK), 1)
    return jnp.where((c <= r) if lower else (c >= r), 1.0, 0.0).astype(F32)


def _hgrn_gates(u_ref, lb_ref, h, D, lv):
    ls = slice(h * HEAD_DIM, (h + 1) * HEAD_DIM)
    qraw = u_ref[:, ls]
    fraw = u_ref[:, D + h * HEAD_DIM:D + (h + 1) * HEAD_DIM]
    v = u_ref[:, 2 * D + h * HEAD_DIM:2 * D + (h + 1) * HEAD_DIM] * lv
    lbv = lb_ref[:, ls]
    sig = _sigmoid(fraw)
    forget = lbv + (1.0 - lbv) * sig
    logf = jnp.log(forget) * lv
    k = (1.0 - forget) * lv
    qsig = _sigmoid(qraw)
    q = qraw * qsig * lv
    return q, k, v, logf, (qraw, qsig, sig, forget, lbv)


def _sub_parts(q, k, b, b_s, I):
    rows = slice(SUB * I, SUB * (I + 1))
    rho = jnp.zeros((1, HEAD_DIM), F32) if I == 0 else b_s[SUB * I - 1:SUB * I, :]
    eI = jnp.exp(b[rows] - rho)
    EI = jnp.exp(jnp.minimum(rho - b, EXP_CAP))
    causal = (lax.broadcasted_iota(jnp.int32, (SUB, CHUNK), 1)
              <= lax.broadcasted_iota(jnp.int32, (SUB, CHUNK), 0) + SUB * I)
    return rows, q[rows] * eI, k * EI, eI, EI, causal


def _chunks_per_step(NC):
    for n in (5, 4, 3, 2):
        if NC % n == 0:
            return n
    return 1


def _hgrn_fwd(u, lb3, layer, gn3, j, name):
    T = u.shape[0]
    D = u.shape[1] // 4
    H = D // HEAD_DIM
    NC = T // CHUNK
    CH = _chunks_per_step(NC)
    R = CH * CHUNK

    def body(u_ref, lb_ref, gn_ref, y_ref, o_ref, sall_ref, st_s, b_s, lf_s, q_s, k_s):
        n = pl.program_id(0)

        @pl.when(n == 0)
        def _():
            st_s[...] = jnp.zeros_like(st_s)

        heads = range(H)
        cols = [slice(h * HEAD_DIM, (h + 1) * HEAD_DIM) for h in heads]
        rows = [slice(c * CHUNK, (c + 1) * CHUNK) for c in range(CH)]
        vb = {}
        for c in range(CH):
            lv = _live(CHUNK, (n * CH + c) * CHUNK, T)
            for h in heads:
                q, k, v, logf, _ = _hgrn_gates(u_ref.at[rows[c]], lb_ref, h, D, lv)
                q_s[rows[c], cols[h]] = q
                k_s[rows[c], cols[h]] = k
                lf_s[rows[c], cols[h]] = logf
                vb[c, h] = v.astype(BF16)
        for c in range(CH):
            b_s[rows[c], :] = jnp.dot(_tri(True), lf_s[rows[c], :], precision=HI, preferred_element_type=F32)
        ops = {}
        for c in range(CH):
            for h in heads:
                b_h = b_s.at[rows[c], cols[h]]
                b = b_h[...]
                q = q_s[rows[c], cols[h]]
                k = k_s[rows[c], cols[h]]
                blast = b_h[CHUNK - 1:CHUNK, :]
                qh = (q * jnp.exp(b)).astype(BF16)
                kt = (k * jnp.exp(blast - b)).astype(BF16)
                subs = []
                for I in range(CHUNK // SUB):
                    _, qI, KI, _, _, causal = _sub_parts(q, k, b, b_h, I)
                    subs.append((qI.astype(BF16), KI.astype(BF16), causal))
                ops[c, h] = (qh, kt, jnp.exp(blast), subs)
        mm = {}
        for h in heads:
            st = st_s[h]
            for c in range(CH):
                qh, kt, eblast, subs = ops[c, h]
                sall_ref[c, h] = st
                o_inter = _dot_nt(qh, st.astype(BF16))
                st = st * eblast + _dot_tn(vb[c, h], kt)
                mm[c, h] = (o_inter, [_dot_nt(qI, KI) for qI, KI, _ in subs])
            st_s[h] = st
        for c in range(CH):
            for h in heads:
                o_inter, ps = mm[c, h]
                p = jnp.concatenate([jnp.where(m, x, 0.0) for x, (_, _, m) in zip(ps, ops[c, h][3])], axis=0).astype(BF16)
                o = o_inter + jnp.dot(p, vb[c, h], preferred_element_type=F32)
                o_ref[rows[c], cols[h]] = o
                graw = u_ref[rows[c], 3 * D + h * HEAD_DIM:3 * D + (h + 1) * HEAD_DIM]
                r = lax.rsqrt(jnp.mean(o * o, axis=-1, keepdims=True) + EPS)
                y_ref[rows[c], cols[h]] = (((o * r) * gn_ref[...]) * (graw * _sigmoid(graw))).astype(BF16)

    return pl.pallas_call(
        body, grid=(NC // CH,),
        in_specs=[pl.BlockSpec((R, 4 * D), lambda n: (n, 0)),
                  pl.BlockSpec((None, 1, D), lambda n: (layer, 0, 0)),
                  pl.BlockSpec((None, 1, HEAD_DIM), lambda n: (j, 0, 0))],
        out_specs=[pl.BlockSpec((R, D), lambda n: (n, 0)), pl.BlockSpec((R, D), lambda n: (n, 0)),
                   pl.BlockSpec((CH, H, HEAD_DIM, HEAD_DIM), lambda n: (n, 0, 0, 0))],
        out_shape=[SDS((T, D), BF16), SDS((T, D), F32), SDS((NC, H, HEAD_DIM, HEAD_DIM), F32)],
        scratch_shapes=[pltpu.VMEM((H, HEAD_DIM, HEAD_DIM), F32)] + [pltpu.VMEM((R, D), F32)] * 4,
        name=name, compiler_params=_params("arbitrary"))(u, lb3, gn3)


def _hgrn_bwd(u, o_raw, dy, sall, lb3, layer, gn3, j, name):
    T = u.shape[0]
    D = u.shape[1] // 4
    H = D // HEAD_DIM
    NC = T // CHUNK
    CH = _chunks_per_step(NC)
    R = CH * CHUNK
    NS = NC // CH

    def body(u_ref, o_ref, dy_ref, sall_ref, lb_ref, gn_ref, du_ref, dlb_ref, dgn_ref, dst_s, b_s, lf_s, q_s, k_s, db_s, dk_s):
        step = pl.program_id(0)
        n = NS - 1 - step

        @pl.when(step == 0)
        def _():
            dst_s[...] = jnp.zeros_like(dst_s)
            dlb_ref[...] = jnp.zeros_like(dlb_ref)
            dgn_ref[...] = jnp.zeros_like(dgn_ref)

        last_row = (_row_ids((CHUNK, 1), 0) == CHUNK - 1).astype(F32)
        gn = gn_ref[...]
        heads = range(H)
        chunks = range(CH)
        cols = [slice(h * HEAD_DIM, (h + 1) * HEAD_DIM) for h in heads]
        rows = [slice(c * CHUNK, (c + 1) * CHUNK) for c in chunks]
        lv = [_live(CHUNK, (n * CH + c) * CHUNK, T) for c in chunks]
        vb, dob = {}, {}
        dgn = jnp.zeros((1, HEAD_DIM), F32)
        for c in chunks:
            for h in heads:
                q, k, v, logf, _ = _hgrn_gates(u_ref.at[rows[c]], lb_ref, h, D, lv[c])
                q_s[rows[c], cols[h]] = q
                k_s[rows[c], cols[h]] = k
                lf_s[rows[c], cols[h]] = logf
                vb[c, h] = v.astype(BF16)
                graw = u_ref[rows[c], 3 * D + h * HEAD_DIM:3 * D + (h + 1) * HEAD_DIM]
                gsig = _sigmoid(graw)
                o = o_ref[rows[c], cols[h]]
                r = lax.rsqrt(jnp.mean(o * o, axis=-1, keepdims=True) + EPS)
                xh = o * r
                dyv = dy_ref[rows[c], cols[h]]
                dsg = dyv * (graw * gsig)
                dgn = dgn + jnp.sum(dsg * xh, axis=0, keepdims=True)
                dxh = dsg * gn
                do = r * (dxh - xh * jnp.mean(dxh * xh, axis=-1, keepdims=True))
                dob[c, h] = do.astype(BF16)
                dgraw = dyv * xh * gn * (gsig * (1.0 + graw * (1.0 - gsig)))
                du_ref[rows[c], 3 * D + h * HEAD_DIM:3 * D + (h + 1) * HEAD_DIM] = (dgraw * lv[c]).astype(BF16)
        dgn_ref[...] += dgn
        for c in chunks:
            b_s[rows[c], :] = jnp.dot(_tri(True), lf_s[rows[c], :], precision=HI, preferred_element_type=F32)
        ops = {}
        for c in chunks:
            for h in heads:
                b_h = b_s.at[rows[c], cols[h]]
                b = b_h[...]
                q = q_s[rows[c], cols[h]]
                k = k_s[rows[c], cols[h]]
                blast = b_h[CHUNK - 1:CHUNK, :]
                eb = jnp.exp(b)
                ekb = jnp.exp(blast - b)
                subs = []
                for I in range(CHUNK // SUB):
                    rws, qI, KI, eI, EI, causal = _sub_parts(q, k, b, b_h, I)
                    subs.append((rws, qI.astype(BF16), KI.astype(BF16), eI, EI, causal))
                ops[c, h] = (eb, ekb, jnp.exp(blast), (q * eb).astype(BF16), (k * ekb).astype(BF16), subs)
        mm = {}
        for h in heads:
            dst = dst_s[h]
            for c in reversed(chunks):
                eb, ekb, eblast, qhb, ktb, subs = ops[c, h]
                st = sall_ref[c, h]
                dstb = dst.astype(BF16)
                dv = _dot_nt(ktb, dstb)
                dqh = jnp.dot(dob[c, h], st.astype(BF16), preferred_element_type=F32)
                dkt = jnp.dot(vb[c, h], dstb, preferred_element_type=F32)
                dblast = jnp.sum(dst * st, axis=0, keepdims=True) * eblast
                dst = dst * eblast + _dot_tn(dob[c, h], qhb)
                dp_full = _dot_nt(dob[c, h], vb[c, h])
                ps = [_dot_nt(qIb, KIb) for _, qIb, KIb, _, _, _ in subs]
                mm[c, h] = (dv, dqh, dkt, dblast, dp_full, ps)
            dst_s[h] = dst
        for c in chunks:
            for h in heads:
                eb, ekb, eblast, qhb, ktb, subs = ops[c, h]
                dv, dqh, dkt, dblast, dp_full, ps = mm[c, h]
                p = jnp.concatenate([jnp.where(sub[5], x, 0.0) for x, sub in zip(ps, subs)], axis=0).astype(BF16)
                dv = dv + _dot_tn(p, dob[c, h])
                du_ref[rows[c], 2 * D + h * HEAD_DIM:2 * D + (h + 1) * HEAD_DIM] = (dv * lv[c]).astype(BF16)
                dq = dqh * eb
                db = dqh * qhb.astype(F32)
                tmp = dkt * ktb.astype(F32)
                dk = dkt * ekb
                db = db - tmp
                dblast = dblast + jnp.sum(tmp, axis=0, keepdims=True)
                dq_parts, db_parts = [], []
                for rws, qIb, KIb, eI, EI, causal in subs:
                    dp = jnp.where(causal, dp_full[rws], 0.0).astype(BF16)
                    dqI = jnp.dot(dp, KIb, preferred_element_type=F32)
                    dKI = _dot_tn(dp, qIb)
                    dq_parts.append(dqI * eI)
                    db_parts.append(dqI * qIb.astype(F32))
                    dk = dk + dKI * EI
                    db = db - dKI * KIb.astype(F32)
                dq = dq + jnp.concatenate(dq_parts, axis=0)
                db_s[rows[c], cols[h]] = db + jnp.concatenate(db_parts, axis=0) + last_row * dblast
                dk_s[rows[c], cols[h]] = dk
                qraw = u_ref[rows[c], cols[h]]
                qsig = _sigmoid(qraw)
                du_ref[rows[c], cols[h]] = (dq * (qsig * (1.0 + qraw * (1.0 - qsig))) * lv[c]).astype(BF16)
        for c in chunks:
            lf_s[rows[c], :] = jnp.dot(_tri(False), db_s[rows[c], :], precision=HI, preferred_element_type=F32)
        for h in heads:
            lbv = lb_ref[:, cols[h]]
            dlb = jnp.zeros((1, HEAD_DIM), F32)
            for c in chunks:
                fraw = u_ref[rows[c], D + h * HEAD_DIM:D + (h + 1) * HEAD_DIM]
                sig = _sigmoid(fraw)
                forget = lbv + (1.0 - lbv) * sig
                dforget = (lf_s[rows[c], cols[h]] / forget - dk_s[rows[c], cols[h]]) * lv[c]
                dlb = dlb + jnp.sum(dforget * (1.0 - sig), axis=0, keepdims=True)
                du_ref[rows[c], D + h * HEAD_DIM:D + (h + 1) * HEAD_DIM] = (dforget * (1.0 - lbv) * sig * (1.0 - sig)).astype(BF16)
            dlb_ref[:, cols[h]] += dlb

    rev = lambda s: (NS - 1 - s, 0)
    return pl.pallas_call(
        body, grid=(NS,),
        in_specs=[pl.BlockSpec((R, 4 * D), rev), pl.BlockSpec((R, D), rev), pl.BlockSpec((R, D), rev),
                  pl.BlockSpec((CH, H, HEAD_DIM, HEAD_DIM), lambda s: (NS - 1 - s, 0, 0, 0)),
                  pl.BlockSpec((None, 1, D), lambda s: (layer, 0, 0)),
                  pl.BlockSpec((None, 1, HEAD_DIM), lambda s: (j, 0, 0))],
        out_specs=[pl.BlockSpec((R, 4 * D), rev), pl.BlockSpec((1, D), lambda s: (0, 0)),
                   pl.BlockSpec((1, HEAD_DIM), lambda s: (0, 0))],
        out_shape=[SDS((T, 4 * D), BF16), SDS((1, D), F32), SDS((1, HEAD_DIM), F32)],
        scratch_shapes=[pltpu.VMEM((H, HEAD_DIM, HEAD_DIM), F32)] + [pltpu.VMEM((R, D), F32)] * 6,
        name=name, compiler_params=_params("arbitrary"))(u, o_raw, dy, sall, lb3, gn3)


def _softmax_layers(p_ref, n_layers):
    rows = [p_ref[l:l + 1, :] for l in range(n_layers)]
    m = functools.reduce(jnp.maximum, rows)
    e = [jnp.exp(x - m) for x in rows]
    tot = functools.reduce(lambda a, b: a + b, e)
    return [x / tot for x in e]


def _lb_fwd(p):
    n_layers, D = p.shape

    def body(p_ref, o_ref):
        s = _softmax_layers(p_ref, n_layers)
        acc = jnp.zeros((1, D), F32)
        o_ref[0:1, :] = acc
        for l in range(1, n_layers):
            acc = acc + s[l]
            o_ref[l:l + 1, :] = acc

    return pl.pallas_call(body, out_shape=SDS(p.shape, F32), name="lb_fwd")(p)


def _lb_bwd(p, dlb):
    n_layers, D = p.shape

    def body(p_ref, d_ref, o_ref):
        s = _softmax_layers(p_ref, n_layers)
        ds = [jnp.zeros((1, D), F32)] * n_layers
        acc = jnp.zeros((1, D), F32)
        for l in range(n_layers - 1, 0, -1):
            acc = acc + d_ref[l:l + 1, :]
            ds[l] = acc
        dot = functools.reduce(lambda a, b: a + b, [s[l] * ds[l] for l in range(n_layers)])
        for l in range(n_layers):
            o_ref[l:l + 1, :] = s[l] * (ds[l] - dot)

    return pl.pallas_call(body, out_shape=SDS(p.shape, F32), name="lb_bwd")(p, dlb)


def _adamw(w, g, m, v, name):
    R, C = w.shape
    tr = _tile(R, 256, 8) if R % 8 == 0 else R

    def body(w_ref, g_ref, m_ref, v_ref, d_ref, mo_ref, vo_ref):
        g_ = g_ref[...]
        m_ = ADAM_B1 * m_ref[...] + (1.0 - ADAM_B1) * g_
        v_ = ADAM_B2 * v_ref[...] + (1.0 - ADAM_B2) * (g_ * g_)
        mh = m_ / (1.0 - ADAM_B1 ** ADAM_STEP)
        vh = v_ / (1.0 - ADAM_B2 ** ADAM_STEP)
        d_ref[...] = -ADAM_LR * (mh / (jnp.sqrt(vh) + ADAM_EPS) + ADAM_WD * w_ref[...])
        mo_ref[...] = m_
        vo_ref[...] = v_

    blk = pl.BlockSpec((tr, C), lambda i: (i, 0))
    return pl.pallas_call(
        body, grid=(R // tr,), in_specs=[blk] * 4, out_specs=[blk] * 3, out_shape=[SDS((R, C), F32)] * 3,
        name=name, compiler_params=_params("parallel"))(w, g, m, v)


def _adamw_layer(w3, m3, v3, g2, layer, outs, name):
    L, R, C = w3.shape
    tr = _tile(R, 256, 8)
    if outs is None:
        outs = tuple(lax.empty(w3.shape, F32) for _ in range(4))

    def body(w_ref, m_ref, v_ref, g_ref, a0, a1, a2, a3, go_ref, d_ref, mo_ref, vo_ref):
        del a0, a1, a2, a3
        g_ = g_ref[...]
        m_ = ADAM_B1 * m_ref[...] + (1.0 - ADAM_B1) * g_
        v_ = ADAM_B2 * v_ref[...] + (1.0 - ADAM_B2) * (g_ * g_)
        mh = m_ / (1.0 - ADAM_B1 ** ADAM_STEP)
        vh = v_ / (1.0 - ADAM_B2 ** ADAM_STEP)
        go_ref[...] = g_
        d_ref[...] = -ADAM_LR * (mh / (jnp.sqrt(vh) + ADAM_EPS) + ADAM_WD * w_ref[...])
        mo_ref[...] = m_
        vo_ref[...] = v_

    lay = pl.BlockSpec((None, tr, C), lambda i: (layer, i, 0))
    return pl.pallas_call(
        body, grid=(R // tr,), in_specs=[lay] * 3 + [pl.BlockSpec((tr, C), lambda i: (i, 0))] + [ANY_SPEC] * 4,
        out_specs=[lay] * 4, out_shape=[SDS(w3.shape, F32)] * 4, input_output_aliases={4: 0, 5: 1, 6: 2, 7: 3},
        name=name, compiler_params=_params("parallel"))(w3, m3, v3, g2, *outs)


SEM_SPEC = pl.BlockSpec(memory_space=pltpu.SEMAPHORE)
HBM_SPEC = pl.BlockSpec(memory_space=pltpu.HBM)
EFFECT = pltpu.SideEffectType.DATAFLOW_SIDE_EFFECTING
N_DEV = 2 * N_CHIPS


def _position():
    x, y, c = lax.axis_index("x"), lax.axis_index("y"), lax.axis_index("c")
    chips = [(1 - x, y), (x, 1 - y), (1 - x, 1 - y)]
    return x, y, c, chips


def _split_start(name, plan, bufs, n_sems, deps=(), earlier=None):
    n = len(bufs)
    held = () if earlier is None else tuple(earlier[1:])

    def body(*refs):
        first_out = n + len(held) + len(deps)
        if earlier is not None:
            sends, recvs = earlier[0](refs[:n], refs[n], refs[n + 1])
            for kw in sends:
                pltpu.make_async_remote_copy(**kw).wait_send()
            for kw in recvs:
                pltpu.make_async_remote_copy(**kw).wait_recv()
        sends, _ = plan(refs[:n], refs[first_out], refs[first_out + 1])
        for kw in sends:
            pltpu.make_async_remote_copy(**kw).start()
        refs[-1][...] = jnp.zeros_like(refs[-1])

    out = pl.pallas_call(
        body, name=name,
        out_shape=(pltpu.SemaphoreType.DMA((n_sems,)), pltpu.SemaphoreType.DMA((n_sems,)),
                   *[pltpu.HBM(b.shape, b.dtype) for b in bufs], SDS((8, 128), F32)),
        in_specs=[HBM_SPEC] * n + [SEM_SPEC] * len(held) + [ANY_SPEC] * len(deps),
        out_specs=(SEM_SPEC, SEM_SPEC, *[HBM_SPEC] * n, pl.BlockSpec(memory_space=pltpu.VMEM)),
        input_output_aliases={i: 2 + i for i in range(n)},
        compiler_params=pltpu.CompilerParams(has_side_effects=EFFECT),
    )(*[pltpu.with_memory_space_constraint(b, pltpu.HBM) for b in bufs], *held, *deps)
    return out[0], out[1], list(out[2:2 + n]), out[-1]


def _split_wait(name, plan, send_sems, recv_sems, bufs, after=()):
    n = len(bufs)

    def body(*refs):
        sends, recvs = plan(refs[:n], refs[n], refs[n + 1])
        for kw in sends:
            pltpu.make_async_remote_copy(**kw).wait_send()
        for kw in recvs:
            pltpu.make_async_remote_copy(**kw).wait_recv()

    out = pl.pallas_call(
        body, name=name, out_shape=tuple(pltpu.HBM(b.shape, b.dtype) for b in bufs),
        in_specs=[HBM_SPEC] * n + [SEM_SPEC, SEM_SPEC] + [ANY_SPEC] * len(after),
        out_specs=tuple([HBM_SPEC] * n), input_output_aliases={i: i for i in range(n)},
        compiler_params=pltpu.CompilerParams(has_side_effects=EFFECT),
    )(*bufs, send_sems, recv_sems, *after)
    return list(out)


def _region(kind, ref, chip, half):
    K, N = ref.shape
    if kind == "col":
        return ref.at[pl.ds(half * (K // 2), K // 2), pl.ds(chip * (N // N_CHIPS), N // N_CHIPS)]
    rows = K // (2 * N_CHIPS)
    return ref.at[pl.ds((2 * chip + half) * rows, rows), :]


def _gather_plan(kinds, over_chips):
    def plan(refs, send_sems, recv_sems):
        x, y, c, chips = _position()
        sends, recvs = [], []
        for f, (ref, kind) in enumerate(zip(refs, kinds)):
            for k, chip in enumerate(chips):
                theirs = 2 * chip[0] + chip[1]
                sem = dict(send_sem=send_sems.at[3 * f + k], recv_sem=recv_sems.at[3 * f + k], device_id_type=MESH)
                if over_chips:
                    out, back, to = _region(kind, ref, 2 * x + y, c), _region(kind, ref, theirs, c), (*chip, c)
                else:
                    out, back, to = _region(kind, ref, theirs, c), _region(kind, ref, theirs, 1 - c), (x, y, 1 - c)
                sends.append(dict(src_ref=out, dst_ref=out, device_id=to, **sem))
                recvs.append(dict(src_ref=back, dst_ref=back, device_id=to, **sem))
        return sends, recvs
    return plan


def _reduce_plan(refs, send_sems, recv_sems):
    x, y, c, _ = _position()
    me = 4 * x + 2 * y + c
    sends, recvs = [], []
    for f in range(len(refs) // 2):
        acc, land = refs[2 * f], refs[2 * f + 1]
        for d in range(1, N_DEV):
            t = (me + d) % N_DEV
            to = dict(device_id=(t // 4, (t // 2) % 2, t % 2), device_id_type=MESH)
            slot = N_DEV - 1 - d
            sends.append(dict(src_ref=acc.at[t % 2, t // 2], dst_ref=land.at[slot], send_sem=send_sems.at[7 * f + d - 1],
                              recv_sem=recv_sems.at[7 * f + slot], **to))
            recvs.append(dict(src_ref=land.at[d - 1], dst_ref=land.at[d - 1], send_sem=send_sems.at[7 * f + d - 1],
                              recv_sem=recv_sems.at[7 * f + d - 1], **to))
    return sends, recvs


def _swap_plan(refs, send_sems, recv_sems):
    x, y, c, _ = _position()
    sends, recvs = [], []
    for f, g in enumerate(refs):
        sem = dict(send_sem=send_sems.at[f], recv_sem=recv_sems.at[f], device_id=(x, y, 1 - c), device_id_type=MESH)
        sends.append(dict(src_ref=g.at[c], dst_ref=g.at[c], **sem))
        recvs.append(dict(src_ref=g.at[1 - c], dst_ref=g.at[1 - c], **sem))
    return sends, recvs


def _sum_pieces(ids2, acc, land, name):
    _, _, nr, nc = acc.shape
    tr = _tile(nr, 256, 16)

    def body(ids_ref, own_ref, land_ref, o_ref):
        del ids_ref
        s = own_ref[...].astype(F32)
        for k in range(N_DEV - 1):
            s = s + land_ref[k].astype(F32)
        o_ref[...] = s

    return pl.pallas_call(
        body,
        grid_spec=pltpu.PrefetchScalarGridSpec(
            num_scalar_prefetch=1, grid=(nr // tr,),
            in_specs=[pl.BlockSpec((None, None, tr, nc), lambda i, ids: (ids[0], ids[1], i, 0)),
                      pl.BlockSpec((N_DEV - 1, tr, nc), lambda i, ids: (0, i, 0))],
            out_specs=pl.BlockSpec((None, tr, nc), lambda i, ids: (ids[0], i, 0))),
        out_shape=SDS((2, nr, nc), F32), name=name, compiler_params=_params("parallel"))(ids2, acc, land)


def _small_plan(refs, send_sems, recv_sems):
    x, y, c, _ = _position()
    me = 4 * x + 2 * y + c
    own, land = refs
    sends, recvs = [], []
    for d in range(1, N_DEV):
        t = (me + d) % N_DEV
        to = dict(device_id=(t // 4, (t // 2) % 2, t % 2), device_id_type=MESH)
        sends.append(dict(src_ref=own, dst_ref=land.at[me], send_sem=send_sems.at[d - 1],
                          recv_sem=recv_sems.at[N_DEV - 1 - d], **to))
        recvs.append(dict(src_ref=land.at[t], dst_ref=land.at[t], send_sem=send_sems.at[d - 1],
                          recv_sem=recv_sems.at[d - 1], **to))
    return sends, recvs


def _sum_blocks(me1, own, land):
    def body(me_ref, own_ref, land_ref, o_ref):
        acc = None
        for d in range(N_DEV):
            term = jnp.where(me_ref[0] == d, own_ref[...], land_ref[d])
            acc = term if acc is None else acc + term
        o_ref[...] = acc

    return pl.pallas_call(
        body,
        grid_spec=pltpu.PrefetchScalarGridSpec(
            num_scalar_prefetch=1, grid=(1,),
            in_specs=[pl.BlockSpec(own.shape, lambda i, me: (0, 0)), pl.BlockSpec(land.shape, lambda i, me: (0, 0, 0))],
            out_specs=pl.BlockSpec(own.shape, lambda i, me: (0, 0))),
        out_shape=SDS(own.shape, F32), name="sum_small", compiler_params=_params("arbitrary"))(me1, own, land)


BIG = {"ev_w_in": "col", "ev_w_out": "row", "od_w_in": "col", "od_w_out": "row", "mlp_w1": "col", "mlp_w2": "row"}
WEIGHTS = ("meta_tokens", "mix_norm_g", "mlp_norm_g", "final_norm_g", "ev_w_in", "ev_conv_w", "ev_conv_b", "ev_ln_g",
           "ev_ln_b", "ev_pool_w", "ev_pool_b", "ev_pool_scale", "ev_w_out", "od_w_in", "od_gnorm_g", "od_w_out",
           "lb_param", "mlp_w1", "mlp_w2")
PACK_UNIT = 1024


def _mixer_names(layer):
    return ("ev_w_in", "ev_w_out") if layer % 2 == 0 else ("od_w_in", "od_w_out")


def _pack(arrays):
    flat = []
    for a in arrays:
        a = a.reshape(-1)
        flat.append(jnp.pad(a, (0, (-a.shape[0]) % PACK_UNIT)))
    return jnp.concatenate(flat).reshape(-1, 128)


def _unpack(packed, shapes):
    flat = packed.reshape(-1)
    out, off = [], 0
    for s in shapes:
        size = 1
        for d in s:
            size *= d
        out.append(flat[off:off + size].reshape(s))
        off += size + (-size) % PACK_UNIT
    return out


def _local_step(x2, target, P, weights, boundary, first_deps=()):
    D = x2.shape[1]
    n_layers = P["mix_norm_g"].shape[0]
    h = jnp.concatenate([jnp.zeros((PAD, D), F32), P["meta_full"], x2], axis=0)
    mix_g = P["mix_norm_g"].reshape(n_layers, 1, D)
    mlp_g = P["mlp_norm_g"].reshape(n_layers, 1, D)
    vec = lambda a: a.reshape(a.shape[0], 1, -1)
    cb3, lg3, lnb3, ps3 = vec(P["ev_conv_b"]), vec(P["ev_ln_g"]), vec(P["ev_ln_b"]), vec(P["ev_pool_scale"])
    pb3 = vec(P["ev_pool_b"])
    gn3 = vec(P["od_gnorm_g"])
    lb_all = _lb_fwd(P["lb_param"])
    lb3 = lb_all.reshape(n_layers, 1, D)
    even = (cb3, lg3, lnb3, P["ev_pool_w"], pb3, ps3)

    saved = []
    deps = tuple(first_deps)
    for layer in range(n_layers):
        j = layer // 2
        w_in, w_out = _mixer_names(layer)
        W = {}
        s = {"h": h, "W": W}
        s["n"] = _rms_fwd(h, mix_g, layer, "mix_norm_0", deps=deps) if layer == 0 else n_next
        deps = ()
        W[w_in], held = weights(layer, w_in, (s["n"],))
        s["u"] = _mm_nn(s["n"], W[w_in], 0, f"mix_in_{layer}", deps=held)
        if layer % 2 == 0:
            s["y"], s["yc"] = _even_fwd(s["u"], P["conv_w_full"], *even, j, f"even_fwd_{layer}")
        else:
            s["y"], s["o"], s["sall"] = _hgrn_fwd(s["u"], lb3, layer, gn3, j, f"hgrn_fwd_{layer}")
        W[w_out], held = weights(layer, w_out, (s["y"],))
        h, s["n2"] = _mm_nn_norm(s["y"], W[w_out], 0, h, mlp_g, layer, f"mix_out_{layer}", deps=held)
        s["h1"] = h
        W["mlp_w1"], held = weights(layer, "mlp_w1", (s["n2"],))
        if layer == 0:
            s["relu"] = _mm_nn(s["n2"], W["mlp_w1"], 0, "mlp_up_0", relu=True, deps=held)
            W["mlp_w2"], held = weights(layer, "mlp_w2", (s["relu"],))
            h, n_next = _mm_nn_norm(s["relu"], W["mlp_w2"], 0, h, mix_g, 1, "mlp_down_0", square=True, deps=held)
        else:
            W["mlp_w2"], more = weights(layer, "mlp_w2", (s["n2"],))
            last = layer + 1 == n_layers
            out = _mlp_fwd(s["n2"], W["mlp_w1"], W["mlp_w2"], h, None if last else mix_g, layer + 1, f"mlp_{layer}", deps=held + more)
            h, s["relu"] = out[0], out[-1]
            n_next = None if last else out[1]
        saved.append(s)

    dh, dhb, dg_final, loss = _final(h, P["final_norm_g"].reshape(1, D), target)

    small = {"final_norm_g": dg_final}
    per_layer = {k: [None] * n_layers for k in ("mix_norm_g", "mlp_norm_g", "lb")}
    per_pair = {k: [None] * (n_layers // 2) for k in
                ("ev_conv_w", "ev_conv_b", "ev_ln_g", "ev_ln_b", "ev_pool_w", "ev_pool_b", "ev_pool_scale", "od_gnorm_g")}
    for layer in reversed(range(n_layers)):
        j = layer // 2
        s = saved[layer]
        W = s["W"]
        w_in, w_out = _mixer_names(layer)
        dw2 = _mm_tn(s["relu"], dhb, "row", f"dw2_{layer}", square=True)
        dz, dh, dhb, per_layer["mlp_norm_g"][layer] = _mlp_bwd(
            dhb, s["relu"], W["mlp_w1"], W["mlp_w2"], s["h1"], mlp_g, layer, dh, f"mlp_bwd_{layer}", deps=deps + (dw2,))
        dw1 = _mm_tn(s["n2"], dz, "col", f"dw1_{layer}")
        deps = boundary(f"mlp{layer}", {("mlp_w1", layer): dw1, ("mlp_w2", layer): dw2}, (dhb, dw1, dw2))
        dy = _mm_nt(dhb, W[w_out], 0, f"d_y_{layer}", deps=deps)
        dwout = _mm_tn(s["y"], dhb, "row", f"dwout_{layer}")
        if layer % 2 == 0:
            du, dcw, dcb, dlg, dlnb, dpw, dpb, dps = _even_bwd(s["u"], s["yc"], dy, P["conv_w_full"], *even, j, f"even_bwd_{layer}")
            for k, val in (("ev_conv_w", dcw), ("ev_conv_b", dcb), ("ev_ln_g", dlg), ("ev_ln_b", dlnb),
                           ("ev_pool_w", dpw), ("ev_pool_b", dpb), ("ev_pool_scale", dps)):
                per_pair[k][j] = val
        else:
            du, per_layer["lb"][layer], per_pair["od_gnorm_g"][j] = _hgrn_bwd(
                s["u"], s["o"], dy, s["sall"], lb3, layer, gn3, j, f"hgrn_bwd_{layer}")
        dwin = _mm_tn(s["n"], du, "col", f"dwin_{layer}")
        deps = boundary(f"mix{layer}", {(w_in, j): dwin, (w_out, j): dwout}, (du, dwin, dwout))
        dh, dhb, per_layer["mix_norm_g"][layer] = _mm_nt_norm(du, W[w_in], 0, s["h"], mix_g, layer, dh, f"d_n_{layer}", deps=deps)
        deps = ()

    small["mix_norm_g"] = jnp.concatenate(per_layer["mix_norm_g"], axis=0)
    small["mlp_norm_g"] = jnp.concatenate(per_layer["mlp_norm_g"], axis=0)
    dlb_all = jnp.concatenate([jnp.zeros((1, D), F32) if g is None else g for g in per_layer["lb"]], axis=0)
    small["lb_param"] = _lb_bwd(P["lb_param"], dlb_all)
    for k, vals in per_pair.items():
        small[k] = jnp.stack(vals, axis=0)
    small["meta_tokens"] = dh[PAD:LEAD]
    return loss, dh, small


def kernel(x, meta_tokens, mix_norm_g, mlp_norm_g, final_norm_g, ev_w_in, ev_conv_w, ev_conv_b, ev_ln_g, ev_ln_b, ev_pool_w, ev_pool_b, ev_pool_scale, ev_w_out, od_w_in, od_gnorm_g, od_w_out, lb_param, mlp_w1, mlp_w2, loss_target, m_meta_tokens, m_mix_norm_g, m_mlp_norm_g, m_final_norm_g, m_ev_w_in, m_ev_conv_w, m_ev_conv_b, m_ev_ln_g, m_ev_ln_b, m_ev_pool_w, m_ev_pool_b, m_ev_pool_scale, m_ev_w_out, m_od_w_in, m_od_gnorm_g, m_od_w_out, m_lb_param, m_mlp_w1, m_mlp_w2, v_meta_tokens, v_mix_norm_g, v_mlp_norm_g, v_final_norm_g, v_ev_w_in, v_ev_conv_w, v_ev_conv_b, v_ev_ln_g, v_ev_ln_b, v_ev_pool_w, v_ev_pool_b, v_ev_pool_scale, v_ev_w_out, v_od_w_in, v_od_gnorm_g, v_od_w_out, v_lb_param, v_mlp_w1, v_mlp_w2):
    given = dict(locals())
    w = {n: given[n] for n in WEIGHTS}
    m = {n: given["m_" + n] for n in WEIGHTS}
    v = {n: given["v_" + n] for n in WEIGHTS}
    n_layers = mix_norm_g.shape[0]
    core = lax.axis_index("c").astype(jnp.int32)
    chip = (2 * lax.axis_index("x") + lax.axis_index("y")).astype(jnp.int32)
    chip1 = chip.reshape(1)
    ids2 = jnp.stack([core, chip])

    conv_pad = jnp.pad(ev_conv_w, ((0, 0), (0, CONV_ROWS - CONV_WIDTH), (0, 0)))
    stages = [[(0, n)] for n in (*_mixer_names(0), "mlp_w1", "mlp_w2")]
    for layer in range(1, n_layers):
        stages += [[(layer, n) for n in _mixer_names(layer)], [(layer, "mlp_w1"), (layer, "mlp_w2")]]
    gathers, where, token = [], {}, ()
    for k, stage in enumerate(stages):
        index = [layer if n.startswith("mlp") else layer // 2 for layer, n in stage]
        kinds = [BIG[n] for _, n in stage]
        bufs = [_cast_place(w[n], i, BIG[n], chip1, BF16, f"place_{n}_{i}") for (_, n), i in zip(stage, index)]
        if k == 0:
            bufs.append(_cast_place(meta_tokens[None], 0, "col", chip1, F32, "place_meta"))
            bufs.append(_cast_place(conv_pad.reshape(1, -1, conv_pad.shape[2]), 0, "col", chip1, F32, "place_conv_w"))
            kinds += ["col", "col"]
        plan = _gather_plan(kinds, True)
        ss, rs, bufs, tok = _split_start(f"gather_start_{k}", plan, bufs, 3 * len(bufs), deps=token)
        token = (tok,)
        gathers.append((kinds, plan, ss, rs, bufs))
        where.update({key: (k, f) for f, key in enumerate(stage)})

    landed, passed, held = {}, {}, []

    def hand_on(k, deps):
        if k not in passed:
            kinds, plan, ss, rs, bufs = gathers[k]
            to_sibling = _gather_plan(kinds, False)
            ss, rs, bufs, tok = _split_start(f"gather_pass_{k}", to_sibling, bufs, 3 * len(bufs), deps=deps, earlier=(plan, ss, rs))
            passed[k] = (to_sibling, ss, rs, bufs)
            held.append(tok)

    def arrived(k, after):
        if k not in landed:
            hand_on(k, after)
            landed[k] = _split_wait(f"gather_wait_{k}", *passed[k], after)
        return landed[k]

    def weights(layer, name, after):
        k, f = where[(layer, name)]
        full = arrived(k, after)[f][None]
        if name == "mlp_w2" and layer + 1 < n_layers:
            hand_on(where[(layer + 1, _mixer_names(layer + 1)[0])][0], after)
        if layer > 0 and name == _mixer_names(layer)[0]:
            hand_on(where[(layer, "mlp_w1")][0], after)
        tokens = tuple(held)
        held.clear()
        return full, tokens

    first = arrived(0, token)
    P = {n: w[n] for n in ("mix_norm_g", "mlp_norm_g", "final_norm_g", "ev_conv_b", "ev_ln_g", "ev_ln_b", "ev_pool_w",
                           "ev_pool_b", "ev_pool_scale", "od_gnorm_g", "lb_param")}
    P["meta_full"] = first[1]
    P["conv_w_full"] = first[2].reshape(ev_conv_w.shape[0], CONV_ROWS, -1)

    pending, outs = [], {n: None for n in BIG}

    def advance(after, fresh=1):
        tokens, still = [], []
        for pos, st in enumerate(pending):
            if st["phase"] == 1 and pos >= len(pending) - fresh:
                still.append(st)
            elif st["phase"] == 1:
                bufs = _split_wait(f"reduce_wait_{st['tag']}", _reduce_plan, st["ss"], st["rs"], st["bufs"], after)
                halves = [_sum_pieces(ids2, bufs[2 * f], bufs[2 * f + 1], f"sum_{st['tag']}_{f}") for f in range(len(bufs) // 2)]
                ss, rs, halves, tok = _split_start(f"swap_start_{st['tag']}", _swap_plan, halves, len(halves))
                tokens.append(tok)
                still.append(dict(st, phase=2, ss=ss, rs=rs, bufs=halves))
            else:
                grads = _split_wait(f"swap_wait_{st['tag']}", _swap_plan, st["ss"], st["rs"], st["bufs"], after)
                for (n, i), g in zip(st["keys"], grads):
                    outs[n] = _adamw_layer(w[n], m[n], v[n], g.reshape(w[n].shape[1:]), i, outs[n], f"adamw_{n}_{i}")
        pending[:] = still
        return tokens

    def boundary(tag, grads, after):
        tokens = advance(after)
        bufs = []
        for acc in grads.values():
            bufs += [acc, lax.empty((N_DEV - 1,) + acc.shape[2:], BF16)]
        ss, rs, bufs, tok = _split_start(f"reduce_start_{tag}", _reduce_plan, bufs, 7 * len(grads))
        pending.append(dict(phase=1, tag=tag, keys=list(grads), ss=ss, rs=rs, bufs=bufs))
        return tuple(tokens + [tok])

    loss, dh, small = _local_step(x[0], loss_target[0], P, weights, boundary, first_deps=token)

    order = [n for n in WEIGHTS if n not in BIG]
    block = _pack([small[n] for n in order] + [loss])
    ss, rs, bufs, tok = _split_start("small_start", _small_plan, [block, lax.empty((N_DEV,) + block.shape, F32)], N_DEV - 1)
    while pending:
        advance((tok,) + tuple(o[0] for o in outs.values() if o is not None), fresh=0)
    block, land = _split_wait("small_wait", _small_plan, ss, rs, bufs, tuple(outs[n][0] for n in BIG))
    packed = _sum_blocks((4 * lax.axis_index("x") + 2 * lax.axis_index("y") + lax.axis_index("c")).astype(jnp.int32).reshape(1), block, land)
    total = _unpack(packed, [small[n].shape for n in order] + [loss.shape])
    loss_sum = total[-1][0, 0]
    gsmall = dict(zip(order, total[:-1]))
    gsmall["meta_tokens"] = lax.dynamic_slice_in_dim(gsmall["meta_tokens"], chip * meta_tokens.shape[1], meta_tokens.shape[1], 1)
    gsmall["ev_conv_w"] = lax.dynamic_slice_in_dim(gsmall["ev_conv_w"][:, :CONV_WIDTH], chip * ev_conv_w.shape[2], ev_conv_w.shape[2], 2)

    g_out, d_out, m_out, v_out = {}, {}, {}, {}
    for n in WEIGHTS:
        if n in BIG:
            g_out[n], d_out[n], m_out[n], v_out[n] = outs[n]
            continue
        shape = w[n].shape
        g = gsmall[n].reshape(shape)
        cols = shape[-1] if len(shape) > 1 else 128
        two = lambda a: a.reshape(-1, cols)
        d_, m_, v_ = _adamw(two(w[n]), two(g), two(m[n]), two(v[n]), f"adamw_{n}")
        g_out[n], d_out[n], m_out[n], v_out[n] = g, d_.reshape(shape), m_.reshape(shape), v_.reshape(shape)

    grad_x = dh[LEAD:][None]
    return (loss_sum, grad_x, *[g_out[n] for n in WEIGHTS], *[d_out[n] for n in WEIGHTS],
            *[m_out[n] for n in WEIGHTS], *[v_out[n] for n in WEIGHTS])
```

```python
import functools

import jax
import jax.numpy as jnp
from jax import lax
from jax.experimental import pallas as pl
from jax.experimental.pallas import tpu as pltpu

F32 = jnp.float32
BF16 = jnp.bfloat16
SDS = jax.ShapeDtypeStruct
MESH = pl.DeviceIdType.MESH
ANY_SPEC = pl.BlockSpec(memory_space=pl.ANY)

N_META = 16
CHUNK = 64
LEAD = CHUNK
PAD = LEAD - N_META
CONV_WIDTH = 31
CONV_ROWS = 32
POOL_WINDOWS = (2, 4, 8, 16)
HEAD_DIM = 128
SUB = 16
EXP_CAP = 80.0
EPS = 1e-6
ADAM_LR = 0.001
ADAM_B1 = 0.9
ADAM_B2 = 0.999
ADAM_EPS = 1e-08
ADAM_WD = 0.01
ADAM_STEP = 10
N_CHIPS = 4
VMEM_LIMIT = 52 << 20
MM_VMEM_BUDGET = 44 << 20


def _params(*sem):
    return pltpu.CompilerParams(dimension_semantics=sem if sem else None, vmem_limit_bytes=VMEM_LIMIT)


def _tile(n, target, unit=CHUNK):
    best = None
    for t in range(unit, min(n, target) + 1, unit):
        if n % t == 0:
            best = t
    assert best is not None, (n, target, unit)
    return best


def _ctile(n, target=512):
    for t in (512, 384, 256, 128):
        if t <= target and n % t == 0:
            return t
    raise ValueError(n)


def _mm_tiles(M, N, per_row, per_col, per_elem):
    best = None
    for tn in (512, 384, 256, 128):
        if N % tn:
            continue
        for tm in sorted((d for d in range(16, M + 1, 16) if M % d == 0), reverse=True):
            if 2 * (tm * per_row + tn * per_col + tm * tn * per_elem) <= MM_VMEM_BUDGET:
                if best is None or tm * tn > best[0] * best[1]:
                    best = (tm, tn)
                break
    assert best is not None, (M, N)
    return best


def _sigmoid(x):
    return 1.0 / (1.0 + jnp.exp(-x))


def _mult(v, m):
    return v if isinstance(v, int) else pl.multiple_of(v, m)


def _row_ids(shape, base):
    return lax.broadcasted_iota(jnp.int32, shape, 0) + base


def _cast_place(w3, layer, kind, chip1, dtype, name):
    _, ks, ns = w3.shape
    tr = _tile(ks, 512, 16)
    full = (ks, ns * N_CHIPS) if kind == "col" else (ks * N_CHIPS, ns)

    def body(chip_ref, w_ref, o_ref):
        del chip_ref
        o_ref[...] = w_ref[...].astype(dtype)

    omap = (lambda i, chip: (i, chip[0])) if kind == "col" else (lambda i, chip: (chip[0] * (ks // tr) + i, 0))
    return pl.pallas_call(
        body,
        grid_spec=pltpu.PrefetchScalarGridSpec(
            num_scalar_prefetch=1, grid=(ks // tr,),
            in_specs=[pl.BlockSpec((None, tr, ns), lambda i, chip: (layer, i, 0))],
            out_specs=pl.BlockSpec((tr, ns), omap)),
        out_shape=SDS(full, dtype), name=name, compiler_params=_params("parallel"))(chip1, w3)


def _rms_fwd(h, g3, layer, name, deps=()):
    T, D = h.shape
    tm = _tile(T, 832)

    def body(h_ref, g_ref, *rest):
        n_ref = rest[-1]
        x = h_ref[...]
        r = lax.rsqrt(jnp.mean(x * x, axis=-1, keepdims=True) + EPS)
        n_ref[...] = ((x * r) * g_ref[...]).astype(BF16)

    return pl.pallas_call(
        body, grid=(T // tm,),
        in_specs=[pl.BlockSpec((tm, D), lambda i: (i, 0)), pl.BlockSpec((None, 1, D), lambda i: (layer, 0, 0))]
        + [ANY_SPEC] * len(deps),
        out_specs=pl.BlockSpec((tm, D), lambda i: (i, 0)), out_shape=SDS((T, D), BF16),
        name=name, compiler_params=_params("parallel"))(h, g3, *deps)


def _final(h, g2, target):
    T, D = h.shape
    tm = _tile(T, 320)
    nsub = tm // CHUNK
    nblk = target.shape[0] // CHUNK

    def body(h_ref, g_ref, *rest):
        t_refs = rest[:nsub]
        dh_ref, dhb_ref, dg_ref, loss_ref = rest[nsub:]
        i = pl.program_id(0)

        @pl.when(i == 0)
        def _():
            dg_ref[...] = jnp.zeros_like(dg_ref)
            loss_ref[...] = jnp.zeros_like(loss_ref)

        g = g_ref[...]
        for q in range(nsub):
            rows = slice(q * CHUNK, (q + 1) * CHUNK)
            x = h_ref[rows, :]
            r = lax.rsqrt(jnp.mean(x * x, axis=-1, keepdims=True) + EPS)
            xh = x * r
            live = jnp.where(i * nsub + q > 0, 1.0, 0.0).astype(F32)
            e = ((xh * g) - t_refs[q][...]) * live
            dy = e * (1.0 / D)
            dxh = dy * g
            dh = r * (dxh - xh * jnp.mean(dxh * xh, axis=-1, keepdims=True))
            dh_ref[rows, :] = dh
            dhb_ref[rows, :] = dh.astype(BF16)
            dg_ref[...] += jnp.sum(dy * xh, axis=0, keepdims=True)
            loss_ref[...] += jnp.sum(e * e) * (0.5 / D)

    row = pl.BlockSpec((tm, D), lambda i: (i, 0))
    t_specs = [pl.BlockSpec((CHUNK, D), functools.partial(lambda i, q: (jnp.clip(i * nsub + q - 1, 0, nblk - 1), 0), q=q))
               for q in range(nsub)]
    return pl.pallas_call(
        body, grid=(T // tm,),
        in_specs=[row, pl.BlockSpec((1, D), lambda i: (0, 0))] + t_specs,
        out_specs=[row, row, pl.BlockSpec((1, D), lambda i: (0, 0)), pl.BlockSpec((1, 128), lambda i: (0, 0))],
        out_shape=[SDS((T, D), F32), SDS((T, D), BF16), SDS((1, D), F32), SDS((1, 128), F32)],
        name="final_loss", compiler_params=_params("arbitrary"))(h, g2, *([target] * nsub))


def _mm_nn(a, w3, layer, name, res=None, relu=False, square=False, deps=()):
    M, K = a.shape
    N = w3.shape[2]
    tm, tn = _mm_tiles(M, N, 2 * K, 2 * K, (2 if relu else 4) + (4 if res is not None else 0))

    def body(*refs):
        lhs = refs[0][...]
        acc = jnp.dot(lhs * lhs if square else lhs, refs[1][...], preferred_element_type=F32)
        if res is not None:
            acc = acc + refs[2][...]
        refs[-1][...] = jnp.maximum(acc, 0.0).astype(BF16) if relu else acc

    in_specs = [pl.BlockSpec((tm, K), lambda i, j: (i, 0)), pl.BlockSpec((None, K, tn), lambda i, j: (layer, 0, j))]
    args = [a, w3]
    tile = pl.BlockSpec((tm, tn), lambda i, j: (i, j))
    if res is not None:
        in_specs.append(tile)
        args.append(res)
    in_specs += [ANY_SPEC] * len(deps)
    args += list(deps)
    return pl.pallas_call(
        body, grid=(M // tm, N // tn), in_specs=in_specs, out_specs=tile,
        out_shape=SDS((M, N), BF16 if relu else F32),
        name=name, compiler_params=_params("parallel", "parallel"))(*args)


def _mm_nt(dy, w3, layer, name, relu=None, deps=()):
    M, N = dy.shape
    K = w3.shape[1]
    tm, tk = _mm_tiles(M, K, 2 * N, 2 * N, 4)

    def body(*refs):
        acc = lax.dot_general(refs[0][...], refs[1][...], (((1,), (1,)), ((), ())), preferred_element_type=F32)
        if relu is not None:
            acc = (acc * (2.0 * refs[2][...].astype(F32))).astype(BF16)
        refs[-1][...] = acc

    tile = pl.BlockSpec((tm, tk), lambda i, j: (i, j))
    in_specs = [pl.BlockSpec((tm, N), lambda i, j: (i, 0)), pl.BlockSpec((None, tk, N), lambda i, j: (layer, j, 0))]
    args = [dy, w3]
    if relu is not None:
        in_specs.append(tile)
        args.append(relu)
    in_specs += [ANY_SPEC] * len(deps)
    args += list(deps)
    return pl.pallas_call(
        body, grid=(M // tm, K // tk), in_specs=in_specs, out_specs=tile,
        out_shape=SDS((M, K), F32 if relu is None else BF16),
        name=name, compiler_params=_params("parallel", "parallel"))(*args)


def _row_tile(M, per_row, fixed):
    for tm in sorted((d for d in range(16, M + 1, 16) if M % d == 0), reverse=True):
        if 2 * (tm * per_row + fixed) <= MM_VMEM_BUDGET:
            return tm
    raise ValueError((M, per_row, fixed))


def _mm_nn_norm(a, w3, layer, res, g3, glayer, name, square=False, deps=()):
    M, K = a.shape
    D = w3.shape[2]
    tm = _row_tile(M, 2 * K + 10 * D, 2 * K * D)

    def body(a_ref, w_ref, r_ref, g_ref, *rest):
        h_ref, n_ref = rest[-2:]
        lhs = a_ref[...]
        x = r_ref[...] + jnp.dot(lhs * lhs if square else lhs, w_ref[...], preferred_element_type=F32)
        h_ref[...] = x
        r = lax.rsqrt(jnp.mean(x * x, axis=-1, keepdims=True) + EPS)
        n_ref[...] = ((x * r) * g_ref[...]).astype(BF16)

    row = pl.BlockSpec((tm, D), lambda i: (i, 0))
    return pl.pallas_call(
        body, grid=(M // tm,),
        in_specs=[pl.BlockSpec((tm, K), lambda i: (i, 0)), pl.BlockSpec((None, K, D), lambda i: (layer, 0, 0)), row,
                  pl.BlockSpec((None, 1, D), lambda i: (glayer, 0, 0))] + [ANY_SPEC] * len(deps),
        out_specs=[row, row], out_shape=[SDS((M, D), F32), SDS((M, D), BF16)],
        name=name, compiler_params=_params("parallel"))(a, w3, res, g3, *deps)


def _tail_fwd(y, w_out, res, mlp_g3, layer, w1, w2, next_g3, name, deps=()):
    M, K = y.shape
    D = w_out.shape[2]
    F = w1.shape[2]
    hb = _ctile(F)
    more = next_g3 is not None
    tm = _row_tile(M, 2 * K + 18 * D + (2 * D if more else 0) + 2 * F, 2 * K * D + 2 * D * F)

    def body(y_ref, wo_ref, res_ref, g_ref, w1_ref, w2_ref, *rest):
        outs = rest[-5:] if more else rest[-4:]
        h1 = res_ref[...] + jnp.dot(y_ref[...], wo_ref[...], preferred_element_type=F32)
        outs[0][...] = h1
        n2 = ((h1 * lax.rsqrt(jnp.mean(h1 * h1, axis=-1, keepdims=True) + EPS)) * g_ref[...]).astype(BF16)
        outs[1][...] = n2
        acc = h1
        for jb in range(F // hb):
            cols = slice(jb * hb, (jb + 1) * hb)
            r = jnp.maximum(jnp.dot(n2, w1_ref[:, cols], preferred_element_type=F32), 0.0).astype(BF16)
            outs[-1][:, cols] = r
            acc = acc + jnp.dot(r * r, w2_ref[cols, :], preferred_element_type=F32)
        outs[2][...] = acc
        if more:
            outs[3][...] = ((acc * lax.rsqrt(jnp.mean(acc * acc, axis=-1, keepdims=True) + EPS)) * rest[0][...]).astype(BF16)

    row = pl.BlockSpec((tm, D), lambda i: (i, 0))
    once = dict(pipeline_mode=pl.Buffered(1))
    in_specs = [pl.BlockSpec((tm, K), lambda i: (i, 0)), pl.BlockSpec((None, K, D), lambda i: (0, 0, 0), **once), row,
                pl.BlockSpec((None, 1, D), lambda i: (layer, 0, 0)),
                pl.BlockSpec((None, D, F), lambda i: (0, 0, 0), **once), pl.BlockSpec((None, F, D), lambda i: (0, 0, 0), **once)]
    args = [y, w_out, res, mlp_g3, w1, w2]
    out_specs, out_shape = [row, row, row], [SDS((M, D), F32), SDS((M, D), BF16), SDS((M, D), F32)]
    if more:
        in_specs.append(pl.BlockSpec((None, 1, D), lambda i: (layer + 1, 0, 0)))
        args.append(next_g3)
        out_specs.append(row)
        out_shape.append(SDS((M, D), BF16))
    out_specs.append(pl.BlockSpec((tm, F), lambda i: (i, 0)))
    out_shape.append(SDS((M, F), BF16))
    in_specs += [ANY_SPEC] * len(deps)
    args += list(deps)
    return pl.pallas_call(
        body, grid=(M // tm,), in_specs=in_specs, out_specs=out_specs, out_shape=out_shape,
        name=name, compiler_params=_params("parallel"))(*args)


def _mlp_bwd(dhb, relu, w1, w2, h, g3, glayer, dh_in, name, deps=()):
    M, D = dhb.shape
    F = w1.shape[2]
    hb = _ctile(F)
    tm = _row_tile(M, 16 * D + 4 * F, 2 * D * F)

    def body(dy_ref, r_ref, w1_ref, w2_ref, h_ref, g_ref, dhi_ref, *rest):
        dz_ref, dh_ref, dhb_ref, dg_ref = rest[-4:]
        dy = dy_ref[...]
        dn = jnp.zeros((tm, D), F32)
        for jb in range(F // hb):
            cols = slice(jb * hb, (jb + 1) * hb)
            dact = lax.dot_general(dy, w2_ref[cols, :], (((1,), (1,)), ((), ())), preferred_element_type=F32)
            dz = (dact * (2.0 * r_ref[:, cols].astype(F32))).astype(BF16)
            dz_ref[:, cols] = dz
            dn = dn + lax.dot_general(dz, w1_ref[:, cols], (((1,), (1,)), ((), ())), preferred_element_type=F32)
        x = h_ref[...]
        r = lax.rsqrt(jnp.mean(x * x, axis=-1, keepdims=True) + EPS)
        xh = x * r
        dxh = dn * g_ref[...]
        dh = dhi_ref[...] + r * (dxh - xh * jnp.mean(dxh * xh, axis=-1, keepdims=True))
        dh_ref[...] = dh
        dhb_ref[...] = dh.astype(BF16)

        @pl.when(pl.program_id(0) == 0)
        def _():
            dg_ref[...] = jnp.zeros_like(dg_ref)

        dg_ref[...] += jnp.sum(dn * xh, axis=0, keepdims=True)

    row = pl.BlockSpec((tm, D), lambda i: (i, 0))
    wide = pl.BlockSpec((tm, F), lambda i: (i, 0))
    once = dict(pipeline_mode=pl.Buffered(1))
    return pl.pallas_call(
        body, grid=(M // tm,),
        in_specs=[row, wide, pl.BlockSpec((None, D, F), lambda i: (0, 0, 0), **once),
                  pl.BlockSpec((None, F, D), lambda i: (0, 0, 0), **once), row,
                  pl.BlockSpec((None, 1, D), lambda i: (glayer, 0, 0)), row] + [ANY_SPEC] * len(deps),
        out_specs=[wide, row, row, pl.BlockSpec((1, D), lambda i: (0, 0))],
        out_shape=[SDS((M, F), BF16), SDS((M, D), F32), SDS((M, D), BF16), SDS((1, D), F32)],
        name=name, compiler_params=_params("arbitrary"))(dhb, relu, w1, w2, h, g3, dh_in, *deps)


def _mm_nt_norm(dy, w3, layer, h, g3, glayer, dh_in, name, deps=()):
    M, N = dy.shape
    D = w3.shape[1]
    tm = _row_tile(M, 2 * N + 14 * D, 2 * N * D)

    def body(dy_ref, w_ref, h_ref, g_ref, dhi_ref, *rest):
        dh_ref, dhb_ref, dg_ref = rest[-3:]
        dn = lax.dot_general(dy_ref[...], w_ref[...], (((1,), (1,)), ((), ())), preferred_element_type=F32)
        x = h_ref[...]
        r = lax.rsqrt(jnp.mean(x * x, axis=-1, keepdims=True) + EPS)
        xh = x * r
        dxh = dn * g_ref[...]
        dh = dhi_ref[...] + r * (dxh - xh * jnp.mean(dxh * xh, axis=-1, keepdims=True))
        dh_ref[...] = dh
        dhb_ref[...] = dh.astype(BF16)

        @pl.when(pl.program_id(0) == 0)
        def _():
            dg_ref[...] = jnp.zeros_like(dg_ref)

        dg_ref[...] += jnp.sum(dn * xh, axis=0, keepdims=True)

    row = pl.BlockSpec((tm, D), lambda i: (i, 0))
    return pl.pallas_call(
        body, grid=(M // tm,),
        in_specs=[pl.BlockSpec((tm, N), lambda i: (i, 0)), pl.BlockSpec((None, D, N), lambda i: (layer, 0, 0)), row,
                  pl.BlockSpec((None, 1, D), lambda i: (glayer, 0, 0)), row] + [ANY_SPEC] * len(deps),
        out_specs=[row, row, pl.BlockSpec((1, D), lambda i: (0, 0))],
        out_shape=[SDS((M, D), F32), SDS((M, D), BF16), SDS((1, D), F32)],
        name=name, compiler_params=_params("arbitrary"))(dy, w3, h, g3, dh_in, *deps)


def _fam_dims(kind, K, N):
    return (K // 2, N // N_CHIPS) if kind == "col" else (K // (2 * N_CHIPS), N)


def _mm_tn(x, dy, kind, name, square=False):
    M, K = x.shape
    N = dy.shape[1]
    nr, nc = _fam_dims(kind, K, N)

    def body(x_ref, dy_ref, o_ref):
        lhs = x_ref[...]
        res = lax.dot_general(lhs * lhs if square else lhs, dy_ref[...], (((0,), (0,)), ((), ())), preferred_element_type=F32)
        o_ref[...] = res.astype(BF16).reshape(o_ref.shape)

    if kind == "col":
        tn = _ctile(nc)
        ct = nc // tn
        grid = (N // tn,)
        in_specs = [pl.BlockSpec((M, K), lambda j: (0, 0)), pl.BlockSpec((M, tn), lambda j: (0, j))]
        out_spec = pl.BlockSpec((2, None, nr, tn), lambda j: (0, j // ct, 0, j % ct))
    else:
        grid = (N_CHIPS,)
        in_specs = [pl.BlockSpec((M, 2 * nr), lambda i: (0, i)), pl.BlockSpec((M, N), lambda i: (0, 0))]
        out_spec = pl.BlockSpec((2, None, nr, N), lambda i: (0, i, 0, 0))
    return pl.pallas_call(
        body, grid=grid, in_specs=in_specs, out_specs=out_spec, out_shape=SDS((2, N_CHIPS, nr, nc), BF16),
        name=name, compiler_params=_params("parallel"))(x, dy)


C_EVEN = 512


def _live(rows, base, total):
    r = _row_ids((rows, 1), base)
    return jnp.logical_and(r >= PAD, r < total).astype(F32)


def _conv_taps(win, w_ref, ls, acc, flip):
    for b in range(8):
        rb = win if b == 0 else pltpu.roll(win, 96 - b, 0)
        for a in range(5):
            o = 8 * a + b
            tap = (30 - o) if flip else (o - 2)
            if 0 <= tap < CONV_WIDTH:
                acc = acc + w_ref[pl.ds(tap, 1), ls] * rb[8 * a:8 * a + CHUNK]
    return acc


def _window_sum(win, levels, forward):
    s = win
    n = win.shape[0]
    for k in range(levels):
        step = 1 << k
        s = s + pltpu.roll(s, (n - step) if forward else step, 0)
    return s


def _pool_count(base, g):
    pos = _row_ids((CHUNK, 1), base) - PAD
    return jnp.clip(pos + 1, 1, POOL_WINDOWS[g]).astype(F32)


def _even_fwd(u, cw3, cb3, lg3, lb3, pw4, pb3, ps3, j, name):
    T = u.shape[0]
    C = C_EVEN
    tm = _tile(T, 320)
    nch = tm // CHUNK
    nblk = T // CHUNK

    def body(u_ref, up_ref, cw_ref, cb_ref, lg_ref, lb_ref, pw_ref, pb_ref, ps_ref, o_ref, yc_ref, a_s, p_s, yc_s):
        row0 = pl.program_id(0) * tm
        up = up_ref[...]
        lp = _live(CHUNK, row0 - CHUNK, T)
        a_s[0:CHUNK, :] = up[:, 0:C] * _sigmoid(up[:, C:2 * C]) * lp
        p_s[0:CHUNK, :] = up[:, 2 * C:3 * C] * lp

        def stage(c, _):
            rs = _mult(c * CHUNK, CHUNK)
            lv = _live(CHUNK, row0 + rs, T)
            a_s[pl.ds(rs + CHUNK, CHUNK), :] = u_ref[pl.ds(rs, CHUNK), 0:C] * _sigmoid(u_ref[pl.ds(rs, CHUNK), C:2 * C]) * lv
            p_s[pl.ds(rs + CHUNK, CHUNK), :] = u_ref[pl.ds(rs, CHUNK), 2 * C:3 * C] * lv
            return 0

        for c in range(nch):
            stage(c, 0)

        def chunk(c, _):
            rs = _mult(c * CHUNK, CHUNK)
            lv = _live(CHUNK, row0 + rs, T)
            for cb in range(4):
                ls = slice(cb * 128, (cb + 1) * 128)
                win = a_s[pl.ds(_mult(rs + 32, 32), 96), ls]
                acc = jnp.broadcast_to(cb_ref[:, ls], (CHUNK, 128))
                yc_s[:, ls] = _conv_taps(win, cw_ref, ls, acc, False)
            y = yc_s[...]
            yc_ref[pl.ds(rs, CHUNK), :] = y
            xc = y - jnp.mean(y, axis=-1, keepdims=True)
            yn = xc * lax.rsqrt(jnp.mean(xc * xc, axis=-1, keepdims=True) + EPS) * lg_ref[...] + lb_ref[...]
            o_ref[pl.ds(rs, CHUNK), 0:C] = (yn * _sigmoid(yn) * lv).astype(BF16)
            for g in range(4):
                ls = slice(g * 128, (g + 1) * 128)
                win = p_s[pl.ds(_mult(rs + 48, 16), 80), ls]
                s = _window_sum(win, g + 1, False)
                d = s[16:80] / _pool_count(row0 + rs, g) - win[16:80]
                yv = jnp.dot(d.astype(BF16), pw_ref[g].astype(BF16), preferred_element_type=F32) + pb_ref[:, ls]
                o_ref[pl.ds(rs, CHUNK), C + g * 128:C + (g + 1) * 128] = (yv * ps_ref[:, ls] * lv).astype(BF16)
            return 0

        for c in range(nch):
            chunk(c, 0)

    vec = pl.BlockSpec((None, 1, C), lambda i: (j, 0, 0))
    return pl.pallas_call(
        body, grid=(T // tm,),
        in_specs=[pl.BlockSpec((tm, 3 * C), lambda i: (i, 0)),
                  pl.BlockSpec((CHUNK, 3 * C), lambda i: (jnp.maximum(i * nch - 1, 0), 0)),
                  pl.BlockSpec((None, CONV_ROWS, C), lambda i: (j, 0, 0)), vec, vec, vec,
                  pl.BlockSpec((None, 4, 128, 128), lambda i: (j, 0, 0, 0)), vec, vec],
        out_specs=[pl.BlockSpec((tm, 2 * C), lambda i: (i, 0)), pl.BlockSpec((tm, C), lambda i: (i, 0))],
        out_shape=[SDS((T, 2 * C), BF16), SDS((T, C), F32)],
        scratch_shapes=[pltpu.VMEM((tm + CHUNK, C), F32), pltpu.VMEM((tm + CHUNK, C), F32), pltpu.VMEM((CHUNK, C), F32)],
        name=name, compiler_params=_params("parallel"))(u, u, cw3, cb3, lg3, lb3, pw4, pb3, ps3)


def _even_bwd(u, yc, dy, cw3, cb3, lg3, lb3, pw4, pb3, ps3, j, name):
    T = u.shape[0]
    C = C_EVEN
    tm = _tile(T, 320)
    nch = tm // CHUNK
    nblk = T // CHUNK
    ntile = T // tm

    def body(u_ref, up_ref, un_ref, yc_ref, ycn_ref, dy_ref, dyn_ref, cw_ref, cb_ref, lg_ref, lb_ref, pw_ref, pb_ref, ps_ref,
             du_ref, dcw_ref, dcb_ref, dlg_ref, dlb_ref, dpw_ref, dpb_ref, dps_ref,
             a_s, p_s, dy_s, dyc_s, dd_s, ddc_s, dw_s):
        i = pl.program_id(0)
        row0 = i * tm

        @pl.when(i == 0)
        def _():
            for ref in (dcb_ref, dlg_ref, dlb_ref, dpw_ref, dpb_ref, dps_ref, dw_s):
                ref[...] = jnp.zeros_like(ref)

        up = up_ref[...]
        lp = _live(CHUNK, row0 - CHUNK, T)
        a_s[0:CHUNK, :] = up[:, 0:C] * _sigmoid(up[:, C:2 * C]) * lp
        p_s[0:CHUNK, :] = up[:, 2 * C:3 * C] * lp
        ln_ = _live(CHUNK, row0 + tm, T)
        p_s[tm + CHUNK:tm + 2 * CHUNK, :] = un_ref[:, 2 * C:3 * C] * ln_
        dy_s[tm:tm + CHUNK, :] = dyn_ref[...] * ln_
        dyc_s[tm + CHUNK:tm + CHUNK + 32, :] = jnp.zeros((32, C), F32)

        def stage(c, _):
            rs = _mult(c * CHUNK, CHUNK)
            lv = _live(CHUNK, row0 + rs, T)
            a_s[pl.ds(rs + CHUNK, CHUNK), :] = u_ref[pl.ds(rs, CHUNK), 0:C] * _sigmoid(u_ref[pl.ds(rs, CHUNK), C:2 * C]) * lv
            p_s[pl.ds(rs + CHUNK, CHUNK), :] = u_ref[pl.ds(rs, CHUNK), 2 * C:3 * C] * lv
            dy_s[pl.ds(rs, CHUNK), :] = dy_ref[pl.ds(rs, CHUNK), :] * lv
            return 0

        for c in range(nch):
            stage(c, 0)

        def first(rs, y, own):
            xc = y - jnp.mean(y, axis=-1, keepdims=True)
            rstd = lax.rsqrt(jnp.mean(xc * xc, axis=-1, keepdims=True) + EPS)
            xh = xc * rstd
            yn = xh * lg_ref[...] + lb_ref[...]
            sg = _sigmoid(yn)
            dyn = dy_s[pl.ds(rs, CHUNK), 0:C] * (sg * (1.0 + yn * (1.0 - sg)))
            dlg_ref[...] += jnp.sum(dyn * xh, axis=0, keepdims=True) * own
            dlb_ref[...] += jnp.sum(dyn, axis=0, keepdims=True) * own
            dxh = dyn * lg_ref[...]
            dyc = rstd * (dxh - jnp.mean(dxh, axis=-1, keepdims=True) - xh * jnp.mean(dxh * xh, axis=-1, keepdims=True))
            dyc_s[pl.ds(rs, CHUNK), :] = dyc
            dcb_ref[...] += jnp.sum(dyc, axis=0, keepdims=True) * own
            for g in range(4):
                ls = slice(g * 128, (g + 1) * 128)
                win = p_s[pl.ds(rs + 48, 80), ls]
                s = _window_sum(win, g + 1, False)
                cnt = _pool_count(row0 + rs, g)
                d = (s[16:80] / cnt - win[16:80]).astype(BF16)
                w = pw_ref[g].astype(BF16)
                pre = jnp.dot(d, w, preferred_element_type=F32) + pb_ref[:, ls]
                dyb = dy_s[pl.ds(rs, CHUNK), C + g * 128:C + (g + 1) * 128]
                dpre = dyb * ps_ref[:, ls]
                dps_ref[:, ls] += jnp.sum(dyb * pre, axis=0, keepdims=True) * own
                dpb_ref[:, ls] += jnp.sum(dpre, axis=0, keepdims=True) * own
                dpre_b = (dpre * own).astype(BF16)
                dpw_ref[g] += lax.dot_general(d, dpre_b, (((0,), (0,)), ((), ())), preferred_element_type=F32)
                dd = lax.dot_general(dpre.astype(BF16), w, (((1,), (1,)), ((), ())), preferred_element_type=F32)
                dd_s[pl.ds(rs, CHUNK), ls] = dd
                ddc_s[pl.ds(rs, CHUNK), ls] = dd / cnt

        def first_in_tile(c, _):
            rs = _mult(c * CHUNK, CHUNK)
            first(rs, yc_ref[pl.ds(rs, CHUNK), :], 1.0)
            return 0

        for c in range(nch):
            first_in_tile(c, 0)
        first(tm, ycn_ref[...], 0.0)
        ddc_s[tm + CHUNK:tm + CHUNK + 16, :] = jnp.zeros((16, C), F32)

        def second(c, _):
            rs = _mult(c * CHUNK, CHUNK)
            lv = _live(CHUNK, row0 + rs, T)
            for cb in range(4):
                ls = slice(cb * 128, (cb + 1) * 128)
                wd = dyc_s[pl.ds(rs, 96), ls]
                da = _conv_taps(wd, cw_ref, ls, jnp.zeros((CHUNK, 128), F32), True)
                wa = a_s[pl.ds(_mult(rs + 32, 32), 96), ls]
                dyc = dyc_s[pl.ds(rs, CHUNK), ls]
                for b in range(8):
                    rb = wa if b == 0 else pltpu.roll(wa, 96 - b, 0)
                    for a in range(5):
                        tap = 8 * a + b - 2
                        if 0 <= tap < CONV_WIDTH:
                            prod = dyc * rb[8 * a:8 * a + CHUNK]
                            part = prod[0:8]
                            for q in range(1, 8):
                                part = part + prod[8 * q:8 * q + 8]
                            dw_s[8 * tap:8 * tap + 8, ls] += part
                val = u_ref[pl.ds(rs, CHUNK), ls]
                sg = _sigmoid(u_ref[pl.ds(rs, CHUNK), C + cb * 128:C + (cb + 1) * 128])
                du_ref[pl.ds(rs, CHUNK), ls] = (da * sg * lv).astype(BF16)
                du_ref[pl.ds(rs, CHUNK), C + cb * 128:C + (cb + 1) * 128] = (da * val * sg * (1.0 - sg) * lv).astype(BF16)
            for g in range(4):
                ls = slice(g * 128, (g + 1) * 128)
                z = _window_sum(ddc_s[pl.ds(rs, 80), ls], g + 1, True)
                dpin = (z[0:CHUNK] - dd_s[pl.ds(rs, CHUNK), ls]) * lv
                du_ref[pl.ds(rs, CHUNK), 2 * C + g * 128:2 * C + (g + 1) * 128] = dpin.astype(BF16)
            return 0

        for c in range(nch):
            second(c, 0)

        @pl.when(i == ntile - 1)
        def _():
            for tap in range(CONV_WIDTH):
                dcw_ref[tap:tap + 1, :] = jnp.sum(dw_s[8 * tap:8 * tap + 8, :], axis=0, keepdims=True)
            dcw_ref[CONV_WIDTH:CONV_ROWS, :] = jnp.zeros((CONV_ROWS - CONV_WIDTH, C), F32)

    vec = pl.BlockSpec((None, 1, C), lambda i: (j, 0, 0))
    ovec = pl.BlockSpec((1, C), lambda i: (0, 0))
    return pl.pallas_call(
        body, grid=(ntile,),
        in_specs=[pl.BlockSpec((tm, 3 * C), lambda i: (i, 0)),
                  pl.BlockSpec((CHUNK, 3 * C), lambda i: (jnp.maximum(i * nch - 1, 0), 0)),
                  pl.BlockSpec((CHUNK, 3 * C), lambda i: (jnp.minimum((i + 1) * nch, nblk - 1), 0)),
                  pl.BlockSpec((tm, C), lambda i: (i, 0)),
                  pl.BlockSpec((CHUNK, C), lambda i: (jnp.minimum((i + 1) * nch, nblk - 1), 0)),
                  pl.BlockSpec((tm, 2 * C), lambda i: (i, 0)),
                  pl.BlockSpec((CHUNK, 2 * C), lambda i: (jnp.minimum((i + 1) * nch, nblk - 1), 0)),
                  pl.BlockSpec((None, CONV_ROWS, C), lambda i: (j, 0, 0)), vec, vec, vec,
                  pl.BlockSpec((None, 4, 128, 128), lambda i: (j, 0, 0, 0)), vec, vec],
        out_specs=[pl.BlockSpec((tm, 3 * C), lambda i: (i, 0)), pl.BlockSpec((CONV_ROWS, C), lambda i: (0, 0)),
                   ovec, ovec, ovec, pl.BlockSpec((4, 128, 128), lambda i: (0, 0, 0)), ovec, ovec],
        out_shape=[SDS((T, 3 * C), BF16), SDS((CONV_ROWS, C), F32), SDS((1, C), F32), SDS((1, C), F32), SDS((1, C), F32),
                   SDS((4, 128, 128), F32), SDS((1, C), F32), SDS((1, C), F32)],
        scratch_shapes=[pltpu.VMEM((tm + CHUNK, C), F32), pltpu.VMEM((tm + 2 * CHUNK, C), F32),
                        pltpu.VMEM((tm + CHUNK, 2 * C), F32),
                        pltpu.VMEM((tm + CHUNK + 32, C), F32), pltpu.VMEM((tm + CHUNK, C), F32),
                        pltpu.VMEM((tm + CHUNK + 16, C), F32), pltpu.VMEM((8 * CONV_ROWS, C), F32)],
        name=name, compiler_params=_params("arbitrary"))(u, u, u, yc, yc, dy, dy, cw3, cb3, lg3, lb3, pw4, pb3, ps3)


HI = lax.Precision.HIGHEST


def _dot_nt(a, b):
    return lax.dot_general(a, b, (((1,), (1,)), ((), ())), preferred_element_type=F32)


def _dot_tn(a, b):
    return lax.dot_general(a, b, (((0,), (0,)), ((), ())), preferred_element_type=F32)


def _tri(lower):
    r = lax.broadcasted_iota(jnp.int32, (CHUNK, CHUNK), 0)
    c = lax.broadcasted_iota(jnp.int32, (CHUNK, CHUNK), 1)
    return jnp.where((c <= r) if lower else (c >= r), 1.0, 0.0).astype(F32)


def _hgrn_gates(u_ref, lb_ref, h, D, lv):
    ls = slice(h * HEAD_DIM, (h + 1) * HEAD_DIM)
    qraw = u_ref[:, ls]
    fraw = u_ref[:, D + h * HEAD_DIM:D + (h + 1) * HEAD_DIM]
    v = u_ref[:, 2 * D + h * HEAD_DIM:2 * D + (h + 1) * HEAD_DIM] * lv
    lbv = lb_ref[:, ls]
    sig = _sigmoid(fraw)
    forget = lbv + (1.0 - lbv) * sig
    logf = jnp.log(forget) * lv
    k = (1.0 - forget) * lv
    qsig = _sigmoid(qraw)
    q = qraw * qsig * lv
    return q, k, v, logf, (qraw, qsig, sig, forget, lbv)


def _sub_parts(q, k, b, b_s, I):
    rows = slice(SUB * I, SUB * (I + 1))
    rho = jnp.zeros((1, HEAD_DIM), F32) if I == 0 else b_s[SUB * I - 1:SUB * I, :]
    eI = jnp.exp(b[rows] - rho)
    EI = jnp.exp(jnp.minimum(rho - b, EXP_CAP))
    causal = (lax.broadcasted_iota(jnp.int32, (SUB, CHUNK), 1)
              <= lax.broadcasted_iota(jnp.int32, (SUB, CHUNK), 0) + SUB * I)
    return rows, q[rows] * eI, k * EI, eI, EI, causal


def _chunks_per_step(NC):
    for n in (5, 4, 3, 2):
        if NC % n == 0:
            return n
    return 1


def _hgrn_fwd(u, lb3, layer, gn3, j, name):
    T = u.shape[0]
    D = u.shape[1] // 4
    H = D // HEAD_DIM
    NC = T // CHUNK
    CH = _chunks_per_step(NC)
    R = CH * CHUNK

    def body(u_ref, lb_ref, gn_ref, y_ref, o_ref, sall_ref, st_s, b_s, lf_s, q_s, k_s):
        n = pl.program_id(0)

        @pl.when(n == 0)
        def _():
            st_s[...] = jnp.zeros_like(st_s)

        heads = range(H)
        cols = [slice(h * HEAD_DIM, (h + 1) * HEAD_DIM) for h in heads]
        rows = [slice(c * CHUNK, (c + 1) * CHUNK) for c in range(CH)]
        vb = {}
        for c in range(CH):
            lv = _live(CHUNK, (n * CH + c) * CHUNK, T)
            for h in heads:
                q, k, v, logf, _ = _hgrn_gates(u_ref.at[rows[c]], lb_ref, h, D, lv)
                q_s[rows[c], cols[h]] = q
                k_s[rows[c], cols[h]] = k
                lf_s[rows[c], cols[h]] = logf
                vb[c, h] = v.astype(BF16)
        for c in range(CH):
            b_s[rows[c], :] = jnp.dot(_tri(True), lf_s[rows[c], :], precision=HI, preferred_element_type=F32)
        ops = {}
        for c in range(CH):
            for h in heads:
                b_h = b_s.at[rows[c], cols[h]]
                b = b_h[...]
                q = q_s[rows[c], cols[h]]
                k = k_s[rows[c], cols[h]]
                blast = b_h[CHUNK - 1:CHUNK, :]
                qh = (q * jnp.exp(b)).astype(BF16)
                kt = (k * jnp.exp(blast - b)).astype(BF16)
                subs = []
                for I in range(CHUNK // SUB):
                    _, qI, KI, _, _, causal = _sub_parts(q, k, b, b_h, I)
                    subs.append((qI.astype(BF16), KI.astype(BF16), causal))
                ops[c, h] = (qh, kt, jnp.exp(blast), subs)
        mm = {}
        for h in heads:
            st = st_s[h]
            for c in range(CH):
                qh, kt, eblast, subs = ops[c, h]
                sall_ref[c, h] = st
                o_inter = _dot_nt(qh, st.astype(BF16))
                st = st * eblast + _dot_tn(vb[c, h], kt)
                mm[c, h] = (o_inter, [_dot_nt(qI, KI) for qI, KI, _ in subs])
            st_s[h] = st
        for c in range(CH):
            for h in heads:
                o_inter, ps = mm[c, h]
                p = jnp.concatenate([jnp.where(m, x, 0.0) for x, (_, _, m) in zip(ps, ops[c, h][3])], axis=0).astype(BF16)
                o = o_inter + jnp.dot(p, vb[c, h], preferred_element_type=F32)
                o_ref[rows[c], cols[h]] = o
                graw = u_ref[rows[c], 3 * D + h * HEAD_DIM:3 * D + (h + 1) * HEAD_DIM]
                r = lax.rsqrt(jnp.mean(o * o, axis=-1, keepdims=True) + EPS)
                y_ref[rows[c], cols[h]] = (((o * r) * gn_ref[...]) * (graw * _sigmoid(graw))).astype(BF16)

    return pl.pallas_call(
        body, grid=(NC // CH,),
        in_specs=[pl.BlockSpec((R, 4 * D), lambda n: (n, 0)),
                  pl.BlockSpec((None, 1, D), lambda n: (layer, 0, 0)),
                  pl.BlockSpec((None, 1, HEAD_DIM), lambda n: (j, 0, 0))],
        out_specs=[pl.BlockSpec((R, D), lambda n: (n, 0)), pl.BlockSpec((R, D), lambda n: (n, 0)),
                   pl.BlockSpec((CH, H, HEAD_DIM, HEAD_DIM), lambda n: (n, 0, 0, 0))],
        out_shape=[SDS((T, D), BF16), SDS((T, D), F32), SDS((NC, H, HEAD_DIM, HEAD_DIM), F32)],
        scratch_shapes=[pltpu.VMEM((H, HEAD_DIM, HEAD_DIM), F32)] + [pltpu.VMEM((R, D), F32)] * 4,
        name=name, compiler_params=_params("arbitrary"))(u, lb3, gn3)


def _hgrn_bwd(u, o_raw, dy, sall, lb3, layer, gn3, j, name):
    T = u.shape[0]
    D = u.shape[1] // 4
    H = D // HEAD_DIM
    NC = T // CHUNK
    CH = _chunks_per_step(NC)
    R = CH * CHUNK
    NS = NC // CH

    def body(u_ref, o_ref, dy_ref, sall_ref, lb_ref, gn_ref, du_ref, dlb_ref, dgn_ref, dst_s, b_s, lf_s, q_s, k_s, db_s, dk_s):
        step = pl.program_id(0)
        n = NS - 1 - step

        @pl.when(step == 0)
        def _():
            dst_s[...] = jnp.zeros_like(dst_s)
            dlb_ref[...] = jnp.zeros_like(dlb_ref)
            dgn_ref[...] = jnp.zeros_like(dgn_ref)

        last_row = (_row_ids((CHUNK, 1), 0) == CHUNK - 1).astype(F32)
        gn = gn_ref[...]
        heads = range(H)
        chunks = range(CH)
        cols = [slice(h * HEAD_DIM, (h + 1) * HEAD_DIM) for h in heads]
        rows = [slice(c * CHUNK, (c + 1) * CHUNK) for c in chunks]
        lv = [_live(CHUNK, (n * CH + c) * CHUNK, T) for c in chunks]
        vb, dob = {}, {}
        dgn = jnp.zeros((1, HEAD_DIM), F32)
        for c in chunks:
            for h in heads:
                q, k, v, logf, _ = _hgrn_gates(u_ref.at[rows[c]], lb_ref, h, D, lv[c])
                q_s[rows[c], cols[h]] = q
                k_s[rows[c], cols[h]] = k
                lf_s[rows[c], cols[h]] = logf
                vb[c, h] = v.astype(BF16)
                graw = u_ref[rows[c], 3 * D + h * HEAD_DIM:3 * D + (h + 1) * HEAD_DIM]
                gsig = _sigmoid(graw)
                o = o_ref[rows[c], cols[h]]
                r = lax.rsqrt(jnp.mean(o * o, axis=-1, keepdims=True) + EPS)
                xh = o * r
                dyv = dy_ref[rows[c], cols[h]]
                dsg = dyv * (graw * gsig)
                dgn = dgn + jnp.sum(dsg * xh, axis=0, keepdims=True)
                dxh = dsg * gn
                do = r * (dxh - xh * jnp.mean(dxh * xh, axis=-1, keepdims=True))
                dob[c, h] = do.astype(BF16)
                dgraw = dyv * xh * gn * (gsig * (1.0 + graw * (1.0 - gsig)))
                du_ref[rows[c], 3 * D + h * HEAD_DIM:3 * D + (h + 1) * HEAD_DIM] = (dgraw * lv[c]).astype(BF16)
        dgn_ref[...] += dgn
        for c in chunks:
            b_s[rows[c], :] = jnp.dot(_tri(True), lf_s[rows[c], :], precision=HI, preferred_element_type=F32)
        ops = {}
        for c in chunks:
            for h in heads:
                b_h = b_s.at[rows[c], cols[h]]
                b = b_h[...]
                q = q_s[rows[c], cols[h]]
                k = k_s[rows[c], cols[h]]
                blast = b_h[CHUNK - 1:CHUNK, :]
                eb = jnp.exp(b)
                ekb = jnp.exp(blast - b)
                subs = []
                for I in range(CHUNK // SUB):
                    rws, qI, KI, eI, EI, causal = _sub_parts(q, k, b, b_h, I)
                    subs.append((rws, qI.astype(BF16), KI.astype(BF16), eI, EI, causal))
                ops[c, h] = (eb, ekb, jnp.exp(blast), (q * eb).astype(BF16), (k * ekb).astype(BF16), subs)
        mm = {}
        for h in heads:
            dst = dst_s[h]
            for c in reversed(chunks):
                eb, ekb, eblast, qhb, ktb, subs = ops[c, h]
                st = sall_ref[c, h]
                dstb = dst.astype(BF16)
                dv = _dot_nt(ktb, dstb)
                dqh = jnp.dot(dob[c, h], st.astype(BF16), preferred_element_type=F32)
                dkt = jnp.dot(vb[c, h], dstb, preferred_element_type=F32)
                dblast = jnp.sum(dst * st, axis=0, keepdims=True) * eblast
                dst = dst * eblast + _dot_tn(dob[c, h], qhb)
                dp_full = _dot_nt(dob[c, h], vb[c, h])
                ps = [_dot_nt(qIb, KIb) for _, qIb, KIb, _, _, _ in subs]
                mm[c, h] = (dv, dqh, dkt, dblast, dp_full, ps)
            dst_s[h] = dst
        for c in chunks:
            for h in heads:
                eb, ekb, eblast, qhb, ktb, subs = ops[c, h]
                dv, dqh, dkt, dblast, dp_full, ps = mm[c, h]
                p = jnp.concatenate([jnp.where(sub[5], x, 0.0) for x, sub in zip(ps, subs)], axis=0).astype(BF16)
                dv = dv + _dot_tn(p, dob[c, h])
                du_ref[rows[c], 2 * D + h * HEAD_DIM:2 * D + (h + 1) * HEAD_DIM] = (dv * lv[c]).astype(BF16)
                dq = dqh * eb
                db = dqh * qhb.astype(F32)
                tmp = dkt * ktb.astype(F32)
                dk = dkt * ekb
                db = db - tmp
                dblast = dblast + jnp.sum(tmp, axis=0, keepdims=True)
                dq_parts, db_parts = [], []
                for rws, qIb, KIb, eI, EI, causal in subs:
                    dp = jnp.where(causal, dp_full[rws], 0.0).astype(BF16)
                    dqI = jnp.dot(dp, KIb, preferred_element_type=F32)
                    dKI = _dot_tn(dp, qIb)
                    dq_parts.append(dqI * eI)
                    db_parts.append(dqI * qIb.astype(F32))
                    dk = dk + dKI * EI
                    db = db - dKI * KIb.astype(F32)
                dq = dq + jnp.concatenate(dq_parts, axis=0)
                db_s[rows[c], cols[h]] = db + jnp.concatenate(db_parts, axis=0) + last_row * dblast
                dk_s[rows[c], cols[h]] = dk
                qraw = u_ref[rows[c], cols[h]]
                qsig = _sigmoid(qraw)
                du_ref[rows[c], cols[h]] = (dq * (qsig * (1.0 + qraw * (1.0 - qsig))) * lv[c]).astype(BF16)
        for c in chunks:
            lf_s[rows[c], :] = jnp.dot(_tri(False), db_s[rows[c], :], precision=HI, preferred_element_type=F32)
        for h in heads:
            lbv = lb_ref[:, cols[h]]
            dlb = jnp.zeros((1, HEAD_DIM), F32)
            for c in chunks:
                fraw = u_ref[rows[c], D + h * HEAD_DIM:D + (h + 1) * HEAD_DIM]
                sig = _sigmoid(fraw)
                forget = lbv + (1.0 - lbv) * sig
                dforget = (lf_s[rows[c], cols[h]] / forget - dk_s[rows[c], cols[h]]) * lv[c]
                dlb = dlb + jnp.sum(dforget * (1.0 - sig), axis=0, keepdims=True)
                du_ref[rows[c], D + h * HEAD_DIM:D + (h + 1) * HEAD_DIM] = (dforget * (1.0 - lbv) * sig * (1.0 - sig)).astype(BF16)
            dlb_ref[:, cols[h]] += dlb

    rev = lambda s: (NS - 1 - s, 0)
    return pl.pallas_call(
        body, grid=(NS,),
        in_specs=[pl.BlockSpec((R, 4 * D), rev), pl.BlockSpec((R, D), rev), pl.BlockSpec((R, D), rev),
                  pl.BlockSpec((CH, H, HEAD_DIM, HEAD_DIM), lambda s: (NS - 1 - s, 0, 0, 0)),
                  pl.BlockSpec((None, 1, D), lambda s: (layer, 0, 0)),
                  pl.BlockSpec((None, 1, HEAD_DIM), lambda s: (j, 0, 0))],
        out_specs=[pl.BlockSpec((R, 4 * D), rev), pl.BlockSpec((1, D), lambda s: (0, 0)),
                   pl.BlockSpec((1, HEAD_DIM), lambda s: (0, 0))],
        out_shape=[SDS((T, 4 * D), BF16), SDS((1, D), F32), SDS((1, HEAD_DIM), F32)],
        scratch_shapes=[pltpu.VMEM((H, HEAD_DIM, HEAD_DIM), F32)] + [pltpu.VMEM((R, D), F32)] * 6,
        name=name, compiler_params=_params("arbitrary"))(u, o_raw, dy, sall, lb3, gn3)


def _softmax_layers(p_ref, n_layers):
    rows = [p_ref[l:l + 1, :] for l in range(n_layers)]
    m = functools.reduce(jnp.maximum, rows)
    e = [jnp.exp(x - m) for x in rows]
    tot = functools.reduce(lambda a, b: a + b, e)
    return [x / tot for x in e]


def _lb_fwd(p):
    n_layers, D = p.shape

    def body(p_ref, o_ref):
        s = _softmax_layers(p_ref, n_layers)
        acc = jnp.zeros((1, D), F32)
        o_ref[0:1, :] = acc
        for l in range(1, n_layers):
            acc = acc + s[l]
            o_ref[l:l + 1, :] = acc

    return pl.pallas_call(body, out_shape=SDS(p.shape, F32), name="lb_fwd")(p)


def _lb_bwd(p, dlb):
    n_layers, D = p.shape

    def body(p_ref, d_ref, o_ref):
        s = _softmax_layers(p_ref, n_layers)
        ds = [jnp.zeros((1, D), F32)] * n_layers
        acc = jnp.zeros((1, D), F32)
        for l in range(n_layers - 1, 0, -1):
            acc = acc + d_ref[l:l + 1, :]
            ds[l] = acc
        dot = functools.reduce(lambda a, b: a + b, [s[l] * ds[l] for l in range(n_layers)])
        for l in range(n_layers):
            o_ref[l:l + 1, :] = s[l] * (ds[l] - dot)

    return pl.pallas_call(body, out_shape=SDS(p.shape, F32), name="lb_bwd")(p, dlb)


def _adamw(w, g, m, v, name):
    R, C = w.shape
    tr = _tile(R, 256, 8) if R % 8 == 0 else R

    def body(w_ref, g_ref, m_ref, v_ref, d_ref, mo_ref, vo_ref):
        g_ = g_ref[...]
        m_ = ADAM_B1 * m_ref[...] + (1.0 - ADAM_B1) * g_
        v_ = ADAM_B2 * v_ref[...] + (1.0 - ADAM_B2) * (g_ * g_)
        mh = m_ / (1.0 - ADAM_B1 ** ADAM_STEP)
        vh = v_ / (1.0 - ADAM_B2 ** ADAM_STEP)
        d_ref[...] = -ADAM_LR * (mh / (jnp.sqrt(vh) + ADAM_EPS) + ADAM_WD * w_ref[...])
        mo_ref[...] = m_
        vo_ref[...] = v_

    blk = pl.BlockSpec((tr, C), lambda i: (i, 0))
    return pl.pallas_call(
        body, grid=(R // tr,), in_specs=[blk] * 4, out_specs=[blk] * 3, out_shape=[SDS((R, C), F32)] * 3,
        name=name, compiler_params=_params("parallel"))(w, g, m, v)


def _adamw_layer(w3, m3, v3, g2, layer, outs, name):
    L, R, C = w3.shape
    tr = _tile(R, 256, 8)
    if outs is None:
        outs = tuple(lax.empty(w3.shape, F32) for _ in range(4))

    def body(w_ref, m_ref, v_ref, g_ref, a0, a1, a2, a3, go_ref, d_ref, mo_ref, vo_ref):
        del a0, a1, a2, a3
        g_ = g_ref[...]
        m_ = ADAM_B1 * m_ref[...] + (1.0 - ADAM_B1) * g_
        v_ = ADAM_B2 * v_ref[...] + (1.0 - ADAM_B2) * (g_ * g_)
        mh = m_ / (1.0 - ADAM_B1 ** ADAM_STEP)
        vh = v_ / (1.0 - ADAM_B2 ** ADAM_STEP)
        go_ref[...] = g_
        d_ref[...] = -ADAM_LR * (mh / (jnp.sqrt(vh) + ADAM_EPS) + ADAM_WD * w_ref[...])
        mo_ref[...] = m_
        vo_ref[...] = v_

    lay = pl.BlockSpec((None, tr, C), lambda i: (layer, i, 0))
    return pl.pallas_call(
        body, grid=(R // tr,), in_specs=[lay] * 3 + [pl.BlockSpec((tr, C), lambda i: (i, 0))] + [ANY_SPEC] * 4,
        out_specs=[lay] * 4, out_shape=[SDS(w3.shape, F32)] * 4, input_output_aliases={4: 0, 5: 1, 6: 2, 7: 3},
        name=name, compiler_params=_params("parallel"))(w3, m3, v3, g2, *outs)


SEM_SPEC = pl.BlockSpec(memory_space=pltpu.SEMAPHORE)
HBM_SPEC = pl.BlockSpec(memory_space=pltpu.HBM)
EFFECT = pltpu.SideEffectType.DATAFLOW_SIDE_EFFECTING
N_DEV = 2 * N_CHIPS


def _position():
    x, y, c = lax.axis_index("x"), lax.axis_index("y"), lax.axis_index("c")
    chips = [(1 - x, y), (x, 1 - y), (1 - x, 1 - y)]
    return x, y, c, chips


def _split_start(name, plan, bufs, n_sems, deps=(), earlier=None):
    n = len(bufs)
    held = () if earlier is None else tuple(earlier[1:])

    def body(*refs):
        first_out = n + len(held) + len(deps)
        if earlier is not None:
            sends, recvs = earlier[0](refs[:n], refs[n], refs[n + 1])
            for kw in sends:
                pltpu.make_async_remote_copy(**kw).wait_send()
            for kw in recvs:
                pltpu.make_async_remote_copy(**kw).wait_recv()
        sends, _ = plan(refs[:n], refs[first_out], refs[first_out + 1])
        for kw in sends:
            pltpu.make_async_remote_copy(**kw).start()
        refs[-1][...] = jnp.zeros_like(refs[-1])

    out = pl.pallas_call(
        body, name=name,
        out_shape=(pltpu.SemaphoreType.DMA((n_sems,)), pltpu.SemaphoreType.DMA((n_sems,)),
                   *[pltpu.HBM(b.shape, b.dtype) for b in bufs], SDS((8, 128), F32)),
        in_specs=[HBM_SPEC] * n + [SEM_SPEC] * len(held) + [ANY_SPEC] * len(deps),
        out_specs=(SEM_SPEC, SEM_SPEC, *[HBM_SPEC] * n, pl.BlockSpec(memory_space=pltpu.VMEM)),
        input_output_aliases={i: 2 + i for i in range(n)},
        compiler_params=pltpu.CompilerParams(has_side_effects=EFFECT),
    )(*[pltpu.with_memory_space_constraint(b, pltpu.HBM) for b in bufs], *held, *deps)
    return out[0], out[1], list(out[2:2 + n]), out[-1]


def _split_wait(name, plan, send_sems, recv_sems, bufs, after=()):
    n = len(bufs)

    def body(*refs):
        sends, recvs = plan(refs[:n], refs[n], refs[n + 1])
        for kw in sends:
            pltpu.make_async_remote_copy(**kw).wait_send()
        for kw in recvs:
            pltpu.make_async_remote_copy(**kw).wait_recv()

    out = pl.pallas_call(
        body, name=name, out_shape=tuple(pltpu.HBM(b.shape, b.dtype) for b in bufs),
        in_specs=[HBM_SPEC] * n + [SEM_SPEC, SEM_SPEC] + [ANY_SPEC] * len(after),
        out_specs=tuple([HBM_SPEC] * n), input_output_aliases={i: i for i in range(n)},
        compiler_params=pltpu.CompilerParams(has_side_effects=EFFECT),
    )(*bufs, send_sems, recv_sems, *after)
    return list(out)


def _region(kind, ref, chip, half):
    K, N = ref.shape
    if kind == "col":
        return ref.at[pl.ds(half * (K // 2), K // 2), pl.ds(chip * (N // N_CHIPS), N // N_CHIPS)]
    rows = K // (2 * N_CHIPS)
    return ref.at[pl.ds((2 * chip + half) * rows, rows), :]


def _gather_plan(kinds, over_chips):
    def plan(refs, send_sems, recv_sems):
        x, y, c, chips = _position()
        sends, recvs = [], []
        for f, (ref, kind) in enumerate(zip(refs, kinds)):
            for k, chip in enumerate(chips):
                theirs = 2 * chip[0] + chip[1]
                sem = dict(send_sem=send_sems.at[3 * f + k], recv_sem=recv_sems.at[3 * f + k], device_id_type=MESH)
                if over_chips:
                    out, back, to = _region(kind, ref, 2 * x + y, c), _region(kind, ref, theirs, c), (*chip, c)
                else:
                    out, back, to = _region(kind, ref, theirs, c), _region(kind, ref, theirs, 1 - c), (x, y, 1 - c)
                sends.append(dict(src_ref=out, dst_ref=out, device_id=to, **sem))
                recvs.append(dict(src_ref=back, dst_ref=back, device_id=to, **sem))
        return sends, recvs
    return plan


def _reduce_plan(refs, send_sems, recv_sems):
    x, y, c, _ = _position()
    me = 4 * x + 2 * y + c
    sends, recvs = [], []
    for f in range(len(refs) // 2):
        acc, land = refs[2 * f], refs[2 * f + 1]
        for d in range(1, N_DEV):
            t = (me + d) % N_DEV
            to = dict(device_id=(t // 4, (t // 2) % 2, t % 2), device_id_type=MESH)
            slot = N_DEV - 1 - d
            sends.append(dict(src_ref=acc.at[t % 2, t // 2], dst_ref=land.at[slot], send_sem=send_sems.at[7 * f + d - 1],
                              recv_sem=recv_sems.at[7 * f + slot], **to))
            recvs.append(dict(src_ref=land.at[d - 1], dst_ref=land.at[d - 1], send_sem=send_sems.at[7 * f + d - 1],
                              recv_sem=recv_sems.at[7 * f + d - 1], **to))
    return sends, recvs


def _swap_plan(refs, send_sems, recv_sems):
    x, y, c, _ = _position()
    sends, recvs = [], []
    for f, g in enumerate(refs):
        sem = dict(send_sem=send_sems.at[f], recv_sem=recv_sems.at[f], device_id=(x, y, 1 - c), device_id_type=MESH)
        sends.append(dict(src_ref=g.at[c], dst_ref=g.at[c], **sem))
        recvs.append(dict(src_ref=g.at[1 - c], dst_ref=g.at[1 - c], **sem))
    return sends, recvs


def _sum_pieces(ids2, acc, land, name):
    _, _, nr, nc = acc.shape
    tr = _tile(nr, 256, 16)

    def body(ids_ref, own_ref, land_ref, o_ref):
        del ids_ref
        s = own_ref[...].astype(F32)
        for k in range(N_DEV - 1):
            s = s + land_ref[k].astype(F32)
        o_ref[...] = s

    return pl.pallas_call(
        body,
        grid_spec=pltpu.PrefetchScalarGridSpec(
            num_scalar_prefetch=1, grid=(nr // tr,),
            in_specs=[pl.BlockSpec((None, None, tr, nc), lambda i, ids: (ids[0], ids[1], i, 0)),
                      pl.BlockSpec((N_DEV - 1, tr, nc), lambda i, ids: (0, i, 0))],
            out_specs=pl.BlockSpec((None, tr, nc), lambda i, ids: (ids[0], i, 0))),
        out_shape=SDS((2, nr, nc), F32), name=name, compiler_params=_params("parallel"))(ids2, acc, land)


def _small_plan(refs, send_sems, recv_sems):
    x, y, c, _ = _position()
    me = 4 * x + 2 * y + c
    own, land = refs
    sends, recvs = [], []
    for d in range(1, N_DEV):
        t = (me + d) % N_DEV
        to = dict(device_id=(t // 4, (t // 2) % 2, t % 2), device_id_type=MESH)
        sends.append(dict(src_ref=own, dst_ref=land.at[me], send_sem=send_sems.at[d - 1],
                          recv_sem=recv_sems.at[N_DEV - 1 - d], **to))
        recvs.append(dict(src_ref=land.at[t], dst_ref=land.at[t], send_sem=send_sems.at[d - 1],
                          recv_sem=recv_sems.at[d - 1], **to))
    return sends, recvs


def _sum_blocks(me1, own, land):
    def body(me_ref, own_ref, land_ref, o_ref):
        acc = None
        for d in range(N_DEV):
            term = jnp.where(me_ref[0] == d, own_ref[...], land_ref[d])
            acc = term if acc is None else acc + term
        o_ref[...] = acc

    return pl.pallas_call(
        body,
        grid_spec=pltpu.PrefetchScalarGridSpec(
            num_scalar_prefetch=1, grid=(1,),
            in_specs=[pl.BlockSpec(own.shape, lambda i, me: (0, 0)), pl.BlockSpec(land.shape, lambda i, me: (0, 0, 0))],
            out_specs=pl.BlockSpec(own.shape, lambda i, me: (0, 0))),
        out_shape=SDS(own.shape, F32), name="sum_small", compiler_params=_params("arbitrary"))(me1, own, land)


BIG = {"ev_w_in": "col", "ev_w_out": "row", "od_w_in": "col", "od_w_out": "row", "mlp_w1": "col", "mlp_w2": "row"}
WEIGHTS = ("meta_tokens", "mix_norm_g", "mlp_norm_g", "final_norm_g", "ev_w_in", "ev_conv_w", "ev_conv_b", "ev_ln_g",
           "ev_ln_b", "ev_pool_w", "ev_pool_b", "ev_pool_scale", "ev_w_out", "od_w_in", "od_gnorm_g", "od_w_out",
           "lb_param", "mlp_w1", "mlp_w2")
PACK_UNIT = 1024


def _mixer_names(layer):
    return ("ev_w_in", "ev_w_out") if layer % 2 == 0 else ("od_w_in", "od_w_out")


def _pack(arrays):
    flat = []
    for a in arrays:
        a = a.reshape(-1)
        flat.append(jnp.pad(a, (0, (-a.shape[0]) % PACK_UNIT)))
    return jnp.concatenate(flat).reshape(-1, 128)


def _unpack(packed, shapes):
    flat = packed.reshape(-1)
    out, off = [], 0
    for s in shapes:
        size = 1
        for d in s:
            size *= d
        out.append(flat[off:off + size].reshape(s))
        off += size + (-size) % PACK_UNIT
    return out


def _local_step(x2, target, P, weights, boundary, first_deps=()):
    D = x2.shape[1]
    n_layers = P["mix_norm_g"].shape[0]
    h = jnp.concatenate([jnp.zeros((PAD, D), F32), P["meta_full"], x2], axis=0)
    mix_g = P["mix_norm_g"].reshape(n_layers, 1, D)
    mlp_g = P["mlp_norm_g"].reshape(n_layers, 1, D)
    vec = lambda a: a.reshape(a.shape[0], 1, -1)
    cb3, lg3, lnb3, ps3 = vec(P["ev_conv_b"]), vec(P["ev_ln_g"]), vec(P["ev_ln_b"]), vec(P["ev_pool_scale"])
    pb3 = vec(P["ev_pool_b"])
    gn3 = vec(P["od_gnorm_g"])
    lb_all = _lb_fwd(P["lb_param"])
    lb3 = lb_all.reshape(n_layers, 1, D)
    even = (cb3, lg3, lnb3, P["ev_pool_w"], pb3, ps3)

    saved = []
    deps = tuple(first_deps)
    for layer in range(n_layers):
        j = layer // 2
        w_in, w_out = _mixer_names(layer)
        W = {}
        s = {"h": h, "W": W}
        s["n"] = _rms_fwd(h, mix_g, layer, "mix_norm_0", deps=deps) if layer == 0 else n_next
        deps = ()
        W[w_in], held = weights(layer, w_in, (s["n"],))
        s["u"] = _mm_nn(s["n"], W[w_in], 0, f"mix_in_{layer}", deps=held)
        if layer % 2 == 0:
            s["y"], s["yc"] = _even_fwd(s["u"], P["conv_w_full"], *even, j, f"even_fwd_{layer}")
        else:
            s["y"], s["o"], s["sall"] = _hgrn_fwd(s["u"], lb3, layer, gn3, j, f"hgrn_fwd_{layer}")
        W[w_out], held = weights(layer, w_out, (s["y"],))
        if layer == 0:
            h, s["n2"] = _mm_nn_norm(s["y"], W[w_out], 0, h, mlp_g, layer, "mix_out_0", deps=held)
            s["h1"] = h
            W["mlp_w1"], held = weights(layer, "mlp_w1", (s["n2"],))
            s["relu"] = _mm_nn(s["n2"], W["mlp_w1"], 0, "mlp_up_0", relu=True, deps=held)
            W["mlp_w2"], held = weights(layer, "mlp_w2", (s["relu"],))
            h, n_next = _mm_nn_norm(s["relu"], W["mlp_w2"], 0, h, mix_g, 1, "mlp_down_0", square=True, deps=held)
        else:
            W["mlp_w1"], more1 = weights(layer, "mlp_w1", (s["y"],))
            W["mlp_w2"], more2 = weights(layer, "mlp_w2", (s["y"],))
            last = layer + 1 == n_layers
            out = _tail_fwd(s["y"], W[w_out], h, mlp_g, layer, W["mlp_w1"], W["mlp_w2"], None if last else mix_g,
                            f"tail_{layer}", deps=held + more1 + more2)
            s["h1"], s["n2"], h, s["relu"] = out[0], out[1], out[2], out[-1]
            n_next = None if last else out[3]
        saved.append(s)

    dh, dhb, dg_final, loss = _final(h, P["final_norm_g"].reshape(1, D), target)

    small = {"final_norm_g": dg_final}
    per_layer = {k: [None] * n_layers for k in ("mix_norm_g", "mlp_norm_g", "lb")}
    per_pair = {k: [None] * (n_layers // 2) for k in
                ("ev_conv_w", "ev_conv_b", "ev_ln_g", "ev_ln_b", "ev_pool_w", "ev_pool_b", "ev_pool_scale", "od_gnorm_g")}
    for layer in reversed(range(n_layers)):
        j = layer // 2
        s = saved[layer]
        W = s["W"]
        w_in, w_out = _mixer_names(layer)
        dw2 = _mm_tn(s["relu"], dhb, "row", f"dw2_{layer}", square=True)
        dz, dh, dhb, per_layer["mlp_norm_g"][layer] = _mlp_bwd(
            dhb, s["relu"], W["mlp_w1"], W["mlp_w2"], s["h1"], mlp_g, layer, dh, f"mlp_bwd_{layer}", deps=deps + (dw2,))
        dw1 = _mm_tn(s["n2"], dz, "col", f"dw1_{layer}")
        deps = boundary(f"mlp{layer}", {("mlp_w1", layer): dw1, ("mlp_w2", layer): dw2}, (dhb, dw1, dw2))
        dy = _mm_nt(dhb, W[w_out], 0, f"d_y_{layer}", deps=deps)
        dwout = _mm_tn(s["y"], dhb, "row", f"dwout_{layer}")
        if layer % 2 == 0:
            du, dcw, dcb, dlg, dlnb, dpw, dpb, dps = _even_bwd(s["u"], s["yc"], dy, P["conv_w_full"], *even, j, f"even_bwd_{layer}")
            for k, val in (("ev_conv_w", dcw), ("ev_conv_b", dcb), ("ev_ln_g", dlg), ("ev_ln_b", dlnb),
                           ("ev_pool_w", dpw), ("ev_pool_b", dpb), ("ev_pool_scale", dps)):
                per_pair[k][j] = val
        else:
            du, per_layer["lb"][layer], per_pair["od_gnorm_g"][j] = _hgrn_bwd(
                s["u"], s["o"], dy, s["sall"], lb3, layer, gn3, j, f"hgrn_bwd_{layer}")
        dwin = _mm_tn(s["n"], du, "col", f"dwin_{layer}")
        deps = boundary(f"mix{layer}", {(w_in, j): dwin, (w_out, j): dwout}, (du, dwin, dwout))
        dh, dhb, per_layer["mix_norm_g"][layer] = _mm_nt_norm(du, W[w_in], 0, s["h"], mix_g, layer, dh, f"d_n_{layer}", deps=deps)
        deps = ()

    small["mix_norm_g"] = jnp.concatenate(per_layer["mix_norm_g"], axis=0)
    small["mlp_norm_g"] = jnp.concatenate(per_layer["mlp_norm_g"], axis=0)
    dlb_all = jnp.concatenate([jnp.zeros((1, D), F32) if g is None else g for g in per_layer["lb"]], axis=0)
    small["lb_param"] = _lb_bwd(P["lb_param"], dlb_all)
    for k, vals in per_pair.items():
        small[k] = jnp.stack(vals, axis=0)
    small["meta_tokens"] = dh[PAD:LEAD]
    return loss, dh, small


def kernel(x, meta_tokens, mix_norm_g, mlp_norm_g, final_norm_g, ev_w_in, ev_conv_w, ev_conv_b, ev_ln_g, ev_ln_b, ev_pool_w, ev_pool_b, ev_pool_scale, ev_w_out, od_w_in, od_gnorm_g, od_w_out, lb_param, mlp_w1, mlp_w2, loss_target, m_meta_tokens, m_mix_norm_g, m_mlp_norm_g, m_final_norm_g, m_ev_w_in, m_ev_conv_w, m_ev_conv_b, m_ev_ln_g, m_ev_ln_b, m_ev_pool_w, m_ev_pool_b, m_ev_pool_scale, m_ev_w_out, m_od_w_in, m_od_gnorm_g, m_od_w_out, m_lb_param, m_mlp_w1, m_mlp_w2, v_meta_tokens, v_mix_norm_g, v_mlp_norm_g, v_final_norm_g, v_ev_w_in, v_ev_conv_w, v_ev_conv_b, v_ev_ln_g, v_ev_ln_b, v_ev_pool_w, v_ev_pool_b, v_ev_pool_scale, v_ev_w_out, v_od_w_in, v_od_gnorm_g, v_od_w_out, v_lb_param, v_mlp_w1, v_mlp_w2):
    given = dict(locals())
    w = {n: given[n] for n in WEIGHTS}
    m = {n: given["m_" + n] for n in WEIGHTS}
    v = {n: given["v_" + n] for n in WEIGHTS}
    n_layers = mix_norm_g.shape[0]
    core = lax.axis_index("c").astype(jnp.int32)
    chip = (2 * lax.axis_index("x") + lax.axis_index("y")).astype(jnp.int32)
    chip1 = chip.reshape(1)
    ids2 = jnp.stack([core, chip])

    conv_pad = jnp.pad(ev_conv_w, ((0, 0), (0, CONV_ROWS - CONV_WIDTH), (0, 0)))
    stages = [[(0, n)] for n in (*_mixer_names(0), "mlp_w1", "mlp_w2")]
    for layer in range(1, n_layers):
        stages += [[(layer, n) for n in _mixer_names(layer)], [(layer, "mlp_w1"), (layer, "mlp_w2")]]
    gathers, where, token = [], {}, ()
    for k, stage in enumerate(stages):
        index = [layer if n.startswith("mlp") else layer // 2 for layer, n in stage]
        kinds = [BIG[n] for _, n in stage]
        bufs = [_cast_place(w[n], i, BIG[n], chip1, BF16, f"place_{n}_{i}") for (_, n), i in zip(stage, index)]
        if k == 0:
            bufs.append(_cast_place(meta_tokens[None], 0, "col", chip1, F32, "place_meta"))
            bufs.append(_cast_place(conv_pad.reshape(1, -1, conv_pad.shape[2]), 0, "col", chip1, F32, "place_conv_w"))
            kinds += ["col", "col"]
        plan = _gather_plan(kinds, True)
        ss, rs, bufs, tok = _split_start(f"gather_start_{k}", plan, bufs, 3 * len(bufs), deps=token)
        token = (tok,)
        gathers.append((kinds, plan, ss, rs, bufs))
        where.update({key: (k, f) for f, key in enumerate(stage)})

    landed, passed, held = {}, {}, []

    def hand_on(k, deps):
        if k not in passed:
            kinds, plan, ss, rs, bufs = gathers[k]
            to_sibling = _gather_plan(kinds, False)
            ss, rs, bufs, tok = _split_start(f"gather_pass_{k}", to_sibling, bufs, 3 * len(bufs), deps=deps, earlier=(plan, ss, rs))
            passed[k] = (to_sibling, ss, rs, bufs)
            held.append(tok)

    def arrived(k, after):
        if k not in landed:
            hand_on(k, after)
            landed[k] = _split_wait(f"gather_wait_{k}", *passed[k], after)
        return landed[k]

    def weights(layer, name, after):
        k, f = where[(layer, name)]
        full = arrived(k, after)[f][None]
        if name == "mlp_w2" and layer + 1 < n_layers:
            hand_on(where[(layer + 1, _mixer_names(layer + 1)[0])][0], after)
        if layer > 0 and name == _mixer_names(layer)[0]:
            hand_on(where[(layer, "mlp_w1")][0], after)
        tokens = tuple(held)
        held.clear()
        return full, tokens

    first = arrived(0, token)
    P = {n: w[n] for n in ("mix_norm_g", "mlp_norm_g", "final_norm_g", "ev_conv_b", "ev_ln_g", "ev_ln_b", "ev_pool_w",
                           "ev_pool_b", "ev_pool_scale", "od_gnorm_g", "lb_param")}
    P["meta_full"] = first[1]
    P["conv_w_full"] = first[2].reshape(ev_conv_w.shape[0], CONV_ROWS, -1)

    pending, outs = [], {n: None for n in BIG}

    def advance(after, fresh=1):
        tokens, still = [], []
        for pos, st in enumerate(pending):
            if st["phase"] == 1 and pos >= len(pending) - fresh:
                still.append(st)
            elif st["phase"] == 1:
                bufs = _split_wait(f"reduce_wait_{st['tag']}", _reduce_plan, st["ss"], st["rs"], st["bufs"], after)
                halves = [_sum_pieces(ids2, bufs[2 * f], bufs[2 * f + 1], f"sum_{st['tag']}_{f}") for f in range(len(bufs) // 2)]
                ss, rs, halves, tok = _split_start(f"swap_start_{st['tag']}", _swap_plan, halves, len(halves))
                tokens.append(tok)
                still.append(dict(st, phase=2, ss=ss, rs=rs, bufs=halves))
            else:
                grads = _split_wait(f"swap_wait_{st['tag']}", _swap_plan, st["ss"], st["rs"], st["bufs"], after)
                for (n, i), g in zip(st["keys"], grads):
                    outs[n] = _adamw_layer(w[n], m[n], v[n], g.reshape(w[n].shape[1:]), i, outs[n], f"adamw_{n}_{i}")
        pending[:] = still
        return tokens

    def boundary(tag, grads, after):
        tokens = advance(after)
        bufs = []
        for acc in grads.values():
            bufs += [acc, lax.empty((N_DEV - 1,) + acc.shape[2:], BF16)]
        ss, rs, bufs, tok = _split_start(f"reduce_start_{tag}", _reduce_plan, bufs, 7 * len(grads))
        pending.append(dict(phase=1, tag=tag, keys=list(grads), ss=ss, rs=rs, bufs=bufs))
        return tuple(tokens + [tok])

    loss, dh, small = _local_step(x[0], loss_target[0], P, weights, boundary, first_deps=token)

    order = [n for n in WEIGHTS if n not in BIG]
    block = _pack([small[n] for n in order] + [loss])
    ss, rs, bufs, tok = _split_start("small_start", _small_plan, [block, lax.empty((N_DEV,) + block.shape, F32)], N_DEV - 1)
    while pending:
        advance((tok,) + tuple(o[0] for o in outs.values() if o is not None), fresh=0)
    block, land = _split_wait("small_wait", _small_plan, ss, rs, bufs, tuple(outs[n][0] for n in BIG))
    packed = _sum_blocks((4 * lax.axis_index("x") + 2 * lax.axis_index("y") + lax.axis_index("c")).astype(jnp.int32).reshape(1), block, land)
    total = _unpack(packed, [small[n].shape for n in order] + [loss.shape])
    loss_sum = total[-1][0, 0]
    gsmall = dict(zip(order, total[:-1]))
    gsmall["meta_tokens"] = lax.dynamic_slice_in_dim(gsmall["meta_tokens"], chip * meta_tokens.shape[1], meta_tokens.shape[1], 1)
    gsmall["ev_conv_w"] = lax.dynamic_slice_in_dim(gsmall["ev_conv_w"][:, :CONV_WIDTH], chip * ev_conv_w.shape[2], ev_conv_w.shape[2], 2)

    g_out, d_out, m_out, v_out = {}, {}, {}, {}
    for n in WEIGHTS:
        if n in BIG:
            g_out[n], d_out[n], m_out[n], v_out[n] = outs[n]
            continue
        shape = w[n].shape
        g = gsmall[n].reshape(shape)
        cols = shape[-1] if len(shape) > 1 else 128
        two = lambda a: a.reshape(-1, cols)
        d_, m_, v_ = _adamw(two(w[n]), two(g), two(m[n]), two(v[n]), f"adamw_{n}")
        g_out[n], d_out[n], m_out[n], v_out[n] = g, d_.reshape(shape), m_.reshape(shape), v_.reshape(shape)

    grad_x = dh[LEAD:][None]
    return (loss_sum, grad_x, *[g_out[n] for n in WEIGHTS], *[d_out[n] for n in WEIGHTS],
            *[m_out[n] for n in WEIGHTS], *[v_out[n] for n in WEIGHTS])
```

```python
import functools

import jax
import jax.numpy as jnp
from jax import lax
from jax.experimental import pallas as pl
from jax.experimental.pallas import tpu as pltpu

F32 = jnp.float32
BF16 = jnp.bfloat16
SDS = jax.ShapeDtypeStruct
MESH = pl.DeviceIdType.MESH
ANY_SPEC = pl.BlockSpec(memory_space=pl.ANY)

N_META = 16
CHUNK = 64
LEAD = CHUNK
PAD = LEAD - N_META
CONV_WIDTH = 31
CONV_ROWS = 32
POOL_WINDOWS = (2, 4, 8, 16)
HEAD_DIM = 128
SUB = 16
EXP_CAP = 80.0
EPS = 1e-6
ADAM_LR = 0.001
ADAM_B1 = 0.9
ADAM_B2 = 0.999
ADAM_EPS = 1e-08
ADAM_WD = 0.01
ADAM_STEP = 10
N_CHIPS = 4
VMEM_LIMIT = 52 << 20
MM_VMEM_BUDGET = 44 << 20


def _params(*sem):
    return pltpu.CompilerParams(dimension_semantics=sem if sem else None, vmem_limit_bytes=VMEM_LIMIT)


def _tile(n, target, unit=CHUNK):
    best = None
    for t in range(unit, min(n, target) + 1, unit):
        if n % t == 0:
            best = t
    assert best is not None, (n, target, unit)
    return best


def _ctile(n, target=512):
    for t in (512, 384, 256, 128):
        if t <= target and n % t == 0:
            return t
    raise ValueError(n)


def _mm_tiles(M, N, per_row, per_col, per_elem):
    best = None
    for tn in (512, 384, 256, 128):
        if N % tn:
            continue
        for tm in sorted((d for d in range(16, M + 1, 16) if M % d == 0), reverse=True):
            if 2 * (tm * per_row + tn * per_col + tm * tn * per_elem) <= MM_VMEM_BUDGET:
                if best is None or tm * tn > best[0] * best[1]:
                    best = (tm, tn)
                break
    assert best is not None, (M, N)
    return best


def _sigmoid(x):
    return 1.0 / (1.0 + jnp.exp(-x))


def _mult(v, m):
    return v if isinstance(v, int) else pl.multiple_of(v, m)


def _row_ids(shape, base):
    return lax.broadcasted_iota(jnp.int32, shape, 0) + base


def _cast_place(w3, layer, kind, chip1, dtype, name):
    _, ks, ns = w3.shape
    tr = _tile(ks, 512, 16)
    full = (ks, ns * N_CHIPS) if kind == "col" else (ks * N_CHIPS, ns)

    def body(chip_ref, w_ref, o_ref):
        del chip_ref
        o_ref[...] = w_ref[...].astype(dtype)

    omap = (lambda i, chip: (i, chip[0])) if kind == "col" else (lambda i, chip: (chip[0] * (ks // tr) + i, 0))
    return pl.pallas_call(
        body,
        grid_spec=pltpu.PrefetchScalarGridSpec(
            num_scalar_prefetch=1, grid=(ks // tr,),
            in_specs=[pl.BlockSpec((None, tr, ns), lambda i, chip: (layer, i, 0))],
            out_specs=pl.BlockSpec((tr, ns), omap)),
        out_shape=SDS(full, dtype), name=name, compiler_params=_params("parallel"))(chip1, w3)


def _rms_fwd(h, g3, layer, name, deps=()):
    T, D = h.shape
    tm = _tile(T, 832)

    def body(h_ref, g_ref, *rest):
        n_ref = rest[-1]
        x = h_ref[...]
        r = lax.rsqrt(jnp.mean(x * x, axis=-1, keepdims=True) + EPS)
        n_ref[...] = ((x * r) * g_ref[...]).astype(BF16)

    return pl.pallas_call(
        body, grid=(T // tm,),
        in_specs=[pl.BlockSpec((tm, D), lambda i: (i, 0)), pl.BlockSpec((None, 1, D), lambda i: (layer, 0, 0))]
        + [ANY_SPEC] * len(deps),
        out_specs=pl.BlockSpec((tm, D), lambda i: (i, 0)), out_shape=SDS((T, D), BF16),
        name=name, compiler_params=_params("parallel"))(h, g3, *deps)


def _final(h, g2, target):
    T, D = h.shape
    tm = _tile(T, 320)
    nsub = tm // CHUNK
    nblk = target.shape[0] // CHUNK

    def body(h_ref, g_ref, *rest):
        t_refs = rest[:nsub]
        dh_ref, dhb_ref, dg_ref, loss_ref = rest[nsub:]
        i = pl.program_id(0)

        @pl.when(i == 0)
        def _():
            dg_ref[...] = jnp.zeros_like(dg_ref)
            loss_ref[...] = jnp.zeros_like(loss_ref)

        g = g_ref[...]
        for q in range(nsub):
            rows = slice(q * CHUNK, (q + 1) * CHUNK)
            x = h_ref[rows, :]
            r = lax.rsqrt(jnp.mean(x * x, axis=-1, keepdims=True) + EPS)
            xh = x * r
            live = jnp.where(i * nsub + q > 0, 1.0, 0.0).astype(F32)
            e = ((xh * g) - t_refs[q][...]) * live
            dy = e * (1.0 / D)
            dxh = dy * g
            dh = r * (dxh - xh * jnp.mean(dxh * xh, axis=-1, keepdims=True))
            dh_ref[rows, :] = dh
            dhb_ref[rows, :] = dh.astype(BF16)
            dg_ref[...] += jnp.sum(dy * xh, axis=0, keepdims=True)
            loss_ref[...] += jnp.sum(e * e) * (0.5 / D)

    row = pl.BlockSpec((tm, D), lambda i: (i, 0))
    t_specs = [pl.BlockSpec((CHUNK, D), functools.partial(lambda i, q: (jnp.clip(i * nsub + q - 1, 0, nblk - 1), 0), q=q))
               for q in range(nsub)]
    return pl.pallas_call(
        body, grid=(T // tm,),
        in_specs=[row, pl.BlockSpec((1, D), lambda i: (0, 0))] + t_specs,
        out_specs=[row, row, pl.BlockSpec((1, D), lambda i: (0, 0)), pl.BlockSpec((1, 128), lambda i: (0, 0))],
        out_shape=[SDS((T, D), F32), SDS((T, D), BF16), SDS((1, D), F32), SDS((1, 128), F32)],
        name="final_loss", compiler_params=_params("arbitrary"))(h, g2, *([target] * nsub))


def _mm_nn(a, w3, layer, name, res=None, relu=False, square=False, deps=()):
    M, K = a.shape
    N = w3.shape[2]
    tm, tn = _mm_tiles(M, N, 2 * K, 2 * K, (2 if relu else 4) + (4 if res is not None else 0))

    def body(*refs):
        lhs = refs[0][...]
        acc = jnp.dot(lhs * lhs if square else lhs, refs[1][...], preferred_element_type=F32)
        if res is not None:
            acc = acc + refs[2][...]
        refs[-1][...] = jnp.maximum(acc, 0.0).astype(BF16) if relu else acc

    in_specs = [pl.BlockSpec((tm, K), lambda i, j: (i, 0)), pl.BlockSpec((None, K, tn), lambda i, j: (layer, 0, j))]
    args = [a, w3]
    tile = pl.BlockSpec((tm, tn), lambda i, j: (i, j))
    if res is not None:
        in_specs.append(tile)
        args.append(res)
    in_specs += [ANY_SPEC] * len(deps)
    args += list(deps)
    return pl.pallas_call(
        body, grid=(M // tm, N // tn), in_specs=in_specs, out_specs=tile,
        out_shape=SDS((M, N), BF16 if relu else F32),
        name=name, compiler_params=_params("parallel", "parallel"))(*args)


def _row_tile(M, per_row, fixed):
    for tm in sorted((d for d in range(16, M + 1, 16) if M % d == 0), reverse=True):
        if 2 * (tm * per_row + fixed) <= MM_VMEM_BUDGET:
            return tm
    raise ValueError((M, per_row, fixed))


def _mm_nn_norm(a, w3, layer, res, g3, glayer, name, square=False, deps=()):
    M, K = a.shape
    D = w3.shape[2]
    tm = _row_tile(M, 2 * K + 10 * D, 2 * K * D)

    def body(a_ref, w_ref, r_ref, g_ref, *rest):
        h_ref, n_ref = rest[-2:]
        lhs = a_ref[...]
        x = r_ref[...] + jnp.dot(lhs * lhs if square else lhs, w_ref[...], preferred_element_type=F32)
        h_ref[...] = x
        r = lax.rsqrt(jnp.mean(x * x, axis=-1, keepdims=True) + EPS)
        n_ref[...] = ((x * r) * g_ref[...]).astype(BF16)

    row = pl.BlockSpec((tm, D), lambda i: (i, 0))
    return pl.pallas_call(
        body, grid=(M // tm,),
        in_specs=[pl.BlockSpec((tm, K), lambda i: (i, 0)), pl.BlockSpec((None, K, D), lambda i: (layer, 0, 0)), row,
                  pl.BlockSpec((None, 1, D), lambda i: (glayer, 0, 0))] + [ANY_SPEC] * len(deps),
        out_specs=[row, row], out_shape=[SDS((M, D), F32), SDS((M, D), BF16)],
        name=name, compiler_params=_params("parallel"))(a, w3, res, g3, *deps)


def _tail_fwd(y, w_out, res, mlp_g3, layer, w1, w2, next_g3, name, deps=()):
    M, K = y.shape
    D = w_out.shape[2]
    F = w1.shape[2]
    hb = _ctile(F)
    more = next_g3 is not None
    tm = _row_tile(M, 2 * K + 18 * D + (2 * D if more else 0) + 2 * F, 2 * K * D + 2 * D * F)

    def body(y_ref, wo_ref, res_ref, g_ref, w1_ref, w2_ref, *rest):
        outs = rest[-5:] if more else rest[-4:]
        h1 = res_ref[...] + jnp.dot(y_ref[...], wo_ref[...], preferred_element_type=F32)
        outs[0][...] = h1
        n2 = ((h1 * lax.rsqrt(jnp.mean(h1 * h1, axis=-1, keepdims=True) + EPS)) * g_ref[...]).astype(BF16)
        outs[1][...] = n2
        acc = h1
        for jb in range(F // hb):
            cols = slice(jb * hb, (jb + 1) * hb)
            r = jnp.maximum(jnp.dot(n2, w1_ref[:, cols], preferred_element_type=F32), 0.0).astype(BF16)
            outs[-1][:, cols] = r
            acc = acc + jnp.dot(r * r, w2_ref[cols, :], preferred_element_type=F32)
        outs[2][...] = acc
        if more:
            outs[3][...] = ((acc * lax.rsqrt(jnp.mean(acc * acc, axis=-1, keepdims=True) + EPS)) * rest[0][...]).astype(BF16)

    row = pl.BlockSpec((tm, D), lambda i: (i, 0))
    once = dict(pipeline_mode=pl.Buffered(1))
    in_specs = [pl.BlockSpec((tm, K), lambda i: (i, 0)), pl.BlockSpec((None, K, D), lambda i: (0, 0, 0), **once), row,
                pl.BlockSpec((None, 1, D), lambda i: (layer, 0, 0)),
                pl.BlockSpec((None, D, F), lambda i: (0, 0, 0), **once), pl.BlockSpec((None, F, D), lambda i: (0, 0, 0), **once)]
    args = [y, w_out, res, mlp_g3, w1, w2]
    out_specs, out_shape = [row, row, row], [SDS((M, D), F32), SDS((M, D), BF16), SDS((M, D), F32)]
    if more:
        in_specs.append(pl.BlockSpec((None, 1, D), lambda i: (layer + 1, 0, 0)))
        args.append(next_g3)
        out_specs.append(row)
        out_shape.append(SDS((M, D), BF16))
    out_specs.append(pl.BlockSpec((tm, F), lambda i: (i, 0)))
    out_shape.append(SDS((M, F), BF16))
    in_specs += [ANY_SPEC] * len(deps)
    args += list(deps)
    return pl.pallas_call(
        body, grid=(M // tm,), in_specs=in_specs, out_specs=out_specs, out_shape=out_shape,
        name=name, compiler_params=_params("parallel"))(*args)


def _mlp_bwd(dhb, relu, w1, w2, w_out, h, g3, glayer, dh_in, name, deps=()):
    M, D = dhb.shape
    F = w1.shape[2]
    K = w_out.shape[1]
    hb = _ctile(F)
    tm = _row_tile(M, 16 * D + 4 * F + 4 * K, 2 * D * F + K * D)

    def body(dy_ref, r_ref, w1_ref, w2_ref, wo_ref, h_ref, g_ref, dhi_ref, *rest):
        dz_ref, dh_ref, dhb_ref, dg_ref, dyo_ref = rest[-5:]
        dy = dy_ref[...]
        dn = jnp.zeros((tm, D), F32)
        for jb in range(F // hb):
            cols = slice(jb * hb, (jb + 1) * hb)
            dact = lax.dot_general(dy, w2_ref[cols, :], (((1,), (1,)), ((), ())), preferred_element_type=F32)
            dz = (dact * (2.0 * r_ref[:, cols].astype(F32))).astype(BF16)
            dz_ref[:, cols] = dz
            dn = dn + lax.dot_general(dz, w1_ref[:, cols], (((1,), (1,)), ((), ())), preferred_element_type=F32)
        x = h_ref[...]
        r = lax.rsqrt(jnp.mean(x * x, axis=-1, keepdims=True) + EPS)
        xh = x * r
        dxh = dn * g_ref[...]
        dh = dhi_ref[...] + r * (dxh - xh * jnp.mean(dxh * xh, axis=-1, keepdims=True))
        dh_ref[...] = dh
        dhb = dh.astype(BF16)
        dhb_ref[...] = dhb
        dyo_ref[...] = lax.dot_general(dhb, wo_ref[...], (((1,), (1,)), ((), ())), preferred_element_type=F32)

        @pl.when(pl.program_id(0) == 0)
        def _():
            dg_ref[...] = jnp.zeros_like(dg_ref)

        dg_ref[...] += jnp.sum(dn * xh, axis=0, keepdims=True)

    row = pl.BlockSpec((tm, D), lambda i: (i, 0))
    wide = pl.BlockSpec((tm, F), lambda i: (i, 0))
    once = dict(pipeline_mode=pl.Buffered(1))
    return pl.pallas_call(
        body, grid=(M // tm,),
        in_specs=[row, wide, pl.BlockSpec((None, D, F), lambda i: (0, 0, 0), **once),
                  pl.BlockSpec((None, F, D), lambda i: (0, 0, 0), **once),
                  pl.BlockSpec((None, K, D), lambda i: (0, 0, 0), **once), row,
                  pl.BlockSpec((None, 1, D), lambda i: (glayer, 0, 0)), row] + [ANY_SPEC] * len(deps),
        out_specs=[wide, row, row, pl.BlockSpec((1, D), lambda i: (0, 0)), pl.BlockSpec((tm, K), lambda i: (i, 0))],
        out_shape=[SDS((M, F), BF16), SDS((M, D), F32), SDS((M, D), BF16), SDS((1, D), F32), SDS((M, K), F32)],
        name=name, compiler_params=_params("arbitrary"))(dhb, relu, w1, w2, w_out, h, g3, dh_in, *deps)


def _mm_nt_norm(dy, w3, layer, h, g3, glayer, dh_in, name, deps=()):
    M, N = dy.shape
    D = w3.shape[1]
    tm = _row_tile(M, 2 * N + 14 * D, 2 * N * D)

    def body(dy_ref, w_ref, h_ref, g_ref, dhi_ref, *rest):
        dh_ref, dhb_ref, dg_ref = rest[-3:]
        dn = lax.dot_general(dy_ref[...], w_ref[...], (((1,), (1,)), ((), ())), preferred_element_type=F32)
        x = h_ref[...]
        r = lax.rsqrt(jnp.mean(x * x, axis=-1, keepdims=True) + EPS)
        xh = x * r
        dxh = dn * g_ref[...]
        dh = dhi_ref[...] + r * (dxh - xh * jnp.mean(dxh * xh, axis=-1, keepdims=True))
        dh_ref[...] = dh
        dhb_ref[...] = dh.astype(BF16)

        @pl.when(pl.program_id(0) == 0)
        def _():
            dg_ref[...] = jnp.zeros_like(dg_ref)

        dg_ref[...] += jnp.sum(dn * xh, axis=0, keepdims=True)

    row = pl.BlockSpec((tm, D), lambda i: (i, 0))
    return pl.pallas_call(
        body, grid=(M // tm,),
        in_specs=[pl.BlockSpec((tm, N), lambda i: (i, 0)), pl.BlockSpec((None, D, N), lambda i: (layer, 0, 0)), row,
                  pl.BlockSpec((None, 1, D), lambda i: (glayer, 0, 0)), row] + [ANY_SPEC] * len(deps),
        out_specs=[row, row, pl.BlockSpec((1, D), lambda i: (0, 0))],
        out_shape=[SDS((M, D), F32), SDS((M, D), BF16), SDS((1, D), F32)],
        name=name, compiler_params=_params("arbitrary"))(dy, w3, h, g3, dh_in, *deps)


def _fam_dims(kind, K, N):
    return (K // 2, N // N_CHIPS) if kind == "col" else (K // (2 * N_CHIPS), N)


def _mm_tn(x, dy, kind, name, square=False, deps=()):
    M, K = x.shape
    N = dy.shape[1]
    nr, nc = _fam_dims(kind, K, N)

    def body(x_ref, dy_ref, *rest):
        o_ref = rest[-1]
        lhs = x_ref[...]
        res = lax.dot_general(lhs * lhs if square else lhs, dy_ref[...], (((0,), (0,)), ((), ())), preferred_element_type=F32)
        o_ref[...] = res.astype(BF16).reshape(o_ref.shape)

    if kind == "col":
        tn = _ctile(nc)
        ct = nc // tn
        grid = (N // tn,)
        in_specs = [pl.BlockSpec((M, K), lambda j: (0, 0)), pl.BlockSpec((M, tn), lambda j: (0, j))]
        out_spec = pl.BlockSpec((2, None, nr, tn), lambda j: (0, j // ct, 0, j % ct))
    else:
        grid = (N_CHIPS,)
        in_specs = [pl.BlockSpec((M, 2 * nr), lambda i: (0, i)), pl.BlockSpec((M, N), lambda i: (0, 0))]
        out_spec = pl.BlockSpec((2, None, nr, N), lambda i: (0, i, 0, 0))
    return pl.pallas_call(
        body, grid=grid, in_specs=in_specs + [ANY_SPEC] * len(deps), out_specs=out_spec,
        out_shape=SDS((2, N_CHIPS, nr, nc), BF16), name=name, compiler_params=_params("parallel"))(x, dy, *deps)


C_EVEN = 512


def _live(rows, base, total):
    r = _row_ids((rows, 1), base)
    return jnp.logical_and(r >= PAD, r < total).astype(F32)


def _conv_taps(win, w_ref, ls, acc, flip):
    for b in range(8):
        rb = win if b == 0 else pltpu.roll(win, 96 - b, 0)
        for a in range(5):
            o = 8 * a + b
            tap = (30 - o) if flip else (o - 2)
            if 0 <= tap < CONV_WIDTH:
                acc = acc + w_ref[pl.ds(tap, 1), ls] * rb[8 * a:8 * a + CHUNK]
    return acc


def _window_sum(win, levels, forward):
    s = win
    n = win.shape[0]
    for k in range(levels):
        step = 1 << k
        s = s + pltpu.roll(s, (n - step) if forward else step, 0)
    return s


def _pool_count(base, g):
    pos = _row_ids((CHUNK, 1), base) - PAD
    return jnp.clip(pos + 1, 1, POOL_WINDOWS[g]).astype(F32)


def _even_fwd(u, cw3, cb3, lg3, lb3, pw4, pb3, ps3, j, name):
    T = u.shape[0]
    C = C_EVEN
    tm = _tile(T, 320)
    nch = tm // CHUNK
    nblk = T // CHUNK

    def body(u_ref, up_ref, cw_ref, cb_ref, lg_ref, lb_ref, pw_ref, pb_ref, ps_ref, o_ref, yc_ref, a_s, p_s, yc_s):
        row0 = pl.program_id(0) * tm
        up = up_ref[...]
        lp = _live(CHUNK, row0 - CHUNK, T)
        a_s[0:CHUNK, :] = up[:, 0:C] * _sigmoid(up[:, C:2 * C]) * lp
        p_s[0:CHUNK, :] = up[:, 2 * C:3 * C] * lp

        def stage(c, _):
            rs = _mult(c * CHUNK, CHUNK)
            lv = _live(CHUNK, row0 + rs, T)
            a_s[pl.ds(rs + CHUNK, CHUNK), :] = u_ref[pl.ds(rs, CHUNK), 0:C] * _sigmoid(u_ref[pl.ds(rs, CHUNK), C:2 * C]) * lv
            p_s[pl.ds(rs + CHUNK, CHUNK), :] = u_ref[pl.ds(rs, CHUNK), 2 * C:3 * C] * lv
            return 0

        for c in range(nch):
            stage(c, 0)

        def chunk(c, _):
            rs = _mult(c * CHUNK, CHUNK)
            lv = _live(CHUNK, row0 + rs, T)
            for cb in range(4):
                ls = slice(cb * 128, (cb + 1) * 128)
                win = a_s[pl.ds(_mult(rs + 32, 32), 96), ls]
                acc = jnp.broadcast_to(cb_ref[:, ls], (CHUNK, 128))
                yc_s[:, ls] = _conv_taps(win, cw_ref, ls, acc, False)
            y = yc_s[...]
            yc_ref[pl.ds(rs, CHUNK), :] = y
            xc = y - jnp.mean(y, axis=-1, keepdims=True)
            yn = xc * lax.rsqrt(jnp.mean(xc * xc, axis=-1, keepdims=True) + EPS) * lg_ref[...] + lb_ref[...]
            o_ref[pl.ds(rs, CHUNK), 0:C] = (yn * _sigmoid(yn) * lv).astype(BF16)
            for g in range(4):
                ls = slice(g * 128, (g + 1) * 128)
                win = p_s[pl.ds(_mult(rs + 48, 16), 80), ls]
                s = _window_sum(win, g + 1, False)
                d = s[16:80] / _pool_count(row0 + rs, g) - win[16:80]
                yv = jnp.dot(d.astype(BF16), pw_ref[g].astype(BF16), preferred_element_type=F32) + pb_ref[:, ls]
                o_ref[pl.ds(rs, CHUNK), C + g * 128:C + (g + 1) * 128] = (yv * ps_ref[:, ls] * lv).astype(BF16)
            return 0

        for c in range(nch):
            chunk(c, 0)

    vec = pl.BlockSpec((None, 1, C), lambda i: (j, 0, 0))
    return pl.pallas_call(
        body, grid=(T // tm,),
        in_specs=[pl.BlockSpec((tm, 3 * C), lambda i: (i, 0)),
                  pl.BlockSpec((CHUNK, 3 * C), lambda i: (jnp.maximum(i * nch - 1, 0), 0)),
                  pl.BlockSpec((None, CONV_ROWS, C), lambda i: (j, 0, 0)), vec, vec, vec,
                  pl.BlockSpec((None, 4, 128, 128), lambda i: (j, 0, 0, 0)), vec, vec],
        out_specs=[pl.BlockSpec((tm, 2 * C), lambda i: (i, 0)), pl.BlockSpec((tm, C), lambda i: (i, 0))],
        out_shape=[SDS((T, 2 * C), BF16), SDS((T, C), F32)],
        scratch_shapes=[pltpu.VMEM((tm + CHUNK, C), F32), pltpu.VMEM((tm + CHUNK, C), F32), pltpu.VMEM((CHUNK, C), F32)],
        name=name, compiler_params=_params("parallel"))(u, u, cw3, cb3, lg3, lb3, pw4, pb3, ps3)


def _even_bwd(u, yc, dy, cw3, cb3, lg3, lb3, pw4, pb3, ps3, j, name):
    T = u.shape[0]
    C = C_EVEN
    tm = _tile(T, 320)
    nch = tm // CHUNK
    nblk = T // CHUNK
    ntile = T // tm

    def body(u_ref, up_ref, un_ref, yc_ref, ycn_ref, dy_ref, dyn_ref, cw_ref, cb_ref, lg_ref, lb_ref, pw_ref, pb_ref, ps_ref,
             du_ref, dcw_ref, dcb_ref, dlg_ref, dlb_ref, dpw_ref, dpb_ref, dps_ref,
             a_s, p_s, dy_s, dyc_s, dd_s, ddc_s, dw_s):
        i = pl.program_id(0)
        row0 = i * tm

        @pl.when(i == 0)
        def _():
            for ref in (dcb_ref, dlg_ref, dlb_ref, dpw_ref, dpb_ref, dps_ref, dw_s):
                ref[...] = jnp.zeros_like(ref)

        up = up_ref[...]
        lp = _live(CHUNK, row0 - CHUNK, T)
        a_s[0:CHUNK, :] = up[:, 0:C] * _sigmoid(up[:, C:2 * C]) * lp
        p_s[0:CHUNK, :] = up[:, 2 * C:3 * C] * lp
        ln_ = _live(CHUNK, row0 + tm, T)
        p_s[tm + CHUNK:tm + 2 * CHUNK, :] = un_ref[:, 2 * C:3 * C] * ln_
        dy_s[tm:tm + CHUNK, :] = dyn_ref[...] * ln_
        dyc_s[tm + CHUNK:tm + CHUNK + 32, :] = jnp.zeros((32, C), F32)

        def stage(c, _):
            rs = _mult(c * CHUNK, CHUNK)
            lv = _live(CHUNK, row0 + rs, T)
            a_s[pl.ds(rs + CHUNK, CHUNK), :] = u_ref[pl.ds(rs, CHUNK), 0:C] * _sigmoid(u_ref[pl.ds(rs, CHUNK), C:2 * C]) * lv
            p_s[pl.ds(rs + CHUNK, CHUNK), :] = u_ref[pl.ds(rs, CHUNK), 2 * C:3 * C] * lv
            dy_s[pl.ds(rs, CHUNK), :] = dy_ref[pl.ds(rs, CHUNK), :] * lv
            return 0

        for c in range(nch):
            stage(c, 0)

        def first(rs, y, own):
            xc = y - jnp.mean(y, axis=-1, keepdims=True)
            rstd = lax.rsqrt(jnp.mean(xc * xc, axis=-1, keepdims=True) + EPS)
            xh = xc * rstd
            yn = xh * lg_ref[...] + lb_ref[...]
            sg = _sigmoid(yn)
            dyn = dy_s[pl.ds(rs, CHUNK), 0:C] * (sg * (1.0 + yn * (1.0 - sg)))
            dlg_ref[...] += jnp.sum(dyn * xh, axis=0, keepdims=True) * own
            dlb_ref[...] += jnp.sum(dyn, axis=0, keepdims=True) * own
            dxh = dyn * lg_ref[...]
            dyc = rstd * (dxh - jnp.mean(dxh, axis=-1, keepdims=True) - xh * jnp.mean(dxh * xh, axis=-1, keepdims=True))
            dyc_s[pl.ds(rs, CHUNK), :] = dyc
            dcb_ref[...] += jnp.sum(dyc, axis=0, keepdims=True) * own
            for g in range(4):
                ls = slice(g * 128, (g + 1) * 128)
                win = p_s[pl.ds(rs + 48, 80), ls]
                s = _window_sum(win, g + 1, False)
                cnt = _pool_count(row0 + rs, g)
                d = (s[16:80] / cnt - win[16:80]).astype(BF16)
                w = pw_ref[g].astype(BF16)
                pre = jnp.dot(d, w, preferred_element_type=F32) + pb_ref[:, ls]
                dyb = dy_s[pl.ds(rs, CHUNK), C + g * 128:C + (g + 1) * 128]
                dpre = dyb * ps_ref[:, ls]
                dps_ref[:, ls] += jnp.sum(dyb * pre, axis=0, keepdims=True) * own
                dpb_ref[:, ls] += jnp.sum(dpre, axis=0, keepdims=True) * own
                dpre_b = (dpre * own).astype(BF16)
                dpw_ref[g] += lax.dot_general(d, dpre_b, (((0,), (0,)), ((), ())), preferred_element_type=F32)
                dd = lax.dot_general(dpre.astype(BF16), w, (((1,), (1,)), ((), ())), preferred_element_type=F32)
                dd_s[pl.ds(rs, CHUNK), ls] = dd
                ddc_s[pl.ds(rs, CHUNK), ls] = dd / cnt

        def first_in_tile(c, _):
            rs = _mult(c * CHUNK, CHUNK)
            first(rs, yc_ref[pl.ds(rs, CHUNK), :], 1.0)
            return 0

        for c in range(nch):
            first_in_tile(c, 0)
        first(tm, ycn_ref[...], 0.0)
        ddc_s[tm + CHUNK:tm + CHUNK + 16, :] = jnp.zeros((16, C), F32)

        def second(c, _):
            rs = _mult(c * CHUNK, CHUNK)
            lv = _live(CHUNK, row0 + rs, T)
            for cb in range(4):
                ls = slice(cb * 128, (cb + 1) * 128)
                wd = dyc_s[pl.ds(rs, 96), ls]
                da = _conv_taps(wd, cw_ref, ls, jnp.zeros((CHUNK, 128), F32), True)
                wa = a_s[pl.ds(_mult(rs + 32, 32), 96), ls]
                dyc = dyc_s[pl.ds(rs, CHUNK), ls]
                for b in range(8):
                    rb = wa if b == 0 else pltpu.roll(wa, 96 - b, 0)
                    for a in range(5):
                        tap = 8 * a + b - 2
                        if 0 <= tap < CONV_WIDTH:
                            prod = dyc * rb[8 * a:8 * a + CHUNK]
                            part = prod[0:8]
                            for q in range(1, 8):
                                part = part + prod[8 * q:8 * q + 8]
                            dw_s[8 * tap:8 * tap + 8, ls] += part
                val = u_ref[pl.ds(rs, CHUNK), ls]
                sg = _sigmoid(u_ref[pl.ds(rs, CHUNK), C + cb * 128:C + (cb + 1) * 128])
                du_ref[pl.ds(rs, CHUNK), ls] = (da * sg * lv).astype(BF16)
                du_ref[pl.ds(rs, CHUNK), C + cb * 128:C + (cb + 1) * 128] = (da * val * sg * (1.0 - sg) * lv).astype(BF16)
            for g in range(4):
                ls = slice(g * 128, (g + 1) * 128)
                z = _window_sum(ddc_s[pl.ds(rs, 80), ls], g + 1, True)
                dpin = (z[0:CHUNK] - dd_s[pl.ds(rs, CHUNK), ls]) * lv
                du_ref[pl.ds(rs, CHUNK), 2 * C + g * 128:2 * C + (g + 1) * 128] = dpin.astype(BF16)
            return 0

        for c in range(nch):
            second(c, 0)

        @pl.when(i == ntile - 1)
        def _():
            for tap in range(CONV_WIDTH):
                dcw_ref[tap:tap + 1, :] = jnp.sum(dw_s[8 * tap:8 * tap + 8, :], axis=0, keepdims=True)
            dcw_ref[CONV_WIDTH:CONV_ROWS, :] = jnp.zeros((CONV_ROWS - CONV_WIDTH, C), F32)

    vec = pl.BlockSpec((None, 1, C), lambda i: (j, 0, 0))
    ovec = pl.BlockSpec((1, C), lambda i: (0, 0))
    return pl.pallas_call(
        body, grid=(ntile,),
        in_specs=[pl.BlockSpec((tm, 3 * C), lambda i: (i, 0)),
                  pl.BlockSpec((CHUNK, 3 * C), lambda i: (jnp.maximum(i * nch - 1, 0), 0)),
                  pl.BlockSpec((CHUNK, 3 * C), lambda i: (jnp.minimum((i + 1) * nch, nblk - 1), 0)),
                  pl.BlockSpec((tm, C), lambda i: (i, 0)),
                  pl.BlockSpec((CHUNK, C), lambda i: (jnp.minimum((i + 1) * nch, nblk - 1), 0)),
                  pl.BlockSpec((tm, 2 * C), lambda i: (i, 0)),
                  pl.BlockSpec((CHUNK, 2 * C), lambda i: (jnp.minimum((i + 1) * nch, nblk - 1), 0)),
                  pl.BlockSpec((None, CONV_ROWS, C), lambda i: (j, 0, 0)), vec, vec, vec,
                  pl.BlockSpec((None, 4, 128, 128), lambda i: (j, 0, 0, 0)), vec, vec],
        out_specs=[pl.BlockSpec((tm, 3 * C), lambda i: (i, 0)), pl.BlockSpec((CONV_ROWS, C), lambda i: (0, 0)),
                   ovec, ovec, ovec, pl.BlockSpec((4, 128, 128), lambda i: (0, 0, 0)), ovec, ovec],
        out_shape=[SDS((T, 3 * C), BF16), SDS((CONV_ROWS, C), F32), SDS((1, C), F32), SDS((1, C), F32), SDS((1, C), F32),
                   SDS((4, 128, 128), F32), SDS((1, C), F32), SDS((1, C), F32)],
        scratch_shapes=[pltpu.VMEM((tm + CHUNK, C), F32), pltpu.VMEM((tm + 2 * CHUNK, C), F32),
                        pltpu.VMEM((tm + CHUNK, 2 * C), F32),
                        pltpu.VMEM((tm + CHUNK + 32, C), F32), pltpu.VMEM((tm + CHUNK, C), F32),
                        pltpu.VMEM((tm + CHUNK + 16, C), F32), pltpu.VMEM((8 * CONV_ROWS, C), F32)],
        name=name, compiler_params=_params("arbitrary"))(u, u, u, yc, yc, dy, dy, cw3, cb3, lg3, lb3, pw4, pb3, ps3)


HI = lax.Precision.HIGHEST


def _dot_nt(a, b):
    return lax.dot_general(a, b, (((1,), (1,)), ((), ())), preferred_element_type=F32)


def _dot_tn(a, b):
    return lax.dot_general(a, b, (((0,), (0,)), ((), ())), preferred_element_type=F32)


def _tri(lower):
    r = lax.broadcasted_iota(jnp.int32, (CHUNK, CHUNK), 0)
    c = lax.broadcasted_iota(jnp.int32, (CHUNK, CHUNK), 1)
    return jnp.where((c <= r) if lower else (c >= r), 1.0, 0.0).astype(F32)


def _hgrn_gates(u_ref, lb_ref, h, D, lv):
    ls = slice(h * HEAD_DIM, (h + 1) * HEAD_DIM)
    qraw = u_ref[:, ls]
    fraw = u_ref[:, D + h * HEAD_DIM:D + (h + 1) * HEAD_DIM]
    v = u_ref[:, 2 * D + h * HEAD_DIM:2 * D + (h + 1) * HEAD_DIM] * lv
    lbv = lb_ref[:, ls]
    sig = _sigmoid(fraw)
    forget = lbv + (1.0 - lbv) * sig
    logf = jnp.log(forget) * lv
    k = (1.0 - forget) * lv
    qsig = _sigmoid(qraw)
    q = qraw * qsig * lv
    return q, k, v, logf, (qraw, qsig, sig, forget, lbv)


def _sub_parts(q, k, b, b_s, I):
    rows = slice(SUB * I, SUB * (I + 1))
    rho = jnp.zeros((1, HEAD_DIM), F32) if I == 0 else b_s[SUB * I - 1:SUB * I, :]
    eI = jnp.exp(b[rows] - rho)
    EI = jnp.exp(jnp.minimum(rho - b, EXP_CAP))
    causal = (lax.broadcasted_iota(jnp.int32, (SUB, CHUNK), 1)
              <= lax.broadcasted_iota(jnp.int32, (SUB, CHUNK), 0) + SUB * I)
    return rows, q[rows] * eI, k * EI, eI, EI, causal


def _chunks_per_step(NC):
    for n in (5, 4, 3, 2):
        if NC % n == 0:
            return n
    return 1


def _hgrn_fwd(u, lb3, layer, gn3, j, name):
    T = u.shape[0]
    D = u.shape[1] // 4
    H = D // HEAD_DIM
    NC = T // CHUNK
    CH = _chunks_per_step(NC)
    R = CH * CHUNK

    def body(u_ref, lb_ref, gn_ref, y_ref, o_ref, sall_ref, st_s, b_s, lf_s, q_s, k_s):
        n = pl.program_id(0)

        @pl.when(n == 0)
        def _():
            st_s[...] = jnp.zeros_like(st_s)

        heads = range(H)
        cols = [slice(h * HEAD_DIM, (h + 1) * HEAD_DIM) for h in heads]
        rows = [slice(c * CHUNK, (c + 1) * CHUNK) for c in range(CH)]
        vb = {}
        for c in range(CH):
            lv = _live(CHUNK, (n * CH + c) * CHUNK, T)
            for h in heads:
                q, k, v, logf, _ = _hgrn_gates(u_ref.at[rows[c]], lb_ref, h, D, lv)
                q_s[rows[c], cols[h]] = q
                k_s[rows[c], cols[h]] = k
                lf_s[rows[c], cols[h]] = logf
                vb[c, h] = v.astype(BF16)
        for c in range(CH):
            b_s[rows[c], :] = jnp.dot(_tri(True), lf_s[rows[c], :], precision=HI, preferred_element_type=F32)
        ops = {}
        for c in range(CH):
            for h in heads:
                b_h = b_s.at[rows[c], cols[h]]
                b = b_h[...]
                q = q_s[rows[c], cols[h]]
                k = k_s[rows[c], cols[h]]
                blast = b_h[CHUNK - 1:CHUNK, :]
                qh = (q * jnp.exp(b)).astype(BF16)
                kt = (k * jnp.exp(blast - b)).astype(BF16)
                subs = []
                for I in range(CHUNK // SUB):
                    _, qI, KI, _, _, causal = _sub_parts(q, k, b, b_h, I)
                    subs.append((qI.astype(BF16), KI.astype(BF16), causal))
                ops[c, h] = (qh, kt, jnp.exp(blast), subs)
        mm = {}
        for h in heads:
            st = st_s[h]
            for c in range(CH):
                qh, kt, eblast, subs = ops[c, h]
                sall_ref[c, h] = st
                o_inter = _dot_nt(qh, st.astype(BF16))
                st = st * eblast + _dot_tn(vb[c, h], kt)
                mm[c, h] = (o_inter, [_dot_nt(qI, KI) for qI, KI, _ in subs])
            st_s[h] = st
        for c in range(CH):
            for h in heads:
                o_inter, ps = mm[c, h]
                p = jnp.concatenate([jnp.where(m, x, 0.0) for x, (_, _, m) in zip(ps, ops[c, h][3])], axis=0).astype(BF16)
                o = o_inter + jnp.dot(p, vb[c, h], preferred_element_type=F32)
                o_ref[rows[c], cols[h]] = o
                graw = u_ref[rows[c], 3 * D + h * HEAD_DIM:3 * D + (h + 1) * HEAD_DIM]
                r = lax.rsqrt(jnp.mean(o * o, axis=-1, keepdims=True) + EPS)
                y_ref[rows[c], cols[h]] = (((o * r) * gn_ref[...]) * (graw * _sigmoid(graw))).astype(BF16)

    return pl.pallas_call(
        body, grid=(NC // CH,),
        in_specs=[pl.BlockSpec((R, 4 * D), lambda n: (n, 0)),
                  pl.BlockSpec((None, 1, D), lambda n: (layer, 0, 0)),
                  pl.BlockSpec((None, 1, HEAD_DIM), lambda n: (j, 0, 0))],
        out_specs=[pl.BlockSpec((R, D), lambda n: (n, 0)), pl.BlockSpec((R, D), lambda n: (n, 0)),
                   pl.BlockSpec((CH, H, HEAD_DIM, HEAD_DIM), lambda n: (n, 0, 0, 0))],
        out_shape=[SDS((T, D), BF16), SDS((T, D), F32), SDS((NC, H, HEAD_DIM, HEAD_DIM), F32)],
        scratch_shapes=[pltpu.VMEM((H, HEAD_DIM, HEAD_DIM), F32)] + [pltpu.VMEM((R, D), F32)] * 4,
        name=name, compiler_params=_params("arbitrary"))(u, lb3, gn3)


def _hgrn_bwd(u, o_raw, dy, sall, lb3, layer, gn3, j, name):
    T = u.shape[0]
    D = u.shape[1] // 4
    H = D // HEAD_DIM
    NC = T // CHUNK
    CH = _chunks_per_step(NC)
    R = CH * CHUNK
    NS = NC // CH

    def body(u_ref, o_ref, dy_ref, sall_ref, lb_ref, gn_ref, du_ref, dlb_ref, dgn_ref, dst_s, b_s, lf_s, q_s, k_s, db_s, dk_s):
        step = pl.program_id(0)
        n = NS - 1 - step

        @pl.when(step == 0)
        def _():
            dst_s[...] = jnp.zeros_like(dst_s)
            dlb_ref[...] = jnp.zeros_like(dlb_ref)
            dgn_ref[...] = jnp.zeros_like(dgn_ref)

        last_row = (_row_ids((CHUNK, 1), 0) == CHUNK - 1).astype(F32)
        gn = gn_ref[...]
        heads = range(H)
        chunks = range(CH)
        cols = [slice(h * HEAD_DIM, (h + 1) * HEAD_DIM) for h in heads]
        rows = [slice(c * CHUNK, (c + 1) * CHUNK) for c in chunks]
        lv = [_live(CHUNK, (n * CH + c) * CHUNK, T) for c in chunks]
        vb, dob = {}, {}
        dgn = jnp.zeros((1, HEAD_DIM), F32)
        for c in chunks:
            for h in heads:
                q, k, v, logf, _ = _hgrn_gates(u_ref.at[rows[c]], lb_ref, h, D, lv[c])
                q_s[rows[c], cols[h]] = q
                k_s[rows[c], cols[h]] = k
                lf_s[rows[c], cols[h]] = logf
                vb[c, h] = v.astype(BF16)
                graw = u_ref[rows[c], 3 * D + h * HEAD_DIM:3 * D + (h + 1) * HEAD_DIM]
                gsig = _sigmoid(graw)
                o = o_ref[rows[c], cols[h]]
                r = lax.rsqrt(jnp.mean(o * o, axis=-1, keepdims=True) + EPS)
                xh = o * r
                dyv = dy_ref[rows[c], cols[h]]
                dsg = dyv * (graw * gsig)
                dgn = dgn + jnp.sum(dsg * xh, axis=0, keepdims=True)
                dxh = dsg * gn
                do = r * (dxh - xh * jnp.mean(dxh * xh, axis=-1, keepdims=True))
                dob[c, h] = do.astype(BF16)
                dgraw = dyv * xh * gn * (gsig * (1.0 + graw * (1.0 - gsig)))
                du_ref[rows[c], 3 * D + h * HEAD_DIM:3 * D + (h + 1) * HEAD_DIM] = (dgraw * lv[c]).astype(BF16)
        dgn_ref[...] += dgn
        for c in chunks:
            b_s[rows[c], :] = jnp.dot(_tri(True), lf_s[rows[c], :], precision=HI, preferred_element_type=F32)
        ops = {}
        for c in chunks:
            for h in heads:
                b_h = b_s.at[rows[c], cols[h]]
                b = b_h[...]
                q = q_s[rows[c], cols[h]]
                k = k_s[rows[c], cols[h]]
                blast = b_h[CHUNK - 1:CHUNK, :]
                eb = jnp.exp(b)
                ekb = jnp.exp(blast - b)
                subs = []
                for I in range(CHUNK // SUB):
                    rws, qI, KI, eI, EI, causal = _sub_parts(q, k, b, b_h, I)
                    subs.append((rws, qI.astype(BF16), KI.astype(BF16), eI, EI, causal))
                ops[c, h] = (eb, ekb, jnp.exp(blast), (q * eb).astype(BF16), (k * ekb).astype(BF16), subs)
        mm = {}
        for h in heads:
            dst = dst_s[h]
            for c in reversed(chunks):
                eb, ekb, eblast, qhb, ktb, subs = ops[c, h]
                st = sall_ref[c, h]
                dstb = dst.astype(BF16)
                dv = _dot_nt(ktb, dstb)
                dqh = jnp.dot(dob[c, h], st.astype(BF16), preferred_element_type=F32)
                dkt = jnp.dot(vb[c, h], dstb, preferred_element_type=F32)
                dblast = jnp.sum(dst * st, axis=0, keepdims=True) * eblast
                dst = dst * eblast + _dot_tn(dob[c, h], qhb)
                dp_full = _dot_nt(dob[c, h], vb[c, h])
                ps = [_dot_nt(qIb, KIb) for _, qIb, KIb, _, _, _ in subs]
                mm[c, h] = (dv, dqh, dkt, dblast, dp_full, ps)
            dst_s[h] = dst
        for c in chunks:
            for h in heads:
                eb, ekb, eblast, qhb, ktb, subs = ops[c, h]
                dv, dqh, dkt, dblast, dp_full, ps = mm[c, h]
                p = jnp.concatenate([jnp.where(sub[5], x, 0.0) for x, sub in zip(ps, subs)], axis=0).astype(BF16)
                dv = dv + _dot_tn(p, dob[c, h])
                du_ref[rows[c], 2 * D + h * HEAD_DIM:2 * D + (h + 1) * HEAD_DIM] = (dv * lv[c]).astype(BF16)
                dq = dqh * eb
                db = dqh * qhb.astype(F32)
                tmp = dkt * ktb.astype(F32)
                dk = dkt * ekb
                db = db - tmp
                dblast = dblast + jnp.sum(tmp, axis=0, keepdims=True)
                dq_parts, db_parts = [], []
                for rws, qIb, KIb, eI, EI, causal in subs:
                    dp = jnp.where(causal, dp_full[rws], 0.0).astype(BF16)
                    dqI = jnp.dot(dp, KIb, preferred_element_type=F32)
                    dKI = _dot_tn(dp, qIb)
                    dq_parts.append(dqI * eI)
                    db_parts.append(dqI * qIb.astype(F32))
                    dk = dk + dKI * EI
                    db = db - dKI * KIb.astype(F32)
                dq = dq + jnp.concatenate(dq_parts, axis=0)
                db_s[rows[c], cols[h]] = db + jnp.concatenate(db_parts, axis=0) + last_row * dblast
                dk_s[rows[c], cols[h]] = dk
                qraw = u_ref[rows[c], cols[h]]
                qsig = _sigmoid(qraw)
                du_ref[rows[c], cols[h]] = (dq * (qsig * (1.0 + qraw * (1.0 - qsig))) * lv[c]).astype(BF16)
        for c in chunks:
            lf_s[rows[c], :] = jnp.dot(_tri(False), db_s[rows[c], :], precision=HI, preferred_element_type=F32)
        for h in heads:
            lbv = lb_ref[:, cols[h]]
            dlb = jnp.zeros((1, HEAD_DIM), F32)
            for c in chunks:
                fraw = u_ref[rows[c], D + h * HEAD_DIM:D + (h + 1) * HEAD_DIM]
                sig = _sigmoid(fraw)
                forget = lbv + (1.0 - lbv) * sig
                dforget = (lf_s[rows[c], cols[h]] / forget - dk_s[rows[c], cols[h]]) * lv[c]
                dlb = dlb + jnp.sum(dforget * (1.0 - sig), axis=0, keepdims=True)
                du_ref[rows[c], D + h * HEAD_DIM:D + (h + 1) * HEAD_DIM] = (dforget * (1.0 - lbv) * sig * (1.0 - sig)).astype(BF16)
            dlb_ref[:, cols[h]] += dlb

    rev = lambda s: (NS - 1 - s, 0)
    return pl.pallas_call(
        body, grid=(NS,),
        in_specs=[pl.BlockSpec((R, 4 * D), rev), pl.BlockSpec((R, D), rev), pl.BlockSpec((R, D), rev),
                  pl.BlockSpec((CH, H, HEAD_DIM, HEAD_DIM), lambda s: (NS - 1 - s, 0, 0, 0)),
                  pl.BlockSpec((None, 1, D), lambda s: (layer, 0, 0)),
                  pl.BlockSpec((None, 1, HEAD_DIM), lambda s: (j, 0, 0))],
        out_specs=[pl.BlockSpec((R, 4 * D), rev), pl.BlockSpec((1, D), lambda s: (0, 0)),
                   pl.BlockSpec((1, HEAD_DIM), lambda s: (0, 0))],
        out_shape=[SDS((T, 4 * D), BF16), SDS((1, D), F32), SDS((1, HEAD_DIM), F32)],
        scratch_shapes=[pltpu.VMEM((H, HEAD_DIM, HEAD_DIM), F32)] + [pltpu.VMEM((R, D), F32)] * 6,
        name=name, compiler_params=_params("arbitrary"))(u, o_raw, dy, sall, lb3, gn3)


def _softmax_layers(p_ref, n_layers):
    rows = [p_ref[l:l + 1, :] for l in range(n_layers)]
    m = functools.reduce(jnp.maximum, rows)
    e = [jnp.exp(x - m) for x in rows]
    tot = functools.reduce(lambda a, b: a + b, e)
    return [x / tot for x in e]


def _lb_fwd(p):
    n_layers, D = p.shape

    def body(p_ref, o_ref):
        s = _softmax_layers(p_ref, n_layers)
        acc = jnp.zeros((1, D), F32)
        o_ref[0:1, :] = acc
        for l in range(1, n_layers):
            acc = acc + s[l]
            o_ref[l:l + 1, :] = acc

    return pl.pallas_call(body, out_shape=SDS(p.shape, F32), name="lb_fwd")(p)


def _lb_bwd(p, dlb):
    n_layers, D = p.shape

    def body(p_ref, d_ref, o_ref):
        s = _softmax_layers(p_ref, n_layers)
        ds = [jnp.zeros((1, D), F32)] * n_layers
        acc = jnp.zeros((1, D), F32)
        for l in range(n_layers - 1, 0, -1):
            acc = acc + d_ref[l:l + 1, :]
            ds[l] = acc
        dot = functools.reduce(lambda a, b: a + b, [s[l] * ds[l] for l in range(n_layers)])
        for l in range(n_layers):
            o_ref[l:l + 1, :] = s[l] * (ds[l] - dot)

    return pl.pallas_call(body, out_shape=SDS(p.shape, F32), name="lb_bwd")(p, dlb)


def _adamw(w, g, m, v, name):
    R, C = w.shape
    tr = _tile(R, 256, 8) if R % 8 == 0 else R

    def body(w_ref, g_ref, m_ref, v_ref, d_ref, mo_ref, vo_ref):
        g_ = g_ref[...]
        m_ = ADAM_B1 * m_ref[...] + (1.0 - ADAM_B1) * g_
        v_ = ADAM_B2 * v_ref[...] + (1.0 - ADAM_B2) * (g_ * g_)
        mh = m_ / (1.0 - ADAM_B1 ** ADAM_STEP)
        vh = v_ / (1.0 - ADAM_B2 ** ADAM_STEP)
        d_ref[...] = -ADAM_LR * (mh / (jnp.sqrt(vh) + ADAM_EPS) + ADAM_WD * w_ref[...])
        mo_ref[...] = m_
        vo_ref[...] = v_

    blk = pl.BlockSpec((tr, C), lambda i: (i, 0))
    return pl.pallas_call(
        body, grid=(R // tr,), in_specs=[blk] * 4, out_specs=[blk] * 3, out_shape=[SDS((R, C), F32)] * 3,
        name=name, compiler_params=_params("parallel"))(w, g, m, v)


def _adamw_layer(w3, m3, v3, g2, layer, outs, name):
    L, R, C = w3.shape
    tr = _tile(R, 256, 8)
    if outs is None:
        outs = tuple(lax.empty(w3.shape, F32) for _ in range(4))

    def body(w_ref, m_ref, v_ref, g_ref, a0, a1, a2, a3, go_ref, d_ref, mo_ref, vo_ref):
        del a0, a1, a2, a3
        g_ = g_ref[...]
        m_ = ADAM_B1 * m_ref[...] + (1.0 - ADAM_B1) * g_
        v_ = ADAM_B2 * v_ref[...] + (1.0 - ADAM_B2) * (g_ * g_)
        mh = m_ / (1.0 - ADAM_B1 ** ADAM_STEP)
        vh = v_ / (1.0 - ADAM_B2 ** ADAM_STEP)
        go_ref[...] = g_
        d_ref[...] = -ADAM_LR * (mh / (jnp.sqrt(vh) + ADAM_EPS) + ADAM_WD * w_ref[...])
        mo_ref[...] = m_
        vo_ref[...] = v_

    lay = pl.BlockSpec((None, tr, C), lambda i: (layer, i, 0))
    return pl.pallas_call(
        body, grid=(R // tr,), in_specs=[lay] * 3 + [pl.BlockSpec((tr, C), lambda i: (i, 0))] + [ANY_SPEC] * 4,
        out_specs=[lay] * 4, out_shape=[SDS(w3.shape, F32)] * 4, input_output_aliases={4: 0, 5: 1, 6: 2, 7: 3},
        name=name, compiler_params=_params("parallel"))(w3, m3, v3, g2, *outs)


SEM_SPEC = pl.BlockSpec(memory_space=pltpu.SEMAPHORE)
HBM_SPEC = pl.BlockSpec(memory_space=pltpu.HBM)
EFFECT = pltpu.SideEffectType.DATAFLOW_SIDE_EFFECTING
N_DEV = 2 * N_CHIPS


def _position():
    x, y, c = lax.axis_index("x"), lax.axis_index("y"), lax.axis_index("c")
    chips = [(1 - x, y), (x, 1 - y), (1 - x, 1 - y)]
    return x, y, c, chips


def _split_start(name, plan, bufs, n_sems, deps=(), earlier=None):
    n = len(bufs)
    held = () if earlier is None else tuple(earlier[1:])

    def body(*refs):
        first_out = n + len(held) + len(deps)
        if earlier is not None:
            sends, recvs = earlier[0](refs[:n], refs[n], refs[n + 1])
            for kw in sends:
                pltpu.make_async_remote_copy(**kw).wait_send()
            for kw in recvs:
                pltpu.make_async_remote_copy(**kw).wait_recv()
        sends, _ = plan(refs[:n], refs[first_out], refs[first_out + 1])
        for kw in sends:
            pltpu.make_async_remote_copy(**kw).start()
        refs[-1][...] = jnp.zeros_like(refs[-1])

    out = pl.pallas_call(
        body, name=name,
        out_shape=(pltpu.SemaphoreType.DMA((n_sems,)), pltpu.SemaphoreType.DMA((n_sems,)),
                   *[pltpu.HBM(b.shape, b.dtype) for b in bufs], SDS((8, 128), F32)),
        in_specs=[HBM_SPEC] * n + [SEM_SPEC] * len(held) + [ANY_SPEC] * len(deps),
        out_specs=(SEM_SPEC, SEM_SPEC, *[HBM_SPEC] * n, pl.BlockSpec(memory_space=pltpu.VMEM)),
        input_output_aliases={i: 2 + i for i in range(n)},
        compiler_params=pltpu.CompilerParams(has_side_effects=EFFECT),
    )(*[pltpu.with_memory_space_constraint(b, pltpu.HBM) for b in bufs], *held, *deps)
    return out[0], out[1], list(out[2:2 + n]), out[-1]


def _split_wait(name, plan, send_sems, recv_sems, bufs, after=()):
    n = len(bufs)

    def body(*refs):
        sends, recvs = plan(refs[:n], refs[n], refs[n + 1])
        for kw in sends:
            pltpu.make_async_remote_copy(**kw).wait_send()
        for kw in recvs:
            pltpu.make_async_remote_copy(**kw).wait_recv()

    out = pl.pallas_call(
        body, name=name, out_shape=tuple(pltpu.HBM(b.shape, b.dtype) for b in bufs),
        in_specs=[HBM_SPEC] * n + [SEM_SPEC, SEM_SPEC] + [ANY_SPEC] * len(after),
        out_specs=tuple([HBM_SPEC] * n), input_output_aliases={i: i for i in range(n)},
        compiler_params=pltpu.CompilerParams(has_side_effects=EFFECT),
    )(*bufs, send_sems, recv_sems, *after)
    return list(out)


def _region(kind, ref, chip, half):
    K, N = ref.shape
    if kind == "col":
        return ref.at[pl.ds(half * (K // 2), K // 2), pl.ds(chip * (N // N_CHIPS), N // N_CHIPS)]
    rows = K // (2 * N_CHIPS)
    return ref.at[pl.ds((2 * chip + half) * rows, rows), :]


def _gather_plan(kinds, over_chips):
    def plan(refs, send_sems, recv_sems):
        x, y, c, chips = _position()
        sends, recvs = [], []
        for f, (ref, kind) in enumerate(zip(refs, kinds)):
            for k, chip in enumerate(chips):
                theirs = 2 * chip[0] + chip[1]
                sem = dict(send_sem=send_sems.at[3 * f + k], recv_sem=recv_sems.at[3 * f + k], device_id_type=MESH)
                if over_chips:
                    out, back, to = _region(kind, ref, 2 * x + y, c), _region(kind, ref, theirs, c), (*chip, c)
                else:
                    out, back, to = _region(kind, ref, theirs, c), _region(kind, ref, theirs, 1 - c), (x, y, 1 - c)
                sends.append(dict(src_ref=out, dst_ref=out, device_id=to, **sem))
                recvs.append(dict(src_ref=back, dst_ref=back, device_id=to, **sem))
        return sends, recvs
    return plan


def _reduce_plan(refs, send_sems, recv_sems):
    x, y, c, _ = _position()
    me = 4 * x + 2 * y + c
    sends, recvs = [], []
    for f in range(len(refs) // 2):
        acc, land = refs[2 * f], refs[2 * f + 1]
        for d in range(1, N_DEV):
            t = (me + d) % N_DEV
            to = dict(device_id=(t // 4, (t // 2) % 2, t % 2), device_id_type=MESH)
            slot = N_DEV - 1 - d
            sends.append(dict(src_ref=acc.at[t % 2, t // 2], dst_ref=land.at[slot], send_sem=send_sems.at[7 * f + d - 1],
                              recv_sem=recv_sems.at[7 * f + slot], **to))
            recvs.append(dict(src_ref=land.at[d - 1], dst_ref=land.at[d - 1], send_sem=send_sems.at[7 * f + d - 1],
                              recv_sem=recv_sems.at[7 * f + d - 1], **to))
    return sends, recvs


def _swap_plan(refs, send_sems, recv_sems):
    x, y, c, _ = _position()
    sends, recvs = [], []
    for f, g in enumerate(refs):
        sem = dict(send_sem=send_sems.at[f], recv_sem=recv_sems.at[f], device_id=(x, y, 1 - c), device_id_type=MESH)
        sends.append(dict(src_ref=g.at[c], dst_ref=g.at[c], **sem))
        recvs.append(dict(src_ref=g.at[1 - c], dst_ref=g.at[1 - c], **sem))
    return sends, recvs


def _sum_pieces(ids2, acc, land, name):
    _, _, nr, nc = acc.shape
    tr = _tile(nr, 256, 16)

    def body(ids_ref, own_ref, land_ref, o_ref):
        del ids_ref
        s = own_ref[...].astype(F32)
        for k in range(N_DEV - 1):
            s = s + land_ref[k].astype(F32)
        o_ref[...] = s

    return pl.pallas_call(
        body,
        grid_spec=pltpu.PrefetchScalarGridSpec(
            num_scalar_prefetch=1, grid=(nr // tr,),
            in_specs=[pl.BlockSpec((None, None, tr, nc), lambda i, ids: (ids[0], ids[1], i, 0)),
                      pl.BlockSpec((N_DEV - 1, tr, nc), lambda i, ids: (0, i, 0))],
            out_specs=pl.BlockSpec((None, tr, nc), lambda i, ids: (ids[0], i, 0))),
        out_shape=SDS((2, nr, nc), F32), name=name, compiler_params=_params("parallel"))(ids2, acc, land)


def _small_plan(refs, send_sems, recv_sems):
    x, y, c, _ = _position()
    me = 4 * x + 2 * y + c
    own, land = refs
    sends, recvs = [], []
    for d in range(1, N_DEV):
        t = (me + d) % N_DEV
        to = dict(device_id=(t // 4, (t // 2) % 2, t % 2), device_id_type=MESH)
        sends.append(dict(src_ref=own, dst_ref=land.at[me], send_sem=send_sems.at[d - 1],
                          recv_sem=recv_sems.at[N_DEV - 1 - d], **to))
        recvs.append(dict(src_ref=land.at[t], dst_ref=land.at[t], send_sem=send_sems.at[d - 1],
                          recv_sem=recv_sems.at[d - 1], **to))
    return sends, recvs


def _sum_blocks(me1, own, land):
    def body(me_ref, own_ref, land_ref, o_ref):
        acc = None
        for d in range(N_DEV):
            term = jnp.where(me_ref[0] == d, own_ref[...], land_ref[d])
            acc = term if acc is None else acc + term
        o_ref[...] = acc

    return pl.pallas_call(
        body,
        grid_spec=pltpu.PrefetchScalarGridSpec(
            num_scalar_prefetch=1, grid=(1,),
            in_specs=[pl.BlockSpec(own.shape, lambda i, me: (0, 0)), pl.BlockSpec(land.shape, lambda i, me: (0, 0, 0))],
            out_specs=pl.BlockSpec(own.shape, lambda i, me: (0, 0))),
        out_shape=SDS(own.shape, F32), name="sum_small", compiler_params=_params("arbitrary"))(me1, own, land)


BIG = {"ev_w_in": "col", "ev_w_out": "row", "od_w_in": "col", "od_w_out": "row", "mlp_w1": "col", "mlp_w2": "row"}
WEIGHTS = ("meta_tokens", "mix_norm_g", "mlp_norm_g", "final_norm_g", "ev_w_in", "ev_conv_w", "ev_conv_b", "ev_ln_g",
           "ev_ln_b", "ev_pool_w", "ev_pool_b", "ev_pool_scale", "ev_w_out", "od_w_in", "od_gnorm_g", "od_w_out",
           "lb_param", "mlp_w1", "mlp_w2")
PACK_UNIT = 1024


def _mixer_names(layer):
    return ("ev_w_in", "ev_w_out") if layer % 2 == 0 else ("od_w_in", "od_w_out")


def _pack(arrays):
    flat = []
    for a in arrays:
        a = a.reshape(-1)
        flat.append(jnp.pad(a, (0, (-a.shape[0]) % PACK_UNIT)))
    return jnp.concatenate(flat).reshape(-1, 128)


def _unpack(packed, shapes):
    flat = packed.reshape(-1)
    out, off = [], 0
    for s in shapes:
        size = 1
        for d in s:
            size *= d
        out.append(flat[off:off + size].reshape(s))
        off += size + (-size) % PACK_UNIT
    return out


def _local_step(x2, target, P, weights, boundary, first_deps=()):
    D = x2.shape[1]
    n_layers = P["mix_norm_g"].shape[0]
    h = jnp.concatenate([jnp.zeros((PAD, D), F32), P["meta_full"], x2], axis=0)
    mix_g = P["mix_norm_g"].reshape(n_layers, 1, D)
    mlp_g = P["mlp_norm_g"].reshape(n_layers, 1, D)
    vec = lambda a: a.reshape(a.shape[0], 1, -1)
    cb3, lg3, lnb3, ps3 = vec(P["ev_conv_b"]), vec(P["ev_ln_g"]), vec(P["ev_ln_b"]), vec(P["ev_pool_scale"])
    pb3 = vec(P["ev_pool_b"])
    gn3 = vec(P["od_gnorm_g"])
    lb_all = _lb_fwd(P["lb_param"])
    lb3 = lb_all.reshape(n_layers, 1, D)
    even = (cb3, lg3, lnb3, P["ev_pool_w"], pb3, ps3)

    saved = []
    deps = tuple(first_deps)
    for layer in range(n_layers):
        j = layer // 2
        w_in, w_out = _mixer_names(layer)
        W = {}
        s = {"h": h, "W": W}
        s["n"] = _rms_fwd(h, mix_g, layer, "mix_norm_0", deps=deps) if layer == 0 else n_next
        deps = ()
        W[w_in], held = weights(layer, w_in, (s["n"],))
        s["u"] = _mm_nn(s["n"], W[w_in], 0, f"mix_in_{layer}", deps=held)
        if layer % 2 == 0:
            s["y"], s["yc"] = _even_fwd(s["u"], P["conv_w_full"], *even, j, f"even_fwd_{layer}")
        else:
            s["y"], s["o"], s["sall"] = _hgrn_fwd(s["u"], lb3, layer, gn3, j, f"hgrn_fwd_{layer}")
        W[w_out], held = weights(layer, w_out, (s["y"],))
        if layer == 0:
            h, s["n2"] = _mm_nn_norm(s["y"], W[w_out], 0, h, mlp_g, layer, "mix_out_0", deps=held)
            s["h1"] = h
            W["mlp_w1"], held = weights(layer, "mlp_w1", (s["n2"],))
            s["relu"] = _mm_nn(s["n2"], W["mlp_w1"], 0, "mlp_up_0", relu=True, deps=held)
            W["mlp_w2"], held = weights(layer, "mlp_w2", (s["relu"],))
            h, n_next = _mm_nn_norm(s["relu"], W["mlp_w2"], 0, h, mix_g, 1, "mlp_down_0", square=True, deps=held)
        else:
            W["mlp_w1"], more1 = weights(layer, "mlp_w1", (s["y"],))
            W["mlp_w2"], more2 = weights(layer, "mlp_w2", (s["y"],))
            last = layer + 1 == n_layers
            out = _tail_fwd(s["y"], W[w_out], h, mlp_g, layer, W["mlp_w1"], W["mlp_w2"], None if last else mix_g,
                            f"tail_{layer}", deps=held + more1 + more2)
            s["h1"], s["n2"], h, s["relu"] = out[0], out[1], out[2], out[-1]
            n_next = None if last else out[3]
        saved.append(s)

    dh, dhb, dg_final, loss = _final(h, P["final_norm_g"].reshape(1, D), target)

    small = {"final_norm_g": dg_final}
    per_layer = {k: [None] * n_layers for k in ("mix_norm_g", "mlp_norm_g", "lb")}
    per_pair = {k: [None] * (n_layers // 2) for k in
                ("ev_conv_w", "ev_conv_b", "ev_ln_g", "ev_ln_b", "ev_pool_w", "ev_pool_b", "ev_pool_scale", "od_gnorm_g")}
    for layer in reversed(range(n_layers)):
        j = layer // 2
        s = saved[layer]
        W = s["W"]
        w_in, w_out = _mixer_names(layer)
        dw2 = _mm_tn(s["relu"], dhb, "row", f"dw2_{layer}", square=True)
        dz, dh, dhb, per_layer["mlp_norm_g"][layer], dy = _mlp_bwd(
            dhb, s["relu"], W["mlp_w1"], W["mlp_w2"], W[w_out], s["h1"], mlp_g, layer, dh, f"mlp_bwd_{layer}", deps=deps + (dw2,))
        dw1 = _mm_tn(s["n2"], dz, "col", f"dw1_{layer}")
        deps = boundary(f"mlp{layer}", {("mlp_w1", layer): dw1, ("mlp_w2", layer): dw2}, (dhb, dw1, dw2))
        dwout = _mm_tn(s["y"], dhb, "row", f"dwout_{layer}", deps=deps)
        if layer % 2 == 0:
            du, dcw, dcb, dlg, dlnb, dpw, dpb, dps = _even_bwd(s["u"], s["yc"], dy, P["conv_w_full"], *even, j, f"even_bwd_{layer}")
            for k, val in (("ev_conv_w", dcw), ("ev_conv_b", dcb), ("ev_ln_g", dlg), ("ev_ln_b", dlnb),
                           ("ev_pool_w", dpw), ("ev_pool_b", dpb), ("ev_pool_scale", dps)):
                per_pair[k][j] = val
        else:
            du, per_layer["lb"][layer], per_pair["od_gnorm_g"][j] = _hgrn_bwd(
                s["u"], s["o"], dy, s["sall"], lb3, layer, gn3, j, f"hgrn_bwd_{layer}")
        dwin = _mm_tn(s["n"], du, "col", f"dwin_{layer}")
        deps = boundary(f"mix{layer}", {(w_in, j): dwin, (w_out, j): dwout}, (du, dwin, dwout))
        dh, dhb, per_layer["mix_norm_g"][layer] = _mm_nt_norm(du, W[w_in], 0, s["h"], mix_g, layer, dh, f"d_n_{layer}", deps=deps)
        deps = ()

    small["mix_norm_g"] = jnp.concatenate(per_layer["mix_norm_g"], axis=0)
    small["mlp_norm_g"] = jnp.concatenate(per_layer["mlp_norm_g"], axis=0)
    dlb_all = jnp.concatenate([jnp.zeros((1, D), F32) if g is None else g for g in per_layer["lb"]], axis=0)
    small["lb_param"] = _lb_bwd(P["lb_param"], dlb_all)
    for k, vals in per_pair.items():
        small[k] = jnp.stack(vals, axis=0)
    small["meta_tokens"] = dh[PAD:LEAD]
    return loss, dh, small


def kernel(x, meta_tokens, mix_norm_g, mlp_norm_g, final_norm_g, ev_w_in, ev_conv_w, ev_conv_b, ev_ln_g, ev_ln_b, ev_pool_w, ev_pool_b, ev_pool_scale, ev_w_out, od_w_in, od_gnorm_g, od_w_out, lb_param, mlp_w1, mlp_w2, loss_target, m_meta_tokens, m_mix_norm_g, m_mlp_norm_g, m_final_norm_g, m_ev_w_in, m_ev_conv_w, m_ev_conv_b, m_ev_ln_g, m_ev_ln_b, m_ev_pool_w, m_ev_pool_b, m_ev_pool_scale, m_ev_w_out, m_od_w_in, m_od_gnorm_g, m_od_w_out, m_lb_param, m_mlp_w1, m_mlp_w2, v_meta_tokens, v_mix_norm_g, v_mlp_norm_g, v_final_norm_g, v_ev_w_in, v_ev_conv_w, v_ev_conv_b, v_ev_ln_g, v_ev_ln_b, v_ev_pool_w, v_ev_pool_b, v_ev_pool_scale, v_ev_w_out, v_od_w_in, v_od_gnorm_g, v_od_w_out, v_lb_param, v_mlp_w1, v_mlp_w2):
    given = dict(locals())
    w = {n: given[n] for n in WEIGHTS}
    m = {n: given["m_" + n] for n in WEIGHTS}
    v = {n: given["v_" + n] for n in WEIGHTS}
    n_layers = mix_norm_g.shape[0]
    core = lax.axis_index("c").astype(jnp.int32)
    chip = (2 * lax.axis_index("x") + lax.axis_index("y")).astype(jnp.int32)
    chip1 = chip.reshape(1)
    ids2 = jnp.stack([core, chip])

    conv_pad = jnp.pad(ev_conv_w, ((0, 0), (0, CONV_ROWS - CONV_WIDTH), (0, 0)))
    stages = [[(0, n)] for n in (*_mixer_names(0), "mlp_w1", "mlp_w2")]
    for layer in range(1, n_layers):
        stages += [[(layer, n) for n in _mixer_names(layer)], [(layer, "mlp_w1"), (layer, "mlp_w2")]]
    gathers, where, token = [], {}, ()
    for k, stage in enumerate(stages):
        index = [layer if n.startswith("mlp") else layer // 2 for layer, n in stage]
        kinds = [BIG[n] for _, n in stage]
        bufs = [_cast_place(w[n], i, BIG[n], chip1, BF16, f"place_{n}_{i}") for (_, n), i in zip(stage, index)]
        if k == 0:
            bufs.append(_cast_place(meta_tokens[None], 0, "col", chip1, F32, "place_meta"))
            bufs.append(_cast_place(conv_pad.reshape(1, -1, conv_pad.shape[2]), 0, "col", chip1, F32, "place_conv_w"))
            kinds += ["col", "col"]
        plan = _gather_plan(kinds, True)
        ss, rs, bufs, tok = _split_start(f"gather_start_{k}", plan, bufs, 3 * len(bufs), deps=token)
        token = (tok,)
        gathers.append((kinds, plan, ss, rs, bufs))
        where.update({key: (k, f) for f, key in enumerate(stage)})

    landed, passed, held = {}, {}, []

    def hand_on(k, deps):
        if k not in passed:
            kinds, plan, ss, rs, bufs = gathers[k]
            to_sibling = _gather_plan(kinds, False)
            ss, rs, bufs, tok = _split_start(f"gather_pass_{k}", to_sibling, bufs, 3 * len(bufs), deps=deps, earlier=(plan, ss, rs))
            passed[k] = (to_sibling, ss, rs, bufs)
            held.append(tok)

    def arrived(k, after):
        if k not in landed:
            hand_on(k, after)
            landed[k] = _split_wait(f"gather_wait_{k}", *passed[k], after)
        return landed[k]

    def weights(layer, name, after):
        k, f = where[(layer, name)]
        full = arrived(k, after)[f][None]
        if name == "mlp_w2" and layer + 1 < n_layers:
            hand_on(where[(layer + 1, _mixer_names(layer + 1)[0])][0], after)
        if layer > 0 and name == _mixer_names(layer)[0]:
            hand_on(where[(layer, "mlp_w1")][0], after)
        tokens = tuple(held)
        held.clear()
        return full, tokens

    first = arrived(0, token)
    P = {n: w[n] for n in ("mix_norm_g", "mlp_norm_g", "final_norm_g", "ev_conv_b", "ev_ln_g", "ev_ln_b", "ev_pool_w",
                           "ev_pool_b", "ev_pool_scale", "od_gnorm_g", "lb_param")}
    P["meta_full"] = first[1]
    P["conv_w_full"] = first[2].reshape(ev_conv_w.shape[0], CONV_ROWS, -1)

    pending, outs = [], {n: None for n in BIG}

    def advance(after, fresh=1):
        tokens, still = [], []
        for pos, st in enumerate(pending):
            if st["phase"] == 1 and pos >= len(pending) - fresh:
                still.append(st)
            elif st["phase"] == 1:
                bufs = _split_wait(f"reduce_wait_{st['tag']}", _reduce_plan, st["ss"], st["rs"], st["bufs"], after)
                halves = [_sum_pieces(ids2, bufs[2 * f], bufs[2 * f + 1], f"sum_{st['tag']}_{f}") for f in range(len(bufs) // 2)]
                ss, rs, halves, tok = _split_start(f"swap_start_{st['tag']}", _swap_plan, halves, len(halves))
                tokens.append(tok)
                still.append(dict(st, phase=2, ss=ss, rs=rs, bufs=halves))
            else:
                grads = _split_wait(f"swap_wait_{st['tag']}", _swap_plan, st["ss"], st["rs"], st["bufs"], after)
                for (n, i), g in zip(st["keys"], grads):
                    outs[n] = _adamw_layer(w[n], m[n], v[n], g.reshape(w[n].shape[1:]), i, outs[n], f"adamw_{n}_{i}")
        pending[:] = still
        return tokens

    def boundary(tag, grads, after):
        tokens = advance(after)
        bufs = []
        for acc in grads.values():
            bufs += [acc, lax.empty((N_DEV - 1,) + acc.shape[2:], BF16)]
        ss, rs, bufs, tok = _split_start(f"reduce_start_{tag}", _reduce_plan, bufs, 7 * len(grads))
        pending.append(dict(phase=1, tag=tag, keys=list(grads), ss=ss, rs=rs, bufs=bufs))
        return tuple(tokens + [tok])

    loss, dh, small = _local_step(x[0], loss_target[0], P, weights, boundary, first_deps=token)

    order = [n for n in WEIGHTS if n not in BIG]
    block = _pack([small[n] for n in order] + [loss])
    ss, rs, bufs, tok = _split_start("small_start", _small_plan, [block, lax.empty((N_DEV,) + block.shape, F32)], N_DEV - 1)
    while pending:
        advance((tok,) + tuple(o[0] for o in outs.values() if o is not None), fresh=0)
    block, land = _split_wait("small_wait", _small_plan, ss, rs, bufs, tuple(outs[n][0] for n in BIG))
    packed = _sum_blocks((4 * lax.axis_index("x") + 2 * lax.axis_index("y") + lax.axis_index("c")).astype(jnp.int32).reshape(1), block, land)
    total = _unpack(packed, [small[n].shape for n in order] + [loss.shape])
    loss_sum = total[-1][0, 0]
    gsmall = dict(zip(order, total[:-1]))
    gsmall["meta_tokens"] = lax.dynamic_slice_in_dim(gsmall["meta_tokens"], chip * meta_tokens.shape[1], meta_tokens.shape[1], 1)
    gsmall["ev_conv_w"] = lax.dynamic_slice_in_dim(gsmall["ev_conv_w"][:, :CONV_WIDTH], chip * ev_conv_w.shape[2], ev_conv_w.shape[2], 2)

    g_out, d_out, m_out, v_out = {}, {}, {}, {}
    for n in WEIGHTS:
        if n in BIG:
            g_out[n], d_out[n], m_out[n], v_out[n] = outs[n]
            continue
        shape = w[n].shape
        g = gsmall[n].reshape(shape)
        cols = shape[-1] if len(shape) > 1 else 128
        two = lambda a: a.reshape(-1, cols)
        d_, m_, v_ = _adamw(two(w[n]), two(g), two(m[n]), two(v[n]), f"adamw_{n}")
        g_out[n], d_out[n], m_out[n], v_out[n] = g, d_.reshape(shape), m_.reshape(shape), v_.reshape(shape)

    grad_x = dh[LEAD:][None]
    return (loss_sum, grad_x, *[g_out[n] for n in WEIGHTS], *[d_out[n] for n in WEIGHTS],
            *[m_out[n] for n in WEIGHTS], *[v_out[n] for n in WEIGHTS])
```

```python
import functools

import jax
import jax.numpy as jnp
from jax import lax
from jax.experimental import pallas as pl
from jax.experimental.pallas import tpu as pltpu

F32 = jnp.float32
BF16 = jnp.bfloat16
SDS = jax.ShapeDtypeStruct
MESH = pl.DeviceIdType.MESH
ANY_SPEC = pl.BlockSpec(memory_space=pl.ANY)

N_META = 16
CHUNK = 64
LEAD = CHUNK
PAD = LEAD - N_META
CONV_WIDTH = 31
CONV_ROWS = 32
POOL_WINDOWS = (2, 4, 8, 16)
HEAD_DIM = 128
SUB = 16
EXP_CAP = 80.0
EPS = 1e-6
ADAM_LR = 0.001
ADAM_B1 = 0.9
ADAM_B2 = 0.999
ADAM_EPS = 1e-08
ADAM_WD = 0.01
ADAM_STEP = 10
N_CHIPS = 4
VMEM_LIMIT = 52 << 20
MM_VMEM_BUDGET = 44 << 20


def _params(*sem):
    return pltpu.CompilerParams(dimension_semantics=sem if sem else None, vmem_limit_bytes=VMEM_LIMIT)


def _tile(n, target, unit=CHUNK):
    best = None
    for t in range(unit, min(n, target) + 1, unit):
        if n % t == 0:
            best = t
    assert best is not None, (n, target, unit)
    return best


def _ctile(n, target=512):
    for t in (512, 384, 256, 128):
        if t <= target and n % t == 0:
            return t
    raise ValueError(n)


def _mm_tiles(M, N, per_row, per_col, per_elem):
    best = None
    for tn in (512, 384, 256, 128):
        if N % tn:
            continue
        for tm in sorted((d for d in range(16, M + 1, 16) if M % d == 0), reverse=True):
            if 2 * (tm * per_row + tn * per_col + tm * tn * per_elem) <= MM_VMEM_BUDGET:
                if best is None or tm * tn > best[0] * best[1]:
                    best = (tm, tn)
                break
    assert best is not None, (M, N)
    return best


def _sigmoid(x):
    return 1.0 / (1.0 + jnp.exp(-x))


def _mult(v, m):
    return v if isinstance(v, int) else pl.multiple_of(v, m)


def _row_ids(shape, base):
    return lax.broadcasted_iota(jnp.int32, shape, 0) + base


def _cast_place(w3, layer, kind, chip1, dtype, name):
    _, ks, ns = w3.shape
    tr = _tile(ks, 512, 16)
    full = (ks, ns * N_CHIPS) if kind == "col" else (ks * N_CHIPS, ns)

    def body(chip_ref, w_ref, o_ref):
        del chip_ref
        o_ref[...] = w_ref[...].astype(dtype)

    omap = (lambda i, chip: (i, chip[0])) if kind == "col" else (lambda i, chip: (chip[0] * (ks // tr) + i, 0))
    return pl.pallas_call(
        body,
        grid_spec=pltpu.PrefetchScalarGridSpec(
            num_scalar_prefetch=1, grid=(ks // tr,),
            in_specs=[pl.BlockSpec((None, tr, ns), lambda i, chip: (layer, i, 0))],
            out_specs=pl.BlockSpec((tr, ns), omap)),
        out_shape=SDS(full, dtype), name=name, compiler_params=_params("parallel"))(chip1, w3)


def _rms_fwd(h, g3, layer, name, deps=()):
    T, D = h.shape
    tm = _tile(T, 832)

    def body(h_ref, g_ref, *rest):
        n_ref = rest[-1]
        x = h_ref[...]
        r = lax.rsqrt(jnp.mean(x * x, axis=-1, keepdims=True) + EPS)
        n_ref[...] = ((x * r) * g_ref[...]).astype(BF16)

    return pl.pallas_call(
        body, grid=(T // tm,),
        in_specs=[pl.BlockSpec((tm, D), lambda i: (i, 0)), pl.BlockSpec((None, 1, D), lambda i: (layer, 0, 0))]
        + [ANY_SPEC] * len(deps),
        out_specs=pl.BlockSpec((tm, D), lambda i: (i, 0)), out_shape=SDS((T, D), BF16),
        name=name, compiler_params=_params("parallel"))(h, g3, *deps)


def _final(h, g2, target):
    T, D = h.shape
    tm = _tile(T, 320)
    nsub = tm // CHUNK
    nblk = target.shape[0] // CHUNK

    def body(h_ref, g_ref, *rest):
        t_refs = rest[:nsub]
        dh_ref, dhb_ref, dg_ref, loss_ref = rest[nsub:]
        i = pl.program_id(0)

        @pl.when(i == 0)
        def _():
            dg_ref[...] = jnp.zeros_like(dg_ref)
            loss_ref[...] = jnp.zeros_like(loss_ref)

        g = g_ref[...]
        for q in range(nsub):
            rows = slice(q * CHUNK, (q + 1) * CHUNK)
            x = h_ref[rows, :]
            r = lax.rsqrt(jnp.mean(x * x, axis=-1, keepdims=True) + EPS)
            xh = x * r
            live = jnp.where(i * nsub + q > 0, 1.0, 0.0).astype(F32)
            e = ((xh * g) - t_refs[q][...]) * live
            dy = e * (1.0 / D)
            dxh = dy * g
            dh = r * (dxh - xh * jnp.mean(dxh * xh, axis=-1, keepdims=True))
            dh_ref[rows, :] = dh
            dhb_ref[rows, :] = dh.astype(BF16)
            dg_ref[...] += jnp.sum(dy * xh, axis=0, keepdims=True)
            loss_ref[...] += jnp.sum(e * e) * (0.5 / D)

    row = pl.BlockSpec((tm, D), lambda i: (i, 0))
    t_specs = [pl.BlockSpec((CHUNK, D), functools.partial(lambda i, q: (jnp.clip(i * nsub + q - 1, 0, nblk - 1), 0), q=q))
               for q in range(nsub)]
    return pl.pallas_call(
        body, grid=(T // tm,),
        in_specs=[row, pl.BlockSpec((1, D), lambda i: (0, 0))] + t_specs,
        out_specs=[row, row, pl.BlockSpec((1, D), lambda i: (0, 0)), pl.BlockSpec((1, 128), lambda i: (0, 0))],
        out_shape=[SDS((T, D), F32), SDS((T, D), BF16), SDS((1, D), F32), SDS((1, 128), F32)],
        name="final_loss", compiler_params=_params("arbitrary"))(h, g2, *([target] * nsub))


def _mm_nn(a, w3, layer, name, res=None, relu=False, square=False, deps=()):
    M, K = a.shape
    N = w3.shape[2]
    tm, tn = _mm_tiles(M, N, 2 * K, 2 * K, (2 if relu else 4) + (4 if res is not None else 0))

    def body(*refs):
        lhs = refs[0][...]
        acc = jnp.dot(lhs * lhs if square else lhs, refs[1][...], preferred_element_type=F32)
        if res is not None:
            acc = acc + refs[2][...]
        refs[-1][...] = jnp.maximum(acc, 0.0).astype(BF16) if relu else acc

    in_specs = [pl.BlockSpec((tm, K), lambda i, j: (i, 0)), pl.BlockSpec((None, K, tn), lambda i, j: (layer, 0, j))]
    args = [a, w3]
    tile = pl.BlockSpec((tm, tn), lambda i, j: (i, j))
    if res is not None:
        in_specs.append(tile)
        args.append(res)
    in_specs += [ANY_SPEC] * len(deps)
    args += list(deps)
    return pl.pallas_call(
        body, grid=(M // tm, N // tn), in_specs=in_specs, out_specs=tile,
        out_shape=SDS((M, N), BF16 if relu else F32),
        name=name, compiler_params=_params("parallel", "parallel"))(*args)


def _row_tile(M, per_row, fixed):
    for tm in sorted((d for d in range(16, M + 1, 16) if M % d == 0), reverse=True):
        if 2 * (tm * per_row + fixed) <= MM_VMEM_BUDGET:
            return tm
    raise ValueError((M, per_row, fixed))


def _mm_nn_norm(a, w3, layer, res, g3, glayer, name, square=False, deps=()):
    M, K = a.shape
    D = w3.shape[2]
    tm = _row_tile(M, 2 * K + 10 * D, 2 * K * D)

    def body(a_ref, w_ref, r_ref, g_ref, *rest):
        h_ref, n_ref = rest[-2:]
        lhs = a_ref[...]
        x = r_ref[...] + jnp.dot(lhs * lhs if square else lhs, w_ref[...], preferred_element_type=F32)
        h_ref[...] = x
        r = lax.rsqrt(jnp.mean(x * x, axis=-1, keepdims=True) + EPS)
        n_ref[...] = ((x * r) * g_ref[...]).astype(BF16)

    row = pl.BlockSpec((tm, D), lambda i: (i, 0))
    return pl.pallas_call(
        body, grid=(M // tm,),
        in_specs=[pl.BlockSpec((tm, K), lambda i: (i, 0)), pl.BlockSpec((None, K, D), lambda i: (layer, 0, 0)), row,
                  pl.BlockSpec((None, 1, D), lambda i: (glayer, 0, 0))] + [ANY_SPEC] * len(deps),
        out_specs=[row, row], out_shape=[SDS((M, D), F32), SDS((M, D), BF16)],
        name=name, compiler_params=_params("parallel"))(a, w3, res, g3, *deps)


def _tail_fwd(y, w_out, res, mlp_g3, layer, w1, w2, next_g3, name, deps=()):
    M, K = y.shape
    D = w_out.shape[2]
    F = w1.shape[2]
    hb = _ctile(F)
    more = next_g3 is not None
    tm = _row_tile(M, 2 * K + 18 * D + (2 * D if more else 0) + 2 * F, 2 * K * D + 2 * D * F)

    def body(y_ref, wo_ref, res_ref, g_ref, w1_ref, w2_ref, *rest):
        outs = rest[-5:] if more else rest[-4:]
        h1 = res_ref[...] + jnp.dot(y_ref[...], wo_ref[...], preferred_element_type=F32)
        outs[0][...] = h1
        n2 = ((h1 * lax.rsqrt(jnp.mean(h1 * h1, axis=-1, keepdims=True) + EPS)) * g_ref[...]).astype(BF16)
        outs[1][...] = n2
        acc = h1
        for jb in range(F // hb):
            cols = slice(jb * hb, (jb + 1) * hb)
            r = jnp.maximum(jnp.dot(n2, w1_ref[:, cols], preferred_element_type=F32), 0.0).astype(BF16)
            outs[-1][:, cols] = r
            acc = acc + jnp.dot(r * r, w2_ref[cols, :], preferred_element_type=F32)
        outs[2][...] = acc
        if more:
            outs[3][...] = ((acc * lax.rsqrt(jnp.mean(acc * acc, axis=-1, keepdims=True) + EPS)) * rest[0][...]).astype(BF16)

    row = pl.BlockSpec((tm, D), lambda i: (i, 0))
    once = dict(pipeline_mode=pl.Buffered(1))
    in_specs = [pl.BlockSpec((tm, K), lambda i: (i, 0)), pl.BlockSpec((None, K, D), lambda i: (0, 0, 0), **once), row,
                pl.BlockSpec((None, 1, D), lambda i: (layer, 0, 0)),
                pl.BlockSpec((None, D, F), lambda i: (0, 0, 0), **once), pl.BlockSpec((None, F, D), lambda i: (0, 0, 0), **once)]
    args = [y, w_out, res, mlp_g3, w1, w2]
    out_specs, out_shape = [row, row, row], [SDS((M, D), F32), SDS((M, D), BF16), SDS((M, D), F32)]
    if more:
        in_specs.append(pl.BlockSpec((None, 1, D), lambda i: (layer + 1, 0, 0)))
        args.append(next_g3)
        out_specs.append(row)
        out_shape.append(SDS((M, D), BF16))
    out_specs.append(pl.BlockSpec((tm, F), lambda i: (i, 0)))
    out_shape.append(SDS((M, F), BF16))
    in_specs += [ANY_SPEC] * len(deps)
    args += list(deps)
    return pl.pallas_call(
        body, grid=(M // tm,), in_specs=in_specs, out_specs=out_specs, out_shape=out_shape,
        name=name, compiler_params=_params("parallel"))(*args)


def _mlp_bwd(dhb, relu, w1, w2, w_out, h, g3, glayer, dh_in, name, deps=()):
    M, D = dhb.shape
    F = w1.shape[2]
    K = w_out.shape[1]
    hb = _ctile(F)
    tm = _row_tile(M, 16 * D + 4 * F + 4 * K, 2 * D * F + K * D)

    def body(dy_ref, r_ref, w1_ref, w2_ref, wo_ref, h_ref, g_ref, dhi_ref, *rest):
        dz_ref, dh_ref, dhb_ref, dg_ref, dyo_ref = rest[-5:]
        dy = dy_ref[...]
        dn = jnp.zeros((tm, D), F32)
        for jb in range(F // hb):
            cols = slice(jb * hb, (jb + 1) * hb)
            dact = lax.dot_general(dy, w2_ref[cols, :], (((1,), (1,)), ((), ())), preferred_element_type=F32)
            dz = (dact * (2.0 * r_ref[:, cols].astype(F32))).astype(BF16)
            dz_ref[:, cols] = dz
            dn = dn + lax.dot_general(dz, w1_ref[:, cols], (((1,), (1,)), ((), ())), preferred_element_type=F32)
        x = h_ref[...]
        r = lax.rsqrt(jnp.mean(x * x, axis=-1, keepdims=True) + EPS)
        xh = x * r
        dxh = dn * g_ref[...]
        dh = dhi_ref[...] + r * (dxh - xh * jnp.mean(dxh * xh, axis=-1, keepdims=True))
        dh_ref[...] = dh
        dhb = dh.astype(BF16)
        dhb_ref[...] = dhb
        dyo_ref[...] = lax.dot_general(dhb, wo_ref[...], (((1,), (1,)), ((), ())), preferred_element_type=F32)

        @pl.when(pl.program_id(0) == 0)
        def _():
            dg_ref[...] = jnp.zeros_like(dg_ref)

        dg_ref[...] += jnp.sum(dn * xh, axis=0, keepdims=True)

    row = pl.BlockSpec((tm, D), lambda i: (i, 0))
    wide = pl.BlockSpec((tm, F), lambda i: (i, 0))
    once = dict(pipeline_mode=pl.Buffered(1))
    return pl.pallas_call(
        body, grid=(M // tm,),
        in_specs=[row, wide, pl.BlockSpec((None, D, F), lambda i: (0, 0, 0), **once),
                  pl.BlockSpec((None, F, D), lambda i: (0, 0, 0), **once),
                  pl.BlockSpec((None, K, D), lambda i: (0, 0, 0), **once), row,
                  pl.BlockSpec((None, 1, D), lambda i: (glayer, 0, 0)), row] + [ANY_SPEC] * len(deps),
        out_specs=[wide, row, row, pl.BlockSpec((1, D), lambda i: (0, 0)), pl.BlockSpec((tm, K), lambda i: (i, 0))],
        out_shape=[SDS((M, F), BF16), SDS((M, D), F32), SDS((M, D), BF16), SDS((1, D), F32), SDS((M, K), F32)],
        name=name, compiler_params=_params("arbitrary"))(dhb, relu, w1, w2, w_out, h, g3, dh_in, *deps)


def _mm_nt_norm(dy, w3, layer, h, g3, glayer, dh_in, name, deps=()):
    M, N = dy.shape
    D = w3.shape[1]
    tm = _row_tile(M, 2 * N + 14 * D, 2 * N * D)

    def body(dy_ref, w_ref, h_ref, g_ref, dhi_ref, *rest):
        dh_ref, dhb_ref, dg_ref = rest[-3:]
        dn = lax.dot_general(dy_ref[...], w_ref[...], (((1,), (1,)), ((), ())), preferred_element_type=F32)
        x = h_ref[...]
        r = lax.rsqrt(jnp.mean(x * x, axis=-1, keepdims=True) + EPS)
        xh = x * r
        dxh = dn * g_ref[...]
        dh = dhi_ref[...] + r * (dxh - xh * jnp.mean(dxh * xh, axis=-1, keepdims=True))
        dh_ref[...] = dh
        dhb_ref[...] = dh.astype(BF16)

        @pl.when(pl.program_id(0) == 0)
        def _():
            dg_ref[...] = jnp.zeros_like(dg_ref)

        dg_ref[...] += jnp.sum(dn * xh, axis=0, keepdims=True)

    row = pl.BlockSpec((tm, D), lambda i: (i, 0))
    return pl.pallas_call(
        body, grid=(M // tm,),
        in_specs=[pl.BlockSpec((tm, N), lambda i: (i, 0)), pl.BlockSpec((None, D, N), lambda i: (layer, 0, 0)), row,
                  pl.BlockSpec((None, 1, D), lambda i: (glayer, 0, 0)), row] + [ANY_SPEC] * len(deps),
        out_specs=[row, row, pl.BlockSpec((1, D), lambda i: (0, 0))],
        out_shape=[SDS((M, D), F32), SDS((M, D), BF16), SDS((1, D), F32)],
        name=name, compiler_params=_params("arbitrary"))(dy, w3, h, g3, dh_in, *deps)


def _fam_dims(kind, K, N):
    return (K // 2, N // N_CHIPS) if kind == "col" else (K // (2 * N_CHIPS), N)


def _mm_tn(x, dy, kind, name, square=False, deps=()):
    M, K = x.shape
    N = dy.shape[1]
    nr, nc = _fam_dims(kind, K, N)

    def body(x_ref, dy_ref, *rest):
        o_ref = rest[-1]
        lhs = x_ref[...]
        res = lax.dot_general(lhs * lhs if square else lhs, dy_ref[...], (((0,), (0,)), ((), ())), preferred_element_type=F32)
        o_ref[...] = res.astype(BF16).reshape(o_ref.shape)

    if kind == "col":
        tn = _ctile(nc)
        ct = nc // tn
        grid = (N // tn,)
        in_specs = [pl.BlockSpec((M, K), lambda j: (0, 0)), pl.BlockSpec((M, tn), lambda j: (0, j))]
        out_spec = pl.BlockSpec((2, None, nr, tn), lambda j: (0, j // ct, 0, j % ct))
    else:
        grid = (N_CHIPS,)
        in_specs = [pl.BlockSpec((M, 2 * nr), lambda i: (0, i)), pl.BlockSpec((M, N), lambda i: (0, 0))]
        out_spec = pl.BlockSpec((2, None, nr, N), lambda i: (0, i, 0, 0))
    return pl.pallas_call(
        body, grid=grid, in_specs=in_specs + [ANY_SPEC] * len(deps), out_specs=out_spec,
        out_shape=SDS((2, N_CHIPS, nr, nc), BF16), name=name, compiler_params=_params("parallel"))(x, dy, *deps)


C_EVEN = 512


def _live(rows, base, total):
    r = _row_ids((rows, 1), base)
    return jnp.logical_and(r >= PAD, r < total).astype(F32)


def _conv_taps(win, w_ref, ls, acc, flip):
    for b in range(8):
        rb = win if b == 0 else pltpu.roll(win, 96 - b, 0)
        for a in range(5):
            o = 8 * a + b
            tap = (30 - o) if flip else (o - 2)
            if 0 <= tap < CONV_WIDTH:
                acc = acc + w_ref[pl.ds(tap, 1), ls] * rb[8 * a:8 * a + CHUNK]
    return acc


def _window_sum(win, levels, forward):
    s = win
    n = win.shape[0]
    for k in range(levels):
        step = 1 << k
        s = s + pltpu.roll(s, (n - step) if forward else step, 0)
    return s


def _pool_count(base, g):
    pos = _row_ids((CHUNK, 1), base) - PAD
    return jnp.clip(pos + 1, 1, POOL_WINDOWS[g]).astype(F32)


def _even_fwd(u, cw3, cb3, lg3, lb3, pw4, pb3, ps3, j, name):
    T = u.shape[0]
    C = C_EVEN
    tm = _tile(T, 320)
    nch = tm // CHUNK
    nblk = T // CHUNK

    def body(u_ref, up_ref, cw_ref, cb_ref, lg_ref, lb_ref, pw_ref, pb_ref, ps_ref, o_ref, yc_ref, a_s, p_s, yc_s):
        row0 = pl.program_id(0) * tm
        up = up_ref[...]
        lp = _live(CHUNK, row0 - CHUNK, T)
        a_s[0:CHUNK, :] = up[:, 0:C] * _sigmoid(up[:, C:2 * C]) * lp
        p_s[0:CHUNK, :] = up[:, 2 * C:3 * C] * lp

        def stage(c, _):
            rs = _mult(c * CHUNK, CHUNK)
            lv = _live(CHUNK, row0 + rs, T)
            a_s[pl.ds(rs + CHUNK, CHUNK), :] = u_ref[pl.ds(rs, CHUNK), 0:C] * _sigmoid(u_ref[pl.ds(rs, CHUNK), C:2 * C]) * lv
            p_s[pl.ds(rs + CHUNK, CHUNK), :] = u_ref[pl.ds(rs, CHUNK), 2 * C:3 * C] * lv
            return 0

        for c in range(nch):
            stage(c, 0)

        def chunk(c, _):
            rs = _mult(c * CHUNK, CHUNK)
            lv = _live(CHUNK, row0 + rs, T)
            for cb in range(4):
                ls = slice(cb * 128, (cb + 1) * 128)
                win = a_s[pl.ds(_mult(rs + 32, 32), 96), ls]
                acc = jnp.broadcast_to(cb_ref[:, ls], (CHUNK, 128))
                yc_s[:, ls] = _conv_taps(win, cw_ref, ls, acc, False)
            y = yc_s[...]
            yc_ref[pl.ds(rs, CHUNK), :] = y
            xc = y - jnp.mean(y, axis=-1, keepdims=True)
            yn = xc * lax.rsqrt(jnp.mean(xc * xc, axis=-1, keepdims=True) + EPS) * lg_ref[...] + lb_ref[...]
            o_ref[pl.ds(rs, CHUNK), 0:C] = (yn * _sigmoid(yn) * lv).astype(BF16)
            for g in range(4):
                ls = slice(g * 128, (g + 1) * 128)
                win = p_s[pl.ds(_mult(rs + 48, 16), 80), ls]
                s = _window_sum(win, g + 1, False)
                d = s[16:80] / _pool_count(row0 + rs, g) - win[16:80]
                yv = jnp.dot(d.astype(BF16), pw_ref[g].astype(BF16), preferred_element_type=F32) + pb_ref[:, ls]
                o_ref[pl.ds(rs, CHUNK), C + g * 128:C + (g + 1) * 128] = (yv * ps_ref[:, ls] * lv).astype(BF16)
            return 0

        for c in range(nch):
            chunk(c, 0)

    vec = pl.BlockSpec((None, 1, C), lambda i: (j, 0, 0))
    return pl.pallas_call(
        body, grid=(T // tm,),
        in_specs=[pl.BlockSpec((tm, 3 * C), lambda i: (i, 0)),
                  pl.BlockSpec((CHUNK, 3 * C), lambda i: (jnp.maximum(i * nch - 1, 0), 0)),
                  pl.BlockSpec((None, CONV_ROWS, C), lambda i: (j, 0, 0)), vec, vec, vec,
                  pl.BlockSpec((None, 4, 128, 128), lambda i: (j, 0, 0, 0)), vec, vec],
        out_specs=[pl.BlockSpec((tm, 2 * C), lambda i: (i, 0)), pl.BlockSpec((tm, C), lambda i: (i, 0))],
        out_shape=[SDS((T, 2 * C), BF16), SDS((T, C), F32)],
        scratch_shapes=[pltpu.VMEM((tm + CHUNK, C), F32), pltpu.VMEM((tm + CHUNK, C), F32), pltpu.VMEM((CHUNK, C), F32)],
        name=name, compiler_params=_params("parallel"))(u, u, cw3, cb3, lg3, lb3, pw4, pb3, ps3)


def _even_bwd(u, yc, dy, cw3, cb3, lg3, lb3, pw4, pb3, ps3, j, name):
    T = u.shape[0]
    C = C_EVEN
    tm = _tile(T, 320)
    nch = tm // CHUNK
    nblk = T // CHUNK
    ntile = T // tm

    def body(u_ref, up_ref, un_ref, yc_ref, ycn_ref, dy_ref, dyn_ref, cw_ref, cb_ref, lg_ref, lb_ref, pw_ref, pb_ref, ps_ref,
             du_ref, dcw_ref, dcb_ref, dlg_ref, dlb_ref, dpw_ref, dpb_ref, dps_ref,
             a_s, p_s, dy_s, dyc_s, dd_s, ddc_s, dw_s):
        i = pl.program_id(0)
        row0 = i * tm

        @pl.when(i == 0)
        def _():
            for ref in (dcb_ref, dlg_ref, dlb_ref, dpw_ref, dpb_ref, dps_ref, dw_s):
                ref[...] = jnp.zeros_like(ref)

        up = up_ref[...]
        lp = _live(CHUNK, row0 - CHUNK, T)
        a_s[0:CHUNK, :] = up[:, 0:C] * _sigmoid(up[:, C:2 * C]) * lp
        p_s[0:CHUNK, :] = up[:, 2 * C:3 * C] * lp
        ln_ = _live(CHUNK, row0 + tm, T)
        p_s[tm + CHUNK:tm + 2 * CHUNK, :] = un_ref[:, 2 * C:3 * C] * ln_
        dy_s[tm:tm + CHUNK, :] = dyn_ref[...] * ln_
        dyc_s[tm + CHUNK:tm + CHUNK + 32, :] = jnp.zeros((32, C), F32)

        def stage(c, _):
            rs = _mult(c * CHUNK, CHUNK)
            lv = _live(CHUNK, row0 + rs, T)
            a_s[pl.ds(rs + CHUNK, CHUNK), :] = u_ref[pl.ds(rs, CHUNK), 0:C] * _sigmoid(u_ref[pl.ds(rs, CHUNK), C:2 * C]) * lv
            p_s[pl.ds(rs + CHUNK, CHUNK), :] = u_ref[pl.ds(rs, CHUNK), 2 * C:3 * C] * lv
            dy_s[pl.ds(rs, CHUNK), :] = dy_ref[pl.ds(rs, CHUNK), :] * lv
            return 0

        for c in range(nch):
            stage(c, 0)

        def first(rs, y, own):
            xc = y - jnp.mean(y, axis=-1, keepdims=True)
            rstd = lax.rsqrt(jnp.mean(xc * xc, axis=-1, keepdims=True) + EPS)
            xh = xc * rstd
            yn = xh * lg_ref[...] + lb_ref[...]
            sg = _sigmoid(yn)
            dyn = dy_s[pl.ds(rs, CHUNK), 0:C] * (sg * (1.0 + yn * (1.0 - sg)))
            dlg_ref[...] += jnp.sum(dyn * xh, axis=0, keepdims=True) * own
            dlb_ref[...] += jnp.sum(dyn, axis=0, keepdims=True) * own
            dxh = dyn * lg_ref[...]
            dyc = rstd * (dxh - jnp.mean(dxh, axis=-1, keepdims=True) - xh * jnp.mean(dxh * xh, axis=-1, keepdims=True))
            dyc_s[pl.ds(rs, CHUNK), :] = dyc
            dcb_ref[...] += jnp.sum(dyc, axis=0, keepdims=True) * own
            for g in range(4):
                ls = slice(g * 128, (g + 1) * 128)
                win = p_s[pl.ds(rs + 48, 80), ls]
                s = _window_sum(win, g + 1, False)
                cnt = _pool_count(row0 + rs, g)
                d = (s[16:80] / cnt - win[16:80]).astype(BF16)
                w = pw_ref[g].astype(BF16)
                pre = jnp.dot(d, w, preferred_element_type=F32) + pb_ref[:, ls]
                dyb = dy_s[pl.ds(rs, CHUNK), C + g * 128:C + (g + 1) * 128]
                dpre = dyb * ps_ref[:, ls]
                dps_ref[:, ls] += jnp.sum(dyb * pre, axis=0, keepdims=True) * own
                dpb_ref[:, ls] += jnp.sum(dpre, axis=0, keepdims=True) * own
                dpre_b = (dpre * own).astype(BF16)
                dpw_ref[g] += lax.dot_general(d, dpre_b, (((0,), (0,)), ((), ())), preferred_element_type=F32)
                dd = lax.dot_general(dpre.astype(BF16), w, (((1,), (1,)), ((), ())), preferred_element_type=F32)
                dd_s[pl.ds(rs, CHUNK), ls] = dd
                ddc_s[pl.ds(rs, CHUNK), ls] = dd / cnt

        def first_in_tile(c, _):
            rs = _mult(c * CHUNK, CHUNK)
            first(rs, yc_ref[pl.ds(rs, CHUNK), :], 1.0)
            return 0

        for c in range(nch):
            first_in_tile(c, 0)
        first(tm, ycn_ref[...], 0.0)
        ddc_s[tm + CHUNK:tm + CHUNK + 16, :] = jnp.zeros((16, C), F32)

        def second(c, _):
            rs = _mult(c * CHUNK, CHUNK)
            lv = _live(CHUNK, row0 + rs, T)
            for cb in range(4):
                ls = slice(cb * 128, (cb + 1) * 128)
                wd = dyc_s[pl.ds(rs, 96), ls]
                da = _conv_taps(wd, cw_ref, ls, jnp.zeros((CHUNK, 128), F32), True)
                wa = a_s[pl.ds(_mult(rs + 32, 32), 96), ls]
                dyc = dyc_s[pl.ds(rs, CHUNK), ls]
                for b in range(8):
                    rb = wa if b == 0 else pltpu.roll(wa, 96 - b, 0)
                    for a in range(5):
                        tap = 8 * a + b - 2
                        if 0 <= tap < CONV_WIDTH:
                            prod = dyc * rb[8 * a:8 * a + CHUNK]
                            part = prod[0:8]
                            for q in range(1, 8):
                                part = part + prod[8 * q:8 * q + 8]
                            dw_s[8 * tap:8 * tap + 8, ls] += part
                val = u_ref[pl.ds(rs, CHUNK), ls]
                sg = _sigmoid(u_ref[pl.ds(rs, CHUNK), C + cb * 128:C + (cb + 1) * 128])
                du_ref[pl.ds(rs, CHUNK), ls] = (da * sg * lv).astype(BF16)
                du_ref[pl.ds(rs, CHUNK), C + cb * 128:C + (cb + 1) * 128] = (da * val * sg * (1.0 - sg) * lv).astype(BF16)
            for g in range(4):
                ls = slice(g * 128, (g + 1) * 128)
                z = _window_sum(ddc_s[pl.ds(rs, 80), ls], g + 1, True)
                dpin = (z[0:CHUNK] - dd_s[pl.ds(rs, CHUNK), ls]) * lv
                du_ref[pl.ds(rs, CHUNK), 2 * C + g * 128:2 * C + (g + 1) * 128] = dpin.astype(BF16)
            return 0

        for c in range(nch):
            second(c, 0)

        @pl.when(i == ntile - 1)
        def _():
            for tap in range(CONV_WIDTH):
                dcw_ref[tap:tap + 1, :] = jnp.sum(dw_s[8 * tap:8 * tap + 8, :], axis=0, keepdims=True)
            dcw_ref[CONV_WIDTH:CONV_ROWS, :] = jnp.zeros((CONV_ROWS - CONV_WIDTH, C), F32)

    vec = pl.BlockSpec((None, 1, C), lambda i: (j, 0, 0))
    ovec = pl.BlockSpec((1, C), lambda i: (0, 0))
    return pl.pallas_call(
        body, grid=(ntile,),
        in_specs=[pl.BlockSpec((tm, 3 * C), lambda i: (i, 0)),
                  pl.BlockSpec((CHUNK, 3 * C), lambda i: (jnp.maximum(i * nch - 1, 0), 0)),
                  pl.BlockSpec((CHUNK, 3 * C), lambda i: (jnp.minimum((i + 1) * nch, nblk - 1), 0)),
                  pl.BlockSpec((tm, C), lambda i: (i, 0)),
                  pl.BlockSpec((CHUNK, C), lambda i: (jnp.minimum((i + 1) * nch, nblk - 1), 0)),
                  pl.BlockSpec((tm, 2 * C), lambda i: (i, 0)),
                  pl.BlockSpec((CHUNK, 2 * C), lambda i: (jnp.minimum((i + 1) * nch, nblk - 1), 0)),
                  pl.BlockSpec((None, CONV_ROWS, C), lambda i: (j, 0, 0)), vec, vec, vec,
                  pl.BlockSpec((None, 4, 128, 128), lambda i: (j, 0, 0, 0)), vec, vec],
        out_specs=[pl.BlockSpec((tm, 3 * C), lambda i: (i, 0)), pl.BlockSpec((CONV_ROWS, C), lambda i: (0, 0)),
                   ovec, ovec, ovec, pl.BlockSpec((4, 128, 128), lambda i: (0, 0, 0)), ovec, ovec],
        out_shape=[SDS((T, 3 * C), BF16), SDS((CONV_ROWS, C), F32), SDS((1, C), F32), SDS((1, C), F32), SDS((1, C), F32),
                   SDS((4, 128, 128), F32), SDS((1, C), F32), SDS((1, C), F32)],
        scratch_shapes=[pltpu.VMEM((tm + CHUNK, C), F32), pltpu.VMEM((tm + 2 * CHUNK, C), F32),
                        pltpu.VMEM((tm + CHUNK, 2 * C), F32),
                        pltpu.VMEM((tm + CHUNK + 32, C), F32), pltpu.VMEM((tm + CHUNK, C), F32),
                        pltpu.VMEM((tm + CHUNK + 16, C), F32), pltpu.VMEM((8 * CONV_ROWS, C), F32)],
        name=name, compiler_params=_params("arbitrary"))(u, u, u, yc, yc, dy, dy, cw3, cb3, lg3, lb3, pw4, pb3, ps3)


HI = lax.Precision.HIGHEST


def _dot_nt(a, b):
    return lax.dot_general(a, b, (((1,), (1,)), ((), ())), preferred_element_type=F32)


def _dot_tn(a, b):
    return lax.dot_general(a, b, (((0,), (0,)), ((), ())), preferred_element_type=F32)


def _tri(lower):
    r = lax.broadcasted_iota(jnp.int32, (CHUNK, CHUNK), 0)
    c = lax.broadcasted_iota(jnp.int32, (CHUNK, CHUNK), 1)
    return jnp.where((c <= r) if lower else (c >= r), 1.0, 0.0).astype(F32)


def _hgrn_gates(u_ref, lb_ref, h, D, lv):
    ls = slice(h * HEAD_DIM, (h + 1) * HEAD_DIM)
    qraw = u_ref[:, ls]
    fraw = u_ref[:, D + h * HEAD_DIM:D + (h + 1) * HEAD_DIM]
    v = u_ref[:, 2 * D + h * HEAD_DIM:2 * D + (h + 1) * HEAD_DIM] * lv
    lbv = lb_ref[:, ls]
    sig = _sigmoid(fraw)
    forget = lbv + (1.0 - lbv) * sig
    logf = jnp.log(forget) * lv
    k = (1.0 - forget) * lv
    qsig = _sigmoid(qraw)
    q = qraw * qsig * lv
    return q, k, v, logf, (qraw, qsig, sig, forget, lbv)


def _sub_parts(q, k, b, b_s, I):
    rows = slice(SUB * I, SUB * (I + 1))
    rho = jnp.zeros((1, HEAD_DIM), F32) if I == 0 else b_s[SUB * I - 1:SUB * I, :]
    eI = jnp.exp(b[rows] - rho)
    EI = jnp.exp(jnp.minimum(rho - b, EXP_CAP))
    causal = (lax.broadcasted_iota(jnp.int32, (SUB, CHUNK), 1)
              <= lax.broadcasted_iota(jnp.int32, (SUB, CHUNK), 0) + SUB * I)
    return rows, q[rows] * eI, k * EI, eI, EI, causal


def _chunks_per_step(NC):
    for n in (5, 4, 3, 2):
        if NC % n == 0:
            return n
    return 1


def _hgrn_fwd(u, lb3, layer, gn3, j, name):
    T = u.shape[0]
    D = u.shape[1] // 4
    H = D // HEAD_DIM
    NC = T // CHUNK
    CH = _chunks_per_step(NC)
    R = CH * CHUNK

    def body(u_ref, lb_ref, gn_ref, y_ref, o_ref, sall_ref, st_s, b_s, lf_s, q_s, k_s):
        n = pl.program_id(0)

        @pl.when(n == 0)
        def _():
            st_s[...] = jnp.zeros_like(st_s)

        heads = range(H)
        cols = [slice(h * HEAD_DIM, (h + 1) * HEAD_DIM) for h in heads]
        rows = [slice(c * CHUNK, (c + 1) * CHUNK) for c in range(CH)]
        vb = {}
        for c in range(CH):
            lv = _live(CHUNK, (n * CH + c) * CHUNK, T)
            for h in heads:
                q, k, v, logf, _ = _hgrn_gates(u_ref.at[rows[c]], lb_ref, h, D, lv)
                q_s[rows[c], cols[h]] = q
                k_s[rows[c], cols[h]] = k
                lf_s[rows[c], cols[h]] = logf
                vb[c, h] = v.astype(BF16)
        for c in range(CH):
            b_s[rows[c], :] = jnp.dot(_tri(True), lf_s[rows[c], :], precision=HI, preferred_element_type=F32)
        ops = {}
        for c in range(CH):
            for h in heads:
                b_h = b_s.at[rows[c], cols[h]]
                b = b_h[...]
                q = q_s[rows[c], cols[h]]
                k = k_s[rows[c], cols[h]]
                blast = b_h[CHUNK - 1:CHUNK, :]
                qh = (q * jnp.exp(b)).astype(BF16)
                kt = (k * jnp.exp(blast - b)).astype(BF16)
                subs = []
                for I in range(CHUNK // SUB):
                    _, qI, KI, _, _, causal = _sub_parts(q, k, b, b_h, I)
                    subs.append((qI.astype(BF16), KI.astype(BF16), causal))
                ops[c, h] = (qh, kt, jnp.exp(blast), subs)
        mm = {}
        for h in heads:
            st = st_s[h]
            for c in range(CH):
                qh, kt, eblast, subs = ops[c, h]
                sall_ref[c, h] = st
                o_inter = _dot_nt(qh, st.astype(BF16))
                st = st * eblast + _dot_tn(vb[c, h], kt)
                mm[c, h] = (o_inter, [_dot_nt(qI, KI) for qI, KI, _ in subs])
            st_s[h] = st
        for c in range(CH):
            for h in heads:
                o_inter, ps = mm[c, h]
                p = jnp.concatenate([jnp.where(m, x, 0.0) for x, (_, _, m) in zip(ps, ops[c, h][3])], axis=0).astype(BF16)
                o = o_inter + jnp.dot(p, vb[c, h], preferred_element_type=F32)
                o_ref[rows[c], cols[h]] = o
                graw = u_ref[rows[c], 3 * D + h * HEAD_DIM:3 * D + (h + 1) * HEAD_DIM]
                r = lax.rsqrt(jnp.mean(o * o, axis=-1, keepdims=True) + EPS)
                y_ref[rows[c], cols[h]] = (((o * r) * gn_ref[...]) * (graw * _sigmoid(graw))).astype(BF16)

    return pl.pallas_call(
        body, grid=(NC // CH,),
        in_specs=[pl.BlockSpec((R, 4 * D), lambda n: (n, 0)),
                  pl.BlockSpec((None, 1, D), lambda n: (layer, 0, 0)),
                  pl.BlockSpec((None, 1, HEAD_DIM), lambda n: (j, 0, 0))],
        out_specs=[pl.BlockSpec((R, D), lambda n: (n, 0)), pl.BlockSpec((R, D), lambda n: (n, 0)),
                   pl.BlockSpec((CH, H, HEAD_DIM, HEAD_DIM), lambda n: (n, 0, 0, 0))],
        out_shape=[SDS((T, D), BF16), SDS((T, D), F32), SDS((NC, H, HEAD_DIM, HEAD_DIM), F32)],
        scratch_shapes=[pltpu.VMEM((H, HEAD_DIM, HEAD_DIM), F32)] + [pltpu.VMEM((R, D), F32)] * 4,
        name=name, compiler_params=_params("arbitrary"))(u, lb3, gn3)


def _hgrn_bwd(u, o_raw, dy, sall, lb3, layer, gn3, j, name):
    T = u.shape[0]
    D = u.shape[1] // 4
    H = D // HEAD_DIM
    NC = T // CHUNK
    CH = _chunks_per_step(NC)
    R = CH * CHUNK
    NS = NC // CH

    def body(u_ref, o_ref, dy_ref, sall_ref, lb_ref, gn_ref, du_ref, dlb_ref, dgn_ref, dst_s, b_s, lf_s, q_s, k_s, db_s, dk_s):
        step = pl.program_id(0)
        n = NS - 1 - step

        @pl.when(step == 0)
        def _():
            dst_s[...] = jnp.zeros_like(dst_s)
            dlb_ref[...] = jnp.zeros_like(dlb_ref)
            dgn_ref[...] = jnp.zeros_like(dgn_ref)

        last_row = (_row_ids((CHUNK, 1), 0) == CHUNK - 1).astype(F32)
        gn = gn_ref[...]
        heads = range(H)
        chunks = range(CH)
        cols = [slice(h * HEAD_DIM, (h + 1) * HEAD_DIM) for h in heads]
        rows = [slice(c * CHUNK, (c + 1) * CHUNK) for c in chunks]
        lv = [_live(CHUNK, (n * CH + c) * CHUNK, T) for c in chunks]
        vb, dob = {}, {}
        dgn = jnp.zeros((1, HEAD_DIM), F32)
        for c in chunks:
            for h in heads:
                q, k, v, logf, _ = _hgrn_gates(u_ref.at[rows[c]], lb_ref, h, D, lv[c])
                q_s[rows[c], cols[h]] = q
                k_s[rows[c], cols[h]] = k
                lf_s[rows[c], cols[h]] = logf
                vb[c, h] = v.astype(BF16)
                graw = u_ref[rows[c], 3 * D + h * HEAD_DIM:3 * D + (h + 1) * HEAD_DIM]
                gsig = _sigmoid(graw)
                o = o_ref[rows[c], cols[h]]
                r = lax.rsqrt(jnp.mean(o * o, axis=-1, keepdims=True) + EPS)
                xh = o * r
                dyv = dy_ref[rows[c], cols[h]]
                dsg = dyv * (graw * gsig)
                dgn = dgn + jnp.sum(dsg * xh, axis=0, keepdims=True)
                dxh = dsg * gn
                do = r * (dxh - xh * jnp.mean(dxh * xh, axis=-1, keepdims=True))
                dob[c, h] = do.astype(BF16)
                dgraw = dyv * xh * gn * (gsig * (1.0 + graw * (1.0 - gsig)))
                du_ref[rows[c], 3 * D + h * HEAD_DIM:3 * D + (h + 1) * HEAD_DIM] = (dgraw * lv[c]).astype(BF16)
        dgn_ref[...] += dgn
        for c in chunks:
            b_s[rows[c], :] = jnp.dot(_tri(True), lf_s[rows[c], :], precision=HI, preferred_element_type=F32)
        ops = {}
        for c in chunks:
            for h in heads:
                b_h = b_s.at[rows[c], cols[h]]
                b = b_h[...]
                q = q_s[rows[c], cols[h]]
                k = k_s[rows[c], cols[h]]
                blast = b_h[CHUNK - 1:CHUNK, :]
                eb = jnp.exp(b)
                ekb = jnp.exp(blast - b)
                subs = []
                for I in range(CHUNK // SUB):
                    rws, qI, KI, eI, EI, causal = _sub_parts(q, k, b, b_h, I)
                    subs.append((rws, qI.astype(BF16), KI.astype(BF16), eI, EI, causal))
                ops[c, h] = (eb, ekb, jnp.exp(blast), (q * eb).astype(BF16), (k * ekb).astype(BF16), subs)
        mm = {}
        for h in heads:
            dst = dst_s[h]
            for c in reversed(chunks):
                eb, ekb, eblast, qhb, ktb, subs = ops[c, h]
                st = sall_ref[c, h]
                dstb = dst.astype(BF16)
                dv = _dot_nt(ktb, dstb)
                dqh = jnp.dot(dob[c, h], st.astype(BF16), preferred_element_type=F32)
                dkt = jnp.dot(vb[c, h], dstb, preferred_element_type=F32)
                dblast = jnp.sum(dst * st, axis=0, keepdims=True) * eblast
                dst = dst * eblast + _dot_tn(dob[c, h], qhb)
                dp_full = _dot_nt(dob[c, h], vb[c, h])
                ps = [_dot_nt(qIb, KIb) for _, qIb, KIb, _, _, _ in subs]
                mm[c, h] = (dv, dqh, dkt, dblast, dp_full, ps)
            dst_s[h] = dst
        for c in chunks:
            for h in heads:
                eb, ekb, eblast, qhb, ktb, subs = ops[c, h]
                dv, dqh, dkt, dblast, dp_full, ps = mm[c, h]
                p = jnp.concatenate([jnp.where(sub[5], x, 0.0) for x, sub in zip(ps, subs)], axis=0).astype(BF16)
                dv = dv + _dot_tn(p, dob[c, h])
                du_ref[rows[c], 2 * D + h * HEAD_DIM:2 * D + (h + 1) * HEAD_DIM] = (dv * lv[c]).astype(BF16)
                dq = dqh * eb
                db = dqh * qhb.astype(F32)
                tmp = dkt * ktb.astype(F32)
                dk = dkt * ekb
                db = db - tmp
                dblast = dblast + jnp.sum(tmp, axis=0, keepdims=True)
                dq_parts, db_parts = [], []
                for rws, qIb, KIb, eI, EI, causal in subs:
                    dp = jnp.where(causal, dp_full[rws], 0.0).astype(BF16)
                    dqI = jnp.dot(dp, KIb, preferred_element_type=F32)
                    dKI = _dot_tn(dp, qIb)
                    dq_parts.append(dqI * eI)
                    db_parts.append(dqI * qIb.astype(F32))
                    dk = dk + dKI * EI
                    db = db - dKI * KIb.astype(F32)
                dq = dq + jnp.concatenate(dq_parts, axis=0)
                db_s[rows[c], cols[h]] = db + jnp.concatenate(db_parts, axis=0) + last_row * dblast
                dk_s[rows[c], cols[h]] = dk
                qraw = u_ref[rows[c], cols[h]]
                qsig = _sigmoid(qraw)
                du_ref[rows[c], cols[h]] = (dq * (qsig * (1.0 + qraw * (1.0 - qsig))) * lv[c]).astype(BF16)
        for c in chunks:
            lf_s[rows[c], :] = jnp.dot(_tri(False), db_s[rows[c], :], precision=HI, preferred_element_type=F32)
        for h in heads:
            lbv = lb_ref[:, cols[h]]
            dlb = jnp.zeros((1, HEAD_DIM), F32)
            for c in chunks:
                fraw = u_ref[rows[c], D + h * HEAD_DIM:D + (h + 1) * HEAD_DIM]
                sig = _sigmoid(fraw)
                forget = lbv + (1.0 - lbv) * sig
                dforget = (lf_s[rows[c], cols[h]] / forget - dk_s[rows[c], cols[h]]) * lv[c]
                dlb = dlb + jnp.sum(dforget * (1.0 - sig), axis=0, keepdims=True)
                du_ref[rows[c], D + h * HEAD_DIM:D + (h + 1) * HEAD_DIM] = (dforget * (1.0 - lbv) * sig * (1.0 - sig)).astype(BF16)
            dlb_ref[:, cols[h]] += dlb

    rev = lambda s: (NS - 1 - s, 0)
    return pl.pallas_call(
        body, grid=(NS,),
        in_specs=[pl.BlockSpec((R, 4 * D), rev), pl.BlockSpec((R, D), rev), pl.BlockSpec((R, D), rev),
                  pl.BlockSpec((CH, H, HEAD_DIM, HEAD_DIM), lambda s: (NS - 1 - s, 0, 0, 0)),
                  pl.BlockSpec((None, 1, D), lambda s: (layer, 0, 0)),
                  pl.BlockSpec((None, 1, HEAD_DIM), lambda s: (j, 0, 0))],
        out_specs=[pl.BlockSpec((R, 4 * D), rev), pl.BlockSpec((1, D), lambda s: (0, 0)),
                   pl.BlockSpec((1, HEAD_DIM), lambda s: (0, 0))],
        out_shape=[SDS((T, 4 * D), BF16), SDS((1, D), F32), SDS((1, HEAD_DIM), F32)],
        scratch_shapes=[pltpu.VMEM((H, HEAD_DIM, HEAD_DIM), F32)] + [pltpu.VMEM((R, D), F32)] * 6,
        name=name, compiler_params=_params("arbitrary"))(u, o_raw, dy, sall, lb3, gn3)


def _softmax_layers(p_ref, n_layers):
    rows = [p_ref[l:l + 1, :] for l in range(n_layers)]
    m = functools.reduce(jnp.maximum, rows)
    e = [jnp.exp(x - m) for x in rows]
    tot = functools.reduce(lambda a, b: a + b, e)
    return [x / tot for x in e]


def _lb_fwd(p):
    n_layers, D = p.shape

    def body(p_ref, o_ref):
        s = _softmax_layers(p_ref, n_layers)
        acc = jnp.zeros((1, D), F32)
        o_ref[0:1, :] = acc
        for l in range(1, n_layers):
            acc = acc + s[l]
            o_ref[l:l + 1, :] = acc

    return pl.pallas_call(body, out_shape=SDS(p.shape, F32), name="lb_fwd")(p)


def _lb_bwd(p, dlb):
    n_layers, D = p.shape

    def body(p_ref, d_ref, o_ref):
        s = _softmax_layers(p_ref, n_layers)
        ds = [jnp.zeros((1, D), F32)] * n_layers
        acc = jnp.zeros((1, D), F32)
        for l in range(n_layers - 1, 0, -1):
            acc = acc + d_ref[l:l + 1, :]
            ds[l] = acc
        dot = functools.reduce(lambda a, b: a + b, [s[l] * ds[l] for l in range(n_layers)])
        for l in range(n_layers):
            o_ref[l:l + 1, :] = s[l] * (ds[l] - dot)

    return pl.pallas_call(body, out_shape=SDS(p.shape, F32), name="lb_bwd")(p, dlb)


def _adamw(w, g, m, v, name):
    R, C = w.shape
    tr = _tile(R, 256, 8) if R % 8 == 0 else R

    def body(w_ref, g_ref, m_ref, v_ref, d_ref, mo_ref, vo_ref):
        g_ = g_ref[...]
        m_ = ADAM_B1 * m_ref[...] + (1.0 - ADAM_B1) * g_
        v_ = ADAM_B2 * v_ref[...] + (1.0 - ADAM_B2) * (g_ * g_)
        mh = m_ / (1.0 - ADAM_B1 ** ADAM_STEP)
        vh = v_ / (1.0 - ADAM_B2 ** ADAM_STEP)
        d_ref[...] = -ADAM_LR * (mh / (jnp.sqrt(vh) + ADAM_EPS) + ADAM_WD * w_ref[...])
        mo_ref[...] = m_
        vo_ref[...] = v_

    blk = pl.BlockSpec((tr, C), lambda i: (i, 0))
    return pl.pallas_call(
        body, grid=(R // tr,), in_specs=[blk] * 4, out_specs=[blk] * 3, out_shape=[SDS((R, C), F32)] * 3,
        name=name, compiler_params=_params("parallel"))(w, g, m, v)


def _adamw_layer(w3, m3, v3, g2, layer, outs, name):
    L, R, C = w3.shape
    tr = _tile(R, 256, 8)
    if outs is None:
        outs = tuple(lax.empty(w3.shape, F32) for _ in range(4))

    def body(w_ref, m_ref, v_ref, g_ref, a0, a1, a2, a3, go_ref, d_ref, mo_ref, vo_ref):
        del a0, a1, a2, a3
        g_ = g_ref[...]
        m_ = ADAM_B1 * m_ref[...] + (1.0 - ADAM_B1) * g_
        v_ = ADAM_B2 * v_ref[...] + (1.0 - ADAM_B2) * (g_ * g_)
        mh = m_ / (1.0 - ADAM_B1 ** ADAM_STEP)
        vh = v_ / (1.0 - ADAM_B2 ** ADAM_STEP)
        go_ref[...] = g_
        d_ref[...] = -ADAM_LR * (mh / (jnp.sqrt(vh) + ADAM_EPS) + ADAM_WD * w_ref[...])
        mo_ref[...] = m_
        vo_ref[...] = v_

    lay = pl.BlockSpec((None, tr, C), lambda i: (layer, i, 0))
    return pl.pallas_call(
        body, grid=(R // tr,), in_specs=[lay] * 3 + [pl.BlockSpec((tr, C), lambda i: (i, 0))] + [ANY_SPEC] * 4,
        out_specs=[lay] * 4, out_shape=[SDS(w3.shape, F32)] * 4, input_output_aliases={4: 0, 5: 1, 6: 2, 7: 3},
        name=name, compiler_params=_params("parallel"))(w3, m3, v3, g2, *outs)


SEM_SPEC = pl.BlockSpec(memory_space=pltpu.SEMAPHORE)
HBM_SPEC = pl.BlockSpec(memory_space=pltpu.HBM)
EFFECT = pltpu.SideEffectType.DATAFLOW_SIDE_EFFECTING
N_DEV = 2 * N_CHIPS


def _position():
    x, y, c = lax.axis_index("x"), lax.axis_index("y"), lax.axis_index("c")
    chips = [(1 - x, y), (x, 1 - y), (1 - x, 1 - y)]
    return x, y, c, chips


def _split_start(name, plan, bufs, n_sems, deps=(), earlier=None):
    n = len(bufs)
    held = () if earlier is None else tuple(earlier[1:])

    def body(*refs):
        first_out = n + len(held) + len(deps)
        if earlier is not None:
            sends, recvs = earlier[0](refs[:n], refs[n], refs[n + 1])
            for kw in sends:
                pltpu.make_async_remote_copy(**kw).wait_send()
            for kw in recvs:
                pltpu.make_async_remote_copy(**kw).wait_recv()
        sends, _ = plan(refs[:n], refs[first_out], refs[first_out + 1])
        for kw in sends:
            pltpu.make_async_remote_copy(**kw).start()
        refs[-1][...] = jnp.zeros_like(refs[-1])

    out = pl.pallas_call(
        body, name=name,
        out_shape=(pltpu.SemaphoreType.DMA((n_sems,)), pltpu.SemaphoreType.DMA((n_sems,)),
                   *[pltpu.HBM(b.shape, b.dtype) for b in bufs], SDS((8, 128), F32)),
        in_specs=[HBM_SPEC] * n + [SEM_SPEC] * len(held) + [ANY_SPEC] * len(deps),
        out_specs=(SEM_SPEC, SEM_SPEC, *[HBM_SPEC] * n, pl.BlockSpec(memory_space=pltpu.VMEM)),
        input_output_aliases={i: 2 + i for i in range(n)},
        compiler_params=pltpu.CompilerParams(has_side_effects=EFFECT),
    )(*[pltpu.with_memory_space_constraint(b, pltpu.HBM) for b in bufs], *held, *deps)
    return out[0], out[1], list(out[2:2 + n]), out[-1]


def _split_wait(name, plan, send_sems, recv_sems, bufs, after=()):
    n = len(bufs)

    def body(*refs):
        sends, recvs = plan(refs[:n], refs[n], refs[n + 1])
        for kw in sends:
            pltpu.make_async_remote_copy(**kw).wait_send()
        for kw in recvs:
            pltpu.make_async_remote_copy(**kw).wait_recv()

    out = pl.pallas_call(
        body, name=name, out_shape=tuple(pltpu.HBM(b.shape, b.dtype) for b in bufs),
        in_specs=[HBM_SPEC] * n + [SEM_SPEC, SEM_SPEC] + [ANY_SPEC] * len(after),
        out_specs=tuple([HBM_SPEC] * n), input_output_aliases={i: i for i in range(n)},
        compiler_params=pltpu.CompilerParams(has_side_effects=EFFECT),
    )(*bufs, send_sems, recv_sems, *after)
    return list(out)


def _region(kind, ref, chip, half):
    K, N = ref.shape
    if kind == "col":
        return ref.at[pl.ds(half * (K // 2), K // 2), pl.ds(chip * (N // N_CHIPS), N // N_CHIPS)]
    rows = K // (2 * N_CHIPS)
    return ref.at[pl.ds((2 * chip + half) * rows, rows), :]


def _gather_plan(kinds, over_chips):
    def plan(refs, send_sems, recv_sems):
        x, y, c, chips = _position()
        sends, recvs = [], []
        for f, (ref, kind) in enumerate(zip(refs, kinds)):
            for k, chip in enumerate(chips):
                theirs = 2 * chip[0] + chip[1]
                sem = dict(send_sem=send_sems.at[3 * f + k], recv_sem=recv_sems.at[3 * f + k], device_id_type=MESH)
                if over_chips:
                    out, back, to = _region(kind, ref, 2 * x + y, c), _region(kind, ref, theirs, c), (*chip, c)
                else:
                    out, back, to = _region(kind, ref, theirs, c), _region(kind, ref, theirs, 1 - c), (x, y, 1 - c)
                sends.append(dict(src_ref=out, dst_ref=out, device_id=to, **sem))
                recvs.append(dict(src_ref=back, dst_ref=back, device_id=to, **sem))
        return sends, recvs
    return plan


def _reduce_plan(refs, send_sems, recv_sems):
    x, y, c, _ = _position()
    me = 4 * x + 2 * y + c
    sends, recvs = [], []
    for f in range(len(refs) // 2):
        acc, land = refs[2 * f], refs[2 * f + 1]
        for d in range(1, N_DEV):
            t = (me + d) % N_DEV
            to = dict(device_id=(t // 4, (t // 2) % 2, t % 2), device_id_type=MESH)
            slot = N_DEV - 1 - d
            sends.append(dict(src_ref=acc.at[t % 2, t // 2], dst_ref=land.at[slot], send_sem=send_sems.at[7 * f + d - 1],
                              recv_sem=recv_sems.at[7 * f + slot], **to))
            recvs.append(dict(src_ref=land.at[d - 1], dst_ref=land.at[d - 1], send_sem=send_sems.at[7 * f + d - 1],
                              recv_sem=recv_sems.at[7 * f + d - 1], **to))
    return sends, recvs


def _swap_plan(refs, send_sems, recv_sems):
    x, y, c, _ = _position()
    sends, recvs = [], []
    for f, g in enumerate(refs):
        sem = dict(send_sem=send_sems.at[f], recv_sem=recv_sems.at[f], device_id=(x, y, 1 - c), device_id_type=MESH)
        sends.append(dict(src_ref=g.at[c], dst_ref=g.at[c], **sem))
        recvs.append(dict(src_ref=g.at[1 - c], dst_ref=g.at[1 - c], **sem))
    return sends, recvs


def _sum_pieces(ids2, acc, land, name):
    _, _, nr, nc = acc.shape
    tr = _tile(nr, 256, 16)

    def body(ids_ref, own_ref, land_ref, o_ref):
        del ids_ref
        s = own_ref[...].astype(F32)
        for k in range(N_DEV - 1):
            s = s + land_ref[k].astype(F32)
        o_ref[...] = s

    return pl.pallas_call(
        body,
        grid_spec=pltpu.PrefetchScalarGridSpec(
            num_scalar_prefetch=1, grid=(nr // tr,),
            in_specs=[pl.BlockSpec((None, None, tr, nc), lambda i, ids: (ids[0], ids[1], i, 0)),
                      pl.BlockSpec((N_DEV - 1, tr, nc), lambda i, ids: (0, i, 0))],
            out_specs=pl.BlockSpec((None, tr, nc), lambda i, ids: (ids[0], i, 0))),
        out_shape=SDS((2, nr, nc), F32), name=name, compiler_params=_params("parallel"))(ids2, acc, land)


def _small_plan(refs, send_sems, recv_sems):
    x, y, c, _ = _position()
    me = 4 * x + 2 * y + c
    own, land = refs
    sends, recvs = [], []
    for d in range(1, N_DEV):
        t = (me + d) % N_DEV
        to = dict(device_id=(t // 4, (t // 2) % 2, t % 2), device_id_type=MESH)
        sends.append(dict(src_ref=own, dst_ref=land.at[me], send_sem=send_sems.at[d - 1],
                          recv_sem=recv_sems.at[N_DEV - 1 - d], **to))
        recvs.append(dict(src_ref=land.at[t], dst_ref=land.at[t], send_sem=send_sems.at[d - 1],
                          recv_sem=recv_sems.at[d - 1], **to))
    return sends, recvs


def _sum_blocks(me1, own, land):
    def body(me_ref, own_ref, land_ref, o_ref):
        acc = None
        for d in range(N_DEV):
            term = jnp.where(me_ref[0] == d, own_ref[...], land_ref[d])
            acc = term if acc is None else acc + term
        o_ref[...] = acc

    return pl.pallas_call(
        body,
        grid_spec=pltpu.PrefetchScalarGridSpec(
            num_scalar_prefetch=1, grid=(1,),
            in_specs=[pl.BlockSpec(own.shape, lambda i, me: (0, 0)), pl.BlockSpec(land.shape, lambda i, me: (0, 0, 0))],
            out_specs=pl.BlockSpec(own.shape, lambda i, me: (0, 0))),
        out_shape=SDS(own.shape, F32), name="sum_small", compiler_params=_params("arbitrary"))(me1, own, land)


BIG = {"ev_w_in": "col", "ev_w_out": "row", "od_w_in": "col", "od_w_out": "row", "mlp_w1": "col", "mlp_w2": "row"}
WEIGHTS = ("meta_tokens", "mix_norm_g", "mlp_norm_g", "final_norm_g", "ev_w_in", "ev_conv_w", "ev_conv_b", "ev_ln_g",
           "ev_ln_b", "ev_pool_w", "ev_pool_b", "ev_pool_scale", "ev_w_out", "od_w_in", "od_gnorm_g", "od_w_out",
           "lb_param", "mlp_w1", "mlp_w2")
PACK_UNIT = 1024


def _mixer_names(layer):
    return ("ev_w_in", "ev_w_out") if layer % 2 == 0 else ("od_w_in", "od_w_out")


def _pack(arrays):
    flat = []
    for a in arrays:
        a = a.reshape(-1)
        flat.append(jnp.pad(a, (0, (-a.shape[0]) % PACK_UNIT)))
    return jnp.concatenate(flat).reshape(-1, 128)


def _unpack(packed, shapes):
    flat = packed.reshape(-1)
    out, off = [], 0
    for s in shapes:
        size = 1
        for d in s:
            size *= d
        out.append(flat[off:off + size].reshape(s))
        off += size + (-size) % PACK_UNIT
    return out


def _local_step(x2, target, P, weights, boundary, first_deps=()):
    D = x2.shape[1]
    n_layers = P["mix_norm_g"].shape[0]
    h = jnp.concatenate([jnp.zeros((PAD, D), F32), P["meta_full"], x2], axis=0)
    mix_g = P["mix_norm_g"].reshape(n_layers, 1, D)
    mlp_g = P["mlp_norm_g"].reshape(n_layers, 1, D)
    vec = lambda a: a.reshape(a.shape[0], 1, -1)
    cb3, lg3, lnb3, ps3 = vec(P["ev_conv_b"]), vec(P["ev_ln_g"]), vec(P["ev_ln_b"]), vec(P["ev_pool_scale"])
    pb3 = vec(P["ev_pool_b"])
    gn3 = vec(P["od_gnorm_g"])
    lb_all = _lb_fwd(P["lb_param"])
    lb3 = lb_all.reshape(n_layers, 1, D)
    even = (cb3, lg3, lnb3, P["ev_pool_w"], pb3, ps3)

    saved = []
    deps = tuple(first_deps)
    for layer in range(n_layers):
        j = layer // 2
        w_in, w_out = _mixer_names(layer)
        W = {}
        s = {"h": h, "W": W}
        s["n"] = _rms_fwd(h, mix_g, layer, "mix_norm_0", deps=deps) if layer == 0 else n_next
        deps = ()
        W[w_in], held = weights(layer, w_in, (s["n"],))
        s["u"] = _mm_nn(s["n"], W[w_in], 0, f"mix_in_{layer}", deps=held)
        if layer % 2 == 0:
            s["y"], s["yc"] = _even_fwd(s["u"], P["conv_w_full"], *even, j, f"even_fwd_{layer}")
        else:
            s["y"], s["o"], s["sall"] = _hgrn_fwd(s["u"], lb3, layer, gn3, j, f"hgrn_fwd_{layer}")
        W[w_out], held = weights(layer, w_out, (s["y"],))
        if layer == 0:
            h, s["n2"] = _mm_nn_norm(s["y"], W[w_out], 0, h, mlp_g, layer, "mix_out_0", deps=held)
            s["h1"] = h
            W["mlp_w1"], held = weights(layer, "mlp_w1", (s["n2"],))
            s["relu"] = _mm_nn(s["n2"], W["mlp_w1"], 0, "mlp_up_0", relu=True, deps=held)
            W["mlp_w2"], held = weights(layer, "mlp_w2", (s["relu"],))
            h, n_next = _mm_nn_norm(s["relu"], W["mlp_w2"], 0, h, mix_g, 1, "mlp_down_0", square=True, deps=held)
        else:
            W["mlp_w1"], more1 = weights(layer, "mlp_w1", (s["y"],))
            W["mlp_w2"], more2 = weights(layer, "mlp_w2", (s["y"],))
            last = layer + 1 == n_layers
            out = _tail_fwd(s["y"], W[w_out], h, mlp_g, layer, W["mlp_w1"], W["mlp_w2"], None if last else mix_g,
                            f"tail_{layer}", deps=held + more1 + more2)
            s["h1"], s["n2"], h, s["relu"] = out[0], out[1], out[2], out[-1]
            n_next = None if last else out[3]
        saved.append(s)

    dh, dhb, dg_final, loss = _final(h, P["final_norm_g"].reshape(1, D), target)

    small = {"final_norm_g": dg_final}
    per_layer = {k: [None] * n_layers for k in ("mix_norm_g", "mlp_norm_g", "lb")}
    per_pair = {k: [None] * (n_layers // 2) for k in
                ("ev_conv_w", "ev_conv_b", "ev_ln_g", "ev_ln_b", "ev_pool_w", "ev_pool_b", "ev_pool_scale", "od_gnorm_g")}
    for layer in reversed(range(n_layers)):
        j = layer // 2
        s = saved[layer]
        W = s["W"]
        w_in, w_out = _mixer_names(layer)
        dw2 = _mm_tn(s["relu"], dhb, "row", f"dw2_{layer}", square=True)
        dz, dh, dhb, per_layer["mlp_norm_g"][layer], dy = _mlp_bwd(
            dhb, s["relu"], W["mlp_w1"], W["mlp_w2"], W[w_out], s["h1"], mlp_g, layer, dh, f"mlp_bwd_{layer}", deps=deps + (dw2,))
        dw1 = _mm_tn(s["n2"], dz, "col", f"dw1_{layer}")
        deps = boundary(f"mlp{layer}", {("mlp_w1", layer): dw1, ("mlp_w2", layer): dw2}, (dhb, dw1, dw2))
        dwout = _mm_tn(s["y"], dhb, "row", f"dwout_{layer}", deps=deps)
        if layer % 2 == 0:
            du, dcw, dcb, dlg, dlnb, dpw, dpb, dps = _even_bwd(s["u"], s["yc"], dy, P["conv_w_full"], *even, j, f"even_bwd_{layer}")
            for k, val in (("ev_conv_w", dcw), ("ev_conv_b", dcb), ("ev_ln_g", dlg), ("ev_ln_b", dlnb),
                           ("ev_pool_w", dpw), ("ev_pool_b", dpb), ("ev_pool_scale", dps)):
                per_pair[k][j] = val
        else:
            du, per_layer["lb"][layer], per_pair["od_gnorm_g"][j] = _hgrn_bwd(
                s["u"], s["o"], dy, s["sall"], lb3, layer, gn3, j, f"hgrn_bwd_{layer}")
        dwin = _mm_tn(s["n"], du, "col", f"dwin_{layer}")
        deps = boundary(f"mix{layer}", {(w_in, j): dwin, (w_out, j): dwout}, (du, dwin, dwout))
        dh, dhb, per_layer["mix_norm_g"][layer] = _mm_nt_norm(du, W[w_in], 0, s["h"], mix_g, layer, dh, f"d_n_{layer}", deps=deps)
        deps = ()

    small["mix_norm_g"] = jnp.concatenate(per_layer["mix_norm_g"], axis=0)
    small["mlp_norm_g"] = jnp.concatenate(per_layer["mlp_norm_g"], axis=0)
    dlb_all = jnp.concatenate([jnp.zeros((1, D), F32) if g is None else g for g in per_layer["lb"]], axis=0)
    small["lb_param"] = _lb_bwd(P["lb_param"], dlb_all)
    for k, vals in per_pair.items():
        small[k] = jnp.stack(vals, axis=0)
    small["meta_tokens"] = dh[PAD:LEAD]
    return loss, dh, small


def kernel(x, meta_tokens, mix_norm_g, mlp_norm_g, final_norm_g, ev_w_in, ev_conv_w, ev_conv_b, ev_ln_g, ev_ln_b, ev_pool_w, ev_pool_b, ev_pool_scale, ev_w_out, od_w_in, od_gnorm_g, od_w_out, lb_param, mlp_w1, mlp_w2, loss_target, m_meta_tokens, m_mix_norm_g, m_mlp_norm_g, m_final_norm_g, m_ev_w_in, m_ev_conv_w, m_ev_conv_b, m_ev_ln_g, m_ev_ln_b, m_ev_pool_w, m_ev_pool_b, m_ev_pool_scale, m_ev_w_out, m_od_w_in, m_od_gnorm_g, m_od_w_out, m_lb_param, m_mlp_w1, m_mlp_w2, v_meta_tokens, v_mix_norm_g, v_mlp_norm_g, v_final_norm_g, v_ev_w_in, v_ev_conv_w, v_ev_conv_b, v_ev_ln_g, v_ev_ln_b, v_ev_pool_w, v_ev_pool_b, v_ev_pool_scale, v_ev_w_out, v_od_w_in, v_od_gnorm_g, v_od_w_out, v_lb_param, v_mlp_w1, v_mlp_w2):
    given = dict(locals())
    w = {n: given[n] for n in WEIGHTS}
    m = {n: given["m_" + n] for n in WEIGHTS}
    v = {n: given["v_" + n] for n in WEIGHTS}
    n_layers = mix_norm_g.shape[0]
    core = lax.axis_index("c").astype(jnp.int32)
    chip = (2 * lax.axis_index("x") + lax.axis_index("y")).astype(jnp.int32)
    chip1 = chip.reshape(1)
    ids2 = jnp.stack([core, chip])

    conv_pad = jnp.pad(ev_conv_w, ((0, 0), (0, CONV_ROWS - CONV_WIDTH), (0, 0)))
    stages = [[(0, n)] for n in (*_mixer_names(0), "mlp_w1", "mlp_w2")]
    for layer in range(1, n_layers):
        stages += [[(layer, n) for n in _mixer_names(layer)], [(layer, "mlp_w1"), (layer, "mlp_w2")]]
    gathers, where, token = [], {}, ()
    for k, stage in enumerate(stages):
        index = [layer if n.startswith("mlp") else layer // 2 for layer, n in stage]
        kinds = [BIG[n] for _, n in stage]
        bufs = [_cast_place(w[n], i, BIG[n], chip1, BF16, f"place_{n}_{i}") for (_, n), i in zip(stage, index)]
        if k == 0:
            bufs.append(_cast_place(meta_tokens[None], 0, "col", chip1, F32, "place_meta"))
            bufs.append(_cast_place(conv_pad.reshape(1, -1, conv_pad.shape[2]), 0, "col", chip1, F32, "place_conv_w"))
            kinds += ["col", "col"]
        plan = _gather_plan(kinds, True)
        ss, rs, bufs, tok = _split_start(f"gather_start_{k}", plan, bufs, 3 * len(bufs), deps=token)
        token = (tok,)
        gathers.append((kinds, plan, ss, rs, bufs))
        where.update({key: (k, f) for f, key in enumerate(stage)})

    landed, passed, held = {}, {}, []

    def hand_on(k, deps):
        if k not in passed:
            kinds, plan, ss, rs, bufs = gathers[k]
            to_sibling = _gather_plan(kinds, False)
            ss, rs, bufs, tok = _split_start(f"gather_pass_{k}", to_sibling, bufs, 3 * len(bufs), deps=deps, earlier=(plan, ss, rs))
            passed[k] = (to_sibling, ss, rs, bufs)
            held.append(tok)

    def arrived(k, after):
        if k not in landed:
            hand_on(k, after)
            landed[k] = _split_wait(f"gather_wait_{k}", *passed[k], after)
        return landed[k]

    def weights(layer, name, after):
        k, f = where[(layer, name)]
        full = arrived(k, after)[f][None]
        if name == "mlp_w2" and layer + 1 < n_layers:
            hand_on(where[(layer + 1, _mixer_names(layer + 1)[0])][0], after)
        if layer > 0 and name == _mixer_names(layer)[0]:
            hand_on(where[(layer, "mlp_w1")][0], after)
        tokens = tuple(held)
        held.clear()
        return full, tokens

    first = arrived(0, token)
    P = {n: w[n] for n in ("mix_norm_g", "mlp_norm_g", "final_norm_g", "ev_conv_b", "ev_ln_g", "ev_ln_b", "ev_pool_w",
                           "ev_pool_b", "ev_pool_scale", "od_gnorm_g", "lb_param")}
    P["meta_full"] = first[1]
    P["conv_w_full"] = first[2].reshape(ev_conv_w.shape[0], CONV_ROWS, -1)

    pending, outs = [], {n: None for n in BIG}

    def advance(after, fresh=1):
        tokens, still = [], []
        for pos, st in enumerate(pending):
            if st["phase"] == 1 and pos >= len(pending) - fresh:
                still.append(st)
            elif st["phase"] == 1:
                if fresh == 0:
                    after = after + tuple(o[0] for o in outs.values() if o is not None)
                bufs = _split_wait(f"reduce_wait_{st['tag']}", _reduce_plan, st["ss"], st["rs"], st["bufs"], after)
                halves = [_sum_pieces(ids2, bufs[2 * f], bufs[2 * f + 1], f"sum_{st['tag']}_{f}") for f in range(len(bufs) // 2)]
                ss, rs, halves, tok = _split_start(f"swap_start_{st['tag']}", _swap_plan, halves, len(halves))
                tokens.append(tok)
                still.append(dict(st, phase=2, ss=ss, rs=rs, bufs=halves))
            else:
                grads = _split_wait(f"swap_wait_{st['tag']}", _swap_plan, st["ss"], st["rs"], st["bufs"], after)
                for (n, i), g in zip(st["keys"], grads):
                    outs[n] = _adamw_layer(w[n], m[n], v[n], g.reshape(w[n].shape[1:]), i, outs[n], f"adamw_{n}_{i}")
        pending[:] = still
        return tokens

    def boundary(tag, grads, after):
        tokens = advance(after)
        bufs = []
        for acc in grads.values():
            bufs += [acc, lax.empty((N_DEV - 1,) + acc.shape[2:], BF16)]
        ss, rs, bufs, tok = _split_start(f"reduce_start_{tag}", _reduce_plan, bufs, 7 * len(grads))
        pending.append(dict(phase=1, tag=tag, keys=list(grads), ss=ss, rs=rs, bufs=bufs))
        return tuple(tokens + [tok])

    loss, dh, small = _local_step(x[0], loss_target[0], P, weights, boundary, first_deps=token)

    order = [n for n in WEIGHTS if n not in BIG]
    block = _pack([small[n] for n in order] + [loss])
    ss, rs, bufs, tok = _split_start("small_start", _small_plan, [block, lax.empty((N_DEV,) + block.shape, F32)], N_DEV - 1)
    while pending:
        advance((tok,) + tuple(o[0] for o in outs.values() if o is not None), fresh=0)
    block, land = _split_wait("small_wait", _small_plan, ss, rs, bufs, tuple(outs[n][0] for n in BIG))
    packed = _sum_blocks((4 * lax.axis_index("x") + 2 * lax.axis_index("y") + lax.axis_index("c")).astype(jnp.int32).reshape(1), block, land)
    total = _unpack(packed, [small[n].shape for n in order] + [loss.shape])
    loss_sum = total[-1][0, 0]
    gsmall = dict(zip(order, total[:-1]))
    gsmall["meta_tokens"] = lax.dynamic_slice_in_dim(gsmall["meta_tokens"], chip * meta_tokens.shape[1], meta_tokens.shape[1], 1)
    gsmall["ev_conv_w"] = lax.dynamic_slice_in_dim(gsmall["ev_conv_w"][:, :CONV_WIDTH], chip * ev_conv_w.shape[2], ev_conv_w.shape[2], 2)

    g_out, d_out, m_out, v_out = {}, {}, {}, {}
    for n in WEIGHTS:
        if n in BIG:
            g_out[n], d_out[n], m_out[n], v_out[n] = outs[n]
            continue
        shape = w[n].shape
        g = gsmall[n].reshape(shape)
        cols = shape[-1] if len(shape) > 1 else 128
        two = lambda a: a.reshape(-1, cols)
        d_, m_, v_ = _adamw(two(w[n]), two(g), two(m[n]), two(v[n]), f"adamw_{n}")
        g_out[n], d_out[n], m_out[n], v_out[n] = g, d_.reshape(shape), m_.reshape(shape), v_.reshape(shape)

    grad_x = dh[LEAD:][None]
    return (loss_sum, grad_x, *[g_out[n] for n in WEIGHTS], *[d_out[n] for n in WEIGHTS],
            *[m_out[n] for n in WEIGHTS], *[v_out[n] for n in WEIGHTS])
```

```python
import functools

import jax
import jax.numpy as jnp
from jax import lax
from jax.experimental import pallas as pl
from jax.experimental.pallas import tpu as pltpu

F32 = jnp.float32
BF16 = jnp.bfloat16
SDS = jax.ShapeDtypeStruct
MESH = pl.DeviceIdType.MESH
ANY_SPEC = pl.BlockSpec(memory_space=pl.ANY)

N_META = 16
CHUNK = 64
LEAD = CHUNK
PAD = LEAD - N_META
CONV_WIDTH = 31
CONV_ROWS = 32
POOL_WINDOWS = (2, 4, 8, 16)
HEAD_DIM = 128
SUB = 16
EXP_CAP = 80.0
EPS = 1e-6
ADAM_LR = 0.001
ADAM_B1 = 0.9
ADAM_B2 = 0.999
ADAM_EPS = 1e-08
ADAM_WD = 0.01
ADAM_STEP = 10
N_CHIPS = 4
VMEM_LIMIT = 58 << 20
MM_VMEM_BUDGET = 50 << 20


def _params(*sem):
    return pltpu.CompilerParams(dimension_semantics=sem if sem else None, vmem_limit_bytes=VMEM_LIMIT)


def _tile(n, target, unit=CHUNK):
    best = None
    for t in range(unit, min(n, target) + 1, unit):
        if n % t == 0:
            best = t
    assert best is not None, (n, target, unit)
    return best


def _ctile(n, target=512):
    for t in (512, 384, 256, 128):
        if t <= target and n % t == 0:
            return t
    raise ValueError(n)


def _mm_tiles(M, N, per_row, per_col, per_elem):
    best = None
    for tn in (512, 384, 256, 128):
        if N % tn:
            continue
        for tm in sorted((d for d in range(16, M + 1, 16) if M % d == 0), reverse=True):
            if 2 * (tm * per_row + tn * per_col + tm * tn * per_elem) <= MM_VMEM_BUDGET:
                if best is None or tm * tn > best[0] * best[1]:
                    best = (tm, tn)
                break
    assert best is not None, (M, N)
    return best


def _sigmoid(x):
    return 1.0 / (1.0 + jnp.exp(-x))


def _mult(v, m):
    return v if isinstance(v, int) else pl.multiple_of(v, m)


def _row_ids(shape, base):
    return lax.broadcasted_iota(jnp.int32, shape, 0) + base


def _cast_place(w3, layer, kind, chip1, dtype, name):
    _, ks, ns = w3.shape
    tr = _tile(ks, 512, 16)
    full = (ks, ns * N_CHIPS) if kind == "col" else (ks * N_CHIPS, ns)

    def body(chip_ref, w_ref, o_ref):
        del chip_ref
        o_ref[...] = w_ref[...].astype(dtype)

    omap = (lambda i, chip: (i, chip[0])) if kind == "col" else (lambda i, chip: (chip[0] * (ks // tr) + i, 0))
    return pl.pallas_call(
        body,
        grid_spec=pltpu.PrefetchScalarGridSpec(
            num_scalar_prefetch=1, grid=(ks // tr,),
            in_specs=[pl.BlockSpec((None, tr, ns), lambda i, chip: (layer, i, 0))],
            out_specs=pl.BlockSpec((tr, ns), omap)),
        out_shape=SDS(full, dtype), name=name, compiler_params=_params("parallel"))(chip1, w3)


def _rms_fwd(h, g3, layer, name, deps=()):
    T, D = h.shape
    tm = _tile(T, 832)

    def body(h_ref, g_ref, *rest):
        n_ref = rest[-1]
        x = h_ref[...]
        r = lax.rsqrt(jnp.mean(x * x, axis=-1, keepdims=True) + EPS)
        n_ref[...] = ((x * r) * g_ref[...]).astype(BF16)

    return pl.pallas_call(
        body, grid=(T // tm,),
        in_specs=[pl.BlockSpec((tm, D), lambda i: (i, 0)), pl.BlockSpec((None, 1, D), lambda i: (layer, 0, 0))]
        + [ANY_SPEC] * len(deps),
        out_specs=pl.BlockSpec((tm, D), lambda i: (i, 0)), out_shape=SDS((T, D), BF16),
        name=name, compiler_params=_params("parallel"))(h, g3, *deps)


def _final(h, g2, target):
    T, D = h.shape
    tm = _tile(T, 320)
    nsub = tm // CHUNK
    nblk = target.shape[0] // CHUNK

    def body(h_ref, g_ref, *rest):
        t_refs = rest[:nsub]
        dh_ref, dhb_ref, dg_ref, loss_ref = rest[nsub:]
        i = pl.program_id(0)

        @pl.when(i == 0)
        def _():
            dg_ref[...] = jnp.zeros_like(dg_ref)
            loss_ref[...] = jnp.zeros_like(loss_ref)

        g = g_ref[...]
        for q in range(nsub):
            rows = slice(q * CHUNK, (q + 1) * CHUNK)
            x = h_ref[rows, :]
            r = lax.rsqrt(jnp.mean(x * x, axis=-1, keepdims=True) + EPS)
            xh = x * r
            live = jnp.where(i * nsub + q > 0, 1.0, 0.0).astype(F32)
            e = ((xh * g) - t_refs[q][...]) * live
            dy = e * (1.0 / D)
            dxh = dy * g
            dh = r * (dxh - xh * jnp.mean(dxh * xh, axis=-1, keepdims=True))
            dh_ref[rows, :] = dh
            dhb_ref[rows, :] = dh.astype(BF16)
            dg_ref[...] += jnp.sum(dy * xh, axis=0, keepdims=True)
            loss_ref[...] += jnp.sum(e * e) * (0.5 / D)

    row = pl.BlockSpec((tm, D), lambda i: (i, 0))
    t_specs = [pl.BlockSpec((CHUNK, D), functools.partial(lambda i, q: (jnp.clip(i * nsub + q - 1, 0, nblk - 1), 0), q=q))
               for q in range(nsub)]
    return pl.pallas_call(
        body, grid=(T // tm,),
        in_specs=[row, pl.BlockSpec((1, D), lambda i: (0, 0))] + t_specs,
        out_specs=[row, row, pl.BlockSpec((1, D), lambda i: (0, 0)), pl.BlockSpec((1, 128), lambda i: (0, 0))],
        out_shape=[SDS((T, D), F32), SDS((T, D), BF16), SDS((1, D), F32), SDS((1, 128), F32)],
        name="final_loss", compiler_params=_params("arbitrary"))(h, g2, *([target] * nsub))


def _mm_nn(a, w3, layer, name, res=None, relu=False, square=False, deps=()):
    M, K = a.shape
    N = w3.shape[2]
    tm, tn = _mm_tiles(M, N, 2 * K, 2 * K, (2 if relu else 4) + (4 if res is not None else 0))

    def body(*refs):
        lhs = refs[0][...]
        acc = jnp.dot(lhs * lhs if square else lhs, refs[1][...], preferred_element_type=F32)
        if res is not None:
            acc = acc + refs[2][...]
        refs[-1][...] = jnp.maximum(acc, 0.0).astype(BF16) if relu else acc

    in_specs = [pl.BlockSpec((tm, K), lambda i, j: (i, 0)), pl.BlockSpec((None, K, tn), lambda i, j: (layer, 0, j))]
    args = [a, w3]
    tile = pl.BlockSpec((tm, tn), lambda i, j: (i, j))
    if res is not None:
        in_specs.append(tile)
        args.append(res)
    in_specs += [ANY_SPEC] * len(deps)
    args += list(deps)
    return pl.pallas_call(
        body, grid=(M // tm, N // tn), in_specs=in_specs, out_specs=tile,
        out_shape=SDS((M, N), BF16 if relu else F32),
        name=name, compiler_params=_params("parallel", "parallel"))(*args)


def _mm_nt(dy, w3, layer, name, relu=None, deps=()):
    M, N = dy.shape
    K = w3.shape[1]
    tm, tk = _mm_tiles(M, K, 2 * N, 2 * N, 4)

    def body(*refs):
        acc = lax.dot_general(refs[0][...], refs[1][...], (((1,), (1,)), ((), ())), preferred_element_type=F32)
        if relu is not None:
            acc = (acc * (2.0 * refs[2][...].astype(F32))).astype(BF16)
        refs[-1][...] = acc

    tile = pl.BlockSpec((tm, tk), lambda i, j: (i, j))
    in_specs = [pl.BlockSpec((tm, N), lambda i, j: (i, 0)), pl.BlockSpec((None, tk, N), lambda i, j: (layer, j, 0))]
    args = [dy, w3]
    if relu is not None:
        in_specs.append(tile)
        args.append(relu)
    in_specs += [ANY_SPEC] * len(deps)
    args += list(deps)
    return pl.pallas_call(
        body, grid=(M // tm, K // tk), in_specs=in_specs, out_specs=tile,
        out_shape=SDS((M, K), F32 if relu is None else BF16),
        name=name, compiler_params=_params("parallel", "parallel"))(*args)


def _row_tile(M, per_row, fixed):
    for tm in sorted((d for d in range(16, M + 1, 16) if M % d == 0), reverse=True):
        if 2 * (tm * per_row + fixed) <= MM_VMEM_BUDGET:
            return tm
    raise ValueError((M, per_row, fixed))


def _mm_nn_norm(a, w3, layer, res, g3, glayer, name, square=False, deps=()):
    M, K = a.shape
    D = w3.shape[2]
    tm = _row_tile(M, 2 * K + 10 * D, 2 * K * D)

    def body(a_ref, w_ref, r_ref, g_ref, *rest):
        h_ref, n_ref = rest[-2:]
        lhs = a_ref[...]
        x = r_ref[...] + jnp.dot(lhs * lhs if square else lhs, w_ref[...], preferred_element_type=F32)
        h_ref[...] = x
        r = lax.rsqrt(jnp.mean(x * x, axis=-1, keepdims=True) + EPS)
        n_ref[...] = ((x * r) * g_ref[...]).astype(BF16)

    row = pl.BlockSpec((tm, D), lambda i: (i, 0))
    return pl.pallas_call(
        body, grid=(M // tm,),
        in_specs=[pl.BlockSpec((tm, K), lambda i: (i, 0)), pl.BlockSpec((None, K, D), lambda i: (layer, 0, 0)), row,
                  pl.BlockSpec((None, 1, D), lambda i: (glayer, 0, 0))] + [ANY_SPEC] * len(deps),
        out_specs=[row, row], out_shape=[SDS((M, D), F32), SDS((M, D), BF16)],
        name=name, compiler_params=_params("parallel"))(a, w3, res, g3, *deps)


def _tail_fwd(y, w_out, res, mlp_g3, layer, w1, w2, next_g3, name, deps=()):
    M, K = y.shape
    D = w_out.shape[2]
    F = w1.shape[2]
    hb = _ctile(F)
    more = next_g3 is not None
    tm = _row_tile(M, 2 * K + 18 * D + (2 * D if more else 0) + 2 * F, 2 * K * D + 2 * D * F)

    def body(y_ref, wo_ref, res_ref, g_ref, w1_ref, w2_ref, *rest):
        outs = rest[-5:] if more else rest[-4:]
        h1 = res_ref[...] + jnp.dot(y_ref[...], wo_ref[...], preferred_element_type=F32)
        outs[0][...] = h1
        n2 = ((h1 * lax.rsqrt(jnp.mean(h1 * h1, axis=-1, keepdims=True) + EPS)) * g_ref[...]).astype(BF16)
        outs[1][...] = n2
        acc = h1
        for jb in range(F // hb):
            cols = slice(jb * hb, (jb + 1) * hb)
            r = jnp.maximum(jnp.dot(n2, w1_ref[:, cols], preferred_element_type=F32), 0.0).astype(BF16)
            outs[-1][:, cols] = r
            acc = acc + jnp.dot(r * r, w2_ref[cols, :], preferred_element_type=F32)
        outs[2][...] = acc
        if more:
            outs[3][...] = ((acc * lax.rsqrt(jnp.mean(acc * acc, axis=-1, keepdims=True) + EPS)) * rest[0][...]).astype(BF16)

    row = pl.BlockSpec((tm, D), lambda i: (i, 0))
    once = dict(pipeline_mode=pl.Buffered(1))
    in_specs = [pl.BlockSpec((tm, K), lambda i: (i, 0)), pl.BlockSpec((None, K, D), lambda i: (0, 0, 0), **once), row,
                pl.BlockSpec((None, 1, D), lambda i: (layer, 0, 0)),
                pl.BlockSpec((None, D, F), lambda i: (0, 0, 0), **once), pl.BlockSpec((None, F, D), lambda i: (0, 0, 0), **once)]
    args = [y, w_out, res, mlp_g3, w1, w2]
    out_specs, out_shape = [row, row, row], [SDS((M, D), F32), SDS((M, D), BF16), SDS((M, D), F32)]
    if more:
        in_specs.append(pl.BlockSpec((None, 1, D), lambda i: (layer + 1, 0, 0)))
        args.append(next_g3)
        out_specs.append(row)
        out_shape.append(SDS((M, D), BF16))
    out_specs.append(pl.BlockSpec((tm, F), lambda i: (i, 0)))
    out_shape.append(SDS((M, F), BF16))
    in_specs += [ANY_SPEC] * len(deps)
    args += list(deps)
    return pl.pallas_call(
        body, grid=(M // tm,), in_specs=in_specs, out_specs=out_specs, out_shape=out_shape,
        name=name, compiler_params=_params("parallel"))(*args)


def _mlp_bwd(dhb, relu, w1, w2, h, g3, glayer, dh_in, name, deps=()):
    M, D = dhb.shape
    F = w1.shape[2]
    hb = _ctile(F)
    tm = _row_tile(M, 16 * D + 4 * F, 2 * D * F)

    def body(dy_ref, r_ref, w1_ref, w2_ref, h_ref, g_ref, dhi_ref, *rest):
        dz_ref, dh_ref, dhb_ref, dg_ref = rest[-4:]
        dy = dy_ref[...]
        dn = jnp.zeros((tm, D), F32)
        for jb in range(F // hb):
            cols = slice(jb * hb, (jb + 1) * hb)
            dact = lax.dot_general(dy, w2_ref[cols, :], (((1,), (1,)), ((), ())), preferred_element_type=F32)
            dz = (dact * (2.0 * r_ref[:, cols].astype(F32))).astype(BF16)
            dz_ref[:, cols] = dz
            dn = dn + lax.dot_general(dz, w1_ref[:, cols], (((1,), (1,)), ((), ())), preferred_element_type=F32)
        x = h_ref[...]
        r = lax.rsqrt(jnp.mean(x * x, axis=-1, keepdims=True) + EPS)
        xh = x * r
        dxh = dn * g_ref[...]
        dh = dhi_ref[...] + r * (dxh - xh * jnp.mean(dxh * xh, axis=-1, keepdims=True))
        dh_ref[...] = dh
        dhb_ref[...] = dh.astype(BF16)

        @pl.when(pl.program_id(0) == 0)
        def _():
            dg_ref[...] = jnp.zeros_like(dg_ref)

        dg_ref[...] += jnp.sum(dn * xh, axis=0, keepdims=True)

    row = pl.BlockSpec((tm, D), lambda i: (i, 0))
    wide = pl.BlockSpec((tm, F), lambda i: (i, 0))
    once = dict(pipeline_mode=pl.Buffered(1))
    return pl.pallas_call(
        body, grid=(M // tm,),
        in_specs=[row, wide, pl.BlockSpec((None, D, F), lambda i: (0, 0, 0), **once),
                  pl.BlockSpec((None, F, D), lambda i: (0, 0, 0), **once), row,
                  pl.BlockSpec((None, 1, D), lambda i: (glayer, 0, 0)), row] + [ANY_SPEC] * len(deps),
        out_specs=[wide, row, row, pl.BlockSpec((1, D), lambda i: (0, 0))],
        out_shape=[SDS((M, F), BF16), SDS((M, D), F32), SDS((M, D), BF16), SDS((1, D), F32)],
        name=name, compiler_params=_params("arbitrary"))(dhb, relu, w1, w2, h, g3, dh_in, *deps)


def _mm_nt_norm(dy, w3, layer, h, g3, glayer, dh_in, name, deps=()):
    M, N = dy.shape
    D = w3.shape[1]
    tm = _row_tile(M, 2 * N + 14 * D, 2 * N * D)

    def body(dy_ref, w_ref, h_ref, g_ref, dhi_ref, *rest):
        dh_ref, dhb_ref, dg_ref = rest[-3:]
        dn = lax.dot_general(dy_ref[...], w_ref[...], (((1,), (1,)), ((), ())), preferred_element_type=F32)
        x = h_ref[...]
        r = lax.rsqrt(jnp.mean(x * x, axis=-1, keepdims=True) + EPS)
        xh = x * r
        dxh = dn * g_ref[...]
        dh = dhi_ref[...] + r * (dxh - xh * jnp.mean(dxh * xh, axis=-1, keepdims=True))
        dh_ref[...] = dh
        dhb_ref[...] = dh.astype(BF16)

        @pl.when(pl.program_id(0) == 0)
        def _():
            dg_ref[...] = jnp.zeros_like(dg_ref)

        dg_ref[...] += jnp.sum(dn * xh, axis=0, keepdims=True)

    row = pl.BlockSpec((tm, D), lambda i: (i, 0))
    return pl.pallas_call(
        body, grid=(M // tm,),
        in_specs=[pl.BlockSpec((tm, N), lambda i: (i, 0)), pl.BlockSpec((None, D, N), lambda i: (layer, 0, 0)), row,
                  pl.BlockSpec((None, 1, D), lambda i: (glayer, 0, 0)), row] + [ANY_SPEC] * len(deps),
        out_specs=[row, row, pl.BlockSpec((1, D), lambda i: (0, 0))],
        out_shape=[SDS((M, D), F32), SDS((M, D), BF16), SDS((1, D), F32)],
        name=name, compiler_params=_params("arbitrary"))(dy, w3, h, g3, dh_in, *deps)


def _fam_dims(kind, K, N):
    return (K // 2, N // N_CHIPS) if kind == "col" else (K // (2 * N_CHIPS), N)


def _mm_tn(x, dy, kind, name, square=False):
    M, K = x.shape
    N = dy.shape[1]
    nr, nc = _fam_dims(kind, K, N)

    def body(x_ref, dy_ref, o_ref):
        lhs = x_ref[...]
        res = lax.dot_general(lhs * lhs if square else lhs, dy_ref[...], (((0,), (0,)), ((), ())), preferred_element_type=F32)
        o_ref[...] = res.astype(BF16).reshape(o_ref.shape)

    if kind == "col":
        tn = _ctile(nc)
        ct = nc // tn
        grid = (N // tn,)
        in_specs = [pl.BlockSpec((M, K), lambda j: (0, 0)), pl.BlockSpec((M, tn), lambda j: (0, j))]
        out_spec = pl.BlockSpec((2, None, nr, tn), lambda j: (0, j // ct, 0, j % ct))
    else:
        grid = (N_CHIPS,)
        in_specs = [pl.BlockSpec((M, 2 * nr), lambda i: (0, i)), pl.BlockSpec((M, N), lambda i: (0, 0))]
        out_spec = pl.BlockSpec((2, None, nr, N), lambda i: (0, i, 0, 0))
    return pl.pallas_call(
        body, grid=grid, in_specs=in_specs, out_specs=out_spec, out_shape=SDS((2, N_CHIPS, nr, nc), BF16),
        name=name, compiler_params=_params("parallel"))(x, dy)


C_EVEN = 512


def _live(rows, base, total):
    r = _row_ids((rows, 1), base)
    return jnp.logical_and(r >= PAD, r < total).astype(F32)


def _conv_taps(win, w_ref, ls, acc, flip):
    for b in range(8):
        rb = win if b == 0 else pltpu.roll(win, 96 - b, 0)
        for a in range(5):
            o = 8 * a + b
            tap = (30 - o) if flip else (o - 2)
            if 0 <= tap < CONV_WIDTH:
                acc = acc + w_ref[pl.ds(tap, 1), ls] * rb[8 * a:8 * a + CHUNK]
    return acc


def _window_sum(win, levels, forward):
    s = win
    n = win.shape[0]
    for k in range(levels):
        step = 1 << k
        s = s + pltpu.roll(s, (n - step) if forward else step, 0)
    return s


def _pool_count(base, g):
    pos = _row_ids((CHUNK, 1), base) - PAD
    return jnp.clip(pos + 1, 1, POOL_WINDOWS[g]).astype(F32)


def _even_fwd(u, cw3, cb3, lg3, lb3, pw4, pb3, ps3, j, name):
    T = u.shape[0]
    C = C_EVEN
    tm = _tile(T, 320)
    nch = tm // CHUNK
    nblk = T // CHUNK

    def body(u_ref, up_ref, cw_ref, cb_ref, lg_ref, lb_ref, pw_ref, pb_ref, ps_ref, o_ref, yc_ref, a_s, p_s, yc_s):
        row0 = pl.program_id(0) * tm
        up = up_ref[...]
        lp = _live(CHUNK, row0 - CHUNK, T)
        a_s[0:CHUNK, :] = up[:, 0:C] * _sigmoid(up[:, C:2 * C]) * lp
        p_s[0:CHUNK, :] = up[:, 2 * C:3 * C] * lp

        def stage(c, _):
            rs = _mult(c * CHUNK, CHUNK)
            lv = _live(CHUNK, row0 + rs, T)
            a_s[pl.ds(rs + CHUNK, CHUNK), :] = u_ref[pl.ds(rs, CHUNK), 0:C] * _sigmoid(u_ref[pl.ds(rs, CHUNK), C:2 * C]) * lv
            p_s[pl.ds(rs + CHUNK, CHUNK), :] = u_ref[pl.ds(rs, CHUNK), 2 * C:3 * C] * lv
            return 0

        for c in range(nch):
            stage(c, 0)

        def chunk(c, _):
            rs = _mult(c * CHUNK, CHUNK)
            lv = _live(CHUNK, row0 + rs, T)
            for cb in range(4):
                ls = slice(cb * 128, (cb + 1) * 128)
                win = a_s[pl.ds(_mult(rs + 32, 32), 96), ls]
                acc = jnp.broadcast_to(cb_ref[:, ls], (CHUNK, 128))
                yc_s[:, ls] = _conv_taps(win, cw_ref, ls, acc, False)
            y = yc_s[...]
            yc_ref[pl.ds(rs, CHUNK), :] = y
            xc = y - jnp.mean(y, axis=-1, keepdims=True)
            yn = xc * lax.rsqrt(jnp.mean(xc * xc, axis=-1, keepdims=True) + EPS) * lg_ref[...] + lb_ref[...]
            o_ref[pl.ds(rs, CHUNK), 0:C] = (yn * _sigmoid(yn) * lv).astype(BF16)
            for g in range(4):
                ls = slice(g * 128, (g + 1) * 128)
                win = p_s[pl.ds(_mult(rs + 48, 16), 80), ls]
                s = _window_sum(win, g + 1, False)
                d = s[16:80] / _pool_count(row0 + rs, g) - win[16:80]
                yv = jnp.dot(d.astype(BF16), pw_ref[g].astype(BF16), preferred_element_type=F32) + pb_ref[:, ls]
                o_ref[pl.ds(rs, CHUNK), C + g * 128:C + (g + 1) * 128] = (yv * ps_ref[:, ls] * lv).astype(BF16)
            return 0

        for c in range(nch):
            chunk(c, 0)

    vec = pl.BlockSpec((None, 1, C), lambda i: (j, 0, 0))
    return pl.pallas_call(
        body, grid=(T // tm,),
        in_specs=[pl.BlockSpec((tm, 3 * C), lambda i: (i, 0)),
                  pl.BlockSpec((CHUNK, 3 * C), lambda i: (jnp.maximum(i * nch - 1, 0), 0)),
                  pl.BlockSpec((None, CONV_ROWS, C), lambda i: (j, 0, 0)), vec, vec, vec,
                  pl.BlockSpec((None, 4, 128, 128), lambda i: (j, 0, 0, 0)), vec, vec],
        out_specs=[pl.BlockSpec((tm, 2 * C), lambda i: (i, 0)), pl.BlockSpec((tm, C), lambda i: (i, 0))],
        out_shape=[SDS((T, 2 * C), BF16), SDS((T, C), F32)],
        scratch_shapes=[pltpu.VMEM((tm + CHUNK, C), F32), pltpu.VMEM((tm + CHUNK, C), F32), pltpu.VMEM((CHUNK, C), F32)],
        name=name, compiler_params=_params("parallel"))(u, u, cw3, cb3, lg3, lb3, pw4, pb3, ps3)


def _even_bwd(u, yc, dy, cw3, cb3, lg3, lb3, pw4, pb3, ps3, j, name):
    T = u.shape[0]
    C = C_EVEN
    tm = _tile(T, 320)
    nch = tm // CHUNK
    nblk = T // CHUNK
    ntile = T // tm

    def body(u_ref, up_ref, un_ref, yc_ref, ycn_ref, dy_ref, dyn_ref, cw_ref, cb_ref, lg_ref, lb_ref, pw_ref, pb_ref, ps_ref,
             du_ref, dcw_ref, dcb_ref, dlg_ref, dlb_ref, dpw_ref, dpb_ref, dps_ref,
             a_s, p_s, dy_s, dyc_s, dd_s, ddc_s, dw_s):
        i = pl.program_id(0)
        row0 = i * tm

        @pl.when(i == 0)
        def _():
            for ref in (dcb_ref, dlg_ref, dlb_ref, dpw_ref, dpb_ref, dps_ref, dw_s):
                ref[...] = jnp.zeros_like(ref)

        up = up_ref[...]
        lp = _live(CHUNK, row0 - CHUNK, T)
        a_s[0:CHUNK, :] = up[:, 0:C] * _sigmoid(up[:, C:2 * C]) * lp
        p_s[0:CHUNK, :] = up[:, 2 * C:3 * C] * lp
        ln_ = _live(CHUNK, row0 + tm, T)
        p_s[tm + CHUNK:tm + 2 * CHUNK, :] = un_ref[:, 2 * C:3 * C] * ln_
        dy_s[tm:tm + CHUNK, :] = dyn_ref[...] * ln_
        dyc_s[tm + CHUNK:tm + CHUNK + 32, :] = jnp.zeros((32, C), F32)

        def stage(c, _):
            rs = _mult(c * CHUNK, CHUNK)
            lv = _live(CHUNK, row0 + rs, T)
            a_s[pl.ds(rs + CHUNK, CHUNK), :] = u_ref[pl.ds(rs, CHUNK), 0:C] * _sigmoid(u_ref[pl.ds(rs, CHUNK), C:2 * C]) * lv
            p_s[pl.ds(rs + CHUNK, CHUNK), :] = u_ref[pl.ds(rs, CHUNK), 2 * C:3 * C] * lv
            dy_s[pl.ds(rs, CHUNK), :] = dy_ref[pl.ds(rs, CHUNK), :] * lv
            return 0

        for c in range(nch):
            stage(c, 0)

        def first(rs, y, own):
            xc = y - jnp.mean(y, axis=-1, keepdims=True)
            rstd = lax.rsqrt(jnp.mean(xc * xc, axis=-1, keepdims=True) + EPS)
            xh = xc * rstd
            yn = xh * lg_ref[...] + lb_ref[...]
            sg = _sigmoid(yn)
            dyn = dy_s[pl.ds(rs, CHUNK), 0:C] * (sg * (1.0 + yn * (1.0 - sg)))
            dlg_ref[...] += jnp.sum(dyn * xh, axis=0, keepdims=True) * own
            dlb_ref[...] += jnp.sum(dyn, axis=0, keepdims=True) * own
            dxh = dyn * lg_ref[...]
            dyc = rstd * (dxh - jnp.mean(dxh, axis=-1, keepdims=True) - xh * jnp.mean(dxh * xh, axis=-1, keepdims=True))
            dyc_s[pl.ds(rs, CHUNK), :] = dyc
            dcb_ref[...] += jnp.sum(dyc, axis=0, keepdims=True) * own
            for g in range(4):
                ls = slice(g * 128, (g + 1) * 128)
                win = p_s[pl.ds(rs + 48, 80), ls]
                s = _window_sum(win, g + 1, False)
                cnt = _pool_count(row0 + rs, g)
                d = (s[16:80] / cnt - win[16:80]).astype(BF16)
                w = pw_ref[g].astype(BF16)
                pre = jnp.dot(d, w, preferred_element_type=F32) + pb_ref[:, ls]
                dyb = dy_s[pl.ds(rs, CHUNK), C + g * 128:C + (g + 1) * 128]
                dpre = dyb * ps_ref[:, ls]
                dps_ref[:, ls] += jnp.sum(dyb * pre, axis=0, keepdims=True) * own
                dpb_ref[:, ls] += jnp.sum(dpre, axis=0, keepdims=True) * own
                dpre_b = (dpre * own).astype(BF16)
                dpw_ref[g] += lax.dot_general(d, dpre_b, (((0,), (0,)), ((), ())), preferred_element_type=F32)
                dd = lax.dot_general(dpre.astype(BF16), w, (((1,), (1,)), ((), ())), preferred_element_type=F32)
                dd_s[pl.ds(rs, CHUNK), ls] = dd
                ddc_s[pl.ds(rs, CHUNK), ls] = dd / cnt

        def first_in_tile(c, _):
            rs = _mult(c * CHUNK, CHUNK)
            first(rs, yc_ref[pl.ds(rs, CHUNK), :], 1.0)
            return 0

        for c in range(nch):
            first_in_tile(c, 0)
        first(tm, ycn_ref[...], 0.0)
        ddc_s[tm + CHUNK:tm + CHUNK + 16, :] = jnp.zeros((16, C), F32)

        def second(c, _):
            rs = _mult(c * CHUNK, CHUNK)
            lv = _live(CHUNK, row0 + rs, T)
            for cb in range(4):
                ls = slice(cb * 128, (cb + 1) * 128)
                wd = dyc_s[pl.ds(rs, 96), ls]
                da = _conv_taps(wd, cw_ref, ls, jnp.zeros((CHUNK, 128), F32), True)
                wa = a_s[pl.ds(_mult(rs + 32, 32), 96), ls]
                dyc = dyc_s[pl.ds(rs, CHUNK), ls]
                for b in range(8):
                    rb = wa if b == 0 else pltpu.roll(wa, 96 - b, 0)
                    for a in range(5):
                        tap = 8 * a + b - 2
                        if 0 <= tap < CONV_WIDTH:
                            prod = dyc * rb[8 * a:8 * a + CHUNK]
                            part = prod[0:8]
                            for q in range(1, 8):
                                part = part + prod[8 * q:8 * q + 8]
                            dw_s[8 * tap:8 * tap + 8, ls] += part
                val = u_ref[pl.ds(rs, CHUNK), ls]
                sg = _sigmoid(u_ref[pl.ds(rs, CHUNK), C + cb * 128:C + (cb + 1) * 128])
                du_ref[pl.ds(rs, CHUNK), ls] = (da * sg * lv).astype(BF16)
                du_ref[pl.ds(rs, CHUNK), C + cb * 128:C + (cb + 1) * 128] = (da * val * sg * (1.0 - sg) * lv).astype(BF16)
            for g in range(4):
                ls = slice(g * 128, (g + 1) * 128)
                z = _window_sum(ddc_s[pl.ds(rs, 80), ls], g + 1, True)
                dpin = (z[0:CHUNK] - dd_s[pl.ds(rs, CHUNK), ls]) * lv
                du_ref[pl.ds(rs, CHUNK), 2 * C + g * 128:2 * C + (g + 1) * 128] = dpin.astype(BF16)
            return 0

        for c in range(nch):
            second(c, 0)

        @pl.when(i == ntile - 1)
        def _():
            for tap in range(CONV_WIDTH):
                dcw_ref[tap:tap + 1, :] = jnp.sum(dw_s[8 * tap:8 * tap + 8, :], axis=0, keepdims=True)
            dcw_ref[CONV_WIDTH:CONV_ROWS, :] = jnp.zeros((CONV_ROWS - CONV_WIDTH, C), F32)

    vec = pl.BlockSpec((None, 1, C), lambda i: (j, 0, 0))
    ovec = pl.BlockSpec((1, C), lambda i: (0, 0))
    return pl.pallas_call(
        body, grid=(ntile,),
        in_specs=[pl.BlockSpec((tm, 3 * C), lambda i: (i, 0)),
                  pl.BlockSpec((CHUNK, 3 * C), lambda i: (jnp.maximum(i * nch - 1, 0), 0)),
                  pl.BlockSpec((CHUNK, 3 * C), lambda i: (jnp.minimum((i + 1) * nch, nblk - 1), 0)),
                  pl.BlockSpec((tm, C), lambda i: (i, 0)),
                  pl.BlockSpec((CHUNK, C), lambda i: (jnp.minimum((i + 1) * nch, nblk - 1), 0)),
                  pl.BlockSpec((tm, 2 * C), lambda i: (i, 0)),
                  pl.BlockSpec((CHUNK, 2 * C), lambda i: (jnp.minimum((i + 1) * nch, nblk - 1), 0)),
                  pl.BlockSpec((None, CONV_ROWS, C), lambda i: (j, 0, 0)), vec, vec, vec,
                  pl.BlockSpec((None, 4, 128, 128), lambda i: (j, 0, 0, 0)), vec, vec],
        out_specs=[pl.BlockSpec((tm, 3 * C), lambda i: (i, 0)), pl.BlockSpec((CONV_ROWS, C), lambda i: (0, 0)),
                   ovec, ovec, ovec, pl.BlockSpec((4, 128, 128), lambda i: (0, 0, 0)), ovec, ovec],
        out_shape=[SDS((T, 3 * C), BF16), SDS((CONV_ROWS, C), F32), SDS((1, C), F32), SDS((1, C), F32), SDS((1, C), F32),
                   SDS((4, 128, 128), F32), SDS((1, C), F32), SDS((1, C), F32)],
        scratch_shapes=[pltpu.VMEM((tm + CHUNK, C), F32), pltpu.VMEM((tm + 2 * CHUNK, C), F32),
                        pltpu.VMEM((tm + CHUNK, 2 * C), F32),
                        pltpu.VMEM((tm + CHUNK + 32, C), F32), pltpu.VMEM((tm + CHUNK, C), F32),
                        pltpu.VMEM((tm + CHUNK + 16, C), F32), pltpu.VMEM((8 * CONV_ROWS, C), F32)],
        name=name, compiler_params=_params("arbitrary"))(u, u, u, yc, yc, dy, dy, cw3, cb3, lg3, lb3, pw4, pb3, ps3)


HI = lax.Precision.HIGHEST


def _dot_nt(a, b):
    return lax.dot_general(a, b, (((1,), (1,)), ((), ())), preferred_element_type=F32)


def _dot_tn(a, b):
    return lax.dot_general(a, b, (((0,), (0,)), ((), ())), preferred_element_type=F32)


def _tri(lower):
    r = lax.broadcasted_iota(jnp.int32, (CHUNK, CHUNK), 0)
    c = lax.broadcasted_iota(jnp.int32, (CHUNK, CHUNK), 1)
    return jnp.where((c <= r) if lower else (c >= r), 1.0, 0.0).astype(F32)


def _hgrn_gates(u_ref, lb_ref, h, D, lv):
    ls = slice(h * HEAD_DIM, (h + 1) * HEAD_DIM)
    qraw = u_ref[:, ls]
    fraw = u_ref[:, D + h * HEAD_DIM:D + (h + 1) * HEAD_DIM]
    v = u_ref[:, 2 * D + h * HEAD_DIM:2 * D + (h + 1) * HEAD_DIM] * lv
    lbv = lb_ref[:, ls]
    sig = _sigmoid(fraw)
    forget = lbv + (1.0 - lbv) * sig
    logf = jnp.log(forget) * lv
    k = (1.0 - forget) * lv
    qsig = _sigmoid(qraw)
    q = qraw * qsig * lv
    return q, k, v, logf, (qraw, qsig, sig, forget, lbv)


def _sub_parts(q, k, b, b_s, I):
    rows = slice(SUB * I, SUB * (I + 1))
    rho = jnp.zeros((1, HEAD_DIM), F32) if I == 0 else b_s[SUB * I - 1:SUB * I, :]
    eI = jnp.exp(b[rows] - rho)
    EI = jnp.exp(jnp.minimum(rho - b, EXP_CAP))
    causal = (lax.broadcasted_iota(jnp.int32, (SUB, CHUNK), 1)
              <= lax.broadcasted_iota(jnp.int32, (SUB, CHUNK), 0) + SUB * I)
    return rows, q[rows] * eI, k * EI, eI, EI, causal


def _chunks_per_step(NC):
    for n in (5, 4, 3, 2):
        if NC % n == 0:
            return n
    return 1


def _hgrn_fwd(u, lb3, layer, gn3, j, name):
    T = u.shape[0]
    D = u.shape[1] // 4
    H = D // HEAD_DIM
    NC = T // CHUNK
    CH = _chunks_per_step(NC)
    R = CH * CHUNK

    def body(u_ref, lb_ref, gn_ref, y_ref, o_ref, sall_ref, st_s, b_s, lf_s, q_s, k_s):
        n = pl.program_id(0)

        @pl.when(n == 0)
        def _():
            st_s[...] = jnp.zeros_like(st_s)

        heads = range(H)
        cols = [slice(h * HEAD_DIM, (h + 1) * HEAD_DIM) for h in heads]
        rows = [slice(c * CHUNK, (c + 1) * CHUNK) for c in range(CH)]
        vb = {}
        for c in range(CH):
            lv = _live(CHUNK, (n * CH + c) * CHUNK, T)
            for h in heads:
                q, k, v, logf, _ = _hgrn_gates(u_ref.at[rows[c]], lb_ref, h, D, lv)
                q_s[rows[c], cols[h]] = q
                k_s[rows[c], cols[h]] = k
                lf_s[rows[c], cols[h]] = logf
                vb[c, h] = v.astype(BF16)
        for c in range(CH):
            b_s[rows[c], :] = jnp.dot(_tri(True), lf_s[rows[c], :], precision=HI, preferred_element_type=F32)
        ops = {}
        for c in range(CH):
            for h in heads:
                b_h = b_s.at[rows[c], cols[h]]
                b = b_h[...]
                q = q_s[rows[c], cols[h]]
                k = k_s[rows[c], cols[h]]
                blast = b_h[CHUNK - 1:CHUNK, :]
                qh = (q * jnp.exp(b)).astype(BF16)
                kt = (k * jnp.exp(blast - b)).astype(BF16)
                subs = []
                for I in range(CHUNK // SUB):
                    _, qI, KI, _, _, causal = _sub_parts(q, k, b, b_h, I)
                    subs.append((qI.astype(BF16), KI.astype(BF16), causal))
                ops[c, h] = (qh, kt, jnp.exp(blast), subs)
        mm = {}
        for h in heads:
            st = st_s[h]
            for c in range(CH):
                qh, kt, eblast, subs = ops[c, h]
                sall_ref[c, h] = st
                o_inter = _dot_nt(qh, st.astype(BF16))
                st = st * eblast + _dot_tn(vb[c, h], kt)
                mm[c, h] = (o_inter, [_dot_nt(qI, KI) for qI, KI, _ in subs])
            st_s[h] = st
        for c in range(CH):
            for h in heads:
                o_inter, ps = mm[c, h]
                p = jnp.concatenate([jnp.where(m, x, 0.0) for x, (_, _, m) in zip(ps, ops[c, h][3])], axis=0).astype(BF16)
                o = o_inter + jnp.dot(p, vb[c, h], preferred_element_type=F32)
                o_ref[rows[c], cols[h]] = o
                graw = u_ref[rows[c], 3 * D + h * HEAD_DIM:3 * D + (h + 1) * HEAD_DIM]
                r = lax.rsqrt(jnp.mean(o * o, axis=-1, keepdims=True) + EPS)
                y_ref[rows[c], cols[h]] = (((o * r) * gn_ref[...]) * (graw * _sigmoid(graw))).astype(BF16)

    return pl.pallas_call(
        body, grid=(NC // CH,),
        in_specs=[pl.BlockSpec((R, 4 * D), lambda n: (n, 0)),
                  pl.BlockSpec((None, 1, D), lambda n: (layer, 0, 0)),
                  pl.BlockSpec((None, 1, HEAD_DIM), lambda n: (j, 0, 0))],
        out_specs=[pl.BlockSpec((R, D), lambda n: (n, 0)), pl.BlockSpec((R, D), lambda n: (n, 0)),
                   pl.BlockSpec((CH, H, HEAD_DIM, HEAD_DIM), lambda n: (n, 0, 0, 0))],
        out_shape=[SDS((T, D), BF16), SDS((T, D), F32), SDS((NC, H, HEAD_DIM, HEAD_DIM), F32)],
        scratch_shapes=[pltpu.VMEM((H, HEAD_DIM, HEAD_DIM), F32)] + [pltpu.VMEM((R, D), F32)] * 4,
        name=name, compiler_params=_params("arbitrary"))(u, lb3, gn3)


def _hgrn_bwd(u, o_raw, dy, sall, lb3, layer, gn3, j, name):
    T = u.shape[0]
    D = u.shape[1] // 4
    H = D // HEAD_DIM
    NC = T // CHUNK
    CH = _chunks_per_step(NC)
    R = CH * CHUNK
    NS = NC // CH

    def body(u_ref, o_ref, dy_ref, sall_ref, lb_ref, gn_ref, du_ref, dlb_ref, dgn_ref, dst_s, b_s, lf_s, q_s, k_s, db_s, dk_s):
        step = pl.program_id(0)
        n = NS - 1 - step

        @pl.when(step == 0)
        def _():
            dst_s[...] = jnp.zeros_like(dst_s)
            dlb_ref[...] = jnp.zeros_like(dlb_ref)
            dgn_ref[...] = jnp.zeros_like(dgn_ref)

        last_row = (_row_ids((CHUNK, 1), 0) == CHUNK - 1).astype(F32)
        gn = gn_ref[...]
        heads = range(H)
        chunks = range(CH)
        cols = [slice(h * HEAD_DIM, (h + 1) * HEAD_DIM) for h in heads]
        rows = [slice(c * CHUNK, (c + 1) * CHUNK) for c in chunks]
        lv = [_live(CHUNK, (n * CH + c) * CHUNK, T) for c in chunks]
        vb, dob = {}, {}
        dgn = jnp.zeros((1, HEAD_DIM), F32)
        for c in chunks:
            for h in heads:
                q, k, v, logf, _ = _hgrn_gates(u_ref.at[rows[c]], lb_ref, h, D, lv[c])
                q_s[rows[c], cols[h]] = q
                k_s[rows[c], cols[h]] = k
                lf_s[rows[c], cols[h]] = logf
                vb[c, h] = v.astype(BF16)
                graw = u_ref[rows[c], 3 * D + h * HEAD_DIM:3 * D + (h + 1) * HEAD_DIM]
                gsig = _sigmoid(graw)
                o = o_ref[rows[c], cols[h]]
                r = lax.rsqrt(jnp.mean(o * o, axis=-1, keepdims=True) + EPS)
                xh = o * r
                dyv = dy_ref[rows[c], cols[h]]
                dsg = dyv * (graw * gsig)
                dgn = dgn + jnp.sum(dsg * xh, axis=0, keepdims=True)
                dxh = dsg * gn
                do = r * (dxh - xh * jnp.mean(dxh * xh, axis=-1, keepdims=True))
                dob[c, h] = do.astype(BF16)
                dgraw = dyv * xh * gn * (gsig * (1.0 + graw * (1.0 - gsig)))
                du_ref[rows[c], 3 * D + h * HEAD_DIM:3 * D + (h + 1) * HEAD_DIM] = (dgraw * lv[c]).astype(BF16)
        dgn_ref[...] += dgn
        for c in chunks:
            b_s[rows[c], :] = jnp.dot(_tri(True), lf_s[rows[c], :], precision=HI, preferred_element_type=F32)
        ops = {}
        for c in chunks:
            for h in heads:
                b_h = b_s.at[rows[c], cols[h]]
                b = b_h[...]
                q = q_s[rows[c], cols[h]]
                k = k_s[rows[c], cols[h]]
                blast = b_h[CHUNK - 1:CHUNK, :]
                eb = jnp.exp(b)
                ekb = jnp.exp(blast - b)
                subs = []
                for I in range(CHUNK // SUB):
                    rws, qI, KI, eI, EI, causal = _sub_parts(q, k, b, b_h, I)
                    subs.append((rws, qI.astype(BF16), KI.astype(BF16), eI, EI, causal))
                ops[c, h] = (eb, ekb, jnp.exp(blast), (q * eb).astype(BF16), (k * ekb).astype(BF16), subs)
        mm = {}
        for h in heads:
            dst = dst_s[h]
            for c in reversed(chunks):
                eb, ekb, eblast, qhb, ktb, subs = ops[c, h]
                st = sall_ref[c, h]
                dstb = dst.astype(BF16)
                dv = _dot_nt(ktb, dstb)
                dqh = jnp.dot(dob[c, h], st.astype(BF16), preferred_element_type=F32)
                dkt = jnp.dot(vb[c, h], dstb, preferred_element_type=F32)
                dblast = jnp.sum(dst * st, axis=0, keepdims=True) * eblast
                dst = dst * eblast + _dot_tn(dob[c, h], qhb)
                dp_full = _dot_nt(dob[c, h], vb[c, h])
                ps = [_dot_nt(qIb, KIb) for _, qIb, KIb, _, _, _ in subs]
                mm[c, h] = (dv, dqh, dkt, dblast, dp_full, ps)
            dst_s[h] = dst
        for c in chunks:
            for h in heads:
                eb, ekb, eblast, qhb, ktb, subs = ops[c, h]
                dv, dqh, dkt, dblast, dp_full, ps = mm[c, h]
                p = jnp.concatenate([jnp.where(sub[5], x, 0.0) for x, sub in zip(ps, subs)], axis=0).astype(BF16)
                dv = dv + _dot_tn(p, dob[c, h])
                du_ref[rows[c], 2 * D + h * HEAD_DIM:2 * D + (h + 1) * HEAD_DIM] = (dv * lv[c]).astype(BF16)
                dq = dqh * eb
                db = dqh * qhb.astype(F32)
                tmp = dkt * ktb.astype(F32)
                dk = dkt * ekb
                db = db - tmp
                dblast = dblast + jnp.sum(tmp, axis=0, keepdims=True)
                dq_parts, db_parts = [], []
                for rws, qIb, KIb, eI, EI, causal in subs:
                    dp = jnp.where(causal, dp_full[rws], 0.0).astype(BF16)
                    dqI = jnp.dot(dp, KIb, preferred_element_type=F32)
                    dKI = _dot_tn(dp, qIb)
                    dq_parts.append(dqI * eI)
                    db_parts.append(dqI * qIb.astype(F32))
                    dk = dk + dKI * EI
                    db = db - dKI * KIb.astype(F32)
                dq = dq + jnp.concatenate(dq_parts, axis=0)
                db_s[rows[c], cols[h]] = db + jnp.concatenate(db_parts, axis=0) + last_row * dblast
                dk_s[rows[c], cols[h]] = dk
                qraw = u_ref[rows[c], cols[h]]
                qsig = _sigmoid(qraw)
                du_ref[rows[c], cols[h]] = (dq * (qsig * (1.0 + qraw * (1.0 - qsig))) * lv[c]).astype(BF16)
        for c in chunks:
            lf_s[rows[c], :] = jnp.dot(_tri(False), db_s[rows[c], :], precision=HI, preferred_element_type=F32)
        for h in heads:
            lbv = lb_ref[:, cols[h]]
            dlb = jnp.zeros((1, HEAD_DIM), F32)
            for c in chunks:
                fraw = u_ref[rows[c], D + h * HEAD_DIM:D + (h + 1) * HEAD_DIM]
                sig = _sigmoid(fraw)
                forget = lbv + (1.0 - lbv) * sig
                dforget = (lf_s[rows[c], cols[h]] / forget - dk_s[rows[c], cols[h]]) * lv[c]
                dlb = dlb + jnp.sum(dforget * (1.0 - sig), axis=0, keepdims=True)
                du_ref[rows[c], D + h * HEAD_DIM:D + (h + 1) * HEAD_DIM] = (dforget * (1.0 - lbv) * sig * (1.0 - sig)).astype(BF16)
            dlb_ref[:, cols[h]] += dlb

    rev = lambda s: (NS - 1 - s, 0)
    return pl.pallas_call(
        body, grid=(NS,),
        in_specs=[pl.BlockSpec((R, 4 * D), rev), pl.BlockSpec((R, D), rev), pl.BlockSpec((R, D), rev),
                  pl.BlockSpec((CH, H, HEAD_DIM, HEAD_DIM), lambda s: (NS - 1 - s, 0, 0, 0)),
                  pl.BlockSpec((None, 1, D), lambda s: (layer, 0, 0)),
                  pl.BlockSpec((None, 1, HEAD_DIM), lambda s: (j, 0, 0))],
        out_specs=[pl.BlockSpec((R, 4 * D), rev), pl.BlockSpec((1, D), lambda s: (0, 0)),
                   pl.BlockSpec((1, HEAD_DIM), lambda s: (0, 0))],
        out_shape=[SDS((T, 4 * D), BF16), SDS((1, D), F32), SDS((1, HEAD_DIM), F32)],
        scratch_shapes=[pltpu.VMEM((H, HEAD_DIM, HEAD_DIM), F32)] + [pltpu.VMEM((R, D), F32)] * 6,
        name=name, compiler_params=_params("arbitrary"))(u, o_raw, dy, sall, lb3, gn3)


def _softmax_layers(p_ref, n_layers):
    rows = [p_ref[l:l + 1, :] for l in range(n_layers)]
    m = functools.reduce(jnp.maximum, rows)
    e = [jnp.exp(x - m) for x in rows]
    tot = functools.reduce(lambda a, b: a + b, e)
    return [x / tot for x in e]


def _lb_fwd(p):
    n_layers, D = p.shape

    def body(p_ref, o_ref):
        s = _softmax_layers(p_ref, n_layers)
        acc = jnp.zeros((1, D), F32)
        o_ref[0:1, :] = acc
        for l in range(1, n_layers):
            acc = acc + s[l]
            o_ref[l:l + 1, :] = acc

    return pl.pallas_call(body, out_shape=SDS(p.shape, F32), name="lb_fwd")(p)


def _lb_bwd(p, dlb):
    n_layers, D = p.shape

    def body(p_ref, d_ref, o_ref):
        s = _softmax_layers(p_ref, n_layers)
        ds = [jnp.zeros((1, D), F32)] * n_layers
        acc = jnp.zeros((1, D), F32)
        for l in range(n_layers - 1, 0, -1):
            acc = acc + d_ref[l:l + 1, :]
            ds[l] = acc
        dot = functools.reduce(lambda a, b: a + b, [s[l] * ds[l] for l in range(n_layers)])
        for l in range(n_layers):
            o_ref[l:l + 1, :] = s[l] * (ds[l] - dot)

    return pl.pallas_call(body, out_shape=SDS(p.shape, F32), name="lb_bwd")(p, dlb)


def _adamw(w, g, m, v, name):
    R, C = w.shape
    tr = _tile(R, 256, 8) if R % 8 == 0 else R

    def body(w_ref, g_ref, m_ref, v_ref, d_ref, mo_ref, vo_ref):
        g_ = g_ref[...]
        m_ = ADAM_B1 * m_ref[...] + (1.0 - ADAM_B1) * g_
        v_ = ADAM_B2 * v_ref[...] + (1.0 - ADAM_B2) * (g_ * g_)
        mh = m_ / (1.0 - ADAM_B1 ** ADAM_STEP)
        vh = v_ / (1.0 - ADAM_B2 ** ADAM_STEP)
        d_ref[...] = -ADAM_LR * (mh / (jnp.sqrt(vh) + ADAM_EPS) + ADAM_WD * w_ref[...])
        mo_ref[...] = m_
        vo_ref[...] = v_

    blk = pl.BlockSpec((tr, C), lambda i: (i, 0))
    return pl.pallas_call(
        body, grid=(R // tr,), in_specs=[blk] * 4, out_specs=[blk] * 3, out_shape=[SDS((R, C), F32)] * 3,
        name=name, compiler_params=_params("parallel"))(w, g, m, v)


def _adamw_layer(w3, m3, v3, g2, layer, outs, name):
    L, R, C = w3.shape
    tr = _tile(R, 256, 8)
    if outs is None:
        outs = tuple(lax.empty(w3.shape, F32) for _ in range(4))

    def body(w_ref, m_ref, v_ref, g_ref, a0, a1, a2, a3, go_ref, d_ref, mo_ref, vo_ref):
        del a0, a1, a2, a3
        g_ = g_ref[...]
        m_ = ADAM_B1 * m_ref[...] + (1.0 - ADAM_B1) * g_
        v_ = ADAM_B2 * v_ref[...] + (1.0 - ADAM_B2) * (g_ * g_)
        mh = m_ / (1.0 - ADAM_B1 ** ADAM_STEP)
        vh = v_ / (1.0 - ADAM_B2 ** ADAM_STEP)
        go_ref[...] = g_
        d_ref[...] = -ADAM_LR * (mh / (jnp.sqrt(vh) + ADAM_EPS) + ADAM_WD * w_ref[...])
        mo_ref[...] = m_
        vo_ref[...] = v_

    lay = pl.BlockSpec((None, tr, C), lambda i: (layer, i, 0))
    return pl.pallas_call(
        body, grid=(R // tr,), in_specs=[lay] * 3 + [pl.BlockSpec((tr, C), lambda i: (i, 0))] + [ANY_SPEC] * 4,
        out_specs=[lay] * 4, out_shape=[SDS(w3.shape, F32)] * 4, input_output_aliases={4: 0, 5: 1, 6: 2, 7: 3},
        name=name, compiler_params=_params("parallel"))(w3, m3, v3, g2, *outs)


SEM_SPEC = pl.BlockSpec(memory_space=pltpu.SEMAPHORE)
HBM_SPEC = pl.BlockSpec(memory_space=pltpu.HBM)
EFFECT = pltpu.SideEffectType.DATAFLOW_SIDE_EFFECTING
N_DEV = 2 * N_CHIPS


def _position():
    x, y, c = lax.axis_index("x"), lax.axis_index("y"), lax.axis_index("c")
    chips = [(1 - x, y), (x, 1 - y), (1 - x, 1 - y)]
    return x, y, c, chips


def _split_start(name, plan, bufs, n_sems, deps=(), earlier=None):
    n = len(bufs)
    held = () if earlier is None else tuple(earlier[1:])

    def body(*refs):
        first_out = n + len(held) + len(deps)
        if earlier is not None:
            sends, recvs = earlier[0](refs[:n], refs[n], refs[n + 1])
            for kw in sends:
                pltpu.make_async_remote_copy(**kw).wait_send()
            for kw in recvs:
                pltpu.make_async_remote_copy(**kw).wait_recv()
        sends, _ = plan(refs[:n], refs[first_out], refs[first_out + 1])
        for kw in sends:
            pltpu.make_async_remote_copy(**kw).start()
        refs[-1][...] = jnp.zeros_like(refs[-1])

    out = pl.pallas_call(
        body, name=name,
        out_shape=(pltpu.SemaphoreType.DMA((n_sems,)), pltpu.SemaphoreType.DMA((n_sems,)),
                   *[pltpu.HBM(b.shape, b.dtype) for b in bufs], SDS((8, 128), F32)),
        in_specs=[HBM_SPEC] * n + [SEM_SPEC] * len(held) + [ANY_SPEC] * len(deps),
        out_specs=(SEM_SPEC, SEM_SPEC, *[HBM_SPEC] * n, pl.BlockSpec(memory_space=pltpu.VMEM)),
        input_output_aliases={i: 2 + i for i in range(n)},
        compiler_params=pltpu.CompilerParams(has_side_effects=EFFECT),
    )(*[pltpu.with_memory_space_constraint(b, pltpu.HBM) for b in bufs], *held, *deps)
    return out[0], out[1], list(out[2:2 + n]), out[-1]


def _split_wait(name, plan, send_sems, recv_sems, bufs, after=()):
    n = len(bufs)

    def body(*refs):
        sends, recvs = plan(refs[:n], refs[n], refs[n + 1])
        for kw in sends:
            pltpu.make_async_remote_copy(**kw).wait_send()
        for kw in recvs:
            pltpu.make_async_remote_copy(**kw).wait_recv()

    out = pl.pallas_call(
        body, name=name, out_shape=tuple(pltpu.HBM(b.shape, b.dtype) for b in bufs),
        in_specs=[HBM_SPEC] * n + [SEM_SPEC, SEM_SPEC] + [ANY_SPEC] * len(after),
        out_specs=tuple([HBM_SPEC] * n), input_output_aliases={i: i for i in range(n)},
        compiler_params=pltpu.CompilerParams(has_side_effects=EFFECT),
    )(*bufs, send_sems, recv_sems, *after)
    return list(out)


def _region(kind, ref, chip, half):
    K, N = ref.shape
    if kind == "col":
        return ref.at[pl.ds(half * (K // 2), K // 2), pl.ds(chip * (N // N_CHIPS), N // N_CHIPS)]
    rows = K // (2 * N_CHIPS)
    return ref.at[pl.ds((2 * chip + half) * rows, rows), :]


def _gather_plan(kinds, over_chips, first=0):
    def plan(refs, send_sems, recv_sems):
        x, y, c, chips = _position()
        sends, recvs = [], []
        for f, (ref, kind) in enumerate(zip(refs, kinds)):
            for k, chip in enumerate(chips):
                theirs = 2 * chip[0] + chip[1]
                at = 3 * (first + f) + k
                sem = dict(send_sem=send_sems.at[at], recv_sem=recv_sems.at[at], device_id_type=MESH)
                if over_chips:
                    out, back, to = _region(kind, ref, 2 * x + y, c), _region(kind, ref, theirs, c), (*chip, c)
                else:
                    out, back, to = _region(kind, ref, theirs, c), _region(kind, ref, theirs, 1 - c), (x, y, 1 - c)
                sends.append(dict(src_ref=out, dst_ref=out, device_id=to, **sem))
                recvs.append(dict(src_ref=back, dst_ref=back, device_id=to, **sem))
        return sends, recvs
    return plan


def _reduce_plan(refs, send_sems, recv_sems):
    x, y, c, _ = _position()
    me = 4 * x + 2 * y + c
    sends, recvs = [], []
    for f in range(len(refs) // 2):
        acc, land = refs[2 * f], refs[2 * f + 1]
        for d in range(1, N_DEV):
            t = (me + d) % N_DEV
            to = dict(device_id=(t // 4, (t // 2) % 2, t % 2), device_id_type=MESH)
            slot = N_DEV - 1 - d
            sends.append(dict(src_ref=acc.at[t % 2, t // 2], dst_ref=land.at[slot], send_sem=send_sems.at[7 * f + d - 1],
                              recv_sem=recv_sems.at[7 * f + slot], **to))
            recvs.append(dict(src_ref=land.at[d - 1], dst_ref=land.at[d - 1], send_sem=send_sems.at[7 * f + d - 1],
                              recv_sem=recv_sems.at[7 * f + d - 1], **to))
    return sends, recvs


def _swap_plan(refs, send_sems, recv_sems):
    x, y, c, _ = _position()
    sends, recvs = [], []
    for f, g in enumerate(refs):
        sem = dict(send_sem=send_sems.at[f], recv_sem=recv_sems.at[f], device_id=(x, y, 1 - c), device_id_type=MESH)
        sends.append(dict(src_ref=g.at[c], dst_ref=g.at[c], **sem))
        recvs.append(dict(src_ref=g.at[1 - c], dst_ref=g.at[1 - c], **sem))
    return sends, recvs


def _sum_pieces(ids2, acc, land, name):
    _, _, nr, nc = acc.shape
    tr = _tile(nr, 256, 16)

    def body(ids_ref, own_ref, land_ref, o_ref):
        del ids_ref
        s = own_ref[...].astype(F32)
        for k in range(N_DEV - 1):
            s = s + land_ref[k].astype(F32)
        o_ref[...] = s

    return pl.pallas_call(
        body,
        grid_spec=pltpu.PrefetchScalarGridSpec(
            num_scalar_prefetch=1, grid=(nr // tr,),
            in_specs=[pl.BlockSpec((None, None, tr, nc), lambda i, ids: (ids[0], ids[1], i, 0)),
                      pl.BlockSpec((N_DEV - 1, tr, nc), lambda i, ids: (0, i, 0))],
            out_specs=pl.BlockSpec((None, tr, nc), lambda i, ids: (ids[0], i, 0))),
        out_shape=SDS((2, nr, nc), F32), name=name, compiler_params=_params("parallel"))(ids2, acc, land)


def _small_plan(refs, send_sems, recv_sems):
    x, y, c, _ = _position()
    me = 4 * x + 2 * y + c
    own, land = refs
    sends, recvs = [], []
    for d in range(1, N_DEV):
        t = (me + d) % N_DEV
        to = dict(device_id=(t // 4, (t // 2) % 2, t % 2), device_id_type=MESH)
        sends.append(dict(src_ref=own, dst_ref=land.at[me], send_sem=send_sems.at[d - 1],
                          recv_sem=recv_sems.at[N_DEV - 1 - d], **to))
        recvs.append(dict(src_ref=land.at[t], dst_ref=land.at[t], send_sem=send_sems.at[d - 1],
                          recv_sem=recv_sems.at[d - 1], **to))
    return sends, recvs


def _sum_blocks(me1, own, land):
    def body(me_ref, own_ref, land_ref, o_ref):
        acc = None
        for d in range(N_DEV):
            term = jnp.where(me_ref[0] == d, own_ref[...], land_ref[d])
            acc = term if acc is None else acc + term
        o_ref[...] = acc

    return pl.pallas_call(
        body,
        grid_spec=pltpu.PrefetchScalarGridSpec(
            num_scalar_prefetch=1, grid=(1,),
            in_specs=[pl.BlockSpec(own.shape, lambda i, me: (0, 0)), pl.BlockSpec(land.shape, lambda i, me: (0, 0, 0))],
            out_specs=pl.BlockSpec(own.shape, lambda i, me: (0, 0))),
        out_shape=SDS(own.shape, F32), name="sum_small", compiler_params=_params("arbitrary"))(me1, own, land)


BIG = {"ev_w_in": "col", "ev_w_out": "row", "od_w_in": "col", "od_w_out": "row", "mlp_w1": "col", "mlp_w2": "row"}
WEIGHTS = ("meta_tokens", "mix_norm_g", "mlp_norm_g", "final_norm_g", "ev_w_in", "ev_conv_w", "ev_conv_b", "ev_ln_g",
           "ev_ln_b", "ev_pool_w", "ev_pool_b", "ev_pool_scale", "ev_w_out", "od_w_in", "od_gnorm_g", "od_w_out",
           "lb_param", "mlp_w1", "mlp_w2")
PACK_UNIT = 1024


def _mixer_names(layer):
    return ("ev_w_in", "ev_w_out") if layer % 2 == 0 else ("od_w_in", "od_w_out")


def _pack(arrays):
    flat = []
    for a in arrays:
        a = a.reshape(-1)
        flat.append(jnp.pad(a, (0, (-a.shape[0]) % PACK_UNIT)))
    return jnp.concatenate(flat).reshape(-1, 128)


def _unpack(packed, shapes):
    flat = packed.reshape(-1)
    out, off = [], 0
    for s in shapes:
        size = 1
        for d in s:
            size *= d
        out.append(flat[off:off + size].reshape(s))
        off += size + (-size) % PACK_UNIT
    return out


def _local_step(x2, target, P, weights, boundary, first_deps=()):
    D = x2.shape[1]
    n_layers = P["mix_norm_g"].shape[0]
    h = jnp.concatenate([jnp.zeros((PAD, D), F32), P["meta_full"], x2], axis=0)
    mix_g = P["mix_norm_g"].reshape(n_layers, 1, D)
    mlp_g = P["mlp_norm_g"].reshape(n_layers, 1, D)
    vec = lambda a: a.reshape(a.shape[0], 1, -1)
    cb3, lg3, lnb3, ps3 = vec(P["ev_conv_b"]), vec(P["ev_ln_g"]), vec(P["ev_ln_b"]), vec(P["ev_pool_scale"])
    pb3 = vec(P["ev_pool_b"])
    gn3 = vec(P["od_gnorm_g"])
    lb_all = _lb_fwd(P["lb_param"])
    lb3 = lb_all.reshape(n_layers, 1, D)
    even = (cb3, lg3, lnb3, P["ev_pool_w"], pb3, ps3)

    saved = []
    deps = tuple(first_deps)
    for layer in range(n_layers):
        j = layer // 2
        w_in, w_out = _mixer_names(layer)
        W = {}
        s = {"h": h, "W": W}
        s["n"] = _rms_fwd(h, mix_g, layer, "mix_norm_0", deps=deps) if layer == 0 else n_next
        deps = ()
        W[w_in], held = weights(layer, w_in, (s["n"],))
        s["u"] = _mm_nn(s["n"], W[w_in], 0, f"mix_in_{layer}", deps=held)
        if layer % 2 == 0:
            s["y"], s["yc"] = _even_fwd(s["u"], P["conv_w_full"], *even, j, f"even_fwd_{layer}")
        else:
            s["y"], s["o"], s["sall"] = _hgrn_fwd(s["u"], lb3, layer, gn3, j, f"hgrn_fwd_{layer}")
        W[w_out], held = weights(layer, w_out, (s["y"],))
        if layer == 0:
            h, s["n2"] = _mm_nn_norm(s["y"], W[w_out], 0, h, mlp_g, layer, "mix_out_0", deps=held)
            s["h1"] = h
            W["mlp_w1"], held = weights(layer, "mlp_w1", (s["n2"],))
            s["relu"] = _mm_nn(s["n2"], W["mlp_w1"], 0, "mlp_up_0", relu=True, deps=held)
            W["mlp_w2"], held = weights(layer, "mlp_w2", (s["relu"],))
            h, n_next = _mm_nn_norm(s["relu"], W["mlp_w2"], 0, h, mix_g, 1, "mlp_down_0", square=True, deps=held)
        else:
            W["mlp_w1"], more1 = weights(layer, "mlp_w1", (s["y"],))
            W["mlp_w2"], more2 = weights(layer, "mlp_w2", (s["y"],))
            last = layer + 1 == n_layers
            out = _tail_fwd(s["y"], W[w_out], h, mlp_g, layer, W["mlp_w1"], W["mlp_w2"], None if last else mix_g,
                            f"tail_{layer}", deps=held + more1 + more2)
            s["h1"], s["n2"], h, s["relu"] = out[0], out[1], out[2], out[-1]
            n_next = None if last else out[3]
        saved.append(s)

    dh, dhb, dg_final, loss = _final(h, P["final_norm_g"].reshape(1, D), target)

    small = {"final_norm_g": dg_final}
    per_layer = {k: [None] * n_layers for k in ("mix_norm_g", "mlp_norm_g", "lb")}
    per_pair = {k: [None] * (n_layers // 2) for k in
                ("ev_conv_w", "ev_conv_b", "ev_ln_g", "ev_ln_b", "ev_pool_w", "ev_pool_b", "ev_pool_scale", "od_gnorm_g")}
    for layer in reversed(range(n_layers)):
        j = layer // 2
        s = saved[layer]
        W = s["W"]
        w_in, w_out = _mixer_names(layer)
        dw2 = _mm_tn(s["relu"], dhb, "row", f"dw2_{layer}", square=True)
        dz, dh, dhb, per_layer["mlp_norm_g"][layer] = _mlp_bwd(
            dhb, s["relu"], W["mlp_w1"], W["mlp_w2"], s["h1"], mlp_g, layer, dh, f"mlp_bwd_{layer}", deps=deps + (dw2,))
        dw1 = _mm_tn(s["n2"], dz, "col", f"dw1_{layer}")
        deps = boundary(f"mlp{layer}", {("mlp_w1", layer): dw1, ("mlp_w2", layer): dw2}, (dhb, dw1, dw2))
        dy = _mm_nt(dhb, W[w_out], 0, f"d_y_{layer}", deps=deps)
        dwout = _mm_tn(s["y"], dhb, "row", f"dwout_{layer}")
        if layer % 2 == 0:
            du, dcw, dcb, dlg, dlnb, dpw, dpb, dps = _even_bwd(s["u"], s["yc"], dy, P["conv_w_full"], *even, j, f"even_bwd_{layer}")
            for k, val in (("ev_conv_w", dcw), ("ev_conv_b", dcb), ("ev_ln_g", dlg), ("ev_ln_b", dlnb),
                           ("ev_pool_w", dpw), ("ev_pool_b", dpb), ("ev_pool_scale", dps)):
                per_pair[k][j] = val
        else:
            du, per_layer["lb"][layer], per_pair["od_gnorm_g"][j] = _hgrn_bwd(
                s["u"], s["o"], dy, s["sall"], lb3, layer, gn3, j, f"hgrn_bwd_{layer}")
        dwin = _mm_tn(s["n"], du, "col", f"dwin_{layer}")
        deps = boundary(f"mix{layer}", {(w_in, j): dwin, (w_out, j): dwout}, (du, dwin, dwout))
        dh, dhb, per_layer["mix_norm_g"][layer] = _mm_nt_norm(du, W[w_in], 0, s["h"], mix_g, layer, dh, f"d_n_{layer}", deps=deps)
        deps = ()

    small["mix_norm_g"] = jnp.concatenate(per_layer["mix_norm_g"], axis=0)
    small["mlp_norm_g"] = jnp.concatenate(per_layer["mlp_norm_g"], axis=0)
    dlb_all = jnp.concatenate([jnp.zeros((1, D), F32) if g is None else g for g in per_layer["lb"]], axis=0)
    small["lb_param"] = _lb_bwd(P["lb_param"], dlb_all)
    for k, vals in per_pair.items():
        small[k] = jnp.stack(vals, axis=0)
    small["meta_tokens"] = dh[PAD:LEAD]
    return loss, dh, small


def kernel(x, meta_tokens, mix_norm_g, mlp_norm_g, final_norm_g, ev_w_in, ev_conv_w, ev_conv_b, ev_ln_g, ev_ln_b, ev_pool_w, ev_pool_b, ev_pool_scale, ev_w_out, od_w_in, od_gnorm_g, od_w_out, lb_param, mlp_w1, mlp_w2, loss_target, m_meta_tokens, m_mix_norm_g, m_mlp_norm_g, m_final_norm_g, m_ev_w_in, m_ev_conv_w, m_ev_conv_b, m_ev_ln_g, m_ev_ln_b, m_ev_pool_w, m_ev_pool_b, m_ev_pool_scale, m_ev_w_out, m_od_w_in, m_od_gnorm_g, m_od_w_out, m_lb_param, m_mlp_w1, m_mlp_w2, v_meta_tokens, v_mix_norm_g, v_mlp_norm_g, v_final_norm_g, v_ev_w_in, v_ev_conv_w, v_ev_conv_b, v_ev_ln_g, v_ev_ln_b, v_ev_pool_w, v_ev_pool_b, v_ev_pool_scale, v_ev_w_out, v_od_w_in, v_od_gnorm_g, v_od_w_out, v_lb_param, v_mlp_w1, v_mlp_w2):
    given = dict(locals())
    w = {n: given[n] for n in WEIGHTS}
    m = {n: given["m_" + n] for n in WEIGHTS}
    v = {n: given["v_" + n] for n in WEIGHTS}
    n_layers = mix_norm_g.shape[0]
    core = lax.axis_index("c").astype(jnp.int32)
    chip = (2 * lax.axis_index("x") + lax.axis_index("y")).astype(jnp.int32)
    chip1 = chip.reshape(1)
    ids2 = jnp.stack([core, chip])

    conv_pad = jnp.pad(ev_conv_w, ((0, 0), (0, CONV_ROWS - CONV_WIDTH), (0, 0)))
    stages = [[(0, n)] for n in (*_mixer_names(0), "mlp_w1", "mlp_w2")]
    for layer in range(1, n_layers):
        stages += [[(layer, n) for n in _mixer_names(layer)], [(layer, "mlp_w1"), (layer, "mlp_w2")]]
    where, stage_kinds, stage_bufs = {}, [], []
    for k, stage in enumerate(stages):
        index = [layer if n.startswith("mlp") else layer // 2 for layer, n in stage]
        kinds = [BIG[n] for _, n in stage]
        bufs = [_cast_place(w[n], i, BIG[n], chip1, BF16, f"place_{n}_{i}") for (_, n), i in zip(stage, index)]
        if k == 0:
            bufs.append(_cast_place(meta_tokens[None], 0, "col", chip1, F32, "place_meta"))
            bufs.append(_cast_place(conv_pad.reshape(1, -1, conv_pad.shape[2]), 0, "col", chip1, F32, "place_conv_w"))
            kinds += ["col", "col"]
        stage_kinds.append(kinds)
        stage_bufs.append(bufs)
        where.update({key: (k, f) for f, key in enumerate(stage)})
    every = [b for bufs in stage_bufs for b in bufs]
    ss, rs, every, tok = _split_start("gather_start", _gather_plan([kd for kinds in stage_kinds for kd in kinds], True),
                                      every, 3 * len(every))
    token = (tok,)
    gathers, at = [], 0
    for kinds in stage_kinds:
        gathers.append((kinds, _gather_plan(kinds, True, first=at), ss, rs, every[at:at + len(kinds)]))
        at += len(kinds)

    landed, passed, held = {}, {}, []

    def hand_on(k, deps):
        if k not in passed:
            kinds, plan, ss, rs, bufs = gathers[k]
            to_sibling = _gather_plan(kinds, False)
            ss, rs, bufs, tok = _split_start(f"gather_pass_{k}", to_sibling, bufs, 3 * len(bufs), deps=deps, earlier=(plan, ss, rs))
            passed[k] = (to_sibling, ss, rs, bufs)
            held.append(tok)

    def arrived(k, after):
        if k not in landed:
            hand_on(k, after)
            landed[k] = _split_wait(f"gather_wait_{k}", *passed[k], after)
        return landed[k]

    def weights(layer, name, after):
        k, f = where[(layer, name)]
        full = arrived(k, after)[f][None]
        if name == "mlp_w2" and layer + 1 < n_layers:
            hand_on(where[(layer + 1, _mixer_names(layer + 1)[0])][0], after)
        if layer > 0 and name == _mixer_names(layer)[0]:
            hand_on(where[(layer, "mlp_w1")][0], after)
        tokens = tuple(held)
        held.clear()
        return full, tokens

    first = arrived(0, token)
    P = {n: w[n] for n in ("mix_norm_g", "mlp_norm_g", "final_norm_g", "ev_conv_b", "ev_ln_g", "ev_ln_b", "ev_pool_w",
                           "ev_pool_b", "ev_pool_scale", "od_gnorm_g", "lb_param")}
    P["meta_full"] = first[1]
    P["conv_w_full"] = first[2].reshape(ev_conv_w.shape[0], CONV_ROWS, -1)

    pending, outs = [], {n: None for n in BIG}

    def advance(after, fresh=1):
        tokens, still = [], []
        for pos, st in enumerate(pending):
            if st["phase"] == 1 and pos >= len(pending) - fresh:
                still.append(st)
            elif st["phase"] == 1:
                bufs = _split_wait(f"reduce_wait_{st['tag']}", _reduce_plan, st["ss"], st["rs"], st["bufs"], after)
                halves = [_sum_pieces(ids2, bufs[2 * f], bufs[2 * f + 1], f"sum_{st['tag']}_{f}") for f in range(len(bufs) // 2)]
                ss, rs, halves, tok = _split_start(f"swap_start_{st['tag']}", _swap_plan, halves, len(halves))
                tokens.append(tok)
                still.append(dict(st, phase=2, ss=ss, rs=rs, bufs=halves))
            else:
                grads = _split_wait(f"swap_wait_{st['tag']}", _swap_plan, st["ss"], st["rs"], st["bufs"], after)
                for (n, i), g in zip(st["keys"], grads):
                    outs[n] = _adamw_layer(w[n], m[n], v[n], g.reshape(w[n].shape[1:]), i, outs[n], f"adamw_{n}_{i}")
        pending[:] = still
        return tokens

    def boundary(tag, grads, after):
        tokens = advance(after)
        bufs = []
        for acc in grads.values():
            bufs += [acc, lax.empty((N_DEV - 1,) + acc.shape[2:], BF16)]
        ss, rs, bufs, tok = _split_start(f"reduce_start_{tag}", _reduce_plan, bufs, 7 * len(grads))
        pending.append(dict(phase=1, tag=tag, keys=list(grads), ss=ss, rs=rs, bufs=bufs))
        return tuple(tokens + [tok])

    loss, dh, small = _local_step(x[0], loss_target[0], P, weights, boundary, first_deps=token)

    order = [n for n in WEIGHTS if n not in BIG]
    block = _pack([small[n] for n in order] + [loss])
    ss, rs, bufs, tok = _split_start("small_start", _small_plan, [block, lax.empty((N_DEV,) + block.shape, F32)], N_DEV - 1)
    while pending:
        advance((tok,) + tuple(o[0] for o in outs.values() if o is not None), fresh=0)
    block, land = _split_wait("small_wait", _small_plan, ss, rs, bufs, tuple(outs[n][0] for n in BIG))
    packed = _sum_blocks((4 * lax.axis_index("x") + 2 * lax.axis_index("y") + lax.axis_index("c")).astype(jnp.int32).reshape(1), block, land)
    total = _unpack(packed, [small[n].shape for n in order] + [loss.shape])
    loss_sum = total[-1][0, 0]
    gsmall = dict(zip(order, total[:-1]))
    gsmall["meta_tokens"] = lax.dynamic_slice_in_dim(gsmall["meta_tokens"], chip * meta_tokens.shape[1], meta_tokens.shape[1], 1)
    gsmall["ev_conv_w"] = lax.dynamic_slice_in_dim(gsmall["ev_conv_w"][:, :CONV_WIDTH], chip * ev_conv_w.shape[2], ev_conv_w.shape[2], 2)

    g_out, d_out, m_out, v_out = {}, {}, {}, {}
    for n in WEIGHTS:
        if n in BIG:
            g_out[n], d_out[n], m_out[n], v_out[n] = outs[n]
            continue
        shape = w[n].shape
        g = gsmall[n].reshape(shape)
        cols = shape[-1] if len(shape) > 1 else 128
        two = lambda a: a.reshape(-1, cols)
        d_, m_, v_ = _adamw(two(w[n]), two(g), two(m[n]), two(v[n]), f"adamw_{n}")
        g_out[n], d_out[n], m_out[n], v_out[n] = g, d_.reshape(shape), m_.reshape(shape), v_.reshape(shape)

    grad_x = dh[LEAD:][None]
    return (loss_sum, grad_x, *[g_out[n] for n in WEIGHTS], *[d_out[n] for n in WEIGHTS],
            *[m_out[n] for n in WEIGHTS], *[v_out[n] for n in WEIGHTS])
```

```python
import functools

import jax
import jax.numpy as jnp
from jax import lax
from jax.experimental import pallas as pl
from jax.experimental.pallas import tpu as pltpu

F32 = jnp.float32
BF16 = jnp.bfloat16
SDS = jax.ShapeDtypeStruct
MESH = pl.DeviceIdType.MESH
ANY_SPEC = pl.BlockSpec(memory_space=pl.ANY)

N_META = 16
CHUNK = 64
LEAD = CHUNK
PAD = LEAD - N_META
CONV_WIDTH = 31
CONV_ROWS = 32
POOL_WINDOWS = (2, 4, 8, 16)
HEAD_DIM = 128
SUB = 16
EXP_CAP = 80.0
EPS = 1e-6
ADAM_LR = 0.001
ADAM_B1 = 0.9
ADAM_B2 = 0.999
ADAM_EPS = 1e-08
ADAM_WD = 0.01
ADAM_STEP = 10
N_CHIPS = 4
VMEM_LIMIT = 58 << 20
MM_VMEM_BUDGET = 50 << 20


def _params(*sem):
    return pltpu.CompilerParams(dimension_semantics=sem if sem else None, vmem_limit_bytes=VMEM_LIMIT)


def _tile(n, target, unit=CHUNK):
    best = None
    for t in range(unit, min(n, target) + 1, unit):
        if n % t == 0:
            best = t
    assert best is not None, (n, target, unit)
    return best


def _ctile(n, target=512):
    for t in (512, 384, 256, 128):
        if t <= target and n % t == 0:
            return t
    raise ValueError(n)


def _mm_tiles(M, N, per_row, per_col, per_elem):
    best = None
    for tn in (512, 384, 256, 128):
        if N % tn:
            continue
        for tm in sorted((d for d in range(16, M + 1, 16) if M % d == 0), reverse=True):
            if 2 * (tm * per_row + tn * per_col + tm * tn * per_elem) <= MM_VMEM_BUDGET:
                if best is None or tm * tn > best[0] * best[1]:
                    best = (tm, tn)
                break
    assert best is not None, (M, N)
    return best


def _sigmoid(x):
    return 1.0 / (1.0 + jnp.exp(-x))


def _mult(v, m):
    return v if isinstance(v, int) else pl.multiple_of(v, m)


def _row_ids(shape, base):
    return lax.broadcasted_iota(jnp.int32, shape, 0) + base


def _cast_place(w3, layer, kind, chip1, dtype, name):
    _, ks, ns = w3.shape
    tr = _tile(ks, 512, 16)
    full = (ks, ns * N_CHIPS) if kind == "col" else (ks * N_CHIPS, ns)

    def body(chip_ref, w_ref, o_ref):
        del chip_ref
        o_ref[...] = w_ref[...].astype(dtype)

    omap = (lambda i, chip: (i, chip[0])) if kind == "col" else (lambda i, chip: (chip[0] * (ks // tr) + i, 0))
    return pl.pallas_call(
        body,
        grid_spec=pltpu.PrefetchScalarGridSpec(
            num_scalar_prefetch=1, grid=(ks // tr,),
            in_specs=[pl.BlockSpec((None, tr, ns), lambda i, chip: (layer, i, 0))],
            out_specs=pl.BlockSpec((tr, ns), omap)),
        out_shape=SDS(full, dtype), name=name, compiler_params=_params("parallel"))(chip1, w3)


def _rms_fwd(h, g3, layer, name, deps=()):
    T, D = h.shape
    tm = _tile(T, 832)

    def body(h_ref, g_ref, *rest):
        n_ref = rest[-1]
        x = h_ref[...]
        r = lax.rsqrt(jnp.mean(x * x, axis=-1, keepdims=True) + EPS)
        n_ref[...] = ((x * r) * g_ref[...]).astype(BF16)

    return pl.pallas_call(
        body, grid=(T // tm,),
        in_specs=[pl.BlockSpec((tm, D), lambda i: (i, 0)), pl.BlockSpec((None, 1, D), lambda i: (layer, 0, 0))]
        + [ANY_SPEC] * len(deps),
        out_specs=pl.BlockSpec((tm, D), lambda i: (i, 0)), out_shape=SDS((T, D), BF16),
        name=name, compiler_params=_params("parallel"))(h, g3, *deps)


def _final(h, g2, target):
    T, D = h.shape
    tm = _tile(T, 320)
    nsub = tm // CHUNK
    nblk = target.shape[0] // CHUNK

    def body(h_ref, g_ref, *rest):
        t_refs = rest[:nsub]
        dh_ref, dhb_ref, dg_ref, loss_ref = rest[nsub:]
        i = pl.program_id(0)

        @pl.when(i == 0)
        def _():
            dg_ref[...] = jnp.zeros_like(dg_ref)
            loss_ref[...] = jnp.zeros_like(loss_ref)

        g = g_ref[...]
        for q in range(nsub):
            rows = slice(q * CHUNK, (q + 1) * CHUNK)
            x = h_ref[rows, :]
            r = lax.rsqrt(jnp.mean(x * x, axis=-1, keepdims=True) + EPS)
            xh = x * r
            live = jnp.where(i * nsub + q > 0, 1.0, 0.0).astype(F32)
            e = ((xh * g) - t_refs[q][...]) * live
            dy = e * (1.0 / D)
            dxh = dy * g
            dh = r * (dxh - xh * jnp.mean(dxh * xh, axis=-1, keepdims=True))
            dh_ref[rows, :] = dh
            dhb_ref[rows, :] = dh.astype(BF16)
            dg_ref[...] += jnp.sum(dy * xh, axis=0, keepdims=True)
            loss_ref[...] += jnp.sum(e * e) * (0.5 / D)

    row = pl.BlockSpec((tm, D), lambda i: (i, 0))
    t_specs = [pl.BlockSpec((CHUNK, D), functools.partial(lambda i, q: (jnp.clip(i * nsub + q - 1, 0, nblk - 1), 0), q=q))
               for q in range(nsub)]
    return pl.pallas_call(
        body, grid=(T // tm,),
        in_specs=[row, pl.BlockSpec((1, D), lambda i: (0, 0))] + t_specs,
        out_specs=[row, row, pl.BlockSpec((1, D), lambda i: (0, 0)), pl.BlockSpec((1, 128), lambda i: (0, 0))],
        out_shape=[SDS((T, D), F32), SDS((T, D), BF16), SDS((1, D), F32), SDS((1, 128), F32)],
        name="final_loss", compiler_params=_params("arbitrary"))(h, g2, *([target] * nsub))


def _mm_nn(a, w3, layer, name, res=None, relu=False, square=False, deps=()):
    M, K = a.shape
    N = w3.shape[2]
    tm, tn = _mm_tiles(M, N, 2 * K, 2 * K, (2 if relu else 4) + (4 if res is not None else 0))

    def body(*refs):
        lhs = refs[0][...]
        acc = jnp.dot(lhs * lhs if square else lhs, refs[1][...], preferred_element_type=F32)
        if res is not None:
            acc = acc + refs[2][...]
        refs[-1][...] = jnp.maximum(acc, 0.0).astype(BF16) if relu else acc

    in_specs = [pl.BlockSpec((tm, K), lambda i, j: (i, 0)), pl.BlockSpec((None, K, tn), lambda i, j: (layer, 0, j))]
    args = [a, w3]
    tile = pl.BlockSpec((tm, tn), lambda i, j: (i, j))
    if res is not None:
        in_specs.append(tile)
        args.append(res)
    in_specs += [ANY_SPEC] * len(deps)
    args += list(deps)
    return pl.pallas_call(
        body, grid=(M // tm, N // tn), in_specs=in_specs, out_specs=tile,
        out_shape=SDS((M, N), BF16 if relu else F32),
        name=name, compiler_params=_params("parallel", "parallel"))(*args)


def _mm_nt(dy, w3, layer, name, relu=None, deps=()):
    M, N = dy.shape
    K = w3.shape[1]
    tm, tk = _mm_tiles(M, K, 2 * N, 2 * N, 4)

    def body(*refs):
        acc = lax.dot_general(refs[0][...], refs[1][...], (((1,), (1,)), ((), ())), preferred_element_type=F32)
        if relu is not None:
            acc = (acc * (2.0 * refs[2][...].astype(F32))).astype(BF16)
        refs[-1][...] = acc

    tile = pl.BlockSpec((tm, tk), lambda i, j: (i, j))
    in_specs = [pl.BlockSpec((tm, N), lambda i, j: (i, 0)), pl.BlockSpec((None, tk, N), lambda i, j: (layer, j, 0))]
    args = [dy, w3]
    if relu is not None:
        in_specs.append(tile)
        args.append(relu)
    in_specs += [ANY_SPEC] * len(deps)
    args += list(deps)
    return pl.pallas_call(
        body, grid=(M // tm, K // tk), in_specs=in_specs, out_specs=tile,
        out_shape=SDS((M, K), F32 if relu is None else BF16),
        name=name, compiler_params=_params("parallel", "parallel"))(*args)


def _row_tile(M, per_row, fixed):
    for tm in sorted((d for d in range(16, M + 1, 16) if M % d == 0), reverse=True):
        if 2 * (tm * per_row + fixed) <= MM_VMEM_BUDGET:
            return tm
    raise ValueError((M, per_row, fixed))


def _mm_nn_norm(a, w3, layer, res, g3, glayer, name, square=False, deps=()):
    M, K = a.shape
    D = w3.shape[2]
    tm = _row_tile(M, 2 * K + 10 * D, 2 * K * D)

    def body(a_ref, w_ref, r_ref, g_ref, *rest):
        h_ref, n_ref = rest[-2:]
        lhs = a_ref[...]
        x = r_ref[...] + jnp.dot(lhs * lhs if square else lhs, w_ref[...], preferred_element_type=F32)
        h_ref[...] = x
        r = lax.rsqrt(jnp.mean(x * x, axis=-1, keepdims=True) + EPS)
        n_ref[...] = ((x * r) * g_ref[...]).astype(BF16)

    row = pl.BlockSpec((tm, D), lambda i: (i, 0))
    return pl.pallas_call(
        body, grid=(M // tm,),
        in_specs=[pl.BlockSpec((tm, K), lambda i: (i, 0)), pl.BlockSpec((None, K, D), lambda i: (layer, 0, 0)), row,
                  pl.BlockSpec((None, 1, D), lambda i: (glayer, 0, 0))] + [ANY_SPEC] * len(deps),
        out_specs=[row, row], out_shape=[SDS((M, D), F32), SDS((M, D), BF16)],
        name=name, compiler_params=_params("parallel"))(a, w3, res, g3, *deps)


def _tail_fwd(y, w_out, res, mlp_g3, layer, w1, w2, next_g3, name, deps=()):
    M, K = y.shape
    D = w_out.shape[2]
    F = w1.shape[2]
    hb = _ctile(F)
    more = next_g3 is not None
    tm = _row_tile(M, 2 * K + 18 * D + (2 * D if more else 0) + 2 * F, 2 * K * D + 2 * D * F)

    def body(y_ref, wo_ref, res_ref, g_ref, w1_ref, w2_ref, *rest):
        outs = rest[-5:] if more else rest[-4:]
        h1 = res_ref[...] + jnp.dot(y_ref[...], wo_ref[...], preferred_element_type=F32)
        outs[0][...] = h1
        n2 = ((h1 * lax.rsqrt(jnp.mean(h1 * h1, axis=-1, keepdims=True) + EPS)) * g_ref[...]).astype(BF16)
        outs[1][...] = n2
        acc = h1
        for jb in range(F // hb):
            cols = slice(jb * hb, (jb + 1) * hb)
            r = jnp.maximum(jnp.dot(n2, w1_ref[:, cols], preferred_element_type=F32), 0.0).astype(BF16)
            outs[-1][:, cols] = r
            acc = acc + jnp.dot(r * r, w2_ref[cols, :], preferred_element_type=F32)
        outs[2][...] = acc
        if more:
            outs[3][...] = ((acc * lax.rsqrt(jnp.mean(acc * acc, axis=-1, keepdims=True) + EPS)) * rest[0][...]).astype(BF16)

    row = pl.BlockSpec((tm, D), lambda i: (i, 0))
    once = dict(pipeline_mode=pl.Buffered(1))
    in_specs = [pl.BlockSpec((tm, K), lambda i: (i, 0)), pl.BlockSpec((None, K, D), lambda i: (0, 0, 0), **once), row,
                pl.BlockSpec((None, 1, D), lambda i: (layer, 0, 0)),
                pl.BlockSpec((None, D, F), lambda i: (0, 0, 0), **once), pl.BlockSpec((None, F, D), lambda i: (0, 0, 0), **once)]
    args = [y, w_out, res, mlp_g3, w1, w2]
    out_specs, out_shape = [row, row, row], [SDS((M, D), F32), SDS((M, D), BF16), SDS((M, D), F32)]
    if more:
        in_specs.append(pl.BlockSpec((None, 1, D), lambda i: (layer + 1, 0, 0)))
        args.append(next_g3)
        out_specs.append(row)
        out_shape.append(SDS((M, D), BF16))
    out_specs.append(pl.BlockSpec((tm, F), lambda i: (i, 0)))
    out_shape.append(SDS((M, F), BF16))
    in_specs += [ANY_SPEC] * len(deps)
    args += list(deps)
    return pl.pallas_call(
        body, grid=(M // tm,), in_specs=in_specs, out_specs=out_specs, out_shape=out_shape,
        name=name, compiler_params=_params("parallel"))(*args)


def _mlp_bwd(dhb, relu, w1, w2, h, g3, glayer, dh_in, name, deps=()):
    M, D = dhb.shape
    F = w1.shape[2]
    hb = _ctile(F)
    tm = _row_tile(M, 16 * D + 4 * F, 2 * D * F)

    def body(dy_ref, r_ref, w1_ref, w2_ref, h_ref, g_ref, dhi_ref, *rest):
        dz_ref, dh_ref, dhb_ref, dg_ref = rest[-4:]
        dy = dy_ref[...]
        dn = jnp.zeros((tm, D), F32)
        for jb in range(F // hb):
            cols = slice(jb * hb, (jb + 1) * hb)
            dact = lax.dot_general(dy, w2_ref[cols, :], (((1,), (1,)), ((), ())), preferred_element_type=F32)
            dz = (dact * (2.0 * r_ref[:, cols].astype(F32))).astype(BF16)
            dz_ref[:, cols] = dz
            dn = dn + lax.dot_general(dz, w1_ref[:, cols], (((1,), (1,)), ((), ())), preferred_element_type=F32)
        x = h_ref[...]
        r = lax.rsqrt(jnp.mean(x * x, axis=-1, keepdims=True) + EPS)
        xh = x * r
        dxh = dn * g_ref[...]
        dh = dhi_ref[...] + r * (dxh - xh * jnp.mean(dxh * xh, axis=-1, keepdims=True))
        dh_ref[...] = dh
        dhb_ref[...] = dh.astype(BF16)

        @pl.when(pl.program_id(0) == 0)
        def _():
            dg_ref[...] = jnp.zeros_like(dg_ref)

        dg_ref[...] += jnp.sum(dn * xh, axis=0, keepdims=True)

    row = pl.BlockSpec((tm, D), lambda i: (i, 0))
    wide = pl.BlockSpec((tm, F), lambda i: (i, 0))
    once = dict(pipeline_mode=pl.Buffered(1))
    return pl.pallas_call(
        body, grid=(M // tm,),
        in_specs=[row, wide, pl.BlockSpec((None, D, F), lambda i: (0, 0, 0), **once),
                  pl.BlockSpec((None, F, D), lambda i: (0, 0, 0), **once), row,
                  pl.BlockSpec((None, 1, D), lambda i: (glayer, 0, 0)), row] + [ANY_SPEC] * len(deps),
        out_specs=[wide, row, row, pl.BlockSpec((1, D), lambda i: (0, 0))],
        out_shape=[SDS((M, F), BF16), SDS((M, D), F32), SDS((M, D), BF16), SDS((1, D), F32)],
        name=name, compiler_params=_params("arbitrary"))(dhb, relu, w1, w2, h, g3, dh_in, *deps)


def _mm_nt_norm(dy, w3, layer, h, g3, glayer, dh_in, name, deps=()):
    M, N = dy.shape
    D = w3.shape[1]
    tm = _row_tile(M, 2 * N + 14 * D, 2 * N * D)

    def body(dy_ref, w_ref, h_ref, g_ref, dhi_ref, *rest):
        dh_ref, dhb_ref, dg_ref = rest[-3:]
        dn = lax.dot_general(dy_ref[...], w_ref[...], (((1,), (1,)), ((), ())), preferred_element_type=F32)
        x = h_ref[...]
        r = lax.rsqrt(jnp.mean(x * x, axis=-1, keepdims=True) + EPS)
        xh = x * r
        dxh = dn * g_ref[...]
        dh = dhi_ref[...] + r * (dxh - xh * jnp.mean(dxh * xh, axis=-1, keepdims=True))
        dh_ref[...] = dh
        dhb_ref[...] = dh.astype(BF16)

        @pl.when(pl.program_id(0) == 0)
        def _():
            dg_ref[...] = jnp.zeros_like(dg_ref)

        dg_ref[...] += jnp.sum(dn * xh, axis=0, keepdims=True)

    row = pl.BlockSpec((tm, D), lambda i: (i, 0))
    return pl.pallas_call(
        body, grid=(M // tm,),
        in_specs=[pl.BlockSpec((tm, N), lambda i: (i, 0)), pl.BlockSpec((None, D, N), lambda i: (layer, 0, 0)), row,
                  pl.BlockSpec((None, 1, D), lambda i: (glayer, 0, 0)), row] + [ANY_SPEC] * len(deps),
        out_specs=[row, row, pl.BlockSpec((1, D), lambda i: (0, 0))],
        out_shape=[SDS((M, D), F32), SDS((M, D), BF16), SDS((1, D), F32)],
        name=name, compiler_params=_params("arbitrary"))(dy, w3, h, g3, dh_in, *deps)


def _fam_dims(kind, K, N):
    return (K // 2, N // N_CHIPS) if kind == "col" else (K // (2 * N_CHIPS), N)


def _mm_tn(x, dy, kind, name, square=False):
    M, K = x.shape
    N = dy.shape[1]
    nr, nc = _fam_dims(kind, K, N)

    def body(x_ref, dy_ref, o_ref):
        lhs = x_ref[...]
        res = lax.dot_general(lhs * lhs if square else lhs, dy_ref[...], (((0,), (0,)), ((), ())), preferred_element_type=F32)
        o_ref[...] = res.astype(BF16).reshape(o_ref.shape)

    if kind == "col":
        tn = _ctile(nc)
        ct = nc // tn
        grid = (N // tn,)
        in_specs = [pl.BlockSpec((M, K), lambda j: (0, 0)), pl.BlockSpec((M, tn), lambda j: (0, j))]
        out_spec = pl.BlockSpec((2, None, nr, tn), lambda j: (0, j // ct, 0, j % ct))
    else:
        grid = (N_CHIPS,)
        in_specs = [pl.BlockSpec((M, 2 * nr), lambda i: (0, i)), pl.BlockSpec((M, N), lambda i: (0, 0))]
        out_spec = pl.BlockSpec((2, None, nr, N), lambda i: (0, i, 0, 0))
    return pl.pallas_call(
        body, grid=grid, in_specs=in_specs, out_specs=out_spec, out_shape=SDS((2, N_CHIPS, nr, nc), BF16),
        name=name, compiler_params=_params("parallel"))(x, dy)


C_EVEN = 512


def _live(rows, base, total):
    r = _row_ids((rows, 1), base)
    return jnp.logical_and(r >= PAD, r < total).astype(F32)


def _conv_taps(win, w_ref, ls, acc, flip):
    for b in range(8):
        rb = win if b == 0 else pltpu.roll(win, 96 - b, 0)
        for a in range(5):
            o = 8 * a + b
            tap = (30 - o) if flip else (o - 2)
            if 0 <= tap < CONV_WIDTH:
                acc = acc + w_ref[pl.ds(tap, 1), ls] * rb[8 * a:8 * a + CHUNK]
    return acc


def _window_sum(win, levels, forward):
    s = win
    n = win.shape[0]
    for k in range(levels):
        step = 1 << k
        s = s + pltpu.roll(s, (n - step) if forward else step, 0)
    return s


def _pool_count(base, g):
    pos = _row_ids((CHUNK, 1), base) - PAD
    return jnp.clip(pos + 1, 1, POOL_WINDOWS[g]).astype(F32)


def _even_fwd(u, cw3, cb3, lg3, lb3, pw4, pb3, ps3, j, name):
    T = u.shape[0]
    C = C_EVEN
    tm = _tile(T, 320)
    nch = tm // CHUNK
    nblk = T // CHUNK

    def body(u_ref, up_ref, cw_ref, cb_ref, lg_ref, lb_ref, pw_ref, pb_ref, ps_ref, o_ref, yc_ref, a_s, p_s, yc_s):
        row0 = pl.program_id(0) * tm
        up = up_ref[...]
        lp = _live(CHUNK, row0 - CHUNK, T)
        a_s[0:CHUNK, :] = up[:, 0:C] * _sigmoid(up[:, C:2 * C]) * lp
        p_s[0:CHUNK, :] = up[:, 2 * C:3 * C] * lp

        def stage(c, _):
            rs = _mult(c * CHUNK, CHUNK)
            lv = _live(CHUNK, row0 + rs, T)
            a_s[pl.ds(rs + CHUNK, CHUNK), :] = u_ref[pl.ds(rs, CHUNK), 0:C] * _sigmoid(u_ref[pl.ds(rs, CHUNK), C:2 * C]) * lv
            p_s[pl.ds(rs + CHUNK, CHUNK), :] = u_ref[pl.ds(rs, CHUNK), 2 * C:3 * C] * lv
            return 0

        for c in range(nch):
            stage(c, 0)

        def chunk(c, _):
            rs = _mult(c * CHUNK, CHUNK)
            lv = _live(CHUNK, row0 + rs, T)
            for cb in range(4):
                ls = slice(cb * 128, (cb + 1) * 128)
                win = a_s[pl.ds(_mult(rs + 32, 32), 96), ls]
                acc = jnp.broadcast_to(cb_ref[:, ls], (CHUNK, 128))
                yc_s[:, ls] = _conv_taps(win, cw_ref, ls, acc, False)
            y = yc_s[...]
            yc_ref[pl.ds(rs, CHUNK), :] = y
            xc = y - jnp.mean(y, axis=-1, keepdims=True)
            yn = xc * lax.rsqrt(jnp.mean(xc * xc, axis=-1, keepdims=True) + EPS) * lg_ref[...] + lb_ref[...]
            o_ref[pl.ds(rs, CHUNK), 0:C] = (yn * _sigmoid(yn) * lv).astype(BF16)
            for g in range(4):
                ls = slice(g * 128, (g + 1) * 128)
                win = p_s[pl.ds(_mult(rs + 48, 16), 80), ls]
                s = _window_sum(win, g + 1, False)
                d = s[16:80] / _pool_count(row0 + rs, g) - win[16:80]
                yv = jnp.dot(d.astype(BF16), pw_ref[g].astype(BF16), preferred_element_type=F32) + pb_ref[:, ls]
                o_ref[pl.ds(rs, CHUNK), C + g * 128:C + (g + 1) * 128] = (yv * ps_ref[:, ls] * lv).astype(BF16)
            return 0

        for c in range(nch):
            chunk(c, 0)

    vec = pl.BlockSpec((None, 1, C), lambda i: (j, 0, 0))
    return pl.pallas_call(
        body, grid=(T // tm,),
        in_specs=[pl.BlockSpec((tm, 3 * C), lambda i: (i, 0)),
                  pl.BlockSpec((CHUNK, 3 * C), lambda i: (jnp.maximum(i * nch - 1, 0), 0)),
                  pl.BlockSpec((None, CONV_ROWS, C), lambda i: (j, 0, 0)), vec, vec, vec,
                  pl.BlockSpec((None, 4, 128, 128), lambda i: (j, 0, 0, 0)), vec, vec],
        out_specs=[pl.BlockSpec((tm, 2 * C), lambda i: (i, 0)), pl.BlockSpec((tm, C), lambda i: (i, 0))],
        out_shape=[SDS((T, 2 * C), BF16), SDS((T, C), F32)],
        scratch_shapes=[pltpu.VMEM((tm + CHUNK, C), F32), pltpu.VMEM((tm + CHUNK, C), F32), pltpu.VMEM((CHUNK, C), F32)],
        name=name, compiler_params=_params("parallel"))(u, u, cw3, cb3, lg3, lb3, pw4, pb3, ps3)


def _even_bwd(u, yc, dy, cw3, cb3, lg3, lb3, pw4, pb3, ps3, j, name):
    T = u.shape[0]
    C = C_EVEN
    tm = _tile(T, 320)
    nch = tm // CHUNK
    nblk = T // CHUNK
    ntile = T // tm

    def body(u_ref, up_ref, un_ref, yc_ref, ycn_ref, dy_ref, dyn_ref, cw_ref, cb_ref, lg_ref, lb_ref, pw_ref, pb_ref, ps_ref,
             du_ref, dcw_ref, dcb_ref, dlg_ref, dlb_ref, dpw_ref, dpb_ref, dps_ref,
             a_s, p_s, dy_s, dyc_s, dd_s, ddc_s, dw_s):
        i = pl.program_id(0)
        row0 = i * tm

        @pl.when(i == 0)
        def _():
            for ref in (dcb_ref, dlg_ref, dlb_ref, dpw_ref, dpb_ref, dps_ref, dw_s):
                ref[...] = jnp.zeros_like(ref)

        up = up_ref[...]
        lp = _live(CHUNK, row0 - CHUNK, T)
        a_s[0:CHUNK, :] = up[:, 0:C] * _sigmoid(up[:, C:2 * C]) * lp
        p_s[0:CHUNK, :] = up[:, 2 * C:3 * C] * lp
        ln_ = _live(CHUNK, row0 + tm, T)
        p_s[tm + CHUNK:tm + 2 * CHUNK, :] = un_ref[:, 2 * C:3 * C] * ln_
        dy_s[tm:tm + CHUNK, :] = dyn_ref[...] * ln_
        dyc_s[tm + CHUNK:tm + CHUNK + 32, :] = jnp.zeros((32, C), F32)

        def stage(c, _):
            rs = _mult(c * CHUNK, CHUNK)
            lv = _live(CHUNK, row0 + rs, T)
            a_s[pl.ds(rs + CHUNK, CHUNK), :] = u_ref[pl.ds(rs, CHUNK), 0:C] * _sigmoid(u_ref[pl.ds(rs, CHUNK), C:2 * C]) * lv
            p_s[pl.ds(rs + CHUNK, CHUNK), :] = u_ref[pl.ds(rs, CHUNK), 2 * C:3 * C] * lv
            dy_s[pl.ds(rs, CHUNK), :] = dy_ref[pl.ds(rs, CHUNK), :] * lv
            return 0

        for c in range(nch):
            stage(c, 0)

        def first(rs, y, own):
            xc = y - jnp.mean(y, axis=-1, keepdims=True)
            rstd = lax.rsqrt(jnp.mean(xc * xc, axis=-1, keepdims=True) + EPS)
            xh = xc * rstd
            yn = xh * lg_ref[...] + lb_ref[...]
            sg = _sigmoid(yn)
            dyn = dy_s[pl.ds(rs, CHUNK), 0:C] * (sg * (1.0 + yn * (1.0 - sg)))
            dlg_ref[...] += jnp.sum(dyn * xh, axis=0, keepdims=True) * own
            dlb_ref[...] += jnp.sum(dyn, axis=0, keepdims=True) * own
            dxh = dyn * lg_ref[...]
            dyc = rstd * (dxh - jnp.mean(dxh, axis=-1, keepdims=True) - xh * jnp.mean(dxh * xh, axis=-1, keepdims=True))
            dyc_s[pl.ds(rs, CHUNK), :] = dyc
            dcb_ref[...] += jnp.sum(dyc, axis=0, keepdims=True) * own
            for g in range(4):
                ls = slice(g * 128, (g + 1) * 128)
                win = p_s[pl.ds(rs + 48, 80), ls]
                s = _window_sum(win, g + 1, False)
                cnt = _pool_count(row0 + rs, g)
                d = (s[16:80] / cnt - win[16:80]).astype(BF16)
                w = pw_ref[g].astype(BF16)
                pre = jnp.dot(d, w, preferred_element_type=F32) + pb_ref[:, ls]
                dyb = dy_s[pl.ds(rs, CHUNK), C + g * 128:C + (g + 1) * 128]
                dpre = dyb * ps_ref[:, ls]
                dps_ref[:, ls] += jnp.sum(dyb * pre, axis=0, keepdims=True) * own
                dpb_ref[:, ls] += jnp.sum(dpre, axis=0, keepdims=True) * own
                dpre_b = (dpre * own).astype(BF16)
                dpw_ref[g] += lax.dot_general(d, dpre_b, (((0,), (0,)), ((), ())), preferred_element_type=F32)
                dd = lax.dot_general(dpre.astype(BF16), w, (((1,), (1,)), ((), ())), preferred_element_type=F32)
                dd_s[pl.ds(rs, CHUNK), ls] = dd
                ddc_s[pl.ds(rs, CHUNK), ls] = dd / cnt

        def first_in_tile(c, _):
            rs = _mult(c * CHUNK, CHUNK)
            first(rs, yc_ref[pl.ds(rs, CHUNK), :], 1.0)
            return 0

        for c in range(nch):
            first_in_tile(c, 0)
        first(tm, ycn_ref[...], 0.0)
        ddc_s[tm + CHUNK:tm + CHUNK + 16, :] = jnp.zeros((16, C), F32)

        def second(c, _):
            rs = _mult(c * CHUNK, CHUNK)
            lv = _live(CHUNK, row0 + rs, T)
            for cb in range(4):
                ls = slice(cb * 128, (cb + 1) * 128)
                wd = dyc_s[pl.ds(rs, 96), ls]
                da = _conv_taps(wd, cw_ref, ls, jnp.zeros((CHUNK, 128), F32), True)
                wa = a_s[pl.ds(_mult(rs + 32, 32), 96), ls]
                dyc = dyc_s[pl.ds(rs, CHUNK), ls]
                for b in range(8):
                    rb = wa if b == 0 else pltpu.roll(wa, 96 - b, 0)
                    for a in range(5):
                        tap = 8 * a + b - 2
                        if 0 <= tap < CONV_WIDTH:
                            prod = dyc * rb[8 * a:8 * a + CHUNK]
                            part = prod[0:8]
                            for q in range(1, 8):
                                part = part + prod[8 * q:8 * q + 8]
                            dw_s[8 * tap:8 * tap + 8, ls] += part
                val = u_ref[pl.ds(rs, CHUNK), ls]
                sg = _sigmoid(u_ref[pl.ds(rs, CHUNK), C + cb * 128:C + (cb + 1) * 128])
                du_ref[pl.ds(rs, CHUNK), ls] = (da * sg * lv).astype(BF16)
                du_ref[pl.ds(rs, CHUNK), C + cb * 128:C + (cb + 1) * 128] = (da * val * sg * (1.0 - sg) * lv).astype(BF16)
            for g in range(4):
                ls = slice(g * 128, (g + 1) * 128)
                z = _window_sum(ddc_s[pl.ds(rs, 80), ls], g + 1, True)
                dpin = (z[0:CHUNK] - dd_s[pl.ds(rs, CHUNK), ls]) * lv
                du_ref[pl.ds(rs, CHUNK), 2 * C + g * 128:2 * C + (g + 1) * 128] = dpin.astype(BF16)
            return 0

        for c in range(nch):
            second(c, 0)

        @pl.when(i == ntile - 1)
        def _():
            for tap in range(CONV_WIDTH):
                dcw_ref[tap:tap + 1, :] = jnp.sum(dw_s[8 * tap:8 * tap + 8, :], axis=0, keepdims=True)
            dcw_ref[CONV_WIDTH:CONV_ROWS, :] = jnp.zeros((CONV_ROWS - CONV_WIDTH, C), F32)

    vec = pl.BlockSpec((None, 1, C), lambda i: (j, 0, 0))
    ovec = pl.BlockSpec((1, C), lambda i: (0, 0))
    return pl.pallas_call(
        body, grid=(ntile,),
        in_specs=[pl.BlockSpec((tm, 3 * C), lambda i: (i, 0)),
                  pl.BlockSpec((CHUNK, 3 * C), lambda i: (jnp.maximum(i * nch - 1, 0), 0)),
                  pl.BlockSpec((CHUNK, 3 * C), lambda i: (jnp.minimum((i + 1) * nch, nblk - 1), 0)),
                  pl.BlockSpec((tm, C), lambda i: (i, 0)),
                  pl.BlockSpec((CHUNK, C), lambda i: (jnp.minimum((i + 1) * nch, nblk - 1), 0)),
                  pl.BlockSpec((tm, 2 * C), lambda i: (i, 0)),
                  pl.BlockSpec((CHUNK, 2 * C), lambda i: (jnp.minimum((i + 1) * nch, nblk - 1), 0)),
                  pl.BlockSpec((None, CONV_ROWS, C), lambda i: (j, 0, 0)), vec, vec, vec,
                  pl.BlockSpec((None, 4, 128, 128), lambda i: (j, 0, 0, 0)), vec, vec],
        out_specs=[pl.BlockSpec((tm, 3 * C), lambda i: (i, 0)), pl.BlockSpec((CONV_ROWS, C), lambda i: (0, 0)),
                   ovec, ovec, ovec, pl.BlockSpec((4, 128, 128), lambda i: (0, 0, 0)), ovec, ovec],
        out_shape=[SDS((T, 3 * C), BF16), SDS((CONV_ROWS, C), F32), SDS((1, C), F32), SDS((1, C), F32), SDS((1, C), F32),
                   SDS((4, 128, 128), F32), SDS((1, C), F32), SDS((1, C), F32)],
        scratch_shapes=[pltpu.VMEM((tm + CHUNK, C), F32), pltpu.VMEM((tm + 2 * CHUNK, C), F32),
                        pltpu.VMEM((tm + CHUNK, 2 * C), F32),
                        pltpu.VMEM((tm + CHUNK + 32, C), F32), pltpu.VMEM((tm + CHUNK, C), F32),
                        pltpu.VMEM((tm + CHUNK + 16, C), F32), pltpu.VMEM((8 * CONV_ROWS, C), F32)],
        name=name, compiler_params=_params("arbitrary"))(u, u, u, yc, yc, dy, dy, cw3, cb3, lg3, lb3, pw4, pb3, ps3)


HI = lax.Precision.HIGHEST


def _dot_nt(a, b):
    return lax.dot_general(a, b, (((1,), (1,)), ((), ())), preferred_element_type=F32)


def _dot_tn(a, b):
    return lax.dot_general(a, b, (((0,), (0,)), ((), ())), preferred_element_type=F32)


def _tri(lower):
    r = lax.broadcasted_iota(jnp.int32, (CHUNK, CHUNK), 0)
    c = lax.broadcasted_iota(jnp.int32, (CHUNK, CHUNK), 1)
    return jnp.where((c <= r) if lower else (c >= r), 1.0, 0.0).astype(F32)


def _hgrn_gates(u_ref, lb_ref, h, D, lv):
    ls = slice(h * HEAD_DIM, (h + 1) * HEAD_DIM)
    qraw = u_ref[:, ls]
    fraw = u_ref[:, D + h * HEAD_DIM:D + (h + 1) * HEAD_DIM]
    v = u_ref[:, 2 * D + h * HEAD_DIM:2 * D + (h + 1) * HEAD_DIM] * lv
    lbv = lb_ref[:, ls]
    sig = _sigmoid(fraw)
    forget = lbv + (1.0 - lbv) * sig
    logf = jnp.log(forget) * lv
    k = (1.0 - forget) * lv
    qsig = _sigmoid(qraw)
    q = qraw * qsig * lv
    return q, k, v, logf, (qraw, qsig, sig, forget, lbv)


def _sub_parts(q, k, b, b_s, I):
    rows = slice(SUB * I, SUB * (I + 1))
    rho = jnp.zeros((1, HEAD_DIM), F32) if I == 0 else b_s[SUB * I - 1:SUB * I, :]
    eI = jnp.exp(b[rows] - rho)
    EI = jnp.exp(jnp.minimum(rho - b, EXP_CAP))
    causal = (lax.broadcasted_iota(jnp.int32, (SUB, CHUNK), 1)
              <= lax.broadcasted_iota(jnp.int32, (SUB, CHUNK), 0) + SUB * I)
    return rows, q[rows] * eI, k * EI, eI, EI, causal


def _chunks_per_step(NC):
    for n in (5, 4, 3, 2):
        if NC % n == 0:
            return n
    return 1


def _hgrn_fwd(u, lb3, layer, gn3, j, name):
    T = u.shape[0]
    D = u.shape[1] // 4
    H = D // HEAD_DIM
    NC = T // CHUNK
    CH = _chunks_per_step(NC)
    R = CH * CHUNK

    def body(u_ref, lb_ref, gn_ref, y_ref, o_ref, sall_ref, st_s, b_s, lf_s, q_s, k_s):
        n = pl.program_id(0)

        @pl.when(n == 0)
        def _():
            st_s[...] = jnp.zeros_like(st_s)

        heads = range(H)
        cols = [slice(h * HEAD_DIM, (h + 1) * HEAD_DIM) for h in heads]
        rows = [slice(c * CHUNK, (c + 1) * CHUNK) for c in range(CH)]
        vb = {}
        for c in range(CH):
            lv = _live(CHUNK, (n * CH + c) * CHUNK, T)
            for h in heads:
                q, k, v, logf, _ = _hgrn_gates(u_ref.at[rows[c]], lb_ref, h, D, lv)
                q_s[rows[c], cols[h]] = q
                k_s[rows[c], cols[h]] = k
                lf_s[rows[c], cols[h]] = logf
                vb[c, h] = v.astype(BF16)
        for c in range(CH):
            b_s[rows[c], :] = jnp.dot(_tri(True), lf_s[rows[c], :], precision=HI, preferred_element_type=F32)
        ops = {}
        for c in range(CH):
            for h in heads:
                b_h = b_s.at[rows[c], cols[h]]
                b = b_h[...]
                q = q_s[rows[c], cols[h]]
                k = k_s[rows[c], cols[h]]
                blast = b_h[CHUNK - 1:CHUNK, :]
                qh = (q * jnp.exp(b)).astype(BF16)
                kt = (k * jnp.exp(blast - b)).astype(BF16)
                subs = []
                for I in range(CHUNK // SUB):
                    _, qI, KI, _, _, causal = _sub_parts(q, k, b, b_h, I)
                    subs.append((qI.astype(BF16), KI.astype(BF16), causal))
                ops[c, h] = (qh, kt, jnp.exp(blast), subs)
        mm = {}
        for h in heads:
            st = st_s[h]
            for c in range(CH):
                qh, kt, eblast, subs = ops[c, h]
                sall_ref[c, h] = st
                o_inter = _dot_nt(qh, st.astype(BF16))
                st = st * eblast + _dot_tn(vb[c, h], kt)
                mm[c, h] = (o_inter, [_dot_nt(qI, KI) for qI, KI, _ in subs])
            st_s[h] = st
        for c in range(CH):
            for h in heads:
                o_inter, ps = mm[c, h]
                p = jnp.concatenate([jnp.where(m, x, 0.0) for x, (_, _, m) in zip(ps, ops[c, h][3])], axis=0).astype(BF16)
                o = o_inter + jnp.dot(p, vb[c, h], preferred_element_type=F32)
                o_ref[rows[c], cols[h]] = o
                graw = u_ref[rows[c], 3 * D + h * HEAD_DIM:3 * D + (h + 1) * HEAD_DIM]
                r = lax.rsqrt(jnp.mean(o * o, axis=-1, keepdims=True) + EPS)
                y_ref[rows[c], cols[h]] = (((o * r) * gn_ref[...]) * (graw * _sigmoid(graw))).astype(BF16)

    return pl.pallas_call(
        body, grid=(NC // CH,),
        in_specs=[pl.BlockSpec((R, 4 * D), lambda n: (n, 0)),
                  pl.BlockSpec((None, 1, D), lambda n: (layer, 0, 0)),
                  pl.BlockSpec((None, 1, HEAD_DIM), lambda n: (j, 0, 0))],
        out_specs=[pl.BlockSpec((R, D), lambda n: (n, 0)), pl.BlockSpec((R, D), lambda n: (n, 0)),
                   pl.BlockSpec((CH, H, HEAD_DIM, HEAD_DIM), lambda n: (n, 0, 0, 0))],
        out_shape=[SDS((T, D), BF16), SDS((T, D), F32), SDS((NC, H, HEAD_DIM, HEAD_DIM), F32)],
        scratch_shapes=[pltpu.VMEM((H, HEAD_DIM, HEAD_DIM), F32)] + [pltpu.VMEM((R, D), F32)] * 4,
        name=name, compiler_params=_params("arbitrary"))(u, lb3, gn3)


def _hgrn_bwd(u, o_raw, dy, sall, lb3, layer, gn3, j, name):
    T = u.shape[0]
    D = u.shape[1] // 4
    H = D // HEAD_DIM
    NC = T // CHUNK
    CH = _chunks_per_step(NC)
    R = CH * CHUNK
    NS = NC // CH

    def body(u_ref, o_ref, dy_ref, sall_ref, lb_ref, gn_ref, du_ref, dlb_ref, dgn_ref, dst_s, b_s, lf_s, q_s, k_s, db_s, dk_s):
        step = pl.program_id(0)
        n = NS - 1 - step

        @pl.when(step == 0)
        def _():
            dst_s[...] = jnp.zeros_like(dst_s)
            dlb_ref[...] = jnp.zeros_like(dlb_ref)
            dgn_ref[...] = jnp.zeros_like(dgn_ref)

        last_row = (_row_ids((CHUNK, 1), 0) == CHUNK - 1).astype(F32)
        gn = gn_ref[...]
        heads = range(H)
        chunks = range(CH)
        cols = [slice(h * HEAD_DIM, (h + 1) * HEAD_DIM) for h in heads]
        rows = [slice(c * CHUNK, (c + 1) * CHUNK) for c in chunks]
        lv = [_live(CHUNK, (n * CH + c) * CHUNK, T) for c in chunks]
        vb, dob = {}, {}
        dgn = jnp.zeros((1, HEAD_DIM), F32)
        for c in chunks:
            for h in heads:
                q, k, v, logf, _ = _hgrn_gates(u_ref.at[rows[c]], lb_ref, h, D, lv[c])
                q_s[rows[c], cols[h]] = q
                k_s[rows[c], cols[h]] = k
                lf_s[rows[c], cols[h]] = logf
                vb[c, h] = v.astype(BF16)
                graw = u_ref[rows[c], 3 * D + h * HEAD_DIM:3 * D + (h + 1) * HEAD_DIM]
                gsig = _sigmoid(graw)
                o = o_ref[rows[c], cols[h]]
                r = lax.rsqrt(jnp.mean(o * o, axis=-1, keepdims=True) + EPS)
                xh = o * r
                dyv = dy_ref[rows[c], cols[h]]
                dsg = dyv * (graw * gsig)
                dgn = dgn + jnp.sum(dsg * xh, axis=0, keepdims=True)
                dxh = dsg * gn
                do = r * (dxh - xh * jnp.mean(dxh * xh, axis=-1, keepdims=True))
                dob[c, h] = do.astype(BF16)
                dgraw = dyv * xh * gn * (gsig * (1.0 + graw * (1.0 - gsig)))
                du_ref[rows[c], 3 * D + h * HEAD_DIM:3 * D + (h + 1) * HEAD_DIM] = (dgraw * lv[c]).astype(BF16)
        dgn_ref[...] += dgn
        for c in chunks:
            b_s[rows[c], :] = jnp.dot(_tri(True), lf_s[rows[c], :], precision=HI, preferred_element_type=F32)
        ops = {}
        for c in chunks:
            for h in heads:
                b_h = b_s.at[rows[c], cols[h]]
                b = b_h[...]
                q = q_s[rows[c], cols[h]]
                k = k_s[rows[c], cols[h]]
                blast = b_h[CHUNK - 1:CHUNK, :]
                eb = jnp.exp(b)
                ekb = jnp.exp(blast - b)
                subs = []
                for I in range(CHUNK // SUB):
                    rws, qI, KI, eI, EI, causal = _sub_parts(q, k, b, b_h, I)
                    subs.append((rws, qI.astype(BF16), KI.astype(BF16), eI, EI, causal))
                ops[c, h] = (eb, ekb, jnp.exp(blast), (q * eb).astype(BF16), (k * ekb).astype(BF16), subs)
        mm = {}
        for h in heads:
            dst = dst_s[h]
            for c in reversed(chunks):
                eb, ekb, eblast, qhb, ktb, subs = ops[c, h]
                st = sall_ref[c, h]
                dstb = dst.astype(BF16)
                dv = _dot_nt(ktb, dstb)
                dqh = jnp.dot(dob[c, h], st.astype(BF16), preferred_element_type=F32)
                dkt = jnp.dot(vb[c, h], dstb, preferred_element_type=F32)
                dblast = jnp.sum(dst * st, axis=0, keepdims=True) * eblast
                dst = dst * eblast + _dot_tn(dob[c, h], qhb)
                dp_full = _dot_nt(dob[c, h], vb[c, h])
                ps = [_dot_nt(qIb, KIb) for _, qIb, KIb, _, _, _ in subs]
                mm[c, h] = (dv, dqh, dkt, dblast, dp_full, ps)
            dst_s[h] = dst
        for c in chunks:
            for h in heads:
                eb, ekb, eblast, qhb, ktb, subs = ops[c, h]
                dv, dqh, dkt, dblast, dp_full, ps = mm[c, h]
                p = jnp.concatenate([jnp.where(sub[5], x, 0.0) for x, sub in zip(ps, subs)], axis=0).astype(BF16)
                dv = dv + _dot_tn(p, dob[c, h])
                du_ref[rows[c], 2 * D + h * HEAD_DIM:2 * D + (h + 1) * HEAD_DIM] = (dv * lv[c]).astype(BF16)
                dq = dqh * eb
                db = dqh * qhb.astype(F32)
                tmp = dkt * ktb.astype(F32)
                dk = dkt * ekb
                db = db - tmp
                dblast = dblast + jnp.sum(tmp, axis=0, keepdims=True)
                dq_parts, db_parts = [], []
                for rws, qIb, KIb, eI, EI, causal in subs:
                    dp = jnp.where(causal, dp_full[rws], 0.0).astype(BF16)
                    dqI = jnp.dot(dp, KIb, preferred_element_type=F32)
                    dKI = _dot_tn(dp, qIb)
                    dq_parts.append(dqI * eI)
                    db_parts.append(dqI * qIb.astype(F32))
                    dk = dk + dKI * EI
                    db = db - dKI * KIb.astype(F32)
                dq = dq + jnp.concatenate(dq_parts, axis=0)
                db_s[rows[c], cols[h]] = db + jnp.concatenate(db_parts, axis=0) + last_row * dblast
                dk_s[rows[c], cols[h]] = dk
                qraw = u_ref[rows[c], cols[h]]
                qsig = _sigmoid(qraw)
                du_ref[rows[c], cols[h]] = (dq * (qsig * (1.0 + qraw * (1.0 - qsig))) * lv[c]).astype(BF16)
        for c in chunks:
            lf_s[rows[c], :] = jnp.dot(_tri(False), db_s[rows[c], :], precision=HI, preferred_element_type=F32)
        for h in heads:
            lbv = lb_ref[:, cols[h]]
            dlb = jnp.zeros((1, HEAD_DIM), F32)
            for c in chunks:
                fraw = u_ref[rows[c], D + h * HEAD_DIM:D + (h + 1) * HEAD_DIM]
                sig = _sigmoid(fraw)
                forget = lbv + (1.0 - lbv) * sig
                dforget = (lf_s[rows[c], cols[h]] / forget - dk_s[rows[c], cols[h]]) * lv[c]
                dlb = dlb + jnp.sum(dforget * (1.0 - sig), axis=0, keepdims=True)
                du_ref[rows[c], D + h * HEAD_DIM:D + (h + 1) * HEAD_DIM] = (dforget * (1.0 - lbv) * sig * (1.0 - sig)).astype(BF16)
            dlb_ref[:, cols[h]] += dlb

    rev = lambda s: (NS - 1 - s, 0)
    return pl.pallas_call(
        body, grid=(NS,),
        in_specs=[pl.BlockSpec((R, 4 * D), rev), pl.BlockSpec((R, D), rev), pl.BlockSpec((R, D), rev),
                  pl.BlockSpec((CH, H, HEAD_DIM, HEAD_DIM), lambda s: (NS - 1 - s, 0, 0, 0)),
                  pl.BlockSpec((None, 1, D), lambda s: (layer, 0, 0)),
                  pl.BlockSpec((None, 1, HEAD_DIM), lambda s: (j, 0, 0))],
        out_specs=[pl.BlockSpec((R, 4 * D), rev), pl.BlockSpec((1, D), lambda s: (0, 0)),
                   pl.BlockSpec((1, HEAD_DIM), lambda s: (0, 0))],
        out_shape=[SDS((T, 4 * D), BF16), SDS((1, D), F32), SDS((1, HEAD_DIM), F32)],
        scratch_shapes=[pltpu.VMEM((H, HEAD_DIM, HEAD_DIM), F32)] + [pltpu.VMEM((R, D), F32)] * 6,
        name=name, compiler_params=_params("arbitrary"))(u, o_raw, dy, sall, lb3, gn3)


def _softmax_layers(p_ref, n_layers):
    rows = [p_ref[l:l + 1, :] for l in range(n_layers)]
    m = functools.reduce(jnp.maximum, rows)
    e = [jnp.exp(x - m) for x in rows]
    tot = functools.reduce(lambda a, b: a + b, e)
    return [x / tot for x in e]


def _lb_fwd(p):
    n_layers, D = p.shape

    def body(p_ref, o_ref):
        s = _softmax_layers(p_ref, n_layers)
        acc = jnp.zeros((1, D), F32)
        o_ref[0:1, :] = acc
        for l in range(1, n_layers):
            acc = acc + s[l]
            o_ref[l:l + 1, :] = acc

    return pl.pallas_call(body, out_shape=SDS(p.shape, F32), name="lb_fwd")(p)


def _lb_bwd(p, dlb):
    n_layers, D = p.shape

    def body(p_ref, d_ref, o_ref):
        s = _softmax_layers(p_ref, n_layers)
        ds = [jnp.zeros((1, D), F32)] * n_layers
        acc = jnp.zeros((1, D), F32)
        for l in range(n_layers - 1, 0, -1):
            acc = acc + d_ref[l:l + 1, :]
            ds[l] = acc
        dot = functools.reduce(lambda a, b: a + b, [s[l] * ds[l] for l in range(n_layers)])
        for l in range(n_layers):
            o_ref[l:l + 1, :] = s[l] * (ds[l] - dot)

    return pl.pallas_call(body, out_shape=SDS(p.shape, F32), name="lb_bwd")(p, dlb)


def _adamw(w, g, m, v, name):
    R, C = w.shape
    tr = _tile(R, 256, 8) if R % 8 == 0 else R

    def body(w_ref, g_ref, m_ref, v_ref, d_ref, mo_ref, vo_ref):
        g_ = g_ref[...]
        m_ = ADAM_B1 * m_ref[...] + (1.0 - ADAM_B1) * g_
        v_ = ADAM_B2 * v_ref[...] + (1.0 - ADAM_B2) * (g_ * g_)
        mh = m_ / (1.0 - ADAM_B1 ** ADAM_STEP)
        vh = v_ / (1.0 - ADAM_B2 ** ADAM_STEP)
        d_ref[...] = -ADAM_LR * (mh / (jnp.sqrt(vh) + ADAM_EPS) + ADAM_WD * w_ref[...])
        mo_ref[...] = m_
        vo_ref[...] = v_

    blk = pl.BlockSpec((tr, C), lambda i: (i, 0))
    return pl.pallas_call(
        body, grid=(R // tr,), in_specs=[blk] * 4, out_specs=[blk] * 3, out_shape=[SDS((R, C), F32)] * 3,
        name=name, compiler_params=_params("parallel"))(w, g, m, v)


def _adamw_layer(w3, m3, v3, g2, layer, outs, name):
    L, R, C = w3.shape
    tr = _tile(R, 256, 8)
    if outs is None:
        outs = tuple(lax.empty(w3.shape, F32) for _ in range(4))

    def body(w_ref, m_ref, v_ref, g_ref, a0, a1, a2, a3, go_ref, d_ref, mo_ref, vo_ref):
        del a0, a1, a2, a3
        g_ = g_ref[...]
        m_ = ADAM_B1 * m_ref[...] + (1.0 - ADAM_B1) * g_
        v_ = ADAM_B2 * v_ref[...] + (1.0 - ADAM_B2) * (g_ * g_)
        mh = m_ / (1.0 - ADAM_B1 ** ADAM_STEP)
        vh = v_ / (1.0 - ADAM_B2 ** ADAM_STEP)
        go_ref[...] = g_
        d_ref[...] = -ADAM_LR * (mh / (jnp.sqrt(vh) + ADAM_EPS) + ADAM_WD * w_ref[...])
        mo_ref[...] = m_
        vo_ref[...] = v_

    lay = pl.BlockSpec((None, tr, C), lambda i: (layer, i, 0))
    return pl.pallas_call(
        body, grid=(R // tr,), in_specs=[lay] * 3 + [pl.BlockSpec((tr, C), lambda i: (i, 0))] + [ANY_SPEC] * 4,
        out_specs=[lay] * 4, out_shape=[SDS(w3.shape, F32)] * 4, input_output_aliases={4: 0, 5: 1, 6: 2, 7: 3},
        name=name, compiler_params=_params("parallel"))(w3, m3, v3, g2, *outs)


SEM_SPEC = pl.BlockSpec(memory_space=pltpu.SEMAPHORE)
HBM_SPEC = pl.BlockSpec(memory_space=pltpu.HBM)
EFFECT = pltpu.SideEffectType.DATAFLOW_SIDE_EFFECTING
N_DEV = 2 * N_CHIPS


def _position():
    x, y, c = lax.axis_index("x"), lax.axis_index("y"), lax.axis_index("c")
    chips = [(1 - x, y), (x, 1 - y), (1 - x, 1 - y)]
    return x, y, c, chips


def _split_start(name, plan, bufs, n_sems, deps=(), earlier=None):
    n = len(bufs)
    held = () if earlier is None else tuple(earlier[1:])

    def body(*refs):
        first_out = n + len(held) + len(deps)
        if earlier is not None:
            sends, recvs = earlier[0](refs[:n], refs[n], refs[n + 1])
            for kw in sends:
                pltpu.make_async_remote_copy(**kw).wait_send()
            for kw in recvs:
                pltpu.make_async_remote_copy(**kw).wait_recv()
        sends, _ = plan(refs[:n], refs[first_out], refs[first_out + 1])
        for kw in sends:
            pltpu.make_async_remote_copy(**kw).start()
        refs[-1][...] = jnp.zeros_like(refs[-1])

    out = pl.pallas_call(
        body, name=name,
        out_shape=(pltpu.SemaphoreType.DMA((n_sems,)), pltpu.SemaphoreType.DMA((n_sems,)),
                   *[pltpu.HBM(b.shape, b.dtype) for b in bufs], SDS((8, 128), F32)),
        in_specs=[HBM_SPEC] * n + [SEM_SPEC] * len(held) + [ANY_SPEC] * len(deps),
        out_specs=(SEM_SPEC, SEM_SPEC, *[HBM_SPEC] * n, pl.BlockSpec(memory_space=pltpu.VMEM)),
        input_output_aliases={i: 2 + i for i in range(n)},
        compiler_params=pltpu.CompilerParams(has_side_effects=EFFECT),
    )(*[pltpu.with_memory_space_constraint(b, pltpu.HBM) for b in bufs], *held, *deps)
    return out[0], out[1], list(out[2:2 + n]), out[-1]


def _split_wait(name, plan, send_sems, recv_sems, bufs, after=()):
    n = len(bufs)

    def body(*refs):
        sends, recvs = plan(refs[:n], refs[n], refs[n + 1])
        for kw in sends:
            pltpu.make_async_remote_copy(**kw).wait_send()
        for kw in recvs:
            pltpu.make_async_remote_copy(**kw).wait_recv()

    out = pl.pallas_call(
        body, name=name, out_shape=tuple(pltpu.HBM(b.shape, b.dtype) for b in bufs),
        in_specs=[HBM_SPEC] * n + [SEM_SPEC, SEM_SPEC] + [ANY_SPEC] * len(after),
        out_specs=tuple([HBM_SPEC] * n), input_output_aliases={i: i for i in range(n)},
        compiler_params=pltpu.CompilerParams(has_side_effects=EFFECT),
    )(*bufs, send_sems, recv_sems, *after)
    return list(out)


def _region(kind, ref, chip, half):
    K, N = ref.shape
    if kind == "col":
        return ref.at[pl.ds(half * (K // 2), K // 2), pl.ds(chip * (N // N_CHIPS), N // N_CHIPS)]
    rows = K // (2 * N_CHIPS)
    return ref.at[pl.ds((2 * chip + half) * rows, rows), :]


def _gather_plan(kinds, over_chips, first=0):
    def plan(refs, send_sems, recv_sems):
        x, y, c, chips = _position()
        sends, recvs = [], []
        for f, (ref, kind) in enumerate(zip(refs, kinds)):
            for k, chip in enumerate(chips):
                theirs = 2 * chip[0] + chip[1]
                at = 3 * (first + f) + k
                sem = dict(send_sem=send_sems.at[at], recv_sem=recv_sems.at[at], device_id_type=MESH)
                if over_chips:
                    out, back, to = _region(kind, ref, 2 * x + y, c), _region(kind, ref, theirs, c), (*chip, c)
                else:
                    out, back, to = _region(kind, ref, theirs, c), _region(kind, ref, theirs, 1 - c), (x, y, 1 - c)
                sends.append(dict(src_ref=out, dst_ref=out, device_id=to, **sem))
                recvs.append(dict(src_ref=back, dst_ref=back, device_id=to, **sem))
        return sends, recvs
    return plan


def _reduce_plan(first=0):
    def plan(refs, send_sems, recv_sems):
        x, y, c, _ = _position()
        me = 4 * x + 2 * y + c
        sends, recvs = [], []
        for f in range(len(refs) // 2):
            acc, land = refs[2 * f], refs[2 * f + 1]
            for d in range(1, N_DEV):
                t = (me + d) % N_DEV
                to = dict(device_id=(t // 4, (t // 2) % 2, t % 2), device_id_type=MESH)
                slot = N_DEV - 1 - d
                at = first + 7 * f
                sends.append(dict(src_ref=acc.at[t % 2, t // 2], dst_ref=land.at[slot], send_sem=send_sems.at[at + d - 1],
                                  recv_sem=recv_sems.at[at + slot], **to))
                recvs.append(dict(src_ref=land.at[d - 1], dst_ref=land.at[d - 1], send_sem=send_sems.at[at + d - 1],
                                  recv_sem=recv_sems.at[at + d - 1], **to))
        return sends, recvs
    return plan


def _swap_plan(first=0):
    def plan(refs, send_sems, recv_sems):
        x, y, c, _ = _position()
        sends, recvs = [], []
        for f, g in enumerate(refs):
            sem = dict(send_sem=send_sems.at[first + f], recv_sem=recv_sems.at[first + f], device_id=(x, y, 1 - c),
                       device_id_type=MESH)
            sends.append(dict(src_ref=g.at[c], dst_ref=g.at[c], **sem))
            recvs.append(dict(src_ref=g.at[1 - c], dst_ref=g.at[1 - c], **sem))
        return sends, recvs
    return plan


def _joined(plans):
    def plan(refs, send_sems, recv_sems):
        sends, recvs, lo = [], [], 0
        for part, n in plans:
            s_, r_ = part(refs[lo:lo + n], send_sems, recv_sems)
            sends += s_
            recvs += r_
            lo += n
        return sends, recvs
    return plan


def _sum_pieces(ids2, acc, land, name):
    _, _, nr, nc = acc.shape
    tr = _tile(nr, 256, 16)

    def body(ids_ref, own_ref, land_ref, o_ref):
        del ids_ref
        s = own_ref[...].astype(F32)
        for k in range(N_DEV - 1):
            s = s + land_ref[k].astype(F32)
        o_ref[...] = s

    return pl.pallas_call(
        body,
        grid_spec=pltpu.PrefetchScalarGridSpec(
            num_scalar_prefetch=1, grid=(nr // tr,),
            in_specs=[pl.BlockSpec((None, None, tr, nc), lambda i, ids: (ids[0], ids[1], i, 0)),
                      pl.BlockSpec((N_DEV - 1, tr, nc), lambda i, ids: (0, i, 0))],
            out_specs=pl.BlockSpec((None, tr, nc), lambda i, ids: (ids[0], i, 0))),
        out_shape=SDS((2, nr, nc), F32), name=name, compiler_params=_params("parallel"))(ids2, acc, land)


def _small_plan(refs, send_sems, recv_sems):
    x, y, c, _ = _position()
    me = 4 * x + 2 * y + c
    own, land = refs
    sends, recvs = [], []
    for d in range(1, N_DEV):
        t = (me + d) % N_DEV
        to = dict(device_id=(t // 4, (t // 2) % 2, t % 2), device_id_type=MESH)
        sends.append(dict(src_ref=own, dst_ref=land.at[me], send_sem=send_sems.at[d - 1],
                          recv_sem=recv_sems.at[N_DEV - 1 - d], **to))
        recvs.append(dict(src_ref=land.at[t], dst_ref=land.at[t], send_sem=send_sems.at[d - 1],
                          recv_sem=recv_sems.at[d - 1], **to))
    return sends, recvs


def _sum_blocks(me1, own, land):
    def body(me_ref, own_ref, land_ref, o_ref):
        acc = None
        for d in range(N_DEV):
            term = jnp.where(me_ref[0] == d, own_ref[...], land_ref[d])
            acc = term if acc is None else acc + term
        o_ref[...] = acc

    return pl.pallas_call(
        body,
        grid_spec=pltpu.PrefetchScalarGridSpec(
            num_scalar_prefetch=1, grid=(1,),
            in_specs=[pl.BlockSpec(own.shape, lambda i, me: (0, 0)), pl.BlockSpec(land.shape, lambda i, me: (0, 0, 0))],
            out_specs=pl.BlockSpec(own.shape, lambda i, me: (0, 0))),
        out_shape=SDS(own.shape, F32), name="sum_small", compiler_params=_params("arbitrary"))(me1, own, land)


BIG = {"ev_w_in": "col", "ev_w_out": "row", "od_w_in": "col", "od_w_out": "row", "mlp_w1": "col", "mlp_w2": "row"}
WEIGHTS = ("meta_tokens", "mix_norm_g", "mlp_norm_g", "final_norm_g", "ev_w_in", "ev_conv_w", "ev_conv_b", "ev_ln_g",
           "ev_ln_b", "ev_pool_w", "ev_pool_b", "ev_pool_scale", "ev_w_out", "od_w_in", "od_gnorm_g", "od_w_out",
           "lb_param", "mlp_w1", "mlp_w2")
PACK_UNIT = 1024


def _mixer_names(layer):
    return ("ev_w_in", "ev_w_out") if layer % 2 == 0 else ("od_w_in", "od_w_out")


def _pack(arrays):
    flat = []
    for a in arrays:
        a = a.reshape(-1)
        flat.append(jnp.pad(a, (0, (-a.shape[0]) % PACK_UNIT)))
    return jnp.concatenate(flat).reshape(-1, 128)


def _unpack(packed, shapes):
    flat = packed.reshape(-1)
    out, off = [], 0
    for s in shapes:
        size = 1
        for d in s:
            size *= d
        out.append(flat[off:off + size].reshape(s))
        off += size + (-size) % PACK_UNIT
    return out


def _local_step(x2, target, P, weights, boundary, first_deps=()):
    D = x2.shape[1]
    n_layers = P["mix_norm_g"].shape[0]
    h = jnp.concatenate([jnp.zeros((PAD, D), F32), P["meta_full"], x2], axis=0)
    mix_g = P["mix_norm_g"].reshape(n_layers, 1, D)
    mlp_g = P["mlp_norm_g"].reshape(n_layers, 1, D)
    vec = lambda a: a.reshape(a.shape[0], 1, -1)
    cb3, lg3, lnb3, ps3 = vec(P["ev_conv_b"]), vec(P["ev_ln_g"]), vec(P["ev_ln_b"]), vec(P["ev_pool_scale"])
    pb3 = vec(P["ev_pool_b"])
    gn3 = vec(P["od_gnorm_g"])
    lb_all = _lb_fwd(P["lb_param"])
    lb3 = lb_all.reshape(n_layers, 1, D)
    even = (cb3, lg3, lnb3, P["ev_pool_w"], pb3, ps3)

    saved = []
    deps = tuple(first_deps)
    for layer in range(n_layers):
        j = layer // 2
        w_in, w_out = _mixer_names(layer)
        W = {}
        s = {"h": h, "W": W}
        s["n"] = _rms_fwd(h, mix_g, layer, "mix_norm_0", deps=deps) if layer == 0 else n_next
        deps = ()
        W[w_in], held = weights(layer, w_in, (s["n"],))
        s["u"] = _mm_nn(s["n"], W[w_in], 0, f"mix_in_{layer}", deps=held)
        if layer % 2 == 0:
            s["y"], s["yc"] = _even_fwd(s["u"], P["conv_w_full"], *even, j, f"even_fwd_{layer}")
        else:
            s["y"], s["o"], s["sall"] = _hgrn_fwd(s["u"], lb3, layer, gn3, j, f"hgrn_fwd_{layer}")
        W[w_out], held = weights(layer, w_out, (s["y"],))
        if layer == 0:
            h, s["n2"] = _mm_nn_norm(s["y"], W[w_out], 0, h, mlp_g, layer, "mix_out_0", deps=held)
            s["h1"] = h
            W["mlp_w1"], held = weights(layer, "mlp_w1", (s["n2"],))
            s["relu"] = _mm_nn(s["n2"], W["mlp_w1"], 0, "mlp_up_0", relu=True, deps=held)
            W["mlp_w2"], held = weights(layer, "mlp_w2", (s["relu"],))
            h, n_next = _mm_nn_norm(s["relu"], W["mlp_w2"], 0, h, mix_g, 1, "mlp_down_0", square=True, deps=held)
        else:
            W["mlp_w1"], more1 = weights(layer, "mlp_w1", (s["y"],))
            W["mlp_w2"], more2 = weights(layer, "mlp_w2", (s["y"],))
            last = layer + 1 == n_layers
            out = _tail_fwd(s["y"], W[w_out], h, mlp_g, layer, W["mlp_w1"], W["mlp_w2"], None if last else mix_g,
                            f"tail_{layer}", deps=held + more1 + more2)
            s["h1"], s["n2"], h, s["relu"] = out[0], out[1], out[2], out[-1]
            n_next = None if last else out[3]
        saved.append(s)

    dh, dhb, dg_final, loss = _final(h, P["final_norm_g"].reshape(1, D), target)

    small = {"final_norm_g": dg_final}
    per_layer = {k: [None] * n_layers for k in ("mix_norm_g", "mlp_norm_g", "lb")}
    per_pair = {k: [None] * (n_layers // 2) for k in
                ("ev_conv_w", "ev_conv_b", "ev_ln_g", "ev_ln_b", "ev_pool_w", "ev_pool_b", "ev_pool_scale", "od_gnorm_g")}
    for layer in reversed(range(n_layers)):
        j = layer // 2
        s = saved[layer]
        W = s["W"]
        w_in, w_out = _mixer_names(layer)
        dw2 = _mm_tn(s["relu"], dhb, "row", f"dw2_{layer}", square=True)
        dz, dh, dhb, per_layer["mlp_norm_g"][layer] = _mlp_bwd(
            dhb, s["relu"], W["mlp_w1"], W["mlp_w2"], s["h1"], mlp_g, layer, dh, f"mlp_bwd_{layer}", deps=deps + (dw2,))
        dw1 = _mm_tn(s["n2"], dz, "col", f"dw1_{layer}")
        deps = boundary(f"mlp{layer}", {("mlp_w1", layer): dw1, ("mlp_w2", layer): dw2}, (dhb, dw1, dw2))
        dy = _mm_nt(dhb, W[w_out], 0, f"d_y_{layer}", deps=deps)
        dwout = _mm_tn(s["y"], dhb, "row", f"dwout_{layer}")
        if layer % 2 == 0:
            du, dcw, dcb, dlg, dlnb, dpw, dpb, dps = _even_bwd(s["u"], s["yc"], dy, P["conv_w_full"], *even, j, f"even_bwd_{layer}")
            for k, val in (("ev_conv_w", dcw), ("ev_conv_b", dcb), ("ev_ln_g", dlg), ("ev_ln_b", dlnb),
                           ("ev_pool_w", dpw), ("ev_pool_b", dpb), ("ev_pool_scale", dps)):
                per_pair[k][j] = val
        else:
            du, per_layer["lb"][layer], per_pair["od_gnorm_g"][j] = _hgrn_bwd(
                s["u"], s["o"], dy, s["sall"], lb3, layer, gn3, j, f"hgrn_bwd_{layer}")
        dwin = _mm_tn(s["n"], du, "col", f"dwin_{layer}")
        deps = boundary(f"mix{layer}", {(w_in, j): dwin, (w_out, j): dwout}, (du, dwin, dwout))
        dh, dhb, per_layer["mix_norm_g"][layer] = _mm_nt_norm(du, W[w_in], 0, s["h"], mix_g, layer, dh, f"d_n_{layer}", deps=deps)
        deps = ()

    small["mix_norm_g"] = jnp.concatenate(per_layer["mix_norm_g"], axis=0)
    small["mlp_norm_g"] = jnp.concatenate(per_layer["mlp_norm_g"], axis=0)
    dlb_all = jnp.concatenate([jnp.zeros((1, D), F32) if g is None else g for g in per_layer["lb"]], axis=0)
    small["lb_param"] = _lb_bwd(P["lb_param"], dlb_all)
    for k, vals in per_pair.items():
        small[k] = jnp.stack(vals, axis=0)
    small["meta_tokens"] = dh[PAD:LEAD]
    return loss, dh, small


def kernel(x, meta_tokens, mix_norm_g, mlp_norm_g, final_norm_g, ev_w_in, ev_conv_w, ev_conv_b, ev_ln_g, ev_ln_b, ev_pool_w, ev_pool_b, ev_pool_scale, ev_w_out, od_w_in, od_gnorm_g, od_w_out, lb_param, mlp_w1, mlp_w2, loss_target, m_meta_tokens, m_mix_norm_g, m_mlp_norm_g, m_final_norm_g, m_ev_w_in, m_ev_conv_w, m_ev_conv_b, m_ev_ln_g, m_ev_ln_b, m_ev_pool_w, m_ev_pool_b, m_ev_pool_scale, m_ev_w_out, m_od_w_in, m_od_gnorm_g, m_od_w_out, m_lb_param, m_mlp_w1, m_mlp_w2, v_meta_tokens, v_mix_norm_g, v_mlp_norm_g, v_final_norm_g, v_ev_w_in, v_ev_conv_w, v_ev_conv_b, v_ev_ln_g, v_ev_ln_b, v_ev_pool_w, v_ev_pool_b, v_ev_pool_scale, v_ev_w_out, v_od_w_in, v_od_gnorm_g, v_od_w_out, v_lb_param, v_mlp_w1, v_mlp_w2):
    given = dict(locals())
    w = {n: given[n] for n in WEIGHTS}
    m = {n: given["m_" + n] for n in WEIGHTS}
    v = {n: given["v_" + n] for n in WEIGHTS}
    n_layers = mix_norm_g.shape[0]
    core = lax.axis_index("c").astype(jnp.int32)
    chip = (2 * lax.axis_index("x") + lax.axis_index("y")).astype(jnp.int32)
    chip1 = chip.reshape(1)
    ids2 = jnp.stack([core, chip])

    conv_pad = jnp.pad(ev_conv_w, ((0, 0), (0, CONV_ROWS - CONV_WIDTH), (0, 0)))
    stages = [[(0, n)] for n in (*_mixer_names(0), "mlp_w1", "mlp_w2")]
    for layer in range(1, n_layers):
        stages += [[(layer, n) for n in _mixer_names(layer)], [(layer, "mlp_w1"), (layer, "mlp_w2")]]
    where, stage_kinds, stage_bufs = {}, [], []
    for k, stage in enumerate(stages):
        index = [layer if n.startswith("mlp") else layer // 2 for layer, n in stage]
        kinds = [BIG[n] for _, n in stage]
        bufs = [_cast_place(w[n], i, BIG[n], chip1, BF16, f"place_{n}_{i}") for (_, n), i in zip(stage, index)]
        if k == 0:
            bufs.append(_cast_place(meta_tokens[None], 0, "col", chip1, F32, "place_meta"))
            bufs.append(_cast_place(conv_pad.reshape(1, -1, conv_pad.shape[2]), 0, "col", chip1, F32, "place_conv_w"))
            kinds += ["col", "col"]
        stage_kinds.append(kinds)
        stage_bufs.append(bufs)
        where.update({key: (k, f) for f, key in enumerate(stage)})
    every = [b for bufs in stage_bufs for b in bufs]
    ss, rs, every, tok = _split_start("gather_start", _gather_plan([kd for kinds in stage_kinds for kd in kinds], True),
                                      every, 3 * len(every))
    token = (tok,)
    gathers, at = [], 0
    for kinds in stage_kinds:
        gathers.append((kinds, _gather_plan(kinds, True, first=at), ss, rs, every[at:at + len(kinds)]))
        at += len(kinds)

    landed, passed, held = {}, {}, []

    def hand_on(k, deps):
        if k not in passed:
            kinds, plan, ss, rs, bufs = gathers[k]
            to_sibling = _gather_plan(kinds, False)
            ss, rs, bufs, tok = _split_start(f"gather_pass_{k}", to_sibling, bufs, 3 * len(bufs), deps=deps, earlier=(plan, ss, rs))
            passed[k] = (to_sibling, ss, rs, bufs)
            held.append(tok)

    def arrived(k, after):
        if k not in landed:
            hand_on(k, after)
            landed[k] = _split_wait(f"gather_wait_{k}", *passed[k], after)
        return landed[k]

    def weights(layer, name, after):
        k, f = where[(layer, name)]
        full = arrived(k, after)[f][None]
        if name == "mlp_w2" and layer + 1 < n_layers:
            hand_on(where[(layer + 1, _mixer_names(layer + 1)[0])][0], after)
        if layer > 0 and name == _mixer_names(layer)[0]:
            hand_on(where[(layer, "mlp_w1")][0], after)
        tokens = tuple(held)
        held.clear()
        return full, tokens

    first = arrived(0, token)
    P = {n: w[n] for n in ("mix_norm_g", "mlp_norm_g", "final_norm_g", "ev_conv_b", "ev_ln_g", "ev_ln_b", "ev_pool_w",
                           "ev_pool_b", "ev_pool_scale", "od_gnorm_g", "lb_param")}
    P["meta_full"] = first[1]
    P["conv_w_full"] = first[2].reshape(ev_conv_w.shape[0], CONV_ROWS, -1)

    pending, outs = [], {n: None for n in BIG}

    def advance(after, fresh=1):
        ready, still = [], []
        for pos, st in enumerate(pending):
            if st["phase"] == 1 and pos >= len(pending) - fresh:
                still.append(st)
            elif st["phase"] == 1:
                bufs = _split_wait(f"reduce_wait_{st['tag']}", st["plan"], st["ss"], st["rs"], st["bufs"], after)
                halves = [_sum_pieces(ids2, bufs[2 * f], bufs[2 * f + 1], f"sum_{st['tag']}_{f}") for f in range(len(bufs) // 2)]
                ready.append((st, halves))
            else:
                grads = _split_wait(f"swap_wait_{st['tag']}", st["plan"], st["ss"], st["rs"], st["bufs"], after)
                for (n, i), g in zip(st["keys"], grads):
                    outs[n] = _adamw_layer(w[n], m[n], v[n], g.reshape(w[n].shape[1:]), i, outs[n], f"adamw_{n}_{i}")
        pending[:] = still
        return ready

    def launch(name, ready, tag=None, grads=None):
        bufs, parts, entries, at = [], [], [], 0
        for st, halves in ready:
            plan = _swap_plan(first=at)
            entries.append((dict(st, phase=2, plan=plan), len(bufs), len(halves)))
            parts.append((plan, len(halves)))
            bufs += halves
            at += len(halves)
        if grads is not None:
            pairs = []
            for acc in grads.values():
                pairs += [acc, lax.empty((N_DEV - 1,) + acc.shape[2:], BF16)]
            plan = _reduce_plan(first=at)
            entries.append((dict(phase=1, tag=tag, keys=list(grads), plan=plan), len(bufs), len(pairs)))
            parts.append((plan, len(pairs)))
            bufs += pairs
            at += 7 * len(grads)
        if not bufs:
            return ()
        ss, rs, bufs, tok = _split_start(name, _joined(parts), bufs, at)
        for st, lo, n in entries:
            pending.append(dict(st, ss=ss, rs=rs, bufs=bufs[lo:lo + n]))
        return (tok,)

    def boundary(tag, grads, after):
        return launch(f"start_{tag}", advance(after), tag, grads)

    loss, dh, small = _local_step(x[0], loss_target[0], P, weights, boundary, first_deps=token)

    order = [n for n in WEIGHTS if n not in BIG]
    block = _pack([small[n] for n in order] + [loss])
    ss, rs, bufs, tok = _split_start("small_start", _small_plan, [block, lax.empty((N_DEV,) + block.shape, F32)], N_DEV - 1)
    for last in range(3):
        launch(f"start_end_{last}", advance((tok,) + tuple(o[0] for o in outs.values() if o is not None), fresh=0))
    assert not pending
    block, land = _split_wait("small_wait", _small_plan, ss, rs, bufs, tuple(outs[n][0] for n in BIG))
    packed = _sum_blocks((4 * lax.axis_index("x") + 2 * lax.axis_index("y") + lax.axis_index("c")).astype(jnp.int32).reshape(1), block, land)
    total = _unpack(packed, [small[n].shape for n in order] + [loss.shape])
    loss_sum = total[-1][0, 0]
    gsmall = dict(zip(order, total[:-1]))
    gsmall["meta_tokens"] = lax.dynamic_slice_in_dim(gsmall["meta_tokens"], chip * meta_tokens.shape[1], meta_tokens.shape[1], 1)
    gsmall["ev_conv_w"] = lax.dynamic_slice_in_dim(gsmall["ev_conv_w"][:, :CONV_WIDTH], chip * ev_conv_w.shape[2], ev_conv_w.shape[2], 2)

    g_out, d_out, m_out, v_out = {}, {}, {}, {}
    for n in WEIGHTS:
        if n in BIG:
            g_out[n], d_out[n], m_out[n], v_out[n] = outs[n]
            continue
        shape = w[n].shape
        g = gsmall[n].reshape(shape)
        cols = shape[-1] if len(shape) > 1 else 128
        two = lambda a: a.reshape(-1, cols)
        d_, m_, v_ = _adamw(two(w[n]), two(g), two(m[n]), two(v[n]), f"adamw_{n}")
        g_out[n], d_out[n], m_out[n], v_out[n] = g, d_.reshape(shape), m_.reshape(shape), v_.reshape(shape)

    grad_x = dh[LEAD:][None]
    return (loss_sum, grad_x, *[g_out[n] for n in WEIGHTS], *[d_out[n] for n in WEIGHTS],
            *[m_out[n] for n in WEIGHTS], *[v_out[n] for n in WEIGHTS])
```

```python
import functools

import jax
import jax.numpy as jnp
from jax import lax
from jax.experimental import pallas as pl
from jax.experimental.pallas import tpu as pltpu

F32 = jnp.float32
BF16 = jnp.bfloat16
SDS = jax.ShapeDtypeStruct
MESH = pl.DeviceIdType.MESH
ANY_SPEC = pl.BlockSpec(memory_space=pl.ANY)

N_META = 16
CHUNK = 64
LEAD = CHUNK
PAD = LEAD - N_META
CONV_WIDTH = 31
CONV_ROWS = 32
POOL_WINDOWS = (2, 4, 8, 16)
HEAD_DIM = 128
SUB = 16
EXP_CAP = 80.0
EPS = 1e-6
ADAM_LR = 0.001
ADAM_B1 = 0.9
ADAM_B2 = 0.999
ADAM_EPS = 1e-08
ADAM_WD = 0.01
ADAM_STEP = 10
N_CHIPS = 4
VMEM_LIMIT = 58 << 20
MM_VMEM_BUDGET = 50 << 20


def _params(*sem):
    return pltpu.CompilerParams(dimension_semantics=sem if sem else None, vmem_limit_bytes=VMEM_LIMIT)


def _tile(n, target, unit=CHUNK):
    best = None
    for t in range(unit, min(n, target) + 1, unit):
        if n % t == 0:
            best = t
    assert best is not None, (n, target, unit)
    return best


def _ctile(n, target=512):
    for t in (512, 384, 256, 128):
        if t <= target and n % t == 0:
            return t
    raise ValueError(n)


def _mm_tiles(M, N, per_row, per_col, per_elem):
    best = None
    for tn in (512, 384, 256, 128):
        if N % tn:
            continue
        for tm in sorted((d for d in range(16, M + 1, 16) if M % d == 0), reverse=True):
            if 2 * (tm * per_row + tn * per_col + tm * tn * per_elem) <= MM_VMEM_BUDGET:
                if best is None or tm * tn > best[0] * best[1]:
                    best = (tm, tn)
                break
    assert best is not None, (M, N)
    return best


def _sigmoid(x):
    return 1.0 / (1.0 + jnp.exp(-x))


def _mult(v, m):
    return v if isinstance(v, int) else pl.multiple_of(v, m)


def _row_ids(shape, base):
    return lax.broadcasted_iota(jnp.int32, shape, 0) + base


def _cast_place(w3, layer, kind, chip1, dtype, name):
    _, ks, ns = w3.shape
    tr = _tile(ks, 512, 16)
    full = (ks, ns * N_CHIPS) if kind == "col" else (ks * N_CHIPS, ns)

    def body(chip_ref, w_ref, o_ref):
        del chip_ref
        o_ref[...] = w_ref[...].astype(dtype)

    omap = (lambda i, chip: (i, chip[0])) if kind == "col" else (lambda i, chip: (chip[0] * (ks // tr) + i, 0))
    return pl.pallas_call(
        body,
        grid_spec=pltpu.PrefetchScalarGridSpec(
            num_scalar_prefetch=1, grid=(ks // tr,),
            in_specs=[pl.BlockSpec((None, tr, ns), lambda i, chip: (layer, i, 0))],
            out_specs=pl.BlockSpec((tr, ns), omap)),
        out_shape=SDS(full, dtype), name=name, compiler_params=_params("parallel"))(chip1, w3)


def _rms_fwd(h, g3, layer, name, deps=()):
    T, D = h.shape
    tm = _tile(T, 832)

    def body(h_ref, g_ref, *rest):
        n_ref = rest[-1]
        x = h_ref[...]
        r = lax.rsqrt(jnp.mean(x * x, axis=-1, keepdims=True) + EPS)
        n_ref[...] = ((x * r) * g_ref[...]).astype(BF16)

    return pl.pallas_call(
        body, grid=(T // tm,),
        in_specs=[pl.BlockSpec((tm, D), lambda i: (i, 0)), pl.BlockSpec((None, 1, D), lambda i: (layer, 0, 0))]
        + [ANY_SPEC] * len(deps),
        out_specs=pl.BlockSpec((tm, D), lambda i: (i, 0)), out_shape=SDS((T, D), BF16),
        name=name, compiler_params=_params("parallel"))(h, g3, *deps)


def _final(h, g2, target):
    T, D = h.shape
    tm = _tile(T, 320)
    nsub = tm // CHUNK
    nblk = target.shape[0] // CHUNK

    def body(h_ref, g_ref, *rest):
        t_refs = rest[:nsub]
        dh_ref, dhb_ref, dg_ref, loss_ref = rest[nsub:]
        i = pl.program_id(0)

        @pl.when(i == 0)
        def _():
            dg_ref[...] = jnp.zeros_like(dg_ref)
            loss_ref[...] = jnp.zeros_like(loss_ref)

        g = g_ref[...]
        for q in range(nsub):
            rows = slice(q * CHUNK, (q + 1) * CHUNK)
            x = h_ref[rows, :]
            r = lax.rsqrt(jnp.mean(x * x, axis=-1, keepdims=True) + EPS)
            xh = x * r
            live = jnp.where(i * nsub + q > 0, 1.0, 0.0).astype(F32)
            e = ((xh * g) - t_refs[q][...]) * live
            dy = e * (1.0 / D)
            dxh = dy * g
            dh = r * (dxh - xh * jnp.mean(dxh * xh, axis=-1, keepdims=True))
            dh_ref[rows, :] = dh
            dhb_ref[rows, :] = dh.astype(BF16)
            dg_ref[...] += jnp.sum(dy * xh, axis=0, keepdims=True)
            loss_ref[...] += jnp.sum(e * e) * (0.5 / D)

    row = pl.BlockSpec((tm, D), lambda i: (i, 0))
    t_specs = [pl.BlockSpec((CHUNK, D), functools.partial(lambda i, q: (jnp.clip(i * nsub + q - 1, 0, nblk - 1), 0), q=q))
               for q in range(nsub)]
    return pl.pallas_call(
        body, grid=(T // tm,),
        in_specs=[row, pl.BlockSpec((1, D), lambda i: (0, 0))] + t_specs,
        out_specs=[row, row, pl.BlockSpec((1, D), lambda i: (0, 0)), pl.BlockSpec((1, 128), lambda i: (0, 0))],
        out_shape=[SDS((T, D), F32), SDS((T, D), BF16), SDS((1, D), F32), SDS((1, 128), F32)],
        name="final_loss", compiler_params=_params("arbitrary"))(h, g2, *([target] * nsub))


def _mm_nn(a, w3, layer, name, res=None, relu=False, square=False, deps=()):
    M, K = a.shape
    N = w3.shape[2]
    tm, tn = _mm_tiles(M, N, 2 * K, 2 * K, (2 if relu else 4) + (4 if res is not None else 0))

    def body(*refs):
        lhs = refs[0][...]
        acc = jnp.dot(lhs * lhs if square else lhs, refs[1][...], preferred_element_type=F32)
        if res is not None:
            acc = acc + refs[2][...]
        refs[-1][...] = jnp.maximum(acc, 0.0).astype(BF16) if relu else acc

    in_specs = [pl.BlockSpec((tm, K), lambda i, j: (i, 0)), pl.BlockSpec((None, K, tn), lambda i, j: (layer, 0, j))]
    args = [a, w3]
    tile = pl.BlockSpec((tm, tn), lambda i, j: (i, j))
    if res is not None:
        in_specs.append(tile)
        args.append(res)
    in_specs += [ANY_SPEC] * len(deps)
    args += list(deps)
    return pl.pallas_call(
        body, grid=(M // tm, N // tn), in_specs=in_specs, out_specs=tile,
        out_shape=SDS((M, N), BF16 if relu else F32),
        name=name, compiler_params=_params("parallel", "parallel"))(*args)


def _mm_nt(dy, w3, layer, name, relu=None, deps=()):
    M, N = dy.shape
    K = w3.shape[1]
    tm, tk = _mm_tiles(M, K, 2 * N, 2 * N, 4)

    def body(*refs):
        acc = lax.dot_general(refs[0][...], refs[1][...], (((1,), (1,)), ((), ())), preferred_element_type=F32)
        if relu is not None:
            acc = (acc * (2.0 * refs[2][...].astype(F32))).astype(BF16)
        refs[-1][...] = acc

    tile = pl.BlockSpec((tm, tk), lambda i, j: (i, j))
    in_specs = [pl.BlockSpec((tm, N), lambda i, j: (i, 0)), pl.BlockSpec((None, tk, N), lambda i, j: (layer, j, 0))]
    args = [dy, w3]
    if relu is not None:
        in_specs.append(tile)
        args.append(relu)
    in_specs += [ANY_SPEC] * len(deps)
    args += list(deps)
    return pl.pallas_call(
        body, grid=(M // tm, K // tk), in_specs=in_specs, out_specs=tile,
        out_shape=SDS((M, K), F32 if relu is None else BF16),
        name=name, compiler_params=_params("parallel", "parallel"))(*args)


def _row_tile(M, per_row, fixed):
    for tm in sorted((d for d in range(16, M + 1, 16) if M % d == 0), reverse=True):
        if 2 * (tm * per_row + fixed) <= MM_VMEM_BUDGET:
            return tm
    raise ValueError((M, per_row, fixed))


def _mm_nn_norm(a, w3, layer, res, g3, glayer, name, square=False, deps=()):
    M, K = a.shape
    D = w3.shape[2]
    tm = _row_tile(M, 2 * K + 10 * D, 2 * K * D)

    def body(a_ref, w_ref, r_ref, g_ref, *rest):
        h_ref, n_ref = rest[-2:]
        lhs = a_ref[...]
        x = r_ref[...] + jnp.dot(lhs * lhs if square else lhs, w_ref[...], preferred_element_type=F32)
        h_ref[...] = x
        r = lax.rsqrt(jnp.mean(x * x, axis=-1, keepdims=True) + EPS)
        n_ref[...] = ((x * r) * g_ref[...]).astype(BF16)

    row = pl.BlockSpec((tm, D), lambda i: (i, 0))
    return pl.pallas_call(
        body, grid=(M // tm,),
        in_specs=[pl.BlockSpec((tm, K), lambda i: (i, 0)), pl.BlockSpec((None, K, D), lambda i: (layer, 0, 0)), row,
                  pl.BlockSpec((None, 1, D), lambda i: (glayer, 0, 0))] + [ANY_SPEC] * len(deps),
        out_specs=[row, row], out_shape=[SDS((M, D), F32), SDS((M, D), BF16)],
        name=name, compiler_params=_params("parallel"))(a, w3, res, g3, *deps)


def _tail_fwd(y, w_out, res, mlp_g3, layer, w1, w2, next_g3, name, deps=()):
    M, K = y.shape
    D = w_out.shape[2]
    F = w1.shape[2]
    hb = _ctile(F)
    more = next_g3 is not None
    tm = _row_tile(M, 2 * K + 18 * D + (2 * D if more else 0) + 2 * F, 2 * K * D + 2 * D * F)

    def body(y_ref, wo_ref, res_ref, g_ref, w1_ref, w2_ref, *rest):
        outs = rest[-5:] if more else rest[-4:]
        h1 = res_ref[...] + jnp.dot(y_ref[...], wo_ref[...], preferred_element_type=F32)
        outs[0][...] = h1
        n2 = ((h1 * lax.rsqrt(jnp.mean(h1 * h1, axis=-1, keepdims=True) + EPS)) * g_ref[...]).astype(BF16)
        outs[1][...] = n2
        acc = h1
        for jb in range(F // hb):
            cols = slice(jb * hb, (jb + 1) * hb)
            r = jnp.maximum(jnp.dot(n2, w1_ref[:, cols], preferred_element_type=F32), 0.0).astype(BF16)
            outs[-1][:, cols] = r
            acc = acc + jnp.dot(r * r, w2_ref[cols, :], preferred_element_type=F32)
        outs[2][...] = acc
        if more:
            outs[3][...] = ((acc * lax.rsqrt(jnp.mean(acc * acc, axis=-1, keepdims=True) + EPS)) * rest[0][...]).astype(BF16)

    row = pl.BlockSpec((tm, D), lambda i: (i, 0))
    once = dict(pipeline_mode=pl.Buffered(1))
    in_specs = [pl.BlockSpec((tm, K), lambda i: (i, 0)), pl.BlockSpec((None, K, D), lambda i: (0, 0, 0), **once), row,
                pl.BlockSpec((None, 1, D), lambda i: (layer, 0, 0)),
                pl.BlockSpec((None, D, F), lambda i: (0, 0, 0), **once), pl.BlockSpec((None, F, D), lambda i: (0, 0, 0), **once)]
    args = [y, w_out, res, mlp_g3, w1, w2]
    out_specs, out_shape = [row, row, row], [SDS((M, D), F32), SDS((M, D), BF16), SDS((M, D), F32)]
    if more:
        in_specs.append(pl.BlockSpec((None, 1, D), lambda i: (layer + 1, 0, 0)))
        args.append(next_g3)
        out_specs.append(row)
        out_shape.append(SDS((M, D), BF16))
    out_specs.append(pl.BlockSpec((tm, F), lambda i: (i, 0)))
    out_shape.append(SDS((M, F), BF16))
    in_specs += [ANY_SPEC] * len(deps)
    args += list(deps)
    return pl.pallas_call(
        body, grid=(M // tm,), in_specs=in_specs, out_specs=out_specs, out_shape=out_shape,
        name=name, compiler_params=_params("parallel"))(*args)


def _mlp_bwd(dhb, relu, w1, w2, h, g3, glayer, dh_in, name, deps=()):
    M, D = dhb.shape
    F = w1.shape[2]
    hb = _ctile(F)
    tm = _row_tile(M, 16 * D + 4 * F, 2 * D * F)

    def body(dy_ref, r_ref, w1_ref, w2_ref, h_ref, g_ref, dhi_ref, *rest):
        dz_ref, dh_ref, dhb_ref, dg_ref = rest[-4:]
        dy = dy_ref[...]
        dn = jnp.zeros((tm, D), F32)
        for jb in range(F // hb):
            cols = slice(jb * hb, (jb + 1) * hb)
            dact = lax.dot_general(dy, w2_ref[cols, :], (((1,), (1,)), ((), ())), preferred_element_type=F32)
            dz = (dact * (2.0 * r_ref[:, cols].astype(F32))).astype(BF16)
            dz_ref[:, cols] = dz
            dn = dn + lax.dot_general(dz, w1_ref[:, cols], (((1,), (1,)), ((), ())), preferred_element_type=F32)
        x = h_ref[...]
        r = lax.rsqrt(jnp.mean(x * x, axis=-1, keepdims=True) + EPS)
        xh = x * r
        dxh = dn * g_ref[...]
        dh = dhi_ref[...] + r * (dxh - xh * jnp.mean(dxh * xh, axis=-1, keepdims=True))
        dh_ref[...] = dh
        dhb_ref[...] = dh.astype(BF16)

        @pl.when(pl.program_id(0) == 0)
        def _():
            dg_ref[...] = jnp.zeros_like(dg_ref)

        dg_ref[...] += jnp.sum(dn * xh, axis=0, keepdims=True)

    row = pl.BlockSpec((tm, D), lambda i: (i, 0))
    wide = pl.BlockSpec((tm, F), lambda i: (i, 0))
    once = dict(pipeline_mode=pl.Buffered(1))
    return pl.pallas_call(
        body, grid=(M // tm,),
        in_specs=[row, wide, pl.BlockSpec((None, D, F), lambda i: (0, 0, 0), **once),
                  pl.BlockSpec((None, F, D), lambda i: (0, 0, 0), **once), row,
                  pl.BlockSpec((None, 1, D), lambda i: (glayer, 0, 0)), row] + [ANY_SPEC] * len(deps),
        out_specs=[wide, row, row, pl.BlockSpec((1, D), lambda i: (0, 0))],
        out_shape=[SDS((M, F), BF16), SDS((M, D), F32), SDS((M, D), BF16), SDS((1, D), F32)],
        name=name, compiler_params=_params("arbitrary"))(dhb, relu, w1, w2, h, g3, dh_in, *deps)


def _mm_nt_norm(dy, w3, layer, h, g3, glayer, dh_in, name, deps=()):
    M, N = dy.shape
    D = w3.shape[1]
    tm = _row_tile(M, 2 * N + 14 * D, 2 * N * D)

    def body(dy_ref, w_ref, h_ref, g_ref, dhi_ref, *rest):
        dh_ref, dhb_ref, dg_ref = rest[-3:]
        dn = lax.dot_general(dy_ref[...], w_ref[...], (((1,), (1,)), ((), ())), preferred_element_type=F32)
        x = h_ref[...]
        r = lax.rsqrt(jnp.mean(x * x, axis=-1, keepdims=True) + EPS)
        xh = x * r
        dxh = dn * g_ref[...]
        dh = dhi_ref[...] + r * (dxh - xh * jnp.mean(dxh * xh, axis=-1, keepdims=True))
        dh_ref[...] = dh
        dhb_ref[...] = dh.astype(BF16)

        @pl.when(pl.program_id(0) == 0)
        def _():
            dg_ref[...] = jnp.zeros_like(dg_ref)

        dg_ref[...] += jnp.sum(dn * xh, axis=0, keepdims=True)

    row = pl.BlockSpec((tm, D), lambda i: (i, 0))
    return pl.pallas_call(
        body, grid=(M // tm,),
        in_specs=[pl.BlockSpec((tm, N), lambda i: (i, 0)), pl.BlockSpec((None, D, N), lambda i: (layer, 0, 0)), row,
                  pl.BlockSpec((None, 1, D), lambda i: (glayer, 0, 0)), row] + [ANY_SPEC] * len(deps),
        out_specs=[row, row, pl.BlockSpec((1, D), lambda i: (0, 0))],
        out_shape=[SDS((M, D), F32), SDS((M, D), BF16), SDS((1, D), F32)],
        name=name, compiler_params=_params("arbitrary"))(dy, w3, h, g3, dh_in, *deps)


def _fam_dims(kind, K, N):
    return (K // 2, N // N_CHIPS) if kind == "col" else (K // (2 * N_CHIPS), N)


def _mm_tn(x, dy, kind, name, square=False):
    M, K = x.shape
    N = dy.shape[1]
    nr, nc = _fam_dims(kind, K, N)

    def body(x_ref, dy_ref, o_ref):
        lhs = x_ref[...]
        res = lax.dot_general(lhs * lhs if square else lhs, dy_ref[...], (((0,), (0,)), ((), ())), preferred_element_type=F32)
        o_ref[...] = res.astype(BF16).reshape(o_ref.shape)

    if kind == "col":
        tn = _ctile(nc)
        ct = nc // tn
        grid = (N // tn,)
        in_specs = [pl.BlockSpec((M, K), lambda j: (0, 0)), pl.BlockSpec((M, tn), lambda j: (0, j))]
        out_spec = pl.BlockSpec((2, None, nr, tn), lambda j: (0, j // ct, 0, j % ct))
    else:
        grid = (N_CHIPS,)
        in_specs = [pl.BlockSpec((M, 2 * nr), lambda i: (0, i)), pl.BlockSpec((M, N), lambda i: (0, 0))]
        out_spec = pl.BlockSpec((2, None, nr, N), lambda i: (0, i, 0, 0))
    return pl.pallas_call(
        body, grid=grid, in_specs=in_specs, out_specs=out_spec, out_shape=SDS((2, N_CHIPS, nr, nc), BF16),
        name=name, compiler_params=_params("parallel"))(x, dy)


C_EVEN = 512


def _live(rows, base, total):
    r = _row_ids((rows, 1), base)
    return jnp.logical_and(r >= PAD, r < total).astype(F32)


def _conv_taps(win, w_ref, ls, acc, flip):
    for b in range(8):
        rb = win if b == 0 else pltpu.roll(win, 96 - b, 0)
        for a in range(5):
            o = 8 * a + b
            tap = (30 - o) if flip else (o - 2)
            if 0 <= tap < CONV_WIDTH:
                acc = acc + w_ref[pl.ds(tap, 1), ls] * rb[8 * a:8 * a + CHUNK]
    return acc


def _window_sum(win, levels, forward):
    s = win
    n = win.shape[0]
    for k in range(levels):
        step = 1 << k
        s = s + pltpu.roll(s, (n - step) if forward else step, 0)
    return s


def _pool_count(base, g):
    pos = _row_ids((CHUNK, 1), base) - PAD
    return jnp.clip(pos + 1, 1, POOL_WINDOWS[g]).astype(F32)


def _even_fwd(u, cw3, cb3, lg3, lb3, pw4, pb3, ps3, j, name):
    T = u.shape[0]
    C = C_EVEN
    tm = _tile(T, 320)
    nch = tm // CHUNK
    nblk = T // CHUNK

    def body(u_ref, up_ref, cw_ref, cb_ref, lg_ref, lb_ref, pw_ref, pb_ref, ps_ref, o_ref, yc_ref, a_s, p_s, yc_s):
        row0 = pl.program_id(0) * tm
        up = up_ref[...]
        lp = _live(CHUNK, row0 - CHUNK, T)
        a_s[0:CHUNK, :] = up[:, 0:C] * _sigmoid(up[:, C:2 * C]) * lp
        p_s[0:CHUNK, :] = up[:, 2 * C:3 * C] * lp

        def stage(c, _):
            rs = _mult(c * CHUNK, CHUNK)
            lv = _live(CHUNK, row0 + rs, T)
            a_s[pl.ds(rs + CHUNK, CHUNK), :] = u_ref[pl.ds(rs, CHUNK), 0:C] * _sigmoid(u_ref[pl.ds(rs, CHUNK), C:2 * C]) * lv
            p_s[pl.ds(rs + CHUNK, CHUNK), :] = u_ref[pl.ds(rs, CHUNK), 2 * C:3 * C] * lv
            return 0

        for c in range(nch):
            stage(c, 0)

        def chunk(c, _):
            rs = _mult(c * CHUNK, CHUNK)
            lv = _live(CHUNK, row0 + rs, T)
            for cb in range(4):
                ls = slice(cb * 128, (cb + 1) * 128)
                win = a_s[pl.ds(_mult(rs + 32, 32), 96), ls]
                acc = jnp.broadcast_to(cb_ref[:, ls], (CHUNK, 128))
                yc_s[:, ls] = _conv_taps(win, cw_ref, ls, acc, False)
            y = yc_s[...]
            yc_ref[pl.ds(rs, CHUNK), :] = y
            xc = y - jnp.mean(y, axis=-1, keepdims=True)
            yn = xc * lax.rsqrt(jnp.mean(xc * xc, axis=-1, keepdims=True) + EPS) * lg_ref[...] + lb_ref[...]
            o_ref[pl.ds(rs, CHUNK), 0:C] = (yn * _sigmoid(yn) * lv).astype(BF16)
            for g in range(4):
                ls = slice(g * 128, (g + 1) * 128)
                win = p_s[pl.ds(_mult(rs + 48, 16), 80), ls]
                s = _window_sum(win, g + 1, False)
                d = s[16:80] / _pool_count(row0 + rs, g) - win[16:80]
                yv = jnp.dot(d.astype(BF16), pw_ref[g].astype(BF16), preferred_element_type=F32) + pb_ref[:, ls]
                o_ref[pl.ds(rs, CHUNK), C + g * 128:C + (g + 1) * 128] = (yv * ps_ref[:, ls] * lv).astype(BF16)
            return 0

        for c in range(nch):
            chunk(c, 0)

    vec = pl.BlockSpec((None, 1, C), lambda i: (j, 0, 0))
    return pl.pallas_call(
        body, grid=(T // tm,),
        in_specs=[pl.BlockSpec((tm, 3 * C), lambda i: (i, 0)),
                  pl.BlockSpec((CHUNK, 3 * C), lambda i: (jnp.maximum(i * nch - 1, 0), 0)),
                  pl.BlockSpec((None, CONV_ROWS, C), lambda i: (j, 0, 0)), vec, vec, vec,
                  pl.BlockSpec((None, 4, 128, 128), lambda i: (j, 0, 0, 0)), vec, vec],
        out_specs=[pl.BlockSpec((tm, 2 * C), lambda i: (i, 0)), pl.BlockSpec((tm, C), lambda i: (i, 0))],
        out_shape=[SDS((T, 2 * C), BF16), SDS((T, C), F32)],
        scratch_shapes=[pltpu.VMEM((tm + CHUNK, C), F32), pltpu.VMEM((tm + CHUNK, C), F32), pltpu.VMEM((CHUNK, C), F32)],
        name=name, compiler_params=_params("parallel"))(u, u, cw3, cb3, lg3, lb3, pw4, pb3, ps3)


def _even_bwd(u, yc, dy, cw3, cb3, lg3, lb3, pw4, pb3, ps3, j, name):
    T = u.shape[0]
    C = C_EVEN
    tm = _tile(T, 320)
    nch = tm // CHUNK
    nblk = T // CHUNK
    ntile = T // tm

    def body(u_ref, up_ref, un_ref, yc_ref, ycn_ref, dy_ref, dyn_ref, cw_ref, cb_ref, lg_ref, lb_ref, pw_ref, pb_ref, ps_ref,
             du_ref, dcw_ref, dcb_ref, dlg_ref, dlb_ref, dpw_ref, dpb_ref, dps_ref,
             a_s, p_s, dy_s, dyc_s, dd_s, ddc_s, dw_s):
        i = pl.program_id(0)
        row0 = i * tm

        @pl.when(i == 0)
        def _():
            for ref in (dcb_ref, dlg_ref, dlb_ref, dpw_ref, dpb_ref, dps_ref, dw_s):
                ref[...] = jnp.zeros_like(ref)

        up = up_ref[...]
        lp = _live(CHUNK, row0 - CHUNK, T)
        a_s[0:CHUNK, :] = up[:, 0:C] * _sigmoid(up[:, C:2 * C]) * lp
        p_s[0:CHUNK, :] = up[:, 2 * C:3 * C] * lp
        ln_ = _live(CHUNK, row0 + tm, T)
        p_s[tm + CHUNK:tm + 2 * CHUNK, :] = un_ref[:, 2 * C:3 * C] * ln_
        dy_s[tm:tm + CHUNK, :] = dyn_ref[...] * ln_
        dyc_s[tm + CHUNK:tm + CHUNK + 32, :] = jnp.zeros((32, C), F32)

        def stage(c, _):
            rs = _mult(c * CHUNK, CHUNK)
            lv = _live(CHUNK, row0 + rs, T)
            a_s[pl.ds(rs + CHUNK, CHUNK), :] = u_ref[pl.ds(rs, CHUNK), 0:C] * _sigmoid(u_ref[pl.ds(rs, CHUNK), C:2 * C]) * lv
            p_s[pl.ds(rs + CHUNK, CHUNK), :] = u_ref[pl.ds(rs, CHUNK), 2 * C:3 * C] * lv
            dy_s[pl.ds(rs, CHUNK), :] = dy_ref[pl.ds(rs, CHUNK), :] * lv
            return 0

        for c in range(nch):
            stage(c, 0)

        def first(rs, y, own):
            xc = y - jnp.mean(y, axis=-1, keepdims=True)
            rstd = lax.rsqrt(jnp.mean(xc * xc, axis=-1, keepdims=True) + EPS)
            xh = xc * rstd
            yn = xh * lg_ref[...] + lb_ref[...]
            sg = _sigmoid(yn)
            dyn = dy_s[pl.ds(rs, CHUNK), 0:C] * (sg * (1.0 + yn * (1.0 - sg)))
            dlg_ref[...] += jnp.sum(dyn * xh, axis=0, keepdims=True) * own
            dlb_ref[...] += jnp.sum(dyn, axis=0, keepdims=True) * own
            dxh = dyn * lg_ref[...]
            dyc = rstd * (dxh - jnp.mean(dxh, axis=-1, keepdims=True) - xh * jnp.mean(dxh * xh, axis=-1, keepdims=True))
            dyc_s[pl.ds(rs, CHUNK), :] = dyc
            dcb_ref[...] += jnp.sum(dyc, axis=0, keepdims=True) * own
            for g in range(4):
                ls = slice(g * 128, (g + 1) * 128)
                win = p_s[pl.ds(rs + 48, 80), ls]
                s = _window_sum(win, g + 1, False)
                cnt = _pool_count(row0 + rs, g)
                d = (s[16:80] / cnt - win[16:80]).astype(BF16)
                w = pw_ref[g].astype(BF16)
                pre = jnp.dot(d, w, preferred_element_type=F32) + pb_ref[:, ls]
                dyb = dy_s[pl.ds(rs, CHUNK), C + g * 128:C + (g + 1) * 128]
                dpre = dyb * ps_ref[:, ls]
                dps_ref[:, ls] += jnp.sum(dyb * pre, axis=0, keepdims=True) * own
                dpb_ref[:, ls] += jnp.sum(dpre, axis=0, keepdims=True) * own
                dpre_b = (dpre * own).astype(BF16)
                dpw_ref[g] += lax.dot_general(d, dpre_b, (((0,), (0,)), ((), ())), preferred_element_type=F32)
                dd = lax.dot_general(dpre.astype(BF16), w, (((1,), (1,)), ((), ())), preferred_element_type=F32)
                dd_s[pl.ds(rs, CHUNK), ls] = dd
                ddc_s[pl.ds(rs, CHUNK), ls] = dd / cnt

        def first_in_tile(c, _):
            rs = _mult(c * CHUNK, CHUNK)
            first(rs, yc_ref[pl.ds(rs, CHUNK), :], 1.0)
            return 0

        for c in range(nch):
            first_in_tile(c, 0)
        first(tm, ycn_ref[...], 0.0)
        ddc_s[tm + CHUNK:tm + CHUNK + 16, :] = jnp.zeros((16, C), F32)

        def second(c, _):
            rs = _mult(c * CHUNK, CHUNK)
            lv = _live(CHUNK, row0 + rs, T)
            for cb in range(4):
                ls = slice(cb * 128, (cb + 1) * 128)
                wd = dyc_s[pl.ds(rs, 96), ls]
                da = _conv_taps(wd, cw_ref, ls, jnp.zeros((CHUNK, 128), F32), True)
                wa = a_s[pl.ds(_mult(rs + 32, 32), 96), ls]
                dyc = dyc_s[pl.ds(rs, CHUNK), ls]
                for b in range(8):
                    rb = wa if b == 0 else pltpu.roll(wa, 96 - b, 0)
                    for a in range(5):
                        tap = 8 * a + b - 2
                        if 0 <= tap < CONV_WIDTH:
                            prod = dyc * rb[8 * a:8 * a + CHUNK]
                            part = prod[0:8]
                            for q in range(1, 8):
                                part = part + prod[8 * q:8 * q + 8]
                            dw_s[8 * tap:8 * tap + 8, ls] += part
                val = u_ref[pl.ds(rs, CHUNK), ls]
                sg = _sigmoid(u_ref[pl.ds(rs, CHUNK), C + cb * 128:C + (cb + 1) * 128])
                du_ref[pl.ds(rs, CHUNK), ls] = (da * sg * lv).astype(BF16)
                du_ref[pl.ds(rs, CHUNK), C + cb * 128:C + (cb + 1) * 128] = (da * val * sg * (1.0 - sg) * lv).astype(BF16)
            for g in range(4):
                ls = slice(g * 128, (g + 1) * 128)
                z = _window_sum(ddc_s[pl.ds(rs, 80), ls], g + 1, True)
                dpin = (z[0:CHUNK] - dd_s[pl.ds(rs, CHUNK), ls]) * lv
                du_ref[pl.ds(rs, CHUNK), 2 * C + g * 128:2 * C + (g + 1) * 128] = dpin.astype(BF16)
            return 0

        for c in range(nch):
            second(c, 0)

        @pl.when(i == ntile - 1)
        def _():
            for tap in range(CONV_WIDTH):
                dcw_ref[tap:tap + 1, :] = jnp.sum(dw_s[8 * tap:8 * tap + 8, :], axis=0, keepdims=True)
            dcw_ref[CONV_WIDTH:CONV_ROWS, :] = jnp.zeros((CONV_ROWS - CONV_WIDTH, C), F32)

    vec = pl.BlockSpec((None, 1, C), lambda i: (j, 0, 0))
    ovec = pl.BlockSpec((1, C), lambda i: (0, 0))
    return pl.pallas_call(
        body, grid=(ntile,),
        in_specs=[pl.BlockSpec((tm, 3 * C), lambda i: (i, 0)),
                  pl.BlockSpec((CHUNK, 3 * C), lambda i: (jnp.maximum(i * nch - 1, 0), 0)),
                  pl.BlockSpec((CHUNK, 3 * C), lambda i: (jnp.minimum((i + 1) * nch, nblk - 1), 0)),
                  pl.BlockSpec((tm, C), lambda i: (i, 0)),
                  pl.BlockSpec((CHUNK, C), lambda i: (jnp.minimum((i + 1) * nch, nblk - 1), 0)),
                  pl.BlockSpec((tm, 2 * C), lambda i: (i, 0)),
                  pl.BlockSpec((CHUNK, 2 * C), lambda i: (jnp.minimum((i + 1) * nch, nblk - 1), 0)),
                  pl.BlockSpec((None, CONV_ROWS, C), lambda i: (j, 0, 0)), vec, vec, vec,
                  pl.BlockSpec((None, 4, 128, 128), lambda i: (j, 0, 0, 0)), vec, vec],
        out_specs=[pl.BlockSpec((tm, 3 * C), lambda i: (i, 0)), pl.BlockSpec((CONV_ROWS, C), lambda i: (0, 0)),
                   ovec, ovec, ovec, pl.BlockSpec((4, 128, 128), lambda i: (0, 0, 0)), ovec, ovec],
        out_shape=[SDS((T, 3 * C), BF16), SDS((CONV_ROWS, C), F32), SDS((1, C), F32), SDS((1, C), F32), SDS((1, C), F32),
                   SDS((4, 128, 128), F32), SDS((1, C), F32), SDS((1, C), F32)],
        scratch_shapes=[pltpu.VMEM((tm + CHUNK, C), F32), pltpu.VMEM((tm + 2 * CHUNK, C), F32),
                        pltpu.VMEM((tm + CHUNK, 2 * C), F32),
                        pltpu.VMEM((tm + CHUNK + 32, C), F32), pltpu.VMEM((tm + CHUNK, C), F32),
                        pltpu.VMEM((tm + CHUNK + 16, C), F32), pltpu.VMEM((8 * CONV_ROWS, C), F32)],
        name=name, compiler_params=_params("arbitrary"))(u, u, u, yc, yc, dy, dy, cw3, cb3, lg3, lb3, pw4, pb3, ps3)


HI = lax.Precision.HIGHEST


def _dot_nt(a, b):
    return lax.dot_general(a, b, (((1,), (1,)), ((), ())), preferred_element_type=F32)


def _dot_tn(a, b):
    return lax.dot_general(a, b, (((0,), (0,)), ((), ())), preferred_element_type=F32)


def _tri(lower):
    r = lax.broadcasted_iota(jnp.int32, (CHUNK, CHUNK), 0)
    c = lax.broadcasted_iota(jnp.int32, (CHUNK, CHUNK), 1)
    return jnp.where((c <= r) if lower else (c >= r), 1.0, 0.0).astype(F32)


def _hgrn_gates(u_ref, lb_ref, h, D, lv):
    ls = slice(h * HEAD_DIM, (h + 1) * HEAD_DIM)
    qraw = u_ref[:, ls]
    fraw = u_ref[:, D + h * HEAD_DIM:D + (h + 1) * HEAD_DIM]
    v = u_ref[:, 2 * D + h * HEAD_DIM:2 * D + (h + 1) * HEAD_DIM] * lv
    lbv = lb_ref[:, ls]
    sig = _sigmoid(fraw)
    forget = lbv + (1.0 - lbv) * sig
    logf = jnp.log(forget) * lv
    k = (1.0 - forget) * lv
    qsig = _sigmoid(qraw)
    q = qraw * qsig * lv
    return q, k, v, logf, (qraw, qsig, sig, forget, lbv)


def _sub_parts(q, k, b, b_s, I):
    rows = slice(SUB * I, SUB * (I + 1))
    rho = jnp.zeros((1, HEAD_DIM), F32) if I == 0 else b_s[SUB * I - 1:SUB * I, :]
    eI = jnp.exp(b[rows] - rho)
    EI = jnp.exp(jnp.minimum(rho - b, EXP_CAP))
    causal = (lax.broadcasted_iota(jnp.int32, (SUB, CHUNK), 1)
              <= lax.broadcasted_iota(jnp.int32, (SUB, CHUNK), 0) + SUB * I)
    return rows, q[rows] * eI, k * EI, eI, EI, causal


def _chunks_per_step(NC):
    for n in (5, 4, 3, 2):
        if NC % n == 0:
            return n
    return 1


def _hgrn_fwd(u, lb3, layer, gn3, j, name):
    T = u.shape[0]
    D = u.shape[1] // 4
    H = D // HEAD_DIM
    NC = T // CHUNK
    CH = _chunks_per_step(NC)
    R = CH * CHUNK

    def body(u_ref, lb_ref, gn_ref, y_ref, o_ref, sall_ref, st_s, b_s, lf_s, q_s, k_s):
        n = pl.program_id(0)

        @pl.when(n == 0)
        def _():
            st_s[...] = jnp.zeros_like(st_s)

        heads = range(H)
        cols = [slice(h * HEAD_DIM, (h + 1) * HEAD_DIM) for h in heads]
        rows = [slice(c * CHUNK, (c + 1) * CHUNK) for c in range(CH)]
        vb = {}
        for c in range(CH):
            lv = _live(CHUNK, (n * CH + c) * CHUNK, T)
            for h in heads:
                q, k, v, logf, _ = _hgrn_gates(u_ref.at[rows[c]], lb_ref, h, D, lv)
                q_s[rows[c], cols[h]] = q
                k_s[rows[c], cols[h]] = k
                lf_s[rows[c], cols[h]] = logf
                vb[c, h] = v.astype(BF16)
        for c in range(CH):
            b_s[rows[c], :] = jnp.dot(_tri(True), lf_s[rows[c], :], precision=HI, preferred_element_type=F32)
        ops = {}
        for c in range(CH):
            for h in heads:
                b_h = b_s.at[rows[c], cols[h]]
                b = b_h[...]
                q = q_s[rows[c], cols[h]]
                k = k_s[rows[c], cols[h]]
                blast = b_h[CHUNK - 1:CHUNK, :]
                qh = (q * jnp.exp(b)).astype(BF16)
                kt = (k * jnp.exp(blast - b)).astype(BF16)
                subs = []
                for I in range(CHUNK // SUB):
                    _, qI, KI, _, _, causal = _sub_parts(q, k, b, b_h, I)
                    subs.append((qI.astype(BF16), KI.astype(BF16), causal))
                ops[c, h] = (qh, kt, jnp.exp(blast), subs)
        mm = {}
        for h in heads:
            st = st_s[h]
            for c in range(CH):
                qh, kt, eblast, subs = ops[c, h]
                sall_ref[c, h] = st
                o_inter = _dot_nt(qh, st.astype(BF16))
                st = st * eblast + _dot_tn(vb[c, h], kt)
                mm[c, h] = (o_inter, [_dot_nt(qI, KI) for qI, KI, _ in subs])
            st_s[h] = st
        for c in range(CH):
            for h in heads:
                o_inter, ps = mm[c, h]
                p = jnp.concatenate([jnp.where(m, x, 0.0) for x, (_, _, m) in zip(ps, ops[c, h][3])], axis=0).astype(BF16)
                o = o_inter + jnp.dot(p, vb[c, h], preferred_element_type=F32)
                o_ref[rows[c], cols[h]] = o
                graw = u_ref[rows[c], 3 * D + h * HEAD_DIM:3 * D + (h + 1) * HEAD_DIM]
                r = lax.rsqrt(jnp.mean(o * o, axis=-1, keepdims=True) + EPS)
                y_ref[rows[c], cols[h]] = (((o * r) * gn_ref[...]) * (graw * _sigmoid(graw))).astype(BF16)

    return pl.pallas_call(
        body, grid=(NC // CH,),
        in_specs=[pl.BlockSpec((R, 4 * D), lambda n: (n, 0)),
                  pl.BlockSpec((None, 1, D), lambda n: (layer, 0, 0)),
                  pl.BlockSpec((None, 1, HEAD_DIM), lambda n: (j, 0, 0))],
        out_specs=[pl.BlockSpec((R, D), lambda n: (n, 0)), pl.BlockSpec((R, D), lambda n: (n, 0)),
                   pl.BlockSpec((CH, H, HEAD_DIM, HEAD_DIM), lambda n: (n, 0, 0, 0))],
        out_shape=[SDS((T, D), BF16), SDS((T, D), F32), SDS((NC, H, HEAD_DIM, HEAD_DIM), F32)],
        scratch_shapes=[pltpu.VMEM((H, HEAD_DIM, HEAD_DIM), F32)] + [pltpu.VMEM((R, D), F32)] * 4,
        name=name, compiler_params=_params("arbitrary"))(u, lb3, gn3)


def _hgrn_bwd(u, o_raw, dy, sall, lb3, layer, gn3, j, name):
    T = u.shape[0]
    D = u.shape[1] // 4
    H = D // HEAD_DIM
    NC = T // CHUNK
    CH = _chunks_per_step(NC)
    R = CH * CHUNK
    NS = NC // CH

    def body(u_ref, o_ref, dy_ref, sall_ref, lb_ref, gn_ref, du_ref, dlb_ref, dgn_ref, dst_s, b_s, lf_s, q_s, k_s, db_s, dk_s):
        step = pl.program_id(0)
        n = NS - 1 - step

        @pl.when(step == 0)
        def _():
            dst_s[...] = jnp.zeros_like(dst_s)
            dlb_ref[...] = jnp.zeros_like(dlb_ref)
            dgn_ref[...] = jnp.zeros_like(dgn_ref)

        last_row = (_row_ids((CHUNK, 1), 0) == CHUNK - 1).astype(F32)
        gn = gn_ref[...]
        heads = range(H)
        chunks = range(CH)
        cols = [slice(h * HEAD_DIM, (h + 1) * HEAD_DIM) for h in heads]
        rows = [slice(c * CHUNK, (c + 1) * CHUNK) for c in chunks]
        lv = [_live(CHUNK, (n * CH + c) * CHUNK, T) for c in chunks]
        vb, dob = {}, {}
        dgn = jnp.zeros((1, HEAD_DIM), F32)
        for c in chunks:
            for h in heads:
                q, k, v, logf, _ = _hgrn_gates(u_ref.at[rows[c]], lb_ref, h, D, lv[c])
                q_s[rows[c], cols[h]] = q
                k_s[rows[c], cols[h]] = k
                lf_s[rows[c], cols[h]] = logf
                vb[c, h] = v.astype(BF16)
                graw = u_ref[rows[c], 3 * D + h * HEAD_DIM:3 * D + (h + 1) * HEAD_DIM]
                gsig = _sigmoid(graw)
                o = o_ref[rows[c], cols[h]]
                r = lax.rsqrt(jnp.mean(o * o, axis=-1, keepdims=True) + EPS)
                xh = o * r
                dyv = dy_ref[rows[c], cols[h]]
                dsg = dyv * (graw * gsig)
                dgn = dgn + jnp.sum(dsg * xh, axis=0, keepdims=True)
                dxh = dsg * gn
                do = r * (dxh - xh * jnp.mean(dxh * xh, axis=-1, keepdims=True))
                dob[c, h] = do.astype(BF16)
                dgraw = dyv * xh * gn * (gsig * (1.0 + graw * (1.0 - gsig)))
                du_ref[rows[c], 3 * D + h * HEAD_DIM:3 * D + (h + 1) * HEAD_DIM] = (dgraw * lv[c]).astype(BF16)
        dgn_ref[...] += dgn
        for c in chunks:
            b_s[rows[c], :] = jnp.dot(_tri(True), lf_s[rows[c], :], precision=HI, preferred_element_type=F32)
        ops = {}
        for c in chunks:
            for h in heads:
                b_h = b_s.at[rows[c], cols[h]]
                b = b_h[...]
                q = q_s[rows[c], cols[h]]
                k = k_s[rows[c], cols[h]]
                blast = b_h[CHUNK - 1:CHUNK, :]
                eb = jnp.exp(b)
                ekb = jnp.exp(blast - b)
                subs = []
                for I in range(CHUNK // SUB):
                    rws, qI, KI, eI, EI, causal = _sub_parts(q, k, b, b_h, I)
                    subs.append((rws, qI.astype(BF16), KI.astype(BF16), eI, EI, causal))
                ops[c, h] = (eb, ekb, jnp.exp(blast), (q * eb).astype(BF16), (k * ekb).astype(BF16), subs)
        mm = {}
        for h in heads:
            dst = dst_s[h]
            for c in reversed(chunks):
                eb, ekb, eblast, qhb, ktb, subs = ops[c, h]
                st = sall_ref[c, h]
                dstb = dst.astype(BF16)
                dv = _dot_nt(ktb, dstb)
                dqh = jnp.dot(dob[c, h], st.astype(BF16), preferred_element_type=F32)
                dkt = jnp.dot(vb[c, h], dstb, preferred_element_type=F32)
                dblast = jnp.sum(dst * st, axis=0, keepdims=True) * eblast
                dst = dst * eblast + _dot_tn(dob[c, h], qhb)
                dp_full = _dot_nt(dob[c, h], vb[c, h])
                ps = [_dot_nt(qIb, KIb) for _, qIb, KIb, _, _, _ in subs]
                mm[c, h] = (dv, dqh, dkt, dblast, dp_full, ps)
            dst_s[h] = dst
        for c in chunks:
            for h in heads:
                eb, ekb, eblast, qhb, ktb, subs = ops[c, h]
                dv, dqh, dkt, dblast, dp_full, ps = mm[c, h]
                p = jnp.concatenate([jnp.where(sub[5], x, 0.0) for x, sub in zip(ps, subs)], axis=0).astype(BF16)
                dv = dv + _dot_tn(p, dob[c, h])
                du_ref[rows[c], 2 * D + h * HEAD_DIM:2 * D + (h + 1) * HEAD_DIM] = (dv * lv[c]).astype(BF16)
                dq = dqh * eb
                db = dqh * qhb.astype(F32)
                tmp = dkt * ktb.astype(F32)
                dk = dkt * ekb
                db = db - tmp
                dblast = dblast + jnp.sum(tmp, axis=0, keepdims=True)
                dq_parts, db_parts = [], []
                for rws, qIb, KIb, eI, EI, causal in subs:
                    dp = jnp.where(causal, dp_full[rws], 0.0).astype(BF16)
                    dqI = jnp.dot(dp, KIb, preferred_element_type=F32)
                    dKI = _dot_tn(dp, qIb)
                    dq_parts.append(dqI * eI)
                    db_parts.append(dqI * qIb.astype(F32))
                    dk = dk + dKI * EI
                    db = db - dKI * KIb.astype(F32)
                dq = dq + jnp.concatenate(dq_parts, axis=0)
                db_s[rows[c], cols[h]] = db + jnp.concatenate(db_parts, axis=0) + last_row * dblast
                dk_s[rows[c], cols[h]] = dk
                qraw = u_ref[rows[c], cols[h]]
                qsig = _sigmoid(qraw)
                du_ref[rows[c], cols[h]] = (dq * (qsig * (1.0 + qraw * (1.0 - qsig))) * lv[c]).astype(BF16)
        for c in chunks:
            lf_s[rows[c], :] = jnp.dot(_tri(False), db_s[rows[c], :], precision=HI, preferred_element_type=F32)
        for h in heads:
            lbv = lb_ref[:, cols[h]]
            dlb = jnp.zeros((1, HEAD_DIM), F32)
            for c in chunks:
                fraw = u_ref[rows[c], D + h * HEAD_DIM:D + (h + 1) * HEAD_DIM]
                sig = _sigmoid(fraw)
                forget = lbv + (1.0 - lbv) * sig
                dforget = (lf_s[rows[c], cols[h]] / forget - dk_s[rows[c], cols[h]]) * lv[c]
                dlb = dlb + jnp.sum(dforget * (1.0 - sig), axis=0, keepdims=True)
                du_ref[rows[c], D + h * HEAD_DIM:D + (h + 1) * HEAD_DIM] = (dforget * (1.0 - lbv) * sig * (1.0 - sig)).astype(BF16)
            dlb_ref[:, cols[h]] += dlb

    rev = lambda s: (NS - 1 - s, 0)
    return pl.pallas_call(
        body, grid=(NS,),
        in_specs=[pl.BlockSpec((R, 4 * D), rev), pl.BlockSpec((R, D), rev), pl.BlockSpec((R, D), rev),
                  pl.BlockSpec((CH, H, HEAD_DIM, HEAD_DIM), lambda s: (NS - 1 - s, 0, 0, 0)),
                  pl.BlockSpec((None, 1, D), lambda s: (layer, 0, 0)),
                  pl.BlockSpec((None, 1, HEAD_DIM), lambda s: (j, 0, 0))],
        out_specs=[pl.BlockSpec((R, 4 * D), rev), pl.BlockSpec((1, D), lambda s: (0, 0)),
                   pl.BlockSpec((1, HEAD_DIM), lambda s: (0, 0))],
        out_shape=[SDS((T, 4 * D), BF16), SDS((1, D), F32), SDS((1, HEAD_DIM), F32)],
        scratch_shapes=[pltpu.VMEM((H, HEAD_DIM, HEAD_DIM), F32)] + [pltpu.VMEM((R, D), F32)] * 6,
        name=name, compiler_params=_params("arbitrary"))(u, o_raw, dy, sall, lb3, gn3)


def _softmax_layers(p_ref, n_layers):
    rows = [p_ref[l:l + 1, :] for l in range(n_layers)]
    m = functools.reduce(jnp.maximum, rows)
    e = [jnp.exp(x - m) for x in rows]
    tot = functools.reduce(lambda a, b: a + b, e)
    return [x / tot for x in e]


def _lb_fwd(p):
    n_layers, D = p.shape

    def body(p_ref, o_ref):
        s = _softmax_layers(p_ref, n_layers)
        acc = jnp.zeros((1, D), F32)
        o_ref[0:1, :] = acc
        for l in range(1, n_layers):
            acc = acc + s[l]
            o_ref[l:l + 1, :] = acc

    return pl.pallas_call(body, out_shape=SDS(p.shape, F32), name="lb_fwd")(p)


def _lb_bwd(p, dlb):
    n_layers, D = p.shape

    def body(p_ref, d_ref, o_ref):
        s = _softmax_layers(p_ref, n_layers)
        ds = [jnp.zeros((1, D), F32)] * n_layers
        acc = jnp.zeros((1, D), F32)
        for l in range(n_layers - 1, 0, -1):
            acc = acc + d_ref[l:l + 1, :]
            ds[l] = acc
        dot = functools.reduce(lambda a, b: a + b, [s[l] * ds[l] for l in range(n_layers)])
        for l in range(n_layers):
            o_ref[l:l + 1, :] = s[l] * (ds[l] - dot)

    return pl.pallas_call(body, out_shape=SDS(p.shape, F32), name="lb_bwd")(p, dlb)


def _adamw(w, g, m, v, name):
    R, C = w.shape
    tr = _tile(R, 256, 8) if R % 8 == 0 else R

    def body(w_ref, g_ref, m_ref, v_ref, d_ref, mo_ref, vo_ref):
        g_ = g_ref[...]
        m_ = ADAM_B1 * m_ref[...] + (1.0 - ADAM_B1) * g_
        v_ = ADAM_B2 * v_ref[...] + (1.0 - ADAM_B2) * (g_ * g_)
        mh = m_ / (1.0 - ADAM_B1 ** ADAM_STEP)
        vh = v_ / (1.0 - ADAM_B2 ** ADAM_STEP)
        d_ref[...] = -ADAM_LR * (mh / (jnp.sqrt(vh) + ADAM_EPS) + ADAM_WD * w_ref[...])
        mo_ref[...] = m_
        vo_ref[...] = v_

    blk = pl.BlockSpec((tr, C), lambda i: (i, 0))
    return pl.pallas_call(
        body, grid=(R // tr,), in_specs=[blk] * 4, out_specs=[blk] * 3, out_shape=[SDS((R, C), F32)] * 3,
        name=name, compiler_params=_params("parallel"))(w, g, m, v)


def _adamw_layer(w3, m3, v3, g2, layer, outs, name):
    L, R, C = w3.shape
    tr = _tile(R, 256, 8)
    if outs is None:
        outs = tuple(lax.empty(w3.shape, F32) for _ in range(4))

    def body(w_ref, m_ref, v_ref, g_ref, a0, a1, a2, a3, go_ref, d_ref, mo_ref, vo_ref):
        del a0, a1, a2, a3
        g_ = g_ref[...]
        m_ = ADAM_B1 * m_ref[...] + (1.0 - ADAM_B1) * g_
        v_ = ADAM_B2 * v_ref[...] + (1.0 - ADAM_B2) * (g_ * g_)
        mh = m_ / (1.0 - ADAM_B1 ** ADAM_STEP)
        vh = v_ / (1.0 - ADAM_B2 ** ADAM_STEP)
        go_ref[...] = g_
        d_ref[...] = -ADAM_LR * (mh / (jnp.sqrt(vh) + ADAM_EPS) + ADAM_WD * w_ref[...])
        mo_ref[...] = m_
        vo_ref[...] = v_

    lay = pl.BlockSpec((None, tr, C), lambda i: (layer, i, 0))
    return pl.pallas_call(
        body, grid=(R // tr,), in_specs=[lay] * 3 + [pl.BlockSpec((tr, C), lambda i: (i, 0))] + [ANY_SPEC] * 4,
        out_specs=[lay] * 4, out_shape=[SDS(w3.shape, F32)] * 4, input_output_aliases={4: 0, 5: 1, 6: 2, 7: 3},
        name=name, compiler_params=_params("parallel"))(w3, m3, v3, g2, *outs)


SEM_SPEC = pl.BlockSpec(memory_space=pltpu.SEMAPHORE)
HBM_SPEC = pl.BlockSpec(memory_space=pltpu.HBM)
EFFECT = pltpu.SideEffectType.DATAFLOW_SIDE_EFFECTING
N_DEV = 2 * N_CHIPS


def _position():
    x, y, c = lax.axis_index("x"), lax.axis_index("y"), lax.axis_index("c")
    chips = [(1 - x, y), (x, 1 - y), (1 - x, 1 - y)]
    return x, y, c, chips


def _split_start(name, plan, bufs, n_sems, deps=(), earlier=None):
    n = len(bufs)
    held = () if earlier is None else tuple(earlier[1:])

    def body(*refs):
        first_out = n + len(held) + len(deps)
        if earlier is not None:
            sends, recvs = earlier[0](refs[:n], refs[n], refs[n + 1])
            for kw in sends:
                pltpu.make_async_remote_copy(**kw).wait_send()
            for kw in recvs:
                pltpu.make_async_remote_copy(**kw).wait_recv()
        sends, _ = plan(refs[:n], refs[first_out], refs[first_out + 1])
        for kw in sends:
            pltpu.make_async_remote_copy(**kw).start()
        refs[-1][...] = jnp.zeros_like(refs[-1])

    out = pl.pallas_call(
        body, name=name,
        out_shape=(pltpu.SemaphoreType.DMA((n_sems,)), pltpu.SemaphoreType.DMA((n_sems,)),
                   *[pltpu.HBM(b.shape, b.dtype) for b in bufs], SDS((8, 128), F32)),
        in_specs=[HBM_SPEC] * n + [SEM_SPEC] * len(held) + [ANY_SPEC] * len(deps),
        out_specs=(SEM_SPEC, SEM_SPEC, *[HBM_SPEC] * n, pl.BlockSpec(memory_space=pltpu.VMEM)),
        input_output_aliases={i: 2 + i for i in range(n)},
        compiler_params=pltpu.CompilerParams(has_side_effects=EFFECT),
    )(*[pltpu.with_memory_space_constraint(b, pltpu.HBM) for b in bufs], *held, *deps)
    return out[0], out[1], list(out[2:2 + n]), out[-1]


def _split_wait(name, plan, send_sems, recv_sems, bufs, after=()):
    n = len(bufs)

    def body(*refs):
        sends, recvs = plan(refs[:n], refs[n], refs[n + 1])
        for kw in sends:
            pltpu.make_async_remote_copy(**kw).wait_send()
        for kw in recvs:
            pltpu.make_async_remote_copy(**kw).wait_recv()

    out = pl.pallas_call(
        body, name=name, out_shape=tuple(pltpu.HBM(b.shape, b.dtype) for b in bufs),
        in_specs=[HBM_SPEC] * n + [SEM_SPEC, SEM_SPEC] + [ANY_SPEC] * len(after),
        out_specs=tuple([HBM_SPEC] * n), input_output_aliases={i: i for i in range(n)},
        compiler_params=pltpu.CompilerParams(has_side_effects=EFFECT),
    )(*bufs, send_sems, recv_sems, *after)
    return list(out)


def _region(kind, ref, chip, half):
    K, N = ref.shape
    if kind == "col":
        return ref.at[pl.ds(half * (K // 2), K // 2), pl.ds(chip * (N // N_CHIPS), N // N_CHIPS)]
    rows = K // (2 * N_CHIPS)
    return ref.at[pl.ds((2 * chip + half) * rows, rows), :]


def _gather_plan(kinds, over_chips, first=0):
    def plan(refs, send_sems, recv_sems):
        x, y, c, chips = _position()
        sends, recvs = [], []
        for f, (ref, kind) in enumerate(zip(refs, kinds)):
            for k, chip in enumerate(chips):
                theirs = 2 * chip[0] + chip[1]
                at = 3 * (first + f) + k
                sem = dict(send_sem=send_sems.at[at], recv_sem=recv_sems.at[at], device_id_type=MESH)
                if over_chips:
                    out, back, to = _region(kind, ref, 2 * x + y, c), _region(kind, ref, theirs, c), (*chip, c)
                else:
                    out, back, to = _region(kind, ref, theirs, c), _region(kind, ref, theirs, 1 - c), (x, y, 1 - c)
                sends.append(dict(src_ref=out, dst_ref=out, device_id=to, **sem))
                recvs.append(dict(src_ref=back, dst_ref=back, device_id=to, **sem))
        return sends, recvs
    return plan


def _reduce_plan(first=0):
    def plan(refs, send_sems, recv_sems):
        x, y, c, _ = _position()
        me = 4 * x + 2 * y + c
        sends, recvs = [], []
        for f in range(len(refs) // 2):
            acc, land = refs[2 * f], refs[2 * f + 1]
            for d in range(1, N_DEV):
                t = (me + d) % N_DEV
                to = dict(device_id=(t // 4, (t // 2) % 2, t % 2), device_id_type=MESH)
                slot = N_DEV - 1 - d
                at = first + 7 * f
                sends.append(dict(src_ref=acc.at[t % 2, t // 2], dst_ref=land.at[slot], send_sem=send_sems.at[at + d - 1],
                                  recv_sem=recv_sems.at[at + slot], **to))
                recvs.append(dict(src_ref=land.at[d - 1], dst_ref=land.at[d - 1], send_sem=send_sems.at[at + d - 1],
                                  recv_sem=recv_sems.at[at + d - 1], **to))
        return sends, recvs
    return plan


def _swap_plan(first=0):
    def plan(refs, send_sems, recv_sems):
        x, y, c, _ = _position()
        sends, recvs = [], []
        for f, g in enumerate(refs):
            sem = dict(send_sem=send_sems.at[first + f], recv_sem=recv_sems.at[first + f], device_id=(x, y, 1 - c),
                       device_id_type=MESH)
            sends.append(dict(src_ref=g.at[c], dst_ref=g.at[c], **sem))
            recvs.append(dict(src_ref=g.at[1 - c], dst_ref=g.at[1 - c], **sem))
        return sends, recvs
    return plan


def _joined(plans):
    def plan(refs, send_sems, recv_sems):
        sends, recvs, lo = [], [], 0
        for part, n in plans:
            s_, r_ = part(refs[lo:lo + n], send_sems, recv_sems)
            sends += s_
            recvs += r_
            lo += n
        return sends, recvs
    return plan


def _sum_pieces(ids2, acc, land, name):
    _, _, nr, nc = acc.shape
    tr = _tile(nr, 256, 16)

    def body(ids_ref, own_ref, land_ref, o_ref):
        del ids_ref
        s = own_ref[...].astype(F32)
        for k in range(N_DEV - 1):
            s = s + land_ref[k].astype(F32)
        o_ref[...] = s

    return pl.pallas_call(
        body,
        grid_spec=pltpu.PrefetchScalarGridSpec(
            num_scalar_prefetch=1, grid=(nr // tr,),
            in_specs=[pl.BlockSpec((None, None, tr, nc), lambda i, ids: (ids[0], ids[1], i, 0)),
                      pl.BlockSpec((N_DEV - 1, tr, nc), lambda i, ids: (0, i, 0))],
            out_specs=pl.BlockSpec((None, tr, nc), lambda i, ids: (ids[0], i, 0))),
        out_shape=SDS((2, nr, nc), F32), name=name, compiler_params=_params("parallel"))(ids2, acc, land)


def _small_plan(refs, send_sems, recv_sems):
    x, y, c, _ = _position()
    me = 4 * x + 2 * y + c
    own, land = refs
    sends, recvs = [], []
    for d in range(1, N_DEV):
        t = (me + d) % N_DEV
        to = dict(device_id=(t // 4, (t // 2) % 2, t % 2), device_id_type=MESH)
        sends.append(dict(src_ref=own, dst_ref=land.at[me], send_sem=send_sems.at[d - 1],
                          recv_sem=recv_sems.at[N_DEV - 1 - d], **to))
        recvs.append(dict(src_ref=land.at[t], dst_ref=land.at[t], send_sem=send_sems.at[d - 1],
                          recv_sem=recv_sems.at[d - 1], **to))
    return sends, recvs


def _sum_blocks(me1, own, land):
    def body(me_ref, own_ref, land_ref, o_ref):
        acc = None
        for d in range(N_DEV):
            term = jnp.where(me_ref[0] == d, own_ref[...], land_ref[d])
            acc = term if acc is None else acc + term
        o_ref[...] = acc

    return pl.pallas_call(
        body,
        grid_spec=pltpu.PrefetchScalarGridSpec(
            num_scalar_prefetch=1, grid=(1,),
            in_specs=[pl.BlockSpec(own.shape, lambda i, me: (0, 0)), pl.BlockSpec(land.shape, lambda i, me: (0, 0, 0))],
            out_specs=pl.BlockSpec(own.shape, lambda i, me: (0, 0))),
        out_shape=SDS(own.shape, F32), name="sum_small", compiler_params=_params("arbitrary"))(me1, own, land)


BIG = {"ev_w_in": "col", "ev_w_out": "row", "od_w_in": "col", "od_w_out": "row", "mlp_w1": "col", "mlp_w2": "row"}
WEIGHTS = ("meta_tokens", "mix_norm_g", "mlp_norm_g", "final_norm_g", "ev_w_in", "ev_conv_w", "ev_conv_b", "ev_ln_g",
           "ev_ln_b", "ev_pool_w", "ev_pool_b", "ev_pool_scale", "ev_w_out", "od_w_in", "od_gnorm_g", "od_w_out",
           "lb_param", "mlp_w1", "mlp_w2")
PACK_UNIT = 1024


def _mixer_names(layer):
    return ("ev_w_in", "ev_w_out") if layer % 2 == 0 else ("od_w_in", "od_w_out")


def _pack(arrays):
    flat = []
    for a in arrays:
        a = a.reshape(-1)
        flat.append(jnp.pad(a, (0, (-a.shape[0]) % PACK_UNIT)))
    return jnp.concatenate(flat).reshape(-1, 128)


def _unpack(packed, shapes):
    flat = packed.reshape(-1)
    out, off = [], 0
    for s in shapes:
        size = 1
        for d in s:
            size *= d
        out.append(flat[off:off + size].reshape(s))
        off += size + (-size) % PACK_UNIT
    return out


def _local_step(x2, target, P, weights, boundary, first_deps=()):
    D = x2.shape[1]
    n_layers = P["mix_norm_g"].shape[0]
    h = jnp.concatenate([jnp.zeros((PAD, D), F32), P["meta_full"], x2], axis=0)
    mix_g = P["mix_norm_g"].reshape(n_layers, 1, D)
    mlp_g = P["mlp_norm_g"].reshape(n_layers, 1, D)
    vec = lambda a: a.reshape(a.shape[0], 1, -1)
    cb3, lg3, lnb3, ps3 = vec(P["ev_conv_b"]), vec(P["ev_ln_g"]), vec(P["ev_ln_b"]), vec(P["ev_pool_scale"])
    pb3 = vec(P["ev_pool_b"])
    gn3 = vec(P["od_gnorm_g"])
    lb_all = _lb_fwd(P["lb_param"])
    lb3 = lb_all.reshape(n_layers, 1, D)
    even = (cb3, lg3, lnb3, P["ev_pool_w"], pb3, ps3)

    saved = []
    deps = tuple(first_deps)
    for layer in range(n_layers):
        j = layer // 2
        w_in, w_out = _mixer_names(layer)
        W = {}
        s = {"h": h, "W": W}
        s["n"] = _rms_fwd(h, mix_g, layer, "mix_norm_0", deps=deps) if layer == 0 else n_next
        deps = ()
        W[w_in], held = weights(layer, w_in, (s["n"],))
        s["u"] = _mm_nn(s["n"], W[w_in], 0, f"mix_in_{layer}", deps=held)
        if layer % 2 == 0:
            s["y"], s["yc"] = _even_fwd(s["u"], P["conv_w_full"], *even, j, f"even_fwd_{layer}")
        else:
            s["y"], s["o"], s["sall"] = _hgrn_fwd(s["u"], lb3, layer, gn3, j, f"hgrn_fwd_{layer}")
        W[w_out], held = weights(layer, w_out, (s["y"],))
        if layer == 0:
            h, s["n2"] = _mm_nn_norm(s["y"], W[w_out], 0, h, mlp_g, layer, "mix_out_0", deps=held)
            s["h1"] = h
            W["mlp_w1"], held = weights(layer, "mlp_w1", (s["n2"],))
            s["relu"] = _mm_nn(s["n2"], W["mlp_w1"], 0, "mlp_up_0", relu=True, deps=held)
            W["mlp_w2"], held = weights(layer, "mlp_w2", (s["relu"],))
            h, n_next = _mm_nn_norm(s["relu"], W["mlp_w2"], 0, h, mix_g, 1, "mlp_down_0", square=True, deps=held)
        else:
            W["mlp_w1"], more1 = weights(layer, "mlp_w1", (s["y"],))
            W["mlp_w2"], more2 = weights(layer, "mlp_w2", (s["y"],))
            last = layer + 1 == n_layers
            out = _tail_fwd(s["y"], W[w_out], h, mlp_g, layer, W["mlp_w1"], W["mlp_w2"], None if last else mix_g,
                            f"tail_{layer}", deps=held + more1 + more2)
            s["h1"], s["n2"], h, s["relu"] = out[0], out[1], out[2], out[-1]
            n_next = None if last else out[3]
        saved.append(s)

    dh, dhb, dg_final, loss = _final(h, P["final_norm_g"].reshape(1, D), target)

    small = {"final_norm_g": dg_final}
    per_layer = {k: [None] * n_layers for k in ("mix_norm_g", "mlp_norm_g", "lb")}
    per_pair = {k: [None] * (n_layers // 2) for k in
                ("ev_conv_w", "ev_conv_b", "ev_ln_g", "ev_ln_b", "ev_pool_w", "ev_pool_b", "ev_pool_scale", "od_gnorm_g")}
    for layer in reversed(range(n_layers)):
        j = layer // 2
        s = saved[layer]
        W = s["W"]
        w_in, w_out = _mixer_names(layer)
        dw2 = _mm_tn(s["relu"], dhb, "row", f"dw2_{layer}", square=True)
        dz, dh, dhb, per_layer["mlp_norm_g"][layer] = _mlp_bwd(
            dhb, s["relu"], W["mlp_w1"], W["mlp_w2"], s["h1"], mlp_g, layer, dh, f"mlp_bwd_{layer}", deps=deps + (dw2,))
        dw1 = _mm_tn(s["n2"], dz, "col", f"dw1_{layer}")
        deps = boundary(f"mlp{layer}", {("mlp_w1", layer): dw1, ("mlp_w2", layer): dw2}, (dhb, dw1, dw2))
        dy = _mm_nt(dhb, W[w_out], 0, f"d_y_{layer}", deps=deps)
        dwout = _mm_tn(s["y"], dhb, "row", f"dwout_{layer}")
        if layer % 2 == 0:
            du, dcw, dcb, dlg, dlnb, dpw, dpb, dps = _even_bwd(s["u"], s["yc"], dy, P["conv_w_full"], *even, j, f"even_bwd_{layer}")
            for k, val in (("ev_conv_w", dcw), ("ev_conv_b", dcb), ("ev_ln_g", dlg), ("ev_ln_b", dlnb),
                           ("ev_pool_w", dpw), ("ev_pool_b", dpb), ("ev_pool_scale", dps)):
                per_pair[k][j] = val
        else:
            du, per_layer["lb"][layer], per_pair["od_gnorm_g"][j] = _hgrn_bwd(
                s["u"], s["o"], dy, s["sall"], lb3, layer, gn3, j, f"hgrn_bwd_{layer}")
        dwin = _mm_tn(s["n"], du, "col", f"dwin_{layer}")
        deps = boundary(f"mix{layer}", {(w_in, j): dwin, (w_out, j): dwout}, (du, dwin, dwout))
        dh, dhb, per_layer["mix_norm_g"][layer] = _mm_nt_norm(du, W[w_in], 0, s["h"], mix_g, layer, dh, f"d_n_{layer}", deps=deps)
        deps = ()

    small["mix_norm_g"] = jnp.concatenate(per_layer["mix_norm_g"], axis=0)
    small["mlp_norm_g"] = jnp.concatenate(per_layer["mlp_norm_g"], axis=0)
    dlb_all = jnp.concatenate([jnp.zeros((1, D), F32) if g is None else g for g in per_layer["lb"]], axis=0)
    small["lb_param"] = _lb_bwd(P["lb_param"], dlb_all)
    for k, vals in per_pair.items():
        small[k] = jnp.stack(vals, axis=0)
    small["meta_tokens"] = dh[PAD:LEAD]
    return loss, dh, small


def kernel(x, meta_tokens, mix_norm_g, mlp_norm_g, final_norm_g, ev_w_in, ev_conv_w, ev_conv_b, ev_ln_g, ev_ln_b, ev_pool_w, ev_pool_b, ev_pool_scale, ev_w_out, od_w_in, od_gnorm_g, od_w_out, lb_param, mlp_w1, mlp_w2, loss_target, m_meta_tokens, m_mix_norm_g, m_mlp_norm_g, m_final_norm_g, m_ev_w_in, m_ev_conv_w, m_ev_conv_b, m_ev_ln_g, m_ev_ln_b, m_ev_pool_w, m_ev_pool_b, m_ev_pool_scale, m_ev_w_out, m_od_w_in, m_od_gnorm_g, m_od_w_out, m_lb_param, m_mlp_w1, m_mlp_w2, v_meta_tokens, v_mix_norm_g, v_mlp_norm_g, v_final_norm_g, v_ev_w_in, v_ev_conv_w, v_ev_conv_b, v_ev_ln_g, v_ev_ln_b, v_ev_pool_w, v_ev_pool_b, v_ev_pool_scale, v_ev_w_out, v_od_w_in, v_od_gnorm_g, v_od_w_out, v_lb_param, v_mlp_w1, v_mlp_w2):
    given = dict(locals())
    w = {n: given[n] for n in WEIGHTS}
    m = {n: given["m_" + n] for n in WEIGHTS}
    v = {n: given["v_" + n] for n in WEIGHTS}
    n_layers = mix_norm_g.shape[0]
    core = lax.axis_index("c").astype(jnp.int32)
    chip = (2 * lax.axis_index("x") + lax.axis_index("y")).astype(jnp.int32)
    chip1 = chip.reshape(1)
    ids2 = jnp.stack([core, chip])

    conv_pad = jnp.pad(ev_conv_w, ((0, 0), (0, CONV_ROWS - CONV_WIDTH), (0, 0)))
    stages = [[(0, n)] for n in (*_mixer_names(0), "mlp_w1", "mlp_w2")]
    for layer in range(1, n_layers):
        stages += [[(layer, n) for n in _mixer_names(layer)], [(layer, "mlp_w1"), (layer, "mlp_w2")]]
    where, stage_kinds, stage_bufs = {}, [], []
    for k, stage in enumerate(stages):
        index = [layer if n.startswith("mlp") else layer // 2 for layer, n in stage]
        kinds = [BIG[n] for _, n in stage]
        bufs = [_cast_place(w[n], i, BIG[n], chip1, BF16, f"place_{n}_{i}") for (_, n), i in zip(stage, index)]
        if k == 0:
            bufs.append(_cast_place(meta_tokens[None], 0, "col", chip1, F32, "place_meta"))
            bufs.append(_cast_place(conv_pad.reshape(1, -1, conv_pad.shape[2]), 0, "col", chip1, F32, "place_conv_w"))
            kinds += ["col", "col"]
        stage_kinds.append(kinds)
        stage_bufs.append(bufs)
        where.update({key: (k, f) for f, key in enumerate(stage)})
    gathers, token, early = [], (), 3
    for lo, hi, name in ((0, early, "gather_start_first"), (early, len(stages), "gather_start_rest")):
        every = [b for bufs in stage_bufs[lo:hi] for b in bufs]
        kinds_all = [kd for kinds in stage_kinds[lo:hi] for kd in kinds]
        ss, rs, every, tok = _split_start(name, _gather_plan(kinds_all, True), every, 3 * len(every), deps=token)
        token = (tok,)
        at = 0
        for kinds in stage_kinds[lo:hi]:
            gathers.append((kinds, _gather_plan(kinds, True, first=at), ss, rs, every[at:at + len(kinds)]))
            at += len(kinds)

    landed, passed, held = {}, {}, []

    def hand_on(k, deps):
        if k not in passed:
            kinds, plan, ss, rs, bufs = gathers[k]
            to_sibling = _gather_plan(kinds, False)
            ss, rs, bufs, tok = _split_start(f"gather_pass_{k}", to_sibling, bufs, 3 * len(bufs), deps=deps, earlier=(plan, ss, rs))
            passed[k] = (to_sibling, ss, rs, bufs)
            held.append(tok)

    def arrived(k, after):
        if k not in landed:
            hand_on(k, after)
            landed[k] = _split_wait(f"gather_wait_{k}", *passed[k], after)
        return landed[k]

    def weights(layer, name, after):
        k, f = where[(layer, name)]
        full = arrived(k, after)[f][None]
        if name == "mlp_w2" and layer + 1 < n_layers:
            hand_on(where[(layer + 1, _mixer_names(layer + 1)[0])][0], after)
        if layer > 0 and name == _mixer_names(layer)[0]:
            hand_on(where[(layer, "mlp_w1")][0], after)
        tokens = tuple(held)
        held.clear()
        return full, tokens

    first = arrived(0, token)
    P = {n: w[n] for n in ("mix_norm_g", "mlp_norm_g", "final_norm_g", "ev_conv_b", "ev_ln_g", "ev_ln_b", "ev_pool_w",
                           "ev_pool_b", "ev_pool_scale", "od_gnorm_g", "lb_param")}
    P["meta_full"] = first[1]
    P["conv_w_full"] = first[2].reshape(ev_conv_w.shape[0], CONV_ROWS, -1)

    pending, outs = [], {n: None for n in BIG}

    def advance(after, fresh=1):
        ready, still = [], []
        for pos, st in enumerate(pending):
            if st["phase"] == 1 and pos >= len(pending) - fresh:
                still.append(st)
            elif st["phase"] == 1:
                bufs = _split_wait(f"reduce_wait_{st['tag']}", st["plan"], st["ss"], st["rs"], st["bufs"], after)
                halves = [_sum_pieces(ids2, bufs[2 * f], bufs[2 * f + 1], f"sum_{st['tag']}_{f}") for f in range(len(bufs) // 2)]
                ready.append((st, halves))
            else:
                grads = _split_wait(f"swap_wait_{st['tag']}", st["plan"], st["ss"], st["rs"], st["bufs"], after)
                for (n, i), g in zip(st["keys"], grads):
                    outs[n] = _adamw_layer(w[n], m[n], v[n], g.reshape(w[n].shape[1:]), i, outs[n], f"adamw_{n}_{i}")
        pending[:] = still
        return ready

    def launch(name, ready, tag=None, grads=None):
        bufs, parts, entries, at = [], [], [], 0
        for st, halves in ready:
            plan = _swap_plan(first=at)
            entries.append((dict(st, phase=2, plan=plan), len(bufs), len(halves)))
            parts.append((plan, len(halves)))
            bufs += halves
            at += len(halves)
        if grads is not None:
            pairs = []
            for acc in grads.values():
                pairs += [acc, lax.empty((N_DEV - 1,) + acc.shape[2:], BF16)]
            plan = _reduce_plan(first=at)
            entries.append((dict(phase=1, tag=tag, keys=list(grads), plan=plan), len(bufs), len(pairs)))
            parts.append((plan, len(pairs)))
            bufs += pairs
            at += 7 * len(grads)
        if not bufs:
            return ()
        ss, rs, bufs, tok = _split_start(name, _joined(parts), bufs, at)
        for st, lo, n in entries:
            pending.append(dict(st, ss=ss, rs=rs, bufs=bufs[lo:lo + n]))
        return (tok,)

    def boundary(tag, grads, after):
        return launch(f"start_{tag}", advance(after), tag, grads)

    loss, dh, small = _local_step(x[0], loss_target[0], P, weights, boundary, first_deps=token)

    order = [n for n in WEIGHTS if n not in BIG]
    block = _pack([small[n] for n in order] + [loss])
    ss, rs, bufs, tok = _split_start("small_start", _small_plan, [block, lax.empty((N_DEV,) + block.shape, F32)], N_DEV - 1)
    for last in range(3):
        launch(f"start_end_{last}", advance((tok,) + tuple(o[0] for o in outs.values() if o is not None), fresh=0))
    assert not pending
    block, land = _split_wait("small_wait", _small_plan, ss, rs, bufs, tuple(outs[n][0] for n in BIG))
    packed = _sum_blocks((4 * lax.axis_index("x") + 2 * lax.axis_index("y") + lax.axis_index("c")).astype(jnp.int32).reshape(1), block, land)
    total = _unpack(packed, [small[n].shape for n in order] + [loss.shape])
    loss_sum = total[-1][0, 0]
    gsmall = dict(zip(order, total[:-1]))
    gsmall["meta_tokens"] = lax.dynamic_slice_in_dim(gsmall["meta_tokens"], chip * meta_tokens.shape[1], meta_tokens.shape[1], 1)
    gsmall["ev_conv_w"] = lax.dynamic_slice_in_dim(gsmall["ev_conv_w"][:, :CONV_WIDTH], chip * ev_conv_w.shape[2], ev_conv_w.shape[2], 2)

    g_out, d_out, m_out, v_out = {}, {}, {}, {}
    for n in WEIGHTS:
        if n in BIG:
            g_out[n], d_out[n], m_out[n], v_out[n] = outs[n]
            continue
        shape = w[n].shape
        g = gsmall[n].reshape(shape)
        cols = shape[-1] if len(shape) > 1 else 128
        two = lambda a: a.reshape(-1, cols)
        d_, m_, v_ = _adamw(two(w[n]), two(g), two(m[n]), two(v[n]), f"adamw_{n}")
        g_out[n], d_out[n], m_out[n], v_out[n] = g, d_.reshape(shape), m_.reshape(shape), v_.reshape(shape)

    grad_x = dh[LEAD:][None]
    return (loss_sum, grad_x, *[g_out[n] for n in WEIGHTS], *[d_out[n] for n in WEIGHTS],
            *[m_out[n] for n in WEIGHTS], *[v_out[n] for n in WEIGHTS])
```

```python
import functools

import jax
import jax.numpy as jnp
from jax import lax
from jax.experimental import pallas as pl
from jax.experimental.pallas import tpu as pltpu

F32 = jnp.float32
BF16 = jnp.bfloat16
SDS = jax.ShapeDtypeStruct
MESH = pl.DeviceIdType.MESH
ANY_SPEC = pl.BlockSpec(memory_space=pl.ANY)

N_META = 16
CHUNK = 64
LEAD = CHUNK
PAD = LEAD - N_META
CONV_WIDTH = 31
CONV_ROWS = 32
POOL_WINDOWS = (2, 4, 8, 16)
HEAD_DIM = 128
SUB = 16
EXP_CAP = 80.0
EPS = 1e-6
ADAM_LR = 0.001
ADAM_B1 = 0.9
ADAM_B2 = 0.999
ADAM_EPS = 1e-08
ADAM_WD = 0.01
ADAM_STEP = 10
N_CHIPS = 4
VMEM_LIMIT = 58 << 20
MM_VMEM_BUDGET = 50 << 20


def _params(*sem):
    return pltpu.CompilerParams(dimension_semantics=sem if sem else None, vmem_limit_bytes=VMEM_LIMIT)


def _tile(n, target, unit=CHUNK):
    best = None
    for t in range(unit, min(n, target) + 1, unit):
        if n % t == 0:
            best = t
    assert best is not None, (n, target, unit)
    return best


def _ctile(n, target=512):
    for t in (512, 384, 256, 128):
        if t <= target and n % t == 0:
            return t
    raise ValueError(n)


def _mm_tiles(M, N, per_row, per_col, per_elem):
    best = None
    for tn in (512, 384, 256, 128):
        if N % tn:
            continue
        for tm in sorted((d for d in range(16, M + 1, 16) if M % d == 0), reverse=True):
            if 2 * (tm * per_row + tn * per_col + tm * tn * per_elem) <= MM_VMEM_BUDGET:
                if best is None or tm * tn > best[0] * best[1]:
                    best = (tm, tn)
                break
    assert best is not None, (M, N)
    return best


def _sigmoid(x):
    return 1.0 / (1.0 + jnp.exp(-x))


def _mult(v, m):
    return v if isinstance(v, int) else pl.multiple_of(v, m)


def _row_ids(shape, base):
    return lax.broadcasted_iota(jnp.int32, shape, 0) + base


def _cast_place(w3, layer, kind, chip1, dtype, name):
    _, ks, ns = w3.shape
    tr = _tile(ks, 512, 16)
    full = (ks, ns * N_CHIPS) if kind == "col" else (ks * N_CHIPS, ns)

    def body(chip_ref, w_ref, o_ref):
        del chip_ref
        o_ref[...] = w_ref[...].astype(dtype)

    omap = (lambda i, chip: (i, chip[0])) if kind == "col" else (lambda i, chip: (chip[0] * (ks // tr) + i, 0))
    return pl.pallas_call(
        body,
        grid_spec=pltpu.PrefetchScalarGridSpec(
            num_scalar_prefetch=1, grid=(ks // tr,),
            in_specs=[pl.BlockSpec((None, tr, ns), lambda i, chip: (layer, i, 0))],
            out_specs=pl.BlockSpec((tr, ns), omap)),
        out_shape=SDS(full, dtype), name=name, compiler_params=_params("parallel"))(chip1, w3)


def _rms_fwd(h, g3, layer, name, deps=()):
    T, D = h.shape
    tm = _tile(T, 832)

    def body(h_ref, g_ref, *rest):
        n_ref = rest[-1]
        x = h_ref[...]
        r = lax.rsqrt(jnp.mean(x * x, axis=-1, keepdims=True) + EPS)
        n_ref[...] = ((x * r) * g_ref[...]).astype(BF16)

    return pl.pallas_call(
        body, grid=(T // tm,),
        in_specs=[pl.BlockSpec((tm, D), lambda i: (i, 0)), pl.BlockSpec((None, 1, D), lambda i: (layer, 0, 0))]
        + [ANY_SPEC] * len(deps),
        out_specs=pl.BlockSpec((tm, D), lambda i: (i, 0)), out_shape=SDS((T, D), BF16),
        name=name, compiler_params=_params("parallel"))(h, g3, *deps)


def _final(h, g2, target):
    T, D = h.shape
    tm = _tile(T, 320)
    nsub = tm // CHUNK
    nblk = target.shape[0] // CHUNK

    def body(h_ref, g_ref, *rest):
        t_refs = rest[:nsub]
        dh_ref, dhb_ref, dg_ref, loss_ref = rest[nsub:]
        i = pl.program_id(0)

        @pl.when(i == 0)
        def _():
            dg_ref[...] = jnp.zeros_like(dg_ref)
            loss_ref[...] = jnp.zeros_like(loss_ref)

        g = g_ref[...]
        for q in range(nsub):
            rows = slice(q * CHUNK, (q + 1) * CHUNK)
            x = h_ref[rows, :]
            r = lax.rsqrt(jnp.mean(x * x, axis=-1, keepdims=True) + EPS)
            xh = x * r
            live = jnp.where(i * nsub + q > 0, 1.0, 0.0).astype(F32)
            e = ((xh * g) - t_refs[q][...]) * live
            dy = e * (1.0 / D)
            dxh = dy * g
            dh = r * (dxh - xh * jnp.mean(dxh * xh, axis=-1, keepdims=True))
            dh_ref[rows, :] = dh
            dhb_ref[rows, :] = dh.astype(BF16)
            dg_ref[...] += jnp.sum(dy * xh, axis=0, keepdims=True)
            loss_ref[...] += jnp.sum(e * e) * (0.5 / D)

    row = pl.BlockSpec((tm, D), lambda i: (i, 0))
    t_specs = [pl.BlockSpec((CHUNK, D), functools.partial(lambda i, q: (jnp.clip(i * nsub + q - 1, 0, nblk - 1), 0), q=q))
               for q in range(nsub)]
    return pl.pallas_call(
        body, grid=(T // tm,),
        in_specs=[row, pl.BlockSpec((1, D), lambda i: (0, 0))] + t_specs,
        out_specs=[row, row, pl.BlockSpec((1, D), lambda i: (0, 0)), pl.BlockSpec((1, 128), lambda i: (0, 0))],
        out_shape=[SDS((T, D), F32), SDS((T, D), BF16), SDS((1, D), F32), SDS((1, 128), F32)],
        name="final_loss", compiler_params=_params("arbitrary"))(h, g2, *([target] * nsub))


def _mm_nn(a, w3, layer, name, res=None, relu=False, square=False, deps=()):
    M, K = a.shape
    N = w3.shape[2]
    tm, tn = _mm_tiles(M, N, 2 * K, 2 * K, (2 if relu else 4) + (4 if res is not None else 0))

    def body(*refs):
        lhs = refs[0][...]
        acc = jnp.dot(lhs * lhs if square else lhs, refs[1][...], preferred_element_type=F32)
        if res is not None:
            acc = acc + refs[2][...]
        refs[-1][...] = jnp.maximum(acc, 0.0).astype(BF16) if relu else acc

    in_specs = [pl.BlockSpec((tm, K), lambda i, j: (i, 0)), pl.BlockSpec((None, K, tn), lambda i, j: (layer, 0, j))]
    args = [a, w3]
    tile = pl.BlockSpec((tm, tn), lambda i, j: (i, j))
    if res is not None:
        in_specs.append(tile)
        args.append(res)
    in_specs += [ANY_SPEC] * len(deps)
    args += list(deps)
    return pl.pallas_call(
        body, grid=(M // tm, N // tn), in_specs=in_specs, out_specs=tile,
        out_shape=SDS((M, N), BF16 if relu else F32),
        name=name, compiler_params=_params("parallel", "parallel"))(*args)


def _mm_nt(dy, w3, layer, name, relu=None, deps=()):
    M, N = dy.shape
    K = w3.shape[1]
    tm, tk = _mm_tiles(M, K, 2 * N, 2 * N, 4)

    def body(*refs):
        acc = lax.dot_general(refs[0][...], refs[1][...], (((1,), (1,)), ((), ())), preferred_element_type=F32)
        if relu is not None:
            acc = (acc * (2.0 * refs[2][...].astype(F32))).astype(BF16)
        refs[-1][...] = acc

    tile = pl.BlockSpec((tm, tk), lambda i, j: (i, j))
    in_specs = [pl.BlockSpec((tm, N), lambda i, j: (i, 0)), pl.BlockSpec((None, tk, N), lambda i, j: (layer, j, 0))]
    args = [dy, w3]
    if relu is not None:
        in_specs.append(tile)
        args.append(relu)
    in_specs += [ANY_SPEC] * len(deps)
    args += list(deps)
    return pl.pallas_call(
        body, grid=(M // tm, K // tk), in_specs=in_specs, out_specs=tile,
        out_shape=SDS((M, K), F32 if relu is None else BF16),
        name=name, compiler_params=_params("parallel", "parallel"))(*args)


def _row_tile(M, per_row, fixed):
    for tm in sorted((d for d in range(16, M + 1, 16) if M % d == 0), reverse=True):
        if 2 * (tm * per_row + fixed) <= MM_VMEM_BUDGET:
            return tm
    raise ValueError((M, per_row, fixed))


def _mm_nn_norm(a, w3, layer, res, g3, glayer, name, square=False, deps=()):
    M, K = a.shape
    D = w3.shape[2]
    tm = _row_tile(M, 2 * K + 10 * D, 2 * K * D)

    def body(a_ref, w_ref, r_ref, g_ref, *rest):
        h_ref, n_ref = rest[-2:]
        lhs = a_ref[...]
        x = r_ref[...] + jnp.dot(lhs * lhs if square else lhs, w_ref[...], preferred_element_type=F32)
        h_ref[...] = x
        r = lax.rsqrt(jnp.mean(x * x, axis=-1, keepdims=True) + EPS)
        n_ref[...] = ((x * r) * g_ref[...]).astype(BF16)

    row = pl.BlockSpec((tm, D), lambda i: (i, 0))
    return pl.pallas_call(
        body, grid=(M // tm,),
        in_specs=[pl.BlockSpec((tm, K), lambda i: (i, 0)), pl.BlockSpec((None, K, D), lambda i: (layer, 0, 0)), row,
                  pl.BlockSpec((None, 1, D), lambda i: (glayer, 0, 0))] + [ANY_SPEC] * len(deps),
        out_specs=[row, row], out_shape=[SDS((M, D), F32), SDS((M, D), BF16)],
        name=name, compiler_params=_params("parallel"))(a, w3, res, g3, *deps)


def _tail_fwd(y, w_out, res, mlp_g3, layer, w1, w2, next_g3, name, deps=()):
    M, K = y.shape
    D = w_out.shape[2]
    F = w1.shape[2]
    hb = _ctile(F)
    more = next_g3 is not None
    tm = _row_tile(M, 2 * K + 18 * D + (2 * D if more else 0) + 2 * F, 2 * K * D + 2 * D * F)

    def body(y_ref, wo_ref, res_ref, g_ref, w1_ref, w2_ref, *rest):
        outs = rest[-5:] if more else rest[-4:]
        h1 = res_ref[...] + jnp.dot(y_ref[...], wo_ref[...], preferred_element_type=F32)
        outs[0][...] = h1
        n2 = ((h1 * lax.rsqrt(jnp.mean(h1 * h1, axis=-1, keepdims=True) + EPS)) * g_ref[...]).astype(BF16)
        outs[1][...] = n2
        acc = h1
        for jb in range(F // hb):
            cols = slice(jb * hb, (jb + 1) * hb)
            r = jnp.maximum(jnp.dot(n2, w1_ref[:, cols], preferred_element_type=F32), 0.0).astype(BF16)
            outs[-1][:, cols] = r
            acc = acc + jnp.dot(r * r, w2_ref[cols, :], preferred_element_type=F32)
        outs[2][...] = acc
        if more:
            outs[3][...] = ((acc * lax.rsqrt(jnp.mean(acc * acc, axis=-1, keepdims=True) + EPS)) * rest[0][...]).astype(BF16)

    row = pl.BlockSpec((tm, D), lambda i: (i, 0))
    once = dict(pipeline_mode=pl.Buffered(1))
    in_specs = [pl.BlockSpec((tm, K), lambda i: (i, 0)), pl.BlockSpec((None, K, D), lambda i: (0, 0, 0), **once), row,
                pl.BlockSpec((None, 1, D), lambda i: (layer, 0, 0)),
                pl.BlockSpec((None, D, F), lambda i: (0, 0, 0), **once), pl.BlockSpec((None, F, D), lambda i: (0, 0, 0), **once)]
    args = [y, w_out, res, mlp_g3, w1, w2]
    out_specs, out_shape = [row, row, row], [SDS((M, D), F32), SDS((M, D), BF16), SDS((M, D), F32)]
    if more:
        in_specs.append(pl.BlockSpec((None, 1, D), lambda i: (layer + 1, 0, 0)))
        args.append(next_g3)
        out_specs.append(row)
        out_shape.append(SDS((M, D), BF16))
    out_specs.append(pl.BlockSpec((tm, F), lambda i: (i, 0)))
    out_shape.append(SDS((M, F), BF16))
    in_specs += [ANY_SPEC] * len(deps)
    args += list(deps)
    return pl.pallas_call(
        body, grid=(M // tm,), in_specs=in_specs, out_specs=out_specs, out_shape=out_shape,
        name=name, compiler_params=_params("parallel"))(*args)


def _mlp_bwd(dhb, relu, w1, w2, h, g3, glayer, dh_in, name, deps=()):
    M, D = dhb.shape
    F = w1.shape[2]
    hb = _ctile(F)
    tm = _row_tile(M, 16 * D + 4 * F, 2 * D * F)

    def body(dy_ref, r_ref, w1_ref, w2_ref, h_ref, g_ref, dhi_ref, *rest):
        dz_ref, dh_ref, dhb_ref, dg_ref = rest[-4:]
        dy = dy_ref[...]
        dn = jnp.zeros((tm, D), F32)
        for jb in range(F // hb):
            cols = slice(jb * hb, (jb + 1) * hb)
            dact = lax.dot_general(dy, w2_ref[cols, :], (((1,), (1,)), ((), ())), preferred_element_type=F32)
            dz = (dact * (2.0 * r_ref[:, cols].astype(F32))).astype(BF16)
            dz_ref[:, cols] = dz
            dn = dn + lax.dot_general(dz, w1_ref[:, cols], (((1,), (1,)), ((), ())), preferred_element_type=F32)
        x = h_ref[...]
        r = lax.rsqrt(jnp.mean(x * x, axis=-1, keepdims=True) + EPS)
        xh = x * r
        dxh = dn * g_ref[...]
        dh = dhi_ref[...] + r * (dxh - xh * jnp.mean(dxh * xh, axis=-1, keepdims=True))
        dh_ref[...] = dh
        dhb_ref[...] = dh.astype(BF16)

        @pl.when(pl.program_id(0) == 0)
        def _():
            dg_ref[...] = jnp.zeros_like(dg_ref)

        dg_ref[...] += jnp.sum(dn * xh, axis=0, keepdims=True)

    row = pl.BlockSpec((tm, D), lambda i: (i, 0))
    wide = pl.BlockSpec((tm, F), lambda i: (i, 0))
    once = dict(pipeline_mode=pl.Buffered(1))
    return pl.pallas_call(
        body, grid=(M // tm,),
        in_specs=[row, wide, pl.BlockSpec((None, D, F), lambda i: (0, 0, 0), **once),
                  pl.BlockSpec((None, F, D), lambda i: (0, 0, 0), **once), row,
                  pl.BlockSpec((None, 1, D), lambda i: (glayer, 0, 0)), row] + [ANY_SPEC] * len(deps),
        out_specs=[wide, row, row, pl.BlockSpec((1, D), lambda i: (0, 0))],
        out_shape=[SDS((M, F), BF16), SDS((M, D), F32), SDS((M, D), BF16), SDS((1, D), F32)],
        name=name, compiler_params=_params("arbitrary"))(dhb, relu, w1, w2, h, g3, dh_in, *deps)


def _mm_nt_norm(dy, w3, layer, h, g3, glayer, dh_in, name, deps=()):
    M, N = dy.shape
    D = w3.shape[1]
    tm = _row_tile(M, 2 * N + 14 * D, 2 * N * D)

    def body(dy_ref, w_ref, h_ref, g_ref, dhi_ref, *rest):
        dh_ref, dhb_ref, dg_ref = rest[-3:]
        dn = lax.dot_general(dy_ref[...], w_ref[...], (((1,), (1,)), ((), ())), preferred_element_type=F32)
        x = h_ref[...]
        r = lax.rsqrt(jnp.mean(x * x, axis=-1, keepdims=True) + EPS)
        xh = x * r
        dxh = dn * g_ref[...]
        dh = dhi_ref[...] + r * (dxh - xh * jnp.mean(dxh * xh, axis=-1, keepdims=True))
        dh_ref[...] = dh
        dhb_ref[...] = dh.astype(BF16)

        @pl.when(pl.program_id(0) == 0)
        def _():
            dg_ref[...] = jnp.zeros_like(dg_ref)

        dg_ref[...] += jnp.sum(dn * xh, axis=0, keepdims=True)

    row = pl.BlockSpec((tm, D), lambda i: (i, 0))
    return pl.pallas_call(
        body, grid=(M // tm,),
        in_specs=[pl.BlockSpec((tm, N), lambda i: (i, 0)), pl.BlockSpec((None, D, N), lambda i: (layer, 0, 0)), row,
                  pl.BlockSpec((None, 1, D), lambda i: (glayer, 0, 0)), row] + [ANY_SPEC] * len(deps),
        out_specs=[row, row, pl.BlockSpec((1, D), lambda i: (0, 0))],
        out_shape=[SDS((M, D), F32), SDS((M, D), BF16), SDS((1, D), F32)],
        name=name, compiler_params=_params("arbitrary"))(dy, w3, h, g3, dh_in, *deps)


def _fam_dims(kind, K, N):
    return (K // 2, N // N_CHIPS) if kind == "col" else (K // (2 * N_CHIPS), N)


def _mm_tn(x, dy, kind, name, square=False):
    M, K = x.shape
    N = dy.shape[1]
    nr, nc = _fam_dims(kind, K, N)

    def body(x_ref, dy_ref, o_ref):
        lhs = x_ref[...]
        res = lax.dot_general(lhs * lhs if square else lhs, dy_ref[...], (((0,), (0,)), ((), ())), preferred_element_type=F32)
        o_ref[...] = res.astype(BF16).reshape(o_ref.shape)

    if kind == "col":
        tn = _ctile(nc)
        ct = nc // tn
        grid = (N // tn,)
        in_specs = [pl.BlockSpec((M, K), lambda j: (0, 0)), pl.BlockSpec((M, tn), lambda j: (0, j))]
        out_spec = pl.BlockSpec((2, None, nr, tn), lambda j: (0, j // ct, 0, j % ct))
    else:
        grid = (N_CHIPS,)
        in_specs = [pl.BlockSpec((M, 2 * nr), lambda i: (0, i)), pl.BlockSpec((M, N), lambda i: (0, 0))]
        out_spec = pl.BlockSpec((2, None, nr, N), lambda i: (0, i, 0, 0))
    return pl.pallas_call(
        body, grid=grid, in_specs=in_specs, out_specs=out_spec, out_shape=SDS((2, N_CHIPS, nr, nc), BF16),
        name=name, compiler_params=_params("parallel"))(x, dy)


C_EVEN = 512


def _live(rows, base, total):
    r = _row_ids((rows, 1), base)
    return jnp.logical_and(r >= PAD, r < total).astype(F32)


def _conv_taps(win, w_ref, ls, acc, flip):
    for b in range(8):
        rb = win if b == 0 else pltpu.roll(win, 96 - b, 0)
        for a in range(5):
            o = 8 * a + b
            tap = (30 - o) if flip else (o - 2)
            if 0 <= tap < CONV_WIDTH:
                acc = acc + w_ref[pl.ds(tap, 1), ls] * rb[8 * a:8 * a + CHUNK]
    return acc


def _window_sum(win, levels, forward):
    s = win
    n = win.shape[0]
    for k in range(levels):
        step = 1 << k
        s = s + pltpu.roll(s, (n - step) if forward else step, 0)
    return s


def _pool_count(base, g):
    pos = _row_ids((CHUNK, 1), base) - PAD
    return jnp.clip(pos + 1, 1, POOL_WINDOWS[g]).astype(F32)


def _even_fwd(u, cw3, cb3, lg3, lb3, pw4, pb3, ps3, j, name):
    T = u.shape[0]
    C = C_EVEN
    tm = _tile(T, 320)
    nch = tm // CHUNK
    nblk = T // CHUNK

    def body(u_ref, up_ref, cw_ref, cb_ref, lg_ref, lb_ref, pw_ref, pb_ref, ps_ref, o_ref, yc_ref, a_s, p_s, yc_s):
        row0 = pl.program_id(0) * tm
        up = up_ref[...]
        lp = _live(CHUNK, row0 - CHUNK, T)
        a_s[0:CHUNK, :] = up[:, 0:C] * _sigmoid(up[:, C:2 * C]) * lp
        p_s[0:CHUNK, :] = up[:, 2 * C:3 * C] * lp

        def stage(c, _):
            rs = _mult(c * CHUNK, CHUNK)
            lv = _live(CHUNK, row0 + rs, T)
            a_s[pl.ds(rs + CHUNK, CHUNK), :] = u_ref[pl.ds(rs, CHUNK), 0:C] * _sigmoid(u_ref[pl.ds(rs, CHUNK), C:2 * C]) * lv
            p_s[pl.ds(rs + CHUNK, CHUNK), :] = u_ref[pl.ds(rs, CHUNK), 2 * C:3 * C] * lv
            return 0

        for c in range(nch):
            stage(c, 0)

        def chunk(c, _):
            rs = _mult(c * CHUNK, CHUNK)
            lv = _live(CHUNK, row0 + rs, T)
            for cb in range(4):
                ls = slice(cb * 128, (cb + 1) * 128)
                win = a_s[pl.ds(_mult(rs + 32, 32), 96), ls]
                acc = jnp.broadcast_to(cb_ref[:, ls], (CHUNK, 128))
                yc_s[:, ls] = _conv_taps(win, cw_ref, ls, acc, False)
            y = yc_s[...]
            yc_ref[pl.ds(rs, CHUNK), :] = y
            xc = y - jnp.mean(y, axis=-1, keepdims=True)
            yn = xc * lax.rsqrt(jnp.mean(xc * xc, axis=-1, keepdims=True) + EPS) * lg_ref[...] + lb_ref[...]
            o_ref[pl.ds(rs, CHUNK), 0:C] = (yn * _sigmoid(yn) * lv).astype(BF16)
            for g in range(4):
                ls = slice(g * 128, (g + 1) * 128)
                win = p_s[pl.ds(_mult(rs + 48, 16), 80), ls]
                s = _window_sum(win, g + 1, False)
                d = s[16:80] / _pool_count(row0 + rs, g) - win[16:80]
                yv = jnp.dot(d.astype(BF16), pw_ref[g].astype(BF16), preferred_element_type=F32) + pb_ref[:, ls]
                o_ref[pl.ds(rs, CHUNK), C + g * 128:C + (g + 1) * 128] = (yv * ps_ref[:, ls] * lv).astype(BF16)
            return 0

        for c in range(nch):
            chunk(c, 0)

    vec = pl.BlockSpec((None, 1, C), lambda i: (j, 0, 0))
    return pl.pallas_call(
        body, grid=(T // tm,),
        in_specs=[pl.BlockSpec((tm, 3 * C), lambda i: (i, 0)),
                  pl.BlockSpec((CHUNK, 3 * C), lambda i: (jnp.maximum(i * nch - 1, 0), 0)),
                  pl.BlockSpec((None, CONV_ROWS, C), lambda i: (j, 0, 0)), vec, vec, vec,
                  pl.BlockSpec((None, 4, 128, 128), lambda i: (j, 0, 0, 0)), vec, vec],
        out_specs=[pl.BlockSpec((tm, 2 * C), lambda i: (i, 0)), pl.BlockSpec((tm, C), lambda i: (i, 0))],
        out_shape=[SDS((T, 2 * C), BF16), SDS((T, C), F32)],
        scratch_shapes=[pltpu.VMEM((tm + CHUNK, C), F32), pltpu.VMEM((tm + CHUNK, C), F32), pltpu.VMEM((CHUNK, C), F32)],
        name=name, compiler_params=_params("parallel"))(u, u, cw3, cb3, lg3, lb3, pw4, pb3, ps3)


def _even_bwd(u, yc, dy, cw3, cb3, lg3, lb3, pw4, pb3, ps3, j, name):
    T = u.shape[0]
    C = C_EVEN
    tm = _tile(T, 320)
    nch = tm // CHUNK
    nblk = T // CHUNK
    ntile = T // tm

    def body(u_ref, up_ref, un_ref, yc_ref, ycn_ref, dy_ref, dyn_ref, cw_ref, cb_ref, lg_ref, lb_ref, pw_ref, pb_ref, ps_ref,
             du_ref, dcw_ref, dcb_ref, dlg_ref, dlb_ref, dpw_ref, dpb_ref, dps_ref,
             a_s, p_s, dy_s, dyc_s, dd_s, ddc_s, dw_s):
        i = pl.program_id(0)
        row0 = i * tm

        @pl.when(i == 0)
        def _():
            for ref in (dcb_ref, dlg_ref, dlb_ref, dpw_ref, dpb_ref, dps_ref, dw_s):
                ref[...] = jnp.zeros_like(ref)

        up = up_ref[...]
        lp = _live(CHUNK, row0 - CHUNK, T)
        a_s[0:CHUNK, :] = up[:, 0:C] * _sigmoid(up[:, C:2 * C]) * lp
        p_s[0:CHUNK, :] = up[:, 2 * C:3 * C] * lp
        ln_ = _live(CHUNK, row0 + tm, T)
        p_s[tm + CHUNK:tm + 2 * CHUNK, :] = un_ref[:, 2 * C:3 * C] * ln_
        dy_s[tm:tm + CHUNK, :] = dyn_ref[...] * ln_
        dyc_s[tm + CHUNK:tm + CHUNK + 32, :] = jnp.zeros((32, C), F32)

        def stage(c, _):
            rs = _mult(c * CHUNK, CHUNK)
            lv = _live(CHUNK, row0 + rs, T)
            a_s[pl.ds(rs + CHUNK, CHUNK), :] = u_ref[pl.ds(rs, CHUNK), 0:C] * _sigmoid(u_ref[pl.ds(rs, CHUNK), C:2 * C]) * lv
            p_s[pl.ds(rs + CHUNK, CHUNK), :] = u_ref[pl.ds(rs, CHUNK), 2 * C:3 * C] * lv
            dy_s[pl.ds(rs, CHUNK), :] = dy_ref[pl.ds(rs, CHUNK), :] * lv
            return 0

        for c in range(nch):
            stage(c, 0)

        def first(rs, y, own):
            xc = y - jnp.mean(y, axis=-1, keepdims=True)
            rstd = lax.rsqrt(jnp.mean(xc * xc, axis=-1, keepdims=True) + EPS)
            xh = xc * rstd
            yn = xh * lg_ref[...] + lb_ref[...]
            sg = _sigmoid(yn)
            dyn = dy_s[pl.ds(rs, CHUNK), 0:C] * (sg * (1.0 + yn * (1.0 - sg)))
            dlg_ref[...] += jnp.sum(dyn * xh, axis=0, keepdims=True) * own
            dlb_ref[...] += jnp.sum(dyn, axis=0, keepdims=True) * own
            dxh = dyn * lg_ref[...]
            dyc = rstd * (dxh - jnp.mean(dxh, axis=-1, keepdims=True) - xh * jnp.mean(dxh * xh, axis=-1, keepdims=True))
            dyc_s[pl.ds(rs, CHUNK), :] = dyc
            dcb_ref[...] += jnp.sum(dyc, axis=0, keepdims=True) * own
            for g in range(4):
                ls = slice(g * 128, (g + 1) * 128)
                win = p_s[pl.ds(rs + 48, 80), ls]
                s = _window_sum(win, g + 1, False)
                cnt = _pool_count(row0 + rs, g)
                d = (s[16:80] / cnt - win[16:80]).astype(BF16)
                w = pw_ref[g].astype(BF16)
                pre = jnp.dot(d, w, preferred_element_type=F32) + pb_ref[:, ls]
                dyb = dy_s[pl.ds(rs, CHUNK), C + g * 128:C + (g + 1) * 128]
                dpre = dyb * ps_ref[:, ls]
                dps_ref[:, ls] += jnp.sum(dyb * pre, axis=0, keepdims=True) * own
                dpb_ref[:, ls] += jnp.sum(dpre, axis=0, keepdims=True) * own
                dpre_b = (dpre * own).astype(BF16)
                dpw_ref[g] += lax.dot_general(d, dpre_b, (((0,), (0,)), ((), ())), preferred_element_type=F32)
                dd = lax.dot_general(dpre.astype(BF16), w, (((1,), (1,)), ((), ())), preferred_element_type=F32)
                dd_s[pl.ds(rs, CHUNK), ls] = dd
                ddc_s[pl.ds(rs, CHUNK), ls] = dd / cnt

        def first_in_tile(c, _):
            rs = _mult(c * CHUNK, CHUNK)
            first(rs, yc_ref[pl.ds(rs, CHUNK), :], 1.0)
            return 0

        for c in range(nch):
            first_in_tile(c, 0)
        first(tm, ycn_ref[...], 0.0)
        ddc_s[tm + CHUNK:tm + CHUNK + 16, :] = jnp.zeros((16, C), F32)

        def second(c, _):
            rs = _mult(c * CHUNK, CHUNK)
            lv = _live(CHUNK, row0 + rs, T)
            for cb in range(4):
                ls = slice(cb * 128, (cb + 1) * 128)
                wd = dyc_s[pl.ds(rs, 96), ls]
                da = _conv_taps(wd, cw_ref, ls, jnp.zeros((CHUNK, 128), F32), True)
                wa = a_s[pl.ds(_mult(rs + 32, 32), 96), ls]
                dyc = dyc_s[pl.ds(rs, CHUNK), ls]
                for b in range(8):
                    rb = wa if b == 0 else pltpu.roll(wa, 96 - b, 0)
                    for a in range(5):
                        tap = 8 * a + b - 2
                        if 0 <= tap < CONV_WIDTH:
                            prod = dyc * rb[8 * a:8 * a + CHUNK]
                            part = prod[0:8]
                            for q in range(1, 8):
                                part = part + prod[8 * q:8 * q + 8]
                            dw_s[8 * tap:8 * tap + 8, ls] += part
                val = u_ref[pl.ds(rs, CHUNK), ls]
                sg = _sigmoid(u_ref[pl.ds(rs, CHUNK), C + cb * 128:C + (cb + 1) * 128])
                du_ref[pl.ds(rs, CHUNK), ls] = (da * sg * lv).astype(BF16)
                du_ref[pl.ds(rs, CHUNK), C + cb * 128:C + (cb + 1) * 128] = (da * val * sg * (1.0 - sg) * lv).astype(BF16)
            for g in range(4):
                ls = slice(g * 128, (g + 1) * 128)
                z = _window_sum(ddc_s[pl.ds(rs, 80), ls], g + 1, True)
                dpin = (z[0:CHUNK] - dd_s[pl.ds(rs, CHUNK), ls]) * lv
                du_ref[pl.ds(rs, CHUNK), 2 * C + g * 128:2 * C + (g + 1) * 128] = dpin.astype(BF16)
            return 0

        for c in range(nch):
            second(c, 0)

        @pl.when(i == ntile - 1)
        def _():
            for tap in range(CONV_WIDTH):
                dcw_ref[tap:tap + 1, :] = jnp.sum(dw_s[8 * tap:8 * tap + 8, :], axis=0, keepdims=True)
            dcw_ref[CONV_WIDTH:CONV_ROWS, :] = jnp.zeros((CONV_ROWS - CONV_WIDTH, C), F32)

    vec = pl.BlockSpec((None, 1, C), lambda i: (j, 0, 0))
    ovec = pl.BlockSpec((1, C), lambda i: (0, 0))
    return pl.pallas_call(
        body, grid=(ntile,),
        in_specs=[pl.BlockSpec((tm, 3 * C), lambda i: (i, 0)),
                  pl.BlockSpec((CHUNK, 3 * C), lambda i: (jnp.maximum(i * nch - 1, 0), 0)),
                  pl.BlockSpec((CHUNK, 3 * C), lambda i: (jnp.minimum((i + 1) * nch, nblk - 1), 0)),
                  pl.BlockSpec((tm, C), lambda i: (i, 0)),
                  pl.BlockSpec((CHUNK, C), lambda i: (jnp.minimum((i + 1) * nch, nblk - 1), 0)),
                  pl.BlockSpec((tm, 2 * C), lambda i: (i, 0)),
                  pl.BlockSpec((CHUNK, 2 * C), lambda i: (jnp.minimum((i + 1) * nch, nblk - 1), 0)),
                  pl.BlockSpec((None, CONV_ROWS, C), lambda i: (j, 0, 0)), vec, vec, vec,
                  pl.BlockSpec((None, 4, 128, 128), lambda i: (j, 0, 0, 0)), vec, vec],
        out_specs=[pl.BlockSpec((tm, 3 * C), lambda i: (i, 0)), pl.BlockSpec((CONV_ROWS, C), lambda i: (0, 0)),
                   ovec, ovec, ovec, pl.BlockSpec((4, 128, 128), lambda i: (0, 0, 0)), ovec, ovec],
        out_shape=[SDS((T, 3 * C), BF16), SDS((CONV_ROWS, C), F32), SDS((1, C), F32), SDS((1, C), F32), SDS((1, C), F32),
                   SDS((4, 128, 128), F32), SDS((1, C), F32), SDS((1, C), F32)],
        scratch_shapes=[pltpu.VMEM((tm + CHUNK, C), F32), pltpu.VMEM((tm + 2 * CHUNK, C), F32),
                        pltpu.VMEM((tm + CHUNK, 2 * C), F32),
                        pltpu.VMEM((tm + CHUNK + 32, C), F32), pltpu.VMEM((tm + CHUNK, C), F32),
                        pltpu.VMEM((tm + CHUNK + 16, C), F32), pltpu.VMEM((8 * CONV_ROWS, C), F32)],
        name=name, compiler_params=_params("arbitrary"))(u, u, u, yc, yc, dy, dy, cw3, cb3, lg3, lb3, pw4, pb3, ps3)


HI = lax.Precision.HIGHEST


def _dot_nt(a, b):
    return lax.dot_general(a, b, (((1,), (1,)), ((), ())), preferred_element_type=F32)


def _dot_tn(a, b):
    return lax.dot_general(a, b, (((0,), (0,)), ((), ())), preferred_element_type=F32)


def _tri(lower):
    r = lax.broadcasted_iota(jnp.int32, (CHUNK, CHUNK), 0)
    c = lax.broadcasted_iota(jnp.int32, (CHUNK, CHUNK), 1)
    return jnp.where((c <= r) if lower else (c >= r), 1.0, 0.0).astype(F32)


def _hgrn_gates(u_ref, lb_ref, h, D, lv):
    ls = slice(h * HEAD_DIM, (h + 1) * HEAD_DIM)
    qraw = u_ref[:, ls]
    fraw = u_ref[:, D + h * HEAD_DIM:D + (h + 1) * HEAD_DIM]
    v = u_ref[:, 2 * D + h * HEAD_DIM:2 * D + (h + 1) * HEAD_DIM] * lv
    lbv = lb_ref[:, ls]
    sig = _sigmoid(fraw)
    forget = lbv + (1.0 - lbv) * sig
    logf = jnp.log(forget) * lv
    k = (1.0 - forget) * lv
    qsig = _sigmoid(qraw)
    q = qraw * qsig * lv
    return q, k, v, logf, (qraw, qsig, sig, forget, lbv)


def _sub_parts(q, k, b, b_s, I):
    rows = slice(SUB * I, SUB * (I + 1))
    rho = jnp.zeros((1, HEAD_DIM), F32) if I == 0 else b_s[SUB * I - 1:SUB * I, :]
    eI = jnp.exp(b[rows] - rho)
    EI = jnp.exp(jnp.minimum(rho - b, EXP_CAP))
    causal = (lax.broadcasted_iota(jnp.int32, (SUB, CHUNK), 1)
              <= lax.broadcasted_iota(jnp.int32, (SUB, CHUNK), 0) + SUB * I)
    return rows, q[rows] * eI, k * EI, eI, EI, causal


def _chunks_per_step(NC):
    for n in (5, 4, 3, 2):
        if NC % n == 0:
            return n
    return 1


def _hgrn_fwd(u, lb3, layer, gn3, j, name):
    T = u.shape[0]
    D = u.shape[1] // 4
    H = D // HEAD_DIM
    NC = T // CHUNK
    CH = _chunks_per_step(NC)
    R = CH * CHUNK

    def body(u_ref, lb_ref, gn_ref, y_ref, o_ref, sall_ref, st_s, b_s, lf_s, q_s, k_s):
        n = pl.program_id(0)

        @pl.when(n == 0)
        def _():
            st_s[...] = jnp.zeros_like(st_s)

        heads = range(H)
        cols = [slice(h * HEAD_DIM, (h + 1) * HEAD_DIM) for h in heads]
        rows = [slice(c * CHUNK, (c + 1) * CHUNK) for c in range(CH)]
        vb = {}
        for c in range(CH):
            lv = _live(CHUNK, (n * CH + c) * CHUNK, T)
            for h in heads:
                q, k, v, logf, _ = _hgrn_gates(u_ref.at[rows[c]], lb_ref, h, D, lv)
                q_s[rows[c], cols[h]] = q
                k_s[rows[c], cols[h]] = k
                lf_s[rows[c], cols[h]] = logf
                vb[c, h] = v.astype(BF16)
        for c in range(CH):
            b_s[rows[c], :] = jnp.dot(_tri(True), lf_s[rows[c], :], precision=HI, preferred_element_type=F32)
        ops = {}
        for c in range(CH):
            for h in heads:
                b_h = b_s.at[rows[c], cols[h]]
                b = b_h[...]
                q = q_s[rows[c], cols[h]]
                k = k_s[rows[c], cols[h]]
                blast = b_h[CHUNK - 1:CHUNK, :]
                qh = (q * jnp.exp(b)).astype(BF16)
                kt = (k * jnp.exp(blast - b)).astype(BF16)
                subs = []
                for I in range(CHUNK // SUB):
                    _, qI, KI, _, _, causal = _sub_parts(q, k, b, b_h, I)
                    subs.append((qI.astype(BF16), KI.astype(BF16), causal))
                ops[c, h] = (qh, kt, jnp.exp(blast), subs)
        mm = {}
        for h in heads:
            st = st_s[h]
            for c in range(CH):
                qh, kt, eblast, subs = ops[c, h]
                sall_ref[c, h] = st
                o_inter = _dot_nt(qh, st.astype(BF16))
                st = st * eblast + _dot_tn(vb[c, h], kt)
                mm[c, h] = (o_inter, [_dot_nt(qI, KI) for qI, KI, _ in subs])
            st_s[h] = st
        for c in range(CH):
            for h in heads:
                o_inter, ps = mm[c, h]
                p = jnp.concatenate([jnp.where(m, x, 0.0) for x, (_, _, m) in zip(ps, ops[c, h][3])], axis=0).astype(BF16)
                o = o_inter + jnp.dot(p, vb[c, h], preferred_element_type=F32)
                o_ref[rows[c], cols[h]] = o
                graw = u_ref[rows[c], 3 * D + h * HEAD_DIM:3 * D + (h + 1) * HEAD_DIM]
                r = lax.rsqrt(jnp.mean(o * o, axis=-1, keepdims=True) + EPS)
                y_ref[rows[c], cols[h]] = (((o * r) * gn_ref[...]) * (graw * _sigmoid(graw))).astype(BF16)

    return pl.pallas_call(
        body, grid=(NC // CH,),
        in_specs=[pl.BlockSpec((R, 4 * D), lambda n: (n, 0)),
                  pl.BlockSpec((None, 1, D), lambda n: (layer, 0, 0)),
                  pl.BlockSpec((None, 1, HEAD_DIM), lambda n: (j, 0, 0))],
        out_specs=[pl.BlockSpec((R, D), lambda n: (n, 0)), pl.BlockSpec((R, D), lambda n: (n, 0)),
                   pl.BlockSpec((CH, H, HEAD_DIM, HEAD_DIM), lambda n: (n, 0, 0, 0))],
        out_shape=[SDS((T, D), BF16), SDS((T, D), F32), SDS((NC, H, HEAD_DIM, HEAD_DIM), F32)],
        scratch_shapes=[pltpu.VMEM((H, HEAD_DIM, HEAD_DIM), F32)] + [pltpu.VMEM((R, D), F32)] * 4,
        name=name, compiler_params=_params("arbitrary"))(u, lb3, gn3)


def _hgrn_bwd(u, o_raw, dy, sall, lb3, layer, gn3, j, name):
    T = u.shape[0]
    D = u.shape[1] // 4
    H = D // HEAD_DIM
    NC = T // CHUNK
    CH = _chunks_per_step(NC)
    R = CH * CHUNK
    NS = NC // CH

    def body(u_ref, o_ref, dy_ref, sall_ref, lb_ref, gn_ref, du_ref, dlb_ref, dgn_ref, dst_s, b_s, lf_s, q_s, k_s, db_s, dk_s):
        step = pl.program_id(0)
        n = NS - 1 - step

        @pl.when(step == 0)
        def _():
            dst_s[...] = jnp.zeros_like(dst_s)
            dlb_ref[...] = jnp.zeros_like(dlb_ref)
            dgn_ref[...] = jnp.zeros_like(dgn_ref)

        last_row = (_row_ids((CHUNK, 1), 0) == CHUNK - 1).astype(F32)
        gn = gn_ref[...]
        heads = range(H)
        chunks = range(CH)
        cols = [slice(h * HEAD_DIM, (h + 1) * HEAD_DIM) for h in heads]
        rows = [slice(c * CHUNK, (c + 1) * CHUNK) for c in chunks]
        lv = [_live(CHUNK, (n * CH + c) * CHUNK, T) for c in chunks]
        vb, dob = {}, {}
        dgn = jnp.zeros((1, HEAD_DIM), F32)
        for c in chunks:
            for h in heads:
                q, k, v, logf, _ = _hgrn_gates(u_ref.at[rows[c]], lb_ref, h, D, lv[c])
                q_s[rows[c], cols[h]] = q
                k_s[rows[c], cols[h]] = k
                lf_s[rows[c], cols[h]] = logf
                vb[c, h] = v.astype(BF16)
                graw = u_ref[rows[c], 3 * D + h * HEAD_DIM:3 * D + (h + 1) * HEAD_DIM]
                gsig = _sigmoid(graw)
                o = o_ref[rows[c], cols[h]]
                r = lax.rsqrt(jnp.mean(o * o, axis=-1, keepdims=True) + EPS)
                xh = o * r
                dyv = dy_ref[rows[c], cols[h]]
                dsg = dyv * (graw * gsig)
                dgn = dgn + jnp.sum(dsg * xh, axis=0, keepdims=True)
                dxh = dsg * gn
                do = r * (dxh - xh * jnp.mean(dxh * xh, axis=-1, keepdims=True))
                dob[c, h] = do.astype(BF16)
                dgraw = dyv * xh * gn * (gsig * (1.0 + graw * (1.0 - gsig)))
                du_ref[rows[c], 3 * D + h * HEAD_DIM:3 * D + (h + 1) * HEAD_DIM] = (dgraw * lv[c]).astype(BF16)
        dgn_ref[...] += dgn
        for c in chunks:
            b_s[rows[c], :] = jnp.dot(_tri(True), lf_s[rows[c], :], precision=HI, preferred_element_type=F32)
        ops = {}
        for c in chunks:
            for h in heads:
                b_h = b_s.at[rows[c], cols[h]]
                b = b_h[...]
                q = q_s[rows[c], cols[h]]
                k = k_s[rows[c], cols[h]]
                blast = b_h[CHUNK - 1:CHUNK, :]
                eb = jnp.exp(b)
                ekb = jnp.exp(blast - b)
                subs = []
                for I in range(CHUNK // SUB):
                    rws, qI, KI, eI, EI, causal = _sub_parts(q, k, b, b_h, I)
                    subs.append((rws, qI.astype(BF16), KI.astype(BF16), eI, EI, causal))
                ops[c, h] = (eb, ekb, jnp.exp(blast), (q * eb).astype(BF16), (k * ekb).astype(BF16), subs)
        mm = {}
        for h in heads:
            dst = dst_s[h]
            for c in reversed(chunks):
                eb, ekb, eblast, qhb, ktb, subs = ops[c, h]
                st = sall_ref[c, h]
                dstb = dst.astype(BF16)
                dv = _dot_nt(ktb, dstb)
                dqh = jnp.dot(dob[c, h], st.astype(BF16), preferred_element_type=F32)
                dkt = jnp.dot(vb[c, h], dstb, preferred_element_type=F32)
                dblast = jnp.sum(dst * st, axis=0, keepdims=True) * eblast
                dst = dst * eblast + _dot_tn(dob[c, h], qhb)
                dp_full = _dot_nt(dob[c, h], vb[c, h])
                ps = [_dot_nt(qIb, KIb) for _, qIb, KIb, _, _, _ in subs]
                mm[c, h] = (dv, dqh, dkt, dblast, dp_full, ps)
            dst_s[h] = dst
        for c in chunks:
            for h in heads:
                eb, ekb, eblast, qhb, ktb, subs = ops[c, h]
                dv, dqh, dkt, dblast, dp_full, ps = mm[c, h]
                p = jnp.concatenate([jnp.where(sub[5], x, 0.0) for x, sub in zip(ps, subs)], axis=0).astype(BF16)
                dv = dv + _dot_tn(p, dob[c, h])
                du_ref[rows[c], 2 * D + h * HEAD_DIM:2 * D + (h + 1) * HEAD_DIM] = (dv * lv[c]).astype(BF16)
                dq = dqh * eb
                db = dqh * qhb.astype(F32)
                tmp = dkt * ktb.astype(F32)
                dk = dkt * ekb
                db = db - tmp
                dblast = dblast + jnp.sum(tmp, axis=0, keepdims=True)
                dq_parts, db_parts = [], []
                for rws, qIb, KIb, eI, EI, causal in subs:
                    dp = jnp.where(causal, dp_full[rws], 0.0).astype(BF16)
                    dqI = jnp.dot(dp, KIb, preferred_element_type=F32)
                    dKI = _dot_tn(dp, qIb)
                    dq_parts.append(dqI * eI)
                    db_parts.append(dqI * qIb.astype(F32))
                    dk = dk + dKI * EI
                    db = db - dKI * KIb.astype(F32)
                dq = dq + jnp.concatenate(dq_parts, axis=0)
                db_s[rows[c], cols[h]] = db + jnp.concatenate(db_parts, axis=0) + last_row * dblast
                dk_s[rows[c], cols[h]] = dk
                qraw = u_ref[rows[c], cols[h]]
                qsig = _sigmoid(qraw)
                du_ref[rows[c], cols[h]] = (dq * (qsig * (1.0 + qraw * (1.0 - qsig))) * lv[c]).astype(BF16)
        for c in chunks:
            lf_s[rows[c], :] = jnp.dot(_tri(False), db_s[rows[c], :], precision=HI, preferred_element_type=F32)
        for h in heads:
            lbv = lb_ref[:, cols[h]]
            dlb = jnp.zeros((1, HEAD_DIM), F32)
            for c in chunks:
                fraw = u_ref[rows[c], D + h * HEAD_DIM:D + (h + 1) * HEAD_DIM]
                sig = _sigmoid(fraw)
                forget = lbv + (1.0 - lbv) * sig
                dforget = (lf_s[rows[c], cols[h]] / forget - dk_s[rows[c], cols[h]]) * lv[c]
                dlb = dlb + jnp.sum(dforget * (1.0 - sig), axis=0, keepdims=True)
                du_ref[rows[c], D + h * HEAD_DIM:D + (h + 1) * HEAD_DIM] = (dforget * (1.0 - lbv) * sig * (1.0 - sig)).astype(BF16)
            dlb_ref[:, cols[h]] += dlb

    rev = lambda s: (NS - 1 - s, 0)
    return pl.pallas_call(
        body, grid=(NS,),
        in_specs=[pl.BlockSpec((R, 4 * D), rev), pl.BlockSpec((R, D), rev), pl.BlockSpec((R, D), rev),
                  pl.BlockSpec((CH, H, HEAD_DIM, HEAD_DIM), lambda s: (NS - 1 - s, 0, 0, 0)),
                  pl.BlockSpec((None, 1, D), lambda s: (layer, 0, 0)),
                  pl.BlockSpec((None, 1, HEAD_DIM), lambda s: (j, 0, 0))],
        out_specs=[pl.BlockSpec((R, 4 * D), rev), pl.BlockSpec((1, D), lambda s: (0, 0)),
                   pl.BlockSpec((1, HEAD_DIM), lambda s: (0, 0))],
        out_shape=[SDS((T, 4 * D), BF16), SDS((1, D), F32), SDS((1, HEAD_DIM), F32)],
        scratch_shapes=[pltpu.VMEM((H, HEAD_DIM, HEAD_DIM), F32)] + [pltpu.VMEM((R, D), F32)] * 6,
        name=name, compiler_params=_params("arbitrary"))(u, o_raw, dy, sall, lb3, gn3)


def _softmax_layers(p_ref, n_layers):
    rows = [p_ref[l:l + 1, :] for l in range(n_layers)]
    m = functools.reduce(jnp.maximum, rows)
    e = [jnp.exp(x - m) for x in rows]
    tot = functools.reduce(lambda a, b: a + b, e)
    return [x / tot for x in e]


def _lb_fwd(p):
    n_layers, D = p.shape

    def body(p_ref, o_ref):
        s = _softmax_layers(p_ref, n_layers)
        acc = jnp.zeros((1, D), F32)
        o_ref[0:1, :] = acc
        for l in range(1, n_layers):
            acc = acc + s[l]
            o_ref[l:l + 1, :] = acc

    return pl.pallas_call(body, out_shape=SDS(p.shape, F32), name="lb_fwd")(p)


def _lb_bwd(p, dlb):
    n_layers, D = p.shape

    def body(p_ref, d_ref, o_ref):
        s = _softmax_layers(p_ref, n_layers)
        ds = [jnp.zeros((1, D), F32)] * n_layers
        acc = jnp.zeros((1, D), F32)
        for l in range(n_layers - 1, 0, -1):
            acc = acc + d_ref[l:l + 1, :]
            ds[l] = acc
        dot = functools.reduce(lambda a, b: a + b, [s[l] * ds[l] for l in range(n_layers)])
        for l in range(n_layers):
            o_ref[l:l + 1, :] = s[l] * (ds[l] - dot)

    return pl.pallas_call(body, out_shape=SDS(p.shape, F32), name="lb_bwd")(p, dlb)


def _adamw_small(items):
    n = len(items)

    def body(*refs):
        for k in range(n):
            w_ref, g_ref, m_ref, v_ref = refs[4 * k:4 * k + 4]
            d_ref, mo_ref, vo_ref = refs[4 * n + 3 * k:4 * n + 3 * k + 3]
            g_ = g_ref[...]
            m_ = ADAM_B1 * m_ref[...] + (1.0 - ADAM_B1) * g_
            v_ = ADAM_B2 * v_ref[...] + (1.0 - ADAM_B2) * (g_ * g_)
            mh = m_ / (1.0 - ADAM_B1 ** ADAM_STEP)
            vh = v_ / (1.0 - ADAM_B2 ** ADAM_STEP)
            d_ref[...] = -ADAM_LR * (mh / (jnp.sqrt(vh) + ADAM_EPS) + ADAM_WD * w_ref[...])
            mo_ref[...] = m_
            vo_ref[...] = v_

    out_shape = [SDS(it[0].shape, F32) for it in items for _ in range(3)]
    res = pl.pallas_call(body, out_shape=out_shape, name="adamw_small")(*[a for it in items for a in it])
    return [res[3 * k:3 * k + 3] for k in range(n)]


def _adamw_layer(w3, m3, v3, g2, layer, outs, name):
    L, R, C = w3.shape
    tr = _tile(R, 256, 8)
    if outs is None:
        outs = tuple(lax.empty(w3.shape, F32) for _ in range(4))

    def body(w_ref, m_ref, v_ref, g_ref, a0, a1, a2, a3, go_ref, d_ref, mo_ref, vo_ref):
        del a0, a1, a2, a3
        g_ = g_ref[...]
        m_ = ADAM_B1 * m_ref[...] + (1.0 - ADAM_B1) * g_
        v_ = ADAM_B2 * v_ref[...] + (1.0 - ADAM_B2) * (g_ * g_)
        mh = m_ / (1.0 - ADAM_B1 ** ADAM_STEP)
        vh = v_ / (1.0 - ADAM_B2 ** ADAM_STEP)
        go_ref[...] = g_
        d_ref[...] = -ADAM_LR * (mh / (jnp.sqrt(vh) + ADAM_EPS) + ADAM_WD * w_ref[...])
        mo_ref[...] = m_
        vo_ref[...] = v_

    lay = pl.BlockSpec((None, tr, C), lambda i: (layer, i, 0))
    return pl.pallas_call(
        body, grid=(R // tr,), in_specs=[lay] * 3 + [pl.BlockSpec((tr, C), lambda i: (i, 0))] + [ANY_SPEC] * 4,
        out_specs=[lay] * 4, out_shape=[SDS(w3.shape, F32)] * 4, input_output_aliases={4: 0, 5: 1, 6: 2, 7: 3},
        name=name, compiler_params=_params("parallel"))(w3, m3, v3, g2, *outs)


SEM_SPEC = pl.BlockSpec(memory_space=pltpu.SEMAPHORE)
HBM_SPEC = pl.BlockSpec(memory_space=pltpu.HBM)
EFFECT = pltpu.SideEffectType.DATAFLOW_SIDE_EFFECTING
N_DEV = 2 * N_CHIPS


def _position():
    x, y, c = lax.axis_index("x"), lax.axis_index("y"), lax.axis_index("c")
    chips = [(1 - x, y), (x, 1 - y), (1 - x, 1 - y)]
    return x, y, c, chips


def _split_start(name, plan, bufs, n_sems, deps=(), earlier=None):
    n = len(bufs)
    held = () if earlier is None else tuple(earlier[1:])

    def body(*refs):
        first_out = n + len(held) + len(deps)
        if earlier is not None:
            sends, recvs = earlier[0](refs[:n], refs[n], refs[n + 1])
            for kw in sends:
                pltpu.make_async_remote_copy(**kw).wait_send()
            for kw in recvs:
                pltpu.make_async_remote_copy(**kw).wait_recv()
        sends, _ = plan(refs[:n], refs[first_out], refs[first_out + 1])
        for kw in sends:
            pltpu.make_async_remote_copy(**kw).start()
        refs[-1][...] = jnp.zeros_like(refs[-1])

    out = pl.pallas_call(
        body, name=name,
        out_shape=(pltpu.SemaphoreType.DMA((n_sems,)), pltpu.SemaphoreType.DMA((n_sems,)),
                   *[pltpu.HBM(b.shape, b.dtype) for b in bufs], SDS((8, 128), F32)),
        in_specs=[HBM_SPEC] * n + [SEM_SPEC] * len(held) + [ANY_SPEC] * len(deps),
        out_specs=(SEM_SPEC, SEM_SPEC, *[HBM_SPEC] * n, pl.BlockSpec(memory_space=pltpu.VMEM)),
        input_output_aliases={i: 2 + i for i in range(n)},
        compiler_params=pltpu.CompilerParams(has_side_effects=EFFECT),
    )(*[pltpu.with_memory_space_constraint(b, pltpu.HBM) for b in bufs], *held, *deps)
    return out[0], out[1], list(out[2:2 + n]), out[-1]


def _split_wait(name, plan, send_sems, recv_sems, bufs, after=()):
    n = len(bufs)

    def body(*refs):
        sends, recvs = plan(refs[:n], refs[n], refs[n + 1])
        for kw in sends:
            pltpu.make_async_remote_copy(**kw).wait_send()
        for kw in recvs:
            pltpu.make_async_remote_copy(**kw).wait_recv()

    out = pl.pallas_call(
        body, name=name, out_shape=tuple(pltpu.HBM(b.shape, b.dtype) for b in bufs),
        in_specs=[HBM_SPEC] * n + [SEM_SPEC, SEM_SPEC] + [ANY_SPEC] * len(after),
        out_specs=tuple([HBM_SPEC] * n), input_output_aliases={i: i for i in range(n)},
        compiler_params=pltpu.CompilerParams(has_side_effects=EFFECT),
    )(*bufs, send_sems, recv_sems, *after)
    return list(out)


def _region(kind, ref, chip, half):
    K, N = ref.shape
    if kind == "col":
        return ref.at[pl.ds(half * (K // 2), K // 2), pl.ds(chip * (N // N_CHIPS), N // N_CHIPS)]
    rows = K // (2 * N_CHIPS)
    return ref.at[pl.ds((2 * chip + half) * rows, rows), :]


def _gather_plan(kinds, over_chips, first=0):
    def plan(refs, send_sems, recv_sems):
        x, y, c, chips = _position()
        sends, recvs = [], []
        for f, (ref, kind) in enumerate(zip(refs, kinds)):
            for k, chip in enumerate(chips):
                theirs = 2 * chip[0] + chip[1]
                at = 3 * (first + f) + k
                sem = dict(send_sem=send_sems.at[at], recv_sem=recv_sems.at[at], device_id_type=MESH)
                if over_chips:
                    out, back, to = _region(kind, ref, 2 * x + y, c), _region(kind, ref, theirs, c), (*chip, c)
                else:
                    out, back, to = _region(kind, ref, theirs, c), _region(kind, ref, theirs, 1 - c), (x, y, 1 - c)
                sends.append(dict(src_ref=out, dst_ref=out, device_id=to, **sem))
                recvs.append(dict(src_ref=back, dst_ref=back, device_id=to, **sem))
        return sends, recvs
    return plan


def _reduce_plan(first=0):
    def plan(refs, send_sems, recv_sems):
        x, y, c, _ = _position()
        me = 4 * x + 2 * y + c
        sends, recvs = [], []
        for f in range(len(refs) // 2):
            acc, land = refs[2 * f], refs[2 * f + 1]
            for d in range(1, N_DEV):
                t = (me + d) % N_DEV
                to = dict(device_id=(t // 4, (t // 2) % 2, t % 2), device_id_type=MESH)
                slot = N_DEV - 1 - d
                at = first + 7 * f
                sends.append(dict(src_ref=acc.at[t % 2, t // 2], dst_ref=land.at[slot], send_sem=send_sems.at[at + d - 1],
                                  recv_sem=recv_sems.at[at + slot], **to))
                recvs.append(dict(src_ref=land.at[d - 1], dst_ref=land.at[d - 1], send_sem=send_sems.at[at + d - 1],
                                  recv_sem=recv_sems.at[at + d - 1], **to))
        return sends, recvs
    return plan


def _swap_plan(first=0):
    def plan(refs, send_sems, recv_sems):
        x, y, c, _ = _position()
        sends, recvs = [], []
        for f, g in enumerate(refs):
            sem = dict(send_sem=send_sems.at[first + f], recv_sem=recv_sems.at[first + f], device_id=(x, y, 1 - c),
                       device_id_type=MESH)
            sends.append(dict(src_ref=g.at[c], dst_ref=g.at[c], **sem))
            recvs.append(dict(src_ref=g.at[1 - c], dst_ref=g.at[1 - c], **sem))
        return sends, recvs
    return plan


def _joined(plans):
    def plan(refs, send_sems, recv_sems):
        sends, recvs, lo = [], [], 0
        for part, n in plans:
            s_, r_ = part(refs[lo:lo + n], send_sems, recv_sems)
            sends += s_
            recvs += r_
            lo += n
        return sends, recvs
    return plan


def _sum_pieces(ids2, acc, land, name):
    _, _, nr, nc = acc.shape
    tr = _tile(nr, 256, 16)

    def body(ids_ref, own_ref, land_ref, o_ref):
        del ids_ref
        s = own_ref[...].astype(F32)
        for k in range(N_DEV - 1):
            s = s + land_ref[k].astype(F32)
        o_ref[...] = s

    return pl.pallas_call(
        body,
        grid_spec=pltpu.PrefetchScalarGridSpec(
            num_scalar_prefetch=1, grid=(nr // tr,),
            in_specs=[pl.BlockSpec((None, None, tr, nc), lambda i, ids: (ids[0], ids[1], i, 0)),
                      pl.BlockSpec((N_DEV - 1, tr, nc), lambda i, ids: (0, i, 0))],
            out_specs=pl.BlockSpec((None, tr, nc), lambda i, ids: (ids[0], i, 0))),
        out_shape=SDS((2, nr, nc), F32), name=name, compiler_params=_params("parallel"))(ids2, acc, land)


def _small_plan(refs, send_sems, recv_sems):
    x, y, c, _ = _position()
    me = 4 * x + 2 * y + c
    own, land = refs
    sends, recvs = [], []
    for d in range(1, N_DEV):
        t = (me + d) % N_DEV
        to = dict(device_id=(t // 4, (t // 2) % 2, t % 2), device_id_type=MESH)
        sends.append(dict(src_ref=own, dst_ref=land.at[me], send_sem=send_sems.at[d - 1],
                          recv_sem=recv_sems.at[N_DEV - 1 - d], **to))
        recvs.append(dict(src_ref=land.at[t], dst_ref=land.at[t], send_sem=send_sems.at[d - 1],
                          recv_sem=recv_sems.at[d - 1], **to))
    return sends, recvs


def _sum_blocks(me1, own, land):
    def body(me_ref, own_ref, land_ref, o_ref):
        acc = None
        for d in range(N_DEV):
            term = jnp.where(me_ref[0] == d, own_ref[...], land_ref[d])
            acc = term if acc is None else acc + term
        o_ref[...] = acc

    return pl.pallas_call(
        body,
        grid_spec=pltpu.PrefetchScalarGridSpec(
            num_scalar_prefetch=1, grid=(1,),
            in_specs=[pl.BlockSpec(own.shape, lambda i, me: (0, 0)), pl.BlockSpec(land.shape, lambda i, me: (0, 0, 0))],
            out_specs=pl.BlockSpec(own.shape, lambda i, me: (0, 0))),
        out_shape=SDS(own.shape, F32), name="sum_small", compiler_params=_params("arbitrary"))(me1, own, land)


BIG = {"ev_w_in": "col", "ev_w_out": "row", "od_w_in": "col", "od_w_out": "row", "mlp_w1": "col", "mlp_w2": "row"}
WEIGHTS = ("meta_tokens", "mix_norm_g", "mlp_norm_g", "final_norm_g", "ev_w_in", "ev_conv_w", "ev_conv_b", "ev_ln_g",
           "ev_ln_b", "ev_pool_w", "ev_pool_b", "ev_pool_scale", "ev_w_out", "od_w_in", "od_gnorm_g", "od_w_out",
           "lb_param", "mlp_w1", "mlp_w2")
PACK_UNIT = 1024


def _mixer_names(layer):
    return ("ev_w_in", "ev_w_out") if layer % 2 == 0 else ("od_w_in", "od_w_out")


def _pack(arrays):
    flat = []
    for a in arrays:
        a = a.reshape(-1)
        flat.append(jnp.pad(a, (0, (-a.shape[0]) % PACK_UNIT)))
    return jnp.concatenate(flat).reshape(-1, 128)


def _unpack(packed, shapes):
    flat = packed.reshape(-1)
    out, off = [], 0
    for s in shapes:
        size = 1
        for d in s:
            size *= d
        out.append(flat[off:off + size].reshape(s))
        off += size + (-size) % PACK_UNIT
    return out


def _local_step(x2, target, P, weights, boundary, first_deps=()):
    D = x2.shape[1]
    n_layers = P["mix_norm_g"].shape[0]
    h = jnp.concatenate([jnp.zeros((PAD, D), F32), P["meta_full"], x2], axis=0)
    mix_g = P["mix_norm_g"].reshape(n_layers, 1, D)
    mlp_g = P["mlp_norm_g"].reshape(n_layers, 1, D)
    vec = lambda a: a.reshape(a.shape[0], 1, -1)
    cb3, lg3, lnb3, ps3 = vec(P["ev_conv_b"]), vec(P["ev_ln_g"]), vec(P["ev_ln_b"]), vec(P["ev_pool_scale"])
    pb3 = vec(P["ev_pool_b"])
    gn3 = vec(P["od_gnorm_g"])
    lb_all = _lb_fwd(P["lb_param"])
    lb3 = lb_all.reshape(n_layers, 1, D)
    even = (cb3, lg3, lnb3, P["ev_pool_w"], pb3, ps3)

    saved = []
    deps = tuple(first_deps)
    for layer in range(n_layers):
        j = layer // 2
        w_in, w_out = _mixer_names(layer)
        W = {}
        s = {"h": h, "W": W}
        s["n"] = _rms_fwd(h, mix_g, layer, "mix_norm_0", deps=deps) if layer == 0 else n_next
        deps = ()
        W[w_in], held = weights(layer, w_in, (s["n"],))
        s["u"] = _mm_nn(s["n"], W[w_in], 0, f"mix_in_{layer}", deps=held)
        if layer % 2 == 0:
            s["y"], s["yc"] = _even_fwd(s["u"], P["conv_w_full"], *even, j, f"even_fwd_{layer}")
        else:
            s["y"], s["o"], s["sall"] = _hgrn_fwd(s["u"], lb3, layer, gn3, j, f"hgrn_fwd_{layer}")
        W[w_out], held = weights(layer, w_out, (s["y"],))
        if layer == 0:
            h, s["n2"] = _mm_nn_norm(s["y"], W[w_out], 0, h, mlp_g, layer, "mix_out_0", deps=held)
            s["h1"] = h
            W["mlp_w1"], held = weights(layer, "mlp_w1", (s["n2"],))
            s["relu"] = _mm_nn(s["n2"], W["mlp_w1"], 0, "mlp_up_0", relu=True, deps=held)
            W["mlp_w2"], held = weights(layer, "mlp_w2", (s["relu"],))
            h, n_next = _mm_nn_norm(s["relu"], W["mlp_w2"], 0, h, mix_g, 1, "mlp_down_0", square=True, deps=held)
        else:
            W["mlp_w1"], more1 = weights(layer, "mlp_w1", (s["y"],))
            W["mlp_w2"], more2 = weights(layer, "mlp_w2", (s["y"],))
            last = layer + 1 == n_layers
            out = _tail_fwd(s["y"], W[w_out], h, mlp_g, layer, W["mlp_w1"], W["mlp_w2"], None if last else mix_g,
                            f"tail_{layer}", deps=held + more1 + more2)
            s["h1"], s["n2"], h, s["relu"] = out[0], out[1], out[2], out[-1]
            n_next = None if last else out[3]
        saved.append(s)

    dh, dhb, dg_final, loss = _final(h, P["final_norm_g"].reshape(1, D), target)

    small = {"final_norm_g": dg_final}
    per_layer = {k: [None] * n_layers for k in ("mix_norm_g", "mlp_norm_g", "lb")}
    per_pair = {k: [None] * (n_layers // 2) for k in
                ("ev_conv_w", "ev_conv_b", "ev_ln_g", "ev_ln_b", "ev_pool_w", "ev_pool_b", "ev_pool_scale", "od_gnorm_g")}
    for layer in reversed(range(n_layers)):
        j = layer // 2
        s = saved[layer]
        W = s["W"]
        w_in, w_out = _mixer_names(layer)
        dw2 = _mm_tn(s["relu"], dhb, "row", f"dw2_{layer}", square=True)
        dz, dh, dhb, per_layer["mlp_norm_g"][layer] = _mlp_bwd(
            dhb, s["relu"], W["mlp_w1"], W["mlp_w2"], s["h1"], mlp_g, layer, dh, f"mlp_bwd_{layer}", deps=deps + (dw2,))
        dw1 = _mm_tn(s["n2"], dz, "col", f"dw1_{layer}")
        deps = boundary(f"mlp{layer}", {("mlp_w1", layer): dw1, ("mlp_w2", layer): dw2}, (dhb, dw1, dw2))
        dy = _mm_nt(dhb, W[w_out], 0, f"d_y_{layer}", deps=deps)
        dwout = _mm_tn(s["y"], dhb, "row", f"dwout_{layer}")
        if layer % 2 == 0:
            du, dcw, dcb, dlg, dlnb, dpw, dpb, dps = _even_bwd(s["u"], s["yc"], dy, P["conv_w_full"], *even, j, f"even_bwd_{layer}")
            for k, val in (("ev_conv_w", dcw), ("ev_conv_b", dcb), ("ev_ln_g", dlg), ("ev_ln_b", dlnb),
                           ("ev_pool_w", dpw), ("ev_pool_b", dpb), ("ev_pool_scale", dps)):
                per_pair[k][j] = val
        else:
            du, per_layer["lb"][layer], per_pair["od_gnorm_g"][j] = _hgrn_bwd(
                s["u"], s["o"], dy, s["sall"], lb3, layer, gn3, j, f"hgrn_bwd_{layer}")
        dwin = _mm_tn(s["n"], du, "col", f"dwin_{layer}")
        deps = boundary(f"mix{layer}", {(w_in, j): dwin, (w_out, j): dwout}, (du, dwin, dwout))
        dh, dhb, per_layer["mix_norm_g"][layer] = _mm_nt_norm(du, W[w_in], 0, s["h"], mix_g, layer, dh, f"d_n_{layer}", deps=deps)
        deps = ()

    small["mix_norm_g"] = jnp.concatenate(per_layer["mix_norm_g"], axis=0)
    small["mlp_norm_g"] = jnp.concatenate(per_layer["mlp_norm_g"], axis=0)
    dlb_all = jnp.concatenate([jnp.zeros((1, D), F32) if g is None else g for g in per_layer["lb"]], axis=0)
    small["lb_param"] = _lb_bwd(P["lb_param"], dlb_all)
    for k, vals in per_pair.items():
        small[k] = jnp.stack(vals, axis=0)
    small["meta_tokens"] = dh[PAD:LEAD]
    return loss, dh, small


def kernel(x, meta_tokens, mix_norm_g, mlp_norm_g, final_norm_g, ev_w_in, ev_conv_w, ev_conv_b, ev_ln_g, ev_ln_b, ev_pool_w, ev_pool_b, ev_pool_scale, ev_w_out, od_w_in, od_gnorm_g, od_w_out, lb_param, mlp_w1, mlp_w2, loss_target, m_meta_tokens, m_mix_norm_g, m_mlp_norm_g, m_final_norm_g, m_ev_w_in, m_ev_conv_w, m_ev_conv_b, m_ev_ln_g, m_ev_ln_b, m_ev_pool_w, m_ev_pool_b, m_ev_pool_scale, m_ev_w_out, m_od_w_in, m_od_gnorm_g, m_od_w_out, m_lb_param, m_mlp_w1, m_mlp_w2, v_meta_tokens, v_mix_norm_g, v_mlp_norm_g, v_final_norm_g, v_ev_w_in, v_ev_conv_w, v_ev_conv_b, v_ev_ln_g, v_ev_ln_b, v_ev_pool_w, v_ev_pool_b, v_ev_pool_scale, v_ev_w_out, v_od_w_in, v_od_gnorm_g, v_od_w_out, v_lb_param, v_mlp_w1, v_mlp_w2):
    given = dict(locals())
    w = {n: given[n] for n in WEIGHTS}
    m = {n: given["m_" + n] for n in WEIGHTS}
    v = {n: given["v_" + n] for n in WEIGHTS}
    n_layers = mix_norm_g.shape[0]
    core = lax.axis_index("c").astype(jnp.int32)
    chip = (2 * lax.axis_index("x") + lax.axis_index("y")).astype(jnp.int32)
    chip1 = chip.reshape(1)
    ids2 = jnp.stack([core, chip])

    conv_pad = jnp.pad(ev_conv_w, ((0, 0), (0, CONV_ROWS - CONV_WIDTH), (0, 0)))
    stages = [[(0, n)] for n in (*_mixer_names(0), "mlp_w1", "mlp_w2")]
    for layer in range(1, n_layers):
        stages += [[(layer, n) for n in _mixer_names(layer)], [(layer, "mlp_w1"), (layer, "mlp_w2")]]
    where, stage_kinds, stage_bufs = {}, [], []
    for k, stage in enumerate(stages):
        index = [layer if n.startswith("mlp") else layer // 2 for layer, n in stage]
        kinds = [BIG[n] for _, n in stage]
        bufs = [_cast_place(w[n], i, BIG[n], chip1, BF16, f"place_{n}_{i}") for (_, n), i in zip(stage, index)]
        if k == 0:
            bufs.append(_cast_place(meta_tokens[None], 0, "col", chip1, F32, "place_meta"))
            bufs.append(_cast_place(conv_pad.reshape(1, -1, conv_pad.shape[2]), 0, "col", chip1, F32, "place_conv_w"))
            kinds += ["col", "col"]
        stage_kinds.append(kinds)
        stage_bufs.append(bufs)
        where.update({key: (k, f) for f, key in enumerate(stage)})
    gathers, token, early = [], (), 3
    for lo, hi, name in ((0, early, "gather_start_first"), (early, len(stages), "gather_start_rest")):
        every = [b for bufs in stage_bufs[lo:hi] for b in bufs]
        kinds_all = [kd for kinds in stage_kinds[lo:hi] for kd in kinds]
        ss, rs, every, tok = _split_start(name, _gather_plan(kinds_all, True), every, 3 * len(every), deps=token)
        token = (tok,)
        at = 0
        for kinds in stage_kinds[lo:hi]:
            gathers.append((kinds, _gather_plan(kinds, True, first=at), ss, rs, every[at:at + len(kinds)]))
            at += len(kinds)

    landed, passed, held = {}, {}, []

    def hand_on(k, deps):
        if k not in passed:
            kinds, plan, ss, rs, bufs = gathers[k]
            to_sibling = _gather_plan(kinds, False)
            ss, rs, bufs, tok = _split_start(f"gather_pass_{k}", to_sibling, bufs, 3 * len(bufs), deps=deps, earlier=(plan, ss, rs))
            passed[k] = (to_sibling, ss, rs, bufs)
            held.append(tok)

    def arrived(k, after):
        if k not in landed:
            hand_on(k, after)
            landed[k] = _split_wait(f"gather_wait_{k}", *passed[k], after)
        return landed[k]

    def weights(layer, name, after):
        k, f = where[(layer, name)]
        full = arrived(k, after)[f][None]
        if name == "mlp_w2" and layer + 1 < n_layers:
            hand_on(where[(layer + 1, _mixer_names(layer + 1)[0])][0], after)
        if layer > 0 and name == _mixer_names(layer)[0]:
            hand_on(where[(layer, "mlp_w1")][0], after)
        tokens = tuple(held)
        held.clear()
        return full, tokens

    first = arrived(0, token)
    P = {n: w[n] for n in ("mix_norm_g", "mlp_norm_g", "final_norm_g", "ev_conv_b", "ev_ln_g", "ev_ln_b", "ev_pool_w",
                           "ev_pool_b", "ev_pool_scale", "od_gnorm_g", "lb_param")}
    P["meta_full"] = first[1]
    P["conv_w_full"] = first[2].reshape(ev_conv_w.shape[0], CONV_ROWS, -1)

    pending, outs = [], {n: None for n in BIG}

    def advance(after, fresh=1):
        ready, still = [], []
        for pos, st in enumerate(pending):
            if st["phase"] == 1 and pos >= len(pending) - fresh:
                still.append(st)
            elif st["phase"] == 1:
                bufs = _split_wait(f"reduce_wait_{st['tag']}", st["plan"], st["ss"], st["rs"], st["bufs"], after)
                halves = [_sum_pieces(ids2, bufs[2 * f], bufs[2 * f + 1], f"sum_{st['tag']}_{f}") for f in range(len(bufs) // 2)]
                ready.append((st, halves))
            else:
                grads = _split_wait(f"swap_wait_{st['tag']}", st["plan"], st["ss"], st["rs"], st["bufs"], after)
                for (n, i), g in zip(st["keys"], grads):
                    outs[n] = _adamw_layer(w[n], m[n], v[n], g.reshape(w[n].shape[1:]), i, outs[n], f"adamw_{n}_{i}")
        pending[:] = still
        return ready

    def launch(name, ready, tag=None, grads=None):
        bufs, parts, entries, at = [], [], [], 0
        for st, halves in ready:
            plan = _swap_plan(first=at)
            entries.append((dict(st, phase=2, plan=plan), len(bufs), len(halves)))
            parts.append((plan, len(halves)))
            bufs += halves
            at += len(halves)
        if grads is not None:
            pairs = []
            for acc in grads.values():
                pairs += [acc, lax.empty((N_DEV - 1,) + acc.shape[2:], BF16)]
            plan = _reduce_plan(first=at)
            entries.append((dict(phase=1, tag=tag, keys=list(grads), plan=plan), len(bufs), len(pairs)))
            parts.append((plan, len(pairs)))
            bufs += pairs
            at += 7 * len(grads)
        if not bufs:
            return ()
        ss, rs, bufs, tok = _split_start(name, _joined(parts), bufs, at)
        for st, lo, n in entries:
            pending.append(dict(st, ss=ss, rs=rs, bufs=bufs[lo:lo + n]))
        return (tok,)

    def boundary(tag, grads, after):
        return launch(f"start_{tag}", advance(after), tag, grads)

    loss, dh, small = _local_step(x[0], loss_target[0], P, weights, boundary, first_deps=token)

    order = [n for n in WEIGHTS if n not in BIG]
    block = _pack([small[n] for n in order] + [loss])
    ss, rs, bufs, tok = _split_start("small_start", _small_plan, [block, lax.empty((N_DEV,) + block.shape, F32)], N_DEV - 1)
    for last in range(3):
        launch(f"start_end_{last}", advance((tok,) + tuple(o[0] for o in outs.values() if o is not None), fresh=0))
    assert not pending
    block, land = _split_wait("small_wait", _small_plan, ss, rs, bufs, tuple(outs[n][0] for n in BIG))
    packed = _sum_blocks((4 * lax.axis_index("x") + 2 * lax.axis_index("y") + lax.axis_index("c")).astype(jnp.int32).reshape(1), block, land)
    total = _unpack(packed, [small[n].shape for n in order] + [loss.shape])
    loss_sum = total[-1][0, 0]
    gsmall = dict(zip(order, total[:-1]))
    gsmall["meta_tokens"] = lax.dynamic_slice_in_dim(gsmall["meta_tokens"], chip * meta_tokens.shape[1], meta_tokens.shape[1], 1)
    gsmall["ev_conv_w"] = lax.dynamic_slice_in_dim(gsmall["ev_conv_w"][:, :CONV_WIDTH], chip * ev_conv_w.shape[2], ev_conv_w.shape[2], 2)

    g_out, d_out, m_out, v_out = {}, {}, {}, {}
    for n in BIG:
        g_out[n], d_out[n], m_out[n], v_out[n] = outs[n]
    items = []
    for n in order:
        cols = w[n].shape[-1] if w[n].ndim > 1 else 128
        items.append([a.reshape(-1, cols) for a in (w[n], gsmall[n], m[n], v[n])])
    for n, (d_, m_, v_) in zip(order, _adamw_small(items)):
        shape = w[n].shape
        g_out[n], d_out[n], m_out[n], v_out[n] = gsmall[n].reshape(shape), d_.reshape(shape), m_.reshape(shape), v_.reshape(shape)

    grad_x = dh[LEAD:][None]
    return (loss_sum, grad_x, *[g_out[n] for n in WEIGHTS], *[d_out[n] for n in WEIGHTS],
            *[m_out[n] for n in WEIGHTS], *[v_out[n] for n in WEIGHTS])
```

```python
import functools

import jax
import jax.numpy as jnp
from jax import lax
from jax.experimental import pallas as pl
from jax.experimental.pallas import tpu as pltpu

F32 = jnp.float32
BF16 = jnp.bfloat16
SDS = jax.ShapeDtypeStruct
MESH = pl.DeviceIdType.MESH
ANY_SPEC = pl.BlockSpec(memory_space=pl.ANY)

N_META = 16
CHUNK = 64
LEAD = CHUNK
PAD = LEAD - N_META
CONV_WIDTH = 31
CONV_ROWS = 32
POOL_WINDOWS = (2, 4, 8, 16)
HEAD_DIM = 128
SUB = 16
EXP_CAP = 80.0
EPS = 1e-6
ADAM_LR = 0.001
ADAM_B1 = 0.9
ADAM_B2 = 0.999
ADAM_EPS = 1e-08
ADAM_WD = 0.01
ADAM_STEP = 10
N_CHIPS = 4
VMEM_LIMIT = 58 << 20
MM_VMEM_BUDGET = 50 << 20


def _params(*sem):
    return pltpu.CompilerParams(dimension_semantics=sem if sem else None, vmem_limit_bytes=VMEM_LIMIT)


def _tile(n, target, unit=CHUNK):
    best = None
    for t in range(unit, min(n, target) + 1, unit):
        if n % t == 0:
            best = t
    assert best is not None, (n, target, unit)
    return best


def _ctile(n, target=512):
    for t in (512, 384, 256, 128):
        if t <= target and n % t == 0:
            return t
    raise ValueError(n)


def _mm_tiles(M, N, per_row, per_col, per_elem):
    best = None
    for tn in (512, 384, 256, 128):
        if N % tn:
            continue
        for tm in sorted((d for d in range(16, M + 1, 16) if M % d == 0), reverse=True):
            if 2 * (tm * per_row + tn * per_col + tm * tn * per_elem) <= MM_VMEM_BUDGET:
                if best is None or tm * tn > best[0] * best[1]:
                    best = (tm, tn)
                break
    assert best is not None, (M, N)
    return best


def _sigmoid(x):
    return 1.0 / (1.0 + jnp.exp(-x))


def _mult(v, m):
    return v if isinstance(v, int) else pl.multiple_of(v, m)


def _row_ids(shape, base):
    return lax.broadcasted_iota(jnp.int32, shape, 0) + base


def _cast_place(w3, layer, kind, chip1, dtype, name):
    _, ks, ns = w3.shape
    tr = _tile(ks, 512, 16)
    full = (ks, ns * N_CHIPS) if kind == "col" else (ks * N_CHIPS, ns)

    def body(chip_ref, w_ref, o_ref):
        del chip_ref
        o_ref[...] = w_ref[...].astype(dtype)

    omap = (lambda i, chip: (i, chip[0])) if kind == "col" else (lambda i, chip: (chip[0] * (ks // tr) + i, 0))
    return pl.pallas_call(
        body,
        grid_spec=pltpu.PrefetchScalarGridSpec(
            num_scalar_prefetch=1, grid=(ks // tr,),
            in_specs=[pl.BlockSpec((None, tr, ns), lambda i, chip: (layer, i, 0))],
            out_specs=pl.BlockSpec((tr, ns), omap)),
        out_shape=SDS(full, dtype), name=name, compiler_params=_params("parallel"))(chip1, w3)


def _rms_fwd(h, g3, layer, name, deps=()):
    T, D = h.shape
    tm = _tile(T, 832)

    def body(h_ref, g_ref, *rest):
        n_ref = rest[-1]
        x = h_ref[...]
        r = lax.rsqrt(jnp.mean(x * x, axis=-1, keepdims=True) + EPS)
        n_ref[...] = ((x * r) * g_ref[...]).astype(BF16)

    return pl.pallas_call(
        body, grid=(T // tm,),
        in_specs=[pl.BlockSpec((tm, D), lambda i: (i, 0)), pl.BlockSpec((None, 1, D), lambda i: (layer, 0, 0))]
        + [ANY_SPEC] * len(deps),
        out_specs=pl.BlockSpec((tm, D), lambda i: (i, 0)), out_shape=SDS((T, D), BF16),
        name=name, compiler_params=_params("parallel"))(h, g3, *deps)


def _final(h, g2, target):
    T, D = h.shape
    tm = _tile(T, 320)
    nsub = tm // CHUNK
    nblk = target.shape[0] // CHUNK

    def body(h_ref, g_ref, *rest):
        t_refs = rest[:nsub]
        dh_ref, dhb_ref, dg_ref, loss_ref = rest[nsub:]
        i = pl.program_id(0)

        @pl.when(i == 0)
        def _():
            dg_ref[...] = jnp.zeros_like(dg_ref)
            loss_ref[...] = jnp.zeros_like(loss_ref)

        g = g_ref[...]
        for q in range(nsub):
            rows = slice(q * CHUNK, (q + 1) * CHUNK)
            x = h_ref[rows, :]
            r = lax.rsqrt(jnp.mean(x * x, axis=-1, keepdims=True) + EPS)
            xh = x * r
            live = jnp.where(i * nsub + q > 0, 1.0, 0.0).astype(F32)
            e = ((xh * g) - t_refs[q][...]) * live
            dy = e * (1.0 / D)
            dxh = dy * g
            dh = r * (dxh - xh * jnp.mean(dxh * xh, axis=-1, keepdims=True))
            dh_ref[rows, :] = dh
            dhb_ref[rows, :] = dh.astype(BF16)
            dg_ref[...] += jnp.sum(dy * xh, axis=0, keepdims=True)
            loss_ref[...] += jnp.sum(e * e) * (0.5 / D)

    row = pl.BlockSpec((tm, D), lambda i: (i, 0))
    t_specs = [pl.BlockSpec((CHUNK, D), functools.partial(lambda i, q: (jnp.clip(i * nsub + q - 1, 0, nblk - 1), 0), q=q))
               for q in range(nsub)]
    return pl.pallas_call(
        body, grid=(T // tm,),
        in_specs=[row, pl.BlockSpec((1, D), lambda i: (0, 0))] + t_specs,
        out_specs=[row, row, pl.BlockSpec((1, D), lambda i: (0, 0)), pl.BlockSpec((1, 128), lambda i: (0, 0))],
        out_shape=[SDS((T, D), F32), SDS((T, D), BF16), SDS((1, D), F32), SDS((1, 128), F32)],
        name="final_loss", compiler_params=_params("arbitrary"))(h, g2, *([target] * nsub))


def _mm_nn(a, w3, layer, name, res=None, relu=False, square=False, deps=()):
    M, K = a.shape
    N = w3.shape[2]
    tm, tn = _mm_tiles(M, N, 2 * K, 2 * K, (2 if relu else 4) + (4 if res is not None else 0))

    def body(*refs):
        lhs = refs[0][...]
        acc = jnp.dot(lhs * lhs if square else lhs, refs[1][...], preferred_element_type=F32)
        if res is not None:
            acc = acc + refs[2][...]
        refs[-1][...] = jnp.maximum(acc, 0.0).astype(BF16) if relu else acc

    in_specs = [pl.BlockSpec((tm, K), lambda i, j: (i, 0)), pl.BlockSpec((None, K, tn), lambda i, j: (layer, 0, j))]
    args = [a, w3]
    tile = pl.BlockSpec((tm, tn), lambda i, j: (i, j))
    if res is not None:
        in_specs.append(tile)
        args.append(res)
    in_specs += [ANY_SPEC] * len(deps)
    args += list(deps)
    return pl.pallas_call(
        body, grid=(M // tm, N // tn), in_specs=in_specs, out_specs=tile,
        out_shape=SDS((M, N), BF16 if relu else F32),
        name=name, compiler_params=_params("parallel", "parallel"))(*args)


def _mm_nt(dy, w3, layer, name, relu=None, deps=()):
    M, N = dy.shape
    K = w3.shape[1]
    tm, tk = _mm_tiles(M, K, 2 * N, 2 * N, 4)

    def body(*refs):
        acc = lax.dot_general(refs[0][...], refs[1][...], (((1,), (1,)), ((), ())), preferred_element_type=F32)
        if relu is not None:
            acc = (acc * (2.0 * refs[2][...].astype(F32))).astype(BF16)
        refs[-1][...] = acc

    tile = pl.BlockSpec((tm, tk), lambda i, j: (i, j))
    in_specs = [pl.BlockSpec((tm, N), lambda i, j: (i, 0)), pl.BlockSpec((None, tk, N), lambda i, j: (layer, j, 0))]
    args = [dy, w3]
    if relu is not None:
        in_specs.append(tile)
        args.append(relu)
    in_specs += [ANY_SPEC] * len(deps)
    args += list(deps)
    return pl.pallas_call(
        body, grid=(M // tm, K // tk), in_specs=in_specs, out_specs=tile,
        out_shape=SDS((M, K), F32 if relu is None else BF16),
        name=name, compiler_params=_params("parallel", "parallel"))(*args)


def _row_tile(M, per_row, fixed):
    for tm in sorted((d for d in range(16, M + 1, 16) if M % d == 0), reverse=True):
        if 2 * (tm * per_row + fixed) <= MM_VMEM_BUDGET:
            return tm
    raise ValueError((M, per_row, fixed))


def _mm_nn_norm(a, w3, layer, res, g3, glayer, name, square=False, deps=()):
    M, K = a.shape
    D = w3.shape[2]
    tm = _row_tile(M, 2 * K + 10 * D, 2 * K * D)

    def body(a_ref, w_ref, r_ref, g_ref, *rest):
        h_ref, n_ref = rest[-2:]
        lhs = a_ref[...]
        x = r_ref[...] + jnp.dot(lhs * lhs if square else lhs, w_ref[...], preferred_element_type=F32)
        h_ref[...] = x
        r = lax.rsqrt(jnp.mean(x * x, axis=-1, keepdims=True) + EPS)
        n_ref[...] = ((x * r) * g_ref[...]).astype(BF16)

    row = pl.BlockSpec((tm, D), lambda i: (i, 0))
    return pl.pallas_call(
        body, grid=(M // tm,),
        in_specs=[pl.BlockSpec((tm, K), lambda i: (i, 0)), pl.BlockSpec((None, K, D), lambda i: (layer, 0, 0)), row,
                  pl.BlockSpec((None, 1, D), lambda i: (glayer, 0, 0))] + [ANY_SPEC] * len(deps),
        out_specs=[row, row], out_shape=[SDS((M, D), F32), SDS((M, D), BF16)],
        name=name, compiler_params=_params("parallel"))(a, w3, res, g3, *deps)


def _tail_fwd(y, w_out, res, mlp_g3, layer, w1, w2, next_g3, name, deps=()):
    M, K = y.shape
    D = w_out.shape[2]
    F = w1.shape[2]
    hb = _ctile(F)
    more = next_g3 is not None
    tm = _row_tile(M, 2 * K + 18 * D + (2 * D if more else 0) + 2 * F, 2 * K * D + 2 * D * F)

    def body(y_ref, wo_ref, res_ref, g_ref, w1_ref, w2_ref, *rest):
        outs = rest[-5:] if more else rest[-4:]
        h1 = res_ref[...] + jnp.dot(y_ref[...], wo_ref[...], preferred_element_type=F32)
        outs[0][...] = h1
        n2 = ((h1 * lax.rsqrt(jnp.mean(h1 * h1, axis=-1, keepdims=True) + EPS)) * g_ref[...]).astype(BF16)
        outs[1][...] = n2
        acc = h1
        for jb in range(F // hb):
            cols = slice(jb * hb, (jb + 1) * hb)
            r = jnp.maximum(jnp.dot(n2, w1_ref[:, cols], preferred_element_type=F32), 0.0).astype(BF16)
            outs[-1][:, cols] = r
            acc = acc + jnp.dot(r * r, w2_ref[cols, :], preferred_element_type=F32)
        outs[2][...] = acc
        if more:
            outs[3][...] = ((acc * lax.rsqrt(jnp.mean(acc * acc, axis=-1, keepdims=True) + EPS)) * rest[0][...]).astype(BF16)

    row = pl.BlockSpec((tm, D), lambda i: (i, 0))
    once = dict(pipeline_mode=pl.Buffered(1))
    in_specs = [pl.BlockSpec((tm, K), lambda i: (i, 0)), pl.BlockSpec((None, K, D), lambda i: (0, 0, 0), **once), row,
                pl.BlockSpec((None, 1, D), lambda i: (layer, 0, 0)),
                pl.BlockSpec((None, D, F), lambda i: (0, 0, 0), **once), pl.BlockSpec((None, F, D), lambda i: (0, 0, 0), **once)]
    args = [y, w_out, res, mlp_g3, w1, w2]
    out_specs, out_shape = [row, row, row], [SDS((M, D), F32), SDS((M, D), BF16), SDS((M, D), F32)]
    if more:
        in_specs.append(pl.BlockSpec((None, 1, D), lambda i: (layer + 1, 0, 0)))
        args.append(next_g3)
        out_specs.append(row)
        out_shape.append(SDS((M, D), BF16))
    out_specs.append(pl.BlockSpec((tm, F), lambda i: (i, 0)))
    out_shape.append(SDS((M, F), BF16))
    in_specs += [ANY_SPEC] * len(deps)
    args += list(deps)
    return pl.pallas_call(
        body, grid=(M // tm,), in_specs=in_specs, out_specs=out_specs, out_shape=out_shape,
        name=name, compiler_params=_params("parallel"))(*args)


def _mlp_bwd(dhb, relu, w1, w2, h, g3, glayer, dh_in, name, deps=()):
    M, D = dhb.shape
    F = w1.shape[2]
    hb = _ctile(F)
    tm = _row_tile(M, 16 * D + 4 * F, 2 * D * F)

    def body(dy_ref, r_ref, w1_ref, w2_ref, h_ref, g_ref, dhi_ref, *rest):
        dz_ref, dh_ref, dhb_ref, dg_ref = rest[-4:]
        dy = dy_ref[...]
        dn = jnp.zeros((tm, D), F32)
        for jb in range(F // hb):
            cols = slice(jb * hb, (jb + 1) * hb)
            dact = lax.dot_general(dy, w2_ref[cols, :], (((1,), (1,)), ((), ())), preferred_element_type=F32)
            dz = (dact * (2.0 * r_ref[:, cols].astype(F32))).astype(BF16)
            dz_ref[:, cols] = dz
            dn = dn + lax.dot_general(dz, w1_ref[:, cols], (((1,), (1,)), ((), ())), preferred_element_type=F32)
        x = h_ref[...]
        r = lax.rsqrt(jnp.mean(x * x, axis=-1, keepdims=True) + EPS)
        xh = x * r
        dxh = dn * g_ref[...]
        dh = dhi_ref[...] + r * (dxh - xh * jnp.mean(dxh * xh, axis=-1, keepdims=True))
        dh_ref[...] = dh
        dhb_ref[...] = dh.astype(BF16)

        @pl.when(pl.program_id(0) == 0)
        def _():
            dg_ref[...] = jnp.zeros_like(dg_ref)

        dg_ref[...] += jnp.sum(dn * xh, axis=0, keepdims=True)

    row = pl.BlockSpec((tm, D), lambda i: (i, 0))
    wide = pl.BlockSpec((tm, F), lambda i: (i, 0))
    once = dict(pipeline_mode=pl.Buffered(1))
    return pl.pallas_call(
        body, grid=(M // tm,),
        in_specs=[row, wide, pl.BlockSpec((None, D, F), lambda i: (0, 0, 0), **once),
                  pl.BlockSpec((None, F, D), lambda i: (0, 0, 0), **once), row,
                  pl.BlockSpec((None, 1, D), lambda i: (glayer, 0, 0)), row] + [ANY_SPEC] * len(deps),
        out_specs=[wide, row, row, pl.BlockSpec((1, D), lambda i: (0, 0))],
        out_shape=[SDS((M, F), BF16), SDS((M, D), F32), SDS((M, D), BF16), SDS((1, D), F32)],
        name=name, compiler_params=_params("arbitrary"))(dhb, relu, w1, w2, h, g3, dh_in, *deps)


def _mm_nt_norm(dy, w3, layer, h, g3, glayer, dh_in, name, deps=()):
    M, N = dy.shape
    D = w3.shape[1]
    tm = _row_tile(M, 2 * N + 14 * D, 2 * N * D)

    def body(dy_ref, w_ref, h_ref, g_ref, dhi_ref, *rest):
        dh_ref, dhb_ref, dg_ref = rest[-3:]
        dn = lax.dot_general(dy_ref[...], w_ref[...], (((1,), (1,)), ((), ())), preferred_element_type=F32)
        x = h_ref[...]
        r = lax.rsqrt(jnp.mean(x * x, axis=-1, keepdims=True) + EPS)
        xh = x * r
        dxh = dn * g_ref[...]
        dh = dhi_ref[...] + r * (dxh - xh * jnp.mean(dxh * xh, axis=-1, keepdims=True))
        dh_ref[...] = dh
        dhb_ref[...] = dh.astype(BF16)

        @pl.when(pl.program_id(0) == 0)
        def _():
            dg_ref[...] = jnp.zeros_like(dg_ref)

        dg_ref[...] += jnp.sum(dn * xh, axis=0, keepdims=True)

    row = pl.BlockSpec((tm, D), lambda i: (i, 0))
    return pl.pallas_call(
        body, grid=(M // tm,),
        in_specs=[pl.BlockSpec((tm, N), lambda i: (i, 0)), pl.BlockSpec((None, D, N), lambda i: (layer, 0, 0)), row,
                  pl.BlockSpec((None, 1, D), lambda i: (glayer, 0, 0)), row] + [ANY_SPEC] * len(deps),
        out_specs=[row, row, pl.BlockSpec((1, D), lambda i: (0, 0))],
        out_shape=[SDS((M, D), F32), SDS((M, D), BF16), SDS((1, D), F32)],
        name=name, compiler_params=_params("arbitrary"))(dy, w3, h, g3, dh_in, *deps)


def _fam_dims(kind, K, N):
    return (K // 2, N // N_CHIPS) if kind == "col" else (K // (2 * N_CHIPS), N)


def _mm_tn(x, dy, kind, name, square=False):
    M, K = x.shape
    N = dy.shape[1]
    nr, nc = _fam_dims(kind, K, N)

    def body(x_ref, dy_ref, o_ref):
        lhs = x_ref[...]
        res = lax.dot_general(lhs * lhs if square else lhs, dy_ref[...], (((0,), (0,)), ((), ())), preferred_element_type=F32)
        o_ref[...] = res.astype(BF16).reshape(o_ref.shape)

    if kind == "col":
        tn = _ctile(nc)
        ct = nc // tn
        grid = (N // tn,)
        in_specs = [pl.BlockSpec((M, K), lambda j: (0, 0)), pl.BlockSpec((M, tn), lambda j: (0, j))]
        out_spec = pl.BlockSpec((2, None, nr, tn), lambda j: (0, j // ct, 0, j % ct))
    else:
        grid = (N_CHIPS,)
        in_specs = [pl.BlockSpec((M, 2 * nr), lambda i: (0, i)), pl.BlockSpec((M, N), lambda i: (0, 0))]
        out_spec = pl.BlockSpec((2, None, nr, N), lambda i: (0, i, 0, 0))
    return pl.pallas_call(
        body, grid=grid, in_specs=in_specs, out_specs=out_spec, out_shape=SDS((2, N_CHIPS, nr, nc), BF16),
        name=name, compiler_params=_params("parallel"))(x, dy)


C_EVEN = 512


def _live(rows, base, total):
    r = _row_ids((rows, 1), base)
    return jnp.logical_and(r >= PAD, r < total).astype(F32)


def _conv_taps(win, w_ref, ls, acc, flip):
    for b in range(8):
        rb = win if b == 0 else pltpu.roll(win, 96 - b, 0)
        for a in range(5):
            o = 8 * a + b
            tap = (30 - o) if flip else (o - 2)
            if 0 <= tap < CONV_WIDTH:
                acc = acc + w_ref[pl.ds(tap, 1), ls] * rb[8 * a:8 * a + CHUNK]
    return acc


def _window_sum(win, levels, forward):
    s = win
    n = win.shape[0]
    for k in range(levels):
        step = 1 << k
        s = s + pltpu.roll(s, (n - step) if forward else step, 0)
    return s


def _pool_count(base, g):
    pos = _row_ids((CHUNK, 1), base) - PAD
    return jnp.clip(pos + 1, 1, POOL_WINDOWS[g]).astype(F32)


def _even_fwd(u, cw3, cb3, lg3, lb3, pw4, pb3, ps3, j, name):
    T = u.shape[0]
    C = C_EVEN
    tm = _tile(T, 320)
    nch = tm // CHUNK
    nblk = T // CHUNK

    def body(u_ref, up_ref, cw_ref, cb_ref, lg_ref, lb_ref, pw_ref, pb_ref, ps_ref, o_ref, yc_ref, a_s, p_s, yc_s):
        row0 = pl.program_id(0) * tm
        up = up_ref[...]
        lp = _live(CHUNK, row0 - CHUNK, T)
        a_s[0:CHUNK, :] = up[:, 0:C] * _sigmoid(up[:, C:2 * C]) * lp
        p_s[0:CHUNK, :] = up[:, 2 * C:3 * C] * lp

        def stage(c, _):
            rs = _mult(c * CHUNK, CHUNK)
            lv = _live(CHUNK, row0 + rs, T)
            a_s[pl.ds(rs + CHUNK, CHUNK), :] = u_ref[pl.ds(rs, CHUNK), 0:C] * _sigmoid(u_ref[pl.ds(rs, CHUNK), C:2 * C]) * lv
            p_s[pl.ds(rs + CHUNK, CHUNK), :] = u_ref[pl.ds(rs, CHUNK), 2 * C:3 * C] * lv
            return 0

        for c in range(nch):
            stage(c, 0)

        def chunk(c, _):
            rs = _mult(c * CHUNK, CHUNK)
            lv = _live(CHUNK, row0 + rs, T)
            for cb in range(4):
                ls = slice(cb * 128, (cb + 1) * 128)
                win = a_s[pl.ds(_mult(rs + 32, 32), 96), ls]
                acc = jnp.broadcast_to(cb_ref[:, ls], (CHUNK, 128))
                yc_s[:, ls] = _conv_taps(win, cw_ref, ls, acc, False)
            y = yc_s[...]
            yc_ref[pl.ds(rs, CHUNK), :] = y
            xc = y - jnp.mean(y, axis=-1, keepdims=True)
            yn = xc * lax.rsqrt(jnp.mean(xc * xc, axis=-1, keepdims=True) + EPS) * lg_ref[...] + lb_ref[...]
            o_ref[pl.ds(rs, CHUNK), 0:C] = (yn * _sigmoid(yn) * lv).astype(BF16)
            for g in range(4):
                ls = slice(g * 128, (g + 1) * 128)
                win = p_s[pl.ds(_mult(rs + 48, 16), 80), ls]
                s = _window_sum(win, g + 1, False)
                d = s[16:80] / _pool_count(row0 + rs, g) - win[16:80]
                yv = jnp.dot(d.astype(BF16), pw_ref[g].astype(BF16), preferred_element_type=F32) + pb_ref[:, ls]
                o_ref[pl.ds(rs, CHUNK), C + g * 128:C + (g + 1) * 128] = (yv * ps_ref[:, ls] * lv).astype(BF16)
            return 0

        for c in range(nch):
            chunk(c, 0)

    vec = pl.BlockSpec((None, 1, C), lambda i: (j, 0, 0))
    return pl.pallas_call(
        body, grid=(T // tm,),
        in_specs=[pl.BlockSpec((tm, 3 * C), lambda i: (i, 0)),
                  pl.BlockSpec((CHUNK, 3 * C), lambda i: (jnp.maximum(i * nch - 1, 0), 0)),
                  pl.BlockSpec((None, CONV_ROWS, C), lambda i: (j, 0, 0)), vec, vec, vec,
                  pl.BlockSpec((None, 4, 128, 128), lambda i: (j, 0, 0, 0)), vec, vec],
        out_specs=[pl.BlockSpec((tm, 2 * C), lambda i: (i, 0)), pl.BlockSpec((tm, C), lambda i: (i, 0))],
        out_shape=[SDS((T, 2 * C), BF16), SDS((T, C), F32)],
        scratch_shapes=[pltpu.VMEM((tm + CHUNK, C), F32), pltpu.VMEM((tm + CHUNK, C), F32), pltpu.VMEM((CHUNK, C), F32)],
        name=name, compiler_params=_params("parallel"))(u, u, cw3, cb3, lg3, lb3, pw4, pb3, ps3)


def _even_bwd(u, yc, dy, cw3, cb3, lg3, lb3, pw4, pb3, ps3, j, name):
    T = u.shape[0]
    C = C_EVEN
    tm = _tile(T, 320)
    nch = tm // CHUNK
    nblk = T // CHUNK
    ntile = T // tm

    def body(u_ref, up_ref, un_ref, yc_ref, ycn_ref, dy_ref, dyn_ref, cw_ref, cb_ref, lg_ref, lb_ref, pw_ref, pb_ref, ps_ref,
             du_ref, dcw_ref, dcb_ref, dlg_ref, dlb_ref, dpw_ref, dpb_ref, dps_ref,
             a_s, p_s, dy_s, dyc_s, dd_s, ddc_s, dw_s):
        i = pl.program_id(0)
        row0 = i * tm

        @pl.when(i == 0)
        def _():
            for ref in (dcb_ref, dlg_ref, dlb_ref, dpw_ref, dpb_ref, dps_ref, dw_s):
                ref[...] = jnp.zeros_like(ref)

        up = up_ref[...]
        lp = _live(CHUNK, row0 - CHUNK, T)
        a_s[0:CHUNK, :] = up[:, 0:C] * _sigmoid(up[:, C:2 * C]) * lp
        p_s[0:CHUNK, :] = up[:, 2 * C:3 * C] * lp
        ln_ = _live(CHUNK, row0 + tm, T)
        p_s[tm + CHUNK:tm + 2 * CHUNK, :] = un_ref[:, 2 * C:3 * C] * ln_
        dy_s[tm:tm + CHUNK, :] = dyn_ref[...] * ln_
        dyc_s[tm + CHUNK:tm + CHUNK + 32, :] = jnp.zeros((32, C), F32)

        def stage(c, _):
            rs = _mult(c * CHUNK, CHUNK)
            lv = _live(CHUNK, row0 + rs, T)
            a_s[pl.ds(rs + CHUNK, CHUNK), :] = u_ref[pl.ds(rs, CHUNK), 0:C] * _sigmoid(u_ref[pl.ds(rs, CHUNK), C:2 * C]) * lv
            p_s[pl.ds(rs + CHUNK, CHUNK), :] = u_ref[pl.ds(rs, CHUNK), 2 * C:3 * C] * lv
            dy_s[pl.ds(rs, CHUNK), :] = dy_ref[pl.ds(rs, CHUNK), :] * lv
            return 0

        for c in range(nch):
            stage(c, 0)

        def first(rs, y, own):
            xc = y - jnp.mean(y, axis=-1, keepdims=True)
            rstd = lax.rsqrt(jnp.mean(xc * xc, axis=-1, keepdims=True) + EPS)
            xh = xc * rstd
            yn = xh * lg_ref[...] + lb_ref[...]
            sg = _sigmoid(yn)
            dyn = dy_s[pl.ds(rs, CHUNK), 0:C] * (sg * (1.0 + yn * (1.0 - sg)))
            dlg_ref[...] += jnp.sum(dyn * xh, axis=0, keepdims=True) * own
            dlb_ref[...] += jnp.sum(dyn, axis=0, keepdims=True) * own
            dxh = dyn * lg_ref[...]
            dyc = rstd * (dxh - jnp.mean(dxh, axis=-1, keepdims=True) - xh * jnp.mean(dxh * xh, axis=-1, keepdims=True))
            dyc_s[pl.ds(rs, CHUNK), :] = dyc
            dcb_ref[...] += jnp.sum(dyc, axis=0, keepdims=True) * own
            for g in range(4):
                ls = slice(g * 128, (g + 1) * 128)
                win = p_s[pl.ds(rs + 48, 80), ls]
                s = _window_sum(win, g + 1, False)
                cnt = _pool_count(row0 + rs, g)
                d = (s[16:80] / cnt - win[16:80]).astype(BF16)
                w = pw_ref[g].astype(BF16)
                pre = jnp.dot(d, w, preferred_element_type=F32) + pb_ref[:, ls]
                dyb = dy_s[pl.ds(rs, CHUNK), C + g * 128:C + (g + 1) * 128]
                dpre = dyb * ps_ref[:, ls]
                dps_ref[:, ls] += jnp.sum(dyb * pre, axis=0, keepdims=True) * own
                dpb_ref[:, ls] += jnp.sum(dpre, axis=0, keepdims=True) * own
                dpre_b = (dpre * own).astype(BF16)
                dpw_ref[g] += lax.dot_general(d, dpre_b, (((0,), (0,)), ((), ())), preferred_element_type=F32)
                dd = lax.dot_general(dpre.astype(BF16), w, (((1,), (1,)), ((), ())), preferred_element_type=F32)
                dd_s[pl.ds(rs, CHUNK), ls] = dd
                ddc_s[pl.ds(rs, CHUNK), ls] = dd / cnt

        def first_in_tile(c, _):
            rs = _mult(c * CHUNK, CHUNK)
            first(rs, yc_ref[pl.ds(rs, CHUNK), :], 1.0)
            return 0

        for c in range(nch):
            first_in_tile(c, 0)
        first(tm, ycn_ref[...], 0.0)
        ddc_s[tm + CHUNK:tm + CHUNK + 16, :] = jnp.zeros((16, C), F32)

        def second(c, _):
            rs = _mult(c * CHUNK, CHUNK)
            lv = _live(CHUNK, row0 + rs, T)
            for cb in range(4):
                ls = slice(cb * 128, (cb + 1) * 128)
                wd = dyc_s[pl.ds(rs, 96), ls]
                da = _conv_taps(wd, cw_ref, ls, jnp.zeros((CHUNK, 128), F32), True)
                wa = a_s[pl.ds(_mult(rs + 32, 32), 96), ls]
                dyc = dyc_s[pl.ds(rs, CHUNK), ls]
                for b in range(8):
                    rb = wa if b == 0 else pltpu.roll(wa, 96 - b, 0)
                    for a in range(5):
                        tap = 8 * a + b - 2
                        if 0 <= tap < CONV_WIDTH:
                            prod = dyc * rb[8 * a:8 * a + CHUNK]
                            part = prod[0:8]
                            for q in range(1, 8):
                                part = part + prod[8 * q:8 * q + 8]
                            dw_s[8 * tap:8 * tap + 8, ls] += part
                val = u_ref[pl.ds(rs, CHUNK), ls]
                sg = _sigmoid(u_ref[pl.ds(rs, CHUNK), C + cb * 128:C + (cb + 1) * 128])
                du_ref[pl.ds(rs, CHUNK), ls] = (da * sg * lv).astype(BF16)
                du_ref[pl.ds(rs, CHUNK), C + cb * 128:C + (cb + 1) * 128] = (da * val * sg * (1.0 - sg) * lv).astype(BF16)
            for g in range(4):
                ls = slice(g * 128, (g + 1) * 128)
                z = _window_sum(ddc_s[pl.ds(rs, 80), ls], g + 1, True)
                dpin = (z[0:CHUNK] - dd_s[pl.ds(rs, CHUNK), ls]) * lv
                du_ref[pl.ds(rs, CHUNK), 2 * C + g * 128:2 * C + (g + 1) * 128] = dpin.astype(BF16)
            return 0

        for c in range(nch):
            second(c, 0)

        @pl.when(i == ntile - 1)
        def _():
            for tap in range(CONV_WIDTH):
                dcw_ref[tap:tap + 1, :] = jnp.sum(dw_s[8 * tap:8 * tap + 8, :], axis=0, keepdims=True)
            dcw_ref[CONV_WIDTH:CONV_ROWS, :] = jnp.zeros((CONV_ROWS - CONV_WIDTH, C), F32)

    vec = pl.BlockSpec((None, 1, C), lambda i: (j, 0, 0))
    ovec = pl.BlockSpec((1, C), lambda i: (0, 0))
    return pl.pallas_call(
        body, grid=(ntile,),
        in_specs=[pl.BlockSpec((tm, 3 * C), lambda i: (i, 0)),
                  pl.BlockSpec((CHUNK, 3 * C), lambda i: (jnp.maximum(i * nch - 1, 0), 0)),
                  pl.BlockSpec((CHUNK, 3 * C), lambda i: (jnp.minimum((i + 1) * nch, nblk - 1), 0)),
                  pl.BlockSpec((tm, C), lambda i: (i, 0)),
                  pl.BlockSpec((CHUNK, C), lambda i: (jnp.minimum((i + 1) * nch, nblk - 1), 0)),
                  pl.BlockSpec((tm, 2 * C), lambda i: (i, 0)),
                  pl.BlockSpec((CHUNK, 2 * C), lambda i: (jnp.minimum((i + 1) * nch, nblk - 1), 0)),
                  pl.BlockSpec((None, CONV_ROWS, C), lambda i: (j, 0, 0)), vec, vec, vec,
                  pl.BlockSpec((None, 4, 128, 128), lambda i: (j, 0, 0, 0)), vec, vec],
        out_specs=[pl.BlockSpec((tm, 3 * C), lambda i: (i, 0)), pl.BlockSpec((CONV_ROWS, C), lambda i: (0, 0)),
                   ovec, ovec, ovec, pl.BlockSpec((4, 128, 128), lambda i: (0, 0, 0)), ovec, ovec],
        out_shape=[SDS((T, 3 * C), BF16), SDS((CONV_ROWS, C), F32), SDS((1, C), F32), SDS((1, C), F32), SDS((1, C), F32),
                   SDS((4, 128, 128), F32), SDS((1, C), F32), SDS((1, C), F32)],
        scratch_shapes=[pltpu.VMEM((tm + CHUNK, C), F32), pltpu.VMEM((tm + 2 * CHUNK, C), F32),
                        pltpu.VMEM((tm + CHUNK, 2 * C), F32),
                        pltpu.VMEM((tm + CHUNK + 32, C), F32), pltpu.VMEM((tm + CHUNK, C), F32),
                        pltpu.VMEM((tm + CHUNK + 16, C), F32), pltpu.VMEM((8 * CONV_ROWS, C), F32)],
        name=name, compiler_params=_params("arbitrary"))(u, u, u, yc, yc, dy, dy, cw3, cb3, lg3, lb3, pw4, pb3, ps3)


HI = lax.Precision.HIGHEST


def _dot_nt(a, b):
    return lax.dot_general(a, b, (((1,), (1,)), ((), ())), preferred_element_type=F32)


def _dot_tn(a, b):
    return lax.dot_general(a, b, (((0,), (0,)), ((), ())), preferred_element_type=F32)


def _tri(lower):
    r = lax.broadcasted_iota(jnp.int32, (CHUNK, CHUNK), 0)
    c = lax.broadcasted_iota(jnp.int32, (CHUNK, CHUNK), 1)
    return jnp.where((c <= r) if lower else (c >= r), 1.0, 0.0).astype(F32)


def _hgrn_gates(u_ref, lb_ref, h, D, lv):
    ls = slice(h * HEAD_DIM, (h + 1) * HEAD_DIM)
    qraw = u_ref[:, ls]
    fraw = u_ref[:, D + h * HEAD_DIM:D + (h + 1) * HEAD_DIM]
    v = u_ref[:, 2 * D + h * HEAD_DIM:2 * D + (h + 1) * HEAD_DIM] * lv
    lbv = lb_ref[:, ls]
    sig = _sigmoid(fraw)
    forget = lbv + (1.0 - lbv) * sig
    logf = jnp.log(forget) * lv
    k = (1.0 - forget) * lv
    qsig = _sigmoid(qraw)
    q = qraw * qsig * lv
    return q, k, v, logf, (qraw, qsig, sig, forget, lbv)


def _sub_parts(q, k, b, b_s, I):
    rows = slice(SUB * I, SUB * (I + 1))
    rho = jnp.zeros((1, HEAD_DIM), F32) if I == 0 else b_s[SUB * I - 1:SUB * I, :]
    eI = jnp.exp(b[rows] - rho)
    EI = jnp.exp(jnp.minimum(rho - b, EXP_CAP))
    causal = (lax.broadcasted_iota(jnp.int32, (SUB, CHUNK), 1)
              <= lax.broadcasted_iota(jnp.int32, (SUB, CHUNK), 0) + SUB * I)
    return rows, q[rows] * eI, k * EI, eI, EI, causal


def _chunks_per_step(NC):
    for n in (5, 4, 3, 2):
        if NC % n == 0:
            return n
    return 1


def _hgrn_fwd(u, lb3, layer, gn3, j, name):
    T = u.shape[0]
    D = u.shape[1] // 4
    H = D // HEAD_DIM
    NC = T // CHUNK
    CH = _chunks_per_step(NC)
    R = CH * CHUNK

    def body(u_ref, lb_ref, gn_ref, y_ref, o_ref, sall_ref, st_s, b_s, lf_s, q_s, k_s):
        n = pl.program_id(0)

        @pl.when(n == 0)
        def _():
            st_s[...] = jnp.zeros_like(st_s)

        heads = range(H)
        cols = [slice(h * HEAD_DIM, (h + 1) * HEAD_DIM) for h in heads]
        rows = [slice(c * CHUNK, (c + 1) * CHUNK) for c in range(CH)]
        vb = {}
        for c in range(CH):
            lv = _live(CHUNK, (n * CH + c) * CHUNK, T)
            for h in heads:
                q, k, v, logf, _ = _hgrn_gates(u_ref.at[rows[c]], lb_ref, h, D, lv)
                q_s[rows[c], cols[h]] = q
                k_s[rows[c], cols[h]] = k
                lf_s[rows[c], cols[h]] = logf
                vb[c, h] = v.astype(BF16)
        for c in range(CH):
            b_s[rows[c], :] = jnp.dot(_tri(True), lf_s[rows[c], :], precision=HI, preferred_element_type=F32)
        ops = {}
        for c in range(CH):
            for h in heads:
                b_h = b_s.at[rows[c], cols[h]]
                b = b_h[...]
                q = q_s[rows[c], cols[h]]
                k = k_s[rows[c], cols[h]]
                blast = b_h[CHUNK - 1:CHUNK, :]
                qh = (q * jnp.exp(b)).astype(BF16)
                kt = (k * jnp.exp(blast - b)).astype(BF16)
                subs = []
                for I in range(CHUNK // SUB):
                    _, qI, KI, _, _, causal = _sub_parts(q, k, b, b_h, I)
                    subs.append((qI.astype(BF16), KI.astype(BF16), causal))
                ops[c, h] = (qh, kt, jnp.exp(blast), subs)
        mm = {}
        for h in heads:
            st = st_s[h]
            for c in range(CH):
                qh, kt, eblast, subs = ops[c, h]
                sall_ref[c, h] = st
                o_inter = _dot_nt(qh, st.astype(BF16))
                st = st * eblast + _dot_tn(vb[c, h], kt)
                mm[c, h] = (o_inter, [_dot_nt(qI, KI) for qI, KI, _ in subs])
            st_s[h] = st
        for c in range(CH):
            for h in heads:
                o_inter, ps = mm[c, h]
                p = jnp.concatenate([jnp.where(m, x, 0.0) for x, (_, _, m) in zip(ps, ops[c, h][3])], axis=0).astype(BF16)
                o = o_inter + jnp.dot(p, vb[c, h], preferred_element_type=F32)
                o_ref[rows[c], cols[h]] = o
                graw = u_ref[rows[c], 3 * D + h * HEAD_DIM:3 * D + (h + 1) * HEAD_DIM]
                r = lax.rsqrt(jnp.mean(o * o, axis=-1, keepdims=True) + EPS)
                y_ref[rows[c], cols[h]] = (((o * r) * gn_ref[...]) * (graw * _sigmoid(graw))).astype(BF16)

    return pl.pallas_call(
        body, grid=(NC // CH,),
        in_specs=[pl.BlockSpec((R, 4 * D), lambda n: (n, 0)),
                  pl.BlockSpec((None, 1, D), lambda n: (layer, 0, 0)),
                  pl.BlockSpec((None, 1, HEAD_DIM), lambda n: (j, 0, 0))],
        out_specs=[pl.BlockSpec((R, D), lambda n: (n, 0)), pl.BlockSpec((R, D), lambda n: (n, 0)),
                   pl.BlockSpec((CH, H, HEAD_DIM, HEAD_DIM), lambda n: (n, 0, 0, 0))],
        out_shape=[SDS((T, D), BF16), SDS((T, D), F32), SDS((NC, H, HEAD_DIM, HEAD_DIM), F32)],
        scratch_shapes=[pltpu.VMEM((H, HEAD_DIM, HEAD_DIM), F32)] + [pltpu.VMEM((R, D), F32)] * 4,
        name=name, compiler_params=_params("arbitrary"))(u, lb3, gn3)


def _hgrn_bwd(u, o_raw, dy, sall, lb3, layer, gn3, j, name):
    T = u.shape[0]
    D = u.shape[1] // 4
    H = D // HEAD_DIM
    NC = T // CHUNK
    CH = _chunks_per_step(NC)
    R = CH * CHUNK
    NS = NC // CH

    def body(u_ref, o_ref, dy_ref, sall_ref, lb_ref, gn_ref, du_ref, dlb_ref, dgn_ref, dst_s, b_s, lf_s, q_s, k_s, db_s, dk_s):
        step = pl.program_id(0)
        n = NS - 1 - step

        @pl.when(step == 0)
        def _():
            dst_s[...] = jnp.zeros_like(dst_s)
            dlb_ref[...] = jnp.zeros_like(dlb_ref)
            dgn_ref[...] = jnp.zeros_like(dgn_ref)

        last_row = (_row_ids((CHUNK, 1), 0) == CHUNK - 1).astype(F32)
        gn = gn_ref[...]
        heads = range(H)
        chunks = range(CH)
        cols = [slice(h * HEAD_DIM, (h + 1) * HEAD_DIM) for h in heads]
        rows = [slice(c * CHUNK, (c + 1) * CHUNK) for c in chunks]
        lv = [_live(CHUNK, (n * CH + c) * CHUNK, T) for c in chunks]
        vb, dob = {}, {}
        dgn = jnp.zeros((1, HEAD_DIM), F32)
        for c in chunks:
            for h in heads:
                q, k, v, logf, _ = _hgrn_gates(u_ref.at[rows[c]], lb_ref, h, D, lv[c])
                q_s[rows[c], cols[h]] = q
                k_s[rows[c], cols[h]] = k
                lf_s[rows[c], cols[h]] = logf
                vb[c, h] = v.astype(BF16)
                graw = u_ref[rows[c], 3 * D + h * HEAD_DIM:3 * D + (h + 1) * HEAD_DIM]
                gsig = _sigmoid(graw)
                o = o_ref[rows[c], cols[h]]
                r = lax.rsqrt(jnp.mean(o * o, axis=-1, keepdims=True) + EPS)
                xh = o * r
                dyv = dy_ref[rows[c], cols[h]]
                dsg = dyv * (graw * gsig)
                dgn = dgn + jnp.sum(dsg * xh, axis=0, keepdims=True)
                dxh = dsg * gn
                do = r * (dxh - xh * jnp.mean(dxh * xh, axis=-1, keepdims=True))
                dob[c, h] = do.astype(BF16)
                dgraw = dyv * xh * gn * (gsig * (1.0 + graw * (1.0 - gsig)))
                du_ref[rows[c], 3 * D + h * HEAD_DIM:3 * D + (h + 1) * HEAD_DIM] = (dgraw * lv[c]).astype(BF16)
        dgn_ref[...] += dgn
        for c in chunks:
            b_s[rows[c], :] = jnp.dot(_tri(True), lf_s[rows[c], :], precision=HI, preferred_element_type=F32)
        ops = {}
        for c in chunks:
            for h in heads:
                b_h = b_s.at[rows[c], cols[h]]
                b = b_h[...]
                q = q_s[rows[c], cols[h]]
                k = k_s[rows[c], cols[h]]
                blast = b_h[CHUNK - 1:CHUNK, :]
                eb = jnp.exp(b)
                ekb = jnp.exp(blast - b)
                subs = []
                for I in range(CHUNK // SUB):
                    rws, qI, KI, eI, EI, causal = _sub_parts(q, k, b, b_h, I)
                    subs.append((rws, qI.astype(BF16), KI.astype(BF16), eI, EI, causal))
                ops[c, h] = (eb, ekb, jnp.exp(blast), (q * eb).astype(BF16), (k * ekb).astype(BF16), subs)
        mm = {}
        for h in heads:
            dst = dst_s[h]
            for c in reversed(chunks):
                eb, ekb, eblast, qhb, ktb, subs = ops[c, h]
                st = sall_ref[c, h]
                dstb = dst.astype(BF16)
                dv = _dot_nt(ktb, dstb)
                dqh = jnp.dot(dob[c, h], st.astype(BF16), preferred_element_type=F32)
                dkt = jnp.dot(vb[c, h], dstb, preferred_element_type=F32)
                dblast = jnp.sum(dst * st, axis=0, keepdims=True) * eblast
                dst = dst * eblast + _dot_tn(dob[c, h], qhb)
                dp_full = _dot_nt(dob[c, h], vb[c, h])
                ps = [_dot_nt(qIb, KIb) for _, qIb, KIb, _, _, _ in subs]
                mm[c, h] = (dv, dqh, dkt, dblast, dp_full, ps)
            dst_s[h] = dst
        for c in chunks:
            for h in heads:
                eb, ekb, eblast, qhb, ktb, subs = ops[c, h]
                dv, dqh, dkt, dblast, dp_full, ps = mm[c, h]
                p = jnp.concatenate([jnp.where(sub[5], x, 0.0) for x, sub in zip(ps, subs)], axis=0).astype(BF16)
                dv = dv + _dot_tn(p, dob[c, h])
                du_ref[rows[c], 2 * D + h * HEAD_DIM:2 * D + (h + 1) * HEAD_DIM] = (dv * lv[c]).astype(BF16)
                dq = dqh * eb
                db = dqh * qhb.astype(F32)
                tmp = dkt * ktb.astype(F32)
                dk = dkt * ekb
                db = db - tmp
                dblast = dblast + jnp.sum(tmp, axis=0, keepdims=True)
                dq_parts, db_parts = [], []
                for rws, qIb, KIb, eI, EI, causal in subs:
                    dp = jnp.where(causal, dp_full[rws], 0.0).astype(BF16)
                    dqI = jnp.dot(dp, KIb, preferred_element_type=F32)
                    dKI = _dot_tn(dp, qIb)
                    dq_parts.append(dqI * eI)
                    db_parts.append(dqI * qIb.astype(F32))
                    dk = dk + dKI * EI
                    db = db - dKI * KIb.astype(F32)
                dq = dq + jnp.concatenate(dq_parts, axis=0)
                db_s[rows[c], cols[h]] = db + jnp.concatenate(db_parts, axis=0) + last_row * dblast
                dk_s[rows[c], cols[h]] = dk
                qraw = u_ref[rows[c], cols[h]]
                qsig = _sigmoid(qraw)
                du_ref[rows[c], cols[h]] = (dq * (qsig * (1.0 + qraw * (1.0 - qsig))) * lv[c]).astype(BF16)
        for c in chunks:
            lf_s[rows[c], :] = jnp.dot(_tri(False), db_s[rows[c], :], precision=HI, preferred_element_type=F32)
        for h in heads:
            lbv = lb_ref[:, cols[h]]
            dlb = jnp.zeros((1, HEAD_DIM), F32)
            for c in chunks:
                fraw = u_ref[rows[c], D + h * HEAD_DIM:D + (h + 1) * HEAD_DIM]
                sig = _sigmoid(fraw)
                forget = lbv + (1.0 - lbv) * sig
                dforget = (lf_s[rows[c], cols[h]] / forget - dk_s[rows[c], cols[h]]) * lv[c]
                dlb = dlb + jnp.sum(dforget * (1.0 - sig), axis=0, keepdims=True)
                du_ref[rows[c], D + h * HEAD_DIM:D + (h + 1) * HEAD_DIM] = (dforget * (1.0 - lbv) * sig * (1.0 - sig)).astype(BF16)
            dlb_ref[:, cols[h]] += dlb

    rev = lambda s: (NS - 1 - s, 0)
    return pl.pallas_call(
        body, grid=(NS,),
        in_specs=[pl.BlockSpec((R, 4 * D), rev), pl.BlockSpec((R, D), rev), pl.BlockSpec((R, D), rev),
                  pl.BlockSpec((CH, H, HEAD_DIM, HEAD_DIM), lambda s: (NS - 1 - s, 0, 0, 0)),
                  pl.BlockSpec((None, 1, D), lambda s: (layer, 0, 0)),
                  pl.BlockSpec((None, 1, HEAD_DIM), lambda s: (j, 0, 0))],
        out_specs=[pl.BlockSpec((R, 4 * D), rev), pl.BlockSpec((1, D), lambda s: (0, 0)),
                   pl.BlockSpec((1, HEAD_DIM), lambda s: (0, 0))],
        out_shape=[SDS((T, 4 * D), BF16), SDS((1, D), F32), SDS((1, HEAD_DIM), F32)],
        scratch_shapes=[pltpu.VMEM((H, HEAD_DIM, HEAD_DIM), F32)] + [pltpu.VMEM((R, D), F32)] * 6,
        name=name, compiler_params=_params("arbitrary"))(u, o_raw, dy, sall, lb3, gn3)


def _softmax_layers(p_ref, n_layers):
    rows = [p_ref[l:l + 1, :] for l in range(n_layers)]
    m = functools.reduce(jnp.maximum, rows)
    e = [jnp.exp(x - m) for x in rows]
    tot = functools.reduce(lambda a, b: a + b, e)
    return [x / tot for x in e]


def _lb_fwd(p):
    n_layers, D = p.shape

    def body(p_ref, o_ref):
        s = _softmax_layers(p_ref, n_layers)
        acc = jnp.zeros((1, D), F32)
        o_ref[0:1, :] = acc
        for l in range(1, n_layers):
            acc = acc + s[l]
            o_ref[l:l + 1, :] = acc

    return pl.pallas_call(body, out_shape=SDS(p.shape, F32), name="lb_fwd")(p)


def _lb_bwd(p, dlb):
    n_layers, D = p.shape

    def body(p_ref, d_ref, o_ref):
        s = _softmax_layers(p_ref, n_layers)
        ds = [jnp.zeros((1, D), F32)] * n_layers
        acc = jnp.zeros((1, D), F32)
        for l in range(n_layers - 1, 0, -1):
            acc = acc + d_ref[l:l + 1, :]
            ds[l] = acc
        dot = functools.reduce(lambda a, b: a + b, [s[l] * ds[l] for l in range(n_layers)])
        for l in range(n_layers):
            o_ref[l:l + 1, :] = s[l] * (ds[l] - dot)

    return pl.pallas_call(body, out_shape=SDS(p.shape, F32), name="lb_bwd")(p, dlb)


def _adamw_small(items):
    n = len(items)

    def body(*refs):
        for k in range(n):
            w_ref, g_ref, m_ref, v_ref = refs[4 * k:4 * k + 4]
            d_ref, mo_ref, vo_ref = refs[4 * n + 3 * k:4 * n + 3 * k + 3]
            g_ = g_ref[...]
            m_ = ADAM_B1 * m_ref[...] + (1.0 - ADAM_B1) * g_
            v_ = ADAM_B2 * v_ref[...] + (1.0 - ADAM_B2) * (g_ * g_)
            mh = m_ / (1.0 - ADAM_B1 ** ADAM_STEP)
            vh = v_ / (1.0 - ADAM_B2 ** ADAM_STEP)
            d_ref[...] = -ADAM_LR * (mh / (jnp.sqrt(vh) + ADAM_EPS) + ADAM_WD * w_ref[...])
            mo_ref[...] = m_
            vo_ref[...] = v_

    out_shape = [SDS(it[0].shape, F32) for it in items for _ in range(3)]
    res = pl.pallas_call(body, out_shape=out_shape, name="adamw_small")(*[a for it in items for a in it])
    return [res[3 * k:3 * k + 3] for k in range(n)]


def _adamw_layer(w3, m3, v3, g2, layer, outs, name):
    L, R, C = w3.shape
    tr = _tile(R, 256, 8)
    if outs is None:
        outs = tuple(lax.empty(w3.shape, F32) for _ in range(4))

    def body(w_ref, m_ref, v_ref, g_ref, a0, a1, a2, a3, go_ref, d_ref, mo_ref, vo_ref):
        del a0, a1, a2, a3
        g_ = g_ref[...]
        m_ = ADAM_B1 * m_ref[...] + (1.0 - ADAM_B1) * g_
        v_ = ADAM_B2 * v_ref[...] + (1.0 - ADAM_B2) * (g_ * g_)
        mh = m_ / (1.0 - ADAM_B1 ** ADAM_STEP)
        vh = v_ / (1.0 - ADAM_B2 ** ADAM_STEP)
        go_ref[...] = g_
        d_ref[...] = -ADAM_LR * (mh / (jnp.sqrt(vh) + ADAM_EPS) + ADAM_WD * w_ref[...])
        mo_ref[...] = m_
        vo_ref[...] = v_

    lay = pl.BlockSpec((None, tr, C), lambda i: (layer, i, 0))
    return pl.pallas_call(
        body, grid=(R // tr,), in_specs=[lay] * 3 + [pl.BlockSpec((tr, C), lambda i: (i, 0))] + [ANY_SPEC] * 4,
        out_specs=[lay] * 4, out_shape=[SDS(w3.shape, F32)] * 4, input_output_aliases={4: 0, 5: 1, 6: 2, 7: 3},
        name=name, compiler_params=_params("parallel"))(w3, m3, v3, g2, *outs)


SEM_SPEC = pl.BlockSpec(memory_space=pltpu.SEMAPHORE)
HBM_SPEC = pl.BlockSpec(memory_space=pltpu.HBM)
EFFECT = pltpu.SideEffectType.DATAFLOW_SIDE_EFFECTING
N_DEV = 2 * N_CHIPS


def _position():
    x, y, c = lax.axis_index("x"), lax.axis_index("y"), lax.axis_index("c")
    chips = [(1 - x, y), (x, 1 - y), (1 - x, 1 - y)]
    return x, y, c, chips


def _split_start(name, plan, bufs, n_sems, deps=(), earlier=None):
    n = len(bufs)
    held = () if earlier is None else tuple(earlier[1:])

    def body(*refs):
        first_out = n + len(held) + len(deps)
        if earlier is not None:
            sends, recvs = earlier[0](refs[:n], refs[n], refs[n + 1])
            for kw in sends:
                pltpu.make_async_remote_copy(**kw).wait_send()
            for kw in recvs:
                pltpu.make_async_remote_copy(**kw).wait_recv()
        sends, _ = plan(refs[:n], refs[first_out], refs[first_out + 1])
        for kw in sends:
            pltpu.make_async_remote_copy(**kw).start()
        refs[-1][...] = jnp.zeros_like(refs[-1])

    out = pl.pallas_call(
        body, name=name,
        out_shape=(pltpu.SemaphoreType.DMA((n_sems,)), pltpu.SemaphoreType.DMA((n_sems,)),
                   *[pltpu.HBM(b.shape, b.dtype) for b in bufs], SDS((8, 128), F32)),
        in_specs=[HBM_SPEC] * n + [SEM_SPEC] * len(held) + [ANY_SPEC] * len(deps),
        out_specs=(SEM_SPEC, SEM_SPEC, *[HBM_SPEC] * n, pl.BlockSpec(memory_space=pltpu.VMEM)),
        input_output_aliases={i: 2 + i for i in range(n)},
        compiler_params=pltpu.CompilerParams(has_side_effects=EFFECT),
    )(*[pltpu.with_memory_space_constraint(b, pltpu.HBM) for b in bufs], *held, *deps)
    return out[0], out[1], list(out[2:2 + n]), out[-1]


def _split_wait(name, plan, send_sems, recv_sems, bufs, after=()):
    n = len(bufs)

    def body(*refs):
        sends, recvs = plan(refs[:n], refs[n], refs[n + 1])
        for kw in sends:
            pltpu.make_async_remote_copy(**kw).wait_send()
        for kw in recvs:
            pltpu.make_async_remote_copy(**kw).wait_recv()

    out = pl.pallas_call(
        body, name=name, out_shape=tuple(pltpu.HBM(b.shape, b.dtype) for b in bufs),
        in_specs=[HBM_SPEC] * n + [SEM_SPEC, SEM_SPEC] + [ANY_SPEC] * len(after),
        out_specs=tuple([HBM_SPEC] * n), input_output_aliases={i: i for i in range(n)},
        compiler_params=pltpu.CompilerParams(has_side_effects=EFFECT),
    )(*bufs, send_sems, recv_sems, *after)
    return list(out)


def _region(kind, ref, chip, half):
    K, N = ref.shape
    if kind == "col":
        return ref.at[pl.ds(half * (K // 2), K // 2), pl.ds(chip * (N // N_CHIPS), N // N_CHIPS)]
    rows = K // (2 * N_CHIPS)
    return ref.at[pl.ds((2 * chip + half) * rows, rows), :]


def _gather_plan(kinds, over_chips, first=0):
    def plan(refs, send_sems, recv_sems):
        x, y, c, chips = _position()
        sends, recvs = [], []
        for f, (ref, kind) in enumerate(zip(refs, kinds)):
            for k, chip in enumerate(chips):
                theirs = 2 * chip[0] + chip[1]
                at = 3 * (first + f) + k
                sem = dict(send_sem=send_sems.at[at], recv_sem=recv_sems.at[at], device_id_type=MESH)
                if over_chips:
                    out, back, to = _region(kind, ref, 2 * x + y, c), _region(kind, ref, theirs, c), (*chip, c)
                else:
                    out, back, to = _region(kind, ref, theirs, c), _region(kind, ref, theirs, 1 - c), (x, y, 1 - c)
                sends.append(dict(src_ref=out, dst_ref=out, device_id=to, **sem))
                recvs.append(dict(src_ref=back, dst_ref=back, device_id=to, **sem))
        return sends, recvs
    return plan


def _reduce_plan(first=0):
    def plan(refs, send_sems, recv_sems):
        x, y, c, _ = _position()
        me = 4 * x + 2 * y + c
        sends, recvs = [], []
        for f in range(len(refs) // 2):
            acc, land = refs[2 * f], refs[2 * f + 1]
            for d in range(1, N_DEV):
                t = (me + d) % N_DEV
                to = dict(device_id=(t // 4, (t // 2) % 2, t % 2), device_id_type=MESH)
                slot = N_DEV - 1 - d
                at = first + 7 * f
                sends.append(dict(src_ref=acc.at[t % 2, t // 2], dst_ref=land.at[slot], send_sem=send_sems.at[at + d - 1],
                                  recv_sem=recv_sems.at[at + slot], **to))
                recvs.append(dict(src_ref=land.at[d - 1], dst_ref=land.at[d - 1], send_sem=send_sems.at[at + d - 1],
                                  recv_sem=recv_sems.at[at + d - 1], **to))
        return sends, recvs
    return plan


def _swap_plan(first=0):
    def plan(refs, send_sems, recv_sems):
        x, y, c, _ = _position()
        sends, recvs = [], []
        for f, g in enumerate(refs):
            sem = dict(send_sem=send_sems.at[first + f], recv_sem=recv_sems.at[first + f], device_id=(x, y, 1 - c),
                       device_id_type=MESH)
            sends.append(dict(src_ref=g.at[c], dst_ref=g.at[c], **sem))
            recvs.append(dict(src_ref=g.at[1 - c], dst_ref=g.at[1 - c], **sem))
        return sends, recvs
    return plan


def _joined(plans):
    def plan(refs, send_sems, recv_sems):
        sends, recvs, lo = [], [], 0
        for part, n in plans:
            s_, r_ = part(refs[lo:lo + n], send_sems, recv_sems)
            sends += s_
            recvs += r_
            lo += n
        return sends, recvs
    return plan


def _sum_pieces(ids2, acc, land, name):
    _, _, nr, nc = acc.shape
    tr = _tile(nr, 256, 16)

    def body(ids_ref, own_ref, land_ref, o_ref):
        del ids_ref
        s = own_ref[...].astype(F32)
        for k in range(N_DEV - 1):
            s = s + land_ref[k].astype(F32)
        o_ref[...] = s

    return pl.pallas_call(
        body,
        grid_spec=pltpu.PrefetchScalarGridSpec(
            num_scalar_prefetch=1, grid=(nr // tr,),
            in_specs=[pl.BlockSpec((None, None, tr, nc), lambda i, ids: (ids[0], ids[1], i, 0)),
                      pl.BlockSpec((N_DEV - 1, tr, nc), lambda i, ids: (0, i, 0))],
            out_specs=pl.BlockSpec((None, tr, nc), lambda i, ids: (ids[0], i, 0))),
        out_shape=SDS((2, nr, nc), F32), name=name, compiler_params=_params("parallel"))(ids2, acc, land)


def _small_plan(refs, send_sems, recv_sems):
    x, y, c, _ = _position()
    me = 4 * x + 2 * y + c
    own, land = refs
    sends, recvs = [], []
    for d in range(1, N_DEV):
        t = (me + d) % N_DEV
        to = dict(device_id=(t // 4, (t // 2) % 2, t % 2), device_id_type=MESH)
        sends.append(dict(src_ref=own, dst_ref=land.at[me], send_sem=send_sems.at[d - 1],
                          recv_sem=recv_sems.at[N_DEV - 1 - d], **to))
        recvs.append(dict(src_ref=land.at[t], dst_ref=land.at[t], send_sem=send_sems.at[d - 1],
                          recv_sem=recv_sems.at[d - 1], **to))
    return sends, recvs


def _sum_blocks(me1, own, land):
    def body(me_ref, own_ref, land_ref, o_ref):
        acc = None
        for d in range(N_DEV):
            term = jnp.where(me_ref[0] == d, own_ref[...], land_ref[d])
            acc = term if acc is None else acc + term
        o_ref[...] = acc

    return pl.pallas_call(
        body,
        grid_spec=pltpu.PrefetchScalarGridSpec(
            num_scalar_prefetch=1, grid=(1,),
            in_specs=[pl.BlockSpec(own.shape, lambda i, me: (0, 0)), pl.BlockSpec(land.shape, lambda i, me: (0, 0, 0))],
            out_specs=pl.BlockSpec(own.shape, lambda i, me: (0, 0))),
        out_shape=SDS(own.shape, F32), name="sum_small", compiler_params=_params("arbitrary"))(me1, own, land)


BIG = {"ev_w_in": "col", "ev_w_out": "row", "od_w_in": "col", "od_w_out": "row", "mlp_w1": "col", "mlp_w2": "row"}
WEIGHTS = ("meta_tokens", "mix_norm_g", "mlp_norm_g", "final_norm_g", "ev_w_in", "ev_conv_w", "ev_conv_b", "ev_ln_g",
           "ev_ln_b", "ev_pool_w", "ev_pool_b", "ev_pool_scale", "ev_w_out", "od_w_in", "od_gnorm_g", "od_w_out",
           "lb_param", "mlp_w1", "mlp_w2")
PACK_UNIT = 1024


def _mixer_names(layer):
    return ("ev_w_in", "ev_w_out") if layer % 2 == 0 else ("od_w_in", "od_w_out")


def _pack(arrays):
    flat = []
    for a in arrays:
        a = a.reshape(-1)
        flat.append(jnp.pad(a, (0, (-a.shape[0]) % PACK_UNIT)))
    return jnp.concatenate(flat).reshape(-1, 128)


def _unpack(packed, shapes):
    flat = packed.reshape(-1)
    out, off = [], 0
    for s in shapes:
        size = 1
        for d in s:
            size *= d
        out.append(flat[off:off + size].reshape(s))
        off += size + (-size) % PACK_UNIT
    return out


def _local_step(x2, target, P, weights, boundary, first_deps=()):
    D = x2.shape[1]
    n_layers = P["mix_norm_g"].shape[0]
    h = jnp.concatenate([jnp.zeros((PAD, D), F32), P["meta_full"], x2], axis=0)
    mix_g = P["mix_norm_g"].reshape(n_layers, 1, D)
    mlp_g = P["mlp_norm_g"].reshape(n_layers, 1, D)
    vec = lambda a: a.reshape(a.shape[0], 1, -1)
    cb3, lg3, lnb3, ps3 = vec(P["ev_conv_b"]), vec(P["ev_ln_g"]), vec(P["ev_ln_b"]), vec(P["ev_pool_scale"])
    pb3 = vec(P["ev_pool_b"])
    gn3 = vec(P["od_gnorm_g"])
    lb_all = _lb_fwd(P["lb_param"])
    lb3 = lb_all.reshape(n_layers, 1, D)
    even = (cb3, lg3, lnb3, P["ev_pool_w"], pb3, ps3)

    saved = []
    deps = tuple(first_deps)
    for layer in range(n_layers):
        j = layer // 2
        w_in, w_out = _mixer_names(layer)
        W = {}
        s = {"h": h, "W": W}
        s["n"] = _rms_fwd(h, mix_g, layer, "mix_norm_0", deps=deps) if layer == 0 else n_next
        deps = ()
        W[w_in], held = weights(layer, w_in, (s["n"],))
        s["u"] = _mm_nn(s["n"], W[w_in], 0, f"mix_in_{layer}", deps=held)
        if layer % 2 == 0:
            s["y"], s["yc"] = _even_fwd(s["u"], P["conv_w_full"], *even, j, f"even_fwd_{layer}")
        else:
            s["y"], s["o"], s["sall"] = _hgrn_fwd(s["u"], lb3, layer, gn3, j, f"hgrn_fwd_{layer}")
        W[w_out], held = weights(layer, w_out, (s["y"],))
        if layer == 0:
            h, s["n2"] = _mm_nn_norm(s["y"], W[w_out], 0, h, mlp_g, layer, "mix_out_0", deps=held)
            s["h1"] = h
            W["mlp_w1"], held = weights(layer, "mlp_w1", (s["n2"],))
            s["relu"] = _mm_nn(s["n2"], W["mlp_w1"], 0, "mlp_up_0", relu=True, deps=held)
            W["mlp_w2"], held = weights(layer, "mlp_w2", (s["relu"],))
            h, n_next = _mm_nn_norm(s["relu"], W["mlp_w2"], 0, h, mix_g, 1, "mlp_down_0", square=True, deps=held)
        else:
            W["mlp_w1"], more1 = weights(layer, "mlp_w1", (s["y"],))
            W["mlp_w2"], more2 = weights(layer, "mlp_w2", (s["y"],))
            last = layer + 1 == n_layers
            out = _tail_fwd(s["y"], W[w_out], h, mlp_g, layer, W["mlp_w1"], W["mlp_w2"], None if last else mix_g,
                            f"tail_{layer}", deps=held + more1 + more2)
            s["h1"], s["n2"], h, s["relu"] = out[0], out[1], out[2], out[-1]
            n_next = None if last else out[3]
        saved.append(s)

    dh, dhb, dg_final, loss = _final(h, P["final_norm_g"].reshape(1, D), target)

    small = {"final_norm_g": dg_final}
    per_layer = {k: [None] * n_layers for k in ("mix_norm_g", "mlp_norm_g", "lb")}
    per_pair = {k: [None] * (n_layers // 2) for k in
                ("ev_conv_w", "ev_conv_b", "ev_ln_g", "ev_ln_b", "ev_pool_w", "ev_pool_b", "ev_pool_scale", "od_gnorm_g")}
    for layer in reversed(range(n_layers)):
        j = layer // 2
        s = saved[layer]
        W = s["W"]
        w_in, w_out = _mixer_names(layer)
        dw2 = _mm_tn(s["relu"], dhb, "row", f"dw2_{layer}", square=True)
        dz, dh, dhb, per_layer["mlp_norm_g"][layer] = _mlp_bwd(
            dhb, s["relu"], W["mlp_w1"], W["mlp_w2"], s["h1"], mlp_g, layer, dh, f"mlp_bwd_{layer}", deps=deps + (dw2,))
        dw1 = _mm_tn(s["n2"], dz, "col", f"dw1_{layer}")
        deps = boundary(f"mlp{layer}", {("mlp_w1", layer): dw1, ("mlp_w2", layer): dw2}, (dhb, dw1, dw2))
        dy = _mm_nt(dhb, W[w_out], 0, f"d_y_{layer}", deps=deps)
        dwout = _mm_tn(s["y"], dhb, "row", f"dwout_{layer}")
        if layer % 2 == 0:
            du, dcw, dcb, dlg, dlnb, dpw, dpb, dps = _even_bwd(s["u"], s["yc"], dy, P["conv_w_full"], *even, j, f"even_bwd_{layer}")
            for k, val in (("ev_conv_w", dcw), ("ev_conv_b", dcb), ("ev_ln_g", dlg), ("ev_ln_b", dlnb),
                           ("ev_pool_w", dpw), ("ev_pool_b", dpb), ("ev_pool_scale", dps)):
                per_pair[k][j] = val
        else:
            du, per_layer["lb"][layer], per_pair["od_gnorm_g"][j] = _hgrn_bwd(
                s["u"], s["o"], dy, s["sall"], lb3, layer, gn3, j, f"hgrn_bwd_{layer}")
        dwin = _mm_tn(s["n"], du, "col", f"dwin_{layer}")
        deps = boundary(f"mix{layer}", {(w_in, j): dwin, (w_out, j): dwout}, (du, dwin, dwout))
        dh, dhb, per_layer["mix_norm_g"][layer] = _mm_nt_norm(du, W[w_in], 0, s["h"], mix_g, layer, dh, f"d_n_{layer}", deps=deps)
        deps = ()

    small["mix_norm_g"] = jnp.concatenate(per_layer["mix_norm_g"], axis=0)
    small["mlp_norm_g"] = jnp.concatenate(per_layer["mlp_norm_g"], axis=0)
    dlb_all = jnp.concatenate([jnp.zeros((1, D), F32) if g is None else g for g in per_layer["lb"]], axis=0)
    small["lb_param"] = _lb_bwd(P["lb_param"], dlb_all)
    for k, vals in per_pair.items():
        small[k] = jnp.stack(vals, axis=0)
    small["meta_tokens"] = dh[PAD:LEAD]
    return loss, dh, small


def kernel(x, meta_tokens, mix_norm_g, mlp_norm_g, final_norm_g, ev_w_in, ev_conv_w, ev_conv_b, ev_ln_g, ev_ln_b, ev_pool_w, ev_pool_b, ev_pool_scale, ev_w_out, od_w_in, od_gnorm_g, od_w_out, lb_param, mlp_w1, mlp_w2, loss_target, m_meta_tokens, m_mix_norm_g, m_mlp_norm_g, m_final_norm_g, m_ev_w_in, m_ev_conv_w, m_ev_conv_b, m_ev_ln_g, m_ev_ln_b, m_ev_pool_w, m_ev_pool_b, m_ev_pool_scale, m_ev_w_out, m_od_w_in, m_od_gnorm_g, m_od_w_out, m_lb_param, m_mlp_w1, m_mlp_w2, v_meta_tokens, v_mix_norm_g, v_mlp_norm_g, v_final_norm_g, v_ev_w_in, v_ev_conv_w, v_ev_conv_b, v_ev_ln_g, v_ev_ln_b, v_ev_pool_w, v_ev_pool_b, v_ev_pool_scale, v_ev_w_out, v_od_w_in, v_od_gnorm_g, v_od_w_out, v_lb_param, v_mlp_w1, v_mlp_w2):
    given = dict(locals())
    w = {n: given[n] for n in WEIGHTS}
    m = {n: given["m_" + n] for n in WEIGHTS}
    v = {n: given["v_" + n] for n in WEIGHTS}
    n_layers = mix_norm_g.shape[0]
    core = lax.axis_index("c").astype(jnp.int32)
    chip = (2 * lax.axis_index("x") + lax.axis_index("y")).astype(jnp.int32)
    chip1 = chip.reshape(1)
    ids2 = jnp.stack([core, chip])

    conv_pad = jnp.pad(ev_conv_w, ((0, 0), (0, CONV_ROWS - CONV_WIDTH), (0, 0)))
    stages = [[(0, n)] for n in (*_mixer_names(0), "mlp_w1", "mlp_w2")]
    for layer in range(1, n_layers):
        stages += [[(layer, n) for n in _mixer_names(layer)], [(layer, "mlp_w1"), (layer, "mlp_w2")]]
    where, stage_kinds, stage_bufs = {}, [], []
    for k, stage in enumerate(stages):
        index = [layer if n.startswith("mlp") else layer // 2 for layer, n in stage]
        kinds = [BIG[n] for _, n in stage]
        bufs = [_cast_place(w[n], i, BIG[n], chip1, BF16, f"place_{n}_{i}") for (_, n), i in zip(stage, index)]
        if k == 0:
            bufs.append(_cast_place(meta_tokens[None], 0, "col", chip1, F32, "place_meta"))
            bufs.append(_cast_place(conv_pad.reshape(1, -1, conv_pad.shape[2]), 0, "col", chip1, F32, "place_conv_w"))
            kinds += ["col", "col"]
        stage_kinds.append(kinds)
        stage_bufs.append(bufs)
        where.update({key: (k, f) for f, key in enumerate(stage)})
    gathers, token, early = [], (), 3
    for lo, hi, name in ((0, early, "gather_start_first"), (early, len(stages), "gather_start_rest")):
        every = [b for bufs in stage_bufs[lo:hi] for b in bufs]
        kinds_all = [kd for kinds in stage_kinds[lo:hi] for kd in kinds]
        ss, rs, every, tok = _split_start(name, _gather_plan(kinds_all, True), every, 3 * len(every), deps=token)
        token = (tok,)
        at = 0
        for kinds in stage_kinds[lo:hi]:
            gathers.append((kinds, _gather_plan(kinds, True, first=at), ss, rs, every[at:at + len(kinds)]))
            at += len(kinds)

    landed, passed, held = {}, {}, []

    def hand_on(k, deps):
        if k not in passed:
            kinds, plan, ss, rs, bufs = gathers[k]
            to_sibling = _gather_plan(kinds, False)
            ss, rs, bufs, tok = _split_start(f"gather_pass_{k}", to_sibling, bufs, 3 * len(bufs), deps=deps, earlier=(plan, ss, rs))
            passed[k] = (to_sibling, ss, rs, bufs)
            held.append(tok)

    def arrived(k, after):
        if k not in landed:
            hand_on(k, after)
            landed[k] = _split_wait(f"gather_wait_{k}", *passed[k], after)
        return landed[k]

    def weights(layer, name, after):
        k, f = where[(layer, name)]
        full = arrived(k, after)[f][None]
        if layer == 0 and name != "mlp_w2":
            hand_on(k + 1, after)
        if name == "mlp_w2" and layer + 1 < n_layers:
            hand_on(where[(layer + 1, _mixer_names(layer + 1)[0])][0], after)
        if layer > 0 and name == _mixer_names(layer)[0]:
            hand_on(where[(layer, "mlp_w1")][0], after)
        tokens = tuple(held)
        held.clear()
        return full, tokens

    first = arrived(0, token)
    P = {n: w[n] for n in ("mix_norm_g", "mlp_norm_g", "final_norm_g", "ev_conv_b", "ev_ln_g", "ev_ln_b", "ev_pool_w",
                           "ev_pool_b", "ev_pool_scale", "od_gnorm_g", "lb_param")}
    P["meta_full"] = first[1]
    P["conv_w_full"] = first[2].reshape(ev_conv_w.shape[0], CONV_ROWS, -1)

    pending, outs = [], {n: None for n in BIG}

    def advance(after, fresh=1):
        ready, still = [], []
        for pos, st in enumerate(pending):
            if st["phase"] == 1 and pos >= len(pending) - fresh:
                still.append(st)
            elif st["phase"] == 1:
                bufs = _split_wait(f"reduce_wait_{st['tag']}", st["plan"], st["ss"], st["rs"], st["bufs"], after)
                halves = [_sum_pieces(ids2, bufs[2 * f], bufs[2 * f + 1], f"sum_{st['tag']}_{f}") for f in range(len(bufs) // 2)]
                ready.append((st, halves))
            else:
                grads = _split_wait(f"swap_wait_{st['tag']}", st["plan"], st["ss"], st["rs"], st["bufs"], after)
                for (n, i), g in zip(st["keys"], grads):
                    outs[n] = _adamw_layer(w[n], m[n], v[n], g.reshape(w[n].shape[1:]), i, outs[n], f"adamw_{n}_{i}")
        pending[:] = still
        return ready

    def launch(name, ready, tag=None, grads=None):
        bufs, parts, entries, at = [], [], [], 0
        for st, halves in ready:
            plan = _swap_plan(first=at)
            entries.append((dict(st, phase=2, plan=plan), len(bufs), len(halves)))
            parts.append((plan, len(halves)))
            bufs += halves
            at += len(halves)
        if grads is not None:
            pairs = []
            for acc in grads.values():
                pairs += [acc, lax.empty((N_DEV - 1,) + acc.shape[2:], BF16)]
            plan = _reduce_plan(first=at)
            entries.append((dict(phase=1, tag=tag, keys=list(grads), plan=plan), len(bufs), len(pairs)))
            parts.append((plan, len(pairs)))
            bufs += pairs
            at += 7 * len(grads)
        if not bufs:
            return ()
        ss, rs, bufs, tok = _split_start(name, _joined(parts), bufs, at)
        for st, lo, n in entries:
            pending.append(dict(st, ss=ss, rs=rs, bufs=bufs[lo:lo + n]))
        return (tok,)

    def boundary(tag, grads, after):
        return launch(f"start_{tag}", advance(after), tag, grads)

    loss, dh, small = _local_step(x[0], loss_target[0], P, weights, boundary, first_deps=token)

    order = [n for n in WEIGHTS if n not in BIG]
    block = _pack([small[n] for n in order] + [loss])
    ss, rs, bufs, tok = _split_start("small_start", _small_plan, [block, lax.empty((N_DEV,) + block.shape, F32)], N_DEV - 1)
    for last in range(3):
        launch(f"start_end_{last}", advance((tok,) + tuple(o[0] for o in outs.values() if o is not None), fresh=0))
    assert not pending
    block, land = _split_wait("small_wait", _small_plan, ss, rs, bufs, tuple(outs[n][0] for n in BIG))
    packed = _sum_blocks((4 * lax.axis_index("x") + 2 * lax.axis_index("y") + lax.axis_index("c")).astype(jnp.int32).reshape(1), block, land)
    total = _unpack(packed, [small[n].shape for n in order] + [loss.shape])
    loss_sum = total[-1][0, 0]
    gsmall = dict(zip(order, total[:-1]))
    gsmall["meta_tokens"] = lax.dynamic_slice_in_dim(gsmall["meta_tokens"], chip * meta_tokens.shape[1], meta_tokens.shape[1], 1)
    gsmall["ev_conv_w"] = lax.dynamic_slice_in_dim(gsmall["ev_conv_w"][:, :CONV_WIDTH], chip * ev_conv_w.shape[2], ev_conv_w.shape[2], 2)

    g_out, d_out, m_out, v_out = {}, {}, {}, {}
    for n in BIG:
        g_out[n], d_out[n], m_out[n], v_out[n] = outs[n]
    items = []
    for n in order:
        cols = w[n].shape[-1] if w[n].ndim > 1 else 128
        items.append([a.reshape(-1, cols) for a in (w[n], gsmall[n], m[n], v[n])])
    for n, (d_, m_, v_) in zip(order, _adamw_small(items)):
        shape = w[n].shape
        g_out[n], d_out[n], m_out[n], v_out[n] = gsmall[n].reshape(shape), d_.reshape(shape), m_.reshape(shape), v_.reshape(shape)

    grad_x = dh[LEAD:][None]
    return (loss_sum, grad_x, *[g_out[n] for n in WEIGHTS], *[d_out[n] for n in WEIGHTS],
            *[m_out[n] for n in WEIGHTS], *[v_out[n] for n in WEIGHTS])
```

```python
import functools

import jax
import jax.numpy as jnp
from jax import lax
from jax.experimental import pallas as pl
from jax.experimental.pallas import tpu as pltpu

F32 = jnp.float32
BF16 = jnp.bfloat16
SDS = jax.ShapeDtypeStruct
MESH = pl.DeviceIdType.MESH
ANY_SPEC = pl.BlockSpec(memory_space=pl.ANY)

N_META = 16
CHUNK = 64
LEAD = CHUNK
PAD = LEAD - N_META
CONV_WIDTH = 31
CONV_ROWS = 32
POOL_WINDOWS = (2, 4, 8, 16)
HEAD_DIM = 128
SUB = 16
EXP_CAP = 80.0
EPS = 1e-6
ADAM_LR = 0.001
ADAM_B1 = 0.9
ADAM_B2 = 0.999
ADAM_EPS = 1e-08
ADAM_WD = 0.01
ADAM_STEP = 10
N_CHIPS = 4
VMEM_LIMIT = 58 << 20
MM_VMEM_BUDGET = 50 << 20


def _params(*sem):
    return pltpu.CompilerParams(dimension_semantics=sem if sem else None, vmem_limit_bytes=VMEM_LIMIT)


def _tile(n, target, unit=CHUNK):
    best = None
    for t in range(unit, min(n, target) + 1, unit):
        if n % t == 0:
            best = t
    assert best is not None, (n, target, unit)
    return best


def _ctile(n, target=512):
    for t in (512, 384, 256, 128):
        if t <= target and n % t == 0:
            return t
    raise ValueError(n)


def _mm_tiles(M, N, per_row, per_col, per_elem):
    best = None
    for tn in (512, 384, 256, 128):
        if N % tn:
            continue
        for tm in sorted((d for d in range(16, M + 1, 16) if M % d == 0), reverse=True):
            if 2 * (tm * per_row + tn * per_col + tm * tn * per_elem) <= MM_VMEM_BUDGET:
                if best is None or tm * tn > best[0] * best[1]:
                    best = (tm, tn)
                break
    assert best is not None, (M, N)
    return best


def _sigmoid(x):
    return 1.0 / (1.0 + jnp.exp(-x))


def _mult(v, m):
    return v if isinstance(v, int) else pl.multiple_of(v, m)


def _row_ids(shape, base):
    return lax.broadcasted_iota(jnp.int32, shape, 0) + base


def _cast_place(w3, layer, kind, chip1, dtype, name):
    _, ks, ns = w3.shape
    tr = _tile(ks, 512, 16)
    full = (ks, ns * N_CHIPS) if kind == "col" else (ks * N_CHIPS, ns)

    def body(chip_ref, w_ref, o_ref):
        del chip_ref
        o_ref[...] = w_ref[...].astype(dtype)

    omap = (lambda i, chip: (i, chip[0])) if kind == "col" else (lambda i, chip: (chip[0] * (ks // tr) + i, 0))
    return pl.pallas_call(
        body,
        grid_spec=pltpu.PrefetchScalarGridSpec(
            num_scalar_prefetch=1, grid=(ks // tr,),
            in_specs=[pl.BlockSpec((None, tr, ns), lambda i, chip: (layer, i, 0))],
            out_specs=pl.BlockSpec((tr, ns), omap)),
        out_shape=SDS(full, dtype), name=name, compiler_params=_params("parallel"))(chip1, w3)


def _rms_fwd(h, g3, layer, name, deps=()):
    T, D = h.shape
    tm = _tile(T, 832)

    def body(h_ref, g_ref, *rest):
        n_ref = rest[-1]
        x = h_ref[...]
        r = lax.rsqrt(jnp.mean(x * x, axis=-1, keepdims=True) + EPS)
        n_ref[...] = ((x * r) * g_ref[...]).astype(BF16)

    return pl.pallas_call(
        body, grid=(T // tm,),
        in_specs=[pl.BlockSpec((tm, D), lambda i: (i, 0)), pl.BlockSpec((None, 1, D), lambda i: (layer, 0, 0))]
        + [ANY_SPEC] * len(deps),
        out_specs=pl.BlockSpec((tm, D), lambda i: (i, 0)), out_shape=SDS((T, D), BF16),
        name=name, compiler_params=_params("parallel"))(h, g3, *deps)


def _final(h, g2, target):
    T, D = h.shape
    tm = _tile(T, 320)
    nsub = tm // CHUNK
    nblk = target.shape[0] // CHUNK

    def body(h_ref, g_ref, *rest):
        t_refs = rest[:nsub]
        dh_ref, dhb_ref, dg_ref, loss_ref = rest[nsub:]
        i = pl.program_id(0)

        @pl.when(i == 0)
        def _():
            dg_ref[...] = jnp.zeros_like(dg_ref)
            loss_ref[...] = jnp.zeros_like(loss_ref)

        g = g_ref[...]
        for q in range(nsub):
            rows = slice(q * CHUNK, (q + 1) * CHUNK)
            x = h_ref[rows, :]
            r = lax.rsqrt(jnp.mean(x * x, axis=-1, keepdims=True) + EPS)
            xh = x * r
            live = jnp.where(i * nsub + q > 0, 1.0, 0.0).astype(F32)
            e = ((xh * g) - t_refs[q][...]) * live
            dy = e * (1.0 / D)
            dxh = dy * g
            dh = r * (dxh - xh * jnp.mean(dxh * xh, axis=-1, keepdims=True))
            dh_ref[rows, :] = dh
            dhb_ref[rows, :] = dh.astype(BF16)
            dg_ref[...] += jnp.sum(dy * xh, axis=0, keepdims=True)
            loss_ref[...] += jnp.sum(e * e) * (0.5 / D)

    row = pl.BlockSpec((tm, D), lambda i: (i, 0))
    t_specs = [pl.BlockSpec((CHUNK, D), functools.partial(lambda i, q: (jnp.clip(i * nsub + q - 1, 0, nblk - 1), 0), q=q))
               for q in range(nsub)]
    return pl.pallas_call(
        body, grid=(T // tm,),
        in_specs=[row, pl.BlockSpec((1, D), lambda i: (0, 0))] + t_specs,
        out_specs=[row, row, pl.BlockSpec((1, D), lambda i: (0, 0)), pl.BlockSpec((1, 128), lambda i: (0, 0))],
        out_shape=[SDS((T, D), F32), SDS((T, D), BF16), SDS((1, D), F32), SDS((1, 128), F32)],
        name="final_loss", compiler_params=_params("arbitrary"))(h, g2, *([target] * nsub))


def _mm_nn(a, w3, layer, name, res=None, relu=False, square=False, deps=()):
    M, K = a.shape
    N = w3.shape[2]
    tm, tn = _mm_tiles(M, N, 2 * K, 2 * K, (2 if relu else 4) + (4 if res is not None else 0))

    def body(*refs):
        lhs = refs[0][...]
        acc = jnp.dot(lhs * lhs if square else lhs, refs[1][...], preferred_element_type=F32)
        if res is not None:
            acc = acc + refs[2][...]
        refs[-1][...] = jnp.maximum(acc, 0.0).astype(BF16) if relu else acc

    in_specs = [pl.BlockSpec((tm, K), lambda i, j: (i, 0)), pl.BlockSpec((None, K, tn), lambda i, j: (layer, 0, j))]
    args = [a, w3]
    tile = pl.BlockSpec((tm, tn), lambda i, j: (i, j))
    if res is not None:
        in_specs.append(tile)
        args.append(res)
    in_specs += [ANY_SPEC] * len(deps)
    args += list(deps)
    return pl.pallas_call(
        body, grid=(M // tm, N // tn), in_specs=in_specs, out_specs=tile,
        out_shape=SDS((M, N), BF16 if relu else F32),
        name=name, compiler_params=_params("parallel", "parallel"))(*args)


def _mm_nt(dy, w3, layer, name, relu=None, deps=()):
    M, N = dy.shape
    K = w3.shape[1]
    tm, tk = _mm_tiles(M, K, 2 * N, 2 * N, 4)

    def body(*refs):
        acc = lax.dot_general(refs[0][...], refs[1][...], (((1,), (1,)), ((), ())), preferred_element_type=F32)
        if relu is not None:
            acc = (acc * (2.0 * refs[2][...].astype(F32))).astype(BF16)
        refs[-1][...] = acc

    tile = pl.BlockSpec((tm, tk), lambda i, j: (i, j))
    in_specs = [pl.BlockSpec((tm, N), lambda i, j: (i, 0)), pl.BlockSpec((None, tk, N), lambda i, j: (layer, j, 0))]
    args = [dy, w3]
    if relu is not None:
        in_specs.append(tile)
        args.append(relu)
    in_specs += [ANY_SPEC] * len(deps)
    args += list(deps)
    return pl.pallas_call(
        body, grid=(M // tm, K // tk), in_specs=in_specs, out_specs=tile,
        out_shape=SDS((M, K), F32 if relu is None else BF16),
        name=name, compiler_params=_params("parallel", "parallel"))(*args)


def _row_tile(M, per_row, fixed):
    for tm in sorted((d for d in range(16, M + 1, 16) if M % d == 0), reverse=True):
        if 2 * (tm * per_row + fixed) <= MM_VMEM_BUDGET:
            return tm
    raise ValueError((M, per_row, fixed))


def _mm_nn_norm(a, w3, layer, res, g3, glayer, name, square=False, deps=()):
    M, K = a.shape
    D = w3.shape[2]
    tm = _row_tile(M, 2 * K + 10 * D, 2 * K * D)

    def body(a_ref, w_ref, r_ref, g_ref, *rest):
        h_ref, n_ref = rest[-2:]
        lhs = a_ref[...]
        x = r_ref[...] + jnp.dot(lhs * lhs if square else lhs, w_ref[...], preferred_element_type=F32)
        h_ref[...] = x
        r = lax.rsqrt(jnp.mean(x * x, axis=-1, keepdims=True) + EPS)
        n_ref[...] = ((x * r) * g_ref[...]).astype(BF16)

    row = pl.BlockSpec((tm, D), lambda i: (i, 0))
    return pl.pallas_call(
        body, grid=(M // tm,),
        in_specs=[pl.BlockSpec((tm, K), lambda i: (i, 0)), pl.BlockSpec((None, K, D), lambda i: (layer, 0, 0)), row,
                  pl.BlockSpec((None, 1, D), lambda i: (glayer, 0, 0))] + [ANY_SPEC] * len(deps),
        out_specs=[row, row], out_shape=[SDS((M, D), F32), SDS((M, D), BF16)],
        name=name, compiler_params=_params("parallel"))(a, w3, res, g3, *deps)


def _tail_fwd(y, w_out, res, mlp_g3, layer, w1, w2, next_g3, name, deps=()):
    M, K = y.shape
    D = w_out.shape[2]
    F = w1.shape[2]
    hb = _ctile(F)
    more = next_g3 is not None
    tm = _row_tile(M, 2 * K + 18 * D + (2 * D if more else 0) + 2 * F, 2 * K * D + 2 * D * F)

    def body(y_ref, wo_ref, res_ref, g_ref, w1_ref, w2_ref, *rest):
        outs = rest[-5:] if more else rest[-4:]
        h1 = res_ref[...] + jnp.dot(y_ref[...], wo_ref[...], preferred_element_type=F32)
        outs[0][...] = h1
        n2 = ((h1 * lax.rsqrt(jnp.mean(h1 * h1, axis=-1, keepdims=True) + EPS)) * g_ref[...]).astype(BF16)
        outs[1][...] = n2
        acc = h1
        for jb in range(F // hb):
            cols = slice(jb * hb, (jb + 1) * hb)
            r = jnp.maximum(jnp.dot(n2, w1_ref[:, cols], preferred_element_type=F32), 0.0).astype(BF16)
            outs[-1][:, cols] = r
            acc = acc + jnp.dot(r * r, w2_ref[cols, :], preferred_element_type=F32)
        outs[2][...] = acc
        if more:
            outs[3][...] = ((acc * lax.rsqrt(jnp.mean(acc * acc, axis=-1, keepdims=True) + EPS)) * rest[0][...]).astype(BF16)

    row = pl.BlockSpec((tm, D), lambda i: (i, 0))
    once = dict(pipeline_mode=pl.Buffered(1))
    in_specs = [pl.BlockSpec((tm, K), lambda i: (i, 0)), pl.BlockSpec((None, K, D), lambda i: (0, 0, 0), **once), row,
                pl.BlockSpec((None, 1, D), lambda i: (layer, 0, 0)),
                pl.BlockSpec((None, D, F), lambda i: (0, 0, 0), **once), pl.BlockSpec((None, F, D), lambda i: (0, 0, 0), **once)]
    args = [y, w_out, res, mlp_g3, w1, w2]
    out_specs, out_shape = [row, row, row], [SDS((M, D), F32), SDS((M, D), BF16), SDS((M, D), F32)]
    if more:
        in_specs.append(pl.BlockSpec((None, 1, D), lambda i: (layer + 1, 0, 0)))
        args.append(next_g3)
        out_specs.append(row)
        out_shape.append(SDS((M, D), BF16))
    out_specs.append(pl.BlockSpec((tm, F), lambda i: (i, 0)))
    out_shape.append(SDS((M, F), BF16))
    in_specs += [ANY_SPEC] * len(deps)
    args += list(deps)
    return pl.pallas_call(
        body, grid=(M // tm,), in_specs=in_specs, out_specs=out_specs, out_shape=out_shape,
        name=name, compiler_params=_params("parallel"))(*args)


def _mlp_bwd(dhb, relu, w1, w2, h, g3, glayer, dh_in, name, deps=()):
    M, D = dhb.shape
    F = w1.shape[2]
    hb = _ctile(F)
    tm = _row_tile(M, 16 * D + 4 * F, 2 * D * F)

    def body(dy_ref, r_ref, w1_ref, w2_ref, h_ref, g_ref, dhi_ref, *rest):
        dz_ref, dh_ref, dhb_ref, dg_ref = rest[-4:]
        dy = dy_ref[...]
        dn = jnp.zeros((tm, D), F32)
        for jb in range(F // hb):
            cols = slice(jb * hb, (jb + 1) * hb)
            dact = lax.dot_general(dy, w2_ref[cols, :], (((1,), (1,)), ((), ())), preferred_element_type=F32)
            dz = (dact * (2.0 * r_ref[:, cols].astype(F32))).astype(BF16)
            dz_ref[:, cols] = dz
            dn = dn + lax.dot_general(dz, w1_ref[:, cols], (((1,), (1,)), ((), ())), preferred_element_type=F32)
        x = h_ref[...]
        r = lax.rsqrt(jnp.mean(x * x, axis=-1, keepdims=True) + EPS)
        xh = x * r
        dxh = dn * g_ref[...]
        dh = dhi_ref[...] + r * (dxh - xh * jnp.mean(dxh * xh, axis=-1, keepdims=True))
        dh_ref[...] = dh
        dhb_ref[...] = dh.astype(BF16)

        @pl.when(pl.program_id(0) == 0)
        def _():
            dg_ref[...] = jnp.zeros_like(dg_ref)

        dg_ref[...] += jnp.sum(dn * xh, axis=0, keepdims=True)

    row = pl.BlockSpec((tm, D), lambda i: (i, 0))
    wide = pl.BlockSpec((tm, F), lambda i: (i, 0))
    once = dict(pipeline_mode=pl.Buffered(1))
    return pl.pallas_call(
        body, grid=(M // tm,),
        in_specs=[row, wide, pl.BlockSpec((None, D, F), lambda i: (0, 0, 0), **once),
                  pl.BlockSpec((None, F, D), lambda i: (0, 0, 0), **once), row,
                  pl.BlockSpec((None, 1, D), lambda i: (glayer, 0, 0)), row] + [ANY_SPEC] * len(deps),
        out_specs=[wide, row, row, pl.BlockSpec((1, D), lambda i: (0, 0))],
        out_shape=[SDS((M, F), BF16), SDS((M, D), F32), SDS((M, D), BF16), SDS((1, D), F32)],
        name=name, compiler_params=_params("arbitrary"))(dhb, relu, w1, w2, h, g3, dh_in, *deps)


def _mm_nt_norm(dy, w3, layer, h, g3, glayer, dh_in, name, deps=(), lead=0):
    M, N = dy.shape
    D = w3.shape[1]
    tm = _row_tile(M, 2 * N + 14 * D, 2 * N * D)
    steps = M // tm
    assert lead % 8 == 0 and lead < tm

    def body(dy_ref, w_ref, h_ref, g_ref, dhi_ref, *rest):
        dn = lax.dot_general(dy_ref[...], w_ref[...], (((1,), (1,)), ((), ())), preferred_element_type=F32)
        x = h_ref[...]
        r = lax.rsqrt(jnp.mean(x * x, axis=-1, keepdims=True) + EPS)
        xh = x * r
        dxh = dn * g_ref[...]
        dh = dhi_ref[...] + r * (dxh - xh * jnp.mean(dxh * xh, axis=-1, keepdims=True))
        i = pl.program_id(0)
        if lead:
            top_ref, tail_ref, dg_ref, buf, sem = rest[len(deps):]

            def copy(step, slot):
                if isinstance(step, int) and step == 0:
                    return pltpu.make_async_copy(buf.at[slot, pl.ds(lead, tm - lead)], tail_ref.at[pl.ds(0, tm - lead)], sem.at[slot])
                return pltpu.make_async_copy(buf.at[slot], tail_ref.at[pl.ds(pl.multiple_of(step * tm - lead, 8), tm)], sem.at[slot])

            slot = i % 2

            if steps > 2:
                pl.when(i == 2)(lambda: copy(0, 0).wait())
                pl.when(i > 2)(lambda: copy(i - 2, slot).wait())
            buf[slot] = dh

            @pl.when(i == 0)
            def _():
                top_ref[...] = dh[:lead]
                copy(0, 0).start()

            if steps > 1:
                pl.when(i > 0)(lambda: copy(i, slot).start())

            @pl.when(i == steps - 1)
            def _():
                for step in range(max(steps - 2, 0), steps):
                    copy(step, step % 2).wait()
        else:
            dh_ref, dhb_ref, dg_ref = rest[len(deps):]
            dh_ref[...] = dh
            dhb_ref[...] = dh.astype(BF16)

        @pl.when(i == 0)
        def _():
            dg_ref[...] = jnp.zeros_like(dg_ref)

        dg_ref[...] += jnp.sum(dn * xh, axis=0, keepdims=True)

    row = pl.BlockSpec((tm, D), lambda i: (i, 0))
    one = pl.BlockSpec((1, D), lambda i: (0, 0))
    if lead:
        outs = dict(out_specs=[pl.BlockSpec((lead, D), lambda i: (0, 0)), ANY_SPEC, one],
                    out_shape=[SDS((lead, D), F32), SDS((M - lead, D), F32), SDS((1, D), F32)],
                    scratch_shapes=[pltpu.VMEM((2, tm, D), F32), pltpu.SemaphoreType.DMA((2,))])
    else:
        outs = dict(out_specs=[row, row, one], out_shape=[SDS((M, D), F32), SDS((M, D), BF16), SDS((1, D), F32)])
    return pl.pallas_call(
        body, grid=(steps,),
        in_specs=[pl.BlockSpec((tm, N), lambda i: (i, 0)), pl.BlockSpec((None, D, N), lambda i: (layer, 0, 0)), row,
                  pl.BlockSpec((None, 1, D), lambda i: (glayer, 0, 0)), row] + [ANY_SPEC] * len(deps),
        name=name, compiler_params=_params("arbitrary"), **outs)(dy, w3, h, g3, dh_in, *deps)


def _fam_dims(kind, K, N):
    return (K // 2, N // N_CHIPS) if kind == "col" else (K // (2 * N_CHIPS), N)


def _mm_tn(x, dy, kind, name, square=False):
    M, K = x.shape
    N = dy.shape[1]
    nr, nc = _fam_dims(kind, K, N)

    def body(x_ref, dy_ref, o_ref):
        lhs = x_ref[...]
        res = lax.dot_general(lhs * lhs if square else lhs, dy_ref[...], (((0,), (0,)), ((), ())), preferred_element_type=F32)
        o_ref[...] = res.astype(BF16).reshape(o_ref.shape)

    if kind == "col":
        tn = _ctile(nc)
        ct = nc // tn
        grid = (N // tn,)
        in_specs = [pl.BlockSpec((M, K), lambda j: (0, 0)), pl.BlockSpec((M, tn), lambda j: (0, j))]
        out_spec = pl.BlockSpec((2, None, nr, tn), lambda j: (0, j // ct, 0, j % ct))
    else:
        grid = (N_CHIPS,)
        in_specs = [pl.BlockSpec((M, 2 * nr), lambda i: (0, i)), pl.BlockSpec((M, N), lambda i: (0, 0))]
        out_spec = pl.BlockSpec((2, None, nr, N), lambda i: (0, i, 0, 0))
    return pl.pallas_call(
        body, grid=grid, in_specs=in_specs, out_specs=out_spec, out_shape=SDS((2, N_CHIPS, nr, nc), BF16),
        name=name, compiler_params=_params("parallel"))(x, dy)


C_EVEN = 512


def _live(rows, base, total):
    r = _row_ids((rows, 1), base)
    return jnp.logical_and(r >= PAD, r < total).astype(F32)


def _conv_taps(win, w_ref, ls, acc, flip):
    for b in range(8):
        rb = win if b == 0 else pltpu.roll(win, 96 - b, 0)
        for a in range(5):
            o = 8 * a + b
            tap = (30 - o) if flip else (o - 2)
            if 0 <= tap < CONV_WIDTH:
                acc = acc + w_ref[pl.ds(tap, 1), ls] * rb[8 * a:8 * a + CHUNK]
    return acc


def _window_sum(win, levels, forward):
    s = win
    n = win.shape[0]
    for k in range(levels):
        step = 1 << k
        s = s + pltpu.roll(s, (n - step) if forward else step, 0)
    return s


def _pool_count(base, g):
    pos = _row_ids((CHUNK, 1), base) - PAD
    return jnp.clip(pos + 1, 1, POOL_WINDOWS[g]).astype(F32)


def _even_fwd(u, cw3, cb3, lg3, lb3, pw4, pb3, ps3, j, name):
    T = u.shape[0]
    C = C_EVEN
    tm = _tile(T, 320)
    nch = tm // CHUNK
    nblk = T // CHUNK

    def body(u_ref, up_ref, cw_ref, cb_ref, lg_ref, lb_ref, pw_ref, pb_ref, ps_ref, o_ref, yc_ref, a_s, p_s, yc_s):
        row0 = pl.program_id(0) * tm
        up = up_ref[...]
        lp = _live(CHUNK, row0 - CHUNK, T)
        a_s[0:CHUNK, :] = up[:, 0:C] * _sigmoid(up[:, C:2 * C]) * lp
        p_s[0:CHUNK, :] = up[:, 2 * C:3 * C] * lp

        def stage(c, _):
            rs = _mult(c * CHUNK, CHUNK)
            lv = _live(CHUNK, row0 + rs, T)
            a_s[pl.ds(rs + CHUNK, CHUNK), :] = u_ref[pl.ds(rs, CHUNK), 0:C] * _sigmoid(u_ref[pl.ds(rs, CHUNK), C:2 * C]) * lv
            p_s[pl.ds(rs + CHUNK, CHUNK), :] = u_ref[pl.ds(rs, CHUNK), 2 * C:3 * C] * lv
            return 0

        for c in range(nch):
            stage(c, 0)

        def chunk(c, _):
            rs = _mult(c * CHUNK, CHUNK)
            lv = _live(CHUNK, row0 + rs, T)
            for cb in range(4):
                ls = slice(cb * 128, (cb + 1) * 128)
                win = a_s[pl.ds(_mult(rs + 32, 32), 96), ls]
                acc = jnp.broadcast_to(cb_ref[:, ls], (CHUNK, 128))
                yc_s[:, ls] = _conv_taps(win, cw_ref, ls, acc, False)
            y = yc_s[...]
            yc_ref[pl.ds(rs, CHUNK), :] = y
            xc = y - jnp.mean(y, axis=-1, keepdims=True)
            yn = xc * lax.rsqrt(jnp.mean(xc * xc, axis=-1, keepdims=True) + EPS) * lg_ref[...] + lb_ref[...]
            o_ref[pl.ds(rs, CHUNK), 0:C] = (yn * _sigmoid(yn) * lv).astype(BF16)
            for g in range(4):
                ls = slice(g * 128, (g + 1) * 128)
                win = p_s[pl.ds(_mult(rs + 48, 16), 80), ls]
                s = _window_sum(win, g + 1, False)
                d = s[16:80] / _pool_count(row0 + rs, g) - win[16:80]
                yv = jnp.dot(d.astype(BF16), pw_ref[g].astype(BF16), preferred_element_type=F32) + pb_ref[:, ls]
                o_ref[pl.ds(rs, CHUNK), C + g * 128:C + (g + 1) * 128] = (yv * ps_ref[:, ls] * lv).astype(BF16)
            return 0

        for c in range(nch):
            chunk(c, 0)

    vec = pl.BlockSpec((None, 1, C), lambda i: (j, 0, 0))
    return pl.pallas_call(
        body, grid=(T // tm,),
        in_specs=[pl.BlockSpec((tm, 3 * C), lambda i: (i, 0)),
                  pl.BlockSpec((CHUNK, 3 * C), lambda i: (jnp.maximum(i * nch - 1, 0), 0)),
                  pl.BlockSpec((None, CONV_ROWS, C), lambda i: (j, 0, 0)), vec, vec, vec,
                  pl.BlockSpec((None, 4, 128, 128), lambda i: (j, 0, 0, 0)), vec, vec],
        out_specs=[pl.BlockSpec((tm, 2 * C), lambda i: (i, 0)), pl.BlockSpec((tm, C), lambda i: (i, 0))],
        out_shape=[SDS((T, 2 * C), BF16), SDS((T, C), F32)],
        scratch_shapes=[pltpu.VMEM((tm + CHUNK, C), F32), pltpu.VMEM((tm + CHUNK, C), F32), pltpu.VMEM((CHUNK, C), F32)],
        name=name, compiler_params=_params("parallel"))(u, u, cw3, cb3, lg3, lb3, pw4, pb3, ps3)


def _even_bwd(u, yc, dy, cw3, cb3, lg3, lb3, pw4, pb3, ps3, j, name):
    T = u.shape[0]
    C = C_EVEN
    tm = _tile(T, 320)
    nch = tm // CHUNK
    nblk = T // CHUNK
    ntile = T // tm

    def body(u_ref, up_ref, un_ref, yc_ref, ycn_ref, dy_ref, dyn_ref, cw_ref, cb_ref, lg_ref, lb_ref, pw_ref, pb_ref, ps_ref,
             du_ref, dcw_ref, dcb_ref, dlg_ref, dlb_ref, dpw_ref, dpb_ref, dps_ref,
             a_s, p_s, dy_s, dyc_s, dd_s, ddc_s, dw_s):
        i = pl.program_id(0)
        row0 = i * tm

        @pl.when(i == 0)
        def _():
            for ref in (dcb_ref, dlg_ref, dlb_ref, dpw_ref, dpb_ref, dps_ref, dw_s):
                ref[...] = jnp.zeros_like(ref)

        up = up_ref[...]
        lp = _live(CHUNK, row0 - CHUNK, T)
        a_s[0:CHUNK, :] = up[:, 0:C] * _sigmoid(up[:, C:2 * C]) * lp
        p_s[0:CHUNK, :] = up[:, 2 * C:3 * C] * lp
        ln_ = _live(CHUNK, row0 + tm, T)
        p_s[tm + CHUNK:tm + 2 * CHUNK, :] = un_ref[:, 2 * C:3 * C] * ln_
        dy_s[tm:tm + CHUNK, :] = dyn_ref[...] * ln_
        dyc_s[tm + CHUNK:tm + CHUNK + 32, :] = jnp.zeros((32, C), F32)

        def stage(c, _):
            rs = _mult(c * CHUNK, CHUNK)
            lv = _live(CHUNK, row0 + rs, T)
            a_s[pl.ds(rs + CHUNK, CHUNK), :] = u_ref[pl.ds(rs, CHUNK), 0:C] * _sigmoid(u_ref[pl.ds(rs, CHUNK), C:2 * C]) * lv
            p_s[pl.ds(rs + CHUNK, CHUNK), :] = u_ref[pl.ds(rs, CHUNK), 2 * C:3 * C] * lv
            dy_s[pl.ds(rs, CHUNK), :] = dy_ref[pl.ds(rs, CHUNK), :] * lv
            return 0

        for c in range(nch):
            stage(c, 0)

        def first(rs, y, own):
            xc = y - jnp.mean(y, axis=-1, keepdims=True)
            rstd = lax.rsqrt(jnp.mean(xc * xc, axis=-1, keepdims=True) + EPS)
            xh = xc * rstd
            yn = xh * lg_ref[...] + lb_ref[...]
            sg = _sigmoid(yn)
            dyn = dy_s[pl.ds(rs, CHUNK), 0:C] * (sg * (1.0 + yn * (1.0 - sg)))
            dlg_ref[...] += jnp.sum(dyn * xh, axis=0, keepdims=True) * own
            dlb_ref[...] += jnp.sum(dyn, axis=0, keepdims=True) * own
            dxh = dyn * lg_ref[...]
            dyc = rstd * (dxh - jnp.mean(dxh, axis=-1, keepdims=True) - xh * jnp.mean(dxh * xh, axis=-1, keepdims=True))
            dyc_s[pl.ds(rs, CHUNK), :] = dyc
            dcb_ref[...] += jnp.sum(dyc, axis=0, keepdims=True) * own
            for g in range(4):
                ls = slice(g * 128, (g + 1) * 128)
                win = p_s[pl.ds(rs + 48, 80), ls]
                s = _window_sum(win, g + 1, False)
                cnt = _pool_count(row0 + rs, g)
                d = (s[16:80] / cnt - win[16:80]).astype(BF16)
                w = pw_ref[g].astype(BF16)
                pre = jnp.dot(d, w, preferred_element_type=F32) + pb_ref[:, ls]
                dyb = dy_s[pl.ds(rs, CHUNK), C + g * 128:C + (g + 1) * 128]
                dpre = dyb * ps_ref[:, ls]
                dps_ref[:, ls] += jnp.sum(dyb * pre, axis=0, keepdims=True) * own
                dpb_ref[:, ls] += jnp.sum(dpre, axis=0, keepdims=True) * own
                dpre_b = (dpre * own).astype(BF16)
                dpw_ref[g] += lax.dot_general(d, dpre_b, (((0,), (0,)), ((), ())), preferred_element_type=F32)
                dd = lax.dot_general(dpre.astype(BF16), w, (((1,), (1,)), ((), ())), preferred_element_type=F32)
                dd_s[pl.ds(rs, CHUNK), ls] = dd
                ddc_s[pl.ds(rs, CHUNK), ls] = dd / cnt

        def first_in_tile(c, _):
            rs = _mult(c * CHUNK, CHUNK)
            first(rs, yc_ref[pl.ds(rs, CHUNK), :], 1.0)
            return 0

        for c in range(nch):
            first_in_tile(c, 0)
        first(tm, ycn_ref[...], 0.0)
        ddc_s[tm + CHUNK:tm + CHUNK + 16, :] = jnp.zeros((16, C), F32)

        def second(c, _):
            rs = _mult(c * CHUNK, CHUNK)
            lv = _live(CHUNK, row0 + rs, T)
            for cb in range(4):
                ls = slice(cb * 128, (cb + 1) * 128)
                wd = dyc_s[pl.ds(rs, 96), ls]
                da = _conv_taps(wd, cw_ref, ls, jnp.zeros((CHUNK, 128), F32), True)
                wa = a_s[pl.ds(_mult(rs + 32, 32), 96), ls]
                dyc = dyc_s[pl.ds(rs, CHUNK), ls]
                for b in range(8):
                    rb = wa if b == 0 else pltpu.roll(wa, 96 - b, 0)
                    for a in range(5):
                        tap = 8 * a + b - 2
                        if 0 <= tap < CONV_WIDTH:
                            prod = dyc * rb[8 * a:8 * a + CHUNK]
                            part = prod[0:8]
                            for q in range(1, 8):
                                part = part + prod[8 * q:8 * q + 8]
                            dw_s[8 * tap:8 * tap + 8, ls] += part
                val = u_ref[pl.ds(rs, CHUNK), ls]
                sg = _sigmoid(u_ref[pl.ds(rs, CHUNK), C + cb * 128:C + (cb + 1) * 128])
                du_ref[pl.ds(rs, CHUNK), ls] = (da * sg * lv).astype(BF16)
                du_ref[pl.ds(rs, CHUNK), C + cb * 128:C + (cb + 1) * 128] = (da * val * sg * (1.0 - sg) * lv).astype(BF16)
            for g in range(4):
                ls = slice(g * 128, (g + 1) * 128)
                z = _window_sum(ddc_s[pl.ds(rs, 80), ls], g + 1, True)
                dpin = (z[0:CHUNK] - dd_s[pl.ds(rs, CHUNK), ls]) * lv
                du_ref[pl.ds(rs, CHUNK), 2 * C + g * 128:2 * C + (g + 1) * 128] = dpin.astype(BF16)
            return 0

        for c in range(nch):
            second(c, 0)

        @pl.when(i == ntile - 1)
        def _():
            for tap in range(CONV_WIDTH):
                dcw_ref[tap:tap + 1, :] = jnp.sum(dw_s[8 * tap:8 * tap + 8, :], axis=0, keepdims=True)
            dcw_ref[CONV_WIDTH:CONV_ROWS, :] = jnp.zeros((CONV_ROWS - CONV_WIDTH, C), F32)

    vec = pl.BlockSpec((None, 1, C), lambda i: (j, 0, 0))
    ovec = pl.BlockSpec((1, C), lambda i: (0, 0))
    return pl.pallas_call(
        body, grid=(ntile,),
        in_specs=[pl.BlockSpec((tm, 3 * C), lambda i: (i, 0)),
                  pl.BlockSpec((CHUNK, 3 * C), lambda i: (jnp.maximum(i * nch - 1, 0), 0)),
                  pl.BlockSpec((CHUNK, 3 * C), lambda i: (jnp.minimum((i + 1) * nch, nblk - 1), 0)),
                  pl.BlockSpec((tm, C), lambda i: (i, 0)),
                  pl.BlockSpec((CHUNK, C), lambda i: (jnp.minimum((i + 1) * nch, nblk - 1), 0)),
                  pl.BlockSpec((tm, 2 * C), lambda i: (i, 0)),
                  pl.BlockSpec((CHUNK, 2 * C), lambda i: (jnp.minimum((i + 1) * nch, nblk - 1), 0)),
                  pl.BlockSpec((None, CONV_ROWS, C), lambda i: (j, 0, 0)), vec, vec, vec,
                  pl.BlockSpec((None, 4, 128, 128), lambda i: (j, 0, 0, 0)), vec, vec],
        out_specs=[pl.BlockSpec((tm, 3 * C), lambda i: (i, 0)), pl.BlockSpec((CONV_ROWS, C), lambda i: (0, 0)),
                   ovec, ovec, ovec, pl.BlockSpec((4, 128, 128), lambda i: (0, 0, 0)), ovec, ovec],
        out_shape=[SDS((T, 3 * C), BF16), SDS((CONV_ROWS, C), F32), SDS((1, C), F32), SDS((1, C), F32), SDS((1, C), F32),
                   SDS((4, 128, 128), F32), SDS((1, C), F32), SDS((1, C), F32)],
        scratch_shapes=[pltpu.VMEM((tm + CHUNK, C), F32), pltpu.VMEM((tm + 2 * CHUNK, C), F32),
                        pltpu.VMEM((tm + CHUNK, 2 * C), F32),
                        pltpu.VMEM((tm + CHUNK + 32, C), F32), pltpu.VMEM((tm + CHUNK, C), F32),
                        pltpu.VMEM((tm + CHUNK + 16, C), F32), pltpu.VMEM((8 * CONV_ROWS, C), F32)],
        name=name, compiler_params=_params("arbitrary"))(u, u, u, yc, yc, dy, dy, cw3, cb3, lg3, lb3, pw4, pb3, ps3)


HI = lax.Precision.HIGHEST


def _dot_nt(a, b):
    return lax.dot_general(a, b, (((1,), (1,)), ((), ())), preferred_element_type=F32)


def _dot_tn(a, b):
    return lax.dot_general(a, b, (((0,), (0,)), ((), ())), preferred_element_type=F32)


def _tri(lower):
    r = lax.broadcasted_iota(jnp.int32, (CHUNK, CHUNK), 0)
    c = lax.broadcasted_iota(jnp.int32, (CHUNK, CHUNK), 1)
    return jnp.where((c <= r) if lower else (c >= r), 1.0, 0.0).astype(F32)


def _hgrn_gates(u_ref, lb_ref, h, D, lv):
    ls = slice(h * HEAD_DIM, (h + 1) * HEAD_DIM)
    qraw = u_ref[:, ls]
    fraw = u_ref[:, D + h * HEAD_DIM:D + (h + 1) * HEAD_DIM]
    v = u_ref[:, 2 * D + h * HEAD_DIM:2 * D + (h + 1) * HEAD_DIM] * lv
    lbv = lb_ref[:, ls]
    sig = _sigmoid(fraw)
    forget = lbv + (1.0 - lbv) * sig
    logf = jnp.log(forget) * lv
    k = (1.0 - forget) * lv
    qsig = _sigmoid(qraw)
    q = qraw * qsig * lv
    return q, k, v, logf, (qraw, qsig, sig, forget, lbv)


def _sub_parts(q, k, b, b_s, I):
    rows = slice(SUB * I, SUB * (I + 1))
    rho = jnp.zeros((1, HEAD_DIM), F32) if I == 0 else b_s[SUB * I - 1:SUB * I, :]
    eI = jnp.exp(b[rows] - rho)
    EI = jnp.exp(jnp.minimum(rho - b, EXP_CAP))
    causal = (lax.broadcasted_iota(jnp.int32, (SUB, CHUNK), 1)
              <= lax.broadcasted_iota(jnp.int32, (SUB, CHUNK), 0) + SUB * I)
    return rows, q[rows] * eI, k * EI, eI, EI, causal


def _chunks_per_step(NC):
    for n in (5, 4, 3, 2):
        if NC % n == 0:
            return n
    return 1


def _hgrn_fwd(u, lb3, layer, gn3, j, name):
    T = u.shape[0]
    D = u.shape[1] // 4
    H = D // HEAD_DIM
    NC = T // CHUNK
    CH = _chunks_per_step(NC)
    R = CH * CHUNK

    def body(u_ref, lb_ref, gn_ref, y_ref, o_ref, sall_ref, st_s, b_s, lf_s, q_s, k_s):
        n = pl.program_id(0)

        @pl.when(n == 0)
        def _():
            st_s[...] = jnp.zeros_like(st_s)

        heads = range(H)
        cols = [slice(h * HEAD_DIM, (h + 1) * HEAD_DIM) for h in heads]
        rows = [slice(c * CHUNK, (c + 1) * CHUNK) for c in range(CH)]
        vb = {}
        for c in range(CH):
            lv = _live(CHUNK, (n * CH + c) * CHUNK, T)
            for h in heads:
                q, k, v, logf, _ = _hgrn_gates(u_ref.at[rows[c]], lb_ref, h, D, lv)
                q_s[rows[c], cols[h]] = q
                k_s[rows[c], cols[h]] = k
                lf_s[rows[c], cols[h]] = logf
                vb[c, h] = v.astype(BF16)
        for c in range(CH):
            b_s[rows[c], :] = jnp.dot(_tri(True), lf_s[rows[c], :], precision=HI, preferred_element_type=F32)
        ops = {}
        for c in range(CH):
            for h in heads:
                b_h = b_s.at[rows[c], cols[h]]
                b = b_h[...]
                q = q_s[rows[c], cols[h]]
                k = k_s[rows[c], cols[h]]
                blast = b_h[CHUNK - 1:CHUNK, :]
                qh = (q * jnp.exp(b)).astype(BF16)
                kt = (k * jnp.exp(blast - b)).astype(BF16)
                subs = []
                for I in range(CHUNK // SUB):
                    _, qI, KI, _, _, causal = _sub_parts(q, k, b, b_h, I)
                    subs.append((qI.astype(BF16), KI.astype(BF16), causal))
                ops[c, h] = (qh, kt, jnp.exp(blast), subs)
        mm = {}
        for h in heads:
            st = st_s[h]
            for c in range(CH):
                qh, kt, eblast, subs = ops[c, h]
                sall_ref[c, h] = st
                o_inter = _dot_nt(qh, st.astype(BF16))
                st = st * eblast + _dot_tn(vb[c, h], kt)
                mm[c, h] = (o_inter, [_dot_nt(qI, KI) for qI, KI, _ in subs])
            st_s[h] = st
        for c in range(CH):
            for h in heads:
                o_inter, ps = mm[c, h]
                p = jnp.concatenate([jnp.where(m, x, 0.0) for x, (_, _, m) in zip(ps, ops[c, h][3])], axis=0).astype(BF16)
                o = o_inter + jnp.dot(p, vb[c, h], preferred_element_type=F32)
                o_ref[rows[c], cols[h]] = o
                graw = u_ref[rows[c], 3 * D + h * HEAD_DIM:3 * D + (h + 1) * HEAD_DIM]
                r = lax.rsqrt(jnp.mean(o * o, axis=-1, keepdims=True) + EPS)
                y_ref[rows[c], cols[h]] = (((o * r) * gn_ref[...]) * (graw * _sigmoid(graw))).astype(BF16)

    return pl.pallas_call(
        body, grid=(NC // CH,),
        in_specs=[pl.BlockSpec((R, 4 * D), lambda n: (n, 0)),
                  pl.BlockSpec((None, 1, D), lambda n: (layer, 0, 0)),
                  pl.BlockSpec((None, 1, HEAD_DIM), lambda n: (j, 0, 0))],
        out_specs=[pl.BlockSpec((R, D), lambda n: (n, 0)), pl.BlockSpec((R, D), lambda n: (n, 0)),
                   pl.BlockSpec((CH, H, HEAD_DIM, HEAD_DIM), lambda n: (n, 0, 0, 0))],
        out_shape=[SDS((T, D), BF16), SDS((T, D), F32), SDS((NC, H, HEAD_DIM, HEAD_DIM), F32)],
        scratch_shapes=[pltpu.VMEM((H, HEAD_DIM, HEAD_DIM), F32)] + [pltpu.VMEM((R, D), F32)] * 4,
        name=name, compiler_params=_params("arbitrary"))(u, lb3, gn3)


def _hgrn_bwd(u, o_raw, dy, sall, lb3, layer, gn3, j, name):
    T = u.shape[0]
    D = u.shape[1] // 4
    H = D // HEAD_DIM
    NC = T // CHUNK
    CH = _chunks_per_step(NC)
    R = CH * CHUNK
    NS = NC // CH

    def body(u_ref, o_ref, dy_ref, sall_ref, lb_ref, gn_ref, du_ref, dlb_ref, dgn_ref, dst_s, b_s, lf_s, q_s, k_s, db_s, dk_s):
        step = pl.program_id(0)
        n = NS - 1 - step

        @pl.when(step == 0)
        def _():
            dst_s[...] = jnp.zeros_like(dst_s)
            dlb_ref[...] = jnp.zeros_like(dlb_ref)
            dgn_ref[...] = jnp.zeros_like(dgn_ref)

        last_row = (_row_ids((CHUNK, 1), 0) == CHUNK - 1).astype(F32)
        gn = gn_ref[...]
        heads = range(H)
        chunks = range(CH)
        cols = [slice(h * HEAD_DIM, (h + 1) * HEAD_DIM) for h in heads]
        rows = [slice(c * CHUNK, (c + 1) * CHUNK) for c in chunks]
        lv = [_live(CHUNK, (n * CH + c) * CHUNK, T) for c in chunks]
        vb, dob = {}, {}
        dgn = jnp.zeros((1, HEAD_DIM), F32)
        for c in chunks:
            for h in heads:
                q, k, v, logf, _ = _hgrn_gates(u_ref.at[rows[c]], lb_ref, h, D, lv[c])
                q_s[rows[c], cols[h]] = q
                k_s[rows[c], cols[h]] = k
                lf_s[rows[c], cols[h]] = logf
                vb[c, h] = v.astype(BF16)
                graw = u_ref[rows[c], 3 * D + h * HEAD_DIM:3 * D + (h + 1) * HEAD_DIM]
                gsig = _sigmoid(graw)
                o = o_ref[rows[c], cols[h]]
                r = lax.rsqrt(jnp.mean(o * o, axis=-1, keepdims=True) + EPS)
                xh = o * r
                dyv = dy_ref[rows[c], cols[h]]
                dsg = dyv * (graw * gsig)
                dgn = dgn + jnp.sum(dsg * xh, axis=0, keepdims=True)
                dxh = dsg * gn
                do = r * (dxh - xh * jnp.mean(dxh * xh, axis=-1, keepdims=True))
                dob[c, h] = do.astype(BF16)
                dgraw = dyv * xh * gn * (gsig * (1.0 + graw * (1.0 - gsig)))
                du_ref[rows[c], 3 * D + h * HEAD_DIM:3 * D + (h + 1) * HEAD_DIM] = (dgraw * lv[c]).astype(BF16)
        dgn_ref[...] += dgn
        for c in chunks:
            b_s[rows[c], :] = jnp.dot(_tri(True), lf_s[rows[c], :], precision=HI, preferred_element_type=F32)
        ops = {}
        for c in chunks:
            for h in heads:
                b_h = b_s.at[rows[c], cols[h]]
                b = b_h[...]
                q = q_s[rows[c], cols[h]]
                k = k_s[rows[c], cols[h]]
                blast = b_h[CHUNK - 1:CHUNK, :]
                eb = jnp.exp(b)
                ekb = jnp.exp(blast - b)
                subs = []
                for I in range(CHUNK // SUB):
                    rws, qI, KI, eI, EI, causal = _sub_parts(q, k, b, b_h, I)
                    subs.append((rws, qI.astype(BF16), KI.astype(BF16), eI, EI, causal))
                ops[c, h] = (eb, ekb, jnp.exp(blast), (q * eb).astype(BF16), (k * ekb).astype(BF16), subs)
        mm = {}
        for h in heads:
            dst = dst_s[h]
            for c in reversed(chunks):
                eb, ekb, eblast, qhb, ktb, subs = ops[c, h]
                st = sall_ref[c, h]
                dstb = dst.astype(BF16)
                dv = _dot_nt(ktb, dstb)
                dqh = jnp.dot(dob[c, h], st.astype(BF16), preferred_element_type=F32)
                dkt = jnp.dot(vb[c, h], dstb, preferred_element_type=F32)
                dblast = jnp.sum(dst * st, axis=0, keepdims=True) * eblast
                dst = dst * eblast + _dot_tn(dob[c, h], qhb)
                dp_full = _dot_nt(dob[c, h], vb[c, h])
                ps = [_dot_nt(qIb, KIb) for _, qIb, KIb, _, _, _ in subs]
                mm[c, h] = (dv, dqh, dkt, dblast, dp_full, ps)
            dst_s[h] = dst
        for c in chunks:
            for h in heads:
                eb, ekb, eblast, qhb, ktb, subs = ops[c, h]
                dv, dqh, dkt, dblast, dp_full, ps = mm[c, h]
                p = jnp.concatenate([jnp.where(sub[5], x, 0.0) for x, sub in zip(ps, subs)], axis=0).astype(BF16)
                dv = dv + _dot_tn(p, dob[c, h])
                du_ref[rows[c], 2 * D + h * HEAD_DIM:2 * D + (h + 1) * HEAD_DIM] = (dv * lv[c]).astype(BF16)
                dq = dqh * eb
                db = dqh * qhb.astype(F32)
                tmp = dkt * ktb.astype(F32)
                dk = dkt * ekb
                db = db - tmp
                dblast = dblast + jnp.sum(tmp, axis=0, keepdims=True)
                dq_parts, db_parts = [], []
                for rws, qIb, KIb, eI, EI, causal in subs:
                    dp = jnp.where(causal, dp_full[rws], 0.0).astype(BF16)
                    dqI = jnp.dot(dp, KIb, preferred_element_type=F32)
                    dKI = _dot_tn(dp, qIb)
                    dq_parts.append(dqI * eI)
                    db_parts.append(dqI * qIb.astype(F32))
                    dk = dk + dKI * EI
                    db = db - dKI * KIb.astype(F32)
                dq = dq + jnp.concatenate(dq_parts, axis=0)
                db_s[rows[c], cols[h]] = db + jnp.concatenate(db_parts, axis=0) + last_row * dblast
                dk_s[rows[c], cols[h]] = dk
                qraw = u_ref[rows[c], cols[h]]
                qsig = _sigmoid(qraw)
                du_ref[rows[c], cols[h]] = (dq * (qsig * (1.0 + qraw * (1.0 - qsig))) * lv[c]).astype(BF16)
        for c in chunks:
            lf_s[rows[c], :] = jnp.dot(_tri(False), db_s[rows[c], :], precision=HI, preferred_element_type=F32)
        for h in heads:
            lbv = lb_ref[:, cols[h]]
            dlb = jnp.zeros((1, HEAD_DIM), F32)
            for c in chunks:
                fraw = u_ref[rows[c], D + h * HEAD_DIM:D + (h + 1) * HEAD_DIM]
                sig = _sigmoid(fraw)
                forget = lbv + (1.0 - lbv) * sig
                dforget = (lf_s[rows[c], cols[h]] / forget - dk_s[rows[c], cols[h]]) * lv[c]
                dlb = dlb + jnp.sum(dforget * (1.0 - sig), axis=0, keepdims=True)
                du_ref[rows[c], D + h * HEAD_DIM:D + (h + 1) * HEAD_DIM] = (dforget * (1.0 - lbv) * sig * (1.0 - sig)).astype(BF16)
            dlb_ref[:, cols[h]] += dlb

    rev = lambda s: (NS - 1 - s, 0)
    return pl.pallas_call(
        body, grid=(NS,),
        in_specs=[pl.BlockSpec((R, 4 * D), rev), pl.BlockSpec((R, D), rev), pl.BlockSpec((R, D), rev),
                  pl.BlockSpec((CH, H, HEAD_DIM, HEAD_DIM), lambda s: (NS - 1 - s, 0, 0, 0)),
                  pl.BlockSpec((None, 1, D), lambda s: (layer, 0, 0)),
                  pl.BlockSpec((None, 1, HEAD_DIM), lambda s: (j, 0, 0))],
        out_specs=[pl.BlockSpec((R, 4 * D), rev), pl.BlockSpec((1, D), lambda s: (0, 0)),
                   pl.BlockSpec((1, HEAD_DIM), lambda s: (0, 0))],
        out_shape=[SDS((T, 4 * D), BF16), SDS((1, D), F32), SDS((1, HEAD_DIM), F32)],
        scratch_shapes=[pltpu.VMEM((H, HEAD_DIM, HEAD_DIM), F32)] + [pltpu.VMEM((R, D), F32)] * 6,
        name=name, compiler_params=_params("arbitrary"))(u, o_raw, dy, sall, lb3, gn3)


def _softmax_layers(p_ref, n_layers):
    rows = [p_ref[l:l + 1, :] for l in range(n_layers)]
    m = functools.reduce(jnp.maximum, rows)
    e = [jnp.exp(x - m) for x in rows]
    tot = functools.reduce(lambda a, b: a + b, e)
    return [x / tot for x in e]


def _lb_fwd(p):
    n_layers, D = p.shape

    def body(p_ref, o_ref):
        s = _softmax_layers(p_ref, n_layers)
        acc = jnp.zeros((1, D), F32)
        o_ref[0:1, :] = acc
        for l in range(1, n_layers):
            acc = acc + s[l]
            o_ref[l:l + 1, :] = acc

    return pl.pallas_call(body, out_shape=SDS(p.shape, F32), name="lb_fwd")(p)


def _lb_bwd(p, dlb):
    n_layers, D = p.shape

    def body(p_ref, d_ref, o_ref):
        s = _softmax_layers(p_ref, n_layers)
        ds = [jnp.zeros((1, D), F32)] * n_layers
        acc = jnp.zeros((1, D), F32)
        for l in range(n_layers - 1, 0, -1):
            acc = acc + d_ref[l:l + 1, :]
            ds[l] = acc
        dot = functools.reduce(lambda a, b: a + b, [s[l] * ds[l] for l in range(n_layers)])
        for l in range(n_layers):
            o_ref[l:l + 1, :] = s[l] * (ds[l] - dot)

    return pl.pallas_call(body, out_shape=SDS(p.shape, F32), name="lb_bwd")(p, dlb)


def _adamw_small(items):
    n = len(items)

    def body(*refs):
        for k in range(n):
            w_ref, g_ref, m_ref, v_ref = refs[4 * k:4 * k + 4]
            d_ref, mo_ref, vo_ref = refs[4 * n + 3 * k:4 * n + 3 * k + 3]
            g_ = g_ref[...]
            m_ = ADAM_B1 * m_ref[...] + (1.0 - ADAM_B1) * g_
            v_ = ADAM_B2 * v_ref[...] + (1.0 - ADAM_B2) * (g_ * g_)
            mh = m_ / (1.0 - ADAM_B1 ** ADAM_STEP)
            vh = v_ / (1.0 - ADAM_B2 ** ADAM_STEP)
            d_ref[...] = -ADAM_LR * (mh / (jnp.sqrt(vh) + ADAM_EPS) + ADAM_WD * w_ref[...])
            mo_ref[...] = m_
            vo_ref[...] = v_

    out_shape = [SDS(it[0].shape, F32) for it in items for _ in range(3)]
    res = pl.pallas_call(body, out_shape=out_shape, name="adamw_small")(*[a for it in items for a in it])
    return [res[3 * k:3 * k + 3] for k in range(n)]


def _adamw_layer(w3, m3, v3, g2, layer, outs, name):
    L, R, C = w3.shape
    tr = _tile(R, 256, 8)
    if outs is None:
        outs = tuple(lax.empty(w3.shape, F32) for _ in range(4))

    def body(w_ref, m_ref, v_ref, g_ref, a0, a1, a2, a3, go_ref, d_ref, mo_ref, vo_ref):
        del a0, a1, a2, a3
        g_ = g_ref[...]
        m_ = ADAM_B1 * m_ref[...] + (1.0 - ADAM_B1) * g_
        v_ = ADAM_B2 * v_ref[...] + (1.0 - ADAM_B2) * (g_ * g_)
        mh = m_ / (1.0 - ADAM_B1 ** ADAM_STEP)
        vh = v_ / (1.0 - ADAM_B2 ** ADAM_STEP)
        go_ref[...] = g_
        d_ref[...] = -ADAM_LR * (mh / (jnp.sqrt(vh) + ADAM_EPS) + ADAM_WD * w_ref[...])
        mo_ref[...] = m_
        vo_ref[...] = v_

    lay = pl.BlockSpec((None, tr, C), lambda i: (layer, i, 0))
    return pl.pallas_call(
        body, grid=(R // tr,), in_specs=[lay] * 3 + [pl.BlockSpec((tr, C), lambda i: (i, 0))] + [ANY_SPEC] * 4,
        out_specs=[lay] * 4, out_shape=[SDS(w3.shape, F32)] * 4, input_output_aliases={4: 0, 5: 1, 6: 2, 7: 3},
        name=name, compiler_params=_params("parallel"))(w3, m3, v3, g2, *outs)


SEM_SPEC = pl.BlockSpec(memory_space=pltpu.SEMAPHORE)
HBM_SPEC = pl.BlockSpec(memory_space=pltpu.HBM)
EFFECT = pltpu.SideEffectType.DATAFLOW_SIDE_EFFECTING
N_DEV = 2 * N_CHIPS


def _position():
    x, y, c = lax.axis_index("x"), lax.axis_index("y"), lax.axis_index("c")
    chips = [(1 - x, y), (x, 1 - y), (1 - x, 1 - y)]
    return x, y, c, chips


def _split_start(name, plan, bufs, n_sems, deps=(), earlier=None):
    n = len(bufs)
    held = () if earlier is None else tuple(earlier[1:])

    def body(*refs):
        first_out = n + len(held) + len(deps)
        if earlier is not None:
            sends, recvs = earlier[0](refs[:n], refs[n], refs[n + 1])
            for kw in sends:
                pltpu.make_async_remote_copy(**kw).wait_send()
            for kw in recvs:
                pltpu.make_async_remote_copy(**kw).wait_recv()
        sends, _ = plan(refs[:n], refs[first_out], refs[first_out + 1])
        for kw in sends:
            pltpu.make_async_remote_copy(**kw).start()
        refs[-1][...] = jnp.zeros_like(refs[-1])

    out = pl.pallas_call(
        body, name=name,
        out_shape=(pltpu.SemaphoreType.DMA((n_sems,)), pltpu.SemaphoreType.DMA((n_sems,)),
                   *[pltpu.HBM(b.shape, b.dtype) for b in bufs], SDS((8, 128), F32)),
        in_specs=[HBM_SPEC] * n + [SEM_SPEC] * len(held) + [ANY_SPEC] * len(deps),
        out_specs=(SEM_SPEC, SEM_SPEC, *[HBM_SPEC] * n, pl.BlockSpec(memory_space=pltpu.VMEM)),
        input_output_aliases={i: 2 + i for i in range(n)},
        compiler_params=pltpu.CompilerParams(has_side_effects=EFFECT),
    )(*[pltpu.with_memory_space_constraint(b, pltpu.HBM) for b in bufs], *held, *deps)
    return out[0], out[1], list(out[2:2 + n]), out[-1]


def _split_wait(name, plan, send_sems, recv_sems, bufs, after=()):
    n = len(bufs)

    def body(*refs):
        sends, recvs = plan(refs[:n], refs[n], refs[n + 1])
        for kw in sends:
            pltpu.make_async_remote_copy(**kw).wait_send()
        for kw in recvs:
            pltpu.make_async_remote_copy(**kw).wait_recv()

    out = pl.pallas_call(
        body, name=name, out_shape=tuple(pltpu.HBM(b.shape, b.dtype) for b in bufs),
        in_specs=[HBM_SPEC] * n + [SEM_SPEC, SEM_SPEC] + [ANY_SPEC] * len(after),
        out_specs=tuple([HBM_SPEC] * n), input_output_aliases={i: i for i in range(n)},
        compiler_params=pltpu.CompilerParams(has_side_effects=EFFECT),
    )(*bufs, send_sems, recv_sems, *after)
    return list(out)


def _region(kind, ref, chip, half):
    K, N = ref.shape
    if kind == "col":
        return ref.at[pl.ds(half * (K // 2), K // 2), pl.ds(chip * (N // N_CHIPS), N // N_CHIPS)]
    rows = K // (2 * N_CHIPS)
    return ref.at[pl.ds((2 * chip + half) * rows, rows), :]


def _gather_plan(kinds, over_chips, first=0):
    def plan(refs, send_sems, recv_sems):
        x, y, c, chips = _position()
        sends, recvs = [], []
        for f, (ref, kind) in enumerate(zip(refs, kinds)):
            for k, chip in enumerate(chips):
                theirs = 2 * chip[0] + chip[1]
                at = 3 * (first + f) + k
                sem = dict(send_sem=send_sems.at[at], recv_sem=recv_sems.at[at], device_id_type=MESH)
                if over_chips:
                    out, back, to = _region(kind, ref, 2 * x + y, c), _region(kind, ref, theirs, c), (*chip, c)
                else:
                    out, back, to = _region(kind, ref, theirs, c), _region(kind, ref, theirs, 1 - c), (x, y, 1 - c)
                sends.append(dict(src_ref=out, dst_ref=out, device_id=to, **sem))
                recvs.append(dict(src_ref=back, dst_ref=back, device_id=to, **sem))
        return sends, recvs
    return plan


def _reduce_plan(first=0):
    def plan(refs, send_sems, recv_sems):
        x, y, c, _ = _position()
        me = 4 * x + 2 * y + c
        sends, recvs = [], []
        for f in range(len(refs) // 2):
            acc, land = refs[2 * f], refs[2 * f + 1]
            for d in range(1, N_DEV):
                t = (me + d) % N_DEV
                to = dict(device_id=(t // 4, (t // 2) % 2, t % 2), device_id_type=MESH)
                slot = N_DEV - 1 - d
                at = first + 7 * f
                sends.append(dict(src_ref=acc.at[t % 2, t // 2], dst_ref=land.at[slot], send_sem=send_sems.at[at + d - 1],
                                  recv_sem=recv_sems.at[at + slot], **to))
                recvs.append(dict(src_ref=land.at[d - 1], dst_ref=land.at[d - 1], send_sem=send_sems.at[at + d - 1],
                                  recv_sem=recv_sems.at[at + d - 1], **to))
        return sends, recvs
    return plan


def _swap_plan(first=0):
    def plan(refs, send_sems, recv_sems):
        x, y, c, _ = _position()
        sends, recvs = [], []
        for f, g in enumerate(refs):
            sem = dict(send_sem=send_sems.at[first + f], recv_sem=recv_sems.at[first + f], device_id=(x, y, 1 - c),
                       device_id_type=MESH)
            sends.append(dict(src_ref=g.at[c], dst_ref=g.at[c], **sem))
            recvs.append(dict(src_ref=g.at[1 - c], dst_ref=g.at[1 - c], **sem))
        return sends, recvs
    return plan


def _joined(plans):
    def plan(refs, send_sems, recv_sems):
        sends, recvs, lo = [], [], 0
        for part, n in plans:
            s_, r_ = part(refs[lo:lo + n], send_sems, recv_sems)
            sends += s_
            recvs += r_
            lo += n
        return sends, recvs
    return plan


def _sum_pieces(ids2, acc, land, name):
    _, _, nr, nc = acc.shape
    tr = _tile(nr, 256, 16)

    def body(ids_ref, own_ref, land_ref, o_ref):
        del ids_ref
        s = own_ref[...].astype(F32)
        for k in range(N_DEV - 1):
            s = s + land_ref[k].astype(F32)
        o_ref[...] = s

    return pl.pallas_call(
        body,
        grid_spec=pltpu.PrefetchScalarGridSpec(
            num_scalar_prefetch=1, grid=(nr // tr,),
            in_specs=[pl.BlockSpec((None, None, tr, nc), lambda i, ids: (ids[0], ids[1], i, 0)),
                      pl.BlockSpec((N_DEV - 1, tr, nc), lambda i, ids: (0, i, 0))],
            out_specs=pl.BlockSpec((None, tr, nc), lambda i, ids: (ids[0], i, 0))),
        out_shape=SDS((2, nr, nc), F32), name=name, compiler_params=_params("parallel"))(ids2, acc, land)


def _small_plan(refs, send_sems, recv_sems):
    x, y, c, _ = _position()
    me = 4 * x + 2 * y + c
    own, land = refs
    sends, recvs = [], []
    for d in range(1, N_DEV):
        t = (me + d) % N_DEV
        to = dict(device_id=(t // 4, (t // 2) % 2, t % 2), device_id_type=MESH)
        sends.append(dict(src_ref=own, dst_ref=land.at[me], send_sem=send_sems.at[d - 1],
                          recv_sem=recv_sems.at[N_DEV - 1 - d], **to))
        recvs.append(dict(src_ref=land.at[t], dst_ref=land.at[t], send_sem=send_sems.at[d - 1],
                          recv_sem=recv_sems.at[d - 1], **to))
    return sends, recvs


def _sum_blocks(me1, own, land):
    def body(me_ref, own_ref, land_ref, o_ref):
        acc = None
        for d in range(N_DEV):
            term = jnp.where(me_ref[0] == d, own_ref[...], land_ref[d])
            acc = term if acc is None else acc + term
        o_ref[...] = acc

    return pl.pallas_call(
        body,
        grid_spec=pltpu.PrefetchScalarGridSpec(
            num_scalar_prefetch=1, grid=(1,),
            in_specs=[pl.BlockSpec(own.shape, lambda i, me: (0, 0)), pl.BlockSpec(land.shape, lambda i, me: (0, 0, 0))],
            out_specs=pl.BlockSpec(own.shape, lambda i, me: (0, 0))),
        out_shape=SDS(own.shape, F32), name="sum_small", compiler_params=_params("arbitrary"))(me1, own, land)


BIG = {"ev_w_in": "col", "ev_w_out": "row", "od_w_in": "col", "od_w_out": "row", "mlp_w1": "col", "mlp_w2": "row"}
WEIGHTS = ("meta_tokens", "mix_norm_g", "mlp_norm_g", "final_norm_g", "ev_w_in", "ev_conv_w", "ev_conv_b", "ev_ln_g",
           "ev_ln_b", "ev_pool_w", "ev_pool_b", "ev_pool_scale", "ev_w_out", "od_w_in", "od_gnorm_g", "od_w_out",
           "lb_param", "mlp_w1", "mlp_w2")
PACK_UNIT = 1024


def _mixer_names(layer):
    return ("ev_w_in", "ev_w_out") if layer % 2 == 0 else ("od_w_in", "od_w_out")


def _pack(arrays):
    flat = []
    for a in arrays:
        a = a.reshape(-1)
        flat.append(jnp.pad(a, (0, (-a.shape[0]) % PACK_UNIT)))
    return jnp.concatenate(flat).reshape(-1, 128)


def _unpack(packed, shapes):
    flat = packed.reshape(-1)
    out, off = [], 0
    for s in shapes:
        size = 1
        for d in s:
            size *= d
        out.append(flat[off:off + size].reshape(s))
        off += size + (-size) % PACK_UNIT
    return out


def _local_step(x2, target, P, weights, boundary, first_deps=()):
    D = x2.shape[1]
    n_layers = P["mix_norm_g"].shape[0]
    h = jnp.concatenate([jnp.zeros((PAD, D), F32), P["meta_full"], x2], axis=0)
    mix_g = P["mix_norm_g"].reshape(n_layers, 1, D)
    mlp_g = P["mlp_norm_g"].reshape(n_layers, 1, D)
    vec = lambda a: a.reshape(a.shape[0], 1, -1)
    cb3, lg3, lnb3, ps3 = vec(P["ev_conv_b"]), vec(P["ev_ln_g"]), vec(P["ev_ln_b"]), vec(P["ev_pool_scale"])
    pb3 = vec(P["ev_pool_b"])
    gn3 = vec(P["od_gnorm_g"])
    lb_all = _lb_fwd(P["lb_param"])
    lb3 = lb_all.reshape(n_layers, 1, D)
    even = (cb3, lg3, lnb3, P["ev_pool_w"], pb3, ps3)

    saved = []
    deps = tuple(first_deps)
    for layer in range(n_layers):
        j = layer // 2
        w_in, w_out = _mixer_names(layer)
        W = {}
        s = {"h": h, "W": W}
        s["n"] = _rms_fwd(h, mix_g, layer, "mix_norm_0", deps=deps) if layer == 0 else n_next
        deps = ()
        W[w_in], held = weights(layer, w_in, (s["n"],))
        s["u"] = _mm_nn(s["n"], W[w_in], 0, f"mix_in_{layer}", deps=held)
        if layer % 2 == 0:
            s["y"], s["yc"] = _even_fwd(s["u"], P["conv_w_full"], *even, j, f"even_fwd_{layer}")
        else:
            s["y"], s["o"], s["sall"] = _hgrn_fwd(s["u"], lb3, layer, gn3, j, f"hgrn_fwd_{layer}")
        W[w_out], held = weights(layer, w_out, (s["y"],))
        if layer == 0:
            h, s["n2"] = _mm_nn_norm(s["y"], W[w_out], 0, h, mlp_g, layer, "mix_out_0", deps=held)
            s["h1"] = h
            W["mlp_w1"], held = weights(layer, "mlp_w1", (s["n2"],))
            s["relu"] = _mm_nn(s["n2"], W["mlp_w1"], 0, "mlp_up_0", relu=True, deps=held)
            W["mlp_w2"], held = weights(layer, "mlp_w2", (s["relu"],))
            h, n_next = _mm_nn_norm(s["relu"], W["mlp_w2"], 0, h, mix_g, 1, "mlp_down_0", square=True, deps=held)
        else:
            W["mlp_w1"], more1 = weights(layer, "mlp_w1", (s["y"],))
            W["mlp_w2"], more2 = weights(layer, "mlp_w2", (s["y"],))
            last = layer + 1 == n_layers
            out = _tail_fwd(s["y"], W[w_out], h, mlp_g, layer, W["mlp_w1"], W["mlp_w2"], None if last else mix_g,
                            f"tail_{layer}", deps=held + more1 + more2)
            s["h1"], s["n2"], h, s["relu"] = out[0], out[1], out[2], out[-1]
            n_next = None if last else out[3]
        saved.append(s)

    dh, dhb, dg_final, loss = _final(h, P["final_norm_g"].reshape(1, D), target)

    small = {"final_norm_g": dg_final}
    per_layer = {k: [None] * n_layers for k in ("mix_norm_g", "mlp_norm_g", "lb")}
    per_pair = {k: [None] * (n_layers // 2) for k in
                ("ev_conv_w", "ev_conv_b", "ev_ln_g", "ev_ln_b", "ev_pool_w", "ev_pool_b", "ev_pool_scale", "od_gnorm_g")}
    for layer in reversed(range(n_layers)):
        j = layer // 2
        s = saved[layer]
        W = s["W"]
        w_in, w_out = _mixer_names(layer)
        dw2 = _mm_tn(s["relu"], dhb, "row", f"dw2_{layer}", square=True)
        dz, dh, dhb, per_layer["mlp_norm_g"][layer] = _mlp_bwd(
            dhb, s["relu"], W["mlp_w1"], W["mlp_w2"], s["h1"], mlp_g, layer, dh, f"mlp_bwd_{layer}", deps=deps + (dw2,))
        dw1 = _mm_tn(s["n2"], dz, "col", f"dw1_{layer}")
        deps = boundary(f"mlp{layer}", {("mlp_w1", layer): dw1, ("mlp_w2", layer): dw2}, (dhb, dw1, dw2))
        dy = _mm_nt(dhb, W[w_out], 0, f"d_y_{layer}", deps=deps)
        dwout = _mm_tn(s["y"], dhb, "row", f"dwout_{layer}")
        if layer % 2 == 0:
            du, dcw, dcb, dlg, dlnb, dpw, dpb, dps = _even_bwd(s["u"], s["yc"], dy, P["conv_w_full"], *even, j, f"even_bwd_{layer}")
            for k, val in (("ev_conv_w", dcw), ("ev_conv_b", dcb), ("ev_ln_g", dlg), ("ev_ln_b", dlnb),
                           ("ev_pool_w", dpw), ("ev_pool_b", dpb), ("ev_pool_scale", dps)):
                per_pair[k][j] = val
        else:
            du, per_layer["lb"][layer], per_pair["od_gnorm_g"][j] = _hgrn_bwd(
                s["u"], s["o"], dy, s["sall"], lb3, layer, gn3, j, f"hgrn_bwd_{layer}")
        dwin = _mm_tn(s["n"], du, "col", f"dwin_{layer}")
        deps = boundary(f"mix{layer}", {(w_in, j): dwin, (w_out, j): dwout}, (du, dwin, dwout))
        dh, dhb, per_layer["mix_norm_g"][layer] = _mm_nt_norm(
            du, W[w_in], 0, s["h"], mix_g, layer, dh, f"d_n_{layer}", deps=deps, lead=0 if layer else LEAD)
        deps = ()
    top, grad_x = dh, dhb

    small["mix_norm_g"] = jnp.concatenate(per_layer["mix_norm_g"], axis=0)
    small["mlp_norm_g"] = jnp.concatenate(per_layer["mlp_norm_g"], axis=0)
    dlb_all = jnp.concatenate([jnp.zeros((1, D), F32) if g is None else g for g in per_layer["lb"]], axis=0)
    small["lb_param"] = _lb_bwd(P["lb_param"], dlb_all)
    for k, vals in per_pair.items():
        small[k] = jnp.stack(vals, axis=0)
    small["meta_tokens"] = top[PAD:LEAD]
    return loss, grad_x, small


def kernel(x, meta_tokens, mix_norm_g, mlp_norm_g, final_norm_g, ev_w_in, ev_conv_w, ev_conv_b, ev_ln_g, ev_ln_b, ev_pool_w, ev_pool_b, ev_pool_scale, ev_w_out, od_w_in, od_gnorm_g, od_w_out, lb_param, mlp_w1, mlp_w2, loss_target, m_meta_tokens, m_mix_norm_g, m_mlp_norm_g, m_final_norm_g, m_ev_w_in, m_ev_conv_w, m_ev_conv_b, m_ev_ln_g, m_ev_ln_b, m_ev_pool_w, m_ev_pool_b, m_ev_pool_scale, m_ev_w_out, m_od_w_in, m_od_gnorm_g, m_od_w_out, m_lb_param, m_mlp_w1, m_mlp_w2, v_meta_tokens, v_mix_norm_g, v_mlp_norm_g, v_final_norm_g, v_ev_w_in, v_ev_conv_w, v_ev_conv_b, v_ev_ln_g, v_ev_ln_b, v_ev_pool_w, v_ev_pool_b, v_ev_pool_scale, v_ev_w_out, v_od_w_in, v_od_gnorm_g, v_od_w_out, v_lb_param, v_mlp_w1, v_mlp_w2):
    given = dict(locals())
    w = {n: given[n] for n in WEIGHTS}
    m = {n: given["m_" + n] for n in WEIGHTS}
    v = {n: given["v_" + n] for n in WEIGHTS}
    n_layers = mix_norm_g.shape[0]
    core = lax.axis_index("c").astype(jnp.int32)
    chip = (2 * lax.axis_index("x") + lax.axis_index("y")).astype(jnp.int32)
    chip1 = chip.reshape(1)
    ids2 = jnp.stack([core, chip])

    conv_pad = jnp.pad(ev_conv_w, ((0, 0), (0, CONV_ROWS - CONV_WIDTH), (0, 0)))
    stages = [[(0, n)] for n in (*_mixer_names(0), "mlp_w1", "mlp_w2")]
    for layer in range(1, n_layers):
        stages += [[(layer, n) for n in _mixer_names(layer)], [(layer, "mlp_w1"), (layer, "mlp_w2")]]
    where, stage_kinds, stage_bufs = {}, [], []
    for k, stage in enumerate(stages):
        index = [layer if n.startswith("mlp") else layer // 2 for layer, n in stage]
        kinds = [BIG[n] for _, n in stage]
        bufs = [_cast_place(w[n], i, BIG[n], chip1, BF16, f"place_{n}_{i}") for (_, n), i in zip(stage, index)]
        if k == 0:
            bufs.append(_cast_place(meta_tokens[None], 0, "col", chip1, F32, "place_meta"))
            bufs.append(_cast_place(conv_pad.reshape(1, -1, conv_pad.shape[2]), 0, "col", chip1, F32, "place_conv_w"))
            kinds += ["col", "col"]
        stage_kinds.append(kinds)
        stage_bufs.append(bufs)
        where.update({key: (k, f) for f, key in enumerate(stage)})
    gathers, token, early = [], (), 3
    for lo, hi, name in ((0, early, "gather_start_first"), (early, len(stages), "gather_start_rest")):
        every = [b for bufs in stage_bufs[lo:hi] for b in bufs]
        kinds_all = [kd for kinds in stage_kinds[lo:hi] for kd in kinds]
        ss, rs, every, tok = _split_start(name, _gather_plan(kinds_all, True), every, 3 * len(every), deps=token)
        token = (tok,)
        at = 0
        for kinds in stage_kinds[lo:hi]:
            gathers.append((kinds, _gather_plan(kinds, True, first=at), ss, rs, every[at:at + len(kinds)]))
            at += len(kinds)

    landed, passed, held = {}, {}, []

    def hand_on(k, deps):
        if k not in passed:
            kinds, plan, ss, rs, bufs = gathers[k]
            to_sibling = _gather_plan(kinds, False)
            ss, rs, bufs, tok = _split_start(f"gather_pass_{k}", to_sibling, bufs, 3 * len(bufs), deps=deps, earlier=(plan, ss, rs))
            passed[k] = (to_sibling, ss, rs, bufs)
            held.append(tok)

    def arrived(k, after):
        if k not in landed:
            hand_on(k, after)
            landed[k] = _split_wait(f"gather_wait_{k}", *passed[k], after)
        return landed[k]

    def weights(layer, name, after):
        k, f = where[(layer, name)]
        full = arrived(k, after)[f][None]
        if layer == 0 and name != "mlp_w2":
            hand_on(k + 1, after)
        if name == "mlp_w2" and layer + 1 < n_layers:
            hand_on(where[(layer + 1, _mixer_names(layer + 1)[0])][0], after)
        if layer > 0 and name == _mixer_names(layer)[0]:
            hand_on(where[(layer, "mlp_w1")][0], after)
        tokens = tuple(held)
        held.clear()
        return full, tokens

    first = arrived(0, token)
    P = {n: w[n] for n in ("mix_norm_g", "mlp_norm_g", "final_norm_g", "ev_conv_b", "ev_ln_g", "ev_ln_b", "ev_pool_w",
                           "ev_pool_b", "ev_pool_scale", "od_gnorm_g", "lb_param")}
    P["meta_full"] = first[1]
    P["conv_w_full"] = first[2].reshape(ev_conv_w.shape[0], CONV_ROWS, -1)

    pending, outs = [], {n: None for n in BIG}

    def advance(after, fresh=1):
        ready, still = [], []
        for pos, st in enumerate(pending):
            if st["phase"] == 1 and pos >= len(pending) - fresh:
                still.append(st)
            elif st["phase"] == 1:
                bufs = _split_wait(f"reduce_wait_{st['tag']}", st["plan"], st["ss"], st["rs"], st["bufs"], after)
                halves = [_sum_pieces(ids2, bufs[2 * f], bufs[2 * f + 1], f"sum_{st['tag']}_{f}") for f in range(len(bufs) // 2)]
                ready.append((st, halves))
            else:
                grads = _split_wait(f"swap_wait_{st['tag']}", st["plan"], st["ss"], st["rs"], st["bufs"], after)
                for (n, i), g in zip(st["keys"], grads):
                    outs[n] = _adamw_layer(w[n], m[n], v[n], g.reshape(w[n].shape[1:]), i, outs[n], f"adamw_{n}_{i}")
        pending[:] = still
        return ready

    def launch(name, ready, tag=None, grads=None):
        bufs, parts, entries, at = [], [], [], 0
        for st, halves in ready:
            plan = _swap_plan(first=at)
            entries.append((dict(st, phase=2, plan=plan), len(bufs), len(halves)))
            parts.append((plan, len(halves)))
            bufs += halves
            at += len(halves)
        if grads is not None:
            pairs = []
            for acc in grads.values():
                pairs += [acc, lax.empty((N_DEV - 1,) + acc.shape[2:], BF16)]
            plan = _reduce_plan(first=at)
            entries.append((dict(phase=1, tag=tag, keys=list(grads), plan=plan), len(bufs), len(pairs)))
            parts.append((plan, len(pairs)))
            bufs += pairs
            at += 7 * len(grads)
        if not bufs:
            return ()
        ss, rs, bufs, tok = _split_start(name, _joined(parts), bufs, at)
        for st, lo, n in entries:
            pending.append(dict(st, ss=ss, rs=rs, bufs=bufs[lo:lo + n]))
        return (tok,)

    def boundary(tag, grads, after):
        return launch(f"start_{tag}", advance(after), tag, grads)

    loss, grad_x, small = _local_step(x[0], loss_target[0], P, weights, boundary, first_deps=token)

    order = [n for n in WEIGHTS if n not in BIG]
    block = _pack([small[n] for n in order] + [loss])
    ss, rs, bufs, tok = _split_start("small_start", _small_plan, [block, lax.empty((N_DEV,) + block.shape, F32)], N_DEV - 1)
    for last in range(3):
        launch(f"start_end_{last}", advance((tok,) + tuple(o[0] for o in outs.values() if o is not None), fresh=0))
    assert not pending
    block, land = _split_wait("small_wait", _small_plan, ss, rs, bufs, tuple(outs[n][0] for n in BIG))
    packed = _sum_blocks((4 * lax.axis_index("x") + 2 * lax.axis_index("y") + lax.axis_index("c")).astype(jnp.int32).reshape(1), block, land)
    total = _unpack(packed, [small[n].shape for n in order] + [loss.shape])
    loss_sum = total[-1][0, 0]
    gsmall = dict(zip(order, total[:-1]))
    gsmall["meta_tokens"] = lax.dynamic_slice_in_dim(gsmall["meta_tokens"], chip * meta_tokens.shape[1], meta_tokens.shape[1], 1)
    gsmall["ev_conv_w"] = lax.dynamic_slice_in_dim(gsmall["ev_conv_w"][:, :CONV_WIDTH], chip * ev_conv_w.shape[2], ev_conv_w.shape[2], 2)

    g_out, d_out, m_out, v_out = {}, {}, {}, {}
    for n in BIG:
        g_out[n], d_out[n], m_out[n], v_out[n] = outs[n]
    items = []
    for n in order:
        cols = w[n].shape[-1] if w[n].ndim > 1 else 128
        items.append([a.reshape(-1, cols) for a in (w[n], gsmall[n], m[n], v[n])])
    for n, (d_, m_, v_) in zip(order, _adamw_small(items)):
        shape = w[n].shape
        g_out[n], d_out[n], m_out[n], v_out[n] = gsmall[n].reshape(shape), d_.reshape(shape), m_.reshape(shape), v_.reshape(shape)

    return (loss_sum, grad_x[None], *[g_out[n] for n in WEIGHTS], *[d_out[n] for n in WEIGHTS],
            *[m_out[n] for n in WEIGHTS], *[v_out[n] for n in WEIGHTS])
```

```python
import functools

import jax
import jax.numpy as jnp
from jax import lax
from jax.experimental import pallas as pl
from jax.experimental.pallas import tpu as pltpu

F32 = jnp.float32
BF16 = jnp.bfloat16
SDS = jax.ShapeDtypeStruct
MESH = pl.DeviceIdType.MESH
ANY_SPEC = pl.BlockSpec(memory_space=pl.ANY)

N_META = 16
CHUNK = 64
LEAD = CHUNK
PAD = LEAD - N_META
CONV_WIDTH = 31
CONV_ROWS = 32
POOL_WINDOWS = (2, 4, 8, 16)
HEAD_DIM = 128
SUB = 16
EXP_CAP = 80.0
EPS = 1e-6
ADAM_LR = 0.001
ADAM_B1 = 0.9
ADAM_B2 = 0.999
ADAM_EPS = 1e-08
ADAM_WD = 0.01
ADAM_STEP = 10
N_CHIPS = 4
VMEM_LIMIT = 58 << 20
MM_VMEM_BUDGET = 50 << 20


def _params(*sem):
    return pltpu.CompilerParams(dimension_semantics=sem if sem else None, vmem_limit_bytes=VMEM_LIMIT)


def _tile(n, target, unit=CHUNK):
    best = None
    for t in range(unit, min(n, target) + 1, unit):
        if n % t == 0:
            best = t
    assert best is not None, (n, target, unit)
    return best


def _ctile(n, target=512):
    for t in (512, 384, 256, 128):
        if t <= target and n % t == 0:
            return t
    raise ValueError(n)


def _mm_tiles(M, N, per_row, per_col, per_elem):
    best = None
    for tn in (512, 384, 256, 128):
        if N % tn:
            continue
        for tm in sorted((d for d in range(16, M + 1, 16) if M % d == 0), reverse=True):
            if 2 * (tm * per_row + tn * per_col + tm * tn * per_elem) <= MM_VMEM_BUDGET:
                if best is None or tm * tn > best[0] * best[1]:
                    best = (tm, tn)
                break
    assert best is not None, (M, N)
    return best


def _sigmoid(x):
    return 1.0 / (1.0 + jnp.exp(-x))


def _mult(v, m):
    return v if isinstance(v, int) else pl.multiple_of(v, m)


def _row_ids(shape, base):
    return lax.broadcasted_iota(jnp.int32, shape, 0) + base


def _cast_place(w3, layer, kind, chip1, dtype, name):
    _, ks, ns = w3.shape
    tr = _tile(ks, 512, 16)
    full = (ks, ns * N_CHIPS) if kind == "col" else (ks * N_CHIPS, ns)

    def body(chip_ref, w_ref, o_ref):
        del chip_ref
        o_ref[...] = w_ref[...].astype(dtype)

    omap = (lambda i, chip: (i, chip[0])) if kind == "col" else (lambda i, chip: (chip[0] * (ks // tr) + i, 0))
    return pl.pallas_call(
        body,
        grid_spec=pltpu.PrefetchScalarGridSpec(
            num_scalar_prefetch=1, grid=(ks // tr,),
            in_specs=[pl.BlockSpec((None, tr, ns), lambda i, chip: (layer, i, 0))],
            out_specs=pl.BlockSpec((tr, ns), omap)),
        out_shape=SDS(full, dtype), name=name, compiler_params=_params("parallel"))(chip1, w3)


def _embed_norm(x2, meta, g3, name, deps=()):
    D = x2.shape[1]
    T = LEAD + x2.shape[0]
    tm = _tile(T, 832)
    steps = T // tm
    assert LEAD < tm and meta.shape[0] == LEAD - PAD

    def body(x_ref, meta_ref, g_ref, *rest):
        h_ref, n_ref, buf, sem = rest[len(deps):]
        i = pl.program_id(0)
        slot = i % 2

        def copy(step, slot):
            if isinstance(step, int) and step == 0:
                return pltpu.make_async_copy(x_ref.at[pl.ds(0, tm - LEAD)], buf.at[slot, pl.ds(LEAD, tm - LEAD)], sem.at[slot])
            return pltpu.make_async_copy(x_ref.at[pl.ds(pl.multiple_of(step * tm - LEAD, 8), tm)], buf.at[slot], sem.at[slot])

        @pl.when(i == 0)
        def _():
            buf[0, :PAD] = jnp.zeros((PAD, D), F32)
            buf[0, PAD:LEAD] = meta_ref[...]
            copy(0, 0).start()

        if steps > 1:
            pl.when(i + 1 < steps)(lambda: copy(i + 1, 1 - slot).start())
            pl.when(i > 0)(lambda: copy(i, slot).wait())
        pl.when(i == 0)(lambda: copy(0, 0).wait())
        x = buf[slot]
        r = lax.rsqrt(jnp.mean(x * x, axis=-1, keepdims=True) + EPS)
        h_ref[...] = x
        n_ref[...] = ((x * r) * g_ref[...]).astype(BF16)

    row = pl.BlockSpec((tm, D), lambda i: (i, 0))
    return pl.pallas_call(
        body, grid=(steps,),
        in_specs=[ANY_SPEC, pl.BlockSpec(meta.shape, lambda i: (0, 0)), pl.BlockSpec((None, 1, D), lambda i: (0, 0, 0))]
        + [ANY_SPEC] * len(deps),
        out_specs=[row, row], out_shape=[SDS((T, D), F32), SDS((T, D), BF16)],
        scratch_shapes=[pltpu.VMEM((2, tm, D), F32), pltpu.SemaphoreType.DMA((2,))],
        name=name, compiler_params=_params("arbitrary"))(x2, meta, g3, *deps)


def _final(h, g2, target):
    T, D = h.shape
    tm = _tile(T, 320)
    nsub = tm // CHUNK
    nblk = target.shape[0] // CHUNK

    def body(h_ref, g_ref, *rest):
        t_refs = rest[:nsub]
        dh_ref, dhb_ref, dg_ref, loss_ref = rest[nsub:]
        i = pl.program_id(0)

        @pl.when(i == 0)
        def _():
            dg_ref[...] = jnp.zeros_like(dg_ref)
            loss_ref[...] = jnp.zeros_like(loss_ref)

        g = g_ref[...]
        for q in range(nsub):
            rows = slice(q * CHUNK, (q + 1) * CHUNK)
            x = h_ref[rows, :]
            r = lax.rsqrt(jnp.mean(x * x, axis=-1, keepdims=True) + EPS)
            xh = x * r
            live = jnp.where(i * nsub + q > 0, 1.0, 0.0).astype(F32)
            e = ((xh * g) - t_refs[q][...]) * live
            dy = e * (1.0 / D)
            dxh = dy * g
            dh = r * (dxh - xh * jnp.mean(dxh * xh, axis=-1, keepdims=True))
            dh_ref[rows, :] = dh
            dhb_ref[rows, :] = dh.astype(BF16)
            dg_ref[...] += jnp.sum(dy * xh, axis=0, keepdims=True)
            loss_ref[...] += jnp.sum(e * e) * (0.5 / D)

    row = pl.BlockSpec((tm, D), lambda i: (i, 0))
    t_specs = [pl.BlockSpec((CHUNK, D), functools.partial(lambda i, q: (jnp.clip(i * nsub + q - 1, 0, nblk - 1), 0), q=q))
               for q in range(nsub)]
    return pl.pallas_call(
        body, grid=(T // tm,),
        in_specs=[row, pl.BlockSpec((1, D), lambda i: (0, 0))] + t_specs,
        out_specs=[row, row, pl.BlockSpec((1, D), lambda i: (0, 0)), pl.BlockSpec((1, 128), lambda i: (0, 0))],
        out_shape=[SDS((T, D), F32), SDS((T, D), BF16), SDS((1, D), F32), SDS((1, 128), F32)],
        name="final_loss", compiler_params=_params("arbitrary"))(h, g2, *([target] * nsub))


def _mm_nn(a, w3, layer, name, res=None, relu=False, square=False, deps=()):
    M, K = a.shape
    N = w3.shape[2]
    tm, tn = _mm_tiles(M, N, 2 * K, 2 * K, (2 if relu else 4) + (4 if res is not None else 0))

    def body(*refs):
        lhs = refs[0][...]
        acc = jnp.dot(lhs * lhs if square else lhs, refs[1][...], preferred_element_type=F32)
        if res is not None:
            acc = acc + refs[2][...]
        refs[-1][...] = jnp.maximum(acc, 0.0).astype(BF16) if relu else acc

    in_specs = [pl.BlockSpec((tm, K), lambda i, j: (i, 0)), pl.BlockSpec((None, K, tn), lambda i, j: (layer, 0, j))]
    args = [a, w3]
    tile = pl.BlockSpec((tm, tn), lambda i, j: (i, j))
    if res is not None:
        in_specs.append(tile)
        args.append(res)
    in_specs += [ANY_SPEC] * len(deps)
    args += list(deps)
    return pl.pallas_call(
        body, grid=(M // tm, N // tn), in_specs=in_specs, out_specs=tile,
        out_shape=SDS((M, N), BF16 if relu else F32),
        name=name, compiler_params=_params("parallel", "parallel"))(*args)


def _mm_nt(dy, w3, layer, name, relu=None, deps=()):
    M, N = dy.shape
    K = w3.shape[1]
    tm, tk = _mm_tiles(M, K, 2 * N, 2 * N, 4)

    def body(*refs):
        acc = lax.dot_general(refs[0][...], refs[1][...], (((1,), (1,)), ((), ())), preferred_element_type=F32)
        if relu is not None:
            acc = (acc * (2.0 * refs[2][...].astype(F32))).astype(BF16)
        refs[-1][...] = acc

    tile = pl.BlockSpec((tm, tk), lambda i, j: (i, j))
    in_specs = [pl.BlockSpec((tm, N), lambda i, j: (i, 0)), pl.BlockSpec((None, tk, N), lambda i, j: (layer, j, 0))]
    args = [dy, w3]
    if relu is not None:
        in_specs.append(tile)
        args.append(relu)
    in_specs += [ANY_SPEC] * len(deps)
    args += list(deps)
    return pl.pallas_call(
        body, grid=(M // tm, K // tk), in_specs=in_specs, out_specs=tile,
        out_shape=SDS((M, K), F32 if relu is None else BF16),
        name=name, compiler_params=_params("parallel", "parallel"))(*args)


def _row_tile(M, per_row, fixed):
    for tm in sorted((d for d in range(16, M + 1, 16) if M % d == 0), reverse=True):
        if 2 * (tm * per_row + fixed) <= MM_VMEM_BUDGET:
            return tm
    raise ValueError((M, per_row, fixed))


def _mm_nn_norm(a, w3, layer, res, g3, glayer, name, square=False, deps=()):
    M, K = a.shape
    D = w3.shape[2]
    tm = _row_tile(M, 2 * K + 10 * D, 2 * K * D)

    def body(a_ref, w_ref, r_ref, g_ref, *rest):
        h_ref, n_ref = rest[-2:]
        lhs = a_ref[...]
        x = r_ref[...] + jnp.dot(lhs * lhs if square else lhs, w_ref[...], preferred_element_type=F32)
        h_ref[...] = x
        r = lax.rsqrt(jnp.mean(x * x, axis=-1, keepdims=True) + EPS)
        n_ref[...] = ((x * r) * g_ref[...]).astype(BF16)

    row = pl.BlockSpec((tm, D), lambda i: (i, 0))
    return pl.pallas_call(
        body, grid=(M // tm,),
        in_specs=[pl.BlockSpec((tm, K), lambda i: (i, 0)), pl.BlockSpec((None, K, D), lambda i: (layer, 0, 0)), row,
                  pl.BlockSpec((None, 1, D), lambda i: (glayer, 0, 0))] + [ANY_SPEC] * len(deps),
        out_specs=[row, row], out_shape=[SDS((M, D), F32), SDS((M, D), BF16)],
        name=name, compiler_params=_params("parallel"))(a, w3, res, g3, *deps)


def _tail_fwd(y, w_out, res, mlp_g3, layer, w1, w2, next_g3, name, deps=()):
    M, K = y.shape
    D = w_out.shape[2]
    F = w1.shape[2]
    hb = _ctile(F)
    more = next_g3 is not None
    tm = _row_tile(M, 2 * K + 18 * D + (2 * D if more else 0) + 2 * F, 2 * K * D + 2 * D * F)

    def body(y_ref, wo_ref, res_ref, g_ref, w1_ref, w2_ref, *rest):
        outs = rest[-5:] if more else rest[-4:]
        h1 = res_ref[...] + jnp.dot(y_ref[...], wo_ref[...], preferred_element_type=F32)
        outs[0][...] = h1
        n2 = ((h1 * lax.rsqrt(jnp.mean(h1 * h1, axis=-1, keepdims=True) + EPS)) * g_ref[...]).astype(BF16)
        outs[1][...] = n2
        acc = h1
        for jb in range(F // hb):
            cols = slice(jb * hb, (jb + 1) * hb)
            r = jnp.maximum(jnp.dot(n2, w1_ref[:, cols], preferred_element_type=F32), 0.0).astype(BF16)
            outs[-1][:, cols] = r
            acc = acc + jnp.dot(r * r, w2_ref[cols, :], preferred_element_type=F32)
        outs[2][...] = acc
        if more:
            outs[3][...] = ((acc * lax.rsqrt(jnp.mean(acc * acc, axis=-1, keepdims=True) + EPS)) * rest[0][...]).astype(BF16)

    row = pl.BlockSpec((tm, D), lambda i: (i, 0))
    once = dict(pipeline_mode=pl.Buffered(1))
    in_specs = [pl.BlockSpec((tm, K), lambda i: (i, 0)), pl.BlockSpec((None, K, D), lambda i: (0, 0, 0), **once), row,
                pl.BlockSpec((None, 1, D), lambda i: (layer, 0, 0)),
                pl.BlockSpec((None, D, F), lambda i: (0, 0, 0), **once), pl.BlockSpec((None, F, D), lambda i: (0, 0, 0), **once)]
    args = [y, w_out, res, mlp_g3, w1, w2]
    out_specs, out_shape = [row, row, row], [SDS((M, D), F32), SDS((M, D), BF16), SDS((M, D), F32)]
    if more:
        in_specs.append(pl.BlockSpec((None, 1, D), lambda i: (layer + 1, 0, 0)))
        args.append(next_g3)
        out_specs.append(row)
        out_shape.append(SDS((M, D), BF16))
    out_specs.append(pl.BlockSpec((tm, F), lambda i: (i, 0)))
    out_shape.append(SDS((M, F), BF16))
    in_specs += [ANY_SPEC] * len(deps)
    args += list(deps)
    return pl.pallas_call(
        body, grid=(M // tm,), in_specs=in_specs, out_specs=out_specs, out_shape=out_shape,
        name=name, compiler_params=_params("parallel"))(*args)


def _mlp_bwd(dhb, relu, w1, w2, h, g3, glayer, dh_in, name, deps=()):
    M, D = dhb.shape
    F = w1.shape[2]
    hb = _ctile(F)
    tm = _row_tile(M, 16 * D + 4 * F, 2 * D * F)

    def body(dy_ref, r_ref, w1_ref, w2_ref, h_ref, g_ref, dhi_ref, *rest):
        dz_ref, dh_ref, dhb_ref, dg_ref = rest[-4:]
        dy = dy_ref[...]
        dn = jnp.zeros((tm, D), F32)
        for jb in range(F // hb):
            cols = slice(jb * hb, (jb + 1) * hb)
            dact = lax.dot_general(dy, w2_ref[cols, :], (((1,), (1,)), ((), ())), preferred_element_type=F32)
            dz = (dact * (2.0 * r_ref[:, cols].astype(F32))).astype(BF16)
            dz_ref[:, cols] = dz
            dn = dn + lax.dot_general(dz, w1_ref[:, cols], (((1,), (1,)), ((), ())), preferred_element_type=F32)
        x = h_ref[...]
        r = lax.rsqrt(jnp.mean(x * x, axis=-1, keepdims=True) + EPS)
        xh = x * r
        dxh = dn * g_ref[...]
        dh = dhi_ref[...] + r * (dxh - xh * jnp.mean(dxh * xh, axis=-1, keepdims=True))
        dh_ref[...] = dh
        dhb_ref[...] = dh.astype(BF16)

        @pl.when(pl.program_id(0) == 0)
        def _():
            dg_ref[...] = jnp.zeros_like(dg_ref)

        dg_ref[...] += jnp.sum(dn * xh, axis=0, keepdims=True)

    row = pl.BlockSpec((tm, D), lambda i: (i, 0))
    wide = pl.BlockSpec((tm, F), lambda i: (i, 0))
    once = dict(pipeline_mode=pl.Buffered(1))
    return pl.pallas_call(
        body, grid=(M // tm,),
        in_specs=[row, wide, pl.BlockSpec((None, D, F), lambda i: (0, 0, 0), **once),
                  pl.BlockSpec((None, F, D), lambda i: (0, 0, 0), **once), row,
                  pl.BlockSpec((None, 1, D), lambda i: (glayer, 0, 0)), row] + [ANY_SPEC] * len(deps),
        out_specs=[wide, row, row, pl.BlockSpec((1, D), lambda i: (0, 0))],
        out_shape=[SDS((M, F), BF16), SDS((M, D), F32), SDS((M, D), BF16), SDS((1, D), F32)],
        name=name, compiler_params=_params("arbitrary"))(dhb, relu, w1, w2, h, g3, dh_in, *deps)


def _mm_nt_norm(dy, w3, layer, h, g3, glayer, dh_in, name, deps=(), lead=0):
    M, N = dy.shape
    D = w3.shape[1]
    tm = _row_tile(M, 2 * N + 14 * D, 2 * N * D)
    steps = M // tm
    assert lead % 8 == 0 and lead < tm

    def body(dy_ref, w_ref, h_ref, g_ref, dhi_ref, *rest):
        dn = lax.dot_general(dy_ref[...], w_ref[...], (((1,), (1,)), ((), ())), preferred_element_type=F32)
        x = h_ref[...]
        r = lax.rsqrt(jnp.mean(x * x, axis=-1, keepdims=True) + EPS)
        xh = x * r
        dxh = dn * g_ref[...]
        dh = dhi_ref[...] + r * (dxh - xh * jnp.mean(dxh * xh, axis=-1, keepdims=True))
        i = pl.program_id(0)
        if lead:
            top_ref, tail_ref, dg_ref, buf, sem = rest[len(deps):]

            def copy(step, slot):
                if isinstance(step, int) and step == 0:
                    return pltpu.make_async_copy(buf.at[slot, pl.ds(lead, tm - lead)], tail_ref.at[pl.ds(0, tm - lead)], sem.at[slot])
                return pltpu.make_async_copy(buf.at[slot], tail_ref.at[pl.ds(pl.multiple_of(step * tm - lead, 8), tm)], sem.at[slot])

            slot = i % 2

            if steps > 2:
                pl.when(i == 2)(lambda: copy(0, 0).wait())
                pl.when(i > 2)(lambda: copy(i - 2, slot).wait())
            buf[slot] = dh

            @pl.when(i == 0)
            def _():
                top_ref[...] = dh[:lead]
                copy(0, 0).start()

            if steps > 1:
                pl.when(i > 0)(lambda: copy(i, slot).start())

            @pl.when(i == steps - 1)
            def _():
                for step in range(max(steps - 2, 0), steps):
                    copy(step, step % 2).wait()
        else:
            dh_ref, dhb_ref, dg_ref = rest[len(deps):]
            dh_ref[...] = dh
            dhb_ref[...] = dh.astype(BF16)

        @pl.when(i == 0)
        def _():
            dg_ref[...] = jnp.zeros_like(dg_ref)

        dg_ref[...] += jnp.sum(dn * xh, axis=0, keepdims=True)

    row = pl.BlockSpec((tm, D), lambda i: (i, 0))
    one = pl.BlockSpec((1, D), lambda i: (0, 0))
    if lead:
        outs = dict(out_specs=[pl.BlockSpec((lead, D), lambda i: (0, 0)), ANY_SPEC, one],
                    out_shape=[SDS((lead, D), F32), SDS((M - lead, D), F32), SDS((1, D), F32)],
                    scratch_shapes=[pltpu.VMEM((2, tm, D), F32), pltpu.SemaphoreType.DMA((2,))])
    else:
        outs = dict(out_specs=[row, row, one], out_shape=[SDS((M, D), F32), SDS((M, D), BF16), SDS((1, D), F32)])
    return pl.pallas_call(
        body, grid=(steps,),
        in_specs=[pl.BlockSpec((tm, N), lambda i: (i, 0)), pl.BlockSpec((None, D, N), lambda i: (layer, 0, 0)), row,
                  pl.BlockSpec((None, 1, D), lambda i: (glayer, 0, 0)), row] + [ANY_SPEC] * len(deps),
        name=name, compiler_params=_params("arbitrary"), **outs)(dy, w3, h, g3, dh_in, *deps)


def _fam_dims(kind, K, N):
    return (K // 2, N // N_CHIPS) if kind == "col" else (K // (2 * N_CHIPS), N)


def _mm_tn(x, dy, kind, name, square=False):
    M, K = x.shape
    N = dy.shape[1]
    nr, nc = _fam_dims(kind, K, N)

    def body(x_ref, dy_ref, o_ref):
        lhs = x_ref[...]
        res = lax.dot_general(lhs * lhs if square else lhs, dy_ref[...], (((0,), (0,)), ((), ())), preferred_element_type=F32)
        o_ref[...] = res.astype(BF16).reshape(o_ref.shape)

    if kind == "col":
        tn = _ctile(nc)
        ct = nc // tn
        grid = (N // tn,)
        in_specs = [pl.BlockSpec((M, K), lambda j: (0, 0)), pl.BlockSpec((M, tn), lambda j: (0, j))]
        out_spec = pl.BlockSpec((2, None, nr, tn), lambda j: (0, j // ct, 0, j % ct))
    else:
        grid = (N_CHIPS,)
        in_specs = [pl.BlockSpec((M, 2 * nr), lambda i: (0, i)), pl.BlockSpec((M, N), lambda i: (0, 0))]
        out_spec = pl.BlockSpec((2, None, nr, N), lambda i: (0, i, 0, 0))
    return pl.pallas_call(
        body, grid=grid, in_specs=in_specs, out_specs=out_spec, out_shape=SDS((2, N_CHIPS, nr, nc), BF16),
        name=name, compiler_params=_params("parallel"))(x, dy)


C_EVEN = 512


def _live(rows, base, total):
    r = _row_ids((rows, 1), base)
    return jnp.logical_and(r >= PAD, r < total).astype(F32)


def _conv_taps(win, w_ref, ls, acc, flip):
    for b in range(8):
        rb = win if b == 0 else pltpu.roll(win, 96 - b, 0)
        for a in range(5):
            o = 8 * a + b
            tap = (30 - o) if flip else (o - 2)
            if 0 <= tap < CONV_WIDTH:
                acc = acc + w_ref[pl.ds(tap, 1), ls] * rb[8 * a:8 * a + CHUNK]
    return acc


def _window_sum(win, levels, forward):
    s = win
    n = win.shape[0]
    for k in range(levels):
        step = 1 << k
        s = s + pltpu.roll(s, (n - step) if forward else step, 0)
    return s


def _pool_count(base, g):
    pos = _row_ids((CHUNK, 1), base) - PAD
    return jnp.clip(pos + 1, 1, POOL_WINDOWS[g]).astype(F32)


def _even_fwd(u, cw3, cb3, lg3, lb3, pw4, pb3, ps3, j, name):
    T = u.shape[0]
    C = C_EVEN
    tm = _tile(T, 320)
    nch = tm // CHUNK
    nblk = T // CHUNK

    def body(u_ref, up_ref, cw_ref, cb_ref, lg_ref, lb_ref, pw_ref, pb_ref, ps_ref, o_ref, yc_ref, a_s, p_s, yc_s):
        row0 = pl.program_id(0) * tm
        up = up_ref[...]
        lp = _live(CHUNK, row0 - CHUNK, T)
        a_s[0:CHUNK, :] = up[:, 0:C] * _sigmoid(up[:, C:2 * C]) * lp
        p_s[0:CHUNK, :] = up[:, 2 * C:3 * C] * lp

        def stage(c, _):
            rs = _mult(c * CHUNK, CHUNK)
            lv = _live(CHUNK, row0 + rs, T)
            a_s[pl.ds(rs + CHUNK, CHUNK), :] = u_ref[pl.ds(rs, CHUNK), 0:C] * _sigmoid(u_ref[pl.ds(rs, CHUNK), C:2 * C]) * lv
            p_s[pl.ds(rs + CHUNK, CHUNK), :] = u_ref[pl.ds(rs, CHUNK), 2 * C:3 * C] * lv
            return 0

        for c in range(nch):
            stage(c, 0)

        def chunk(c, _):
            rs = _mult(c * CHUNK, CHUNK)
            lv = _live(CHUNK, row0 + rs, T)
            for cb in range(4):
                ls = slice(cb * 128, (cb + 1) * 128)
                win = a_s[pl.ds(_mult(rs + 32, 32), 96), ls]
                acc = jnp.broadcast_to(cb_ref[:, ls], (CHUNK, 128))
                yc_s[:, ls] = _conv_taps(win, cw_ref, ls, acc, False)
            y = yc_s[...]
            yc_ref[pl.ds(rs, CHUNK), :] = y
            xc = y - jnp.mean(y, axis=-1, keepdims=True)
            yn = xc * lax.rsqrt(jnp.mean(xc * xc, axis=-1, keepdims=True) + EPS) * lg_ref[...] + lb_ref[...]
            o_ref[pl.ds(rs, CHUNK), 0:C] = (yn * _sigmoid(yn) * lv).astype(BF16)
            for g in range(4):
                ls = slice(g * 128, (g + 1) * 128)
                win = p_s[pl.ds(_mult(rs + 48, 16), 80), ls]
                s = _window_sum(win, g + 1, False)
                d = s[16:80] / _pool_count(row0 + rs, g) - win[16:80]
                yv = jnp.dot(d.astype(BF16), pw_ref[g].astype(BF16), preferred_element_type=F32) + pb_ref[:, ls]
                o_ref[pl.ds(rs, CHUNK), C + g * 128:C + (g + 1) * 128] = (yv * ps_ref[:, ls] * lv).astype(BF16)
            return 0

        for c in range(nch):
            chunk(c, 0)

    vec = pl.BlockSpec((None, 1, C), lambda i: (j, 0, 0))
    return pl.pallas_call(
        body, grid=(T // tm,),
        in_specs=[pl.BlockSpec((tm, 3 * C), lambda i: (i, 0)),
                  pl.BlockSpec((CHUNK, 3 * C), lambda i: (jnp.maximum(i * nch - 1, 0), 0)),
                  pl.BlockSpec((None, CONV_ROWS, C), lambda i: (j, 0, 0)), vec, vec, vec,
                  pl.BlockSpec((None, 4, 128, 128), lambda i: (j, 0, 0, 0)), vec, vec],
        out_specs=[pl.BlockSpec((tm, 2 * C), lambda i: (i, 0)), pl.BlockSpec((tm, C), lambda i: (i, 0))],
        out_shape=[SDS((T, 2 * C), BF16), SDS((T, C), F32)],
        scratch_shapes=[pltpu.VMEM((tm + CHUNK, C), F32), pltpu.VMEM((tm + CHUNK, C), F32), pltpu.VMEM((CHUNK, C), F32)],
        name=name, compiler_params=_params("parallel"))(u, u, cw3, cb3, lg3, lb3, pw4, pb3, ps3)


def _even_bwd(u, yc, dy, cw3, cb3, lg3, lb3, pw4, pb3, ps3, j, name):
    T = u.shape[0]
    C = C_EVEN
    tm = _tile(T, 320)
    nch = tm // CHUNK
    nblk = T // CHUNK
    ntile = T // tm

    def body(u_ref, up_ref, un_ref, yc_ref, ycn_ref, dy_ref, dyn_ref, cw_ref, cb_ref, lg_ref, lb_ref, pw_ref, pb_ref, ps_ref,
             du_ref, dcw_ref, dcb_ref, dlg_ref, dlb_ref, dpw_ref, dpb_ref, dps_ref,
             a_s, p_s, dy_s, dyc_s, dd_s, ddc_s, dw_s):
        i = pl.program_id(0)
        row0 = i * tm

        @pl.when(i == 0)
        def _():
            for ref in (dcb_ref, dlg_ref, dlb_ref, dpw_ref, dpb_ref, dps_ref, dw_s):
                ref[...] = jnp.zeros_like(ref)

        up = up_ref[...]
        lp = _live(CHUNK, row0 - CHUNK, T)
        a_s[0:CHUNK, :] = up[:, 0:C] * _sigmoid(up[:, C:2 * C]) * lp
        p_s[0:CHUNK, :] = up[:, 2 * C:3 * C] * lp
        ln_ = _live(CHUNK, row0 + tm, T)
        p_s[tm + CHUNK:tm + 2 * CHUNK, :] = un_ref[:, 2 * C:3 * C] * ln_
        dy_s[tm:tm + CHUNK, :] = dyn_ref[...] * ln_
        dyc_s[tm + CHUNK:tm + CHUNK + 32, :] = jnp.zeros((32, C), F32)

        def stage(c, _):
            rs = _mult(c * CHUNK, CHUNK)
            lv = _live(CHUNK, row0 + rs, T)
            a_s[pl.ds(rs + CHUNK, CHUNK), :] = u_ref[pl.ds(rs, CHUNK), 0:C] * _sigmoid(u_ref[pl.ds(rs, CHUNK), C:2 * C]) * lv
            p_s[pl.ds(rs + CHUNK, CHUNK), :] = u_ref[pl.ds(rs, CHUNK), 2 * C:3 * C] * lv
            dy_s[pl.ds(rs, CHUNK), :] = dy_ref[pl.ds(rs, CHUNK), :] * lv
            return 0

        for c in range(nch):
            stage(c, 0)

        def first(rs, y, own):
            xc = y - jnp.mean(y, axis=-1, keepdims=True)
            rstd = lax.rsqrt(jnp.mean(xc * xc, axis=-1, keepdims=True) + EPS)
            xh = xc * rstd
            yn = xh * lg_ref[...] + lb_ref[...]
            sg = _sigmoid(yn)
            dyn = dy_s[pl.ds(rs, CHUNK), 0:C] * (sg * (1.0 + yn * (1.0 - sg)))
            dlg_ref[...] += jnp.sum(dyn * xh, axis=0, keepdims=True) * own
            dlb_ref[...] += jnp.sum(dyn, axis=0, keepdims=True) * own
            dxh = dyn * lg_ref[...]
            dyc = rstd * (dxh - jnp.mean(dxh, axis=-1, keepdims=True) - xh * jnp.mean(dxh * xh, axis=-1, keepdims=True))
            dyc_s[pl.ds(rs, CHUNK), :] = dyc
            dcb_ref[...] += jnp.sum(dyc, axis=0, keepdims=True) * own
            for g in range(4):
                ls = slice(g * 128, (g + 1) * 128)
                win = p_s[pl.ds(rs + 48, 80), ls]
                s = _window_sum(win, g + 1, False)
                cnt = _pool_count(row0 + rs, g)
                d = (s[16:80] / cnt - win[16:80]).astype(BF16)
                w = pw_ref[g].astype(BF16)
                pre = jnp.dot(d, w, preferred_element_type=F32) + pb_ref[:, ls]
                dyb = dy_s[pl.ds(rs, CHUNK), C + g * 128:C + (g + 1) * 128]
                dpre = dyb * ps_ref[:, ls]
                dps_ref[:, ls] += jnp.sum(dyb * pre, axis=0, keepdims=True) * own
                dpb_ref[:, ls] += jnp.sum(dpre, axis=0, keepdims=True) * own
                dpre_b = (dpre * own).astype(BF16)
                dpw_ref[g] += lax.dot_general(d, dpre_b, (((0,), (0,)), ((), ())), preferred_element_type=F32)
                dd = lax.dot_general(dpre.astype(BF16), w, (((1,), (1,)), ((), ())), preferred_element_type=F32)
                dd_s[pl.ds(rs, CHUNK), ls] = dd
                ddc_s[pl.ds(rs, CHUNK), ls] = dd / cnt

        def first_in_tile(c, _):
            rs = _mult(c * CHUNK, CHUNK)
            first(rs, yc_ref[pl.ds(rs, CHUNK), :], 1.0)
            return 0

        for c in range(nch):
            first_in_tile(c, 0)
        first(tm, ycn_ref[...], 0.0)
        ddc_s[tm + CHUNK:tm + CHUNK + 16, :] = jnp.zeros((16, C), F32)

        def second(c, _):
            rs = _mult(c * CHUNK, CHUNK)
            lv = _live(CHUNK, row0 + rs, T)
            for cb in range(4):
                ls = slice(cb * 128, (cb + 1) * 128)
                wd = dyc_s[pl.ds(rs, 96), ls]
                da = _conv_taps(wd, cw_ref, ls, jnp.zeros((CHUNK, 128), F32), True)
                wa = a_s[pl.ds(_mult(rs + 32, 32), 96), ls]
                dyc = dyc_s[pl.ds(rs, CHUNK), ls]
                for b in range(8):
                    rb = wa if b == 0 else pltpu.roll(wa, 96 - b, 0)
                    for a in range(5):
                        tap = 8 * a + b - 2
                        if 0 <= tap < CONV_WIDTH:
                            prod = dyc * rb[8 * a:8 * a + CHUNK]
                            part = prod[0:8]
                            for q in range(1, 8):
                                part = part + prod[8 * q:8 * q + 8]
                            dw_s[8 * tap:8 * tap + 8, ls] += part
                val = u_ref[pl.ds(rs, CHUNK), ls]
                sg = _sigmoid(u_ref[pl.ds(rs, CHUNK), C + cb * 128:C + (cb + 1) * 128])
                du_ref[pl.ds(rs, CHUNK), ls] = (da * sg * lv).astype(BF16)
                du_ref[pl.ds(rs, CHUNK), C + cb * 128:C + (cb + 1) * 128] = (da * val * sg * (1.0 - sg) * lv).astype(BF16)
            for g in range(4):
                ls = slice(g * 128, (g + 1) * 128)
                z = _window_sum(ddc_s[pl.ds(rs, 80), ls], g + 1, True)
                dpin = (z[0:CHUNK] - dd_s[pl.ds(rs, CHUNK), ls]) * lv
                du_ref[pl.ds(rs, CHUNK), 2 * C + g * 128:2 * C + (g + 1) * 128] = dpin.astype(BF16)
            return 0

        for c in range(nch):
            second(c, 0)

        @pl.when(i == ntile - 1)
        def _():
            for tap in range(CONV_WIDTH):
                dcw_ref[tap:tap + 1, :] = jnp.sum(dw_s[8 * tap:8 * tap + 8, :], axis=0, keepdims=True)
            dcw_ref[CONV_WIDTH:CONV_ROWS, :] = jnp.zeros((CONV_ROWS - CONV_WIDTH, C), F32)

    vec = pl.BlockSpec((None, 1, C), lambda i: (j, 0, 0))
    ovec = pl.BlockSpec((1, C), lambda i: (0, 0))
    return pl.pallas_call(
        body, grid=(ntile,),
        in_specs=[pl.BlockSpec((tm, 3 * C), lambda i: (i, 0)),
                  pl.BlockSpec((CHUNK, 3 * C), lambda i: (jnp.maximum(i * nch - 1, 0), 0)),
                  pl.BlockSpec((CHUNK, 3 * C), lambda i: (jnp.minimum((i + 1) * nch, nblk - 1), 0)),
                  pl.BlockSpec((tm, C), lambda i: (i, 0)),
                  pl.BlockSpec((CHUNK, C), lambda i: (jnp.minimum((i + 1) * nch, nblk - 1), 0)),
                  pl.BlockSpec((tm, 2 * C), lambda i: (i, 0)),
                  pl.BlockSpec((CHUNK, 2 * C), lambda i: (jnp.minimum((i + 1) * nch, nblk - 1), 0)),
                  pl.BlockSpec((None, CONV_ROWS, C), lambda i: (j, 0, 0)), vec, vec, vec,
                  pl.BlockSpec((None, 4, 128, 128), lambda i: (j, 0, 0, 0)), vec, vec],
        out_specs=[pl.BlockSpec((tm, 3 * C), lambda i: (i, 0)), pl.BlockSpec((CONV_ROWS, C), lambda i: (0, 0)),
                   ovec, ovec, ovec, pl.BlockSpec((4, 128, 128), lambda i: (0, 0, 0)), ovec, ovec],
        out_shape=[SDS((T, 3 * C), BF16), SDS((CONV_ROWS, C), F32), SDS((1, C), F32), SDS((1, C), F32), SDS((1, C), F32),
                   SDS((4, 128, 128), F32), SDS((1, C), F32), SDS((1, C), F32)],
        scratch_shapes=[pltpu.VMEM((tm + CHUNK, C), F32), pltpu.VMEM((tm + 2 * CHUNK, C), F32),
                        pltpu.VMEM((tm + CHUNK, 2 * C), F32),
                        pltpu.VMEM((tm + CHUNK + 32, C), F32), pltpu.VMEM((tm + CHUNK, C), F32),
                        pltpu.VMEM((tm + CHUNK + 16, C), F32), pltpu.VMEM((8 * CONV_ROWS, C), F32)],
        name=name, compiler_params=_params("arbitrary"))(u, u, u, yc, yc, dy, dy, cw3, cb3, lg3, lb3, pw4, pb3, ps3)


HI = lax.Precision.HIGHEST


def _dot_nt(a, b):
    return lax.dot_general(a, b, (((1,), (1,)), ((), ())), preferred_element_type=F32)


def _dot_tn(a, b):
    return lax.dot_general(a, b, (((0,), (0,)), ((), ())), preferred_element_type=F32)


def _tri(lower):
    r = lax.broadcasted_iota(jnp.int32, (CHUNK, CHUNK), 0)
    c = lax.broadcasted_iota(jnp.int32, (CHUNK, CHUNK), 1)
    return jnp.where((c <= r) if lower else (c >= r), 1.0, 0.0).astype(F32)


def _hgrn_gates(u_ref, lb_ref, h, D, lv):
    ls = slice(h * HEAD_DIM, (h + 1) * HEAD_DIM)
    qraw = u_ref[:, ls]
    fraw = u_ref[:, D + h * HEAD_DIM:D + (h + 1) * HEAD_DIM]
    v = u_ref[:, 2 * D + h * HEAD_DIM:2 * D + (h + 1) * HEAD_DIM] * lv
    lbv = lb_ref[:, ls]
    sig = _sigmoid(fraw)
    forget = lbv + (1.0 - lbv) * sig
    logf = jnp.log(forget) * lv
    k = (1.0 - forget) * lv
    qsig = _sigmoid(qraw)
    q = qraw * qsig * lv
    return q, k, v, logf, (qraw, qsig, sig, forget, lbv)


def _sub_parts(q, k, b, b_s, I):
    rows = slice(SUB * I, SUB * (I + 1))
    rho = jnp.zeros((1, HEAD_DIM), F32) if I == 0 else b_s[SUB * I - 1:SUB * I, :]
    eI = jnp.exp(b[rows] - rho)
    EI = jnp.exp(jnp.minimum(rho - b, EXP_CAP))
    causal = (lax.broadcasted_iota(jnp.int32, (SUB, CHUNK), 1)
              <= lax.broadcasted_iota(jnp.int32, (SUB, CHUNK), 0) + SUB * I)
    return rows, q[rows] * eI, k * EI, eI, EI, causal


def _chunks_per_step(NC):
    for n in (5, 4, 3, 2):
        if NC % n == 0:
            return n
    return 1


def _hgrn_fwd(u, lb3, layer, gn3, j, name):
    T = u.shape[0]
    D = u.shape[1] // 4
    H = D // HEAD_DIM
    NC = T // CHUNK
    CH = _chunks_per_step(NC)
    R = CH * CHUNK

    def body(u_ref, lb_ref, gn_ref, y_ref, o_ref, sall_ref, st_s, b_s, lf_s, q_s, k_s):
        n = pl.program_id(0)

        @pl.when(n == 0)
        def _():
            st_s[...] = jnp.zeros_like(st_s)

        heads = range(H)
        cols = [slice(h * HEAD_DIM, (h + 1) * HEAD_DIM) for h in heads]
        rows = [slice(c * CHUNK, (c + 1) * CHUNK) for c in range(CH)]
        vb = {}
        for c in range(CH):
            lv = _live(CHUNK, (n * CH + c) * CHUNK, T)
            for h in heads:
                q, k, v, logf, _ = _hgrn_gates(u_ref.at[rows[c]], lb_ref, h, D, lv)
                q_s[rows[c], cols[h]] = q
                k_s[rows[c], cols[h]] = k
                lf_s[rows[c], cols[h]] = logf
                vb[c, h] = v.astype(BF16)
        for c in range(CH):
            b_s[rows[c], :] = jnp.dot(_tri(True), lf_s[rows[c], :], precision=HI, preferred_element_type=F32)
        ops = {}
        for c in range(CH):
            for h in heads:
                b_h = b_s.at[rows[c], cols[h]]
                b = b_h[...]
                q = q_s[rows[c], cols[h]]
                k = k_s[rows[c], cols[h]]
                blast = b_h[CHUNK - 1:CHUNK, :]
                qh = (q * jnp.exp(b)).astype(BF16)
                kt = (k * jnp.exp(blast - b)).astype(BF16)
                subs = []
                for I in range(CHUNK // SUB):
                    _, qI, KI, _, _, causal = _sub_parts(q, k, b, b_h, I)
                    subs.append((qI.astype(BF16), KI.astype(BF16), causal))
                ops[c, h] = (qh, kt, jnp.exp(blast), subs)
        mm = {}
        for h in heads:
            st = st_s[h]
            for c in range(CH):
                qh, kt, eblast, subs = ops[c, h]
                sall_ref[c, h] = st
                o_inter = _dot_nt(qh, st.astype(BF16))
                st = st * eblast + _dot_tn(vb[c, h], kt)
                mm[c, h] = (o_inter, [_dot_nt(qI, KI) for qI, KI, _ in subs])
            st_s[h] = st
        for c in range(CH):
            for h in heads:
                o_inter, ps = mm[c, h]
                p = jnp.concatenate([jnp.where(m, x, 0.0) for x, (_, _, m) in zip(ps, ops[c, h][3])], axis=0).astype(BF16)
                o = o_inter + jnp.dot(p, vb[c, h], preferred_element_type=F32)
                o_ref[rows[c], cols[h]] = o
                graw = u_ref[rows[c], 3 * D + h * HEAD_DIM:3 * D + (h + 1) * HEAD_DIM]
                r = lax.rsqrt(jnp.mean(o * o, axis=-1, keepdims=True) + EPS)
                y_ref[rows[c], cols[h]] = (((o * r) * gn_ref[...]) * (graw * _sigmoid(graw))).astype(BF16)

    return pl.pallas_call(
        body, grid=(NC // CH,),
        in_specs=[pl.BlockSpec((R, 4 * D), lambda n: (n, 0)),
                  pl.BlockSpec((None, 1, D), lambda n: (layer, 0, 0)),
                  pl.BlockSpec((None, 1, HEAD_DIM), lambda n: (j, 0, 0))],
        out_specs=[pl.BlockSpec((R, D), lambda n: (n, 0)), pl.BlockSpec((R, D), lambda n: (n, 0)),
                   pl.BlockSpec((CH, H, HEAD_DIM, HEAD_DIM), lambda n: (n, 0, 0, 0))],
        out_shape=[SDS((T, D), BF16), SDS((T, D), F32), SDS((NC, H, HEAD_DIM, HEAD_DIM), F32)],
        scratch_shapes=[pltpu.VMEM((H, HEAD_DIM, HEAD_DIM), F32)] + [pltpu.VMEM((R, D), F32)] * 4,
        name=name, compiler_params=_params("arbitrary"))(u, lb3, gn3)


def _hgrn_bwd(u, o_raw, dy, sall, lb3, layer, gn3, j, name):
    T = u.shape[0]
    D = u.shape[1] // 4
    H = D // HEAD_DIM
    NC = T // CHUNK
    CH = _chunks_per_step(NC)
    R = CH * CHUNK
    NS = NC // CH

    def body(u_ref, o_ref, dy_ref, sall_ref, lb_ref, gn_ref, du_ref, dlb_ref, dgn_ref, dst_s, b_s, lf_s, q_s, k_s, db_s, dk_s):
        step = pl.program_id(0)
        n = NS - 1 - step

        @pl.when(step == 0)
        def _():
            dst_s[...] = jnp.zeros_like(dst_s)
            dlb_ref[...] = jnp.zeros_like(dlb_ref)
            dgn_ref[...] = jnp.zeros_like(dgn_ref)

        last_row = (_row_ids((CHUNK, 1), 0) == CHUNK - 1).astype(F32)
        gn = gn_ref[...]
        heads = range(H)
        chunks = range(CH)
        cols = [slice(h * HEAD_DIM, (h + 1) * HEAD_DIM) for h in heads]
        rows = [slice(c * CHUNK, (c + 1) * CHUNK) for c in chunks]
        lv = [_live(CHUNK, (n * CH + c) * CHUNK, T) for c in chunks]
        vb, dob = {}, {}
        dgn = jnp.zeros((1, HEAD_DIM), F32)
        for c in chunks:
            for h in heads:
                q, k, v, logf, _ = _hgrn_gates(u_ref.at[rows[c]], lb_ref, h, D, lv[c])
                q_s[rows[c], cols[h]] = q
                k_s[rows[c], cols[h]] = k
                lf_s[rows[c], cols[h]] = logf
                vb[c, h] = v.astype(BF16)
                graw = u_ref[rows[c], 3 * D + h * HEAD_DIM:3 * D + (h + 1) * HEAD_DIM]
                gsig = _sigmoid(graw)
                o = o_ref[rows[c], cols[h]]
                r = lax.rsqrt(jnp.mean(o * o, axis=-1, keepdims=True) + EPS)
                xh = o * r
                dyv = dy_ref[rows[c], cols[h]]
                dsg = dyv * (graw * gsig)
                dgn = dgn + jnp.sum(dsg * xh, axis=0, keepdims=True)
                dxh = dsg * gn
                do = r * (dxh - xh * jnp.mean(dxh * xh, axis=-1, keepdims=True))
                dob[c, h] = do.astype(BF16)
                dgraw = dyv * xh * gn * (gsig * (1.0 + graw * (1.0 - gsig)))
                du_ref[rows[c], 3 * D + h * HEAD_DIM:3 * D + (h + 1) * HEAD_DIM] = (dgraw * lv[c]).astype(BF16)
        dgn_ref[...] += dgn
        for c in chunks:
            b_s[rows[c], :] = jnp.dot(_tri(True), lf_s[rows[c], :], precision=HI, preferred_element_type=F32)
        ops = {}
        for c in chunks:
            for h in heads:
                b_h = b_s.at[rows[c], cols[h]]
                b = b_h[...]
                q = q_s[rows[c], cols[h]]
                k = k_s[rows[c], cols[h]]
                blast = b_h[CHUNK - 1:CHUNK, :]
                eb = jnp.exp(b)
                ekb = jnp.exp(blast - b)
                subs = []
                for I in range(CHUNK // SUB):
                    rws, qI, KI, eI, EI, causal = _sub_parts(q, k, b, b_h, I)
                    subs.append((rws, qI.astype(BF16), KI.astype(BF16), eI, EI, causal))
                ops[c, h] = (eb, ekb, jnp.exp(blast), (q * eb).astype(BF16), (k * ekb).astype(BF16), subs)
        mm = {}
        for h in heads:
            dst = dst_s[h]
            for c in reversed(chunks):
                eb, ekb, eblast, qhb, ktb, subs = ops[c, h]
                st = sall_ref[c, h]
                dstb = dst.astype(BF16)
                dv = _dot_nt(ktb, dstb)
                dqh = jnp.dot(dob[c, h], st.astype(BF16), preferred_element_type=F32)
                dkt = jnp.dot(vb[c, h], dstb, preferred_element_type=F32)
                dblast = jnp.sum(dst * st, axis=0, keepdims=True) * eblast
                dst = dst * eblast + _dot_tn(dob[c, h], qhb)
                dp_full = _dot_nt(dob[c, h], vb[c, h])
                ps = [_dot_nt(qIb, KIb) for _, qIb, KIb, _, _, _ in subs]
                mm[c, h] = (dv, dqh, dkt, dblast, dp_full, ps)
            dst_s[h] = dst
        for c in chunks:
            for h in heads:
                eb, ekb, eblast, qhb, ktb, subs = ops[c, h]
                dv, dqh, dkt, dblast, dp_full, ps = mm[c, h]
                p = jnp.concatenate([jnp.where(sub[5], x, 0.0) for x, sub in zip(ps, subs)], axis=0).astype(BF16)
                dv = dv + _dot_tn(p, dob[c, h])
                du_ref[rows[c], 2 * D + h * HEAD_DIM:2 * D + (h + 1) * HEAD_DIM] = (dv * lv[c]).astype(BF16)
                dq = dqh * eb
                db = dqh * qhb.astype(F32)
                tmp = dkt * ktb.astype(F32)
                dk = dkt * ekb
                db = db - tmp
                dblast = dblast + jnp.sum(tmp, axis=0, keepdims=True)
                dq_parts, db_parts = [], []
                for rws, qIb, KIb, eI, EI, causal in subs:
                    dp = jnp.where(causal, dp_full[rws], 0.0).astype(BF16)
                    dqI = jnp.dot(dp, KIb, preferred_element_type=F32)
                    dKI = _dot_tn(dp, qIb)
                    dq_parts.append(dqI * eI)
                    db_parts.append(dqI * qIb.astype(F32))
                    dk = dk + dKI * EI
                    db = db - dKI * KIb.astype(F32)
                dq = dq + jnp.concatenate(dq_parts, axis=0)
                db_s[rows[c], cols[h]] = db + jnp.concatenate(db_parts, axis=0) + last_row * dblast
                dk_s[rows[c], cols[h]] = dk
                qraw = u_ref[rows[c], cols[h]]
                qsig = _sigmoid(qraw)
                du_ref[rows[c], cols[h]] = (dq * (qsig * (1.0 + qraw * (1.0 - qsig))) * lv[c]).astype(BF16)
        for c in chunks:
            lf_s[rows[c], :] = jnp.dot(_tri(False), db_s[rows[c], :], precision=HI, preferred_element_type=F32)
        for h in heads:
            lbv = lb_ref[:, cols[h]]
            dlb = jnp.zeros((1, HEAD_DIM), F32)
            for c in chunks:
                fraw = u_ref[rows[c], D + h * HEAD_DIM:D + (h + 1) * HEAD_DIM]
                sig = _sigmoid(fraw)
                forget = lbv + (1.0 - lbv) * sig
                dforget = (lf_s[rows[c], cols[h]] / forget - dk_s[rows[c], cols[h]]) * lv[c]
                dlb = dlb + jnp.sum(dforget * (1.0 - sig), axis=0, keepdims=True)
                du_ref[rows[c], D + h * HEAD_DIM:D + (h + 1) * HEAD_DIM] = (dforget * (1.0 - lbv) * sig * (1.0 - sig)).astype(BF16)
            dlb_ref[:, cols[h]] += dlb

    rev = lambda s: (NS - 1 - s, 0)
    return pl.pallas_call(
        body, grid=(NS,),
        in_specs=[pl.BlockSpec((R, 4 * D), rev), pl.BlockSpec((R, D), rev), pl.BlockSpec((R, D), rev),
                  pl.BlockSpec((CH, H, HEAD_DIM, HEAD_DIM), lambda s: (NS - 1 - s, 0, 0, 0)),
                  pl.BlockSpec((None, 1, D), lambda s: (layer, 0, 0)),
                  pl.BlockSpec((None, 1, HEAD_DIM), lambda s: (j, 0, 0))],
        out_specs=[pl.BlockSpec((R, 4 * D), rev), pl.BlockSpec((1, D), lambda s: (0, 0)),
                   pl.BlockSpec((1, HEAD_DIM), lambda s: (0, 0))],
        out_shape=[SDS((T, 4 * D), BF16), SDS((1, D), F32), SDS((1, HEAD_DIM), F32)],
        scratch_shapes=[pltpu.VMEM((H, HEAD_DIM, HEAD_DIM), F32)] + [pltpu.VMEM((R, D), F32)] * 6,
        name=name, compiler_params=_params("arbitrary"))(u, o_raw, dy, sall, lb3, gn3)


def _softmax_layers(p_ref, n_layers):
    rows = [p_ref[l:l + 1, :] for l in range(n_layers)]
    m = functools.reduce(jnp.maximum, rows)
    e = [jnp.exp(x - m) for x in rows]
    tot = functools.reduce(lambda a, b: a + b, e)
    return [x / tot for x in e]


def _lb_fwd(p):
    n_layers, D = p.shape

    def body(p_ref, o_ref):
        s = _softmax_layers(p_ref, n_layers)
        acc = jnp.zeros((1, D), F32)
        o_ref[0:1, :] = acc
        for l in range(1, n_layers):
            acc = acc + s[l]
            o_ref[l:l + 1, :] = acc

    return pl.pallas_call(body, out_shape=SDS(p.shape, F32), name="lb_fwd")(p)


def _lb_bwd(p, dlb):
    n_layers, D = p.shape

    def body(p_ref, d_ref, o_ref):
        s = _softmax_layers(p_ref, n_layers)
        ds = [jnp.zeros((1, D), F32)] * n_layers
        acc = jnp.zeros((1, D), F32)
        for l in range(n_layers - 1, 0, -1):
            acc = acc + d_ref[l:l + 1, :]
            ds[l] = acc
        dot = functools.reduce(lambda a, b: a + b, [s[l] * ds[l] for l in range(n_layers)])
        for l in range(n_layers):
            o_ref[l:l + 1, :] = s[l] * (ds[l] - dot)

    return pl.pallas_call(body, out_shape=SDS(p.shape, F32), name="lb_bwd")(p, dlb)


def _adamw_small(items):
    n = len(items)

    def body(*refs):
        for k in range(n):
            w_ref, g_ref, m_ref, v_ref = refs[4 * k:4 * k + 4]
            d_ref, mo_ref, vo_ref = refs[4 * n + 3 * k:4 * n + 3 * k + 3]
            g_ = g_ref[...]
            m_ = ADAM_B1 * m_ref[...] + (1.0 - ADAM_B1) * g_
            v_ = ADAM_B2 * v_ref[...] + (1.0 - ADAM_B2) * (g_ * g_)
            mh = m_ / (1.0 - ADAM_B1 ** ADAM_STEP)
            vh = v_ / (1.0 - ADAM_B2 ** ADAM_STEP)
            d_ref[...] = -ADAM_LR * (mh / (jnp.sqrt(vh) + ADAM_EPS) + ADAM_WD * w_ref[...])
            mo_ref[...] = m_
            vo_ref[...] = v_

    out_shape = [SDS(it[0].shape, F32) for it in items for _ in range(3)]
    res = pl.pallas_call(body, out_shape=out_shape, name="adamw_small")(*[a for it in items for a in it])
    return [res[3 * k:3 * k + 3] for k in range(n)]


def _adamw_layer(w3, m3, v3, g2, layer, outs, name):
    L, R, C = w3.shape
    tr = _tile(R, 256, 8)
    if outs is None:
        outs = tuple(lax.empty(w3.shape, F32) for _ in range(4))

    def body(w_ref, m_ref, v_ref, g_ref, a0, a1, a2, a3, go_ref, d_ref, mo_ref, vo_ref):
        del a0, a1, a2, a3
        g_ = g_ref[...]
        m_ = ADAM_B1 * m_ref[...] + (1.0 - ADAM_B1) * g_
        v_ = ADAM_B2 * v_ref[...] + (1.0 - ADAM_B2) * (g_ * g_)
        mh = m_ / (1.0 - ADAM_B1 ** ADAM_STEP)
        vh = v_ / (1.0 - ADAM_B2 ** ADAM_STEP)
        go_ref[...] = g_
        d_ref[...] = -ADAM_LR * (mh / (jnp.sqrt(vh) + ADAM_EPS) + ADAM_WD * w_ref[...])
        mo_ref[...] = m_
        vo_ref[...] = v_

    lay = pl.BlockSpec((None, tr, C), lambda i: (layer, i, 0))
    return pl.pallas_call(
        body, grid=(R // tr,), in_specs=[lay] * 3 + [pl.BlockSpec((tr, C), lambda i: (i, 0))] + [ANY_SPEC] * 4,
        out_specs=[lay] * 4, out_shape=[SDS(w3.shape, F32)] * 4, input_output_aliases={4: 0, 5: 1, 6: 2, 7: 3},
        name=name, compiler_params=_params("parallel"))(w3, m3, v3, g2, *outs)


SEM_SPEC = pl.BlockSpec(memory_space=pltpu.SEMAPHORE)
HBM_SPEC = pl.BlockSpec(memory_space=pltpu.HBM)
EFFECT = pltpu.SideEffectType.DATAFLOW_SIDE_EFFECTING
N_DEV = 2 * N_CHIPS


def _position():
    x, y, c = lax.axis_index("x"), lax.axis_index("y"), lax.axis_index("c")
    chips = [(1 - x, y), (x, 1 - y), (1 - x, 1 - y)]
    return x, y, c, chips


def _split_start(name, plan, bufs, n_sems, deps=(), earlier=None):
    n = len(bufs)
    held = () if earlier is None else tuple(earlier[1:])

    def body(*refs):
        first_out = n + len(held) + len(deps)
        if earlier is not None:
            sends, recvs = earlier[0](refs[:n], refs[n], refs[n + 1])
            for kw in sends:
                pltpu.make_async_remote_copy(**kw).wait_send()
            for kw in recvs:
                pltpu.make_async_remote_copy(**kw).wait_recv()
        sends, _ = plan(refs[:n], refs[first_out], refs[first_out + 1])
        for kw in sends:
            pltpu.make_async_remote_copy(**kw).start()
        refs[-1][...] = jnp.zeros_like(refs[-1])

    out = pl.pallas_call(
        body, name=name,
        out_shape=(pltpu.SemaphoreType.DMA((n_sems,)), pltpu.SemaphoreType.DMA((n_sems,)),
                   *[pltpu.HBM(b.shape, b.dtype) for b in bufs], SDS((8, 128), F32)),
        in_specs=[HBM_SPEC] * n + [SEM_SPEC] * len(held) + [ANY_SPEC] * len(deps),
        out_specs=(SEM_SPEC, SEM_SPEC, *[HBM_SPEC] * n, pl.BlockSpec(memory_space=pltpu.VMEM)),
        input_output_aliases={i: 2 + i for i in range(n)},
        compiler_params=pltpu.CompilerParams(has_side_effects=EFFECT),
    )(*[pltpu.with_memory_space_constraint(b, pltpu.HBM) for b in bufs], *held, *deps)
    return out[0], out[1], list(out[2:2 + n]), out[-1]


def _split_wait(name, plan, send_sems, recv_sems, bufs, after=()):
    n = len(bufs)

    def body(*refs):
        sends, recvs = plan(refs[:n], refs[n], refs[n + 1])
        for kw in sends:
            pltpu.make_async_remote_copy(**kw).wait_send()
        for kw in recvs:
            pltpu.make_async_remote_copy(**kw).wait_recv()

    out = pl.pallas_call(
        body, name=name, out_shape=tuple(pltpu.HBM(b.shape, b.dtype) for b in bufs),
        in_specs=[HBM_SPEC] * n + [SEM_SPEC, SEM_SPEC] + [ANY_SPEC] * len(after),
        out_specs=tuple([HBM_SPEC] * n), input_output_aliases={i: i for i in range(n)},
        compiler_params=pltpu.CompilerParams(has_side_effects=EFFECT),
    )(*bufs, send_sems, recv_sems, *after)
    return list(out)


def _region(kind, ref, chip, half):
    K, N = ref.shape
    if kind == "col":
        return ref.at[pl.ds(half * (K // 2), K // 2), pl.ds(chip * (N // N_CHIPS), N // N_CHIPS)]
    rows = K // (2 * N_CHIPS)
    return ref.at[pl.ds((2 * chip + half) * rows, rows), :]


def _gather_plan(kinds, over_chips, first=0):
    def plan(refs, send_sems, recv_sems):
        x, y, c, chips = _position()
        sends, recvs = [], []
        for f, (ref, kind) in enumerate(zip(refs, kinds)):
            for k, chip in enumerate(chips):
                theirs = 2 * chip[0] + chip[1]
                at = 3 * (first + f) + k
                sem = dict(send_sem=send_sems.at[at], recv_sem=recv_sems.at[at], device_id_type=MESH)
                if over_chips:
                    out, back, to = _region(kind, ref, 2 * x + y, c), _region(kind, ref, theirs, c), (*chip, c)
                else:
                    out, back, to = _region(kind, ref, theirs, c), _region(kind, ref, theirs, 1 - c), (x, y, 1 - c)
                sends.append(dict(src_ref=out, dst_ref=out, device_id=to, **sem))
                recvs.append(dict(src_ref=back, dst_ref=back, device_id=to, **sem))
        return sends, recvs
    return plan


def _reduce_plan(first=0):
    def plan(refs, send_sems, recv_sems):
        x, y, c, _ = _position()
        me = 4 * x + 2 * y + c
        sends, recvs = [], []
        for f in range(len(refs) // 2):
            acc, land = refs[2 * f], refs[2 * f + 1]
            for d in range(1, N_DEV):
                t = (me + d) % N_DEV
                to = dict(device_id=(t // 4, (t // 2) % 2, t % 2), device_id_type=MESH)
                slot = N_DEV - 1 - d
                at = first + 7 * f
                sends.append(dict(src_ref=acc.at[t % 2, t // 2], dst_ref=land.at[slot], send_sem=send_sems.at[at + d - 1],
                                  recv_sem=recv_sems.at[at + slot], **to))
                recvs.append(dict(src_ref=land.at[d - 1], dst_ref=land.at[d - 1], send_sem=send_sems.at[at + d - 1],
                                  recv_sem=recv_sems.at[at + d - 1], **to))
        return sends, recvs
    return plan


def _swap_plan(first=0):
    def plan(refs, send_sems, recv_sems):
        x, y, c, _ = _position()
        sends, recvs = [], []
        for f, g in enumerate(refs):
            sem = dict(send_sem=send_sems.at[first + f], recv_sem=recv_sems.at[first + f], device_id=(x, y, 1 - c),
                       device_id_type=MESH)
            sends.append(dict(src_ref=g.at[c], dst_ref=g.at[c], **sem))
            recvs.append(dict(src_ref=g.at[1 - c], dst_ref=g.at[1 - c], **sem))
        return sends, recvs
    return plan


def _joined(plans):
    def plan(refs, send_sems, recv_sems):
        sends, recvs, lo = [], [], 0
        for part, n in plans:
            s_, r_ = part(refs[lo:lo + n], send_sems, recv_sems)
            sends += s_
            recvs += r_
            lo += n
        return sends, recvs
    return plan


def _sum_pieces(ids2, acc, land, name):
    _, _, nr, nc = acc.shape
    tr = _tile(nr, 256, 16)

    def body(ids_ref, own_ref, land_ref, o_ref):
        del ids_ref
        s = own_ref[...].astype(F32)
        for k in range(N_DEV - 1):
            s = s + land_ref[k].astype(F32)
        o_ref[...] = s

    return pl.pallas_call(
        body,
        grid_spec=pltpu.PrefetchScalarGridSpec(
            num_scalar_prefetch=1, grid=(nr // tr,),
            in_specs=[pl.BlockSpec((None, None, tr, nc), lambda i, ids: (ids[0], ids[1], i, 0)),
                      pl.BlockSpec((N_DEV - 1, tr, nc), lambda i, ids: (0, i, 0))],
            out_specs=pl.BlockSpec((None, tr, nc), lambda i, ids: (ids[0], i, 0))),
        out_shape=SDS((2, nr, nc), F32), name=name, compiler_params=_params("parallel"))(ids2, acc, land)


def _small_plan(refs, send_sems, recv_sems):
    x, y, c, _ = _position()
    me = 4 * x + 2 * y + c
    own, land = refs
    sends, recvs = [], []
    for d in range(1, N_DEV):
        t = (me + d) % N_DEV
        to = dict(device_id=(t // 4, (t // 2) % 2, t % 2), device_id_type=MESH)
        sends.append(dict(src_ref=own, dst_ref=land.at[me], send_sem=send_sems.at[d - 1],
                          recv_sem=recv_sems.at[N_DEV - 1 - d], **to))
        recvs.append(dict(src_ref=land.at[t], dst_ref=land.at[t], send_sem=send_sems.at[d - 1],
                          recv_sem=recv_sems.at[d - 1], **to))
    return sends, recvs


def _sum_blocks(me1, own, land):
    def body(me_ref, own_ref, land_ref, o_ref):
        acc = None
        for d in range(N_DEV):
            term = jnp.where(me_ref[0] == d, own_ref[...], land_ref[d])
            acc = term if acc is None else acc + term
        o_ref[...] = acc

    return pl.pallas_call(
        body,
        grid_spec=pltpu.PrefetchScalarGridSpec(
            num_scalar_prefetch=1, grid=(1,),
            in_specs=[pl.BlockSpec(own.shape, lambda i, me: (0, 0)), pl.BlockSpec(land.shape, lambda i, me: (0, 0, 0))],
            out_specs=pl.BlockSpec(own.shape, lambda i, me: (0, 0))),
        out_shape=SDS(own.shape, F32), name="sum_small", compiler_params=_params("arbitrary"))(me1, own, land)


BIG = {"ev_w_in": "col", "ev_w_out": "row", "od_w_in": "col", "od_w_out": "row", "mlp_w1": "col", "mlp_w2": "row"}
WEIGHTS = ("meta_tokens", "mix_norm_g", "mlp_norm_g", "final_norm_g", "ev_w_in", "ev_conv_w", "ev_conv_b", "ev_ln_g",
           "ev_ln_b", "ev_pool_w", "ev_pool_b", "ev_pool_scale", "ev_w_out", "od_w_in", "od_gnorm_g", "od_w_out",
           "lb_param", "mlp_w1", "mlp_w2")
PACK_UNIT = 1024


def _mixer_names(layer):
    return ("ev_w_in", "ev_w_out") if layer % 2 == 0 else ("od_w_in", "od_w_out")


def _pack(arrays):
    flat = []
    for a in arrays:
        a = a.reshape(-1)
        flat.append(jnp.pad(a, (0, (-a.shape[0]) % PACK_UNIT)))
    return jnp.concatenate(flat).reshape(-1, 128)


def _unpack(packed, shapes):
    flat = packed.reshape(-1)
    out, off = [], 0
    for s in shapes:
        size = 1
        for d in s:
            size *= d
        out.append(flat[off:off + size].reshape(s))
        off += size + (-size) % PACK_UNIT
    return out


def _local_step(x2, target, P, weights, boundary, first_deps=()):
    D = x2.shape[1]
    n_layers = P["mix_norm_g"].shape[0]
    mix_g = P["mix_norm_g"].reshape(n_layers, 1, D)
    mlp_g = P["mlp_norm_g"].reshape(n_layers, 1, D)
    vec = lambda a: a.reshape(a.shape[0], 1, -1)
    cb3, lg3, lnb3, ps3 = vec(P["ev_conv_b"]), vec(P["ev_ln_g"]), vec(P["ev_ln_b"]), vec(P["ev_pool_scale"])
    pb3 = vec(P["ev_pool_b"])
    gn3 = vec(P["od_gnorm_g"])
    lb_all = _lb_fwd(P["lb_param"])
    lb3 = lb_all.reshape(n_layers, 1, D)
    even = (cb3, lg3, lnb3, P["ev_pool_w"], pb3, ps3)

    saved = []
    h, n_next = _embed_norm(x2, P["meta_full"], mix_g, "mix_norm_0", deps=tuple(first_deps))
    deps = ()
    for layer in range(n_layers):
        j = layer // 2
        w_in, w_out = _mixer_names(layer)
        W = {}
        s = {"h": h, "W": W, "n": n_next}
        W[w_in], held = weights(layer, w_in, (s["n"],))
        s["u"] = _mm_nn(s["n"], W[w_in], 0, f"mix_in_{layer}", deps=held)
        if layer % 2 == 0:
            s["y"], s["yc"] = _even_fwd(s["u"], P["conv_w_full"], *even, j, f"even_fwd_{layer}")
        else:
            s["y"], s["o"], s["sall"] = _hgrn_fwd(s["u"], lb3, layer, gn3, j, f"hgrn_fwd_{layer}")
        W[w_out], held = weights(layer, w_out, (s["y"],))
        if layer == 0:
            h, s["n2"] = _mm_nn_norm(s["y"], W[w_out], 0, h, mlp_g, layer, "mix_out_0", deps=held)
            s["h1"] = h
            W["mlp_w1"], held = weights(layer, "mlp_w1", (s["n2"],))
            s["relu"] = _mm_nn(s["n2"], W["mlp_w1"], 0, "mlp_up_0", relu=True, deps=held)
            W["mlp_w2"], held = weights(layer, "mlp_w2", (s["relu"],))
            h, n_next = _mm_nn_norm(s["relu"], W["mlp_w2"], 0, h, mix_g, 1, "mlp_down_0", square=True, deps=held)
        else:
            W["mlp_w1"], more1 = weights(layer, "mlp_w1", (s["y"],))
            W["mlp_w2"], more2 = weights(layer, "mlp_w2", (s["y"],))
            last = layer + 1 == n_layers
            out = _tail_fwd(s["y"], W[w_out], h, mlp_g, layer, W["mlp_w1"], W["mlp_w2"], None if last else mix_g,
                            f"tail_{layer}", deps=held + more1 + more2)
            s["h1"], s["n2"], h, s["relu"] = out[0], out[1], out[2], out[-1]
            n_next = None if last else out[3]
        saved.append(s)

    dh, dhb, dg_final, loss = _final(h, P["final_norm_g"].reshape(1, D), target)

    small = {"final_norm_g": dg_final}
    per_layer = {k: [None] * n_layers for k in ("mix_norm_g", "mlp_norm_g", "lb")}
    per_pair = {k: [None] * (n_layers // 2) for k in
                ("ev_conv_w", "ev_conv_b", "ev_ln_g", "ev_ln_b", "ev_pool_w", "ev_pool_b", "ev_pool_scale", "od_gnorm_g")}
    for layer in reversed(range(n_layers)):
        j = layer // 2
        s = saved[layer]
        W = s["W"]
        w_in, w_out = _mixer_names(layer)
        dw2 = _mm_tn(s["relu"], dhb, "row", f"dw2_{layer}", square=True)
        dz, dh, dhb, per_layer["mlp_norm_g"][layer] = _mlp_bwd(
            dhb, s["relu"], W["mlp_w1"], W["mlp_w2"], s["h1"], mlp_g, layer, dh, f"mlp_bwd_{layer}", deps=deps + (dw2,))
        dw1 = _mm_tn(s["n2"], dz, "col", f"dw1_{layer}")
        deps = boundary(f"mlp{layer}", {("mlp_w1", layer): dw1, ("mlp_w2", layer): dw2}, (dhb, dw1, dw2))
        dy = _mm_nt(dhb, W[w_out], 0, f"d_y_{layer}", deps=deps)
        dwout = _mm_tn(s["y"], dhb, "row", f"dwout_{layer}")
        if layer % 2 == 0:
            du, dcw, dcb, dlg, dlnb, dpw, dpb, dps = _even_bwd(s["u"], s["yc"], dy, P["conv_w_full"], *even, j, f"even_bwd_{layer}")
            for k, val in (("ev_conv_w", dcw), ("ev_conv_b", dcb), ("ev_ln_g", dlg), ("ev_ln_b", dlnb),
                           ("ev_pool_w", dpw), ("ev_pool_b", dpb), ("ev_pool_scale", dps)):
                per_pair[k][j] = val
        else:
            du, per_layer["lb"][layer], per_pair["od_gnorm_g"][j] = _hgrn_bwd(
                s["u"], s["o"], dy, s["sall"], lb3, layer, gn3, j, f"hgrn_bwd_{layer}")
        dwin = _mm_tn(s["n"], du, "col", f"dwin_{layer}")
        deps = boundary(f"mix{layer}", {(w_in, j): dwin, (w_out, j): dwout}, (du, dwin, dwout))
        dh, dhb, per_layer["mix_norm_g"][layer] = _mm_nt_norm(
            du, W[w_in], 0, s["h"], mix_g, layer, dh, f"d_n_{layer}", deps=deps, lead=0 if layer else LEAD)
        deps = ()
    top, grad_x = dh, dhb

    small["mix_norm_g"] = jnp.concatenate(per_layer["mix_norm_g"], axis=0)
    small["mlp_norm_g"] = jnp.concatenate(per_layer["mlp_norm_g"], axis=0)
    dlb_all = jnp.concatenate([jnp.zeros((1, D), F32) if g is None else g for g in per_layer["lb"]], axis=0)
    small["lb_param"] = _lb_bwd(P["lb_param"], dlb_all)
    for k, vals in per_pair.items():
        small[k] = jnp.stack(vals, axis=0)
    small["meta_tokens"] = top[PAD:LEAD]
    return loss, grad_x, small


def kernel(x, meta_tokens, mix_norm_g, mlp_norm_g, final_norm_g, ev_w_in, ev_conv_w, ev_conv_b, ev_ln_g, ev_ln_b, ev_pool_w, ev_pool_b, ev_pool_scale, ev_w_out, od_w_in, od_gnorm_g, od_w_out, lb_param, mlp_w1, mlp_w2, loss_target, m_meta_tokens, m_mix_norm_g, m_mlp_norm_g, m_final_norm_g, m_ev_w_in, m_ev_conv_w, m_ev_conv_b, m_ev_ln_g, m_ev_ln_b, m_ev_pool_w, m_ev_pool_b, m_ev_pool_scale, m_ev_w_out, m_od_w_in, m_od_gnorm_g, m_od_w_out, m_lb_param, m_mlp_w1, m_mlp_w2, v_meta_tokens, v_mix_norm_g, v_mlp_norm_g, v_final_norm_g, v_ev_w_in, v_ev_conv_w, v_ev_conv_b, v_ev_ln_g, v_ev_ln_b, v_ev_pool_w, v_ev_pool_b, v_ev_pool_scale, v_ev_w_out, v_od_w_in, v_od_gnorm_g, v_od_w_out, v_lb_param, v_mlp_w1, v_mlp_w2):
    given = dict(locals())
    w = {n: given[n] for n in WEIGHTS}
    m = {n: given["m_" + n] for n in WEIGHTS}
    v = {n: given["v_" + n] for n in WEIGHTS}
    n_layers = mix_norm_g.shape[0]
    core = lax.axis_index("c").astype(jnp.int32)
    chip = (2 * lax.axis_index("x") + lax.axis_index("y")).astype(jnp.int32)
    chip1 = chip.reshape(1)
    ids2 = jnp.stack([core, chip])

    conv_pad = jnp.pad(ev_conv_w, ((0, 0), (0, CONV_ROWS - CONV_WIDTH), (0, 0)))
    stages = [[(0, n)] for n in (*_mixer_names(0), "mlp_w1", "mlp_w2")]
    for layer in range(1, n_layers):
        stages += [[(layer, n) for n in _mixer_names(layer)], [(layer, "mlp_w1"), (layer, "mlp_w2")]]
    where, stage_kinds, stage_bufs = {}, [], []
    for k, stage in enumerate(stages):
        index = [layer if n.startswith("mlp") else layer // 2 for layer, n in stage]
        kinds = [BIG[n] for _, n in stage]
        bufs = [_cast_place(w[n], i, BIG[n], chip1, BF16, f"place_{n}_{i}") for (_, n), i in zip(stage, index)]
        if k == 0:
            bufs.append(_cast_place(meta_tokens[None], 0, "col", chip1, F32, "place_meta"))
            bufs.append(_cast_place(conv_pad.reshape(1, -1, conv_pad.shape[2]), 0, "col", chip1, F32, "place_conv_w"))
            kinds += ["col", "col"]
        stage_kinds.append(kinds)
        stage_bufs.append(bufs)
        where.update({key: (k, f) for f, key in enumerate(stage)})
    gathers, token, early = [], (), 3
    for lo, hi, name in ((0, early, "gather_start_first"), (early, len(stages), "gather_start_rest")):
        every = [b for bufs in stage_bufs[lo:hi] for b in bufs]
        kinds_all = [kd for kinds in stage_kinds[lo:hi] for kd in kinds]
        ss, rs, every, tok = _split_start(name, _gather_plan(kinds_all, True), every, 3 * len(every), deps=token)
        token = (tok,)
        at = 0
        for kinds in stage_kinds[lo:hi]:
            gathers.append((kinds, _gather_plan(kinds, True, first=at), ss, rs, every[at:at + len(kinds)]))
            at += len(kinds)

    landed, passed, held = {}, {}, []

    def hand_on(k, deps):
        if k not in passed:
            kinds, plan, ss, rs, bufs = gathers[k]
            to_sibling = _gather_plan(kinds, False)
            ss, rs, bufs, tok = _split_start(f"gather_pass_{k}", to_sibling, bufs, 3 * len(bufs), deps=deps, earlier=(plan, ss, rs))
            passed[k] = (to_sibling, ss, rs, bufs)
            held.append(tok)

    def arrived(k, after):
        if k not in landed:
            hand_on(k, after)
            landed[k] = _split_wait(f"gather_wait_{k}", *passed[k], after)
        return landed[k]

    def weights(layer, name, after):
        k, f = where[(layer, name)]
        full = arrived(k, after)[f][None]
        if layer == 0 and name != "mlp_w2":
            hand_on(k + 1, after)
        if name == "mlp_w2" and layer + 1 < n_layers:
            hand_on(where[(layer + 1, _mixer_names(layer + 1)[0])][0], after)
        if layer > 0 and name == _mixer_names(layer)[0]:
            hand_on(where[(layer, "mlp_w1")][0], after)
        tokens = tuple(held)
        held.clear()
        return full, tokens

    first = arrived(0, token)
    P = {n: w[n] for n in ("mix_norm_g", "mlp_norm_g", "final_norm_g", "ev_conv_b", "ev_ln_g", "ev_ln_b", "ev_pool_w",
                           "ev_pool_b", "ev_pool_scale", "od_gnorm_g", "lb_param")}
    P["meta_full"] = first[1]
    P["conv_w_full"] = first[2].reshape(ev_conv_w.shape[0], CONV_ROWS, -1)

    pending, outs = [], {n: None for n in BIG}

    def advance(after, fresh=1):
        ready, still = [], []
        for pos, st in enumerate(pending):
            if st["phase"] == 1 and pos >= len(pending) - fresh:
                still.append(st)
            elif st["phase"] == 1:
                bufs = _split_wait(f"reduce_wait_{st['tag']}", st["plan"], st["ss"], st["rs"], st["bufs"], after)
                halves = [_sum_pieces(ids2, bufs[2 * f], bufs[2 * f + 1], f"sum_{st['tag']}_{f}") for f in range(len(bufs) // 2)]
                ready.append((st, halves))
            else:
                grads = _split_wait(f"swap_wait_{st['tag']}", st["plan"], st["ss"], st["rs"], st["bufs"], after)
                for (n, i), g in zip(st["keys"], grads):
                    outs[n] = _adamw_layer(w[n], m[n], v[n], g.reshape(w[n].shape[1:]), i, outs[n], f"adamw_{n}_{i}")
        pending[:] = still
        return ready

    def launch(name, ready, tag=None, grads=None):
        bufs, parts, entries, at = [], [], [], 0
        for st, halves in ready:
            plan = _swap_plan(first=at)
            entries.append((dict(st, phase=2, plan=plan), len(bufs), len(halves)))
            parts.append((plan, len(halves)))
            bufs += halves
            at += len(halves)
        if grads is not None:
            pairs = []
            for acc in grads.values():
                pairs += [acc, lax.empty((N_DEV - 1,) + acc.shape[2:], BF16)]
            plan = _reduce_plan(first=at)
            entries.append((dict(phase=1, tag=tag, keys=list(grads), plan=plan), len(bufs), len(pairs)))
            parts.append((plan, len(pairs)))
            bufs += pairs
            at += 7 * len(grads)
        if not bufs:
            return ()
        ss, rs, bufs, tok = _split_start(name, _joined(parts), bufs, at)
        for st, lo, n in entries:
            pending.append(dict(st, ss=ss, rs=rs, bufs=bufs[lo:lo + n]))
        return (tok,)

    def boundary(tag, grads, after):
        return launch(f"start_{tag}", advance(after), tag, grads)

    loss, grad_x, small = _local_step(x[0], loss_target[0], P, weights, boundary, first_deps=token)

    order = [n for n in WEIGHTS if n not in BIG]
    block = _pack([small[n] for n in order] + [loss])
    ss, rs, bufs, tok = _split_start("small_start", _small_plan, [block, lax.empty((N_DEV,) + block.shape, F32)], N_DEV - 1)
    for last in range(3):
        launch(f"start_end_{last}", advance((tok,) + tuple(o[0] for o in outs.values() if o is not None), fresh=0))
    assert not pending
    block, land = _split_wait("small_wait", _small_plan, ss, rs, bufs, tuple(outs[n][0] for n in BIG))
    packed = _sum_blocks((4 * lax.axis_index("x") + 2 * lax.axis_index("y") + lax.axis_index("c")).astype(jnp.int32).reshape(1), block, land)
    total = _unpack(packed, [small[n].shape for n in order] + [loss.shape])
    loss_sum = total[-1][0, 0]
    gsmall = dict(zip(order, total[:-1]))
    gsmall["meta_tokens"] = lax.dynamic_slice_in_dim(gsmall["meta_tokens"], chip * meta_tokens.shape[1], meta_tokens.shape[1], 1)
    gsmall["ev_conv_w"] = lax.dynamic_slice_in_dim(gsmall["ev_conv_w"][:, :CONV_WIDTH], chip * ev_conv_w.shape[2], ev_conv_w.shape[2], 2)

    g_out, d_out, m_out, v_out = {}, {}, {}, {}
    for n in BIG:
        g_out[n], d_out[n], m_out[n], v_out[n] = outs[n]
    items = []
    for n in order:
        cols = w[n].shape[-1] if w[n].ndim > 1 else 128
        items.append([a.reshape(-1, cols) for a in (w[n], gsmall[n], m[n], v[n])])
    for n, (d_, m_, v_) in zip(order, _adamw_small(items)):
        shape = w[n].shape
        g_out[n], d_out[n], m_out[n], v_out[n] = gsmall[n].reshape(shape), d_.reshape(shape), m_.reshape(shape), v_.reshape(shape)

    return (loss_sum, grad_x[None], *[g_out[n] for n in WEIGHTS], *[d_out[n] for n in WEIGHTS],
            *[m_out[n] for n in WEIGHTS], *[v_out[n] for n in WEIGHTS])
```
